```python
import jax, jax.numpy as jnp
from jax import lax
import numpy as np

D_MODEL = 1024
BATCH = 8
SEQ = 2048
DEPTH = 2

HEAD_DIM = 64
ROT_DIM = HEAD_DIM // 4
ROPE_THETA = 500000.0
DIL_GROUPS = ((128, 1), (512, 4), (2048, 16))
N_GROUPS = len(DIL_GROUPS)
HEADS_PER_GROUP = 8
GROUP_WIDTH = HEADS_PER_GROUP * HEAD_DIM
BLOCK = 128
N_MEM = 256
MEM_HEADS = 4
MEM_WIDTH = MEM_HEADS * HEAD_DIM
CONV_WIDTH = D_MODEL
CONV_K = 3
N_MIXERS = 2
N_ATTN_LAYERS = (DEPTH + 1) // 2
N_CONV_LAYERS = DEPTH // 2
BRANCH_A = GROUP_WIDTH + MEM_WIDTH
BRANCH_B = CONV_WIDTH + MEM_WIDTH
IN_A = 3 * N_GROUPS * GROUP_WIDTH + MEM_WIDTH + BRANCH_A
IN_B = 3 * CONV_WIDTH + MEM_WIDTH + BRANCH_B
EPS = 1e-6

kernel_name = "hybrid_dilated_attn_shortconv_memory"


def rms_norm(x, g):
    xf = x.astype(jnp.float32)
    y = xf * lax.rsqrt(jnp.mean(xf * xf, axis=-1, keepdims=True) + EPS)
    return (y * g.astype(jnp.float32)).astype(x.dtype)


def partial_rope(t, positions):
    half = ROT_DIM // 2
    inv_freq = ROPE_THETA ** (-jnp.arange(half, dtype=jnp.float32) * (2.0 / ROT_DIM))
    ang = positions.astype(jnp.float32)[:, :, None] * inv_freq
    cos = jnp.cos(ang)[:, :, None, :]
    sin = jnp.sin(ang)[:, :, None, :]
    tr = t[..., :ROT_DIM].astype(jnp.float32)
    t1, t2 = tr[..., :half], tr[..., half:]
    rot = jnp.concatenate([t1 * cos - t2 * sin, t2 * cos + t1 * sin], axis=-1)
    return jnp.concatenate([rot.astype(t.dtype), t[..., ROT_DIM:]], axis=-1)


def dilated_window_attention(q, k, v, window, dilation):
    b, s, h, dh = q.shape
    n_stream = s // dilation
    span = window // dilation
    nb = -(-n_stream // BLOCK)
    lp = nb * BLOCK

    def to_streams(t):
        t = t.reshape(b, n_stream, dilation, h, dh).transpose(0, 2, 1, 3, 4)
        return jnp.pad(t, ((0, 0), (0, 0), (0, lp - n_stream), (0, 0), (0, 0)))

    def banded(t):
        t = jnp.pad(t, ((0, 0), (0, 0), (BLOCK, 0), (0, 0), (0, 0)))
        t = t.reshape(b, dilation, nb + 1, BLOCK, h, dh)
        return jnp.concatenate([t[:, :, :-1], t[:, :, 1:]], axis=3)

    qb = to_streams(q).reshape(b, dilation, nb, BLOCK, h, dh)
    kb = banded(to_streams(k))
    vb = banded(to_streams(v))

    scores = jnp.einsum('brnqhd,brnkhd->brnhqk', qb, kb).astype(jnp.float32) * (dh ** -0.5)
    qi = jnp.arange(BLOCK)[:, None]
    kj = jnp.arange(2 * BLOCK)[None, :]
    blk = jnp.arange(nb)[:, None, None]
    dist = qi + BLOCK - kj
    kpos = blk * BLOCK + kj - BLOCK
    valid = (dist >= 0) & (dist <= span) & (kpos >= 0)
    scores = jnp.where(valid[None, None, :, None], scores, -jnp.inf)
    lse = jax.nn.logsumexp(scores, axis=-1)
    p = jnp.exp(scores - lse[..., None])
    out = jnp.einsum('brnhqk,brnkhd->brnqhd', p.astype(vb.dtype), vb).astype(jnp.float32)

    out = out.reshape(b, dilation, lp, h, dh)[:, :, :n_stream]
    out = out.transpose(0, 2, 1, 3, 4).reshape(b, s, h, dh)
    lse = lse.transpose(0, 1, 2, 4, 3).reshape(b, dilation, lp, h)[:, :, :n_stream]
    lse = lse.transpose(0, 2, 1, 3).reshape(b, s, h)
    return out, lse


def memory_cross_attention(qm, mem_n, w_mem_kv):
    b, s, _ = qm.shape
    kv = mem_n @ w_mem_kv
    km, vm = jnp.split(kv, 2, axis=-1)
    km = km.reshape(b, N_MEM, MEM_HEADS, HEAD_DIM)
    vm = vm.reshape(b, N_MEM, MEM_HEADS, HEAD_DIM)
    qh = qm.reshape(b, s, MEM_HEADS, HEAD_DIM)
    scores = jnp.einsum('bshd,bmhd->bhsm', qh, km).astype(jnp.float32) * (HEAD_DIM ** -0.5)
    p = jax.nn.softmax(scores, axis=-1)
    out = jnp.einsum('bhsm,bmhd->bshd', p.astype(vm.dtype), vm)
    return out.reshape(b, s, MEM_WIDTH)


def dilated_attention_layer(hn, positions, mem_n, w_in, w_mem_kv, w_out):
    b, s, _ = hn.shape
    gw = N_GROUPS * GROUP_WIDTH
    proj = hn @ w_in
    q, k, v, qm, z = jnp.split(proj, [gw, 2 * gw, 3 * gw, 3 * gw + MEM_WIDTH], axis=-1)
    n_heads = N_GROUPS * HEADS_PER_GROUP
    q = partial_rope(q.reshape(b, s, n_heads, HEAD_DIM), positions)
    k = partial_rope(k.reshape(b, s, n_heads, HEAD_DIM), positions)
    v = v.reshape(b, s, n_heads, HEAD_DIM)
    outs, lses = [], []
    for g, (window, dilation) in enumerate(DIL_GROUPS):
        sl = slice(g * HEADS_PER_GROUP, (g + 1) * HEADS_PER_GROUP)
        o, l = dilated_window_attention(q[:, :, sl], k[:, :, sl], v[:, :, sl], window, dilation)
        outs.append(o)
        lses.append(l)
    wts = jax.nn.softmax(jnp.stack(lses, axis=0), axis=0)
    mix = jnp.sum(wts[..., None] * jnp.stack(outs, axis=0), axis=0)
    mix = mix.reshape(b, s, GROUP_WIDTH).astype(hn.dtype)
    mem_out = memory_cross_attention(qm, mem_n, w_mem_kv)
    y = jnp.concatenate([mix, mem_out], axis=-1) * jax.nn.silu(z)
    return y @ w_out


def short_conv_layer(hn, mem_n, w_in, conv_w, w_mem_kv, w_out):
    c = CONV_WIDTH
    proj = hn @ w_in
    bg, cg, u, qm, z = jnp.split(proj, [c, 2 * c, 3 * c, 3 * c + MEM_WIDTH], axis=-1)
    conv = lax.conv_general_dilated(
        cg * u, conv_w[:, None, :].astype(u.dtype),
        window_strides=(1,), padding=((CONV_K - 1, 0),),
        dimension_numbers=('NWC', 'WIO', 'NWC'), feature_group_count=c)
    mix = bg * conv
    mem_out = memory_cross_attention(qm, mem_n, w_mem_kv)
    y = jnp.concatenate([mix, mem_out], axis=-1) * jax.nn.silu(z)
    return y @ w_out


def _fwd_setup_inputs(seed: int = 0) -> dict:
    key = jax.random.key(seed)
    ks = jax.random.split(key, 14)
    f32 = jnp.float32

    def nrm(k, shape, fan_in):
        return jax.random.normal(k, shape, f32) * (fan_in ** -0.5)

    x = jax.random.normal(ks[0], (BATCH, SEQ, D_MODEL), f32)
    mem = jax.random.normal(ks[1], (BATCH, N_MEM, D_MODEL), f32)
    offset = jax.random.randint(ks[2], (BATCH, 1), 0, 1024, dtype=jnp.int32)
    positions = offset + jnp.arange(SEQ, dtype=jnp.int32)[None, :]
    norm_g = 1.0 + 0.05 * jax.random.normal(ks[3], (DEPTH, D_MODEL), f32)
    mem_norm_g = 1.0 + 0.05 * jax.random.normal(ks[4], (DEPTH, D_MODEL), f32)
    w_mem_kv = nrm(ks[5], (DEPTH, D_MODEL, 2 * MEM_WIDTH), D_MODEL)
    attn_w_in = nrm(ks[6], (N_ATTN_LAYERS, D_MODEL, IN_A), D_MODEL)
    attn_w_out = nrm(ks[7], (N_ATTN_LAYERS, BRANCH_A, D_MODEL), BRANCH_A)
    conv_w_in = nrm(ks[8], (N_CONV_LAYERS, D_MODEL, IN_B), D_MODEL)
    conv_w = nrm(ks[9], (N_CONV_LAYERS, CONV_K, CONV_WIDTH), CONV_K)
    conv_w_out = nrm(ks[10], (N_CONV_LAYERS, BRANCH_B, D_MODEL), BRANCH_B)
    final_g = 1.0 + 0.05 * jax.random.normal(ks[11], (D_MODEL,), f32)
    return {"x": x, "mem": mem, "positions": positions, "norm_g": norm_g,
            "mem_norm_g": mem_norm_g, "w_mem_kv": w_mem_kv,
            "attn_w_in": attn_w_in, "attn_w_out": attn_w_out,
            "conv_w_in": conv_w_in, "conv_w": conv_w, "conv_w_out": conv_w_out,
            "final_g": final_g}


def _fwd_reference(x, mem, positions, norm_g, mem_norm_g, w_mem_kv, attn_w_in, attn_w_out,
              conv_w_in, conv_w, conv_w_out, final_g):
    h = x
    for i in range(DEPTH):
        j = i // N_MIXERS
        hn = rms_norm(h, norm_g[i])
        mem_n = rms_norm(mem, mem_norm_g[i])
        if i % N_MIXERS == 0:
            delta = dilated_attention_layer(hn, positions, mem_n, attn_w_in[j],
                                            w_mem_kv[i], attn_w_out[j])
        else:
            delta = short_conv_layer(hn, mem_n, conv_w_in[j], conv_w[j],
                                     w_mem_kv[i], conv_w_out[j])
        h = h + delta
    return rms_norm(h, final_g)


import jax as _jax
import jax.numpy as _jnp

TWIN_FORMAT = 'train_step'
FWD_PARAMS = ['x', 'mem', 'positions', 'norm_g', 'mem_norm_g', 'w_mem_kv', 'attn_w_in', 'attn_w_out', 'conv_w_in', 'conv_w', 'conv_w_out', 'final_g']
TWIN_WEIGHTS = ['norm_g', 'mem_norm_g', 'w_mem_kv', 'attn_w_in', 'attn_w_out', 'conv_w_in', 'conv_w', 'conv_w_out', 'final_g']
TWIN_DIFF_INPUT = 'x'
TWIN_INPUTS = ['x', 'mem', 'positions', 'norm_g', 'mem_norm_g', 'w_mem_kv', 'attn_w_in', 'attn_w_out', 'conv_w_in', 'conv_w', 'conv_w_out', 'final_g', 'loss_target', 'm_norm_g', 'm_mem_norm_g', 'm_w_mem_kv', 'm_attn_w_in', 'm_attn_w_out', 'm_conv_w_in', 'm_conv_w', 'm_conv_w_out', 'm_final_g', 'v_norm_g', 'v_mem_norm_g', 'v_w_mem_kv', 'v_attn_w_in', 'v_attn_w_out', 'v_conv_w_in', 'v_conv_w', 'v_conv_w_out', 'v_final_g']
TWIN_OUTPUTS = ['loss', 'grad_x', 'grad_norm_g', 'grad_mem_norm_g', 'grad_w_mem_kv', 'grad_attn_w_in', 'grad_attn_w_out', 'grad_conv_w_in', 'grad_conv_w', 'grad_conv_w_out', 'grad_final_g', 'delta_norm_g', 'delta_mem_norm_g', 'delta_w_mem_kv', 'delta_attn_w_in', 'delta_attn_w_out', 'delta_conv_w_in', 'delta_conv_w', 'delta_conv_w_out', 'delta_final_g', 'new_m_norm_g', 'new_m_mem_norm_g', 'new_m_w_mem_kv', 'new_m_attn_w_in', 'new_m_attn_w_out', 'new_m_conv_w_in', 'new_m_conv_w', 'new_m_conv_w_out', 'new_m_final_g', 'new_v_norm_g', 'new_v_mem_norm_g', 'new_v_w_mem_kv', 'new_v_attn_w_in', 'new_v_attn_w_out', 'new_v_conv_w_in', 'new_v_conv_w', 'new_v_conv_w_out', 'new_v_final_g']
TWIN_LEAF_KINDS = {'loss': 'loss', 'grad_x': 'grad_x', 'grad_norm_g': 'grad_w', 'grad_mem_norm_g': 'grad_w', 'grad_w_mem_kv': 'grad_w', 'grad_attn_w_in': 'grad_w', 'grad_attn_w_out': 'grad_w', 'grad_conv_w_in': 'grad_w', 'grad_conv_w': 'grad_w', 'grad_conv_w_out': 'grad_w', 'grad_final_g': 'grad_w', 'delta_norm_g': 'delta_w', 'delta_mem_norm_g': 'delta_w', 'delta_w_mem_kv': 'delta_w', 'delta_attn_w_in': 'delta_w', 'delta_attn_w_out': 'delta_w', 'delta_conv_w_in': 'delta_w', 'delta_conv_w': 'delta_w', 'delta_conv_w_out': 'delta_w', 'delta_final_g': 'delta_w', 'new_m_norm_g': 'new_m', 'new_m_mem_norm_g': 'new_m', 'new_m_w_mem_kv': 'new_m', 'new_m_attn_w_in': 'new_m', 'new_m_attn_w_out': 'new_m', 'new_m_conv_w_in': 'new_m', 'new_m_conv_w': 'new_m', 'new_m_conv_w_out': 'new_m', 'new_m_final_g': 'new_m', 'new_v_norm_g': 'new_v', 'new_v_mem_norm_g': 'new_v', 'new_v_w_mem_kv': 'new_v', 'new_v_attn_w_in': 'new_v', 'new_v_attn_w_out': 'new_v', 'new_v_conv_w_in': 'new_v', 'new_v_conv_w': 'new_v', 'new_v_conv_w_out': 'new_v', 'new_v_final_g': 'new_v'}


def _forward(args):
    return _fwd_reference(*[args[k] for k in FWD_PARAMS])


def _output_shape():
    out = _jax.eval_shape(lambda: _forward(_fwd_setup_inputs(0)))
    return out.shape, out.dtype

N_MICROBATCH = 1
ADAM_LR = 0.001
ADAM_B1 = 0.9
ADAM_B2 = 0.999
ADAM_EPS = 1e-08
ADAM_WD = 0.01
ADAM_STEP = 10
PER_EXAMPLE_BATCH_AXIS = {'x': 0, 'mem': 0, 'positions': 0, 'loss_target': 0}
SHARED_INPUTS = []
_WEIGHT_DTYPES = {'norm_g': _jnp.float32, 'mem_norm_g': _jnp.float32, 'w_mem_kv': _jnp.float32, 'attn_w_in': _jnp.float32, 'attn_w_out': _jnp.float32, 'conv_w_in': _jnp.float32, 'conv_w': _jnp.float32, 'conv_w_out': _jnp.float32, 'final_g': _jnp.float32}
MOMENT_SCALE = {'norm_g': 8.595805e-02, 'mem_norm_g': 6.842644e-03, 'w_mem_kv': 9.420699e-03, 'attn_w_in': 1.255412e-02, 'attn_w_out': 1.597460e-02, 'conv_w_in': 5.564802e-02, 'conv_w': 5.890486e-02, 'conv_w_out': 5.915467e-02, 'final_g': 1.603775e+01}


def _to_microbatches(a, axis):
    t = _jnp.moveaxis(a, axis, 0)
    t = t.reshape((N_MICROBATCH, t.shape[0] // N_MICROBATCH) + t.shape[1:])
    return _jnp.moveaxis(t, 1, axis + 1)


def setup_inputs(seed: int = 0) -> dict:
    inp = _fwd_setup_inputs(seed)
    key = _jax.random.fold_in(_jax.random.key(seed), 7919)
    shape, _ = _output_shape()
    out = dict(inp)
    out["loss_target"] = _jax.random.normal(_jax.random.fold_in(key, 0), shape, _jnp.float32)
    for i, name in enumerate(TWIN_WEIGHTS):
        w = inp[name].astype(_jnp.float32)
        if MOMENT_SCALE is None:
            s = _jnp.sqrt(_jnp.mean(_jnp.square(w)) + 1e-30)
        else:
            s = MOMENT_SCALE[name]
        km, kv = _jax.random.split(_jax.random.fold_in(key, i + 1))
        out[name] = w
        out["m_" + name] = s * _jax.random.normal(km, w.shape, _jnp.float32)
        out["v_" + name] = (s * s) * _jax.random.uniform(kv, w.shape, _jnp.float32, 0.5, 1.5)
    if N_MICROBATCH > 1:
        for name, axis in PER_EXAMPLE_BATCH_AXIS.items():
            out[name] = _to_microbatches(out[name], axis)
    return {'x': out['x'], 'mem': out['mem'], 'positions': out['positions'], 'norm_g': out['norm_g'], 'mem_norm_g': out['mem_norm_g'], 'w_mem_kv': out['w_mem_kv'], 'attn_w_in': out['attn_w_in'], 'attn_w_out': out['attn_w_out'], 'conv_w_in': out['conv_w_in'], 'conv_w': out['conv_w'], 'conv_w_out': out['conv_w_out'], 'final_g': out['final_g'], 'loss_target': out['loss_target'], 'm_norm_g': out['m_norm_g'], 'm_mem_norm_g': out['m_mem_norm_g'], 'm_w_mem_kv': out['m_w_mem_kv'], 'm_attn_w_in': out['m_attn_w_in'], 'm_attn_w_out': out['m_attn_w_out'], 'm_conv_w_in': out['m_conv_w_in'], 'm_conv_w': out['m_conv_w'], 'm_conv_w_out': out['m_conv_w_out'], 'm_final_g': out['m_final_g'], 'v_norm_g': out['v_norm_g'], 'v_mem_norm_g': out['v_mem_norm_g'], 'v_w_mem_kv': out['v_w_mem_kv'], 'v_attn_w_in': out['v_attn_w_in'], 'v_attn_w_out': out['v_attn_w_out'], 'v_conv_w_in': out['v_conv_w_in'], 'v_conv_w': out['v_conv_w'], 'v_conv_w_out': out['v_conv_w_out'], 'v_final_g': out['v_final_g']}


def _loss(weights, diff, rest, loss_target):
    with _jax.named_scope("forward"):
        args = {**rest, TWIN_DIFF_INPUT: diff, **{k: w.astype(_WEIGHT_DTYPES[k]) for k, w in weights.items()}}
        y = _forward(args)
    with _jax.named_scope("loss_head"):
        err = _jnp.square(y.astype(_jnp.float32) - loss_target)
        return 0.5 * _jnp.sum(_jnp.mean(err, axis=-1)) if err.ndim else 0.5 * err


def _adamw(w, g, m, v):
    m = ADAM_B1 * m + (1.0 - ADAM_B1) * g
    v = ADAM_B2 * v + (1.0 - ADAM_B2) * _jnp.square(g)
    m_hat = m / (1.0 - ADAM_B1 ** ADAM_STEP)
    v_hat = v / (1.0 - ADAM_B2 ** ADAM_STEP)
    delta = -ADAM_LR * (m_hat / (_jnp.sqrt(v_hat) + ADAM_EPS) + ADAM_WD * w)
    return delta, m, v


def reference(x, mem, positions, norm_g, mem_norm_g, w_mem_kv, attn_w_in, attn_w_out, conv_w_in, conv_w, conv_w_out, final_g, loss_target, m_norm_g, m_mem_norm_g, m_w_mem_kv, m_attn_w_in, m_attn_w_out, m_conv_w_in, m_conv_w, m_conv_w_out, m_final_g, v_norm_g, v_mem_norm_g, v_w_mem_kv, v_attn_w_in, v_attn_w_out, v_conv_w_in, v_conv_w, v_conv_w_out, v_final_g):
    given = dict(x=x, mem=mem, positions=positions, norm_g=norm_g, mem_norm_g=mem_norm_g, w_mem_kv=w_mem_kv, attn_w_in=attn_w_in, attn_w_out=attn_w_out, conv_w_in=conv_w_in, conv_w=conv_w, conv_w_out=conv_w_out, final_g=final_g, loss_target=loss_target, m_norm_g=m_norm_g, m_mem_norm_g=m_mem_norm_g, m_w_mem_kv=m_w_mem_kv, m_attn_w_in=m_attn_w_in, m_attn_w_out=m_attn_w_out, m_conv_w_in=m_conv_w_in, m_conv_w=m_conv_w, m_conv_w_out=m_conv_w_out, m_final_g=m_final_g, v_norm_g=v_norm_g, v_mem_norm_g=v_mem_norm_g, v_w_mem_kv=v_w_mem_kv, v_attn_w_in=v_attn_w_in, v_attn_w_out=v_attn_w_out, v_conv_w_in=v_conv_w_in, v_conv_w=v_conv_w, v_conv_w_out=v_conv_w_out, v_final_g=v_final_g)
    weights = {n: given[n] for n in TWIN_WEIGHTS}
    shared = {n: given[n] for n in SHARED_INPUTS}
    per_example = {n: given[n] for n in ['x', 'mem', 'positions']}
    grad_fn = _jax.value_and_grad(_loss, argnums=(0, 1))

    def one_microbatch(ex, loss_target):
        ex = dict(ex)
        diff = ex.pop(TWIN_DIFF_INPUT)
        return grad_fn(weights, diff, {**shared, **ex}, loss_target)

    if N_MICROBATCH == 1:
        loss, (grad_w, grad_x) = one_microbatch(per_example, given["loss_target"])
    else:
        def body(carry, xs):
            loss_sum, grad_sum = carry
            l_k, (gw_k, gx_k) = one_microbatch(xs[0], xs[1])
            with _jax.named_scope("update"):
                return (loss_sum + l_k, _jax.tree.map(_jnp.add, grad_sum, gw_k)), gx_k

        init = (_jnp.zeros((), _jnp.float32), _jax.tree.map(_jnp.zeros_like, weights))
        (loss, grad_w), grad_x = _jax.lax.scan(body, init, (per_example, given["loss_target"]))
    with _jax.named_scope("update"):
        delta_w, new_m, new_v = {}, {}, {}
        for n in TWIN_WEIGHTS:
            delta_w[n], new_m[n], new_v[n] = _adamw(weights[n], grad_w[n], given["m_" + n], given["v_" + n])
    return (loss, grad_x, *[grad_w[n] for n in TWIN_WEIGHTS], *[delta_w[n] for n in TWIN_WEIGHTS],
            *[new_m[n] for n in TWIN_WEIGHTS], *[new_v[n] for n in TWIN_WEIGHTS])
```

```python
import functools

import numpy as np
import jax
import jax.numpy as jnp
from jax import lax
from jax.experimental import pallas as pl
from jax.experimental.pallas import tpu as pltpu

F32 = jnp.float32
BF16 = jnp.bfloat16

S = 2048
D = 1024
TM = 256
NT = S // TM
HD = 64
GW = 512
NQ = 3 * GW
MW = 256
NM = 256
IN_A = 3 * NQ + MW + GW + MW
IN_B = 3 * D + MW + D + MW
BR_A = GW + MW
BR_B = D + MW
SH_A = IN_A // 4
SH_B = IN_B // 4
SH_O = D // 4
QBLK = 128
DILATIONS = (1, 4, 16)
EPS = 1e-6
SCALE = HD ** -0.5
NEG = -1e30
ROPE_THETA = 500000.0

ADAM_LR = 0.001
ADAM_B1 = 0.9
ADAM_B2 = 0.999
ADAM_EPS = 1e-08
ADAM_WD = 0.01
ADAM_STEP = 10

VMEM_LIMIT_BYTES = 60 * 1024 * 1024


def _params(sem=None):
    if sem is None:
        return pltpu.CompilerParams(vmem_limit_bytes=VMEM_LIMIT_BYTES)
    return pltpu.CompilerParams(dimension_semantics=sem, vmem_limit_bytes=VMEM_LIMIT_BYTES)


def _full(shape):
    nd = len(shape)
    return pl.BlockSpec(shape, lambda *_: (0,) * nd)


def _rows(width, tm=TM):
    return pl.BlockSpec((tm, width), lambda i: (i, 0))


def _sds(shape, dtype):
    return jax.ShapeDtypeStruct(shape, dtype)


def _silu_parts(z):
    sig = 1.0 / (1.0 + jnp.exp(-z))
    return z * sig, sig * (1.0 + z * (1.0 - sig))


def _dot(a, b):
    return jnp.dot(a, b, preferred_element_type=F32)


def _dot_nt(a, b):
    return lax.dot_general(a, b, (((1,), (1,)), ((), ())), preferred_element_type=F32)


def _dot_tn(a, b):
    return lax.dot_general(a, b, (((0,), (0,)), ((), ())), preferred_element_type=F32)


def _rope_fwd(t, c, s1, s2):
    return t * c + pltpu.roll(t, 120, 1) * s1 + pltpu.roll(t, 8, 1) * s2


def _rope_bwd(g, c, s1, s2):
    return g * c + pltpu.roll(g * s1, 8, 1) + pltpu.roll(g * s2, 120, 1)


def _mem_attn(qm, kv):
    res = []
    for h in range(MW // HD):
        sl = slice(h * HD, (h + 1) * HD)
        s = _dot_nt(qm[:, sl], kv[:, sl]) * SCALE
        e = jnp.exp(s - jnp.max(s, axis=-1, keepdims=True))
        p = e / jnp.sum(e, axis=-1, keepdims=True)
        res.append((p, _dot(p.astype(BF16), kv[:, MW + h * HD:MW + (h + 1) * HD])))
    return res


def _mem_attn_bwd(dmo, heads, qm, kv, dqm_store, dkv_ref):
    for h, (p, mo) in enumerate(heads):
        sl = slice(h * HD, (h + 1) * HD)
        vs = slice(MW + h * HD, MW + (h + 1) * HD)
        dmo_h = dmo[:, sl]
        dmo_b = dmo_h.astype(BF16)
        dp = _dot_nt(dmo_b, kv[:, vs])
        delta = jnp.sum(dmo_h * mo, axis=-1, keepdims=True)
        ds = (p * (dp - delta) * SCALE).astype(BF16)
        dqm_store(h, _dot(ds, kv[:, sl]))
        dkv_ref[:, sl] += _dot_tn(ds, qm[:, sl])
        dkv_ref[:, vs] += _dot_tn(p.astype(BF16), dmo_b)


def _merge(o_refs, l_refs):
    ls = [r[...] for r in l_refs]
    m = jnp.maximum(jnp.maximum(ls[0], ls[1]), ls[2])
    es = [jnp.exp(l - m) for l in ls]
    inv = 1.0 / (es[0] + es[1] + es[2])
    ws = [e * inv for e in es]
    os_ = [r[...] for r in o_refs]
    mix = ws[0] * os_[0] + ws[1] * os_[1] + ws[2] * os_[2]
    return ws, mix


def _conv_taps(cg, u, cgp, up, first):
    a = cg * u
    ap = jnp.where(first, 0.0, cgp * up)
    row = lax.broadcasted_iota(jnp.int32, a.shape, 0)
    a1 = jnp.where(row == 0, ap[7:8, :], pltpu.roll(a, 1, 0))
    a2 = jnp.where(row == 0, ap[6:7, :], jnp.where(row == 1, ap[7:8, :], pltpu.roll(a, 2, 0)))
    return a, a1, a2


def _rope_tables(posf):
    half = 8
    invf = np.float32(ROPE_THETA) ** (-np.arange(half, dtype=np.float32) * np.float32(2.0 / 16))
    lane = np.arange(128)
    table = np.where((lane % HD) < 16, invf[lane % half], 0.0).astype(np.float32)[None, :]

    def body(pos_ref, invf_ref, c_ref, s1_ref, s2_ref):
        ang = pos_ref[...] * invf_ref[...]
        jm = lax.broadcasted_iota(jnp.int32, ang.shape, 1) & (HD - 1)
        cs = jnp.cos(ang)
        sn = jnp.sin(ang)
        c_ref[...] = jnp.where(jm < 16, cs, 1.0)
        s1_ref[...] = jnp.where(jm < 8, -sn, 0.0)
        s2_ref[...] = jnp.where((jm >= 8) & (jm < 16), sn, 0.0)

    out = _sds((S, 128), F32)
    return pl.pallas_call(
        body, name="rope_tables", grid=(NT,),
        in_specs=[_rows(1), _full((1, 128))],
        out_specs=[_rows(128)] * 3, out_shape=[out] * 3,
        compiler_params=_params(("parallel",)),
    )(posf, jnp.asarray(table))


def _in_proj_a(x, g0, w_in, c, s1, s2):
    def body(x_ref, g_ref, w_ref, c_ref, s1_ref, s2_ref, hn_ref, q_ref, k_ref, v_ref, qm_ref, z_ref, proj):
        xf = x_ref[...]
        hn = xf * lax.rsqrt(jnp.mean(xf * xf, axis=-1, keepdims=True) + EPS) * g_ref[...]
        hb = hn.astype(BF16)
        hn_ref[...] = hb
        for s in range(4):
            proj[:, s * SH_A:(s + 1) * SH_A] = _dot(hb, w_ref[s])
        cc, a1, a2 = c_ref[...], s1_ref[...], s2_ref[...]
        for j in range(NQ // 128):
            q_ref[:, j * 128:(j + 1) * 128] = _rope_fwd(proj[:, j * 128:(j + 1) * 128], cc, a1, a2).astype(BF16)
            k_ref[:, j * 128:(j + 1) * 128] = _rope_fwd(
                proj[:, NQ + j * 128:NQ + (j + 1) * 128], cc, a1, a2).astype(BF16)
        v_ref[...] = proj[:, 2 * NQ:3 * NQ].astype(BF16)
        qm_ref[...] = proj[:, 3 * NQ:3 * NQ + MW].astype(BF16)
        z_ref[...] = proj[:, 3 * NQ + MW:]

    return pl.pallas_call(
        body, name="in_proj_a", grid=(NT,),
        in_specs=[_rows(D), _full((1, D)), _full((4, D, SH_A)), _rows(128), _rows(128), _rows(128)],
        out_specs=[_rows(D), _rows(NQ), _rows(NQ), _rows(NQ), _rows(MW), _rows(BR_A)],
        out_shape=[_sds((S, D), BF16), _sds((S, NQ), BF16), _sds((S, NQ), BF16), _sds((S, NQ), BF16),
                   _sds((S, MW), BF16), _sds((S, BR_A), F32)],
        scratch_shapes=[pltpu.VMEM((TM, IN_A), F32)],
        compiler_params=_params(("parallel",)),
    )(x, g0, w_in, c, s1, s2)


def _mem_fwd(mem, mg, wkv):
    def body(mem_ref, mg_ref, w_ref, memn_ref, kv_ref):
        mf = mem_ref[...]
        n = mf * lax.rsqrt(jnp.mean(mf * mf, axis=-1, keepdims=True) + EPS)
        for i in range(2):
            mn = (n * mg_ref[i:i + 1, :]).astype(BF16)
            memn_ref[i] = mn
            acc = _dot(mn[:, 0:NM], w_ref[0, i])
            for s in range(1, 4):
                acc += _dot(mn[:, s * NM:(s + 1) * NM], w_ref[s, i])
            kv_ref[i] = acc.astype(BF16)

    return pl.pallas_call(
        body, name="mem_fwd", grid=(1,),
        in_specs=[_full((NM, D)), _full((2, D)), _full((4, 2, NM, 2 * MW))],
        out_specs=[_full((2, NM, D)), _full((2, NM, 2 * MW))],
        out_shape=[_sds((2, NM, D), BF16), _sds((2, NM, 2 * MW), BF16)],
        compiler_params=_params(("arbitrary",)),
    )(mem, mg, wkv)


def _band_mask(j):
    qi = lax.broadcasted_iota(jnp.int32, (QBLK, 2 * QBLK), 0)
    kj = lax.broadcasted_iota(jnp.int32, (QBLK, 2 * QBLK), 1)
    dist = qi + QBLK - kj
    return (dist >= 0) & (dist <= QBLK) & ((kj >= QBLK) | (j > 0))


def _attn_fwd(q, k, v, g):
    d = DILATIONS[g]
    ln = S // d
    nb = ln // QBLK

    def body(q_ref, k_ref, v_ref, o_ref, l_ref):
        def blk(j, carry):
            r0 = pl.multiple_of(j * QBLK, QBLK)
            p0 = pl.multiple_of(jnp.maximum(j - 1, 0) * QBLK, QBLK)
            qb = q_ref[pl.ds(r0, QBLK), :]
            kk = jnp.concatenate([k_ref[pl.ds(p0, QBLK), :], k_ref[pl.ds(r0, QBLK), :]], axis=0)
            vv = jnp.concatenate([v_ref[pl.ds(p0, QBLK), :], v_ref[pl.ds(r0, QBLK), :]], axis=0)
            valid = _band_mask(j)
            for h in range(GW // HD):
                sl = slice(h * HD, (h + 1) * HD)
                s = jnp.where(valid, _dot_nt(qb[:, sl], kk[:, sl]) * SCALE, NEG)
                m = jnp.max(s, axis=-1, keepdims=True)
                e = jnp.exp(s - m)
                l = jnp.sum(e, axis=-1, keepdims=True)
                o_ref[pl.ds(r0, QBLK), sl] = _dot(e.astype(BF16), vv[:, sl]) / l
                l_ref[pl.ds(r0, QBLK), sl] = jnp.broadcast_to(m + jnp.log(l), (QBLK, HD))
            return carry

        lax.fori_loop(0, nb, blk, 0)

    qkv_spec = pl.BlockSpec((ln, GW), lambda r: (0, r * 3 + g))
    out_spec = pl.BlockSpec((ln, GW), lambda r: (0, r))
    o, l = pl.pallas_call(
        body, name=f"attn_fwd_g{g}", grid=(d,),
        in_specs=[qkv_spec] * 3, out_specs=[out_spec] * 2,
        out_shape=[_sds((ln, d * GW), F32)] * 2,
        compiler_params=_params(("parallel",)),
    )(q.reshape(ln, d * NQ), k.reshape(ln, d * NQ), v.reshape(ln, d * NQ))
    return o.reshape(S, GW), l.reshape(S, GW)


def _attn_out(os_, ls, qm, kv0, z, x, w_out):
    def body(o0, o1, o2, l0, l1, l2, qm_ref, kv_ref, z_ref, x_ref, w_ref, h_ref, ybuf):
        _, mix = _merge((o0, o1, o2), (l0, l1, l2))
        sz, _ = _silu_parts(z_ref[...])
        ybuf[:, :GW] = (mix * sz[:, :GW]).astype(BF16)
        for h, (_, mo) in enumerate(_mem_attn(qm_ref[...], kv_ref[...])):
            sl = slice(GW + h * HD, GW + (h + 1) * HD)
            ybuf[:, sl] = (mo * sz[:, sl]).astype(BF16)
        yb = ybuf[...]
        for s in range(4):
            cs = slice(s * SH_O, (s + 1) * SH_O)
            h_ref[:, cs] = x_ref[:, cs] + _dot(yb, w_ref[s])

    return pl.pallas_call(
        body, name="attn_out", grid=(NT,),
        in_specs=[_rows(GW)] * 6 + [_rows(MW), _full((NM, 2 * MW)), _rows(BR_A), _rows(D), _full((4, BR_A, SH_O))],
        out_specs=_rows(D), out_shape=_sds((S, D), F32),
        scratch_shapes=[pltpu.VMEM((TM, BR_A), BF16)],
        compiler_params=_params(("parallel",)),
    )(*os_, *ls, qm, kv0, z, x, w_out)


def _in_proj_b(h1, g1, w_in):
    def body(x_ref, g_ref, w_ref, hn_ref, bg_ref, cg_ref, u_ref, qm_ref, z_ref, proj):
        xf = x_ref[...]
        hn = xf * lax.rsqrt(jnp.mean(xf * xf, axis=-1, keepdims=True) + EPS) * g_ref[...]
        hb = hn.astype(BF16)
        hn_ref[...] = hb
        for s in range(4):
            proj[:, s * SH_B:(s + 1) * SH_B] = _dot(hb, w_ref[s])
        bg_ref[...] = proj[:, :D]
        cg_ref[...] = proj[:, D:2 * D]
        u_ref[...] = proj[:, 2 * D:3 * D]
        qm_ref[...] = proj[:, 3 * D:3 * D + MW].astype(BF16)
        z_ref[...] = proj[:, 3 * D + MW:]

    return pl.pallas_call(
        body, name="in_proj_b", grid=(NT,),
        in_specs=[_rows(D), _full((1, D)), _full((4, D, SH_B))],
        out_specs=[_rows(D), _rows(D), _rows(D), _rows(D), _rows(MW), _rows(BR_B)],
        out_shape=[_sds((S, D), BF16), _sds((S, D), F32), _sds((S, D), F32), _sds((S, D), F32),
                   _sds((S, MW), BF16), _sds((S, BR_B), F32)],
        scratch_shapes=[pltpu.VMEM((TM, IN_B), F32)],
        compiler_params=_params(("parallel",)),
    )(h1, g1, w_in)


def _prev8(width):
    return pl.BlockSpec((8, width), lambda i: (jnp.maximum(i * (TM // 8) - 1, 0), 0))


def _conv_out_loss(bg, cg, u, cw, qm, kv1, z, h1, w_out, fg, tgt):
    def body(bg_ref, cg_ref, u_ref, cgp_ref, up_ref, cw_ref, qm_ref, kv_ref, z_ref, h_ref, w_ref, fg_ref, t_ref,
             dh_ref, loss_ref, dfg_ref, ybuf):
        i = pl.program_id(0)
        a, a1, a2 = _conv_taps(cg_ref[...], u_ref[...], cgp_ref[...], up_ref[...], i == 0)
        conv = cw_ref[0:1, :] * a2 + cw_ref[1:2, :] * a1 + cw_ref[2:3, :] * a
        sz, _ = _silu_parts(z_ref[...])
        ybuf[:, :D] = (bg_ref[...] * conv * sz[:, :D]).astype(BF16)
        for h, (_, mo) in enumerate(_mem_attn(qm_ref[...], kv_ref[...])):
            sl = slice(D + h * HD, D + (h + 1) * HD)
            ybuf[:, sl] = (mo * sz[:, sl]).astype(BF16)
        h2 = h_ref[...] + _dot(ybuf[...], w_ref[...])
        rstd = lax.rsqrt(jnp.mean(h2 * h2, axis=-1, keepdims=True) + EPS)
        n = h2 * rstd
        fgv = fg_ref[...]
        err = n * fgv - t_ref[...]
        dout = err * (1.0 / D)
        dn = dout * fgv
        dh_ref[...] = rstd * (dn - n * jnp.mean(dn * n, axis=-1, keepdims=True))

        @pl.when(i == 0)
        def _():
            loss_ref[...] = jnp.zeros_like(loss_ref)
            dfg_ref[...] = jnp.zeros_like(dfg_ref)

        loss_ref[...] += jnp.sum(err * err) * (0.5 / D)
        dfg_ref[...] += jnp.sum(dout * n, axis=0, keepdims=True)

    return pl.pallas_call(
        body, name="conv_out_loss", grid=(NT,),
        in_specs=[_rows(D), _rows(D), _rows(D), _prev8(D), _prev8(D), _full((8, D)), _rows(MW),
                  _full((NM, 2 * MW)), _rows(BR_B), _rows(D), _full((BR_B, D)), _full((1, D)), _rows(D)],
        out_specs=[_rows(D), _full((1, 128)), _full((1, D))],
        out_shape=[_sds((S, D), F32), _sds((1, 128), F32), _sds((1, D), F32)],
        scratch_shapes=[pltpu.VMEM((TM, BR_B), BF16)],
        compiler_params=_params(("arbitrary",)),
    )(bg, cg, u, cg, u, cw, qm, kv1, z, h1, w_out, fg, tgt)


def _conv_bwd(dh2, bg, cg, u, cw, qm, kv1, z, w_out):
    rev = lambda i: (NT - 1 - i, 0)
    rows = lambda w: pl.BlockSpec((TM, w), rev)
    prev8 = pl.BlockSpec((8, D), lambda i: (jnp.maximum((NT - 1 - i) * (TM // 8) - 1, 0), 0))

    def body(dh_ref, bg_ref, cg_ref, u_ref, cgp_ref, up_ref, cw_ref, qm_ref, kv_ref, z_ref, w_ref,
             dproj_ref, dw_ref, dcw_ref, dkv_ref, ybuf, carry):
        i = pl.program_id(0)

        @pl.when(i == 0)
        def _():
            dw_ref[...] = jnp.zeros_like(dw_ref)
            dcw_ref[...] = jnp.zeros_like(dcw_ref)
            dkv_ref[...] = jnp.zeros_like(dkv_ref)
            carry[...] = jnp.zeros_like(carry)

        bgv, cgv, uv = bg_ref[...], cg_ref[...], u_ref[...]
        a, a1, a2 = _conv_taps(cgv, uv, cgp_ref[...], up_ref[...], i == NT - 1)
        w0, w1, w2 = cw_ref[0:1, :], cw_ref[1:2, :], cw_ref[2:3, :]
        conv = w0 * a2 + w1 * a1 + w2 * a
        mix = bgv * conv
        zv = z_ref[...]
        sz, dsz = _silu_parts(zv)
        qmv, kvv = qm_ref[...], kv_ref[...]
        heads = _mem_attn(qmv, kvv)
        ybuf[:, :D] = (mix * sz[:, :D]).astype(BF16)
        for h, (_, mo) in enumerate(heads):
            sl = slice(D + h * HD, D + (h + 1) * HD)
            ybuf[:, sl] = (mo * sz[:, sl]).astype(BF16)
        dhb = dh_ref[...].astype(BF16)
        dw_ref[...] += _dot_tn(ybuf[...], dhb)
        dy = _dot_nt(dhb, w_ref[...])
        dcat = dy * sz
        dproj_ref[:, 3 * D + MW:3 * D + MW + D] = (dy[:, :D] * mix * dsz[:, :D]).astype(BF16)
        for h, (_, mo) in enumerate(heads):
            sl = slice(D + h * HD, D + (h + 1) * HD)
            dproj_ref[:, 3 * D + MW + D + h * HD:3 * D + MW + D + (h + 1) * HD] = (
                dy[:, sl] * mo * dsz[:, sl]).astype(BF16)
        dmix = dcat[:, :D]
        dproj_ref[:, :D] = (dmix * conv).astype(BF16)
        dc = dmix * bgv
        nxt = carry[...]
        row = lax.broadcasted_iota(jnp.int32, dc.shape, 0)
        dc1 = jnp.where(row == TM - 1, nxt[0:1, :], pltpu.roll(dc, TM - 1, 0))
        dc2 = jnp.where(row == TM - 2, nxt[0:1, :], jnp.where(row == TM - 1, nxt[1:2, :], pltpu.roll(dc, TM - 2, 0)))
        carry[...] = dc[0:8, :]
        da = w2 * dc + w1 * dc1 + w0 * dc2
        dproj_ref[:, D:2 * D] = (da * uv).astype(BF16)
        dproj_ref[:, 2 * D:3 * D] = (da * cgv).astype(BF16)
        dcw_ref[0:1, :] += jnp.sum(dc * a2, axis=0, keepdims=True)
        dcw_ref[1:2, :] += jnp.sum(dc * a1, axis=0, keepdims=True)
        dcw_ref[2:3, :] += jnp.sum(dc * a, axis=0, keepdims=True)

        def dqm_store(h, val):
            dproj_ref[:, 3 * D + h * HD:3 * D + (h + 1) * HD] = val.astype(BF16)

        _mem_attn_bwd(dcat[:, D:], heads, qmv, kvv, dqm_store, dkv_ref)

    return pl.pallas_call(
        body, name="conv_bwd", grid=(NT,),
        in_specs=[rows(D), rows(D), rows(D), rows(D), prev8, prev8, _full((8, D)), rows(MW),
                  _full((NM, 2 * MW)), rows(BR_B), _full((BR_B, D))],
        out_specs=[rows(IN_B), _full((BR_B, D)), _full((8, D)), _full((NM, 2 * MW))],
        out_shape=[_sds((S, IN_B), BF16), _sds((BR_B, D), F32), _sds((8, D), F32), _sds((NM, 2 * MW), F32)],
        scratch_shapes=[pltpu.VMEM((TM, BR_B), BF16), pltpu.VMEM((8, D), F32)],
        compiler_params=_params(("arbitrary",)),
    )(dh2, bg, cg, u, cg, u, cw, qm, kv1, z, w_out)


def _in_proj_bwd(dproj, w_in, xin, g, dres, width, name):
    sh = width // 4

    def body(dp_ref, w_ref, x_ref, g_ref, dr_ref, dx_ref, dg_ref):
        i = pl.program_id(0)
        dhn = _dot_nt(dp_ref[:, 0:sh], w_ref[0])
        for s in range(1, 4):
            dhn += _dot_nt(dp_ref[:, s * sh:(s + 1) * sh], w_ref[s])
        xf = x_ref[...]
        rstd = lax.rsqrt(jnp.mean(xf * xf, axis=-1, keepdims=True) + EPS)
        n = xf * rstd
        dn = dhn * g_ref[...]
        dx_ref[...] = dr_ref[...] + rstd * (dn - n * jnp.mean(dn * n, axis=-1, keepdims=True))

        @pl.when(i == 0)
        def _():
            dg_ref[...] = jnp.zeros_like(dg_ref)

        dg_ref[...] += jnp.sum(dhn * n, axis=0, keepdims=True)

    return pl.pallas_call(
        body, name=name, grid=(NT,),
        in_specs=[_rows(width), _full((4, D, sh)), _rows(D), _full((1, D)), _rows(D)],
        out_specs=[_rows(D), _full((1, D))],
        out_shape=[_sds((S, D), F32), _sds((1, D), F32)],
        compiler_params=_params(("arbitrary",)),
    )(dproj, w_in, xin, g, dres)


def _w_in_grad(hn, dproj, width, name):
    sh = width // 4

    def body(hn_ref, dp_ref, dw_ref):
        dw_ref[0] = _dot_tn(hn_ref[...], dp_ref[...])

    return pl.pallas_call(
        body, name=name, grid=(4,),
        in_specs=[_full((S, D)), pl.BlockSpec((S, sh), lambda s: (0, s))],
        out_specs=pl.BlockSpec((1, D, sh), lambda s: (s, 0, 0)),
        out_shape=_sds((4, D, sh), F32),
        compiler_params=_params(("parallel",)),
    )(hn, dproj)


def _attn_out_bwd(dh1, os_, ls, qm, kv0, z, w_out):
    ones_bd = np.kron(np.eye(GW // HD, dtype=np.float32), np.ones((HD, HD), np.float32))

    def body(dh_ref, o0, o1, o2, l0, l1, l2, qm_ref, kv_ref, z_ref, w_ref, bd_ref,
             do0, do1, do2, dd0, dd1, dd2, dqm_ref, dz_ref, dw_ref, dkv_ref, ybuf):
        i = pl.program_id(0)

        @pl.when(i == 0)
        def _():
            dw_ref[...] = jnp.zeros_like(dw_ref)
            dkv_ref[...] = jnp.zeros_like(dkv_ref)

        ws, mix = _merge((o0, o1, o2), (l0, l1, l2))
        sz, dsz = _silu_parts(z_ref[...])
        qmv, kvv = qm_ref[...], kv_ref[...]
        heads = _mem_attn(qmv, kvv)
        ybuf[:, :GW] = (mix * sz[:, :GW]).astype(BF16)
        for h, (_, mo) in enumerate(heads):
            sl = slice(GW + h * HD, GW + (h + 1) * HD)
            ybuf[:, sl] = (mo * sz[:, sl]).astype(BF16)
        yb = ybuf[...]
        dh = dh_ref[...]
        dy = None
        for s in range(4):
            dhb = dh[:, s * SH_O:(s + 1) * SH_O].astype(BF16)
            dw_ref[s] += _dot_tn(yb, dhb)
            part = _dot_nt(dhb, w_ref[s])
            dy = part if dy is None else dy + part
        dcat = dy * sz
        dz_ref[:, :GW] = (dy[:, :GW] * mix * dsz[:, :GW]).astype(BF16)
        for h, (_, mo) in enumerate(heads):
            sl = slice(GW + h * HD, GW + (h + 1) * HD)
            dz_ref[:, sl] = (dy[:, sl] * mo * dsz[:, sl]).astype(BF16)
        dmix = dcat[:, :GW]
        prod = dmix * mix
        hi = prod.astype(BF16)
        lo = (prod - hi.astype(F32)).astype(BF16)
        bd = bd_ref[...]
        tot = _dot(hi, bd) + _dot(lo, bd)
        for w, do_ref, dd_ref in zip(ws, (do0, do1, do2), (dd0, dd1, dd2)):
            do_ref[...] = (w * dmix).astype(BF16)
            dd_ref[...] = w * tot

        def dqm_store(h, val):
            dqm_ref[:, h * HD:(h + 1) * HD] = val.astype(BF16)

        _mem_attn_bwd(dcat[:, GW:], heads, qmv, kvv, dqm_store, dkv_ref)

    return pl.pallas_call(
        body, name="attn_out_bwd", grid=(NT,),
        in_specs=[_rows(D)] + [_rows(GW)] * 6 + [_rows(MW), _full((NM, 2 * MW)), _rows(BR_A),
                                                   _full((4, BR_A, SH_O)), _full((GW, GW))],
        out_specs=[_rows(GW)] * 6 + [_rows(MW), _rows(BR_A), _full((4, BR_A, SH_O)), _full((NM, 2 * MW))],
        out_shape=[_sds((S, GW), BF16)] * 3 + [_sds((S, GW), F32)] * 3 + [
            _sds((S, MW), BF16), _sds((S, BR_A), BF16), _sds((4, BR_A, SH_O), F32), _sds((NM, 2 * MW), F32)],
        scratch_shapes=[pltpu.VMEM((TM, BR_A), BF16)],
        compiler_params=_params(("arbitrary",)),
    )(dh1, *os_, *ls, qm, kv0, z, w_out, jnp.asarray(ones_bd, dtype=BF16))


def _attn_bwd(q, k, v, do, lse, dd, g):
    d = DILATIONS[g]
    ln = S // d
    nb = ln // QBLK

    def body(q_ref, k_ref, v_ref, do_ref, l_ref, dd_ref, dq_ref, dk_ref, dv_ref):
        dk_ref[...] = jnp.zeros_like(dk_ref)
        dv_ref[...] = jnp.zeros_like(dv_ref)

        def blk(j, carry):
            r0 = pl.multiple_of(j * QBLK, QBLK)
            p0 = pl.multiple_of(jnp.maximum(j - 1, 0) * QBLK, QBLK)
            qb = q_ref[pl.ds(r0, QBLK), :]
            dob = do_ref[pl.ds(r0, QBLK), :]
            lb = l_ref[pl.ds(r0, QBLK), :]
            ddb = dd_ref[pl.ds(r0, QBLK), :]
            kk = jnp.concatenate([k_ref[pl.ds(p0, QBLK), :], k_ref[pl.ds(r0, QBLK), :]], axis=0)
            vv = jnp.concatenate([v_ref[pl.ds(p0, QBLK), :], v_ref[pl.ds(r0, QBLK), :]], axis=0)
            valid = _band_mask(j)
            for h in range(GW // HD):
                sl = slice(h * HD, (h + 1) * HD)
                s = _dot_nt(qb[:, sl], kk[:, sl]) * SCALE
                p = jnp.where(valid, jnp.exp(s - lb[:, h * HD:h * HD + 1]), 0.0)
                dp = _dot_nt(dob[:, sl], vv[:, sl])
                ds = (p * (dp - ddb[:, h * HD:h * HD + 1]) * SCALE).astype(BF16)
                pb = p.astype(BF16)
                dq_ref[pl.ds(r0, QBLK), sl] = _dot(ds, kk[:, sl])
                dkk = _dot_tn(ds, qb[:, sl])
                dvv = _dot_tn(pb, dob[:, sl])
                dk_ref[pl.ds(p0, QBLK), sl] += dkk[:QBLK]
                dk_ref[pl.ds(r0, QBLK), sl] += dkk[QBLK:]
                dv_ref[pl.ds(p0, QBLK), sl] += dvv[:QBLK]
                dv_ref[pl.ds(r0, QBLK), sl] += dvv[QBLK:]
            return carry

        lax.fori_loop(0, nb, blk, 0)

    qkv_spec = pl.BlockSpec((ln, GW), lambda r: (0, r * 3 + g))
    one_spec = pl.BlockSpec((ln, GW), lambda r: (0, r))
    outs = pl.pallas_call(
        body, name=f"attn_bwd_g{g}", grid=(d,),
        in_specs=[qkv_spec] * 3 + [one_spec] * 3, out_specs=[one_spec] * 3,
        out_shape=[_sds((ln, d * GW), F32)] * 3,
        compiler_params=_params(("parallel",)),
    )(q.reshape(ln, d * NQ), k.reshape(ln, d * NQ), v.reshape(ln, d * NQ),
      do.reshape(ln, d * GW), lse.reshape(ln, d * GW), dd.reshape(ln, d * GW))
    return [t.reshape(S, GW) for t in outs]


def _qkv_bwd(dqs, dks, dvs, dqm, dz, c, s1, s2):
    def body(q0, q1, q2, k0, k1, k2, v0, v1, v2, dqm_ref, dz_ref, c_ref, s1_ref, s2_ref, dp_ref):
        cc, a1, a2 = c_ref[...], s1_ref[...], s2_ref[...]
        for g, (qr, kr, vr) in enumerate(((q0, k0, v0), (q1, k1, v1), (q2, k2, v2))):
            for j in range(GW // 128):
                ls_ = slice(j * 128, (j + 1) * 128)
                c0 = g * GW + j * 128
                dp_ref[:, c0:c0 + 128] = _rope_bwd(qr[:, ls_], cc, a1, a2).astype(BF16)
                dp_ref[:, NQ + c0:NQ + c0 + 128] = _rope_bwd(kr[:, ls_], cc, a1, a2).astype(BF16)
            dp_ref[:, 2 * NQ + g * GW:2 * NQ + (g + 1) * GW] = vr[...].astype(BF16)
        dp_ref[:, 3 * NQ:3 * NQ + MW] = dqm_ref[...]
        dp_ref[:, 3 * NQ + MW:] = dz_ref[...]

    return pl.pallas_call(
        body, name="qkv_bwd", grid=(NT,),
        in_specs=[_rows(GW)] * 9 + [_rows(MW), _rows(BR_A), _rows(128), _rows(128), _rows(128)],
        out_specs=_rows(IN_A), out_shape=_sds((S, IN_A), BF16),
        compiler_params=_params(("parallel",)),
    )(*dqs, *dks, *dvs, dqm, dz, c, s1, s2)


def _mem_bwd(mem, mg, memn, wkv, dkv0, dkv1):
    def body(mem_ref, mg_ref, memn_ref, w_ref, d0_ref, d1_ref, dw_ref, dg_ref):
        mf = mem_ref[...]
        n = mf * lax.rsqrt(jnp.mean(mf * mf, axis=-1, keepdims=True) + EPS)
        for i, d_ref in enumerate((d0_ref, d1_ref)):
            dkv = d_ref[...].astype(BF16)
            mn = memn_ref[i]
            for s in range(4):
                cs = slice(s * NM, (s + 1) * NM)
                dw_ref[s, i] = _dot_tn(mn[:, cs], dkv)
                dmn = _dot_nt(dkv, w_ref[s, i])
                dg_ref[i:i + 1, cs] = jnp.sum(dmn * n[:, cs], axis=0, keepdims=True)

    return pl.pallas_call(
        body, name="mem_bwd", grid=(1,),
        in_specs=[_full((NM, D)), _full((2, D)), _full((2, NM, D)), _full((4, 2, NM, 2 * MW)),
                  _full((NM, 2 * MW)), _full((NM, 2 * MW))],
        out_specs=[_full((4, 2, NM, 2 * MW)), _full((2, D))],
        out_shape=[_sds((4, 2, NM, 2 * MW), F32), _sds((2, D), F32)],
        compiler_params=_params(("arbitrary",)),
    )(mem, mg, memn, wkv, dkv0, dkv1)


def _local_step(x, mem, posf, norm_g, mem_norm_g, final_g, tgt, wkv, w_in_a, w_out_a, w_in_b, cw8, w_out_b):
    g0, g1 = norm_g[0:1], norm_g[1:2]
    c, s1, s2 = _rope_tables(posf)
    memn, kv = _mem_fwd(mem, mem_norm_g, wkv)
    hn0, q, k, v, qm0, z0 = _in_proj_a(x, g0, w_in_a, c, s1, s2)
    fwd = [_attn_fwd(q, k, v, g) for g in range(3)]
    os_, ls = [f[0] for f in fwd], [f[1] for f in fwd]
    h1 = _attn_out(os_, ls, qm0, kv[0], z0, x, w_out_a)
    hn1, bg, cg, u, qm1, z1 = _in_proj_b(h1, g1, w_in_b)
    dh2, loss, dfg = _conv_out_loss(bg, cg, u, cw8, qm1, kv[1], z1, h1, w_out_b, final_g.reshape(1, D), tgt)

    dproj_b, dw_out_b, dcw, dkv1 = _conv_bwd(dh2, bg, cg, u, cw8, qm1, kv[1], z1, w_out_b)
    dh1, dg1 = _in_proj_bwd(dproj_b, w_in_b, h1, g1, dh2, IN_B, "in_proj_b_bwd")
    dw_in_b = _w_in_grad(hn1, dproj_b, IN_B, "w_in_b_grad")
    outs = _attn_out_bwd(dh1, os_, ls, qm0, kv[0], z0, w_out_a)
    dos, dds, dqm, dz, dw_out_a, dkv0 = outs[0:3], outs[3:6], outs[6], outs[7], outs[8], outs[9]
    bwd = [_attn_bwd(q, k, v, dos[g], ls[g], dds[g], g) for g in range(3)]
    dproj_a = _qkv_bwd([b[0] for b in bwd], [b[1] for b in bwd], [b[2] for b in bwd], dqm, dz, c, s1, s2)
    gx, dg0 = _in_proj_bwd(dproj_a, w_in_a, x, g0, dh1, IN_A, "in_proj_a_bwd")
    dw_in_a = _w_in_grad(hn0, dproj_a, IN_A, "w_in_a_grad")
    dwkv, dmg = _mem_bwd(mem, mem_norm_g, memn, wkv, dkv0, dkv1)
    small = dict(loss=loss, dnorm=jnp.concatenate([dg0, dg1], axis=0), dmemnorm=dmg, dfinal=dfg, dconv=dcw)
    big = dict(wkv=dwkv, w_in_a=dw_in_a, w_out_a=dw_out_a, w_in_b=dw_in_b, w_out_b=dw_out_b)
    return gx, small, big


MESH = pl.DeviceIdType.MESH
ANY = pl.BlockSpec(memory_space=pl.ANY)
BIG = (("wkv", 2, NM, 2 * MW), ("w_in_a", 1, D, SH_A), ("w_out_a", 1, BR_A, SH_O),
       ("w_in_b", 1, D, SH_B), ("w_out_b", 1, BR_B // 4, D))
NBIG = len(BIG)
CW_ROWS = 8


def _place():
    x, y, c = lax.axis_index("x"), lax.axis_index("y"), lax.axis_index("c")
    chips = ((1 - x, y), (x, 1 - y), (1 - x, 1 - y))
    return x, y, c, chips


def _remote(src, dst, ssem, rsem, dev):
    return pltpu.make_async_remote_copy(src_ref=src, dst_ref=dst, send_sem=ssem, recv_sem=rsem,
                                        device_id=dev, device_id_type=MESH)


def _cast_weights(ws):
    nblk = 4

    def body(*refs):
        for i in range(NBIG):
            refs[NBIG + i][...] = refs[i][...].astype(BF16)

    specs = [pl.BlockSpec((k, r // nblk, cdim), lambda i: (0, i, 0)) for _, k, r, cdim in BIG]
    return pl.pallas_call(
        body, name="cast_weights", grid=(nblk,), in_specs=specs, out_specs=specs,
        out_shape=[_sds((k, r, cdim), BF16) for _, k, r, cdim in BIG],
        compiler_params=_params(("parallel",)),
    )(*ws)


def _gather_weights(wb, cw):
    def body(*refs):
        src = refs[:NBIG + 1]
        dst = refs[NBIG + 1:2 * NBIG + 2]
        loc_sems, send_sems, recv_sems, fsend_sems, frecv_sems = refs[2 * NBIG + 2:]
        x, y, c, chips = _place()
        me = 2 * x + y
        locs = [pltpu.make_async_copy(src[i], dst[i].at[me], loc_sems.at[i]) for i in range(NBIG + 1)]
        for cp in locs:
            cp.start()

        def half(ref, i, which):
            if i == NBIG:
                return ref
            h = BIG[i][2] // 2
            return ref.at[:, pl.ds(which * h, h), :]

        sends = []
        for j, (px, py) in enumerate(chips):
            for i in range(NBIG + 1):
                sends.append(_remote(half(src[i], i, c), half(dst[i].at[me], i, c),
                                     send_sems.at[j, i], recv_sems.at[j, i], (px, py, c)))
        for cp in sends:
            cp.start()
        fwds = []
        for j, (px, py) in enumerate(chips):
            for i in range(NBIG + 1):
                got = half(dst[i].at[2 * px + py], i, c)
                _remote(got, got, send_sems.at[j, i], recv_sems.at[j, i], (px, py, c)).wait_recv()
                if i < NBIG:
                    fwds.append(_remote(got, got, fsend_sems.at[j, i], frecv_sems.at[j, i], (x, y, 1 - c)))
                    fwds[-1].start()
        for j, (px, py) in enumerate(chips):
            for i in range(NBIG):
                got = half(dst[i].at[2 * px + py], i, 1 - c)
                _remote(got, got, fsend_sems.at[j, i], frecv_sems.at[j, i], (x, y, 1 - c)).wait_recv()
        for cp in sends + fwds:
            cp.wait_send()
        for cp in locs:
            cp.wait()

    out_shape = [_sds((4, k, r, cdim), BF16) for _, k, r, cdim in BIG] + [_sds((4, CW_ROWS, SH_O), F32)]
    return pl.pallas_call(
        body, name="gather_weights", in_specs=[ANY] * (NBIG + 1), out_specs=[ANY] * (NBIG + 1), out_shape=out_shape,
        scratch_shapes=[pltpu.SemaphoreType.DMA((NBIG + 1,)), pltpu.SemaphoreType.DMA((3, NBIG + 1)),
                        pltpu.SemaphoreType.DMA((3, NBIG + 1)), pltpu.SemaphoreType.DMA((3, NBIG)),
                        pltpu.SemaphoreType.DMA((3, NBIG))],
    )(*wb, cw)


def _pair_exchange(gs):
    def body(*refs):
        src, dst = refs[:NBIG], refs[NBIG:2 * NBIG]
        send_sems, recv_sems = refs[2 * NBIG:]
        x, y, c, _ = _place()
        cps = []
        for i in range(NBIG):
            h = BIG[i][2] // 2
            cps.append(_remote(src[i].at[:, :, pl.ds((1 - c) * h, h), :], dst[i], send_sems.at[i], recv_sems.at[i],
                               (x, y, 1 - c)))
            cps[-1].start()
        for cp in cps:
            cp.wait()

    return pl.pallas_call(
        body, name="pair_exchange", in_specs=[ANY] * NBIG, out_specs=[ANY] * NBIG,
        out_shape=[_sds((4, k, r // 2, cdim), F32) for _, k, r, cdim in BIG],
        scratch_shapes=[pltpu.SemaphoreType.DMA((NBIG,)), pltpu.SemaphoreType.DMA((NBIG,))],
    )(*gs)


def _pair_sum(place, g, r1, i):
    _, k, r, cdim = BIG[i]
    h = r // 2

    def body(pref, g_ref, r_ref, o_ref):
        o_ref[...] = (g_ref[...] + r_ref[...]).astype(BF16)

    grid_spec = pltpu.PrefetchScalarGridSpec(
        num_scalar_prefetch=1, grid=(4, k),
        in_specs=[pl.BlockSpec((1, 1, h, cdim), lambda s, t, pref: (s, t, pref[0], 0)),
                  pl.BlockSpec((1, 1, h, cdim), lambda s, t, pref: (s, t, 0, 0))],
        out_specs=pl.BlockSpec((1, 1, h, cdim), lambda s, t, pref: (s, t, 0, 0)))
    return pl.pallas_call(
        body, name=f"pair_sum_{BIG[i][0]}", grid_spec=grid_spec, out_shape=_sds((4, k, h, cdim), BF16),
        compiler_params=_params(("parallel", "parallel")),
    )(place, g, r1)


def _chip_exchange(ps):
    def body(*refs):
        src, dst = refs[:NBIG], refs[NBIG:2 * NBIG]
        send_sems, recv_sems = refs[2 * NBIG:]
        x, y, c, chips = _place()
        cps = []
        for j, (px, py) in enumerate(chips):
            for i in range(NBIG):
                cps.append(_remote(src[i].at[2 * px + py], dst[i].at[j], send_sems.at[j, i], recv_sems.at[j, i],
                                   (px, py, c)))
                cps[-1].start()
        for cp in cps:
            cp.wait()

    return pl.pallas_call(
        body, name="chip_exchange", in_specs=[ANY] * NBIG, out_specs=[ANY] * NBIG,
        out_shape=[_sds((3, k, r // 2, cdim), BF16) for _, k, r, cdim in BIG],
        scratch_shapes=[pltpu.SemaphoreType.DMA((3, NBIG)), pltpu.SemaphoreType.DMA((3, NBIG))],
    )(*ps)


def _chip_sum(place, g, r1, r2, i):
    _, k, r, cdim = BIG[i]
    h = r // 2

    def body(pref, g_ref, r1_ref, r2_ref, o_ref):
        acc = g_ref[0, 0] + r1_ref[0, 0]
        for j in range(3):
            acc = acc + r2_ref[j, 0].astype(F32)
        o_ref[0] = acc

    grid_spec = pltpu.PrefetchScalarGridSpec(
        num_scalar_prefetch=1, grid=(k,),
        in_specs=[pl.BlockSpec((1, 1, h, cdim), lambda t, pref: (pref[1], t, pref[0], 0)),
                  pl.BlockSpec((1, 1, h, cdim), lambda t, pref: (pref[1], t, 0, 0)),
                  pl.BlockSpec((3, 1, h, cdim), lambda t, pref: (0, t, 0, 0))],
        out_specs=pl.BlockSpec((1, h, cdim), lambda t, pref: (t, 0, 0)))
    return pl.pallas_call(
        body, name=f"chip_sum_{BIG[i][0]}", grid_spec=grid_spec, out_shape=_sds((k, h, cdim), F32),
        compiler_params=_params(("parallel",)),
    )(place, g, r1, r2)


def _pair_gather(hs):
    def body(*refs):
        src, dst = refs[:NBIG], refs[NBIG:2 * NBIG]
        loc_sems, send_sems, recv_sems = refs[2 * NBIG:]
        x, y, c, _ = _place()
        cps, locs = [], []
        for i in range(NBIG):
            h = BIG[i][2] // 2
            mine = dst[i].at[:, pl.ds(c * h, h), :]
            locs.append(pltpu.make_async_copy(src[i], mine, loc_sems.at[i]))
            locs[-1].start()
            cps.append(_remote(src[i], mine, send_sems.at[i], recv_sems.at[i], (x, y, 1 - c)))
            cps[-1].start()
        for i in range(NBIG):
            h = BIG[i][2] // 2
            theirs = dst[i].at[:, pl.ds((1 - c) * h, h), :]
            _remote(theirs, theirs, send_sems.at[i], recv_sems.at[i], (x, y, 1 - c)).wait_recv()
        for cp in cps:
            cp.wait_send()
        for cp in locs:
            cp.wait()

    return pl.pallas_call(
        body, name="pair_gather", in_specs=[ANY] * NBIG, out_specs=[ANY] * NBIG,
        out_shape=[_sds((k, r, cdim), F32) for _, k, r, cdim in BIG],
        scratch_shapes=[pltpu.SemaphoreType.DMA((NBIG,)), pltpu.SemaphoreType.DMA((NBIG,)),
                        pltpu.SemaphoreType.DMA((NBIG,))],
    )(*hs)


SMALL_ROWS = 40


def _all_reduce_small(pack):
    def body(p_ref, o_ref, slots, send_sems, recv_sems):
        x, y, c, _ = _place()
        me = 4 * x + 2 * y + c
        cps = []
        for r in range(1, 8):
            peer = (x if not r & 4 else 1 - x, y if not r & 2 else 1 - y, c if not r & 1 else 1 - c)
            cps.append(_remote(p_ref, slots.at[r], send_sems.at[r - 1], recv_sems.at[r - 1], peer))
            cps[-1].start()
        slots[0] = p_ref[...]
        for cp in cps:
            cp.wait()
        acc = slots[me]
        for dev in range(1, 8):
            acc = acc + slots[jnp.bitwise_xor(me, dev)]
        o_ref[...] = acc

    vm = pl.BlockSpec(memory_space=pltpu.VMEM)
    return pl.pallas_call(
        body, name="all_reduce_small", in_specs=[vm], out_specs=vm, out_shape=_sds((SMALL_ROWS, D), F32),
        scratch_shapes=[pltpu.VMEM((8, SMALL_ROWS, D), F32), pltpu.SemaphoreType.DMA((7,)),
                        pltpu.SemaphoreType.DMA((7,))],
    )(pack)


def _adamw_math(w, g, m, v):
    m = ADAM_B1 * m + (1.0 - ADAM_B1) * g
    v = ADAM_B2 * v + (1.0 - ADAM_B2) * (g * g)
    m_hat = m / (1.0 - ADAM_B1 ** ADAM_STEP)
    v_hat = v / (1.0 - ADAM_B2 ** ADAM_STEP)
    delta = -ADAM_LR * (m_hat / (jnp.sqrt(v_hat) + ADAM_EPS) + ADAM_WD * w)
    return delta, m, v


def _adamw_big(w, g, m, v, i):
    _, k, r, cdim = BIG[i]
    nblk = 4 if k == 1 else 1

    def body(w_ref, g_ref, m_ref, v_ref, d_ref, nm_ref, nv_ref):
        d_ref[...], nm_ref[...], nv_ref[...] = _adamw_math(w_ref[...], g_ref[...], m_ref[...], v_ref[...])

    spec = pl.BlockSpec((1, r // nblk, cdim), lambda t, b: (t, b, 0))
    return pl.pallas_call(
        body, name=f"adamw_{BIG[i][0]}", grid=(k, nblk), in_specs=[spec] * 4, out_specs=[spec] * 3,
        out_shape=[_sds((k, r, cdim), F32)] * 3,
        compiler_params=_params(("parallel", "parallel")),
    )(w, g, m, v)


def _adamw_small(ws, gs, ms, vs):
    n = len(ws)

    def body(*refs):
        for i in range(n):
            w_ref, g_ref, m_ref, v_ref = refs[i], refs[n + i], refs[2 * n + i], refs[3 * n + i]
            d, nm, nv = _adamw_math(w_ref[...], g_ref[...], m_ref[...], v_ref[...])
            refs[4 * n + i][...] = d
            refs[5 * n + i][...] = nm
            refs[6 * n + i][...] = nv

    specs = [_full(w.shape) for w in ws]
    outs = pl.pallas_call(
        body, name="adamw_small", grid=(1,), in_specs=specs * 4, out_specs=specs * 3,
        out_shape=[_sds(w.shape, F32) for w in ws] * 3,
        compiler_params=_params(("arbitrary",)),
    )(*ws, *gs, *ms, *vs)
    return outs[:n], outs[n:2 * n], outs[2 * n:]


def _pad_rows(a, rows):
    return jnp.pad(a, ((0, rows - a.shape[0]), (0, 0)))


def kernel(x, mem, positions, norm_g, mem_norm_g, w_mem_kv, attn_w_in, attn_w_out, conv_w_in, conv_w, conv_w_out, final_g, loss_target, m_norm_g, m_mem_norm_g, m_w_mem_kv, m_attn_w_in, m_attn_w_out, m_conv_w_in, m_conv_w, m_conv_w_out, m_final_g, v_norm_g, v_mem_norm_g, v_w_mem_kv, v_attn_w_in, v_attn_w_out, v_conv_w_in, v_conv_w, v_conv_w_out, v_final_g):
    mx, my, mc = lax.axis_index("x"), lax.axis_index("y"), lax.axis_index("c")
    place = jnp.stack([mc, 2 * mx + my]).astype(jnp.int32)

    w_big = [w_mem_kv, attn_w_in, attn_w_out, conv_w_in, conv_w_out]
    m_big = [m_w_mem_kv, m_attn_w_in, m_attn_w_out, m_conv_w_in, m_conv_w_out]
    v_big = [v_w_mem_kv, v_attn_w_in, v_attn_w_out, v_conv_w_in, v_conv_w_out]
    wb = _cast_weights(w_big)
    full = _gather_weights(wb, _pad_rows(conv_w[0], CW_ROWS))
    wkv_f, w_in_a_f, w_out_a_f, w_in_b_f, w_out_b_f, cw_f = full
    cw8 = cw_f.transpose(1, 0, 2).reshape(CW_ROWS, D)

    gx, small, big = _local_step(
        x[0], mem[0], positions[0].astype(F32).reshape(S, 1), norm_g, mem_norm_g, final_g, loss_target[0],
        wkv_f, w_in_a_f.reshape(4, D, SH_A), w_out_a_f.reshape(4, BR_A, SH_O), w_in_b_f.reshape(4, D, SH_B), cw8,
        w_out_b_f.reshape(BR_B, D))

    gs = [big["wkv"], big["w_in_a"].reshape(4, 1, D, SH_A), big["w_out_a"].reshape(4, 1, BR_A, SH_O),
          big["w_in_b"].reshape(4, 1, D, SH_B), big["w_out_b"].reshape(4, 1, BR_B // 4, D)]
    r1 = _pair_exchange(gs)
    ps = [_pair_sum(place, gs[i], r1[i], i) for i in range(NBIG)]
    r2 = _chip_exchange(ps)
    hs = [_chip_sum(place, gs[i], r1[i], r2[i], i) for i in range(NBIG)]
    g_big = _pair_gather(hs)

    pack = jnp.concatenate([_pad_rows(small["dnorm"], 8), _pad_rows(small["dmemnorm"], 8), _pad_rows(small["dfinal"], 8),
                            small["dconv"], _pad_rows(jnp.pad(small["loss"], ((0, 0), (0, D - 128))), 8)], axis=0)
    tot = _all_reduce_small(pack)
    loss = tot[32, 0]
    g_norm, g_memnorm, g_final = tot[0:2], tot[8:10], tot[16]
    g_conv = lax.dynamic_slice(tot, (24, (2 * mx + my) * SH_O), (3, SH_O))

    upd = [_adamw_big(w_big[i], g_big[i], m_big[i], v_big[i], i) for i in range(NBIG)]
    sw = [norm_g, mem_norm_g, final_g.reshape(1, D), conv_w[0]]
    sg = [g_norm, g_memnorm, g_final.reshape(1, D), g_conv]
    sm = [m_norm_g, m_mem_norm_g, m_final_g.reshape(1, D), m_conv_w[0]]
    sv = [v_norm_g, v_mem_norm_g, v_final_g.reshape(1, D), v_conv_w[0]]
    sd, snm, snv = _adamw_small(sw, sg, sm, sv)

    def order(norm, memnorm, wkv, w_in_a, w_out_a, w_in_b, conv, w_out_b, final):
        return (norm, memnorm, wkv, w_in_a, w_out_a, w_in_b, conv.reshape(1, 3, SH_O), w_out_b, final.reshape(D))

    grads = order(g_norm, g_memnorm, g_big[0], g_big[1], g_big[2], g_big[3], g_conv, g_big[4], g_final)
    deltas = order(sd[0], sd[1], upd[0][0], upd[1][0], upd[2][0], upd[3][0], sd[3], upd[4][0], sd[2])
    new_m = order(snm[0], snm[1], upd[0][1], upd[1][1], upd[2][1], upd[3][1], snm[3], upd[4][1], snm[2])
    new_v = order(snv[0], snv[1], upd[0][2], upd[1][2], upd[2][2], upd[3][2], snv[3], upd[4][2], snv[2])
    return (loss, gx[None], *grads, *deltas, *new_m, *new_v)
```

```python
import functools

import numpy as np
import jax
import jax.numpy as jnp
from jax import lax
from jax.experimental import pallas as pl
from jax.experimental.pallas import tpu as pltpu

F32 = jnp.float32
BF16 = jnp.bfloat16

S = 2048
D = 1024
TM = 256
NT = S // TM
HD = 64
GW = 512
NQ = 3 * GW
MW = 256
NM = 256
IN_A = 3 * NQ + MW + GW + MW
IN_B = 3 * D + MW + D + MW
BR_A = GW + MW
BR_B = D + MW
SH_A = IN_A // 4
SH_B = IN_B // 4
SH_O = D // 4
QBLK = 128
DILATIONS = (1, 4, 16)
EPS = 1e-6
SCALE = HD ** -0.5
NEG = -1e30
ROPE_THETA = 500000.0

ADAM_LR = 0.001
ADAM_B1 = 0.9
ADAM_B2 = 0.999
ADAM_EPS = 1e-08
ADAM_WD = 0.01
ADAM_STEP = 10

VMEM_LIMIT_BYTES = 60 * 1024 * 1024


def _params(sem=None):
    if sem is None:
        return pltpu.CompilerParams(vmem_limit_bytes=VMEM_LIMIT_BYTES)
    return pltpu.CompilerParams(dimension_semantics=sem, vmem_limit_bytes=VMEM_LIMIT_BYTES)


def _full(shape):
    nd = len(shape)
    return pl.BlockSpec(shape, lambda *_: (0,) * nd)


def _rows(width, tm=TM):
    return pl.BlockSpec((tm, width), lambda i: (i, 0))


def _sds(shape, dtype):
    return jax.ShapeDtypeStruct(shape, dtype)


def _silu_parts(z):
    sig = 1.0 / (1.0 + jnp.exp(-z))
    return z * sig, sig * (1.0 + z * (1.0 - sig))


def _dot(a, b):
    return jnp.dot(a, b, preferred_element_type=F32)


def _dot_nt(a, b):
    return lax.dot_general(a, b, (((1,), (1,)), ((), ())), preferred_element_type=F32)


def _dot_tn(a, b):
    return lax.dot_general(a, b, (((0,), (0,)), ((), ())), preferred_element_type=F32)


def _rope_fwd(t, c, s1, s2):
    return t * c + pltpu.roll(t, 120, 1) * s1 + pltpu.roll(t, 8, 1) * s2


def _rope_bwd(g, c, s1, s2):
    return g * c + pltpu.roll(g * s1, 8, 1) + pltpu.roll(g * s2, 120, 1)


def _mem_attn(qm, kv):
    res = []
    for h in range(MW // HD):
        sl = slice(h * HD, (h + 1) * HD)
        s = _dot_nt(qm[:, sl], kv[:, sl]) * SCALE
        e = jnp.exp(s - jnp.max(s, axis=-1, keepdims=True))
        p = e / jnp.sum(e, axis=-1, keepdims=True)
        res.append((p, _dot(p.astype(BF16), kv[:, MW + h * HD:MW + (h + 1) * HD])))
    return res


def _mem_attn_bwd(dmo, heads, qm, kv, dqm_store, dkv_ref):
    for h, (p, mo) in enumerate(heads):
        sl = slice(h * HD, (h + 1) * HD)
        vs = slice(MW + h * HD, MW + (h + 1) * HD)
        dmo_h = dmo[:, sl]
        dmo_b = dmo_h.astype(BF16)
        dp = _dot_nt(dmo_b, kv[:, vs])
        delta = jnp.sum(dmo_h * mo, axis=-1, keepdims=True)
        ds = (p * (dp - delta) * SCALE).astype(BF16)
        dqm_store(h, _dot(ds, kv[:, sl]))
        dkv_ref[:, sl] += _dot_tn(ds, qm[:, sl])
        dkv_ref[:, vs] += _dot_tn(p.astype(BF16), dmo_b)


def _merge(o_refs, l_refs):
    ls = [r[...] for r in l_refs]
    m = jnp.maximum(jnp.maximum(ls[0], ls[1]), ls[2])
    es = [jnp.exp(l - m) for l in ls]
    inv = 1.0 / (es[0] + es[1] + es[2])
    ws = [e * inv for e in es]
    os_ = [r[...] for r in o_refs]
    mix = ws[0] * os_[0] + ws[1] * os_[1] + ws[2] * os_[2]
    return ws, mix


def _conv_taps(cg, u, cgp, up, first):
    a = cg * u
    ap = jnp.where(first, 0.0, cgp * up)
    row = lax.broadcasted_iota(jnp.int32, a.shape, 0)
    a1 = jnp.where(row == 0, ap[7:8, :], pltpu.roll(a, 1, 0))
    a2 = jnp.where(row == 0, ap[6:7, :], jnp.where(row == 1, ap[7:8, :], pltpu.roll(a, 2, 0)))
    return a, a1, a2


def _rope_tables(posf):
    half = 8
    invf = np.float32(ROPE_THETA) ** (-np.arange(half, dtype=np.float32) * np.float32(2.0 / 16))
    lane = np.arange(128)
    table = np.where((lane % HD) < 16, invf[lane % half], 0.0).astype(np.float32)[None, :]

    def body(pos_ref, invf_ref, c_ref, s1_ref, s2_ref):
        ang = pos_ref[...] * invf_ref[...]
        jm = lax.broadcasted_iota(jnp.int32, ang.shape, 1) & (HD - 1)
        cs = jnp.cos(ang)
        sn = jnp.sin(ang)
        c_ref[...] = jnp.where(jm < 16, cs, 1.0)
        s1_ref[...] = jnp.where(jm < 8, -sn, 0.0)
        s2_ref[...] = jnp.where((jm >= 8) & (jm < 16), sn, 0.0)

    out = _sds((S, 128), F32)
    return pl.pallas_call(
        body, name="rope_tables", grid=(NT,),
        in_specs=[_rows(1), _full((1, 128))],
        out_specs=[_rows(128)] * 3, out_shape=[out] * 3,
        compiler_params=_params(("parallel",)),
    )(posf, jnp.asarray(table))


def _in_proj_a(x, g0, w_in, c, s1, s2):
    def body(x_ref, g_ref, w_ref, c_ref, s1_ref, s2_ref, hn_ref, q_ref, k_ref, v_ref, qm_ref, z_ref, proj):
        xf = x_ref[...]
        hn = xf * lax.rsqrt(jnp.mean(xf * xf, axis=-1, keepdims=True) + EPS) * g_ref[...]
        hb = hn.astype(BF16)
        hn_ref[...] = hb
        for s in range(4):
            proj[:, s * SH_A:(s + 1) * SH_A] = _dot(hb, w_ref[s])
        cc, a1, a2 = c_ref[...], s1_ref[...], s2_ref[...]
        for j in range(NQ // 128):
            q_ref[:, j * 128:(j + 1) * 128] = _rope_fwd(proj[:, j * 128:(j + 1) * 128], cc, a1, a2).astype(BF16)
            k_ref[:, j * 128:(j + 1) * 128] = _rope_fwd(
                proj[:, NQ + j * 128:NQ + (j + 1) * 128], cc, a1, a2).astype(BF16)
        v_ref[...] = proj[:, 2 * NQ:3 * NQ].astype(BF16)
        qm_ref[...] = proj[:, 3 * NQ:3 * NQ + MW].astype(BF16)
        z_ref[...] = proj[:, 3 * NQ + MW:]

    return pl.pallas_call(
        body, name="in_proj_a", grid=(NT,),
        in_specs=[_rows(D), _full((1, D)), _full((4, D, SH_A)), _rows(128), _rows(128), _rows(128)],
        out_specs=[_rows(D), _rows(NQ), _rows(NQ), _rows(NQ), _rows(MW), _rows(BR_A)],
        out_shape=[_sds((S, D), BF16), _sds((S, NQ), BF16), _sds((S, NQ), BF16), _sds((S, NQ), BF16),
                   _sds((S, MW), BF16), _sds((S, BR_A), F32)],
        scratch_shapes=[pltpu.VMEM((TM, IN_A), F32)],
        compiler_params=_params(("parallel",)),
    )(x, g0, w_in, c, s1, s2)


def _mem_fwd(mem, mg, wkv):
    def body(mem_ref, mg_ref, w_ref, memn_ref, kv_ref):
        mf = mem_ref[...]
        n = mf * lax.rsqrt(jnp.mean(mf * mf, axis=-1, keepdims=True) + EPS)
        for i in range(2):
            mn = (n * mg_ref[i:i + 1, :]).astype(BF16)
            memn_ref[i] = mn
            acc = _dot(mn[:, 0:NM], w_ref[0, i])
            for s in range(1, 4):
                acc += _dot(mn[:, s * NM:(s + 1) * NM], w_ref[s, i])
            kv_ref[i] = acc.astype(BF16)

    return pl.pallas_call(
        body, name="mem_fwd", grid=(1,),
        in_specs=[_full((NM, D)), _full((2, D)), _full((4, 2, NM, 2 * MW))],
        out_specs=[_full((2, NM, D)), _full((2, NM, 2 * MW))],
        out_shape=[_sds((2, NM, D), BF16), _sds((2, NM, 2 * MW), BF16)],
        compiler_params=_params(("arbitrary",)),
    )(mem, mg, wkv)


def _band_mask(j):
    qi = lax.broadcasted_iota(jnp.int32, (QBLK, 2 * QBLK), 0)
    kj = lax.broadcasted_iota(jnp.int32, (QBLK, 2 * QBLK), 1)
    dist = qi + QBLK - kj
    return (dist >= 0) & (dist <= QBLK) & ((kj >= QBLK) | (j > 0))


def _attn_fwd(q, k, v, g):
    d = DILATIONS[g]
    ln = S // d
    nb = ln // QBLK

    def body(q_ref, k_ref, v_ref, o_ref, l_ref):
        def blk(j, carry):
            r0 = pl.multiple_of(j * QBLK, QBLK)
            p0 = pl.multiple_of(jnp.maximum(j - 1, 0) * QBLK, QBLK)
            qb = q_ref[pl.ds(r0, QBLK), :]
            kk = jnp.concatenate([k_ref[pl.ds(p0, QBLK), :], k_ref[pl.ds(r0, QBLK), :]], axis=0)
            vv = jnp.concatenate([v_ref[pl.ds(p0, QBLK), :], v_ref[pl.ds(r0, QBLK), :]], axis=0)
            valid = _band_mask(j)
            for h in range(GW // HD):
                sl = slice(h * HD, (h + 1) * HD)
                s = jnp.where(valid, _dot_nt(qb[:, sl], kk[:, sl]) * SCALE, NEG)
                m = jnp.max(s, axis=-1, keepdims=True)
                e = jnp.exp(s - m)
                l = jnp.sum(e, axis=-1, keepdims=True)
                o_ref[pl.ds(r0, QBLK), sl] = _dot(e.astype(BF16), vv[:, sl]) / l
                l_ref[pl.ds(r0, QBLK), sl] = jnp.broadcast_to(m + jnp.log(l), (QBLK, HD))
            return carry

        lax.fori_loop(0, nb, blk, 0)

    qkv_spec = pl.BlockSpec((ln, GW), lambda r: (0, r * 3 + g))
    out_spec = pl.BlockSpec((ln, GW), lambda r: (0, r))
    o, l = pl.pallas_call(
        body, name=f"attn_fwd_g{g}", grid=(d,),
        in_specs=[qkv_spec] * 3, out_specs=[out_spec] * 2,
        out_shape=[_sds((ln, d * GW), F32)] * 2,
        compiler_params=_params(("parallel",)),
    )(q.reshape(ln, d * NQ), k.reshape(ln, d * NQ), v.reshape(ln, d * NQ))
    return o.reshape(S, GW), l.reshape(S, GW)


def _attn_out(os_, ls, qm, kv0, z, x, w_out):
    def body(o0, o1, o2, l0, l1, l2, qm_ref, kv_ref, z_ref, x_ref, w_ref, h_ref, ybuf):
        _, mix = _merge((o0, o1, o2), (l0, l1, l2))
        sz, _ = _silu_parts(z_ref[...])
        ybuf[:, :GW] = (mix * sz[:, :GW]).astype(BF16)
        for h, (_, mo) in enumerate(_mem_attn(qm_ref[...], kv_ref[...])):
            sl = slice(GW + h * HD, GW + (h + 1) * HD)
            ybuf[:, sl] = (mo * sz[:, sl]).astype(BF16)
        yb = ybuf[...]
        for s in range(4):
            cs = slice(s * SH_O, (s + 1) * SH_O)
            h_ref[:, cs] = x_ref[:, cs] + _dot(yb, w_ref[s])

    return pl.pallas_call(
        body, name="attn_out", grid=(NT,),
        in_specs=[_rows(GW)] * 6 + [_rows(MW), _full((NM, 2 * MW)), _rows(BR_A), _rows(D), _full((4, BR_A, SH_O))],
        out_specs=_rows(D), out_shape=_sds((S, D), F32),
        scratch_shapes=[pltpu.VMEM((TM, BR_A), BF16)],
        compiler_params=_params(("parallel",)),
    )(*os_, *ls, qm, kv0, z, x, w_out)


def _in_proj_b(h1, g1, w_in):
    def body(x_ref, g_ref, w_ref, hn_ref, bg_ref, cg_ref, u_ref, qm_ref, z_ref, proj):
        xf = x_ref[...]
        hn = xf * lax.rsqrt(jnp.mean(xf * xf, axis=-1, keepdims=True) + EPS) * g_ref[...]
        hb = hn.astype(BF16)
        hn_ref[...] = hb
        for s in range(4):
            proj[:, s * SH_B:(s + 1) * SH_B] = _dot(hb, w_ref[s])
        bg_ref[...] = proj[:, :D]
        cg_ref[...] = proj[:, D:2 * D]
        u_ref[...] = proj[:, 2 * D:3 * D]
        qm_ref[...] = proj[:, 3 * D:3 * D + MW].astype(BF16)
        z_ref[...] = proj[:, 3 * D + MW:]

    return pl.pallas_call(
        body, name="in_proj_b", grid=(NT,),
        in_specs=[_rows(D), _full((1, D)), _full((4, D, SH_B))],
        out_specs=[_rows(D), _rows(D), _rows(D), _rows(D), _rows(MW), _rows(BR_B)],
        out_shape=[_sds((S, D), BF16), _sds((S, D), F32), _sds((S, D), F32), _sds((S, D), F32),
                   _sds((S, MW), BF16), _sds((S, BR_B), F32)],
        scratch_shapes=[pltpu.VMEM((TM, IN_B), F32)],
        compiler_params=_params(("parallel",)),
    )(h1, g1, w_in)


def _prev8(width):
    return pl.BlockSpec((8, width), lambda i: (jnp.maximum(i * (TM // 8) - 1, 0), 0))


def _conv_out_loss(bg, cg, u, cw, qm, kv1, z, h1, w_out, fg, tgt):
    def body(bg_ref, cg_ref, u_ref, cgp_ref, up_ref, cw_ref, qm_ref, kv_ref, z_ref, h_ref, w_ref, fg_ref, t_ref,
             dh_ref, loss_ref, dfg_ref, ybuf):
        i = pl.program_id(0)
        a, a1, a2 = _conv_taps(cg_ref[...], u_ref[...], cgp_ref[...], up_ref[...], i == 0)
        conv = cw_ref[0:1, :] * a2 + cw_ref[1:2, :] * a1 + cw_ref[2:3, :] * a
        sz, _ = _silu_parts(z_ref[...])
        ybuf[:, :D] = (bg_ref[...] * conv * sz[:, :D]).astype(BF16)
        for h, (_, mo) in enumerate(_mem_attn(qm_ref[...], kv_ref[...])):
            sl = slice(D + h * HD, D + (h + 1) * HD)
            ybuf[:, sl] = (mo * sz[:, sl]).astype(BF16)
        h2 = h_ref[...] + _dot(ybuf[...], w_ref[...])
        rstd = lax.rsqrt(jnp.mean(h2 * h2, axis=-1, keepdims=True) + EPS)
        n = h2 * rstd
        fgv = fg_ref[...]
        err = n * fgv - t_ref[...]
        dout = err * (1.0 / D)
        dn = dout * fgv
        dh_ref[...] = rstd * (dn - n * jnp.mean(dn * n, axis=-1, keepdims=True))

        @pl.when(i == 0)
        def _():
            loss_ref[...] = jnp.zeros_like(loss_ref)
            dfg_ref[...] = jnp.zeros_like(dfg_ref)

        loss_ref[...] += jnp.sum(err * err) * (0.5 / D)
        dfg_ref[...] += jnp.sum(dout * n, axis=0, keepdims=True)

    return pl.pallas_call(
        body, name="conv_out_loss", grid=(NT,),
        in_specs=[_rows(D), _rows(D), _rows(D), _prev8(D), _prev8(D), _full((8, D)), _rows(MW),
                  _full((NM, 2 * MW)), _rows(BR_B), _rows(D), _full((BR_B, D)), _full((1, D)), _rows(D)],
        out_specs=[_rows(D), _full((1, 128)), _full((1, D))],
        out_shape=[_sds((S, D), F32), _sds((1, 128), F32), _sds((1, D), F32)],
        scratch_shapes=[pltpu.VMEM((TM, BR_B), BF16)],
        compiler_params=_params(("arbitrary",)),
    )(bg, cg, u, cg, u, cw, qm, kv1, z, h1, w_out, fg, tgt)


def _conv_bwd(dh2, bg, cg, u, cw, qm, kv1, z, w_out):
    rev = lambda i: (NT - 1 - i, 0)
    rows = lambda w: pl.BlockSpec((TM, w), rev)
    prev8 = pl.BlockSpec((8, D), lambda i: (jnp.maximum((NT - 1 - i) * (TM // 8) - 1, 0), 0))

    def body(dh_ref, bg_ref, cg_ref, u_ref, cgp_ref, up_ref, cw_ref, qm_ref, kv_ref, z_ref, w_ref,
             dproj_ref, dw_ref, dcw_ref, dkv_ref, ybuf, carry):
        i = pl.program_id(0)

        @pl.when(i == 0)
        def _():
            dw_ref[...] = jnp.zeros_like(dw_ref)
            dcw_ref[...] = jnp.zeros_like(dcw_ref)
            dkv_ref[...] = jnp.zeros_like(dkv_ref)
            carry[...] = jnp.zeros_like(carry)

        bgv, cgv, uv = bg_ref[...], cg_ref[...], u_ref[...]
        a, a1, a2 = _conv_taps(cgv, uv, cgp_ref[...], up_ref[...], i == NT - 1)
        w0, w1, w2 = cw_ref[0:1, :], cw_ref[1:2, :], cw_ref[2:3, :]
        conv = w0 * a2 + w1 * a1 + w2 * a
        mix = bgv * conv
        zv = z_ref[...]
        sz, dsz = _silu_parts(zv)
        qmv, kvv = qm_ref[...], kv_ref[...]
        heads = _mem_attn(qmv, kvv)
        ybuf[:, :D] = (mix * sz[:, :D]).astype(BF16)
        for h, (_, mo) in enumerate(heads):
            sl = slice(D + h * HD, D + (h + 1) * HD)
            ybuf[:, sl] = (mo * sz[:, sl]).astype(BF16)
        dhb = dh_ref[...].astype(BF16)
        dw_ref[...] += _dot_tn(ybuf[...], dhb)
        dy = _dot_nt(dhb, w_ref[...])
        dcat = dy * sz
        dproj_ref[:, 3 * D + MW:3 * D + MW + D] = (dy[:, :D] * mix * dsz[:, :D]).astype(BF16)
        for h, (_, mo) in enumerate(heads):
            sl = slice(D + h * HD, D + (h + 1) * HD)
            dproj_ref[:, 3 * D + MW + D + h * HD:3 * D + MW + D + (h + 1) * HD] = (
                dy[:, sl] * mo * dsz[:, sl]).astype(BF16)
        dmix = dcat[:, :D]
        dproj_ref[:, :D] = (dmix * conv).astype(BF16)
        dc = dmix * bgv
        nxt = carry[...]
        row = lax.broadcasted_iota(jnp.int32, dc.shape, 0)
        dc1 = jnp.where(row == TM - 1, nxt[0:1, :], pltpu.roll(dc, TM - 1, 0))
        dc2 = jnp.where(row == TM - 2, nxt[0:1, :], jnp.where(row == TM - 1, nxt[1:2, :], pltpu.roll(dc, TM - 2, 0)))
        carry[...] = dc[0:8, :]
        da = w2 * dc + w1 * dc1 + w0 * dc2
        dproj_ref[:, D:2 * D] = (da * uv).astype(BF16)
        dproj_ref[:, 2 * D:3 * D] = (da * cgv).astype(BF16)
        dcw_ref[0:1, :] += jnp.sum(dc * a2, axis=0, keepdims=True)
        dcw_ref[1:2, :] += jnp.sum(dc * a1, axis=0, keepdims=True)
        dcw_ref[2:3, :] += jnp.sum(dc * a, axis=0, keepdims=True)

        def dqm_store(h, val):
            dproj_ref[:, 3 * D + h * HD:3 * D + (h + 1) * HD] = val.astype(BF16)

        _mem_attn_bwd(dcat[:, D:], heads, qmv, kvv, dqm_store, dkv_ref)

    return pl.pallas_call(
        body, name="conv_bwd", grid=(NT,),
        in_specs=[rows(D), rows(D), rows(D), rows(D), prev8, prev8, _full((8, D)), rows(MW),
                  _full((NM, 2 * MW)), rows(BR_B), _full((BR_B, D))],
        out_specs=[rows(IN_B), _full((BR_B, D)), _full((8, D)), _full((NM, 2 * MW))],
        out_shape=[_sds((S, IN_B), BF16), _sds((BR_B, D), F32), _sds((8, D), F32), _sds((NM, 2 * MW), F32)],
        scratch_shapes=[pltpu.VMEM((TM, BR_B), BF16), pltpu.VMEM((8, D), F32)],
        compiler_params=_params(("arbitrary",)),
    )(dh2, bg, cg, u, cg, u, cw, qm, kv1, z, w_out)


def _in_proj_bwd(dproj, w_in, xin, g, dres, width, name):
    sh = width // 4

    def body(dp_ref, w_ref, x_ref, g_ref, dr_ref, dx_ref, dg_ref):
        i = pl.program_id(0)
        dhn = _dot_nt(dp_ref[:, 0:sh], w_ref[0])
        for s in range(1, 4):
            dhn += _dot_nt(dp_ref[:, s * sh:(s + 1) * sh], w_ref[s])
        xf = x_ref[...]
        rstd = lax.rsqrt(jnp.mean(xf * xf, axis=-1, keepdims=True) + EPS)
        n = xf * rstd
        dn = dhn * g_ref[...]
        dx_ref[...] = dr_ref[...] + rstd * (dn - n * jnp.mean(dn * n, axis=-1, keepdims=True))

        @pl.when(i == 0)
        def _():
            dg_ref[...] = jnp.zeros_like(dg_ref)

        dg_ref[...] += jnp.sum(dhn * n, axis=0, keepdims=True)

    return pl.pallas_call(
        body, name=name, grid=(NT,),
        in_specs=[_rows(width), _full((4, D, sh)), _rows(D), _full((1, D)), _rows(D)],
        out_specs=[_rows(D), _full((1, D))],
        out_shape=[_sds((S, D), F32), _sds((1, D), F32)],
        compiler_params=_params(("arbitrary",)),
    )(dproj, w_in, xin, g, dres)


def _w_in_grad(hn, dproj, width, name):
    sh = width // 4

    def body(hn_ref, dp_ref, dw_ref):
        dw_ref[0] = _dot_tn(hn_ref[...], dp_ref[...])

    return pl.pallas_call(
        body, name=name, grid=(4,),
        in_specs=[_full((S, D)), pl.BlockSpec((S, sh), lambda s: (0, s))],
        out_specs=pl.BlockSpec((1, D, sh), lambda s: (s, 0, 0)),
        out_shape=_sds((4, D, sh), F32),
        compiler_params=_params(("parallel",)),
    )(hn, dproj)


def _attn_out_bwd(dh1, os_, ls, qm, kv0, z, w_out):
    ones_bd = np.kron(np.eye(GW // HD, dtype=np.float32), np.ones((HD, HD), np.float32))

    def body(dh_ref, o0, o1, o2, l0, l1, l2, qm_ref, kv_ref, z_ref, w_ref, bd_ref,
             do0, do1, do2, dd0, dd1, dd2, dqm_ref, dz_ref, dw_ref, dkv_ref, ybuf):
        i = pl.program_id(0)

        @pl.when(i == 0)
        def _():
            dw_ref[...] = jnp.zeros_like(dw_ref)
            dkv_ref[...] = jnp.zeros_like(dkv_ref)

        ws, mix = _merge((o0, o1, o2), (l0, l1, l2))
        sz, dsz = _silu_parts(z_ref[...])
        qmv, kvv = qm_ref[...], kv_ref[...]
        heads = _mem_attn(qmv, kvv)
        ybuf[:, :GW] = (mix * sz[:, :GW]).astype(BF16)
        for h, (_, mo) in enumerate(heads):
            sl = slice(GW + h * HD, GW + (h + 1) * HD)
            ybuf[:, sl] = (mo * sz[:, sl]).astype(BF16)
        yb = ybuf[...]
        dh = dh_ref[...]
        dy = None
        for s in range(4):
            dhb = dh[:, s * SH_O:(s + 1) * SH_O].astype(BF16)
            dw_ref[s] += _dot_tn(yb, dhb)
            part = _dot_nt(dhb, w_ref[s])
            dy = part if dy is None else dy + part
        dcat = dy * sz
        dz_ref[:, :GW] = (dy[:, :GW] * mix * dsz[:, :GW]).astype(BF16)
        for h, (_, mo) in enumerate(heads):
            sl = slice(GW + h * HD, GW + (h + 1) * HD)
            dz_ref[:, sl] = (dy[:, sl] * mo * dsz[:, sl]).astype(BF16)
        dmix = dcat[:, :GW]
        prod = dmix * mix
        hi = prod.astype(BF16)
        lo = (prod - hi.astype(F32)).astype(BF16)
        bd = bd_ref[...]
        tot = _dot(hi, bd) + _dot(lo, bd)
        for w, do_ref, dd_ref in zip(ws, (do0, do1, do2), (dd0, dd1, dd2)):
            do_ref[...] = (w * dmix).astype(BF16)
            dd_ref[...] = w * tot

        def dqm_store(h, val):
            dqm_ref[:, h * HD:(h + 1) * HD] = val.astype(BF16)

        _mem_attn_bwd(dcat[:, GW:], heads, qmv, kvv, dqm_store, dkv_ref)

    return pl.pallas_call(
        body, name="attn_out_bwd", grid=(NT,),
        in_specs=[_rows(D)] + [_rows(GW)] * 6 + [_rows(MW), _full((NM, 2 * MW)), _rows(BR_A),
                                                   _full((4, BR_A, SH_O)), _full((GW, GW))],
        out_specs=[_rows(GW)] * 6 + [_rows(MW), _rows(BR_A), _full((4, BR_A, SH_O)), _full((NM, 2 * MW))],
        out_shape=[_sds((S, GW), BF16)] * 3 + [_sds((S, GW), F32)] * 3 + [
            _sds((S, MW), BF16), _sds((S, BR_A), BF16), _sds((4, BR_A, SH_O), F32), _sds((NM, 2 * MW), F32)],
        scratch_shapes=[pltpu.VMEM((TM, BR_A), BF16)],
        compiler_params=_params(("arbitrary",)),
    )(dh1, *os_, *ls, qm, kv0, z, w_out, jnp.asarray(ones_bd, dtype=BF16))


def _attn_bwd(q, k, v, do, lse, dd, g):
    d = DILATIONS[g]
    ln = S // d
    nb = ln // QBLK

    def body(q_ref, k_ref, v_ref, do_ref, l_ref, dd_ref, dq_ref, dk_ref, dv_ref):
        dk_ref[...] = jnp.zeros_like(dk_ref)
        dv_ref[...] = jnp.zeros_like(dv_ref)

        def blk(j, carry):
            r0 = pl.multiple_of(j * QBLK, QBLK)
            p0 = pl.multiple_of(jnp.maximum(j - 1, 0) * QBLK, QBLK)
            qb = q_ref[pl.ds(r0, QBLK), :]
            dob = do_ref[pl.ds(r0, QBLK), :]
            lb = l_ref[pl.ds(r0, QBLK), :]
            ddb = dd_ref[pl.ds(r0, QBLK), :]
            kk = jnp.concatenate([k_ref[pl.ds(p0, QBLK), :], k_ref[pl.ds(r0, QBLK), :]], axis=0)
            vv = jnp.concatenate([v_ref[pl.ds(p0, QBLK), :], v_ref[pl.ds(r0, QBLK), :]], axis=0)
            valid = _band_mask(j)
            for h in range(GW // HD):
                sl = slice(h * HD, (h + 1) * HD)
                s = _dot_nt(qb[:, sl], kk[:, sl]) * SCALE
                p = jnp.where(valid, jnp.exp(s - lb[:, h * HD:h * HD + 1]), 0.0)
                dp = _dot_nt(dob[:, sl], vv[:, sl])
                ds = (p * (dp - ddb[:, h * HD:h * HD + 1]) * SCALE).astype(BF16)
                pb = p.astype(BF16)
                dq_ref[pl.ds(r0, QBLK), sl] = _dot(ds, kk[:, sl])
                dkk = _dot_tn(ds, qb[:, sl])
                dvv = _dot_tn(pb, dob[:, sl])
                dk_ref[pl.ds(p0, QBLK), sl] += dkk[:QBLK]
                dk_ref[pl.ds(r0, QBLK), sl] += dkk[QBLK:]
                dv_ref[pl.ds(p0, QBLK), sl] += dvv[:QBLK]
                dv_ref[pl.ds(r0, QBLK), sl] += dvv[QBLK:]
            return carry

        lax.fori_loop(0, nb, blk, 0)

    qkv_spec = pl.BlockSpec((ln, GW), lambda r: (0, r * 3 + g))
    one_spec = pl.BlockSpec((ln, GW), lambda r: (0, r))
    outs = pl.pallas_call(
        body, name=f"attn_bwd_g{g}", grid=(d,),
        in_specs=[qkv_spec] * 3 + [one_spec] * 3, out_specs=[one_spec] * 3,
        out_shape=[_sds((ln, d * GW), F32)] * 3,
        compiler_params=_params(("parallel",)),
    )(q.reshape(ln, d * NQ), k.reshape(ln, d * NQ), v.reshape(ln, d * NQ),
      do.reshape(ln, d * GW), lse.reshape(ln, d * GW), dd.reshape(ln, d * GW))
    return [t.reshape(S, GW) for t in outs]


def _qkv_bwd(dqs, dks, dvs, dqm, dz, c, s1, s2):
    def body(q0, q1, q2, k0, k1, k2, v0, v1, v2, dqm_ref, dz_ref, c_ref, s1_ref, s2_ref, dp_ref):
        cc, a1, a2 = c_ref[...], s1_ref[...], s2_ref[...]
        for g, (qr, kr, vr) in enumerate(((q0, k0, v0), (q1, k1, v1), (q2, k2, v2))):
            for j in range(GW // 128):
                ls_ = slice(j * 128, (j + 1) * 128)
                c0 = g * GW + j * 128
                dp_ref[:, c0:c0 + 128] = _rope_bwd(qr[:, ls_], cc, a1, a2).astype(BF16)
                dp_ref[:, NQ + c0:NQ + c0 + 128] = _rope_bwd(kr[:, ls_], cc, a1, a2).astype(BF16)
            dp_ref[:, 2 * NQ + g * GW:2 * NQ + (g + 1) * GW] = vr[...].astype(BF16)
        dp_ref[:, 3 * NQ:3 * NQ + MW] = dqm_ref[...]
        dp_ref[:, 3 * NQ + MW:] = dz_ref[...]

    return pl.pallas_call(
        body, name="qkv_bwd", grid=(NT,),
        in_specs=[_rows(GW)] * 9 + [_rows(MW), _rows(BR_A), _rows(128), _rows(128), _rows(128)],
        out_specs=_rows(IN_A), out_shape=_sds((S, IN_A), BF16),
        compiler_params=_params(("parallel",)),
    )(*dqs, *dks, *dvs, dqm, dz, c, s1, s2)


def _mem_bwd(mem, mg, memn, wkv, dkv0, dkv1):
    def body(mem_ref, mg_ref, memn_ref, w_ref, d0_ref, d1_ref, dw_ref, dg_ref):
        mf = mem_ref[...]
        n = mf * lax.rsqrt(jnp.mean(mf * mf, axis=-1, keepdims=True) + EPS)
        for i, d_ref in enumerate((d0_ref, d1_ref)):
            dkv = d_ref[...].astype(BF16)
            mn = memn_ref[i]
            for s in range(4):
                cs = slice(s * NM, (s + 1) * NM)
                dw_ref[s, i] = _dot_tn(mn[:, cs], dkv)
                dmn = _dot_nt(dkv, w_ref[s, i])
                dg_ref[i:i + 1, cs] = jnp.sum(dmn * n[:, cs], axis=0, keepdims=True)

    return pl.pallas_call(
        body, name="mem_bwd", grid=(1,),
        in_specs=[_full((NM, D)), _full((2, D)), _full((2, NM, D)), _full((4, 2, NM, 2 * MW)),
                  _full((NM, 2 * MW)), _full((NM, 2 * MW))],
        out_specs=[_full((4, 2, NM, 2 * MW)), _full((2, D))],
        out_shape=[_sds((4, 2, NM, 2 * MW), F32), _sds((2, D), F32)],
        compiler_params=_params(("arbitrary",)),
    )(mem, mg, memn, wkv, dkv0, dkv1)


def _local_step(x, mem, posf, norm_g, mem_norm_g, final_g, tgt, wkv, w_in_a, w_out_a, w_in_b, cw8, w_out_b):
    g0, g1 = norm_g[0:1], norm_g[1:2]
    c, s1, s2 = _rope_tables(posf)
    memn, kv = _mem_fwd(mem, mem_norm_g, wkv)
    hn0, q, k, v, qm0, z0 = _in_proj_a(x, g0, w_in_a, c, s1, s2)
    fwd = [_attn_fwd(q, k, v, g) for g in range(3)]
    os_, ls = [f[0] for f in fwd], [f[1] for f in fwd]
    h1 = _attn_out(os_, ls, qm0, kv[0], z0, x, w_out_a)
    hn1, bg, cg, u, qm1, z1 = _in_proj_b(h1, g1, w_in_b)
    dh2, loss, dfg = _conv_out_loss(bg, cg, u, cw8, qm1, kv[1], z1, h1, w_out_b, final_g.reshape(1, D), tgt)

    dproj_b, dw_out_b, dcw, dkv1 = _conv_bwd(dh2, bg, cg, u, cw8, qm1, kv[1], z1, w_out_b)
    dh1, dg1 = _in_proj_bwd(dproj_b, w_in_b, h1, g1, dh2, IN_B, "in_proj_b_bwd")
    dw_in_b = _w_in_grad(hn1, dproj_b, IN_B, "w_in_b_grad")
    outs = _attn_out_bwd(dh1, os_, ls, qm0, kv[0], z0, w_out_a)
    dos, dds, dqm, dz, dw_out_a, dkv0 = outs[0:3], outs[3:6], outs[6], outs[7], outs[8], outs[9]
    bwd = [_attn_bwd(q, k, v, dos[g], ls[g], dds[g], g) for g in range(3)]
    dproj_a = _qkv_bwd([b[0] for b in bwd], [b[1] for b in bwd], [b[2] for b in bwd], dqm, dz, c, s1, s2)
    gx, dg0 = _in_proj_bwd(dproj_a, w_in_a, x, g0, dh1, IN_A, "in_proj_a_bwd")
    dw_in_a = _w_in_grad(hn0, dproj_a, IN_A, "w_in_a_grad")
    dwkv, dmg = _mem_bwd(mem, mem_norm_g, memn, wkv, dkv0, dkv1)
    small = dict(loss=loss, dnorm=jnp.concatenate([dg0, dg1], axis=0), dmemnorm=dmg, dfinal=dfg, dconv=dcw)
    big = dict(wkv=dwkv, w_in_a=dw_in_a, w_out_a=dw_out_a, w_in_b=dw_in_b, w_out_b=dw_out_b)
    return gx, small, big


MESH = pl.DeviceIdType.MESH
ANY = pl.BlockSpec(memory_space=pl.ANY)
BIG = (("wkv", 2, NM, 2 * MW), ("w_in_a", 1, D, SH_A), ("w_out_a", 1, BR_A, SH_O),
       ("w_in_b", 1, D, SH_B), ("w_out_b", 1, BR_B // 4, D))
NBIG = len(BIG)
CW_ROWS = 8


def _place():
    x, y, c = lax.axis_index("x"), lax.axis_index("y"), lax.axis_index("c")
    chips = ((1 - x, y), (x, 1 - y), (1 - x, 1 - y))
    return x, y, c, chips


def _remote(src, dst, ssem, rsem, dev):
    return pltpu.make_async_remote_copy(src_ref=src, dst_ref=dst, send_sem=ssem, recv_sem=rsem,
                                        device_id=dev, device_id_type=MESH)


def _cast_weights(place, ws):
    nblk = 4

    def body(pref, *refs):
        for i in range(NBIG):
            refs[NBIG + i][0] = refs[i][...].astype(BF16)

    grid_spec = pltpu.PrefetchScalarGridSpec(
        num_scalar_prefetch=1, grid=(nblk,),
        in_specs=[pl.BlockSpec((k, r // nblk, cdim), lambda i, pref: (0, i, 0)) for _, k, r, cdim in BIG],
        out_specs=[pl.BlockSpec((1, k, r // nblk, cdim), lambda i, pref: (pref[1], 0, i, 0)) for _, k, r, cdim in BIG])
    return pl.pallas_call(
        body, name="cast_weights", grid_spec=grid_spec,
        out_shape=[_sds((4, k, r, cdim), BF16) for _, k, r, cdim in BIG],
        compiler_params=_params(("parallel",)),
    )(place, *ws)


def _gather_weights(wb, cw):
    def body(*refs):
        src_cw = refs[NBIG]
        dst = refs[NBIG + 1:2 * NBIG + 2]
        loc_sem, send_sems, recv_sems, fsend_sems, frecv_sems = refs[2 * NBIG + 2:]
        x, y, c, chips = _place()
        me = 2 * x + y
        loc = pltpu.make_async_copy(src_cw, dst[NBIG].at[me], loc_sem)
        loc.start()

        def half(ref, i, which):
            if i == NBIG:
                return ref
            h = BIG[i][2] // 2
            return ref.at[:, pl.ds(which * h, h), :]

        sends = []
        for j, (px, py) in enumerate(chips):
            for i in range(NBIG + 1):
                mine = src_cw if i == NBIG else half(dst[i].at[me], i, c)
                sends.append(_remote(mine, half(dst[i].at[me], i, c),
                                     send_sems.at[j, i], recv_sems.at[j, i], (px, py, c)))
        for cp in sends:
            cp.start()
        fwds = []
        for j, (px, py) in enumerate(chips):
            for i in range(NBIG + 1):
                got = half(dst[i].at[2 * px + py], i, c)
                _remote(got, got, send_sems.at[j, i], recv_sems.at[j, i], (px, py, c)).wait_recv()
                if i < NBIG:
                    fwds.append(_remote(got, got, fsend_sems.at[j, i], frecv_sems.at[j, i], (x, y, 1 - c)))
                    fwds[-1].start()
        for j, (px, py) in enumerate(chips):
            for i in range(NBIG):
                got = half(dst[i].at[2 * px + py], i, 1 - c)
                _remote(got, got, fsend_sems.at[j, i], frecv_sems.at[j, i], (x, y, 1 - c)).wait_recv()
        for cp in sends + fwds:
            cp.wait_send()
        loc.wait()

    out_shape = [_sds((4, k, r, cdim), BF16) for _, k, r, cdim in BIG] + [_sds((4, CW_ROWS, SH_O), F32)]
    return pl.pallas_call(
        body, name="gather_weights", in_specs=[ANY] * (NBIG + 1), out_specs=[ANY] * (NBIG + 1), out_shape=out_shape,
        input_output_aliases={i: i for i in range(NBIG)},
        scratch_shapes=[pltpu.SemaphoreType.DMA, pltpu.SemaphoreType.DMA((3, NBIG + 1)),
                        pltpu.SemaphoreType.DMA((3, NBIG + 1)), pltpu.SemaphoreType.DMA((3, NBIG)),
                        pltpu.SemaphoreType.DMA((3, NBIG))],
    )(*wb, cw)


def _pair_exchange(gs):
    def body(*refs):
        src, dst = refs[:NBIG], refs[NBIG:2 * NBIG]
        send_sems, recv_sems = refs[2 * NBIG:]
        x, y, c, _ = _place()
        cps = []
        for i in range(NBIG):
            h = BIG[i][2] // 2
            cps.append(_remote(src[i].at[:, :, pl.ds((1 - c) * h, h), :], dst[i], send_sems.at[i], recv_sems.at[i],
                               (x, y, 1 - c)))
            cps[-1].start()
        for cp in cps:
            cp.wait()

    return pl.pallas_call(
        body, name="pair_exchange", in_specs=[ANY] * NBIG, out_specs=[ANY] * NBIG,
        out_shape=[_sds((4, k, r // 2, cdim), F32) for _, k, r, cdim in BIG],
        scratch_shapes=[pltpu.SemaphoreType.DMA((NBIG,)), pltpu.SemaphoreType.DMA((NBIG,))],
    )(*gs)


def _pair_sum(place, g, r1, i):
    _, k, r, cdim = BIG[i]
    h = r // 2

    def body(pref, g_ref, r_ref, o_ref):
        o_ref[...] = (g_ref[...] + r_ref[...]).astype(BF16)

    grid_spec = pltpu.PrefetchScalarGridSpec(
        num_scalar_prefetch=1, grid=(4, k),
        in_specs=[pl.BlockSpec((1, 1, h, cdim), lambda s, t, pref: (s, t, pref[0], 0)),
                  pl.BlockSpec((1, 1, h, cdim), lambda s, t, pref: (s, t, 0, 0))],
        out_specs=pl.BlockSpec((1, 1, h, cdim), lambda s, t, pref: (s, t, 0, 0)))
    return pl.pallas_call(
        body, name=f"pair_sum_{BIG[i][0]}", grid_spec=grid_spec, out_shape=_sds((4, k, h, cdim), BF16),
        compiler_params=_params(("parallel", "parallel")),
    )(place, g, r1)


def _chip_exchange(ps):
    def body(*refs):
        src, dst = refs[:NBIG], refs[NBIG:2 * NBIG]
        send_sems, recv_sems = refs[2 * NBIG:]
        x, y, c, chips = _place()
        cps = []
        for j, (px, py) in enumerate(chips):
            for i in range(NBIG):
                cps.append(_remote(src[i].at[2 * px + py], dst[i].at[j], send_sems.at[j, i], recv_sems.at[j, i],
                                   (px, py, c)))
                cps[-1].start()
        for cp in cps:
            cp.wait()

    return pl.pallas_call(
        body, name="chip_exchange", in_specs=[ANY] * NBIG, out_specs=[ANY] * NBIG,
        out_shape=[_sds((3, k, r // 2, cdim), BF16) for _, k, r, cdim in BIG],
        scratch_shapes=[pltpu.SemaphoreType.DMA((3, NBIG)), pltpu.SemaphoreType.DMA((3, NBIG))],
    )(*ps)


def _chip_sum(place, g, r1, r2, i):
    _, k, r, cdim = BIG[i]
    h = r // 2

    def body(pref, g_ref, r1_ref, r2_ref, o_ref):
        acc = g_ref[0, 0] + r1_ref[0, 0]
        for j in range(3):
            acc = acc + r2_ref[j, 0].astype(F32)
        o_ref[0] = acc

    grid_spec = pltpu.PrefetchScalarGridSpec(
        num_scalar_prefetch=1, grid=(k,),
        in_specs=[pl.BlockSpec((1, 1, h, cdim), lambda t, pref: (pref[1], t, pref[0], 0)),
                  pl.BlockSpec((1, 1, h, cdim), lambda t, pref: (pref[1], t, 0, 0)),
                  pl.BlockSpec((3, 1, h, cdim), lambda t, pref: (0, t, 0, 0))],
        out_specs=pl.BlockSpec((1, h, cdim), lambda t, pref: (t, pref[0], 0)))
    return pl.pallas_call(
        body, name=f"chip_sum_{BIG[i][0]}", grid_spec=grid_spec, out_shape=_sds((k, r, cdim), F32),
        compiler_params=_params(("parallel",)),
    )(place, g, r1, r2)


def _pair_gather(hs):
    def body(*refs):
        dst = refs[NBIG:2 * NBIG]
        send_sems, recv_sems = refs[2 * NBIG:]
        x, y, c, _ = _place()
        cps = []
        for i in range(NBIG):
            h = BIG[i][2] // 2
            mine = dst[i].at[:, pl.ds(c * h, h), :]
            cps.append(_remote(mine, mine, send_sems.at[i], recv_sems.at[i], (x, y, 1 - c)))
            cps[-1].start()
        for i in range(NBIG):
            h = BIG[i][2] // 2
            theirs = dst[i].at[:, pl.ds((1 - c) * h, h), :]
            _remote(theirs, theirs, send_sems.at[i], recv_sems.at[i], (x, y, 1 - c)).wait_recv()
        for cp in cps:
            cp.wait_send()

    return pl.pallas_call(
        body, name="pair_gather", in_specs=[ANY] * NBIG, out_specs=[ANY] * NBIG,
        out_shape=[_sds((k, r, cdim), F32) for _, k, r, cdim in BIG],
        input_output_aliases={i: i for i in range(NBIG)},
        scratch_shapes=[pltpu.SemaphoreType.DMA((NBIG,)), pltpu.SemaphoreType.DMA((NBIG,))],
    )(*hs)


SMALL_ROWS = 40


def _all_reduce_small(pack):
    def body(p_ref, o_ref, slots, send_sems, recv_sems):
        x, y, c, _ = _place()
        me = 4 * x + 2 * y + c
        cps = []
        for r in range(1, 8):
            peer = (x if not r & 4 else 1 - x, y if not r & 2 else 1 - y, c if not r & 1 else 1 - c)
            cps.append(_remote(p_ref, slots.at[r], send_sems.at[r - 1], recv_sems.at[r - 1], peer))
            cps[-1].start()
        slots[0] = p_ref[...]
        for cp in cps:
            cp.wait()
        acc = slots[me]
        for dev in range(1, 8):
            acc = acc + slots[jnp.bitwise_xor(me, dev)]
        o_ref[...] = acc

    vm = pl.BlockSpec(memory_space=pltpu.VMEM)
    return pl.pallas_call(
        body, name="all_reduce_small", in_specs=[vm], out_specs=vm, out_shape=_sds((SMALL_ROWS, D), F32),
        scratch_shapes=[pltpu.VMEM((8, SMALL_ROWS, D), F32), pltpu.SemaphoreType.DMA((7,)),
                        pltpu.SemaphoreType.DMA((7,))],
    )(pack)


def _adamw_math(w, g, m, v):
    m = ADAM_B1 * m + (1.0 - ADAM_B1) * g
    v = ADAM_B2 * v + (1.0 - ADAM_B2) * (g * g)
    m_hat = m / (1.0 - ADAM_B1 ** ADAM_STEP)
    v_hat = v / (1.0 - ADAM_B2 ** ADAM_STEP)
    delta = -ADAM_LR * (m_hat / (jnp.sqrt(v_hat) + ADAM_EPS) + ADAM_WD * w)
    return delta, m, v


def _adamw_big(w, g, m, v, i):
    _, k, r, cdim = BIG[i]
    nblk = 4 if k == 1 else 1

    def body(w_ref, g_ref, m_ref, v_ref, d_ref, nm_ref, nv_ref):
        d_ref[...], nm_ref[...], nv_ref[...] = _adamw_math(w_ref[...], g_ref[...], m_ref[...], v_ref[...])

    spec = pl.BlockSpec((1, r // nblk, cdim), lambda t, b: (t, b, 0))
    return pl.pallas_call(
        body, name=f"adamw_{BIG[i][0]}", grid=(k, nblk), in_specs=[spec] * 4, out_specs=[spec] * 3,
        out_shape=[_sds((k, r, cdim), F32)] * 3,
        compiler_params=_params(("parallel", "parallel")),
    )(w, g, m, v)


def _adamw_small(ws, gs, ms, vs):
    n = len(ws)

    def body(*refs):
        for i in range(n):
            w_ref, g_ref, m_ref, v_ref = refs[i], refs[n + i], refs[2 * n + i], refs[3 * n + i]
            d, nm, nv = _adamw_math(w_ref[...], g_ref[...], m_ref[...], v_ref[...])
            refs[4 * n + i][...] = d
            refs[5 * n + i][...] = nm
            refs[6 * n + i][...] = nv

    specs = [_full(w.shape) for w in ws]
    outs = pl.pallas_call(
        body, name="adamw_small", grid=(1,), in_specs=specs * 4, out_specs=specs * 3,
        out_shape=[_sds(w.shape, F32) for w in ws] * 3,
        compiler_params=_params(("arbitrary",)),
    )(*ws, *gs, *ms, *vs)
    return outs[:n], outs[n:2 * n], outs[2 * n:]


def _pad_rows(a, rows):
    return jnp.pad(a, ((0, rows - a.shape[0]), (0, 0)))


def kernel(x, mem, positions, norm_g, mem_norm_g, w_mem_kv, attn_w_in, attn_w_out, conv_w_in, conv_w, conv_w_out, final_g, loss_target, m_norm_g, m_mem_norm_g, m_w_mem_kv, m_attn_w_in, m_attn_w_out, m_conv_w_in, m_conv_w, m_conv_w_out, m_final_g, v_norm_g, v_mem_norm_g, v_w_mem_kv, v_attn_w_in, v_attn_w_out, v_conv_w_in, v_conv_w, v_conv_w_out, v_final_g):
    mx, my, mc = lax.axis_index("x"), lax.axis_index("y"), lax.axis_index("c")
    place = jnp.stack([mc, 2 * mx + my]).astype(jnp.int32)

    w_big = [w_mem_kv, attn_w_in, attn_w_out, conv_w_in, conv_w_out]
    m_big = [m_w_mem_kv, m_attn_w_in, m_attn_w_out, m_conv_w_in, m_conv_w_out]
    v_big = [v_w_mem_kv, v_attn_w_in, v_attn_w_out, v_conv_w_in, v_conv_w_out]
    wb = _cast_weights(place, w_big)
    full = _gather_weights(wb, _pad_rows(conv_w[0], CW_ROWS))
    wkv_f, w_in_a_f, w_out_a_f, w_in_b_f, w_out_b_f, cw_f = full
    cw8 = cw_f.transpose(1, 0, 2).reshape(CW_ROWS, D)

    gx, small, big = _local_step(
        x[0], mem[0], positions[0].astype(F32).reshape(S, 1), norm_g, mem_norm_g, final_g, loss_target[0],
        wkv_f, w_in_a_f.reshape(4, D, SH_A), w_out_a_f.reshape(4, BR_A, SH_O), w_in_b_f.reshape(4, D, SH_B), cw8,
        w_out_b_f.reshape(BR_B, D))

    gs = [big["wkv"], big["w_in_a"].reshape(4, 1, D, SH_A), big["w_out_a"].reshape(4, 1, BR_A, SH_O),
          big["w_in_b"].reshape(4, 1, D, SH_B), big["w_out_b"].reshape(4, 1, BR_B // 4, D)]
    r1 = _pair_exchange(gs)
    ps = [_pair_sum(place, gs[i], r1[i], i) for i in range(NBIG)]
    r2 = _chip_exchange(ps)
    hs = [_chip_sum(place, gs[i], r1[i], r2[i], i) for i in range(NBIG)]
    g_big = _pair_gather(hs)

    pack = jnp.concatenate([_pad_rows(small["dnorm"], 8), _pad_rows(small["dmemnorm"], 8), _pad_rows(small["dfinal"], 8),
                            small["dconv"], _pad_rows(jnp.pad(small["loss"], ((0, 0), (0, D - 128))), 8)], axis=0)
    tot = _all_reduce_small(pack)
    loss = tot[32, 0]
    g_norm, g_memnorm, g_final = tot[0:2], tot[8:10], tot[16]
    g_conv = lax.dynamic_slice(tot, (24, (2 * mx + my) * SH_O), (3, SH_O))

    upd = [_adamw_big(w_big[i], g_big[i], m_big[i], v_big[i], i) for i in range(NBIG)]
    sw = [norm_g, mem_norm_g, final_g.reshape(1, D), conv_w[0]]
    sg = [g_norm, g_memnorm, g_final.reshape(1, D), g_conv]
    sm = [m_norm_g, m_mem_norm_g, m_final_g.reshape(1, D), m_conv_w[0]]
    sv = [v_norm_g, v_mem_norm_g, v_final_g.reshape(1, D), v_conv_w[0]]
    sd, snm, snv = _adamw_small(sw, sg, sm, sv)

    def order(norm, memnorm, wkv, w_in_a, w_out_a, w_in_b, conv, w_out_b, final):
        return (norm, memnorm, wkv, w_in_a, w_out_a, w_in_b, conv.reshape(1, 3, SH_O), w_out_b, final.reshape(D))

    grads = order(g_norm, g_memnorm, g_big[0], g_big[1], g_big[2], g_big[3], g_conv, g_big[4], g_final)
    deltas = order(sd[0], sd[1], upd[0][0], upd[1][0], upd[2][0], upd[3][0], sd[3], upd[4][0], sd[2])
    new_m = order(snm[0], snm[1], upd[0][1], upd[1][1], upd[2][1], upd[3][1], snm[3], upd[4][1], snm[2])
    new_v = order(snv[0], snv[1], upd[0][2], upd[1][2], upd[2][2], upd[3][2], snv[3], upd[4][2], snv[2])
    return (loss, gx[None], *grads, *deltas, *new_m, *new_v)
```

```python
import functools

import numpy as np
import jax
import jax.numpy as jnp
from jax import lax
from jax.experimental import pallas as pl
from jax.experimental.pallas import tpu as pltpu

F32 = jnp.float32
BF16 = jnp.bfloat16

S = 2048
D = 1024
TM = 256
NT = S // TM
HD = 64
GW = 512
NQ = 3 * GW
MW = 256
NM = 256
IN_A = 3 * NQ + MW + GW + MW
IN_B = 3 * D + MW + D + MW
BR_A = GW + MW
BR_B = D + MW
SH_A = IN_A // 4
SH_B = IN_B // 4
SH_O = D // 4
QBLK = 128
DILATIONS = (1, 4, 16)
EPS = 1e-6
SCALE = HD ** -0.5
NEG = -1e30
ROPE_THETA = 500000.0

ADAM_LR = 0.001
ADAM_B1 = 0.9
ADAM_B2 = 0.999
ADAM_EPS = 1e-08
ADAM_WD = 0.01
ADAM_STEP = 10

VMEM_LIMIT_BYTES = 60 * 1024 * 1024


def _params(sem=None):
    if sem is None:
        return pltpu.CompilerParams(vmem_limit_bytes=VMEM_LIMIT_BYTES)
    return pltpu.CompilerParams(dimension_semantics=sem, vmem_limit_bytes=VMEM_LIMIT_BYTES)


def _full(shape):
    nd = len(shape)
    return pl.BlockSpec(shape, lambda *_: (0,) * nd)


def _rows(width, tm=TM):
    return pl.BlockSpec((tm, width), lambda i: (i, 0))


def _sds(shape, dtype):
    return jax.ShapeDtypeStruct(shape, dtype)


def _silu_parts(z):
    sig = 1.0 / (1.0 + jnp.exp(-z))
    return z * sig, sig * (1.0 + z * (1.0 - sig))


def _dot(a, b):
    return jnp.dot(a, b, preferred_element_type=F32)


def _dot_nt(a, b):
    return lax.dot_general(a, b, (((1,), (1,)), ((), ())), preferred_element_type=F32)


def _dot_tn(a, b):
    return lax.dot_general(a, b, (((0,), (0,)), ((), ())), preferred_element_type=F32)


def _rope_fwd(t, c, s1, s2):
    return t * c + pltpu.roll(t, 120, 1) * s1 + pltpu.roll(t, 8, 1) * s2


def _rope_bwd(g, c, s1, s2):
    return g * c + pltpu.roll(g * s1, 8, 1) + pltpu.roll(g * s2, 120, 1)


def _mem_attn(qm, kv):
    res = []
    for h in range(MW // HD):
        sl = slice(h * HD, (h + 1) * HD)
        s = _dot_nt(qm[:, sl], kv[:, sl]) * SCALE
        e = jnp.exp(s - jnp.max(s, axis=-1, keepdims=True))
        p = e / jnp.sum(e, axis=-1, keepdims=True)
        res.append((p, _dot(p.astype(BF16), kv[:, MW + h * HD:MW + (h + 1) * HD])))
    return res


def _mem_attn_bwd(dmo, heads, qm, kv, dqm_store, dkv_ref):
    for h, (p, mo) in enumerate(heads):
        sl = slice(h * HD, (h + 1) * HD)
        vs = slice(MW + h * HD, MW + (h + 1) * HD)
        dmo_h = dmo[:, sl]
        dmo_b = dmo_h.astype(BF16)
        dp = _dot_nt(dmo_b, kv[:, vs])
        delta = jnp.sum(dmo_h * mo, axis=-1, keepdims=True)
        ds = (p * (dp - delta) * SCALE).astype(BF16)
        dqm_store(h, _dot(ds, kv[:, sl]))
        dkv_ref[:, sl] += _dot_tn(ds, qm[:, sl])
        dkv_ref[:, vs] += _dot_tn(p.astype(BF16), dmo_b)


def _merge(o_refs, l_refs):
    ls = [r[...] for r in l_refs]
    m = jnp.maximum(jnp.maximum(ls[0], ls[1]), ls[2])
    es = [jnp.exp(l - m) for l in ls]
    inv = 1.0 / (es[0] + es[1] + es[2])
    ws = [e * inv for e in es]
    os_ = [r[...] for r in o_refs]
    mix = ws[0] * os_[0] + ws[1] * os_[1] + ws[2] * os_[2]
    return ws, mix


def _conv_taps(cg, u, cgp, up, first):
    a = cg * u
    ap = jnp.where(first, 0.0, cgp * up)
    row = lax.broadcasted_iota(jnp.int32, a.shape, 0)
    a1 = jnp.where(row == 0, ap[7:8, :], pltpu.roll(a, 1, 0))
    a2 = jnp.where(row == 0, ap[6:7, :], jnp.where(row == 1, ap[7:8, :], pltpu.roll(a, 2, 0)))
    return a, a1, a2


def _rope_tables(posf):
    half = 8
    invf = np.float32(ROPE_THETA) ** (-np.arange(half, dtype=np.float32) * np.float32(2.0 / 16))
    lane = np.arange(128)
    table = np.where((lane % HD) < 16, invf[lane % half], 0.0).astype(np.float32)[None, :]

    def body(pos_ref, invf_ref, c_ref, s1_ref, s2_ref):
        ang = pos_ref[...] * invf_ref[...]
        jm = lax.broadcasted_iota(jnp.int32, ang.shape, 1) & (HD - 1)
        cs = jnp.cos(ang)
        sn = jnp.sin(ang)
        c_ref[...] = jnp.where(jm < 16, cs, 1.0)
        s1_ref[...] = jnp.where(jm < 8, -sn, 0.0)
        s2_ref[...] = jnp.where((jm >= 8) & (jm < 16), sn, 0.0)

    out = _sds((S, 128), F32)
    return pl.pallas_call(
        body, name="rope_tables", grid=(NT,),
        in_specs=[_rows(1), _full((1, 128))],
        out_specs=[_rows(128)] * 3, out_shape=[out] * 3,
        compiler_params=_params(("parallel",)),
    )(posf, jnp.asarray(table))


def _in_proj_a(x, g0, w_in, c, s1, s2):
    def body(x_ref, g_ref, w_ref, c_ref, s1_ref, s2_ref, hn_ref, q_ref, k_ref, v_ref, qm_ref, z_ref, proj):
        xf = x_ref[...]
        hn = xf * lax.rsqrt(jnp.mean(xf * xf, axis=-1, keepdims=True) + EPS) * g_ref[...]
        hb = hn.astype(BF16)
        hn_ref[...] = hb
        for s in range(4):
            proj[:, s * SH_A:(s + 1) * SH_A] = _dot(hb, w_ref[s])
        cc, a1, a2 = c_ref[...], s1_ref[...], s2_ref[...]
        for j in range(NQ // 128):
            q_ref[:, j * 128:(j + 1) * 128] = (
                _rope_fwd(proj[:, j * 128:(j + 1) * 128], cc, a1, a2) * SCALE).astype(BF16)
            k_ref[:, j * 128:(j + 1) * 128] = _rope_fwd(
                proj[:, NQ + j * 128:NQ + (j + 1) * 128], cc, a1, a2).astype(BF16)
        v_ref[...] = proj[:, 2 * NQ:3 * NQ].astype(BF16)
        qm_ref[...] = proj[:, 3 * NQ:3 * NQ + MW].astype(BF16)
        z_ref[...] = proj[:, 3 * NQ + MW:]

    return pl.pallas_call(
        body, name="in_proj_a", grid=(NT,),
        in_specs=[_rows(D), _full((1, D)), _full((4, D, SH_A)), _rows(128), _rows(128), _rows(128)],
        out_specs=[_rows(D), _rows(NQ), _rows(NQ), _rows(NQ), _rows(MW), _rows(BR_A)],
        out_shape=[_sds((S, D), BF16), _sds((S, NQ), BF16), _sds((S, NQ), BF16), _sds((S, NQ), BF16),
                   _sds((S, MW), BF16), _sds((S, BR_A), F32)],
        scratch_shapes=[pltpu.VMEM((TM, IN_A), F32)],
        compiler_params=_params(("parallel",)),
    )(x, g0, w_in, c, s1, s2)


def _mem_fwd(mem, mg, wkv):
    def body(mem_ref, mg_ref, w_ref, memn_ref, kv_ref):
        mf = mem_ref[...]
        n = mf * lax.rsqrt(jnp.mean(mf * mf, axis=-1, keepdims=True) + EPS)
        for i in range(2):
            mn = (n * mg_ref[i:i + 1, :]).astype(BF16)
            memn_ref[i] = mn
            acc = _dot(mn[:, 0:NM], w_ref[0, i])
            for s in range(1, 4):
                acc += _dot(mn[:, s * NM:(s + 1) * NM], w_ref[s, i])
            kv_ref[i] = acc.astype(BF16)

    return pl.pallas_call(
        body, name="mem_fwd", grid=(1,),
        in_specs=[_full((NM, D)), _full((2, D)), _full((4, 2, NM, 2 * MW))],
        out_specs=[_full((2, NM, D)), _full((2, NM, 2 * MW))],
        out_shape=[_sds((2, NM, D), BF16), _sds((2, NM, 2 * MW), BF16)],
        compiler_params=_params(("arbitrary",)),
    )(mem, mg, wkv)


def _band_mask(j):
    qi = lax.broadcasted_iota(jnp.int32, (QBLK, 2 * QBLK), 0)
    kj = lax.broadcasted_iota(jnp.int32, (QBLK, 2 * QBLK), 1)
    dist = qi + QBLK - kj
    return (dist >= 0) & (dist <= QBLK) & ((kj >= QBLK) | (j > 0))


LANES = 128
NCHUNK = GW // LANES
FWD_UNROLL = 16
BWD_UNROLL = 4


def _perm_matrix(d):
    n = TM // d
    p = np.zeros((TM, TM), np.float32)
    for r in range(d):
        for i in range(n):
            p[r * n + i, i * d + r] = 1.0
    return p


def _split_dot(p, x, parts):
    acc = None
    for _ in range(parts):
        hi = x.astype(BF16)
        term = _dot(p, hi)
        acc = term if acc is None else acc + term
        x = x - hi.astype(F32)
    return acc


def _tile_to_streams(y, dst, t, d):
    n, ln = TM // d, S // d
    for r in range(d):
        dst[r * ln + t * n:r * ln + (t + 1) * n, :] = y[r * n:(r + 1) * n].astype(dst.dtype)


def _tile_from_streams(src, t, d):
    n, ln = TM // d, S // d
    return jnp.concatenate([src[r * ln + t * n:r * ln + (t + 1) * n, :] for r in range(d)], axis=0)


def _head_masks():
    first = lax.broadcasted_iota(jnp.int32, (TM, LANES), 1) < HD
    return first, jnp.logical_not(first)


def _attn_fwd(q, k, v, g):
    d = DILATIONS[g]
    nb = S // d // QBLK
    perm = _perm_matrix(d)

    def body(q_ref, k_ref, v_ref, p_ref, pt_ref, o_ref, l_ref, ls_ref, q0, q1, ks, vs, os_):
        first, second = _head_masks()
        pm = p_ref[...]
        for t in range(NT):
            rows = slice(t * TM, (t + 1) * TM)
            if d == 1:
                qt = q_ref[rows, :].astype(F32)
            else:
                qt = _dot(pm, q_ref[rows, :])
                _tile_to_streams(_dot(pm, k_ref[rows, :]), ks, t, d)
                _tile_to_streams(_dot(pm, v_ref[rows, :]), vs, t, d)
            _tile_to_streams(jnp.where(first, qt, 0.0), q0, t, d)
            _tile_to_streams(jnp.where(second, qt, 0.0), q1, t, d)
        kref, vref = (k_ref, v_ref) if d == 1 else (ks, vs)
        oref, lref = (o_ref, l_ref) if d == 1 else (os_, ls_ref)

        def blk(b, carry):
            r0 = pl.multiple_of(b * QBLK, QBLK)
            p0 = pl.multiple_of(jnp.maximum(b - 1, 0) * QBLK, QBLK)
            kk = jnp.concatenate([kref[pl.ds(p0, QBLK), :], kref[pl.ds(r0, QBLK), :]], axis=0)
            vv = jnp.concatenate([vref[pl.ds(p0, QBLK), :], vref[pl.ds(r0, QBLK), :]], axis=0)
            valid = _band_mask(b & (nb - 1))
            acc, den, lse = [], [], []
            for qh in (q0, q1):
                s = jnp.where(valid, _dot_nt(qh[pl.ds(r0, QBLK), :], kk), NEG)
                m = jnp.max(s, axis=-1, keepdims=True)
                e = jnp.exp(s - m)
                l = jnp.sum(e, axis=-1, keepdims=True)
                acc.append(_dot(e.astype(BF16), vv))
                den.append(l)
                lse.append(m + jnp.log(l))
            f = first[:QBLK]
            oref[pl.ds(r0, QBLK), :] = jnp.where(f, acc[0], acc[1]) / jnp.where(f, den[0], den[1])
            lref[pl.ds(r0, QBLK), :] = jnp.where(f, lse[0], lse[1])
            return carry

        lax.fori_loop(0, S // QBLK, blk, 0, unroll=FWD_UNROLL)
        if d > 1:
            ptm = pt_ref[...]
            for t in range(NT):
                rows = slice(t * TM, (t + 1) * TM)
                o_ref[rows, :] = _split_dot(ptm, _tile_from_streams(os_, t, d), 2)
                l_ref[rows, :] = _split_dot(ptm, _tile_from_streams(ls_ref, t, d), 3)

    qkv_spec = pl.BlockSpec((S, LANES), lambda c: (0, g * NCHUNK + c))
    out_spec = pl.BlockSpec((S, LANES), lambda c: (0, c))
    n_out = 2 if d == 1 else 3
    outs = pl.pallas_call(
        body if d > 1 else functools.partial(_drop_arg, body, 7), name=f"attn_fwd_g{g}", grid=(NCHUNK,),
        in_specs=[qkv_spec] * 3 + [_full((TM, TM))] * 2, out_specs=[out_spec] * n_out,
        out_shape=[_sds((S, GW), F32)] * n_out,
        scratch_shapes=[pltpu.VMEM((S, LANES), BF16)] * 4 + [pltpu.VMEM((S, LANES), F32)],
        compiler_params=_params(("parallel",)),
    )(q, k, v, jnp.asarray(perm, BF16), jnp.asarray(perm.T, BF16))
    return (outs[0], outs[1], outs[1]) if d == 1 else tuple(outs)


def _drop_arg(body, pos, *refs):
    return body(*refs[:pos], None, *refs[pos:])


def _attn_out(os_, ls, qm, kv0, z, x, w_out):
    def body(o0, o1, o2, l0, l1, l2, qm_ref, kv_ref, z_ref, x_ref, w_ref, h_ref, ybuf):
        _, mix = _merge((o0, o1, o2), (l0, l1, l2))
        sz, _ = _silu_parts(z_ref[...])
        ybuf[:, :GW] = (mix * sz[:, :GW]).astype(BF16)
        for h, (_, mo) in enumerate(_mem_attn(qm_ref[...], kv_ref[...])):
            sl = slice(GW + h * HD, GW + (h + 1) * HD)
            ybuf[:, sl] = (mo * sz[:, sl]).astype(BF16)
        yb = ybuf[...]
        for s in range(4):
            cs = slice(s * SH_O, (s + 1) * SH_O)
            h_ref[:, cs] = x_ref[:, cs] + _dot(yb, w_ref[s])

    return pl.pallas_call(
        body, name="attn_out", grid=(NT,),
        in_specs=[_rows(GW)] * 6 + [_rows(MW), _full((NM, 2 * MW)), _rows(BR_A), _rows(D), _full((4, BR_A, SH_O))],
        out_specs=_rows(D), out_shape=_sds((S, D), F32),
        scratch_shapes=[pltpu.VMEM((TM, BR_A), BF16)],
        compiler_params=_params(("parallel",)),
    )(*os_, *ls, qm, kv0, z, x, w_out)


def _in_proj_b(h1, g1, w_in):
    def body(x_ref, g_ref, w_ref, hn_ref, bg_ref, cg_ref, u_ref, qm_ref, z_ref, proj):
        xf = x_ref[...]
        hn = xf * lax.rsqrt(jnp.mean(xf * xf, axis=-1, keepdims=True) + EPS) * g_ref[...]
        hb = hn.astype(BF16)
        hn_ref[...] = hb
        for s in range(4):
            proj[:, s * SH_B:(s + 1) * SH_B] = _dot(hb, w_ref[s])
        bg_ref[...] = proj[:, :D]
        cg_ref[...] = proj[:, D:2 * D]
        u_ref[...] = proj[:, 2 * D:3 * D]
        qm_ref[...] = proj[:, 3 * D:3 * D + MW].astype(BF16)
        z_ref[...] = proj[:, 3 * D + MW:]

    return pl.pallas_call(
        body, name="in_proj_b", grid=(NT,),
        in_specs=[_rows(D), _full((1, D)), _full((4, D, SH_B))],
        out_specs=[_rows(D), _rows(D), _rows(D), _rows(D), _rows(MW), _rows(BR_B)],
        out_shape=[_sds((S, D), BF16), _sds((S, D), F32), _sds((S, D), F32), _sds((S, D), F32),
                   _sds((S, MW), BF16), _sds((S, BR_B), F32)],
        scratch_shapes=[pltpu.VMEM((TM, IN_B), F32)],
        compiler_params=_params(("parallel",)),
    )(h1, g1, w_in)


def _prev8(width):
    return pl.BlockSpec((8, width), lambda i: (jnp.maximum(i * (TM // 8) - 1, 0), 0))


def _conv_out_loss(bg, cg, u, cw, qm, kv1, z, h1, w_out, fg, tgt):
    def body(bg_ref, cg_ref, u_ref, cgp_ref, up_ref, cw_ref, qm_ref, kv_ref, z_ref, h_ref, w_ref, fg_ref, t_ref,
             dh_ref, loss_ref, dfg_ref, ybuf):
        i = pl.program_id(0)
        a, a1, a2 = _conv_taps(cg_ref[...], u_ref[...], cgp_ref[...], up_ref[...], i == 0)
        conv = cw_ref[0:1, :] * a2 + cw_ref[1:2, :] * a1 + cw_ref[2:3, :] * a
        sz, _ = _silu_parts(z_ref[...])
        ybuf[:, :D] = (bg_ref[...] * conv * sz[:, :D]).astype(BF16)
        for h, (_, mo) in enumerate(_mem_attn(qm_ref[...], kv_ref[...])):
            sl = slice(D + h * HD, D + (h + 1) * HD)
            ybuf[:, sl] = (mo * sz[:, sl]).astype(BF16)
        h2 = h_ref[...] + _dot(ybuf[...], w_ref[...])
        rstd = lax.rsqrt(jnp.mean(h2 * h2, axis=-1, keepdims=True) + EPS)
        n = h2 * rstd
        fgv = fg_ref[...]
        err = n * fgv - t_ref[...]
        dout = err * (1.0 / D)
        dn = dout * fgv
        dh_ref[...] = rstd * (dn - n * jnp.mean(dn * n, axis=-1, keepdims=True))

        @pl.when(i == 0)
        def _():
            loss_ref[...] = jnp.zeros_like(loss_ref)
            dfg_ref[...] = jnp.zeros_like(dfg_ref)

        loss_ref[...] += jnp.sum(err * err) * (0.5 / D)
        dfg_ref[...] += jnp.sum(dout * n, axis=0, keepdims=True)

    return pl.pallas_call(
        body, name="conv_out_loss", grid=(NT,),
        in_specs=[_rows(D), _rows(D), _rows(D), _prev8(D), _prev8(D), _full((8, D)), _rows(MW),
                  _full((NM, 2 * MW)), _rows(BR_B), _rows(D), _full((BR_B, D)), _full((1, D)), _rows(D)],
        out_specs=[_rows(D), _full((1, 128)), _full((1, D))],
        out_shape=[_sds((S, D), F32), _sds((1, 128), F32), _sds((1, D), F32)],
        scratch_shapes=[pltpu.VMEM((TM, BR_B), BF16)],
        compiler_params=_params(("arbitrary",)),
    )(bg, cg, u, cg, u, cw, qm, kv1, z, h1, w_out, fg, tgt)


def _conv_bwd(dh2, bg, cg, u, cw, qm, kv1, z, w_out):
    rev = lambda i: (NT - 1 - i, 0)
    rows = lambda w: pl.BlockSpec((TM, w), rev)
    prev8 = pl.BlockSpec((8, D), lambda i: (jnp.maximum((NT - 1 - i) * (TM // 8) - 1, 0), 0))

    def body(dh_ref, bg_ref, cg_ref, u_ref, cgp_ref, up_ref, cw_ref, qm_ref, kv_ref, z_ref, w_ref,
             dproj_ref, dw_ref, dcw_ref, dkv_ref, ybuf, carry):
        i = pl.program_id(0)

        @pl.when(i == 0)
        def _():
            dw_ref[...] = jnp.zeros_like(dw_ref)
            dcw_ref[...] = jnp.zeros_like(dcw_ref)
            dkv_ref[...] = jnp.zeros_like(dkv_ref)
            carry[...] = jnp.zeros_like(carry)

        bgv, cgv, uv = bg_ref[...], cg_ref[...], u_ref[...]
        a, a1, a2 = _conv_taps(cgv, uv, cgp_ref[...], up_ref[...], i == NT - 1)
        w0, w1, w2 = cw_ref[0:1, :], cw_ref[1:2, :], cw_ref[2:3, :]
        conv = w0 * a2 + w1 * a1 + w2 * a
        mix = bgv * conv
        zv = z_ref[...]
        sz, dsz = _silu_parts(zv)
        qmv, kvv = qm_ref[...], kv_ref[...]
        heads = _mem_attn(qmv, kvv)
        ybuf[:, :D] = (mix * sz[:, :D]).astype(BF16)
        for h, (_, mo) in enumerate(heads):
            sl = slice(D + h * HD, D + (h + 1) * HD)
            ybuf[:, sl] = (mo * sz[:, sl]).astype(BF16)
        dhb = dh_ref[...].astype(BF16)
        dw_ref[...] += _dot_tn(ybuf[...], dhb)
        dy = _dot_nt(dhb, w_ref[...])
        dcat = dy * sz
        dproj_ref[:, 3 * D + MW:3 * D + MW + D] = (dy[:, :D] * mix * dsz[:, :D]).astype(BF16)
        for h, (_, mo) in enumerate(heads):
            sl = slice(D + h * HD, D + (h + 1) * HD)
            dproj_ref[:, 3 * D + MW + D + h * HD:3 * D + MW + D + (h + 1) * HD] = (
                dy[:, sl] * mo * dsz[:, sl]).astype(BF16)
        dmix = dcat[:, :D]
        dproj_ref[:, :D] = (dmix * conv).astype(BF16)
        dc = dmix * bgv
        nxt = carry[...]
        row = lax.broadcasted_iota(jnp.int32, dc.shape, 0)
        dc1 = jnp.where(row == TM - 1, nxt[0:1, :], pltpu.roll(dc, TM - 1, 0))
        dc2 = jnp.where(row == TM - 2, nxt[0:1, :], jnp.where(row == TM - 1, nxt[1:2, :], pltpu.roll(dc, TM - 2, 0)))
        carry[...] = dc[0:8, :]
        da = w2 * dc + w1 * dc1 + w0 * dc2
        dproj_ref[:, D:2 * D] = (da * uv).astype(BF16)
        dproj_ref[:, 2 * D:3 * D] = (da * cgv).astype(BF16)
        dcw_ref[0:1, :] += jnp.sum(dc * a2, axis=0, keepdims=True)
        dcw_ref[1:2, :] += jnp.sum(dc * a1, axis=0, keepdims=True)
        dcw_ref[2:3, :] += jnp.sum(dc * a, axis=0, keepdims=True)

        def dqm_store(h, val):
            dproj_ref[:, 3 * D + h * HD:3 * D + (h + 1) * HD] = val.astype(BF16)

        _mem_attn_bwd(dcat[:, D:], heads, qmv, kvv, dqm_store, dkv_ref)

    return pl.pallas_call(
        body, name="conv_bwd", grid=(NT,),
        in_specs=[rows(D), rows(D), rows(D), rows(D), prev8, prev8, _full((8, D)), rows(MW),
                  _full((NM, 2 * MW)), rows(BR_B), _full((BR_B, D))],
        out_specs=[rows(IN_B), _full((BR_B, D)), _full((8, D)), _full((NM, 2 * MW))],
        out_shape=[_sds((S, IN_B), BF16), _sds((BR_B, D), F32), _sds((8, D), F32), _sds((NM, 2 * MW), F32)],
        scratch_shapes=[pltpu.VMEM((TM, BR_B), BF16), pltpu.VMEM((8, D), F32)],
        compiler_params=_params(("arbitrary",)),
    )(dh2, bg, cg, u, cg, u, cw, qm, kv1, z, w_out)


def _in_proj_bwd(dproj, w_in, xin, g, dres, width, name):
    sh = width // 4

    def body(dp_ref, w_ref, x_ref, g_ref, dr_ref, dx_ref, dg_ref):
        i = pl.program_id(0)
        dhn = _dot_nt(dp_ref[:, 0:sh], w_ref[0])
        for s in range(1, 4):
            dhn += _dot_nt(dp_ref[:, s * sh:(s + 1) * sh], w_ref[s])
        xf = x_ref[...]
        rstd = lax.rsqrt(jnp.mean(xf * xf, axis=-1, keepdims=True) + EPS)
        n = xf * rstd
        dn = dhn * g_ref[...]
        dx_ref[...] = dr_ref[...] + rstd * (dn - n * jnp.mean(dn * n, axis=-1, keepdims=True))

        @pl.when(i == 0)
        def _():
            dg_ref[...] = jnp.zeros_like(dg_ref)

        dg_ref[...] += jnp.sum(dhn * n, axis=0, keepdims=True)

    return pl.pallas_call(
        body, name=name, grid=(NT,),
        in_specs=[_rows(width), _full((4, D, sh)), _rows(D), _full((1, D)), _rows(D)],
        out_specs=[_rows(D), _full((1, D))],
        out_shape=[_sds((S, D), F32), _sds((1, D), F32)],
        compiler_params=_params(("arbitrary",)),
    )(dproj, w_in, xin, g, dres)


def _w_in_grad(hn, dproj, width, name):
    sh = width // 4

    def body(hn_ref, dp_ref, dw_ref):
        dw_ref[0] = _dot_tn(hn_ref[...], dp_ref[...])

    return pl.pallas_call(
        body, name=name, grid=(4,),
        in_specs=[_full((S, D)), pl.BlockSpec((S, sh), lambda s: (0, s))],
        out_specs=pl.BlockSpec((1, D, sh), lambda s: (s, 0, 0)),
        out_shape=_sds((4, D, sh), F32),
        compiler_params=_params(("parallel",)),
    )(hn, dproj)


def _attn_out_bwd(dh1, os_, ls, qm, kv0, z, w_out):
    ones_bd = np.kron(np.eye(GW // HD, dtype=np.float32), np.ones((HD, HD), np.float32))

    def body(dh_ref, o0, o1, o2, l0, l1, l2, qm_ref, kv_ref, z_ref, w_ref, bd_ref,
             do0, do1, do2, dd0, dd1, dd2, dqm_ref, dz_ref, dw_ref, dkv_ref, ybuf):
        i = pl.program_id(0)

        @pl.when(i == 0)
        def _():
            dw_ref[...] = jnp.zeros_like(dw_ref)
            dkv_ref[...] = jnp.zeros_like(dkv_ref)

        ws, mix = _merge((o0, o1, o2), (l0, l1, l2))
        sz, dsz = _silu_parts(z_ref[...])
        qmv, kvv = qm_ref[...], kv_ref[...]
        heads = _mem_attn(qmv, kvv)
        ybuf[:, :GW] = (mix * sz[:, :GW]).astype(BF16)
        for h, (_, mo) in enumerate(heads):
            sl = slice(GW + h * HD, GW + (h + 1) * HD)
            ybuf[:, sl] = (mo * sz[:, sl]).astype(BF16)
        yb = ybuf[...]
        dh = dh_ref[...]
        dy = None
        for s in range(4):
            dhb = dh[:, s * SH_O:(s + 1) * SH_O].astype(BF16)
            dw_ref[s] += _dot_tn(yb, dhb)
            part = _dot_nt(dhb, w_ref[s])
            dy = part if dy is None else dy + part
        dcat = dy * sz
        dz_ref[:, :GW] = (dy[:, :GW] * mix * dsz[:, :GW]).astype(BF16)
        for h, (_, mo) in enumerate(heads):
            sl = slice(GW + h * HD, GW + (h + 1) * HD)
            dz_ref[:, sl] = (dy[:, sl] * mo * dsz[:, sl]).astype(BF16)
        dmix = dcat[:, :GW]
        prod = dmix * mix
        hi = prod.astype(BF16)
        lo = (prod - hi.astype(F32)).astype(BF16)
        bd = bd_ref[...]
        tot = _dot(hi, bd) + _dot(lo, bd)
        for w, do_ref, dd_ref in zip(ws, (do0, do1, do2), (dd0, dd1, dd2)):
            do_ref[...] = (w * dmix).astype(BF16)
            dd_ref[...] = w * tot

        def dqm_store(h, val):
            dqm_ref[:, h * HD:(h + 1) * HD] = val.astype(BF16)

        _mem_attn_bwd(dcat[:, GW:], heads, qmv, kvv, dqm_store, dkv_ref)

    return pl.pallas_call(
        body, name="attn_out_bwd", grid=(NT,),
        in_specs=[_rows(D)] + [_rows(GW)] * 6 + [_rows(MW), _full((NM, 2 * MW)), _rows(BR_A),
                                                   _full((4, BR_A, SH_O)), _full((GW, GW))],
        out_specs=[_rows(GW)] * 6 + [_rows(MW), _rows(BR_A), _full((4, BR_A, SH_O)), _full((NM, 2 * MW))],
        out_shape=[_sds((S, GW), BF16)] * 3 + [_sds((S, GW), F32)] * 3 + [
            _sds((S, MW), BF16), _sds((S, BR_A), BF16), _sds((4, BR_A, SH_O), F32), _sds((NM, 2 * MW), F32)],
        scratch_shapes=[pltpu.VMEM((TM, BR_A), BF16)],
        compiler_params=_params(("arbitrary",)),
    )(dh1, *os_, *ls, qm, kv0, z, w_out, jnp.asarray(ones_bd, dtype=BF16))


def _attn_bwd(q, k, v, do, lse_s, dd, g):
    d = DILATIONS[g]
    nb = S // d // QBLK
    perm = _perm_matrix(d)

    def body(q_ref, k_ref, v_ref, do_ref, l_ref, dd_ref, p_ref, pt_ref, dq_ref, dk_ref, dv_ref,
             q0, q1, g0, g1, ks, vs, dds, dqs, dks, dvs):
        first, second = _head_masks()
        pm = p_ref[...]
        for t in range(NT):
            rows = slice(t * TM, (t + 1) * TM)
            if d == 1:
                qt = q_ref[rows, :].astype(F32)
                gt = do_ref[rows, :].astype(F32)
            else:
                qt = _dot(pm, q_ref[rows, :])
                gt = _dot(pm, do_ref[rows, :])
                _tile_to_streams(_dot(pm, k_ref[rows, :]), ks, t, d)
                _tile_to_streams(_dot(pm, v_ref[rows, :]), vs, t, d)
                _tile_to_streams(_split_dot(pm, dd_ref[rows, :], 2), dds, t, d)
            _tile_to_streams(jnp.where(first, qt, 0.0), q0, t, d)
            _tile_to_streams(jnp.where(second, qt, 0.0), q1, t, d)
            _tile_to_streams(jnp.where(first, gt, 0.0), g0, t, d)
            _tile_to_streams(jnp.where(second, gt, 0.0), g1, t, d)
        kref, vref, ddref = (k_ref, v_ref, dd_ref) if d == 1 else (ks, vs, dds)
        dqref, dkref, dvref = (dq_ref, dk_ref, dv_ref) if d == 1 else (dqs, dks, dvs)
        dkref[...] = jnp.zeros_like(dkref)
        dvref[...] = jnp.zeros_like(dvref)

        def blk(b, carry):
            r0 = pl.multiple_of(b * QBLK, QBLK)
            p0 = pl.multiple_of(jnp.maximum(b - 1, 0) * QBLK, QBLK)
            kk = jnp.concatenate([kref[pl.ds(p0, QBLK), :], kref[pl.ds(r0, QBLK), :]], axis=0)
            vv = jnp.concatenate([vref[pl.ds(p0, QBLK), :], vref[pl.ds(r0, QBLK), :]], axis=0)
            lb = l_ref[pl.ds(r0, QBLK), :]
            ddb = ddref[pl.ds(r0, QBLK), :]
            valid = _band_mask(b & (nb - 1))
            dqh, dkk, dvv = [], None, None
            for h, (qh, gh) in enumerate(((q0, g0), (q1, g1))):
                qb = qh[pl.ds(r0, QBLK), :]
                gb = gh[pl.ds(r0, QBLK), :]
                s = _dot_nt(qb, kk)
                p = jnp.where(valid, jnp.exp(s - lb[:, h * HD:h * HD + 1]), 0.0)
                dp = _dot_nt(gb, vv)
                ds = (p * (dp - ddb[:, h * HD:h * HD + 1])).astype(BF16)
                dqh.append(_dot(ds, kk))
                tk = _dot_tn(ds, qb)
                tv = _dot_tn(p.astype(BF16), gb)
                dkk = tk if dkk is None else dkk + tk
                dvv = tv if dvv is None else dvv + tv
            dqref[pl.ds(r0, QBLK), :] = jnp.where(first[:QBLK], dqh[0], dqh[1])
            dkref[pl.ds(p0, QBLK), :] += dkk[:QBLK]
            dkref[pl.ds(r0, QBLK), :] += dkk[QBLK:]
            dvref[pl.ds(p0, QBLK), :] += dvv[:QBLK]
            dvref[pl.ds(r0, QBLK), :] += dvv[QBLK:]
            return carry

        lax.fori_loop(0, S // QBLK, blk, 0, unroll=BWD_UNROLL)
        if d > 1:
            ptm = pt_ref[...]
            for t in range(NT):
                rows = slice(t * TM, (t + 1) * TM)
                dq_ref[rows, :] = _split_dot(ptm, _tile_from_streams(dqs, t, d), 2)
                dk_ref[rows, :] = _split_dot(ptm, _tile_from_streams(dks, t, d), 2)
                dv_ref[rows, :] = _split_dot(ptm, _tile_from_streams(dvs, t, d), 2)

    qkv_spec = pl.BlockSpec((S, LANES), lambda c: (0, g * NCHUNK + c))
    one_spec = pl.BlockSpec((S, LANES), lambda c: (0, c))
    return pl.pallas_call(
        body, name=f"attn_bwd_g{g}", grid=(NCHUNK,),
        in_specs=[qkv_spec] * 3 + [one_spec] * 3 + [_full((TM, TM))] * 2, out_specs=[one_spec] * 3,
        out_shape=[_sds((S, GW), F32)] * 3,
        scratch_shapes=[pltpu.VMEM((S, LANES), BF16)] * 6 + [pltpu.VMEM((S, LANES), F32)] * 4,
        compiler_params=_params(("parallel",)),
    )(q, k, v, do, lse_s, dd, jnp.asarray(perm, BF16), jnp.asarray(perm.T, BF16))


def _qkv_bwd(dqs, dks, dvs, dqm, dz, c, s1, s2):
    def body(q0, q1, q2, k0, k1, k2, v0, v1, v2, dqm_ref, dz_ref, c_ref, s1_ref, s2_ref, dp_ref):
        cc, a1, a2 = c_ref[...], s1_ref[...], s2_ref[...]
        for g, (qr, kr, vr) in enumerate(((q0, k0, v0), (q1, k1, v1), (q2, k2, v2))):
            for j in range(GW // 128):
                ls_ = slice(j * 128, (j + 1) * 128)
                c0 = g * GW + j * 128
                dp_ref[:, c0:c0 + 128] = (_rope_bwd(qr[:, ls_], cc, a1, a2) * SCALE).astype(BF16)
                dp_ref[:, NQ + c0:NQ + c0 + 128] = _rope_bwd(kr[:, ls_], cc, a1, a2).astype(BF16)
            dp_ref[:, 2 * NQ + g * GW:2 * NQ + (g + 1) * GW] = vr[...].astype(BF16)
        dp_ref[:, 3 * NQ:3 * NQ + MW] = dqm_ref[...]
        dp_ref[:, 3 * NQ + MW:] = dz_ref[...]

    return pl.pallas_call(
        body, name="qkv_bwd", grid=(NT,),
        in_specs=[_rows(GW)] * 9 + [_rows(MW), _rows(BR_A), _rows(128), _rows(128), _rows(128)],
        out_specs=_rows(IN_A), out_shape=_sds((S, IN_A), BF16),
        compiler_params=_params(("parallel",)),
    )(*dqs, *dks, *dvs, dqm, dz, c, s1, s2)


def _mem_bwd(mem, mg, memn, wkv, dkv0, dkv1):
    def body(mem_ref, mg_ref, memn_ref, w_ref, d0_ref, d1_ref, dw_ref, dg_ref):
        mf = mem_ref[...]
        n = mf * lax.rsqrt(jnp.mean(mf * mf, axis=-1, keepdims=True) + EPS)
        for i, d_ref in enumerate((d0_ref, d1_ref)):
            dkv = d_ref[...].astype(BF16)
            mn = memn_ref[i]
            for s in range(4):
                cs = slice(s * NM, (s + 1) * NM)
                dw_ref[s, i] = _dot_tn(mn[:, cs], dkv)
                dmn = _dot_nt(dkv, w_ref[s, i])
                dg_ref[i:i + 1, cs] = jnp.sum(dmn * n[:, cs], axis=0, keepdims=True)

    return pl.pallas_call(
        body, name="mem_bwd", grid=(1,),
        in_specs=[_full((NM, D)), _full((2, D)), _full((2, NM, D)), _full((4, 2, NM, 2 * MW)),
                  _full((NM, 2 * MW)), _full((NM, 2 * MW))],
        out_specs=[_full((4, 2, NM, 2 * MW)), _full((2, D))],
        out_shape=[_sds((4, 2, NM, 2 * MW), F32), _sds((2, D), F32)],
        compiler_params=_params(("arbitrary",)),
    )(mem, mg, memn, wkv, dkv0, dkv1)


def _local_step(x, mem, posf, norm_g, mem_norm_g, final_g, tgt, wkv, w_in_a, w_out_a, w_in_b, cw8, w_out_b):
    g0, g1 = norm_g[0:1], norm_g[1:2]
    c, s1, s2 = _rope_tables(posf)
    memn, kv = _mem_fwd(mem, mem_norm_g, wkv)
    hn0, q, k, v, qm0, z0 = _in_proj_a(x, g0, w_in_a, c, s1, s2)
    fwd = [_attn_fwd(q, k, v, g) for g in range(3)]
    os_, ls, lss = [f[0] for f in fwd], [f[1] for f in fwd], [f[2] for f in fwd]
    h1 = _attn_out(os_, ls, qm0, kv[0], z0, x, w_out_a)
    hn1, bg, cg, u, qm1, z1 = _in_proj_b(h1, g1, w_in_b)
    dh2, loss, dfg = _conv_out_loss(bg, cg, u, cw8, qm1, kv[1], z1, h1, w_out_b, final_g.reshape(1, D), tgt)

    dproj_b, dw_out_b, dcw, dkv1 = _conv_bwd(dh2, bg, cg, u, cw8, qm1, kv[1], z1, w_out_b)
    dh1, dg1 = _in_proj_bwd(dproj_b, w_in_b, h1, g1, dh2, IN_B, "in_proj_b_bwd")
    dw_in_b = _w_in_grad(hn1, dproj_b, IN_B, "w_in_b_grad")
    outs = _attn_out_bwd(dh1, os_, ls, qm0, kv[0], z0, w_out_a)
    dos, dds, dqm, dz, dw_out_a, dkv0 = outs[0:3], outs[3:6], outs[6], outs[7], outs[8], outs[9]
    bwd = [_attn_bwd(q, k, v, dos[g], lss[g], dds[g], g) for g in range(3)]
    dproj_a = _qkv_bwd([b[0] for b in bwd], [b[1] for b in bwd], [b[2] for b in bwd], dqm, dz, c, s1, s2)
    gx, dg0 = _in_proj_bwd(dproj_a, w_in_a, x, g0, dh1, IN_A, "in_proj_a_bwd")
    dw_in_a = _w_in_grad(hn0, dproj_a, IN_A, "w_in_a_grad")
    dwkv, dmg = _mem_bwd(mem, mem_norm_g, memn, wkv, dkv0, dkv1)
    small = dict(loss=loss, dnorm=jnp.concatenate([dg0, dg1], axis=0), dmemnorm=dmg, dfinal=dfg, dconv=dcw)
    big = dict(wkv=dwkv, w_in_a=dw_in_a, w_out_a=dw_out_a, w_in_b=dw_in_b, w_out_b=dw_out_b)
    return gx, small, big


MESH = pl.DeviceIdType.MESH
ANY = pl.BlockSpec(memory_space=pl.ANY)
BIG = (("wkv", 2, NM, 2 * MW), ("w_in_a", 1, D, SH_A), ("w_out_a", 1, BR_A, SH_O),
       ("w_in_b", 1, D, SH_B), ("w_out_b", 1, BR_B // 4, D))
NBIG = len(BIG)
CW_ROWS = 8


def _place():
    x, y, c = lax.axis_index("x"), lax.axis_index("y"), lax.axis_index("c")
    chips = ((1 - x, y), (x, 1 - y), (1 - x, 1 - y))
    return x, y, c, chips


def _remote(src, dst, ssem, rsem, dev):
    return pltpu.make_async_remote_copy(src_ref=src, dst_ref=dst, send_sem=ssem, recv_sem=rsem,
                                        device_id=dev, device_id_type=MESH)


def _cast_weights(place, ws):
    nblk = 4

    def body(pref, *refs):
        for i in range(NBIG):
            refs[NBIG + i][0] = refs[i][...].astype(BF16)

    grid_spec = pltpu.PrefetchScalarGridSpec(
        num_scalar_prefetch=1, grid=(nblk,),
        in_specs=[pl.BlockSpec((k, r // nblk, cdim), lambda i, pref: (0, i, 0)) for _, k, r, cdim in BIG],
        out_specs=[pl.BlockSpec((1, k, r // nblk, cdim), lambda i, pref: (pref[1], 0, i, 0)) for _, k, r, cdim in BIG])
    return pl.pallas_call(
        body, name="cast_weights", grid_spec=grid_spec,
        out_shape=[_sds((4, k, r, cdim), BF16) for _, k, r, cdim in BIG],
        compiler_params=_params(("parallel",)),
    )(place, *ws)


def _gather_weights(wb, cw):
    def body(*refs):
        src_cw = refs[NBIG]
        dst = refs[NBIG + 1:2 * NBIG + 2]
        loc_sem, send_sems, recv_sems, fsend_sems, frecv_sems = refs[2 * NBIG + 2:]
        x, y, c, chips = _place()
        me = 2 * x + y
        loc = pltpu.make_async_copy(src_cw, dst[NBIG].at[me], loc_sem)
        loc.start()

        def half(ref, i, which):
            if i == NBIG:
                return ref
            h = BIG[i][2] // 2
            return ref.at[:, pl.ds(which * h, h), :]

        sends = []
        for j, (px, py) in enumerate(chips):
            for i in range(NBIG + 1):
                mine = src_cw if i == NBIG else half(dst[i].at[me], i, c)
                sends.append(_remote(mine, half(dst[i].at[me], i, c),
                                     send_sems.at[j, i], recv_sems.at[j, i], (px, py, c)))
        for cp in sends:
            cp.start()
        fwds = []
        for j, (px, py) in enumerate(chips):
            for i in range(NBIG + 1):
                got = half(dst[i].at[2 * px + py], i, c)
                _remote(got, got, send_sems.at[j, i], recv_sems.at[j, i], (px, py, c)).wait_recv()
                if i < NBIG:
                    fwds.append(_remote(got, got, fsend_sems.at[j, i], frecv_sems.at[j, i], (x, y, 1 - c)))
                    fwds[-1].start()
        for j, (px, py) in enumerate(chips):
            for i in range(NBIG):
                got = half(dst[i].at[2 * px + py], i, 1 - c)
                _remote(got, got, fsend_sems.at[j, i], frecv_sems.at[j, i], (x, y, 1 - c)).wait_recv()
        for cp in sends + fwds:
            cp.wait_send()
        loc.wait()

    out_shape = [_sds((4, k, r, cdim), BF16) for _, k, r, cdim in BIG] + [_sds((4, CW_ROWS, SH_O), F32)]
    return pl.pallas_call(
        body, name="gather_weights", in_specs=[ANY] * (NBIG + 1), out_specs=[ANY] * (NBIG + 1), out_shape=out_shape,
        input_output_aliases={i: i for i in range(NBIG)},
        scratch_shapes=[pltpu.SemaphoreType.DMA, pltpu.SemaphoreType.DMA((3, NBIG + 1)),
                        pltpu.SemaphoreType.DMA((3, NBIG + 1)), pltpu.SemaphoreType.DMA((3, NBIG)),
                        pltpu.SemaphoreType.DMA((3, NBIG))],
    )(*wb, cw)


def _pair_exchange(gs):
    def body(*refs):
        src, dst = refs[:NBIG], refs[NBIG:2 * NBIG]
        send_sems, recv_sems = refs[2 * NBIG:]
        x, y, c, _ = _place()
        cps = []
        for i in range(NBIG):
            h = BIG[i][2] // 2
            cps.append(_remote(src[i].at[:, :, pl.ds((1 - c) * h, h), :], dst[i], send_sems.at[i], recv_sems.at[i],
                               (x, y, 1 - c)))
            cps[-1].start()
        for cp in cps:
            cp.wait()

    return pl.pallas_call(
        body, name="pair_exchange", in_specs=[ANY] * NBIG, out_specs=[ANY] * NBIG,
        out_shape=[_sds((4, k, r // 2, cdim), F32) for _, k, r, cdim in BIG],
        scratch_shapes=[pltpu.SemaphoreType.DMA((NBIG,)), pltpu.SemaphoreType.DMA((NBIG,))],
    )(*gs)


def _pair_sum(place, g, r1, i):
    _, k, r, cdim = BIG[i]
    h = r // 2

    def body(pref, g_ref, r_ref, o_ref):
        o_ref[...] = (g_ref[...] + r_ref[...]).astype(BF16)

    grid_spec = pltpu.PrefetchScalarGridSpec(
        num_scalar_prefetch=1, grid=(4, k),
        in_specs=[pl.BlockSpec((1, 1, h, cdim), lambda s, t, pref: (s, t, pref[0], 0)),
                  pl.BlockSpec((1, 1, h, cdim), lambda s, t, pref: (s, t, 0, 0))],
        out_specs=pl.BlockSpec((1, 1, h, cdim), lambda s, t, pref: (s, t, 0, 0)))
    return pl.pallas_call(
        body, name=f"pair_sum_{BIG[i][0]}", grid_spec=grid_spec, out_shape=_sds((4, k, h, cdim), BF16),
        compiler_params=_params(("parallel", "parallel")),
    )(place, g, r1)


def _chip_exchange(ps):
    def body(*refs):
        src, dst = refs[:NBIG], refs[NBIG:2 * NBIG]
        send_sems, recv_sems = refs[2 * NBIG:]
        x, y, c, chips = _place()
        cps = []
        for j, (px, py) in enumerate(chips):
            for i in range(NBIG):
                cps.append(_remote(src[i].at[2 * px + py], dst[i].at[j], send_sems.at[j, i], recv_sems.at[j, i],
                                   (px, py, c)))
                cps[-1].start()
        for cp in cps:
            cp.wait()

    return pl.pallas_call(
        body, name="chip_exchange", in_specs=[ANY] * NBIG, out_specs=[ANY] * NBIG,
        out_shape=[_sds((3, k, r // 2, cdim), BF16) for _, k, r, cdim in BIG],
        scratch_shapes=[pltpu.SemaphoreType.DMA((3, NBIG)), pltpu.SemaphoreType.DMA((3, NBIG))],
    )(*ps)


def _chip_sum(place, g, r1, r2, i):
    _, k, r, cdim = BIG[i]
    h = r // 2

    def body(pref, g_ref, r1_ref, r2_ref, o_ref):
        acc = g_ref[0, 0] + r1_ref[0, 0]
        for j in range(3):
            acc = acc + r2_ref[j, 0].astype(F32)
        o_ref[0] = acc

    grid_spec = pltpu.PrefetchScalarGridSpec(
        num_scalar_prefetch=1, grid=(k,),
        in_specs=[pl.BlockSpec((1, 1, h, cdim), lambda t, pref: (pref[1], t, pref[0], 0)),
                  pl.BlockSpec((1, 1, h, cdim), lambda t, pref: (pref[1], t, 0, 0)),
                  pl.BlockSpec((3, 1, h, cdim), lambda t, pref: (0, t, 0, 0))],
        out_specs=pl.BlockSpec((1, h, cdim), lambda t, pref: (t, pref[0], 0)))
    return pl.pallas_call(
        body, name=f"chip_sum_{BIG[i][0]}", grid_spec=grid_spec, out_shape=_sds((k, r, cdim), F32),
        compiler_params=_params(("parallel",)),
    )(place, g, r1, r2)


def _pair_gather(hs):
    def body(*refs):
        dst = refs[NBIG:2 * NBIG]
        send_sems, recv_sems = refs[2 * NBIG:]
        x, y, c, _ = _place()
        cps = []
        for i in range(NBIG):
            h = BIG[i][2] // 2
            mine = dst[i].at[:, pl.ds(c * h, h), :]
            cps.append(_remote(mine, mine, send_sems.at[i], recv_sems.at[i], (x, y, 1 - c)))
            cps[-1].start()
        for i in range(NBIG):
            h = BIG[i][2] // 2
            theirs = dst[i].at[:, pl.ds((1 - c) * h, h), :]
            _remote(theirs, theirs, send_sems.at[i], recv_sems.at[i], (x, y, 1 - c)).wait_recv()
        for cp in cps:
            cp.wait_send()

    return pl.pallas_call(
        body, name="pair_gather", in_specs=[ANY] * NBIG, out_specs=[ANY] * NBIG,
        out_shape=[_sds((k, r, cdim), F32) for _, k, r, cdim in BIG],
        input_output_aliases={i: i for i in range(NBIG)},
        scratch_shapes=[pltpu.SemaphoreType.DMA((NBIG,)), pltpu.SemaphoreType.DMA((NBIG,))],
    )(*hs)


SMALL_ROWS = 40


def _all_reduce_small(pack):
    def body(p_ref, o_ref, slots, send_sems, recv_sems):
        x, y, c, _ = _place()
        me = 4 * x + 2 * y + c
        cps = []
        for r in range(1, 8):
            peer = (x if not r & 4 else 1 - x, y if not r & 2 else 1 - y, c if not r & 1 else 1 - c)
            cps.append(_remote(p_ref, slots.at[r], send_sems.at[r - 1], recv_sems.at[r - 1], peer))
            cps[-1].start()
        slots[0] = p_ref[...]
        for cp in cps:
            cp.wait()
        acc = slots[me]
        for dev in range(1, 8):
            acc = acc + slots[jnp.bitwise_xor(me, dev)]
        o_ref[...] = acc

    vm = pl.BlockSpec(memory_space=pltpu.VMEM)
    return pl.pallas_call(
        body, name="all_reduce_small", in_specs=[vm], out_specs=vm, out_shape=_sds((SMALL_ROWS, D), F32),
        scratch_shapes=[pltpu.VMEM((8, SMALL_ROWS, D), F32), pltpu.SemaphoreType.DMA((7,)),
                        pltpu.SemaphoreType.DMA((7,))],
    )(pack)


def _adamw_math(w, g, m, v):
    m = ADAM_B1 * m + (1.0 - ADAM_B1) * g
    v = ADAM_B2 * v + (1.0 - ADAM_B2) * (g * g)
    m_hat = m / (1.0 - ADAM_B1 ** ADAM_STEP)
    v_hat = v / (1.0 - ADAM_B2 ** ADAM_STEP)
    delta = -ADAM_LR * (m_hat / (jnp.sqrt(v_hat) + ADAM_EPS) + ADAM_WD * w)
    return delta, m, v


def _adamw_big(w, g, m, v, i):
    _, k, r, cdim = BIG[i]
    nblk = 4 if k == 1 else 1

    def body(w_ref, g_ref, m_ref, v_ref, d_ref, nm_ref, nv_ref):
        d_ref[...], nm_ref[...], nv_ref[...] = _adamw_math(w_ref[...], g_ref[...], m_ref[...], v_ref[...])

    spec = pl.BlockSpec((1, r // nblk, cdim), lambda t, b: (t, b, 0))
    return pl.pallas_call(
        body, name=f"adamw_{BIG[i][0]}", grid=(k, nblk), in_specs=[spec] * 4, out_specs=[spec] * 3,
        out_shape=[_sds((k, r, cdim), F32)] * 3,
        compiler_params=_params(("parallel", "parallel")),
    )(w, g, m, v)


def _adamw_small(ws, gs, ms, vs):
    n = len(ws)

    def body(*refs):
        for i in range(n):
            w_ref, g_ref, m_ref, v_ref = refs[i], refs[n + i], refs[2 * n + i], refs[3 * n + i]
            d, nm, nv = _adamw_math(w_ref[...], g_ref[...], m_ref[...], v_ref[...])
            refs[4 * n + i][...] = d
            refs[5 * n + i][...] = nm
            refs[6 * n + i][...] = nv

    specs = [_full(w.shape) for w in ws]
    outs = pl.pallas_call(
        body, name="adamw_small", grid=(1,), in_specs=specs * 4, out_specs=specs * 3,
        out_shape=[_sds(w.shape, F32) for w in ws] * 3,
        compiler_params=_params(("arbitrary",)),
    )(*ws, *gs, *ms, *vs)
    return outs[:n], outs[n:2 * n], outs[2 * n:]


def _pad_rows(a, rows):
    return jnp.pad(a, ((0, rows - a.shape[0]), (0, 0)))


def kernel(x, mem, positions, norm_g, mem_norm_g, w_mem_kv, attn_w_in, attn_w_out, conv_w_in, conv_w, conv_w_out, final_g, loss_target, m_norm_g, m_mem_norm_g, m_w_mem_kv, m_attn_w_in, m_attn_w_out, m_conv_w_in, m_conv_w, m_conv_w_out, m_final_g, v_norm_g, v_mem_norm_g, v_w_mem_kv, v_attn_w_in, v_attn_w_out, v_conv_w_in, v_conv_w, v_conv_w_out, v_final_g):
    mx, my, mc = lax.axis_index("x"), lax.axis_index("y"), lax.axis_index("c")
    place = jnp.stack([mc, 2 * mx + my]).astype(jnp.int32)

    w_big = [w_mem_kv, attn_w_in, attn_w_out, conv_w_in, conv_w_out]
    m_big = [m_w_mem_kv, m_attn_w_in, m_attn_w_out, m_conv_w_in, m_conv_w_out]
    v_big = [v_w_mem_kv, v_attn_w_in, v_attn_w_out, v_conv_w_in, v_conv_w_out]
    wb = _cast_weights(place, w_big)
    full = _gather_weights(wb, _pad_rows(conv_w[0], CW_ROWS))
    wkv_f, w_in_a_f, w_out_a_f, w_in_b_f, w_out_b_f, cw_f = full
    cw8 = cw_f.transpose(1, 0, 2).reshape(CW_ROWS, D)

    gx, small, big = _local_step(
        x[0], mem[0], positions[0].astype(F32).reshape(S, 1), norm_g, mem_norm_g, final_g, loss_target[0],
        wkv_f, w_in_a_f.reshape(4, D, SH_A), w_out_a_f.reshape(4, BR_A, SH_O), w_in_b_f.reshape(4, D, SH_B), cw8,
        w_out_b_f.reshape(BR_B, D))

    gs = [big["wkv"], big["w_in_a"].reshape(4, 1, D, SH_A), big["w_out_a"].reshape(4, 1, BR_A, SH_O),
          big["w_in_b"].reshape(4, 1, D, SH_B), big["w_out_b"].reshape(4, 1, BR_B // 4, D)]
    r1 = _pair_exchange(gs)
    ps = [_pair_sum(place, gs[i], r1[i], i) for i in range(NBIG)]
    r2 = _chip_exchange(ps)
    hs = [_chip_sum(place, gs[i], r1[i], r2[i], i) for i in range(NBIG)]
    g_big = _pair_gather(hs)

    pack = jnp.concatenate([_pad_rows(small["dnorm"], 8), _pad_rows(small["dmemnorm"], 8), _pad_rows(small["dfinal"], 8),
                            small["dconv"], _pad_rows(jnp.pad(small["loss"], ((0, 0), (0, D - 128))), 8)], axis=0)
    tot = _all_reduce_small(pack)
    loss = tot[32, 0]
    g_norm, g_memnorm, g_final = tot[0:2], tot[8:10], tot[16]
    g_conv = lax.dynamic_slice(tot, (24, (2 * mx + my) * SH_O), (3, SH_O))

    upd = [_adamw_big(w_big[i], g_big[i], m_big[i], v_big[i], i) for i in range(NBIG)]
    sw = [norm_g, mem_norm_g, final_g.reshape(1, D), conv_w[0]]
    sg = [g_norm, g_memnorm, g_final.reshape(1, D), g_conv]
    sm = [m_norm_g, m_mem_norm_g, m_final_g.reshape(1, D), m_conv_w[0]]
    sv = [v_norm_g, v_mem_norm_g, v_final_g.reshape(1, D), v_conv_w[0]]
    sd, snm, snv = _adamw_small(sw, sg, sm, sv)

    def order(norm, memnorm, wkv, w_in_a, w_out_a, w_in_b, conv, w_out_b, final):
        return (norm, memnorm, wkv, w_in_a, w_out_a, w_in_b, conv.reshape(1, 3, SH_O), w_out_b, final.reshape(D))

    grads = order(g_norm, g_memnorm, g_big[0], g_big[1], g_big[2], g_big[3], g_conv, g_big[4], g_final)
    deltas = order(sd[0], sd[1], upd[0][0], upd[1][0], upd[2][0], upd[3][0], sd[3], upd[4][0], sd[2])
    new_m = order(snm[0], snm[1], upd[0][1], upd[1][1], upd[2][1], upd[3][1], snm[3], upd[4][1], snm[2])
    new_v = order(snv[0], snv[1], upd[0][2], upd[1][2], upd[2][2], upd[3][2], snv[3], upd[4][2], snv[2])
    return (loss, gx[None], *grads, *deltas, *new_m, *new_v)
```

```python
import functools

import numpy as np
import jax
import jax.numpy as jnp
from jax import lax
from jax.experimental import pallas as pl
from jax.experimental.pallas import tpu as pltpu

F32 = jnp.float32
BF16 = jnp.bfloat16

S = 2048
D = 1024
TM = 256
NT = S // TM
HD = 64
GW = 512
NQ = 3 * GW
MW = 256
NM = 256
IN_A = 3 * NQ + MW + GW + MW
IN_B = 3 * D + MW + D + MW
BR_A = GW + MW
BR_B = D + MW
SH_A = IN_A // 4
SH_B = IN_B // 4
SH_O = D // 4
QBLK = 128
DILATIONS = (1, 4, 16)
EPS = 1e-6
SCALE = HD ** -0.5
NEG = -1e30
ROPE_THETA = 500000.0

ADAM_LR = 0.001
ADAM_B1 = 0.9
ADAM_B2 = 0.999
ADAM_EPS = 1e-08
ADAM_WD = 0.01
ADAM_STEP = 10

VMEM_LIMIT_BYTES = 60 * 1024 * 1024


def _params(sem=None):
    if sem is None:
        return pltpu.CompilerParams(vmem_limit_bytes=VMEM_LIMIT_BYTES)
    return pltpu.CompilerParams(dimension_semantics=sem, vmem_limit_bytes=VMEM_LIMIT_BYTES)


def _full(shape):
    nd = len(shape)
    return pl.BlockSpec(shape, lambda *_: (0,) * nd)


def _rows(width, tm=TM):
    return pl.BlockSpec((tm, width), lambda i: (i, 0))


def _sds(shape, dtype):
    return jax.ShapeDtypeStruct(shape, dtype)


def _silu_parts(z):
    sig = 1.0 / (1.0 + jnp.exp(-z))
    return z * sig, sig * (1.0 + z * (1.0 - sig))


def _dot(a, b):
    return jnp.dot(a, b, preferred_element_type=F32)


def _dot_nt(a, b):
    return lax.dot_general(a, b, (((1,), (1,)), ((), ())), preferred_element_type=F32)


def _dot_tn(a, b):
    return lax.dot_general(a, b, (((0,), (0,)), ((), ())), preferred_element_type=F32)


def _rope_fwd(t, c, s1, s2):
    return t * c + pltpu.roll(t, 120, 1) * s1 + pltpu.roll(t, 8, 1) * s2


def _rope_bwd(g, c, s1, s2):
    return g * c + pltpu.roll(g * s1, 8, 1) + pltpu.roll(g * s2, 120, 1)


def _mem_attn(qm, kv):
    res = []
    for h in range(MW // HD):
        sl = slice(h * HD, (h + 1) * HD)
        s = _dot_nt(qm[:, sl], kv[:, sl]) * SCALE
        e = jnp.exp(s - jnp.max(s, axis=-1, keepdims=True))
        p = e / jnp.sum(e, axis=-1, keepdims=True)
        res.append((p, _dot(p.astype(BF16), kv[:, MW + h * HD:MW + (h + 1) * HD])))
    return res


def _mem_attn_bwd(dmo, heads, qm, kv, dqm_store, dkv_ref):
    for h, (p, mo) in enumerate(heads):
        sl = slice(h * HD, (h + 1) * HD)
        vs = slice(MW + h * HD, MW + (h + 1) * HD)
        dmo_h = dmo[:, sl]
        dmo_b = dmo_h.astype(BF16)
        dp = _dot_nt(dmo_b, kv[:, vs])
        delta = jnp.sum(dmo_h * mo, axis=-1, keepdims=True)
        ds = (p * (dp - delta) * SCALE).astype(BF16)
        dqm_store(h, _dot(ds, kv[:, sl]))
        dkv_ref[:, sl] += _dot_tn(ds, qm[:, sl])
        dkv_ref[:, vs] += _dot_tn(p.astype(BF16), dmo_b)


def _merge(o_refs, l_refs):
    ls = [r[...] for r in l_refs]
    m = jnp.maximum(jnp.maximum(ls[0], ls[1]), ls[2])
    es = [jnp.exp(l - m) for l in ls]
    inv = 1.0 / (es[0] + es[1] + es[2])
    ws = [e * inv for e in es]
    os_ = [r[...] for r in o_refs]
    mix = ws[0] * os_[0] + ws[1] * os_[1] + ws[2] * os_[2]
    return ws, mix


def _conv_taps(cg, u, cgp, up, first):
    a = cg * u
    ap = jnp.where(first, 0.0, cgp * up)
    row = lax.broadcasted_iota(jnp.int32, a.shape, 0)
    a1 = jnp.where(row == 0, ap[7:8, :], pltpu.roll(a, 1, 0))
    a2 = jnp.where(row == 0, ap[6:7, :], jnp.where(row == 1, ap[7:8, :], pltpu.roll(a, 2, 0)))
    return a, a1, a2


def _rope_tables(posf):
    half = 8
    invf = np.float32(ROPE_THETA) ** (-np.arange(half, dtype=np.float32) * np.float32(2.0 / 16))
    lane = np.arange(128)
    table = np.where((lane % HD) < 16, invf[lane % half], 0.0).astype(np.float32)[None, :]

    def body(pos_ref, invf_ref, c_ref, s1_ref, s2_ref):
        ang = pos_ref[...] * invf_ref[...]
        jm = lax.broadcasted_iota(jnp.int32, ang.shape, 1) & (HD - 1)
        cs = jnp.cos(ang)
        sn = jnp.sin(ang)
        c_ref[...] = jnp.where(jm < 16, cs, 1.0)
        s1_ref[...] = jnp.where(jm < 8, -sn, 0.0)
        s2_ref[...] = jnp.where((jm >= 8) & (jm < 16), sn, 0.0)

    out = _sds((S, 128), F32)
    return pl.pallas_call(
        body, name="rope_tables", grid=(NT,),
        in_specs=[_rows(1), _full((1, 128))],
        out_specs=[_rows(128)] * 3, out_shape=[out] * 3,
        compiler_params=_params(("parallel",)),
    )(posf, jnp.asarray(table))


def _in_proj_a(x, g0, w_in, c, s1, s2, after):
    def body(x_ref, g_ref, w_ref, c_ref, s1_ref, s2_ref, hn_ref, q_ref, k_ref, v_ref, qm_ref, z_ref, proj):
        xf = x_ref[...]
        hn = xf * lax.rsqrt(jnp.mean(xf * xf, axis=-1, keepdims=True) + EPS) * g_ref[...]
        hb = hn.astype(BF16)
        hn_ref[...] = hb
        for s in range(4):
            proj[:, s * SH_A:(s + 1) * SH_A] = _dot(hb, w_ref[s])
        cc, a1, a2 = c_ref[...], s1_ref[...], s2_ref[...]
        for j in range(NQ // 128):
            q_ref[:, j * 128:(j + 1) * 128] = (
                _rope_fwd(proj[:, j * 128:(j + 1) * 128], cc, a1, a2) * SCALE).astype(BF16)
            k_ref[:, j * 128:(j + 1) * 128] = _rope_fwd(
                proj[:, NQ + j * 128:NQ + (j + 1) * 128], cc, a1, a2).astype(BF16)
        v_ref[...] = proj[:, 2 * NQ:3 * NQ].astype(BF16)
        qm_ref[...] = proj[:, 3 * NQ:3 * NQ + MW].astype(BF16)
        z_ref[...] = proj[:, 3 * NQ + MW:]

    return pl.pallas_call(
        functools.partial(_skip_arg, body, 6), name="in_proj_a", grid=(NT,),
        in_specs=[_rows(D), _full((1, D)), _full((4, D, SH_A)), _rows(128), _rows(128), _rows(128),
                  pl.BlockSpec(memory_space=pl.ANY)],
        out_specs=[_rows(D), _rows(NQ), _rows(NQ), _rows(NQ), _rows(MW), _rows(BR_A)],
        out_shape=[_sds((S, D), BF16), _sds((S, NQ), BF16), _sds((S, NQ), BF16), _sds((S, NQ), BF16),
                   _sds((S, MW), BF16), _sds((S, BR_A), F32)],
        scratch_shapes=[pltpu.VMEM((TM, IN_A), F32)],
        compiler_params=_params(("parallel",)),
    )(x, g0, w_in, c, s1, s2, after)


def _mem_fwd(mem, mg, wkv):
    def body(mem_ref, mg_ref, w_ref, memn_ref, kv_ref):
        mf = mem_ref[...]
        n = mf * lax.rsqrt(jnp.mean(mf * mf, axis=-1, keepdims=True) + EPS)
        for i in range(2):
            mn = (n * mg_ref[i:i + 1, :]).astype(BF16)
            memn_ref[i] = mn
            acc = _dot(mn[:, 0:NM], w_ref[0, i])
            for s in range(1, 4):
                acc += _dot(mn[:, s * NM:(s + 1) * NM], w_ref[s, i])
            kv_ref[i] = acc.astype(BF16)

    return pl.pallas_call(
        body, name="mem_fwd", grid=(1,),
        in_specs=[_full((NM, D)), _full((2, D)), _full((4, 2, NM, 2 * MW))],
        out_specs=[_full((2, NM, D)), _full((2, NM, 2 * MW))],
        out_shape=[_sds((2, NM, D), BF16), _sds((2, NM, 2 * MW), BF16)],
        compiler_params=_params(("arbitrary",)),
    )(mem, mg, wkv)


def _band_mask(j):
    qi = lax.broadcasted_iota(jnp.int32, (QBLK, 2 * QBLK), 0)
    kj = lax.broadcasted_iota(jnp.int32, (QBLK, 2 * QBLK), 1)
    dist = qi + QBLK - kj
    return (dist >= 0) & (dist <= QBLK) & ((kj >= QBLK) | (j > 0))


LANES = 128
NCHUNK = GW // LANES
FWD_UNROLL = 16
BWD_UNROLL = 4


def _perm_matrix(d):
    n = TM // d
    p = np.zeros((TM, TM), np.float32)
    for r in range(d):
        for i in range(n):
            p[r * n + i, i * d + r] = 1.0
    return p


def _split_dot(p, x, parts):
    acc = None
    for _ in range(parts):
        hi = x.astype(BF16)
        term = _dot(p, hi)
        acc = term if acc is None else acc + term
        x = x - hi.astype(F32)
    return acc


def _tile_to_streams(y, dst, t, d):
    n, ln = TM // d, S // d
    for r in range(d):
        dst[r * ln + t * n:r * ln + (t + 1) * n, :] = y[r * n:(r + 1) * n].astype(dst.dtype)


def _tile_from_streams(src, t, d):
    n, ln = TM // d, S // d
    return jnp.concatenate([src[r * ln + t * n:r * ln + (t + 1) * n, :] for r in range(d)], axis=0)


def _head_masks():
    first = lax.broadcasted_iota(jnp.int32, (TM, LANES), 1) < HD
    return first, jnp.logical_not(first)


def _attn_fwd(q, k, v, g):
    d = DILATIONS[g]
    nb = S // d // QBLK
    perm = _perm_matrix(d)

    def body(q_ref, k_ref, v_ref, p_ref, pt_ref, o_ref, l_ref, ls_ref, q0, q1, ks, vs, os_):
        first, second = _head_masks()
        pm = p_ref[...]
        for t in range(NT):
            rows = slice(t * TM, (t + 1) * TM)
            if d == 1:
                qt = q_ref[rows, :].astype(F32)
            else:
                qt = _dot(pm, q_ref[rows, :])
                _tile_to_streams(_dot(pm, k_ref[rows, :]), ks, t, d)
                _tile_to_streams(_dot(pm, v_ref[rows, :]), vs, t, d)
            _tile_to_streams(jnp.where(first, qt, 0.0), q0, t, d)
            _tile_to_streams(jnp.where(second, qt, 0.0), q1, t, d)
        kref, vref = (k_ref, v_ref) if d == 1 else (ks, vs)
        oref, lref = (o_ref, l_ref) if d == 1 else (os_, ls_ref)

        def blk(b, carry):
            r0 = pl.multiple_of(b * QBLK, QBLK)
            p0 = pl.multiple_of(jnp.maximum(b - 1, 0) * QBLK, QBLK)
            kk = jnp.concatenate([kref[pl.ds(p0, QBLK), :], kref[pl.ds(r0, QBLK), :]], axis=0)
            vv = jnp.concatenate([vref[pl.ds(p0, QBLK), :], vref[pl.ds(r0, QBLK), :]], axis=0)
            valid = _band_mask(b & (nb - 1))
            acc, den, lse = [], [], []
            for qh in (q0, q1):
                s = jnp.where(valid, _dot_nt(qh[pl.ds(r0, QBLK), :], kk), NEG)
                m = jnp.max(s, axis=-1, keepdims=True)
                e = jnp.exp(s - m)
                l = jnp.sum(e, axis=-1, keepdims=True)
                acc.append(_dot(e.astype(BF16), vv))
                den.append(l)
                lse.append(m + jnp.log(l))
            f = first[:QBLK]
            oref[pl.ds(r0, QBLK), :] = jnp.where(f, acc[0], acc[1]) / jnp.where(f, den[0], den[1])
            lref[pl.ds(r0, QBLK), :] = jnp.where(f, lse[0], lse[1])
            return carry

        lax.fori_loop(0, S // QBLK, blk, 0, unroll=FWD_UNROLL)
        if d > 1:
            ptm = pt_ref[...]
            for t in range(NT):
                rows = slice(t * TM, (t + 1) * TM)
                o_ref[rows, :] = _split_dot(ptm, _tile_from_streams(os_, t, d), 2)
                l_ref[rows, :] = _split_dot(ptm, _tile_from_streams(ls_ref, t, d), 3)

    qkv_spec = pl.BlockSpec((S, LANES), lambda c: (0, g * NCHUNK + c))
    out_spec = pl.BlockSpec((S, LANES), lambda c: (0, c))
    n_out = 2 if d == 1 else 3
    outs = pl.pallas_call(
        body if d > 1 else functools.partial(_drop_arg, body, 7), name=f"attn_fwd_g{g}", grid=(NCHUNK,),
        in_specs=[qkv_spec] * 3 + [_full((TM, TM))] * 2, out_specs=[out_spec] * n_out,
        out_shape=[_sds((S, GW), F32)] * n_out,
        scratch_shapes=[pltpu.VMEM((S, LANES), BF16)] * 4 + [pltpu.VMEM((S, LANES), F32)],
        compiler_params=_params(("parallel",)),
    )(q, k, v, jnp.asarray(perm, BF16), jnp.asarray(perm.T, BF16))
    return (outs[0], outs[1], outs[1]) if d == 1 else tuple(outs)


def _drop_arg(body, pos, *refs):
    return body(*refs[:pos], None, *refs[pos:])


def _attn_out(os_, ls, qm, kv0, z, x, w_out):
    def body(o0, o1, o2, l0, l1, l2, qm_ref, kv_ref, z_ref, x_ref, w_ref, h_ref, ybuf):
        _, mix = _merge((o0, o1, o2), (l0, l1, l2))
        sz, _ = _silu_parts(z_ref[...])
        ybuf[:, :GW] = (mix * sz[:, :GW]).astype(BF16)
        for h, (_, mo) in enumerate(_mem_attn(qm_ref[...], kv_ref[...])):
            sl = slice(GW + h * HD, GW + (h + 1) * HD)
            ybuf[:, sl] = (mo * sz[:, sl]).astype(BF16)
        yb = ybuf[...]
        for s in range(4):
            cs = slice(s * SH_O, (s + 1) * SH_O)
            h_ref[:, cs] = x_ref[:, cs] + _dot(yb, w_ref[s])

    return pl.pallas_call(
        body, name="attn_out", grid=(NT,),
        in_specs=[_rows(GW)] * 6 + [_rows(MW), _full((NM, 2 * MW)), _rows(BR_A), _rows(D), _full((4, BR_A, SH_O))],
        out_specs=_rows(D), out_shape=_sds((S, D), F32),
        scratch_shapes=[pltpu.VMEM((TM, BR_A), BF16)],
        compiler_params=_params(("parallel",)),
    )(*os_, *ls, qm, kv0, z, x, w_out)


def _in_proj_b(h1, g1, w_in):
    def body(x_ref, g_ref, w_ref, hn_ref, bg_ref, cg_ref, u_ref, qm_ref, z_ref, proj):
        xf = x_ref[...]
        hn = xf * lax.rsqrt(jnp.mean(xf * xf, axis=-1, keepdims=True) + EPS) * g_ref[...]
        hb = hn.astype(BF16)
        hn_ref[...] = hb
        for s in range(4):
            proj[:, s * SH_B:(s + 1) * SH_B] = _dot(hb, w_ref[s])
        bg_ref[...] = proj[:, :D]
        cg_ref[...] = proj[:, D:2 * D]
        u_ref[...] = proj[:, 2 * D:3 * D]
        qm_ref[...] = proj[:, 3 * D:3 * D + MW].astype(BF16)
        z_ref[...] = proj[:, 3 * D + MW:]

    return pl.pallas_call(
        body, name="in_proj_b", grid=(NT,),
        in_specs=[_rows(D), _full((1, D)), _full((4, D, SH_B))],
        out_specs=[_rows(D), _rows(D), _rows(D), _rows(D), _rows(MW), _rows(BR_B)],
        out_shape=[_sds((S, D), BF16), _sds((S, D), F32), _sds((S, D), F32), _sds((S, D), F32),
                   _sds((S, MW), BF16), _sds((S, BR_B), F32)],
        scratch_shapes=[pltpu.VMEM((TM, IN_B), F32)],
        compiler_params=_params(("parallel",)),
    )(h1, g1, w_in)


def _prev8(width):
    return pl.BlockSpec((8, width), lambda i: (jnp.maximum(i * (TM // 8) - 1, 0), 0))


def _conv_out_loss(bg, cg, u, cw, qm, kv1, z, h1, w_out, fg, tgt):
    def body(bg_ref, cg_ref, u_ref, cgp_ref, up_ref, cw_ref, qm_ref, kv_ref, z_ref, h_ref, w_ref, fg_ref, t_ref,
             dh_ref, loss_ref, dfg_ref, ybuf):
        i = pl.program_id(0)
        a, a1, a2 = _conv_taps(cg_ref[...], u_ref[...], cgp_ref[...], up_ref[...], i == 0)
        conv = cw_ref[0:1, :] * a2 + cw_ref[1:2, :] * a1 + cw_ref[2:3, :] * a
        sz, _ = _silu_parts(z_ref[...])
        ybuf[:, :D] = (bg_ref[...] * conv * sz[:, :D]).astype(BF16)
        for h, (_, mo) in enumerate(_mem_attn(qm_ref[...], kv_ref[...])):
            sl = slice(D + h * HD, D + (h + 1) * HD)
            ybuf[:, sl] = (mo * sz[:, sl]).astype(BF16)
        h2 = h_ref[...] + _dot(ybuf[...], w_ref[...])
        rstd = lax.rsqrt(jnp.mean(h2 * h2, axis=-1, keepdims=True) + EPS)
        n = h2 * rstd
        fgv = fg_ref[...]
        err = n * fgv - t_ref[...]
        dout = err * (1.0 / D)
        dn = dout * fgv
        dh_ref[...] = rstd * (dn - n * jnp.mean(dn * n, axis=-1, keepdims=True))

        @pl.when(i == 0)
        def _():
            loss_ref[...] = jnp.zeros_like(loss_ref)
            dfg_ref[...] = jnp.zeros_like(dfg_ref)

        loss_ref[...] += jnp.sum(err * err) * (0.5 / D)
        dfg_ref[...] += jnp.sum(dout * n, axis=0, keepdims=True)

    return pl.pallas_call(
        body, name="conv_out_loss", grid=(NT,),
        in_specs=[_rows(D), _rows(D), _rows(D), _prev8(D), _prev8(D), _full((8, D)), _rows(MW),
                  _full((NM, 2 * MW)), _rows(BR_B), _rows(D), _full((BR_B, D)), _full((1, D)), _rows(D)],
        out_specs=[_rows(D), _full((1, 128)), _full((1, D))],
        out_shape=[_sds((S, D), F32), _sds((1, 128), F32), _sds((1, D), F32)],
        scratch_shapes=[pltpu.VMEM((TM, BR_B), BF16)],
        compiler_params=_params(("arbitrary",)),
    )(bg, cg, u, cg, u, cw, qm, kv1, z, h1, w_out, fg, tgt)


def _conv_bwd(dh2, bg, cg, u, cw, qm, kv1, z, w_out):
    rev = lambda i: (NT - 1 - i, 0)
    rows = lambda w: pl.BlockSpec((TM, w), rev)
    prev8 = pl.BlockSpec((8, D), lambda i: (jnp.maximum((NT - 1 - i) * (TM // 8) - 1, 0), 0))

    def body(dh_ref, bg_ref, cg_ref, u_ref, cgp_ref, up_ref, cw_ref, qm_ref, kv_ref, z_ref, w_ref,
             dproj_ref, dw_ref, dcw_ref, dkv_ref, ybuf, carry):
        i = pl.program_id(0)

        @pl.when(i == 0)
        def _():
            dw_ref[...] = jnp.zeros_like(dw_ref)
            dcw_ref[...] = jnp.zeros_like(dcw_ref)
            dkv_ref[...] = jnp.zeros_like(dkv_ref)
            carry[...] = jnp.zeros_like(carry)

        bgv, cgv, uv = bg_ref[...], cg_ref[...], u_ref[...]
        a, a1, a2 = _conv_taps(cgv, uv, cgp_ref[...], up_ref[...], i == NT - 1)
        w0, w1, w2 = cw_ref[0:1, :], cw_ref[1:2, :], cw_ref[2:3, :]
        conv = w0 * a2 + w1 * a1 + w2 * a
        mix = bgv * conv
        zv = z_ref[...]
        sz, dsz = _silu_parts(zv)
        qmv, kvv = qm_ref[...], kv_ref[...]
        heads = _mem_attn(qmv, kvv)
        ybuf[:, :D] = (mix * sz[:, :D]).astype(BF16)
        for h, (_, mo) in enumerate(heads):
            sl = slice(D + h * HD, D + (h + 1) * HD)
            ybuf[:, sl] = (mo * sz[:, sl]).astype(BF16)
        dhb = dh_ref[...].astype(BF16)
        dw_ref[...] += _dot_tn(ybuf[...], dhb)
        dy = _dot_nt(dhb, w_ref[...])
        dcat = dy * sz
        dproj_ref[:, 3 * D + MW:3 * D + MW + D] = (dy[:, :D] * mix * dsz[:, :D]).astype(BF16)
        for h, (_, mo) in enumerate(heads):
            sl = slice(D + h * HD, D + (h + 1) * HD)
            dproj_ref[:, 3 * D + MW + D + h * HD:3 * D + MW + D + (h + 1) * HD] = (
                dy[:, sl] * mo * dsz[:, sl]).astype(BF16)
        dmix = dcat[:, :D]
        dproj_ref[:, :D] = (dmix * conv).astype(BF16)
        dc = dmix * bgv
        nxt = carry[...]
        row = lax.broadcasted_iota(jnp.int32, dc.shape, 0)
        dc1 = jnp.where(row == TM - 1, nxt[0:1, :], pltpu.roll(dc, TM - 1, 0))
        dc2 = jnp.where(row == TM - 2, nxt[0:1, :], jnp.where(row == TM - 1, nxt[1:2, :], pltpu.roll(dc, TM - 2, 0)))
        carry[...] = dc[0:8, :]
        da = w2 * dc + w1 * dc1 + w0 * dc2
        dproj_ref[:, D:2 * D] = (da * uv).astype(BF16)
        dproj_ref[:, 2 * D:3 * D] = (da * cgv).astype(BF16)
        dcw_ref[0:1, :] += jnp.sum(dc * a2, axis=0, keepdims=True)
        dcw_ref[1:2, :] += jnp.sum(dc * a1, axis=0, keepdims=True)
        dcw_ref[2:3, :] += jnp.sum(dc * a, axis=0, keepdims=True)

        def dqm_store(h, val):
            dproj_ref[:, 3 * D + h * HD:3 * D + (h + 1) * HD] = val.astype(BF16)

        _mem_attn_bwd(dcat[:, D:], heads, qmv, kvv, dqm_store, dkv_ref)

    return pl.pallas_call(
        body, name="conv_bwd", grid=(NT,),
        in_specs=[rows(D), rows(D), rows(D), rows(D), prev8, prev8, _full((8, D)), rows(MW),
                  _full((NM, 2 * MW)), rows(BR_B), _full((BR_B, D))],
        out_specs=[rows(IN_B), _full((BR_B, D)), _full((8, D)), _full((NM, 2 * MW))],
        out_shape=[_sds((S, IN_B), BF16), _sds((BR_B, D), F32), _sds((8, D), F32), _sds((NM, 2 * MW), F32)],
        scratch_shapes=[pltpu.VMEM((TM, BR_B), BF16), pltpu.VMEM((8, D), F32)],
        compiler_params=_params(("arbitrary",)),
    )(dh2, bg, cg, u, cg, u, cw, qm, kv1, z, w_out)


def _in_proj_bwd(dproj, w_in, xin, g, dres, after, width, name):
    sh = width // 4

    def body(dp_ref, w_ref, x_ref, g_ref, dr_ref, dx_ref, dg_ref):
        i = pl.program_id(0)
        dhn = _dot_nt(dp_ref[:, 0:sh], w_ref[0])
        for s in range(1, 4):
            dhn += _dot_nt(dp_ref[:, s * sh:(s + 1) * sh], w_ref[s])
        xf = x_ref[...]
        rstd = lax.rsqrt(jnp.mean(xf * xf, axis=-1, keepdims=True) + EPS)
        n = xf * rstd
        dn = dhn * g_ref[...]
        dx_ref[...] = dr_ref[...] + rstd * (dn - n * jnp.mean(dn * n, axis=-1, keepdims=True))

        @pl.when(i == 0)
        def _():
            dg_ref[...] = jnp.zeros_like(dg_ref)

        dg_ref[...] += jnp.sum(dhn * n, axis=0, keepdims=True)

    return pl.pallas_call(
        functools.partial(_skip_arg, body, 5), name=name, grid=(NT,),
        in_specs=[_rows(width), _full((4, D, sh)), _rows(D), _full((1, D)), _rows(D), pl.BlockSpec(memory_space=pl.ANY)],
        out_specs=[_rows(D), _full((1, D))],
        out_shape=[_sds((S, D), F32), _sds((1, D), F32)],
        compiler_params=_params(("arbitrary",)),
    )(dproj, w_in, xin, g, dres, after)


def _w_in_grad(hn, dproj, width, name):
    sh = width // 4

    def body(hn_ref, dp_ref, dw_ref):
        dw_ref[0] = _dot_tn(hn_ref[...], dp_ref[...])

    return pl.pallas_call(
        body, name=name, grid=(4,),
        in_specs=[_full((S, D)), pl.BlockSpec((S, sh), lambda s: (0, s))],
        out_specs=pl.BlockSpec((1, D, sh), lambda s: (s, 0, 0)),
        out_shape=_sds((4, D, sh), F32),
        compiler_params=_params(("parallel",)),
    )(hn, dproj)


def _attn_out_bwd(dh1, os_, ls, qm, kv0, z, w_out):
    ones_bd = np.kron(np.eye(GW // HD, dtype=np.float32), np.ones((HD, HD), np.float32))

    def body(dh_ref, o0, o1, o2, l0, l1, l2, qm_ref, kv_ref, z_ref, w_ref, bd_ref,
             do0, do1, do2, dd0, dd1, dd2, dqm_ref, dz_ref, dw_ref, dkv_ref, ybuf):
        i = pl.program_id(0)

        @pl.when(i == 0)
        def _():
            dw_ref[...] = jnp.zeros_like(dw_ref)
            dkv_ref[...] = jnp.zeros_like(dkv_ref)

        ws, mix = _merge((o0, o1, o2), (l0, l1, l2))
        sz, dsz = _silu_parts(z_ref[...])
        qmv, kvv = qm_ref[...], kv_ref[...]
        heads = _mem_attn(qmv, kvv)
        ybuf[:, :GW] = (mix * sz[:, :GW]).astype(BF16)
        for h, (_, mo) in enumerate(heads):
            sl = slice(GW + h * HD, GW + (h + 1) * HD)
            ybuf[:, sl] = (mo * sz[:, sl]).astype(BF16)
        yb = ybuf[...]
        dh = dh_ref[...]
        dy = None
        for s in range(4):
            dhb = dh[:, s * SH_O:(s + 1) * SH_O].astype(BF16)
            dw_ref[s] += _dot_tn(yb, dhb)
            part = _dot_nt(dhb, w_ref[s])
            dy = part if dy is None else dy + part
        dcat = dy * sz
        dz_ref[:, :GW] = (dy[:, :GW] * mix * dsz[:, :GW]).astype(BF16)
        for h, (_, mo) in enumerate(heads):
            sl = slice(GW + h * HD, GW + (h + 1) * HD)
            dz_ref[:, sl] = (dy[:, sl] * mo * dsz[:, sl]).astype(BF16)
        dmix = dcat[:, :GW]
        prod = dmix * mix
        hi = prod.astype(BF16)
        lo = (prod - hi.astype(F32)).astype(BF16)
        bd = bd_ref[...]
        tot = _dot(hi, bd) + _dot(lo, bd)
        for w, do_ref, dd_ref in zip(ws, (do0, do1, do2), (dd0, dd1, dd2)):
            do_ref[...] = (w * dmix).astype(BF16)
            dd_ref[...] = w * tot

        def dqm_store(h, val):
            dqm_ref[:, h * HD:(h + 1) * HD] = val.astype(BF16)

        _mem_attn_bwd(dcat[:, GW:], heads, qmv, kvv, dqm_store, dkv_ref)

    return pl.pallas_call(
        body, name="attn_out_bwd", grid=(NT,),
        in_specs=[_rows(D)] + [_rows(GW)] * 6 + [_rows(MW), _full((NM, 2 * MW)), _rows(BR_A),
                                                   _full((4, BR_A, SH_O)), _full((GW, GW))],
        out_specs=[_rows(GW)] * 6 + [_rows(MW), _rows(BR_A), _full((4, BR_A, SH_O)), _full((NM, 2 * MW))],
        out_shape=[_sds((S, GW), BF16)] * 3 + [_sds((S, GW), F32)] * 3 + [
            _sds((S, MW), BF16), _sds((S, BR_A), BF16), _sds((4, BR_A, SH_O), F32), _sds((NM, 2 * MW), F32)],
        scratch_shapes=[pltpu.VMEM((TM, BR_A), BF16)],
        compiler_params=_params(("arbitrary",)),
    )(dh1, *os_, *ls, qm, kv0, z, w_out, jnp.asarray(ones_bd, dtype=BF16))


def _attn_bwd(q, k, v, do, lse_s, dd, g):
    d = DILATIONS[g]
    nb = S // d // QBLK
    perm = _perm_matrix(d)

    def body(q_ref, k_ref, v_ref, do_ref, l_ref, dd_ref, p_ref, pt_ref, dq_ref, dk_ref, dv_ref,
             q0, q1, g0, g1, ks, vs, dds, dqs, dks, dvs):
        first, second = _head_masks()
        pm = p_ref[...]
        for t in range(NT):
            rows = slice(t * TM, (t + 1) * TM)
            if d == 1:
                qt = q_ref[rows, :].astype(F32)
                gt = do_ref[rows, :].astype(F32)
            else:
                qt = _dot(pm, q_ref[rows, :])
                gt = _dot(pm, do_ref[rows, :])
                _tile_to_streams(_dot(pm, k_ref[rows, :]), ks, t, d)
                _tile_to_streams(_dot(pm, v_ref[rows, :]), vs, t, d)
                _tile_to_streams(_split_dot(pm, dd_ref[rows, :], 2), dds, t, d)
            _tile_to_streams(jnp.where(first, qt, 0.0), q0, t, d)
            _tile_to_streams(jnp.where(second, qt, 0.0), q1, t, d)
            _tile_to_streams(jnp.where(first, gt, 0.0), g0, t, d)
            _tile_to_streams(jnp.where(second, gt, 0.0), g1, t, d)
        kref, vref, ddref = (k_ref, v_ref, dd_ref) if d == 1 else (ks, vs, dds)
        dqref, dkref, dvref = (dq_ref, dk_ref, dv_ref) if d == 1 else (dqs, dks, dvs)
        dkref[...] = jnp.zeros_like(dkref)
        dvref[...] = jnp.zeros_like(dvref)

        def blk(b, carry):
            r0 = pl.multiple_of(b * QBLK, QBLK)
            p0 = pl.multiple_of(jnp.maximum(b - 1, 0) * QBLK, QBLK)
            kk = jnp.concatenate([kref[pl.ds(p0, QBLK), :], kref[pl.ds(r0, QBLK), :]], axis=0)
            vv = jnp.concatenate([vref[pl.ds(p0, QBLK), :], vref[pl.ds(r0, QBLK), :]], axis=0)
            lb = l_ref[pl.ds(r0, QBLK), :]
            ddb = ddref[pl.ds(r0, QBLK), :]
            valid = _band_mask(b & (nb - 1))
            dqh, dkk, dvv = [], None, None
            for h, (qh, gh) in enumerate(((q0, g0), (q1, g1))):
                qb = qh[pl.ds(r0, QBLK), :]
                gb = gh[pl.ds(r0, QBLK), :]
                s = _dot_nt(qb, kk)
                p = jnp.where(valid, jnp.exp(s - lb[:, h * HD:h * HD + 1]), 0.0)
                dp = _dot_nt(gb, vv)
                ds = (p * (dp - ddb[:, h * HD:h * HD + 1])).astype(BF16)
                dqh.append(_dot(ds, kk))
                tk = _dot_tn(ds, qb)
                tv = _dot_tn(p.astype(BF16), gb)
                dkk = tk if dkk is None else dkk + tk
                dvv = tv if dvv is None else dvv + tv
            dqref[pl.ds(r0, QBLK), :] = jnp.where(first[:QBLK], dqh[0], dqh[1])
            dkref[pl.ds(p0, QBLK), :] += dkk[:QBLK]
            dkref[pl.ds(r0, QBLK), :] += dkk[QBLK:]
            dvref[pl.ds(p0, QBLK), :] += dvv[:QBLK]
            dvref[pl.ds(r0, QBLK), :] += dvv[QBLK:]
            return carry

        lax.fori_loop(0, S // QBLK, blk, 0, unroll=BWD_UNROLL)
        if d > 1:
            ptm = pt_ref[...]
            for t in range(NT):
                rows = slice(t * TM, (t + 1) * TM)
                dq_ref[rows, :] = _split_dot(ptm, _tile_from_streams(dqs, t, d), 2)
                dk_ref[rows, :] = _split_dot(ptm, _tile_from_streams(dks, t, d), 2)
                dv_ref[rows, :] = _split_dot(ptm, _tile_from_streams(dvs, t, d), 2)

    qkv_spec = pl.BlockSpec((S, LANES), lambda c: (0, g * NCHUNK + c))
    one_spec = pl.BlockSpec((S, LANES), lambda c: (0, c))
    return pl.pallas_call(
        body, name=f"attn_bwd_g{g}", grid=(NCHUNK,),
        in_specs=[qkv_spec] * 3 + [one_spec] * 3 + [_full((TM, TM))] * 2, out_specs=[one_spec] * 3,
        out_shape=[_sds((S, GW), F32)] * 3,
        scratch_shapes=[pltpu.VMEM((S, LANES), BF16)] * 6 + [pltpu.VMEM((S, LANES), F32)] * 4,
        compiler_params=_params(("parallel",)),
    )(q, k, v, do, lse_s, dd, jnp.asarray(perm, BF16), jnp.asarray(perm.T, BF16))


def _qkv_bwd(dqs, dks, dvs, dqm, dz, c, s1, s2):
    def body(q0, q1, q2, k0, k1, k2, v0, v1, v2, dqm_ref, dz_ref, c_ref, s1_ref, s2_ref, dp_ref):
        cc, a1, a2 = c_ref[...], s1_ref[...], s2_ref[...]
        for g, (qr, kr, vr) in enumerate(((q0, k0, v0), (q1, k1, v1), (q2, k2, v2))):
            for j in range(GW // 128):
                ls_ = slice(j * 128, (j + 1) * 128)
                c0 = g * GW + j * 128
                dp_ref[:, c0:c0 + 128] = (_rope_bwd(qr[:, ls_], cc, a1, a2) * SCALE).astype(BF16)
                dp_ref[:, NQ + c0:NQ + c0 + 128] = _rope_bwd(kr[:, ls_], cc, a1, a2).astype(BF16)
            dp_ref[:, 2 * NQ + g * GW:2 * NQ + (g + 1) * GW] = vr[...].astype(BF16)
        dp_ref[:, 3 * NQ:3 * NQ + MW] = dqm_ref[...]
        dp_ref[:, 3 * NQ + MW:] = dz_ref[...]

    return pl.pallas_call(
        body, name="qkv_bwd", grid=(NT,),
        in_specs=[_rows(GW)] * 9 + [_rows(MW), _rows(BR_A), _rows(128), _rows(128), _rows(128)],
        out_specs=_rows(IN_A), out_shape=_sds((S, IN_A), BF16),
        compiler_params=_params(("parallel",)),
    )(*dqs, *dks, *dvs, dqm, dz, c, s1, s2)


def _mem_bwd(mem, mg, memn, wkv, dkv0, dkv1):
    def body(mem_ref, mg_ref, memn_ref, w_ref, d0_ref, d1_ref, dw_ref, dg_ref):
        mf = mem_ref[...]
        n = mf * lax.rsqrt(jnp.mean(mf * mf, axis=-1, keepdims=True) + EPS)
        for i, d_ref in enumerate((d0_ref, d1_ref)):
            dkv = d_ref[...].astype(BF16)
            mn = memn_ref[i]
            for s in range(4):
                cs = slice(s * NM, (s + 1) * NM)
                dw_ref[s, i] = _dot_tn(mn[:, cs], dkv)
                dmn = _dot_nt(dkv, w_ref[s, i])
                dg_ref[i:i + 1, cs] = jnp.sum(dmn * n[:, cs], axis=0, keepdims=True)

    return pl.pallas_call(
        body, name="mem_bwd", grid=(1,),
        in_specs=[_full((NM, D)), _full((2, D)), _full((2, NM, D)), _full((4, 2, NM, 2 * MW)),
                  _full((NM, 2 * MW)), _full((NM, 2 * MW))],
        out_specs=[_full((4, 2, NM, 2 * MW)), _full((2, D))],
        out_shape=[_sds((4, 2, NM, 2 * MW), F32), _sds((2, D), F32)],
        compiler_params=_params(("arbitrary",)),
    )(mem, mg, memn, wkv, dkv0, dkv1)


MESH = pl.DeviceIdType.MESH
ANY = pl.BlockSpec(memory_space=pl.ANY)
BIG = (("wkv", 2, NM, 2 * MW), ("w_in_a", 1, D, SH_A), ("w_out_a", 1, BR_A, SH_O),
       ("w_in_b", 1, D, SH_B), ("w_out_b", 1, BR_B // 4, D))
NBIG = len(BIG)
CW_ROWS = 8


def _place():
    x, y, c = lax.axis_index("x"), lax.axis_index("y"), lax.axis_index("c")
    chips = ((1 - x, y), (x, 1 - y), (1 - x, 1 - y))
    return x, y, c, chips


def _remote(src, dst, ssem, rsem, dev):
    return pltpu.make_async_remote_copy(src_ref=src, dst_ref=dst, send_sem=ssem, recv_sem=rsem,
                                        device_id=dev, device_id_type=MESH)


def _cast_weights(place, ws):
    nblk = 4

    def body(pref, *refs):
        for i in range(NBIG):
            refs[NBIG + i][0] = refs[i][...].astype(BF16)

    grid_spec = pltpu.PrefetchScalarGridSpec(
        num_scalar_prefetch=1, grid=(nblk,),
        in_specs=[pl.BlockSpec((k, r // nblk, cdim), lambda i, pref: (0, i, 0)) for _, k, r, cdim in BIG],
        out_specs=[pl.BlockSpec((1, k, r // nblk, cdim), lambda i, pref: (pref[1], 0, i, 0)) for _, k, r, cdim in BIG])
    return pl.pallas_call(
        body, name="cast_weights", grid_spec=grid_spec,
        out_shape=[_sds((4, k, r, cdim), BF16) for _, k, r, cdim in BIG],
        compiler_params=_params(("parallel",)),
    )(place, *ws)


LAYER_A = (0, 1, 2)
LAYER_B = (3, 4)
HBM = pl.BlockSpec(memory_space=pltpu.HBM)
SEM = pl.BlockSpec(memory_space=pltpu.SEMAPHORE)
EFFECT = pltpu.SideEffectType.DATAFLOW_SIDE_EFFECTING
TOKEN = (8, 128)


def _half(ref, w, which):
    h = BIG[w][2] // 2
    return ref.at[:, pl.ds(which * h, h), :]


def _skip_arg(body, pos, *refs):
    return body(*refs[:pos], *refs[pos + 1:])


def _gather_weights(wb, cw, idx, name):
    n = len(idx)

    def body(*refs):
        src_cw = refs[n]
        dst = refs[n + 1:2 * n + 2]
        loc_sem, send_sems, recv_sems, fsend_sems, frecv_sems = refs[2 * n + 2:]
        x, y, c, chips = _place()
        me = 2 * x + y
        loc = pltpu.make_async_copy(src_cw, dst[n].at[me], loc_sem)
        loc.start()
        sends = []
        for j, (px, py) in enumerate(chips):
            for i in range(n):
                mine = _half(dst[i].at[me], idx[i], c)
                sends.append(_remote(mine, mine, send_sems.at[j, i], recv_sems.at[j, i], (px, py, c)))
            sends.append(_remote(src_cw, dst[n].at[me], send_sems.at[j, n], recv_sems.at[j, n], (px, py, c)))
        for cp in sends:
            cp.start()
        fwds = []
        for j, (px, py) in enumerate(chips):
            for i in range(n):
                got = _half(dst[i].at[2 * px + py], idx[i], c)
                _remote(got, got, send_sems.at[j, i], recv_sems.at[j, i], (px, py, c)).wait_recv()
                fwds.append(_remote(got, got, fsend_sems.at[j, i], frecv_sems.at[j, i], (x, y, 1 - c)))
                fwds[-1].start()
            got = dst[n].at[2 * px + py]
            _remote(got, got, send_sems.at[j, n], recv_sems.at[j, n], (px, py, c)).wait_recv()
        for j, (px, py) in enumerate(chips):
            for i in range(n):
                got = _half(dst[i].at[2 * px + py], idx[i], 1 - c)
                _remote(got, got, fsend_sems.at[j, i], frecv_sems.at[j, i], (x, y, 1 - c)).wait_recv()
        for cp in sends + fwds:
            cp.wait_send()
        loc.wait()

    out_shape = [_sds(w.shape, BF16) for w in wb] + [_sds((4, CW_ROWS, SH_O), F32)]
    return pl.pallas_call(
        body, name=name, in_specs=[ANY] * (n + 1), out_specs=[ANY] * (n + 1), out_shape=out_shape,
        input_output_aliases={i: i for i in range(n)},
        scratch_shapes=[pltpu.SemaphoreType.DMA, pltpu.SemaphoreType.DMA((3, n + 1)),
                        pltpu.SemaphoreType.DMA((3, n + 1)), pltpu.SemaphoreType.DMA((3, n)),
                        pltpu.SemaphoreType.DMA((3, n))],
    )(*wb, cw)


def _gather_start(wb, after, idx, name):
    n = len(idx)

    def body(*refs):
        src = refs[:n]
        send_sems, recv_sems = refs[n + 1], refs[n + 2]
        token = refs[2 * n + 3]
        x, y, c, chips = _place()
        me = 2 * x + y
        for j, (px, py) in enumerate(chips):
            for i in range(n):
                mine = _half(src[i].at[me], idx[i], c)
                _remote(mine, mine, send_sems.at[j * n + i], recv_sems.at[j * n + i], (px, py, c)).start()
        token[...] = jnp.zeros(TOKEN, F32)

    outs = pl.pallas_call(
        body, name=name, in_specs=[HBM] * n + [ANY],
        out_specs=(SEM, SEM) + (HBM,) * n + (pl.BlockSpec(memory_space=pltpu.VMEM),),
        out_shape=(pltpu.SemaphoreType.DMA((3 * n,)), pltpu.SemaphoreType.DMA((3 * n,)))
        + tuple(pltpu.HBM(w.shape, w.dtype) for w in wb) + (_sds(TOKEN, F32),),
        input_output_aliases={i: 2 + i for i in range(n)},
        compiler_params=pltpu.CompilerParams(has_side_effects=EFFECT),
    )(*[pltpu.with_memory_space_constraint(w, pltpu.HBM) for w in wb], after)
    return outs[0], outs[1], list(outs[2:2 + n]), outs[2 + n]


def _gather_wait(send_sems, recv_sems, wb, after, idx, name):
    n = len(idx)

    def body(*refs):
        buf = refs[:n]
        send_sems, recv_sems = refs[n], refs[n + 1]
        x, y, c, chips = _place()
        me = 2 * x + y
        for j, (px, py) in enumerate(chips):
            for i in range(n):
                mine = _half(buf[i].at[me], idx[i], c)
                got = _half(buf[i].at[2 * px + py], idx[i], c)
                _remote(mine, mine, send_sems.at[j * n + i], recv_sems.at[j * n + i], (px, py, c)).wait_send()
                _remote(got, got, send_sems.at[j * n + i], recv_sems.at[j * n + i], (px, py, c)).wait_recv()

    outs = pl.pallas_call(
        body, name=name, in_specs=[HBM] * n + [SEM, SEM] + [ANY] * len(after), out_specs=(HBM,) * n,
        out_shape=tuple(pltpu.HBM(w.shape, w.dtype) for w in wb),
        input_output_aliases={i: i for i in range(n)},
        compiler_params=pltpu.CompilerParams(has_side_effects=EFFECT),
    )(*wb, send_sems, recv_sems, *after)
    return list(outs)


def _gather_forward(wb, idx, name):
    n = len(idx)

    def body(*refs):
        dst = refs[n:2 * n]
        send_sems, recv_sems = refs[2 * n:]
        x, y, c, chips = _place()
        fwds = []
        for j, (px, py) in enumerate(chips):
            for i in range(n):
                got = _half(dst[i].at[2 * px + py], idx[i], c)
                fwds.append(_remote(got, got, send_sems.at[j, i], recv_sems.at[j, i], (x, y, 1 - c)))
                fwds[-1].start()
        for j, (px, py) in enumerate(chips):
            for i in range(n):
                got = _half(dst[i].at[2 * px + py], idx[i], 1 - c)
                _remote(got, got, send_sems.at[j, i], recv_sems.at[j, i], (x, y, 1 - c)).wait_recv()
        for cp in fwds:
            cp.wait_send()

    return pl.pallas_call(
        body, name=name, in_specs=[ANY] * n, out_specs=[ANY] * n, out_shape=[_sds(w.shape, BF16) for w in wb],
        input_output_aliases={i: i for i in range(n)},
        scratch_shapes=[pltpu.SemaphoreType.DMA((3, n)), pltpu.SemaphoreType.DMA((3, n))],
    )(*wb)


def _pair_exchange(gs, idx, name):
    n = len(idx)

    def body(*refs):
        src, dst = refs[:n], refs[n:2 * n]
        send_sems, recv_sems = refs[2 * n:]
        x, y, c, _ = _place()
        cps = []
        for i in range(n):
            h = BIG[idx[i]][2] // 2
            cps.append(_remote(src[i].at[:, :, pl.ds((1 - c) * h, h), :], dst[i], send_sems.at[i], recv_sems.at[i],
                               (x, y, 1 - c)))
            cps[-1].start()
        for cp in cps:
            cp.wait()

    return pl.pallas_call(
        body, name=name, in_specs=[ANY] * n, out_specs=[ANY] * n,
        out_shape=[_sds((4, BIG[w][1], BIG[w][2] // 2, BIG[w][3]), F32) for w in idx],
        scratch_shapes=[pltpu.SemaphoreType.DMA((n,)), pltpu.SemaphoreType.DMA((n,))],
    )(*gs)


def _pair_sum(place, g, r1, i):
    _, k, r, cdim = BIG[i]
    h = r // 2

    def body(pref, g_ref, r_ref, o_ref):
        o_ref[...] = (g_ref[...] + r_ref[...]).astype(BF16)

    grid_spec = pltpu.PrefetchScalarGridSpec(
        num_scalar_prefetch=1, grid=(4, k),
        in_specs=[pl.BlockSpec((1, 1, h, cdim), lambda s, t, pref: (s, t, pref[0], 0)),
                  pl.BlockSpec((1, 1, h, cdim), lambda s, t, pref: (s, t, 0, 0))],
        out_specs=pl.BlockSpec((1, 1, h, cdim), lambda s, t, pref: (s, t, 0, 0)))
    return pl.pallas_call(
        body, name=f"pair_sum_{BIG[i][0]}", grid_spec=grid_spec, out_shape=_sds((4, k, h, cdim), BF16),
        compiler_params=_params(("parallel", "parallel")),
    )(place, g, r1)


def _chip_start(ps, idx, name):
    n = len(idx)

    def body(*refs):
        src, land = refs[:n], refs[n:2 * n]
        send_sems, recv_sems = refs[2 * n], refs[2 * n + 1]
        token = refs[4 * n + 2]
        x, y, c, chips = _place()
        for j, (px, py) in enumerate(chips):
            for i in range(n):
                _remote(src[i].at[2 * px + py], land[i].at[j], send_sems.at[j * n + i], recv_sems.at[j * n + i],
                        (px, py, c)).start()
        token[...] = jnp.zeros(TOKEN, F32)

    lands = [lax.empty((3,) + p.shape[1:], BF16) for p in ps]
    outs = pl.pallas_call(
        body, name=name, in_specs=[HBM] * (2 * n),
        out_specs=(SEM, SEM) + (HBM,) * (2 * n) + (pl.BlockSpec(memory_space=pltpu.VMEM),),
        out_shape=(pltpu.SemaphoreType.DMA((3 * n,)), pltpu.SemaphoreType.DMA((3 * n,)))
        + tuple(pltpu.HBM(a.shape, a.dtype) for a in list(ps) + lands) + (_sds(TOKEN, F32),),
        input_output_aliases={i: 2 + i for i in range(2 * n)},
        compiler_params=pltpu.CompilerParams(has_side_effects=EFFECT),
    )(*[pltpu.with_memory_space_constraint(a, pltpu.HBM) for a in list(ps) + lands])
    return outs[0], outs[1], list(outs[2:2 + n]), list(outs[2 + n:2 + 2 * n]), outs[2 + 2 * n]


def _chip_wait(send_sems, recv_sems, ps, lands, after, idx, name):
    n = len(idx)

    def body(*refs):
        src, land = refs[:n], refs[n:2 * n]
        send_sems, recv_sems = refs[2 * n], refs[2 * n + 1]
        x, y, c, chips = _place()
        for j, (px, py) in enumerate(chips):
            for i in range(n):
                cp = _remote(src[i].at[2 * px + py], land[i].at[j], send_sems.at[j * n + i], recv_sems.at[j * n + i],
                             (px, py, c))
                cp.wait_send()
                cp.wait_recv()

    arrays = list(ps) + list(lands)
    outs = pl.pallas_call(
        body, name=name, in_specs=[HBM] * (2 * n) + [SEM, SEM] + [ANY] * len(after), out_specs=(HBM,) * (2 * n),
        out_shape=tuple(pltpu.HBM(a.shape, a.dtype) for a in arrays),
        input_output_aliases={i: i for i in range(2 * n)},
        compiler_params=pltpu.CompilerParams(has_side_effects=EFFECT),
    )(*arrays, send_sems, recv_sems, *after)
    return list(outs[n:])


def _chip_sum(place, g, r1, r2, i):
    _, k, r, cdim = BIG[i]
    h = r // 2

    def body(pref, g_ref, r1_ref, r2_ref, o_ref):
        acc = g_ref[0, 0] + r1_ref[0, 0]
        for j in range(3):
            acc = acc + r2_ref[j, 0].astype(F32)
        o_ref[0] = acc

    grid_spec = pltpu.PrefetchScalarGridSpec(
        num_scalar_prefetch=1, grid=(k,),
        in_specs=[pl.BlockSpec((1, 1, h, cdim), lambda t, pref: (pref[1], t, pref[0], 0)),
                  pl.BlockSpec((1, 1, h, cdim), lambda t, pref: (pref[1], t, 0, 0)),
                  pl.BlockSpec((3, 1, h, cdim), lambda t, pref: (0, t, 0, 0))],
        out_specs=pl.BlockSpec((1, h, cdim), lambda t, pref: (t, pref[0], 0)))
    return pl.pallas_call(
        body, name=f"chip_sum_{BIG[i][0]}", grid_spec=grid_spec, out_shape=_sds((k, r, cdim), F32),
        compiler_params=_params(("parallel",)),
    )(place, g, r1, r2)


def _pair_gather(hs, idx, name):
    n = len(idx)

    def body(*refs):
        dst = refs[n:2 * n]
        send_sems, recv_sems = refs[2 * n:]
        x, y, c, _ = _place()
        cps = []
        for i in range(n):
            mine = _half(dst[i], idx[i], c)
            cps.append(_remote(mine, mine, send_sems.at[i], recv_sems.at[i], (x, y, 1 - c)))
            cps[-1].start()
        for i in range(n):
            theirs = _half(dst[i], idx[i], 1 - c)
            _remote(theirs, theirs, send_sems.at[i], recv_sems.at[i], (x, y, 1 - c)).wait_recv()
        for cp in cps:
            cp.wait_send()

    return pl.pallas_call(
        body, name=name, in_specs=[ANY] * n, out_specs=[ANY] * n,
        out_shape=[_sds(BIG[w][1:], F32) for w in idx],
        input_output_aliases={i: i for i in range(n)},
        scratch_shapes=[pltpu.SemaphoreType.DMA((n,)), pltpu.SemaphoreType.DMA((n,))],
    )(*hs)


SMALL_ROWS = 40


def _all_reduce_small(pack):
    def body(p_ref, o_ref, slots, send_sems, recv_sems):
        x, y, c, _ = _place()
        me = 4 * x + 2 * y + c
        cps = []
        for r in range(1, 8):
            peer = (x if not r & 4 else 1 - x, y if not r & 2 else 1 - y, c if not r & 1 else 1 - c)
            cps.append(_remote(p_ref, slots.at[r], send_sems.at[r - 1], recv_sems.at[r - 1], peer))
            cps[-1].start()
        slots[0] = p_ref[...]
        for cp in cps:
            cp.wait()
        acc = slots[me]
        for dev in range(1, 8):
            acc = acc + slots[jnp.bitwise_xor(me, dev)]
        o_ref[...] = acc

    vm = pl.BlockSpec(memory_space=pltpu.VMEM)
    return pl.pallas_call(
        body, name="all_reduce_small", in_specs=[vm], out_specs=vm, out_shape=_sds((SMALL_ROWS, D), F32),
        scratch_shapes=[pltpu.VMEM((8, SMALL_ROWS, D), F32), pltpu.SemaphoreType.DMA((7,)),
                        pltpu.SemaphoreType.DMA((7,))],
    )(pack)


def _adamw_math(w, g, m, v):
    m = ADAM_B1 * m + (1.0 - ADAM_B1) * g
    v = ADAM_B2 * v + (1.0 - ADAM_B2) * (g * g)
    m_hat = m / (1.0 - ADAM_B1 ** ADAM_STEP)
    v_hat = v / (1.0 - ADAM_B2 ** ADAM_STEP)
    delta = -ADAM_LR * (m_hat / (jnp.sqrt(v_hat) + ADAM_EPS) + ADAM_WD * w)
    return delta, m, v


def _adamw_big(w, g, m, v, i):
    _, k, r, cdim = BIG[i]
    nblk = 4 if k == 1 else 1

    def body(w_ref, g_ref, m_ref, v_ref, d_ref, nm_ref, nv_ref):
        d_ref[...], nm_ref[...], nv_ref[...] = _adamw_math(w_ref[...], g_ref[...], m_ref[...], v_ref[...])

    spec = pl.BlockSpec((1, r // nblk, cdim), lambda t, b: (t, b, 0))
    return pl.pallas_call(
        body, name=f"adamw_{BIG[i][0]}", grid=(k, nblk), in_specs=[spec] * 4, out_specs=[spec] * 3,
        out_shape=[_sds((k, r, cdim), F32)] * 3,
        compiler_params=_params(("parallel", "parallel")),
    )(w, g, m, v)


def _adamw_small(ws, gs, ms, vs):
    n = len(ws)

    def body(*refs):
        for i in range(n):
            w_ref, g_ref, m_ref, v_ref = refs[i], refs[n + i], refs[2 * n + i], refs[3 * n + i]
            d, nm, nv = _adamw_math(w_ref[...], g_ref[...], m_ref[...], v_ref[...])
            refs[4 * n + i][...] = d
            refs[5 * n + i][...] = nm
            refs[6 * n + i][...] = nv

    specs = [_full(w.shape) for w in ws]
    outs = pl.pallas_call(
        body, name="adamw_small", grid=(1,), in_specs=specs * 4, out_specs=specs * 3,
        out_shape=[_sds(w.shape, F32) for w in ws] * 3,
        compiler_params=_params(("arbitrary",)),
    )(*ws, *gs, *ms, *vs)
    return outs[:n], outs[n:2 * n], outs[2 * n:]


def _pad_rows(a, rows):
    return jnp.pad(a, ((0, rows - a.shape[0]), (0, 0)))


def kernel(x, mem, positions, norm_g, mem_norm_g, w_mem_kv, attn_w_in, attn_w_out, conv_w_in, conv_w, conv_w_out, final_g, loss_target, m_norm_g, m_mem_norm_g, m_w_mem_kv, m_attn_w_in, m_attn_w_out, m_conv_w_in, m_conv_w, m_conv_w_out, m_final_g, v_norm_g, v_mem_norm_g, v_w_mem_kv, v_attn_w_in, v_attn_w_out, v_conv_w_in, v_conv_w, v_conv_w_out, v_final_g):
    mx, my, mc = lax.axis_index("x"), lax.axis_index("y"), lax.axis_index("c")
    place = jnp.stack([mc, 2 * mx + my]).astype(jnp.int32)

    w_big = [w_mem_kv, attn_w_in, attn_w_out, conv_w_in, conv_w_out]
    m_big = [m_w_mem_kv, m_attn_w_in, m_attn_w_out, m_conv_w_in, m_conv_w_out]
    v_big = [v_w_mem_kv, v_attn_w_in, v_attn_w_out, v_conv_w_in, v_conv_w_out]
    wb = _cast_weights(place, w_big)
    *full_a, cw_f = _gather_weights([wb[i] for i in LAYER_A], _pad_rows(conv_w[0], CW_ROWS), LAYER_A, "gather_a")
    wkv_f = full_a[0]
    w_in_a = full_a[1].reshape(4, D, SH_A)
    w_out_a = full_a[2].reshape(4, BR_A, SH_O)
    cw8 = cw_f.transpose(1, 0, 2).reshape(CW_ROWS, D)
    gb_send, gb_recv, gb_bufs, gb_token = _gather_start([wb[i] for i in LAYER_B], cw_f, LAYER_B, "gather_b_start")

    xs, tgt = x[0], loss_target[0]
    g0, g1 = norm_g[0:1], norm_g[1:2]
    rc, rs1, rs2 = _rope_tables(positions[0].astype(F32).reshape(S, 1))
    memn, kv = _mem_fwd(mem[0], mem_norm_g, wkv_f)
    hn0, q, k, v, qm0, z0 = _in_proj_a(xs, g0, w_in_a, rc, rs1, rs2, gb_token)
    fwd = [_attn_fwd(q, k, v, g) for g in range(3)]
    os_, ls, lss = [f[0] for f in fwd], [f[1] for f in fwd], [f[2] for f in fwd]
    h1 = _attn_out(os_, ls, qm0, kv[0], z0, xs, w_out_a)

    full_b = _gather_forward(_gather_wait(gb_send, gb_recv, gb_bufs, [h1], LAYER_B, "gather_b_wait"), LAYER_B,
                             "gather_b_forward")
    w_in_b = full_b[0].reshape(4, D, SH_B)
    w_out_b = full_b[1].reshape(BR_B, D)
    hn1, bg, cg, u, qm1, z1 = _in_proj_b(h1, g1, w_in_b)
    dh2, loss_part, dfg = _conv_out_loss(bg, cg, u, cw8, qm1, kv[1], z1, h1, w_out_b, final_g.reshape(1, D), tgt)

    dproj_b, dw_out_b, dcw, dkv1 = _conv_bwd(dh2, bg, cg, u, cw8, qm1, kv[1], z1, w_out_b)
    dw_in_b = _w_in_grad(hn1, dproj_b, IN_B, "w_in_b_grad")
    gs_b = [dw_in_b.reshape(4, 1, D, SH_B), dw_out_b.reshape(4, 1, BR_B // 4, D)]
    r1_b = _pair_exchange(gs_b, LAYER_B, "pair_exchange_b")
    ps_b = [_pair_sum(place, gs_b[i], r1_b[i], w) for i, w in enumerate(LAYER_B)]
    cb_send, cb_recv, cb_src, cb_land, cb_token = _chip_start(ps_b, LAYER_B, "chip_b_start")

    dh1, dg1 = _in_proj_bwd(dproj_b, w_in_b, h1, g1, dh2, cb_token, IN_B, "in_proj_b_bwd")
    outs = _attn_out_bwd(dh1, os_, ls, qm0, kv[0], z0, w_out_a)
    dos, dds, dqm, dz, dw_out_a, dkv0 = outs[0:3], outs[3:6], outs[6], outs[7], outs[8], outs[9]
    bwd = [_attn_bwd(q, k, v, dos[g], lss[g], dds[g], g) for g in range(3)]
    dproj_a = _qkv_bwd([b[0] for b in bwd], [b[1] for b in bwd], [b[2] for b in bwd], dqm, dz, rc, rs1, rs2)
    dw_in_a = _w_in_grad(hn0, dproj_a, IN_A, "w_in_a_grad")
    dwkv, dmg = _mem_bwd(mem[0], mem_norm_g, memn, wkv_f, dkv0, dkv1)

    r2_b = _chip_wait(cb_send, cb_recv, cb_src, cb_land, [dw_in_a, dwkv], LAYER_B, "chip_b_wait")
    hs_b = [_chip_sum(place, gs_b[i], r1_b[i], r2_b[i], w) for i, w in enumerate(LAYER_B)]

    gs_a = [dwkv, dw_in_a.reshape(4, 1, D, SH_A), dw_out_a.reshape(4, 1, BR_A, SH_O)]
    r1_a = _pair_exchange(gs_a, LAYER_A, "pair_exchange_a")
    ps_a = [_pair_sum(place, gs_a[i], r1_a[i], w) for i, w in enumerate(LAYER_A)]
    ca_send, ca_recv, ca_src, ca_land, ca_token = _chip_start(ps_a, LAYER_A, "chip_a_start")

    gx, dg0 = _in_proj_bwd(dproj_a, w_in_a, xs, g0, dh1, ca_token, IN_A, "in_proj_a_bwd")
    g_b = _pair_gather(hs_b, LAYER_B, "pair_gather_b")

    pack = jnp.concatenate([_pad_rows(jnp.concatenate([dg0, dg1], axis=0), 8), _pad_rows(dmg, 8), _pad_rows(dfg, 8),
                            dcw, _pad_rows(jnp.pad(loss_part, ((0, 0), (0, D - 128))), 8)], axis=0)
    tot = _all_reduce_small(pack)
    loss = tot[32, 0]
    g_norm, g_memnorm, g_final = tot[0:2], tot[8:10], tot[16]
    g_conv = lax.dynamic_slice(tot, (24, (2 * mx + my) * SH_O), (3, SH_O))
    upd_b = [_adamw_big(w_big[w], g_b[i], m_big[w], v_big[w], w) for i, w in enumerate(LAYER_B)]

    r2_a = _chip_wait(ca_send, ca_recv, ca_src, ca_land, [tot, upd_b[0][0], upd_b[1][0]], LAYER_A, "chip_a_wait")
    hs_a = [_chip_sum(place, gs_a[i], r1_a[i], r2_a[i], w) for i, w in enumerate(LAYER_A)]
    g_a = _pair_gather(hs_a, LAYER_A, "pair_gather_a")
    upd_a = [_adamw_big(w_big[w], g_a[i], m_big[w], v_big[w], w) for i, w in enumerate(LAYER_A)]
    g_big = list(g_a) + list(g_b)
    upd = upd_a + upd_b
    sw = [norm_g, mem_norm_g, final_g.reshape(1, D), conv_w[0]]
    sg = [g_norm, g_memnorm, g_final.reshape(1, D), g_conv]
    sm = [m_norm_g, m_mem_norm_g, m_final_g.reshape(1, D), m_conv_w[0]]
    sv = [v_norm_g, v_mem_norm_g, v_final_g.reshape(1, D), v_conv_w[0]]
    sd, snm, snv = _adamw_small(sw, sg, sm, sv)

    def order(norm, memnorm, wkv, w_in_a, w_out_a, w_in_b, conv, w_out_b, final):
        return (norm, memnorm, wkv, w_in_a, w_out_a, w_in_b, conv.reshape(1, 3, SH_O), w_out_b, final.reshape(D))

    grads = order(g_norm, g_memnorm, g_big[0], g_big[1], g_big[2], g_big[3], g_conv, g_big[4], g_final)
    deltas = order(sd[0], sd[1], upd[0][0], upd[1][0], upd[2][0], upd[3][0], sd[3], upd[4][0], sd[2])
    new_m = order(snm[0], snm[1], upd[0][1], upd[1][1], upd[2][1], upd[3][1], snm[3], upd[4][1], snm[2])
    new_v = order(snv[0], snv[1], upd[0][2], upd[1][2], upd[2][2], upd[3][2], snv[3], upd[4][2], snv[2])
    return (loss, gx[None], *grads, *deltas, *new_m, *new_v)
```

```python
import functools

import numpy as np
import jax
import jax.numpy as jnp
from jax import lax
from jax.experimental import pallas as pl
from jax.experimental.pallas import tpu as pltpu

F32 = jnp.float32
BF16 = jnp.bfloat16

S = 2048
D = 1024
TM = 256
NT = S // TM
HD = 64
GW = 512
NQ = 3 * GW
MW = 256
NM = 256
IN_A = 3 * NQ + MW + GW + MW
IN_B = 3 * D + MW + D + MW
BR_A = GW + MW
BR_B = D + MW
SH_A = IN_A // 4
SH_B = IN_B // 4
SH_O = D // 4
QBLK = 128
DILATIONS = (1, 4, 16)
EPS = 1e-6
SCALE = HD ** -0.5
NEG = -1e30
ROPE_THETA = 500000.0

ADAM_LR = 0.001
ADAM_B1 = 0.9
ADAM_B2 = 0.999
ADAM_EPS = 1e-08
ADAM_WD = 0.01
ADAM_STEP = 10

VMEM_LIMIT_BYTES = 60 * 1024 * 1024


def _params(sem=None):
    if sem is None:
        return pltpu.CompilerParams(vmem_limit_bytes=VMEM_LIMIT_BYTES)
    return pltpu.CompilerParams(dimension_semantics=sem, vmem_limit_bytes=VMEM_LIMIT_BYTES)


def _full(shape):
    nd = len(shape)
    return pl.BlockSpec(shape, lambda *_: (0,) * nd)


def _rows(width, tm=TM):
    return pl.BlockSpec((tm, width), lambda i: (i, 0))


def _sds(shape, dtype):
    return jax.ShapeDtypeStruct(shape, dtype)


def _silu_parts(z):
    sig = 1.0 / (1.0 + jnp.exp(-z))
    return z * sig, sig * (1.0 + z * (1.0 - sig))


def _dot(a, b):
    return jnp.dot(a, b, preferred_element_type=F32)


def _dot_nt(a, b):
    return lax.dot_general(a, b, (((1,), (1,)), ((), ())), preferred_element_type=F32)


def _dot_tn(a, b):
    return lax.dot_general(a, b, (((0,), (0,)), ((), ())), preferred_element_type=F32)


def _rope_fwd(t, c, s1, s2):
    return t * c + pltpu.roll(t, 120, 1) * s1 + pltpu.roll(t, 8, 1) * s2


def _rope_bwd(g, c, s1, s2):
    return g * c + pltpu.roll(g * s1, 8, 1) + pltpu.roll(g * s2, 120, 1)


def _mem_attn(qm, kv):
    res = []
    for h in range(MW // HD):
        sl = slice(h * HD, (h + 1) * HD)
        s = _dot_nt(qm[:, sl], kv[:, sl]) * SCALE
        e = jnp.exp(s - jnp.max(s, axis=-1, keepdims=True))
        p = e / jnp.sum(e, axis=-1, keepdims=True)
        res.append((p, _dot(p.astype(BF16), kv[:, MW + h * HD:MW + (h + 1) * HD])))
    return res


def _mem_attn_bwd(dmo, heads, qm, kv, dqm_store, dkv_ref):
    for h, (p, mo) in enumerate(heads):
        sl = slice(h * HD, (h + 1) * HD)
        vs = slice(MW + h * HD, MW + (h + 1) * HD)
        dmo_h = dmo[:, sl]
        dmo_b = dmo_h.astype(BF16)
        dp = _dot_nt(dmo_b, kv[:, vs])
        delta = jnp.sum(dmo_h * mo, axis=-1, keepdims=True)
        ds = (p * (dp - delta) * SCALE).astype(BF16)
        dqm_store(h, _dot(ds, kv[:, sl]))
        dkv_ref[:, sl] += _dot_tn(ds, qm[:, sl])
        dkv_ref[:, vs] += _dot_tn(p.astype(BF16), dmo_b)


def _merge(o_refs, l_refs):
    ls = [r[...] for r in l_refs]
    m = jnp.maximum(jnp.maximum(ls[0], ls[1]), ls[2])
    es = [jnp.exp(l - m) for l in ls]
    inv = 1.0 / (es[0] + es[1] + es[2])
    ws = [e * inv for e in es]
    os_ = [r[...] for r in o_refs]
    mix = ws[0] * os_[0] + ws[1] * os_[1] + ws[2] * os_[2]
    return ws, mix


def _conv_taps(cg, u, cgp, up, first):
    a = cg * u
    ap = jnp.where(first, 0.0, cgp * up)
    row = lax.broadcasted_iota(jnp.int32, a.shape, 0)
    a1 = jnp.where(row == 0, ap[7:8, :], pltpu.roll(a, 1, 0))
    a2 = jnp.where(row == 0, ap[6:7, :], jnp.where(row == 1, ap[7:8, :], pltpu.roll(a, 2, 0)))
    return a, a1, a2


def _rope_tables(posf, after):
    half = 8
    invf = np.float32(ROPE_THETA) ** (-np.arange(half, dtype=np.float32) * np.float32(2.0 / 16))
    lane = np.arange(128)
    table = np.where((lane % HD) < 16, invf[lane % half], 0.0).astype(np.float32)[None, :]

    def body(pos_ref, invf_ref, c_ref, s1_ref, s2_ref):
        ang = pos_ref[...] * invf_ref[...]
        jm = lax.broadcasted_iota(jnp.int32, ang.shape, 1) & (HD - 1)
        cs = jnp.cos(ang)
        sn = jnp.sin(ang)
        c_ref[...] = jnp.where(jm < 16, cs, 1.0)
        s1_ref[...] = jnp.where(jm < 8, -sn, 0.0)
        s2_ref[...] = jnp.where((jm >= 8) & (jm < 16), sn, 0.0)

    out = _sds((S, 128), F32)
    return pl.pallas_call(
        functools.partial(_skip_arg, body, 2), name="rope_tables", grid=(NT,),
        in_specs=[_rows(1), _full((1, 128)), pl.BlockSpec(memory_space=pl.ANY)],
        out_specs=[_rows(128)] * 3, out_shape=[out] * 3,
        compiler_params=_params(("parallel",)),
    )(posf, jnp.asarray(table), after)


def _in_proj_a(x, g0, w_in, c, s1, s2, after):
    def body(x_ref, g_ref, w_ref, c_ref, s1_ref, s2_ref, hn_ref, q_ref, k_ref, v_ref, qm_ref, z_ref, proj):
        xf = x_ref[...]
        hn = xf * lax.rsqrt(jnp.mean(xf * xf, axis=-1, keepdims=True) + EPS) * g_ref[...]
        hb = hn.astype(BF16)
        hn_ref[...] = hb
        for s in range(4):
            proj[:, s * SH_A:(s + 1) * SH_A] = _dot(hb, w_ref[s])
        cc, a1, a2 = c_ref[...], s1_ref[...], s2_ref[...]
        for j in range(NQ // 128):
            q_ref[:, j * 128:(j + 1) * 128] = (
                _rope_fwd(proj[:, j * 128:(j + 1) * 128], cc, a1, a2) * SCALE).astype(BF16)
            k_ref[:, j * 128:(j + 1) * 128] = _rope_fwd(
                proj[:, NQ + j * 128:NQ + (j + 1) * 128], cc, a1, a2).astype(BF16)
        v_ref[...] = proj[:, 2 * NQ:3 * NQ].astype(BF16)
        qm_ref[...] = proj[:, 3 * NQ:3 * NQ + MW].astype(BF16)
        z_ref[...] = proj[:, 3 * NQ + MW:]

    return pl.pallas_call(
        functools.partial(_skip_arg, body, 6), name="in_proj_a", grid=(NT,),
        in_specs=[_rows(D), _full((1, D)), _full((4, D, SH_A)), _rows(128), _rows(128), _rows(128),
                  pl.BlockSpec(memory_space=pl.ANY)],
        out_specs=[_rows(D), _rows(NQ), _rows(NQ), _rows(NQ), _rows(MW), _rows(BR_A)],
        out_shape=[_sds((S, D), BF16), _sds((S, NQ), BF16), _sds((S, NQ), BF16), _sds((S, NQ), BF16),
                   _sds((S, MW), BF16), _sds((S, BR_A), F32)],
        scratch_shapes=[pltpu.VMEM((TM, IN_A), F32)],
        compiler_params=_params(("parallel",)),
    )(x, g0, w_in, c, s1, s2, after)


def _mem_fwd(mem, mg, wkv):
    def body(mem_ref, mg_ref, w_ref, memn_ref, kv_ref):
        mf = mem_ref[...]
        n = mf * lax.rsqrt(jnp.mean(mf * mf, axis=-1, keepdims=True) + EPS)
        for i in range(2):
            mn = (n * mg_ref[i:i + 1, :]).astype(BF16)
            memn_ref[i] = mn
            acc = _dot(mn[:, 0:NM], w_ref[0, i])
            for s in range(1, 4):
                acc += _dot(mn[:, s * NM:(s + 1) * NM], w_ref[s, i])
            kv_ref[i] = acc.astype(BF16)

    return pl.pallas_call(
        body, name="mem_fwd", grid=(1,),
        in_specs=[_full((NM, D)), _full((2, D)), _full((4, 2, NM, 2 * MW))],
        out_specs=[_full((2, NM, D)), _full((2, NM, 2 * MW))],
        out_shape=[_sds((2, NM, D), BF16), _sds((2, NM, 2 * MW), BF16)],
        compiler_params=_params(("arbitrary",)),
    )(mem, mg, wkv)


def _band_mask(j):
    qi = lax.broadcasted_iota(jnp.int32, (QBLK, 2 * QBLK), 0)
    kj = lax.broadcasted_iota(jnp.int32, (QBLK, 2 * QBLK), 1)
    dist = qi + QBLK - kj
    return (dist >= 0) & (dist <= QBLK) & ((kj >= QBLK) | (j > 0))


LANES = 128
NCHUNK = GW // LANES
FWD_UNROLL = 16
BWD_UNROLL = 4


def _perm_matrix(d):
    n = TM // d
    p = np.zeros((TM, TM), np.float32)
    for r in range(d):
        for i in range(n):
            p[r * n + i, i * d + r] = 1.0
    return p


def _split_dot(p, x, parts):
    acc = None
    for _ in range(parts):
        hi = x.astype(BF16)
        term = _dot(p, hi)
        acc = term if acc is None else acc + term
        x = x - hi.astype(F32)
    return acc


def _tile_to_streams(y, dst, t, d):
    n, ln = TM // d, S // d
    for r in range(d):
        dst[r * ln + t * n:r * ln + (t + 1) * n, :] = y[r * n:(r + 1) * n].astype(dst.dtype)


def _tile_from_streams(src, t, d):
    n, ln = TM // d, S // d
    return jnp.concatenate([src[r * ln + t * n:r * ln + (t + 1) * n, :] for r in range(d)], axis=0)


def _head_masks():
    first = lax.broadcasted_iota(jnp.int32, (TM, LANES), 1) < HD
    return first, jnp.logical_not(first)


def _attn_fwd(q, k, v, g):
    d = DILATIONS[g]
    nb = S // d // QBLK
    perm = _perm_matrix(d)

    def body(q_ref, k_ref, v_ref, p_ref, pt_ref, o_ref, l_ref, ls_ref, q0, q1, ks, vs, os_):
        first, second = _head_masks()
        pm = p_ref[...]
        for t in range(NT):
            rows = slice(t * TM, (t + 1) * TM)
            if d == 1:
                qt = q_ref[rows, :].astype(F32)
            else:
                qt = _dot(pm, q_ref[rows, :])
                _tile_to_streams(_dot(pm, k_ref[rows, :]), ks, t, d)
                _tile_to_streams(_dot(pm, v_ref[rows, :]), vs, t, d)
            _tile_to_streams(jnp.where(first, qt, 0.0), q0, t, d)
            _tile_to_streams(jnp.where(second, qt, 0.0), q1, t, d)
        kref, vref = (k_ref, v_ref) if d == 1 else (ks, vs)
        oref, lref = (o_ref, l_ref) if d == 1 else (os_, ls_ref)

        def blk(b, carry):
            r0 = pl.multiple_of(b * QBLK, QBLK)
            p0 = pl.multiple_of(jnp.maximum(b - 1, 0) * QBLK, QBLK)
            kk = jnp.concatenate([kref[pl.ds(p0, QBLK), :], kref[pl.ds(r0, QBLK), :]], axis=0)
            vv = jnp.concatenate([vref[pl.ds(p0, QBLK), :], vref[pl.ds(r0, QBLK), :]], axis=0)
            valid = _band_mask(b & (nb - 1))
            acc, den, lse = [], [], []
            for qh in (q0, q1):
                s = jnp.where(valid, _dot_nt(qh[pl.ds(r0, QBLK), :], kk), NEG)
                m = jnp.max(s, axis=-1, keepdims=True)
                e = jnp.exp(s - m)
                l = jnp.sum(e, axis=-1, keepdims=True)
                acc.append(_dot(e.astype(BF16), vv))
                den.append(l)
                lse.append(m + jnp.log(l))
            f = first[:QBLK]
            oref[pl.ds(r0, QBLK), :] = jnp.where(f, acc[0], acc[1]) / jnp.where(f, den[0], den[1])
            lref[pl.ds(r0, QBLK), :] = jnp.where(f, lse[0], lse[1])
            return carry

        lax.fori_loop(0, S // QBLK, blk, 0, unroll=FWD_UNROLL)
        if d > 1:
            ptm = pt_ref[...]
            for t in range(NT):
                rows = slice(t * TM, (t + 1) * TM)
                o_ref[rows, :] = _split_dot(ptm, _tile_from_streams(os_, t, d), 2)
                l_ref[rows, :] = _split_dot(ptm, _tile_from_streams(ls_ref, t, d), 3)

    qkv_spec = pl.BlockSpec((S, LANES), lambda c: (0, g * NCHUNK + c))
    out_spec = pl.BlockSpec((S, LANES), lambda c: (0, c))
    n_out = 2 if d == 1 else 3
    outs = pl.pallas_call(
        body if d > 1 else functools.partial(_drop_arg, body, 7), name=f"attn_fwd_g{g}", grid=(NCHUNK,),
        in_specs=[qkv_spec] * 3 + [_full((TM, TM))] * 2, out_specs=[out_spec] * n_out,
        out_shape=[_sds((S, GW), F32)] * n_out,
        scratch_shapes=[pltpu.VMEM((S, LANES), BF16)] * 4 + [pltpu.VMEM((S, LANES), F32)],
        compiler_params=_params(("parallel",)),
    )(q, k, v, jnp.asarray(perm, BF16), jnp.asarray(perm.T, BF16))
    return (outs[0], outs[1], outs[1]) if d == 1 else tuple(outs)


def _drop_arg(body, pos, *refs):
    return body(*refs[:pos], None, *refs[pos:])


def _attn_out(os_, ls, qm, kv0, z, x, w_out):
    def body(o0, o1, o2, l0, l1, l2, qm_ref, kv_ref, z_ref, x_ref, w_ref, h_ref, ybuf):
        _, mix = _merge((o0, o1, o2), (l0, l1, l2))
        sz, _ = _silu_parts(z_ref[...])
        ybuf[:, :GW] = (mix * sz[:, :GW]).astype(BF16)
        for h, (_, mo) in enumerate(_mem_attn(qm_ref[...], kv_ref[...])):
            sl = slice(GW + h * HD, GW + (h + 1) * HD)
            ybuf[:, sl] = (mo * sz[:, sl]).astype(BF16)
        yb = ybuf[...]
        for s in range(4):
            cs = slice(s * SH_O, (s + 1) * SH_O)
            h_ref[:, cs] = x_ref[:, cs] + _dot(yb, w_ref[s])

    return pl.pallas_call(
        body, name="attn_out", grid=(NT,),
        in_specs=[_rows(GW)] * 6 + [_rows(MW), _full((NM, 2 * MW)), _rows(BR_A), _rows(D), _full((4, BR_A, SH_O))],
        out_specs=_rows(D), out_shape=_sds((S, D), F32),
        scratch_shapes=[pltpu.VMEM((TM, BR_A), BF16)],
        compiler_params=_params(("parallel",)),
    )(*os_, *ls, qm, kv0, z, x, w_out)


def _in_proj_b(h1, g1, w_in):
    def body(x_ref, g_ref, w_ref, hn_ref, bg_ref, cg_ref, u_ref, qm_ref, z_ref, proj):
        xf = x_ref[...]
        hn = xf * lax.rsqrt(jnp.mean(xf * xf, axis=-1, keepdims=True) + EPS) * g_ref[...]
        hb = hn.astype(BF16)
        hn_ref[...] = hb
        for s in range(4):
            proj[:, s * SH_B:(s + 1) * SH_B] = _dot(hb, w_ref[s])
        bg_ref[...] = proj[:, :D]
        cg_ref[...] = proj[:, D:2 * D]
        u_ref[...] = proj[:, 2 * D:3 * D]
        qm_ref[...] = proj[:, 3 * D:3 * D + MW].astype(BF16)
        z_ref[...] = proj[:, 3 * D + MW:]

    return pl.pallas_call(
        body, name="in_proj_b", grid=(NT,),
        in_specs=[_rows(D), _full((1, D)), _full((4, D, SH_B))],
        out_specs=[_rows(D), _rows(D), _rows(D), _rows(D), _rows(MW), _rows(BR_B)],
        out_shape=[_sds((S, D), BF16), _sds((S, D), F32), _sds((S, D), F32), _sds((S, D), F32),
                   _sds((S, MW), BF16), _sds((S, BR_B), F32)],
        scratch_shapes=[pltpu.VMEM((TM, IN_B), F32)],
        compiler_params=_params(("parallel",)),
    )(h1, g1, w_in)


def _prev8(width):
    return pl.BlockSpec((8, width), lambda i: (jnp.maximum(i * (TM // 8) - 1, 0), 0))


def _conv_out_loss(bg, cg, u, cw, qm, kv1, z, h1, w_out, fg, tgt):
    def body(bg_ref, cg_ref, u_ref, cgp_ref, up_ref, cw_ref, qm_ref, kv_ref, z_ref, h_ref, w_ref, fg_ref, t_ref,
             dh_ref, loss_ref, dfg_ref, ybuf):
        i = pl.program_id(0)
        a, a1, a2 = _conv_taps(cg_ref[...], u_ref[...], cgp_ref[...], up_ref[...], i == 0)
        conv = cw_ref[0:1, :] * a2 + cw_ref[1:2, :] * a1 + cw_ref[2:3, :] * a
        sz, _ = _silu_parts(z_ref[...])
        ybuf[:, :D] = (bg_ref[...] * conv * sz[:, :D]).astype(BF16)
        for h, (_, mo) in enumerate(_mem_attn(qm_ref[...], kv_ref[...])):
            sl = slice(D + h * HD, D + (h + 1) * HD)
            ybuf[:, sl] = (mo * sz[:, sl]).astype(BF16)
        h2 = h_ref[...] + _dot(ybuf[...], w_ref[...])
        rstd = lax.rsqrt(jnp.mean(h2 * h2, axis=-1, keepdims=True) + EPS)
        n = h2 * rstd
        fgv = fg_ref[...]
        err = n * fgv - t_ref[...]
        dout = err * (1.0 / D)
        dn = dout * fgv
        dh_ref[...] = rstd * (dn - n * jnp.mean(dn * n, axis=-1, keepdims=True))

        @pl.when(i == 0)
        def _():
            loss_ref[...] = jnp.zeros_like(loss_ref)
            dfg_ref[...] = jnp.zeros_like(dfg_ref)

        loss_ref[...] += jnp.sum(err * err) * (0.5 / D)
        dfg_ref[...] += jnp.sum(dout * n, axis=0, keepdims=True)

    return pl.pallas_call(
        body, name="conv_out_loss", grid=(NT,),
        in_specs=[_rows(D), _rows(D), _rows(D), _prev8(D), _prev8(D), _full((8, D)), _rows(MW),
                  _full((NM, 2 * MW)), _rows(BR_B), _rows(D), _full((BR_B, D)), _full((1, D)), _rows(D)],
        out_specs=[_rows(D), _full((1, 128)), _full((1, D))],
        out_shape=[_sds((S, D), F32), _sds((1, 128), F32), _sds((1, D), F32)],
        scratch_shapes=[pltpu.VMEM((TM, BR_B), BF16)],
        compiler_params=_params(("arbitrary",)),
    )(bg, cg, u, cg, u, cw, qm, kv1, z, h1, w_out, fg, tgt)


def _conv_bwd(dh2, bg, cg, u, cw, qm, kv1, z, w_out):
    rev = lambda i: (NT - 1 - i, 0)
    rows = lambda w: pl.BlockSpec((TM, w), rev)
    prev8 = pl.BlockSpec((8, D), lambda i: (jnp.maximum((NT - 1 - i) * (TM // 8) - 1, 0), 0))

    def body(dh_ref, bg_ref, cg_ref, u_ref, cgp_ref, up_ref, cw_ref, qm_ref, kv_ref, z_ref, w_ref,
             dproj_ref, dw_ref, dcw_ref, dkv_ref, ybuf, carry):
        i = pl.program_id(0)

        @pl.when(i == 0)
        def _():
            dw_ref[...] = jnp.zeros_like(dw_ref)
            dcw_ref[...] = jnp.zeros_like(dcw_ref)
            dkv_ref[...] = jnp.zeros_like(dkv_ref)
            carry[...] = jnp.zeros_like(carry)

        bgv, cgv, uv = bg_ref[...], cg_ref[...], u_ref[...]
        a, a1, a2 = _conv_taps(cgv, uv, cgp_ref[...], up_ref[...], i == NT - 1)
        w0, w1, w2 = cw_ref[0:1, :], cw_ref[1:2, :], cw_ref[2:3, :]
        conv = w0 * a2 + w1 * a1 + w2 * a
        mix = bgv * conv
        zv = z_ref[...]
        sz, dsz = _silu_parts(zv)
        qmv, kvv = qm_ref[...], kv_ref[...]
        heads = _mem_attn(qmv, kvv)
        ybuf[:, :D] = (mix * sz[:, :D]).astype(BF16)
        for h, (_, mo) in enumerate(heads):
            sl = slice(D + h * HD, D + (h + 1) * HD)
            ybuf[:, sl] = (mo * sz[:, sl]).astype(BF16)
        dhb = dh_ref[...].astype(BF16)
        dw_ref[...] += _dot_tn(ybuf[...], dhb)
        dy = _dot_nt(dhb, w_ref[...])
        dcat = dy * sz
        dproj_ref[:, 3 * D + MW:3 * D + MW + D] = (dy[:, :D] * mix * dsz[:, :D]).astype(BF16)
        for h, (_, mo) in enumerate(heads):
            sl = slice(D + h * HD, D + (h + 1) * HD)
            dproj_ref[:, 3 * D + MW + D + h * HD:3 * D + MW + D + (h + 1) * HD] = (
                dy[:, sl] * mo * dsz[:, sl]).astype(BF16)
        dmix = dcat[:, :D]
        dproj_ref[:, :D] = (dmix * conv).astype(BF16)
        dc = dmix * bgv
        nxt = carry[...]
        row = lax.broadcasted_iota(jnp.int32, dc.shape, 0)
        dc1 = jnp.where(row == TM - 1, nxt[0:1, :], pltpu.roll(dc, TM - 1, 0))
        dc2 = jnp.where(row == TM - 2, nxt[0:1, :], jnp.where(row == TM - 1, nxt[1:2, :], pltpu.roll(dc, TM - 2, 0)))
        carry[...] = dc[0:8, :]
        da = w2 * dc + w1 * dc1 + w0 * dc2
        dproj_ref[:, D:2 * D] = (da * uv).astype(BF16)
        dproj_ref[:, 2 * D:3 * D] = (da * cgv).astype(BF16)
        dcw_ref[0:1, :] += jnp.sum(dc * a2, axis=0, keepdims=True)
        dcw_ref[1:2, :] += jnp.sum(dc * a1, axis=0, keepdims=True)
        dcw_ref[2:3, :] += jnp.sum(dc * a, axis=0, keepdims=True)

        def dqm_store(h, val):
            dproj_ref[:, 3 * D + h * HD:3 * D + (h + 1) * HD] = val.astype(BF16)

        _mem_attn_bwd(dcat[:, D:], heads, qmv, kvv, dqm_store, dkv_ref)

    return pl.pallas_call(
        body, name="conv_bwd", grid=(NT,),
        in_specs=[rows(D), rows(D), rows(D), rows(D), prev8, prev8, _full((8, D)), rows(MW),
                  _full((NM, 2 * MW)), rows(BR_B), _full((BR_B, D))],
        out_specs=[rows(IN_B), _full((BR_B, D)), _full((8, D)), _full((NM, 2 * MW))],
        out_shape=[_sds((S, IN_B), BF16), _sds((BR_B, D), F32), _sds((8, D), F32), _sds((NM, 2 * MW), F32)],
        scratch_shapes=[pltpu.VMEM((TM, BR_B), BF16), pltpu.VMEM((8, D), F32)],
        compiler_params=_params(("arbitrary",)),
    )(dh2, bg, cg, u, cg, u, cw, qm, kv1, z, w_out)


def _in_proj_bwd(dproj, w_in, xin, g, dres, after, width, name):
    sh = width // 4

    def body(dp_ref, w_ref, x_ref, g_ref, dr_ref, dx_ref, dg_ref):
        i = pl.program_id(0)
        dhn = _dot_nt(dp_ref[:, 0:sh], w_ref[0])
        for s in range(1, 4):
            dhn += _dot_nt(dp_ref[:, s * sh:(s + 1) * sh], w_ref[s])
        xf = x_ref[...]
        rstd = lax.rsqrt(jnp.mean(xf * xf, axis=-1, keepdims=True) + EPS)
        n = xf * rstd
        dn = dhn * g_ref[...]
        dx_ref[...] = dr_ref[...] + rstd * (dn - n * jnp.mean(dn * n, axis=-1, keepdims=True))

        @pl.when(i == 0)
        def _():
            dg_ref[...] = jnp.zeros_like(dg_ref)

        dg_ref[...] += jnp.sum(dhn * n, axis=0, keepdims=True)

    return pl.pallas_call(
        functools.partial(_skip_arg, body, 5), name=name, grid=(NT,),
        in_specs=[_rows(width), _full((4, D, sh)), _rows(D), _full((1, D)), _rows(D), pl.BlockSpec(memory_space=pl.ANY)],
        out_specs=[_rows(D), _full((1, D))],
        out_shape=[_sds((S, D), F32), _sds((1, D), F32)],
        compiler_params=_params(("arbitrary",)),
    )(dproj, w_in, xin, g, dres, after)


def _w_in_grad(hn, dproj, width, name):
    sh = width // 4

    def body(hn_ref, dp_ref, dw_ref):
        dw_ref[0] = _dot_tn(hn_ref[...], dp_ref[...])

    return pl.pallas_call(
        body, name=name, grid=(4,),
        in_specs=[_full((S, D)), pl.BlockSpec((S, sh), lambda s: (0, s))],
        out_specs=pl.BlockSpec((1, D, sh), lambda s: (s, 0, 0)),
        out_shape=_sds((4, D, sh), F32),
        compiler_params=_params(("parallel",)),
    )(hn, dproj)


def _attn_out_bwd(dh1, os_, ls, qm, kv0, z, w_out, after):
    ones_bd = np.kron(np.eye(GW // HD, dtype=np.float32), np.ones((HD, HD), np.float32))

    def body(dh_ref, o0, o1, o2, l0, l1, l2, qm_ref, kv_ref, z_ref, w_ref, bd_ref,
             do0, do1, do2, dd0, dd1, dd2, dqm_ref, dz_ref, dw_ref, dkv_ref, ybuf):
        i = pl.program_id(0)

        @pl.when(i == 0)
        def _():
            dw_ref[...] = jnp.zeros_like(dw_ref)
            dkv_ref[...] = jnp.zeros_like(dkv_ref)

        ws, mix = _merge((o0, o1, o2), (l0, l1, l2))
        sz, dsz = _silu_parts(z_ref[...])
        qmv, kvv = qm_ref[...], kv_ref[...]
        heads = _mem_attn(qmv, kvv)
        ybuf[:, :GW] = (mix * sz[:, :GW]).astype(BF16)
        for h, (_, mo) in enumerate(heads):
            sl = slice(GW + h * HD, GW + (h + 1) * HD)
            ybuf[:, sl] = (mo * sz[:, sl]).astype(BF16)
        yb = ybuf[...]
        dh = dh_ref[...]
        dy = None
        for s in range(4):
            dhb = dh[:, s * SH_O:(s + 1) * SH_O].astype(BF16)
            dw_ref[s] += _dot_tn(yb, dhb)
            part = _dot_nt(dhb, w_ref[s])
            dy = part if dy is None else dy + part
        dcat = dy * sz
        dz_ref[:, :GW] = (dy[:, :GW] * mix * dsz[:, :GW]).astype(BF16)
        for h, (_, mo) in enumerate(heads):
            sl = slice(GW + h * HD, GW + (h + 1) * HD)
            dz_ref[:, sl] = (dy[:, sl] * mo * dsz[:, sl]).astype(BF16)
        dmix = dcat[:, :GW]
        prod = dmix * mix
        hi = prod.astype(BF16)
        lo = (prod - hi.astype(F32)).astype(BF16)
        bd = bd_ref[...]
        tot = _dot(hi, bd) + _dot(lo, bd)
        for w, do_ref, dd_ref in zip(ws, (do0, do1, do2), (dd0, dd1, dd2)):
            do_ref[...] = (w * dmix).astype(BF16)
            dd_ref[...] = w * tot

        def dqm_store(h, val):
            dqm_ref[:, h * HD:(h + 1) * HD] = val.astype(BF16)

        _mem_attn_bwd(dcat[:, GW:], heads, qmv, kvv, dqm_store, dkv_ref)

    return pl.pallas_call(
        functools.partial(_skip_arg, body, 12), name="attn_out_bwd", grid=(NT,),
        in_specs=[_rows(D)] + [_rows(GW)] * 6 + [_rows(MW), _full((NM, 2 * MW)), _rows(BR_A),
                                                   _full((4, BR_A, SH_O)), _full((GW, GW)),
                                                   pl.BlockSpec(memory_space=pl.ANY)],
        out_specs=[_rows(GW)] * 6 + [_rows(MW), _rows(BR_A), _full((4, BR_A, SH_O)), _full((NM, 2 * MW))],
        out_shape=[_sds((S, GW), BF16)] * 3 + [_sds((S, GW), F32)] * 3 + [
            _sds((S, MW), BF16), _sds((S, BR_A), BF16), _sds((4, BR_A, SH_O), F32), _sds((NM, 2 * MW), F32)],
        scratch_shapes=[pltpu.VMEM((TM, BR_A), BF16)],
        compiler_params=_params(("arbitrary",)),
    )(dh1, *os_, *ls, qm, kv0, z, w_out, jnp.asarray(ones_bd, dtype=BF16), after)


def _attn_bwd(q, k, v, do, lse_s, dd, g):
    d = DILATIONS[g]
    nb = S // d // QBLK
    perm = _perm_matrix(d)

    def body(q_ref, k_ref, v_ref, do_ref, l_ref, dd_ref, p_ref, pt_ref, dq_ref, dk_ref, dv_ref,
             q0, q1, g0, g1, ks, vs, dds, dqs, dks, dvs):
        first, second = _head_masks()
        pm = p_ref[...]
        for t in range(NT):
            rows = slice(t * TM, (t + 1) * TM)
            if d == 1:
                qt = q_ref[rows, :].astype(F32)
                gt = do_ref[rows, :].astype(F32)
            else:
                qt = _dot(pm, q_ref[rows, :])
                gt = _dot(pm, do_ref[rows, :])
                _tile_to_streams(_dot(pm, k_ref[rows, :]), ks, t, d)
                _tile_to_streams(_dot(pm, v_ref[rows, :]), vs, t, d)
                _tile_to_streams(_split_dot(pm, dd_ref[rows, :], 2), dds, t, d)
            _tile_to_streams(jnp.where(first, qt, 0.0), q0, t, d)
            _tile_to_streams(jnp.where(second, qt, 0.0), q1, t, d)
            _tile_to_streams(jnp.where(first, gt, 0.0), g0, t, d)
            _tile_to_streams(jnp.where(second, gt, 0.0), g1, t, d)
        kref, vref, ddref = (k_ref, v_ref, dd_ref) if d == 1 else (ks, vs, dds)
        dqref, dkref, dvref = (dq_ref, dk_ref, dv_ref) if d == 1 else (dqs, dks, dvs)
        dkref[...] = jnp.zeros_like(dkref)
        dvref[...] = jnp.zeros_like(dvref)

        def blk(b, carry):
            r0 = pl.multiple_of(b * QBLK, QBLK)
            p0 = pl.multiple_of(jnp.maximum(b - 1, 0) * QBLK, QBLK)
            kk = jnp.concatenate([kref[pl.ds(p0, QBLK), :], kref[pl.ds(r0, QBLK), :]], axis=0)
            vv = jnp.concatenate([vref[pl.ds(p0, QBLK), :], vref[pl.ds(r0, QBLK), :]], axis=0)
            lb = l_ref[pl.ds(r0, QBLK), :]
            ddb = ddref[pl.ds(r0, QBLK), :]
            valid = _band_mask(b & (nb - 1))
            dqh, dkk, dvv = [], None, None
            for h, (qh, gh) in enumerate(((q0, g0), (q1, g1))):
                qb = qh[pl.ds(r0, QBLK), :]
                gb = gh[pl.ds(r0, QBLK), :]
                s = _dot_nt(qb, kk)
                p = jnp.where(valid, jnp.exp(s - lb[:, h * HD:h * HD + 1]), 0.0)
                dp = _dot_nt(gb, vv)
                ds = (p * (dp - ddb[:, h * HD:h * HD + 1])).astype(BF16)
                dqh.append(_dot(ds, kk))
                tk = _dot_tn(ds, qb)
                tv = _dot_tn(p.astype(BF16), gb)
                dkk = tk if dkk is None else dkk + tk
                dvv = tv if dvv is None else dvv + tv
            dqref[pl.ds(r0, QBLK), :] = jnp.where(first[:QBLK], dqh[0], dqh[1])
            dkref[pl.ds(p0, QBLK), :] += dkk[:QBLK]
            dkref[pl.ds(r0, QBLK), :] += dkk[QBLK:]
            dvref[pl.ds(p0, QBLK), :] += dvv[:QBLK]
            dvref[pl.ds(r0, QBLK), :] += dvv[QBLK:]
            return carry

        lax.fori_loop(0, S // QBLK, blk, 0, unroll=BWD_UNROLL)
        if d > 1:
            ptm = pt_ref[...]
            for t in range(NT):
                rows = slice(t * TM, (t + 1) * TM)
                dq_ref[rows, :] = _split_dot(ptm, _tile_from_streams(dqs, t, d), 2)
                dk_ref[rows, :] = _split_dot(ptm, _tile_from_streams(dks, t, d), 2)
                dv_ref[rows, :] = _split_dot(ptm, _tile_from_streams(dvs, t, d), 2)

    qkv_spec = pl.BlockSpec((S, LANES), lambda c: (0, g * NCHUNK + c))
    one_spec = pl.BlockSpec((S, LANES), lambda c: (0, c))
    return pl.pallas_call(
        body, name=f"attn_bwd_g{g}", grid=(NCHUNK,),
        in_specs=[qkv_spec] * 3 + [one_spec] * 3 + [_full((TM, TM))] * 2, out_specs=[one_spec] * 3,
        out_shape=[_sds((S, GW), F32)] * 3,
        scratch_shapes=[pltpu.VMEM((S, LANES), BF16)] * 6 + [pltpu.VMEM((S, LANES), F32)] * 4,
        compiler_params=_params(("parallel",)),
    )(q, k, v, do, lse_s, dd, jnp.asarray(perm, BF16), jnp.asarray(perm.T, BF16))


def _qkv_bwd(dqs, dks, dvs, dqm, dz, c, s1, s2):
    def body(q0, q1, q2, k0, k1, k2, v0, v1, v2, dqm_ref, dz_ref, c_ref, s1_ref, s2_ref, dp_ref):
        cc, a1, a2 = c_ref[...], s1_ref[...], s2_ref[...]
        for g, (qr, kr, vr) in enumerate(((q0, k0, v0), (q1, k1, v1), (q2, k2, v2))):
            for j in range(GW // 128):
                ls_ = slice(j * 128, (j + 1) * 128)
                c0 = g * GW + j * 128
                dp_ref[:, c0:c0 + 128] = (_rope_bwd(qr[:, ls_], cc, a1, a2) * SCALE).astype(BF16)
                dp_ref[:, NQ + c0:NQ + c0 + 128] = _rope_bwd(kr[:, ls_], cc, a1, a2).astype(BF16)
            dp_ref[:, 2 * NQ + g * GW:2 * NQ + (g + 1) * GW] = vr[...].astype(BF16)
        dp_ref[:, 3 * NQ:3 * NQ + MW] = dqm_ref[...]
        dp_ref[:, 3 * NQ + MW:] = dz_ref[...]

    return pl.pallas_call(
        body, name="qkv_bwd", grid=(NT,),
        in_specs=[_rows(GW)] * 9 + [_rows(MW), _rows(BR_A), _rows(128), _rows(128), _rows(128)],
        out_specs=_rows(IN_A), out_shape=_sds((S, IN_A), BF16),
        compiler_params=_params(("parallel",)),
    )(*dqs, *dks, *dvs, dqm, dz, c, s1, s2)


def _mem_bwd(mem, mg, memn, wkv, dkv0, dkv1):
    def body(mem_ref, mg_ref, memn_ref, w_ref, d0_ref, d1_ref, dw_ref, dg_ref):
        mf = mem_ref[...]
        n = mf * lax.rsqrt(jnp.mean(mf * mf, axis=-1, keepdims=True) + EPS)
        for i, d_ref in enumerate((d0_ref, d1_ref)):
            dkv = d_ref[...].astype(BF16)
            mn = memn_ref[i]
            for s in range(4):
                cs = slice(s * NM, (s + 1) * NM)
                dw_ref[s, i] = _dot_tn(mn[:, cs], dkv)
                dmn = _dot_nt(dkv, w_ref[s, i])
                dg_ref[i:i + 1, cs] = jnp.sum(dmn * n[:, cs], axis=0, keepdims=True)

    return pl.pallas_call(
        body, name="mem_bwd", grid=(1,),
        in_specs=[_full((NM, D)), _full((2, D)), _full((2, NM, D)), _full((4, 2, NM, 2 * MW)),
                  _full((NM, 2 * MW)), _full((NM, 2 * MW))],
        out_specs=[_full((4, 2, NM, 2 * MW)), _full((2, D))],
        out_shape=[_sds((4, 2, NM, 2 * MW), F32), _sds((2, D), F32)],
        compiler_params=_params(("arbitrary",)),
    )(mem, mg, memn, wkv, dkv0, dkv1)


MESH = pl.DeviceIdType.MESH
ANY = pl.BlockSpec(memory_space=pl.ANY)
BIG = (("wkv", 2, NM, 2 * MW), ("w_in_a", 1, D, SH_A), ("w_out_a", 1, BR_A, SH_O),
       ("w_in_b", 1, D, SH_B), ("w_out_b", 1, BR_B // 4, D))
NBIG = len(BIG)
CW_ROWS = 8


def _place():
    x, y, c = lax.axis_index("x"), lax.axis_index("y"), lax.axis_index("c")
    chips = ((1 - x, y), (x, 1 - y), (1 - x, 1 - y))
    return x, y, c, chips


def _remote(src, dst, ssem, rsem, dev):
    return pltpu.make_async_remote_copy(src_ref=src, dst_ref=dst, send_sem=ssem, recv_sem=rsem,
                                        device_id=dev, device_id_type=MESH)


def _cast_weights(place, ws):
    nblk = 4

    def body(pref, *refs):
        for i in range(NBIG):
            refs[NBIG + i][0] = refs[i][...].astype(BF16)

    grid_spec = pltpu.PrefetchScalarGridSpec(
        num_scalar_prefetch=1, grid=(nblk,),
        in_specs=[pl.BlockSpec((k, r // nblk, cdim), lambda i, pref: (0, i, 0)) for _, k, r, cdim in BIG],
        out_specs=[pl.BlockSpec((1, k, r // nblk, cdim), lambda i, pref: (pref[1], 0, i, 0)) for _, k, r, cdim in BIG])
    return pl.pallas_call(
        body, name="cast_weights", grid_spec=grid_spec,
        out_shape=[_sds((4, k, r, cdim), BF16) for _, k, r, cdim in BIG],
        compiler_params=_params(("parallel",)),
    )(place, *ws)


LAYER_A = (0, 1, 2)
LAYER_B = (3, 4)
HBM = pl.BlockSpec(memory_space=pltpu.HBM)
SEM = pl.BlockSpec(memory_space=pltpu.SEMAPHORE)
EFFECT = pltpu.SideEffectType.DATAFLOW_SIDE_EFFECTING
TOKEN = (8, 128)


def _half(ref, w, which):
    h = BIG[w][2] // 2
    return ref.at[:, pl.ds(which * h, h), :]


def _skip_arg(body, pos, *refs):
    return body(*refs[:pos], *refs[pos + 1:])


def _gather_weights(wb, cw, idx, name):
    n = len(idx)

    def body(*refs):
        src_cw = refs[n]
        dst = refs[n + 1:2 * n + 2]
        loc_sem, send_sems, recv_sems, fsend_sems, frecv_sems = refs[2 * n + 2:]
        x, y, c, chips = _place()
        me = 2 * x + y
        loc = pltpu.make_async_copy(src_cw, dst[n].at[me], loc_sem)
        loc.start()
        sends = []
        for j, (px, py) in enumerate(chips):
            for i in range(n):
                mine = _half(dst[i].at[me], idx[i], c)
                sends.append(_remote(mine, mine, send_sems.at[j, i], recv_sems.at[j, i], (px, py, c)))
            sends.append(_remote(src_cw, dst[n].at[me], send_sems.at[j, n], recv_sems.at[j, n], (px, py, c)))
        for cp in sends:
            cp.start()
        fwds = []
        for j, (px, py) in enumerate(chips):
            for i in range(n):
                got = _half(dst[i].at[2 * px + py], idx[i], c)
                _remote(got, got, send_sems.at[j, i], recv_sems.at[j, i], (px, py, c)).wait_recv()
                fwds.append(_remote(got, got, fsend_sems.at[j, i], frecv_sems.at[j, i], (x, y, 1 - c)))
                fwds[-1].start()
            got = dst[n].at[2 * px + py]
            _remote(got, got, send_sems.at[j, n], recv_sems.at[j, n], (px, py, c)).wait_recv()
        for j, (px, py) in enumerate(chips):
            for i in range(n):
                got = _half(dst[i].at[2 * px + py], idx[i], 1 - c)
                _remote(got, got, fsend_sems.at[j, i], frecv_sems.at[j, i], (x, y, 1 - c)).wait_recv()
        for cp in sends + fwds:
            cp.wait_send()
        loc.wait()

    out_shape = [_sds(w.shape, BF16) for w in wb] + [_sds((4, CW_ROWS, SH_O), F32)]
    return pl.pallas_call(
        body, name=name, in_specs=[ANY] * (n + 1), out_specs=[ANY] * (n + 1), out_shape=out_shape,
        input_output_aliases={i: i for i in range(n)},
        scratch_shapes=[pltpu.SemaphoreType.DMA, pltpu.SemaphoreType.DMA((3, n + 1)),
                        pltpu.SemaphoreType.DMA((3, n + 1)), pltpu.SemaphoreType.DMA((3, n)),
                        pltpu.SemaphoreType.DMA((3, n))],
    )(*wb, cw)


def _gather_start(wb, after, idx, name):
    n = len(idx)

    def body(*refs):
        src = refs[:n]
        send_sems, recv_sems = refs[n + 1], refs[n + 2]
        token = refs[2 * n + 3]
        x, y, c, chips = _place()
        me = 2 * x + y
        for j, (px, py) in enumerate(chips):
            for i in range(n):
                mine = _half(src[i].at[me], idx[i], c)
                _remote(mine, mine, send_sems.at[j * n + i], recv_sems.at[j * n + i], (px, py, c)).start()
        token[...] = jnp.zeros(TOKEN, F32)

    outs = pl.pallas_call(
        body, name=name, in_specs=[HBM] * n + [ANY],
        out_specs=(SEM, SEM) + (HBM,) * n + (pl.BlockSpec(memory_space=pltpu.VMEM),),
        out_shape=(pltpu.SemaphoreType.DMA((3 * n,)), pltpu.SemaphoreType.DMA((3 * n,)))
        + tuple(pltpu.HBM(w.shape, w.dtype) for w in wb) + (_sds(TOKEN, F32),),
        input_output_aliases={i: 2 + i for i in range(n)},
        compiler_params=pltpu.CompilerParams(has_side_effects=EFFECT),
    )(*[pltpu.with_memory_space_constraint(w, pltpu.HBM) for w in wb], after)
    return outs[0], outs[1], list(outs[2:2 + n]), outs[2 + n]


def _gather_wait(send_sems, recv_sems, wb, after, idx, name):
    n = len(idx)

    def body(*refs):
        buf = refs[:n]
        send_sems, recv_sems = refs[n], refs[n + 1]
        x, y, c, chips = _place()
        me = 2 * x + y
        for j, (px, py) in enumerate(chips):
            for i in range(n):
                mine = _half(buf[i].at[me], idx[i], c)
                got = _half(buf[i].at[2 * px + py], idx[i], c)
                _remote(mine, mine, send_sems.at[j * n + i], recv_sems.at[j * n + i], (px, py, c)).wait_send()
                _remote(got, got, send_sems.at[j * n + i], recv_sems.at[j * n + i], (px, py, c)).wait_recv()

    outs = pl.pallas_call(
        body, name=name, in_specs=[HBM] * n + [SEM, SEM] + [ANY] * len(after), out_specs=(HBM,) * n,
        out_shape=tuple(pltpu.HBM(w.shape, w.dtype) for w in wb),
        input_output_aliases={i: i for i in range(n)},
        compiler_params=pltpu.CompilerParams(has_side_effects=EFFECT),
    )(*wb, send_sems, recv_sems, *after)
    return list(outs)


def _gather_forward(wb, idx, name, cw=None):
    n = len(idx)
    m = n if cw is None else n + 1

    def body(*refs):
        dst = refs[m:2 * m]
        send_sems, recv_sems = refs[2 * m], refs[2 * m + 1]
        x, y, c, chips = _place()
        cps = []
        for j, (px, py) in enumerate(chips):
            for i in range(n):
                got = _half(dst[i].at[2 * px + py], idx[i], c)
                cps.append(_remote(got, got, send_sems.at[j, i], recv_sems.at[j, i], (x, y, 1 - c)))
                cps[-1].start()
        if cw is not None:
            src_cw, loc_sem = refs[n], refs[2 * m + 2]
            me = 2 * x + y
            loc = pltpu.make_async_copy(src_cw, dst[n].at[me], loc_sem)
            loc.start()
            for j, (px, py) in enumerate(chips):
                cps.append(_remote(src_cw, dst[n].at[me], send_sems.at[j, n], recv_sems.at[j, n], (px, py, c)))
                cps[-1].start()
        for j, (px, py) in enumerate(chips):
            for i in range(n):
                got = _half(dst[i].at[2 * px + py], idx[i], 1 - c)
                _remote(got, got, send_sems.at[j, i], recv_sems.at[j, i], (x, y, 1 - c)).wait_recv()
            if cw is not None:
                got = dst[n].at[2 * px + py]
                _remote(got, got, send_sems.at[j, n], recv_sems.at[j, n], (px, py, c)).wait_recv()
        for cp in cps:
            cp.wait_send()
        if cw is not None:
            loc.wait()

    out_shape = [_sds(w.shape, BF16) for w in wb]
    scratch = [pltpu.SemaphoreType.DMA((3, m)), pltpu.SemaphoreType.DMA((3, m))]
    args = list(wb)
    if cw is not None:
        out_shape.append(_sds((4, CW_ROWS, SH_O), F32))
        scratch.append(pltpu.SemaphoreType.DMA)
        args.append(cw)
    return pl.pallas_call(
        body, name=name, in_specs=[ANY] * m, out_specs=[ANY] * m, out_shape=out_shape,
        input_output_aliases={i: i for i in range(n)}, scratch_shapes=scratch,
    )(*args)


def _pair_exchange(gs, idx, name):
    n = len(idx)

    def body(*refs):
        src, dst = refs[:n], refs[n:2 * n]
        send_sems, recv_sems = refs[2 * n:]
        x, y, c, _ = _place()
        cps = []
        for i in range(n):
            h = BIG[idx[i]][2] // 2
            cps.append(_remote(src[i].at[:, :, pl.ds((1 - c) * h, h), :], dst[i], send_sems.at[i], recv_sems.at[i],
                               (x, y, 1 - c)))
            cps[-1].start()
        for cp in cps:
            cp.wait()

    return pl.pallas_call(
        body, name=name, in_specs=[ANY] * n, out_specs=[ANY] * n,
        out_shape=[_sds((4, BIG[w][1], BIG[w][2] // 2, BIG[w][3]), F32) for w in idx],
        scratch_shapes=[pltpu.SemaphoreType.DMA((n,)), pltpu.SemaphoreType.DMA((n,))],
    )(*gs)


def _pair_start(gs, idx, name):
    n = len(idx)

    def body(*refs):
        src, land = refs[:n], refs[n:2 * n]
        send_sems, recv_sems = refs[2 * n], refs[2 * n + 1]
        token = refs[4 * n + 2]
        x, y, c, _ = _place()
        for i in range(n):
            h = BIG[idx[i]][2] // 2
            _remote(src[i].at[:, :, pl.ds((1 - c) * h, h), :], land[i], send_sems.at[i], recv_sems.at[i],
                    (x, y, 1 - c)).start()
        token[...] = jnp.zeros(TOKEN, F32)

    lands = [lax.empty((4, BIG[w][1], BIG[w][2] // 2, BIG[w][3]), F32) for w in idx]
    arrays = list(gs) + lands
    outs = pl.pallas_call(
        body, name=name, in_specs=[HBM] * (2 * n),
        out_specs=(SEM, SEM) + (HBM,) * (2 * n) + (pl.BlockSpec(memory_space=pltpu.VMEM),),
        out_shape=(pltpu.SemaphoreType.DMA((n,)), pltpu.SemaphoreType.DMA((n,)))
        + tuple(pltpu.HBM(a.shape, a.dtype) for a in arrays) + (_sds(TOKEN, F32),),
        input_output_aliases={i: 2 + i for i in range(2 * n)},
        compiler_params=pltpu.CompilerParams(has_side_effects=EFFECT),
    )(*[pltpu.with_memory_space_constraint(a, pltpu.HBM) for a in arrays])
    return outs[0], outs[1], list(outs[2:2 + n]), list(outs[2 + n:2 + 2 * n]), outs[2 + 2 * n]


def _pair_wait(send_sems, recv_sems, gs, lands, after, idx, name):
    n = len(idx)

    def body(*refs):
        src, land = refs[:n], refs[n:2 * n]
        send_sems, recv_sems = refs[2 * n], refs[2 * n + 1]
        x, y, c, _ = _place()
        for i in range(n):
            h = BIG[idx[i]][2] // 2
            cp = _remote(src[i].at[:, :, pl.ds((1 - c) * h, h), :], land[i], send_sems.at[i], recv_sems.at[i],
                         (x, y, 1 - c))
            cp.wait_send()
            cp.wait_recv()

    arrays = list(gs) + list(lands)
    outs = pl.pallas_call(
        body, name=name, in_specs=[HBM] * (2 * n) + [SEM, SEM] + [ANY] * len(after), out_specs=(HBM,) * (2 * n),
        out_shape=tuple(pltpu.HBM(a.shape, a.dtype) for a in arrays),
        input_output_aliases={i: i for i in range(2 * n)},
        compiler_params=pltpu.CompilerParams(has_side_effects=EFFECT),
    )(*arrays, send_sems, recv_sems, *after)
    return list(outs[:n]), list(outs[n:])


def _pair_sum(place, g, r1, i):
    _, k, r, cdim = BIG[i]
    h = r // 2

    def body(pref, g_ref, r_ref, o_ref):
        o_ref[...] = (g_ref[...] + r_ref[...]).astype(BF16)

    grid_spec = pltpu.PrefetchScalarGridSpec(
        num_scalar_prefetch=1, grid=(4, k),
        in_specs=[pl.BlockSpec((1, 1, h, cdim), lambda s, t, pref: (s, t, pref[0], 0)),
                  pl.BlockSpec((1, 1, h, cdim), lambda s, t, pref: (s, t, 0, 0))],
        out_specs=pl.BlockSpec((1, 1, h, cdim), lambda s, t, pref: (s, t, 0, 0)))
    return pl.pallas_call(
        body, name=f"pair_sum_{BIG[i][0]}", grid_spec=grid_spec, out_shape=_sds((4, k, h, cdim), BF16),
        compiler_params=_params(("parallel", "parallel")),
    )(place, g, r1)


def _chip_start(ps, idx, name):
    n = len(idx)

    def body(*refs):
        src, land = refs[:n], refs[n:2 * n]
        send_sems, recv_sems = refs[2 * n], refs[2 * n + 1]
        token = refs[4 * n + 2]
        x, y, c, chips = _place()
        for j, (px, py) in enumerate(chips):
            for i in range(n):
                _remote(src[i].at[2 * px + py], land[i].at[j], send_sems.at[j * n + i], recv_sems.at[j * n + i],
                        (px, py, c)).start()
        token[...] = jnp.zeros(TOKEN, F32)

    lands = [lax.empty((3,) + p.shape[1:], BF16) for p in ps]
    outs = pl.pallas_call(
        body, name=name, in_specs=[HBM] * (2 * n),
        out_specs=(SEM, SEM) + (HBM,) * (2 * n) + (pl.BlockSpec(memory_space=pltpu.VMEM),),
        out_shape=(pltpu.SemaphoreType.DMA((3 * n,)), pltpu.SemaphoreType.DMA((3 * n,)))
        + tuple(pltpu.HBM(a.shape, a.dtype) for a in list(ps) + lands) + (_sds(TOKEN, F32),),
        input_output_aliases={i: 2 + i for i in range(2 * n)},
        compiler_params=pltpu.CompilerParams(has_side_effects=EFFECT),
    )(*[pltpu.with_memory_space_constraint(a, pltpu.HBM) for a in list(ps) + lands])
    return outs[0], outs[1], list(outs[2:2 + n]), list(outs[2 + n:2 + 2 * n]), outs[2 + 2 * n]


def _chip_wait(send_sems, recv_sems, ps, lands, after, idx, name):
    n = len(idx)

    def body(*refs):
        src, land = refs[:n], refs[n:2 * n]
        send_sems, recv_sems = refs[2 * n], refs[2 * n + 1]
        x, y, c, chips = _place()
        for j, (px, py) in enumerate(chips):
            for i in range(n):
                cp = _remote(src[i].at[2 * px + py], land[i].at[j], send_sems.at[j * n + i], recv_sems.at[j * n + i],
                             (px, py, c))
                cp.wait_send()
                cp.wait_recv()

    arrays = list(ps) + list(lands)
    outs = pl.pallas_call(
        body, name=name, in_specs=[HBM] * (2 * n) + [SEM, SEM] + [ANY] * len(after), out_specs=(HBM,) * (2 * n),
        out_shape=tuple(pltpu.HBM(a.shape, a.dtype) for a in arrays),
        input_output_aliases={i: i for i in range(2 * n)},
        compiler_params=pltpu.CompilerParams(has_side_effects=EFFECT),
    )(*arrays, send_sems, recv_sems, *after)
    return list(outs[n:])


def _chip_sum(place, g, r1, r2, i):
    _, k, r, cdim = BIG[i]
    h = r // 2

    def body(pref, g_ref, r1_ref, r2_ref, o_ref):
        acc = g_ref[0, 0] + r1_ref[0, 0]
        for j in range(3):
            acc = acc + r2_ref[j, 0].astype(F32)
        o_ref[0] = acc

    grid_spec = pltpu.PrefetchScalarGridSpec(
        num_scalar_prefetch=1, grid=(k,),
        in_specs=[pl.BlockSpec((1, 1, h, cdim), lambda t, pref: (pref[1], t, pref[0], 0)),
                  pl.BlockSpec((1, 1, h, cdim), lambda t, pref: (pref[1], t, 0, 0)),
                  pl.BlockSpec((3, 1, h, cdim), lambda t, pref: (0, t, 0, 0))],
        out_specs=pl.BlockSpec((1, h, cdim), lambda t, pref: (t, pref[0], 0)))
    return pl.pallas_call(
        body, name=f"chip_sum_{BIG[i][0]}", grid_spec=grid_spec, out_shape=_sds((k, r, cdim), F32),
        compiler_params=_params(("parallel",)),
    )(place, g, r1, r2)


def _pair_gather(hs, idx, name):
    n = len(idx)

    def body(*refs):
        dst = refs[n:2 * n]
        send_sems, recv_sems = refs[2 * n:]
        x, y, c, _ = _place()
        cps = []
        for i in range(n):
            mine = _half(dst[i], idx[i], c)
            cps.append(_remote(mine, mine, send_sems.at[i], recv_sems.at[i], (x, y, 1 - c)))
            cps[-1].start()
        for i in range(n):
            theirs = _half(dst[i], idx[i], 1 - c)
            _remote(theirs, theirs, send_sems.at[i], recv_sems.at[i], (x, y, 1 - c)).wait_recv()
        for cp in cps:
            cp.wait_send()

    return pl.pallas_call(
        body, name=name, in_specs=[ANY] * n, out_specs=[ANY] * n,
        out_shape=[_sds(BIG[w][1:], F32) for w in idx],
        input_output_aliases={i: i for i in range(n)},
        scratch_shapes=[pltpu.SemaphoreType.DMA((n,)), pltpu.SemaphoreType.DMA((n,))],
    )(*hs)


SMALL_ROWS = 40


def _all_reduce_small(pack, after):
    def body(p_ref, o_ref, slots, send_sems, recv_sems):
        x, y, c, _ = _place()
        me = 4 * x + 2 * y + c
        cps = []
        for r in range(1, 8):
            peer = (x if not r & 4 else 1 - x, y if not r & 2 else 1 - y, c if not r & 1 else 1 - c)
            cps.append(_remote(p_ref, slots.at[r], send_sems.at[r - 1], recv_sems.at[r - 1], peer))
            cps[-1].start()
        slots[0] = p_ref[...]
        for cp in cps:
            cp.wait()
        acc = slots[me]
        for dev in range(1, 8):
            acc = acc + slots[jnp.bitwise_xor(me, dev)]
        o_ref[...] = acc

    vm = pl.BlockSpec(memory_space=pltpu.VMEM)
    return pl.pallas_call(
        functools.partial(_skip_arg, body, 1), name="all_reduce_small", in_specs=[vm, ANY], out_specs=vm,
        out_shape=_sds((SMALL_ROWS, D), F32),
        scratch_shapes=[pltpu.VMEM((8, SMALL_ROWS, D), F32), pltpu.SemaphoreType.DMA((7,)),
                        pltpu.SemaphoreType.DMA((7,))],
    )(pack, after)


def _adamw_math(w, g, m, v):
    m = ADAM_B1 * m + (1.0 - ADAM_B1) * g
    v = ADAM_B2 * v + (1.0 - ADAM_B2) * (g * g)
    m_hat = m / (1.0 - ADAM_B1 ** ADAM_STEP)
    v_hat = v / (1.0 - ADAM_B2 ** ADAM_STEP)
    delta = -ADAM_LR * (m_hat / (jnp.sqrt(v_hat) + ADAM_EPS) + ADAM_WD * w)
    return delta, m, v


def _adamw_big(w, g, m, v, i):
    _, k, r, cdim = BIG[i]
    nblk = 4 if k == 1 else 1

    def body(w_ref, g_ref, m_ref, v_ref, d_ref, nm_ref, nv_ref):
        d_ref[...], nm_ref[...], nv_ref[...] = _adamw_math(w_ref[...], g_ref[...], m_ref[...], v_ref[...])

    spec = pl.BlockSpec((1, r // nblk, cdim), lambda t, b: (t, b, 0))
    return pl.pallas_call(
        body, name=f"adamw_{BIG[i][0]}", grid=(k, nblk), in_specs=[spec] * 4, out_specs=[spec] * 3,
        out_shape=[_sds((k, r, cdim), F32)] * 3,
        compiler_params=_params(("parallel", "parallel")),
    )(w, g, m, v)


def _adamw_small(ws, gs, ms, vs):
    n = len(ws)

    def body(*refs):
        for i in range(n):
            w_ref, g_ref, m_ref, v_ref = refs[i], refs[n + i], refs[2 * n + i], refs[3 * n + i]
            d, nm, nv = _adamw_math(w_ref[...], g_ref[...], m_ref[...], v_ref[...])
            refs[4 * n + i][...] = d
            refs[5 * n + i][...] = nm
            refs[6 * n + i][...] = nv

    specs = [_full(w.shape) for w in ws]
    outs = pl.pallas_call(
        body, name="adamw_small", grid=(1,), in_specs=specs * 4, out_specs=specs * 3,
        out_shape=[_sds(w.shape, F32) for w in ws] * 3,
        compiler_params=_params(("arbitrary",)),
    )(*ws, *gs, *ms, *vs)
    return outs[:n], outs[n:2 * n], outs[2 * n:]


def _pad_rows(a, rows):
    return jnp.pad(a, ((0, rows - a.shape[0]), (0, 0)))


def kernel(x, mem, positions, norm_g, mem_norm_g, w_mem_kv, attn_w_in, attn_w_out, conv_w_in, conv_w, conv_w_out, final_g, loss_target, m_norm_g, m_mem_norm_g, m_w_mem_kv, m_attn_w_in, m_attn_w_out, m_conv_w_in, m_conv_w, m_conv_w_out, m_final_g, v_norm_g, v_mem_norm_g, v_w_mem_kv, v_attn_w_in, v_attn_w_out, v_conv_w_in, v_conv_w, v_conv_w_out, v_final_g):
    mx, my, mc = lax.axis_index("x"), lax.axis_index("y"), lax.axis_index("c")
    place = jnp.stack([mc, 2 * mx + my]).astype(jnp.int32)

    w_big = [w_mem_kv, attn_w_in, attn_w_out, conv_w_in, conv_w_out]
    m_big = [m_w_mem_kv, m_attn_w_in, m_attn_w_out, m_conv_w_in, m_conv_w_out]
    v_big = [v_w_mem_kv, v_attn_w_in, v_attn_w_out, v_conv_w_in, v_conv_w_out]
    wb = _cast_weights(place, w_big)
    first, rest = (1,), (0, 2)
    a1_send, a1_recv, a1_bufs, a1_token = _gather_start([wb[i] for i in first], place, first, "gather_a1_start")
    a2_send, a2_recv, a2_bufs, a2_token = _gather_start([wb[i] for i in rest], a1_token, rest, "gather_a2_start")
    gb_send, gb_recv, gb_bufs, gb_token = _gather_start([wb[i] for i in LAYER_B], a2_token, LAYER_B, "gather_b_start")

    xs, tgt = x[0], loss_target[0]
    g0, g1 = norm_g[0:1], norm_g[1:2]
    rc, rs1, rs2 = _rope_tables(positions[0].astype(F32).reshape(S, 1), gb_token)
    a1_bufs = _gather_wait(a1_send, a1_recv, a1_bufs, [rc], first, "gather_a1_wait")
    w_in_a = _gather_forward(a1_bufs, first, "gather_a1_forward")[0].reshape(4, D, SH_A)
    hn0, q, k, v, qm0, z0 = _in_proj_a(xs, g0, w_in_a, rc, rs1, rs2, gb_token)
    a2_bufs = _gather_wait(a2_send, a2_recv, a2_bufs, [q], rest, "gather_a2_wait")
    wkv_f, w_out_a = _gather_forward(a2_bufs, rest, "gather_a2_forward")
    w_out_a = w_out_a.reshape(4, BR_A, SH_O)
    memn, kv = _mem_fwd(mem[0], mem_norm_g, wkv_f)
    fwd = [_attn_fwd(q, k, v, g) for g in range(3)]
    os_, ls, lss = [f[0] for f in fwd], [f[1] for f in fwd], [f[2] for f in fwd]
    h1 = _attn_out(os_, ls, qm0, kv[0], z0, xs, w_out_a)

    gb_bufs = _gather_wait(gb_send, gb_recv, gb_bufs, [h1], LAYER_B, "gather_b_wait")
    w_in_b, w_out_b, cw_f = _gather_forward(gb_bufs, LAYER_B, "gather_b_forward", _pad_rows(conv_w[0], CW_ROWS))
    w_in_b = w_in_b.reshape(4, D, SH_B)
    w_out_b = w_out_b.reshape(BR_B, D)
    cw8 = cw_f.transpose(1, 0, 2).reshape(CW_ROWS, D)
    hn1, bg, cg, u, qm1, z1 = _in_proj_b(h1, g1, w_in_b)
    dh2, loss_part, dfg = _conv_out_loss(bg, cg, u, cw8, qm1, kv[1], z1, h1, w_out_b, final_g.reshape(1, D), tgt)

    dproj_b, dw_out_b, dcw, dkv1 = _conv_bwd(dh2, bg, cg, u, cw8, qm1, kv[1], z1, w_out_b)
    dw_in_b = _w_in_grad(hn1, dproj_b, IN_B, "w_in_b_grad")
    gs_b = [dw_in_b.reshape(4, 1, D, SH_B), dw_out_b.reshape(4, 1, BR_B // 4, D)]
    pb_send, pb_recv, gs_b, pb_land, pb_token = _pair_start(gs_b, LAYER_B, "pair_b_start")
    dh1, dg1 = _in_proj_bwd(dproj_b, w_in_b, h1, g1, dh2, pb_token, IN_B, "in_proj_b_bwd")
    gs_b, r1_b = _pair_wait(pb_send, pb_recv, gs_b, pb_land, [dh1], LAYER_B, "pair_b_wait")
    ps_b = [_pair_sum(place, gs_b[i], r1_b[i], w) for i, w in enumerate(LAYER_B)]
    cb_send, cb_recv, cb_src, cb_land, cb_token = _chip_start(ps_b, LAYER_B, "chip_b_start")

    outs = _attn_out_bwd(dh1, os_, ls, qm0, kv[0], z0, w_out_a, cb_token)
    dos, dds, dqm, dz, dw_out_a, dkv0 = outs[0:3], outs[3:6], outs[6], outs[7], outs[8], outs[9]
    bwd = [_attn_bwd(q, k, v, dos[g], lss[g], dds[g], g) for g in range(3)]
    dproj_a = _qkv_bwd([b[0] for b in bwd], [b[1] for b in bwd], [b[2] for b in bwd], dqm, dz, rc, rs1, rs2)
    dw_in_a = _w_in_grad(hn0, dproj_a, IN_A, "w_in_a_grad")
    dwkv, dmg = _mem_bwd(mem[0], mem_norm_g, memn, wkv_f, dkv0, dkv1)

    r2_b = _chip_wait(cb_send, cb_recv, cb_src, cb_land, [dw_in_a, dwkv], LAYER_B, "chip_b_wait")
    hs_b = [_chip_sum(place, gs_b[i], r1_b[i], r2_b[i], w) for i, w in enumerate(LAYER_B)]

    gs_a = [dwkv, dw_in_a.reshape(4, 1, D, SH_A), dw_out_a.reshape(4, 1, BR_A, SH_O)]
    r1_a = _pair_exchange(gs_a, LAYER_A, "pair_exchange_a")
    ps_a = [_pair_sum(place, gs_a[i], r1_a[i], w) for i, w in enumerate(LAYER_A)]
    ca_send, ca_recv, ca_src, ca_land, ca_token = _chip_start(ps_a, LAYER_A, "chip_a_start")

    gx, dg0 = _in_proj_bwd(dproj_a, w_in_a, xs, g0, dh1, ca_token, IN_A, "in_proj_a_bwd")
    g_b = _pair_gather(hs_b, LAYER_B, "pair_gather_b")
    upd_b = [_adamw_big(w_big[w], g_b[i], m_big[w], v_big[w], w) for i, w in enumerate(LAYER_B)]
    r2_a = _chip_wait(ca_send, ca_recv, ca_src, ca_land, [gx, upd_b[0][0], upd_b[1][0]], LAYER_A, "chip_a_wait")

    pack = jnp.concatenate([_pad_rows(jnp.concatenate([dg0, dg1], axis=0), 8), _pad_rows(dmg, 8), _pad_rows(dfg, 8),
                            dcw, _pad_rows(jnp.pad(loss_part, ((0, 0), (0, D - 128))), 8)], axis=0)
    tot = _all_reduce_small(pack, r2_a[0])
    loss = tot[32, 0]
    g_norm, g_memnorm, g_final = tot[0:2], tot[8:10], tot[16]
    g_conv = lax.dynamic_slice(tot, (24, (2 * mx + my) * SH_O), (3, SH_O))
    hs_a = [_chip_sum(place, gs_a[i], r1_a[i], r2_a[i], w) for i, w in enumerate(LAYER_A)]
    g_a = _pair_gather(hs_a, LAYER_A, "pair_gather_a")
    upd_a = [_adamw_big(w_big[w], g_a[i], m_big[w], v_big[w], w) for i, w in enumerate(LAYER_A)]
    g_big = list(g_a) + list(g_b)
    upd = upd_a + upd_b
    sw = [norm_g, mem_norm_g, final_g.reshape(1, D), conv_w[0]]
    sg = [g_norm, g_memnorm, g_final.reshape(1, D), g_conv]
    sm = [m_norm_g, m_mem_norm_g, m_final_g.reshape(1, D), m_conv_w[0]]
    sv = [v_norm_g, v_mem_norm_g, v_final_g.reshape(1, D), v_conv_w[0]]
    sd, snm, snv = _adamw_small(sw, sg, sm, sv)

    def order(norm, memnorm, wkv, w_in_a, w_out_a, w_in_b, conv, w_out_b, final):
        return (norm, memnorm, wkv, w_in_a, w_out_a, w_in_b, conv.reshape(1, 3, SH_O), w_out_b, final.reshape(D))

    grads = order(g_norm, g_memnorm, g_big[0], g_big[1], g_big[2], g_big[3], g_conv, g_big[4], g_final)
    deltas = order(sd[0], sd[1], upd[0][0], upd[1][0], upd[2][0], upd[3][0], sd[3], upd[4][0], sd[2])
    new_m = order(snm[0], snm[1], upd[0][1], upd[1][1], upd[2][1], upd[3][1], snm[3], upd[4][1], snm[2])
    new_v = order(snv[0], snv[1], upd[0][2], upd[1][2], upd[2][2], upd[3][2], snv[3], upd[4][2], snv[2])
    return (loss, gx[None], *grads, *deltas, *new_m, *new_v)
```

```python
import functools

import numpy as np
import jax
import jax.numpy as jnp
from jax import lax
from jax.experimental import pallas as pl
from jax.experimental.pallas import tpu as pltpu

F32 = jnp.float32
BF16 = jnp.bfloat16

S = 2048
D = 1024
TM = 256
NT = S // TM
HD = 64
GW = 512
NQ = 3 * GW
MW = 256
NM = 256
IN_A = 3 * NQ + MW + GW + MW
IN_B = 3 * D + MW + D + MW
BR_A = GW + MW
BR_B = D + MW
SH_A = IN_A // 4
SH_B = IN_B // 4
SH_O = D // 4
QBLK = 128
DILATIONS = (1, 4, 16)
EPS = 1e-6
SCALE = HD ** -0.5
NEG = -1e30
ROPE_THETA = 500000.0

ADAM_LR = 0.001
ADAM_B1 = 0.9
ADAM_B2 = 0.999
ADAM_EPS = 1e-08
ADAM_WD = 0.01
ADAM_STEP = 10

VMEM_LIMIT_BYTES = 60 * 1024 * 1024


def _params(sem=None):
    if sem is None:
        return pltpu.CompilerParams(vmem_limit_bytes=VMEM_LIMIT_BYTES)
    return pltpu.CompilerParams(dimension_semantics=sem, vmem_limit_bytes=VMEM_LIMIT_BYTES)


def _full(shape):
    nd = len(shape)
    return pl.BlockSpec(shape, lambda *_: (0,) * nd)


def _rows(width, tm=TM):
    return pl.BlockSpec((tm, width), lambda i: (i, 0))


def _sds(shape, dtype):
    return jax.ShapeDtypeStruct(shape, dtype)


def _silu_parts(z):
    sig = 1.0 / (1.0 + jnp.exp(-z))
    return z * sig, sig * (1.0 + z * (1.0 - sig))


def _dot(a, b):
    return jnp.dot(a, b, preferred_element_type=F32)


def _dot_nt(a, b):
    return lax.dot_general(a, b, (((1,), (1,)), ((), ())), preferred_element_type=F32)


def _dot_tn(a, b):
    return lax.dot_general(a, b, (((0,), (0,)), ((), ())), preferred_element_type=F32)


def _rope_fwd(t, c, s1, s2):
    return t * c + pltpu.roll(t, 120, 1) * s1 + pltpu.roll(t, 8, 1) * s2


def _rope_bwd(g, c, s1, s2):
    return g * c + pltpu.roll(g * s1, 8, 1) + pltpu.roll(g * s2, 120, 1)


def _mem_attn(qm, kv):
    res = []
    for h in range(MW // HD):
        sl = slice(h * HD, (h + 1) * HD)
        s = _dot_nt(qm[:, sl], kv[:, sl]) * SCALE
        e = jnp.exp(s - jnp.max(s, axis=-1, keepdims=True))
        p = e / jnp.sum(e, axis=-1, keepdims=True)
        res.append((p, _dot(p.astype(BF16), kv[:, MW + h * HD:MW + (h + 1) * HD])))
    return res


def _mem_attn_bwd(dmo, heads, qm, kv, dqm_store, dkv_ref):
    for h, (p, mo) in enumerate(heads):
        sl = slice(h * HD, (h + 1) * HD)
        vs = slice(MW + h * HD, MW + (h + 1) * HD)
        dmo_h = dmo[:, sl]
        dmo_b = dmo_h.astype(BF16)
        dp = _dot_nt(dmo_b, kv[:, vs])
        delta = jnp.sum(dmo_h * mo, axis=-1, keepdims=True)
        ds = (p * (dp - delta) * SCALE).astype(BF16)
        dqm_store(h, _dot(ds, kv[:, sl]))
        dkv_ref[:, sl] += _dot_tn(ds, qm[:, sl])
        dkv_ref[:, vs] += _dot_tn(p.astype(BF16), dmo_b)


def _merge(o_refs, l_refs):
    ls = [r[...] for r in l_refs]
    m = jnp.maximum(jnp.maximum(ls[0], ls[1]), ls[2])
    es = [jnp.exp(l - m) for l in ls]
    inv = 1.0 / (es[0] + es[1] + es[2])
    ws = [e * inv for e in es]
    os_ = [r[...] for r in o_refs]
    mix = ws[0] * os_[0] + ws[1] * os_[1] + ws[2] * os_[2]
    return ws, mix


def _conv_taps(cg, u, cgp, up, first):
    a = cg * u
    ap = jnp.where(first, 0.0, cgp * up)
    row = lax.broadcasted_iota(jnp.int32, a.shape, 0)
    a1 = jnp.where(row == 0, ap[7:8, :], pltpu.roll(a, 1, 0))
    a2 = jnp.where(row == 0, ap[6:7, :], jnp.where(row == 1, ap[7:8, :], pltpu.roll(a, 2, 0)))
    return a, a1, a2


def _rope_tables(posf, after):
    half = 8
    invf = np.float32(ROPE_THETA) ** (-np.arange(half, dtype=np.float32) * np.float32(2.0 / 16))
    lane = np.arange(128)
    table = np.where((lane % HD) < 16, invf[lane % half], 0.0).astype(np.float32)[None, :]

    def body(pos_ref, invf_ref, c_ref, s1_ref, s2_ref):
        ang = pos_ref[...] * invf_ref[...]
        jm = lax.broadcasted_iota(jnp.int32, ang.shape, 1) & (HD - 1)
        cs = jnp.cos(ang)
        sn = jnp.sin(ang)
        c_ref[...] = jnp.where(jm < 16, cs, 1.0)
        s1_ref[...] = jnp.where(jm < 8, -sn, 0.0)
        s2_ref[...] = jnp.where((jm >= 8) & (jm < 16), sn, 0.0)

    out = _sds((S, 128), F32)
    return pl.pallas_call(
        functools.partial(_skip_arg, body, 2), name="rope_tables", grid=(NT,),
        in_specs=[_rows(1), _full((1, 128)), pl.BlockSpec(memory_space=pl.ANY)],
        out_specs=[_rows(128)] * 3, out_shape=[out] * 3,
        compiler_params=_params(("parallel",)),
    )(posf, jnp.asarray(table), after)


def _in_proj_a(x, g0, w_in, c, s1, s2, after):
    def body(x_ref, g_ref, w_ref, c_ref, s1_ref, s2_ref, hn_ref, q_ref, k_ref, v_ref, qm_ref, z_ref, proj):
        xf = x_ref[...]
        hn = xf * lax.rsqrt(jnp.mean(xf * xf, axis=-1, keepdims=True) + EPS) * g_ref[...]
        hb = hn.astype(BF16)
        hn_ref[...] = hb
        for s in range(4):
            proj[:, s * SH_A:(s + 1) * SH_A] = _dot(hb, w_ref[s])
        cc, a1, a2 = c_ref[...], s1_ref[...], s2_ref[...]
        for j in range(NQ // 128):
            q_ref[:, j * 128:(j + 1) * 128] = (
                _rope_fwd(proj[:, j * 128:(j + 1) * 128], cc, a1, a2) * SCALE).astype(BF16)
            k_ref[:, j * 128:(j + 1) * 128] = _rope_fwd(
                proj[:, NQ + j * 128:NQ + (j + 1) * 128], cc, a1, a2).astype(BF16)
        v_ref[...] = proj[:, 2 * NQ:3 * NQ].astype(BF16)
        qm_ref[...] = proj[:, 3 * NQ:3 * NQ + MW].astype(BF16)
        z_ref[...] = proj[:, 3 * NQ + MW:]

    return pl.pallas_call(
        functools.partial(_skip_arg, body, 6), name="in_proj_a", grid=(NT,),
        in_specs=[_rows(D), _full((1, D)), _full((4, D, SH_A)), _rows(128), _rows(128), _rows(128),
                  pl.BlockSpec(memory_space=pl.ANY)],
        out_specs=[_rows(D), _rows(NQ), _rows(NQ), _rows(NQ), _rows(MW), _rows(BR_A)],
        out_shape=[_sds((S, D), BF16), _sds((S, NQ), BF16), _sds((S, NQ), BF16), _sds((S, NQ), BF16),
                   _sds((S, MW), BF16), _sds((S, BR_A), F32)],
        scratch_shapes=[pltpu.VMEM((TM, IN_A), F32)],
        compiler_params=_params(("parallel",)),
    )(x, g0, w_in, c, s1, s2, after)


def _mem_fwd(mem, mg, wkv):
    def body(mem_ref, mg_ref, w_ref, memn_ref, kv_ref):
        mf = mem_ref[...]
        n = mf * lax.rsqrt(jnp.mean(mf * mf, axis=-1, keepdims=True) + EPS)
        for i in range(2):
            mn = (n * mg_ref[i:i + 1, :]).astype(BF16)
            memn_ref[i] = mn
            acc = _dot(mn[:, 0:NM], w_ref[0, i])
            for s in range(1, 4):
                acc += _dot(mn[:, s * NM:(s + 1) * NM], w_ref[s, i])
            kv_ref[i] = acc.astype(BF16)

    return pl.pallas_call(
        body, name="mem_fwd", grid=(1,),
        in_specs=[_full((NM, D)), _full((2, D)), _full((4, 2, NM, 2 * MW))],
        out_specs=[_full((2, NM, D)), _full((2, NM, 2 * MW))],
        out_shape=[_sds((2, NM, D), BF16), _sds((2, NM, 2 * MW), BF16)],
        compiler_params=_params(("arbitrary",)),
    )(mem, mg, wkv)


def _band_mask(j):
    qi = lax.broadcasted_iota(jnp.int32, (QBLK, 2 * QBLK), 0)
    kj = lax.broadcasted_iota(jnp.int32, (QBLK, 2 * QBLK), 1)
    dist = qi + QBLK - kj
    return (dist >= 0) & (dist <= QBLK) & ((kj >= QBLK) | (j > 0))


LANES = 128
NCHUNK = GW // LANES
FWD_UNROLL = 16
BWD_UNROLL = 4


def _perm_matrix(d):
    n = TM // d
    p = np.zeros((TM, TM), np.float32)
    for r in range(d):
        for i in range(n):
            p[r * n + i, i * d + r] = 1.0
    return p


def _split_dot(p, x, parts):
    hi = x.astype(BF16)
    rem = x - hi.astype(F32)
    lo = rem.astype(BF16)
    both = _dot(p, jnp.concatenate([hi, lo], axis=1))
    acc = both[:, :LANES] + both[:, LANES:]
    if parts == 3:
        acc = acc + _dot(p, (rem - lo.astype(F32)).astype(BF16))
    return acc


def _pair_dot(p, a, b):
    both = _dot(p, jnp.concatenate([a, b], axis=1))
    return both[:, :LANES], both[:, LANES:]


def _tile_to_streams(y, dst, t, d):
    n, ln = TM // d, S // d
    for r in range(d):
        dst[r * ln + t * n:r * ln + (t + 1) * n, :] = y[r * n:(r + 1) * n].astype(dst.dtype)


def _tile_from_streams(src, t, d):
    n, ln = TM // d, S // d
    return jnp.concatenate([src[r * ln + t * n:r * ln + (t + 1) * n, :] for r in range(d)], axis=0)


def _head_masks():
    first = lax.broadcasted_iota(jnp.int32, (TM, LANES), 1) < HD
    return first, jnp.logical_not(first)


def _attn_fwd(q, k, v, g):
    d = DILATIONS[g]
    nb = S // d // QBLK
    perm = _perm_matrix(d)

    def body(q_ref, k_ref, v_ref, p_ref, pt_ref, o_ref, l_ref, ls_ref, q0, q1, ks, vs, os_):
        first, second = _head_masks()
        pm = p_ref[...]
        for t in range(NT):
            rows = slice(t * TM, (t + 1) * TM)
            if d == 1:
                qt = q_ref[rows, :].astype(F32)
            else:
                qt, kt = _pair_dot(pm, q_ref[rows, :], k_ref[rows, :])
                _tile_to_streams(kt, ks, t, d)
                _tile_to_streams(_dot(pm, v_ref[rows, :]), vs, t, d)
            _tile_to_streams(jnp.where(first, qt, 0.0), q0, t, d)
            _tile_to_streams(jnp.where(second, qt, 0.0), q1, t, d)
        kref, vref = (k_ref, v_ref) if d == 1 else (ks, vs)
        oref, lref = (o_ref, l_ref) if d == 1 else (os_, ls_ref)

        def blk(b, carry):
            r0 = pl.multiple_of(b * QBLK, QBLK)
            p0 = pl.multiple_of(jnp.maximum(b - 1, 0) * QBLK, QBLK)
            kk = jnp.concatenate([kref[pl.ds(p0, QBLK), :], kref[pl.ds(r0, QBLK), :]], axis=0)
            vv = jnp.concatenate([vref[pl.ds(p0, QBLK), :], vref[pl.ds(r0, QBLK), :]], axis=0)
            valid = _band_mask(b & (nb - 1))
            acc, den, lse = [], [], []
            for qh in (q0, q1):
                s = jnp.where(valid, _dot_nt(qh[pl.ds(r0, QBLK), :], kk), NEG)
                m = jnp.max(s, axis=-1, keepdims=True)
                e = jnp.exp(s - m)
                l = jnp.sum(e, axis=-1, keepdims=True)
                acc.append(_dot(e.astype(BF16), vv))
                den.append(l)
                lse.append(m + jnp.log(l))
            f = first[:QBLK]
            oref[pl.ds(r0, QBLK), :] = jnp.where(f, acc[0], acc[1]) / jnp.where(f, den[0], den[1])
            lref[pl.ds(r0, QBLK), :] = jnp.where(f, lse[0], lse[1])
            return carry

        lax.fori_loop(0, S // QBLK, blk, 0, unroll=FWD_UNROLL)
        if d > 1:
            ptm = pt_ref[...]
            for t in range(NT):
                rows = slice(t * TM, (t + 1) * TM)
                o_ref[rows, :] = _split_dot(ptm, _tile_from_streams(os_, t, d), 2)
                l_ref[rows, :] = _split_dot(ptm, _tile_from_streams(ls_ref, t, d), 3)

    qkv_spec = pl.BlockSpec((S, LANES), lambda c: (0, g * NCHUNK + c))
    out_spec = pl.BlockSpec((S, LANES), lambda c: (0, c))
    n_out = 2 if d == 1 else 3
    outs = pl.pallas_call(
        body if d > 1 else functools.partial(_drop_arg, body, 7), name=f"attn_fwd_g{g}", grid=(NCHUNK,),
        in_specs=[qkv_spec] * 3 + [_full((TM, TM))] * 2, out_specs=[out_spec] * n_out,
        out_shape=[_sds((S, GW), F32)] * n_out,
        scratch_shapes=[pltpu.VMEM((S, LANES), BF16)] * 4 + [pltpu.VMEM((S, LANES), F32)],
        compiler_params=_params(("parallel",)),
    )(q, k, v, jnp.asarray(perm, BF16), jnp.asarray(perm.T, BF16))
    return (outs[0], outs[1], outs[1]) if d == 1 else tuple(outs)


def _drop_arg(body, pos, *refs):
    return body(*refs[:pos], None, *refs[pos:])


def _attn_out(os_, ls, qm, kv0, z, x, w_out):
    def body(o0, o1, o2, l0, l1, l2, qm_ref, kv_ref, z_ref, x_ref, w_ref, h_ref, ybuf):
        _, mix = _merge((o0, o1, o2), (l0, l1, l2))
        sz, _ = _silu_parts(z_ref[...])
        ybuf[:, :GW] = (mix * sz[:, :GW]).astype(BF16)
        for h, (_, mo) in enumerate(_mem_attn(qm_ref[...], kv_ref[...])):
            sl = slice(GW + h * HD, GW + (h + 1) * HD)
            ybuf[:, sl] = (mo * sz[:, sl]).astype(BF16)
        yb = ybuf[...]
        for s in range(4):
            cs = slice(s * SH_O, (s + 1) * SH_O)
            h_ref[:, cs] = x_ref[:, cs] + _dot(yb, w_ref[s])

    return pl.pallas_call(
        body, name="attn_out", grid=(NT,),
        in_specs=[_rows(GW)] * 6 + [_rows(MW), _full((NM, 2 * MW)), _rows(BR_A), _rows(D), _full((4, BR_A, SH_O))],
        out_specs=_rows(D), out_shape=_sds((S, D), F32),
        scratch_shapes=[pltpu.VMEM((TM, BR_A), BF16)],
        compiler_params=_params(("parallel",)),
    )(*os_, *ls, qm, kv0, z, x, w_out)


def _in_proj_b(h1, g1, w_in):
    def body(x_ref, g_ref, w_ref, hn_ref, bg_ref, cg_ref, u_ref, qm_ref, z_ref, proj):
        xf = x_ref[...]
        hn = xf * lax.rsqrt(jnp.mean(xf * xf, axis=-1, keepdims=True) + EPS) * g_ref[...]
        hb = hn.astype(BF16)
        hn_ref[...] = hb
        for s in range(4):
            proj[:, s * SH_B:(s + 1) * SH_B] = _dot(hb, w_ref[s])
        bg_ref[...] = proj[:, :D]
        cg_ref[...] = proj[:, D:2 * D]
        u_ref[...] = proj[:, 2 * D:3 * D]
        qm_ref[...] = proj[:, 3 * D:3 * D + MW].astype(BF16)
        z_ref[...] = proj[:, 3 * D + MW:]

    return pl.pallas_call(
        body, name="in_proj_b", grid=(NT,),
        in_specs=[_rows(D), _full((1, D)), _full((4, D, SH_B))],
        out_specs=[_rows(D), _rows(D), _rows(D), _rows(D), _rows(MW), _rows(BR_B)],
        out_shape=[_sds((S, D), BF16), _sds((S, D), F32), _sds((S, D), F32), _sds((S, D), F32),
                   _sds((S, MW), BF16), _sds((S, BR_B), F32)],
        scratch_shapes=[pltpu.VMEM((TM, IN_B), F32)],
        compiler_params=_params(("parallel",)),
    )(h1, g1, w_in)


def _prev8(width):
    return pl.BlockSpec((8, width), lambda i: (jnp.maximum(i * (TM // 8) - 1, 0), 0))


def _conv_out_loss(bg, cg, u, cw, qm, kv1, z, h1, w_out, fg, tgt):
    def body(bg_ref, cg_ref, u_ref, cgp_ref, up_ref, cw_ref, qm_ref, kv_ref, z_ref, h_ref, w_ref, fg_ref, t_ref,
             dh_ref, loss_ref, dfg_ref, ybuf):
        i = pl.program_id(0)
        a, a1, a2 = _conv_taps(cg_ref[...], u_ref[...], cgp_ref[...], up_ref[...], i == 0)
        conv = cw_ref[0:1, :] * a2 + cw_ref[1:2, :] * a1 + cw_ref[2:3, :] * a
        sz, _ = _silu_parts(z_ref[...])
        ybuf[:, :D] = (bg_ref[...] * conv * sz[:, :D]).astype(BF16)
        for h, (_, mo) in enumerate(_mem_attn(qm_ref[...], kv_ref[...])):
            sl = slice(D + h * HD, D + (h + 1) * HD)
            ybuf[:, sl] = (mo * sz[:, sl]).astype(BF16)
        h2 = h_ref[...] + _dot(ybuf[...], w_ref[...])
        rstd = lax.rsqrt(jnp.mean(h2 * h2, axis=-1, keepdims=True) + EPS)
        n = h2 * rstd
        fgv = fg_ref[...]
        err = n * fgv - t_ref[...]
        dout = err * (1.0 / D)
        dn = dout * fgv
        dh_ref[...] = rstd * (dn - n * jnp.mean(dn * n, axis=-1, keepdims=True))

        @pl.when(i == 0)
        def _():
            loss_ref[...] = jnp.zeros_like(loss_ref)
            dfg_ref[...] = jnp.zeros_like(dfg_ref)

        loss_ref[...] += jnp.sum(err * err) * (0.5 / D)
        dfg_ref[...] += jnp.sum(dout * n, axis=0, keepdims=True)

    return pl.pallas_call(
        body, name="conv_out_loss", grid=(NT,),
        in_specs=[_rows(D), _rows(D), _rows(D), _prev8(D), _prev8(D), _full((8, D)), _rows(MW),
                  _full((NM, 2 * MW)), _rows(BR_B), _rows(D), _full((BR_B, D)), _full((1, D)), _rows(D)],
        out_specs=[_rows(D), _full((1, 128)), _full((1, D))],
        out_shape=[_sds((S, D), F32), _sds((1, 128), F32), _sds((1, D), F32)],
        scratch_shapes=[pltpu.VMEM((TM, BR_B), BF16)],
        compiler_params=_params(("arbitrary",)),
    )(bg, cg, u, cg, u, cw, qm, kv1, z, h1, w_out, fg, tgt)


def _conv_bwd(dh2, bg, cg, u, cw, qm, kv1, z, w_out):
    rev = lambda i: (NT - 1 - i, 0)
    rows = lambda w: pl.BlockSpec((TM, w), rev)
    prev8 = pl.BlockSpec((8, D), lambda i: (jnp.maximum((NT - 1 - i) * (TM // 8) - 1, 0), 0))

    def body(dh_ref, bg_ref, cg_ref, u_ref, cgp_ref, up_ref, cw_ref, qm_ref, kv_ref, z_ref, w_ref,
             dproj_ref, dw_ref, dcw_ref, dkv_ref, ybuf, carry):
        i = pl.program_id(0)

        @pl.when(i == 0)
        def _():
            dw_ref[...] = jnp.zeros_like(dw_ref)
            dcw_ref[...] = jnp.zeros_like(dcw_ref)
            dkv_ref[...] = jnp.zeros_like(dkv_ref)
            carry[...] = jnp.zeros_like(carry)

        bgv, cgv, uv = bg_ref[...], cg_ref[...], u_ref[...]
        a, a1, a2 = _conv_taps(cgv, uv, cgp_ref[...], up_ref[...], i == NT - 1)
        w0, w1, w2 = cw_ref[0:1, :], cw_ref[1:2, :], cw_ref[2:3, :]
        conv = w0 * a2 + w1 * a1 + w2 * a
        mix = bgv * conv
        zv = z_ref[...]
        sz, dsz = _silu_parts(zv)
        qmv, kvv = qm_ref[...], kv_ref[...]
        heads = _mem_attn(qmv, kvv)
        ybuf[:, :D] = (mix * sz[:, :D]).astype(BF16)
        for h, (_, mo) in enumerate(heads):
            sl = slice(D + h * HD, D + (h + 1) * HD)
            ybuf[:, sl] = (mo * sz[:, sl]).astype(BF16)
        dhb = dh_ref[...].astype(BF16)
        dw_ref[...] += _dot_tn(ybuf[...], dhb)
        dy = _dot_nt(dhb, w_ref[...])
        dcat = dy * sz
        dproj_ref[:, 3 * D + MW:3 * D + MW + D] = (dy[:, :D] * mix * dsz[:, :D]).astype(BF16)
        for h, (_, mo) in enumerate(heads):
            sl = slice(D + h * HD, D + (h + 1) * HD)
            dproj_ref[:, 3 * D + MW + D + h * HD:3 * D + MW + D + (h + 1) * HD] = (
                dy[:, sl] * mo * dsz[:, sl]).astype(BF16)
        dmix = dcat[:, :D]
        dproj_ref[:, :D] = (dmix * conv).astype(BF16)
        dc = dmix * bgv
        nxt = carry[...]
        row = lax.broadcasted_iota(jnp.int32, dc.shape, 0)
        dc1 = jnp.where(row == TM - 1, nxt[0:1, :], pltpu.roll(dc, TM - 1, 0))
        dc2 = jnp.where(row == TM - 2, nxt[0:1, :], jnp.where(row == TM - 1, nxt[1:2, :], pltpu.roll(dc, TM - 2, 0)))
        carry[...] = dc[0:8, :]
        da = w2 * dc + w1 * dc1 + w0 * dc2
        dproj_ref[:, D:2 * D] = (da * uv).astype(BF16)
        dproj_ref[:, 2 * D:3 * D] = (da * cgv).astype(BF16)
        dcw_ref[0:1, :] += jnp.sum(dc * a2, axis=0, keepdims=True)
        dcw_ref[1:2, :] += jnp.sum(dc * a1, axis=0, keepdims=True)
        dcw_ref[2:3, :] += jnp.sum(dc * a, axis=0, keepdims=True)

        def dqm_store(h, val):
            dproj_ref[:, 3 * D + h * HD:3 * D + (h + 1) * HD] = val.astype(BF16)

        _mem_attn_bwd(dcat[:, D:], heads, qmv, kvv, dqm_store, dkv_ref)

    return pl.pallas_call(
        body, name="conv_bwd", grid=(NT,),
        in_specs=[rows(D), rows(D), rows(D), rows(D), prev8, prev8, _full((8, D)), rows(MW),
                  _full((NM, 2 * MW)), rows(BR_B), _full((BR_B, D))],
        out_specs=[rows(IN_B), _full((BR_B, D)), _full((8, D)), _full((NM, 2 * MW))],
        out_shape=[_sds((S, IN_B), BF16), _sds((BR_B, D), F32), _sds((8, D), F32), _sds((NM, 2 * MW), F32)],
        scratch_shapes=[pltpu.VMEM((TM, BR_B), BF16), pltpu.VMEM((8, D), F32)],
        compiler_params=_params(("arbitrary",)),
    )(dh2, bg, cg, u, cg, u, cw, qm, kv1, z, w_out)


def _in_proj_bwd(dproj, w_in, xin, g, dres, after, width, name):
    sh = width // 4

    def body(dp_ref, w_ref, x_ref, g_ref, dr_ref, dx_ref, dg_ref):
        i = pl.program_id(0)
        dhn = _dot_nt(dp_ref[:, 0:sh], w_ref[0])
        for s in range(1, 4):
            dhn += _dot_nt(dp_ref[:, s * sh:(s + 1) * sh], w_ref[s])
        xf = x_ref[...]
        rstd = lax.rsqrt(jnp.mean(xf * xf, axis=-1, keepdims=True) + EPS)
        n = xf * rstd
        dn = dhn * g_ref[...]
        dx_ref[...] = dr_ref[...] + rstd * (dn - n * jnp.mean(dn * n, axis=-1, keepdims=True))

        @pl.when(i == 0)
        def _():
            dg_ref[...] = jnp.zeros_like(dg_ref)

        dg_ref[...] += jnp.sum(dhn * n, axis=0, keepdims=True)

    return pl.pallas_call(
        functools.partial(_skip_arg, body, 5), name=name, grid=(NT,),
        in_specs=[_rows(width), _full((4, D, sh)), _rows(D), _full((1, D)), _rows(D), pl.BlockSpec(memory_space=pl.ANY)],
        out_specs=[_rows(D), _full((1, D))],
        out_shape=[_sds((S, D), F32), _sds((1, D), F32)],
        compiler_params=_params(("arbitrary",)),
    )(dproj, w_in, xin, g, dres, after)


def _w_in_grad(hn, dproj, width, name):
    sh = width // 4

    def body(hn_ref, dp_ref, dw_ref):
        dw_ref[0] = _dot_tn(hn_ref[...], dp_ref[...])

    return pl.pallas_call(
        body, name=name, grid=(4,),
        in_specs=[_full((S, D)), pl.BlockSpec((S, sh), lambda s: (0, s))],
        out_specs=pl.BlockSpec((1, D, sh), lambda s: (s, 0, 0)),
        out_shape=_sds((4, D, sh), F32),
        compiler_params=_params(("parallel",)),
    )(hn, dproj)


def _attn_out_bwd(dh1, os_, ls, qm, kv0, z, w_out, after):
    ones_bd = np.kron(np.eye(GW // HD, dtype=np.float32), np.ones((HD, HD), np.float32))

    def body(dh_ref, o0, o1, o2, l0, l1, l2, qm_ref, kv_ref, z_ref, w_ref, bd_ref,
             do0, do1, do2, dd0, dd1, dd2, dqm_ref, dz_ref, dw_ref, dkv_ref, ybuf):
        i = pl.program_id(0)

        @pl.when(i == 0)
        def _():
            dw_ref[...] = jnp.zeros_like(dw_ref)
            dkv_ref[...] = jnp.zeros_like(dkv_ref)

        ws, mix = _merge((o0, o1, o2), (l0, l1, l2))
        sz, dsz = _silu_parts(z_ref[...])
        qmv, kvv = qm_ref[...], kv_ref[...]
        heads = _mem_attn(qmv, kvv)
        ybuf[:, :GW] = (mix * sz[:, :GW]).astype(BF16)
        for h, (_, mo) in enumerate(heads):
            sl = slice(GW + h * HD, GW + (h + 1) * HD)
            ybuf[:, sl] = (mo * sz[:, sl]).astype(BF16)
        yb = ybuf[...]
        dh = dh_ref[...]
        dy = None
        for s in range(4):
            dhb = dh[:, s * SH_O:(s + 1) * SH_O].astype(BF16)
            dw_ref[s] += _dot_tn(yb, dhb)
            part = _dot_nt(dhb, w_ref[s])
            dy = part if dy is None else dy + part
        dcat = dy * sz
        dz_ref[:, :GW] = (dy[:, :GW] * mix * dsz[:, :GW]).astype(BF16)
        for h, (_, mo) in enumerate(heads):
            sl = slice(GW + h * HD, GW + (h + 1) * HD)
            dz_ref[:, sl] = (dy[:, sl] * mo * dsz[:, sl]).astype(BF16)
        dmix = dcat[:, :GW]
        prod = dmix * mix
        hi = prod.astype(BF16)
        lo = (prod - hi.astype(F32)).astype(BF16)
        bd = bd_ref[...]
        tot = _dot(hi, bd) + _dot(lo, bd)
        for w, do_ref, dd_ref in zip(ws, (do0, do1, do2), (dd0, dd1, dd2)):
            do_ref[...] = (w * dmix).astype(BF16)
            dd_ref[...] = w * tot

        def dqm_store(h, val):
            dqm_ref[:, h * HD:(h + 1) * HD] = val.astype(BF16)

        _mem_attn_bwd(dcat[:, GW:], heads, qmv, kvv, dqm_store, dkv_ref)

    return pl.pallas_call(
        functools.partial(_skip_arg, body, 12), name="attn_out_bwd", grid=(NT,),
        in_specs=[_rows(D)] + [_rows(GW)] * 6 + [_rows(MW), _full((NM, 2 * MW)), _rows(BR_A),
                                                   _full((4, BR_A, SH_O)), _full((GW, GW)),
                                                   pl.BlockSpec(memory_space=pl.ANY)],
        out_specs=[_rows(GW)] * 6 + [_rows(MW), _rows(BR_A), _full((4, BR_A, SH_O)), _full((NM, 2 * MW))],
        out_shape=[_sds((S, GW), BF16)] * 3 + [_sds((S, GW), F32)] * 3 + [
            _sds((S, MW), BF16), _sds((S, BR_A), BF16), _sds((4, BR_A, SH_O), F32), _sds((NM, 2 * MW), F32)],
        scratch_shapes=[pltpu.VMEM((TM, BR_A), BF16)],
        compiler_params=_params(("arbitrary",)),
    )(dh1, *os_, *ls, qm, kv0, z, w_out, jnp.asarray(ones_bd, dtype=BF16), after)


def _attn_bwd(q, k, v, do, lse_s, dd, g):
    d = DILATIONS[g]
    nb = S // d // QBLK
    perm = _perm_matrix(d)

    def body(q_ref, k_ref, v_ref, do_ref, l_ref, dd_ref, p_ref, pt_ref, dq_ref, dk_ref, dv_ref,
             q0, q1, g0, g1, ks, vs, dds, dqs, dks, dvs):
        first, second = _head_masks()
        pm = p_ref[...]
        for t in range(NT):
            rows = slice(t * TM, (t + 1) * TM)
            if d == 1:
                qt = q_ref[rows, :].astype(F32)
                gt = do_ref[rows, :].astype(F32)
            else:
                qt, gt = _pair_dot(pm, q_ref[rows, :], do_ref[rows, :])
                kt, vt = _pair_dot(pm, k_ref[rows, :], v_ref[rows, :])
                _tile_to_streams(kt, ks, t, d)
                _tile_to_streams(vt, vs, t, d)
                _tile_to_streams(_split_dot(pm, dd_ref[rows, :], 2), dds, t, d)
            _tile_to_streams(jnp.where(first, qt, 0.0), q0, t, d)
            _tile_to_streams(jnp.where(second, qt, 0.0), q1, t, d)
            _tile_to_streams(jnp.where(first, gt, 0.0), g0, t, d)
            _tile_to_streams(jnp.where(second, gt, 0.0), g1, t, d)
        kref, vref, ddref = (k_ref, v_ref, dd_ref) if d == 1 else (ks, vs, dds)
        dqref, dkref, dvref = (dq_ref, dk_ref, dv_ref) if d == 1 else (dqs, dks, dvs)
        dkref[...] = jnp.zeros_like(dkref)
        dvref[...] = jnp.zeros_like(dvref)

        def blk(b, carry):
            r0 = pl.multiple_of(b * QBLK, QBLK)
            p0 = pl.multiple_of(jnp.maximum(b - 1, 0) * QBLK, QBLK)
            kk = jnp.concatenate([kref[pl.ds(p0, QBLK), :], kref[pl.ds(r0, QBLK), :]], axis=0)
            vv = jnp.concatenate([vref[pl.ds(p0, QBLK), :], vref[pl.ds(r0, QBLK), :]], axis=0)
            lb = l_ref[pl.ds(r0, QBLK), :]
            ddb = ddref[pl.ds(r0, QBLK), :]
            valid = _band_mask(b & (nb - 1))
            dqh, dkk, dvv = [], None, None
            for h, (qh, gh) in enumerate(((q0, g0), (q1, g1))):
                qb = qh[pl.ds(r0, QBLK), :]
                gb = gh[pl.ds(r0, QBLK), :]
                s = _dot_nt(qb, kk)
                p = jnp.where(valid, jnp.exp(s - lb[:, h * HD:h * HD + 1]), 0.0)
                dp = _dot_nt(gb, vv)
                ds = (p * (dp - ddb[:, h * HD:h * HD + 1])).astype(BF16)
                dqh.append(_dot(ds, kk))
                tk = _dot_tn(ds, qb)
                tv = _dot_tn(p.astype(BF16), gb)
                dkk = tk if dkk is None else dkk + tk
                dvv = tv if dvv is None else dvv + tv
            dqref[pl.ds(r0, QBLK), :] = jnp.where(first[:QBLK], dqh[0], dqh[1])
            dkref[pl.ds(p0, QBLK), :] += dkk[:QBLK]
            dkref[pl.ds(r0, QBLK), :] += dkk[QBLK:]
            dvref[pl.ds(p0, QBLK), :] += dvv[:QBLK]
            dvref[pl.ds(r0, QBLK), :] += dvv[QBLK:]
            return carry

        lax.fori_loop(0, S // QBLK, blk, 0, unroll=BWD_UNROLL)
        if d > 1:
            ptm = pt_ref[...]
            for t in range(NT):
                rows = slice(t * TM, (t + 1) * TM)
                dq_ref[rows, :] = _split_dot(ptm, _tile_from_streams(dqs, t, d), 2)
                dk_ref[rows, :] = _split_dot(ptm, _tile_from_streams(dks, t, d), 2)
                dv_ref[rows, :] = _split_dot(ptm, _tile_from_streams(dvs, t, d), 2)

    qkv_spec = pl.BlockSpec((S, LANES), lambda c: (0, g * NCHUNK + c))
    one_spec = pl.BlockSpec((S, LANES), lambda c: (0, c))
    return pl.pallas_call(
        body, name=f"attn_bwd_g{g}", grid=(NCHUNK,),
        in_specs=[qkv_spec] * 3 + [one_spec] * 3 + [_full((TM, TM))] * 2, out_specs=[one_spec] * 3,
        out_shape=[_sds((S, GW), F32)] * 3,
        scratch_shapes=[pltpu.VMEM((S, LANES), BF16)] * 6 + [pltpu.VMEM((S, LANES), F32)] * 4,
        compiler_params=_params(("parallel",)),
    )(q, k, v, do, lse_s, dd, jnp.asarray(perm, BF16), jnp.asarray(perm.T, BF16))


def _qkv_bwd(dqs, dks, dvs, dqm, dz, c, s1, s2):
    def body(q0, q1, q2, k0, k1, k2, v0, v1, v2, dqm_ref, dz_ref, c_ref, s1_ref, s2_ref, dp_ref):
        cc, a1, a2 = c_ref[...], s1_ref[...], s2_ref[...]
        for g, (qr, kr, vr) in enumerate(((q0, k0, v0), (q1, k1, v1), (q2, k2, v2))):
            for j in range(GW // 128):
                ls_ = slice(j * 128, (j + 1) * 128)
                c0 = g * GW + j * 128
                dp_ref[:, c0:c0 + 128] = (_rope_bwd(qr[:, ls_], cc, a1, a2) * SCALE).astype(BF16)
                dp_ref[:, NQ + c0:NQ + c0 + 128] = _rope_bwd(kr[:, ls_], cc, a1, a2).astype(BF16)
            dp_ref[:, 2 * NQ + g * GW:2 * NQ + (g + 1) * GW] = vr[...].astype(BF16)
        dp_ref[:, 3 * NQ:3 * NQ + MW] = dqm_ref[...]
        dp_ref[:, 3 * NQ + MW:] = dz_ref[...]

    return pl.pallas_call(
        body, name="qkv_bwd", grid=(NT,),
        in_specs=[_rows(GW)] * 9 + [_rows(MW), _rows(BR_A), _rows(128), _rows(128), _rows(128)],
        out_specs=_rows(IN_A), out_shape=_sds((S, IN_A), BF16),
        compiler_params=_params(("parallel",)),
    )(*dqs, *dks, *dvs, dqm, dz, c, s1, s2)


def _mem_bwd(mem, mg, memn, wkv, dkv0, dkv1):
    def body(mem_ref, mg_ref, memn_ref, w_ref, d0_ref, d1_ref, dw_ref, dg_ref):
        mf = mem_ref[...]
        n = mf * lax.rsqrt(jnp.mean(mf * mf, axis=-1, keepdims=True) + EPS)
        for i, d_ref in enumerate((d0_ref, d1_ref)):
            dkv = d_ref[...].astype(BF16)
            mn = memn_ref[i]
            for s in range(4):
                cs = slice(s * NM, (s + 1) * NM)
                dw_ref[s, i] = _dot_tn(mn[:, cs], dkv)
                dmn = _dot_nt(dkv, w_ref[s, i])
                dg_ref[i:i + 1, cs] = jnp.sum(dmn * n[:, cs], axis=0, keepdims=True)

    return pl.pallas_call(
        body, name="mem_bwd", grid=(1,),
        in_specs=[_full((NM, D)), _full((2, D)), _full((2, NM, D)), _full((4, 2, NM, 2 * MW)),
                  _full((NM, 2 * MW)), _full((NM, 2 * MW))],
        out_specs=[_full((4, 2, NM, 2 * MW)), _full((2, D))],
        out_shape=[_sds((4, 2, NM, 2 * MW), F32), _sds((2, D), F32)],
        compiler_params=_params(("arbitrary",)),
    )(mem, mg, memn, wkv, dkv0, dkv1)


MESH = pl.DeviceIdType.MESH
ANY = pl.BlockSpec(memory_space=pl.ANY)
BIG = (("wkv", 2, NM, 2 * MW), ("w_in_a", 1, D, SH_A), ("w_out_a", 1, BR_A, SH_O),
       ("w_in_b", 1, D, SH_B), ("w_out_b", 1, BR_B // 4, D))
NBIG = len(BIG)
CW_ROWS = 8


def _place():
    x, y, c = lax.axis_index("x"), lax.axis_index("y"), lax.axis_index("c")
    chips = ((1 - x, y), (x, 1 - y), (1 - x, 1 - y))
    return x, y, c, chips


def _remote(src, dst, ssem, rsem, dev):
    return pltpu.make_async_remote_copy(src_ref=src, dst_ref=dst, send_sem=ssem, recv_sem=rsem,
                                        device_id=dev, device_id_type=MESH)


def _cast_weights(place, ws, after, idx, name):
    nblk = 4
    n = len(idx)
    dims = [BIG[w][1:] for w in idx]

    def body(pref, *refs):
        for i in range(n):
            refs[n + 1 + i][0] = refs[i][...].astype(BF16)

    grid_spec = pltpu.PrefetchScalarGridSpec(
        num_scalar_prefetch=1, grid=(nblk,),
        in_specs=[pl.BlockSpec((k, r // nblk, cdim), lambda i, pref: (0, i, 0)) for k, r, cdim in dims]
        + [pl.BlockSpec(memory_space=pl.ANY)],
        out_specs=[pl.BlockSpec((1, k, r // nblk, cdim), lambda i, pref: (pref[1], 0, i, 0)) for k, r, cdim in dims])
    return pl.pallas_call(
        body, name=name, grid_spec=grid_spec,
        out_shape=[_sds((4, k, r, cdim), BF16) for k, r, cdim in dims],
        compiler_params=_params(("parallel",)),
    )(place, *ws, after)


LAYER_A = (0, 1, 2)
LAYER_B = (3, 4)
HBM = pl.BlockSpec(memory_space=pltpu.HBM)
SEM = pl.BlockSpec(memory_space=pltpu.SEMAPHORE)
EFFECT = pltpu.SideEffectType.DATAFLOW_SIDE_EFFECTING
TOKEN = (8, 128)


def _half(ref, w, which):
    h = BIG[w][2] // 2
    return ref.at[:, pl.ds(which * h, h), :]


def _skip_arg(body, pos, *refs):
    return body(*refs[:pos], *refs[pos + 1:])


def _gather_weights(wb, cw, idx, name):
    n = len(idx)

    def body(*refs):
        src_cw = refs[n]
        dst = refs[n + 1:2 * n + 2]
        loc_sem, send_sems, recv_sems, fsend_sems, frecv_sems = refs[2 * n + 2:]
        x, y, c, chips = _place()
        me = 2 * x + y
        loc = pltpu.make_async_copy(src_cw, dst[n].at[me], loc_sem)
        loc.start()
        sends = []
        for j, (px, py) in enumerate(chips):
            for i in range(n):
                mine = _half(dst[i].at[me], idx[i], c)
                sends.append(_remote(mine, mine, send_sems.at[j, i], recv_sems.at[j, i], (px, py, c)))
            sends.append(_remote(src_cw, dst[n].at[me], send_sems.at[j, n], recv_sems.at[j, n], (px, py, c)))
        for cp in sends:
            cp.start()
        fwds = []
        for j, (px, py) in enumerate(chips):
            for i in range(n):
                got = _half(dst[i].at[2 * px + py], idx[i], c)
                _remote(got, got, send_sems.at[j, i], recv_sems.at[j, i], (px, py, c)).wait_recv()
                fwds.append(_remote(got, got, fsend_sems.at[j, i], frecv_sems.at[j, i], (x, y, 1 - c)))
                fwds[-1].start()
            got = dst[n].at[2 * px + py]
            _remote(got, got, send_sems.at[j, n], recv_sems.at[j, n], (px, py, c)).wait_recv()
        for j, (px, py) in enumerate(chips):
            for i in range(n):
                got = _half(dst[i].at[2 * px + py], idx[i], 1 - c)
                _remote(got, got, fsend_sems.at[j, i], frecv_sems.at[j, i], (x, y, 1 - c)).wait_recv()
        for cp in sends + fwds:
            cp.wait_send()
        loc.wait()

    out_shape = [_sds(w.shape, BF16) for w in wb] + [_sds((4, CW_ROWS, SH_O), F32)]
    return pl.pallas_call(
        body, name=name, in_specs=[ANY] * (n + 1), out_specs=[ANY] * (n + 1), out_shape=out_shape,
        input_output_aliases={i: i for i in range(n)},
        scratch_shapes=[pltpu.SemaphoreType.DMA, pltpu.SemaphoreType.DMA((3, n + 1)),
                        pltpu.SemaphoreType.DMA((3, n + 1)), pltpu.SemaphoreType.DMA((3, n)),
                        pltpu.SemaphoreType.DMA((3, n))],
    )(*wb, cw)


def _gather_start(wb, after, idx, name):
    n = len(idx)

    def body(*refs):
        src = refs[:n]
        send_sems, recv_sems = refs[n + 1], refs[n + 2]
        token = refs[2 * n + 3]
        x, y, c, chips = _place()
        me = 2 * x + y
        for j, (px, py) in enumerate(chips):
            for i in range(n):
                mine = _half(src[i].at[me], idx[i], c)
                _remote(mine, mine, send_sems.at[j * n + i], recv_sems.at[j * n + i], (px, py, c)).start()
        token[...] = jnp.zeros(TOKEN, F32)

    outs = pl.pallas_call(
        body, name=name, in_specs=[HBM] * n + [ANY],
        out_specs=(SEM, SEM) + (HBM,) * n + (pl.BlockSpec(memory_space=pltpu.VMEM),),
        out_shape=(pltpu.SemaphoreType.DMA((3 * n,)), pltpu.SemaphoreType.DMA((3 * n,)))
        + tuple(pltpu.HBM(w.shape, w.dtype) for w in wb) + (_sds(TOKEN, F32),),
        input_output_aliases={i: 2 + i for i in range(n)},
        compiler_params=pltpu.CompilerParams(has_side_effects=EFFECT),
    )(*[pltpu.with_memory_space_constraint(w, pltpu.HBM) for w in wb], after)
    return outs[0], outs[1], list(outs[2:2 + n]), outs[2 + n]


def _gather_wait(send_sems, recv_sems, wb, after, idx, name):
    n = len(idx)

    def body(*refs):
        buf = refs[:n]
        send_sems, recv_sems = refs[n], refs[n + 1]
        x, y, c, chips = _place()
        me = 2 * x + y
        for j, (px, py) in enumerate(chips):
            for i in range(n):
                mine = _half(buf[i].at[me], idx[i], c)
                got = _half(buf[i].at[2 * px + py], idx[i], c)
                _remote(mine, mine, send_sems.at[j * n + i], recv_sems.at[j * n + i], (px, py, c)).wait_send()
                _remote(got, got, send_sems.at[j * n + i], recv_sems.at[j * n + i], (px, py, c)).wait_recv()

    outs = pl.pallas_call(
        body, name=name, in_specs=[HBM] * n + [SEM, SEM] + [ANY] * len(after), out_specs=(HBM,) * n,
        out_shape=tuple(pltpu.HBM(w.shape, w.dtype) for w in wb),
        input_output_aliases={i: i for i in range(n)},
        compiler_params=pltpu.CompilerParams(has_side_effects=EFFECT),
    )(*wb, send_sems, recv_sems, *after)
    return list(outs)


def _gather_forward(wb, idx, name, cw=None):
    n = len(idx)
    m = n if cw is None else n + 1

    def body(*refs):
        dst = refs[m:2 * m]
        send_sems, recv_sems = refs[2 * m], refs[2 * m + 1]
        x, y, c, chips = _place()
        cps = []
        for j, (px, py) in enumerate(chips):
            for i in range(n):
                got = _half(dst[i].at[2 * px + py], idx[i], c)
                cps.append(_remote(got, got, send_sems.at[j, i], recv_sems.at[j, i], (x, y, 1 - c)))
                cps[-1].start()
        if cw is not None:
            src_cw, loc_sem = refs[n], refs[2 * m + 2]
            me = 2 * x + y
            loc = pltpu.make_async_copy(src_cw, dst[n].at[me], loc_sem)
            loc.start()
            for j, (px, py) in enumerate(chips):
                cps.append(_remote(src_cw, dst[n].at[me], send_sems.at[j, n], recv_sems.at[j, n], (px, py, c)))
                cps[-1].start()
        for j, (px, py) in enumerate(chips):
            for i in range(n):
                got = _half(dst[i].at[2 * px + py], idx[i], 1 - c)
                _remote(got, got, send_sems.at[j, i], recv_sems.at[j, i], (x, y, 1 - c)).wait_recv()
            if cw is not None:
                got = dst[n].at[2 * px + py]
                _remote(got, got, send_sems.at[j, n], recv_sems.at[j, n], (px, py, c)).wait_recv()
        for cp in cps:
            cp.wait_send()
        if cw is not None:
            loc.wait()

    out_shape = [_sds(w.shape, BF16) for w in wb]
    scratch = [pltpu.SemaphoreType.DMA((3, m)), pltpu.SemaphoreType.DMA((3, m))]
    args = list(wb)
    if cw is not None:
        out_shape.append(_sds((4, CW_ROWS, SH_O), F32))
        scratch.append(pltpu.SemaphoreType.DMA)
        args.append(cw)
    return pl.pallas_call(
        body, name=name, in_specs=[ANY] * m, out_specs=[ANY] * m, out_shape=out_shape,
        input_output_aliases={i: i for i in range(n)}, scratch_shapes=scratch,
    )(*args)


def _pair_exchange(gs, idx, name):
    n = len(idx)

    def body(*refs):
        src, dst = refs[:n], refs[n:2 * n]
        send_sems, recv_sems = refs[2 * n:]
        x, y, c, _ = _place()
        cps = []
        for i in range(n):
            h = BIG[idx[i]][2] // 2
            cps.append(_remote(src[i].at[:, :, pl.ds((1 - c) * h, h), :], dst[i], send_sems.at[i], recv_sems.at[i],
                               (x, y, 1 - c)))
            cps[-1].start()
        for cp in cps:
            cp.wait()

    return pl.pallas_call(
        body, name=name, in_specs=[ANY] * n, out_specs=[ANY] * n,
        out_shape=[_sds((4, BIG[w][1], BIG[w][2] // 2, BIG[w][3]), F32) for w in idx],
        scratch_shapes=[pltpu.SemaphoreType.DMA((n,)), pltpu.SemaphoreType.DMA((n,))],
    )(*gs)


def _pair_start(gs, idx, name):
    n = len(idx)

    def body(*refs):
        src, land = refs[:n], refs[n:2 * n]
        send_sems, recv_sems = refs[2 * n], refs[2 * n + 1]
        token = refs[4 * n + 2]
        x, y, c, _ = _place()
        for i in range(n):
            h = BIG[idx[i]][2] // 2
            _remote(src[i].at[:, :, pl.ds((1 - c) * h, h), :], land[i], send_sems.at[i], recv_sems.at[i],
                    (x, y, 1 - c)).start()
        token[...] = jnp.zeros(TOKEN, F32)

    lands = [lax.empty((4, BIG[w][1], BIG[w][2] // 2, BIG[w][3]), F32) for w in idx]
    arrays = list(gs) + lands
    outs = pl.pallas_call(
        body, name=name, in_specs=[HBM] * (2 * n),
        out_specs=(SEM, SEM) + (HBM,) * (2 * n) + (pl.BlockSpec(memory_space=pltpu.VMEM),),
        out_shape=(pltpu.SemaphoreType.DMA((n,)), pltpu.SemaphoreType.DMA((n,)))
        + tuple(pltpu.HBM(a.shape, a.dtype) for a in arrays) + (_sds(TOKEN, F32),),
        input_output_aliases={i: 2 + i for i in range(2 * n)},
        compiler_params=pltpu.CompilerParams(has_side_effects=EFFECT),
    )(*[pltpu.with_memory_space_constraint(a, pltpu.HBM) for a in arrays])
    return outs[0], outs[1], list(outs[2:2 + n]), list(outs[2 + n:2 + 2 * n]), outs[2 + 2 * n]


def _pair_wait(send_sems, recv_sems, gs, lands, after, idx, name):
    n = len(idx)

    def body(*refs):
        src, land = refs[:n], refs[n:2 * n]
        send_sems, recv_sems = refs[2 * n], refs[2 * n + 1]
        x, y, c, _ = _place()
        for i in range(n):
            h = BIG[idx[i]][2] // 2
            cp = _remote(src[i].at[:, :, pl.ds((1 - c) * h, h), :], land[i], send_sems.at[i], recv_sems.at[i],
                         (x, y, 1 - c))
            cp.wait_send()
            cp.wait_recv()

    arrays = list(gs) + list(lands)
    outs = pl.pallas_call(
        body, name=name, in_specs=[HBM] * (2 * n) + [SEM, SEM] + [ANY] * len(after), out_specs=(HBM,) * (2 * n),
        out_shape=tuple(pltpu.HBM(a.shape, a.dtype) for a in arrays),
        input_output_aliases={i: i for i in range(2 * n)},
        compiler_params=pltpu.CompilerParams(has_side_effects=EFFECT),
    )(*arrays, send_sems, recv_sems, *after)
    return list(outs[:n]), list(outs[n:])


def _pair_sum(place, g, r1, i):
    _, k, r, cdim = BIG[i]
    h = r // 2

    def body(pref, g_ref, r_ref, o_ref):
        o_ref[...] = (g_ref[...] + r_ref[...]).astype(BF16)

    grid_spec = pltpu.PrefetchScalarGridSpec(
        num_scalar_prefetch=1, grid=(4, k),
        in_specs=[pl.BlockSpec((1, 1, h, cdim), lambda s, t, pref: (s, t, pref[0], 0)),
                  pl.BlockSpec((1, 1, h, cdim), lambda s, t, pref: (s, t, 0, 0))],
        out_specs=pl.BlockSpec((1, 1, h, cdim), lambda s, t, pref: (s, t, 0, 0)))
    return pl.pallas_call(
        body, name=f"pair_sum_{BIG[i][0]}", grid_spec=grid_spec, out_shape=_sds((4, k, h, cdim), BF16),
        compiler_params=_params(("parallel", "parallel")),
    )(place, g, r1)


def _chip_start(ps, idx, name):
    n = len(idx)

    def body(*refs):
        src, land = refs[:n], refs[n:2 * n]
        send_sems, recv_sems = refs[2 * n], refs[2 * n + 1]
        token = refs[4 * n + 2]
        x, y, c, chips = _place()
        for j, (px, py) in enumerate(chips):
            for i in range(n):
                _remote(src[i].at[2 * px + py], land[i].at[j], send_sems.at[j * n + i], recv_sems.at[j * n + i],
                        (px, py, c)).start()
        token[...] = jnp.zeros(TOKEN, F32)

    lands = [lax.empty((3,) + p.shape[1:], BF16) for p in ps]
    outs = pl.pallas_call(
        body, name=name, in_specs=[HBM] * (2 * n),
        out_specs=(SEM, SEM) + (HBM,) * (2 * n) + (pl.BlockSpec(memory_space=pltpu.VMEM),),
        out_shape=(pltpu.SemaphoreType.DMA((3 * n,)), pltpu.SemaphoreType.DMA((3 * n,)))
        + tuple(pltpu.HBM(a.shape, a.dtype) for a in list(ps) + lands) + (_sds(TOKEN, F32),),
        input_output_aliases={i: 2 + i for i in range(2 * n)},
        compiler_params=pltpu.CompilerParams(has_side_effects=EFFECT),
    )(*[pltpu.with_memory_space_constraint(a, pltpu.HBM) for a in list(ps) + lands])
    return outs[0], outs[1], list(outs[2:2 + n]), list(outs[2 + n:2 + 2 * n]), outs[2 + 2 * n]


def _chip_wait(send_sems, recv_sems, ps, lands, after, idx, name):
    n = len(idx)

    def body(*refs):
        src, land = refs[:n], refs[n:2 * n]
        send_sems, recv_sems = refs[2 * n], refs[2 * n + 1]
        x, y, c, chips = _place()
        for j, (px, py) in enumerate(chips):
            for i in range(n):
                cp = _remote(src[i].at[2 * px + py], land[i].at[j], send_sems.at[j * n + i], recv_sems.at[j * n + i],
                             (px, py, c))
                cp.wait_send()
                cp.wait_recv()

    arrays = list(ps) + list(lands)
    outs = pl.pallas_call(
        body, name=name, in_specs=[HBM] * (2 * n) + [SEM, SEM] + [ANY] * len(after), out_specs=(HBM,) * (2 * n),
        out_shape=tuple(pltpu.HBM(a.shape, a.dtype) for a in arrays),
        input_output_aliases={i: i for i in range(2 * n)},
        compiler_params=pltpu.CompilerParams(has_side_effects=EFFECT),
    )(*arrays, send_sems, recv_sems, *after)
    return list(outs[n:])


def _chip_sum(place, g, r1, r2, i):
    _, k, r, cdim = BIG[i]
    h = r // 2

    def body(pref, g_ref, r1_ref, r2_ref, o_ref):
        acc = g_ref[0, 0] + r1_ref[0, 0]
        for j in range(3):
            acc = acc + r2_ref[j, 0].astype(F32)
        o_ref[0] = acc

    grid_spec = pltpu.PrefetchScalarGridSpec(
        num_scalar_prefetch=1, grid=(k,),
        in_specs=[pl.BlockSpec((1, 1, h, cdim), lambda t, pref: (pref[1], t, pref[0], 0)),
                  pl.BlockSpec((1, 1, h, cdim), lambda t, pref: (pref[1], t, 0, 0)),
                  pl.BlockSpec((3, 1, h, cdim), lambda t, pref: (0, t, 0, 0))],
        out_specs=pl.BlockSpec((1, h, cdim), lambda t, pref: (t, pref[0], 0)))
    return pl.pallas_call(
        body, name=f"chip_sum_{BIG[i][0]}", grid_spec=grid_spec, out_shape=_sds((k, r, cdim), F32),
        compiler_params=_params(("parallel",)),
    )(place, g, r1, r2)


def _pair_gather(hs, idx, name):
    n = len(idx)

    def body(*refs):
        dst = refs[n:2 * n]
        send_sems, recv_sems = refs[2 * n:]
        x, y, c, _ = _place()
        cps = []
        for i in range(n):
            mine = _half(dst[i], idx[i], c)
            cps.append(_remote(mine, mine, send_sems.at[i], recv_sems.at[i], (x, y, 1 - c)))
            cps[-1].start()
        for i in range(n):
            theirs = _half(dst[i], idx[i], 1 - c)
            _remote(theirs, theirs, send_sems.at[i], recv_sems.at[i], (x, y, 1 - c)).wait_recv()
        for cp in cps:
            cp.wait_send()

    return pl.pallas_call(
        body, name=name, in_specs=[ANY] * n, out_specs=[ANY] * n,
        out_shape=[_sds(BIG[w][1:], F32) for w in idx],
        input_output_aliases={i: i for i in range(n)},
        scratch_shapes=[pltpu.SemaphoreType.DMA((n,)), pltpu.SemaphoreType.DMA((n,))],
    )(*hs)


SMALL_ROWS = 40


def _all_reduce_small(pack, after):
    def body(p_ref, o_ref, slots, send_sems, recv_sems):
        x, y, c, _ = _place()
        me = 4 * x + 2 * y + c
        cps = []
        for r in range(1, 8):
            peer = (x if not r & 4 else 1 - x, y if not r & 2 else 1 - y, c if not r & 1 else 1 - c)
            cps.append(_remote(p_ref, slots.at[r], send_sems.at[r - 1], recv_sems.at[r - 1], peer))
            cps[-1].start()
        slots[0] = p_ref[...]
        for cp in cps:
            cp.wait()
        acc = slots[me]
        for dev in range(1, 8):
            acc = acc + slots[jnp.bitwise_xor(me, dev)]
        o_ref[...] = acc

    vm = pl.BlockSpec(memory_space=pltpu.VMEM)
    return pl.pallas_call(
        functools.partial(_skip_arg, body, 1), name="all_reduce_small", in_specs=[vm, ANY], out_specs=vm,
        out_shape=_sds((SMALL_ROWS, D), F32),
        scratch_shapes=[pltpu.VMEM((8, SMALL_ROWS, D), F32), pltpu.SemaphoreType.DMA((7,)),
                        pltpu.SemaphoreType.DMA((7,))],
    )(pack, after)


def _adamw_math(w, g, m, v):
    m = ADAM_B1 * m + (1.0 - ADAM_B1) * g
    v = ADAM_B2 * v + (1.0 - ADAM_B2) * (g * g)
    m_hat = m / (1.0 - ADAM_B1 ** ADAM_STEP)
    v_hat = v / (1.0 - ADAM_B2 ** ADAM_STEP)
    delta = -ADAM_LR * (m_hat / (jnp.sqrt(v_hat) + ADAM_EPS) + ADAM_WD * w)
    return delta, m, v


def _adamw_big(w, g, m, v, i):
    _, k, r, cdim = BIG[i]
    nblk = 4 if k == 1 else 1

    def body(w_ref, g_ref, m_ref, v_ref, d_ref, nm_ref, nv_ref, go_ref):
        gv = g_ref[...]
        d_ref[...], nm_ref[...], nv_ref[...] = _adamw_math(w_ref[...], gv, m_ref[...], v_ref[...])
        go_ref[...] = gv

    spec = pl.BlockSpec((1, r // nblk, cdim), lambda t, b: (t, b, 0))
    return pl.pallas_call(
        body, name=f"adamw_{BIG[i][0]}", grid=(k, nblk), in_specs=[spec] * 4, out_specs=[spec] * 4,
        out_shape=[_sds((k, r, cdim), F32)] * 4,
        compiler_params=_params(("parallel", "parallel")),
    )(w, g, m, v)


def _adamw_small(ws, gs, ms, vs):
    n = len(ws)

    def body(*refs):
        for i in range(n):
            w_ref, g_ref, m_ref, v_ref = refs[i], refs[n + i], refs[2 * n + i], refs[3 * n + i]
            d, nm, nv = _adamw_math(w_ref[...], g_ref[...], m_ref[...], v_ref[...])
            refs[4 * n + i][...] = d
            refs[5 * n + i][...] = nm
            refs[6 * n + i][...] = nv

    specs = [_full(w.shape) for w in ws]
    outs = pl.pallas_call(
        body, name="adamw_small", grid=(1,), in_specs=specs * 4, out_specs=specs * 3,
        out_shape=[_sds(w.shape, F32) for w in ws] * 3,
        compiler_params=_params(("arbitrary",)),
    )(*ws, *gs, *ms, *vs)
    return outs[:n], outs[n:2 * n], outs[2 * n:]


def _pad_rows(a, rows):
    return jnp.pad(a, ((0, rows - a.shape[0]), (0, 0)))


def kernel(x, mem, positions, norm_g, mem_norm_g, w_mem_kv, attn_w_in, attn_w_out, conv_w_in, conv_w, conv_w_out, final_g, loss_target, m_norm_g, m_mem_norm_g, m_w_mem_kv, m_attn_w_in, m_attn_w_out, m_conv_w_in, m_conv_w, m_conv_w_out, m_final_g, v_norm_g, v_mem_norm_g, v_w_mem_kv, v_attn_w_in, v_attn_w_out, v_conv_w_in, v_conv_w, v_conv_w_out, v_final_g):
    mx, my, mc = lax.axis_index("x"), lax.axis_index("y"), lax.axis_index("c")
    place = jnp.stack([mc, 2 * mx + my]).astype(jnp.int32)

    w_big = [w_mem_kv, attn_w_in, attn_w_out, conv_w_in, conv_w_out]
    m_big = [m_w_mem_kv, m_attn_w_in, m_attn_w_out, m_conv_w_in, m_conv_w_out]
    v_big = [v_w_mem_kv, v_attn_w_in, v_attn_w_out, v_conv_w_in, v_conv_w_out]
    first, rest = (1,), (0, 2, 3, 4)
    wb1 = _cast_weights(place, [w_big[i] for i in first], place, first, "cast_w_in_a")
    a1_send, a1_recv, a1_bufs, a1_token = _gather_start(wb1, place, first, "gather_a1_start")
    wbr = _cast_weights(place, [w_big[i] for i in rest], a1_token, rest, "cast_weights")
    rest = (0, 2)
    a2_send, a2_recv, a2_bufs, a2_token = _gather_start([wbr[0], wbr[1]], a1_token, rest, "gather_a2_start")
    gb_send, gb_recv, gb_bufs, gb_token = _gather_start([wbr[2], wbr[3]], a2_token, LAYER_B, "gather_b_start")

    xs, tgt = x[0], loss_target[0]
    g0, g1 = norm_g[0:1], norm_g[1:2]
    rc, rs1, rs2 = _rope_tables(positions[0].astype(F32).reshape(S, 1), gb_token)
    a1_bufs = _gather_wait(a1_send, a1_recv, a1_bufs, [rc], first, "gather_a1_wait")
    w_in_a = _gather_forward(a1_bufs, first, "gather_a1_forward")[0].reshape(4, D, SH_A)
    hn0, q, k, v, qm0, z0 = _in_proj_a(xs, g0, w_in_a, rc, rs1, rs2, gb_token)
    a2_bufs = _gather_wait(a2_send, a2_recv, a2_bufs, [q], rest, "gather_a2_wait")
    wkv_f, w_out_a = _gather_forward(a2_bufs, rest, "gather_a2_forward")
    w_out_a = w_out_a.reshape(4, BR_A, SH_O)
    memn, kv = _mem_fwd(mem[0], mem_norm_g, wkv_f)
    fwd = [_attn_fwd(q, k, v, g) for g in range(3)]
    os_, ls, lss = [f[0] for f in fwd], [f[1] for f in fwd], [f[2] for f in fwd]
    h1 = _attn_out(os_, ls, qm0, kv[0], z0, xs, w_out_a)

    gb_bufs = _gather_wait(gb_send, gb_recv, gb_bufs, [h1], LAYER_B, "gather_b_wait")
    w_in_b, w_out_b, cw_f = _gather_forward(gb_bufs, LAYER_B, "gather_b_forward", _pad_rows(conv_w[0], CW_ROWS))
    w_in_b = w_in_b.reshape(4, D, SH_B)
    w_out_b = w_out_b.reshape(BR_B, D)
    cw8 = cw_f.transpose(1, 0, 2).reshape(CW_ROWS, D)
    hn1, bg, cg, u, qm1, z1 = _in_proj_b(h1, g1, w_in_b)
    dh2, loss_part, dfg = _conv_out_loss(bg, cg, u, cw8, qm1, kv[1], z1, h1, w_out_b, final_g.reshape(1, D), tgt)

    dproj_b, dw_out_b, dcw, dkv1 = _conv_bwd(dh2, bg, cg, u, cw8, qm1, kv[1], z1, w_out_b)
    dw_in_b = _w_in_grad(hn1, dproj_b, IN_B, "w_in_b_grad")
    gs_b = [dw_in_b.reshape(4, 1, D, SH_B), dw_out_b.reshape(4, 1, BR_B // 4, D)]
    pb_send, pb_recv, gs_b, pb_land, pb_token = _pair_start(gs_b, LAYER_B, "pair_b_start")
    dh1, dg1 = _in_proj_bwd(dproj_b, w_in_b, h1, g1, dh2, pb_token, IN_B, "in_proj_b_bwd")
    gs_b, r1_b = _pair_wait(pb_send, pb_recv, gs_b, pb_land, [dh1], LAYER_B, "pair_b_wait")
    ps_b = [_pair_sum(place, gs_b[i], r1_b[i], w) for i, w in enumerate(LAYER_B)]
    cb_send, cb_recv, cb_src, cb_land, cb_token = _chip_start(ps_b, LAYER_B, "chip_b_start")

    outs = _attn_out_bwd(dh1, os_, ls, qm0, kv[0], z0, w_out_a, cb_token)
    dos, dds, dqm, dz, dw_out_a, dkv0 = outs[0:3], outs[3:6], outs[6], outs[7], outs[8], outs[9]
    bwd = [_attn_bwd(q, k, v, dos[g], lss[g], dds[g], g) for g in range(3)]
    dproj_a = _qkv_bwd([b[0] for b in bwd], [b[1] for b in bwd], [b[2] for b in bwd], dqm, dz, rc, rs1, rs2)
    dw_in_a = _w_in_grad(hn0, dproj_a, IN_A, "w_in_a_grad")
    dwkv, dmg = _mem_bwd(mem[0], mem_norm_g, memn, wkv_f, dkv0, dkv1)

    gs_a = [dwkv, dw_in_a.reshape(4, 1, D, SH_A), dw_out_a.reshape(4, 1, BR_A, SH_O)]
    r1_a = _pair_exchange(gs_a, LAYER_A, "pair_exchange_a")
    ps_a = [_pair_sum(place, gs_a[i], r1_a[i], w) for i, w in enumerate(LAYER_A)]
    ca_send, ca_recv, ca_src, ca_land, ca_token = _chip_start(ps_a, LAYER_A, "chip_a_start")

    gx, dg0 = _in_proj_bwd(dproj_a, w_in_a, xs, g0, dh1, ca_token, IN_A, "in_proj_a_bwd")
    r2_b = _chip_wait(cb_send, cb_recv, cb_src, cb_land, [ca_token], LAYER_B, "chip_b_wait")
    hs_b = [_chip_sum(place, gs_b[i], r1_b[i], r2_b[i], w) for i, w in enumerate(LAYER_B)]
    g_b = _pair_gather(hs_b, LAYER_B, "pair_gather_b")
    upd_b = [_adamw_big(w_big[w], g_b[i], m_big[w], v_big[w], w) for i, w in enumerate(LAYER_B)]
    r2_a = _chip_wait(ca_send, ca_recv, ca_src, ca_land, [gx, upd_b[0][0], upd_b[1][0]], LAYER_A, "chip_a_wait")

    pack = jnp.concatenate([_pad_rows(jnp.concatenate([dg0, dg1], axis=0), 8), _pad_rows(dmg, 8), _pad_rows(dfg, 8),
                            dcw, _pad_rows(jnp.pad(loss_part, ((0, 0), (0, D - 128))), 8)], axis=0)
    tot = _all_reduce_small(pack, r2_a[0])
    loss = tot[32, 0]
    g_norm, g_memnorm, g_final = tot[0:2], tot[8:10], tot[16]
    g_conv = lax.dynamic_slice(tot, (24, (2 * mx + my) * SH_O), (3, SH_O))
    hs_a = [_chip_sum(place, gs_a[i], r1_a[i], r2_a[i], w) for i, w in enumerate(LAYER_A)]
    g_a = _pair_gather(hs_a, LAYER_A, "pair_gather_a")
    upd_a = [_adamw_big(w_big[w], g_a[i], m_big[w], v_big[w], w) for i, w in enumerate(LAYER_A)]
    upd = upd_a + upd_b
    g_big = [u[3] for u in upd]
    sw = [norm_g, mem_norm_g, final_g.reshape(1, D), conv_w[0]]
    sg = [g_norm, g_memnorm, g_final.reshape(1, D), g_conv]
    sm = [m_norm_g, m_mem_norm_g, m_final_g.reshape(1, D), m_conv_w[0]]
    sv = [v_norm_g, v_mem_norm_g, v_final_g.reshape(1, D), v_conv_w[0]]
    sd, snm, snv = _adamw_small(sw, sg, sm, sv)

    def order(norm, memnorm, wkv, w_in_a, w_out_a, w_in_b, conv, w_out_b, final):
        return (norm, memnorm, wkv, w_in_a, w_out_a, w_in_b, conv.reshape(1, 3, SH_O), w_out_b, final.reshape(D))

    grads = order(g_norm, g_memnorm, g_big[0], g_big[1], g_big[2], g_big[3], g_conv, g_big[4], g_final)
    deltas = order(sd[0], sd[1], upd[0][0], upd[1][0], upd[2][0], upd[3][0], sd[3], upd[4][0], sd[2])
    new_m = order(snm[0], snm[1], upd[0][1], upd[1][1], upd[2][1], upd[3][1], snm[3], upd[4][1], snm[2])
    new_v = order(snv[0], snv[1], upd[0][2], upd[1][2], upd[2][2], upd[3][2], snv[3], upd[4][2], snv[2])
    return (loss, gx[None], *grads, *deltas, *new_m, *new_v)
```

```python
import functools

import numpy as np
import jax
import jax.numpy as jnp
from jax import lax
from jax.experimental import pallas as pl
from jax.experimental.pallas import tpu as pltpu

F32 = jnp.float32
BF16 = jnp.bfloat16

S = 2048
D = 1024
TM = 256
NT = S // TM
HD = 64
GW = 512
NQ = 3 * GW
MW = 256
NM = 256
IN_A = 3 * NQ + MW + GW + MW
IN_B = 3 * D + MW + D + MW
BR_A = GW + MW
BR_B = D + MW
SH_A = IN_A // 4
SH_B = IN_B // 4
SH_O = D // 4
QBLK = 128
DILATIONS = (1, 4, 16)
EPS = 1e-6
SCALE = HD ** -0.5
NEG = -1e30
ROPE_THETA = 500000.0

ADAM_LR = 0.001
ADAM_B1 = 0.9
ADAM_B2 = 0.999
ADAM_EPS = 1e-08
ADAM_WD = 0.01
ADAM_STEP = 10

VMEM_LIMIT_BYTES = 60 * 1024 * 1024


def _params(sem=None):
    if sem is None:
        return pltpu.CompilerParams(vmem_limit_bytes=VMEM_LIMIT_BYTES)
    return pltpu.CompilerParams(dimension_semantics=sem, vmem_limit_bytes=VMEM_LIMIT_BYTES)


def _full(shape):
    nd = len(shape)
    return pl.BlockSpec(shape, lambda *_: (0,) * nd)


def _rows(width, tm=TM):
    return pl.BlockSpec((tm, width), lambda i: (i, 0))


def _sds(shape, dtype):
    return jax.ShapeDtypeStruct(shape, dtype)


def _silu_parts(z):
    sig = 1.0 / (1.0 + jnp.exp(-z))
    return z * sig, sig * (1.0 + z * (1.0 - sig))


def _dot(a, b):
    return jnp.dot(a, b, preferred_element_type=F32)


def _dot_nt(a, b):
    return lax.dot_general(a, b, (((1,), (1,)), ((), ())), preferred_element_type=F32)


def _dot_tn(a, b):
    return lax.dot_general(a, b, (((0,), (0,)), ((), ())), preferred_element_type=F32)


def _rope_fwd(t, c, s1, s2):
    return t * c + pltpu.roll(t, 120, 1) * s1 + pltpu.roll(t, 8, 1) * s2


def _rope_bwd(g, c, s1, s2):
    return g * c + pltpu.roll(g * s1, 8, 1) + pltpu.roll(g * s2, 120, 1)


def _mem_attn(qm, kv):
    res = []
    for h in range(MW // HD):
        sl = slice(h * HD, (h + 1) * HD)
        s = _dot_nt(qm[:, sl], kv[:, sl]) * SCALE
        e = jnp.exp(s - jnp.max(s, axis=-1, keepdims=True))
        p = e / jnp.sum(e, axis=-1, keepdims=True)
        res.append((p, _dot(p.astype(BF16), kv[:, MW + h * HD:MW + (h + 1) * HD])))
    return res


def _mem_attn_bwd(dmo, heads, qm, kv, dqm_store, dkv_ref):
    for h, (p, mo) in enumerate(heads):
        sl = slice(h * HD, (h + 1) * HD)
        vs = slice(MW + h * HD, MW + (h + 1) * HD)
        dmo_h = dmo[:, sl]
        dmo_b = dmo_h.astype(BF16)
        dp = _dot_nt(dmo_b, kv[:, vs])
        delta = jnp.sum(dmo_h * mo, axis=-1, keepdims=True)
        ds = (p * (dp - delta) * SCALE).astype(BF16)
        dqm_store(h, _dot(ds, kv[:, sl]))
        dkv_ref[:, sl] += _dot_tn(ds, qm[:, sl])
        dkv_ref[:, vs] += _dot_tn(p.astype(BF16), dmo_b)


def _merge(o_refs, l_refs):
    ls = [r[...] for r in l_refs]
    m = jnp.maximum(jnp.maximum(ls[0], ls[1]), ls[2])
    es = [jnp.exp(l - m) for l in ls]
    inv = 1.0 / (es[0] + es[1] + es[2])
    ws = [e * inv for e in es]
    os_ = [r[...] for r in o_refs]
    mix = ws[0] * os_[0] + ws[1] * os_[1] + ws[2] * os_[2]
    return ws, mix


def _conv_taps(cg, u, cgp, up, first):
    a = cg * u
    ap = jnp.where(first, 0.0, cgp * up)
    row = lax.broadcasted_iota(jnp.int32, a.shape, 0)
    a1 = jnp.where(row == 0, ap[7:8, :], pltpu.roll(a, 1, 0))
    a2 = jnp.where(row == 0, ap[6:7, :], jnp.where(row == 1, ap[7:8, :], pltpu.roll(a, 2, 0)))
    return a, a1, a2


def _rope_tables(posf, after):
    half = 8
    invf = np.float32(ROPE_THETA) ** (-np.arange(half, dtype=np.float32) * np.float32(2.0 / 16))
    lane = np.arange(128)
    table = np.where((lane % HD) < 16, invf[lane % half], 0.0).astype(np.float32)[None, :]

    def body(pos_ref, invf_ref, c_ref, s1_ref, s2_ref):
        ang = pos_ref[...] * invf_ref[...]
        jm = lax.broadcasted_iota(jnp.int32, ang.shape, 1) & (HD - 1)
        cs = jnp.cos(ang)
        sn = jnp.sin(ang)
        c_ref[...] = jnp.where(jm < 16, cs, 1.0)
        s1_ref[...] = jnp.where(jm < 8, -sn, 0.0)
        s2_ref[...] = jnp.where((jm >= 8) & (jm < 16), sn, 0.0)

    out = _sds((S, 128), F32)
    return pl.pallas_call(
        functools.partial(_skip_arg, body, 2), name="rope_tables", grid=(NT,),
        in_specs=[_rows(1), _full((1, 128)), pl.BlockSpec(memory_space=pl.ANY)],
        out_specs=[_rows(128)] * 3, out_shape=[out] * 3,
        compiler_params=_params(("parallel",)),
    )(posf, jnp.asarray(table), after)


def _in_proj_a(x, g0, w_in, c, s1, s2, after):
    def body(x_ref, g_ref, w_ref, c_ref, s1_ref, s2_ref, hn_ref, q_ref, k_ref, v_ref, qm_ref, z_ref, proj):
        xf = x_ref[...]
        hn = xf * lax.rsqrt(jnp.mean(xf * xf, axis=-1, keepdims=True) + EPS) * g_ref[...]
        hb = hn.astype(BF16)
        hn_ref[...] = hb
        for s in range(4):
            proj[:, s * SH_A:(s + 1) * SH_A] = _dot(hb, w_ref[s])
        cc, a1, a2 = c_ref[...], s1_ref[...], s2_ref[...]
        for j in range(NQ // 128):
            q_ref[:, j * 128:(j + 1) * 128] = (
                _rope_fwd(proj[:, j * 128:(j + 1) * 128], cc, a1, a2) * SCALE).astype(BF16)
            k_ref[:, j * 128:(j + 1) * 128] = _rope_fwd(
                proj[:, NQ + j * 128:NQ + (j + 1) * 128], cc, a1, a2).astype(BF16)
        v_ref[...] = proj[:, 2 * NQ:3 * NQ].astype(BF16)
        qm_ref[...] = proj[:, 3 * NQ:3 * NQ + MW].astype(BF16)
        z_ref[...] = proj[:, 3 * NQ + MW:]

    return pl.pallas_call(
        functools.partial(_skip_arg, body, 6), name="in_proj_a", grid=(NT,),
        in_specs=[_rows(D), _full((1, D)), _full((4, D, SH_A)), _rows(128), _rows(128), _rows(128),
                  pl.BlockSpec(memory_space=pl.ANY)],
        out_specs=[_rows(D), _rows(NQ), _rows(NQ), _rows(NQ), _rows(MW), _rows(BR_A)],
        out_shape=[_sds((S, D), BF16), _sds((S, NQ), BF16), _sds((S, NQ), BF16), _sds((S, NQ), BF16),
                   _sds((S, MW), BF16), _sds((S, BR_A), F32)],
        scratch_shapes=[pltpu.VMEM((TM, IN_A), F32)],
        compiler_params=_params(("parallel",)),
    )(x, g0, w_in, c, s1, s2, after)


def _mem_fwd(mem, mg, wkv):
    def body(mem_ref, mg_ref, w_ref, memn_ref, kv_ref):
        mf = mem_ref[...]
        n = mf * lax.rsqrt(jnp.mean(mf * mf, axis=-1, keepdims=True) + EPS)
        for i in range(2):
            mn = (n * mg_ref[i:i + 1, :]).astype(BF16)
            memn_ref[i] = mn
            acc = _dot(mn[:, 0:NM], w_ref[0, i])
            for s in range(1, 4):
                acc += _dot(mn[:, s * NM:(s + 1) * NM], w_ref[s, i])
            kv_ref[i] = acc.astype(BF16)

    return pl.pallas_call(
        body, name="mem_fwd", grid=(1,),
        in_specs=[_full((NM, D)), _full((2, D)), _full((4, 2, NM, 2 * MW))],
        out_specs=[_full((2, NM, D)), _full((2, NM, 2 * MW))],
        out_shape=[_sds((2, NM, D), BF16), _sds((2, NM, 2 * MW), BF16)],
        compiler_params=_params(("arbitrary",)),
    )(mem, mg, wkv)


def _band_mask(j):
    qi = lax.broadcasted_iota(jnp.int32, (QBLK, 2 * QBLK), 0)
    kj = lax.broadcasted_iota(jnp.int32, (QBLK, 2 * QBLK), 1)
    dist = qi + QBLK - kj
    return (dist >= 0) & (dist <= QBLK) & ((kj >= QBLK) | (j > 0))


LANES = 128
NCHUNK = GW // LANES
FWD_UNROLL = 16
BWD_UNROLL = 4


def _perm_matrix(d):
    n = TM // d
    p = np.zeros((TM, TM), np.float32)
    for r in range(d):
        for i in range(n):
            p[r * n + i, i * d + r] = 1.0
    return p


def _split_dot(p, x, parts):
    hi = x.astype(BF16)
    rem = x - hi.astype(F32)
    lo = rem.astype(BF16)
    both = _dot(p, jnp.concatenate([hi, lo], axis=1))
    acc = both[:, :LANES] + both[:, LANES:]
    if parts == 3:
        acc = acc + _dot(p, (rem - lo.astype(F32)).astype(BF16))
    return acc


def _pair_dot(p, a, b):
    both = _dot(p, jnp.concatenate([a, b], axis=1))
    return both[:, :LANES], both[:, LANES:]


def _tile_to_streams(y, dst, t, d):
    n, ln = TM // d, S // d
    for r in range(d):
        dst[r * ln + t * n:r * ln + (t + 1) * n, :] = y[r * n:(r + 1) * n].astype(dst.dtype)


def _tile_from_streams(src, t, d):
    n, ln = TM // d, S // d
    return jnp.concatenate([src[r * ln + t * n:r * ln + (t + 1) * n, :] for r in range(d)], axis=0)


def _head_masks():
    first = lax.broadcasted_iota(jnp.int32, (TM, LANES), 1) < HD
    return first, jnp.logical_not(first)


def _attn_fwd(q, k, v, g):
    d = DILATIONS[g]
    nb = S // d // QBLK
    perm = _perm_matrix(d)

    def body(q_ref, k_ref, v_ref, p_ref, pt_ref, o_ref, l_ref, ls_ref, q0, q1, ks, vs, os_):
        first, second = _head_masks()
        pm = p_ref[...]
        for t in range(NT):
            rows = slice(t * TM, (t + 1) * TM)
            if d == 1:
                qt = q_ref[rows, :].astype(F32)
            else:
                qt, kt = _pair_dot(pm, q_ref[rows, :], k_ref[rows, :])
                _tile_to_streams(kt, ks, t, d)
                _tile_to_streams(_dot(pm, v_ref[rows, :]), vs, t, d)
            _tile_to_streams(jnp.where(first, qt, 0.0), q0, t, d)
            _tile_to_streams(jnp.where(second, qt, 0.0), q1, t, d)
        kref, vref = (k_ref, v_ref) if d == 1 else (ks, vs)
        oref, lref = (o_ref, l_ref) if d == 1 else (os_, ls_ref)

        def blk(b, carry):
            r0 = pl.multiple_of(b * QBLK, QBLK)
            p0 = pl.multiple_of(jnp.maximum(b - 1, 0) * QBLK, QBLK)
            kk = jnp.concatenate([kref[pl.ds(p0, QBLK), :], kref[pl.ds(r0, QBLK), :]], axis=0)
            vv = jnp.concatenate([vref[pl.ds(p0, QBLK), :], vref[pl.ds(r0, QBLK), :]], axis=0)
            valid = _band_mask(b & (nb - 1))
            acc, den, lse = [], [], []
            for qh in (q0, q1):
                s = jnp.where(valid, _dot_nt(qh[pl.ds(r0, QBLK), :], kk), NEG)
                m = jnp.max(s, axis=-1, keepdims=True)
                e = jnp.exp(s - m)
                l = jnp.sum(e, axis=-1, keepdims=True)
                acc.append(_dot(e.astype(BF16), vv))
                den.append(l)
                lse.append(m + jnp.log(l))
            f = first[:QBLK]
            oref[pl.ds(r0, QBLK), :] = jnp.where(f, acc[0], acc[1]) / jnp.where(f, den[0], den[1])
            lref[pl.ds(r0, QBLK), :] = jnp.where(f, lse[0], lse[1])
            return carry

        lax.fori_loop(0, S // QBLK, blk, 0, unroll=FWD_UNROLL)
        if d > 1:
            ptm = pt_ref[...]
            for t in range(NT):
                rows = slice(t * TM, (t + 1) * TM)
                o_ref[rows, :] = _split_dot(ptm, _tile_from_streams(os_, t, d), 2)
                l_ref[rows, :] = _split_dot(ptm, _tile_from_streams(ls_ref, t, d), 3)

    qkv_spec = pl.BlockSpec((S, LANES), lambda c: (0, g * NCHUNK + c))
    out_spec = pl.BlockSpec((S, LANES), lambda c: (0, c))
    n_out = 2 if d == 1 else 3
    outs = pl.pallas_call(
        body if d > 1 else functools.partial(_drop_arg, body, 7), name=f"attn_fwd_g{g}", grid=(NCHUNK,),
        in_specs=[qkv_spec] * 3 + [_full((TM, TM))] * 2, out_specs=[out_spec] * n_out,
        out_shape=[_sds((S, GW), F32)] * n_out,
        scratch_shapes=[pltpu.VMEM((S, LANES), BF16)] * 4 + [pltpu.VMEM((S, LANES), F32)],
        compiler_params=_params(("parallel",)),
    )(q, k, v, jnp.asarray(perm, BF16), jnp.asarray(perm.T, BF16))
    return (outs[0], outs[1], outs[1]) if d == 1 else tuple(outs)


def _drop_arg(body, pos, *refs):
    return body(*refs[:pos], None, *refs[pos:])


def _attn_out(os_, ls, qm, kv0, z, x, w_out):
    def body(o0, o1, o2, l0, l1, l2, qm_ref, kv_ref, z_ref, x_ref, w_ref, h_ref, ybuf):
        _, mix = _merge((o0, o1, o2), (l0, l1, l2))
        sz, _ = _silu_parts(z_ref[...])
        ybuf[:, :GW] = (mix * sz[:, :GW]).astype(BF16)
        for h, (_, mo) in enumerate(_mem_attn(qm_ref[...], kv_ref[...])):
            sl = slice(GW + h * HD, GW + (h + 1) * HD)
            ybuf[:, sl] = (mo * sz[:, sl]).astype(BF16)
        yb = ybuf[...]
        for s in range(4):
            cs = slice(s * SH_O, (s + 1) * SH_O)
            h_ref[:, cs] = x_ref[:, cs] + _dot(yb, w_ref[s])

    return pl.pallas_call(
        body, name="attn_out", grid=(NT,),
        in_specs=[_rows(GW)] * 6 + [_rows(MW), _full((NM, 2 * MW)), _rows(BR_A), _rows(D), _full((4, BR_A, SH_O))],
        out_specs=_rows(D), out_shape=_sds((S, D), F32),
        scratch_shapes=[pltpu.VMEM((TM, BR_A), BF16)],
        compiler_params=_params(("parallel",)),
    )(*os_, *ls, qm, kv0, z, x, w_out)


def _in_proj_b(h1, g1, w_in):
    def body(x_ref, g_ref, w_ref, hn_ref, bg_ref, cg_ref, u_ref, qm_ref, z_ref, proj):
        xf = x_ref[...]
        hn = xf * lax.rsqrt(jnp.mean(xf * xf, axis=-1, keepdims=True) + EPS) * g_ref[...]
        hb = hn.astype(BF16)
        hn_ref[...] = hb
        for s in range(4):
            proj[:, s * SH_B:(s + 1) * SH_B] = _dot(hb, w_ref[s])
        bg_ref[...] = proj[:, :D]
        cg_ref[...] = proj[:, D:2 * D]
        u_ref[...] = proj[:, 2 * D:3 * D]
        qm_ref[...] = proj[:, 3 * D:3 * D + MW].astype(BF16)
        z_ref[...] = proj[:, 3 * D + MW:]

    return pl.pallas_call(
        body, name="in_proj_b", grid=(NT,),
        in_specs=[_rows(D), _full((1, D)), _full((4, D, SH_B))],
        out_specs=[_rows(D), _rows(D), _rows(D), _rows(D), _rows(MW), _rows(BR_B)],
        out_shape=[_sds((S, D), BF16), _sds((S, D), F32), _sds((S, D), F32), _sds((S, D), F32),
                   _sds((S, MW), BF16), _sds((S, BR_B), F32)],
        scratch_shapes=[pltpu.VMEM((TM, IN_B), F32)],
        compiler_params=_params(("parallel",)),
    )(h1, g1, w_in)


def _prev8(width):
    return pl.BlockSpec((8, width), lambda i: (jnp.maximum(i * (TM // 8) - 1, 0), 0))


def _conv_out_loss(bg, cg, u, cw, qm, kv1, z, h1, w_out, fg, tgt):
    def body(bg_ref, cg_ref, u_ref, cgp_ref, up_ref, cw_ref, qm_ref, kv_ref, z_ref, h_ref, w_ref, fg_ref, t_ref,
             dh_ref, loss_ref, dfg_ref, ybuf):
        i = pl.program_id(0)
        a, a1, a2 = _conv_taps(cg_ref[...], u_ref[...], cgp_ref[...], up_ref[...], i == 0)
        conv = cw_ref[0:1, :] * a2 + cw_ref[1:2, :] * a1 + cw_ref[2:3, :] * a
        sz, _ = _silu_parts(z_ref[...])
        ybuf[:, :D] = (bg_ref[...] * conv * sz[:, :D]).astype(BF16)
        for h, (_, mo) in enumerate(_mem_attn(qm_ref[...], kv_ref[...])):
            sl = slice(D + h * HD, D + (h + 1) * HD)
            ybuf[:, sl] = (mo * sz[:, sl]).astype(BF16)
        h2 = h_ref[...] + _dot(ybuf[...], w_ref[...])
        rstd = lax.rsqrt(jnp.mean(h2 * h2, axis=-1, keepdims=True) + EPS)
        n = h2 * rstd
        fgv = fg_ref[...]
        err = n * fgv - t_ref[...]
        dout = err * (1.0 / D)
        dn = dout * fgv
        dh_ref[...] = rstd * (dn - n * jnp.mean(dn * n, axis=-1, keepdims=True))

        @pl.when(i == 0)
        def _():
            loss_ref[...] = jnp.zeros_like(loss_ref)
            dfg_ref[...] = jnp.zeros_like(dfg_ref)

        loss_ref[...] += jnp.sum(err * err) * (0.5 / D)
        dfg_ref[...] += jnp.sum(dout * n, axis=0, keepdims=True)

    return pl.pallas_call(
        body, name="conv_out_loss", grid=(NT,),
        in_specs=[_rows(D), _rows(D), _rows(D), _prev8(D), _prev8(D), _full((8, D)), _rows(MW),
                  _full((NM, 2 * MW)), _rows(BR_B), _rows(D), _full((BR_B, D)), _full((1, D)), _rows(D)],
        out_specs=[_rows(D), _full((1, 128)), _full((1, D))],
        out_shape=[_sds((S, D), F32), _sds((1, 128), F32), _sds((1, D), F32)],
        scratch_shapes=[pltpu.VMEM((TM, BR_B), BF16)],
        compiler_params=_params(("arbitrary",)),
    )(bg, cg, u, cg, u, cw, qm, kv1, z, h1, w_out, fg, tgt)


def _conv_bwd(dh2, bg, cg, u, cw, qm, kv1, z, w_out):
    rev = lambda i: (NT - 1 - i, 0)
    rows = lambda w: pl.BlockSpec((TM, w), rev)
    prev8 = pl.BlockSpec((8, D), lambda i: (jnp.maximum((NT - 1 - i) * (TM // 8) - 1, 0), 0))

    def body(dh_ref, bg_ref, cg_ref, u_ref, cgp_ref, up_ref, cw_ref, qm_ref, kv_ref, z_ref, w_ref,
             dproj_ref, dw_ref, dcw_ref, dkv_ref, ybuf, carry):
        i = pl.program_id(0)

        @pl.when(i == 0)
        def _():
            dw_ref[...] = jnp.zeros_like(dw_ref)
            dcw_ref[...] = jnp.zeros_like(dcw_ref)
            dkv_ref[...] = jnp.zeros_like(dkv_ref)
            carry[...] = jnp.zeros_like(carry)

        bgv, cgv, uv = bg_ref[...], cg_ref[...], u_ref[...]
        a, a1, a2 = _conv_taps(cgv, uv, cgp_ref[...], up_ref[...], i == NT - 1)
        w0, w1, w2 = cw_ref[0:1, :], cw_ref[1:2, :], cw_ref[2:3, :]
        conv = w0 * a2 + w1 * a1 + w2 * a
        mix = bgv * conv
        zv = z_ref[...]
        sz, dsz = _silu_parts(zv)
        qmv, kvv = qm_ref[...], kv_ref[...]
        heads = _mem_attn(qmv, kvv)
        ybuf[:, :D] = (mix * sz[:, :D]).astype(BF16)
        for h, (_, mo) in enumerate(heads):
            sl = slice(D + h * HD, D + (h + 1) * HD)
            ybuf[:, sl] = (mo * sz[:, sl]).astype(BF16)
        dhb = dh_ref[...].astype(BF16)
        dw_ref[...] += _dot_tn(ybuf[...], dhb)
        dy = _dot_nt(dhb, w_ref[...])
        dcat = dy * sz
        dproj_ref[:, 3 * D + MW:3 * D + MW + D] = (dy[:, :D] * mix * dsz[:, :D]).astype(BF16)
        for h, (_, mo) in enumerate(heads):
            sl = slice(D + h * HD, D + (h + 1) * HD)
            dproj_ref[:, 3 * D + MW + D + h * HD:3 * D + MW + D + (h + 1) * HD] = (
                dy[:, sl] * mo * dsz[:, sl]).astype(BF16)
        dmix = dcat[:, :D]
        dproj_ref[:, :D] = (dmix * conv).astype(BF16)
        dc = dmix * bgv
        nxt = carry[...]
        row = lax.broadcasted_iota(jnp.int32, dc.shape, 0)
        dc1 = jnp.where(row == TM - 1, nxt[0:1, :], pltpu.roll(dc, TM - 1, 0))
        dc2 = jnp.where(row == TM - 2, nxt[0:1, :], jnp.where(row == TM - 1, nxt[1:2, :], pltpu.roll(dc, TM - 2, 0)))
        carry[...] = dc[0:8, :]
        da = w2 * dc + w1 * dc1 + w0 * dc2
        dproj_ref[:, D:2 * D] = (da * uv).astype(BF16)
        dproj_ref[:, 2 * D:3 * D] = (da * cgv).astype(BF16)
        dcw_ref[0:1, :] += jnp.sum(dc * a2, axis=0, keepdims=True)
        dcw_ref[1:2, :] += jnp.sum(dc * a1, axis=0, keepdims=True)
        dcw_ref[2:3, :] += jnp.sum(dc * a, axis=0, keepdims=True)

        def dqm_store(h, val):
            dproj_ref[:, 3 * D + h * HD:3 * D + (h + 1) * HD] = val.astype(BF16)

        _mem_attn_bwd(dcat[:, D:], heads, qmv, kvv, dqm_store, dkv_ref)

    return pl.pallas_call(
        body, name="conv_bwd", grid=(NT,),
        in_specs=[rows(D), rows(D), rows(D), rows(D), prev8, prev8, _full((8, D)), rows(MW),
                  _full((NM, 2 * MW)), rows(BR_B), _full((BR_B, D))],
        out_specs=[rows(IN_B), _full((BR_B, D)), _full((8, D)), _full((NM, 2 * MW))],
        out_shape=[_sds((S, IN_B), BF16), _sds((BR_B, D), F32), _sds((8, D), F32), _sds((NM, 2 * MW), F32)],
        scratch_shapes=[pltpu.VMEM((TM, BR_B), BF16), pltpu.VMEM((8, D), F32)],
        compiler_params=_params(("arbitrary",)),
    )(dh2, bg, cg, u, cg, u, cw, qm, kv1, z, w_out)


def _in_proj_bwd(dproj, w_in, xin, g, dres, after, width, name):
    sh = width // 4

    def body(dp_ref, w_ref, x_ref, g_ref, dr_ref, dx_ref, dg_ref):
        i = pl.program_id(0)
        dhn = _dot_nt(dp_ref[:, 0:sh], w_ref[0])
        for s in range(1, 4):
            dhn += _dot_nt(dp_ref[:, s * sh:(s + 1) * sh], w_ref[s])
        xf = x_ref[...]
        rstd = lax.rsqrt(jnp.mean(xf * xf, axis=-1, keepdims=True) + EPS)
        n = xf * rstd
        dn = dhn * g_ref[...]
        dx_ref[...] = dr_ref[...] + rstd * (dn - n * jnp.mean(dn * n, axis=-1, keepdims=True))

        @pl.when(i == 0)
        def _():
            dg_ref[...] = jnp.zeros_like(dg_ref)

        dg_ref[...] += jnp.sum(dhn * n, axis=0, keepdims=True)

    return pl.pallas_call(
        functools.partial(_skip_arg, body, 5), name=name, grid=(NT,),
        in_specs=[_rows(width), _full((4, D, sh)), _rows(D), _full((1, D)), _rows(D), pl.BlockSpec(memory_space=pl.ANY)],
        out_specs=[_rows(D), _full((1, D))],
        out_shape=[_sds((S, D), F32), _sds((1, D), F32)],
        compiler_params=_params(("arbitrary",)),
    )(dproj, w_in, xin, g, dres, after)


def _w_in_grad(hn, dproj, width, name):
    sh = width // 4

    def body(hn_ref, dp_ref, dw_ref):
        dw_ref[0] = _dot_tn(hn_ref[...], dp_ref[...])

    return pl.pallas_call(
        body, name=name, grid=(4,),
        in_specs=[_full((S, D)), pl.BlockSpec((S, sh), lambda s: (0, s))],
        out_specs=pl.BlockSpec((1, D, sh), lambda s: (s, 0, 0)),
        out_shape=_sds((4, D, sh), F32),
        compiler_params=_params(("parallel",)),
    )(hn, dproj)


def _attn_out_bwd(dh1, os_, ls, qm, kv0, z, w_out, after):
    ones_bd = np.kron(np.eye(GW // HD, dtype=np.float32), np.ones((HD, HD), np.float32))

    def body(dh_ref, o0, o1, o2, l0, l1, l2, qm_ref, kv_ref, z_ref, w_ref, bd_ref,
             do0, do1, do2, dd0, dd1, dd2, dqm_ref, dz_ref, dw_ref, dkv_ref, ybuf):
        i = pl.program_id(0)

        @pl.when(i == 0)
        def _():
            dw_ref[...] = jnp.zeros_like(dw_ref)
            dkv_ref[...] = jnp.zeros_like(dkv_ref)

        ws, mix = _merge((o0, o1, o2), (l0, l1, l2))
        sz, dsz = _silu_parts(z_ref[...])
        qmv, kvv = qm_ref[...], kv_ref[...]
        heads = _mem_attn(qmv, kvv)
        ybuf[:, :GW] = (mix * sz[:, :GW]).astype(BF16)
        for h, (_, mo) in enumerate(heads):
            sl = slice(GW + h * HD, GW + (h + 1) * HD)
            ybuf[:, sl] = (mo * sz[:, sl]).astype(BF16)
        yb = ybuf[...]
        dh = dh_ref[...]
        dy = None
        for s in range(4):
            dhb = dh[:, s * SH_O:(s + 1) * SH_O].astype(BF16)
            dw_ref[s] += _dot_tn(yb, dhb)
            part = _dot_nt(dhb, w_ref[s])
            dy = part if dy is None else dy + part
        dcat = dy * sz
        dz_ref[:, :GW] = (dy[:, :GW] * mix * dsz[:, :GW]).astype(BF16)
        for h, (_, mo) in enumerate(heads):
            sl = slice(GW + h * HD, GW + (h + 1) * HD)
            dz_ref[:, sl] = (dy[:, sl] * mo * dsz[:, sl]).astype(BF16)
        dmix = dcat[:, :GW]
        prod = dmix * mix
        hi = prod.astype(BF16)
        lo = (prod - hi.astype(F32)).astype(BF16)
        bd = bd_ref[...]
        tot = _dot(hi, bd) + _dot(lo, bd)
        for w, do_ref, dd_ref in zip(ws, (do0, do1, do2), (dd0, dd1, dd2)):
            do_ref[...] = (w * dmix).astype(BF16)
            dd_ref[...] = w * tot

        def dqm_store(h, val):
            dqm_ref[:, h * HD:(h + 1) * HD] = val.astype(BF16)

        _mem_attn_bwd(dcat[:, GW:], heads, qmv, kvv, dqm_store, dkv_ref)

    return pl.pallas_call(
        functools.partial(_skip_arg, body, 12), name="attn_out_bwd", grid=(NT,),
        in_specs=[_rows(D)] + [_rows(GW)] * 6 + [_rows(MW), _full((NM, 2 * MW)), _rows(BR_A),
                                                   _full((4, BR_A, SH_O)), _full((GW, GW)),
                                                   pl.BlockSpec(memory_space=pl.ANY)],
        out_specs=[_rows(GW)] * 6 + [_rows(MW), _rows(BR_A), _full((4, BR_A, SH_O)), _full((NM, 2 * MW))],
        out_shape=[_sds((S, GW), BF16)] * 3 + [_sds((S, GW), F32)] * 3 + [
            _sds((S, MW), BF16), _sds((S, BR_A), BF16), _sds((4, BR_A, SH_O), F32), _sds((NM, 2 * MW), F32)],
        scratch_shapes=[pltpu.VMEM((TM, BR_A), BF16)],
        compiler_params=_params(("arbitrary",)),
    )(dh1, *os_, *ls, qm, kv0, z, w_out, jnp.asarray(ones_bd, dtype=BF16), after)


def _attn_bwd(q, k, v, do, lse_s, dd, g):
    d = DILATIONS[g]
    nb = S // d // QBLK
    perm = _perm_matrix(d)

    def body(q_ref, k_ref, v_ref, do_ref, l_ref, dd_ref, p_ref, pt_ref, dq_ref, dk_ref, dv_ref,
             q0, q1, g0, g1, ks, vs, dds, dqs, dks, dvs):
        first, second = _head_masks()
        pm = p_ref[...]
        for t in range(NT):
            rows = slice(t * TM, (t + 1) * TM)
            if d == 1:
                qt = q_ref[rows, :].astype(F32)
                gt = do_ref[rows, :].astype(F32)
            else:
                qt, gt = _pair_dot(pm, q_ref[rows, :], do_ref[rows, :])
                kt, vt = _pair_dot(pm, k_ref[rows, :], v_ref[rows, :])
                _tile_to_streams(kt, ks, t, d)
                _tile_to_streams(vt, vs, t, d)
                _tile_to_streams(_split_dot(pm, dd_ref[rows, :], 2), dds, t, d)
            _tile_to_streams(jnp.where(first, qt, 0.0), q0, t, d)
            _tile_to_streams(jnp.where(second, qt, 0.0), q1, t, d)
            _tile_to_streams(jnp.where(first, gt, 0.0), g0, t, d)
            _tile_to_streams(jnp.where(second, gt, 0.0), g1, t, d)
        kref, vref, ddref = (k_ref, v_ref, dd_ref) if d == 1 else (ks, vs, dds)
        dqref, dkref, dvref = (dq_ref, dk_ref, dv_ref) if d == 1 else (dqs, dks, dvs)
        dkref[...] = jnp.zeros_like(dkref)
        dvref[...] = jnp.zeros_like(dvref)

        def blk(b, carry):
            r0 = pl.multiple_of(b * QBLK, QBLK)
            p0 = pl.multiple_of(jnp.maximum(b - 1, 0) * QBLK, QBLK)
            kk = jnp.concatenate([kref[pl.ds(p0, QBLK), :], kref[pl.ds(r0, QBLK), :]], axis=0)
            vv = jnp.concatenate([vref[pl.ds(p0, QBLK), :], vref[pl.ds(r0, QBLK), :]], axis=0)
            lb = l_ref[pl.ds(r0, QBLK), :]
            ddb = ddref[pl.ds(r0, QBLK), :]
            valid = _band_mask(b & (nb - 1))
            dqh, dkk, dvv = [], None, None
            for h, (qh, gh) in enumerate(((q0, g0), (q1, g1))):
                qb = qh[pl.ds(r0, QBLK), :]
                gb = gh[pl.ds(r0, QBLK), :]
                s = _dot_nt(qb, kk)
                p = jnp.where(valid, jnp.exp(s - lb[:, h * HD:h * HD + 1]), 0.0)
                dp = _dot_nt(gb, vv)
                ds = (p * (dp - ddb[:, h * HD:h * HD + 1])).astype(BF16)
                dqh.append(_dot(ds, kk))
                tk = _dot_tn(ds, qb)
                tv = _dot_tn(p.astype(BF16), gb)
                dkk = tk if dkk is None else dkk + tk
                dvv = tv if dvv is None else dvv + tv
            dqref[pl.ds(r0, QBLK), :] = jnp.where(first[:QBLK], dqh[0], dqh[1])
            dkref[pl.ds(p0, QBLK), :] += dkk[:QBLK]
            dkref[pl.ds(r0, QBLK), :] += dkk[QBLK:]
            dvref[pl.ds(p0, QBLK), :] += dvv[:QBLK]
            dvref[pl.ds(r0, QBLK), :] += dvv[QBLK:]
            return carry

        lax.fori_loop(0, S // QBLK, blk, 0, unroll=BWD_UNROLL)
        if d > 1:
            ptm = pt_ref[...]
            for t in range(NT):
                rows = slice(t * TM, (t + 1) * TM)
                dq_ref[rows, :] = _split_dot(ptm, _tile_from_streams(dqs, t, d), 2)
                dk_ref[rows, :] = _split_dot(ptm, _tile_from_streams(dks, t, d), 2)
                dv_ref[rows, :] = _split_dot(ptm, _tile_from_streams(dvs, t, d), 2)

    qkv_spec = pl.BlockSpec((S, LANES), lambda c: (0, g * NCHUNK + c))
    one_spec = pl.BlockSpec((S, LANES), lambda c: (0, c))
    return pl.pallas_call(
        body, name=f"attn_bwd_g{g}", grid=(NCHUNK,),
        in_specs=[qkv_spec] * 3 + [one_spec] * 3 + [_full((TM, TM))] * 2, out_specs=[one_spec] * 3,
        out_shape=[_sds((S, GW), F32)] * 3,
        scratch_shapes=[pltpu.VMEM((S, LANES), BF16)] * 6 + [pltpu.VMEM((S, LANES), F32)] * 4,
        compiler_params=_params(("parallel",)),
    )(q, k, v, do, lse_s, dd, jnp.asarray(perm, BF16), jnp.asarray(perm.T, BF16))


def _qkv_bwd(dqs, dks, dvs, dqm, dz, c, s1, s2):
    def body(q0, q1, q2, k0, k1, k2, v0, v1, v2, dqm_ref, dz_ref, c_ref, s1_ref, s2_ref, dp_ref):
        cc, a1, a2 = c_ref[...], s1_ref[...], s2_ref[...]
        for g, (qr, kr, vr) in enumerate(((q0, k0, v0), (q1, k1, v1), (q2, k2, v2))):
            for j in range(GW // 128):
                ls_ = slice(j * 128, (j + 1) * 128)
                c0 = g * GW + j * 128
                dp_ref[:, c0:c0 + 128] = (_rope_bwd(qr[:, ls_], cc, a1, a2) * SCALE).astype(BF16)
                dp_ref[:, NQ + c0:NQ + c0 + 128] = _rope_bwd(kr[:, ls_], cc, a1, a2).astype(BF16)
            dp_ref[:, 2 * NQ + g * GW:2 * NQ + (g + 1) * GW] = vr[...].astype(BF16)
        dp_ref[:, 3 * NQ:3 * NQ + MW] = dqm_ref[...]
        dp_ref[:, 3 * NQ + MW:] = dz_ref[...]

    return pl.pallas_call(
        body, name="qkv_bwd", grid=(NT,),
        in_specs=[_rows(GW)] * 9 + [_rows(MW), _rows(BR_A), _rows(128), _rows(128), _rows(128)],
        out_specs=_rows(IN_A), out_shape=_sds((S, IN_A), BF16),
        compiler_params=_params(("parallel",)),
    )(*dqs, *dks, *dvs, dqm, dz, c, s1, s2)


def _mem_bwd(mem, mg, memn, wkv, dkv0, dkv1):
    def body(mem_ref, mg_ref, memn_ref, w_ref, d0_ref, d1_ref, dw_ref, dg_ref):
        mf = mem_ref[...]
        n = mf * lax.rsqrt(jnp.mean(mf * mf, axis=-1, keepdims=True) + EPS)
        for i, d_ref in enumerate((d0_ref, d1_ref)):
            dkv = d_ref[...].astype(BF16)
            mn = memn_ref[i]
            for s in range(4):
                cs = slice(s * NM, (s + 1) * NM)
                dw_ref[s, i] = _dot_tn(mn[:, cs], dkv)
                dmn = _dot_nt(dkv, w_ref[s, i])
                dg_ref[i:i + 1, cs] = jnp.sum(dmn * n[:, cs], axis=0, keepdims=True)

    return pl.pallas_call(
        body, name="mem_bwd", grid=(1,),
        in_specs=[_full((NM, D)), _full((2, D)), _full((2, NM, D)), _full((4, 2, NM, 2 * MW)),
                  _full((NM, 2 * MW)), _full((NM, 2 * MW))],
        out_specs=[_full((4, 2, NM, 2 * MW)), _full((2, D))],
        out_shape=[_sds((4, 2, NM, 2 * MW), F32), _sds((2, D), F32)],
        compiler_params=_params(("arbitrary",)),
    )(mem, mg, memn, wkv, dkv0, dkv1)


MESH = pl.DeviceIdType.MESH
ANY = pl.BlockSpec(memory_space=pl.ANY)
BIG = (("wkv", 2, NM, 2 * MW), ("w_in_a", 1, D, SH_A), ("w_out_a", 1, BR_A, SH_O),
       ("w_in_b", 1, D, SH_B), ("w_out_b", 1, BR_B // 4, D))
NBIG = len(BIG)
CW_ROWS = 8


def _place():
    x, y, c = lax.axis_index("x"), lax.axis_index("y"), lax.axis_index("c")
    chips = ((1 - x, y), (x, 1 - y), (1 - x, 1 - y))
    return x, y, c, chips


def _remote(src, dst, ssem, rsem, dev):
    return pltpu.make_async_remote_copy(src_ref=src, dst_ref=dst, send_sem=ssem, recv_sem=rsem,
                                        device_id=dev, device_id_type=MESH)


def _cast_weights(place, ws, after, idx, name):
    nblk = 4
    n = len(idx)
    dims = [BIG[w][1:] for w in idx]

    def body(pref, *refs):
        for i in range(n):
            refs[n + 1 + i][0] = refs[i][...].astype(BF16)

    grid_spec = pltpu.PrefetchScalarGridSpec(
        num_scalar_prefetch=1, grid=(nblk,),
        in_specs=[pl.BlockSpec((k, r // nblk, cdim), lambda i, pref: (0, i, 0)) for k, r, cdim in dims]
        + [pl.BlockSpec(memory_space=pl.ANY)],
        out_specs=[pl.BlockSpec((1, k, r // nblk, cdim), lambda i, pref: (pref[1], 0, i, 0)) for k, r, cdim in dims])
    return pl.pallas_call(
        body, name=name, grid_spec=grid_spec,
        out_shape=[_sds((4, k, r, cdim), BF16) for k, r, cdim in dims],
        compiler_params=_params(("parallel",)),
    )(place, *ws, after)


LAYER_A = (0, 1, 2)
LAYER_B = (3, 4)
HBM = pl.BlockSpec(memory_space=pltpu.HBM)
SEM = pl.BlockSpec(memory_space=pltpu.SEMAPHORE)
EFFECT = pltpu.SideEffectType.DATAFLOW_SIDE_EFFECTING
TOKEN = (8, 128)


def _half(ref, w, which):
    h = BIG[w][2] // 2
    return ref.at[:, pl.ds(which * h, h), :]


def _skip_arg(body, pos, *refs):
    return body(*refs[:pos], *refs[pos + 1:])


def _gather_weights(wb, cw, idx, name):
    n = len(idx)

    def body(*refs):
        src_cw = refs[n]
        dst = refs[n + 1:2 * n + 2]
        loc_sem, send_sems, recv_sems, fsend_sems, frecv_sems = refs[2 * n + 2:]
        x, y, c, chips = _place()
        me = 2 * x + y
        loc = pltpu.make_async_copy(src_cw, dst[n].at[me], loc_sem)
        loc.start()
        sends = []
        for j, (px, py) in enumerate(chips):
            for i in range(n):
                mine = _half(dst[i].at[me], idx[i], c)
                sends.append(_remote(mine, mine, send_sems.at[j, i], recv_sems.at[j, i], (px, py, c)))
            sends.append(_remote(src_cw, dst[n].at[me], send_sems.at[j, n], recv_sems.at[j, n], (px, py, c)))
        for cp in sends:
            cp.start()
        fwds = []
        for j, (px, py) in enumerate(chips):
            for i in range(n):
                got = _half(dst[i].at[2 * px + py], idx[i], c)
                _remote(got, got, send_sems.at[j, i], recv_sems.at[j, i], (px, py, c)).wait_recv()
                fwds.append(_remote(got, got, fsend_sems.at[j, i], frecv_sems.at[j, i], (x, y, 1 - c)))
                fwds[-1].start()
            got = dst[n].at[2 * px + py]
            _remote(got, got, send_sems.at[j, n], recv_sems.at[j, n], (px, py, c)).wait_recv()
        for j, (px, py) in enumerate(chips):
            for i in range(n):
                got = _half(dst[i].at[2 * px + py], idx[i], 1 - c)
                _remote(got, got, fsend_sems.at[j, i], frecv_sems.at[j, i], (x, y, 1 - c)).wait_recv()
        for cp in sends + fwds:
            cp.wait_send()
        loc.wait()

    out_shape = [_sds(w.shape, BF16) for w in wb] + [_sds((4, CW_ROWS, SH_O), F32)]
    return pl.pallas_call(
        body, name=name, in_specs=[ANY] * (n + 1), out_specs=[ANY] * (n + 1), out_shape=out_shape,
        input_output_aliases={i: i for i in range(n)},
        scratch_shapes=[pltpu.SemaphoreType.DMA, pltpu.SemaphoreType.DMA((3, n + 1)),
                        pltpu.SemaphoreType.DMA((3, n + 1)), pltpu.SemaphoreType.DMA((3, n)),
                        pltpu.SemaphoreType.DMA((3, n))],
    )(*wb, cw)


def _gather_start(wb, after, idx, name):
    n = len(idx)

    def body(*refs):
        src = refs[:n]
        send_sems, recv_sems = refs[n + 1], refs[n + 2]
        token = refs[2 * n + 3]
        x, y, c, chips = _place()
        me = 2 * x + y
        for j, (px, py) in enumerate(chips):
            for i in range(n):
                mine = _half(src[i].at[me], idx[i], c)
                _remote(mine, mine, send_sems.at[j * n + i], recv_sems.at[j * n + i], (px, py, c)).start()
        token[...] = jnp.zeros(TOKEN, F32)

    outs = pl.pallas_call(
        body, name=name, in_specs=[HBM] * n + [ANY],
        out_specs=(SEM, SEM) + (HBM,) * n + (pl.BlockSpec(memory_space=pltpu.VMEM),),
        out_shape=(pltpu.SemaphoreType.DMA((3 * n,)), pltpu.SemaphoreType.DMA((3 * n,)))
        + tuple(pltpu.HBM(w.shape, w.dtype) for w in wb) + (_sds(TOKEN, F32),),
        input_output_aliases={i: 2 + i for i in range(n)},
        compiler_params=pltpu.CompilerParams(has_side_effects=EFFECT),
    )(*[pltpu.with_memory_space_constraint(w, pltpu.HBM) for w in wb], after)
    return outs[0], outs[1], list(outs[2:2 + n]), outs[2 + n]


def _gather_wait(send_sems, recv_sems, wb, after, idx, name):
    n = len(idx)

    def body(*refs):
        buf = refs[:n]
        send_sems, recv_sems = refs[n], refs[n + 1]
        x, y, c, chips = _place()
        me = 2 * x + y
        for j, (px, py) in enumerate(chips):
            for i in range(n):
                mine = _half(buf[i].at[me], idx[i], c)
                got = _half(buf[i].at[2 * px + py], idx[i], c)
                _remote(mine, mine, send_sems.at[j * n + i], recv_sems.at[j * n + i], (px, py, c)).wait_send()
                _remote(got, got, send_sems.at[j * n + i], recv_sems.at[j * n + i], (px, py, c)).wait_recv()

    outs = pl.pallas_call(
        body, name=name, in_specs=[HBM] * n + [SEM, SEM] + [ANY] * len(after), out_specs=(HBM,) * n,
        out_shape=tuple(pltpu.HBM(w.shape, w.dtype) for w in wb),
        input_output_aliases={i: i for i in range(n)},
        compiler_params=pltpu.CompilerParams(has_side_effects=EFFECT),
    )(*wb, send_sems, recv_sems, *after)
    return list(outs)


def _gather_forward(wb, idx, name, cw=None):
    n = len(idx)
    m = n if cw is None else n + 1

    def body(*refs):
        dst = refs[m:2 * m]
        send_sems, recv_sems = refs[2 * m], refs[2 * m + 1]
        x, y, c, chips = _place()
        cps = []
        for j, (px, py) in enumerate(chips):
            for i in range(n):
                got = _half(dst[i].at[2 * px + py], idx[i], c)
                cps.append(_remote(got, got, send_sems.at[j, i], recv_sems.at[j, i], (x, y, 1 - c)))
                cps[-1].start()
        if cw is not None:
            src_cw, loc_sem = refs[n], refs[2 * m + 2]
            me = 2 * x + y
            loc = pltpu.make_async_copy(src_cw, dst[n].at[me], loc_sem)
            loc.start()
            for j, (px, py) in enumerate(chips):
                cps.append(_remote(src_cw, dst[n].at[me], send_sems.at[j, n], recv_sems.at[j, n], (px, py, c)))
                cps[-1].start()
        for j, (px, py) in enumerate(chips):
            for i in range(n):
                got = _half(dst[i].at[2 * px + py], idx[i], 1 - c)
                _remote(got, got, send_sems.at[j, i], recv_sems.at[j, i], (x, y, 1 - c)).wait_recv()
            if cw is not None:
                got = dst[n].at[2 * px + py]
                _remote(got, got, send_sems.at[j, n], recv_sems.at[j, n], (px, py, c)).wait_recv()
        for cp in cps:
            cp.wait_send()
        if cw is not None:
            loc.wait()

    out_shape = [_sds(w.shape, BF16) for w in wb]
    scratch = [pltpu.SemaphoreType.DMA((3, m)), pltpu.SemaphoreType.DMA((3, m))]
    args = list(wb)
    if cw is not None:
        out_shape.append(_sds((4, CW_ROWS, SH_O), F32))
        scratch.append(pltpu.SemaphoreType.DMA)
        args.append(cw)
    return pl.pallas_call(
        body, name=name, in_specs=[ANY] * m, out_specs=[ANY] * m, out_shape=out_shape,
        input_output_aliases={i: i for i in range(n)}, scratch_shapes=scratch,
    )(*args)


def _pair_exchange(gs, idx, name):
    n = len(idx)

    def body(*refs):
        src, dst = refs[:n], refs[n:2 * n]
        send_sems, recv_sems = refs[2 * n:]
        x, y, c, _ = _place()
        cps = []
        for i in range(n):
            h = BIG[idx[i]][2] // 2
            cps.append(_remote(src[i].at[:, :, pl.ds((1 - c) * h, h), :], dst[i], send_sems.at[i], recv_sems.at[i],
                               (x, y, 1 - c)))
            cps[-1].start()
        for cp in cps:
            cp.wait()

    return pl.pallas_call(
        body, name=name, in_specs=[ANY] * n, out_specs=[ANY] * n,
        out_shape=[_sds((4, BIG[w][1], BIG[w][2] // 2, BIG[w][3]), F32) for w in idx],
        scratch_shapes=[pltpu.SemaphoreType.DMA((n,)), pltpu.SemaphoreType.DMA((n,))],
    )(*gs)


def _pair_start(gs, idx, name):
    n = len(idx)

    def body(*refs):
        src, land = refs[:n], refs[n:2 * n]
        send_sems, recv_sems = refs[2 * n], refs[2 * n + 1]
        token = refs[4 * n + 2]
        x, y, c, _ = _place()
        for i in range(n):
            h = BIG[idx[i]][2] // 2
            _remote(src[i].at[:, :, pl.ds((1 - c) * h, h), :], land[i], send_sems.at[i], recv_sems.at[i],
                    (x, y, 1 - c)).start()
        token[...] = jnp.zeros(TOKEN, F32)

    lands = [lax.empty((4, BIG[w][1], BIG[w][2] // 2, BIG[w][3]), F32) for w in idx]
    arrays = list(gs) + lands
    outs = pl.pallas_call(
        body, name=name, in_specs=[HBM] * (2 * n),
        out_specs=(SEM, SEM) + (HBM,) * (2 * n) + (pl.BlockSpec(memory_space=pltpu.VMEM),),
        out_shape=(pltpu.SemaphoreType.DMA((n,)), pltpu.SemaphoreType.DMA((n,)))
        + tuple(pltpu.HBM(a.shape, a.dtype) for a in arrays) + (_sds(TOKEN, F32),),
        input_output_aliases={i: 2 + i for i in range(2 * n)},
        compiler_params=pltpu.CompilerParams(has_side_effects=EFFECT),
    )(*[pltpu.with_memory_space_constraint(a, pltpu.HBM) for a in arrays])
    return outs[0], outs[1], list(outs[2:2 + n]), list(outs[2 + n:2 + 2 * n]), outs[2 + 2 * n]


def _pair_wait(send_sems, recv_sems, gs, lands, after, idx, name):
    n = len(idx)

    def body(*refs):
        src, land = refs[:n], refs[n:2 * n]
        send_sems, recv_sems = refs[2 * n], refs[2 * n + 1]
        x, y, c, _ = _place()
        for i in range(n):
            h = BIG[idx[i]][2] // 2
            cp = _remote(src[i].at[:, :, pl.ds((1 - c) * h, h), :], land[i], send_sems.at[i], recv_sems.at[i],
                         (x, y, 1 - c))
            cp.wait_send()
            cp.wait_recv()

    arrays = list(gs) + list(lands)
    outs = pl.pallas_call(
        body, name=name, in_specs=[HBM] * (2 * n) + [SEM, SEM] + [ANY] * len(after), out_specs=(HBM,) * (2 * n),
        out_shape=tuple(pltpu.HBM(a.shape, a.dtype) for a in arrays),
        input_output_aliases={i: i for i in range(2 * n)},
        compiler_params=pltpu.CompilerParams(has_side_effects=EFFECT),
    )(*arrays, send_sems, recv_sems, *after)
    return list(outs[:n]), list(outs[n:])


def _pair_sum(place, g, r1, i):
    _, k, r, cdim = BIG[i]
    h = r // 2

    def body(pref, g_ref, r_ref, o_ref):
        o_ref[...] = (g_ref[...] + r_ref[...]).astype(BF16)

    grid_spec = pltpu.PrefetchScalarGridSpec(
        num_scalar_prefetch=1, grid=(4, k),
        in_specs=[pl.BlockSpec((1, 1, h, cdim), lambda s, t, pref: (s, t, pref[0], 0)),
                  pl.BlockSpec((1, 1, h, cdim), lambda s, t, pref: (s, t, 0, 0))],
        out_specs=pl.BlockSpec((1, 1, h, cdim), lambda s, t, pref: (s, t, 0, 0)))
    return pl.pallas_call(
        body, name=f"pair_sum_{BIG[i][0]}", grid_spec=grid_spec, out_shape=_sds((4, k, h, cdim), BF16),
        compiler_params=_params(("parallel", "parallel")),
    )(place, g, r1)


def _pair_sums(place, gs, r1s, idx, name):
    n = len(idx)
    dims = [(BIG[w][1], BIG[w][2] // 2, BIG[w][3]) for w in idx]

    def body(pref, *refs):
        for i in range(n):
            refs[2 * n + i][...] = (refs[i][...] + refs[n + i][...]).astype(BF16)

    mine = [pl.BlockSpec((1, k, h, cdim), lambda s, pref: (s, 0, pref[0], 0)) for k, h, cdim in dims]
    whole = [pl.BlockSpec((1, k, h, cdim), lambda s, pref: (s, 0, 0, 0)) for k, h, cdim in dims]
    grid_spec = pltpu.PrefetchScalarGridSpec(num_scalar_prefetch=1, grid=(4,), in_specs=mine + whole, out_specs=whole)
    return pl.pallas_call(
        body, name=name, grid_spec=grid_spec, out_shape=[_sds((4, k, h, cdim), BF16) for k, h, cdim in dims],
        compiler_params=_params(("parallel",)),
    )(place, *gs, *r1s)


def _chip_start(ps, idx, name):
    n = len(idx)

    def body(*refs):
        src, land = refs[:n], refs[n:2 * n]
        send_sems, recv_sems = refs[2 * n], refs[2 * n + 1]
        token = refs[4 * n + 2]
        x, y, c, chips = _place()
        for j, (px, py) in enumerate(chips):
            for i in range(n):
                _remote(src[i].at[2 * px + py], land[i].at[j], send_sems.at[j * n + i], recv_sems.at[j * n + i],
                        (px, py, c)).start()
        token[...] = jnp.zeros(TOKEN, F32)

    lands = [lax.empty((3,) + p.shape[1:], BF16) for p in ps]
    outs = pl.pallas_call(
        body, name=name, in_specs=[HBM] * (2 * n),
        out_specs=(SEM, SEM) + (HBM,) * (2 * n) + (pl.BlockSpec(memory_space=pltpu.VMEM),),
        out_shape=(pltpu.SemaphoreType.DMA((3 * n,)), pltpu.SemaphoreType.DMA((3 * n,)))
        + tuple(pltpu.HBM(a.shape, a.dtype) for a in list(ps) + lands) + (_sds(TOKEN, F32),),
        input_output_aliases={i: 2 + i for i in range(2 * n)},
        compiler_params=pltpu.CompilerParams(has_side_effects=EFFECT),
    )(*[pltpu.with_memory_space_constraint(a, pltpu.HBM) for a in list(ps) + lands])
    return outs[0], outs[1], list(outs[2:2 + n]), list(outs[2 + n:2 + 2 * n]), outs[2 + 2 * n]


def _chip_wait(send_sems, recv_sems, ps, lands, after, idx, name):
    n = len(idx)

    def body(*refs):
        src, land = refs[:n], refs[n:2 * n]
        send_sems, recv_sems = refs[2 * n], refs[2 * n + 1]
        x, y, c, chips = _place()
        for j, (px, py) in enumerate(chips):
            for i in range(n):
                cp = _remote(src[i].at[2 * px + py], land[i].at[j], send_sems.at[j * n + i], recv_sems.at[j * n + i],
                             (px, py, c))
                cp.wait_send()
                cp.wait_recv()

    arrays = list(ps) + list(lands)
    outs = pl.pallas_call(
        body, name=name, in_specs=[HBM] * (2 * n) + [SEM, SEM] + [ANY] * len(after), out_specs=(HBM,) * (2 * n),
        out_shape=tuple(pltpu.HBM(a.shape, a.dtype) for a in arrays),
        input_output_aliases={i: i for i in range(2 * n)},
        compiler_params=pltpu.CompilerParams(has_side_effects=EFFECT),
    )(*arrays, send_sems, recv_sems, *after)
    return list(outs[n:])


def _chip_sum(place, g, r1, r2, i):
    _, k, r, cdim = BIG[i]
    h = r // 2

    def body(pref, g_ref, r1_ref, r2_ref, o_ref):
        acc = g_ref[0, 0] + r1_ref[0, 0]
        for j in range(3):
            acc = acc + r2_ref[j, 0].astype(F32)
        o_ref[0] = acc

    grid_spec = pltpu.PrefetchScalarGridSpec(
        num_scalar_prefetch=1, grid=(k,),
        in_specs=[pl.BlockSpec((1, 1, h, cdim), lambda t, pref: (pref[1], t, pref[0], 0)),
                  pl.BlockSpec((1, 1, h, cdim), lambda t, pref: (pref[1], t, 0, 0)),
                  pl.BlockSpec((3, 1, h, cdim), lambda t, pref: (0, t, 0, 0))],
        out_specs=pl.BlockSpec((1, h, cdim), lambda t, pref: (t, pref[0], 0)))
    return pl.pallas_call(
        body, name=f"chip_sum_{BIG[i][0]}", grid_spec=grid_spec, out_shape=_sds((k, r, cdim), F32),
        compiler_params=_params(("parallel",)),
    )(place, g, r1, r2)


def _chip_sums(place, gs, r1s, r2s, idx, name):
    n = len(idx)
    dims = [(BIG[w][1], BIG[w][2] // 4, BIG[w][3]) for w in idx]

    def body(pref, *refs):
        for i in range(n):
            acc = refs[i][0] + refs[n + i][0]
            for j in range(3):
                acc = acc + refs[2 * n + i][j].astype(F32)
            refs[3 * n + i][...] = acc

    in_specs = ([pl.BlockSpec((1, k, q, cdim), lambda t, pref: (pref[1], 0, pref[0] * 2 + t, 0)) for k, q, cdim in dims]
                + [pl.BlockSpec((1, k, q, cdim), lambda t, pref: (pref[1], 0, t, 0)) for k, q, cdim in dims]
                + [pl.BlockSpec((3, k, q, cdim), lambda t, pref: (0, 0, t, 0)) for k, q, cdim in dims])
    out_specs = [pl.BlockSpec((k, q, cdim), lambda t, pref: (0, pref[0] * 2 + t, 0)) for k, q, cdim in dims]
    grid_spec = pltpu.PrefetchScalarGridSpec(num_scalar_prefetch=1, grid=(2,), in_specs=in_specs, out_specs=out_specs)
    return pl.pallas_call(
        body, name=name, grid_spec=grid_spec, out_shape=[_sds(BIG[w][1:], F32) for w in idx],
        compiler_params=_params(("parallel",)),
    )(place, *gs, *r1s, *r2s)


def _pair_gather(hs, idx, name):
    n = len(idx)

    def body(*refs):
        dst = refs[n:2 * n]
        send_sems, recv_sems = refs[2 * n:]
        x, y, c, _ = _place()
        cps = []
        for i in range(n):
            mine = _half(dst[i], idx[i], c)
            cps.append(_remote(mine, mine, send_sems.at[i], recv_sems.at[i], (x, y, 1 - c)))
            cps[-1].start()
        for i in range(n):
            theirs = _half(dst[i], idx[i], 1 - c)
            _remote(theirs, theirs, send_sems.at[i], recv_sems.at[i], (x, y, 1 - c)).wait_recv()
        for cp in cps:
            cp.wait_send()

    return pl.pallas_call(
        body, name=name, in_specs=[ANY] * n, out_specs=[ANY] * n,
        out_shape=[_sds(BIG[w][1:], F32) for w in idx],
        input_output_aliases={i: i for i in range(n)},
        scratch_shapes=[pltpu.SemaphoreType.DMA((n,)), pltpu.SemaphoreType.DMA((n,))],
    )(*hs)


SMALL_ROWS = 40


def _all_reduce_small(pack, after):
    def body(p_ref, o_ref, slots, send_sems, recv_sems):
        x, y, c, _ = _place()
        me = 4 * x + 2 * y + c
        cps = []
        for r in range(1, 8):
            peer = (x if not r & 4 else 1 - x, y if not r & 2 else 1 - y, c if not r & 1 else 1 - c)
            cps.append(_remote(p_ref, slots.at[r], send_sems.at[r - 1], recv_sems.at[r - 1], peer))
            cps[-1].start()
        slots[0] = p_ref[...]
        for cp in cps:
            cp.wait()
        acc = slots[me]
        for dev in range(1, 8):
            acc = acc + slots[jnp.bitwise_xor(me, dev)]
        o_ref[...] = acc

    vm = pl.BlockSpec(memory_space=pltpu.VMEM)
    return pl.pallas_call(
        functools.partial(_skip_arg, body, 1), name="all_reduce_small", in_specs=[vm, ANY], out_specs=vm,
        out_shape=_sds((SMALL_ROWS, D), F32),
        scratch_shapes=[pltpu.VMEM((8, SMALL_ROWS, D), F32), pltpu.SemaphoreType.DMA((7,)),
                        pltpu.SemaphoreType.DMA((7,))],
    )(pack, after)


def _adamw_math(w, g, m, v):
    m = ADAM_B1 * m + (1.0 - ADAM_B1) * g
    v = ADAM_B2 * v + (1.0 - ADAM_B2) * (g * g)
    m_hat = m / (1.0 - ADAM_B1 ** ADAM_STEP)
    v_hat = v / (1.0 - ADAM_B2 ** ADAM_STEP)
    delta = -ADAM_LR * (m_hat / (jnp.sqrt(v_hat) + ADAM_EPS) + ADAM_WD * w)
    return delta, m, v


def _adamw_big(w, g, m, v, i):
    _, k, r, cdim = BIG[i]
    nblk = 4 if k == 1 else 1

    def body(w_ref, g_ref, m_ref, v_ref, d_ref, nm_ref, nv_ref, go_ref):
        gv = g_ref[...]
        d_ref[...], nm_ref[...], nv_ref[...] = _adamw_math(w_ref[...], gv, m_ref[...], v_ref[...])
        go_ref[...] = gv

    spec = pl.BlockSpec((1, r // nblk, cdim), lambda t, b: (t, b, 0))
    return pl.pallas_call(
        body, name=f"adamw_{BIG[i][0]}", grid=(k, nblk), in_specs=[spec] * 4, out_specs=[spec] * 4,
        out_shape=[_sds((k, r, cdim), F32)] * 4,
        compiler_params=_params(("parallel", "parallel")),
    )(w, g, m, v)


def _small_start(pack, after):
    def body(pack_ref, land_ref, after_ref, send_sems, recv_sems, pack_thru, land_thru, token):
        x, y, c, _ = _place()
        for r in range(1, 8):
            peer = (x if not r & 4 else 1 - x, y if not r & 2 else 1 - y, c if not r & 1 else 1 - c)
            _remote(pack_ref, land_ref.at[r - 1], send_sems.at[r - 1], recv_sems.at[r - 1], peer).start()
        token[...] = jnp.zeros(TOKEN, F32)

    land = lax.empty((7, SMALL_ROWS, D), F32)
    outs = pl.pallas_call(
        body, name="small_start", in_specs=[HBM, HBM, ANY],
        out_specs=(SEM, SEM, HBM, HBM, pl.BlockSpec(memory_space=pltpu.VMEM)),
        out_shape=(pltpu.SemaphoreType.DMA((7,)), pltpu.SemaphoreType.DMA((7,)), pltpu.HBM(pack.shape, F32),
                   pltpu.HBM(land.shape, F32), _sds(TOKEN, F32)),
        input_output_aliases={0: 2, 1: 3},
        compiler_params=pltpu.CompilerParams(has_side_effects=EFFECT),
    )(pltpu.with_memory_space_constraint(pack, pltpu.HBM), pltpu.with_memory_space_constraint(land, pltpu.HBM), after)
    return outs[:4]


def _small_wait(send_sems, recv_sems, pack, land, after):
    def body(pack_ref, land_ref, send_sems, recv_sems, *rest):
        x, y, c, _ = _place()
        for r in range(1, 8):
            peer = (x if not r & 4 else 1 - x, y if not r & 2 else 1 - y, c if not r & 1 else 1 - c)
            cp = _remote(pack_ref, land_ref.at[r - 1], send_sems.at[r - 1], recv_sems.at[r - 1], peer)
            cp.wait_send()
            cp.wait_recv()

    return pl.pallas_call(
        body, name="small_wait", in_specs=[HBM, HBM, SEM, SEM] + [ANY] * len(after), out_specs=(HBM, HBM),
        out_shape=(pltpu.HBM(pack.shape, F32), pltpu.HBM(land.shape, F32)),
        input_output_aliases={0: 0, 1: 1},
        compiler_params=pltpu.CompilerParams(has_side_effects=EFFECT),
    )(pack, land, send_sems, recv_sems, *after)


def _small_update(place, pack, land, ws, ms, vs):
    n = len(ws)

    def body(pref, pack_ref, land_ref, *refs):
        chip = pref[1]
        me = 2 * chip + pref[0]
        own = pack_ref[...]
        tot = None
        for dev in range(8):
            r = jnp.bitwise_xor(me, dev)
            term = jnp.where(r == 0, own, land_ref[jnp.maximum(r - 1, 0)])
            tot = term if tot is None else tot + term
        out, buf = refs[3 * n:-1], refs[-1]
        buf[...] = tot
        g_conv = jnp.zeros((3, SH_O), F32)
        for s in range(4):
            g_conv = g_conv + jnp.where(chip == s, buf[24:27, s * SH_O:(s + 1) * SH_O], 0.0)
        gs = [buf[0:2, :], buf[8:10, :], buf[16:17, :], g_conv]
        out[0][...] = buf[32:33, 0:128]
        for i in range(n):
            d, nm, nv = _adamw_math(refs[i][...], gs[i], refs[n + i][...], refs[2 * n + i][...])
            out[1 + i][...] = gs[i]
            out[1 + n + i][...] = d
            out[1 + 2 * n + i][...] = nm
            out[1 + 3 * n + i][...] = nv

    def full(shape):
        nd = len(shape)
        return pl.BlockSpec(shape, lambda i, pref: (0,) * nd)

    specs = [full(w.shape) for w in ws]
    grid_spec = pltpu.PrefetchScalarGridSpec(
        num_scalar_prefetch=1, grid=(1,),
        in_specs=[full(pack.shape), full(land.shape)] + specs * 3, out_specs=[full((1, 128))] + specs * 4,
        scratch_shapes=[pltpu.VMEM((SMALL_ROWS, D), F32)])
    outs = pl.pallas_call(
        body, name="small_update", grid_spec=grid_spec,
        out_shape=[_sds((1, 128), F32)] + [_sds(w.shape, F32) for w in ws] * 4,
        compiler_params=_params(("arbitrary",)),
    )(place, pack, land, *ws, *ms, *vs)
    return outs[0], outs[1:1 + n], outs[1 + n:1 + 2 * n], outs[1 + 2 * n:1 + 3 * n], outs[1 + 3 * n:]


def _adamw_layer(ws, gs, ms, vs, idx, name):
    n = len(idx)
    dims = [(BIG[w][1], BIG[w][2] // 4, BIG[w][3]) for w in idx]

    def body(*refs):
        for i in range(n):
            gv = refs[n + i][...]
            d, nm, nv = _adamw_math(refs[i][...], gv, refs[2 * n + i][...], refs[3 * n + i][...])
            refs[4 * n + i][...] = d
            refs[5 * n + i][...] = nm
            refs[6 * n + i][...] = nv
            refs[7 * n + i][...] = gv

    specs = [pl.BlockSpec((k, q, cdim), lambda t: (0, t, 0)) for k, q, cdim in dims]
    outs = pl.pallas_call(
        body, name=name, grid=(4,), in_specs=specs * 4, out_specs=specs * 4,
        out_shape=[_sds(BIG[w][1:], F32) for w in idx] * 4,
        compiler_params=_params(("parallel",)),
    )(*ws, *gs, *ms, *vs)
    return [tuple(outs[j * n + i] for j in range(4)) for i in range(n)]


def _adamw_small(ws, gs, ms, vs):
    n = len(ws)

    def body(*refs):
        for i in range(n):
            w_ref, g_ref, m_ref, v_ref = refs[i], refs[n + i], refs[2 * n + i], refs[3 * n + i]
            d, nm, nv = _adamw_math(w_ref[...], g_ref[...], m_ref[...], v_ref[...])
            refs[4 * n + i][...] = d
            refs[5 * n + i][...] = nm
            refs[6 * n + i][...] = nv

    specs = [_full(w.shape) for w in ws]
    outs = pl.pallas_call(
        body, name="adamw_small", grid=(1,), in_specs=specs * 4, out_specs=specs * 3,
        out_shape=[_sds(w.shape, F32) for w in ws] * 3,
        compiler_params=_params(("arbitrary",)),
    )(*ws, *gs, *ms, *vs)
    return outs[:n], outs[n:2 * n], outs[2 * n:]


def _pad_rows(a, rows):
    return jnp.pad(a, ((0, rows - a.shape[0]), (0, 0)))


def kernel(x, mem, positions, norm_g, mem_norm_g, w_mem_kv, attn_w_in, attn_w_out, conv_w_in, conv_w, conv_w_out, final_g, loss_target, m_norm_g, m_mem_norm_g, m_w_mem_kv, m_attn_w_in, m_attn_w_out, m_conv_w_in, m_conv_w, m_conv_w_out, m_final_g, v_norm_g, v_mem_norm_g, v_w_mem_kv, v_attn_w_in, v_attn_w_out, v_conv_w_in, v_conv_w, v_conv_w_out, v_final_g):
    mx, my, mc = lax.axis_index("x"), lax.axis_index("y"), lax.axis_index("c")
    place = jnp.stack([mc, 2 * mx + my]).astype(jnp.int32)

    w_big = [w_mem_kv, attn_w_in, attn_w_out, conv_w_in, conv_w_out]
    m_big = [m_w_mem_kv, m_attn_w_in, m_attn_w_out, m_conv_w_in, m_conv_w_out]
    v_big = [v_w_mem_kv, v_attn_w_in, v_attn_w_out, v_conv_w_in, v_conv_w_out]
    first, rest = (1,), (0, 2, 3, 4)
    wb1 = _cast_weights(place, [w_big[i] for i in first], place, first, "cast_w_in_a")
    a1_send, a1_recv, a1_bufs, a1_token = _gather_start(wb1, place, first, "gather_a1_start")
    wbr = _cast_weights(place, [w_big[i] for i in rest], a1_token, rest, "cast_weights")
    rest = (0, 2)
    a2_send, a2_recv, a2_bufs, a2_token = _gather_start([wbr[0], wbr[1]], a1_token, rest, "gather_a2_start")
    gb_send, gb_recv, gb_bufs, gb_token = _gather_start([wbr[2], wbr[3]], a2_token, LAYER_B, "gather_b_start")

    xs, tgt = x[0], loss_target[0]
    g0, g1 = norm_g[0:1], norm_g[1:2]
    rc, rs1, rs2 = _rope_tables(positions[0].astype(F32).reshape(S, 1), gb_token)
    a1_bufs = _gather_wait(a1_send, a1_recv, a1_bufs, [rc], first, "gather_a1_wait")
    w_in_a = _gather_forward(a1_bufs, first, "gather_a1_forward")[0].reshape(4, D, SH_A)
    hn0, q, k, v, qm0, z0 = _in_proj_a(xs, g0, w_in_a, rc, rs1, rs2, gb_token)
    a2_bufs = _gather_wait(a2_send, a2_recv, a2_bufs, [q], rest, "gather_a2_wait")
    wkv_f, w_out_a = _gather_forward(a2_bufs, rest, "gather_a2_forward")
    w_out_a = w_out_a.reshape(4, BR_A, SH_O)
    memn, kv = _mem_fwd(mem[0], mem_norm_g, wkv_f)
    fwd = [_attn_fwd(q, k, v, g) for g in range(3)]
    os_, ls, lss = [f[0] for f in fwd], [f[1] for f in fwd], [f[2] for f in fwd]
    h1 = _attn_out(os_, ls, qm0, kv[0], z0, xs, w_out_a)

    gb_bufs = _gather_wait(gb_send, gb_recv, gb_bufs, [h1], LAYER_B, "gather_b_wait")
    w_in_b, w_out_b, cw_f = _gather_forward(gb_bufs, LAYER_B, "gather_b_forward", _pad_rows(conv_w[0], CW_ROWS))
    w_in_b = w_in_b.reshape(4, D, SH_B)
    w_out_b = w_out_b.reshape(BR_B, D)
    cw8 = cw_f.transpose(1, 0, 2).reshape(CW_ROWS, D)
    hn1, bg, cg, u, qm1, z1 = _in_proj_b(h1, g1, w_in_b)
    dh2, loss_part, dfg = _conv_out_loss(bg, cg, u, cw8, qm1, kv[1], z1, h1, w_out_b, final_g.reshape(1, D), tgt)

    dproj_b, dw_out_b, dcw, dkv1 = _conv_bwd(dh2, bg, cg, u, cw8, qm1, kv[1], z1, w_out_b)
    dw_in_b = _w_in_grad(hn1, dproj_b, IN_B, "w_in_b_grad")
    gs_b = [dw_in_b.reshape(4, 1, D, SH_B), dw_out_b.reshape(4, 1, BR_B // 4, D)]
    pb_send, pb_recv, gs_b, pb_land, pb_token = _pair_start(gs_b, LAYER_B, "pair_b_start")
    dh1, dg1 = _in_proj_bwd(dproj_b, w_in_b, h1, g1, dh2, pb_token, IN_B, "in_proj_b_bwd")
    gs_b, r1_b = _pair_wait(pb_send, pb_recv, gs_b, pb_land, [dh1], LAYER_B, "pair_b_wait")
    ps_b = _pair_sums(place, gs_b, r1_b, LAYER_B, "pair_sums_b")
    cb_send, cb_recv, cb_src, cb_land, cb_token = _chip_start(ps_b, LAYER_B, "chip_b_start")

    outs = _attn_out_bwd(dh1, os_, ls, qm0, kv[0], z0, w_out_a, cb_token)
    dos, dds, dqm, dz, dw_out_a, dkv0 = outs[0:3], outs[3:6], outs[6], outs[7], outs[8], outs[9]
    bwd = [_attn_bwd(q, k, v, dos[g], lss[g], dds[g], g) for g in range(3)]
    dproj_a = _qkv_bwd([b[0] for b in bwd], [b[1] for b in bwd], [b[2] for b in bwd], dqm, dz, rc, rs1, rs2)
    dw_in_a = _w_in_grad(hn0, dproj_a, IN_A, "w_in_a_grad")
    dwkv, dmg = _mem_bwd(mem[0], mem_norm_g, memn, wkv_f, dkv0, dkv1)

    gs_a = [dwkv, dw_in_a.reshape(4, 1, D, SH_A), dw_out_a.reshape(4, 1, BR_A, SH_O)]
    r1_a = _pair_exchange(gs_a, LAYER_A, "pair_exchange_a")
    ps_a = _pair_sums(place, gs_a, r1_a, LAYER_A, "pair_sums_a")
    ca_send, ca_recv, ca_src, ca_land, ca_token = _chip_start(ps_a, LAYER_A, "chip_a_start")

    gx, dg0 = _in_proj_bwd(dproj_a, w_in_a, xs, g0, dh1, ca_token, IN_A, "in_proj_a_bwd")
    pack = jnp.concatenate([_pad_rows(jnp.concatenate([dg0, dg1], axis=0), 8), _pad_rows(dmg, 8), _pad_rows(dfg, 8),
                            dcw, _pad_rows(jnp.pad(loss_part, ((0, 0), (0, D - 128))), 8)], axis=0)
    sm_send, sm_recv, pack, sm_land = _small_start(pack, ca_token)
    r2_b = _chip_wait(cb_send, cb_recv, cb_src, cb_land, [ca_token], LAYER_B, "chip_b_wait")
    hs_b = _chip_sums(place, gs_b, r1_b, r2_b, LAYER_B, "chip_sums_b")
    g_b = _pair_gather(hs_b, LAYER_B, "pair_gather_b")
    upd_b = _adamw_layer([w_big[w] for w in LAYER_B], g_b, [m_big[w] for w in LAYER_B], [v_big[w] for w in LAYER_B],
                         LAYER_B, "adamw_b")
    r2_a = _chip_wait(ca_send, ca_recv, ca_src, ca_land, [gx, upd_b[0][0], upd_b[1][0]], LAYER_A, "chip_a_wait")
    hs_a = _chip_sums(place, gs_a, r1_a, r2_a, LAYER_A, "chip_sums_a")
    g_a = _pair_gather(hs_a, LAYER_A, "pair_gather_a")
    upd_a = _adamw_layer([w_big[w] for w in LAYER_A], g_a, [m_big[w] for w in LAYER_A], [v_big[w] for w in LAYER_A],
                         LAYER_A, "adamw_a")
    upd = upd_a + upd_b
    g_big = [u[3] for u in upd]
    pack, sm_land = _small_wait(sm_send, sm_recv, pack, sm_land, [r2_a[0]])
    sw = [norm_g, mem_norm_g, final_g.reshape(1, D), conv_w[0]]
    sm = [m_norm_g, m_mem_norm_g, m_final_g.reshape(1, D), m_conv_w[0]]
    sv = [v_norm_g, v_mem_norm_g, v_final_g.reshape(1, D), v_conv_w[0]]
    loss_row, sg, sd, snm, snv = _small_update(place, pack, sm_land, sw, sm, sv)
    loss = loss_row[0, 0]
    g_norm, g_memnorm, g_final, g_conv = sg

    def order(norm, memnorm, wkv, w_in_a, w_out_a, w_in_b, conv, w_out_b, final):
        return (norm, memnorm, wkv, w_in_a, w_out_a, w_in_b, conv.reshape(1, 3, SH_O), w_out_b, final.reshape(D))

    grads = order(g_norm, g_memnorm, g_big[0], g_big[1], g_big[2], g_big[3], g_conv, g_big[4], g_final)
    deltas = order(sd[0], sd[1], upd[0][0], upd[1][0], upd[2][0], upd[3][0], sd[3], upd[4][0], sd[2])
    new_m = order(snm[0], snm[1], upd[0][1], upd[1][1], upd[2][1], upd[3][1], snm[3], upd[4][1], snm[2])
    new_v = order(snv[0], snv[1], upd[0][2], upd[1][2], upd[2][2], upd[3][2], snv[3], upd[4][2], snv[2])
    return (loss, gx[None], *grads, *deltas, *new_m, *new_v)
```

```python
import functools

import numpy as np
import jax
import jax.numpy as jnp
from jax import lax
from jax.experimental import pallas as pl
from jax.experimental.pallas import tpu as pltpu

F32 = jnp.float32
BF16 = jnp.bfloat16

S = 2048
D = 1024
TM = 256
NT = S // TM
HD = 64
GW = 512
NQ = 3 * GW
MW = 256
NM = 256
IN_A = 3 * NQ + MW + GW + MW
IN_B = 3 * D + MW + D + MW
BR_A = GW + MW
BR_B = D + MW
SH_A = IN_A // 4
SH_B = IN_B // 4
SH_O = D // 4
QBLK = 128
DILATIONS = (1, 4, 16)
EPS = 1e-6
SCALE = HD ** -0.5
NEG = -1e30
ROPE_THETA = 500000.0

ADAM_LR = 0.001
ADAM_B1 = 0.9
ADAM_B2 = 0.999
ADAM_EPS = 1e-08
ADAM_WD = 0.01
ADAM_STEP = 10

VMEM_LIMIT_BYTES = 60 * 1024 * 1024


def _params(sem=None):
    if sem is None:
        return pltpu.CompilerParams(vmem_limit_bytes=VMEM_LIMIT_BYTES)
    return pltpu.CompilerParams(dimension_semantics=sem, vmem_limit_bytes=VMEM_LIMIT_BYTES)


def _full(shape):
    nd = len(shape)
    return pl.BlockSpec(shape, lambda *_: (0,) * nd)


def _rows(width, tm=TM):
    return pl.BlockSpec((tm, width), lambda i: (i, 0))


def _sds(shape, dtype):
    return jax.ShapeDtypeStruct(shape, dtype)


def _silu_parts(z):
    sig = 1.0 / (1.0 + jnp.exp(-z))
    return z * sig, sig * (1.0 + z * (1.0 - sig))


def _dot(a, b):
    return jnp.dot(a, b, preferred_element_type=F32)


def _dot_nt(a, b):
    return lax.dot_general(a, b, (((1,), (1,)), ((), ())), preferred_element_type=F32)


def _dot_tn(a, b):
    return lax.dot_general(a, b, (((0,), (0,)), ((), ())), preferred_element_type=F32)


def _rope_fwd(t, c, s1, s2):
    return t * c + pltpu.roll(t, 120, 1) * s1 + pltpu.roll(t, 8, 1) * s2


def _rope_bwd(g, c, s1, s2):
    return g * c + pltpu.roll(g * s1, 8, 1) + pltpu.roll(g * s2, 120, 1)


def _mem_attn(qm, kv):
    res = []
    for h in range(MW // HD):
        sl = slice(h * HD, (h + 1) * HD)
        s = _dot_nt(qm[:, sl], kv[:, sl]) * SCALE
        e = jnp.exp(s - jnp.max(s, axis=-1, keepdims=True))
        p = e / jnp.sum(e, axis=-1, keepdims=True)
        res.append((p, _dot(p.astype(BF16), kv[:, MW + h * HD:MW + (h + 1) * HD])))
    return res


def _mem_attn_bwd(dmo, heads, qm, kv, dqm_store, dkv_ref):
    for h, (p, mo) in enumerate(heads):
        sl = slice(h * HD, (h + 1) * HD)
        vs = slice(MW + h * HD, MW + (h + 1) * HD)
        dmo_h = dmo[:, sl]
        dmo_b = dmo_h.astype(BF16)
        dp = _dot_nt(dmo_b, kv[:, vs])
        delta = jnp.sum(dmo_h * mo, axis=-1, keepdims=True)
        ds = (p * (dp - delta) * SCALE).astype(BF16)
        dqm_store(h, _dot(ds, kv[:, sl]))
        dkv_ref[:, sl] += _dot_tn(ds, qm[:, sl])
        dkv_ref[:, vs] += _dot_tn(p.astype(BF16), dmo_b)


def _merge(o_refs, l_refs):
    ls = [r[...] for r in l_refs]
    m = jnp.maximum(jnp.maximum(ls[0], ls[1]), ls[2])
    es = [jnp.exp(l - m) for l in ls]
    inv = 1.0 / (es[0] + es[1] + es[2])
    ws = [e * inv for e in es]
    os_ = [r[...] for r in o_refs]
    mix = ws[0] * os_[0] + ws[1] * os_[1] + ws[2] * os_[2]
    return ws, mix


def _conv_taps(cg, u, cgp, up, first):
    a = cg * u
    ap = jnp.where(first, 0.0, cgp * up)
    row = lax.broadcasted_iota(jnp.int32, a.shape, 0)
    a1 = jnp.where(row == 0, ap[7:8, :], pltpu.roll(a, 1, 0))
    a2 = jnp.where(row == 0, ap[6:7, :], jnp.where(row == 1, ap[7:8, :], pltpu.roll(a, 2, 0)))
    return a, a1, a2


def _rope_tables(posf, after):
    half = 8
    invf = np.float32(ROPE_THETA) ** (-np.arange(half, dtype=np.float32) * np.float32(2.0 / 16))
    lane = np.arange(128)
    table = np.where((lane % HD) < 16, invf[lane % half], 0.0).astype(np.float32)[None, :]

    def body(pos_ref, invf_ref, c_ref, s1_ref, s2_ref):
        ang = pos_ref[...] * invf_ref[...]
        jm = lax.broadcasted_iota(jnp.int32, ang.shape, 1) & (HD - 1)
        cs = jnp.cos(ang)
        sn = jnp.sin(ang)
        c_ref[...] = jnp.where(jm < 16, cs, 1.0)
        s1_ref[...] = jnp.where(jm < 8, -sn, 0.0)
        s2_ref[...] = jnp.where((jm >= 8) & (jm < 16), sn, 0.0)

    out = _sds((S, 128), F32)
    return pl.pallas_call(
        functools.partial(_skip_arg, body, 2), name="rope_tables", grid=(NT,),
        in_specs=[_rows(1), _full((1, 128)), pl.BlockSpec(memory_space=pl.ANY)],
        out_specs=[_rows(128)] * 3, out_shape=[out] * 3,
        compiler_params=_params(("parallel",)),
    )(posf, jnp.asarray(table), after)


def _in_proj_a(x, g0, w_in, c, s1, s2, after):
    def body(x_ref, g_ref, w_ref, c_ref, s1_ref, s2_ref, hn_ref, q_ref, k_ref, v_ref, qm_ref, z_ref, proj):
        xf = x_ref[...]
        hn = xf * lax.rsqrt(jnp.mean(xf * xf, axis=-1, keepdims=True) + EPS) * g_ref[...]
        hb = hn.astype(BF16)
        hn_ref[...] = hb
        for s in range(4):
            proj[:, s * SH_A:(s + 1) * SH_A] = _dot(hb, w_ref[s])
        cc, a1, a2 = c_ref[...], s1_ref[...], s2_ref[...]
        for j in range(NQ // 128):
            q_ref[:, j * 128:(j + 1) * 128] = (
                _rope_fwd(proj[:, j * 128:(j + 1) * 128], cc, a1, a2) * SCALE).astype(BF16)
            k_ref[:, j * 128:(j + 1) * 128] = _rope_fwd(
                proj[:, NQ + j * 128:NQ + (j + 1) * 128], cc, a1, a2).astype(BF16)
        v_ref[...] = proj[:, 2 * NQ:3 * NQ].astype(BF16)
        qm_ref[...] = proj[:, 3 * NQ:3 * NQ + MW].astype(BF16)
        z_ref[...] = proj[:, 3 * NQ + MW:]

    return pl.pallas_call(
        functools.partial(_skip_arg, body, 6), name="in_proj_a", grid=(NT,),
        in_specs=[_rows(D), _full((1, D)), _full((4, D, SH_A)), _rows(128), _rows(128), _rows(128),
                  pl.BlockSpec(memory_space=pl.ANY)],
        out_specs=[_rows(D), _rows(NQ), _rows(NQ), _rows(NQ), _rows(MW), _rows(BR_A)],
        out_shape=[_sds((S, D), BF16), _sds((S, NQ), BF16), _sds((S, NQ), BF16), _sds((S, NQ), BF16),
                   _sds((S, MW), BF16), _sds((S, BR_A), F32)],
        scratch_shapes=[pltpu.VMEM((TM, IN_A), F32)],
        compiler_params=_params(("parallel",)),
    )(x, g0, w_in, c, s1, s2, after)


def _mem_fwd(mem, mg, wkv):
    def body(mem_ref, mg_ref, w_ref, memn_ref, kv_ref):
        mf = mem_ref[...]
        n = mf * lax.rsqrt(jnp.mean(mf * mf, axis=-1, keepdims=True) + EPS)
        for i in range(2):
            mn = (n * mg_ref[i:i + 1, :]).astype(BF16)
            memn_ref[i] = mn
            acc = _dot(mn[:, 0:NM], w_ref[0, i])
            for s in range(1, 4):
                acc += _dot(mn[:, s * NM:(s + 1) * NM], w_ref[s, i])
            kv_ref[i] = acc.astype(BF16)

    return pl.pallas_call(
        body, name="mem_fwd", grid=(1,),
        in_specs=[_full((NM, D)), _full((2, D)), _full((4, 2, NM, 2 * MW))],
        out_specs=[_full((2, NM, D)), _full((2, NM, 2 * MW))],
        out_shape=[_sds((2, NM, D), BF16), _sds((2, NM, 2 * MW), BF16)],
        compiler_params=_params(("arbitrary",)),
    )(mem, mg, wkv)


def _band_mask(j):
    qi = lax.broadcasted_iota(jnp.int32, (QBLK, 2 * QBLK), 0)
    kj = lax.broadcasted_iota(jnp.int32, (QBLK, 2 * QBLK), 1)
    dist = qi + QBLK - kj
    return (dist >= 0) & (dist <= QBLK) & ((kj >= QBLK) | (j > 0))


LANES = 128
NCHUNK = GW // LANES
FWD_UNROLL = 16
BWD_UNROLL = 4


def _perm_matrix(d):
    n = TM // d
    p = np.zeros((TM, TM), np.float32)
    for r in range(d):
        for i in range(n):
            p[r * n + i, i * d + r] = 1.0
    return p


def _split_dot(p, x, parts):
    hi = x.astype(BF16)
    rem = x - hi.astype(F32)
    lo = rem.astype(BF16)
    both = _dot(p, jnp.concatenate([hi, lo], axis=1))
    acc = both[:, :LANES] + both[:, LANES:]
    if parts == 3:
        acc = acc + _dot(p, (rem - lo.astype(F32)).astype(BF16))
    return acc


def _pair_dot(p, a, b):
    both = _dot(p, jnp.concatenate([a, b], axis=1))
    return both[:, :LANES], both[:, LANES:]


def _tile_to_streams(y, dst, t, d):
    n, ln = TM // d, S // d
    for r in range(d):
        dst[r * ln + t * n:r * ln + (t + 1) * n, :] = y[r * n:(r + 1) * n].astype(dst.dtype)


def _tile_from_streams(src, t, d):
    n, ln = TM // d, S // d
    return jnp.concatenate([src[r * ln + t * n:r * ln + (t + 1) * n, :] for r in range(d)], axis=0)


def _head_masks():
    first = lax.broadcasted_iota(jnp.int32, (TM, LANES), 1) < HD
    return first, jnp.logical_not(first)


def _attn_fwd(q, k, v, g, after):
    d = DILATIONS[g]
    nb = S // d // QBLK
    perm = _perm_matrix(d)

    def body(q_ref, k_ref, v_ref, p_ref, pt_ref, o_ref, l_ref, ls_ref, q0, q1, ks, vs, os_):
        first, second = _head_masks()
        pm = p_ref[...]
        for t in range(NT):
            rows = slice(t * TM, (t + 1) * TM)
            if d == 1:
                qt = q_ref[rows, :].astype(F32)
            else:
                qt, kt = _pair_dot(pm, q_ref[rows, :], k_ref[rows, :])
                _tile_to_streams(kt, ks, t, d)
                _tile_to_streams(_dot(pm, v_ref[rows, :]), vs, t, d)
            _tile_to_streams(jnp.where(first, qt, 0.0), q0, t, d)
            _tile_to_streams(jnp.where(second, qt, 0.0), q1, t, d)
        kref, vref = (k_ref, v_ref) if d == 1 else (ks, vs)
        oref, lref = (o_ref, l_ref) if d == 1 else (os_, ls_ref)

        def blk(b, carry):
            r0 = pl.multiple_of(b * QBLK, QBLK)
            p0 = pl.multiple_of(jnp.maximum(b - 1, 0) * QBLK, QBLK)
            kk = jnp.concatenate([kref[pl.ds(p0, QBLK), :], kref[pl.ds(r0, QBLK), :]], axis=0)
            vv = jnp.concatenate([vref[pl.ds(p0, QBLK), :], vref[pl.ds(r0, QBLK), :]], axis=0)
            valid = _band_mask(b & (nb - 1))
            acc, den, lse = [], [], []
            for qh in (q0, q1):
                s = jnp.where(valid, _dot_nt(qh[pl.ds(r0, QBLK), :], kk), NEG)
                m = jnp.max(s, axis=-1, keepdims=True)
                e = jnp.exp(s - m)
                l = jnp.sum(e, axis=-1, keepdims=True)
                acc.append(_dot(e.astype(BF16), vv))
                den.append(l)
                lse.append(m + jnp.log(l))
            f = first[:QBLK]
            oref[pl.ds(r0, QBLK), :] = jnp.where(f, acc[0], acc[1]) / jnp.where(f, den[0], den[1])
            lref[pl.ds(r0, QBLK), :] = jnp.where(f, lse[0], lse[1])
            return carry

        lax.fori_loop(0, S // QBLK, blk, 0, unroll=FWD_UNROLL)
        if d > 1:
            ptm = pt_ref[...]
            for t in range(NT):
                rows = slice(t * TM, (t + 1) * TM)
                o_ref[rows, :] = _split_dot(ptm, _tile_from_streams(os_, t, d), 2)
                l_ref[rows, :] = _split_dot(ptm, _tile_from_streams(ls_ref, t, d), 3)

    qkv_spec = pl.BlockSpec((S, LANES), lambda c: (0, g * NCHUNK + c))
    out_spec = pl.BlockSpec((S, LANES), lambda c: (0, c))
    n_out = 2 if d == 1 else 3
    inner = body if d > 1 else functools.partial(_drop_arg, body, 7)
    outs = pl.pallas_call(
        functools.partial(_skip_arg, inner, 5), name=f"attn_fwd_g{g}", grid=(NCHUNK,),
        in_specs=[qkv_spec] * 3 + [_full((TM, TM))] * 2 + [pl.BlockSpec(memory_space=pl.ANY)],
        out_specs=[out_spec] * n_out, out_shape=[_sds((S, GW), F32)] * n_out,
        scratch_shapes=[pltpu.VMEM((S, LANES), BF16)] * 4 + [pltpu.VMEM((S, LANES), F32)],
        compiler_params=_params(("parallel",)),
    )(q, k, v, jnp.asarray(perm, BF16), jnp.asarray(perm.T, BF16), after)
    return (outs[0], outs[1], outs[1]) if d == 1 else tuple(outs)


def _drop_arg(body, pos, *refs):
    return body(*refs[:pos], None, *refs[pos:])


def _attn_out(os_, ls, qm, kv0, z, x, w_out):
    def body(o0, o1, o2, l0, l1, l2, qm_ref, kv_ref, z_ref, x_ref, w_ref, h_ref, ybuf):
        _, mix = _merge((o0, o1, o2), (l0, l1, l2))
        sz, _ = _silu_parts(z_ref[...])
        ybuf[:, :GW] = (mix * sz[:, :GW]).astype(BF16)
        for h, (_, mo) in enumerate(_mem_attn(qm_ref[...], kv_ref[...])):
            sl = slice(GW + h * HD, GW + (h + 1) * HD)
            ybuf[:, sl] = (mo * sz[:, sl]).astype(BF16)
        yb = ybuf[...]
        for s in range(4):
            cs = slice(s * SH_O, (s + 1) * SH_O)
            h_ref[:, cs] = x_ref[:, cs] + _dot(yb, w_ref[s])

    return pl.pallas_call(
        body, name="attn_out", grid=(NT,),
        in_specs=[_rows(GW)] * 6 + [_rows(MW), _full((NM, 2 * MW)), _rows(BR_A), _rows(D), _full((4, BR_A, SH_O))],
        out_specs=_rows(D), out_shape=_sds((S, D), F32),
        scratch_shapes=[pltpu.VMEM((TM, BR_A), BF16)],
        compiler_params=_params(("parallel",)),
    )(*os_, *ls, qm, kv0, z, x, w_out)


def _in_proj_b(h1, g1, w_in):
    def body(x_ref, g_ref, w_ref, hn_ref, bg_ref, cg_ref, u_ref, qm_ref, z_ref, proj):
        xf = x_ref[...]
        hn = xf * lax.rsqrt(jnp.mean(xf * xf, axis=-1, keepdims=True) + EPS) * g_ref[...]
        hb = hn.astype(BF16)
        hn_ref[...] = hb
        for s in range(4):
            proj[:, s * SH_B:(s + 1) * SH_B] = _dot(hb, w_ref[s])
        bg_ref[...] = proj[:, :D]
        cg_ref[...] = proj[:, D:2 * D]
        u_ref[...] = proj[:, 2 * D:3 * D]
        qm_ref[...] = proj[:, 3 * D:3 * D + MW].astype(BF16)
        z_ref[...] = proj[:, 3 * D + MW:]

    return pl.pallas_call(
        body, name="in_proj_b", grid=(NT,),
        in_specs=[_rows(D), _full((1, D)), _full((4, D, SH_B))],
        out_specs=[_rows(D), _rows(D), _rows(D), _rows(D), _rows(MW), _rows(BR_B)],
        out_shape=[_sds((S, D), BF16), _sds((S, D), F32), _sds((S, D), F32), _sds((S, D), F32),
                   _sds((S, MW), BF16), _sds((S, BR_B), F32)],
        scratch_shapes=[pltpu.VMEM((TM, IN_B), F32)],
        compiler_params=_params(("parallel",)),
    )(h1, g1, w_in)


def _prev8(width):
    return pl.BlockSpec((8, width), lambda i: (jnp.maximum(i * (TM // 8) - 1, 0), 0))


def _conv_out_loss(bg, cg, u, cw, qm, kv1, z, h1, w_out, fg, tgt):
    def body(bg_ref, cg_ref, u_ref, cgp_ref, up_ref, cw_ref, qm_ref, kv_ref, z_ref, h_ref, w_ref, fg_ref, t_ref,
             dh_ref, loss_ref, dfg_ref, ybuf):
        i = pl.program_id(0)
        a, a1, a2 = _conv_taps(cg_ref[...], u_ref[...], cgp_ref[...], up_ref[...], i == 0)
        conv = cw_ref[0:1, :] * a2 + cw_ref[1:2, :] * a1 + cw_ref[2:3, :] * a
        sz, _ = _silu_parts(z_ref[...])
        ybuf[:, :D] = (bg_ref[...] * conv * sz[:, :D]).astype(BF16)
        for h, (_, mo) in enumerate(_mem_attn(qm_ref[...], kv_ref[...])):
            sl = slice(D + h * HD, D + (h + 1) * HD)
            ybuf[:, sl] = (mo * sz[:, sl]).astype(BF16)
        h2 = h_ref[...] + _dot(ybuf[...], w_ref[...])
        rstd = lax.rsqrt(jnp.mean(h2 * h2, axis=-1, keepdims=True) + EPS)
        n = h2 * rstd
        fgv = fg_ref[...]
        err = n * fgv - t_ref[...]
        dout = err * (1.0 / D)
        dn = dout * fgv
        dh_ref[...] = rstd * (dn - n * jnp.mean(dn * n, axis=-1, keepdims=True))

        @pl.when(i == 0)
        def _():
            loss_ref[...] = jnp.zeros_like(loss_ref)
            dfg_ref[...] = jnp.zeros_like(dfg_ref)

        loss_ref[...] += jnp.sum(err * err) * (0.5 / D)
        dfg_ref[...] += jnp.sum(dout * n, axis=0, keepdims=True)

    return pl.pallas_call(
        body, name="conv_out_loss", grid=(NT,),
        in_specs=[_rows(D), _rows(D), _rows(D), _prev8(D), _prev8(D), _full((8, D)), _rows(MW),
                  _full((NM, 2 * MW)), _rows(BR_B), _rows(D), _full((BR_B, D)), _full((1, D)), _rows(D)],
        out_specs=[_rows(D), _full((1, 128)), _full((1, D))],
        out_shape=[_sds((S, D), F32), _sds((1, 128), F32), _sds((1, D), F32)],
        scratch_shapes=[pltpu.VMEM((TM, BR_B), BF16)],
        compiler_params=_params(("arbitrary",)),
    )(bg, cg, u, cg, u, cw, qm, kv1, z, h1, w_out, fg, tgt)


def _conv_bwd(dh2, bg, cg, u, cw, qm, kv1, z, w_out):
    rev = lambda i: (NT - 1 - i, 0)
    rows = lambda w: pl.BlockSpec((TM, w), rev)
    prev8 = pl.BlockSpec((8, D), lambda i: (jnp.maximum((NT - 1 - i) * (TM // 8) - 1, 0), 0))

    def body(dh_ref, bg_ref, cg_ref, u_ref, cgp_ref, up_ref, cw_ref, qm_ref, kv_ref, z_ref, w_ref,
             dproj_ref, dw_ref, dcw_ref, dkv_ref, ybuf, carry):
        i = pl.program_id(0)

        @pl.when(i == 0)
        def _():
            dw_ref[...] = jnp.zeros_like(dw_ref)
            dcw_ref[...] = jnp.zeros_like(dcw_ref)
            dkv_ref[...] = jnp.zeros_like(dkv_ref)
            carry[...] = jnp.zeros_like(carry)

        bgv, cgv, uv = bg_ref[...], cg_ref[...], u_ref[...]
        a, a1, a2 = _conv_taps(cgv, uv, cgp_ref[...], up_ref[...], i == NT - 1)
        w0, w1, w2 = cw_ref[0:1, :], cw_ref[1:2, :], cw_ref[2:3, :]
        conv = w0 * a2 + w1 * a1 + w2 * a
        mix = bgv * conv
        zv = z_ref[...]
        sz, dsz = _silu_parts(zv)
        qmv, kvv = qm_ref[...], kv_ref[...]
        heads = _mem_attn(qmv, kvv)
        ybuf[:, :D] = (mix * sz[:, :D]).astype(BF16)
        for h, (_, mo) in enumerate(heads):
            sl = slice(D + h * HD, D + (h + 1) * HD)
            ybuf[:, sl] = (mo * sz[:, sl]).astype(BF16)
        dhb = dh_ref[...].astype(BF16)
        dw_ref[...] += _dot_tn(ybuf[...], dhb)
        dy = _dot_nt(dhb, w_ref[...])
        dcat = dy * sz
        dproj_ref[:, 3 * D + MW:3 * D + MW + D] = (dy[:, :D] * mix * dsz[:, :D]).astype(BF16)
        for h, (_, mo) in enumerate(heads):
            sl = slice(D + h * HD, D + (h + 1) * HD)
            dproj_ref[:, 3 * D + MW + D + h * HD:3 * D + MW + D + (h + 1) * HD] = (
                dy[:, sl] * mo * dsz[:, sl]).astype(BF16)
        dmix = dcat[:, :D]
        dproj_ref[:, :D] = (dmix * conv).astype(BF16)
        dc = dmix * bgv
        nxt = carry[...]
        row = lax.broadcasted_iota(jnp.int32, dc.shape, 0)
        dc1 = jnp.where(row == TM - 1, nxt[0:1, :], pltpu.roll(dc, TM - 1, 0))
        dc2 = jnp.where(row == TM - 2, nxt[0:1, :], jnp.where(row == TM - 1, nxt[1:2, :], pltpu.roll(dc, TM - 2, 0)))
        carry[...] = dc[0:8, :]
        da = w2 * dc + w1 * dc1 + w0 * dc2
        dproj_ref[:, D:2 * D] = (da * uv).astype(BF16)
        dproj_ref[:, 2 * D:3 * D] = (da * cgv).astype(BF16)
        dcw_ref[0:1, :] += jnp.sum(dc * a2, axis=0, keepdims=True)
        dcw_ref[1:2, :] += jnp.sum(dc * a1, axis=0, keepdims=True)
        dcw_ref[2:3, :] += jnp.sum(dc * a, axis=0, keepdims=True)

        def dqm_store(h, val):
            dproj_ref[:, 3 * D + h * HD:3 * D + (h + 1) * HD] = val.astype(BF16)

        _mem_attn_bwd(dcat[:, D:], heads, qmv, kvv, dqm_store, dkv_ref)

    return pl.pallas_call(
        body, name="conv_bwd", grid=(NT,),
        in_specs=[rows(D), rows(D), rows(D), rows(D), prev8, prev8, _full((8, D)), rows(MW),
                  _full((NM, 2 * MW)), rows(BR_B), _full((BR_B, D))],
        out_specs=[rows(IN_B), _full((BR_B, D)), _full((8, D)), _full((NM, 2 * MW))],
        out_shape=[_sds((S, IN_B), BF16), _sds((BR_B, D), F32), _sds((8, D), F32), _sds((NM, 2 * MW), F32)],
        scratch_shapes=[pltpu.VMEM((TM, BR_B), BF16), pltpu.VMEM((8, D), F32)],
        compiler_params=_params(("arbitrary",)),
    )(dh2, bg, cg, u, cg, u, cw, qm, kv1, z, w_out)


def _in_proj_bwd(dproj, w_in, xin, g, dres, after, width, name):
    sh = width // 4

    def body(dp_ref, w_ref, x_ref, g_ref, dr_ref, dx_ref, dg_ref):
        i = pl.program_id(0)
        dhn = _dot_nt(dp_ref[:, 0:sh], w_ref[0])
        for s in range(1, 4):
            dhn += _dot_nt(dp_ref[:, s * sh:(s + 1) * sh], w_ref[s])
        xf = x_ref[...]
        rstd = lax.rsqrt(jnp.mean(xf * xf, axis=-1, keepdims=True) + EPS)
        n = xf * rstd
        dn = dhn * g_ref[...]
        dx_ref[...] = dr_ref[...] + rstd * (dn - n * jnp.mean(dn * n, axis=-1, keepdims=True))

        @pl.when(i == 0)
        def _():
            dg_ref[...] = jnp.zeros_like(dg_ref)

        dg_ref[...] += jnp.sum(dhn * n, axis=0, keepdims=True)

    return pl.pallas_call(
        functools.partial(_skip_arg, body, 5), name=name, grid=(NT,),
        in_specs=[_rows(width), _full((4, D, sh)), _rows(D), _full((1, D)), _rows(D), pl.BlockSpec(memory_space=pl.ANY)],
        out_specs=[_rows(D), _full((1, D))],
        out_shape=[_sds((S, D), F32), _sds((1, D), F32)],
        compiler_params=_params(("arbitrary",)),
    )(dproj, w_in, xin, g, dres, after)


def _w_in_grad(hn, dproj, width, name):
    sh = width // 4

    def body(hn_ref, dp_ref, dw_ref):
        dw_ref[0] = _dot_tn(hn_ref[...], dp_ref[...])

    return pl.pallas_call(
        body, name=name, grid=(4,),
        in_specs=[_full((S, D)), pl.BlockSpec((S, sh), lambda s: (0, s))],
        out_specs=pl.BlockSpec((1, D, sh), lambda s: (s, 0, 0)),
        out_shape=_sds((4, D, sh), F32),
        compiler_params=_params(("parallel",)),
    )(hn, dproj)


def _attn_out_bwd(dh1, os_, ls, qm, kv0, z, w_out, after):
    ones_bd = np.kron(np.eye(GW // HD, dtype=np.float32), np.ones((HD, HD), np.float32))

    def body(dh_ref, o0, o1, o2, l0, l1, l2, qm_ref, kv_ref, z_ref, w_ref, bd_ref,
             do0, do1, do2, dd0, dd1, dd2, dqm_ref, dz_ref, dw_ref, dkv_ref, ybuf):
        i = pl.program_id(0)

        @pl.when(i == 0)
        def _():
            dw_ref[...] = jnp.zeros_like(dw_ref)
            dkv_ref[...] = jnp.zeros_like(dkv_ref)

        ws, mix = _merge((o0, o1, o2), (l0, l1, l2))
        sz, dsz = _silu_parts(z_ref[...])
        qmv, kvv = qm_ref[...], kv_ref[...]
        heads = _mem_attn(qmv, kvv)
        ybuf[:, :GW] = (mix * sz[:, :GW]).astype(BF16)
        for h, (_, mo) in enumerate(heads):
            sl = slice(GW + h * HD, GW + (h + 1) * HD)
            ybuf[:, sl] = (mo * sz[:, sl]).astype(BF16)
        yb = ybuf[...]
        dh = dh_ref[...]
        dy = None
        for s in range(4):
            dhb = dh[:, s * SH_O:(s + 1) * SH_O].astype(BF16)
            dw_ref[s] += _dot_tn(yb, dhb)
            part = _dot_nt(dhb, w_ref[s])
            dy = part if dy is None else dy + part
        dcat = dy * sz
        dz_ref[:, :GW] = (dy[:, :GW] * mix * dsz[:, :GW]).astype(BF16)
        for h, (_, mo) in enumerate(heads):
            sl = slice(GW + h * HD, GW + (h + 1) * HD)
            dz_ref[:, sl] = (dy[:, sl] * mo * dsz[:, sl]).astype(BF16)
        dmix = dcat[:, :GW]
        prod = dmix * mix
        hi = prod.astype(BF16)
        lo = (prod - hi.astype(F32)).astype(BF16)
        bd = bd_ref[...]
        tot = _dot(hi, bd) + _dot(lo, bd)
        for w, do_ref, dd_ref in zip(ws, (do0, do1, do2), (dd0, dd1, dd2)):
            do_ref[...] = (w * dmix).astype(BF16)
            dd_ref[...] = w * tot

        def dqm_store(h, val):
            dqm_ref[:, h * HD:(h + 1) * HD] = val.astype(BF16)

        _mem_attn_bwd(dcat[:, GW:], heads, qmv, kvv, dqm_store, dkv_ref)

    return pl.pallas_call(
        functools.partial(_skip_arg, body, 12), name="attn_out_bwd", grid=(NT,),
        in_specs=[_rows(D)] + [_rows(GW)] * 6 + [_rows(MW), _full((NM, 2 * MW)), _rows(BR_A),
                                                   _full((4, BR_A, SH_O)), _full((GW, GW)),
                                                   pl.BlockSpec(memory_space=pl.ANY)],
        out_specs=[_rows(GW)] * 6 + [_rows(MW), _rows(BR_A), _full((4, BR_A, SH_O)), _full((NM, 2 * MW))],
        out_shape=[_sds((S, GW), BF16)] * 3 + [_sds((S, GW), F32)] * 3 + [
            _sds((S, MW), BF16), _sds((S, BR_A), BF16), _sds((4, BR_A, SH_O), F32), _sds((NM, 2 * MW), F32)],
        scratch_shapes=[pltpu.VMEM((TM, BR_A), BF16)],
        compiler_params=_params(("arbitrary",)),
    )(dh1, *os_, *ls, qm, kv0, z, w_out, jnp.asarray(ones_bd, dtype=BF16), after)


def _attn_bwd(q, k, v, do, lse_s, dd, g):
    d = DILATIONS[g]
    nb = S // d // QBLK
    perm = _perm_matrix(d)

    def body(q_ref, k_ref, v_ref, do_ref, l_ref, dd_ref, p_ref, pt_ref, dq_ref, dk_ref, dv_ref,
             q0, q1, g0, g1, ks, vs, dds, dqs, dks, dvs):
        first, second = _head_masks()
        pm = p_ref[...]
        for t in range(NT):
            rows = slice(t * TM, (t + 1) * TM)
            if d == 1:
                qt = q_ref[rows, :].astype(F32)
                gt = do_ref[rows, :].astype(F32)
            else:
                qt, gt = _pair_dot(pm, q_ref[rows, :], do_ref[rows, :])
                kt, vt = _pair_dot(pm, k_ref[rows, :], v_ref[rows, :])
                _tile_to_streams(kt, ks, t, d)
                _tile_to_streams(vt, vs, t, d)
                _tile_to_streams(_split_dot(pm, dd_ref[rows, :], 2), dds, t, d)
            _tile_to_streams(jnp.where(first, qt, 0.0), q0, t, d)
            _tile_to_streams(jnp.where(second, qt, 0.0), q1, t, d)
            _tile_to_streams(jnp.where(first, gt, 0.0), g0, t, d)
            _tile_to_streams(jnp.where(second, gt, 0.0), g1, t, d)
        kref, vref, ddref = (k_ref, v_ref, dd_ref) if d == 1 else (ks, vs, dds)
        dqref, dkref, dvref = (dq_ref, dk_ref, dv_ref) if d == 1 else (dqs, dks, dvs)
        dkref[...] = jnp.zeros_like(dkref)
        dvref[...] = jnp.zeros_like(dvref)

        def blk(b, carry):
            r0 = pl.multiple_of(b * QBLK, QBLK)
            p0 = pl.multiple_of(jnp.maximum(b - 1, 0) * QBLK, QBLK)
            kk = jnp.concatenate([kref[pl.ds(p0, QBLK), :], kref[pl.ds(r0, QBLK), :]], axis=0)
            vv = jnp.concatenate([vref[pl.ds(p0, QBLK), :], vref[pl.ds(r0, QBLK), :]], axis=0)
            lb = l_ref[pl.ds(r0, QBLK), :]
            ddb = ddref[pl.ds(r0, QBLK), :]
            valid = _band_mask(b & (nb - 1))
            dqh, dkk, dvv = [], None, None
            for h, (qh, gh) in enumerate(((q0, g0), (q1, g1))):
                qb = qh[pl.ds(r0, QBLK), :]
                gb = gh[pl.ds(r0, QBLK), :]
                s = _dot_nt(qb, kk)
                p = jnp.where(valid, jnp.exp(s - lb[:, h * HD:h * HD + 1]), 0.0)
                dp = _dot_nt(gb, vv)
                ds = (p * (dp - ddb[:, h * HD:h * HD + 1])).astype(BF16)
                dqh.append(_dot(ds, kk))
                tk = _dot_tn(ds, qb)
                tv = _dot_tn(p.astype(BF16), gb)
                dkk = tk if dkk is None else dkk + tk
                dvv = tv if dvv is None else dvv + tv
            dqref[pl.ds(r0, QBLK), :] = jnp.where(first[:QBLK], dqh[0], dqh[1])
            dkref[pl.ds(p0, QBLK), :] += dkk[:QBLK]
            dkref[pl.ds(r0, QBLK), :] += dkk[QBLK:]
            dvref[pl.ds(p0, QBLK), :] += dvv[:QBLK]
            dvref[pl.ds(r0, QBLK), :] += dvv[QBLK:]
            return carry

        lax.fori_loop(0, S // QBLK, blk, 0, unroll=BWD_UNROLL)
        if d > 1:
            ptm = pt_ref[...]
            for t in range(NT):
                rows = slice(t * TM, (t + 1) * TM)
                dq_ref[rows, :] = _split_dot(ptm, _tile_from_streams(dqs, t, d), 2)
                dk_ref[rows, :] = _split_dot(ptm, _tile_from_streams(dks, t, d), 2)
                dv_ref[rows, :] = _split_dot(ptm, _tile_from_streams(dvs, t, d), 2)

    qkv_spec = pl.BlockSpec((S, LANES), lambda c: (0, g * NCHUNK + c))
    one_spec = pl.BlockSpec((S, LANES), lambda c: (0, c))
    return pl.pallas_call(
        body, name=f"attn_bwd_g{g}", grid=(NCHUNK,),
        in_specs=[qkv_spec] * 3 + [one_spec] * 3 + [_full((TM, TM))] * 2, out_specs=[one_spec] * 3,
        out_shape=[_sds((S, GW), F32)] * 3,
        scratch_shapes=[pltpu.VMEM((S, LANES), BF16)] * 6 + [pltpu.VMEM((S, LANES), F32)] * 4,
        compiler_params=_params(("parallel",)),
    )(q, k, v, do, lse_s, dd, jnp.asarray(perm, BF16), jnp.asarray(perm.T, BF16))


def _qkv_bwd(dqs, dks, dvs, dqm, dz, c, s1, s2):
    def body(q0, q1, q2, k0, k1, k2, v0, v1, v2, dqm_ref, dz_ref, c_ref, s1_ref, s2_ref, dp_ref):
        cc, a1, a2 = c_ref[...], s1_ref[...], s2_ref[...]
        for g, (qr, kr, vr) in enumerate(((q0, k0, v0), (q1, k1, v1), (q2, k2, v2))):
            for j in range(GW // 128):
                ls_ = slice(j * 128, (j + 1) * 128)
                c0 = g * GW + j * 128
                dp_ref[:, c0:c0 + 128] = (_rope_bwd(qr[:, ls_], cc, a1, a2) * SCALE).astype(BF16)
                dp_ref[:, NQ + c0:NQ + c0 + 128] = _rope_bwd(kr[:, ls_], cc, a1, a2).astype(BF16)
            dp_ref[:, 2 * NQ + g * GW:2 * NQ + (g + 1) * GW] = vr[...].astype(BF16)
        dp_ref[:, 3 * NQ:3 * NQ + MW] = dqm_ref[...]
        dp_ref[:, 3 * NQ + MW:] = dz_ref[...]

    return pl.pallas_call(
        body, name="qkv_bwd", grid=(NT,),
        in_specs=[_rows(GW)] * 9 + [_rows(MW), _rows(BR_A), _rows(128), _rows(128), _rows(128)],
        out_specs=_rows(IN_A), out_shape=_sds((S, IN_A), BF16),
        compiler_params=_params(("parallel",)),
    )(*dqs, *dks, *dvs, dqm, dz, c, s1, s2)


def _mem_bwd(mem, mg, memn, wkv, dkv0, dkv1):
    def body(mem_ref, mg_ref, memn_ref, w_ref, d0_ref, d1_ref, dw_ref, dg_ref):
        mf = mem_ref[...]
        n = mf * lax.rsqrt(jnp.mean(mf * mf, axis=-1, keepdims=True) + EPS)
        for i, d_ref in enumerate((d0_ref, d1_ref)):
            dkv = d_ref[...].astype(BF16)
            mn = memn_ref[i]
            for s in range(4):
                cs = slice(s * NM, (s + 1) * NM)
                dw_ref[s, i] = _dot_tn(mn[:, cs], dkv)
                dmn = _dot_nt(dkv, w_ref[s, i])
                dg_ref[i:i + 1, cs] = jnp.sum(dmn * n[:, cs], axis=0, keepdims=True)

    return pl.pallas_call(
        body, name="mem_bwd", grid=(1,),
        in_specs=[_full((NM, D)), _full((2, D)), _full((2, NM, D)), _full((4, 2, NM, 2 * MW)),
                  _full((NM, 2 * MW)), _full((NM, 2 * MW))],
        out_specs=[_full((4, 2, NM, 2 * MW)), _full((2, D))],
        out_shape=[_sds((4, 2, NM, 2 * MW), F32), _sds((2, D), F32)],
        compiler_params=_params(("arbitrary",)),
    )(mem, mg, memn, wkv, dkv0, dkv1)


MESH = pl.DeviceIdType.MESH
ANY = pl.BlockSpec(memory_space=pl.ANY)
BIG = (("wkv", 2, NM, 2 * MW), ("w_in_a", 1, D, SH_A), ("w_out_a", 1, BR_A, SH_O),
       ("w_in_b", 1, D, SH_B), ("w_out_b", 1, BR_B // 4, D))
NBIG = len(BIG)
CW_ROWS = 8


def _place():
    x, y, c = lax.axis_index("x"), lax.axis_index("y"), lax.axis_index("c")
    chips = ((1 - x, y), (x, 1 - y), (1 - x, 1 - y))
    return x, y, c, chips


def _remote(src, dst, ssem, rsem, dev):
    return pltpu.make_async_remote_copy(src_ref=src, dst_ref=dst, send_sem=ssem, recv_sem=rsem,
                                        device_id=dev, device_id_type=MESH)


def _cast_weights(place, ws, after, idx, name):
    nblk = 4
    n = len(idx)
    dims = [BIG[w][1:] for w in idx]

    def body(pref, *refs):
        for i in range(n):
            refs[n + 1 + i][0] = refs[i][...].astype(BF16)

    grid_spec = pltpu.PrefetchScalarGridSpec(
        num_scalar_prefetch=1, grid=(nblk,),
        in_specs=[pl.BlockSpec((k, r // nblk, cdim), lambda i, pref: (0, i, 0)) for k, r, cdim in dims]
        + [pl.BlockSpec(memory_space=pl.ANY)],
        out_specs=[pl.BlockSpec((1, k, r // nblk, cdim), lambda i, pref: (pref[1], 0, i, 0)) for k, r, cdim in dims])
    return pl.pallas_call(
        body, name=name, grid_spec=grid_spec,
        out_shape=[_sds((4, k, r, cdim), BF16) for k, r, cdim in dims],
        compiler_params=_params(("parallel",)),
    )(place, *ws, after)


LAYER_A = (0, 1, 2)
LAYER_B = (3, 4)
HBM = pl.BlockSpec(memory_space=pltpu.HBM)
SEM = pl.BlockSpec(memory_space=pltpu.SEMAPHORE)
EFFECT = pltpu.SideEffectType.DATAFLOW_SIDE_EFFECTING
TOKEN = (8, 128)


def _half(ref, w, which):
    h = BIG[w][2] // 2
    return ref.at[:, pl.ds(which * h, h), :]


def _skip_arg(body, pos, *refs):
    return body(*refs[:pos], *refs[pos + 1:])


def _gather_weights(wb, cw, idx, name):
    n = len(idx)

    def body(*refs):
        src_cw = refs[n]
        dst = refs[n + 1:2 * n + 2]
        loc_sem, send_sems, recv_sems, fsend_sems, frecv_sems = refs[2 * n + 2:]
        x, y, c, chips = _place()
        me = 2 * x + y
        loc = pltpu.make_async_copy(src_cw, dst[n].at[me], loc_sem)
        loc.start()
        sends = []
        for j, (px, py) in enumerate(chips):
            for i in range(n):
                mine = _half(dst[i].at[me], idx[i], c)
                sends.append(_remote(mine, mine, send_sems.at[j, i], recv_sems.at[j, i], (px, py, c)))
            sends.append(_remote(src_cw, dst[n].at[me], send_sems.at[j, n], recv_sems.at[j, n], (px, py, c)))
        for cp in sends:
            cp.start()
        fwds = []
        for j, (px, py) in enumerate(chips):
            for i in range(n):
                got = _half(dst[i].at[2 * px + py], idx[i], c)
                _remote(got, got, send_sems.at[j, i], recv_sems.at[j, i], (px, py, c)).wait_recv()
                fwds.append(_remote(got, got, fsend_sems.at[j, i], frecv_sems.at[j, i], (x, y, 1 - c)))
                fwds[-1].start()
            got = dst[n].at[2 * px + py]
            _remote(got, got, send_sems.at[j, n], recv_sems.at[j, n], (px, py, c)).wait_recv()
        for j, (px, py) in enumerate(chips):
            for i in range(n):
                got = _half(dst[i].at[2 * px + py], idx[i], 1 - c)
                _remote(got, got, fsend_sems.at[j, i], frecv_sems.at[j, i], (x, y, 1 - c)).wait_recv()
        for cp in sends + fwds:
            cp.wait_send()
        loc.wait()

    out_shape = [_sds(w.shape, BF16) for w in wb] + [_sds((4, CW_ROWS, SH_O), F32)]
    return pl.pallas_call(
        body, name=name, in_specs=[ANY] * (n + 1), out_specs=[ANY] * (n + 1), out_shape=out_shape,
        input_output_aliases={i: i for i in range(n)},
        scratch_shapes=[pltpu.SemaphoreType.DMA, pltpu.SemaphoreType.DMA((3, n + 1)),
                        pltpu.SemaphoreType.DMA((3, n + 1)), pltpu.SemaphoreType.DMA((3, n)),
                        pltpu.SemaphoreType.DMA((3, n))],
    )(*wb, cw)


def _gather_start(wb, after, idx, name):
    n = len(idx)

    def body(*refs):
        src = refs[:n]
        send_sems, recv_sems = refs[n + 1], refs[n + 2]
        token = refs[2 * n + 3]
        x, y, c, chips = _place()
        me = 2 * x + y
        for j, (px, py) in enumerate(chips):
            for i in range(n):
                mine = _half(src[i].at[me], idx[i], c)
                _remote(mine, mine, send_sems.at[j * n + i], recv_sems.at[j * n + i], (px, py, c)).start()
        token[...] = jnp.zeros(TOKEN, F32)

    outs = pl.pallas_call(
        body, name=name, in_specs=[HBM] * n + [ANY],
        out_specs=(SEM, SEM) + (HBM,) * n + (pl.BlockSpec(memory_space=pltpu.VMEM),),
        out_shape=(pltpu.SemaphoreType.DMA((3 * n,)), pltpu.SemaphoreType.DMA((3 * n,)))
        + tuple(pltpu.HBM(w.shape, w.dtype) for w in wb) + (_sds(TOKEN, F32),),
        input_output_aliases={i: 2 + i for i in range(n)},
        compiler_params=pltpu.CompilerParams(has_side_effects=EFFECT),
    )(*[pltpu.with_memory_space_constraint(w, pltpu.HBM) for w in wb], after)
    return outs[0], outs[1], list(outs[2:2 + n]), outs[2 + n]


def _gather_wait(send_sems, recv_sems, wb, after, idx, name):
    n = len(idx)

    def body(*refs):
        buf = refs[:n]
        send_sems, recv_sems = refs[n], refs[n + 1]
        x, y, c, chips = _place()
        me = 2 * x + y
        for j, (px, py) in enumerate(chips):
            for i in range(n):
                mine = _half(buf[i].at[me], idx[i], c)
                got = _half(buf[i].at[2 * px + py], idx[i], c)
                _remote(mine, mine, send_sems.at[j * n + i], recv_sems.at[j * n + i], (px, py, c)).wait_send()
                _remote(got, got, send_sems.at[j * n + i], recv_sems.at[j * n + i], (px, py, c)).wait_recv()

    outs = pl.pallas_call(
        body, name=name, in_specs=[HBM] * n + [SEM, SEM] + [ANY] * len(after), out_specs=(HBM,) * n,
        out_shape=tuple(pltpu.HBM(w.shape, w.dtype) for w in wb),
        input_output_aliases={i: i for i in range(n)},
        compiler_params=pltpu.CompilerParams(has_side_effects=EFFECT),
    )(*wb, send_sems, recv_sems, *after)
    return list(outs)


def _gather_forward(wb, idx, name, cw=None):
    n = len(idx)
    m = n if cw is None else n + 1

    def body(*refs):
        dst = refs[m:2 * m]
        send_sems, recv_sems = refs[2 * m], refs[2 * m + 1]
        x, y, c, chips = _place()
        cps = []
        for j, (px, py) in enumerate(chips):
            for i in range(n):
                got = _half(dst[i].at[2 * px + py], idx[i], c)
                cps.append(_remote(got, got, send_sems.at[j, i], recv_sems.at[j, i], (x, y, 1 - c)))
                cps[-1].start()
        if cw is not None:
            src_cw, loc_sem = refs[n], refs[2 * m + 2]
            me = 2 * x + y
            loc = pltpu.make_async_copy(src_cw, dst[n].at[me], loc_sem)
            loc.start()
            for j, (px, py) in enumerate(chips):
                cps.append(_remote(src_cw, dst[n].at[me], send_sems.at[j, n], recv_sems.at[j, n], (px, py, c)))
                cps[-1].start()
        for j, (px, py) in enumerate(chips):
            for i in range(n):
                got = _half(dst[i].at[2 * px + py], idx[i], 1 - c)
                _remote(got, got, send_sems.at[j, i], recv_sems.at[j, i], (x, y, 1 - c)).wait_recv()
            if cw is not None:
                got = dst[n].at[2 * px + py]
                _remote(got, got, send_sems.at[j, n], recv_sems.at[j, n], (px, py, c)).wait_recv()
        for cp in cps:
            cp.wait_send()
        if cw is not None:
            loc.wait()

    out_shape = [_sds(w.shape, BF16) for w in wb]
    scratch = [pltpu.SemaphoreType.DMA((3, m)), pltpu.SemaphoreType.DMA((3, m))]
    args = list(wb)
    if cw is not None:
        out_shape.append(_sds((4, CW_ROWS, SH_O), F32))
        scratch.append(pltpu.SemaphoreType.DMA)
        args.append(cw)
    return pl.pallas_call(
        body, name=name, in_specs=[ANY] * m, out_specs=[ANY] * m, out_shape=out_shape,
        input_output_aliases={i: i for i in range(n)}, scratch_shapes=scratch,
    )(*args)


def _forward_start(wb, cw, after, idx, name):
    n = len(idx)
    m = n if cw is None else n + 2

    def body(*refs):
        buf = refs[:n]
        send_sems, recv_sems = refs[m + 1], refs[m + 2]
        token = refs[2 * m + 3]
        x, y, c, chips = _place()
        for j, (px, py) in enumerate(chips):
            for i in range(n):
                got = _half(buf[i].at[2 * px + py], idx[i], c)
                _remote(got, got, send_sems.at[j * (n + 1) + i], recv_sems.at[j * (n + 1) + i], (x, y, 1 - c)).start()
            if cw is not None:
                _remote(refs[n], refs[n + 1].at[2 * x + y], send_sems.at[j * (n + 1) + n],
                        recv_sems.at[j * (n + 1) + n], (px, py, c)).start()
        token[...] = jnp.zeros(TOKEN, F32)

    arrays = list(wb) if cw is None else list(wb) + [cw, lax.empty((4, CW_ROWS, SH_O), F32)]
    outs = pl.pallas_call(
        body, name=name, in_specs=[HBM] * m + [ANY],
        out_specs=(SEM, SEM) + (HBM,) * m + (pl.BlockSpec(memory_space=pltpu.VMEM),),
        out_shape=(pltpu.SemaphoreType.DMA((3 * (n + 1),)), pltpu.SemaphoreType.DMA((3 * (n + 1),)))
        + tuple(pltpu.HBM(a.shape, a.dtype) for a in arrays) + (_sds(TOKEN, F32),),
        input_output_aliases={i: 2 + i for i in range(m)},
        compiler_params=pltpu.CompilerParams(has_side_effects=EFFECT),
    )(*[pltpu.with_memory_space_constraint(a, pltpu.HBM) for a in arrays], after)
    return outs[0], outs[1], list(outs[2:2 + m]), outs[2 + m]


def _forward_wait(send_sems, recv_sems, arrays, after, idx, with_cw, name):
    n = len(idx)
    m = len(arrays)

    def body(*refs):
        buf = refs[:n]
        send_sems, recv_sems = refs[m], refs[m + 1]
        x, y, c, chips = _place()
        for j, (px, py) in enumerate(chips):
            for i in range(n):
                sent = _half(buf[i].at[2 * px + py], idx[i], c)
                got = _half(buf[i].at[2 * px + py], idx[i], 1 - c)
                k = j * (n + 1) + i
                _remote(sent, sent, send_sems.at[k], recv_sems.at[k], (x, y, 1 - c)).wait_send()
                _remote(got, got, send_sems.at[k], recv_sems.at[k], (x, y, 1 - c)).wait_recv()
            if with_cw:
                k = j * (n + 1) + n
                theirs = refs[n + 1].at[2 * px + py]
                _remote(refs[n], theirs, send_sems.at[k], recv_sems.at[k], (px, py, c)).wait_send()
                _remote(refs[n], theirs, send_sems.at[k], recv_sems.at[k], (px, py, c)).wait_recv()

    outs = pl.pallas_call(
        body, name=name, in_specs=[HBM] * m + [SEM, SEM] + [ANY] * len(after), out_specs=(HBM,) * m,
        out_shape=tuple(pltpu.HBM(a.shape, a.dtype) for a in arrays),
        input_output_aliases={i: i for i in range(m)},
        compiler_params=pltpu.CompilerParams(has_side_effects=EFFECT),
    )(*arrays, send_sems, recv_sems, *after)
    return list(outs)


def _pair_exchange(gs, idx, name):
    n = len(idx)

    def body(*refs):
        src, dst = refs[:n], refs[n:2 * n]
        send_sems, recv_sems = refs[2 * n:]
        x, y, c, _ = _place()
        cps = []
        for i in range(n):
            h = BIG[idx[i]][2] // 2
            cps.append(_remote(src[i].at[:, :, pl.ds((1 - c) * h, h), :], dst[i], send_sems.at[i], recv_sems.at[i],
                               (x, y, 1 - c)))
            cps[-1].start()
        for cp in cps:
            cp.wait()

    return pl.pallas_call(
        body, name=name, in_specs=[ANY] * n, out_specs=[ANY] * n,
        out_shape=[_sds((4, BIG[w][1], BIG[w][2] // 2, BIG[w][3]), F32) for w in idx],
        scratch_shapes=[pltpu.SemaphoreType.DMA((n,)), pltpu.SemaphoreType.DMA((n,))],
    )(*gs)


def _pair_start(gs, idx, name):
    n = len(idx)

    def body(*refs):
        src, land = refs[:n], refs[n:2 * n]
        send_sems, recv_sems = refs[2 * n], refs[2 * n + 1]
        token = refs[4 * n + 2]
        x, y, c, _ = _place()
        for i in range(n):
            h = BIG[idx[i]][2] // 2
            _remote(src[i].at[:, :, pl.ds((1 - c) * h, h), :], land[i], send_sems.at[i], recv_sems.at[i],
                    (x, y, 1 - c)).start()
        token[...] = jnp.zeros(TOKEN, F32)

    lands = [lax.empty((4, BIG[w][1], BIG[w][2] // 2, BIG[w][3]), F32) for w in idx]
    arrays = list(gs) + lands
    outs = pl.pallas_call(
        body, name=name, in_specs=[HBM] * (2 * n),
        out_specs=(SEM, SEM) + (HBM,) * (2 * n) + (pl.BlockSpec(memory_space=pltpu.VMEM),),
        out_shape=(pltpu.SemaphoreType.DMA((n,)), pltpu.SemaphoreType.DMA((n,)))
        + tuple(pltpu.HBM(a.shape, a.dtype) for a in arrays) + (_sds(TOKEN, F32),),
        input_output_aliases={i: 2 + i for i in range(2 * n)},
        compiler_params=pltpu.CompilerParams(has_side_effects=EFFECT),
    )(*[pltpu.with_memory_space_constraint(a, pltpu.HBM) for a in arrays])
    return outs[0], outs[1], list(outs[2:2 + n]), list(outs[2 + n:2 + 2 * n]), outs[2 + 2 * n]


def _pair_wait(send_sems, recv_sems, gs, lands, after, idx, name):
    n = len(idx)

    def body(*refs):
        src, land = refs[:n], refs[n:2 * n]
        send_sems, recv_sems = refs[2 * n], refs[2 * n + 1]
        x, y, c, _ = _place()
        for i in range(n):
            h = BIG[idx[i]][2] // 2
            cp = _remote(src[i].at[:, :, pl.ds((1 - c) * h, h), :], land[i], send_sems.at[i], recv_sems.at[i],
                         (x, y, 1 - c))
            cp.wait_send()
            cp.wait_recv()

    arrays = list(gs) + list(lands)
    outs = pl.pallas_call(
        body, name=name, in_specs=[HBM] * (2 * n) + [SEM, SEM] + [ANY] * len(after), out_specs=(HBM,) * (2 * n),
        out_shape=tuple(pltpu.HBM(a.shape, a.dtype) for a in arrays),
        input_output_aliases={i: i for i in range(2 * n)},
        compiler_params=pltpu.CompilerParams(has_side_effects=EFFECT),
    )(*arrays, send_sems, recv_sems, *after)
    return list(outs[:n]), list(outs[n:])


def _pair_sum(place, g, r1, i):
    _, k, r, cdim = BIG[i]
    h = r // 2

    def body(pref, g_ref, r_ref, o_ref):
        o_ref[...] = (g_ref[...] + r_ref[...]).astype(BF16)

    grid_spec = pltpu.PrefetchScalarGridSpec(
        num_scalar_prefetch=1, grid=(4, k),
        in_specs=[pl.BlockSpec((1, 1, h, cdim), lambda s, t, pref: (s, t, pref[0], 0)),
                  pl.BlockSpec((1, 1, h, cdim), lambda s, t, pref: (s, t, 0, 0))],
        out_specs=pl.BlockSpec((1, 1, h, cdim), lambda s, t, pref: (s, t, 0, 0)))
    return pl.pallas_call(
        body, name=f"pair_sum_{BIG[i][0]}", grid_spec=grid_spec, out_shape=_sds((4, k, h, cdim), BF16),
        compiler_params=_params(("parallel", "parallel")),
    )(place, g, r1)


def _pair_sums(place, gs, r1s, idx, name):
    n = len(idx)
    dims = [(BIG[w][1], BIG[w][2] // 2, BIG[w][3]) for w in idx]

    def body(pref, *refs):
        for i in range(n):
            refs[2 * n + i][...] = (refs[i][...] + refs[n + i][...]).astype(BF16)

    mine = [pl.BlockSpec((1, k, h, cdim), lambda s, pref: (s, 0, pref[0], 0)) for k, h, cdim in dims]
    whole = [pl.BlockSpec((1, k, h, cdim), lambda s, pref: (s, 0, 0, 0)) for k, h, cdim in dims]
    grid_spec = pltpu.PrefetchScalarGridSpec(num_scalar_prefetch=1, grid=(4,), in_specs=mine + whole, out_specs=whole)
    return pl.pallas_call(
        body, name=name, grid_spec=grid_spec, out_shape=[_sds((4, k, h, cdim), BF16) for k, h, cdim in dims],
        compiler_params=_params(("parallel",)),
    )(place, *gs, *r1s)


def _chip_start(ps, idx, name):
    n = len(idx)

    def body(*refs):
        src, land = refs[:n], refs[n:2 * n]
        send_sems, recv_sems = refs[2 * n], refs[2 * n + 1]
        token = refs[4 * n + 2]
        x, y, c, chips = _place()
        for j, (px, py) in enumerate(chips):
            for i in range(n):
                _remote(src[i].at[2 * px + py], land[i].at[j], send_sems.at[j * n + i], recv_sems.at[j * n + i],
                        (px, py, c)).start()
        token[...] = jnp.zeros(TOKEN, F32)

    lands = [lax.empty((3,) + p.shape[1:], BF16) for p in ps]
    outs = pl.pallas_call(
        body, name=name, in_specs=[HBM] * (2 * n),
        out_specs=(SEM, SEM) + (HBM,) * (2 * n) + (pl.BlockSpec(memory_space=pltpu.VMEM),),
        out_shape=(pltpu.SemaphoreType.DMA((3 * n,)), pltpu.SemaphoreType.DMA((3 * n,)))
        + tuple(pltpu.HBM(a.shape, a.dtype) for a in list(ps) + lands) + (_sds(TOKEN, F32),),
        input_output_aliases={i: 2 + i for i in range(2 * n)},
        compiler_params=pltpu.CompilerParams(has_side_effects=EFFECT),
    )(*[pltpu.with_memory_space_constraint(a, pltpu.HBM) for a in list(ps) + lands])
    return outs[0], outs[1], list(outs[2:2 + n]), list(outs[2 + n:2 + 2 * n]), outs[2 + 2 * n]


def _chip_wait(send_sems, recv_sems, ps, lands, after, idx, name):
    n = len(idx)

    def body(*refs):
        src, land = refs[:n], refs[n:2 * n]
        send_sems, recv_sems = refs[2 * n], refs[2 * n + 1]
        x, y, c, chips = _place()
        for j, (px, py) in enumerate(chips):
            for i in range(n):
                cp = _remote(src[i].at[2 * px + py], land[i].at[j], send_sems.at[j * n + i], recv_sems.at[j * n + i],
                             (px, py, c))
                cp.wait_send()
                cp.wait_recv()

    arrays = list(ps) + list(lands)
    outs = pl.pallas_call(
        body, name=name, in_specs=[HBM] * (2 * n) + [SEM, SEM] + [ANY] * len(after), out_specs=(HBM,) * (2 * n),
        out_shape=tuple(pltpu.HBM(a.shape, a.dtype) for a in arrays),
        input_output_aliases={i: i for i in range(2 * n)},
        compiler_params=pltpu.CompilerParams(has_side_effects=EFFECT),
    )(*arrays, send_sems, recv_sems, *after)
    return list(outs[n:])


def _chip_sum(place, g, r1, r2, i):
    _, k, r, cdim = BIG[i]
    h = r // 2

    def body(pref, g_ref, r1_ref, r2_ref, o_ref):
        acc = g_ref[0, 0] + r1_ref[0, 0]
        for j in range(3):
            acc = acc + r2_ref[j, 0].astype(F32)
        o_ref[0] = acc

    grid_spec = pltpu.PrefetchScalarGridSpec(
        num_scalar_prefetch=1, grid=(k,),
        in_specs=[pl.BlockSpec((1, 1, h, cdim), lambda t, pref: (pref[1], t, pref[0], 0)),
                  pl.BlockSpec((1, 1, h, cdim), lambda t, pref: (pref[1], t, 0, 0)),
                  pl.BlockSpec((3, 1, h, cdim), lambda t, pref: (0, t, 0, 0))],
        out_specs=pl.BlockSpec((1, h, cdim), lambda t, pref: (t, pref[0], 0)))
    return pl.pallas_call(
        body, name=f"chip_sum_{BIG[i][0]}", grid_spec=grid_spec, out_shape=_sds((k, r, cdim), F32),
        compiler_params=_params(("parallel",)),
    )(place, g, r1, r2)


def _chip_sums(place, gs, r1s, r2s, idx, name):
    n = len(idx)
    dims = [(BIG[w][1], BIG[w][2] // 4, BIG[w][3]) for w in idx]

    def body(pref, *refs):
        for i in range(n):
            acc = refs[i][0] + refs[n + i][0]
            for j in range(3):
                acc = acc + refs[2 * n + i][j].astype(F32)
            refs[3 * n + i][...] = acc

    in_specs = ([pl.BlockSpec((1, k, q, cdim), lambda t, pref: (pref[1], 0, pref[0] * 2 + t, 0)) for k, q, cdim in dims]
                + [pl.BlockSpec((1, k, q, cdim), lambda t, pref: (pref[1], 0, t, 0)) for k, q, cdim in dims]
                + [pl.BlockSpec((3, k, q, cdim), lambda t, pref: (0, 0, t, 0)) for k, q, cdim in dims])
    out_specs = [pl.BlockSpec((k, q, cdim), lambda t, pref: (0, pref[0] * 2 + t, 0)) for k, q, cdim in dims]
    grid_spec = pltpu.PrefetchScalarGridSpec(num_scalar_prefetch=1, grid=(2,), in_specs=in_specs, out_specs=out_specs)
    return pl.pallas_call(
        body, name=name, grid_spec=grid_spec, out_shape=[_sds(BIG[w][1:], F32) for w in idx],
        compiler_params=_params(("parallel",)),
    )(place, *gs, *r1s, *r2s)


def _pair_gather(hs, idx, name):
    n = len(idx)

    def body(*refs):
        dst = refs[n:2 * n]
        send_sems, recv_sems = refs[2 * n:]
        x, y, c, _ = _place()
        cps = []
        for i in range(n):
            mine = _half(dst[i], idx[i], c)
            cps.append(_remote(mine, mine, send_sems.at[i], recv_sems.at[i], (x, y, 1 - c)))
            cps[-1].start()
        for i in range(n):
            theirs = _half(dst[i], idx[i], 1 - c)
            _remote(theirs, theirs, send_sems.at[i], recv_sems.at[i], (x, y, 1 - c)).wait_recv()
        for cp in cps:
            cp.wait_send()

    return pl.pallas_call(
        body, name=name, in_specs=[ANY] * n, out_specs=[ANY] * n,
        out_shape=[_sds(BIG[w][1:], F32) for w in idx],
        input_output_aliases={i: i for i in range(n)},
        scratch_shapes=[pltpu.SemaphoreType.DMA((n,)), pltpu.SemaphoreType.DMA((n,))],
    )(*hs)


SMALL_ROWS = 40


def _all_reduce_small(pack, after):
    def body(p_ref, o_ref, slots, send_sems, recv_sems):
        x, y, c, _ = _place()
        me = 4 * x + 2 * y + c
        cps = []
        for r in range(1, 8):
            peer = (x if not r & 4 else 1 - x, y if not r & 2 else 1 - y, c if not r & 1 else 1 - c)
            cps.append(_remote(p_ref, slots.at[r], send_sems.at[r - 1], recv_sems.at[r - 1], peer))
            cps[-1].start()
        slots[0] = p_ref[...]
        for cp in cps:
            cp.wait()
        acc = slots[me]
        for dev in range(1, 8):
            acc = acc + slots[jnp.bitwise_xor(me, dev)]
        o_ref[...] = acc

    vm = pl.BlockSpec(memory_space=pltpu.VMEM)
    return pl.pallas_call(
        functools.partial(_skip_arg, body, 1), name="all_reduce_small", in_specs=[vm, ANY], out_specs=vm,
        out_shape=_sds((SMALL_ROWS, D), F32),
        scratch_shapes=[pltpu.VMEM((8, SMALL_ROWS, D), F32), pltpu.SemaphoreType.DMA((7,)),
                        pltpu.SemaphoreType.DMA((7,))],
    )(pack, after)


def _adamw_math(w, g, m, v):
    m = ADAM_B1 * m + (1.0 - ADAM_B1) * g
    v = ADAM_B2 * v + (1.0 - ADAM_B2) * (g * g)
    m_hat = m / (1.0 - ADAM_B1 ** ADAM_STEP)
    v_hat = v / (1.0 - ADAM_B2 ** ADAM_STEP)
    delta = -ADAM_LR * (m_hat / (jnp.sqrt(v_hat) + ADAM_EPS) + ADAM_WD * w)
    return delta, m, v


def _adamw_big(w, g, m, v, i):
    _, k, r, cdim = BIG[i]
    nblk = 4 if k == 1 else 1

    def body(w_ref, g_ref, m_ref, v_ref, d_ref, nm_ref, nv_ref, go_ref):
        gv = g_ref[...]
        d_ref[...], nm_ref[...], nv_ref[...] = _adamw_math(w_ref[...], gv, m_ref[...], v_ref[...])
        go_ref[...] = gv

    spec = pl.BlockSpec((1, r // nblk, cdim), lambda t, b: (t, b, 0))
    return pl.pallas_call(
        body, name=f"adamw_{BIG[i][0]}", grid=(k, nblk), in_specs=[spec] * 4, out_specs=[spec] * 4,
        out_shape=[_sds((k, r, cdim), F32)] * 4,
        compiler_params=_params(("parallel", "parallel")),
    )(w, g, m, v)


def _small_start(pack, after):
    def body(pack_ref, land_ref, after_ref, send_sems, recv_sems, pack_thru, land_thru, token):
        x, y, c, _ = _place()
        for r in range(1, 8):
            peer = (x if not r & 4 else 1 - x, y if not r & 2 else 1 - y, c if not r & 1 else 1 - c)
            _remote(pack_ref, land_ref.at[r - 1], send_sems.at[r - 1], recv_sems.at[r - 1], peer).start()
        token[...] = jnp.zeros(TOKEN, F32)

    land = lax.empty((7, SMALL_ROWS, D), F32)
    outs = pl.pallas_call(
        body, name="small_start", in_specs=[HBM, HBM, ANY],
        out_specs=(SEM, SEM, HBM, HBM, pl.BlockSpec(memory_space=pltpu.VMEM)),
        out_shape=(pltpu.SemaphoreType.DMA((7,)), pltpu.SemaphoreType.DMA((7,)), pltpu.HBM(pack.shape, F32),
                   pltpu.HBM(land.shape, F32), _sds(TOKEN, F32)),
        input_output_aliases={0: 2, 1: 3},
        compiler_params=pltpu.CompilerParams(has_side_effects=EFFECT),
    )(pltpu.with_memory_space_constraint(pack, pltpu.HBM), pltpu.with_memory_space_constraint(land, pltpu.HBM), after)
    return outs


def _small_wait(send_sems, recv_sems, pack, land, after):
    def body(pack_ref, land_ref, send_sems, recv_sems, *rest):
        x, y, c, _ = _place()
        for r in range(1, 8):
            peer = (x if not r & 4 else 1 - x, y if not r & 2 else 1 - y, c if not r & 1 else 1 - c)
            cp = _remote(pack_ref, land_ref.at[r - 1], send_sems.at[r - 1], recv_sems.at[r - 1], peer)
            cp.wait_send()
            cp.wait_recv()

    return pl.pallas_call(
        body, name="small_wait", in_specs=[HBM, HBM, SEM, SEM] + [ANY] * len(after), out_specs=(HBM, HBM),
        out_shape=(pltpu.HBM(pack.shape, F32), pltpu.HBM(land.shape, F32)),
        input_output_aliases={0: 0, 1: 1},
        compiler_params=pltpu.CompilerParams(has_side_effects=EFFECT),
    )(pack, land, send_sems, recv_sems, *after)


def _small_update(place, pack, land, ws, ms, vs):
    n = len(ws)

    def body(pref, pack_ref, land_ref, *refs):
        chip = pref[1]
        me = 2 * chip + pref[0]
        own = pack_ref[...]
        tot = None
        for dev in range(8):
            r = jnp.bitwise_xor(me, dev)
            term = jnp.where(r == 0, own, land_ref[jnp.maximum(r - 1, 0)])
            tot = term if tot is None else tot + term
        out, buf = refs[3 * n:-1], refs[-1]
        buf[...] = tot
        g_conv = jnp.zeros((3, SH_O), F32)
        for s in range(4):
            g_conv = g_conv + jnp.where(chip == s, buf[24:27, s * SH_O:(s + 1) * SH_O], 0.0)
        gs = [buf[0:2, :], buf[8:10, :], buf[16:17, :], g_conv]
        out[0][...] = buf[32:33, 0:128]
        for i in range(n):
            d, nm, nv = _adamw_math(refs[i][...], gs[i], refs[n + i][...], refs[2 * n + i][...])
            out[1 + i][...] = gs[i]
            out[1 + n + i][...] = d
            out[1 + 2 * n + i][...] = nm
            out[1 + 3 * n + i][...] = nv

    def full(shape):
        nd = len(shape)
        return pl.BlockSpec(shape, lambda i, pref: (0,) * nd)

    specs = [full(w.shape) for w in ws]
    grid_spec = pltpu.PrefetchScalarGridSpec(
        num_scalar_prefetch=1, grid=(1,),
        in_specs=[full(pack.shape), full(land.shape)] + specs * 3, out_specs=[full((1, 128))] + specs * 4,
        scratch_shapes=[pltpu.VMEM((SMALL_ROWS, D), F32)])
    outs = pl.pallas_call(
        body, name="small_update", grid_spec=grid_spec,
        out_shape=[_sds((1, 128), F32)] + [_sds(w.shape, F32) for w in ws] * 4,
        compiler_params=_params(("arbitrary",)),
    )(place, pack, land, *ws, *ms, *vs)
    return outs[0], outs[1:1 + n], outs[1 + n:1 + 2 * n], outs[1 + 2 * n:1 + 3 * n], outs[1 + 3 * n:]


def _adamw_layer(ws, gs, ms, vs, idx, name):
    n = len(idx)
    dims = [(BIG[w][1], BIG[w][2] // 4, BIG[w][3]) for w in idx]

    def body(*refs):
        for i in range(n):
            gv = refs[n + i][...]
            d, nm, nv = _adamw_math(refs[i][...], gv, refs[2 * n + i][...], refs[3 * n + i][...])
            refs[4 * n + i][...] = d
            refs[5 * n + i][...] = nm
            refs[6 * n + i][...] = nv
            refs[7 * n + i][...] = gv

    specs = [pl.BlockSpec((k, q, cdim), lambda t: (0, t, 0)) for k, q, cdim in dims]
    outs = pl.pallas_call(
        body, name=name, grid=(4,), in_specs=specs * 4, out_specs=specs * 4,
        out_shape=[_sds(BIG[w][1:], F32) for w in idx] * 4,
        compiler_params=_params(("parallel",)),
    )(*ws, *gs, *ms, *vs)
    return [tuple(outs[j * n + i] for j in range(4)) for i in range(n)]


def _adamw_small(ws, gs, ms, vs):
    n = len(ws)

    def body(*refs):
        for i in range(n):
            w_ref, g_ref, m_ref, v_ref = refs[i], refs[n + i], refs[2 * n + i], refs[3 * n + i]
            d, nm, nv = _adamw_math(w_ref[...], g_ref[...], m_ref[...], v_ref[...])
            refs[4 * n + i][...] = d
            refs[5 * n + i][...] = nm
            refs[6 * n + i][...] = nv

    specs = [_full(w.shape) for w in ws]
    outs = pl.pallas_call(
        body, name="adamw_small", grid=(1,), in_specs=specs * 4, out_specs=specs * 3,
        out_shape=[_sds(w.shape, F32) for w in ws] * 3,
        compiler_params=_params(("arbitrary",)),
    )(*ws, *gs, *ms, *vs)
    return outs[:n], outs[n:2 * n], outs[2 * n:]


def _pad_rows(a, rows):
    return jnp.pad(a, ((0, rows - a.shape[0]), (0, 0)))


def kernel(x, mem, positions, norm_g, mem_norm_g, w_mem_kv, attn_w_in, attn_w_out, conv_w_in, conv_w, conv_w_out, final_g, loss_target, m_norm_g, m_mem_norm_g, m_w_mem_kv, m_attn_w_in, m_attn_w_out, m_conv_w_in, m_conv_w, m_conv_w_out, m_final_g, v_norm_g, v_mem_norm_g, v_w_mem_kv, v_attn_w_in, v_attn_w_out, v_conv_w_in, v_conv_w, v_conv_w_out, v_final_g):
    mx, my, mc = lax.axis_index("x"), lax.axis_index("y"), lax.axis_index("c")
    place = jnp.stack([mc, 2 * mx + my]).astype(jnp.int32)

    w_big = [w_mem_kv, attn_w_in, attn_w_out, conv_w_in, conv_w_out]
    m_big = [m_w_mem_kv, m_attn_w_in, m_attn_w_out, m_conv_w_in, m_conv_w_out]
    v_big = [v_w_mem_kv, v_attn_w_in, v_attn_w_out, v_conv_w_in, v_conv_w_out]
    first, rest = (1,), (0, 2, 3, 4)
    wb1 = _cast_weights(place, [w_big[i] for i in first], place, first, "cast_w_in_a")
    a1_send, a1_recv, a1_bufs, a1_token = _gather_start(wb1, place, first, "gather_a1_start")
    wbr = _cast_weights(place, [w_big[i] for i in rest], a1_token, rest, "cast_weights")
    rest = (0, 2)
    a2_send, a2_recv, a2_bufs, a2_token = _gather_start([wbr[0], wbr[1]], a1_token, rest, "gather_a2_start")
    gb_send, gb_recv, gb_bufs, gb_token = _gather_start([wbr[2], wbr[3]], a2_token, LAYER_B, "gather_b_start")

    xs, tgt = x[0], loss_target[0]
    g0, g1 = norm_g[0:1], norm_g[1:2]
    rc, rs1, rs2 = _rope_tables(positions[0].astype(F32).reshape(S, 1), gb_token)
    a1_bufs = _gather_wait(a1_send, a1_recv, a1_bufs, [rc], first, "gather_a1_wait")
    w_in_a = _gather_forward(a1_bufs, first, "gather_a1_forward")[0].reshape(4, D, SH_A)
    hn0, q, k, v, qm0, z0 = _in_proj_a(xs, g0, w_in_a, rc, rs1, rs2, gb_token)
    a2_bufs = _gather_wait(a2_send, a2_recv, a2_bufs, [q], rest, "gather_a2_wait")
    f2_send, f2_recv, a2_bufs, f2_token = _forward_start(a2_bufs, None, q, rest, "forward_a2_start")
    fwd = [_attn_fwd(q, k, v, 0, f2_token)]
    cw_own = _pad_rows(conv_w[0], CW_ROWS)
    gb_bufs = _gather_wait(gb_send, gb_recv, gb_bufs, [fwd[0][0]], LAYER_B, "gather_b_wait")
    fb_send, fb_recv, gb_bufs, fb_token = _forward_start(gb_bufs, cw_own, fwd[0][0], LAYER_B, "forward_b_start")
    fwd += [_attn_fwd(q, k, v, g, fb_token) for g in (1, 2)]
    os_, ls, lss = [f[0] for f in fwd], [f[1] for f in fwd], [f[2] for f in fwd]
    wkv_f, w_out_a = _forward_wait(f2_send, f2_recv, a2_bufs, [os_[2]], rest, False, "forward_a2_wait")
    w_out_a = w_out_a.reshape(4, BR_A, SH_O)
    memn, kv = _mem_fwd(mem[0], mem_norm_g, wkv_f)
    h1 = _attn_out(os_, ls, qm0, kv[0], z0, xs, w_out_a)

    w_in_b, w_out_b, _, cw_f = _forward_wait(fb_send, fb_recv, gb_bufs, [h1], LAYER_B, True, "forward_b_wait")
    w_in_b = w_in_b.reshape(4, D, SH_B)
    w_out_b = w_out_b.reshape(BR_B, D)
    cw_f = lax.dynamic_update_slice(cw_f, cw_own[None], (2 * mx + my, 0, 0))
    cw8 = cw_f.transpose(1, 0, 2).reshape(CW_ROWS, D)
    hn1, bg, cg, u, qm1, z1 = _in_proj_b(h1, g1, w_in_b)
    dh2, loss_part, dfg = _conv_out_loss(bg, cg, u, cw8, qm1, kv[1], z1, h1, w_out_b, final_g.reshape(1, D), tgt)

    dproj_b, dw_out_b, dcw, dkv1 = _conv_bwd(dh2, bg, cg, u, cw8, qm1, kv[1], z1, w_out_b)
    dw_in_b = _w_in_grad(hn1, dproj_b, IN_B, "w_in_b_grad")
    gs_b = [dw_in_b.reshape(4, 1, D, SH_B), dw_out_b.reshape(4, 1, BR_B // 4, D)]
    pb_send, pb_recv, gs_b, pb_land, pb_token = _pair_start(gs_b, LAYER_B, "pair_b_start")
    dh1, dg1 = _in_proj_bwd(dproj_b, w_in_b, h1, g1, dh2, pb_token, IN_B, "in_proj_b_bwd")
    gs_b, r1_b = _pair_wait(pb_send, pb_recv, gs_b, pb_land, [dh1], LAYER_B, "pair_b_wait")
    ps_b = _pair_sums(place, gs_b, r1_b, LAYER_B, "pair_sums_b")
    cb_send, cb_recv, cb_src, cb_land, cb_token = _chip_start(ps_b, LAYER_B, "chip_b_start")

    outs = _attn_out_bwd(dh1, os_, ls, qm0, kv[0], z0, w_out_a, cb_token)
    dos, dds, dqm, dz, dw_out_a, dkv0 = outs[0:3], outs[3:6], outs[6], outs[7], outs[8], outs[9]
    bwd = [_attn_bwd(q, k, v, dos[g], lss[g], dds[g], g) for g in range(3)]
    dproj_a = _qkv_bwd([b[0] for b in bwd], [b[1] for b in bwd], [b[2] for b in bwd], dqm, dz, rc, rs1, rs2)
    dw_in_a = _w_in_grad(hn0, dproj_a, IN_A, "w_in_a_grad")
    dwkv, dmg = _mem_bwd(mem[0], mem_norm_g, memn, wkv_f, dkv0, dkv1)

    gs_a = [dwkv, dw_in_a.reshape(4, 1, D, SH_A), dw_out_a.reshape(4, 1, BR_A, SH_O)]
    r1_a = _pair_exchange(gs_a, LAYER_A, "pair_exchange_a")
    ps_a = _pair_sums(place, gs_a, r1_a, LAYER_A, "pair_sums_a")
    ca_send, ca_recv, ca_src, ca_land, ca_token = _chip_start(ps_a, LAYER_A, "chip_a_start")

    gx, dg0 = _in_proj_bwd(dproj_a, w_in_a, xs, g0, dh1, ca_token, IN_A, "in_proj_a_bwd")
    pack = jnp.concatenate([_pad_rows(jnp.concatenate([dg0, dg1], axis=0), 8), _pad_rows(dmg, 8), _pad_rows(dfg, 8),
                            dcw, _pad_rows(jnp.pad(loss_part, ((0, 0), (0, D - 128))), 8)], axis=0)
    sm_send, sm_recv, pack, sm_land, sm_token = _small_start(pack, ca_token)
    r2_b = _chip_wait(cb_send, cb_recv, cb_src, cb_land, [ca_token], LAYER_B, "chip_b_wait")
    hs_b = _chip_sums(place, gs_b, r1_b, r2_b, LAYER_B, "chip_sums_b")
    g_b = _pair_gather(hs_b, LAYER_B, "pair_gather_b")
    upd_b = _adamw_layer([w_big[w] for w in LAYER_B], g_b, [m_big[w] for w in LAYER_B], [v_big[w] for w in LAYER_B],
                         LAYER_B, "adamw_b")
    r2_a = _chip_wait(ca_send, ca_recv, ca_src, ca_land, [gx, upd_b[0][0], upd_b[1][0], sm_token], LAYER_A,
                      "chip_a_wait")
    hs_a = _chip_sums(place, gs_a, r1_a, r2_a, LAYER_A, "chip_sums_a")
    g_a = _pair_gather(hs_a, LAYER_A, "pair_gather_a")
    upd_a = _adamw_layer([w_big[w] for w in LAYER_A], g_a, [m_big[w] for w in LAYER_A], [v_big[w] for w in LAYER_A],
                         LAYER_A, "adamw_a")
    upd = upd_a + upd_b
    g_big = [u[3] for u in upd]
    pack, sm_land = _small_wait(sm_send, sm_recv, pack, sm_land, [r2_a[0]])
    sw = [norm_g, mem_norm_g, final_g.reshape(1, D), conv_w[0]]
    sm = [m_norm_g, m_mem_norm_g, m_final_g.reshape(1, D), m_conv_w[0]]
    sv = [v_norm_g, v_mem_norm_g, v_final_g.reshape(1, D), v_conv_w[0]]
    loss_row, sg, sd, snm, snv = _small_update(place, pack, sm_land, sw, sm, sv)
    loss = loss_row[0, 0]
    g_norm, g_memnorm, g_final, g_conv = sg

    def order(norm, memnorm, wkv, w_in_a, w_out_a, w_in_b, conv, w_out_b, final):
        return (norm, memnorm, wkv, w_in_a, w_out_a, w_in_b, conv.reshape(1, 3, SH_O), w_out_b, final.reshape(D))

    grads = order(g_norm, g_memnorm, g_big[0], g_big[1], g_big[2], g_big[3], g_conv, g_big[4], g_final)
    deltas = order(sd[0], sd[1], upd[0][0], upd[1][0], upd[2][0], upd[3][0], sd[3], upd[4][0], sd[2])
    new_m = order(snm[0], snm[1], upd[0][1], upd[1][1], upd[2][1], upd[3][1], snm[3], upd[4][1], snm[2])
    new_v = order(snv[0], snv[1], upd[0][2], upd[1][2], upd[2][2], upd[3][2], snv[3], upd[4][2], snv[2])
    return (loss, gx[None], *grads, *deltas, *new_m, *new_v)
```

```python
import functools

import numpy as np
import jax
import jax.numpy as jnp
from jax import lax
from jax.experimental import pallas as pl
from jax.experimental.pallas import tpu as pltpu

F32 = jnp.float32
BF16 = jnp.bfloat16

S = 2048
D = 1024
TM = 256
NT = S // TM
HD = 64
GW = 512
NQ = 3 * GW
MW = 256
NM = 256
IN_A = 3 * NQ + MW + GW + MW
IN_B = 3 * D + MW + D + MW
BR_A = GW + MW
BR_B = D + MW
SH_A = IN_A // 4
SH_B = IN_B // 4
SH_O = D // 4
QBLK = 128
DILATIONS = (1, 4, 16)
EPS = 1e-6
SCALE = HD ** -0.5
NEG = -1e30
ROPE_THETA = 500000.0

ADAM_LR = 0.001
ADAM_B1 = 0.9
ADAM_B2 = 0.999
ADAM_EPS = 1e-08
ADAM_WD = 0.01
ADAM_STEP = 10

VMEM_LIMIT_BYTES = 60 * 1024 * 1024


def _params(sem=None):
    if sem is None:
        return pltpu.CompilerParams(vmem_limit_bytes=VMEM_LIMIT_BYTES)
    return pltpu.CompilerParams(dimension_semantics=sem, vmem_limit_bytes=VMEM_LIMIT_BYTES)


def _full(shape):
    nd = len(shape)
    return pl.BlockSpec(shape, lambda *_: (0,) * nd)


def _rows(width, tm=TM):
    return pl.BlockSpec((tm, width), lambda i: (i, 0))


def _sds(shape, dtype):
    return jax.ShapeDtypeStruct(shape, dtype)


def _silu_parts(z):
    sig = 1.0 / (1.0 + jnp.exp(-z))
    return z * sig, sig * (1.0 + z * (1.0 - sig))


def _dot(a, b):
    return jnp.dot(a, b, preferred_element_type=F32)


def _dot_nt(a, b):
    return lax.dot_general(a, b, (((1,), (1,)), ((), ())), preferred_element_type=F32)


def _dot_tn(a, b):
    return lax.dot_general(a, b, (((0,), (0,)), ((), ())), preferred_element_type=F32)


def _rope_fwd(t, c, s1, s2):
    return t * c + pltpu.roll(t, 120, 1) * s1 + pltpu.roll(t, 8, 1) * s2


def _rope_bwd(g, c, s1, s2):
    return g * c + pltpu.roll(g * s1, 8, 1) + pltpu.roll(g * s2, 120, 1)


def _mem_attn(qm, kv):
    res = []
    for h in range(MW // HD):
        sl = slice(h * HD, (h + 1) * HD)
        s = _dot_nt(qm[:, sl], kv[:, sl]) * SCALE
        e = jnp.exp(s - jnp.max(s, axis=-1, keepdims=True))
        p = e / jnp.sum(e, axis=-1, keepdims=True)
        res.append((p, _dot(p.astype(BF16), kv[:, MW + h * HD:MW + (h + 1) * HD])))
    return res


def _mem_attn_bwd(dmo, heads, qm, kv, dqm_store, dkv_ref):
    for h, (p, mo) in enumerate(heads):
        sl = slice(h * HD, (h + 1) * HD)
        vs = slice(MW + h * HD, MW + (h + 1) * HD)
        dmo_h = dmo[:, sl]
        dmo_b = dmo_h.astype(BF16)
        dp = _dot_nt(dmo_b, kv[:, vs])
        delta = jnp.sum(dmo_h * mo, axis=-1, keepdims=True)
        ds = (p * (dp - delta) * SCALE).astype(BF16)
        dqm_store(h, _dot(ds, kv[:, sl]))
        dkv_ref[:, sl] += _dot_tn(ds, qm[:, sl])
        dkv_ref[:, vs] += _dot_tn(p.astype(BF16), dmo_b)


def _merge(o_refs, l_refs):
    ls = [r[...] for r in l_refs]
    m = jnp.maximum(jnp.maximum(ls[0], ls[1]), ls[2])
    es = [jnp.exp(l - m) for l in ls]
    inv = 1.0 / (es[0] + es[1] + es[2])
    ws = [e * inv for e in es]
    os_ = [r[...] for r in o_refs]
    mix = ws[0] * os_[0] + ws[1] * os_[1] + ws[2] * os_[2]
    return ws, mix


def _conv_taps(cg, u, cgp, up, first):
    a = cg * u
    ap = jnp.where(first, 0.0, cgp * up)
    row = lax.broadcasted_iota(jnp.int32, a.shape, 0)
    a1 = jnp.where(row == 0, ap[7:8, :], pltpu.roll(a, 1, 0))
    a2 = jnp.where(row == 0, ap[6:7, :], jnp.where(row == 1, ap[7:8, :], pltpu.roll(a, 2, 0)))
    return a, a1, a2


def _rope_tables(posf, after):
    half = 8
    invf = np.float32(ROPE_THETA) ** (-np.arange(half, dtype=np.float32) * np.float32(2.0 / 16))
    lane = np.arange(128)
    table = np.where((lane % HD) < 16, invf[lane % half], 0.0).astype(np.float32)[None, :]

    def body(pos_ref, invf_ref, c_ref, s1_ref, s2_ref):
        ang = pos_ref[...] * invf_ref[...]
        jm = lax.broadcasted_iota(jnp.int32, ang.shape, 1) & (HD - 1)
        cs = jnp.cos(ang)
        sn = jnp.sin(ang)
        c_ref[...] = jnp.where(jm < 16, cs, 1.0)
        s1_ref[...] = jnp.where(jm < 8, -sn, 0.0)
        s2_ref[...] = jnp.where((jm >= 8) & (jm < 16), sn, 0.0)

    out = _sds((S, 128), F32)
    return pl.pallas_call(
        functools.partial(_skip_arg, body, 2), name="rope_tables", grid=(NT,),
        in_specs=[_rows(1), _full((1, 128)), pl.BlockSpec(memory_space=pl.ANY)],
        out_specs=[_rows(128)] * 3, out_shape=[out] * 3,
        compiler_params=_params(("parallel",)),
    )(posf, jnp.asarray(table), after)


def _in_proj_a(x, g0, w_in, c, s1, s2, after):
    def body(x_ref, g_ref, w_ref, c_ref, s1_ref, s2_ref, hn_ref, q_ref, k_ref, v_ref, qm_ref, z_ref, proj):
        xf = x_ref[...]
        hn = xf * lax.rsqrt(jnp.mean(xf * xf, axis=-1, keepdims=True) + EPS) * g_ref[...]
        hb = hn.astype(BF16)
        hn_ref[...] = hb
        for s in range(4):
            proj[:, s * SH_A:(s + 1) * SH_A] = _dot(hb, w_ref[s])
        cc, a1, a2 = c_ref[...], s1_ref[...], s2_ref[...]
        for j in range(NQ // 128):
            q_ref[:, j * 128:(j + 1) * 128] = (
                _rope_fwd(proj[:, j * 128:(j + 1) * 128], cc, a1, a2) * SCALE).astype(BF16)
            k_ref[:, j * 128:(j + 1) * 128] = _rope_fwd(
                proj[:, NQ + j * 128:NQ + (j + 1) * 128], cc, a1, a2).astype(BF16)
        v_ref[...] = proj[:, 2 * NQ:3 * NQ].astype(BF16)
        qm_ref[...] = proj[:, 3 * NQ:3 * NQ + MW].astype(BF16)
        z_ref[...] = proj[:, 3 * NQ + MW:]

    return pl.pallas_call(
        functools.partial(_skip_arg, body, 6), name="in_proj_a", grid=(NT,),
        in_specs=[_rows(D), _full((1, D)), _full((4, D, SH_A)), _rows(128), _rows(128), _rows(128),
                  pl.BlockSpec(memory_space=pl.ANY)],
        out_specs=[_rows(D), _rows(NQ), _rows(NQ), _rows(NQ), _rows(MW), _rows(BR_A)],
        out_shape=[_sds((S, D), BF16), _sds((S, NQ), BF16), _sds((S, NQ), BF16), _sds((S, NQ), BF16),
                   _sds((S, MW), BF16), _sds((S, BR_A), F32)],
        scratch_shapes=[pltpu.VMEM((TM, IN_A), F32)],
        compiler_params=_params(("parallel",)),
    )(x, g0, w_in, c, s1, s2, after)


def _mem_fwd(mem, mg, wkv):
    def body(mem_ref, mg_ref, w_ref, memn_ref, kv_ref):
        mf = mem_ref[...]
        n = mf * lax.rsqrt(jnp.mean(mf * mf, axis=-1, keepdims=True) + EPS)
        for i in range(2):
            mn = (n * mg_ref[i:i + 1, :]).astype(BF16)
            memn_ref[i] = mn
            acc = _dot(mn[:, 0:NM], w_ref[0, i])
            for s in range(1, 4):
                acc += _dot(mn[:, s * NM:(s + 1) * NM], w_ref[s, i])
            kv_ref[i] = acc.astype(BF16)

    return pl.pallas_call(
        body, name="mem_fwd", grid=(1,),
        in_specs=[_full((NM, D)), _full((2, D)), _full((4, 2, NM, 2 * MW))],
        out_specs=[_full((2, NM, D)), _full((2, NM, 2 * MW))],
        out_shape=[_sds((2, NM, D), BF16), _sds((2, NM, 2 * MW), BF16)],
        compiler_params=_params(("arbitrary",)),
    )(mem, mg, wkv)


def _band_mask(j):
    qi = lax.broadcasted_iota(jnp.int32, (QBLK, 2 * QBLK), 0)
    kj = lax.broadcasted_iota(jnp.int32, (QBLK, 2 * QBLK), 1)
    dist = qi + QBLK - kj
    return (dist >= 0) & (dist <= QBLK) & ((kj >= QBLK) | (j > 0))


LANES = 128
NCHUNK = GW // LANES
FWD_UNROLL = 8
BWD_UNROLL = 4


def _perm_matrix(d):
    n = TM // d
    p = np.zeros((TM, TM), np.float32)
    for r in range(d):
        for i in range(n):
            p[r * n + i, i * d + r] = 1.0
    return p


def _split_dot(p, x, parts):
    hi = x.astype(BF16)
    rem = x - hi.astype(F32)
    lo = rem.astype(BF16)
    both = _dot(p, jnp.concatenate([hi, lo], axis=1))
    acc = both[:, :LANES] + both[:, LANES:]
    if parts == 3:
        acc = acc + _dot(p, (rem - lo.astype(F32)).astype(BF16))
    return acc


def _pair_dot(p, a, b):
    both = _dot(p, jnp.concatenate([a, b], axis=1))
    return both[:, :LANES], both[:, LANES:]


def _tile_to_streams(y, dst, t, d):
    n, ln = TM // d, S // d
    for r in range(d):
        dst[r * ln + t * n:r * ln + (t + 1) * n, :] = y[r * n:(r + 1) * n].astype(dst.dtype)


def _tile_from_streams(src, t, d):
    n, ln = TM // d, S // d
    return jnp.concatenate([src[r * ln + t * n:r * ln + (t + 1) * n, :] for r in range(d)], axis=0)


def _head_masks():
    first = lax.broadcasted_iota(jnp.int32, (TM, LANES), 1) < HD
    return first, jnp.logical_not(first)


def _attn_fwd(q, k, v, g, after):
    d = DILATIONS[g]
    nb = S // d // QBLK
    perm = _perm_matrix(d)

    def body(q_ref, k_ref, v_ref, p_ref, pt_ref, o_ref, l_ref, ls_ref, q0, q1, ks, vs, os_):
        first, second = _head_masks()
        pm = p_ref[...]
        for t in range(NT):
            rows = slice(t * TM, (t + 1) * TM)
            if d == 1:
                qt = q_ref[rows, :].astype(F32)
            else:
                qt, kt = _pair_dot(pm, q_ref[rows, :], k_ref[rows, :])
                _tile_to_streams(kt, ks, t, d)
                _tile_to_streams(_dot(pm, v_ref[rows, :]), vs, t, d)
            _tile_to_streams(jnp.where(first, qt, 0.0), q0, t, d)
            _tile_to_streams(jnp.where(second, qt, 0.0), q1, t, d)
        kref, vref = (k_ref, v_ref) if d == 1 else (ks, vs)
        oref, lref = (o_ref, l_ref) if d == 1 else (os_, ls_ref)

        def blk(b, carry):
            r0 = pl.multiple_of(b * QBLK, QBLK)
            p0 = pl.multiple_of(jnp.maximum(b - 1, 0) * QBLK, QBLK)
            kk = jnp.concatenate([kref[pl.ds(p0, QBLK), :], kref[pl.ds(r0, QBLK), :]], axis=0)
            vv = jnp.concatenate([vref[pl.ds(p0, QBLK), :], vref[pl.ds(r0, QBLK), :]], axis=0)
            valid = _band_mask(b & (nb - 1))
            qq = jnp.concatenate([q0[pl.ds(r0, QBLK), :], q1[pl.ds(r0, QBLK), :]], axis=0)
            s = jnp.where(jnp.concatenate([valid, valid], axis=0), _dot_nt(qq, kk), NEG)
            m = jnp.max(s, axis=-1, keepdims=True)
            e = jnp.exp(s - m)
            l = jnp.sum(e, axis=-1, keepdims=True)
            acc = _dot(e.astype(BF16), vv)
            lse = m + jnp.log(l)
            f = first[:QBLK]
            oref[pl.ds(r0, QBLK), :] = jnp.where(f, acc[:QBLK], acc[QBLK:]) / jnp.where(f, l[:QBLK], l[QBLK:])
            lref[pl.ds(r0, QBLK), :] = jnp.where(f, lse[:QBLK], lse[QBLK:])
            return carry

        lax.fori_loop(0, S // QBLK, blk, 0, unroll=FWD_UNROLL)
        if d > 1:
            ptm = pt_ref[...]
            for t in range(NT):
                rows = slice(t * TM, (t + 1) * TM)
                o_ref[rows, :] = _split_dot(ptm, _tile_from_streams(os_, t, d), 2)
                l_ref[rows, :] = _split_dot(ptm, _tile_from_streams(ls_ref, t, d), 3)

    qkv_spec = pl.BlockSpec((S, LANES), lambda c: (0, g * NCHUNK + c))
    out_spec = pl.BlockSpec((S, LANES), lambda c: (0, c))
    n_out = 2 if d == 1 else 3
    inner = body if d > 1 else functools.partial(_drop_arg, body, 7)
    outs = pl.pallas_call(
        functools.partial(_skip_arg, inner, 5), name=f"attn_fwd_g{g}", grid=(NCHUNK,),
        in_specs=[qkv_spec] * 3 + [_full((TM, TM))] * 2 + [pl.BlockSpec(memory_space=pl.ANY)],
        out_specs=[out_spec] * n_out, out_shape=[_sds((S, GW), F32)] * n_out,
        scratch_shapes=[pltpu.VMEM((S, LANES), BF16)] * 4 + [pltpu.VMEM((S, LANES), F32)],
        compiler_params=_params(("parallel",)),
    )(q, k, v, jnp.asarray(perm, BF16), jnp.asarray(perm.T, BF16), after)
    return (outs[0], outs[1], outs[1]) if d == 1 else tuple(outs)


def _drop_arg(body, pos, *refs):
    return body(*refs[:pos], None, *refs[pos:])


def _attn_out(os_, ls, qm, kv0, z, x, w_out):
    def body(o0, o1, o2, l0, l1, l2, qm_ref, kv_ref, z_ref, x_ref, w_ref, h_ref, ybuf):
        _, mix = _merge((o0, o1, o2), (l0, l1, l2))
        sz, _ = _silu_parts(z_ref[...])
        ybuf[:, :GW] = (mix * sz[:, :GW]).astype(BF16)
        for h, (_, mo) in enumerate(_mem_attn(qm_ref[...], kv_ref[...])):
            sl = slice(GW + h * HD, GW + (h + 1) * HD)
            ybuf[:, sl] = (mo * sz[:, sl]).astype(BF16)
        yb = ybuf[...]
        for s in range(4):
            cs = slice(s * SH_O, (s + 1) * SH_O)
            h_ref[:, cs] = x_ref[:, cs] + _dot(yb, w_ref[s])

    return pl.pallas_call(
        body, name="attn_out", grid=(NT,),
        in_specs=[_rows(GW)] * 6 + [_rows(MW), _full((NM, 2 * MW)), _rows(BR_A), _rows(D), _full((4, BR_A, SH_O))],
        out_specs=_rows(D), out_shape=_sds((S, D), F32),
        scratch_shapes=[pltpu.VMEM((TM, BR_A), BF16)],
        compiler_params=_params(("parallel",)),
    )(*os_, *ls, qm, kv0, z, x, w_out)


def _in_proj_b(h1, g1, w_in):
    def body(x_ref, g_ref, w_ref, hn_ref, bg_ref, cg_ref, u_ref, qm_ref, z_ref, proj):
        xf = x_ref[...]
        hn = xf * lax.rsqrt(jnp.mean(xf * xf, axis=-1, keepdims=True) + EPS) * g_ref[...]
        hb = hn.astype(BF16)
        hn_ref[...] = hb
        for s in range(4):
            proj[:, s * SH_B:(s + 1) * SH_B] = _dot(hb, w_ref[s])
        bg_ref[...] = proj[:, :D]
        cg_ref[...] = proj[:, D:2 * D]
        u_ref[...] = proj[:, 2 * D:3 * D]
        qm_ref[...] = proj[:, 3 * D:3 * D + MW].astype(BF16)
        z_ref[...] = proj[:, 3 * D + MW:]

    return pl.pallas_call(
        body, name="in_proj_b", grid=(NT,),
        in_specs=[_rows(D), _full((1, D)), _full((4, D, SH_B))],
        out_specs=[_rows(D), _rows(D), _rows(D), _rows(D), _rows(MW), _rows(BR_B)],
        out_shape=[_sds((S, D), BF16), _sds((S, D), F32), _sds((S, D), F32), _sds((S, D), F32),
                   _sds((S, MW), BF16), _sds((S, BR_B), F32)],
        scratch_shapes=[pltpu.VMEM((TM, IN_B), F32)],
        compiler_params=_params(("parallel",)),
    )(h1, g1, w_in)


def _prev8(width):
    return pl.BlockSpec((8, width), lambda i: (jnp.maximum(i * (TM // 8) - 1, 0), 0))


def _conv_out_loss(bg, cg, u, cw, qm, kv1, z, h1, w_out, fg, tgt):
    def body(bg_ref, cg_ref, u_ref, cgp_ref, up_ref, cw_ref, qm_ref, kv_ref, z_ref, h_ref, w_ref, fg_ref, t_ref,
             dh_ref, loss_ref, dfg_ref, ybuf):
        i = pl.program_id(0)
        a, a1, a2 = _conv_taps(cg_ref[...], u_ref[...], cgp_ref[...], up_ref[...], i == 0)
        conv = cw_ref[0:1, :] * a2 + cw_ref[1:2, :] * a1 + cw_ref[2:3, :] * a
        sz, _ = _silu_parts(z_ref[...])
        ybuf[:, :D] = (bg_ref[...] * conv * sz[:, :D]).astype(BF16)
        for h, (_, mo) in enumerate(_mem_attn(qm_ref[...], kv_ref[...])):
            sl = slice(D + h * HD, D + (h + 1) * HD)
            ybuf[:, sl] = (mo * sz[:, sl]).astype(BF16)
        h2 = h_ref[...] + _dot(ybuf[...], w_ref[...])
        rstd = lax.rsqrt(jnp.mean(h2 * h2, axis=-1, keepdims=True) + EPS)
        n = h2 * rstd
        fgv = fg_ref[...]
        err = n * fgv - t_ref[...]
        dout = err * (1.0 / D)
        dn = dout * fgv
        dh_ref[...] = rstd * (dn - n * jnp.mean(dn * n, axis=-1, keepdims=True))

        @pl.when(i == 0)
        def _():
            loss_ref[...] = jnp.zeros_like(loss_ref)
            dfg_ref[...] = jnp.zeros_like(dfg_ref)

        loss_ref[...] += jnp.sum(err * err) * (0.5 / D)
        dfg_ref[...] += jnp.sum(dout * n, axis=0, keepdims=True)

    return pl.pallas_call(
        body, name="conv_out_loss", grid=(NT,),
        in_specs=[_rows(D), _rows(D), _rows(D), _prev8(D), _prev8(D), _full((8, D)), _rows(MW),
                  _full((NM, 2 * MW)), _rows(BR_B), _rows(D), _full((BR_B, D)), _full((1, D)), _rows(D)],
        out_specs=[_rows(D), _full((1, 128)), _full((1, D))],
        out_shape=[_sds((S, D), F32), _sds((1, 128), F32), _sds((1, D), F32)],
        scratch_shapes=[pltpu.VMEM((TM, BR_B), BF16)],
        compiler_params=_params(("arbitrary",)),
    )(bg, cg, u, cg, u, cw, qm, kv1, z, h1, w_out, fg, tgt)


def _conv_bwd(dh2, bg, cg, u, cw, qm, kv1, z, w_out):
    rev = lambda i: (NT - 1 - i, 0)
    rows = lambda w: pl.BlockSpec((TM, w), rev)
    prev8 = pl.BlockSpec((8, D), lambda i: (jnp.maximum((NT - 1 - i) * (TM // 8) - 1, 0), 0))

    def body(dh_ref, bg_ref, cg_ref, u_ref, cgp_ref, up_ref, cw_ref, qm_ref, kv_ref, z_ref, w_ref,
             dproj_ref, dw_ref, dcw_ref, dkv_ref, ybuf, carry):
        i = pl.program_id(0)

        @pl.when(i == 0)
        def _():
            dw_ref[...] = jnp.zeros_like(dw_ref)
            dcw_ref[...] = jnp.zeros_like(dcw_ref)
            dkv_ref[...] = jnp.zeros_like(dkv_ref)
            carry[...] = jnp.zeros_like(carry)

        bgv, cgv, uv = bg_ref[...], cg_ref[...], u_ref[...]
        a, a1, a2 = _conv_taps(cgv, uv, cgp_ref[...], up_ref[...], i == NT - 1)
        w0, w1, w2 = cw_ref[0:1, :], cw_ref[1:2, :], cw_ref[2:3, :]
        conv = w0 * a2 + w1 * a1 + w2 * a
        mix = bgv * conv
        zv = z_ref[...]
        sz, dsz = _silu_parts(zv)
        qmv, kvv = qm_ref[...], kv_ref[...]
        heads = _mem_attn(qmv, kvv)
        ybuf[:, :D] = (mix * sz[:, :D]).astype(BF16)
        for h, (_, mo) in enumerate(heads):
            sl = slice(D + h * HD, D + (h + 1) * HD)
            ybuf[:, sl] = (mo * sz[:, sl]).astype(BF16)
        dhb = dh_ref[...].astype(BF16)
        dw_ref[...] += _dot_tn(ybuf[...], dhb)
        dy = _dot_nt(dhb, w_ref[...])
        dcat = dy * sz
        dproj_ref[:, 3 * D + MW:3 * D + MW + D] = (dy[:, :D] * mix * dsz[:, :D]).astype(BF16)
        for h, (_, mo) in enumerate(heads):
            sl = slice(D + h * HD, D + (h + 1) * HD)
            dproj_ref[:, 3 * D + MW + D + h * HD:3 * D + MW + D + (h + 1) * HD] = (
                dy[:, sl] * mo * dsz[:, sl]).astype(BF16)
        dmix = dcat[:, :D]
        dproj_ref[:, :D] = (dmix * conv).astype(BF16)
        dc = dmix * bgv
        nxt = carry[...]
        row = lax.broadcasted_iota(jnp.int32, dc.shape, 0)
        dc1 = jnp.where(row == TM - 1, nxt[0:1, :], pltpu.roll(dc, TM - 1, 0))
        dc2 = jnp.where(row == TM - 2, nxt[0:1, :], jnp.where(row == TM - 1, nxt[1:2, :], pltpu.roll(dc, TM - 2, 0)))
        carry[...] = dc[0:8, :]
        da = w2 * dc + w1 * dc1 + w0 * dc2
        dproj_ref[:, D:2 * D] = (da * uv).astype(BF16)
        dproj_ref[:, 2 * D:3 * D] = (da * cgv).astype(BF16)
        dcw_ref[0:1, :] += jnp.sum(dc * a2, axis=0, keepdims=True)
        dcw_ref[1:2, :] += jnp.sum(dc * a1, axis=0, keepdims=True)
        dcw_ref[2:3, :] += jnp.sum(dc * a, axis=0, keepdims=True)

        def dqm_store(h, val):
            dproj_ref[:, 3 * D + h * HD:3 * D + (h + 1) * HD] = val.astype(BF16)

        _mem_attn_bwd(dcat[:, D:], heads, qmv, kvv, dqm_store, dkv_ref)

    return pl.pallas_call(
        body, name="conv_bwd", grid=(NT,),
        in_specs=[rows(D), rows(D), rows(D), rows(D), prev8, prev8, _full((8, D)), rows(MW),
                  _full((NM, 2 * MW)), rows(BR_B), _full((BR_B, D))],
        out_specs=[rows(IN_B), _full((BR_B, D)), _full((8, D)), _full((NM, 2 * MW))],
        out_shape=[_sds((S, IN_B), BF16), _sds((BR_B, D), F32), _sds((8, D), F32), _sds((NM, 2 * MW), F32)],
        scratch_shapes=[pltpu.VMEM((TM, BR_B), BF16), pltpu.VMEM((8, D), F32)],
        compiler_params=_params(("arbitrary",)),
    )(dh2, bg, cg, u, cg, u, cw, qm, kv1, z, w_out)


def _in_proj_bwd(dproj, w_in, xin, g, dres, after, width, name):
    sh = width // 4

    def body(dp_ref, w_ref, x_ref, g_ref, dr_ref, dx_ref, dg_ref):
        i = pl.program_id(0)
        dhn = _dot_nt(dp_ref[:, 0:sh], w_ref[0])
        for s in range(1, 4):
            dhn += _dot_nt(dp_ref[:, s * sh:(s + 1) * sh], w_ref[s])
        xf = x_ref[...]
        rstd = lax.rsqrt(jnp.mean(xf * xf, axis=-1, keepdims=True) + EPS)
        n = xf * rstd
        dn = dhn * g_ref[...]
        dx_ref[...] = dr_ref[...] + rstd * (dn - n * jnp.mean(dn * n, axis=-1, keepdims=True))

        @pl.when(i == 0)
        def _():
            dg_ref[...] = jnp.zeros_like(dg_ref)

        dg_ref[...] += jnp.sum(dhn * n, axis=0, keepdims=True)

    return pl.pallas_call(
        functools.partial(_skip_arg, body, 5), name=name, grid=(NT,),
        in_specs=[_rows(width), _full((4, D, sh)), _rows(D), _full((1, D)), _rows(D), pl.BlockSpec(memory_space=pl.ANY)],
        out_specs=[_rows(D), _full((1, D))],
        out_shape=[_sds((S, D), F32), _sds((1, D), F32)],
        compiler_params=_params(("arbitrary",)),
    )(dproj, w_in, xin, g, dres, after)


def _w_in_grad(hn, dproj, width, name):
    sh = width // 4

    def body(hn_ref, dp_ref, dw_ref):
        dw_ref[0] = _dot_tn(hn_ref[...], dp_ref[...])

    return pl.pallas_call(
        body, name=name, grid=(4,),
        in_specs=[_full((S, D)), pl.BlockSpec((S, sh), lambda s: (0, s))],
        out_specs=pl.BlockSpec((1, D, sh), lambda s: (s, 0, 0)),
        out_shape=_sds((4, D, sh), F32),
        compiler_params=_params(("parallel",)),
    )(hn, dproj)


def _attn_out_bwd(dh1, os_, ls, qm, kv0, z, w_out, after):
    ones_bd = np.kron(np.eye(GW // HD, dtype=np.float32), np.ones((HD, HD), np.float32))

    def body(dh_ref, o0, o1, o2, l0, l1, l2, qm_ref, kv_ref, z_ref, w_ref, bd_ref,
             do0, do1, do2, dd0, dd1, dd2, dqm_ref, dz_ref, dw_ref, dkv_ref, ybuf):
        i = pl.program_id(0)

        @pl.when(i == 0)
        def _():
            dw_ref[...] = jnp.zeros_like(dw_ref)
            dkv_ref[...] = jnp.zeros_like(dkv_ref)

        ws, mix = _merge((o0, o1, o2), (l0, l1, l2))
        sz, dsz = _silu_parts(z_ref[...])
        qmv, kvv = qm_ref[...], kv_ref[...]
        heads = _mem_attn(qmv, kvv)
        ybuf[:, :GW] = (mix * sz[:, :GW]).astype(BF16)
        for h, (_, mo) in enumerate(heads):
            sl = slice(GW + h * HD, GW + (h + 1) * HD)
            ybuf[:, sl] = (mo * sz[:, sl]).astype(BF16)
        yb = ybuf[...]
        dh = dh_ref[...]
        dy = None
        for s in range(4):
            dhb = dh[:, s * SH_O:(s + 1) * SH_O].astype(BF16)
            dw_ref[s] += _dot_tn(yb, dhb)
            part = _dot_nt(dhb, w_ref[s])
            dy = part if dy is None else dy + part
        dcat = dy * sz
        dz_ref[:, :GW] = (dy[:, :GW] * mix * dsz[:, :GW]).astype(BF16)
        for h, (_, mo) in enumerate(heads):
            sl = slice(GW + h * HD, GW + (h + 1) * HD)
            dz_ref[:, sl] = (dy[:, sl] * mo * dsz[:, sl]).astype(BF16)
        dmix = dcat[:, :GW]
        prod = dmix * mix
        hi = prod.astype(BF16)
        lo = (prod - hi.astype(F32)).astype(BF16)
        bd = bd_ref[...]
        tot = _dot(hi, bd) + _dot(lo, bd)
        for w, do_ref, dd_ref in zip(ws, (do0, do1, do2), (dd0, dd1, dd2)):
            do_ref[...] = (w * dmix).astype(BF16)
            dd_ref[...] = w * tot

        def dqm_store(h, val):
            dqm_ref[:, h * HD:(h + 1) * HD] = val.astype(BF16)

        _mem_attn_bwd(dcat[:, GW:], heads, qmv, kvv, dqm_store, dkv_ref)

    return pl.pallas_call(
        functools.partial(_skip_arg, body, 12), name="attn_out_bwd", grid=(NT,),
        in_specs=[_rows(D)] + [_rows(GW)] * 6 + [_rows(MW), _full((NM, 2 * MW)), _rows(BR_A),
                                                   _full((4, BR_A, SH_O)), _full((GW, GW)),
                                                   pl.BlockSpec(memory_space=pl.ANY)],
        out_specs=[_rows(GW)] * 6 + [_rows(MW), _rows(BR_A), _full((4, BR_A, SH_O)), _full((NM, 2 * MW))],
        out_shape=[_sds((S, GW), BF16)] * 3 + [_sds((S, GW), F32)] * 3 + [
            _sds((S, MW), BF16), _sds((S, BR_A), BF16), _sds((4, BR_A, SH_O), F32), _sds((NM, 2 * MW), F32)],
        scratch_shapes=[pltpu.VMEM((TM, BR_A), BF16)],
        compiler_params=_params(("arbitrary",)),
    )(dh1, *os_, *ls, qm, kv0, z, w_out, jnp.asarray(ones_bd, dtype=BF16), after)


def _attn_bwd(q, k, v, do, lse_s, dd, g):
    d = DILATIONS[g]
    nb = S // d // QBLK
    perm = _perm_matrix(d)

    def body(q_ref, k_ref, v_ref, do_ref, l_ref, dd_ref, p_ref, pt_ref, dq_ref, dk_ref, dv_ref,
             q0, q1, g0, g1, ks, vs, dds, dqs, dks, dvs):
        first, second = _head_masks()
        pm = p_ref[...]
        for t in range(NT):
            rows = slice(t * TM, (t + 1) * TM)
            if d == 1:
                qt = q_ref[rows, :].astype(F32)
                gt = do_ref[rows, :].astype(F32)
            else:
                qt, gt = _pair_dot(pm, q_ref[rows, :], do_ref[rows, :])
                kt, vt = _pair_dot(pm, k_ref[rows, :], v_ref[rows, :])
                _tile_to_streams(kt, ks, t, d)
                _tile_to_streams(vt, vs, t, d)
                _tile_to_streams(_split_dot(pm, dd_ref[rows, :], 2), dds, t, d)
            _tile_to_streams(jnp.where(first, qt, 0.0), q0, t, d)
            _tile_to_streams(jnp.where(second, qt, 0.0), q1, t, d)
            _tile_to_streams(jnp.where(first, gt, 0.0), g0, t, d)
            _tile_to_streams(jnp.where(second, gt, 0.0), g1, t, d)
        kref, vref, ddref = (k_ref, v_ref, dd_ref) if d == 1 else (ks, vs, dds)
        dqref, dkref, dvref = (dq_ref, dk_ref, dv_ref) if d == 1 else (dqs, dks, dvs)
        dkref[...] = jnp.zeros_like(dkref)
        dvref[...] = jnp.zeros_like(dvref)

        def blk(b, carry):
            r0 = pl.multiple_of(b * QBLK, QBLK)
            p0 = pl.multiple_of(jnp.maximum(b - 1, 0) * QBLK, QBLK)
            kk = jnp.concatenate([kref[pl.ds(p0, QBLK), :], kref[pl.ds(r0, QBLK), :]], axis=0)
            vv = jnp.concatenate([vref[pl.ds(p0, QBLK), :], vref[pl.ds(r0, QBLK), :]], axis=0)
            lb = l_ref[pl.ds(r0, QBLK), :]
            ddb = ddref[pl.ds(r0, QBLK), :]
            lcol = jnp.concatenate([lb[:, 0:1], lb[:, HD:HD + 1]], axis=0)
            dcol = jnp.concatenate([ddb[:, 0:1], ddb[:, HD:HD + 1]], axis=0)
            valid = _band_mask(b & (nb - 1))
            valid2 = jnp.concatenate([valid, valid], axis=0)
            qq = jnp.concatenate([q0[pl.ds(r0, QBLK), :], q1[pl.ds(r0, QBLK), :]], axis=0)
            gg = jnp.concatenate([g0[pl.ds(r0, QBLK), :], g1[pl.ds(r0, QBLK), :]], axis=0)
            p = jnp.where(valid2, jnp.exp(_dot_nt(qq, kk) - lcol), 0.0)
            ds = (p * (_dot_nt(gg, vv) - dcol)).astype(BF16)
            dq2 = _dot(ds, kk)
            dqref[pl.ds(r0, QBLK), :] = jnp.where(first[:QBLK], dq2[:QBLK], dq2[QBLK:])
            dkk = _dot_tn(ds, qq)
            dvv = _dot_tn(p.astype(BF16), gg)
            dkref[pl.ds(p0, QBLK), :] += dkk[:QBLK]
            dkref[pl.ds(r0, QBLK), :] += dkk[QBLK:]
            dvref[pl.ds(p0, QBLK), :] += dvv[:QBLK]
            dvref[pl.ds(r0, QBLK), :] += dvv[QBLK:]
            return carry

        lax.fori_loop(0, S // QBLK, blk, 0, unroll=BWD_UNROLL)

        if d > 1:
            ptm = pt_ref[...]
            for t in range(NT):
                rows = slice(t * TM, (t + 1) * TM)
                dq_ref[rows, :] = _split_dot(ptm, _tile_from_streams(dqs, t, d), 2)
                dk_ref[rows, :] = _split_dot(ptm, _tile_from_streams(dks, t, d), 2)
                dv_ref[rows, :] = _split_dot(ptm, _tile_from_streams(dvs, t, d), 2)

    qkv_spec = pl.BlockSpec((S, LANES), lambda c: (0, g * NCHUNK + c))
    one_spec = pl.BlockSpec((S, LANES), lambda c: (0, c))
    return pl.pallas_call(
        body, name=f"attn_bwd_g{g}", grid=(NCHUNK,),
        in_specs=[qkv_spec] * 3 + [one_spec] * 3 + [_full((TM, TM))] * 2, out_specs=[one_spec] * 3,
        out_shape=[_sds((S, GW), F32)] * 3,
        scratch_shapes=[pltpu.VMEM((S, LANES), BF16)] * 6 + [pltpu.VMEM((S, LANES), F32)] * 4,
        compiler_params=_params(("parallel",)),
    )(q, k, v, do, lse_s, dd, jnp.asarray(perm, BF16), jnp.asarray(perm.T, BF16))


def _qkv_bwd(dqs, dks, dvs, dqm, dz, c, s1, s2):
    def body(q0, q1, q2, k0, k1, k2, v0, v1, v2, dqm_ref, dz_ref, c_ref, s1_ref, s2_ref, dp_ref):
        cc, a1, a2 = c_ref[...], s1_ref[...], s2_ref[...]
        for g, (qr, kr, vr) in enumerate(((q0, k0, v0), (q1, k1, v1), (q2, k2, v2))):
            for j in range(GW // 128):
                ls_ = slice(j * 128, (j + 1) * 128)
                c0 = g * GW + j * 128
                dp_ref[:, c0:c0 + 128] = (_rope_bwd(qr[:, ls_], cc, a1, a2) * SCALE).astype(BF16)
                dp_ref[:, NQ + c0:NQ + c0 + 128] = _rope_bwd(kr[:, ls_], cc, a1, a2).astype(BF16)
            dp_ref[:, 2 * NQ + g * GW:2 * NQ + (g + 1) * GW] = vr[...].astype(BF16)
        dp_ref[:, 3 * NQ:3 * NQ + MW] = dqm_ref[...]
        dp_ref[:, 3 * NQ + MW:] = dz_ref[...]

    return pl.pallas_call(
        body, name="qkv_bwd", grid=(NT,),
        in_specs=[_rows(GW)] * 9 + [_rows(MW), _rows(BR_A), _rows(128), _rows(128), _rows(128)],
        out_specs=_rows(IN_A), out_shape=_sds((S, IN_A), BF16),
        compiler_params=_params(("parallel",)),
    )(*dqs, *dks, *dvs, dqm, dz, c, s1, s2)


def _mem_bwd(mem, mg, memn, wkv, dkv0, dkv1):
    def body(mem_ref, mg_ref, memn_ref, w_ref, d0_ref, d1_ref, dw_ref, dg_ref):
        mf = mem_ref[...]
        n = mf * lax.rsqrt(jnp.mean(mf * mf, axis=-1, keepdims=True) + EPS)
        for i, d_ref in enumerate((d0_ref, d1_ref)):
            dkv = d_ref[...].astype(BF16)
            mn = memn_ref[i]
            for s in range(4):
                cs = slice(s * NM, (s + 1) * NM)
                dw_ref[s, i] = _dot_tn(mn[:, cs], dkv)
                dmn = _dot_nt(dkv, w_ref[s, i])
                dg_ref[i:i + 1, cs] = jnp.sum(dmn * n[:, cs], axis=0, keepdims=True)

    return pl.pallas_call(
        body, name="mem_bwd", grid=(1,),
        in_specs=[_full((NM, D)), _full((2, D)), _full((2, NM, D)), _full((4, 2, NM, 2 * MW)),
                  _full((NM, 2 * MW)), _full((NM, 2 * MW))],
        out_specs=[_full((4, 2, NM, 2 * MW)), _full((2, D))],
        out_shape=[_sds((4, 2, NM, 2 * MW), F32), _sds((2, D), F32)],
        compiler_params=_params(("arbitrary",)),
    )(mem, mg, memn, wkv, dkv0, dkv1)


MESH = pl.DeviceIdType.MESH
ANY = pl.BlockSpec(memory_space=pl.ANY)
BIG = (("wkv", 2, NM, 2 * MW), ("w_in_a", 1, D, SH_A), ("w_out_a", 1, BR_A, SH_O),
       ("w_in_b", 1, D, SH_B), ("w_out_b", 1, BR_B // 4, D))
NBIG = len(BIG)
CW_ROWS = 8


def _place():
    x, y, c = lax.axis_index("x"), lax.axis_index("y"), lax.axis_index("c")
    chips = ((1 - x, y), (x, 1 - y), (1 - x, 1 - y))
    return x, y, c, chips


def _remote(src, dst, ssem, rsem, dev):
    return pltpu.make_async_remote_copy(src_ref=src, dst_ref=dst, send_sem=ssem, recv_sem=rsem,
                                        device_id=dev, device_id_type=MESH)


def _cast_weights(place, ws, after, idx, name):
    nblk = 4
    n = len(idx)
    dims = [BIG[w][1:] for w in idx]

    def body(pref, *refs):
        for i in range(n):
            refs[n + 1 + i][0] = refs[i][...].astype(BF16)

    grid_spec = pltpu.PrefetchScalarGridSpec(
        num_scalar_prefetch=1, grid=(nblk,),
        in_specs=[pl.BlockSpec((k, r // nblk, cdim), lambda i, pref: (0, i, 0)) for k, r, cdim in dims]
        + [pl.BlockSpec(memory_space=pl.ANY)],
        out_specs=[pl.BlockSpec((1, k, r // nblk, cdim), lambda i, pref: (pref[1], 0, i, 0)) for k, r, cdim in dims])
    return pl.pallas_call(
        body, name=name, grid_spec=grid_spec,
        out_shape=[_sds((4, k, r, cdim), BF16) for k, r, cdim in dims],
        compiler_params=_params(("parallel",)),
    )(place, *ws, after)


LAYER_A = (0, 1, 2)
LAYER_B = (3, 4)
HBM = pl.BlockSpec(memory_space=pltpu.HBM)
SEM = pl.BlockSpec(memory_space=pltpu.SEMAPHORE)
EFFECT = pltpu.SideEffectType.DATAFLOW_SIDE_EFFECTING
TOKEN = (8, 128)


def _half(ref, w, which):
    h = BIG[w][2] // 2
    return ref.at[:, pl.ds(which * h, h), :]


def _skip_arg(body, pos, *refs):
    return body(*refs[:pos], *refs[pos + 1:])


def _gather_weights(wb, cw, idx, name):
    n = len(idx)

    def body(*refs):
        src_cw = refs[n]
        dst = refs[n + 1:2 * n + 2]
        loc_sem, send_sems, recv_sems, fsend_sems, frecv_sems = refs[2 * n + 2:]
        x, y, c, chips = _place()
        me = 2 * x + y
        loc = pltpu.make_async_copy(src_cw, dst[n].at[me], loc_sem)
        loc.start()
        sends = []
        for j, (px, py) in enumerate(chips):
            for i in range(n):
                mine = _half(dst[i].at[me], idx[i], c)
                sends.append(_remote(mine, mine, send_sems.at[j, i], recv_sems.at[j, i], (px, py, c)))
            sends.append(_remote(src_cw, dst[n].at[me], send_sems.at[j, n], recv_sems.at[j, n], (px, py, c)))
        for cp in sends:
            cp.start()
        fwds = []
        for j, (px, py) in enumerate(chips):
            for i in range(n):
                got = _half(dst[i].at[2 * px + py], idx[i], c)
                _remote(got, got, send_sems.at[j, i], recv_sems.at[j, i], (px, py, c)).wait_recv()
                fwds.append(_remote(got, got, fsend_sems.at[j, i], frecv_sems.at[j, i], (x, y, 1 - c)))
                fwds[-1].start()
            got = dst[n].at[2 * px + py]
            _remote(got, got, send_sems.at[j, n], recv_sems.at[j, n], (px, py, c)).wait_recv()
        for j, (px, py) in enumerate(chips):
            for i in range(n):
                got = _half(dst[i].at[2 * px + py], idx[i], 1 - c)
                _remote(got, got, fsend_sems.at[j, i], frecv_sems.at[j, i], (x, y, 1 - c)).wait_recv()
        for cp in sends + fwds:
            cp.wait_send()
        loc.wait()

    out_shape = [_sds(w.shape, BF16) for w in wb] + [_sds((4, CW_ROWS, SH_O), F32)]
    return pl.pallas_call(
        body, name=name, in_specs=[ANY] * (n + 1), out_specs=[ANY] * (n + 1), out_shape=out_shape,
        input_output_aliases={i: i for i in range(n)},
        scratch_shapes=[pltpu.SemaphoreType.DMA, pltpu.SemaphoreType.DMA((3, n + 1)),
                        pltpu.SemaphoreType.DMA((3, n + 1)), pltpu.SemaphoreType.DMA((3, n)),
                        pltpu.SemaphoreType.DMA((3, n))],
    )(*wb, cw)


def _gather_start(wb, after, idx, name):
    n = len(idx)

    def body(*refs):
        src = refs[:n]
        send_sems, recv_sems = refs[n + 1], refs[n + 2]
        token = refs[2 * n + 3]
        x, y, c, chips = _place()
        me = 2 * x + y
        for j, (px, py) in enumerate(chips):
            for i in range(n):
                mine = _half(src[i].at[me], idx[i], c)
                _remote(mine, mine, send_sems.at[j * n + i], recv_sems.at[j * n + i], (px, py, c)).start()
        token[...] = jnp.zeros(TOKEN, F32)

    outs = pl.pallas_call(
        body, name=name, in_specs=[HBM] * n + [ANY],
        out_specs=(SEM, SEM) + (HBM,) * n + (pl.BlockSpec(memory_space=pltpu.VMEM),),
        out_shape=(pltpu.SemaphoreType.DMA((3 * n,)), pltpu.SemaphoreType.DMA((3 * n,)))
        + tuple(pltpu.HBM(w.shape, w.dtype) for w in wb) + (_sds(TOKEN, F32),),
        input_output_aliases={i: 2 + i for i in range(n)},
        compiler_params=pltpu.CompilerParams(has_side_effects=EFFECT),
    )(*[pltpu.with_memory_space_constraint(w, pltpu.HBM) for w in wb], after)
    return outs[0], outs[1], list(outs[2:2 + n]), outs[2 + n]


def _gather_wait(send_sems, recv_sems, wb, after, idx, name):
    n = len(idx)

    def body(*refs):
        buf = refs[:n]
        send_sems, recv_sems = refs[n], refs[n + 1]
        x, y, c, chips = _place()
        me = 2 * x + y
        for j, (px, py) in enumerate(chips):
            for i in range(n):
                mine = _half(buf[i].at[me], idx[i], c)
                got = _half(buf[i].at[2 * px + py], idx[i], c)
                _remote(mine, mine, send_sems.at[j * n + i], recv_sems.at[j * n + i], (px, py, c)).wait_send()
                _remote(got, got, send_sems.at[j * n + i], recv_sems.at[j * n + i], (px, py, c)).wait_recv()

    outs = pl.pallas_call(
        body, name=name, in_specs=[HBM] * n + [SEM, SEM] + [ANY] * len(after), out_specs=(HBM,) * n,
        out_shape=tuple(pltpu.HBM(w.shape, w.dtype) for w in wb),
        input_output_aliases={i: i for i in range(n)},
        compiler_params=pltpu.CompilerParams(has_side_effects=EFFECT),
    )(*wb, send_sems, recv_sems, *after)
    return list(outs)


def _gather_forward(wb, idx, name, cw=None):
    n = len(idx)
    m = n if cw is None else n + 1

    def body(*refs):
        dst = refs[m:2 * m]
        send_sems, recv_sems = refs[2 * m], refs[2 * m + 1]
        x, y, c, chips = _place()
        cps = []
        for j, (px, py) in enumerate(chips):
            for i in range(n):
                got = _half(dst[i].at[2 * px + py], idx[i], c)
                cps.append(_remote(got, got, send_sems.at[j, i], recv_sems.at[j, i], (x, y, 1 - c)))
                cps[-1].start()
        if cw is not None:
            src_cw, loc_sem = refs[n], refs[2 * m + 2]
            me = 2 * x + y
            loc = pltpu.make_async_copy(src_cw, dst[n].at[me], loc_sem)
            loc.start()
            for j, (px, py) in enumerate(chips):
                cps.append(_remote(src_cw, dst[n].at[me], send_sems.at[j, n], recv_sems.at[j, n], (px, py, c)))
                cps[-1].start()
        for j, (px, py) in enumerate(chips):
            for i in range(n):
                got = _half(dst[i].at[2 * px + py], idx[i], 1 - c)
                _remote(got, got, send_sems.at[j, i], recv_sems.at[j, i], (x, y, 1 - c)).wait_recv()
            if cw is not None:
                got = dst[n].at[2 * px + py]
                _remote(got, got, send_sems.at[j, n], recv_sems.at[j, n], (px, py, c)).wait_recv()
        for cp in cps:
            cp.wait_send()
        if cw is not None:
            loc.wait()

    out_shape = [_sds(w.shape, BF16) for w in wb]
    scratch = [pltpu.SemaphoreType.DMA((3, m)), pltpu.SemaphoreType.DMA((3, m))]
    args = list(wb)
    if cw is not None:
        out_shape.append(_sds((4, CW_ROWS, SH_O), F32))
        scratch.append(pltpu.SemaphoreType.DMA)
        args.append(cw)
    return pl.pallas_call(
        body, name=name, in_specs=[ANY] * m, out_specs=[ANY] * m, out_shape=out_shape,
        input_output_aliases={i: i for i in range(n)}, scratch_shapes=scratch,
    )(*args)


def _forward_start(wb, cw, after, idx, name):
    n = len(idx)
    m = n if cw is None else n + 2

    def body(*refs):
        buf = refs[:n]
        send_sems, recv_sems = refs[m + 1], refs[m + 2]
        token = refs[2 * m + 3]
        x, y, c, chips = _place()
        for j, (px, py) in enumerate(chips):
            for i in range(n):
                got = _half(buf[i].at[2 * px + py], idx[i], c)
                _remote(got, got, send_sems.at[j * (n + 1) + i], recv_sems.at[j * (n + 1) + i], (x, y, 1 - c)).start()
            if cw is not None:
                _remote(refs[n], refs[n + 1].at[2 * x + y], send_sems.at[j * (n + 1) + n],
                        recv_sems.at[j * (n + 1) + n], (px, py, c)).start()
        token[...] = jnp.zeros(TOKEN, F32)

    arrays = list(wb) if cw is None else list(wb) + [cw, lax.empty((4, CW_ROWS, SH_O), F32)]
    outs = pl.pallas_call(
        body, name=name, in_specs=[HBM] * m + [ANY],
        out_specs=(SEM, SEM) + (HBM,) * m + (pl.BlockSpec(memory_space=pltpu.VMEM),),
        out_shape=(pltpu.SemaphoreType.DMA((3 * (n + 1),)), pltpu.SemaphoreType.DMA((3 * (n + 1),)))
        + tuple(pltpu.HBM(a.shape, a.dtype) for a in arrays) + (_sds(TOKEN, F32),),
        input_output_aliases={i: 2 + i for i in range(m)},
        compiler_params=pltpu.CompilerParams(has_side_effects=EFFECT),
    )(*[pltpu.with_memory_space_constraint(a, pltpu.HBM) for a in arrays], after)
    return outs[0], outs[1], list(outs[2:2 + m]), outs[2 + m]


def _forward_wait(send_sems, recv_sems, arrays, after, idx, with_cw, name):
    n = len(idx)
    m = len(arrays)

    def body(*refs):
        buf = refs[:n]
        send_sems, recv_sems = refs[m], refs[m + 1]
        x, y, c, chips = _place()
        for j, (px, py) in enumerate(chips):
            for i in range(n):
                sent = _half(buf[i].at[2 * px + py], idx[i], c)
                got = _half(buf[i].at[2 * px + py], idx[i], 1 - c)
                k = j * (n + 1) + i
                _remote(sent, sent, send_sems.at[k], recv_sems.at[k], (x, y, 1 - c)).wait_send()
                _remote(got, got, send_sems.at[k], recv_sems.at[k], (x, y, 1 - c)).wait_recv()
            if with_cw:
                k = j * (n + 1) + n
                theirs = refs[n + 1].at[2 * px + py]
                _remote(refs[n], theirs, send_sems.at[k], recv_sems.at[k], (px, py, c)).wait_send()
                _remote(refs[n], theirs, send_sems.at[k], recv_sems.at[k], (px, py, c)).wait_recv()

    outs = pl.pallas_call(
        body, name=name, in_specs=[HBM] * m + [SEM, SEM] + [ANY] * len(after), out_specs=(HBM,) * m,
        out_shape=tuple(pltpu.HBM(a.shape, a.dtype) for a in arrays),
        input_output_aliases={i: i for i in range(m)},
        compiler_params=pltpu.CompilerParams(has_side_effects=EFFECT),
    )(*arrays, send_sems, recv_sems, *after)
    return list(outs)


def _pair_exchange(gs, idx, name):
    n = len(idx)

    def body(*refs):
        src, dst = refs[:n], refs[n:2 * n]
        send_sems, recv_sems = refs[2 * n:]
        x, y, c, _ = _place()
        cps = []
        for i in range(n):
            h = BIG[idx[i]][2] // 2
            cps.append(_remote(src[i].at[:, :, pl.ds((1 - c) * h, h), :], dst[i], send_sems.at[i], recv_sems.at[i],
                               (x, y, 1 - c)))
            cps[-1].start()
        for cp in cps:
            cp.wait()

    return pl.pallas_call(
        body, name=name, in_specs=[ANY] * n, out_specs=[ANY] * n,
        out_shape=[_sds((4, BIG[w][1], BIG[w][2] // 2, BIG[w][3]), F32) for w in idx],
        scratch_shapes=[pltpu.SemaphoreType.DMA((n,)), pltpu.SemaphoreType.DMA((n,))],
    )(*gs)


def _pair_start(gs, idx, name):
    n = len(idx)

    def body(*refs):
        src, land = refs[:n], refs[n:2 * n]
        send_sems, recv_sems = refs[2 * n], refs[2 * n + 1]
        token = refs[4 * n + 2]
        x, y, c, _ = _place()
        for i in range(n):
            h = BIG[idx[i]][2] // 2
            _remote(src[i].at[:, :, pl.ds((1 - c) * h, h), :], land[i], send_sems.at[i], recv_sems.at[i],
                    (x, y, 1 - c)).start()
        token[...] = jnp.zeros(TOKEN, F32)

    lands = [lax.empty((4, BIG[w][1], BIG[w][2] // 2, BIG[w][3]), F32) for w in idx]
    arrays = list(gs) + lands
    outs = pl.pallas_call(
        body, name=name, in_specs=[HBM] * (2 * n),
        out_specs=(SEM, SEM) + (HBM,) * (2 * n) + (pl.BlockSpec(memory_space=pltpu.VMEM),),
        out_shape=(pltpu.SemaphoreType.DMA((n,)), pltpu.SemaphoreType.DMA((n,)))
        + tuple(pltpu.HBM(a.shape, a.dtype) for a in arrays) + (_sds(TOKEN, F32),),
        input_output_aliases={i: 2 + i for i in range(2 * n)},
        compiler_params=pltpu.CompilerParams(has_side_effects=EFFECT),
    )(*[pltpu.with_memory_space_constraint(a, pltpu.HBM) for a in arrays])
    return outs[0], outs[1], list(outs[2:2 + n]), list(outs[2 + n:2 + 2 * n]), outs[2 + 2 * n]


def _pair_wait(send_sems, recv_sems, gs, lands, after, idx, name):
    n = len(idx)

    def body(*refs):
        src, land = refs[:n], refs[n:2 * n]
        send_sems, recv_sems = refs[2 * n], refs[2 * n + 1]
        x, y, c, _ = _place()
        for i in range(n):
            h = BIG[idx[i]][2] // 2
            cp = _remote(src[i].at[:, :, pl.ds((1 - c) * h, h), :], land[i], send_sems.at[i], recv_sems.at[i],
                         (x, y, 1 - c))
            cp.wait_send()
            cp.wait_recv()

    arrays = list(gs) + list(lands)
    outs = pl.pallas_call(
        body, name=name, in_specs=[HBM] * (2 * n) + [SEM, SEM] + [ANY] * len(after), out_specs=(HBM,) * (2 * n),
        out_shape=tuple(pltpu.HBM(a.shape, a.dtype) for a in arrays),
        input_output_aliases={i: i for i in range(2 * n)},
        compiler_params=pltpu.CompilerParams(has_side_effects=EFFECT),
    )(*arrays, send_sems, recv_sems, *after)
    return list(outs[:n]), list(outs[n:])


def _pair_sum(place, g, r1, i):
    _, k, r, cdim = BIG[i]
    h = r // 2

    def body(pref, g_ref, r_ref, o_ref):
        o_ref[...] = (g_ref[...] + r_ref[...]).astype(BF16)

    grid_spec = pltpu.PrefetchScalarGridSpec(
        num_scalar_prefetch=1, grid=(4, k),
        in_specs=[pl.BlockSpec((1, 1, h, cdim), lambda s, t, pref: (s, t, pref[0], 0)),
                  pl.BlockSpec((1, 1, h, cdim), lambda s, t, pref: (s, t, 0, 0))],
        out_specs=pl.BlockSpec((1, 1, h, cdim), lambda s, t, pref: (s, t, 0, 0)))
    return pl.pallas_call(
        body, name=f"pair_sum_{BIG[i][0]}", grid_spec=grid_spec, out_shape=_sds((4, k, h, cdim), BF16),
        compiler_params=_params(("parallel", "parallel")),
    )(place, g, r1)


def _pair_sums(place, gs, r1s, idx, name):
    n = len(idx)
    dims = [(BIG[w][1], BIG[w][2] // 2, BIG[w][3]) for w in idx]

    def body(pref, *refs):
        for i in range(n):
            refs[2 * n + i][...] = (refs[i][...] + refs[n + i][...]).astype(BF16)

    mine = [pl.BlockSpec((1, k, h, cdim), lambda s, pref: (s, 0, pref[0], 0)) for k, h, cdim in dims]
    whole = [pl.BlockSpec((1, k, h, cdim), lambda s, pref: (s, 0, 0, 0)) for k, h, cdim in dims]
    grid_spec = pltpu.PrefetchScalarGridSpec(num_scalar_prefetch=1, grid=(4,), in_specs=mine + whole, out_specs=whole)
    return pl.pallas_call(
        body, name=name, grid_spec=grid_spec, out_shape=[_sds((4, k, h, cdim), BF16) for k, h, cdim in dims],
        compiler_params=_params(("parallel",)),
    )(place, *gs, *r1s)


def _chip_start(ps, idx, name):
    n = len(idx)

    def body(*refs):
        src, land = refs[:n], refs[n:2 * n]
        send_sems, recv_sems = refs[2 * n], refs[2 * n + 1]
        token = refs[4 * n + 2]
        x, y, c, chips = _place()
        for j, (px, py) in enumerate(chips):
            for i in range(n):
                _remote(src[i].at[2 * px + py], land[i].at[j], send_sems.at[j * n + i], recv_sems.at[j * n + i],
                        (px, py, c)).start()
        token[...] = jnp.zeros(TOKEN, F32)

    lands = [lax.empty((3,) + p.shape[1:], BF16) for p in ps]
    outs = pl.pallas_call(
        body, name=name, in_specs=[HBM] * (2 * n),
        out_specs=(SEM, SEM) + (HBM,) * (2 * n) + (pl.BlockSpec(memory_space=pltpu.VMEM),),
        out_shape=(pltpu.SemaphoreType.DMA((3 * n,)), pltpu.SemaphoreType.DMA((3 * n,)))
        + tuple(pltpu.HBM(a.shape, a.dtype) for a in list(ps) + lands) + (_sds(TOKEN, F32),),
        input_output_aliases={i: 2 + i for i in range(2 * n)},
        compiler_params=pltpu.CompilerParams(has_side_effects=EFFECT),
    )(*[pltpu.with_memory_space_constraint(a, pltpu.HBM) for a in list(ps) + lands])
    return outs[0], outs[1], list(outs[2:2 + n]), list(outs[2 + n:2 + 2 * n]), outs[2 + 2 * n]


def _chip_wait(send_sems, recv_sems, ps, lands, after, idx, name):
    n = len(idx)

    def body(*refs):
        src, land = refs[:n], refs[n:2 * n]
        send_sems, recv_sems = refs[2 * n], refs[2 * n + 1]
        x, y, c, chips = _place()
        for j, (px, py) in enumerate(chips):
            for i in range(n):
                cp = _remote(src[i].at[2 * px + py], land[i].at[j], send_sems.at[j * n + i], recv_sems.at[j * n + i],
                             (px, py, c))
                cp.wait_send()
                cp.wait_recv()

    arrays = list(ps) + list(lands)
    outs = pl.pallas_call(
        body, name=name, in_specs=[HBM] * (2 * n) + [SEM, SEM] + [ANY] * len(after), out_specs=(HBM,) * (2 * n),
        out_shape=tuple(pltpu.HBM(a.shape, a.dtype) for a in arrays),
        input_output_aliases={i: i for i in range(2 * n)},
        compiler_params=pltpu.CompilerParams(has_side_effects=EFFECT),
    )(*arrays, send_sems, recv_sems, *after)
    return list(outs[n:])


def _chip_sum(place, g, r1, r2, i):
    _, k, r, cdim = BIG[i]
    h = r // 2

    def body(pref, g_ref, r1_ref, r2_ref, o_ref):
        acc = g_ref[0, 0] + r1_ref[0, 0]
        for j in range(3):
            acc = acc + r2_ref[j, 0].astype(F32)
        o_ref[0] = acc

    grid_spec = pltpu.PrefetchScalarGridSpec(
        num_scalar_prefetch=1, grid=(k,),
        in_specs=[pl.BlockSpec((1, 1, h, cdim), lambda t, pref: (pref[1], t, pref[0], 0)),
                  pl.BlockSpec((1, 1, h, cdim), lambda t, pref: (pref[1], t, 0, 0)),
                  pl.BlockSpec((3, 1, h, cdim), lambda t, pref: (0, t, 0, 0))],
        out_specs=pl.BlockSpec((1, h, cdim), lambda t, pref: (t, pref[0], 0)))
    return pl.pallas_call(
        body, name=f"chip_sum_{BIG[i][0]}", grid_spec=grid_spec, out_shape=_sds((k, r, cdim), F32),
        compiler_params=_params(("parallel",)),
    )(place, g, r1, r2)


def _chip_sums(place, gs, r1s, r2s, idx, name):
    n = len(idx)
    dims = [(BIG[w][1], BIG[w][2] // 4, BIG[w][3]) for w in idx]

    def body(pref, *refs):
        for i in range(n):
            acc = refs[i][0] + refs[n + i][0]
            for j in range(3):
                acc = acc + refs[2 * n + i][j].astype(F32)
            refs[3 * n + i][...] = acc

    in_specs = ([pl.BlockSpec((1, k, q, cdim), lambda t, pref: (pref[1], 0, pref[0] * 2 + t, 0)) for k, q, cdim in dims]
                + [pl.BlockSpec((1, k, q, cdim), lambda t, pref: (pref[1], 0, t, 0)) for k, q, cdim in dims]
                + [pl.BlockSpec((3, k, q, cdim), lambda t, pref: (0, 0, t, 0)) for k, q, cdim in dims])
    out_specs = [pl.BlockSpec((k, q, cdim), lambda t, pref: (0, pref[0] * 2 + t, 0)) for k, q, cdim in dims]
    grid_spec = pltpu.PrefetchScalarGridSpec(num_scalar_prefetch=1, grid=(2,), in_specs=in_specs, out_specs=out_specs)
    return pl.pallas_call(
        body, name=name, grid_spec=grid_spec, out_shape=[_sds(BIG[w][1:], F32) for w in idx],
        compiler_params=_params(("parallel",)),
    )(place, *gs, *r1s, *r2s)


def _pair_gather(hs, idx, name):
    n = len(idx)

    def body(*refs):
        dst = refs[n:2 * n]
        send_sems, recv_sems = refs[2 * n:]
        x, y, c, _ = _place()
        cps = []
        for i in range(n):
            mine = _half(dst[i], idx[i], c)
            cps.append(_remote(mine, mine, send_sems.at[i], recv_sems.at[i], (x, y, 1 - c)))
            cps[-1].start()
        for i in range(n):
            theirs = _half(dst[i], idx[i], 1 - c)
            _remote(theirs, theirs, send_sems.at[i], recv_sems.at[i], (x, y, 1 - c)).wait_recv()
        for cp in cps:
            cp.wait_send()

    return pl.pallas_call(
        body, name=name, in_specs=[ANY] * n, out_specs=[ANY] * n,
        out_shape=[_sds(BIG[w][1:], F32) for w in idx],
        input_output_aliases={i: i for i in range(n)},
        scratch_shapes=[pltpu.SemaphoreType.DMA((n,)), pltpu.SemaphoreType.DMA((n,))],
    )(*hs)


SMALL_ROWS = 40


def _all_reduce_small(pack, after):
    def body(p_ref, o_ref, slots, send_sems, recv_sems):
        x, y, c, _ = _place()
        me = 4 * x + 2 * y + c
        cps = []
        for r in range(1, 8):
            peer = (x if not r & 4 else 1 - x, y if not r & 2 else 1 - y, c if not r & 1 else 1 - c)
            cps.append(_remote(p_ref, slots.at[r], send_sems.at[r - 1], recv_sems.at[r - 1], peer))
            cps[-1].start()
        slots[0] = p_ref[...]
        for cp in cps:
            cp.wait()
        acc = slots[me]
        for dev in range(1, 8):
            acc = acc + slots[jnp.bitwise_xor(me, dev)]
        o_ref[...] = acc

    vm = pl.BlockSpec(memory_space=pltpu.VMEM)
    return pl.pallas_call(
        functools.partial(_skip_arg, body, 1), name="all_reduce_small", in_specs=[vm, ANY], out_specs=vm,
        out_shape=_sds((SMALL_ROWS, D), F32),
        scratch_shapes=[pltpu.VMEM((8, SMALL_ROWS, D), F32), pltpu.SemaphoreType.DMA((7,)),
                        pltpu.SemaphoreType.DMA((7,))],
    )(pack, after)


def _adamw_math(w, g, m, v):
    m = ADAM_B1 * m + (1.0 - ADAM_B1) * g
    v = ADAM_B2 * v + (1.0 - ADAM_B2) * (g * g)
    m_hat = m / (1.0 - ADAM_B1 ** ADAM_STEP)
    v_hat = v / (1.0 - ADAM_B2 ** ADAM_STEP)
    delta = -ADAM_LR * (m_hat / (jnp.sqrt(v_hat) + ADAM_EPS) + ADAM_WD * w)
    return delta, m, v


def _adamw_big(w, g, m, v, i):
    _, k, r, cdim = BIG[i]
    nblk = 4 if k == 1 else 1

    def body(w_ref, g_ref, m_ref, v_ref, d_ref, nm_ref, nv_ref, go_ref):
        gv = g_ref[...]
        d_ref[...], nm_ref[...], nv_ref[...] = _adamw_math(w_ref[...], gv, m_ref[...], v_ref[...])
        go_ref[...] = gv

    spec = pl.BlockSpec((1, r // nblk, cdim), lambda t, b: (t, b, 0))
    return pl.pallas_call(
        body, name=f"adamw_{BIG[i][0]}", grid=(k, nblk), in_specs=[spec] * 4, out_specs=[spec] * 4,
        out_shape=[_sds((k, r, cdim), F32)] * 4,
        compiler_params=_params(("parallel", "parallel")),
    )(w, g, m, v)


def _small_start(pack, after):
    def body(pack_ref, land_ref, after_ref, send_sems, recv_sems, pack_thru, land_thru, token):
        x, y, c, _ = _place()
        for r in range(1, 8):
            peer = (x if not r & 4 else 1 - x, y if not r & 2 else 1 - y, c if not r & 1 else 1 - c)
            _remote(pack_ref, land_ref.at[r - 1], send_sems.at[r - 1], recv_sems.at[r - 1], peer).start()
        token[...] = jnp.zeros(TOKEN, F32)

    land = lax.empty((7, SMALL_ROWS, D), F32)
    outs = pl.pallas_call(
        body, name="small_start", in_specs=[HBM, HBM, ANY],
        out_specs=(SEM, SEM, HBM, HBM, pl.BlockSpec(memory_space=pltpu.VMEM)),
        out_shape=(pltpu.SemaphoreType.DMA((7,)), pltpu.SemaphoreType.DMA((7,)), pltpu.HBM(pack.shape, F32),
                   pltpu.HBM(land.shape, F32), _sds(TOKEN, F32)),
        input_output_aliases={0: 2, 1: 3},
        compiler_params=pltpu.CompilerParams(has_side_effects=EFFECT),
    )(pltpu.with_memory_space_constraint(pack, pltpu.HBM), pltpu.with_memory_space_constraint(land, pltpu.HBM), after)
    return outs


def _small_wait(send_sems, recv_sems, pack, land, after):
    def body(pack_ref, land_ref, send_sems, recv_sems, *rest):
        x, y, c, _ = _place()
        for r in range(1, 8):
            peer = (x if not r & 4 else 1 - x, y if not r & 2 else 1 - y, c if not r & 1 else 1 - c)
            cp = _remote(pack_ref, land_ref.at[r - 1], send_sems.at[r - 1], recv_sems.at[r - 1], peer)
            cp.wait_send()
            cp.wait_recv()

    return pl.pallas_call(
        body, name="small_wait", in_specs=[HBM, HBM, SEM, SEM] + [ANY] * len(after), out_specs=(HBM, HBM),
        out_shape=(pltpu.HBM(pack.shape, F32), pltpu.HBM(land.shape, F32)),
        input_output_aliases={0: 0, 1: 1},
        compiler_params=pltpu.CompilerParams(has_side_effects=EFFECT),
    )(pack, land, send_sems, recv_sems, *after)


def _small_update(place, pack, land, ws, ms, vs):
    n = len(ws)

    def body(pref, pack_ref, land_ref, *refs):
        chip = pref[1]
        me = 2 * chip + pref[0]
        own = pack_ref[...]
        tot = None
        for dev in range(8):
            r = jnp.bitwise_xor(me, dev)
            term = jnp.where(r == 0, own, land_ref[jnp.maximum(r - 1, 0)])
            tot = term if tot is None else tot + term
        out, buf = refs[3 * n:-1], refs[-1]
        buf[...] = tot
        g_conv = jnp.zeros((3, SH_O), F32)
        for s in range(4):
            g_conv = g_conv + jnp.where(chip == s, buf[24:27, s * SH_O:(s + 1) * SH_O], 0.0)
        gs = [buf[0:2, :], buf[8:10, :], buf[16:17, :], g_conv]
        out[0][...] = buf[32:33, 0:128]
        for i in range(n):
            d, nm, nv = _adamw_math(refs[i][...], gs[i], refs[n + i][...], refs[2 * n + i][...])
            out[1 + i][...] = gs[i]
            out[1 + n + i][...] = d
            out[1 + 2 * n + i][...] = nm
            out[1 + 3 * n + i][...] = nv

    def full(shape):
        nd = len(shape)
        return pl.BlockSpec(shape, lambda i, pref: (0,) * nd)

    specs = [full(w.shape) for w in ws]
    grid_spec = pltpu.PrefetchScalarGridSpec(
        num_scalar_prefetch=1, grid=(1,),
        in_specs=[full(pack.shape), full(land.shape)] + specs * 3, out_specs=[full((1, 128))] + specs * 4,
        scratch_shapes=[pltpu.VMEM((SMALL_ROWS, D), F32)])
    outs = pl.pallas_call(
        body, name="small_update", grid_spec=grid_spec,
        out_shape=[_sds((1, 128), F32)] + [_sds(w.shape, F32) for w in ws] * 4,
        compiler_params=_params(("arbitrary",)),
    )(place, pack, land, *ws, *ms, *vs)
    return outs[0], outs[1:1 + n], outs[1 + n:1 + 2 * n], outs[1 + 2 * n:1 + 3 * n], outs[1 + 3 * n:]


def _adamw_layer(ws, gs, ms, vs, idx, name):
    n = len(idx)
    dims = [(BIG[w][1], BIG[w][2] // 4, BIG[w][3]) for w in idx]

    def body(*refs):
        for i in range(n):
            gv = refs[n + i][...]
            d, nm, nv = _adamw_math(refs[i][...], gv, refs[2 * n + i][...], refs[3 * n + i][...])
            refs[4 * n + i][...] = d
            refs[5 * n + i][...] = nm
            refs[6 * n + i][...] = nv
            refs[7 * n + i][...] = gv

    specs = [pl.BlockSpec((k, q, cdim), lambda t: (0, t, 0)) for k, q, cdim in dims]
    outs = pl.pallas_call(
        body, name=name, grid=(4,), in_specs=specs * 4, out_specs=specs * 4,
        out_shape=[_sds(BIG[w][1:], F32) for w in idx] * 4,
        compiler_params=_params(("parallel",)),
    )(*ws, *gs, *ms, *vs)
    return [tuple(outs[j * n + i] for j in range(4)) for i in range(n)]


def _adamw_small(ws, gs, ms, vs):
    n = len(ws)

    def body(*refs):
        for i in range(n):
            w_ref, g_ref, m_ref, v_ref = refs[i], refs[n + i], refs[2 * n + i], refs[3 * n + i]
            d, nm, nv = _adamw_math(w_ref[...], g_ref[...], m_ref[...], v_ref[...])
            refs[4 * n + i][...] = d
            refs[5 * n + i][...] = nm
            refs[6 * n + i][...] = nv

    specs = [_full(w.shape) for w in ws]
    outs = pl.pallas_call(
        body, name="adamw_small", grid=(1,), in_specs=specs * 4, out_specs=specs * 3,
        out_shape=[_sds(w.shape, F32) for w in ws] * 3,
        compiler_params=_params(("arbitrary",)),
    )(*ws, *gs, *ms, *vs)
    return outs[:n], outs[n:2 * n], outs[2 * n:]


def _pad_rows(a, rows):
    return jnp.pad(a, ((0, rows - a.shape[0]), (0, 0)))


def kernel(x, mem, positions, norm_g, mem_norm_g, w_mem_kv, attn_w_in, attn_w_out, conv_w_in, conv_w, conv_w_out, final_g, loss_target, m_norm_g, m_mem_norm_g, m_w_mem_kv, m_attn_w_in, m_attn_w_out, m_conv_w_in, m_conv_w, m_conv_w_out, m_final_g, v_norm_g, v_mem_norm_g, v_w_mem_kv, v_attn_w_in, v_attn_w_out, v_conv_w_in, v_conv_w, v_conv_w_out, v_final_g):
    mx, my, mc = lax.axis_index("x"), lax.axis_index("y"), lax.axis_index("c")
    place = jnp.stack([mc, 2 * mx + my]).astype(jnp.int32)

    w_big = [w_mem_kv, attn_w_in, attn_w_out, conv_w_in, conv_w_out]
    m_big = [m_w_mem_kv, m_attn_w_in, m_attn_w_out, m_conv_w_in, m_conv_w_out]
    v_big = [v_w_mem_kv, v_attn_w_in, v_attn_w_out, v_conv_w_in, v_conv_w_out]
    first, rest = (1,), (0, 2, 3, 4)
    wb1 = _cast_weights(place, [w_big[i] for i in first], place, first, "cast_w_in_a")
    a1_send, a1_recv, a1_bufs, a1_token = _gather_start(wb1, place, first, "gather_a1_start")
    wbr = _cast_weights(place, [w_big[i] for i in rest], a1_token, rest, "cast_weights")
    rest = (0, 2)
    a2_send, a2_recv, a2_bufs, a2_token = _gather_start([wbr[0], wbr[1]], a1_token, rest, "gather_a2_start")
    gb_send, gb_recv, gb_bufs, gb_token = _gather_start([wbr[2], wbr[3]], a2_token, LAYER_B, "gather_b_start")

    xs, tgt = x[0], loss_target[0]
    g0, g1 = norm_g[0:1], norm_g[1:2]
    rc, rs1, rs2 = _rope_tables(positions[0].astype(F32).reshape(S, 1), gb_token)
    a1_bufs = _gather_wait(a1_send, a1_recv, a1_bufs, [rc], first, "gather_a1_wait")
    w_in_a = _gather_forward(a1_bufs, first, "gather_a1_forward")[0].reshape(4, D, SH_A)
    hn0, q, k, v, qm0, z0 = _in_proj_a(xs, g0, w_in_a, rc, rs1, rs2, gb_token)
    a2_bufs = _gather_wait(a2_send, a2_recv, a2_bufs, [q], rest, "gather_a2_wait")
    f2_send, f2_recv, a2_bufs, f2_token = _forward_start(a2_bufs, None, q, rest, "forward_a2_start")
    fwd = [_attn_fwd(q, k, v, 0, f2_token)]
    fwd.append(_attn_fwd(q, k, v, 1, fwd[0][0]))
    cw_own = _pad_rows(conv_w[0], CW_ROWS)
    gb_bufs = _gather_wait(gb_send, gb_recv, gb_bufs, [fwd[1][0]], LAYER_B, "gather_b_wait")
    fb_send, fb_recv, gb_bufs, fb_token = _forward_start(gb_bufs, cw_own, fwd[1][0], LAYER_B, "forward_b_start")
    fwd.append(_attn_fwd(q, k, v, 2, fb_token))
    os_, ls, lss = [f[0] for f in fwd], [f[1] for f in fwd], [f[2] for f in fwd]
    wkv_f, w_out_a = _forward_wait(f2_send, f2_recv, a2_bufs, [os_[2]], rest, False, "forward_a2_wait")
    w_out_a = w_out_a.reshape(4, BR_A, SH_O)
    memn, kv = _mem_fwd(mem[0], mem_norm_g, wkv_f)
    h1 = _attn_out(os_, ls, qm0, kv[0], z0, xs, w_out_a)

    w_in_b, w_out_b, _, cw_f = _forward_wait(fb_send, fb_recv, gb_bufs, [h1], LAYER_B, True, "forward_b_wait")
    w_in_b = w_in_b.reshape(4, D, SH_B)
    w_out_b = w_out_b.reshape(BR_B, D)
    cw_f = lax.dynamic_update_slice(cw_f, cw_own[None], (2 * mx + my, 0, 0))
    cw8 = cw_f.transpose(1, 0, 2).reshape(CW_ROWS, D)
    hn1, bg, cg, u, qm1, z1 = _in_proj_b(h1, g1, w_in_b)
    dh2, loss_part, dfg = _conv_out_loss(bg, cg, u, cw8, qm1, kv[1], z1, h1, w_out_b, final_g.reshape(1, D), tgt)

    dproj_b, dw_out_b, dcw, dkv1 = _conv_bwd(dh2, bg, cg, u, cw8, qm1, kv[1], z1, w_out_b)
    dw_in_b = _w_in_grad(hn1, dproj_b, IN_B, "w_in_b_grad")
    gs_b = [dw_in_b.reshape(4, 1, D, SH_B), dw_out_b.reshape(4, 1, BR_B // 4, D)]
    pb_send, pb_recv, gs_b, pb_land, pb_token = _pair_start(gs_b, LAYER_B, "pair_b_start")
    dh1, dg1 = _in_proj_bwd(dproj_b, w_in_b, h1, g1, dh2, pb_token, IN_B, "in_proj_b_bwd")
    gs_b, r1_b = _pair_wait(pb_send, pb_recv, gs_b, pb_land, [dh1], LAYER_B, "pair_b_wait")
    ps_b = _pair_sums(place, gs_b, r1_b, LAYER_B, "pair_sums_b")
    cb_send, cb_recv, cb_src, cb_land, cb_token = _chip_start(ps_b, LAYER_B, "chip_b_start")

    outs = _attn_out_bwd(dh1, os_, ls, qm0, kv[0], z0, w_out_a, cb_token)
    dos, dds, dqm, dz, dw_out_a, dkv0 = outs[0:3], outs[3:6], outs[6], outs[7], outs[8], outs[9]
    bwd = [_attn_bwd(q, k, v, dos[g], lss[g], dds[g], g) for g in range(3)]
    dproj_a = _qkv_bwd([b[0] for b in bwd], [b[1] for b in bwd], [b[2] for b in bwd], dqm, dz, rc, rs1, rs2)
    dw_in_a = _w_in_grad(hn0, dproj_a, IN_A, "w_in_a_grad")
    dwkv, dmg = _mem_bwd(mem[0], mem_norm_g, memn, wkv_f, dkv0, dkv1)

    gs_a = [dwkv, dw_in_a.reshape(4, 1, D, SH_A), dw_out_a.reshape(4, 1, BR_A, SH_O)]
    r1_a = _pair_exchange(gs_a, LAYER_A, "pair_exchange_a")
    ps_a = _pair_sums(place, gs_a, r1_a, LAYER_A, "pair_sums_a")
    ca_send, ca_recv, ca_src, ca_land, ca_token = _chip_start(ps_a, LAYER_A, "chip_a_start")

    gx, dg0 = _in_proj_bwd(dproj_a, w_in_a, xs, g0, dh1, ca_token, IN_A, "in_proj_a_bwd")
    pack = jnp.concatenate([_pad_rows(jnp.concatenate([dg0, dg1], axis=0), 8), _pad_rows(dmg, 8), _pad_rows(dfg, 8),
                            dcw, _pad_rows(jnp.pad(loss_part, ((0, 0), (0, D - 128))), 8)], axis=0)
    sm_send, sm_recv, pack, sm_land, sm_token = _small_start(pack, ca_token)
    r2_b = _chip_wait(cb_send, cb_recv, cb_src, cb_land, [ca_token], LAYER_B, "chip_b_wait")
    hs_b = _chip_sums(place, gs_b, r1_b, r2_b, LAYER_B, "chip_sums_b")
    g_b = _pair_gather(hs_b, LAYER_B, "pair_gather_b")
    upd_b = _adamw_layer([w_big[w] for w in LAYER_B], g_b, [m_big[w] for w in LAYER_B], [v_big[w] for w in LAYER_B],
                         LAYER_B, "adamw_b")
    r2_a = _chip_wait(ca_send, ca_recv, ca_src, ca_land, [gx, upd_b[0][0], upd_b[1][0], sm_token], LAYER_A,
                      "chip_a_wait")
    hs_a = _chip_sums(place, gs_a, r1_a, r2_a, LAYER_A, "chip_sums_a")
    g_a = _pair_gather(hs_a, LAYER_A, "pair_gather_a")
    upd_a = _adamw_layer([w_big[w] for w in LAYER_A], g_a, [m_big[w] for w in LAYER_A], [v_big[w] for w in LAYER_A],
                         LAYER_A, "adamw_a")
    upd = upd_a + upd_b
    g_big = [u[3] for u in upd]
    pack, sm_land = _small_wait(sm_send, sm_recv, pack, sm_land, [r2_a[0]])
    sw = [norm_g, mem_norm_g, final_g.reshape(1, D), conv_w[0]]
    sm = [m_norm_g, m_mem_norm_g, m_final_g.reshape(1, D), m_conv_w[0]]
    sv = [v_norm_g, v_mem_norm_g, v_final_g.reshape(1, D), v_conv_w[0]]
    loss_row, sg, sd, snm, snv = _small_update(place, pack, sm_land, sw, sm, sv)
    loss = loss_row[0, 0]
    g_norm, g_memnorm, g_final, g_conv = sg

    def order(norm, memnorm, wkv, w_in_a, w_out_a, w_in_b, conv, w_out_b, final):
        return (norm, memnorm, wkv, w_in_a, w_out_a, w_in_b, conv.reshape(1, 3, SH_O), w_out_b, final.reshape(D))

    grads = order(g_norm, g_memnorm, g_big[0], g_big[1], g_big[2], g_big[3], g_conv, g_big[4], g_final)
    deltas = order(sd[0], sd[1], upd[0][0], upd[1][0], upd[2][0], upd[3][0], sd[3], upd[4][0], sd[2])
    new_m = order(snm[0], snm[1], upd[0][1], upd[1][1], upd[2][1], upd[3][1], snm[3], upd[4][1], snm[2])
    new_v = order(snv[0], snv[1], upd[0][2], upd[1][2], upd[2][2], upd[3][2], snv[3], upd[4][2], snv[2])
    return (loss, gx[None], *grads, *deltas, *new_m, *new_v)
```

```python
import functools

import numpy as np
import jax
import jax.numpy as jnp
from jax import lax
from jax.experimental import pallas as pl
from jax.experimental.pallas import tpu as pltpu

F32 = jnp.float32
BF16 = jnp.bfloat16

S = 2048
D = 1024
TM = 256
NT = S // TM
HD = 64
GW = 512
NQ = 3 * GW
MW = 256
NM = 256
IN_A = 3 * NQ + MW + GW + MW
IN_B = 3 * D + MW + D + MW
BR_A = GW + MW
BR_B = D + MW
SH_A = IN_A // 4
SH_B = IN_B // 4
SH_O = D // 4
QBLK = 128
DILATIONS = (1, 4, 16)
EPS = 1e-6
SCALE = HD ** -0.5
NEG = -1e30
ROPE_THETA = 500000.0

ADAM_LR = 0.001
ADAM_B1 = 0.9
ADAM_B2 = 0.999
ADAM_EPS = 1e-08
ADAM_WD = 0.01
ADAM_STEP = 10

VMEM_LIMIT_BYTES = 60 * 1024 * 1024


def _params(sem=None):
    if sem is None:
        return pltpu.CompilerParams(vmem_limit_bytes=VMEM_LIMIT_BYTES)
    return pltpu.CompilerParams(dimension_semantics=sem, vmem_limit_bytes=VMEM_LIMIT_BYTES)


def _full(shape):
    nd = len(shape)
    return pl.BlockSpec(shape, lambda *_: (0,) * nd)


def _rows(width, tm=TM):
    return pl.BlockSpec((tm, width), lambda i: (i, 0))


def _sds(shape, dtype):
    return jax.ShapeDtypeStruct(shape, dtype)


def _silu_parts(z):
    sig = 0.5 * jnp.tanh(0.5 * z) + 0.5
    return z * sig, sig * (1.0 + z * (1.0 - sig))


def _dot(a, b):
    return jnp.dot(a, b, preferred_element_type=F32)


def _dot_nt(a, b):
    return lax.dot_general(a, b, (((1,), (1,)), ((), ())), preferred_element_type=F32)


def _dot_tn(a, b):
    return lax.dot_general(a, b, (((0,), (0,)), ((), ())), preferred_element_type=F32)


def _rope_fwd(t, c, s1, s2):
    return t * c + pltpu.roll(t, 120, 1) * s1 + pltpu.roll(t, 8, 1) * s2


def _rope_bwd(g, c, s1, s2):
    return g * c + pltpu.roll(g * s1, 8, 1) + pltpu.roll(g * s2, 120, 1)


MEM_HEADS = MW // HD


def _stack_heads(x):
    head = lax.broadcasted_iota(jnp.int32, x.shape, 1) // HD
    return jnp.concatenate([jnp.where(head == h, x, 0.0) for h in range(MEM_HEADS)], axis=0).astype(BF16)


def _unstack_heads(x4):
    tm = x4.shape[0] // MEM_HEADS
    head = lax.broadcasted_iota(jnp.int32, (tm, MW), 1) // HD
    out = x4[:tm]
    for h in range(1, MEM_HEADS):
        out = jnp.where(head == h, x4[h * tm:(h + 1) * tm], out)
    return out


def _mem_attn(qm, kv):
    q4 = _stack_heads(qm.astype(F32))
    s = _dot_nt(q4, kv[:, :MW]) * SCALE
    e = jnp.exp(s - jnp.max(s, axis=-1, keepdims=True))
    p = e * (1.0 / jnp.sum(e, axis=-1, keepdims=True))
    return p, _unstack_heads(_dot(p.astype(BF16), kv[:, MW:])), q4


def _mem_attn_bwd(dmo, p, mo, q4, kv, dkv_ref):
    tm = dmo.shape[0]
    head = lax.broadcasted_iota(jnp.int32, dmo.shape, 1) // HD
    prod = dmo * mo
    delta = jnp.concatenate([jnp.sum(jnp.where(head == h, prod, 0.0), axis=-1, keepdims=True)
                             for h in range(MEM_HEADS)], axis=0)
    d4 = _stack_heads(dmo)
    ds = (p * (_dot_nt(d4, kv[:, MW:]) - delta) * SCALE).astype(BF16)
    dkv_ref[:, :MW] += _dot_tn(ds, q4)
    dkv_ref[:, MW:] += _dot_tn(p.astype(BF16), d4)
    return _unstack_heads(_dot(ds, kv[:, :MW]))


def _merge(o_refs, l_refs):
    ls = [r[...] for r in l_refs]
    m = jnp.maximum(jnp.maximum(ls[0], ls[1]), ls[2])
    es = [jnp.exp(l - m) for l in ls]
    inv = 1.0 / (es[0] + es[1] + es[2])
    ws = [e * inv for e in es]
    os_ = [r[...] for r in o_refs]
    mix = ws[0] * os_[0] + ws[1] * os_[1] + ws[2] * os_[2]
    return ws, mix


def _conv_taps(cg, u, cgp, up, first):
    a = cg * u
    ap = jnp.where(first, 0.0, cgp * up)
    row = lax.broadcasted_iota(jnp.int32, a.shape, 0)
    a1 = jnp.where(row == 0, ap[7:8, :], pltpu.roll(a, 1, 0))
    a2 = jnp.where(row == 0, ap[6:7, :], jnp.where(row == 1, ap[7:8, :], pltpu.roll(a, 2, 0)))
    return a, a1, a2


def _rope_tables(posf, after):
    half = 8
    invf = np.float32(ROPE_THETA) ** (-np.arange(half, dtype=np.float32) * np.float32(2.0 / 16))
    lane = np.arange(128)
    table = np.where((lane % HD) < 16, invf[lane % half], 0.0).astype(np.float32)[None, :]

    def body(pos_ref, invf_ref, c_ref, s1_ref, s2_ref):
        ang = pos_ref[...] * invf_ref[...]
        jm = lax.broadcasted_iota(jnp.int32, ang.shape, 1) & (HD - 1)
        cs = jnp.cos(ang)
        sn = jnp.sin(ang)
        c_ref[...] = jnp.where(jm < 16, cs, 1.0)
        s1_ref[...] = jnp.where(jm < 8, -sn, 0.0)
        s2_ref[...] = jnp.where((jm >= 8) & (jm < 16), sn, 0.0)

    out = _sds((S, 128), F32)
    return pl.pallas_call(
        functools.partial(_skip_arg, body, 2), name="rope_tables", grid=(NT,),
        in_specs=[_rows(1), _full((1, 128)), pl.BlockSpec(memory_space=pl.ANY)],
        out_specs=[_rows(128)] * 3, out_shape=[out] * 3,
        compiler_params=_params(("parallel",)),
    )(posf, jnp.asarray(table), after)


def _in_proj_a(x, g0, w_in, c, s1, s2, after):
    def body(x_ref, g_ref, w_ref, c_ref, s1_ref, s2_ref, hn_ref, q_ref, k_ref, v_ref, qm_ref, z_ref, proj):
        xf = x_ref[...]
        hn = xf * lax.rsqrt(jnp.mean(xf * xf, axis=-1, keepdims=True) + EPS) * g_ref[...]
        hb = hn.astype(BF16)
        hn_ref[...] = hb
        for s in range(4):
            proj[:, s * SH_A:(s + 1) * SH_A] = _dot(hb, w_ref[s])
        cc, a1, a2 = c_ref[...], s1_ref[...], s2_ref[...]
        for j in range(NQ // 128):
            q_ref[:, j * 128:(j + 1) * 128] = (
                _rope_fwd(proj[:, j * 128:(j + 1) * 128], cc, a1, a2) * SCALE).astype(BF16)
            k_ref[:, j * 128:(j + 1) * 128] = _rope_fwd(
                proj[:, NQ + j * 128:NQ + (j + 1) * 128], cc, a1, a2).astype(BF16)
        v_ref[...] = proj[:, 2 * NQ:3 * NQ].astype(BF16)
        qm_ref[...] = proj[:, 3 * NQ:3 * NQ + MW].astype(BF16)
        z_ref[...] = proj[:, 3 * NQ + MW:]

    return pl.pallas_call(
        functools.partial(_skip_arg, body, 6), name="in_proj_a", grid=(NT,),
        in_specs=[_rows(D), _full((1, D)), _full((4, D, SH_A)), _rows(128), _rows(128), _rows(128),
                  pl.BlockSpec(memory_space=pl.ANY)],
        out_specs=[_rows(D), _rows(NQ), _rows(NQ), _rows(NQ), _rows(MW), _rows(BR_A)],
        out_shape=[_sds((S, D), BF16), _sds((S, NQ), BF16), _sds((S, NQ), BF16), _sds((S, NQ), BF16),
                   _sds((S, MW), BF16), _sds((S, BR_A), F32)],
        scratch_shapes=[pltpu.VMEM((TM, IN_A), F32)],
        compiler_params=_params(("parallel",)),
    )(x, g0, w_in, c, s1, s2, after)


def _mem_fwd(mem, mg, wkv):
    def body(mem_ref, mg_ref, w_ref, memn_ref, kv_ref):
        mf = mem_ref[...]
        n = mf * lax.rsqrt(jnp.mean(mf * mf, axis=-1, keepdims=True) + EPS)
        for i in range(2):
            mn = (n * mg_ref[i:i + 1, :]).astype(BF16)
            memn_ref[i] = mn
            acc = _dot(mn[:, 0:NM], w_ref[0, i])
            for s in range(1, 4):
                acc += _dot(mn[:, s * NM:(s + 1) * NM], w_ref[s, i])
            kv_ref[i] = acc.astype(BF16)

    return pl.pallas_call(
        body, name="mem_fwd", grid=(1,),
        in_specs=[_full((NM, D)), _full((2, D)), _full((4, 2, NM, 2 * MW))],
        out_specs=[_full((2, NM, D)), _full((2, NM, 2 * MW))],
        out_shape=[_sds((2, NM, D), BF16), _sds((2, NM, 2 * MW), BF16)],
        compiler_params=_params(("arbitrary",)),
    )(mem, mg, wkv)


def _band_mask(j):
    qi = lax.broadcasted_iota(jnp.int32, (QBLK, 2 * QBLK), 0)
    kj = lax.broadcasted_iota(jnp.int32, (QBLK, 2 * QBLK), 1)
    dist = qi + QBLK - kj
    return (dist >= 0) & (dist <= QBLK) & ((kj >= QBLK) | (j > 0))


LANES = 128
NCHUNK = GW // LANES
FWD_UNROLL = 16
BWD_UNROLL = 4


def _perm_matrix(d):
    n = TM // d
    p = np.zeros((TM, TM), np.float32)
    for r in range(d):
        for i in range(n):
            p[r * n + i, i * d + r] = 1.0
    return p


def _split_dot(p, x, parts):
    hi = x.astype(BF16)
    rem = x - hi.astype(F32)
    lo = rem.astype(BF16)
    both = _dot(p, jnp.concatenate([hi, lo], axis=1))
    acc = both[:, :LANES] + both[:, LANES:]
    if parts == 3:
        acc = acc + _dot(p, (rem - lo.astype(F32)).astype(BF16))
    return acc


def _pair_dot(p, a, b):
    both = _dot(p, jnp.concatenate([a, b], axis=1))
    return both[:, :LANES], both[:, LANES:]


def _tile_to_streams(y, dst, t, d):
    n, ln = TM // d, S // d
    for r in range(d):
        dst[r * ln + t * n:r * ln + (t + 1) * n, :] = y[r * n:(r + 1) * n].astype(dst.dtype)


def _tile_from_streams(src, t, d):
    n, ln = TM // d, S // d
    return jnp.concatenate([src[r * ln + t * n:r * ln + (t + 1) * n, :] for r in range(d)], axis=0)


def _head_masks():
    first = lax.broadcasted_iota(jnp.int32, (TM, LANES), 1) < HD
    return first, jnp.logical_not(first)


def _attn_fwd(q, k, v, g, after):
    d = DILATIONS[g]
    nb = S // d // QBLK
    perm = _perm_matrix(d)

    def body(q_ref, k_ref, v_ref, p_ref, pt_ref, o_ref, l_ref, ls_ref, q0, q1, ks, vs, os_):
        first, second = _head_masks()
        pm = p_ref[...]
        for t in range(NT):
            rows = slice(t * TM, (t + 1) * TM)
            if d == 1:
                qt = q_ref[rows, :].astype(F32)
            else:
                qt, kt = _pair_dot(pm, q_ref[rows, :], k_ref[rows, :])
                _tile_to_streams(kt, ks, t, d)
                _tile_to_streams(_dot(pm, v_ref[rows, :]), vs, t, d)
            _tile_to_streams(jnp.where(first, qt, 0.0), q0, t, d)
            _tile_to_streams(jnp.where(second, qt, 0.0), q1, t, d)
        kref, vref = (k_ref, v_ref) if d == 1 else (ks, vs)
        oref, lref = (o_ref, l_ref) if d == 1 else (os_, ls_ref)

        def blk(b, carry):
            r0 = pl.multiple_of(b * QBLK, QBLK)
            p0 = pl.multiple_of(jnp.maximum(b - 1, 0) * QBLK, QBLK)
            kk = jnp.concatenate([kref[pl.ds(p0, QBLK), :], kref[pl.ds(r0, QBLK), :]], axis=0)
            vv = jnp.concatenate([vref[pl.ds(p0, QBLK), :], vref[pl.ds(r0, QBLK), :]], axis=0)
            valid = _band_mask(b & (nb - 1))
            acc, lse = [], []
            for qh in (q0, q1):
                s = jnp.where(valid, _dot_nt(qh[pl.ds(r0, QBLK), :], kk), NEG)
                m = jnp.max(s, axis=-1, keepdims=True)
                e = jnp.exp(s - m)
                l = jnp.sum(e, axis=-1, keepdims=True)
                acc.append(_dot(e.astype(BF16), vv) * (1.0 / l))
                lse.append(m + jnp.log(l))
            f = first[:QBLK]
            oref[pl.ds(r0, QBLK), :] = jnp.where(f, acc[0], acc[1])
            lref[pl.ds(r0, QBLK), :] = jnp.where(f, lse[0], lse[1])
            return carry

        lax.fori_loop(0, S // QBLK, blk, 0, unroll=FWD_UNROLL)
        if d > 1:
            ptm = pt_ref[...]
            for t in range(NT):
                rows = slice(t * TM, (t + 1) * TM)
                o_ref[rows, :] = _split_dot(ptm, _tile_from_streams(os_, t, d), 2)
                l_ref[rows, :] = _split_dot(ptm, _tile_from_streams(ls_ref, t, d), 3)

    qkv_spec = pl.BlockSpec((S, LANES), lambda c: (0, g * NCHUNK + c))
    out_spec = pl.BlockSpec((S, LANES), lambda c: (0, c))
    n_out = 2 if d == 1 else 3
    inner = body if d > 1 else functools.partial(_drop_arg, body, 7)
    outs = pl.pallas_call(
        functools.partial(_skip_arg, inner, 5), name=f"attn_fwd_g{g}", grid=(NCHUNK,),
        in_specs=[qkv_spec] * 3 + [_full((TM, TM))] * 2 + [pl.BlockSpec(memory_space=pl.ANY)],
        out_specs=[out_spec] * n_out, out_shape=[_sds((S, GW), F32)] * n_out,
        scratch_shapes=[pltpu.VMEM((S, LANES), BF16)] * 4 + [pltpu.VMEM((S, LANES), F32)],
        compiler_params=_params(("parallel",)),
    )(q, k, v, jnp.asarray(perm, BF16), jnp.asarray(perm.T, BF16), after)
    return (outs[0], outs[1], outs[1]) if d == 1 else tuple(outs)


def _drop_arg(body, pos, *refs):
    return body(*refs[:pos], None, *refs[pos:])


def _attn_out(os_, ls, qm, kv0, z, x, w_out):
    def body(o0, o1, o2, l0, l1, l2, qm_ref, kv_ref, z_ref, x_ref, w_ref, h_ref, ybuf):
        _, mix = _merge((o0, o1, o2), (l0, l1, l2))
        sz, _ = _silu_parts(z_ref[...])
        ybuf[:, :GW] = (mix * sz[:, :GW]).astype(BF16)
        _, mo, _ = _mem_attn(qm_ref[...], kv_ref[...])
        ybuf[:, GW:] = (mo * sz[:, GW:]).astype(BF16)
        yb = ybuf[...]
        for s in range(4):
            cs = slice(s * SH_O, (s + 1) * SH_O)
            h_ref[:, cs] = x_ref[:, cs] + _dot(yb, w_ref[s])

    return pl.pallas_call(
        body, name="attn_out", grid=(NT,),
        in_specs=[_rows(GW)] * 6 + [_rows(MW), _full((NM, 2 * MW)), _rows(BR_A), _rows(D), _full((4, BR_A, SH_O))],
        out_specs=_rows(D), out_shape=_sds((S, D), F32),
        scratch_shapes=[pltpu.VMEM((TM, BR_A), BF16)],
        compiler_params=_params(("parallel",)),
    )(*os_, *ls, qm, kv0, z, x, w_out)


def _in_proj_b(h1, g1, w_in):
    def body(x_ref, g_ref, w_ref, hn_ref, bg_ref, cg_ref, u_ref, qm_ref, z_ref, proj):
        xf = x_ref[...]
        hn = xf * lax.rsqrt(jnp.mean(xf * xf, axis=-1, keepdims=True) + EPS) * g_ref[...]
        hb = hn.astype(BF16)
        hn_ref[...] = hb
        for s in range(4):
            proj[:, s * SH_B:(s + 1) * SH_B] = _dot(hb, w_ref[s])
        bg_ref[...] = proj[:, :D]
        cg_ref[...] = proj[:, D:2 * D]
        u_ref[...] = proj[:, 2 * D:3 * D]
        qm_ref[...] = proj[:, 3 * D:3 * D + MW].astype(BF16)
        z_ref[...] = proj[:, 3 * D + MW:]

    return pl.pallas_call(
        body, name="in_proj_b", grid=(NT,),
        in_specs=[_rows(D), _full((1, D)), _full((4, D, SH_B))],
        out_specs=[_rows(D), _rows(D), _rows(D), _rows(D), _rows(MW), _rows(BR_B)],
        out_shape=[_sds((S, D), BF16), _sds((S, D), F32), _sds((S, D), F32), _sds((S, D), F32),
                   _sds((S, MW), BF16), _sds((S, BR_B), F32)],
        scratch_shapes=[pltpu.VMEM((TM, IN_B), F32)],
        compiler_params=_params(("parallel",)),
    )(h1, g1, w_in)


def _prev8(width):
    return pl.BlockSpec((8, width), lambda i: (jnp.maximum(i * (TM // 8) - 1, 0), 0))


def _conv_out_loss(bg, cg, u, cw, qm, kv1, z, h1, w_out, fg, tgt):
    def body(bg_ref, cg_ref, u_ref, cgp_ref, up_ref, cw_ref, qm_ref, kv_ref, z_ref, h_ref, w_ref, fg_ref, t_ref,
             dh_ref, loss_ref, dfg_ref, ybuf):
        i = pl.program_id(0)
        a, a1, a2 = _conv_taps(cg_ref[...], u_ref[...], cgp_ref[...], up_ref[...], i == 0)
        conv = cw_ref[0:1, :] * a2 + cw_ref[1:2, :] * a1 + cw_ref[2:3, :] * a
        sz, _ = _silu_parts(z_ref[...])
        ybuf[:, :D] = (bg_ref[...] * conv * sz[:, :D]).astype(BF16)
        _, mo, _ = _mem_attn(qm_ref[...], kv_ref[...])
        ybuf[:, D:] = (mo * sz[:, D:]).astype(BF16)
        h2 = h_ref[...] + _dot(ybuf[...], w_ref[...])
        rstd = lax.rsqrt(jnp.mean(h2 * h2, axis=-1, keepdims=True) + EPS)
        n = h2 * rstd
        fgv = fg_ref[...]
        err = n * fgv - t_ref[...]
        dout = err * (1.0 / D)
        dn = dout * fgv
        dh_ref[...] = rstd * (dn - n * jnp.mean(dn * n, axis=-1, keepdims=True))

        @pl.when(i == 0)
        def _():
            loss_ref[...] = jnp.zeros_like(loss_ref)
            dfg_ref[...] = jnp.zeros_like(dfg_ref)

        loss_ref[...] += jnp.sum(err * err) * (0.5 / D)
        dfg_ref[...] += jnp.sum(dout * n, axis=0, keepdims=True)

    return pl.pallas_call(
        body, name="conv_out_loss", grid=(NT,),
        in_specs=[_rows(D), _rows(D), _rows(D), _prev8(D), _prev8(D), _full((8, D)), _rows(MW),
                  _full((NM, 2 * MW)), _rows(BR_B), _rows(D), _full((BR_B, D)), _full((1, D)), _rows(D)],
        out_specs=[_rows(D), _full((1, 128)), _full((1, D))],
        out_shape=[_sds((S, D), F32), _sds((1, 128), F32), _sds((1, D), F32)],
        scratch_shapes=[pltpu.VMEM((TM, BR_B), BF16)],
        compiler_params=_params(("arbitrary",)),
    )(bg, cg, u, cg, u, cw, qm, kv1, z, h1, w_out, fg, tgt)


def _conv_bwd(dh2, bg, cg, u, cw, qm, kv1, z, w_out):
    rev = lambda i: (NT - 1 - i, 0)
    rows = lambda w: pl.BlockSpec((TM, w), rev)
    prev8 = pl.BlockSpec((8, D), lambda i: (jnp.maximum((NT - 1 - i) * (TM // 8) - 1, 0), 0))

    def body(dh_ref, bg_ref, cg_ref, u_ref, cgp_ref, up_ref, cw_ref, qm_ref, kv_ref, z_ref, w_ref,
             dproj_ref, dw_ref, dcw_ref, dkv_ref, ybuf, carry):
        i = pl.program_id(0)

        @pl.when(i == 0)
        def _():
            dw_ref[...] = jnp.zeros_like(dw_ref)
            dcw_ref[...] = jnp.zeros_like(dcw_ref)
            dkv_ref[...] = jnp.zeros_like(dkv_ref)
            carry[...] = jnp.zeros_like(carry)

        bgv, cgv, uv = bg_ref[...], cg_ref[...], u_ref[...]
        a, a1, a2 = _conv_taps(cgv, uv, cgp_ref[...], up_ref[...], i == NT - 1)
        w0, w1, w2 = cw_ref[0:1, :], cw_ref[1:2, :], cw_ref[2:3, :]
        conv = w0 * a2 + w1 * a1 + w2 * a
        mix = bgv * conv
        sz, dsz = _silu_parts(z_ref[...])
        kvv = kv_ref[...]
        p, mo, q4 = _mem_attn(qm_ref[...], kvv)
        ybuf[:, :D] = (mix * sz[:, :D]).astype(BF16)
        ybuf[:, D:] = (mo * sz[:, D:]).astype(BF16)
        dhb = dh_ref[...].astype(BF16)
        dw_ref[...] += _dot_tn(ybuf[...], dhb)
        dy = _dot_nt(dhb, w_ref[...])
        dcat = dy * sz
        dproj_ref[:, 3 * D + MW:3 * D + MW + D] = (dy[:, :D] * mix * dsz[:, :D]).astype(BF16)
        dproj_ref[:, 3 * D + MW + D:] = (dy[:, D:] * mo * dsz[:, D:]).astype(BF16)
        dmix = dcat[:, :D]
        dproj_ref[:, :D] = (dmix * conv).astype(BF16)
        dc = dmix * bgv
        nxt = carry[...]
        row = lax.broadcasted_iota(jnp.int32, dc.shape, 0)
        dc1 = jnp.where(row == TM - 1, nxt[0:1, :], pltpu.roll(dc, TM - 1, 0))
        dc2 = jnp.where(row == TM - 2, nxt[0:1, :], jnp.where(row == TM - 1, nxt[1:2, :], pltpu.roll(dc, TM - 2, 0)))
        carry[...] = dc[0:8, :]
        da = w2 * dc + w1 * dc1 + w0 * dc2
        dproj_ref[:, D:2 * D] = (da * uv).astype(BF16)
        dproj_ref[:, 2 * D:3 * D] = (da * cgv).astype(BF16)
        dcw_ref[0:1, :] += jnp.sum(dc * a2, axis=0, keepdims=True)
        dcw_ref[1:2, :] += jnp.sum(dc * a1, axis=0, keepdims=True)
        dcw_ref[2:3, :] += jnp.sum(dc * a, axis=0, keepdims=True)
        dproj_ref[:, 3 * D:3 * D + MW] = _mem_attn_bwd(dcat[:, D:], p, mo, q4, kvv, dkv_ref).astype(BF16)

    return pl.pallas_call(
        body, name="conv_bwd", grid=(NT,),
        in_specs=[rows(D), rows(D), rows(D), rows(D), prev8, prev8, _full((8, D)), rows(MW),
                  _full((NM, 2 * MW)), rows(BR_B), _full((BR_B, D))],
        out_specs=[rows(IN_B), _full((BR_B, D)), _full((8, D)), _full((NM, 2 * MW))],
        out_shape=[_sds((S, IN_B), BF16), _sds((BR_B, D), F32), _sds((8, D), F32), _sds((NM, 2 * MW), F32)],
        scratch_shapes=[pltpu.VMEM((TM, BR_B), BF16), pltpu.VMEM((8, D), F32)],
        compiler_params=_params(("arbitrary",)),
    )(dh2, bg, cg, u, cg, u, cw, qm, kv1, z, w_out)


def _in_proj_bwd(dproj, w_in, xin, g, dres, after, width, name):
    sh = width // 4

    def body(dp_ref, w_ref, x_ref, g_ref, dr_ref, dx_ref, dg_ref):
        i = pl.program_id(0)
        dhn = _dot_nt(dp_ref[:, 0:sh], w_ref[0])
        for s in range(1, 4):
            dhn += _dot_nt(dp_ref[:, s * sh:(s + 1) * sh], w_ref[s])
        xf = x_ref[...]
        rstd = lax.rsqrt(jnp.mean(xf * xf, axis=-1, keepdims=True) + EPS)
        n = xf * rstd
        dn = dhn * g_ref[...]
        dx_ref[...] = dr_ref[...] + rstd * (dn - n * jnp.mean(dn * n, axis=-1, keepdims=True))

        @pl.when(i == 0)
        def _():
            dg_ref[...] = jnp.zeros_like(dg_ref)

        dg_ref[...] += jnp.sum(dhn * n, axis=0, keepdims=True)

    return pl.pallas_call(
        functools.partial(_skip_arg, body, 5), name=name, grid=(NT,),
        in_specs=[_rows(width), _full((4, D, sh)), _rows(D), _full((1, D)), _rows(D), pl.BlockSpec(memory_space=pl.ANY)],
        out_specs=[_rows(D), _full((1, D))],
        out_shape=[_sds((S, D), F32), _sds((1, D), F32)],
        compiler_params=_params(("arbitrary",)),
    )(dproj, w_in, xin, g, dres, after)


def _w_in_grad(hn, dproj, width, name):
    sh = width // 4

    def body(hn_ref, dp_ref, dw_ref):
        dw_ref[0] = _dot_tn(hn_ref[...], dp_ref[...])

    return pl.pallas_call(
        body, name=name, grid=(4,),
        in_specs=[_full((S, D)), pl.BlockSpec((S, sh), lambda s: (0, s))],
        out_specs=pl.BlockSpec((1, D, sh), lambda s: (s, 0, 0)),
        out_shape=_sds((4, D, sh), F32),
        compiler_params=_params(("parallel",)),
    )(hn, dproj)


def _attn_out_bwd(dh1, os_, ls, qm, kv0, z, w_out, after):
    ones_bd = np.kron(np.eye(GW // HD, dtype=np.float32), np.ones((HD, HD), np.float32))

    def body(dh_ref, o0, o1, o2, l0, l1, l2, qm_ref, kv_ref, z_ref, w_ref, bd_ref,
             do0, do1, do2, dd0, dd1, dd2, dqm_ref, dz_ref, dw_ref, dkv_ref, ybuf):
        i = pl.program_id(0)

        @pl.when(i == 0)
        def _():
            dw_ref[...] = jnp.zeros_like(dw_ref)
            dkv_ref[...] = jnp.zeros_like(dkv_ref)

        ws, mix = _merge((o0, o1, o2), (l0, l1, l2))
        sz, dsz = _silu_parts(z_ref[...])
        kvv = kv_ref[...]
        p, mo, q4 = _mem_attn(qm_ref[...], kvv)
        ybuf[:, :GW] = (mix * sz[:, :GW]).astype(BF16)
        ybuf[:, GW:] = (mo * sz[:, GW:]).astype(BF16)
        yb = ybuf[...]
        dh = dh_ref[...]
        dy = None
        for s in range(4):
            dhb = dh[:, s * SH_O:(s + 1) * SH_O].astype(BF16)
            dw_ref[s] += _dot_tn(yb, dhb)
            part = _dot_nt(dhb, w_ref[s])
            dy = part if dy is None else dy + part
        dcat = dy * sz
        dz_ref[:, :GW] = (dy[:, :GW] * mix * dsz[:, :GW]).astype(BF16)
        dz_ref[:, GW:] = (dy[:, GW:] * mo * dsz[:, GW:]).astype(BF16)
        dmix = dcat[:, :GW]
        prod = dmix * mix
        hi = prod.astype(BF16)
        lo = (prod - hi.astype(F32)).astype(BF16)
        bd = bd_ref[...]
        tot = _dot(hi, bd) + _dot(lo, bd)
        for w, do_ref, dd_ref in zip(ws, (do0, do1, do2), (dd0, dd1, dd2)):
            do_ref[...] = (w * dmix).astype(BF16)
            dd_ref[...] = w * tot

        dqm_ref[...] = _mem_attn_bwd(dcat[:, GW:], p, mo, q4, kvv, dkv_ref).astype(BF16)

    return pl.pallas_call(
        functools.partial(_skip_arg, body, 12), name="attn_out_bwd", grid=(NT,),
        in_specs=[_rows(D)] + [_rows(GW)] * 6 + [_rows(MW), _full((NM, 2 * MW)), _rows(BR_A),
                                                   _full((4, BR_A, SH_O)), _full((GW, GW)),
                                                   pl.BlockSpec(memory_space=pl.ANY)],
        out_specs=[_rows(GW)] * 6 + [_rows(MW), _rows(BR_A), _full((4, BR_A, SH_O)), _full((NM, 2 * MW))],
        out_shape=[_sds((S, GW), BF16)] * 3 + [_sds((S, GW), F32)] * 3 + [
            _sds((S, MW), BF16), _sds((S, BR_A), BF16), _sds((4, BR_A, SH_O), F32), _sds((NM, 2 * MW), F32)],
        scratch_shapes=[pltpu.VMEM((TM, BR_A), BF16)],
        compiler_params=_params(("arbitrary",)),
    )(dh1, *os_, *ls, qm, kv0, z, w_out, jnp.asarray(ones_bd, dtype=BF16), after)


def _attn_bwd(q, k, v, do, lse_s, dd, g):
    d = DILATIONS[g]
    nb = S // d // QBLK
    perm = _perm_matrix(d)

    def body(q_ref, k_ref, v_ref, do_ref, l_ref, dd_ref, p_ref, pt_ref, dq_ref, dk_ref, dv_ref,
             q0, q1, g0, g1, ks, vs, dds, dqs, dks, dvs):
        first, second = _head_masks()
        pm = p_ref[...]
        for t in range(NT):
            rows = slice(t * TM, (t + 1) * TM)
            if d == 1:
                qt = q_ref[rows, :].astype(F32)
                gt = do_ref[rows, :].astype(F32)
            else:
                qt, gt = _pair_dot(pm, q_ref[rows, :], do_ref[rows, :])
                kt, vt = _pair_dot(pm, k_ref[rows, :], v_ref[rows, :])
                _tile_to_streams(kt, ks, t, d)
                _tile_to_streams(vt, vs, t, d)
                _tile_to_streams(_split_dot(pm, dd_ref[rows, :], 2), dds, t, d)
            _tile_to_streams(jnp.where(first, qt, 0.0), q0, t, d)
            _tile_to_streams(jnp.where(second, qt, 0.0), q1, t, d)
            _tile_to_streams(jnp.where(first, gt, 0.0), g0, t, d)
            _tile_to_streams(jnp.where(second, gt, 0.0), g1, t, d)
        kref, vref, ddref = (k_ref, v_ref, dd_ref) if d == 1 else (ks, vs, dds)
        dqref, dkref, dvref = (dq_ref, dk_ref, dv_ref) if d == 1 else (dqs, dks, dvs)
        dkref[...] = jnp.zeros_like(dkref)
        dvref[...] = jnp.zeros_like(dvref)

        def blk(b, carry):
            r0 = pl.multiple_of(b * QBLK, QBLK)
            p0 = pl.multiple_of(jnp.maximum(b - 1, 0) * QBLK, QBLK)
            kk = jnp.concatenate([kref[pl.ds(p0, QBLK), :], kref[pl.ds(r0, QBLK), :]], axis=0)
            vv = jnp.concatenate([vref[pl.ds(p0, QBLK), :], vref[pl.ds(r0, QBLK), :]], axis=0)
            lb = l_ref[pl.ds(r0, QBLK), :]
            ddb = ddref[pl.ds(r0, QBLK), :]
            lcol = jnp.concatenate([lb[:, 0:1], lb[:, HD:HD + 1]], axis=0)
            dcol = jnp.concatenate([ddb[:, 0:1], ddb[:, HD:HD + 1]], axis=0)
            valid = _band_mask(b & (nb - 1))
            valid2 = jnp.concatenate([valid, valid], axis=0)
            qq = jnp.concatenate([q0[pl.ds(r0, QBLK), :], q1[pl.ds(r0, QBLK), :]], axis=0)
            gg = jnp.concatenate([g0[pl.ds(r0, QBLK), :], g1[pl.ds(r0, QBLK), :]], axis=0)
            p = jnp.where(valid2, jnp.exp(_dot_nt(qq, kk) - lcol), 0.0)
            ds = (p * (_dot_nt(gg, vv) - dcol)).astype(BF16)
            dq2 = _dot(ds, kk)
            dqref[pl.ds(r0, QBLK), :] = jnp.where(first[:QBLK], dq2[:QBLK], dq2[QBLK:])
            dkk = _dot_tn(ds, qq)
            dvv = _dot_tn(p.astype(BF16), gg)
            dkref[pl.ds(p0, QBLK), :] += dkk[:QBLK]
            dkref[pl.ds(r0, QBLK), :] += dkk[QBLK:]
            dvref[pl.ds(p0, QBLK), :] += dvv[:QBLK]
            dvref[pl.ds(r0, QBLK), :] += dvv[QBLK:]
            return carry

        lax.fori_loop(0, S // QBLK, blk, 0, unroll=BWD_UNROLL)

        if d > 1:
            ptm = pt_ref[...]
            for t in range(NT):
                rows = slice(t * TM, (t + 1) * TM)
                dq_ref[rows, :] = _split_dot(ptm, _tile_from_streams(dqs, t, d), 2)
                dk_ref[rows, :] = _split_dot(ptm, _tile_from_streams(dks, t, d), 2)
                dv_ref[rows, :] = _split_dot(ptm, _tile_from_streams(dvs, t, d), 2)

    qkv_spec = pl.BlockSpec((S, LANES), lambda c: (0, g * NCHUNK + c))
    one_spec = pl.BlockSpec((S, LANES), lambda c: (0, c))
    return pl.pallas_call(
        body, name=f"attn_bwd_g{g}", grid=(NCHUNK,),
        in_specs=[qkv_spec] * 3 + [one_spec] * 3 + [_full((TM, TM))] * 2, out_specs=[one_spec] * 3,
        out_shape=[_sds((S, GW), F32)] * 3,
        scratch_shapes=[pltpu.VMEM((S, LANES), BF16)] * 6 + [pltpu.VMEM((S, LANES), F32)] * 4,
        compiler_params=_params(("parallel",)),
    )(q, k, v, do, lse_s, dd, jnp.asarray(perm, BF16), jnp.asarray(perm.T, BF16))


def _qkv_bwd(dqs, dks, dvs, dqm, dz, c, s1, s2):
    def body(q0, q1, q2, k0, k1, k2, v0, v1, v2, dqm_ref, dz_ref, c_ref, s1_ref, s2_ref, dp_ref):
        cc, a1, a2 = c_ref[...], s1_ref[...], s2_ref[...]
        for g, (qr, kr, vr) in enumerate(((q0, k0, v0), (q1, k1, v1), (q2, k2, v2))):
            for j in range(GW // 128):
                ls_ = slice(j * 128, (j + 1) * 128)
                c0 = g * GW + j * 128
                dp_ref[:, c0:c0 + 128] = (_rope_bwd(qr[:, ls_], cc, a1, a2) * SCALE).astype(BF16)
                dp_ref[:, NQ + c0:NQ + c0 + 128] = _rope_bwd(kr[:, ls_], cc, a1, a2).astype(BF16)
            dp_ref[:, 2 * NQ + g * GW:2 * NQ + (g + 1) * GW] = vr[...].astype(BF16)
        dp_ref[:, 3 * NQ:3 * NQ + MW] = dqm_ref[...]
        dp_ref[:, 3 * NQ + MW:] = dz_ref[...]

    return pl.pallas_call(
        body, name="qkv_bwd", grid=(NT,),
        in_specs=[_rows(GW)] * 9 + [_rows(MW), _rows(BR_A), _rows(128), _rows(128), _rows(128)],
        out_specs=_rows(IN_A), out_shape=_sds((S, IN_A), BF16),
        compiler_params=_params(("parallel",)),
    )(*dqs, *dks, *dvs, dqm, dz, c, s1, s2)


def _mem_bwd(mem, mg, memn, wkv, dkv0, dkv1):
    def body(mem_ref, mg_ref, memn_ref, w_ref, d0_ref, d1_ref, dw_ref, dg_ref):
        mf = mem_ref[...]
        n = mf * lax.rsqrt(jnp.mean(mf * mf, axis=-1, keepdims=True) + EPS)
        for i, d_ref in enumerate((d0_ref, d1_ref)):
            dkv = d_ref[...].astype(BF16)
            mn = memn_ref[i]
            for s in range(4):
                cs = slice(s * NM, (s + 1) * NM)
                dw_ref[s, i] = _dot_tn(mn[:, cs], dkv)
                dmn = _dot_nt(dkv, w_ref[s, i])
                dg_ref[i:i + 1, cs] = jnp.sum(dmn * n[:, cs], axis=0, keepdims=True)

    return pl.pallas_call(
        body, name="mem_bwd", grid=(1,),
        in_specs=[_full((NM, D)), _full((2, D)), _full((2, NM, D)), _full((4, 2, NM, 2 * MW)),
                  _full((NM, 2 * MW)), _full((NM, 2 * MW))],
        out_specs=[_full((4, 2, NM, 2 * MW)), _full((2, D))],
        out_shape=[_sds((4, 2, NM, 2 * MW), F32), _sds((2, D), F32)],
        compiler_params=_params(("arbitrary",)),
    )(mem, mg, memn, wkv, dkv0, dkv1)


MESH = pl.DeviceIdType.MESH
ANY = pl.BlockSpec(memory_space=pl.ANY)
BIG = (("wkv", 2, NM, 2 * MW), ("w_in_a", 1, D, SH_A), ("w_out_a", 1, BR_A, SH_O),
       ("w_in_b", 1, D, SH_B), ("w_out_b", 1, BR_B // 4, D))
NBIG = len(BIG)
CW_ROWS = 8


def _place():
    x, y, c = lax.axis_index("x"), lax.axis_index("y"), lax.axis_index("c")
    chips = ((1 - x, y), (x, 1 - y), (1 - x, 1 - y))
    return x, y, c, chips


def _remote(src, dst, ssem, rsem, dev):
    return pltpu.make_async_remote_copy(src_ref=src, dst_ref=dst, send_sem=ssem, recv_sem=rsem,
                                        device_id=dev, device_id_type=MESH)


def _cast_weights(place, ws, after, idx, name):
    nblk = 4
    n = len(idx)
    dims = [BIG[w][1:] for w in idx]

    def body(pref, *refs):
        for i in range(n):
            refs[n + 1 + i][0] = refs[i][...].astype(BF16)

    grid_spec = pltpu.PrefetchScalarGridSpec(
        num_scalar_prefetch=1, grid=(nblk,),
        in_specs=[pl.BlockSpec((k, r // nblk, cdim), lambda i, pref: (0, i, 0)) for k, r, cdim in dims]
        + [pl.BlockSpec(memory_space=pl.ANY)],
        out_specs=[pl.BlockSpec((1, k, r // nblk, cdim), lambda i, pref: (pref[1], 0, i, 0)) for k, r, cdim in dims])
    return pl.pallas_call(
        body, name=name, grid_spec=grid_spec,
        out_shape=[_sds((4, k, r, cdim), BF16) for k, r, cdim in dims],
        compiler_params=_params(("parallel",)),
    )(place, *ws, after)


LAYER_A = (0, 1, 2)
LAYER_B = (3, 4)
HBM = pl.BlockSpec(memory_space=pltpu.HBM)
SEM = pl.BlockSpec(memory_space=pltpu.SEMAPHORE)
EFFECT = pltpu.SideEffectType.DATAFLOW_SIDE_EFFECTING
TOKEN = (8, 128)


def _half(ref, w, which):
    h = BIG[w][2] // 2
    return ref.at[:, pl.ds(which * h, h), :]


def _skip_arg(body, pos, *refs):
    return body(*refs[:pos], *refs[pos + 1:])


def _gather_weights(wb, cw, idx, name):
    n = len(idx)

    def body(*refs):
        src_cw = refs[n]
        dst = refs[n + 1:2 * n + 2]
        loc_sem, send_sems, recv_sems, fsend_sems, frecv_sems = refs[2 * n + 2:]
        x, y, c, chips = _place()
        me = 2 * x + y
        loc = pltpu.make_async_copy(src_cw, dst[n].at[me], loc_sem)
        loc.start()
        sends = []
        for j, (px, py) in enumerate(chips):
            for i in range(n):
                mine = _half(dst[i].at[me], idx[i], c)
                sends.append(_remote(mine, mine, send_sems.at[j, i], recv_sems.at[j, i], (px, py, c)))
            sends.append(_remote(src_cw, dst[n].at[me], send_sems.at[j, n], recv_sems.at[j, n], (px, py, c)))
        for cp in sends:
            cp.start()
        fwds = []
        for j, (px, py) in enumerate(chips):
            for i in range(n):
                got = _half(dst[i].at[2 * px + py], idx[i], c)
                _remote(got, got, send_sems.at[j, i], recv_sems.at[j, i], (px, py, c)).wait_recv()
                fwds.append(_remote(got, got, fsend_sems.at[j, i], frecv_sems.at[j, i], (x, y, 1 - c)))
                fwds[-1].start()
            got = dst[n].at[2 * px + py]
            _remote(got, got, send_sems.at[j, n], recv_sems.at[j, n], (px, py, c)).wait_recv()
        for j, (px, py) in enumerate(chips):
            for i in range(n):
                got = _half(dst[i].at[2 * px + py], idx[i], 1 - c)
                _remote(got, got, fsend_sems.at[j, i], frecv_sems.at[j, i], (x, y, 1 - c)).wait_recv()
        for cp in sends + fwds:
            cp.wait_send()
        loc.wait()

    out_shape = [_sds(w.shape, BF16) for w in wb] + [_sds((4, CW_ROWS, SH_O), F32)]
    return pl.pallas_call(
        body, name=name, in_specs=[ANY] * (n + 1), out_specs=[ANY] * (n + 1), out_shape=out_shape,
        input_output_aliases={i: i for i in range(n)},
        scratch_shapes=[pltpu.SemaphoreType.DMA, pltpu.SemaphoreType.DMA((3, n + 1)),
                        pltpu.SemaphoreType.DMA((3, n + 1)), pltpu.SemaphoreType.DMA((3, n)),
                        pltpu.SemaphoreType.DMA((3, n))],
    )(*wb, cw)


def _gather_start(wb, after, idx, name):
    n = len(idx)

    def body(*refs):
        src = refs[:n]
        send_sems, recv_sems = refs[n + 1], refs[n + 2]
        token = refs[2 * n + 3]
        x, y, c, chips = _place()
        me = 2 * x + y
        for j, (px, py) in enumerate(chips):
            for i in range(n):
                mine = _half(src[i].at[me], idx[i], c)
                _remote(mine, mine, send_sems.at[j * n + i], recv_sems.at[j * n + i], (px, py, c)).start()
        token[...] = jnp.zeros(TOKEN, F32)

    outs = pl.pallas_call(
        body, name=name, in_specs=[HBM] * n + [ANY],
        out_specs=(SEM, SEM) + (HBM,) * n + (pl.BlockSpec(memory_space=pltpu.VMEM),),
        out_shape=(pltpu.SemaphoreType.DMA((3 * n,)), pltpu.SemaphoreType.DMA((3 * n,)))
        + tuple(pltpu.HBM(w.shape, w.dtype) for w in wb) + (_sds(TOKEN, F32),),
        input_output_aliases={i: 2 + i for i in range(n)},
        compiler_params=pltpu.CompilerParams(has_side_effects=EFFECT),
    )(*[pltpu.with_memory_space_constraint(w, pltpu.HBM) for w in wb], after)
    return outs[0], outs[1], list(outs[2:2 + n]), outs[2 + n]


def _gather_wait(send_sems, recv_sems, wb, after, idx, name):
    n = len(idx)

    def body(*refs):
        buf = refs[:n]
        send_sems, recv_sems = refs[n], refs[n + 1]
        x, y, c, chips = _place()
        me = 2 * x + y
        for j, (px, py) in enumerate(chips):
            for i in range(n):
                mine = _half(buf[i].at[me], idx[i], c)
                got = _half(buf[i].at[2 * px + py], idx[i], c)
                _remote(mine, mine, send_sems.at[j * n + i], recv_sems.at[j * n + i], (px, py, c)).wait_send()
                _remote(got, got, send_sems.at[j * n + i], recv_sems.at[j * n + i], (px, py, c)).wait_recv()

    outs = pl.pallas_call(
        body, name=name, in_specs=[HBM] * n + [SEM, SEM] + [ANY] * len(after), out_specs=(HBM,) * n,
        out_shape=tuple(pltpu.HBM(w.shape, w.dtype) for w in wb),
        input_output_aliases={i: i for i in range(n)},
        compiler_params=pltpu.CompilerParams(has_side_effects=EFFECT),
    )(*wb, send_sems, recv_sems, *after)
    return list(outs)


def _gather_forward(wb, idx, name, cw=None):
    n = len(idx)
    m = n if cw is None else n + 1

    def body(*refs):
        dst = refs[m:2 * m]
        send_sems, recv_sems = refs[2 * m], refs[2 * m + 1]
        x, y, c, chips = _place()
        cps = []
        for j, (px, py) in enumerate(chips):
            for i in range(n):
                got = _half(dst[i].at[2 * px + py], idx[i], c)
                cps.append(_remote(got, got, send_sems.at[j, i], recv_sems.at[j, i], (x, y, 1 - c)))
                cps[-1].start()
        if cw is not None:
            src_cw, loc_sem = refs[n], refs[2 * m + 2]
            me = 2 * x + y
            loc = pltpu.make_async_copy(src_cw, dst[n].at[me], loc_sem)
            loc.start()
            for j, (px, py) in enumerate(chips):
                cps.append(_remote(src_cw, dst[n].at[me], send_sems.at[j, n], recv_sems.at[j, n], (px, py, c)))
                cps[-1].start()
        for j, (px, py) in enumerate(chips):
            for i in range(n):
                got = _half(dst[i].at[2 * px + py], idx[i], 1 - c)
                _remote(got, got, send_sems.at[j, i], recv_sems.at[j, i], (x, y, 1 - c)).wait_recv()
            if cw is not None:
                got = dst[n].at[2 * px + py]
                _remote(got, got, send_sems.at[j, n], recv_sems.at[j, n], (px, py, c)).wait_recv()
        for cp in cps:
            cp.wait_send()
        if cw is not None:
            loc.wait()

    out_shape = [_sds(w.shape, BF16) for w in wb]
    scratch = [pltpu.SemaphoreType.DMA((3, m)), pltpu.SemaphoreType.DMA((3, m))]
    args = list(wb)
    if cw is not None:
        out_shape.append(_sds((4, CW_ROWS, SH_O), F32))
        scratch.append(pltpu.SemaphoreType.DMA)
        args.append(cw)
    return pl.pallas_call(
        body, name=name, in_specs=[ANY] * m, out_specs=[ANY] * m, out_shape=out_shape,
        input_output_aliases={i: i for i in range(n)}, scratch_shapes=scratch,
    )(*args)


def _forward_start(wb, cw, after, idx, name):
    n = len(idx)
    m = n if cw is None else n + 2

    def body(*refs):
        buf = refs[:n]
        send_sems, recv_sems = refs[m + 1], refs[m + 2]
        token = refs[2 * m + 3]
        x, y, c, chips = _place()
        for j, (px, py) in enumerate(chips):
            for i in range(n):
                got = _half(buf[i].at[2 * px + py], idx[i], c)
                _remote(got, got, send_sems.at[j * (n + 1) + i], recv_sems.at[j * (n + 1) + i], (x, y, 1 - c)).start()
            if cw is not None:
                _remote(refs[n], refs[n + 1].at[2 * x + y], send_sems.at[j * (n + 1) + n],
                        recv_sems.at[j * (n + 1) + n], (px, py, c)).start()
        token[...] = jnp.zeros(TOKEN, F32)

    arrays = list(wb) if cw is None else list(wb) + [cw, lax.empty((4, CW_ROWS, SH_O), F32)]
    outs = pl.pallas_call(
        body, name=name, in_specs=[HBM] * m + [ANY],
        out_specs=(SEM, SEM) + (HBM,) * m + (pl.BlockSpec(memory_space=pltpu.VMEM),),
        out_shape=(pltpu.SemaphoreType.DMA((3 * (n + 1),)), pltpu.SemaphoreType.DMA((3 * (n + 1),)))
        + tuple(pltpu.HBM(a.shape, a.dtype) for a in arrays) + (_sds(TOKEN, F32),),
        input_output_aliases={i: 2 + i for i in range(m)},
        compiler_params=pltpu.CompilerParams(has_side_effects=EFFECT),
    )(*[pltpu.with_memory_space_constraint(a, pltpu.HBM) for a in arrays], after)
    return outs[0], outs[1], list(outs[2:2 + m]), outs[2 + m]


def _forward_wait(send_sems, recv_sems, arrays, after, idx, with_cw, name):
    n = len(idx)
    m = len(arrays)

    def body(*refs):
        buf = refs[:n]
        send_sems, recv_sems = refs[m], refs[m + 1]
        x, y, c, chips = _place()
        for j, (px, py) in enumerate(chips):
            for i in range(n):
                sent = _half(buf[i].at[2 * px + py], idx[i], c)
                got = _half(buf[i].at[2 * px + py], idx[i], 1 - c)
                k = j * (n + 1) + i
                _remote(sent, sent, send_sems.at[k], recv_sems.at[k], (x, y, 1 - c)).wait_send()
                _remote(got, got, send_sems.at[k], recv_sems.at[k], (x, y, 1 - c)).wait_recv()
            if with_cw:
                k = j * (n + 1) + n
                theirs = refs[n + 1].at[2 * px + py]
                _remote(refs[n], theirs, send_sems.at[k], recv_sems.at[k], (px, py, c)).wait_send()
                _remote(refs[n], theirs, send_sems.at[k], recv_sems.at[k], (px, py, c)).wait_recv()

    outs = pl.pallas_call(
        body, name=name, in_specs=[HBM] * m + [SEM, SEM] + [ANY] * len(after), out_specs=(HBM,) * m,
        out_shape=tuple(pltpu.HBM(a.shape, a.dtype) for a in arrays),
        input_output_aliases={i: i for i in range(m)},
        compiler_params=pltpu.CompilerParams(has_side_effects=EFFECT),
    )(*arrays, send_sems, recv_sems, *after)
    return list(outs)


def _pair_exchange(gs, idx, name):
    n = len(idx)

    def body(*refs):
        src, dst = refs[:n], refs[n:2 * n]
        send_sems, recv_sems = refs[2 * n:]
        x, y, c, _ = _place()
        cps = []
        for i in range(n):
            h = BIG[idx[i]][2] // 2
            cps.append(_remote(src[i].at[:, :, pl.ds((1 - c) * h, h), :], dst[i], send_sems.at[i], recv_sems.at[i],
                               (x, y, 1 - c)))
            cps[-1].start()
        for cp in cps:
            cp.wait()

    return pl.pallas_call(
        body, name=name, in_specs=[ANY] * n, out_specs=[ANY] * n,
        out_shape=[_sds((4, BIG[w][1], BIG[w][2] // 2, BIG[w][3]), F32) for w in idx],
        scratch_shapes=[pltpu.SemaphoreType.DMA((n,)), pltpu.SemaphoreType.DMA((n,))],
    )(*gs)


def _pair_start(gs, idx, name):
    n = len(idx)

    def body(*refs):
        src, land = refs[:n], refs[n:2 * n]
        send_sems, recv_sems = refs[2 * n], refs[2 * n + 1]
        token = refs[4 * n + 2]
        x, y, c, _ = _place()
        for i in range(n):
            h = BIG[idx[i]][2] // 2
            _remote(src[i].at[:, :, pl.ds((1 - c) * h, h), :], land[i], send_sems.at[i], recv_sems.at[i],
                    (x, y, 1 - c)).start()
        token[...] = jnp.zeros(TOKEN, F32)

    lands = [lax.empty((4, BIG[w][1], BIG[w][2] // 2, BIG[w][3]), F32) for w in idx]
    arrays = list(gs) + lands
    outs = pl.pallas_call(
        body, name=name, in_specs=[HBM] * (2 * n),
        out_specs=(SEM, SEM) + (HBM,) * (2 * n) + (pl.BlockSpec(memory_space=pltpu.VMEM),),
        out_shape=(pltpu.SemaphoreType.DMA((n,)), pltpu.SemaphoreType.DMA((n,)))
        + tuple(pltpu.HBM(a.shape, a.dtype) for a in arrays) + (_sds(TOKEN, F32),),
        input_output_aliases={i: 2 + i for i in range(2 * n)},
        compiler_params=pltpu.CompilerParams(has_side_effects=EFFECT),
    )(*[pltpu.with_memory_space_constraint(a, pltpu.HBM) for a in arrays])
    return outs[0], outs[1], list(outs[2:2 + n]), list(outs[2 + n:2 + 2 * n]), outs[2 + 2 * n]


def _pair_wait(send_sems, recv_sems, gs, lands, after, idx, name):
    n = len(idx)

    def body(*refs):
        src, land = refs[:n], refs[n:2 * n]
        send_sems, recv_sems = refs[2 * n], refs[2 * n + 1]
        x, y, c, _ = _place()
        for i in range(n):
            h = BIG[idx[i]][2] // 2
            cp = _remote(src[i].at[:, :, pl.ds((1 - c) * h, h), :], land[i], send_sems.at[i], recv_sems.at[i],
                         (x, y, 1 - c))
            cp.wait_send()
            cp.wait_recv()

    arrays = list(gs) + list(lands)
    outs = pl.pallas_call(
        body, name=name, in_specs=[HBM] * (2 * n) + [SEM, SEM] + [ANY] * len(after), out_specs=(HBM,) * (2 * n),
        out_shape=tuple(pltpu.HBM(a.shape, a.dtype) for a in arrays),
        input_output_aliases={i: i for i in range(2 * n)},
        compiler_params=pltpu.CompilerParams(has_side_effects=EFFECT),
    )(*arrays, send_sems, recv_sems, *after)
    return list(outs[:n]), list(outs[n:])


def _pair_sum(place, g, r1, i):
    _, k, r, cdim = BIG[i]
    h = r // 2

    def body(pref, g_ref, r_ref, o_ref):
        o_ref[...] = (g_ref[...] + r_ref[...]).astype(BF16)

    grid_spec = pltpu.PrefetchScalarGridSpec(
        num_scalar_prefetch=1, grid=(4, k),
        in_specs=[pl.BlockSpec((1, 1, h, cdim), lambda s, t, pref: (s, t, pref[0], 0)),
                  pl.BlockSpec((1, 1, h, cdim), lambda s, t, pref: (s, t, 0, 0))],
        out_specs=pl.BlockSpec((1, 1, h, cdim), lambda s, t, pref: (s, t, 0, 0)))
    return pl.pallas_call(
        body, name=f"pair_sum_{BIG[i][0]}", grid_spec=grid_spec, out_shape=_sds((4, k, h, cdim), BF16),
        compiler_params=_params(("parallel", "parallel")),
    )(place, g, r1)


def _pair_sums(place, gs, r1s, idx, name):
    n = len(idx)
    dims = [(BIG[w][1], BIG[w][2] // 2, BIG[w][3]) for w in idx]

    def body(pref, *refs):
        for i in range(n):
            refs[2 * n + i][...] = (refs[i][...] + refs[n + i][...]).astype(BF16)

    mine = [pl.BlockSpec((1, k, h, cdim), lambda s, pref: (s, 0, pref[0], 0)) for k, h, cdim in dims]
    whole = [pl.BlockSpec((1, k, h, cdim), lambda s, pref: (s, 0, 0, 0)) for k, h, cdim in dims]
    grid_spec = pltpu.PrefetchScalarGridSpec(num_scalar_prefetch=1, grid=(4,), in_specs=mine + whole, out_specs=whole)
    return pl.pallas_call(
        body, name=name, grid_spec=grid_spec, out_shape=[_sds((4, k, h, cdim), BF16) for k, h, cdim in dims],
        compiler_params=_params(("parallel",)),
    )(place, *gs, *r1s)


def _chip_start(ps, idx, name):
    n = len(idx)

    def body(*refs):
        src, land = refs[:n], refs[n:2 * n]
        send_sems, recv_sems = refs[2 * n], refs[2 * n + 1]
        token = refs[4 * n + 2]
        x, y, c, chips = _place()
        for j, (px, py) in enumerate(chips):
            for i in range(n):
                _remote(src[i].at[2 * px + py], land[i].at[j], send_sems.at[j * n + i], recv_sems.at[j * n + i],
                        (px, py, c)).start()
        token[...] = jnp.zeros(TOKEN, F32)

    lands = [lax.empty((3,) + p.shape[1:], BF16) for p in ps]
    outs = pl.pallas_call(
        body, name=name, in_specs=[HBM] * (2 * n),
        out_specs=(SEM, SEM) + (HBM,) * (2 * n) + (pl.BlockSpec(memory_space=pltpu.VMEM),),
        out_shape=(pltpu.SemaphoreType.DMA((3 * n,)), pltpu.SemaphoreType.DMA((3 * n,)))
        + tuple(pltpu.HBM(a.shape, a.dtype) for a in list(ps) + lands) + (_sds(TOKEN, F32),),
        input_output_aliases={i: 2 + i for i in range(2 * n)},
        compiler_params=pltpu.CompilerParams(has_side_effects=EFFECT),
    )(*[pltpu.with_memory_space_constraint(a, pltpu.HBM) for a in list(ps) + lands])
    return outs[0], outs[1], list(outs[2:2 + n]), list(outs[2 + n:2 + 2 * n]), outs[2 + 2 * n]


def _chip_wait(send_sems, recv_sems, ps, lands, after, idx, name):
    n = len(idx)

    def body(*refs):
        src, land = refs[:n], refs[n:2 * n]
        send_sems, recv_sems = refs[2 * n], refs[2 * n + 1]
        x, y, c, chips = _place()
        for j, (px, py) in enumerate(chips):
            for i in range(n):
                cp = _remote(src[i].at[2 * px + py], land[i].at[j], send_sems.at[j * n + i], recv_sems.at[j * n + i],
                             (px, py, c))
                cp.wait_send()
                cp.wait_recv()

    arrays = list(ps) + list(lands)
    outs = pl.pallas_call(
        body, name=name, in_specs=[HBM] * (2 * n) + [SEM, SEM] + [ANY] * len(after), out_specs=(HBM,) * (2 * n),
        out_shape=tuple(pltpu.HBM(a.shape, a.dtype) for a in arrays),
        input_output_aliases={i: i for i in range(2 * n)},
        compiler_params=pltpu.CompilerParams(has_side_effects=EFFECT),
    )(*arrays, send_sems, recv_sems, *after)
    return list(outs[n:])


def _chip_sum(place, g, r1, r2, i):
    _, k, r, cdim = BIG[i]
    h = r // 2

    def body(pref, g_ref, r1_ref, r2_ref, o_ref):
        acc = g_ref[0, 0] + r1_ref[0, 0]
        for j in range(3):
            acc = acc + r2_ref[j, 0].astype(F32)
        o_ref[0] = acc

    grid_spec = pltpu.PrefetchScalarGridSpec(
        num_scalar_prefetch=1, grid=(k,),
        in_specs=[pl.BlockSpec((1, 1, h, cdim), lambda t, pref: (pref[1], t, pref[0], 0)),
                  pl.BlockSpec((1, 1, h, cdim), lambda t, pref: (pref[1], t, 0, 0)),
                  pl.BlockSpec((3, 1, h, cdim), lambda t, pref: (0, t, 0, 0))],
        out_specs=pl.BlockSpec((1, h, cdim), lambda t, pref: (t, pref[0], 0)))
    return pl.pallas_call(
        body, name=f"chip_sum_{BIG[i][0]}", grid_spec=grid_spec, out_shape=_sds((k, r, cdim), F32),
        compiler_params=_params(("parallel",)),
    )(place, g, r1, r2)


def _chip_sums(place, gs, r1s, r2s, idx, name):
    n = len(idx)
    dims = [(BIG[w][1], BIG[w][2] // 4, BIG[w][3]) for w in idx]

    def body(pref, *refs):
        for i in range(n):
            acc = refs[i][0] + refs[n + i][0]
            for j in range(3):
                acc = acc + refs[2 * n + i][j].astype(F32)
            refs[3 * n + i][...] = acc

    in_specs = ([pl.BlockSpec((1, k, q, cdim), lambda t, pref: (pref[1], 0, pref[0] * 2 + t, 0)) for k, q, cdim in dims]
                + [pl.BlockSpec((1, k, q, cdim), lambda t, pref: (pref[1], 0, t, 0)) for k, q, cdim in dims]
                + [pl.BlockSpec((3, k, q, cdim), lambda t, pref: (0, 0, t, 0)) for k, q, cdim in dims])
    out_specs = [pl.BlockSpec((k, q, cdim), lambda t, pref: (0, pref[0] * 2 + t, 0)) for k, q, cdim in dims]
    grid_spec = pltpu.PrefetchScalarGridSpec(num_scalar_prefetch=1, grid=(2,), in_specs=in_specs, out_specs=out_specs)
    return pl.pallas_call(
        body, name=name, grid_spec=grid_spec, out_shape=[_sds(BIG[w][1:], F32) for w in idx],
        compiler_params=_params(("parallel",)),
    )(place, *gs, *r1s, *r2s)


def _pair_gather(hs, idx, name):
    n = len(idx)

    def body(*refs):
        dst = refs[n:2 * n]
        send_sems, recv_sems = refs[2 * n:]
        x, y, c, _ = _place()
        cps = []
        for i in range(n):
            mine = _half(dst[i], idx[i], c)
            cps.append(_remote(mine, mine, send_sems.at[i], recv_sems.at[i], (x, y, 1 - c)))
            cps[-1].start()
        for i in range(n):
            theirs = _half(dst[i], idx[i], 1 - c)
            _remote(theirs, theirs, send_sems.at[i], recv_sems.at[i], (x, y, 1 - c)).wait_recv()
        for cp in cps:
            cp.wait_send()

    return pl.pallas_call(
        body, name=name, in_specs=[ANY] * n, out_specs=[ANY] * n,
        out_shape=[_sds(BIG[w][1:], F32) for w in idx],
        input_output_aliases={i: i for i in range(n)},
        scratch_shapes=[pltpu.SemaphoreType.DMA((n,)), pltpu.SemaphoreType.DMA((n,))],
    )(*hs)


SMALL_ROWS = 40


def _all_reduce_small(pack, after):
    def body(p_ref, o_ref, slots, send_sems, recv_sems):
        x, y, c, _ = _place()
        me = 4 * x + 2 * y + c
        cps = []
        for r in range(1, 8):
            peer = (x if not r & 4 else 1 - x, y if not r & 2 else 1 - y, c if not r & 1 else 1 - c)
            cps.append(_remote(p_ref, slots.at[r], send_sems.at[r - 1], recv_sems.at[r - 1], peer))
            cps[-1].start()
        slots[0] = p_ref[...]
        for cp in cps:
            cp.wait()
        acc = slots[me]
        for dev in range(1, 8):
            acc = acc + slots[jnp.bitwise_xor(me, dev)]
        o_ref[...] = acc

    vm = pl.BlockSpec(memory_space=pltpu.VMEM)
    return pl.pallas_call(
        functools.partial(_skip_arg, body, 1), name="all_reduce_small", in_specs=[vm, ANY], out_specs=vm,
        out_shape=_sds((SMALL_ROWS, D), F32),
        scratch_shapes=[pltpu.VMEM((8, SMALL_ROWS, D), F32), pltpu.SemaphoreType.DMA((7,)),
                        pltpu.SemaphoreType.DMA((7,))],
    )(pack, after)


def _adamw_math(w, g, m, v):
    m = ADAM_B1 * m + (1.0 - ADAM_B1) * g
    v = ADAM_B2 * v + (1.0 - ADAM_B2) * (g * g)
    m_hat = m / (1.0 - ADAM_B1 ** ADAM_STEP)
    v_hat = v / (1.0 - ADAM_B2 ** ADAM_STEP)
    delta = -ADAM_LR * (m_hat / (jnp.sqrt(v_hat) + ADAM_EPS) + ADAM_WD * w)
    return delta, m, v


def _adamw_big(w, g, m, v, i):
    _, k, r, cdim = BIG[i]
    nblk = 4 if k == 1 else 1

    def body(w_ref, g_ref, m_ref, v_ref, d_ref, nm_ref, nv_ref, go_ref):
        gv = g_ref[...]
        d_ref[...], nm_ref[...], nv_ref[...] = _adamw_math(w_ref[...], gv, m_ref[...], v_ref[...])
        go_ref[...] = gv

    spec = pl.BlockSpec((1, r // nblk, cdim), lambda t, b: (t, b, 0))
    return pl.pallas_call(
        body, name=f"adamw_{BIG[i][0]}", grid=(k, nblk), in_specs=[spec] * 4, out_specs=[spec] * 4,
        out_shape=[_sds((k, r, cdim), F32)] * 4,
        compiler_params=_params(("parallel", "parallel")),
    )(w, g, m, v)


def _small_start(pack, after):
    def body(pack_ref, land_ref, after_ref, send_sems, recv_sems, pack_thru, land_thru, token):
        x, y, c, _ = _place()
        for r in range(1, 8):
            peer = (x if not r & 4 else 1 - x, y if not r & 2 else 1 - y, c if not r & 1 else 1 - c)
            _remote(pack_ref, land_ref.at[r - 1], send_sems.at[r - 1], recv_sems.at[r - 1], peer).start()
        token[...] = jnp.zeros(TOKEN, F32)

    land = lax.empty((7, SMALL_ROWS, D), F32)
    outs = pl.pallas_call(
        body, name="small_start", in_specs=[HBM, HBM, ANY],
        out_specs=(SEM, SEM, HBM, HBM, pl.BlockSpec(memory_space=pltpu.VMEM)),
        out_shape=(pltpu.SemaphoreType.DMA((7,)), pltpu.SemaphoreType.DMA((7,)), pltpu.HBM(pack.shape, F32),
                   pltpu.HBM(land.shape, F32), _sds(TOKEN, F32)),
        input_output_aliases={0: 2, 1: 3},
        compiler_params=pltpu.CompilerParams(has_side_effects=EFFECT),
    )(pltpu.with_memory_space_constraint(pack, pltpu.HBM), pltpu.with_memory_space_constraint(land, pltpu.HBM), after)
    return outs


def _small_wait(send_sems, recv_sems, pack, land, after):
    def body(pack_ref, land_ref, send_sems, recv_sems, *rest):
        x, y, c, _ = _place()
        for r in range(1, 8):
            peer = (x if not r & 4 else 1 - x, y if not r & 2 else 1 - y, c if not r & 1 else 1 - c)
            cp = _remote(pack_ref, land_ref.at[r - 1], send_sems.at[r - 1], recv_sems.at[r - 1], peer)
            cp.wait_send()
            cp.wait_recv()

    return pl.pallas_call(
        body, name="small_wait", in_specs=[HBM, HBM, SEM, SEM] + [ANY] * len(after), out_specs=(HBM, HBM),
        out_shape=(pltpu.HBM(pack.shape, F32), pltpu.HBM(land.shape, F32)),
        input_output_aliases={0: 0, 1: 1},
        compiler_params=pltpu.CompilerParams(has_side_effects=EFFECT),
    )(pack, land, send_sems, recv_sems, *after)


def _small_update(place, pack, land, ws, ms, vs):
    n = len(ws)

    def body(pref, pack_ref, land_ref, *refs):
        chip = pref[1]
        me = 2 * chip + pref[0]
        own = pack_ref[...]
        tot = None
        for dev in range(8):
            r = jnp.bitwise_xor(me, dev)
            term = jnp.where(r == 0, own, land_ref[jnp.maximum(r - 1, 0)])
            tot = term if tot is None else tot + term
        out, buf = refs[3 * n:-1], refs[-1]
        buf[...] = tot
        g_conv = jnp.zeros((3, SH_O), F32)
        for s in range(4):
            g_conv = g_conv + jnp.where(chip == s, buf[24:27, s * SH_O:(s + 1) * SH_O], 0.0)
        gs = [buf[0:2, :], buf[8:10, :], buf[16:17, :], g_conv]
        out[0][...] = buf[32:33, 0:128]
        for i in range(n):
            d, nm, nv = _adamw_math(refs[i][...], gs[i], refs[n + i][...], refs[2 * n + i][...])
            out[1 + i][...] = gs[i]
            out[1 + n + i][...] = d
            out[1 + 2 * n + i][...] = nm
            out[1 + 3 * n + i][...] = nv

    def full(shape):
        nd = len(shape)
        return pl.BlockSpec(shape, lambda i, pref: (0,) * nd)

    specs = [full(w.shape) for w in ws]
    grid_spec = pltpu.PrefetchScalarGridSpec(
        num_scalar_prefetch=1, grid=(1,),
        in_specs=[full(pack.shape), full(land.shape)] + specs * 3, out_specs=[full((1, 128))] + specs * 4,
        scratch_shapes=[pltpu.VMEM((SMALL_ROWS, D), F32)])
    outs = pl.pallas_call(
        body, name="small_update", grid_spec=grid_spec,
        out_shape=[_sds((1, 128), F32)] + [_sds(w.shape, F32) for w in ws] * 4,
        compiler_params=_params(("arbitrary",)),
    )(place, pack, land, *ws, *ms, *vs)
    return outs[0], outs[1:1 + n], outs[1 + n:1 + 2 * n], outs[1 + 2 * n:1 + 3 * n], outs[1 + 3 * n:]


def _adamw_layer(ws, gs, ms, vs, idx, name):
    n = len(idx)
    dims = [(BIG[w][1], BIG[w][2] // 4, BIG[w][3]) for w in idx]

    def body(*refs):
        for i in range(n):
            gv = refs[n + i][...]
            d, nm, nv = _adamw_math(refs[i][...], gv, refs[2 * n + i][...], refs[3 * n + i][...])
            refs[4 * n + i][...] = d
            refs[5 * n + i][...] = nm
            refs[6 * n + i][...] = nv
            refs[7 * n + i][...] = gv

    specs = [pl.BlockSpec((k, q, cdim), lambda t: (0, t, 0)) for k, q, cdim in dims]
    outs = pl.pallas_call(
        body, name=name, grid=(4,), in_specs=specs * 4, out_specs=specs * 4,
        out_shape=[_sds(BIG[w][1:], F32) for w in idx] * 4,
        compiler_params=_params(("parallel",)),
    )(*ws, *gs, *ms, *vs)
    return [tuple(outs[j * n + i] for j in range(4)) for i in range(n)]


def _adamw_small(ws, gs, ms, vs):
    n = len(ws)

    def body(*refs):
        for i in range(n):
            w_ref, g_ref, m_ref, v_ref = refs[i], refs[n + i], refs[2 * n + i], refs[3 * n + i]
            d, nm, nv = _adamw_math(w_ref[...], g_ref[...], m_ref[...], v_ref[...])
            refs[4 * n + i][...] = d
            refs[5 * n + i][...] = nm
            refs[6 * n + i][...] = nv

    specs = [_full(w.shape) for w in ws]
    outs = pl.pallas_call(
        body, name="adamw_small", grid=(1,), in_specs=specs * 4, out_specs=specs * 3,
        out_shape=[_sds(w.shape, F32) for w in ws] * 3,
        compiler_params=_params(("arbitrary",)),
    )(*ws, *gs, *ms, *vs)
    return outs[:n], outs[n:2 * n], outs[2 * n:]


def _pad_rows(a, rows):
    return jnp.pad(a, ((0, rows - a.shape[0]), (0, 0)))


def kernel(x, mem, positions, norm_g, mem_norm_g, w_mem_kv, attn_w_in, attn_w_out, conv_w_in, conv_w, conv_w_out, final_g, loss_target, m_norm_g, m_mem_norm_g, m_w_mem_kv, m_attn_w_in, m_attn_w_out, m_conv_w_in, m_conv_w, m_conv_w_out, m_final_g, v_norm_g, v_mem_norm_g, v_w_mem_kv, v_attn_w_in, v_attn_w_out, v_conv_w_in, v_conv_w, v_conv_w_out, v_final_g):
    mx, my, mc = lax.axis_index("x"), lax.axis_index("y"), lax.axis_index("c")
    place = jnp.stack([mc, 2 * mx + my]).astype(jnp.int32)

    w_big = [w_mem_kv, attn_w_in, attn_w_out, conv_w_in, conv_w_out]
    m_big = [m_w_mem_kv, m_attn_w_in, m_attn_w_out, m_conv_w_in, m_conv_w_out]
    v_big = [v_w_mem_kv, v_attn_w_in, v_attn_w_out, v_conv_w_in, v_conv_w_out]
    first, rest = (1,), (0, 2, 3, 4)
    wb1 = _cast_weights(place, [w_big[i] for i in first], place, first, "cast_w_in_a")
    a1_send, a1_recv, a1_bufs, a1_token = _gather_start(wb1, place, first, "gather_a1_start")
    wbr = _cast_weights(place, [w_big[i] for i in rest], a1_token, rest, "cast_weights")
    rest = (0, 2)
    a2_send, a2_recv, a2_bufs, a2_token = _gather_start([wbr[0], wbr[1]], a1_token, rest, "gather_a2_start")
    gb_send, gb_recv, gb_bufs, gb_token = _gather_start([wbr[2], wbr[3]], a2_token, LAYER_B, "gather_b_start")

    xs, tgt = x[0], loss_target[0]
    g0, g1 = norm_g[0:1], norm_g[1:2]
    rc, rs1, rs2 = _rope_tables(positions[0].astype(F32).reshape(S, 1), gb_token)
    a1_bufs = _gather_wait(a1_send, a1_recv, a1_bufs, [rc], first, "gather_a1_wait")
    w_in_a = _gather_forward(a1_bufs, first, "gather_a1_forward")[0].reshape(4, D, SH_A)
    hn0, q, k, v, qm0, z0 = _in_proj_a(xs, g0, w_in_a, rc, rs1, rs2, gb_token)
    a2_bufs = _gather_wait(a2_send, a2_recv, a2_bufs, [q], rest, "gather_a2_wait")
    f2_send, f2_recv, a2_bufs, f2_token = _forward_start(a2_bufs, None, q, rest, "forward_a2_start")
    fwd = [_attn_fwd(q, k, v, 0, f2_token)]
    fwd.append(_attn_fwd(q, k, v, 1, fwd[0][0]))
    cw_own = _pad_rows(conv_w[0], CW_ROWS)
    gb_bufs = _gather_wait(gb_send, gb_recv, gb_bufs, [fwd[1][0]], LAYER_B, "gather_b_wait")
    fb_send, fb_recv, gb_bufs, fb_token = _forward_start(gb_bufs, cw_own, fwd[1][0], LAYER_B, "forward_b_start")
    fwd.append(_attn_fwd(q, k, v, 2, fb_token))
    os_, ls, lss = [f[0] for f in fwd], [f[1] for f in fwd], [f[2] for f in fwd]
    wkv_f, w_out_a = _forward_wait(f2_send, f2_recv, a2_bufs, [os_[2]], rest, False, "forward_a2_wait")
    w_out_a = w_out_a.reshape(4, BR_A, SH_O)
    memn, kv = _mem_fwd(mem[0], mem_norm_g, wkv_f)
    h1 = _attn_out(os_, ls, qm0, kv[0], z0, xs, w_out_a)

    w_in_b, w_out_b, _, cw_f = _forward_wait(fb_send, fb_recv, gb_bufs, [h1], LAYER_B, True, "forward_b_wait")
    w_in_b = w_in_b.reshape(4, D, SH_B)
    w_out_b = w_out_b.reshape(BR_B, D)
    cw_f = lax.dynamic_update_slice(cw_f, cw_own[None], (2 * mx + my, 0, 0))
    cw8 = cw_f.transpose(1, 0, 2).reshape(CW_ROWS, D)
    hn1, bg, cg, u, qm1, z1 = _in_proj_b(h1, g1, w_in_b)
    dh2, loss_part, dfg = _conv_out_loss(bg, cg, u, cw8, qm1, kv[1], z1, h1, w_out_b, final_g.reshape(1, D), tgt)

    dproj_b, dw_out_b, dcw, dkv1 = _conv_bwd(dh2, bg, cg, u, cw8, qm1, kv[1], z1, w_out_b)
    dw_in_b = _w_in_grad(hn1, dproj_b, IN_B, "w_in_b_grad")
    gs_b = [dw_in_b.reshape(4, 1, D, SH_B), dw_out_b.reshape(4, 1, BR_B // 4, D)]
    pb_send, pb_recv, gs_b, pb_land, pb_token = _pair_start(gs_b, LAYER_B, "pair_b_start")
    dh1, dg1 = _in_proj_bwd(dproj_b, w_in_b, h1, g1, dh2, pb_token, IN_B, "in_proj_b_bwd")
    gs_b, r1_b = _pair_wait(pb_send, pb_recv, gs_b, pb_land, [dh1], LAYER_B, "pair_b_wait")
    ps_b = _pair_sums(place, gs_b, r1_b, LAYER_B, "pair_sums_b")
    cb_send, cb_recv, cb_src, cb_land, cb_token = _chip_start(ps_b, LAYER_B, "chip_b_start")

    outs = _attn_out_bwd(dh1, os_, ls, qm0, kv[0], z0, w_out_a, cb_token)
    dos, dds, dqm, dz, dw_out_a, dkv0 = outs[0:3], outs[3:6], outs[6], outs[7], outs[8], outs[9]
    bwd = [_attn_bwd(q, k, v, dos[g], lss[g], dds[g], g) for g in range(3)]
    dproj_a = _qkv_bwd([b[0] for b in bwd], [b[1] for b in bwd], [b[2] for b in bwd], dqm, dz, rc, rs1, rs2)
    dw_in_a = _w_in_grad(hn0, dproj_a, IN_A, "w_in_a_grad")
    dwkv, dmg = _mem_bwd(mem[0], mem_norm_g, memn, wkv_f, dkv0, dkv1)

    gs_a = [dwkv, dw_in_a.reshape(4, 1, D, SH_A), dw_out_a.reshape(4, 1, BR_A, SH_O)]
    r1_a = _pair_exchange(gs_a, LAYER_A, "pair_exchange_a")
    ps_a = _pair_sums(place, gs_a, r1_a, LAYER_A, "pair_sums_a")
    ca_send, ca_recv, ca_src, ca_land, ca_token = _chip_start(ps_a, LAYER_A, "chip_a_start")

    gx, dg0 = _in_proj_bwd(dproj_a, w_in_a, xs, g0, dh1, ca_token, IN_A, "in_proj_a_bwd")
    pack = jnp.concatenate([_pad_rows(jnp.concatenate([dg0, dg1], axis=0), 8), _pad_rows(dmg, 8), _pad_rows(dfg, 8),
                            dcw, _pad_rows(jnp.pad(loss_part, ((0, 0), (0, D - 128))), 8)], axis=0)
    sm_send, sm_recv, pack, sm_land, sm_token = _small_start(pack, ca_token)
    r2_b = _chip_wait(cb_send, cb_recv, cb_src, cb_land, [ca_token], LAYER_B, "chip_b_wait")
    hs_b = _chip_sums(place, gs_b, r1_b, r2_b, LAYER_B, "chip_sums_b")
    g_b = _pair_gather(hs_b, LAYER_B, "pair_gather_b")
    upd_b = _adamw_layer([w_big[w] for w in LAYER_B], g_b, [m_big[w] for w in LAYER_B], [v_big[w] for w in LAYER_B],
                         LAYER_B, "adamw_b")
    r2_a = _chip_wait(ca_send, ca_recv, ca_src, ca_land, [gx, upd_b[0][0], upd_b[1][0], sm_token], LAYER_A,
                      "chip_a_wait")
    hs_a = _chip_sums(place, gs_a, r1_a, r2_a, LAYER_A, "chip_sums_a")
    g_a = _pair_gather(hs_a, LAYER_A, "pair_gather_a")
    upd_a = _adamw_layer([w_big[w] for w in LAYER_A], g_a, [m_big[w] for w in LAYER_A], [v_big[w] for w in LAYER_A],
                         LAYER_A, "adamw_a")
    upd = upd_a + upd_b
    g_big = [u[3] for u in upd]
    pack, sm_land = _small_wait(sm_send, sm_recv, pack, sm_land, [r2_a[0]])
    sw = [norm_g, mem_norm_g, final_g.reshape(1, D), conv_w[0]]
    sm = [m_norm_g, m_mem_norm_g, m_final_g.reshape(1, D), m_conv_w[0]]
    sv = [v_norm_g, v_mem_norm_g, v_final_g.reshape(1, D), v_conv_w[0]]
    loss_row, sg, sd, snm, snv = _small_update(place, pack, sm_land, sw, sm, sv)
    loss = loss_row[0, 0]
    g_norm, g_memnorm, g_final, g_conv = sg

    def order(norm, memnorm, wkv, w_in_a, w_out_a, w_in_b, conv, w_out_b, final):
        return (norm, memnorm, wkv, w_in_a, w_out_a, w_in_b, conv.reshape(1, 3, SH_O), w_out_b, final.reshape(D))

    grads = order(g_norm, g_memnorm, g_big[0], g_big[1], g_big[2], g_big[3], g_conv, g_big[4], g_final)
    deltas = order(sd[0], sd[1], upd[0][0], upd[1][0], upd[2][0], upd[3][0], sd[3], upd[4][0], sd[2])
    new_m = order(snm[0], snm[1], upd[0][1], upd[1][1], upd[2][1], upd[3][1], snm[3], upd[4][1], snm[2])
    new_v = order(snv[0], snv[1], upd[0][2], upd[1][2], upd[2][2], upd[3][2], snv[3], upd[4][2], snv[2])
    return (loss, gx[None], *grads, *deltas, *new_m, *new_v)
```

```python
import functools

import numpy as np
import jax
import jax.numpy as jnp
from jax import lax
from jax.experimental import pallas as pl
from jax.experimental.pallas import tpu as pltpu

F32 = jnp.float32
BF16 = jnp.bfloat16

S = 2048
D = 1024
TM = 256
NT = S // TM
HD = 64
GW = 512
NQ = 3 * GW
MW = 256
NM = 256
IN_A = 3 * NQ + MW + GW + MW
IN_B = 3 * D + MW + D + MW
BR_A = GW + MW
BR_B = D + MW
SH_A = IN_A // 4
SH_B = IN_B // 4
SH_O = D // 4
QBLK = 128
DILATIONS = (1, 4, 16)
EPS = 1e-6
SCALE = HD ** -0.5
NEG = -1e30
ROPE_THETA = 500000.0

ADAM_LR = 0.001
ADAM_B1 = 0.9
ADAM_B2 = 0.999
ADAM_EPS = 1e-08
ADAM_WD = 0.01
ADAM_STEP = 10

VMEM_LIMIT_BYTES = 60 * 1024 * 1024


def _params(sem=None):
    if sem is None:
        return pltpu.CompilerParams(vmem_limit_bytes=VMEM_LIMIT_BYTES)
    return pltpu.CompilerParams(dimension_semantics=sem, vmem_limit_bytes=VMEM_LIMIT_BYTES)


def _full(shape):
    nd = len(shape)
    return pl.BlockSpec(shape, lambda *_: (0,) * nd)


def _rows(width, tm=TM):
    return pl.BlockSpec((tm, width), lambda i: (i, 0))


def _sds(shape, dtype):
    return jax.ShapeDtypeStruct(shape, dtype)


def _silu_parts(z):
    sig = 0.5 * jnp.tanh(0.5 * z) + 0.5
    return z * sig, sig * (1.0 + z * (1.0 - sig))


def _dot(a, b):
    return jnp.dot(a, b, preferred_element_type=F32)


def _dot_nt(a, b):
    return lax.dot_general(a, b, (((1,), (1,)), ((), ())), preferred_element_type=F32)


def _dot_tn(a, b):
    return lax.dot_general(a, b, (((0,), (0,)), ((), ())), preferred_element_type=F32)


def _rope_fwd(t, c, s1, s2):
    return t * c + pltpu.roll(t, 120, 1) * s1 + pltpu.roll(t, 8, 1) * s2


def _rope_bwd(g, c, s1, s2):
    return g * c + pltpu.roll(g * s1, 8, 1) + pltpu.roll(g * s2, 120, 1)


MEM_HEADS = MW // HD


def _stack_heads(x):
    head = lax.broadcasted_iota(jnp.int32, x.shape, 1) // HD
    return jnp.concatenate([jnp.where(head == h, x, 0.0) for h in range(MEM_HEADS)], axis=0).astype(BF16)


def _unstack_heads(x4):
    tm = x4.shape[0] // MEM_HEADS
    head = lax.broadcasted_iota(jnp.int32, (tm, MW), 1) // HD
    out = x4[:tm]
    for h in range(1, MEM_HEADS):
        out = jnp.where(head == h, x4[h * tm:(h + 1) * tm], out)
    return out


def _mem_attn(qm, kv):
    q4 = _stack_heads(qm.astype(F32))
    s = _dot_nt(q4, kv[:, :MW]) * SCALE
    e = jnp.exp(s - jnp.max(s, axis=-1, keepdims=True))
    p = e * (1.0 / jnp.sum(e, axis=-1, keepdims=True))
    return p, _unstack_heads(_dot(p.astype(BF16), kv[:, MW:])), q4


def _mem_attn_bwd(dmo, p, mo, q4, kv, dkv_ref):
    tm = dmo.shape[0]
    head = lax.broadcasted_iota(jnp.int32, dmo.shape, 1) // HD
    prod = dmo * mo
    delta = jnp.concatenate([jnp.sum(jnp.where(head == h, prod, 0.0), axis=-1, keepdims=True)
                             for h in range(MEM_HEADS)], axis=0)
    d4 = _stack_heads(dmo)
    ds = (p * (_dot_nt(d4, kv[:, MW:]) - delta) * SCALE).astype(BF16)
    dkv_ref[:, :MW] += _dot_tn(ds, q4)
    dkv_ref[:, MW:] += _dot_tn(p.astype(BF16), d4)
    return _unstack_heads(_dot(ds, kv[:, :MW]))


def _merge(o_refs, l_refs):
    ls = [r[...] for r in l_refs]
    m = jnp.maximum(jnp.maximum(ls[0], ls[1]), ls[2])
    es = [jnp.exp(l - m) for l in ls]
    inv = 1.0 / (es[0] + es[1] + es[2])
    ws = [e * inv for e in es]
    os_ = [r[...] for r in o_refs]
    mix = ws[0] * os_[0] + ws[1] * os_[1] + ws[2] * os_[2]
    return ws, mix


def _conv_taps(cg, u, cgp, up, first):
    a = cg * u
    ap = jnp.where(first, 0.0, cgp * up)
    row = lax.broadcasted_iota(jnp.int32, a.shape, 0)
    a1 = jnp.where(row == 0, ap[7:8, :], pltpu.roll(a, 1, 0))
    a2 = jnp.where(row == 0, ap[6:7, :], jnp.where(row == 1, ap[7:8, :], pltpu.roll(a, 2, 0)))
    return a, a1, a2


def _rope_tables(posf, after):
    half = 8
    invf = np.float32(ROPE_THETA) ** (-np.arange(half, dtype=np.float32) * np.float32(2.0 / 16))
    lane = np.arange(128)
    table = np.where((lane % HD) < 16, invf[lane % half], 0.0).astype(np.float32)[None, :]

    def body(pos_ref, invf_ref, c_ref, s1_ref, s2_ref):
        ang = pos_ref[...] * invf_ref[...]
        jm = lax.broadcasted_iota(jnp.int32, ang.shape, 1) & (HD - 1)
        cs = jnp.cos(ang)
        sn = jnp.sin(ang)
        c_ref[...] = jnp.where(jm < 16, cs, 1.0)
        s1_ref[...] = jnp.where(jm < 8, -sn, 0.0)
        s2_ref[...] = jnp.where((jm >= 8) & (jm < 16), sn, 0.0)

    out = _sds((S, 128), F32)
    return pl.pallas_call(
        functools.partial(_skip_arg, body, 2), name="rope_tables", grid=(NT,),
        in_specs=[_rows(1), _full((1, 128)), pl.BlockSpec(memory_space=pl.ANY)],
        out_specs=[_rows(128)] * 3, out_shape=[out] * 3,
        compiler_params=_params(("parallel",)),
    )(posf, jnp.asarray(table), after)


def _in_proj_a(x, g0, w_in, c, s1, s2, after):
    def body(x_ref, g_ref, w_ref, c_ref, s1_ref, s2_ref, hn_ref, q_ref, k_ref, v_ref, qm_ref, z_ref, proj):
        xf = x_ref[...]
        hn = xf * lax.rsqrt(jnp.mean(xf * xf, axis=-1, keepdims=True) + EPS) * g_ref[...]
        hb = hn.astype(BF16)
        hn_ref[...] = hb
        for s in range(4):
            proj[:, s * SH_A:(s + 1) * SH_A] = _dot(hb, w_ref[s])
        cc, a1, a2 = c_ref[...], s1_ref[...], s2_ref[...]
        for j in range(NQ // 128):
            q_ref[:, j * 128:(j + 1) * 128] = (
                _rope_fwd(proj[:, j * 128:(j + 1) * 128], cc, a1, a2) * SCALE).astype(BF16)
            k_ref[:, j * 128:(j + 1) * 128] = _rope_fwd(
                proj[:, NQ + j * 128:NQ + (j + 1) * 128], cc, a1, a2).astype(BF16)
        v_ref[...] = proj[:, 2 * NQ:3 * NQ].astype(BF16)
        qm_ref[...] = proj[:, 3 * NQ:3 * NQ + MW].astype(BF16)
        z_ref[...] = proj[:, 3 * NQ + MW:]

    return pl.pallas_call(
        functools.partial(_skip_arg, body, 6), name="in_proj_a", grid=(NT,),
        in_specs=[_rows(D), _full((1, D)), _full((4, D, SH_A)), _rows(128), _rows(128), _rows(128),
                  pl.BlockSpec(memory_space=pl.ANY)],
        out_specs=[_rows(D), _rows(NQ), _rows(NQ), _rows(NQ), _rows(MW), _rows(BR_A)],
        out_shape=[_sds((S, D), BF16), _sds((S, NQ), BF16), _sds((S, NQ), BF16), _sds((S, NQ), BF16),
                   _sds((S, MW), BF16), _sds((S, BR_A), F32)],
        scratch_shapes=[pltpu.VMEM((TM, IN_A), F32)],
        compiler_params=_params(("parallel",)),
    )(x, g0, w_in, c, s1, s2, after)


def _mem_fwd(mem, mg, wkv):
    def body(mem_ref, mg_ref, w_ref, memn_ref, kv_ref):
        mf = mem_ref[...]
        n = mf * lax.rsqrt(jnp.mean(mf * mf, axis=-1, keepdims=True) + EPS)
        for i in range(2):
            mn = (n * mg_ref[i:i + 1, :]).astype(BF16)
            memn_ref[i] = mn
            acc = _dot(mn[:, 0:NM], w_ref[0, i])
            for s in range(1, 4):
                acc += _dot(mn[:, s * NM:(s + 1) * NM], w_ref[s, i])
            kv_ref[i] = acc.astype(BF16)

    return pl.pallas_call(
        body, name="mem_fwd", grid=(1,),
        in_specs=[_full((NM, D)), _full((2, D)), _full((4, 2, NM, 2 * MW))],
        out_specs=[_full((2, NM, D)), _full((2, NM, 2 * MW))],
        out_shape=[_sds((2, NM, D), BF16), _sds((2, NM, 2 * MW), BF16)],
        compiler_params=_params(("arbitrary",)),
    )(mem, mg, wkv)


def _band_mask(j):
    qi = lax.broadcasted_iota(jnp.int32, (QBLK, 2 * QBLK), 0)
    kj = lax.broadcasted_iota(jnp.int32, (QBLK, 2 * QBLK), 1)
    dist = qi + QBLK - kj
    return (dist >= 0) & (dist <= QBLK) & ((kj >= QBLK) | (j > 0))


LANES = 128
NCHUNK = GW // LANES
FWD_UNROLL = 16
BWD_UNROLL = 4


def _perm_matrix(d):
    n = TM // d
    p = np.zeros((TM, TM), np.float32)
    for r in range(d):
        for i in range(n):
            p[r * n + i, i * d + r] = 1.0
    return p


def _split_dot(p, x, parts):
    hi = x.astype(BF16)
    rem = x - hi.astype(F32)
    lo = rem.astype(BF16)
    both = _dot(p, jnp.concatenate([hi, lo], axis=1))
    acc = both[:, :LANES] + both[:, LANES:]
    if parts == 3:
        acc = acc + _dot(p, (rem - lo.astype(F32)).astype(BF16))
    return acc


def _pair_dot(p, a, b):
    both = _dot(p, jnp.concatenate([a, b], axis=1))
    return both[:, :LANES], both[:, LANES:]


def _tile_to_streams(y, dst, t, d):
    n, ln = TM // d, S // d
    for r in range(d):
        dst[r * ln + t * n:r * ln + (t + 1) * n, :] = y[r * n:(r + 1) * n].astype(dst.dtype)


def _tile_from_streams(src, t, d):
    n, ln = TM // d, S // d
    return jnp.concatenate([src[r * ln + t * n:r * ln + (t + 1) * n, :] for r in range(d)], axis=0)


def _head_masks():
    first = lax.broadcasted_iota(jnp.int32, (TM, LANES), 1) < HD
    return first, jnp.logical_not(first)


def _attn_fwd(q, k, v, g, after):
    d = DILATIONS[g]
    nb = S // d // QBLK
    perm = _perm_matrix(d)

    def body(q_ref, k_ref, v_ref, p_ref, pt_ref, o_ref, l_ref, ls_ref, q0, q1, ks, vs, os_):
        first, second = _head_masks()
        pm = p_ref[...]
        for t in range(NT):
            rows = slice(t * TM, (t + 1) * TM)
            if d == 1:
                qt = q_ref[rows, :].astype(F32)
            else:
                qt, kt = _pair_dot(pm, q_ref[rows, :], k_ref[rows, :])
                _tile_to_streams(kt, ks, t, d)
                _tile_to_streams(_dot(pm, v_ref[rows, :]), vs, t, d)
            _tile_to_streams(jnp.where(first, qt, 0.0), q0, t, d)
            _tile_to_streams(jnp.where(second, qt, 0.0), q1, t, d)
        kref, vref = (k_ref, v_ref) if d == 1 else (ks, vs)
        oref, lref = (o_ref, l_ref) if d == 1 else (os_, ls_ref)

        def blk(b, carry):
            r0 = pl.multiple_of(b * QBLK, QBLK)
            p0 = pl.multiple_of(jnp.maximum(b - 1, 0) * QBLK, QBLK)
            kk = jnp.concatenate([kref[pl.ds(p0, QBLK), :], kref[pl.ds(r0, QBLK), :]], axis=0)
            vv = jnp.concatenate([vref[pl.ds(p0, QBLK), :], vref[pl.ds(r0, QBLK), :]], axis=0)
            valid = _band_mask(b & (nb - 1))
            acc, lse = [], []
            for qh in (q0, q1):
                s = jnp.where(valid, _dot_nt(qh[pl.ds(r0, QBLK), :], kk), NEG)
                m = jnp.max(s, axis=-1, keepdims=True)
                e = jnp.exp(s - m)
                l = jnp.sum(e, axis=-1, keepdims=True)
                acc.append(_dot(e.astype(BF16), vv) * (1.0 / l))
                lse.append(m + jnp.log(l))
            f = first[:QBLK]
            oref[pl.ds(r0, QBLK), :] = jnp.where(f, acc[0], acc[1])
            lref[pl.ds(r0, QBLK), :] = jnp.where(f, lse[0], lse[1])
            return carry

        lax.fori_loop(0, S // QBLK, blk, 0, unroll=FWD_UNROLL)
        if d > 1:
            ptm = pt_ref[...]
            for t in range(NT):
                rows = slice(t * TM, (t + 1) * TM)
                o_ref[rows, :] = _split_dot(ptm, _tile_from_streams(os_, t, d), 2)
                l_ref[rows, :] = _split_dot(ptm, _tile_from_streams(ls_ref, t, d), 3)

    qkv_spec = pl.BlockSpec((S, LANES), lambda c: (0, g * NCHUNK + c))
    out_spec = pl.BlockSpec((S, LANES), lambda c: (0, c))
    n_out = 2 if d == 1 else 3
    inner = body if d > 1 else functools.partial(_drop_arg, body, 7)
    outs = pl.pallas_call(
        functools.partial(_skip_arg, inner, 5), name=f"attn_fwd_g{g}", grid=(NCHUNK,),
        in_specs=[qkv_spec] * 3 + [_full((TM, TM))] * 2 + [pl.BlockSpec(memory_space=pl.ANY)],
        out_specs=[out_spec] * n_out, out_shape=[_sds((S, GW), F32)] * n_out,
        scratch_shapes=[pltpu.VMEM((S, LANES), BF16)] * 4 + [pltpu.VMEM((S, LANES), F32)],
        compiler_params=_params(("parallel",)),
    )(q, k, v, jnp.asarray(perm, BF16), jnp.asarray(perm.T, BF16), after)
    return (outs[0], outs[1], outs[1]) if d == 1 else tuple(outs)


def _drop_arg(body, pos, *refs):
    return body(*refs[:pos], None, *refs[pos:])


def _attn_out(os_, ls, qm, kv0, z, x, w_out):
    def body(o0, o1, o2, l0, l1, l2, qm_ref, kv_ref, z_ref, x_ref, w_ref, h_ref, ybuf):
        _, mix = _merge((o0, o1, o2), (l0, l1, l2))
        sz, _ = _silu_parts(z_ref[...])
        ybuf[:, :GW] = (mix * sz[:, :GW]).astype(BF16)
        _, mo, _ = _mem_attn(qm_ref[...], kv_ref[...])
        ybuf[:, GW:] = (mo * sz[:, GW:]).astype(BF16)
        yb = ybuf[...]
        for s in range(4):
            cs = slice(s * SH_O, (s + 1) * SH_O)
            h_ref[:, cs] = x_ref[:, cs] + _dot(yb, w_ref[s])

    return pl.pallas_call(
        body, name="attn_out", grid=(NT,),
        in_specs=[_rows(GW)] * 6 + [_rows(MW), _full((NM, 2 * MW)), _rows(BR_A), _rows(D), _full((4, BR_A, SH_O))],
        out_specs=_rows(D), out_shape=_sds((S, D), F32),
        scratch_shapes=[pltpu.VMEM((TM, BR_A), BF16)],
        compiler_params=_params(("parallel",)),
    )(*os_, *ls, qm, kv0, z, x, w_out)


def _in_proj_b(h1, g1, w_in):
    def body(x_ref, g_ref, w_ref, hn_ref, bg_ref, cg_ref, u_ref, qm_ref, z_ref, proj):
        xf = x_ref[...]
        hn = xf * lax.rsqrt(jnp.mean(xf * xf, axis=-1, keepdims=True) + EPS) * g_ref[...]
        hb = hn.astype(BF16)
        hn_ref[...] = hb
        for s in range(4):
            proj[:, s * SH_B:(s + 1) * SH_B] = _dot(hb, w_ref[s])
        bg_ref[...] = proj[:, :D]
        cg_ref[...] = proj[:, D:2 * D]
        u_ref[...] = proj[:, 2 * D:3 * D]
        qm_ref[...] = proj[:, 3 * D:3 * D + MW].astype(BF16)
        z_ref[...] = proj[:, 3 * D + MW:]

    return pl.pallas_call(
        body, name="in_proj_b", grid=(NT,),
        in_specs=[_rows(D), _full((1, D)), _full((4, D, SH_B))],
        out_specs=[_rows(D), _rows(D), _rows(D), _rows(D), _rows(MW), _rows(BR_B)],
        out_shape=[_sds((S, D), BF16), _sds((S, D), F32), _sds((S, D), F32), _sds((S, D), F32),
                   _sds((S, MW), BF16), _sds((S, BR_B), F32)],
        scratch_shapes=[pltpu.VMEM((TM, IN_B), F32)],
        compiler_params=_params(("parallel",)),
    )(h1, g1, w_in)


def _prev8(width):
    return pl.BlockSpec((8, width), lambda i: (jnp.maximum(i * (TM // 8) - 1, 0), 0))


def _conv_out_loss(bg, cg, u, cw, qm, kv1, z, h1, w_out, fg, tgt):
    def body(bg_ref, cg_ref, u_ref, cgp_ref, up_ref, cw_ref, qm_ref, kv_ref, z_ref, h_ref, w_ref, fg_ref, t_ref,
             dh_ref, loss_ref, dfg_ref, ybuf):
        i = pl.program_id(0)
        a, a1, a2 = _conv_taps(cg_ref[...], u_ref[...], cgp_ref[...], up_ref[...], i == 0)
        conv = cw_ref[0:1, :] * a2 + cw_ref[1:2, :] * a1 + cw_ref[2:3, :] * a
        sz, _ = _silu_parts(z_ref[...])
        ybuf[:, :D] = (bg_ref[...] * conv * sz[:, :D]).astype(BF16)
        _, mo, _ = _mem_attn(qm_ref[...], kv_ref[...])
        ybuf[:, D:] = (mo * sz[:, D:]).astype(BF16)
        h2 = h_ref[...] + _dot(ybuf[...], w_ref[...])
        rstd = lax.rsqrt(jnp.mean(h2 * h2, axis=-1, keepdims=True) + EPS)
        n = h2 * rstd
        fgv = fg_ref[...]
        err = n * fgv - t_ref[...]
        dout = err * (1.0 / D)
        dn = dout * fgv
        dh_ref[...] = rstd * (dn - n * jnp.mean(dn * n, axis=-1, keepdims=True))

        @pl.when(i == 0)
        def _():
            loss_ref[...] = jnp.zeros_like(loss_ref)
            dfg_ref[...] = jnp.zeros_like(dfg_ref)

        loss_ref[...] += jnp.sum(err * err) * (0.5 / D)
        dfg_ref[...] += jnp.sum(dout * n, axis=0, keepdims=True)

    return pl.pallas_call(
        body, name="conv_out_loss", grid=(NT,),
        in_specs=[_rows(D), _rows(D), _rows(D), _prev8(D), _prev8(D), _full((8, D)), _rows(MW),
                  _full((NM, 2 * MW)), _rows(BR_B), _rows(D), _full((BR_B, D)), _full((1, D)), _rows(D)],
        out_specs=[_rows(D), _full((1, 128)), _full((1, D))],
        out_shape=[_sds((S, D), F32), _sds((1, 128), F32), _sds((1, D), F32)],
        scratch_shapes=[pltpu.VMEM((TM, BR_B), BF16)],
        compiler_params=_params(("arbitrary",)),
    )(bg, cg, u, cg, u, cw, qm, kv1, z, h1, w_out, fg, tgt)


def _conv_bwd(dh2, bg, cg, u, cw, qm, kv1, z, w_out):
    rev = lambda i: (NT - 1 - i, 0)
    rows = lambda w: pl.BlockSpec((TM, w), rev)
    prev8 = pl.BlockSpec((8, D), lambda i: (jnp.maximum((NT - 1 - i) * (TM // 8) - 1, 0), 0))

    def body(dh_ref, bg_ref, cg_ref, u_ref, cgp_ref, up_ref, cw_ref, qm_ref, kv_ref, z_ref, w_ref,
             dproj_ref, dw_ref, dcw_ref, dkv_ref, dwb_ref, ybuf, carry):
        i = pl.program_id(0)

        @pl.when(i == 0)
        def _():
            dw_ref[...] = jnp.zeros_like(dw_ref)
            dcw_ref[...] = jnp.zeros_like(dcw_ref)
            dkv_ref[...] = jnp.zeros_like(dkv_ref)
            carry[...] = jnp.zeros_like(carry)

        bgv, cgv, uv = bg_ref[...], cg_ref[...], u_ref[...]
        a, a1, a2 = _conv_taps(cgv, uv, cgp_ref[...], up_ref[...], i == NT - 1)
        w0, w1, w2 = cw_ref[0:1, :], cw_ref[1:2, :], cw_ref[2:3, :]
        conv = w0 * a2 + w1 * a1 + w2 * a
        mix = bgv * conv
        sz, dsz = _silu_parts(z_ref[...])
        kvv = kv_ref[...]
        p, mo, q4 = _mem_attn(qm_ref[...], kvv)
        ybuf[:, :D] = (mix * sz[:, :D]).astype(BF16)
        ybuf[:, D:] = (mo * sz[:, D:]).astype(BF16)
        dhb = dh_ref[...].astype(BF16)
        dw_ref[...] += _dot_tn(ybuf[...], dhb)
        dy = _dot_nt(dhb, w_ref[...])
        dcat = dy * sz
        dproj_ref[:, 3 * D + MW:3 * D + MW + D] = (dy[:, :D] * mix * dsz[:, :D]).astype(BF16)
        dproj_ref[:, 3 * D + MW + D:] = (dy[:, D:] * mo * dsz[:, D:]).astype(BF16)
        dmix = dcat[:, :D]
        dproj_ref[:, :D] = (dmix * conv).astype(BF16)
        dc = dmix * bgv
        nxt = carry[...]
        row = lax.broadcasted_iota(jnp.int32, dc.shape, 0)
        dc1 = jnp.where(row == TM - 1, nxt[0:1, :], pltpu.roll(dc, TM - 1, 0))
        dc2 = jnp.where(row == TM - 2, nxt[0:1, :], jnp.where(row == TM - 1, nxt[1:2, :], pltpu.roll(dc, TM - 2, 0)))
        carry[...] = dc[0:8, :]
        da = w2 * dc + w1 * dc1 + w0 * dc2
        dproj_ref[:, D:2 * D] = (da * uv).astype(BF16)
        dproj_ref[:, 2 * D:3 * D] = (da * cgv).astype(BF16)
        dcw_ref[0:1, :] += jnp.sum(dc * a2, axis=0, keepdims=True)
        dcw_ref[1:2, :] += jnp.sum(dc * a1, axis=0, keepdims=True)
        dcw_ref[2:3, :] += jnp.sum(dc * a, axis=0, keepdims=True)
        dproj_ref[:, 3 * D:3 * D + MW] = _mem_attn_bwd(dcat[:, D:], p, mo, q4, kvv, dkv_ref).astype(BF16)

        @pl.when(i == NT - 1)
        def _():
            dwb_ref[...] = dw_ref[...].astype(BF16)

    return pl.pallas_call(
        body, name="conv_bwd", grid=(NT,),
        in_specs=[rows(D), rows(D), rows(D), rows(D), prev8, prev8, _full((8, D)), rows(MW),
                  _full((NM, 2 * MW)), rows(BR_B), _full((BR_B, D))],
        out_specs=[rows(IN_B), _full((BR_B, D)), _full((8, D)), _full((NM, 2 * MW)), _full((BR_B, D))],
        out_shape=[_sds((S, IN_B), BF16), _sds((BR_B, D), F32), _sds((8, D), F32), _sds((NM, 2 * MW), F32),
                   _sds((BR_B, D), BF16)],
        scratch_shapes=[pltpu.VMEM((TM, BR_B), BF16), pltpu.VMEM((8, D), F32)],
        compiler_params=_params(("arbitrary",)),
    )(dh2, bg, cg, u, cg, u, cw, qm, kv1, z, w_out)


def _in_proj_bwd(dproj, w_in, xin, g, dres, after, width, name):
    sh = width // 4

    def body(dp_ref, w_ref, x_ref, g_ref, dr_ref, dx_ref, dg_ref):
        i = pl.program_id(0)
        dhn = _dot_nt(dp_ref[:, 0:sh], w_ref[0])
        for s in range(1, 4):
            dhn += _dot_nt(dp_ref[:, s * sh:(s + 1) * sh], w_ref[s])
        xf = x_ref[...]
        rstd = lax.rsqrt(jnp.mean(xf * xf, axis=-1, keepdims=True) + EPS)
        n = xf * rstd
        dn = dhn * g_ref[...]
        dx_ref[...] = dr_ref[...] + rstd * (dn - n * jnp.mean(dn * n, axis=-1, keepdims=True))

        @pl.when(i == 0)
        def _():
            dg_ref[...] = jnp.zeros_like(dg_ref)

        dg_ref[...] += jnp.sum(dhn * n, axis=0, keepdims=True)

    return pl.pallas_call(
        functools.partial(_skip_arg, body, 5), name=name, grid=(NT,),
        in_specs=[_rows(width), _full((4, D, sh)), _rows(D), _full((1, D)), _rows(D), pl.BlockSpec(memory_space=pl.ANY)],
        out_specs=[_rows(D), _full((1, D))],
        out_shape=[_sds((S, D), F32), _sds((1, D), F32)],
        compiler_params=_params(("arbitrary",)),
    )(dproj, w_in, xin, g, dres, after)


def _w_in_grad(hn, dproj, width, name):
    sh = width // 4

    def body(hn_ref, dp_ref, dw_ref, dwb_ref):
        dw = _dot_tn(hn_ref[...], dp_ref[...])
        dw_ref[0] = dw
        dwb_ref[0] = dw.astype(BF16)

    spec = pl.BlockSpec((1, D, sh), lambda s: (s, 0, 0))
    return pl.pallas_call(
        body, name=name, grid=(4,),
        in_specs=[_full((S, D)), pl.BlockSpec((S, sh), lambda s: (0, s))],
        out_specs=[spec, spec], out_shape=[_sds((4, D, sh), F32), _sds((4, D, sh), BF16)],
        compiler_params=_params(("parallel",)),
    )(hn, dproj)


def _attn_out_bwd(dh1, os_, ls, qm, kv0, z, w_out, after):
    ones_bd = np.kron(np.eye(GW // HD, dtype=np.float32), np.ones((HD, HD), np.float32))

    def body(dh_ref, o0, o1, o2, l0, l1, l2, qm_ref, kv_ref, z_ref, w_ref, bd_ref,
             do0, do1, do2, dd0, dd1, dd2, dqm_ref, dz_ref, dw_ref, dkv_ref, dwb_ref, ybuf):
        i = pl.program_id(0)

        @pl.when(i == 0)
        def _():
            dw_ref[...] = jnp.zeros_like(dw_ref)
            dkv_ref[...] = jnp.zeros_like(dkv_ref)

        ws, mix = _merge((o0, o1, o2), (l0, l1, l2))
        sz, dsz = _silu_parts(z_ref[...])
        kvv = kv_ref[...]
        p, mo, q4 = _mem_attn(qm_ref[...], kvv)
        ybuf[:, :GW] = (mix * sz[:, :GW]).astype(BF16)
        ybuf[:, GW:] = (mo * sz[:, GW:]).astype(BF16)
        yb = ybuf[...]
        dh = dh_ref[...]
        dy = None
        for s in range(4):
            dhb = dh[:, s * SH_O:(s + 1) * SH_O].astype(BF16)
            dw_ref[s] += _dot_tn(yb, dhb)
            part = _dot_nt(dhb, w_ref[s])
            dy = part if dy is None else dy + part
        dcat = dy * sz
        dz_ref[:, :GW] = (dy[:, :GW] * mix * dsz[:, :GW]).astype(BF16)
        dz_ref[:, GW:] = (dy[:, GW:] * mo * dsz[:, GW:]).astype(BF16)
        dmix = dcat[:, :GW]
        prod = dmix * mix
        hi = prod.astype(BF16)
        lo = (prod - hi.astype(F32)).astype(BF16)
        bd = bd_ref[...]
        tot = _dot(hi, bd) + _dot(lo, bd)
        for w, do_ref, dd_ref in zip(ws, (do0, do1, do2), (dd0, dd1, dd2)):
            do_ref[...] = (w * dmix).astype(BF16)
            dd_ref[...] = w * tot

        dqm_ref[...] = _mem_attn_bwd(dcat[:, GW:], p, mo, q4, kvv, dkv_ref).astype(BF16)

        @pl.when(i == NT - 1)
        def _():
            dwb_ref[...] = dw_ref[...].astype(BF16)

    return pl.pallas_call(
        functools.partial(_skip_arg, body, 12), name="attn_out_bwd", grid=(NT,),
        in_specs=[_rows(D)] + [_rows(GW)] * 6 + [_rows(MW), _full((NM, 2 * MW)), _rows(BR_A),
                                                   _full((4, BR_A, SH_O)), _full((GW, GW)),
                                                   pl.BlockSpec(memory_space=pl.ANY)],
        out_specs=[_rows(GW)] * 6 + [_rows(MW), _rows(BR_A), _full((4, BR_A, SH_O)), _full((NM, 2 * MW)),
                                     _full((4, BR_A, SH_O))],
        out_shape=[_sds((S, GW), BF16)] * 3 + [_sds((S, GW), F32)] * 3 + [
            _sds((S, MW), BF16), _sds((S, BR_A), BF16), _sds((4, BR_A, SH_O), F32), _sds((NM, 2 * MW), F32),
            _sds((4, BR_A, SH_O), BF16)],
        scratch_shapes=[pltpu.VMEM((TM, BR_A), BF16)],
        compiler_params=_params(("arbitrary",)),
    )(dh1, *os_, *ls, qm, kv0, z, w_out, jnp.asarray(ones_bd, dtype=BF16), after)


def _attn_bwd(q, k, v, do, lse_s, dd, g):
    d = DILATIONS[g]
    nb = S // d // QBLK
    perm = _perm_matrix(d)

    def body(q_ref, k_ref, v_ref, do_ref, l_ref, dd_ref, p_ref, pt_ref, dq_ref, dk_ref, dv_ref,
             q0, q1, g0, g1, ks, vs, dds, dqs, dks, dvs):
        first, second = _head_masks()
        pm = p_ref[...]
        for t in range(NT):
            rows = slice(t * TM, (t + 1) * TM)
            if d == 1:
                qt = q_ref[rows, :].astype(F32)
                gt = do_ref[rows, :].astype(F32)
            else:
                qt, gt = _pair_dot(pm, q_ref[rows, :], do_ref[rows, :])
                kt, vt = _pair_dot(pm, k_ref[rows, :], v_ref[rows, :])
                _tile_to_streams(kt, ks, t, d)
                _tile_to_streams(vt, vs, t, d)
                _tile_to_streams(_split_dot(pm, dd_ref[rows, :], 2), dds, t, d)
            _tile_to_streams(jnp.where(first, qt, 0.0), q0, t, d)
            _tile_to_streams(jnp.where(second, qt, 0.0), q1, t, d)
            _tile_to_streams(jnp.where(first, gt, 0.0), g0, t, d)
            _tile_to_streams(jnp.where(second, gt, 0.0), g1, t, d)
        kref, vref, ddref = (k_ref, v_ref, dd_ref) if d == 1 else (ks, vs, dds)
        dqref, dkref, dvref = (dq_ref, dk_ref, dv_ref) if d == 1 else (dqs, dks, dvs)
        dkref[...] = jnp.zeros_like(dkref)
        dvref[...] = jnp.zeros_like(dvref)

        def blk(b, carry):
            r0 = pl.multiple_of(b * QBLK, QBLK)
            p0 = pl.multiple_of(jnp.maximum(b - 1, 0) * QBLK, QBLK)
            kk = jnp.concatenate([kref[pl.ds(p0, QBLK), :], kref[pl.ds(r0, QBLK), :]], axis=0)
            vv = jnp.concatenate([vref[pl.ds(p0, QBLK), :], vref[pl.ds(r0, QBLK), :]], axis=0)
            lb = l_ref[pl.ds(r0, QBLK), :]
            ddb = ddref[pl.ds(r0, QBLK), :]
            lcol = jnp.concatenate([lb[:, 0:1], lb[:, HD:HD + 1]], axis=0)
            dcol = jnp.concatenate([ddb[:, 0:1], ddb[:, HD:HD + 1]], axis=0)
            valid = _band_mask(b & (nb - 1))
            valid2 = jnp.concatenate([valid, valid], axis=0)
            qq = jnp.concatenate([q0[pl.ds(r0, QBLK), :], q1[pl.ds(r0, QBLK), :]], axis=0)
            gg = jnp.concatenate([g0[pl.ds(r0, QBLK), :], g1[pl.ds(r0, QBLK), :]], axis=0)
            p = jnp.where(valid2, jnp.exp(_dot_nt(qq, kk) - lcol), 0.0)
            ds = (p * (_dot_nt(gg, vv) - dcol)).astype(BF16)
            dq2 = _dot(ds, kk)
            dqref[pl.ds(r0, QBLK), :] = jnp.where(first[:QBLK], dq2[:QBLK], dq2[QBLK:])
            dkk = _dot_tn(ds, qq)
            dvv = _dot_tn(p.astype(BF16), gg)
            dkref[pl.ds(p0, QBLK), :] += dkk[:QBLK]
            dkref[pl.ds(r0, QBLK), :] += dkk[QBLK:]
            dvref[pl.ds(p0, QBLK), :] += dvv[:QBLK]
            dvref[pl.ds(r0, QBLK), :] += dvv[QBLK:]
            return carry

        lax.fori_loop(0, S // QBLK, blk, 0, unroll=BWD_UNROLL)

        if d > 1:
            ptm = pt_ref[...]
            for t in range(NT):
                rows = slice(t * TM, (t + 1) * TM)
                dq_ref[rows, :] = _split_dot(ptm, _tile_from_streams(dqs, t, d), 2)
                dk_ref[rows, :] = _split_dot(ptm, _tile_from_streams(dks, t, d), 2)
                dv_ref[rows, :] = _split_dot(ptm, _tile_from_streams(dvs, t, d), 2)

    qkv_spec = pl.BlockSpec((S, LANES), lambda c: (0, g * NCHUNK + c))
    one_spec = pl.BlockSpec((S, LANES), lambda c: (0, c))
    return pl.pallas_call(
        body, name=f"attn_bwd_g{g}", grid=(NCHUNK,),
        in_specs=[qkv_spec] * 3 + [one_spec] * 3 + [_full((TM, TM))] * 2, out_specs=[one_spec] * 3,
        out_shape=[_sds((S, GW), F32)] * 3,
        scratch_shapes=[pltpu.VMEM((S, LANES), BF16)] * 6 + [pltpu.VMEM((S, LANES), F32)] * 4,
        compiler_params=_params(("parallel",)),
    )(q, k, v, do, lse_s, dd, jnp.asarray(perm, BF16), jnp.asarray(perm.T, BF16))


def _qkv_bwd(dqs, dks, dvs, dqm, dz, c, s1, s2):
    def body(q0, q1, q2, k0, k1, k2, v0, v1, v2, dqm_ref, dz_ref, c_ref, s1_ref, s2_ref, dp_ref):
        cc, a1, a2 = c_ref[...], s1_ref[...], s2_ref[...]
        for g, (qr, kr, vr) in enumerate(((q0, k0, v0), (q1, k1, v1), (q2, k2, v2))):
            for j in range(GW // 128):
                ls_ = slice(j * 128, (j + 1) * 128)
                c0 = g * GW + j * 128
                dp_ref[:, c0:c0 + 128] = (_rope_bwd(qr[:, ls_], cc, a1, a2) * SCALE).astype(BF16)
                dp_ref[:, NQ + c0:NQ + c0 + 128] = _rope_bwd(kr[:, ls_], cc, a1, a2).astype(BF16)
            dp_ref[:, 2 * NQ + g * GW:2 * NQ + (g + 1) * GW] = vr[...].astype(BF16)
        dp_ref[:, 3 * NQ:3 * NQ + MW] = dqm_ref[...]
        dp_ref[:, 3 * NQ + MW:] = dz_ref[...]

    return pl.pallas_call(
        body, name="qkv_bwd", grid=(NT,),
        in_specs=[_rows(GW)] * 9 + [_rows(MW), _rows(BR_A), _rows(128), _rows(128), _rows(128)],
        out_specs=_rows(IN_A), out_shape=_sds((S, IN_A), BF16),
        compiler_params=_params(("parallel",)),
    )(*dqs, *dks, *dvs, dqm, dz, c, s1, s2)


def _mem_bwd(mem, mg, memn, wkv, dkv0, dkv1):
    def body(mem_ref, mg_ref, memn_ref, w_ref, d0_ref, d1_ref, dw_ref, dwb_ref, dg_ref):
        mf = mem_ref[...]
        n = mf * lax.rsqrt(jnp.mean(mf * mf, axis=-1, keepdims=True) + EPS)
        for i, d_ref in enumerate((d0_ref, d1_ref)):
            dkv = d_ref[...].astype(BF16)
            mn = memn_ref[i]
            for s in range(4):
                cs = slice(s * NM, (s + 1) * NM)
                dw = _dot_tn(mn[:, cs], dkv)
                dw_ref[s, i] = dw
                dwb_ref[s, i] = dw.astype(BF16)
                dmn = _dot_nt(dkv, w_ref[s, i])
                dg_ref[i:i + 1, cs] = jnp.sum(dmn * n[:, cs], axis=0, keepdims=True)

    return pl.pallas_call(
        body, name="mem_bwd", grid=(1,),
        in_specs=[_full((NM, D)), _full((2, D)), _full((2, NM, D)), _full((4, 2, NM, 2 * MW)),
                  _full((NM, 2 * MW)), _full((NM, 2 * MW))],
        out_specs=[_full((4, 2, NM, 2 * MW)), _full((4, 2, NM, 2 * MW)), _full((2, D))],
        out_shape=[_sds((4, 2, NM, 2 * MW), F32), _sds((4, 2, NM, 2 * MW), BF16), _sds((2, D), F32)],
        compiler_params=_params(("arbitrary",)),
    )(mem, mg, memn, wkv, dkv0, dkv1)


MESH = pl.DeviceIdType.MESH
ANY = pl.BlockSpec(memory_space=pl.ANY)
BIG = (("wkv", 2, NM, 2 * MW), ("w_in_a", 1, D, SH_A), ("w_out_a", 1, BR_A, SH_O),
       ("w_in_b", 1, D, SH_B), ("w_out_b", 1, BR_B // 4, D))
NBIG = len(BIG)
CW_ROWS = 8


def _place():
    x, y, c = lax.axis_index("x"), lax.axis_index("y"), lax.axis_index("c")
    chips = ((1 - x, y), (x, 1 - y), (1 - x, 1 - y))
    return x, y, c, chips


def _remote(src, dst, ssem, rsem, dev):
    return pltpu.make_async_remote_copy(src_ref=src, dst_ref=dst, send_sem=ssem, recv_sem=rsem,
                                        device_id=dev, device_id_type=MESH)


def _cast_weights(place, ws, after, idx, name):
    nblk = 4
    n = len(idx)
    dims = [BIG[w][1:] for w in idx]

    def body(pref, *refs):
        for i in range(n):
            refs[n + 1 + i][0] = refs[i][...].astype(BF16)

    grid_spec = pltpu.PrefetchScalarGridSpec(
        num_scalar_prefetch=1, grid=(nblk,),
        in_specs=[pl.BlockSpec((k, r // nblk, cdim), lambda i, pref: (0, i, 0)) for k, r, cdim in dims]
        + [pl.BlockSpec(memory_space=pl.ANY)],
        out_specs=[pl.BlockSpec((1, k, r // nblk, cdim), lambda i, pref: (pref[1], 0, i, 0)) for k, r, cdim in dims])
    return pl.pallas_call(
        body, name=name, grid_spec=grid_spec,
        out_shape=[_sds((4, k, r, cdim), BF16) for k, r, cdim in dims],
        compiler_params=_params(("parallel",)),
    )(place, *ws, after)


LAYER_A = (0, 1, 2)
LAYER_B = (3, 4)
HBM = pl.BlockSpec(memory_space=pltpu.HBM)
SEM = pl.BlockSpec(memory_space=pltpu.SEMAPHORE)
EFFECT = pltpu.SideEffectType.DATAFLOW_SIDE_EFFECTING
TOKEN = (8, 128)


def _half(ref, w, which):
    h = BIG[w][2] // 2
    return ref.at[:, pl.ds(which * h, h), :]


def _skip_arg(body, pos, *refs):
    return body(*refs[:pos], *refs[pos + 1:])


def _gather_weights(wb, cw, idx, name):
    n = len(idx)

    def body(*refs):
        src_cw = refs[n]
        dst = refs[n + 1:2 * n + 2]
        loc_sem, send_sems, recv_sems, fsend_sems, frecv_sems = refs[2 * n + 2:]
        x, y, c, chips = _place()
        me = 2 * x + y
        loc = pltpu.make_async_copy(src_cw, dst[n].at[me], loc_sem)
        loc.start()
        sends = []
        for j, (px, py) in enumerate(chips):
            for i in range(n):
                mine = _half(dst[i].at[me], idx[i], c)
                sends.append(_remote(mine, mine, send_sems.at[j, i], recv_sems.at[j, i], (px, py, c)))
            sends.append(_remote(src_cw, dst[n].at[me], send_sems.at[j, n], recv_sems.at[j, n], (px, py, c)))
        for cp in sends:
            cp.start()
        fwds = []
        for j, (px, py) in enumerate(chips):
            for i in range(n):
                got = _half(dst[i].at[2 * px + py], idx[i], c)
                _remote(got, got, send_sems.at[j, i], recv_sems.at[j, i], (px, py, c)).wait_recv()
                fwds.append(_remote(got, got, fsend_sems.at[j, i], frecv_sems.at[j, i], (x, y, 1 - c)))
                fwds[-1].start()
            got = dst[n].at[2 * px + py]
            _remote(got, got, send_sems.at[j, n], recv_sems.at[j, n], (px, py, c)).wait_recv()
        for j, (px, py) in enumerate(chips):
            for i in range(n):
                got = _half(dst[i].at[2 * px + py], idx[i], 1 - c)
                _remote(got, got, fsend_sems.at[j, i], frecv_sems.at[j, i], (x, y, 1 - c)).wait_recv()
        for cp in sends + fwds:
            cp.wait_send()
        loc.wait()

    out_shape = [_sds(w.shape, BF16) for w in wb] + [_sds((4, CW_ROWS, SH_O), F32)]
    return pl.pallas_call(
        body, name=name, in_specs=[ANY] * (n + 1), out_specs=[ANY] * (n + 1), out_shape=out_shape,
        input_output_aliases={i: i for i in range(n)},
        scratch_shapes=[pltpu.SemaphoreType.DMA, pltpu.SemaphoreType.DMA((3, n + 1)),
                        pltpu.SemaphoreType.DMA((3, n + 1)), pltpu.SemaphoreType.DMA((3, n)),
                        pltpu.SemaphoreType.DMA((3, n))],
    )(*wb, cw)


def _gather_start(wb, after, idx, name):
    n = len(idx)

    def body(*refs):
        src = refs[:n]
        send_sems, recv_sems = refs[n + 1], refs[n + 2]
        token = refs[2 * n + 3]
        x, y, c, chips = _place()
        me = 2 * x + y
        for j, (px, py) in enumerate(chips):
            for i in range(n):
                mine = _half(src[i].at[me], idx[i], c)
                _remote(mine, mine, send_sems.at[j * n + i], recv_sems.at[j * n + i], (px, py, c)).start()
        token[...] = jnp.zeros(TOKEN, F32)

    outs = pl.pallas_call(
        body, name=name, in_specs=[HBM] * n + [ANY],
        out_specs=(SEM, SEM) + (HBM,) * n + (pl.BlockSpec(memory_space=pltpu.VMEM),),
        out_shape=(pltpu.SemaphoreType.DMA((3 * n,)), pltpu.SemaphoreType.DMA((3 * n,)))
        + tuple(pltpu.HBM(w.shape, w.dtype) for w in wb) + (_sds(TOKEN, F32),),
        input_output_aliases={i: 2 + i for i in range(n)},
        compiler_params=pltpu.CompilerParams(has_side_effects=EFFECT),
    )(*[pltpu.with_memory_space_constraint(w, pltpu.HBM) for w in wb], after)
    return outs[0], outs[1], list(outs[2:2 + n]), outs[2 + n]


def _gather_wait(send_sems, recv_sems, wb, after, idx, name):
    n = len(idx)

    def body(*refs):
        buf = refs[:n]
        send_sems, recv_sems = refs[n], refs[n + 1]
        x, y, c, chips = _place()
        me = 2 * x + y
        for j, (px, py) in enumerate(chips):
            for i in range(n):
                mine = _half(buf[i].at[me], idx[i], c)
                got = _half(buf[i].at[2 * px + py], idx[i], c)
                _remote(mine, mine, send_sems.at[j * n + i], recv_sems.at[j * n + i], (px, py, c)).wait_send()
                _remote(got, got, send_sems.at[j * n + i], recv_sems.at[j * n + i], (px, py, c)).wait_recv()

    outs = pl.pallas_call(
        body, name=name, in_specs=[HBM] * n + [SEM, SEM] + [ANY] * len(after), out_specs=(HBM,) * n,
        out_shape=tuple(pltpu.HBM(w.shape, w.dtype) for w in wb),
        input_output_aliases={i: i for i in range(n)},
        compiler_params=pltpu.CompilerParams(has_side_effects=EFFECT),
    )(*wb, send_sems, recv_sems, *after)
    return list(outs)


def _gather_forward(wb, idx, name, cw=None):
    n = len(idx)
    m = n if cw is None else n + 1

    def body(*refs):
        dst = refs[m:2 * m]
        send_sems, recv_sems = refs[2 * m], refs[2 * m + 1]
        x, y, c, chips = _place()
        cps = []
        for j, (px, py) in enumerate(chips):
            for i in range(n):
                got = _half(dst[i].at[2 * px + py], idx[i], c)
                cps.append(_remote(got, got, send_sems.at[j, i], recv_sems.at[j, i], (x, y, 1 - c)))
                cps[-1].start()
        if cw is not None:
            src_cw, loc_sem = refs[n], refs[2 * m + 2]
            me = 2 * x + y
            loc = pltpu.make_async_copy(src_cw, dst[n].at[me], loc_sem)
            loc.start()
            for j, (px, py) in enumerate(chips):
                cps.append(_remote(src_cw, dst[n].at[me], send_sems.at[j, n], recv_sems.at[j, n], (px, py, c)))
                cps[-1].start()
        for j, (px, py) in enumerate(chips):
            for i in range(n):
                got = _half(dst[i].at[2 * px + py], idx[i], 1 - c)
                _remote(got, got, send_sems.at[j, i], recv_sems.at[j, i], (x, y, 1 - c)).wait_recv()
            if cw is not None:
                got = dst[n].at[2 * px + py]
                _remote(got, got, send_sems.at[j, n], recv_sems.at[j, n], (px, py, c)).wait_recv()
        for cp in cps:
            cp.wait_send()
        if cw is not None:
            loc.wait()

    out_shape = [_sds(w.shape, BF16) for w in wb]
    scratch = [pltpu.SemaphoreType.DMA((3, m)), pltpu.SemaphoreType.DMA((3, m))]
    args = list(wb)
    if cw is not None:
        out_shape.append(_sds((4, CW_ROWS, SH_O), F32))
        scratch.append(pltpu.SemaphoreType.DMA)
        args.append(cw)
    return pl.pallas_call(
        body, name=name, in_specs=[ANY] * m, out_specs=[ANY] * m, out_shape=out_shape,
        input_output_aliases={i: i for i in range(n)}, scratch_shapes=scratch,
    )(*args)


def _forward_start(wb, cw, after, idx, name):
    n = len(idx)
    m = n if cw is None else n + 2

    def body(*refs):
        buf = refs[:n]
        send_sems, recv_sems = refs[m + 1], refs[m + 2]
        token = refs[2 * m + 3]
        x, y, c, chips = _place()
        for j, (px, py) in enumerate(chips):
            for i in range(n):
                got = _half(buf[i].at[2 * px + py], idx[i], c)
                _remote(got, got, send_sems.at[j * (n + 1) + i], recv_sems.at[j * (n + 1) + i], (x, y, 1 - c)).start()
            if cw is not None:
                _remote(refs[n], refs[n + 1].at[2 * x + y], send_sems.at[j * (n + 1) + n],
                        recv_sems.at[j * (n + 1) + n], (px, py, c)).start()
        token[...] = jnp.zeros(TOKEN, F32)

    arrays = list(wb) if cw is None else list(wb) + [cw, lax.empty((4, CW_ROWS, SH_O), F32)]
    outs = pl.pallas_call(
        body, name=name, in_specs=[HBM] * m + [ANY],
        out_specs=(SEM, SEM) + (HBM,) * m + (pl.BlockSpec(memory_space=pltpu.VMEM),),
        out_shape=(pltpu.SemaphoreType.DMA((3 * (n + 1),)), pltpu.SemaphoreType.DMA((3 * (n + 1),)))
        + tuple(pltpu.HBM(a.shape, a.dtype) for a in arrays) + (_sds(TOKEN, F32),),
        input_output_aliases={i: 2 + i for i in range(m)},
        compiler_params=pltpu.CompilerParams(has_side_effects=EFFECT),
    )(*[pltpu.with_memory_space_constraint(a, pltpu.HBM) for a in arrays], after)
    return outs[0], outs[1], list(outs[2:2 + m]), outs[2 + m]


def _forward_wait(send_sems, recv_sems, arrays, after, idx, with_cw, name):
    n = len(idx)
    m = len(arrays)

    def body(*refs):
        buf = refs[:n]
        send_sems, recv_sems = refs[m], refs[m + 1]
        x, y, c, chips = _place()
        for j, (px, py) in enumerate(chips):
            for i in range(n):
                sent = _half(buf[i].at[2 * px + py], idx[i], c)
                got = _half(buf[i].at[2 * px + py], idx[i], 1 - c)
                k = j * (n + 1) + i
                _remote(sent, sent, send_sems.at[k], recv_sems.at[k], (x, y, 1 - c)).wait_send()
                _remote(got, got, send_sems.at[k], recv_sems.at[k], (x, y, 1 - c)).wait_recv()
            if with_cw:
                k = j * (n + 1) + n
                theirs = refs[n + 1].at[2 * px + py]
                _remote(refs[n], theirs, send_sems.at[k], recv_sems.at[k], (px, py, c)).wait_send()
                _remote(refs[n], theirs, send_sems.at[k], recv_sems.at[k], (px, py, c)).wait_recv()

    outs = pl.pallas_call(
        body, name=name, in_specs=[HBM] * m + [SEM, SEM] + [ANY] * len(after), out_specs=(HBM,) * m,
        out_shape=tuple(pltpu.HBM(a.shape, a.dtype) for a in arrays),
        input_output_aliases={i: i for i in range(m)},
        compiler_params=pltpu.CompilerParams(has_side_effects=EFFECT),
    )(*arrays, send_sems, recv_sems, *after)
    return list(outs)


def _pair_exchange(gs, idx, name):
    n = len(idx)

    def body(*refs):
        src, dst = refs[:n], refs[n:2 * n]
        send_sems, recv_sems = refs[2 * n:]
        x, y, c, _ = _place()
        cps = []
        for i in range(n):
            h = BIG[idx[i]][2] // 2
            cps.append(_remote(src[i].at[:, :, pl.ds((1 - c) * h, h), :], dst[i], send_sems.at[i], recv_sems.at[i],
                               (x, y, 1 - c)))
            cps[-1].start()
        for cp in cps:
            cp.wait()

    return pl.pallas_call(
        body, name=name, in_specs=[ANY] * n, out_specs=[ANY] * n,
        out_shape=[_sds((4, BIG[w][1], BIG[w][2] // 2, BIG[w][3]), BF16) for w in idx],
        scratch_shapes=[pltpu.SemaphoreType.DMA((n,)), pltpu.SemaphoreType.DMA((n,))],
    )(*gs)


def _pair_start(gs, idx, name):
    n = len(idx)

    def body(*refs):
        src, land = refs[:n], refs[n:2 * n]
        send_sems, recv_sems = refs[2 * n], refs[2 * n + 1]
        token = refs[4 * n + 2]
        x, y, c, _ = _place()
        for i in range(n):
            h = BIG[idx[i]][2] // 2
            _remote(src[i].at[:, :, pl.ds((1 - c) * h, h), :], land[i], send_sems.at[i], recv_sems.at[i],
                    (x, y, 1 - c)).start()
        token[...] = jnp.zeros(TOKEN, F32)

    lands = [lax.empty((4, BIG[w][1], BIG[w][2] // 2, BIG[w][3]), BF16) for w in idx]
    arrays = list(gs) + lands
    outs = pl.pallas_call(
        body, name=name, in_specs=[HBM] * (2 * n),
        out_specs=(SEM, SEM) + (HBM,) * (2 * n) + (pl.BlockSpec(memory_space=pltpu.VMEM),),
        out_shape=(pltpu.SemaphoreType.DMA((n,)), pltpu.SemaphoreType.DMA((n,)))
        + tuple(pltpu.HBM(a.shape, a.dtype) for a in arrays) + (_sds(TOKEN, F32),),
        input_output_aliases={i: 2 + i for i in range(2 * n)},
        compiler_params=pltpu.CompilerParams(has_side_effects=EFFECT),
    )(*[pltpu.with_memory_space_constraint(a, pltpu.HBM) for a in arrays])
    return outs[0], outs[1], list(outs[2:2 + n]), list(outs[2 + n:2 + 2 * n]), outs[2 + 2 * n]


def _pair_wait(send_sems, recv_sems, gs, lands, after, idx, name):
    n = len(idx)

    def body(*refs):
        src, land = refs[:n], refs[n:2 * n]
        send_sems, recv_sems = refs[2 * n], refs[2 * n + 1]
        x, y, c, _ = _place()
        for i in range(n):
            h = BIG[idx[i]][2] // 2
            cp = _remote(src[i].at[:, :, pl.ds((1 - c) * h, h), :], land[i], send_sems.at[i], recv_sems.at[i],
                         (x, y, 1 - c))
            cp.wait_send()
            cp.wait_recv()

    arrays = list(gs) + list(lands)
    outs = pl.pallas_call(
        body, name=name, in_specs=[HBM] * (2 * n) + [SEM, SEM] + [ANY] * len(after), out_specs=(HBM,) * (2 * n),
        out_shape=tuple(pltpu.HBM(a.shape, a.dtype) for a in arrays),
        input_output_aliases={i: i for i in range(2 * n)},
        compiler_params=pltpu.CompilerParams(has_side_effects=EFFECT),
    )(*arrays, send_sems, recv_sems, *after)
    return list(outs[:n]), list(outs[n:])


def _pair_sum(place, g, r1, i):
    _, k, r, cdim = BIG[i]
    h = r // 2

    def body(pref, g_ref, r_ref, o_ref):
        o_ref[...] = (g_ref[...] + r_ref[...]).astype(BF16)

    grid_spec = pltpu.PrefetchScalarGridSpec(
        num_scalar_prefetch=1, grid=(4, k),
        in_specs=[pl.BlockSpec((1, 1, h, cdim), lambda s, t, pref: (s, t, pref[0], 0)),
                  pl.BlockSpec((1, 1, h, cdim), lambda s, t, pref: (s, t, 0, 0))],
        out_specs=pl.BlockSpec((1, 1, h, cdim), lambda s, t, pref: (s, t, 0, 0)))
    return pl.pallas_call(
        body, name=f"pair_sum_{BIG[i][0]}", grid_spec=grid_spec, out_shape=_sds((4, k, h, cdim), BF16),
        compiler_params=_params(("parallel", "parallel")),
    )(place, g, r1)


def _pair_sums(place, gs, r1s, idx, name):
    n = len(idx)
    dims = [(BIG[w][1], BIG[w][2] // 2, BIG[w][3]) for w in idx]

    def body(pref, *refs):
        for i in range(n):
            refs[2 * n + i][...] = (refs[i][...] + refs[n + i][...].astype(F32)).astype(BF16)

    mine = [pl.BlockSpec((1, k, h, cdim), lambda s, pref: (s, 0, pref[0], 0)) for k, h, cdim in dims]
    whole = [pl.BlockSpec((1, k, h, cdim), lambda s, pref: (s, 0, 0, 0)) for k, h, cdim in dims]
    grid_spec = pltpu.PrefetchScalarGridSpec(num_scalar_prefetch=1, grid=(4,), in_specs=mine + whole, out_specs=whole)
    return pl.pallas_call(
        body, name=name, grid_spec=grid_spec, out_shape=[_sds((4, k, h, cdim), BF16) for k, h, cdim in dims],
        compiler_params=_params(("parallel",)),
    )(place, *gs, *r1s)


def _chip_start(ps, idx, name):
    n = len(idx)

    def body(*refs):
        src, land = refs[:n], refs[n:2 * n]
        send_sems, recv_sems = refs[2 * n], refs[2 * n + 1]
        token = refs[4 * n + 2]
        x, y, c, chips = _place()
        for j, (px, py) in enumerate(chips):
            for i in range(n):
                _remote(src[i].at[2 * px + py], land[i].at[j], send_sems.at[j * n + i], recv_sems.at[j * n + i],
                        (px, py, c)).start()
        token[...] = jnp.zeros(TOKEN, F32)

    lands = [lax.empty((3,) + p.shape[1:], BF16) for p in ps]
    outs = pl.pallas_call(
        body, name=name, in_specs=[HBM] * (2 * n),
        out_specs=(SEM, SEM) + (HBM,) * (2 * n) + (pl.BlockSpec(memory_space=pltpu.VMEM),),
        out_shape=(pltpu.SemaphoreType.DMA((3 * n,)), pltpu.SemaphoreType.DMA((3 * n,)))
        + tuple(pltpu.HBM(a.shape, a.dtype) for a in list(ps) + lands) + (_sds(TOKEN, F32),),
        input_output_aliases={i: 2 + i for i in range(2 * n)},
        compiler_params=pltpu.CompilerParams(has_side_effects=EFFECT),
    )(*[pltpu.with_memory_space_constraint(a, pltpu.HBM) for a in list(ps) + lands])
    return outs[0], outs[1], list(outs[2:2 + n]), list(outs[2 + n:2 + 2 * n]), outs[2 + 2 * n]


def _chip_wait(send_sems, recv_sems, ps, lands, after, idx, name):
    n = len(idx)

    def body(*refs):
        src, land = refs[:n], refs[n:2 * n]
        send_sems, recv_sems = refs[2 * n], refs[2 * n + 1]
        x, y, c, chips = _place()
        for j, (px, py) in enumerate(chips):
            for i in range(n):
                cp = _remote(src[i].at[2 * px + py], land[i].at[j], send_sems.at[j * n + i], recv_sems.at[j * n + i],
                             (px, py, c))
                cp.wait_send()
                cp.wait_recv()

    arrays = list(ps) + list(lands)
    outs = pl.pallas_call(
        body, name=name, in_specs=[HBM] * (2 * n) + [SEM, SEM] + [ANY] * len(after), out_specs=(HBM,) * (2 * n),
        out_shape=tuple(pltpu.HBM(a.shape, a.dtype) for a in arrays),
        input_output_aliases={i: i for i in range(2 * n)},
        compiler_params=pltpu.CompilerParams(has_side_effects=EFFECT),
    )(*arrays, send_sems, recv_sems, *after)
    return list(outs[n:])


def _chip_sum(place, g, r1, r2, i):
    _, k, r, cdim = BIG[i]
    h = r // 2

    def body(pref, g_ref, r1_ref, r2_ref, o_ref):
        acc = g_ref[0, 0] + r1_ref[0, 0]
        for j in range(3):
            acc = acc + r2_ref[j, 0].astype(F32)
        o_ref[0] = acc

    grid_spec = pltpu.PrefetchScalarGridSpec(
        num_scalar_prefetch=1, grid=(k,),
        in_specs=[pl.BlockSpec((1, 1, h, cdim), lambda t, pref: (pref[1], t, pref[0], 0)),
                  pl.BlockSpec((1, 1, h, cdim), lambda t, pref: (pref[1], t, 0, 0)),
                  pl.BlockSpec((3, 1, h, cdim), lambda t, pref: (0, t, 0, 0))],
        out_specs=pl.BlockSpec((1, h, cdim), lambda t, pref: (t, pref[0], 0)))
    return pl.pallas_call(
        body, name=f"chip_sum_{BIG[i][0]}", grid_spec=grid_spec, out_shape=_sds((k, r, cdim), F32),
        compiler_params=_params(("parallel",)),
    )(place, g, r1, r2)


def _chip_sums(place, gs, r1s, r2s, idx, name):
    n = len(idx)
    dims = [(BIG[w][1], BIG[w][2] // 4, BIG[w][3]) for w in idx]

    def body(pref, *refs):
        for i in range(n):
            acc = refs[i][0] + refs[n + i][0].astype(F32)
            for j in range(3):
                acc = acc + refs[2 * n + i][j].astype(F32)
            refs[3 * n + i][...] = acc

    in_specs = ([pl.BlockSpec((1, k, q, cdim), lambda t, pref: (pref[1], 0, pref[0] * 2 + t, 0)) for k, q, cdim in dims]
                + [pl.BlockSpec((1, k, q, cdim), lambda t, pref: (pref[1], 0, t, 0)) for k, q, cdim in dims]
                + [pl.BlockSpec((3, k, q, cdim), lambda t, pref: (0, 0, t, 0)) for k, q, cdim in dims])
    out_specs = [pl.BlockSpec((k, q, cdim), lambda t, pref: (0, pref[0] * 2 + t, 0)) for k, q, cdim in dims]
    grid_spec = pltpu.PrefetchScalarGridSpec(num_scalar_prefetch=1, grid=(2,), in_specs=in_specs, out_specs=out_specs)
    return pl.pallas_call(
        body, name=name, grid_spec=grid_spec, out_shape=[_sds(BIG[w][1:], F32) for w in idx],
        compiler_params=_params(("parallel",)),
    )(place, *gs, *r1s, *r2s)


def _pair_gather(hs, idx, name):
    n = len(idx)

    def body(*refs):
        dst = refs[n:2 * n]
        send_sems, recv_sems = refs[2 * n:]
        x, y, c, _ = _place()
        cps = []
        for i in range(n):
            mine = _half(dst[i], idx[i], c)
            cps.append(_remote(mine, mine, send_sems.at[i], recv_sems.at[i], (x, y, 1 - c)))
            cps[-1].start()
        for i in range(n):
            theirs = _half(dst[i], idx[i], 1 - c)
            _remote(theirs, theirs, send_sems.at[i], recv_sems.at[i], (x, y, 1 - c)).wait_recv()
        for cp in cps:
            cp.wait_send()

    return pl.pallas_call(
        body, name=name, in_specs=[ANY] * n, out_specs=[ANY] * n,
        out_shape=[_sds(BIG[w][1:], F32) for w in idx],
        input_output_aliases={i: i for i in range(n)},
        scratch_shapes=[pltpu.SemaphoreType.DMA((n,)), pltpu.SemaphoreType.DMA((n,))],
    )(*hs)


SMALL_ROWS = 40


def _all_reduce_small(pack, after):
    def body(p_ref, o_ref, slots, send_sems, recv_sems):
        x, y, c, _ = _place()
        me = 4 * x + 2 * y + c
        cps = []
        for r in range(1, 8):
            peer = (x if not r & 4 else 1 - x, y if not r & 2 else 1 - y, c if not r & 1 else 1 - c)
            cps.append(_remote(p_ref, slots.at[r], send_sems.at[r - 1], recv_sems.at[r - 1], peer))
            cps[-1].start()
        slots[0] = p_ref[...]
        for cp in cps:
            cp.wait()
        acc = slots[me]
        for dev in range(1, 8):
            acc = acc + slots[jnp.bitwise_xor(me, dev)]
        o_ref[...] = acc

    vm = pl.BlockSpec(memory_space=pltpu.VMEM)
    return pl.pallas_call(
        functools.partial(_skip_arg, body, 1), name="all_reduce_small", in_specs=[vm, ANY], out_specs=vm,
        out_shape=_sds((SMALL_ROWS, D), F32),
        scratch_shapes=[pltpu.VMEM((8, SMALL_ROWS, D), F32), pltpu.SemaphoreType.DMA((7,)),
                        pltpu.SemaphoreType.DMA((7,))],
    )(pack, after)


def _adamw_math(w, g, m, v):
    m = ADAM_B1 * m + (1.0 - ADAM_B1) * g
    v = ADAM_B2 * v + (1.0 - ADAM_B2) * (g * g)
    m_hat = m / (1.0 - ADAM_B1 ** ADAM_STEP)
    v_hat = v / (1.0 - ADAM_B2 ** ADAM_STEP)
    delta = -ADAM_LR * (m_hat / (jnp.sqrt(v_hat) + ADAM_EPS) + ADAM_WD * w)
    return delta, m, v


def _adamw_big(w, g, m, v, i):
    _, k, r, cdim = BIG[i]
    nblk = 4 if k == 1 else 1

    def body(w_ref, g_ref, m_ref, v_ref, d_ref, nm_ref, nv_ref, go_ref):
        gv = g_ref[...]
        d_ref[...], nm_ref[...], nv_ref[...] = _adamw_math(w_ref[...], gv, m_ref[...], v_ref[...])
        go_ref[...] = gv

    spec = pl.BlockSpec((1, r // nblk, cdim), lambda t, b: (t, b, 0))
    return pl.pallas_call(
        body, name=f"adamw_{BIG[i][0]}", grid=(k, nblk), in_specs=[spec] * 4, out_specs=[spec] * 4,
        out_shape=[_sds((k, r, cdim), F32)] * 4,
        compiler_params=_params(("parallel", "parallel")),
    )(w, g, m, v)


def _small_start(pack, after):
    def body(pack_ref, land_ref, after_ref, send_sems, recv_sems, pack_thru, land_thru, token):
        x, y, c, _ = _place()
        for r in range(1, 8):
            peer = (x if not r & 4 else 1 - x, y if not r & 2 else 1 - y, c if not r & 1 else 1 - c)
            _remote(pack_ref, land_ref.at[r - 1], send_sems.at[r - 1], recv_sems.at[r - 1], peer).start()
        token[...] = jnp.zeros(TOKEN, F32)

    land = lax.empty((7, SMALL_ROWS, D), F32)
    outs = pl.pallas_call(
        body, name="small_start", in_specs=[HBM, HBM, ANY],
        out_specs=(SEM, SEM, HBM, HBM, pl.BlockSpec(memory_space=pltpu.VMEM)),
        out_shape=(pltpu.SemaphoreType.DMA((7,)), pltpu.SemaphoreType.DMA((7,)), pltpu.HBM(pack.shape, F32),
                   pltpu.HBM(land.shape, F32), _sds(TOKEN, F32)),
        input_output_aliases={0: 2, 1: 3},
        compiler_params=pltpu.CompilerParams(has_side_effects=EFFECT),
    )(pltpu.with_memory_space_constraint(pack, pltpu.HBM), pltpu.with_memory_space_constraint(land, pltpu.HBM), after)
    return outs


def _small_wait(send_sems, recv_sems, pack, land, after):
    def body(pack_ref, land_ref, send_sems, recv_sems, *rest):
        x, y, c, _ = _place()
        for r in range(1, 8):
            peer = (x if not r & 4 else 1 - x, y if not r & 2 else 1 - y, c if not r & 1 else 1 - c)
            cp = _remote(pack_ref, land_ref.at[r - 1], send_sems.at[r - 1], recv_sems.at[r - 1], peer)
            cp.wait_send()
            cp.wait_recv()

    return pl.pallas_call(
        body, name="small_wait", in_specs=[HBM, HBM, SEM, SEM] + [ANY] * len(after), out_specs=(HBM, HBM),
        out_shape=(pltpu.HBM(pack.shape, F32), pltpu.HBM(land.shape, F32)),
        input_output_aliases={0: 0, 1: 1},
        compiler_params=pltpu.CompilerParams(has_side_effects=EFFECT),
    )(pack, land, send_sems, recv_sems, *after)


def _small_update(place, pack, land, ws, ms, vs):
    n = len(ws)

    def body(pref, pack_ref, land_ref, *refs):
        chip = pref[1]
        me = 2 * chip + pref[0]
        own = pack_ref[...]
        tot = None
        for dev in range(8):
            r = jnp.bitwise_xor(me, dev)
            term = jnp.where(r == 0, own, land_ref[jnp.maximum(r - 1, 0)])
            tot = term if tot is None else tot + term
        out, buf = refs[3 * n:-1], refs[-1]
        buf[...] = tot
        g_conv = jnp.zeros((3, SH_O), F32)
        for s in range(4):
            g_conv = g_conv + jnp.where(chip == s, buf[24:27, s * SH_O:(s + 1) * SH_O], 0.0)
        gs = [buf[0:2, :], buf[8:10, :], buf[16:17, :], g_conv]
        out[0][...] = buf[32:33, 0:128]
        for i in range(n):
            d, nm, nv = _adamw_math(refs[i][...], gs[i], refs[n + i][...], refs[2 * n + i][...])
            out[1 + i][...] = gs[i]
            out[1 + n + i][...] = d
            out[1 + 2 * n + i][...] = nm
            out[1 + 3 * n + i][...] = nv

    def full(shape):
        nd = len(shape)
        return pl.BlockSpec(shape, lambda i, pref: (0,) * nd)

    specs = [full(w.shape) for w in ws]
    grid_spec = pltpu.PrefetchScalarGridSpec(
        num_scalar_prefetch=1, grid=(1,),
        in_specs=[full(pack.shape), full(land.shape)] + specs * 3, out_specs=[full((1, 128))] + specs * 4,
        scratch_shapes=[pltpu.VMEM((SMALL_ROWS, D), F32)])
    outs = pl.pallas_call(
        body, name="small_update", grid_spec=grid_spec,
        out_shape=[_sds((1, 128), F32)] + [_sds(w.shape, F32) for w in ws] * 4,
        compiler_params=_params(("arbitrary",)),
    )(place, pack, land, *ws, *ms, *vs)
    return outs[0], outs[1:1 + n], outs[1 + n:1 + 2 * n], outs[1 + 2 * n:1 + 3 * n], outs[1 + 3 * n:]


def _adamw_layer(ws, gs, ms, vs, idx, name):
    n = len(idx)
    dims = [(BIG[w][1], BIG[w][2] // 4, BIG[w][3]) for w in idx]

    def body(*refs):
        for i in range(n):
            gv = refs[n + i][...]
            d, nm, nv = _adamw_math(refs[i][...], gv, refs[2 * n + i][...], refs[3 * n + i][...])
            refs[4 * n + i][...] = d
            refs[5 * n + i][...] = nm
            refs[6 * n + i][...] = nv
            refs[7 * n + i][...] = gv

    specs = [pl.BlockSpec((k, q, cdim), lambda t: (0, t, 0)) for k, q, cdim in dims]
    outs = pl.pallas_call(
        body, name=name, grid=(4,), in_specs=specs * 4, out_specs=specs * 4,
        out_shape=[_sds(BIG[w][1:], F32) for w in idx] * 4,
        compiler_params=_params(("parallel",)),
    )(*ws, *gs, *ms, *vs)
    return [tuple(outs[j * n + i] for j in range(4)) for i in range(n)]


def _adamw_small(ws, gs, ms, vs):
    n = len(ws)

    def body(*refs):
        for i in range(n):
            w_ref, g_ref, m_ref, v_ref = refs[i], refs[n + i], refs[2 * n + i], refs[3 * n + i]
            d, nm, nv = _adamw_math(w_ref[...], g_ref[...], m_ref[...], v_ref[...])
            refs[4 * n + i][...] = d
            refs[5 * n + i][...] = nm
            refs[6 * n + i][...] = nv

    specs = [_full(w.shape) for w in ws]
    outs = pl.pallas_call(
        body, name="adamw_small", grid=(1,), in_specs=specs * 4, out_specs=specs * 3,
        out_shape=[_sds(w.shape, F32) for w in ws] * 3,
        compiler_params=_params(("arbitrary",)),
    )(*ws, *gs, *ms, *vs)
    return outs[:n], outs[n:2 * n], outs[2 * n:]


def _pad_rows(a, rows):
    return jnp.pad(a, ((0, rows - a.shape[0]), (0, 0)))


def kernel(x, mem, positions, norm_g, mem_norm_g, w_mem_kv, attn_w_in, attn_w_out, conv_w_in, conv_w, conv_w_out, final_g, loss_target, m_norm_g, m_mem_norm_g, m_w_mem_kv, m_attn_w_in, m_attn_w_out, m_conv_w_in, m_conv_w, m_conv_w_out, m_final_g, v_norm_g, v_mem_norm_g, v_w_mem_kv, v_attn_w_in, v_attn_w_out, v_conv_w_in, v_conv_w, v_conv_w_out, v_final_g):
    mx, my, mc = lax.axis_index("x"), lax.axis_index("y"), lax.axis_index("c")
    place = jnp.stack([mc, 2 * mx + my]).astype(jnp.int32)

    w_big = [w_mem_kv, attn_w_in, attn_w_out, conv_w_in, conv_w_out]
    m_big = [m_w_mem_kv, m_attn_w_in, m_attn_w_out, m_conv_w_in, m_conv_w_out]
    v_big = [v_w_mem_kv, v_attn_w_in, v_attn_w_out, v_conv_w_in, v_conv_w_out]
    first, rest = (1,), (0, 2, 3, 4)
    wb1 = _cast_weights(place, [w_big[i] for i in first], place, first, "cast_w_in_a")
    a1_send, a1_recv, a1_bufs, a1_token = _gather_start(wb1, place, first, "gather_a1_start")
    wbr = _cast_weights(place, [w_big[i] for i in rest], a1_token, rest, "cast_weights")
    rest = (0, 2)
    a2_send, a2_recv, a2_bufs, a2_token = _gather_start([wbr[0], wbr[1]], a1_token, rest, "gather_a2_start")
    gb_send, gb_recv, gb_bufs, gb_token = _gather_start([wbr[2], wbr[3]], a2_token, LAYER_B, "gather_b_start")

    xs, tgt = x[0], loss_target[0]
    g0, g1 = norm_g[0:1], norm_g[1:2]
    rc, rs1, rs2 = _rope_tables(positions[0].astype(F32).reshape(S, 1), gb_token)
    a1_bufs = _gather_wait(a1_send, a1_recv, a1_bufs, [rc], first, "gather_a1_wait")
    w_in_a = _gather_forward(a1_bufs, first, "gather_a1_forward")[0].reshape(4, D, SH_A)
    hn0, q, k, v, qm0, z0 = _in_proj_a(xs, g0, w_in_a, rc, rs1, rs2, gb_token)
    a2_bufs = _gather_wait(a2_send, a2_recv, a2_bufs, [q], rest, "gather_a2_wait")
    f2_send, f2_recv, a2_bufs, f2_token = _forward_start(a2_bufs, None, q, rest, "forward_a2_start")
    fwd = [_attn_fwd(q, k, v, 0, f2_token)]
    fwd.append(_attn_fwd(q, k, v, 1, fwd[0][0]))
    cw_own = _pad_rows(conv_w[0], CW_ROWS)
    gb_bufs = _gather_wait(gb_send, gb_recv, gb_bufs, [fwd[1][0]], LAYER_B, "gather_b_wait")
    fb_send, fb_recv, gb_bufs, fb_token = _forward_start(gb_bufs, cw_own, fwd[1][0], LAYER_B, "forward_b_start")
    fwd.append(_attn_fwd(q, k, v, 2, fb_token))
    os_, ls, lss = [f[0] for f in fwd], [f[1] for f in fwd], [f[2] for f in fwd]
    wkv_f, w_out_a = _forward_wait(f2_send, f2_recv, a2_bufs, [os_[2]], rest, False, "forward_a2_wait")
    w_out_a = w_out_a.reshape(4, BR_A, SH_O)
    memn, kv = _mem_fwd(mem[0], mem_norm_g, wkv_f)
    h1 = _attn_out(os_, ls, qm0, kv[0], z0, xs, w_out_a)

    w_in_b, w_out_b, _, cw_f = _forward_wait(fb_send, fb_recv, gb_bufs, [h1], LAYER_B, True, "forward_b_wait")
    w_in_b = w_in_b.reshape(4, D, SH_B)
    w_out_b = w_out_b.reshape(BR_B, D)
    cw_f = lax.dynamic_update_slice(cw_f, cw_own[None], (2 * mx + my, 0, 0))
    cw8 = cw_f.transpose(1, 0, 2).reshape(CW_ROWS, D)
    hn1, bg, cg, u, qm1, z1 = _in_proj_b(h1, g1, w_in_b)
    dh2, loss_part, dfg = _conv_out_loss(bg, cg, u, cw8, qm1, kv[1], z1, h1, w_out_b, final_g.reshape(1, D), tgt)

    dproj_b, dw_out_b, dcw, dkv1, dw_out_b16 = _conv_bwd(dh2, bg, cg, u, cw8, qm1, kv[1], z1, w_out_b)
    dw_in_b, dw_in_b16 = _w_in_grad(hn1, dproj_b, IN_B, "w_in_b_grad")
    gs_b = [dw_in_b.reshape(4, 1, D, SH_B), dw_out_b.reshape(4, 1, BR_B // 4, D)]
    gb_b = [dw_in_b16.reshape(4, 1, D, SH_B), dw_out_b16.reshape(4, 1, BR_B // 4, D)]
    pb_send, pb_recv, gb_b, pb_land, pb_token = _pair_start(gb_b, LAYER_B, "pair_b_start")
    dh1, dg1 = _in_proj_bwd(dproj_b, w_in_b, h1, g1, dh2, pb_token, IN_B, "in_proj_b_bwd")
    _, r1_b = _pair_wait(pb_send, pb_recv, gb_b, pb_land, [dh1], LAYER_B, "pair_b_wait")
    ps_b = _pair_sums(place, gs_b, r1_b, LAYER_B, "pair_sums_b")
    cb_send, cb_recv, cb_src, cb_land, cb_token = _chip_start(ps_b, LAYER_B, "chip_b_start")

    outs = _attn_out_bwd(dh1, os_, ls, qm0, kv[0], z0, w_out_a, cb_token)
    dos, dds, dqm, dz, dw_out_a, dkv0, dw_out_a16 = outs[0:3], outs[3:6], outs[6], outs[7], outs[8], outs[9], outs[10]
    bwd = [_attn_bwd(q, k, v, dos[g], lss[g], dds[g], g) for g in range(3)]
    dproj_a = _qkv_bwd([b[0] for b in bwd], [b[1] for b in bwd], [b[2] for b in bwd], dqm, dz, rc, rs1, rs2)
    dw_in_a, dw_in_a16 = _w_in_grad(hn0, dproj_a, IN_A, "w_in_a_grad")
    dwkv, dwkv16, dmg = _mem_bwd(mem[0], mem_norm_g, memn, wkv_f, dkv0, dkv1)

    gs_a = [dwkv, dw_in_a.reshape(4, 1, D, SH_A), dw_out_a.reshape(4, 1, BR_A, SH_O)]
    r1_a = _pair_exchange([dwkv16, dw_in_a16.reshape(4, 1, D, SH_A), dw_out_a16.reshape(4, 1, BR_A, SH_O)], LAYER_A,
                          "pair_exchange_a")
    ps_a = _pair_sums(place, gs_a, r1_a, LAYER_A, "pair_sums_a")
    ca_send, ca_recv, ca_src, ca_land, ca_token = _chip_start(ps_a, LAYER_A, "chip_a_start")

    gx, dg0 = _in_proj_bwd(dproj_a, w_in_a, xs, g0, dh1, ca_token, IN_A, "in_proj_a_bwd")
    pack = jnp.concatenate([_pad_rows(jnp.concatenate([dg0, dg1], axis=0), 8), _pad_rows(dmg, 8), _pad_rows(dfg, 8),
                            dcw, _pad_rows(jnp.pad(loss_part, ((0, 0), (0, D - 128))), 8)], axis=0)
    sm_send, sm_recv, pack, sm_land, sm_token = _small_start(pack, ca_token)
    r2_b = _chip_wait(cb_send, cb_recv, cb_src, cb_land, [ca_token], LAYER_B, "chip_b_wait")
    hs_b = _chip_sums(place, gs_b, r1_b, r2_b, LAYER_B, "chip_sums_b")
    g_b = _pair_gather(hs_b, LAYER_B, "pair_gather_b")
    upd_b = _adamw_layer([w_big[w] for w in LAYER_B], g_b, [m_big[w] for w in LAYER_B], [v_big[w] for w in LAYER_B],
                         LAYER_B, "adamw_b")
    r2_a = _chip_wait(ca_send, ca_recv, ca_src, ca_land, [gx, upd_b[0][0], upd_b[1][0], sm_token], LAYER_A,
                      "chip_a_wait")
    hs_a = _chip_sums(place, gs_a, r1_a, r2_a, LAYER_A, "chip_sums_a")
    g_a = _pair_gather(hs_a, LAYER_A, "pair_gather_a")
    upd_a = _adamw_layer([w_big[w] for w in LAYER_A], g_a, [m_big[w] for w in LAYER_A], [v_big[w] for w in LAYER_A],
                         LAYER_A, "adamw_a")
    upd = upd_a + upd_b
    g_big = [u[3] for u in upd]
    pack, sm_land = _small_wait(sm_send, sm_recv, pack, sm_land, [r2_a[0]])
    sw = [norm_g, mem_norm_g, final_g.reshape(1, D), conv_w[0]]
    sm = [m_norm_g, m_mem_norm_g, m_final_g.reshape(1, D), m_conv_w[0]]
    sv = [v_norm_g, v_mem_norm_g, v_final_g.reshape(1, D), v_conv_w[0]]
    loss_row, sg, sd, snm, snv = _small_update(place, pack, sm_land, sw, sm, sv)
    loss = loss_row[0, 0]
    g_norm, g_memnorm, g_final, g_conv = sg

    def order(norm, memnorm, wkv, w_in_a, w_out_a, w_in_b, conv, w_out_b, final):
        return (norm, memnorm, wkv, w_in_a, w_out_a, w_in_b, conv.reshape(1, 3, SH_O), w_out_b, final.reshape(D))

    grads = order(g_norm, g_memnorm, g_big[0], g_big[1], g_big[2], g_big[3], g_conv, g_big[4], g_final)
    deltas = order(sd[0], sd[1], upd[0][0], upd[1][0], upd[2][0], upd[3][0], sd[3], upd[4][0], sd[2])
    new_m = order(snm[0], snm[1], upd[0][1], upd[1][1], upd[2][1], upd[3][1], snm[3], upd[4][1], snm[2])
    new_v = order(snv[0], snv[1], upd[0][2], upd[1][2], upd[2][2], upd[3][2], snv[3], upd[4][2], snv[2])
    return (loss, gx[None], *grads, *deltas, *new_m, *new_v)
```

```python
import functools

import numpy as np
import jax
import jax.numpy as jnp
from jax import lax
from jax.experimental import pallas as pl
from jax.experimental.pallas import tpu as pltpu

F32 = jnp.float32
BF16 = jnp.bfloat16

S = 2048
D = 1024
TM = 256
NT = S // TM
HD = 64
GW = 512
NQ = 3 * GW
MW = 256
NM = 256
IN_A = 3 * NQ + MW + GW + MW
IN_B = 3 * D + MW + D + MW
BR_A = GW + MW
BR_B = D + MW
SH_A = IN_A // 4
SH_B = IN_B // 4
SH_O = D // 4
QBLK = 128
DILATIONS = (1, 4, 16)
EPS = 1e-6
SCALE = HD ** -0.5
NEG = -1e30
ROPE_THETA = 500000.0

ADAM_LR = 0.001
ADAM_B1 = 0.9
ADAM_B2 = 0.999
ADAM_EPS = 1e-08
ADAM_WD = 0.01
ADAM_STEP = 10

VMEM_LIMIT_BYTES = 60 * 1024 * 1024


def _params(sem=None):
    if sem is None:
        return pltpu.CompilerParams(vmem_limit_bytes=VMEM_LIMIT_BYTES)
    return pltpu.CompilerParams(dimension_semantics=sem, vmem_limit_bytes=VMEM_LIMIT_BYTES)


def _full(shape):
    nd = len(shape)
    return pl.BlockSpec(shape, lambda *_: (0,) * nd)


def _rows(width, tm=TM):
    return pl.BlockSpec((tm, width), lambda i: (i, 0))


def _sds(shape, dtype):
    return jax.ShapeDtypeStruct(shape, dtype)


def _silu_parts(z):
    sig = 0.5 * jnp.tanh(0.5 * z) + 0.5
    return z * sig, sig * (1.0 + z * (1.0 - sig))


def _dot(a, b):
    return jnp.dot(a, b, preferred_element_type=F32)


def _dot_nt(a, b):
    return lax.dot_general(a, b, (((1,), (1,)), ((), ())), preferred_element_type=F32)


def _dot_tn(a, b):
    return lax.dot_general(a, b, (((0,), (0,)), ((), ())), preferred_element_type=F32)


def _rope_fwd(t, c, s1, s2):
    return t * c + pltpu.roll(t, 120, 1) * s1 + pltpu.roll(t, 8, 1) * s2


def _rope_bwd(g, c, s1, s2):
    return g * c + pltpu.roll(g * s1, 8, 1) + pltpu.roll(g * s2, 120, 1)


MEM_HEADS = MW // HD


def _stack_heads(x):
    head = lax.broadcasted_iota(jnp.int32, x.shape, 1) // HD
    return jnp.concatenate([jnp.where(head == h, x, 0.0) for h in range(MEM_HEADS)], axis=0).astype(BF16)


def _unstack_heads(x4):
    tm = x4.shape[0] // MEM_HEADS
    head = lax.broadcasted_iota(jnp.int32, (tm, MW), 1) // HD
    out = x4[:tm]
    for h in range(1, MEM_HEADS):
        out = jnp.where(head == h, x4[h * tm:(h + 1) * tm], out)
    return out


def _mem_attn(qm, kv):
    q4 = _stack_heads(qm.astype(F32))
    s = _dot_nt(q4, kv[:, :MW]) * SCALE
    e = jnp.exp(s - jnp.max(s, axis=-1, keepdims=True))
    p = e * (1.0 / jnp.sum(e, axis=-1, keepdims=True))
    return p, _unstack_heads(_dot(p.astype(BF16), kv[:, MW:])), q4


def _mem_attn_bwd(dmo, p, mo, q4, kv, dkv_ref):
    tm = dmo.shape[0]
    head = lax.broadcasted_iota(jnp.int32, dmo.shape, 1) // HD
    prod = dmo * mo
    delta = jnp.concatenate([jnp.sum(jnp.where(head == h, prod, 0.0), axis=-1, keepdims=True)
                             for h in range(MEM_HEADS)], axis=0)
    d4 = _stack_heads(dmo)
    ds = (p * (_dot_nt(d4, kv[:, MW:]) - delta) * SCALE).astype(BF16)
    dkv_ref[:, :MW] += _dot_tn(ds, q4)
    dkv_ref[:, MW:] += _dot_tn(p.astype(BF16), d4)
    return _unstack_heads(_dot(ds, kv[:, :MW]))


def _merge(o_refs, l_refs):
    ls = [r[...] for r in l_refs]
    m = jnp.maximum(jnp.maximum(ls[0], ls[1]), ls[2])
    es = [jnp.exp(l - m) for l in ls]
    inv = 1.0 / (es[0] + es[1] + es[2])
    ws = [e * inv for e in es]
    os_ = [r[...] for r in o_refs]
    mix = ws[0] * os_[0] + ws[1] * os_[1] + ws[2] * os_[2]
    return ws, mix


def _conv_taps(cg, u, cgp, up, first):
    a = cg * u
    ap = jnp.where(first, 0.0, cgp * up)
    row = lax.broadcasted_iota(jnp.int32, a.shape, 0)
    a1 = jnp.where(row == 0, ap[7:8, :], pltpu.roll(a, 1, 0))
    a2 = jnp.where(row == 0, ap[6:7, :], jnp.where(row == 1, ap[7:8, :], pltpu.roll(a, 2, 0)))
    return a, a1, a2


def _rope_tables(posf, after):
    half = 8
    invf = np.float32(ROPE_THETA) ** (-np.arange(half, dtype=np.float32) * np.float32(2.0 / 16))
    lane = np.arange(128)
    table = np.where((lane % HD) < 16, invf[lane % half], 0.0).astype(np.float32)[None, :]

    def body(pos_ref, invf_ref, c_ref, s1_ref, s2_ref):
        ang = pos_ref[...] * invf_ref[...]
        jm = lax.broadcasted_iota(jnp.int32, ang.shape, 1) & (HD - 1)
        cs = jnp.cos(ang)
        sn = jnp.sin(ang)
        c_ref[...] = jnp.where(jm < 16, cs, 1.0)
        s1_ref[...] = jnp.where(jm < 8, -sn, 0.0)
        s2_ref[...] = jnp.where((jm >= 8) & (jm < 16), sn, 0.0)

    out = _sds((S, 128), F32)
    return pl.pallas_call(
        functools.partial(_skip_arg, body, 2), name="rope_tables", grid=(NT,),
        in_specs=[_rows(1), _full((1, 128)), pl.BlockSpec(memory_space=pl.ANY)],
        out_specs=[_rows(128)] * 3, out_shape=[out] * 3,
        compiler_params=_params(("parallel",)),
    )(posf, jnp.asarray(table), after)


def _in_proj_a(x, g0, w_in, c, s1, s2, after):
    def body(x_ref, g_ref, w_ref, c_ref, s1_ref, s2_ref, hn_ref, q_ref, k_ref, v_ref, qm_ref, z_ref, proj):
        xf = x_ref[...]
        hn = xf * lax.rsqrt(jnp.mean(xf * xf, axis=-1, keepdims=True) + EPS) * g_ref[...]
        hb = hn.astype(BF16)
        hn_ref[...] = hb
        for s in range(4):
            proj[:, s * SH_A:(s + 1) * SH_A] = _dot(hb, w_ref[s])
        cc, a1, a2 = c_ref[...], s1_ref[...], s2_ref[...]
        for j in range(NQ // 128):
            q_ref[:, j * 128:(j + 1) * 128] = (
                _rope_fwd(proj[:, j * 128:(j + 1) * 128], cc, a1, a2) * SCALE).astype(BF16)
            k_ref[:, j * 128:(j + 1) * 128] = _rope_fwd(
                proj[:, NQ + j * 128:NQ + (j + 1) * 128], cc, a1, a2).astype(BF16)
        v_ref[...] = proj[:, 2 * NQ:3 * NQ].astype(BF16)
        qm_ref[...] = proj[:, 3 * NQ:3 * NQ + MW].astype(BF16)
        z_ref[...] = proj[:, 3 * NQ + MW:]

    return pl.pallas_call(
        functools.partial(_skip_arg, body, 6), name="in_proj_a", grid=(NT,),
        in_specs=[_rows(D), _full((1, D)), _full((4, D, SH_A)), _rows(128), _rows(128), _rows(128),
                  pl.BlockSpec(memory_space=pl.ANY)],
        out_specs=[_rows(D), _rows(NQ), _rows(NQ), _rows(NQ), _rows(MW), _rows(BR_A)],
        out_shape=[_sds((S, D), BF16), _sds((S, NQ), BF16), _sds((S, NQ), BF16), _sds((S, NQ), BF16),
                   _sds((S, MW), BF16), _sds((S, BR_A), F32)],
        scratch_shapes=[pltpu.VMEM((TM, IN_A), F32)],
        compiler_params=_params(("parallel",)),
    )(x, g0, w_in, c, s1, s2, after)


def _mem_fwd(mem, mg, wkv):
    def body(mem_ref, mg_ref, w_ref, memn_ref, kv_ref):
        mf = mem_ref[...]
        n = mf * lax.rsqrt(jnp.mean(mf * mf, axis=-1, keepdims=True) + EPS)
        for i in range(2):
            mn = (n * mg_ref[i:i + 1, :]).astype(BF16)
            memn_ref[i] = mn
            acc = _dot(mn[:, 0:NM], w_ref[0, i])
            for s in range(1, 4):
                acc += _dot(mn[:, s * NM:(s + 1) * NM], w_ref[s, i])
            kv_ref[i] = acc.astype(BF16)

    return pl.pallas_call(
        body, name="mem_fwd", grid=(1,),
        in_specs=[_full((NM, D)), _full((2, D)), _full((4, 2, NM, 2 * MW))],
        out_specs=[_full((2, NM, D)), _full((2, NM, 2 * MW))],
        out_shape=[_sds((2, NM, D), BF16), _sds((2, NM, 2 * MW), BF16)],
        compiler_params=_params(("arbitrary",)),
    )(mem, mg, wkv)


def _band_mask(j):
    qi = lax.broadcasted_iota(jnp.int32, (QBLK, 2 * QBLK), 0)
    kj = lax.broadcasted_iota(jnp.int32, (QBLK, 2 * QBLK), 1)
    dist = qi + QBLK - kj
    return (dist >= 0) & (dist <= QBLK) & ((kj >= QBLK) | (j > 0))


LANES = 128
NCHUNK = GW // LANES
FWD_UNROLL = 16
BWD_UNROLL = 4


def _perm_matrix(d):
    n = TM // d
    p = np.zeros((TM, TM), np.float32)
    for r in range(d):
        for i in range(n):
            p[r * n + i, i * d + r] = 1.0
    return p


def _split_dot(p, x, parts):
    hi = x.astype(BF16)
    rem = x - hi.astype(F32)
    lo = rem.astype(BF16)
    both = _dot(p, jnp.concatenate([hi, lo], axis=1))
    acc = both[:, :LANES] + both[:, LANES:]
    if parts == 3:
        acc = acc + _dot(p, (rem - lo.astype(F32)).astype(BF16))
    return acc


def _pair_dot(p, a, b):
    both = _dot(p, jnp.concatenate([a, b], axis=1))
    return both[:, :LANES], both[:, LANES:]


def _tile_to_streams(y, dst, t, d):
    n, ln = TM // d, S // d
    for r in range(d):
        dst[r * ln + t * n:r * ln + (t + 1) * n, :] = y[r * n:(r + 1) * n].astype(dst.dtype)


def _tile_from_streams(src, t, d):
    n, ln = TM // d, S // d
    return jnp.concatenate([src[r * ln + t * n:r * ln + (t + 1) * n, :] for r in range(d)], axis=0)


def _head_masks():
    first = lax.broadcasted_iota(jnp.int32, (TM, LANES), 1) < HD
    return first, jnp.logical_not(first)


def _attn_fwd(q, k, v, g, after):
    d = DILATIONS[g]
    nb = S // d // QBLK
    perm = _perm_matrix(d)

    def body(q_ref, k_ref, v_ref, p_ref, pt_ref, o_ref, l_ref, ls_ref, q0, q1, ks, vs, os_):
        first, second = _head_masks()
        pm = p_ref[...]
        for t in range(NT):
            rows = slice(t * TM, (t + 1) * TM)
            if d == 1:
                qt = q_ref[rows, :].astype(F32)
            else:
                qt, kt = _pair_dot(pm, q_ref[rows, :], k_ref[rows, :])
                _tile_to_streams(kt, ks, t, d)
                _tile_to_streams(_dot(pm, v_ref[rows, :]), vs, t, d)
            _tile_to_streams(jnp.where(first, qt, 0.0), q0, t, d)
            _tile_to_streams(jnp.where(second, qt, 0.0), q1, t, d)
        kref, vref = (k_ref, v_ref) if d == 1 else (ks, vs)
        oref, lref = (o_ref, l_ref) if d == 1 else (os_, ls_ref)

        def blk(b, carry):
            r0 = pl.multiple_of(b * QBLK, QBLK)
            p0 = pl.multiple_of(jnp.maximum(b - 1, 0) * QBLK, QBLK)
            kk = jnp.concatenate([kref[pl.ds(p0, QBLK), :], kref[pl.ds(r0, QBLK), :]], axis=0)
            vv = jnp.concatenate([vref[pl.ds(p0, QBLK), :], vref[pl.ds(r0, QBLK), :]], axis=0)
            valid = _band_mask(b & (nb - 1))
            acc, lse = [], []
            for qh in (q0, q1):
                s = jnp.where(valid, _dot_nt(qh[pl.ds(r0, QBLK), :], kk), NEG)
                m = jnp.max(s, axis=-1, keepdims=True)
                e = jnp.exp(s - m)
                l = jnp.sum(e, axis=-1, keepdims=True)
                acc.append(_dot(e.astype(BF16), vv) * (1.0 / l))
                lse.append(m + jnp.log(l))
            f = first[:QBLK]
            oref[pl.ds(r0, QBLK), :] = jnp.where(f, acc[0], acc[1])
            lref[pl.ds(r0, QBLK), :] = jnp.where(f, lse[0], lse[1])
            return carry

        lax.fori_loop(0, S // QBLK, blk, 0, unroll=FWD_UNROLL)
        if d > 1:
            ptm = pt_ref[...]
            for t in range(NT):
                rows = slice(t * TM, (t + 1) * TM)
                o_ref[rows, :] = _split_dot(ptm, _tile_from_streams(os_, t, d), 2)
                l_ref[rows, :] = _split_dot(ptm, _tile_from_streams(ls_ref, t, d), 3)

    qkv_spec = pl.BlockSpec((S, LANES), lambda c: (0, g * NCHUNK + c))
    out_spec = pl.BlockSpec((S, LANES), lambda c: (0, c))
    n_out = 2 if d == 1 else 3
    inner = body if d > 1 else functools.partial(_drop_arg, body, 7)
    outs = pl.pallas_call(
        functools.partial(_skip_arg, inner, 5), name=f"attn_fwd_g{g}", grid=(NCHUNK,),
        in_specs=[qkv_spec] * 3 + [_full((TM, TM))] * 2 + [pl.BlockSpec(memory_space=pl.ANY)],
        out_specs=[out_spec] * n_out, out_shape=[_sds((S, GW), F32)] * n_out,
        scratch_shapes=[pltpu.VMEM((S, LANES), BF16)] * 4 + [pltpu.VMEM((S, LANES), F32)],
        compiler_params=_params(("parallel",)),
    )(q, k, v, jnp.asarray(perm, BF16), jnp.asarray(perm.T, BF16), after)
    return (outs[0], outs[1], outs[1]) if d == 1 else tuple(outs)


def _drop_arg(body, pos, *refs):
    return body(*refs[:pos], None, *refs[pos:])


def _attn_out(os_, ls, qm, kv0, z, x, w_out):
    def body(o0, o1, o2, l0, l1, l2, qm_ref, kv_ref, z_ref, x_ref, w_ref, h_ref, ybuf):
        _, mix = _merge((o0, o1, o2), (l0, l1, l2))
        sz, _ = _silu_parts(z_ref[...])
        ybuf[:, :GW] = (mix * sz[:, :GW]).astype(BF16)
        _, mo, _ = _mem_attn(qm_ref[...], kv_ref[...])
        ybuf[:, GW:] = (mo * sz[:, GW:]).astype(BF16)
        yb = ybuf[...]
        for s in range(4):
            cs = slice(s * SH_O, (s + 1) * SH_O)
            h_ref[:, cs] = x_ref[:, cs] + _dot(yb, w_ref[s])

    return pl.pallas_call(
        body, name="attn_out", grid=(NT,),
        in_specs=[_rows(GW)] * 6 + [_rows(MW), _full((NM, 2 * MW)), _rows(BR_A), _rows(D), _full((4, BR_A, SH_O))],
        out_specs=_rows(D), out_shape=_sds((S, D), F32),
        scratch_shapes=[pltpu.VMEM((TM, BR_A), BF16)],
        compiler_params=_params(("parallel",)),
    )(*os_, *ls, qm, kv0, z, x, w_out)


def _in_proj_b(h1, g1, w_in):
    def body(x_ref, g_ref, w_ref, hn_ref, bg_ref, cg_ref, u_ref, qm_ref, z_ref, proj):
        xf = x_ref[...]
        hn = xf * lax.rsqrt(jnp.mean(xf * xf, axis=-1, keepdims=True) + EPS) * g_ref[...]
        hb = hn.astype(BF16)
        hn_ref[...] = hb
        for s in range(4):
            proj[:, s * SH_B:(s + 1) * SH_B] = _dot(hb, w_ref[s])
        bg_ref[...] = proj[:, :D]
        cg_ref[...] = proj[:, D:2 * D]
        u_ref[...] = proj[:, 2 * D:3 * D]
        qm_ref[...] = proj[:, 3 * D:3 * D + MW].astype(BF16)
        z_ref[...] = proj[:, 3 * D + MW:]

    return pl.pallas_call(
        body, name="in_proj_b", grid=(NT,),
        in_specs=[_rows(D), _full((1, D)), _full((4, D, SH_B))],
        out_specs=[_rows(D), _rows(D), _rows(D), _rows(D), _rows(MW), _rows(BR_B)],
        out_shape=[_sds((S, D), BF16), _sds((S, D), F32), _sds((S, D), F32), _sds((S, D), F32),
                   _sds((S, MW), BF16), _sds((S, BR_B), F32)],
        scratch_shapes=[pltpu.VMEM((TM, IN_B), F32)],
        compiler_params=_params(("parallel",)),
    )(h1, g1, w_in)


def _prev8(width):
    return pl.BlockSpec((8, width), lambda i: (jnp.maximum(i * (TM // 8) - 1, 0), 0))


def _conv_out_loss(bg, cg, u, cw, qm, kv1, z, h1, w_out, fg, tgt):
    def body(bg_ref, cg_ref, u_ref, cgp_ref, up_ref, cw_ref, qm_ref, kv_ref, z_ref, h_ref, w_ref, fg_ref, t_ref,
             dh_ref, loss_ref, dfg_ref, ybuf):
        i = pl.program_id(0)
        a, a1, a2 = _conv_taps(cg_ref[...], u_ref[...], cgp_ref[...], up_ref[...], i == 0)
        conv = cw_ref[0:1, :] * a2 + cw_ref[1:2, :] * a1 + cw_ref[2:3, :] * a
        sz, _ = _silu_parts(z_ref[...])
        ybuf[:, :D] = (bg_ref[...] * conv * sz[:, :D]).astype(BF16)
        _, mo, _ = _mem_attn(qm_ref[...], kv_ref[...])
        ybuf[:, D:] = (mo * sz[:, D:]).astype(BF16)
        h2 = h_ref[...] + _dot(ybuf[...], w_ref[...])
        rstd = lax.rsqrt(jnp.mean(h2 * h2, axis=-1, keepdims=True) + EPS)
        n = h2 * rstd
        fgv = fg_ref[...]
        err = n * fgv - t_ref[...]
        dout = err * (1.0 / D)
        dn = dout * fgv
        dh_ref[...] = rstd * (dn - n * jnp.mean(dn * n, axis=-1, keepdims=True))

        @pl.when(i == 0)
        def _():
            loss_ref[...] = jnp.zeros_like(loss_ref)
            dfg_ref[...] = jnp.zeros_like(dfg_ref)

        loss_ref[...] += jnp.sum(err * err) * (0.5 / D)
        dfg_ref[...] += jnp.sum(dout * n, axis=0, keepdims=True)

    return pl.pallas_call(
        body, name="conv_out_loss", grid=(NT,),
        in_specs=[_rows(D), _rows(D), _rows(D), _prev8(D), _prev8(D), _full((8, D)), _rows(MW),
                  _full((NM, 2 * MW)), _rows(BR_B), _rows(D), _full((BR_B, D)), _full((1, D)), _rows(D)],
        out_specs=[_rows(D), _full((1, 128)), _full((1, D))],
        out_shape=[_sds((S, D), F32), _sds((1, 128), F32), _sds((1, D), F32)],
        scratch_shapes=[pltpu.VMEM((TM, BR_B), BF16)],
        compiler_params=_params(("arbitrary",)),
    )(bg, cg, u, cg, u, cw, qm, kv1, z, h1, w_out, fg, tgt)


def _conv_bwd(dh2, bg, cg, u, cw, qm, kv1, z, w_out):
    rev = lambda i: (NT - 1 - i, 0)
    rows = lambda w: pl.BlockSpec((TM, w), rev)
    prev8 = pl.BlockSpec((8, D), lambda i: (jnp.maximum((NT - 1 - i) * (TM // 8) - 1, 0), 0))

    def body(dh_ref, bg_ref, cg_ref, u_ref, cgp_ref, up_ref, cw_ref, qm_ref, kv_ref, z_ref, w_ref,
             dproj_ref, dw_ref, dcw_ref, dkv_ref, dwb_ref, ybuf, carry):
        i = pl.program_id(0)

        @pl.when(i == 0)
        def _():
            dw_ref[...] = jnp.zeros_like(dw_ref)
            dcw_ref[...] = jnp.zeros_like(dcw_ref)
            dkv_ref[...] = jnp.zeros_like(dkv_ref)
            carry[...] = jnp.zeros_like(carry)

        bgv, cgv, uv = bg_ref[...], cg_ref[...], u_ref[...]
        a, a1, a2 = _conv_taps(cgv, uv, cgp_ref[...], up_ref[...], i == NT - 1)
        w0, w1, w2 = cw_ref[0:1, :], cw_ref[1:2, :], cw_ref[2:3, :]
        conv = w0 * a2 + w1 * a1 + w2 * a
        mix = bgv * conv
        sz, dsz = _silu_parts(z_ref[...])
        kvv = kv_ref[...]
        p, mo, q4 = _mem_attn(qm_ref[...], kvv)
        ybuf[:, :D] = (mix * sz[:, :D]).astype(BF16)
        ybuf[:, D:] = (mo * sz[:, D:]).astype(BF16)
        dhb = dh_ref[...].astype(BF16)
        dw_ref[...] += _dot_tn(ybuf[...], dhb)
        dy = _dot_nt(dhb, w_ref[...])
        dcat = dy * sz
        dproj_ref[:, 3 * D + MW:3 * D + MW + D] = (dy[:, :D] * mix * dsz[:, :D]).astype(BF16)
        dproj_ref[:, 3 * D + MW + D:] = (dy[:, D:] * mo * dsz[:, D:]).astype(BF16)
        dmix = dcat[:, :D]
        dproj_ref[:, :D] = (dmix * conv).astype(BF16)
        dc = dmix * bgv
        nxt = carry[...]
        row = lax.broadcasted_iota(jnp.int32, dc.shape, 0)
        dc1 = jnp.where(row == TM - 1, nxt[0:1, :], pltpu.roll(dc, TM - 1, 0))
        dc2 = jnp.where(row == TM - 2, nxt[0:1, :], jnp.where(row == TM - 1, nxt[1:2, :], pltpu.roll(dc, TM - 2, 0)))
        carry[...] = dc[0:8, :]
        da = w2 * dc + w1 * dc1 + w0 * dc2
        dproj_ref[:, D:2 * D] = (da * uv).astype(BF16)
        dproj_ref[:, 2 * D:3 * D] = (da * cgv).astype(BF16)
        dcw_ref[0:1, :] += jnp.sum(dc * a2, axis=0, keepdims=True)
        dcw_ref[1:2, :] += jnp.sum(dc * a1, axis=0, keepdims=True)
        dcw_ref[2:3, :] += jnp.sum(dc * a, axis=0, keepdims=True)
        dproj_ref[:, 3 * D:3 * D + MW] = _mem_attn_bwd(dcat[:, D:], p, mo, q4, kvv, dkv_ref).astype(BF16)

        @pl.when(i == NT - 1)
        def _():
            dwb_ref[...] = dw_ref[...].astype(BF16)

    return pl.pallas_call(
        body, name="conv_bwd", grid=(NT,),
        in_specs=[rows(D), rows(D), rows(D), rows(D), prev8, prev8, _full((8, D)), rows(MW),
                  _full((NM, 2 * MW)), rows(BR_B), _full((BR_B, D))],
        out_specs=[rows(IN_B), _full((BR_B, D)), _full((8, D)), _full((NM, 2 * MW)), _full((BR_B, D))],
        out_shape=[_sds((S, IN_B), BF16), _sds((BR_B, D), F32), _sds((8, D), F32), _sds((NM, 2 * MW), F32),
                   _sds((BR_B, D), BF16)],
        scratch_shapes=[pltpu.VMEM((TM, BR_B), BF16), pltpu.VMEM((8, D), F32)],
        compiler_params=_params(("arbitrary",)),
    )(dh2, bg, cg, u, cg, u, cw, qm, kv1, z, w_out)


def _in_proj_bwd(dproj, w_in, xin, g, dres, after, width, name):
    sh = width // 4

    def body(dp_ref, w_ref, x_ref, g_ref, dr_ref, dx_ref, dg_ref):
        i = pl.program_id(0)
        dhn = _dot_nt(dp_ref[:, 0:sh], w_ref[0])
        for s in range(1, 4):
            dhn += _dot_nt(dp_ref[:, s * sh:(s + 1) * sh], w_ref[s])
        xf = x_ref[...]
        rstd = lax.rsqrt(jnp.mean(xf * xf, axis=-1, keepdims=True) + EPS)
        n = xf * rstd
        dn = dhn * g_ref[...]
        dx_ref[...] = dr_ref[...] + rstd * (dn - n * jnp.mean(dn * n, axis=-1, keepdims=True))

        @pl.when(i == 0)
        def _():
            dg_ref[...] = jnp.zeros_like(dg_ref)

        dg_ref[...] += jnp.sum(dhn * n, axis=0, keepdims=True)

    return pl.pallas_call(
        functools.partial(_skip_arg, body, 5), name=name, grid=(NT,),
        in_specs=[_rows(width), _full((4, D, sh)), _rows(D), _full((1, D)), _rows(D), pl.BlockSpec(memory_space=pl.ANY)],
        out_specs=[_rows(D), _full((1, D))],
        out_shape=[_sds((S, D), F32), _sds((1, D), F32)],
        compiler_params=_params(("arbitrary",)),
    )(dproj, w_in, xin, g, dres, after)


def _w_in_grad(hn, dproj, width, name):
    sh = width // 4

    def body(hn_ref, dp_ref, dw_ref, dwb_ref):
        dw = _dot_tn(hn_ref[...], dp_ref[...])
        dw_ref[0] = dw
        dwb_ref[0] = dw.astype(BF16)

    spec = pl.BlockSpec((1, D, sh), lambda s: (s, 0, 0))
    return pl.pallas_call(
        body, name=name, grid=(4,),
        in_specs=[_full((S, D)), pl.BlockSpec((S, sh), lambda s: (0, s))],
        out_specs=[spec, spec], out_shape=[_sds((4, D, sh), F32), _sds((4, D, sh), BF16)],
        compiler_params=_params(("parallel",)),
    )(hn, dproj)


def _attn_out_bwd(dh1, os_, ls, qm, kv0, z, w_out, after):
    ones_bd = np.kron(np.eye(GW // HD, dtype=np.float32), np.ones((HD, HD), np.float32))

    def body(dh_ref, o0, o1, o2, l0, l1, l2, qm_ref, kv_ref, z_ref, w_ref, bd_ref,
             do0, do1, do2, dd0, dd1, dd2, dqm_ref, dz_ref, dw_ref, dkv_ref, dwb_ref, ybuf):
        i = pl.program_id(0)

        @pl.when(i == 0)
        def _():
            dw_ref[...] = jnp.zeros_like(dw_ref)
            dkv_ref[...] = jnp.zeros_like(dkv_ref)

        ws, mix = _merge((o0, o1, o2), (l0, l1, l2))
        sz, dsz = _silu_parts(z_ref[...])
        kvv = kv_ref[...]
        p, mo, q4 = _mem_attn(qm_ref[...], kvv)
        ybuf[:, :GW] = (mix * sz[:, :GW]).astype(BF16)
        ybuf[:, GW:] = (mo * sz[:, GW:]).astype(BF16)
        yb = ybuf[...]
        dh = dh_ref[...]
        dy = None
        for s in range(4):
            dhb = dh[:, s * SH_O:(s + 1) * SH_O].astype(BF16)
            dw_ref[s] += _dot_tn(yb, dhb)
            part = _dot_nt(dhb, w_ref[s])
            dy = part if dy is None else dy + part
        dcat = dy * sz
        dz_ref[:, :GW] = (dy[:, :GW] * mix * dsz[:, :GW]).astype(BF16)
        dz_ref[:, GW:] = (dy[:, GW:] * mo * dsz[:, GW:]).astype(BF16)
        dmix = dcat[:, :GW]
        prod = dmix * mix
        hi = prod.astype(BF16)
        lo = (prod - hi.astype(F32)).astype(BF16)
        bd = bd_ref[...]
        tot = _dot(hi, bd) + _dot(lo, bd)
        for w, do_ref, dd_ref in zip(ws, (do0, do1, do2), (dd0, dd1, dd2)):
            do_ref[...] = (w * dmix).astype(BF16)
            dd_ref[...] = w * tot

        dqm_ref[...] = _mem_attn_bwd(dcat[:, GW:], p, mo, q4, kvv, dkv_ref).astype(BF16)

        @pl.when(i == NT - 1)
        def _():
            dwb_ref[...] = dw_ref[...].astype(BF16)

    return pl.pallas_call(
        functools.partial(_skip_arg, body, 12), name="attn_out_bwd", grid=(NT,),
        in_specs=[_rows(D)] + [_rows(GW)] * 6 + [_rows(MW), _full((NM, 2 * MW)), _rows(BR_A),
                                                   _full((4, BR_A, SH_O)), _full((GW, GW)),
                                                   pl.BlockSpec(memory_space=pl.ANY)],
        out_specs=[_rows(GW)] * 6 + [_rows(MW), _rows(BR_A), _full((4, BR_A, SH_O)), _full((NM, 2 * MW)),
                                     _full((4, BR_A, SH_O))],
        out_shape=[_sds((S, GW), BF16)] * 3 + [_sds((S, GW), F32)] * 3 + [
            _sds((S, MW), BF16), _sds((S, BR_A), BF16), _sds((4, BR_A, SH_O), F32), _sds((NM, 2 * MW), F32),
            _sds((4, BR_A, SH_O), BF16)],
        scratch_shapes=[pltpu.VMEM((TM, BR_A), BF16)],
        compiler_params=_params(("arbitrary",)),
    )(dh1, *os_, *ls, qm, kv0, z, w_out, jnp.asarray(ones_bd, dtype=BF16), after)


def _attn_bwd(q, k, v, do, lse_s, dd, g):
    d = DILATIONS[g]
    nb = S // d // QBLK
    perm = _perm_matrix(d)

    def body(q_ref, k_ref, v_ref, do_ref, l_ref, dd_ref, p_ref, pt_ref, dq_ref, dk_ref, dv_ref,
             q0, q1, g0, g1, ks, vs, dds, dqs, dks, dvs):
        first, second = _head_masks()
        pm = p_ref[...]
        for t in range(NT):
            rows = slice(t * TM, (t + 1) * TM)
            if d == 1:
                qt = q_ref[rows, :].astype(F32)
                gt = do_ref[rows, :].astype(F32)
            else:
                qt, gt = _pair_dot(pm, q_ref[rows, :], do_ref[rows, :])
                kt, vt = _pair_dot(pm, k_ref[rows, :], v_ref[rows, :])
                _tile_to_streams(kt, ks, t, d)
                _tile_to_streams(vt, vs, t, d)
                _tile_to_streams(_split_dot(pm, dd_ref[rows, :], 2), dds, t, d)
            _tile_to_streams(jnp.where(first, qt, 0.0), q0, t, d)
            _tile_to_streams(jnp.where(second, qt, 0.0), q1, t, d)
            _tile_to_streams(jnp.where(first, gt, 0.0), g0, t, d)
            _tile_to_streams(jnp.where(second, gt, 0.0), g1, t, d)
        kref, vref, ddref = (k_ref, v_ref, dd_ref) if d == 1 else (ks, vs, dds)
        dqref, dkref, dvref = dqs, dks, dvs
        dkref[...] = jnp.zeros_like(dkref)
        dvref[...] = jnp.zeros_like(dvref)

        def blk(b, carry):
            r0 = pl.multiple_of(b * QBLK, QBLK)
            p0 = pl.multiple_of(jnp.maximum(b - 1, 0) * QBLK, QBLK)
            kk = jnp.concatenate([kref[pl.ds(p0, QBLK), :], kref[pl.ds(r0, QBLK), :]], axis=0)
            vv = jnp.concatenate([vref[pl.ds(p0, QBLK), :], vref[pl.ds(r0, QBLK), :]], axis=0)
            lb = l_ref[pl.ds(r0, QBLK), :]
            ddb = ddref[pl.ds(r0, QBLK), :]
            lcol = jnp.concatenate([lb[:, 0:1], lb[:, HD:HD + 1]], axis=0)
            dcol = jnp.concatenate([ddb[:, 0:1], ddb[:, HD:HD + 1]], axis=0)
            valid = _band_mask(b & (nb - 1))
            valid2 = jnp.concatenate([valid, valid], axis=0)
            qq = jnp.concatenate([q0[pl.ds(r0, QBLK), :], q1[pl.ds(r0, QBLK), :]], axis=0)
            gg = jnp.concatenate([g0[pl.ds(r0, QBLK), :], g1[pl.ds(r0, QBLK), :]], axis=0)
            p = jnp.where(valid2, jnp.exp(_dot_nt(qq, kk) - lcol), 0.0)
            ds = (p * (_dot_nt(gg, vv) - dcol)).astype(BF16)
            dq2 = _dot(ds, kk)
            dqref[pl.ds(r0, QBLK), :] = jnp.where(first[:QBLK], dq2[:QBLK], dq2[QBLK:])
            dkk = _dot_tn(ds, qq)
            dvv = _dot_tn(p.astype(BF16), gg)
            dkref[pl.ds(p0, QBLK), :] += dkk[:QBLK]
            dkref[pl.ds(r0, QBLK), :] += dkk[QBLK:]
            dvref[pl.ds(p0, QBLK), :] += dvv[:QBLK]
            dvref[pl.ds(r0, QBLK), :] += dvv[QBLK:]
            return carry

        lax.fori_loop(0, S // QBLK, blk, 0, unroll=BWD_UNROLL)

        ptm = pt_ref[...] if d > 1 else None
        for t in range(NT):
            rows = slice(t * TM, (t + 1) * TM)
            if d == 1:
                dq_ref[rows, :] = dqs[rows, :].astype(BF16)
                dk_ref[rows, :] = dks[rows, :].astype(BF16)
                dv_ref[rows, :] = dvs[rows, :].astype(BF16)
            else:
                tq, tk = _pair_dot(ptm, _tile_from_streams(dqs, t, d).astype(BF16),
                                   _tile_from_streams(dks, t, d).astype(BF16))
                dq_ref[rows, :] = tq.astype(BF16)
                dk_ref[rows, :] = tk.astype(BF16)
                dv_ref[rows, :] = _dot(ptm, _tile_from_streams(dvs, t, d).astype(BF16)).astype(BF16)

    qkv_spec = pl.BlockSpec((S, LANES), lambda c: (0, g * NCHUNK + c))
    one_spec = pl.BlockSpec((S, LANES), lambda c: (0, c))
    return pl.pallas_call(
        body, name=f"attn_bwd_g{g}", grid=(NCHUNK,),
        in_specs=[qkv_spec] * 3 + [one_spec] * 3 + [_full((TM, TM))] * 2, out_specs=[one_spec] * 3,
        out_shape=[_sds((S, GW), BF16)] * 3,
        scratch_shapes=[pltpu.VMEM((S, LANES), BF16)] * 6 + [pltpu.VMEM((S, LANES), F32)] * 4,
        compiler_params=_params(("parallel",)),
    )(q, k, v, do, lse_s, dd, jnp.asarray(perm, BF16), jnp.asarray(perm.T, BF16))


def _qkv_bwd(dqs, dks, dvs, dqm, dz, c, s1, s2):
    def body(q0, q1, q2, k0, k1, k2, v0, v1, v2, dqm_ref, dz_ref, c_ref, s1_ref, s2_ref, dp_ref):
        cc, a1, a2 = c_ref[...], s1_ref[...], s2_ref[...]
        for g, (qr, kr, vr) in enumerate(((q0, k0, v0), (q1, k1, v1), (q2, k2, v2))):
            for j in range(GW // 128):
                ls_ = slice(j * 128, (j + 1) * 128)
                c0 = g * GW + j * 128
                dp_ref[:, c0:c0 + 128] = (_rope_bwd(qr[:, ls_].astype(F32), cc, a1, a2) * SCALE).astype(BF16)
                dp_ref[:, NQ + c0:NQ + c0 + 128] = _rope_bwd(kr[:, ls_].astype(F32), cc, a1, a2).astype(BF16)
            dp_ref[:, 2 * NQ + g * GW:2 * NQ + (g + 1) * GW] = vr[...]
        dp_ref[:, 3 * NQ:3 * NQ + MW] = dqm_ref[...]
        dp_ref[:, 3 * NQ + MW:] = dz_ref[...]

    return pl.pallas_call(
        body, name="qkv_bwd", grid=(NT,),
        in_specs=[_rows(GW)] * 9 + [_rows(MW), _rows(BR_A), _rows(128), _rows(128), _rows(128)],
        out_specs=_rows(IN_A), out_shape=_sds((S, IN_A), BF16),
        compiler_params=_params(("parallel",)),
    )(*dqs, *dks, *dvs, dqm, dz, c, s1, s2)


def _mem_bwd(mem, mg, memn, wkv, dkv0, dkv1):
    def body(mem_ref, mg_ref, memn_ref, w_ref, d0_ref, d1_ref, dw_ref, dwb_ref, dg_ref):
        mf = mem_ref[...]
        n = mf * lax.rsqrt(jnp.mean(mf * mf, axis=-1, keepdims=True) + EPS)
        for i, d_ref in enumerate((d0_ref, d1_ref)):
            dkv = d_ref[...].astype(BF16)
            mn = memn_ref[i]
            for s in range(4):
                cs = slice(s * NM, (s + 1) * NM)
                dw = _dot_tn(mn[:, cs], dkv)
                dw_ref[s, i] = dw
                dwb_ref[s, i] = dw.astype(BF16)
                dmn = _dot_nt(dkv, w_ref[s, i])
                dg_ref[i:i + 1, cs] = jnp.sum(dmn * n[:, cs], axis=0, keepdims=True)

    return pl.pallas_call(
        body, name="mem_bwd", grid=(1,),
        in_specs=[_full((NM, D)), _full((2, D)), _full((2, NM, D)), _full((4, 2, NM, 2 * MW)),
                  _full((NM, 2 * MW)), _full((NM, 2 * MW))],
        out_specs=[_full((4, 2, NM, 2 * MW)), _full((4, 2, NM, 2 * MW)), _full((2, D))],
        out_shape=[_sds((4, 2, NM, 2 * MW), F32), _sds((4, 2, NM, 2 * MW), BF16), _sds((2, D), F32)],
        compiler_params=_params(("arbitrary",)),
    )(mem, mg, memn, wkv, dkv0, dkv1)


MESH = pl.DeviceIdType.MESH
ANY = pl.BlockSpec(memory_space=pl.ANY)
BIG = (("wkv", 2, NM, 2 * MW), ("w_in_a", 1, D, SH_A), ("w_out_a", 1, BR_A, SH_O),
       ("w_in_b", 1, D, SH_B), ("w_out_b", 1, BR_B // 4, D))
NBIG = len(BIG)
CW_ROWS = 8


def _place():
    x, y, c = lax.axis_index("x"), lax.axis_index("y"), lax.axis_index("c")
    chips = ((1 - x, y), (x, 1 - y), (1 - x, 1 - y))
    return x, y, c, chips


def _remote(src, dst, ssem, rsem, dev):
    return pltpu.make_async_remote_copy(src_ref=src, dst_ref=dst, send_sem=ssem, recv_sem=rsem,
                                        device_id=dev, device_id_type=MESH)


def _cast_weights(place, ws, after, idx, name):
    nblk = 4
    n = len(idx)
    dims = [BIG[w][1:] for w in idx]

    def body(pref, *refs):
        for i in range(n):
            refs[n + 1 + i][0] = refs[i][...].astype(BF16)

    grid_spec = pltpu.PrefetchScalarGridSpec(
        num_scalar_prefetch=1, grid=(nblk,),
        in_specs=[pl.BlockSpec((k, r // nblk, cdim), lambda i, pref: (0, i, 0)) for k, r, cdim in dims]
        + [pl.BlockSpec(memory_space=pl.ANY)],
        out_specs=[pl.BlockSpec((1, k, r // nblk, cdim), lambda i, pref: (pref[1], 0, i, 0)) for k, r, cdim in dims])
    return pl.pallas_call(
        body, name=name, grid_spec=grid_spec,
        out_shape=[_sds((4, k, r, cdim), BF16) for k, r, cdim in dims],
        compiler_params=_params(("parallel",)),
    )(place, *ws, after)


LAYER_A = (0, 1, 2)
LAYER_B = (3, 4)
HBM = pl.BlockSpec(memory_space=pltpu.HBM)
SEM = pl.BlockSpec(memory_space=pltpu.SEMAPHORE)
EFFECT = pltpu.SideEffectType.DATAFLOW_SIDE_EFFECTING
TOKEN = (8, 128)


def _half(ref, w, which):
    h = BIG[w][2] // 2
    return ref.at[:, pl.ds(which * h, h), :]


def _skip_arg(body, pos, *refs):
    return body(*refs[:pos], *refs[pos + 1:])


def _gather_weights(wb, cw, idx, name):
    n = len(idx)

    def body(*refs):
        src_cw = refs[n]
        dst = refs[n + 1:2 * n + 2]
        loc_sem, send_sems, recv_sems, fsend_sems, frecv_sems = refs[2 * n + 2:]
        x, y, c, chips = _place()
        me = 2 * x + y
        loc = pltpu.make_async_copy(src_cw, dst[n].at[me], loc_sem)
        loc.start()
        sends = []
        for j, (px, py) in enumerate(chips):
            for i in range(n):
                mine = _half(dst[i].at[me], idx[i], c)
                sends.append(_remote(mine, mine, send_sems.at[j, i], recv_sems.at[j, i], (px, py, c)))
            sends.append(_remote(src_cw, dst[n].at[me], send_sems.at[j, n], recv_sems.at[j, n], (px, py, c)))
        for cp in sends:
            cp.start()
        fwds = []
        for j, (px, py) in enumerate(chips):
            for i in range(n):
                got = _half(dst[i].at[2 * px + py], idx[i], c)
                _remote(got, got, send_sems.at[j, i], recv_sems.at[j, i], (px, py, c)).wait_recv()
                fwds.append(_remote(got, got, fsend_sems.at[j, i], frecv_sems.at[j, i], (x, y, 1 - c)))
                fwds[-1].start()
            got = dst[n].at[2 * px + py]
            _remote(got, got, send_sems.at[j, n], recv_sems.at[j, n], (px, py, c)).wait_recv()
        for j, (px, py) in enumerate(chips):
            for i in range(n):
                got = _half(dst[i].at[2 * px + py], idx[i], 1 - c)
                _remote(got, got, fsend_sems.at[j, i], frecv_sems.at[j, i], (x, y, 1 - c)).wait_recv()
        for cp in sends + fwds:
            cp.wait_send()
        loc.wait()

    out_shape = [_sds(w.shape, BF16) for w in wb] + [_sds((4, CW_ROWS, SH_O), F32)]
    return pl.pallas_call(
        body, name=name, in_specs=[ANY] * (n + 1), out_specs=[ANY] * (n + 1), out_shape=out_shape,
        input_output_aliases={i: i for i in range(n)},
        scratch_shapes=[pltpu.SemaphoreType.DMA, pltpu.SemaphoreType.DMA((3, n + 1)),
                        pltpu.SemaphoreType.DMA((3, n + 1)), pltpu.SemaphoreType.DMA((3, n)),
                        pltpu.SemaphoreType.DMA((3, n))],
    )(*wb, cw)


def _gather_start(wb, after, idx, name):
    n = len(idx)

    def body(*refs):
        src = refs[:n]
        send_sems, recv_sems = refs[n + 1], refs[n + 2]
        token = refs[2 * n + 3]
        x, y, c, chips = _place()
        me = 2 * x + y
        for j, (px, py) in enumerate(chips):
            for i in range(n):
                mine = _half(src[i].at[me], idx[i], c)
                _remote(mine, mine, send_sems.at[j * n + i], recv_sems.at[j * n + i], (px, py, c)).start()
        token[...] = jnp.zeros(TOKEN, F32)

    outs = pl.pallas_call(
        body, name=name, in_specs=[HBM] * n + [ANY],
        out_specs=(SEM, SEM) + (HBM,) * n + (pl.BlockSpec(memory_space=pltpu.VMEM),),
        out_shape=(pltpu.SemaphoreType.DMA((3 * n,)), pltpu.SemaphoreType.DMA((3 * n,)))
        + tuple(pltpu.HBM(w.shape, w.dtype) for w in wb) + (_sds(TOKEN, F32),),
        input_output_aliases={i: 2 + i for i in range(n)},
        compiler_params=pltpu.CompilerParams(has_side_effects=EFFECT),
    )(*[pltpu.with_memory_space_constraint(w, pltpu.HBM) for w in wb], after)
    return outs[0], outs[1], list(outs[2:2 + n]), outs[2 + n]


def _gather_wait(send_sems, recv_sems, wb, after, idx, name, started=None):
    n = len(idx)
    started = idx if started is None else started
    n_all = len(started)
    pos = [started.index(w) for w in idx]

    def body(*refs):
        buf = refs[:n]
        send_sems, recv_sems = refs[n], refs[n + 1]
        x, y, c, chips = _place()
        me = 2 * x + y
        for j, (px, py) in enumerate(chips):
            for i in range(n):
                mine = _half(buf[i].at[me], idx[i], c)
                got = _half(buf[i].at[2 * px + py], idx[i], c)
                k = j * n_all + pos[i]
                _remote(mine, mine, send_sems.at[k], recv_sems.at[k], (px, py, c)).wait_send()
                _remote(got, got, send_sems.at[k], recv_sems.at[k], (px, py, c)).wait_recv()

    outs = pl.pallas_call(
        body, name=name, in_specs=[HBM] * n + [SEM, SEM] + [ANY] * len(after), out_specs=(HBM,) * n,
        out_shape=tuple(pltpu.HBM(w.shape, w.dtype) for w in wb),
        input_output_aliases={i: i for i in range(n)},
        compiler_params=pltpu.CompilerParams(has_side_effects=EFFECT),
    )(*wb, send_sems, recv_sems, *after)
    return list(outs)


def _gather_forward(wb, idx, name, cw=None):
    n = len(idx)
    m = n if cw is None else n + 1

    def body(*refs):
        dst = refs[m:2 * m]
        send_sems, recv_sems = refs[2 * m], refs[2 * m + 1]
        x, y, c, chips = _place()
        cps = []
        for j, (px, py) in enumerate(chips):
            for i in range(n):
                got = _half(dst[i].at[2 * px + py], idx[i], c)
                cps.append(_remote(got, got, send_sems.at[j, i], recv_sems.at[j, i], (x, y, 1 - c)))
                cps[-1].start()
        if cw is not None:
            src_cw, loc_sem = refs[n], refs[2 * m + 2]
            me = 2 * x + y
            loc = pltpu.make_async_copy(src_cw, dst[n].at[me], loc_sem)
            loc.start()
            for j, (px, py) in enumerate(chips):
                cps.append(_remote(src_cw, dst[n].at[me], send_sems.at[j, n], recv_sems.at[j, n], (px, py, c)))
                cps[-1].start()
        for j, (px, py) in enumerate(chips):
            for i in range(n):
                got = _half(dst[i].at[2 * px + py], idx[i], 1 - c)
                _remote(got, got, send_sems.at[j, i], recv_sems.at[j, i], (x, y, 1 - c)).wait_recv()
            if cw is not None:
                got = dst[n].at[2 * px + py]
                _remote(got, got, send_sems.at[j, n], recv_sems.at[j, n], (px, py, c)).wait_recv()
        for cp in cps:
            cp.wait_send()
        if cw is not None:
            loc.wait()

    out_shape = [_sds(w.shape, BF16) for w in wb]
    scratch = [pltpu.SemaphoreType.DMA((3, m)), pltpu.SemaphoreType.DMA((3, m))]
    args = list(wb)
    if cw is not None:
        out_shape.append(_sds((4, CW_ROWS, SH_O), F32))
        scratch.append(pltpu.SemaphoreType.DMA)
        args.append(cw)
    return pl.pallas_call(
        body, name=name, in_specs=[ANY] * m, out_specs=[ANY] * m, out_shape=out_shape,
        input_output_aliases={i: i for i in range(n)}, scratch_shapes=scratch,
    )(*args)


def _forward_start(wb, cw, after, idx, name):
    n = len(idx)
    m = n if cw is None else n + 2

    def body(*refs):
        buf = refs[:n]
        send_sems, recv_sems = refs[m + 1], refs[m + 2]
        token = refs[2 * m + 3]
        x, y, c, chips = _place()
        for j, (px, py) in enumerate(chips):
            for i in range(n):
                got = _half(buf[i].at[2 * px + py], idx[i], c)
                _remote(got, got, send_sems.at[j * (n + 1) + i], recv_sems.at[j * (n + 1) + i], (x, y, 1 - c)).start()
            if cw is not None:
                _remote(refs[n], refs[n + 1].at[2 * x + y], send_sems.at[j * (n + 1) + n],
                        recv_sems.at[j * (n + 1) + n], (px, py, c)).start()
        token[...] = jnp.zeros(TOKEN, F32)

    arrays = list(wb) if cw is None else list(wb) + [cw, lax.empty((4, CW_ROWS, SH_O), F32)]
    outs = pl.pallas_call(
        body, name=name, in_specs=[HBM] * m + [ANY],
        out_specs=(SEM, SEM) + (HBM,) * m + (pl.BlockSpec(memory_space=pltpu.VMEM),),
        out_shape=(pltpu.SemaphoreType.DMA((3 * (n + 1),)), pltpu.SemaphoreType.DMA((3 * (n + 1),)))
        + tuple(pltpu.HBM(a.shape, a.dtype) for a in arrays) + (_sds(TOKEN, F32),),
        input_output_aliases={i: 2 + i for i in range(m)},
        compiler_params=pltpu.CompilerParams(has_side_effects=EFFECT),
    )(*[pltpu.with_memory_space_constraint(a, pltpu.HBM) for a in arrays], after)
    return outs[0], outs[1], list(outs[2:2 + m]), outs[2 + m]


def _forward_wait(send_sems, recv_sems, arrays, after, idx, with_cw, name):
    n = len(idx)
    m = len(arrays)

    def body(*refs):
        buf = refs[:n]
        send_sems, recv_sems = refs[m], refs[m + 1]
        x, y, c, chips = _place()
        for j, (px, py) in enumerate(chips):
            for i in range(n):
                sent = _half(buf[i].at[2 * px + py], idx[i], c)
                got = _half(buf[i].at[2 * px + py], idx[i], 1 - c)
                k = j * (n + 1) + i
                _remote(sent, sent, send_sems.at[k], recv_sems.at[k], (x, y, 1 - c)).wait_send()
                _remote(got, got, send_sems.at[k], recv_sems.at[k], (x, y, 1 - c)).wait_recv()
            if with_cw:
                k = j * (n + 1) + n
                theirs = refs[n + 1].at[2 * px + py]
                _remote(refs[n], theirs, send_sems.at[k], recv_sems.at[k], (px, py, c)).wait_send()
                _remote(refs[n], theirs, send_sems.at[k], recv_sems.at[k], (px, py, c)).wait_recv()

    outs = pl.pallas_call(
        body, name=name, in_specs=[HBM] * m + [SEM, SEM] + [ANY] * len(after), out_specs=(HBM,) * m,
        out_shape=tuple(pltpu.HBM(a.shape, a.dtype) for a in arrays),
        input_output_aliases={i: i for i in range(m)},
        compiler_params=pltpu.CompilerParams(has_side_effects=EFFECT),
    )(*arrays, send_sems, recv_sems, *after)
    return list(outs)


def _pair_exchange(gs, idx, name):
    n = len(idx)

    def body(*refs):
        src, dst = refs[:n], refs[n:2 * n]
        send_sems, recv_sems = refs[2 * n:]
        x, y, c, _ = _place()
        cps = []
        for i in range(n):
            h = BIG[idx[i]][2] // 2
            cps.append(_remote(src[i].at[:, :, pl.ds((1 - c) * h, h), :], dst[i], send_sems.at[i], recv_sems.at[i],
                               (x, y, 1 - c)))
            cps[-1].start()
        for cp in cps:
            cp.wait()

    return pl.pallas_call(
        body, name=name, in_specs=[ANY] * n, out_specs=[ANY] * n,
        out_shape=[_sds((4, BIG[w][1], BIG[w][2] // 2, BIG[w][3]), BF16) for w in idx],
        scratch_shapes=[pltpu.SemaphoreType.DMA((n,)), pltpu.SemaphoreType.DMA((n,))],
    )(*gs)


def _pair_start(gs, idx, name):
    n = len(idx)

    def body(*refs):
        src, land = refs[:n], refs[n:2 * n]
        send_sems, recv_sems = refs[2 * n], refs[2 * n + 1]
        token = refs[4 * n + 2]
        x, y, c, _ = _place()
        for i in range(n):
            h = BIG[idx[i]][2] // 2
            _remote(src[i].at[:, :, pl.ds((1 - c) * h, h), :], land[i], send_sems.at[i], recv_sems.at[i],
                    (x, y, 1 - c)).start()
        token[...] = jnp.zeros(TOKEN, F32)

    lands = [lax.empty((4, BIG[w][1], BIG[w][2] // 2, BIG[w][3]), BF16) for w in idx]
    arrays = list(gs) + lands
    outs = pl.pallas_call(
        body, name=name, in_specs=[HBM] * (2 * n),
        out_specs=(SEM, SEM) + (HBM,) * (2 * n) + (pl.BlockSpec(memory_space=pltpu.VMEM),),
        out_shape=(pltpu.SemaphoreType.DMA((n,)), pltpu.SemaphoreType.DMA((n,)))
        + tuple(pltpu.HBM(a.shape, a.dtype) for a in arrays) + (_sds(TOKEN, F32),),
        input_output_aliases={i: 2 + i for i in range(2 * n)},
        compiler_params=pltpu.CompilerParams(has_side_effects=EFFECT),
    )(*[pltpu.with_memory_space_constraint(a, pltpu.HBM) for a in arrays])
    return outs[0], outs[1], list(outs[2:2 + n]), list(outs[2 + n:2 + 2 * n]), outs[2 + 2 * n]


def _pair_wait(send_sems, recv_sems, gs, lands, after, idx, name):
    n = len(idx)

    def body(*refs):
        src, land = refs[:n], refs[n:2 * n]
        send_sems, recv_sems = refs[2 * n], refs[2 * n + 1]
        x, y, c, _ = _place()
        for i in range(n):
            h = BIG[idx[i]][2] // 2
            cp = _remote(src[i].at[:, :, pl.ds((1 - c) * h, h), :], land[i], send_sems.at[i], recv_sems.at[i],
                         (x, y, 1 - c))
            cp.wait_send()
            cp.wait_recv()

    arrays = list(gs) + list(lands)
    outs = pl.pallas_call(
        body, name=name, in_specs=[HBM] * (2 * n) + [SEM, SEM] + [ANY] * len(after), out_specs=(HBM,) * (2 * n),
        out_shape=tuple(pltpu.HBM(a.shape, a.dtype) for a in arrays),
        input_output_aliases={i: i for i in range(2 * n)},
        compiler_params=pltpu.CompilerParams(has_side_effects=EFFECT),
    )(*arrays, send_sems, recv_sems, *after)
    return list(outs[:n]), list(outs[n:])


def _pair_sum(place, g, r1, i):
    _, k, r, cdim = BIG[i]
    h = r // 2

    def body(pref, g_ref, r_ref, o_ref):
        o_ref[...] = (g_ref[...] + r_ref[...]).astype(BF16)

    grid_spec = pltpu.PrefetchScalarGridSpec(
        num_scalar_prefetch=1, grid=(4, k),
        in_specs=[pl.BlockSpec((1, 1, h, cdim), lambda s, t, pref: (s, t, pref[0], 0)),
                  pl.BlockSpec((1, 1, h, cdim), lambda s, t, pref: (s, t, 0, 0))],
        out_specs=pl.BlockSpec((1, 1, h, cdim), lambda s, t, pref: (s, t, 0, 0)))
    return pl.pallas_call(
        body, name=f"pair_sum_{BIG[i][0]}", grid_spec=grid_spec, out_shape=_sds((4, k, h, cdim), BF16),
        compiler_params=_params(("parallel", "parallel")),
    )(place, g, r1)


def _pair_sums(place, gs, r1s, idx, name):
    n = len(idx)
    dims = [(BIG[w][1], BIG[w][2] // 2, BIG[w][3]) for w in idx]

    def body(pref, *refs):
        for i in range(n):
            refs[2 * n + i][...] = (refs[i][...] + refs[n + i][...].astype(F32)).astype(BF16)

    mine = [pl.BlockSpec((1, k, h, cdim), lambda s, pref: (s, 0, pref[0], 0)) for k, h, cdim in dims]
    whole = [pl.BlockSpec((1, k, h, cdim), lambda s, pref: (s, 0, 0, 0)) for k, h, cdim in dims]
    grid_spec = pltpu.PrefetchScalarGridSpec(num_scalar_prefetch=1, grid=(4,), in_specs=mine + whole, out_specs=whole)
    return pl.pallas_call(
        body, name=name, grid_spec=grid_spec, out_shape=[_sds((4, k, h, cdim), BF16) for k, h, cdim in dims],
        compiler_params=_params(("parallel",)),
    )(place, *gs, *r1s)


def _chip_start(ps, idx, name):
    n = len(idx)

    def body(*refs):
        src, land = refs[:n], refs[n:2 * n]
        send_sems, recv_sems = refs[2 * n], refs[2 * n + 1]
        token = refs[4 * n + 2]
        x, y, c, chips = _place()
        for j, (px, py) in enumerate(chips):
            for i in range(n):
                _remote(src[i].at[2 * px + py], land[i].at[j], send_sems.at[j * n + i], recv_sems.at[j * n + i],
                        (px, py, c)).start()
        token[...] = jnp.zeros(TOKEN, F32)

    lands = [lax.empty((3,) + p.shape[1:], BF16) for p in ps]
    outs = pl.pallas_call(
        body, name=name, in_specs=[HBM] * (2 * n),
        out_specs=(SEM, SEM) + (HBM,) * (2 * n) + (pl.BlockSpec(memory_space=pltpu.VMEM),),
        out_shape=(pltpu.SemaphoreType.DMA((3 * n,)), pltpu.SemaphoreType.DMA((3 * n,)))
        + tuple(pltpu.HBM(a.shape, a.dtype) for a in list(ps) + lands) + (_sds(TOKEN, F32),),
        input_output_aliases={i: 2 + i for i in range(2 * n)},
        compiler_params=pltpu.CompilerParams(has_side_effects=EFFECT),
    )(*[pltpu.with_memory_space_constraint(a, pltpu.HBM) for a in list(ps) + lands])
    return outs[0], outs[1], list(outs[2:2 + n]), list(outs[2 + n:2 + 2 * n]), outs[2 + 2 * n]


def _chip_wait(send_sems, recv_sems, ps, lands, after, idx, name):
    n = len(idx)

    def body(*refs):
        src, land = refs[:n], refs[n:2 * n]
        send_sems, recv_sems = refs[2 * n], refs[2 * n + 1]
        x, y, c, chips = _place()
        for j, (px, py) in enumerate(chips):
            for i in range(n):
                cp = _remote(src[i].at[2 * px + py], land[i].at[j], send_sems.at[j * n + i], recv_sems.at[j * n + i],
                             (px, py, c))
                cp.wait_send()
                cp.wait_recv()

    arrays = list(ps) + list(lands)
    outs = pl.pallas_call(
        body, name=name, in_specs=[HBM] * (2 * n) + [SEM, SEM] + [ANY] * len(after), out_specs=(HBM,) * (2 * n),
        out_shape=tuple(pltpu.HBM(a.shape, a.dtype) for a in arrays),
        input_output_aliases={i: i for i in range(2 * n)},
        compiler_params=pltpu.CompilerParams(has_side_effects=EFFECT),
    )(*arrays, send_sems, recv_sems, *after)
    return list(outs[n:])


def _chip_sum(place, g, r1, r2, i):
    _, k, r, cdim = BIG[i]
    h = r // 2

    def body(pref, g_ref, r1_ref, r2_ref, o_ref):
        acc = g_ref[0, 0] + r1_ref[0, 0]
        for j in range(3):
            acc = acc + r2_ref[j, 0].astype(F32)
        o_ref[0] = acc

    grid_spec = pltpu.PrefetchScalarGridSpec(
        num_scalar_prefetch=1, grid=(k,),
        in_specs=[pl.BlockSpec((1, 1, h, cdim), lambda t, pref: (pref[1], t, pref[0], 0)),
                  pl.BlockSpec((1, 1, h, cdim), lambda t, pref: (pref[1], t, 0, 0)),
                  pl.BlockSpec((3, 1, h, cdim), lambda t, pref: (0, t, 0, 0))],
        out_specs=pl.BlockSpec((1, h, cdim), lambda t, pref: (t, pref[0], 0)))
    return pl.pallas_call(
        body, name=f"chip_sum_{BIG[i][0]}", grid_spec=grid_spec, out_shape=_sds((k, r, cdim), F32),
        compiler_params=_params(("parallel",)),
    )(place, g, r1, r2)


def _chip_sums(place, gs, r1s, r2s, idx, name):
    n = len(idx)
    dims = [(BIG[w][1], BIG[w][2] // 4, BIG[w][3]) for w in idx]

    def body(pref, *refs):
        for i in range(n):
            acc = refs[i][0] + refs[n + i][0].astype(F32)
            for j in range(3):
                acc = acc + refs[2 * n + i][j].astype(F32)
            refs[3 * n + i][...] = acc

    in_specs = ([pl.BlockSpec((1, k, q, cdim), lambda t, pref: (pref[1], 0, pref[0] * 2 + t, 0)) for k, q, cdim in dims]
                + [pl.BlockSpec((1, k, q, cdim), lambda t, pref: (pref[1], 0, t, 0)) for k, q, cdim in dims]
                + [pl.BlockSpec((3, k, q, cdim), lambda t, pref: (0, 0, t, 0)) for k, q, cdim in dims])
    out_specs = [pl.BlockSpec((k, q, cdim), lambda t, pref: (0, pref[0] * 2 + t, 0)) for k, q, cdim in dims]
    grid_spec = pltpu.PrefetchScalarGridSpec(num_scalar_prefetch=1, grid=(2,), in_specs=in_specs, out_specs=out_specs)
    return pl.pallas_call(
        body, name=name, grid_spec=grid_spec, out_shape=[_sds(BIG[w][1:], F32) for w in idx],
        compiler_params=_params(("parallel",)),
    )(place, *gs, *r1s, *r2s)


def _pair_gather(hs, idx, name):
    n = len(idx)

    def body(*refs):
        dst = refs[n:2 * n]
        send_sems, recv_sems = refs[2 * n:]
        x, y, c, _ = _place()
        cps = []
        for i in range(n):
            mine = _half(dst[i], idx[i], c)
            cps.append(_remote(mine, mine, send_sems.at[i], recv_sems.at[i], (x, y, 1 - c)))
            cps[-1].start()
        for i in range(n):
            theirs = _half(dst[i], idx[i], 1 - c)
            _remote(theirs, theirs, send_sems.at[i], recv_sems.at[i], (x, y, 1 - c)).wait_recv()
        for cp in cps:
            cp.wait_send()

    return pl.pallas_call(
        body, name=name, in_specs=[ANY] * n, out_specs=[ANY] * n,
        out_shape=[_sds(BIG[w][1:], F32) for w in idx],
        input_output_aliases={i: i for i in range(n)},
        scratch_shapes=[pltpu.SemaphoreType.DMA((n,)), pltpu.SemaphoreType.DMA((n,))],
    )(*hs)


SMALL_ROWS = 40


def _all_reduce_small(pack, after):
    def body(p_ref, o_ref, slots, send_sems, recv_sems):
        x, y, c, _ = _place()
        me = 4 * x + 2 * y + c
        cps = []
        for r in range(1, 8):
            peer = (x if not r & 4 else 1 - x, y if not r & 2 else 1 - y, c if not r & 1 else 1 - c)
            cps.append(_remote(p_ref, slots.at[r], send_sems.at[r - 1], recv_sems.at[r - 1], peer))
            cps[-1].start()
        slots[0] = p_ref[...]
        for cp in cps:
            cp.wait()
        acc = slots[me]
        for dev in range(1, 8):
            acc = acc + slots[jnp.bitwise_xor(me, dev)]
        o_ref[...] = acc

    vm = pl.BlockSpec(memory_space=pltpu.VMEM)
    return pl.pallas_call(
        functools.partial(_skip_arg, body, 1), name="all_reduce_small", in_specs=[vm, ANY], out_specs=vm,
        out_shape=_sds((SMALL_ROWS, D), F32),
        scratch_shapes=[pltpu.VMEM((8, SMALL_ROWS, D), F32), pltpu.SemaphoreType.DMA((7,)),
                        pltpu.SemaphoreType.DMA((7,))],
    )(pack, after)


def _adamw_math(w, g, m, v):
    m = ADAM_B1 * m + (1.0 - ADAM_B1) * g
    v = ADAM_B2 * v + (1.0 - ADAM_B2) * (g * g)
    m_hat = m / (1.0 - ADAM_B1 ** ADAM_STEP)
    v_hat = v / (1.0 - ADAM_B2 ** ADAM_STEP)
    delta = -ADAM_LR * (m_hat / (jnp.sqrt(v_hat) + ADAM_EPS) + ADAM_WD * w)
    return delta, m, v


def _adamw_big(w, g, m, v, i):
    _, k, r, cdim = BIG[i]
    nblk = 4 if k == 1 else 1

    def body(w_ref, g_ref, m_ref, v_ref, d_ref, nm_ref, nv_ref, go_ref):
        gv = g_ref[...]
        d_ref[...], nm_ref[...], nv_ref[...] = _adamw_math(w_ref[...], gv, m_ref[...], v_ref[...])
        go_ref[...] = gv

    spec = pl.BlockSpec((1, r // nblk, cdim), lambda t, b: (t, b, 0))
    return pl.pallas_call(
        body, name=f"adamw_{BIG[i][0]}", grid=(k, nblk), in_specs=[spec] * 4, out_specs=[spec] * 4,
        out_shape=[_sds((k, r, cdim), F32)] * 4,
        compiler_params=_params(("parallel", "parallel")),
    )(w, g, m, v)


def _small_start(pack, after):
    def body(pack_ref, land_ref, after_ref, send_sems, recv_sems, pack_thru, land_thru, token):
        x, y, c, _ = _place()
        for r in range(1, 8):
            peer = (x if not r & 4 else 1 - x, y if not r & 2 else 1 - y, c if not r & 1 else 1 - c)
            _remote(pack_ref, land_ref.at[r - 1], send_sems.at[r - 1], recv_sems.at[r - 1], peer).start()
        token[...] = jnp.zeros(TOKEN, F32)

    land = lax.empty((7, SMALL_ROWS, D), F32)
    outs = pl.pallas_call(
        body, name="small_start", in_specs=[HBM, HBM, ANY],
        out_specs=(SEM, SEM, HBM, HBM, pl.BlockSpec(memory_space=pltpu.VMEM)),
        out_shape=(pltpu.SemaphoreType.DMA((7,)), pltpu.SemaphoreType.DMA((7,)), pltpu.HBM(pack.shape, F32),
                   pltpu.HBM(land.shape, F32), _sds(TOKEN, F32)),
        input_output_aliases={0: 2, 1: 3},
        compiler_params=pltpu.CompilerParams(has_side_effects=EFFECT),
    )(pltpu.with_memory_space_constraint(pack, pltpu.HBM), pltpu.with_memory_space_constraint(land, pltpu.HBM), after)
    return outs


def _small_wait(send_sems, recv_sems, pack, land, after):
    def body(pack_ref, land_ref, send_sems, recv_sems, *rest):
        x, y, c, _ = _place()
        for r in range(1, 8):
            peer = (x if not r & 4 else 1 - x, y if not r & 2 else 1 - y, c if not r & 1 else 1 - c)
            cp = _remote(pack_ref, land_ref.at[r - 1], send_sems.at[r - 1], recv_sems.at[r - 1], peer)
            cp.wait_send()
            cp.wait_recv()

    return pl.pallas_call(
        body, name="small_wait", in_specs=[HBM, HBM, SEM, SEM] + [ANY] * len(after), out_specs=(HBM, HBM),
        out_shape=(pltpu.HBM(pack.shape, F32), pltpu.HBM(land.shape, F32)),
        input_output_aliases={0: 0, 1: 1},
        compiler_params=pltpu.CompilerParams(has_side_effects=EFFECT),
    )(pack, land, send_sems, recv_sems, *after)


def _small_update(place, pack, land, ws, ms, vs):
    n = len(ws)

    def body(pref, pack_ref, land_ref, *refs):
        chip = pref[1]
        me = 2 * chip + pref[0]
        own = pack_ref[...]
        tot = None
        for dev in range(8):
            r = jnp.bitwise_xor(me, dev)
            term = jnp.where(r == 0, own, land_ref[jnp.maximum(r - 1, 0)])
            tot = term if tot is None else tot + term
        out, buf = refs[3 * n:-1], refs[-1]
        buf[...] = tot
        g_conv = jnp.zeros((3, SH_O), F32)
        for s in range(4):
            g_conv = g_conv + jnp.where(chip == s, buf[24:27, s * SH_O:(s + 1) * SH_O], 0.0)
        gs = [buf[0:2, :], buf[8:10, :], buf[16:17, :], g_conv]
        out[0][...] = buf[32:33, 0:128]
        for i in range(n):
            d, nm, nv = _adamw_math(refs[i][...], gs[i], refs[n + i][...], refs[2 * n + i][...])
            out[1 + i][...] = gs[i]
            out[1 + n + i][...] = d
            out[1 + 2 * n + i][...] = nm
            out[1 + 3 * n + i][...] = nv

    def full(shape):
        nd = len(shape)
        return pl.BlockSpec(shape, lambda i, pref: (0,) * nd)

    specs = [full(w.shape) for w in ws]
    grid_spec = pltpu.PrefetchScalarGridSpec(
        num_scalar_prefetch=1, grid=(1,),
        in_specs=[full(pack.shape), full(land.shape)] + specs * 3, out_specs=[full((1, 128))] + specs * 4,
        scratch_shapes=[pltpu.VMEM((SMALL_ROWS, D), F32)])
    outs = pl.pallas_call(
        body, name="small_update", grid_spec=grid_spec,
        out_shape=[_sds((1, 128), F32)] + [_sds(w.shape, F32) for w in ws] * 4,
        compiler_params=_params(("arbitrary",)),
    )(place, pack, land, *ws, *ms, *vs)
    return outs[0], outs[1:1 + n], outs[1 + n:1 + 2 * n], outs[1 + 2 * n:1 + 3 * n], outs[1 + 3 * n:]


def _adamw_layer(ws, gs, ms, vs, idx, name):
    n = len(idx)
    dims = [(BIG[w][1], BIG[w][2] // 4, BIG[w][3]) for w in idx]

    def body(*refs):
        for i in range(n):
            gv = refs[n + i][...]
            d, nm, nv = _adamw_math(refs[i][...], gv, refs[2 * n + i][...], refs[3 * n + i][...])
            refs[4 * n + i][...] = d
            refs[5 * n + i][...] = nm
            refs[6 * n + i][...] = nv
            refs[7 * n + i][...] = gv

    specs = [pl.BlockSpec((k, q, cdim), lambda t: (0, t, 0)) for k, q, cdim in dims]
    outs = pl.pallas_call(
        body, name=name, grid=(4,), in_specs=specs * 4, out_specs=specs * 4,
        out_shape=[_sds(BIG[w][1:], F32) for w in idx] * 4,
        compiler_params=_params(("parallel",)),
    )(*ws, *gs, *ms, *vs)
    return [tuple(outs[j * n + i] for j in range(4)) for i in range(n)]


def _adamw_small(ws, gs, ms, vs):
    n = len(ws)

    def body(*refs):
        for i in range(n):
            w_ref, g_ref, m_ref, v_ref = refs[i], refs[n + i], refs[2 * n + i], refs[3 * n + i]
            d, nm, nv = _adamw_math(w_ref[...], g_ref[...], m_ref[...], v_ref[...])
            refs[4 * n + i][...] = d
            refs[5 * n + i][...] = nm
            refs[6 * n + i][...] = nv

    specs = [_full(w.shape) for w in ws]
    outs = pl.pallas_call(
        body, name="adamw_small", grid=(1,), in_specs=specs * 4, out_specs=specs * 3,
        out_shape=[_sds(w.shape, F32) for w in ws] * 3,
        compiler_params=_params(("arbitrary",)),
    )(*ws, *gs, *ms, *vs)
    return outs[:n], outs[n:2 * n], outs[2 * n:]


def _pad_rows(a, rows):
    return jnp.pad(a, ((0, rows - a.shape[0]), (0, 0)))


def kernel(x, mem, positions, norm_g, mem_norm_g, w_mem_kv, attn_w_in, attn_w_out, conv_w_in, conv_w, conv_w_out, final_g, loss_target, m_norm_g, m_mem_norm_g, m_w_mem_kv, m_attn_w_in, m_attn_w_out, m_conv_w_in, m_conv_w, m_conv_w_out, m_final_g, v_norm_g, v_mem_norm_g, v_w_mem_kv, v_attn_w_in, v_attn_w_out, v_conv_w_in, v_conv_w, v_conv_w_out, v_final_g):
    mx, my, mc = lax.axis_index("x"), lax.axis_index("y"), lax.axis_index("c")
    place = jnp.stack([mc, 2 * mx + my]).astype(jnp.int32)

    w_big = [w_mem_kv, attn_w_in, attn_w_out, conv_w_in, conv_w_out]
    m_big = [m_w_mem_kv, m_attn_w_in, m_attn_w_out, m_conv_w_in, m_conv_w_out]
    v_big = [v_w_mem_kv, v_attn_w_in, v_attn_w_out, v_conv_w_in, v_conv_w_out]
    first, rest = (1,), (0, 2, 3, 4)
    wb1 = _cast_weights(place, [w_big[i] for i in first], place, first, "cast_w_in_a")
    a1_send, a1_recv, a1_bufs, a1_token = _gather_start(wb1, place, first, "gather_a1_start")
    wbr = _cast_weights(place, [w_big[i] for i in rest], a1_token, rest, "cast_weights")
    r_send, r_recv, r_bufs, gb_token = _gather_start(wbr, a1_token, rest, "gather_rest_start")
    a2_send, a2_recv, gb_send, gb_recv = r_send, r_recv, r_send, r_recv
    a2_bufs, gb_bufs = r_bufs[:2], r_bufs[2:]
    started, rest = rest, (0, 2)

    xs, tgt = x[0], loss_target[0]
    g0, g1 = norm_g[0:1], norm_g[1:2]
    rc, rs1, rs2 = _rope_tables(positions[0].astype(F32).reshape(S, 1), gb_token)
    a1_bufs = _gather_wait(a1_send, a1_recv, a1_bufs, [rc], first, "gather_a1_wait")
    w_in_a = _gather_forward(a1_bufs, first, "gather_a1_forward")[0].reshape(4, D, SH_A)
    hn0, q, k, v, qm0, z0 = _in_proj_a(xs, g0, w_in_a, rc, rs1, rs2, gb_token)
    a2_bufs = _gather_wait(a2_send, a2_recv, a2_bufs, [q], rest, "gather_a2_wait", started)
    f2_send, f2_recv, a2_bufs, f2_token = _forward_start(a2_bufs, None, q, rest, "forward_a2_start")
    fwd = [_attn_fwd(q, k, v, 0, f2_token)]
    fwd.append(_attn_fwd(q, k, v, 1, fwd[0][0]))
    cw_own = _pad_rows(conv_w[0], CW_ROWS)
    gb_bufs = _gather_wait(gb_send, gb_recv, gb_bufs, [fwd[1][0]], LAYER_B, "gather_b_wait", started)
    fb_send, fb_recv, gb_bufs, fb_token = _forward_start(gb_bufs, cw_own, fwd[1][0], LAYER_B, "forward_b_start")
    fwd.append(_attn_fwd(q, k, v, 2, fb_token))
    os_, ls, lss = [f[0] for f in fwd], [f[1] for f in fwd], [f[2] for f in fwd]
    wkv_f, w_out_a = _forward_wait(f2_send, f2_recv, a2_bufs, [os_[2]], rest, False, "forward_a2_wait")
    w_out_a = w_out_a.reshape(4, BR_A, SH_O)
    memn, kv = _mem_fwd(mem[0], mem_norm_g, wkv_f)
    h1 = _attn_out(os_, ls, qm0, kv[0], z0, xs, w_out_a)

    w_in_b, w_out_b, _, cw_f = _forward_wait(fb_send, fb_recv, gb_bufs, [h1], LAYER_B, True, "forward_b_wait")
    w_in_b = w_in_b.reshape(4, D, SH_B)
    w_out_b = w_out_b.reshape(BR_B, D)
    cw_f = lax.dynamic_update_slice(cw_f, cw_own[None], (2 * mx + my, 0, 0))
    cw8 = cw_f.transpose(1, 0, 2).reshape(CW_ROWS, D)
    hn1, bg, cg, u, qm1, z1 = _in_proj_b(h1, g1, w_in_b)
    dh2, loss_part, dfg = _conv_out_loss(bg, cg, u, cw8, qm1, kv[1], z1, h1, w_out_b, final_g.reshape(1, D), tgt)

    dproj_b, dw_out_b, dcw, dkv1, dw_out_b16 = _conv_bwd(dh2, bg, cg, u, cw8, qm1, kv[1], z1, w_out_b)
    dw_in_b, dw_in_b16 = _w_in_grad(hn1, dproj_b, IN_B, "w_in_b_grad")
    gs_b = [dw_in_b.reshape(4, 1, D, SH_B), dw_out_b.reshape(4, 1, BR_B // 4, D)]
    gb_b = [dw_in_b16.reshape(4, 1, D, SH_B), dw_out_b16.reshape(4, 1, BR_B // 4, D)]
    pb_send, pb_recv, gb_b, pb_land, pb_token = _pair_start(gb_b, LAYER_B, "pair_b_start")
    dh1, dg1 = _in_proj_bwd(dproj_b, w_in_b, h1, g1, dh2, pb_token, IN_B, "in_proj_b_bwd")
    _, r1_b = _pair_wait(pb_send, pb_recv, gb_b, pb_land, [dh1], LAYER_B, "pair_b_wait")
    ps_b = _pair_sums(place, gs_b, r1_b, LAYER_B, "pair_sums_b")
    cb_send, cb_recv, cb_src, cb_land, cb_token = _chip_start(ps_b, LAYER_B, "chip_b_start")

    outs = _attn_out_bwd(dh1, os_, ls, qm0, kv[0], z0, w_out_a, cb_token)
    dos, dds, dqm, dz, dw_out_a, dkv0, dw_out_a16 = outs[0:3], outs[3:6], outs[6], outs[7], outs[8], outs[9], outs[10]
    bwd = [_attn_bwd(q, k, v, dos[g], lss[g], dds[g], g) for g in range(3)]
    dproj_a = _qkv_bwd([b[0] for b in bwd], [b[1] for b in bwd], [b[2] for b in bwd], dqm, dz, rc, rs1, rs2)
    dw_in_a, dw_in_a16 = _w_in_grad(hn0, dproj_a, IN_A, "w_in_a_grad")
    dwkv, dwkv16, dmg = _mem_bwd(mem[0], mem_norm_g, memn, wkv_f, dkv0, dkv1)

    gs_a = [dwkv, dw_in_a.reshape(4, 1, D, SH_A), dw_out_a.reshape(4, 1, BR_A, SH_O)]
    r1_a = _pair_exchange([dwkv16, dw_in_a16.reshape(4, 1, D, SH_A), dw_out_a16.reshape(4, 1, BR_A, SH_O)], LAYER_A,
                          "pair_exchange_a")
    ps_a = _pair_sums(place, gs_a, r1_a, LAYER_A, "pair_sums_a")
    ca_send, ca_recv, ca_src, ca_land, ca_token = _chip_start(ps_a, LAYER_A, "chip_a_start")

    gx, dg0 = _in_proj_bwd(dproj_a, w_in_a, xs, g0, dh1, ca_token, IN_A, "in_proj_a_bwd")
    pack = jnp.concatenate([_pad_rows(jnp.concatenate([dg0, dg1], axis=0), 8), _pad_rows(dmg, 8), _pad_rows(dfg, 8),
                            dcw, _pad_rows(jnp.pad(loss_part, ((0, 0), (0, D - 128))), 8)], axis=0)
    sm_send, sm_recv, pack, sm_land, sm_token = _small_start(pack, ca_token)
    r2_b = _chip_wait(cb_send, cb_recv, cb_src, cb_land, [ca_token], LAYER_B, "chip_b_wait")
    hs_b = _chip_sums(place, gs_b, r1_b, r2_b, LAYER_B, "chip_sums_b")
    g_b = _pair_gather(hs_b, LAYER_B, "pair_gather_b")
    upd_b = _adamw_layer([w_big[w] for w in LAYER_B], g_b, [m_big[w] for w in LAYER_B], [v_big[w] for w in LAYER_B],
                         LAYER_B, "adamw_b")
    r2_a = _chip_wait(ca_send, ca_recv, ca_src, ca_land, [gx, upd_b[0][0], upd_b[1][0], sm_token], LAYER_A,
                      "chip_a_wait")
    hs_a = _chip_sums(place, gs_a, r1_a, r2_a, LAYER_A, "chip_sums_a")
    g_a = _pair_gather(hs_a, LAYER_A, "pair_gather_a")
    upd_a = _adamw_layer([w_big[w] for w in LAYER_A], g_a, [m_big[w] for w in LAYER_A], [v_big[w] for w in LAYER_A],
                         LAYER_A, "adamw_a")
    upd = upd_a + upd_b
    g_big = [u[3] for u in upd]
    pack, sm_land = _small_wait(sm_send, sm_recv, pack, sm_land, [r2_a[0]])
    sw = [norm_g, mem_norm_g, final_g.reshape(1, D), conv_w[0]]
    sm = [m_norm_g, m_mem_norm_g, m_final_g.reshape(1, D), m_conv_w[0]]
    sv = [v_norm_g, v_mem_norm_g, v_final_g.reshape(1, D), v_conv_w[0]]
    loss_row, sg, sd, snm, snv = _small_update(place, pack, sm_land, sw, sm, sv)
    loss = loss_row[0, 0]
    g_norm, g_memnorm, g_final, g_conv = sg

    def order(norm, memnorm, wkv, w_in_a, w_out_a, w_in_b, conv, w_out_b, final):
        return (norm, memnorm, wkv, w_in_a, w_out_a, w_in_b, conv.reshape(1, 3, SH_O), w_out_b, final.reshape(D))

    grads = order(g_norm, g_memnorm, g_big[0], g_big[1], g_big[2], g_big[3], g_conv, g_big[4], g_final)
    deltas = order(sd[0], sd[1], upd[0][0], upd[1][0], upd[2][0], upd[3][0], sd[3], upd[4][0], sd[2])
    new_m = order(snm[0], snm[1], upd[0][1], upd[1][1], upd[2][1], upd[3][1], snm[3], upd[4][1], snm[2])
    new_v = order(snv[0], snv[1], upd[0][2], upd[1][2], upd[2][2], upd[3][2], snv[3], upd[4][2], snv[2])
    return (loss, gx[None], *grads, *deltas, *new_m, *new_v)
```

```python
import functools

import numpy as np
import jax
import jax.numpy as jnp
from jax import lax
from jax.experimental import pallas as pl
from jax.experimental.pallas import tpu as pltpu

F32 = jnp.float32
BF16 = jnp.bfloat16

S = 2048
D = 1024
TM = 256
NT = S // TM
HD = 64
GW = 512
NQ = 3 * GW
MW = 256
NM = 256
IN_A = 3 * NQ + MW + GW + MW
IN_B = 3 * D + MW + D + MW
BR_A = GW + MW
BR_B = D + MW
SH_A = IN_A // 4
SH_B = IN_B // 4
SH_O = D // 4
QBLK = 128
DILATIONS = (1, 4, 16)
EPS = 1e-6
SCALE = HD ** -0.5
NEG = -1e30
ROPE_THETA = 500000.0

ADAM_LR = 0.001
ADAM_B1 = 0.9
ADAM_B2 = 0.999
ADAM_EPS = 1e-08
ADAM_WD = 0.01
ADAM_STEP = 10

VMEM_LIMIT_BYTES = 60 * 1024 * 1024


def _params(sem=None):
    if sem is None:
        return pltpu.CompilerParams(vmem_limit_bytes=VMEM_LIMIT_BYTES)
    return pltpu.CompilerParams(dimension_semantics=sem, vmem_limit_bytes=VMEM_LIMIT_BYTES)


def _full(shape):
    nd = len(shape)
    return pl.BlockSpec(shape, lambda *_: (0,) * nd)


def _rows(width, tm=TM):
    return pl.BlockSpec((tm, width), lambda i: (i, 0))


def _sds(shape, dtype):
    return jax.ShapeDtypeStruct(shape, dtype)


def _silu_parts(z):
    sig = 0.5 * jnp.tanh(0.5 * z) + 0.5
    return z * sig, sig * (1.0 + z * (1.0 - sig))


def _dot(a, b):
    return jnp.dot(a, b, preferred_element_type=F32)


def _dot_nt(a, b):
    return lax.dot_general(a, b, (((1,), (1,)), ((), ())), preferred_element_type=F32)


def _dot_tn(a, b):
    return lax.dot_general(a, b, (((0,), (0,)), ((), ())), preferred_element_type=F32)


def _rope_fwd(t, c, s1, s2):
    return t * c + pltpu.roll(t, 120, 1) * s1 + pltpu.roll(t, 8, 1) * s2


def _rope_bwd(g, c, s1, s2):
    return g * c + pltpu.roll(g * s1, 8, 1) + pltpu.roll(g * s2, 120, 1)


MEM_HEADS = MW // HD


def _stack_heads(x):
    head = lax.broadcasted_iota(jnp.int32, x.shape, 1) // HD
    return jnp.concatenate([jnp.where(head == h, x, 0.0) for h in range(MEM_HEADS)], axis=0).astype(BF16)


def _unstack_heads(x4):
    tm = x4.shape[0] // MEM_HEADS
    head = lax.broadcasted_iota(jnp.int32, (tm, MW), 1) // HD
    out = x4[:tm]
    for h in range(1, MEM_HEADS):
        out = jnp.where(head == h, x4[h * tm:(h + 1) * tm], out)
    return out


def _mem_attn(qm, kv):
    q4 = _stack_heads(qm.astype(F32))
    s = _dot_nt(q4, kv[:, :MW]) * SCALE
    e = jnp.exp(s - jnp.max(s, axis=-1, keepdims=True))
    p = e * (1.0 / jnp.sum(e, axis=-1, keepdims=True))
    return p, _unstack_heads(_dot(p.astype(BF16), kv[:, MW:])), q4


def _mem_attn_bwd(dmo, p, mo, q4, kv, dkv_ref):
    tm = dmo.shape[0]
    head = lax.broadcasted_iota(jnp.int32, dmo.shape, 1) // HD
    prod = dmo * mo
    delta = jnp.concatenate([jnp.sum(jnp.where(head == h, prod, 0.0), axis=-1, keepdims=True)
                             for h in range(MEM_HEADS)], axis=0)
    d4 = _stack_heads(dmo)
    ds = (p * (_dot_nt(d4, kv[:, MW:]) - delta) * SCALE).astype(BF16)
    dkv_ref[:, :MW] += _dot_tn(ds, q4)
    dkv_ref[:, MW:] += _dot_tn(p.astype(BF16), d4)
    return _unstack_heads(_dot(ds, kv[:, :MW]))


def _merge(o_refs, l_refs):
    ls = [r[...] for r in l_refs]
    m = jnp.maximum(jnp.maximum(ls[0], ls[1]), ls[2])
    es = [jnp.exp(l - m) for l in ls]
    inv = 1.0 / (es[0] + es[1] + es[2])
    ws = [e * inv for e in es]
    os_ = [r[...] for r in o_refs]
    mix = ws[0] * os_[0] + ws[1] * os_[1] + ws[2] * os_[2]
    return ws, mix


def _conv_taps(cg, u, cgp, up, first):
    a = cg * u
    ap = jnp.where(first, 0.0, cgp * up)
    row = lax.broadcasted_iota(jnp.int32, a.shape, 0)
    a1 = jnp.where(row == 0, ap[7:8, :], pltpu.roll(a, 1, 0))
    a2 = jnp.where(row == 0, ap[6:7, :], jnp.where(row == 1, ap[7:8, :], pltpu.roll(a, 2, 0)))
    return a, a1, a2


def _rope_tables(posf, after):
    half = 8
    invf = np.float32(ROPE_THETA) ** (-np.arange(half, dtype=np.float32) * np.float32(2.0 / 16))
    lane = np.arange(128)
    table = np.where((lane % HD) < 16, invf[lane % half], 0.0).astype(np.float32)[None, :]

    def body(pos_ref, invf_ref, c_ref, s1_ref, s2_ref):
        ang = pos_ref[...] * invf_ref[...]
        jm = lax.broadcasted_iota(jnp.int32, ang.shape, 1) & (HD - 1)
        cs = jnp.cos(ang)
        sn = jnp.sin(ang)
        c_ref[...] = jnp.where(jm < 16, cs, 1.0)
        s1_ref[...] = jnp.where(jm < 8, -sn, 0.0)
        s2_ref[...] = jnp.where((jm >= 8) & (jm < 16), sn, 0.0)

    out = _sds((S, 128), F32)
    return pl.pallas_call(
        functools.partial(_skip_arg, body, 2), name="rope_tables", grid=(NT,),
        in_specs=[_rows(1), _full((1, 128)), pl.BlockSpec(memory_space=pl.ANY)],
        out_specs=[_rows(128)] * 3, out_shape=[out] * 3,
        compiler_params=_params(("parallel",)),
    )(posf, jnp.asarray(table), after)


def _in_proj_a(x, g0, w_in, c, s1, s2, after):
    def body(x_ref, g_ref, w_ref, c_ref, s1_ref, s2_ref, hn_ref, q_ref, k_ref, v_ref, qm_ref, z_ref, proj):
        xf = x_ref[...]
        hn = xf * lax.rsqrt(jnp.mean(xf * xf, axis=-1, keepdims=True) + EPS) * g_ref[...]
        hb = hn.astype(BF16)
        hn_ref[...] = hb
        for s in range(4):
            proj[:, s * SH_A:(s + 1) * SH_A] = _dot(hb, w_ref[s])
        cc, a1, a2 = c_ref[...], s1_ref[...], s2_ref[...]
        for j in range(NQ // 128):
            q_ref[:, j * 128:(j + 1) * 128] = (
                _rope_fwd(proj[:, j * 128:(j + 1) * 128], cc, a1, a2) * SCALE).astype(BF16)
            k_ref[:, j * 128:(j + 1) * 128] = _rope_fwd(
                proj[:, NQ + j * 128:NQ + (j + 1) * 128], cc, a1, a2).astype(BF16)
        v_ref[...] = proj[:, 2 * NQ:3 * NQ].astype(BF16)
        qm_ref[...] = proj[:, 3 * NQ:3 * NQ + MW].astype(BF16)
        z_ref[...] = proj[:, 3 * NQ + MW:]

    return pl.pallas_call(
        functools.partial(_skip_arg, body, 6), name="in_proj_a", grid=(NT,),
        in_specs=[_rows(D), _full((1, D)), _full((4, D, SH_A)), _rows(128), _rows(128), _rows(128),
                  pl.BlockSpec(memory_space=pl.ANY)],
        out_specs=[_rows(D), _rows(NQ), _rows(NQ), _rows(NQ), _rows(MW), _rows(BR_A)],
        out_shape=[_sds((S, D), BF16), _sds((S, NQ), BF16), _sds((S, NQ), BF16), _sds((S, NQ), BF16),
                   _sds((S, MW), BF16), _sds((S, BR_A), F32)],
        scratch_shapes=[pltpu.VMEM((TM, IN_A), F32)],
        compiler_params=_params(("parallel",)),
    )(x, g0, w_in, c, s1, s2, after)


def _mem_fwd(mem, mg, wkv):
    def body(mem_ref, mg_ref, w_ref, memn_ref, kv_ref):
        mf = mem_ref[...]
        n = mf * lax.rsqrt(jnp.mean(mf * mf, axis=-1, keepdims=True) + EPS)
        for i in range(2):
            mn = (n * mg_ref[i:i + 1, :]).astype(BF16)
            memn_ref[i] = mn
            acc = _dot(mn[:, 0:NM], w_ref[0, i])
            for s in range(1, 4):
                acc += _dot(mn[:, s * NM:(s + 1) * NM], w_ref[s, i])
            kv_ref[i] = acc.astype(BF16)

    return pl.pallas_call(
        body, name="mem_fwd", grid=(1,),
        in_specs=[_full((NM, D)), _full((2, D)), _full((4, 2, NM, 2 * MW))],
        out_specs=[_full((2, NM, D)), _full((2, NM, 2 * MW))],
        out_shape=[_sds((2, NM, D), BF16), _sds((2, NM, 2 * MW), BF16)],
        compiler_params=_params(("arbitrary",)),
    )(mem, mg, wkv)


def _band_mask(j):
    qi = lax.broadcasted_iota(jnp.int32, (QBLK, 2 * QBLK), 0)
    kj = lax.broadcasted_iota(jnp.int32, (QBLK, 2 * QBLK), 1)
    dist = qi + QBLK - kj
    return (dist >= 0) & (dist <= QBLK) & ((kj >= QBLK) | (j > 0))


LANES = 128
NCHUNK = GW // LANES
FWD_UNROLL = 16
BWD_UNROLL = 4
CONV_CHUNK = 256


def _perm_matrix(d):
    n = TM // d
    p = np.zeros((TM, TM), np.float32)
    for r in range(d):
        for i in range(n):
            p[r * n + i, i * d + r] = 1.0
    return p


def _split_dot(p, x, parts):
    hi = x.astype(BF16)
    rem = x - hi.astype(F32)
    lo = rem.astype(BF16)
    both = _dot(p, jnp.concatenate([hi, lo], axis=1))
    acc = both[:, :LANES] + both[:, LANES:]
    if parts == 3:
        acc = acc + _dot(p, (rem - lo.astype(F32)).astype(BF16))
    return acc


def _pair_dot(p, a, b):
    both = _dot(p, jnp.concatenate([a, b], axis=1))
    return both[:, :LANES], both[:, LANES:]


def _tile_to_streams(y, dst, t, d):
    n, ln = TM // d, S // d
    for r in range(d):
        dst[r * ln + t * n:r * ln + (t + 1) * n, :] = y[r * n:(r + 1) * n].astype(dst.dtype)


def _tile_from_streams(src, t, d):
    n, ln = TM // d, S // d
    return jnp.concatenate([src[r * ln + t * n:r * ln + (t + 1) * n, :] for r in range(d)], axis=0)


def _head_masks():
    first = lax.broadcasted_iota(jnp.int32, (TM, LANES), 1) < HD
    return first, jnp.logical_not(first)


def _attn_fwd(q, k, v, g, after):
    d = DILATIONS[g]
    nb = S // d // QBLK
    perm = _perm_matrix(d)

    def body(q_ref, k_ref, v_ref, p_ref, pt_ref, o_ref, l_ref, ls_ref, q0, q1, ks, vs, os_):
        first, second = _head_masks()
        pm = p_ref[...]
        for t in range(NT):
            rows = slice(t * TM, (t + 1) * TM)
            if d == 1:
                qt = q_ref[rows, :].astype(F32)
            else:
                qt, kt = _pair_dot(pm, q_ref[rows, :], k_ref[rows, :])
                _tile_to_streams(kt, ks, t, d)
                _tile_to_streams(_dot(pm, v_ref[rows, :]), vs, t, d)
            _tile_to_streams(jnp.where(first, qt, 0.0), q0, t, d)
            _tile_to_streams(jnp.where(second, qt, 0.0), q1, t, d)
        kref, vref = (k_ref, v_ref) if d == 1 else (ks, vs)
        oref, lref = (o_ref, l_ref) if d == 1 else (os_, ls_ref)

        def blk(b, carry):
            r0 = pl.multiple_of(b * QBLK, QBLK)
            p0 = pl.multiple_of(jnp.maximum(b - 1, 0) * QBLK, QBLK)
            kk = jnp.concatenate([kref[pl.ds(p0, QBLK), :], kref[pl.ds(r0, QBLK), :]], axis=0)
            vv = jnp.concatenate([vref[pl.ds(p0, QBLK), :], vref[pl.ds(r0, QBLK), :]], axis=0)
            valid = _band_mask(b & (nb - 1))
            acc, lse = [], []
            for qh in (q0, q1):
                s = jnp.where(valid, _dot_nt(qh[pl.ds(r0, QBLK), :], kk), NEG)
                m = jnp.max(s, axis=-1, keepdims=True)
                e = jnp.exp(s - m)
                l = jnp.sum(e, axis=-1, keepdims=True)
                acc.append(_dot(e.astype(BF16), vv) * (1.0 / l))
                lse.append(m + jnp.log(l))
            f = first[:QBLK]
            oref[pl.ds(r0, QBLK), :] = jnp.where(f, acc[0], acc[1])
            lref[pl.ds(r0, QBLK), :] = jnp.where(f, lse[0], lse[1])
            return carry

        lax.fori_loop(0, S // QBLK, blk, 0, unroll=FWD_UNROLL)
        if d > 1:
            ptm = pt_ref[...]
            for t in range(NT):
                rows = slice(t * TM, (t + 1) * TM)
                o_ref[rows, :] = _split_dot(ptm, _tile_from_streams(os_, t, d), 2)
                l_ref[rows, :] = _split_dot(ptm, _tile_from_streams(ls_ref, t, d), 3)

    qkv_spec = pl.BlockSpec((S, LANES), lambda c: (0, g * NCHUNK + c))
    out_spec = pl.BlockSpec((S, LANES), lambda c: (0, c))
    n_out = 2 if d == 1 else 3
    inner = body if d > 1 else functools.partial(_drop_arg, body, 7)
    outs = pl.pallas_call(
        functools.partial(_skip_arg, inner, 5), name=f"attn_fwd_g{g}", grid=(NCHUNK,),
        in_specs=[qkv_spec] * 3 + [_full((TM, TM))] * 2 + [pl.BlockSpec(memory_space=pl.ANY)],
        out_specs=[out_spec] * n_out, out_shape=[_sds((S, GW), F32)] * n_out,
        scratch_shapes=[pltpu.VMEM((S, LANES), BF16)] * 4 + [pltpu.VMEM((S, LANES), F32)],
        compiler_params=_params(("parallel",)),
    )(q, k, v, jnp.asarray(perm, BF16), jnp.asarray(perm.T, BF16), after)
    return (outs[0], outs[1], outs[1]) if d == 1 else tuple(outs)


def _drop_arg(body, pos, *refs):
    return body(*refs[:pos], None, *refs[pos:])


def _attn_out(os_, ls, qm, kv0, z, x, w_out):
    def body(o0, o1, o2, l0, l1, l2, qm_ref, kv_ref, z_ref, x_ref, w_ref, h_ref, ybuf):
        _, mix = _merge((o0, o1, o2), (l0, l1, l2))
        sz, _ = _silu_parts(z_ref[...])
        ybuf[:, :GW] = (mix * sz[:, :GW]).astype(BF16)
        _, mo, _ = _mem_attn(qm_ref[...], kv_ref[...])
        ybuf[:, GW:] = (mo * sz[:, GW:]).astype(BF16)
        yb = ybuf[...]
        for s in range(4):
            cs = slice(s * SH_O, (s + 1) * SH_O)
            h_ref[:, cs] = x_ref[:, cs] + _dot(yb, w_ref[s])

    return pl.pallas_call(
        body, name="attn_out", grid=(NT,),
        in_specs=[_rows(GW)] * 6 + [_rows(MW), _full((NM, 2 * MW)), _rows(BR_A), _rows(D), _full((4, BR_A, SH_O))],
        out_specs=_rows(D), out_shape=_sds((S, D), F32),
        scratch_shapes=[pltpu.VMEM((TM, BR_A), BF16)],
        compiler_params=_params(("parallel",)),
    )(*os_, *ls, qm, kv0, z, x, w_out)


def _in_proj_b(h1, g1, w_in):
    def body(x_ref, g_ref, w_ref, hn_ref, bg_ref, cg_ref, u_ref, qm_ref, z_ref, proj):
        xf = x_ref[...]
        hn = xf * lax.rsqrt(jnp.mean(xf * xf, axis=-1, keepdims=True) + EPS) * g_ref[...]
        hb = hn.astype(BF16)
        hn_ref[...] = hb
        for s in range(4):
            proj[:, s * SH_B:(s + 1) * SH_B] = _dot(hb, w_ref[s])
        bg_ref[...] = proj[:, :D]
        cg_ref[...] = proj[:, D:2 * D]
        u_ref[...] = proj[:, 2 * D:3 * D]
        qm_ref[...] = proj[:, 3 * D:3 * D + MW].astype(BF16)
        z_ref[...] = proj[:, 3 * D + MW:]

    return pl.pallas_call(
        body, name="in_proj_b", grid=(NT,),
        in_specs=[_rows(D), _full((1, D)), _full((4, D, SH_B))],
        out_specs=[_rows(D), _rows(D), _rows(D), _rows(D), _rows(MW), _rows(BR_B)],
        out_shape=[_sds((S, D), BF16), _sds((S, D), F32), _sds((S, D), F32), _sds((S, D), F32),
                   _sds((S, MW), BF16), _sds((S, BR_B), F32)],
        scratch_shapes=[pltpu.VMEM((TM, IN_B), F32)],
        compiler_params=_params(("parallel",)),
    )(h1, g1, w_in)


def _prev8(width):
    return pl.BlockSpec((8, width), lambda i: (jnp.maximum(i * (TM // 8) - 1, 0), 0))


def _conv_out_loss(bg, cg, u, cw, qm, kv1, z, h1, w_out, fg, tgt):
    def body(bg_ref, cg_ref, u_ref, cgp_ref, up_ref, cw_ref, qm_ref, kv_ref, z_ref, h_ref, w_ref, fg_ref, t_ref,
             dh_ref, loss_ref, dfg_ref, ybuf):
        i = pl.program_id(0)
        a, a1, a2 = _conv_taps(cg_ref[...], u_ref[...], cgp_ref[...], up_ref[...], i == 0)
        conv = cw_ref[0:1, :] * a2 + cw_ref[1:2, :] * a1 + cw_ref[2:3, :] * a
        sz, _ = _silu_parts(z_ref[...])
        ybuf[:, :D] = (bg_ref[...] * conv * sz[:, :D]).astype(BF16)
        _, mo, _ = _mem_attn(qm_ref[...], kv_ref[...])
        ybuf[:, D:] = (mo * sz[:, D:]).astype(BF16)
        h2 = h_ref[...] + _dot(ybuf[...], w_ref[...])
        rstd = lax.rsqrt(jnp.mean(h2 * h2, axis=-1, keepdims=True) + EPS)
        n = h2 * rstd
        fgv = fg_ref[...]
        err = n * fgv - t_ref[...]
        dout = err * (1.0 / D)
        dn = dout * fgv
        dh_ref[...] = rstd * (dn - n * jnp.mean(dn * n, axis=-1, keepdims=True))

        @pl.when(i == 0)
        def _():
            loss_ref[...] = jnp.zeros_like(loss_ref)
            dfg_ref[...] = jnp.zeros_like(dfg_ref)

        loss_ref[...] += jnp.sum(err * err) * (0.5 / D)
        dfg_ref[...] += jnp.sum(dout * n, axis=0, keepdims=True)

    return pl.pallas_call(
        body, name="conv_out_loss", grid=(NT,),
        in_specs=[_rows(D), _rows(D), _rows(D), _prev8(D), _prev8(D), _full((8, D)), _rows(MW),
                  _full((NM, 2 * MW)), _rows(BR_B), _rows(D), _full((BR_B, D)), _full((1, D)), _rows(D)],
        out_specs=[_rows(D), _full((1, 128)), _full((1, D))],
        out_shape=[_sds((S, D), F32), _sds((1, 128), F32), _sds((1, D), F32)],
        scratch_shapes=[pltpu.VMEM((TM, BR_B), BF16)],
        compiler_params=_params(("arbitrary",)),
    )(bg, cg, u, cg, u, cw, qm, kv1, z, h1, w_out, fg, tgt)


def _conv_bwd(dh2, bg, cg, u, cw, qm, kv1, z, w_out):
    rev = lambda i: (NT - 1 - i, 0)
    rows = lambda w: pl.BlockSpec((TM, w), rev)
    prev8 = pl.BlockSpec((8, D), lambda i: (jnp.maximum((NT - 1 - i) * (TM // 8) - 1, 0), 0))

    def body(dh_ref, bg_ref, cg_ref, u_ref, cgp_ref, up_ref, cw_ref, qm_ref, kv_ref, z_ref, w_ref,
             dproj_ref, dw_ref, dcw_ref, dkv_ref, dwb_ref, ybuf, carry):
        i = pl.program_id(0)

        @pl.when(i == 0)
        def _():
            dw_ref[...] = jnp.zeros_like(dw_ref)
            dcw_ref[...] = jnp.zeros_like(dcw_ref)
            dkv_ref[...] = jnp.zeros_like(dkv_ref)
            carry[...] = jnp.zeros_like(carry)

        dhb = dh_ref[...].astype(BF16)
        dy = _dot_nt(dhb, w_ref[...])
        kvv = kv_ref[...]
        p, mo, q4 = _mem_attn(qm_ref[...], kvv)
        szm, dszm = _silu_parts(z_ref[:, D:])
        ybuf[:, D:] = (mo * szm).astype(BF16)
        dym = dy[:, D:]
        dproj_ref[:, 3 * D + MW + D:] = (dym * mo * dszm).astype(BF16)
        first_tile = i == NT - 1
        for c in range(D // CONV_CHUNK):
            cs = slice(c * CONV_CHUNK, (c + 1) * CONV_CHUNK)
            bgv, cgv, uv = bg_ref[:, cs], cg_ref[:, cs], u_ref[:, cs]
            a, a1, a2 = _conv_taps(cgv, uv, cgp_ref[:, cs], up_ref[:, cs], first_tile)
            w0, w1, w2 = cw_ref[0:1, cs], cw_ref[1:2, cs], cw_ref[2:3, cs]
            conv = w0 * a2 + w1 * a1 + w2 * a
            mix = bgv * conv
            sz, dsz = _silu_parts(z_ref[:, cs])
            ybuf[:, cs] = (mix * sz).astype(BF16)
            dyc = dy[:, cs]
            dproj_ref[:, 3 * D + MW + c * CONV_CHUNK:3 * D + MW + (c + 1) * CONV_CHUNK] = (
                dyc * mix * dsz).astype(BF16)
            dmix = dyc * sz
            dproj_ref[:, cs] = (dmix * conv).astype(BF16)
            dc = dmix * bgv
            nxt = carry[:, cs]
            row = lax.broadcasted_iota(jnp.int32, dc.shape, 0)
            dc1 = jnp.where(row == TM - 1, nxt[0:1, :], pltpu.roll(dc, TM - 1, 0))
            dc2 = jnp.where(row == TM - 2, nxt[0:1, :],
                            jnp.where(row == TM - 1, nxt[1:2, :], pltpu.roll(dc, TM - 2, 0)))
            carry[:, cs] = dc[0:8, :]
            da = w2 * dc + w1 * dc1 + w0 * dc2
            dproj_ref[:, D + c * CONV_CHUNK:D + (c + 1) * CONV_CHUNK] = (da * uv).astype(BF16)
            dproj_ref[:, 2 * D + c * CONV_CHUNK:2 * D + (c + 1) * CONV_CHUNK] = (da * cgv).astype(BF16)
            dcw_ref[0:1, cs] += jnp.sum(dc * a2, axis=0, keepdims=True)
            dcw_ref[1:2, cs] += jnp.sum(dc * a1, axis=0, keepdims=True)
            dcw_ref[2:3, cs] += jnp.sum(dc * a, axis=0, keepdims=True)
        dw_ref[...] += _dot_tn(ybuf[...], dhb)
        dproj_ref[:, 3 * D:3 * D + MW] = _mem_attn_bwd(dym * szm, p, mo, q4, kvv, dkv_ref).astype(BF16)

        @pl.when(i == NT - 1)
        def _():
            dwb_ref[...] = dw_ref[...].astype(BF16)

    return pl.pallas_call(
        body, name="conv_bwd", grid=(NT,),
        in_specs=[rows(D), rows(D), rows(D), rows(D), prev8, prev8, _full((8, D)), rows(MW),
                  _full((NM, 2 * MW)), rows(BR_B), _full((BR_B, D))],
        out_specs=[rows(IN_B), _full((BR_B, D)), _full((8, D)), _full((NM, 2 * MW)), _full((BR_B, D))],
        out_shape=[_sds((S, IN_B), BF16), _sds((BR_B, D), F32), _sds((8, D), F32), _sds((NM, 2 * MW), F32),
                   _sds((BR_B, D), BF16)],
        scratch_shapes=[pltpu.VMEM((TM, BR_B), BF16), pltpu.VMEM((8, D), F32)],
        compiler_params=_params(("arbitrary",)),
    )(dh2, bg, cg, u, cg, u, cw, qm, kv1, z, w_out)


def _in_proj_bwd(dproj, w_in, xin, g, dres, after, width, name):
    sh = width // 4

    def body(dp_ref, w_ref, x_ref, g_ref, dr_ref, dx_ref, dg_ref):
        i = pl.program_id(0)
        dhn = _dot_nt(dp_ref[:, 0:sh], w_ref[0])
        for s in range(1, 4):
            dhn += _dot_nt(dp_ref[:, s * sh:(s + 1) * sh], w_ref[s])
        xf = x_ref[...]
        rstd = lax.rsqrt(jnp.mean(xf * xf, axis=-1, keepdims=True) + EPS)
        n = xf * rstd
        dn = dhn * g_ref[...]
        dx_ref[...] = dr_ref[...] + rstd * (dn - n * jnp.mean(dn * n, axis=-1, keepdims=True))

        @pl.when(i == 0)
        def _():
            dg_ref[...] = jnp.zeros_like(dg_ref)

        dg_ref[...] += jnp.sum(dhn * n, axis=0, keepdims=True)

    return pl.pallas_call(
        functools.partial(_skip_arg, body, 5), name=name, grid=(NT,),
        in_specs=[_rows(width), _full((4, D, sh)), _rows(D), _full((1, D)), _rows(D), pl.BlockSpec(memory_space=pl.ANY)],
        out_specs=[_rows(D), _full((1, D))],
        out_shape=[_sds((S, D), F32), _sds((1, D), F32)],
        compiler_params=_params(("arbitrary",)),
    )(dproj, w_in, xin, g, dres, after)


def _w_in_grad(hn, dproj, width, name):
    sh = width // 4

    def body(hn_ref, dp_ref, dw_ref, dwb_ref):
        dw = _dot_tn(hn_ref[...], dp_ref[...])
        dw_ref[0] = dw
        dwb_ref[0] = dw.astype(BF16)

    spec = pl.BlockSpec((1, D, sh), lambda s: (s, 0, 0))
    return pl.pallas_call(
        body, name=name, grid=(4,),
        in_specs=[_full((S, D)), pl.BlockSpec((S, sh), lambda s: (0, s))],
        out_specs=[spec, spec], out_shape=[_sds((4, D, sh), F32), _sds((4, D, sh), BF16)],
        compiler_params=_params(("parallel",)),
    )(hn, dproj)


def _attn_out_bwd(dh1, os_, ls, qm, kv0, z, w_out, after):
    ones_bd = np.kron(np.eye(GW // HD, dtype=np.float32), np.ones((HD, HD), np.float32))

    def body(dh_ref, o0, o1, o2, l0, l1, l2, qm_ref, kv_ref, z_ref, w_ref, bd_ref,
             do0, do1, do2, dd0, dd1, dd2, dqm_ref, dz_ref, dw_ref, dkv_ref, dwb_ref, ybuf):
        i = pl.program_id(0)

        @pl.when(i == 0)
        def _():
            dw_ref[...] = jnp.zeros_like(dw_ref)
            dkv_ref[...] = jnp.zeros_like(dkv_ref)

        ws, mix = _merge((o0, o1, o2), (l0, l1, l2))
        sz, dsz = _silu_parts(z_ref[...])
        kvv = kv_ref[...]
        p, mo, q4 = _mem_attn(qm_ref[...], kvv)
        ybuf[:, :GW] = (mix * sz[:, :GW]).astype(BF16)
        ybuf[:, GW:] = (mo * sz[:, GW:]).astype(BF16)
        yb = ybuf[...]
        dh = dh_ref[...]
        dy = None
        for s in range(4):
            dhb = dh[:, s * SH_O:(s + 1) * SH_O].astype(BF16)
            dw_ref[s] += _dot_tn(yb, dhb)
            part = _dot_nt(dhb, w_ref[s])
            dy = part if dy is None else dy + part
        dcat = dy * sz
        dz_ref[:, :GW] = (dy[:, :GW] * mix * dsz[:, :GW]).astype(BF16)
        dz_ref[:, GW:] = (dy[:, GW:] * mo * dsz[:, GW:]).astype(BF16)
        dmix = dcat[:, :GW]
        prod = dmix * mix
        hi = prod.astype(BF16)
        lo = (prod - hi.astype(F32)).astype(BF16)
        bd = bd_ref[...]
        tot = _dot(hi, bd) + _dot(lo, bd)
        for w, do_ref, dd_ref in zip(ws, (do0, do1, do2), (dd0, dd1, dd2)):
            do_ref[...] = (w * dmix).astype(BF16)
            dd_ref[...] = w * tot

        dqm_ref[...] = _mem_attn_bwd(dcat[:, GW:], p, mo, q4, kvv, dkv_ref).astype(BF16)

        @pl.when(i == NT - 1)
        def _():
            dwb_ref[...] = dw_ref[...].astype(BF16)

    return pl.pallas_call(
        functools.partial(_skip_arg, body, 12), name="attn_out_bwd", grid=(NT,),
        in_specs=[_rows(D)] + [_rows(GW)] * 6 + [_rows(MW), _full((NM, 2 * MW)), _rows(BR_A),
                                                   _full((4, BR_A, SH_O)), _full((GW, GW)),
                                                   pl.BlockSpec(memory_space=pl.ANY)],
        out_specs=[_rows(GW)] * 6 + [_rows(MW), _rows(BR_A), _full((4, BR_A, SH_O)), _full((NM, 2 * MW)),
                                     _full((4, BR_A, SH_O))],
        out_shape=[_sds((S, GW), BF16)] * 3 + [_sds((S, GW), F32)] * 3 + [
            _sds((S, MW), BF16), _sds((S, BR_A), BF16), _sds((4, BR_A, SH_O), F32), _sds((NM, 2 * MW), F32),
            _sds((4, BR_A, SH_O), BF16)],
        scratch_shapes=[pltpu.VMEM((TM, BR_A), BF16)],
        compiler_params=_params(("arbitrary",)),
    )(dh1, *os_, *ls, qm, kv0, z, w_out, jnp.asarray(ones_bd, dtype=BF16), after)


def _attn_bwd(q, k, v, do, lse_s, dd, g):
    d = DILATIONS[g]
    nb = S // d // QBLK
    perm = _perm_matrix(d)

    def body(q_ref, k_ref, v_ref, do_ref, l_ref, dd_ref, p_ref, pt_ref, dq_ref, dk_ref, dv_ref,
             q0, q1, g0, g1, ks, vs, dds, dqs, dks, dvs):
        first, second = _head_masks()
        pm = p_ref[...]
        for t in range(NT):
            rows = slice(t * TM, (t + 1) * TM)
            if d == 1:
                qt = q_ref[rows, :].astype(F32)
                gt = do_ref[rows, :].astype(F32)
            else:
                qt, gt = _pair_dot(pm, q_ref[rows, :], do_ref[rows, :])
                kt, vt = _pair_dot(pm, k_ref[rows, :], v_ref[rows, :])
                _tile_to_streams(kt, ks, t, d)
                _tile_to_streams(vt, vs, t, d)
                _tile_to_streams(_split_dot(pm, dd_ref[rows, :], 2), dds, t, d)
            _tile_to_streams(jnp.where(first, qt, 0.0), q0, t, d)
            _tile_to_streams(jnp.where(second, qt, 0.0), q1, t, d)
            _tile_to_streams(jnp.where(first, gt, 0.0), g0, t, d)
            _tile_to_streams(jnp.where(second, gt, 0.0), g1, t, d)
        kref, vref, ddref = (k_ref, v_ref, dd_ref) if d == 1 else (ks, vs, dds)
        dqref, dkref, dvref = dqs, dks, dvs
        dkref[...] = jnp.zeros_like(dkref)
        dvref[...] = jnp.zeros_like(dvref)

        def blk(b, carry):
            r0 = pl.multiple_of(b * QBLK, QBLK)
            p0 = pl.multiple_of(jnp.maximum(b - 1, 0) * QBLK, QBLK)
            kk = jnp.concatenate([kref[pl.ds(p0, QBLK), :], kref[pl.ds(r0, QBLK), :]], axis=0)
            vv = jnp.concatenate([vref[pl.ds(p0, QBLK), :], vref[pl.ds(r0, QBLK), :]], axis=0)
            lb = l_ref[pl.ds(r0, QBLK), :]
            ddb = ddref[pl.ds(r0, QBLK), :]
            lcol = jnp.concatenate([lb[:, 0:1], lb[:, HD:HD + 1]], axis=0)
            dcol = jnp.concatenate([ddb[:, 0:1], ddb[:, HD:HD + 1]], axis=0)
            valid = _band_mask(b & (nb - 1))
            valid2 = jnp.concatenate([valid, valid], axis=0)
            qq = jnp.concatenate([q0[pl.ds(r0, QBLK), :], q1[pl.ds(r0, QBLK), :]], axis=0)
            gg = jnp.concatenate([g0[pl.ds(r0, QBLK), :], g1[pl.ds(r0, QBLK), :]], axis=0)
            p = jnp.where(valid2, jnp.exp(_dot_nt(qq, kk) - lcol), 0.0)
            ds = (p * (_dot_nt(gg, vv) - dcol)).astype(BF16)
            dq2 = _dot(ds, kk)
            dqref[pl.ds(r0, QBLK), :] = jnp.where(first[:QBLK], dq2[:QBLK], dq2[QBLK:])
            dkk = _dot_tn(ds, qq)
            dvv = _dot_tn(p.astype(BF16), gg)
            dkref[pl.ds(p0, QBLK), :] += dkk[:QBLK]
            dkref[pl.ds(r0, QBLK), :] += dkk[QBLK:]
            dvref[pl.ds(p0, QBLK), :] += dvv[:QBLK]
            dvref[pl.ds(r0, QBLK), :] += dvv[QBLK:]
            return carry

        lax.fori_loop(0, S // QBLK, blk, 0, unroll=BWD_UNROLL)

        ptm = pt_ref[...] if d > 1 else None
        for t in range(NT):
            rows = slice(t * TM, (t + 1) * TM)
            if d == 1:
                dq_ref[rows, :] = dqs[rows, :].astype(BF16)
                dk_ref[rows, :] = dks[rows, :].astype(BF16)
                dv_ref[rows, :] = dvs[rows, :].astype(BF16)
            else:
                tq, tk = _pair_dot(ptm, _tile_from_streams(dqs, t, d).astype(BF16),
                                   _tile_from_streams(dks, t, d).astype(BF16))
                dq_ref[rows, :] = tq.astype(BF16)
                dk_ref[rows, :] = tk.astype(BF16)
                dv_ref[rows, :] = _dot(ptm, _tile_from_streams(dvs, t, d).astype(BF16)).astype(BF16)

    qkv_spec = pl.BlockSpec((S, LANES), lambda c: (0, g * NCHUNK + c))
    one_spec = pl.BlockSpec((S, LANES), lambda c: (0, c))
    return pl.pallas_call(
        body, name=f"attn_bwd_g{g}", grid=(NCHUNK,),
        in_specs=[qkv_spec] * 3 + [one_spec] * 3 + [_full((TM, TM))] * 2, out_specs=[one_spec] * 3,
        out_shape=[_sds((S, GW), BF16)] * 3,
        scratch_shapes=[pltpu.VMEM((S, LANES), BF16)] * 6 + [pltpu.VMEM((S, LANES), F32)] * 4,
        compiler_params=_params(("parallel",)),
    )(q, k, v, do, lse_s, dd, jnp.asarray(perm, BF16), jnp.asarray(perm.T, BF16))


def _qkv_bwd(dqs, dks, dvs, dqm, dz, c, s1, s2):
    def body(q0, q1, q2, k0, k1, k2, v0, v1, v2, dqm_ref, dz_ref, c_ref, s1_ref, s2_ref, dp_ref):
        cc, a1, a2 = c_ref[...], s1_ref[...], s2_ref[...]
        for g, (qr, kr, vr) in enumerate(((q0, k0, v0), (q1, k1, v1), (q2, k2, v2))):
            for j in range(GW // 128):
                ls_ = slice(j * 128, (j + 1) * 128)
                c0 = g * GW + j * 128
                dp_ref[:, c0:c0 + 128] = (_rope_bwd(qr[:, ls_].astype(F32), cc, a1, a2) * SCALE).astype(BF16)
                dp_ref[:, NQ + c0:NQ + c0 + 128] = _rope_bwd(kr[:, ls_].astype(F32), cc, a1, a2).astype(BF16)
            dp_ref[:, 2 * NQ + g * GW:2 * NQ + (g + 1) * GW] = vr[...]
        dp_ref[:, 3 * NQ:3 * NQ + MW] = dqm_ref[...]
        dp_ref[:, 3 * NQ + MW:] = dz_ref[...]

    return pl.pallas_call(
        body, name="qkv_bwd", grid=(NT,),
        in_specs=[_rows(GW)] * 9 + [_rows(MW), _rows(BR_A), _rows(128), _rows(128), _rows(128)],
        out_specs=_rows(IN_A), out_shape=_sds((S, IN_A), BF16),
        compiler_params=_params(("parallel",)),
    )(*dqs, *dks, *dvs, dqm, dz, c, s1, s2)


def _mem_bwd(mem, mg, memn, wkv, dkv0, dkv1):
    def body(mem_ref, mg_ref, memn_ref, w_ref, d0_ref, d1_ref, dw_ref, dwb_ref, dg_ref):
        mf = mem_ref[...]
        n = mf * lax.rsqrt(jnp.mean(mf * mf, axis=-1, keepdims=True) + EPS)
        for i, d_ref in enumerate((d0_ref, d1_ref)):
            dkv = d_ref[...].astype(BF16)
            mn = memn_ref[i]
            for s in range(4):
                cs = slice(s * NM, (s + 1) * NM)
                dw = _dot_tn(mn[:, cs], dkv)
                dw_ref[s, i] = dw
                dwb_ref[s, i] = dw.astype(BF16)
                dmn = _dot_nt(dkv, w_ref[s, i])
                dg_ref[i:i + 1, cs] = jnp.sum(dmn * n[:, cs], axis=0, keepdims=True)

    return pl.pallas_call(
        body, name="mem_bwd", grid=(1,),
        in_specs=[_full((NM, D)), _full((2, D)), _full((2, NM, D)), _full((4, 2, NM, 2 * MW)),
                  _full((NM, 2 * MW)), _full((NM, 2 * MW))],
        out_specs=[_full((4, 2, NM, 2 * MW)), _full((4, 2, NM, 2 * MW)), _full((2, D))],
        out_shape=[_sds((4, 2, NM, 2 * MW), F32), _sds((4, 2, NM, 2 * MW), BF16), _sds((2, D), F32)],
        compiler_params=_params(("arbitrary",)),
    )(mem, mg, memn, wkv, dkv0, dkv1)


MESH = pl.DeviceIdType.MESH
ANY = pl.BlockSpec(memory_space=pl.ANY)
BIG = (("wkv", 2, NM, 2 * MW), ("w_in_a", 1, D, SH_A), ("w_out_a", 1, BR_A, SH_O),
       ("w_in_b", 1, D, SH_B), ("w_out_b", 1, BR_B // 4, D))
NBIG = len(BIG)
CW_ROWS = 8


def _place():
    x, y, c = lax.axis_index("x"), lax.axis_index("y"), lax.axis_index("c")
    chips = ((1 - x, y), (x, 1 - y), (1 - x, 1 - y))
    return x, y, c, chips


def _remote(src, dst, ssem, rsem, dev):
    return pltpu.make_async_remote_copy(src_ref=src, dst_ref=dst, send_sem=ssem, recv_sem=rsem,
                                        device_id=dev, device_id_type=MESH)


def _cast_weights(place, ws, after, idx, name):
    nblk = 4
    n = len(idx)
    dims = [BIG[w][1:] for w in idx]

    def body(pref, *refs):
        for i in range(n):
            refs[n + 1 + i][0] = refs[i][...].astype(BF16)

    grid_spec = pltpu.PrefetchScalarGridSpec(
        num_scalar_prefetch=1, grid=(nblk,),
        in_specs=[pl.BlockSpec((k, r // nblk, cdim), lambda i, pref: (0, i, 0)) for k, r, cdim in dims]
        + [pl.BlockSpec(memory_space=pl.ANY)],
        out_specs=[pl.BlockSpec((1, k, r // nblk, cdim), lambda i, pref: (pref[1], 0, i, 0)) for k, r, cdim in dims])
    return pl.pallas_call(
        body, name=name, grid_spec=grid_spec,
        out_shape=[_sds((4, k, r, cdim), BF16) for k, r, cdim in dims],
        compiler_params=_params(("parallel",)),
    )(place, *ws, after)


LAYER_A = (0, 1, 2)
LAYER_B = (3, 4)
HBM = pl.BlockSpec(memory_space=pltpu.HBM)
SEM = pl.BlockSpec(memory_space=pltpu.SEMAPHORE)
EFFECT = pltpu.SideEffectType.DATAFLOW_SIDE_EFFECTING
TOKEN = (8, 128)


def _half(ref, w, which):
    h = BIG[w][2] // 2
    return ref.at[:, pl.ds(which * h, h), :]


def _skip_arg(body, pos, *refs):
    return body(*refs[:pos], *refs[pos + 1:])


def _gather_start(wb, after, idx, name):
    n = len(idx)

    def body(*refs):
        src = refs[:n]
        send_sems, recv_sems = refs[n + 1], refs[n + 2]
        token = refs[2 * n + 3]
        x, y, c, chips = _place()
        me = 2 * x + y
        for j, (px, py) in enumerate(chips):
            for i in range(n):
                mine = _half(src[i].at[me], idx[i], c)
                _remote(mine, mine, send_sems.at[j * n + i], recv_sems.at[j * n + i], (px, py, c)).start()
        token[...] = jnp.zeros(TOKEN, F32)

    outs = pl.pallas_call(
        body, name=name, in_specs=[HBM] * n + [ANY],
        out_specs=(SEM, SEM) + (HBM,) * n + (pl.BlockSpec(memory_space=pltpu.VMEM),),
        out_shape=(pltpu.SemaphoreType.DMA((3 * n,)), pltpu.SemaphoreType.DMA((3 * n,)))
        + tuple(pltpu.HBM(w.shape, w.dtype) for w in wb) + (_sds(TOKEN, F32),),
        input_output_aliases={i: 2 + i for i in range(n)},
        compiler_params=pltpu.CompilerParams(has_side_effects=EFFECT),
    )(*[pltpu.with_memory_space_constraint(w, pltpu.HBM) for w in wb], after)
    return outs[0], outs[1], list(outs[2:2 + n]), outs[2 + n]


def _gather_wait(send_sems, recv_sems, wb, after, idx, name, started=None):
    n = len(idx)
    started = idx if started is None else started
    n_all = len(started)
    pos = [started.index(w) for w in idx]

    def body(*refs):
        buf = refs[:n]
        send_sems, recv_sems = refs[n], refs[n + 1]
        x, y, c, chips = _place()
        me = 2 * x + y
        for j, (px, py) in enumerate(chips):
            for i in range(n):
                mine = _half(buf[i].at[me], idx[i], c)
                got = _half(buf[i].at[2 * px + py], idx[i], c)
                k = j * n_all + pos[i]
                _remote(mine, mine, send_sems.at[k], recv_sems.at[k], (px, py, c)).wait_send()
                _remote(got, got, send_sems.at[k], recv_sems.at[k], (px, py, c)).wait_recv()

    outs = pl.pallas_call(
        body, name=name, in_specs=[HBM] * n + [SEM, SEM] + [ANY] * len(after), out_specs=(HBM,) * n,
        out_shape=tuple(pltpu.HBM(w.shape, w.dtype) for w in wb),
        input_output_aliases={i: i for i in range(n)},
        compiler_params=pltpu.CompilerParams(has_side_effects=EFFECT),
    )(*wb, send_sems, recv_sems, *after)
    return list(outs)


def _gather_forward(wb, idx, name):
    n = len(idx)

    def body(*refs):
        dst = refs[n:2 * n]
        send_sems, recv_sems = refs[2 * n], refs[2 * n + 1]
        x, y, c, chips = _place()
        cps = []
        for j, (px, py) in enumerate(chips):
            for i in range(n):
                got = _half(dst[i].at[2 * px + py], idx[i], c)
                cps.append(_remote(got, got, send_sems.at[j, i], recv_sems.at[j, i], (x, y, 1 - c)))
                cps[-1].start()
        for j, (px, py) in enumerate(chips):
            for i in range(n):
                got = _half(dst[i].at[2 * px + py], idx[i], 1 - c)
                _remote(got, got, send_sems.at[j, i], recv_sems.at[j, i], (x, y, 1 - c)).wait_recv()
        for cp in cps:
            cp.wait_send()

    return pl.pallas_call(
        body, name=name, in_specs=[ANY] * n, out_specs=[ANY] * n, out_shape=[_sds(w.shape, BF16) for w in wb],
        input_output_aliases={i: i for i in range(n)},
        scratch_shapes=[pltpu.SemaphoreType.DMA((3, n)), pltpu.SemaphoreType.DMA((3, n))],
    )(*wb)


def _forward_start(wb, cw, after, idx, name):
    n = len(idx)
    m = n if cw is None else n + 2

    def body(*refs):
        buf = refs[:n]
        send_sems, recv_sems = refs[m + 1], refs[m + 2]
        token = refs[2 * m + 3]
        x, y, c, chips = _place()
        for j, (px, py) in enumerate(chips):
            for i in range(n):
                got = _half(buf[i].at[2 * px + py], idx[i], c)
                _remote(got, got, send_sems.at[j * (n + 1) + i], recv_sems.at[j * (n + 1) + i], (x, y, 1 - c)).start()
            if cw is not None:
                _remote(refs[n], refs[n + 1].at[2 * x + y], send_sems.at[j * (n + 1) + n],
                        recv_sems.at[j * (n + 1) + n], (px, py, c)).start()
        token[...] = jnp.zeros(TOKEN, F32)

    arrays = list(wb) if cw is None else list(wb) + [cw, lax.empty((4, CW_ROWS, SH_O), F32)]
    outs = pl.pallas_call(
        body, name=name, in_specs=[HBM] * m + [ANY],
        out_specs=(SEM, SEM) + (HBM,) * m + (pl.BlockSpec(memory_space=pltpu.VMEM),),
        out_shape=(pltpu.SemaphoreType.DMA((3 * (n + 1),)), pltpu.SemaphoreType.DMA((3 * (n + 1),)))
        + tuple(pltpu.HBM(a.shape, a.dtype) for a in arrays) + (_sds(TOKEN, F32),),
        input_output_aliases={i: 2 + i for i in range(m)},
        compiler_params=pltpu.CompilerParams(has_side_effects=EFFECT),
    )(*[pltpu.with_memory_space_constraint(a, pltpu.HBM) for a in arrays], after)
    return outs[0], outs[1], list(outs[2:2 + m]), outs[2 + m]


def _forward_wait(send_sems, recv_sems, arrays, after, idx, with_cw, name):
    n = len(idx)
    m = len(arrays)

    def body(*refs):
        buf = refs[:n]
        send_sems, recv_sems = refs[m], refs[m + 1]
        x, y, c, chips = _place()
        for j, (px, py) in enumerate(chips):
            for i in range(n):
                sent = _half(buf[i].at[2 * px + py], idx[i], c)
                got = _half(buf[i].at[2 * px + py], idx[i], 1 - c)
                k = j * (n + 1) + i
                _remote(sent, sent, send_sems.at[k], recv_sems.at[k], (x, y, 1 - c)).wait_send()
                _remote(got, got, send_sems.at[k], recv_sems.at[k], (x, y, 1 - c)).wait_recv()
            if with_cw:
                k = j * (n + 1) + n
                theirs = refs[n + 1].at[2 * px + py]
                _remote(refs[n], theirs, send_sems.at[k], recv_sems.at[k], (px, py, c)).wait_send()
                _remote(refs[n], theirs, send_sems.at[k], recv_sems.at[k], (px, py, c)).wait_recv()

    outs = pl.pallas_call(
        body, name=name, in_specs=[HBM] * m + [SEM, SEM] + [ANY] * len(after), out_specs=(HBM,) * m,
        out_shape=tuple(pltpu.HBM(a.shape, a.dtype) for a in arrays),
        input_output_aliases={i: i for i in range(m)},
        compiler_params=pltpu.CompilerParams(has_side_effects=EFFECT),
    )(*arrays, send_sems, recv_sems, *after)
    return list(outs)


def _pair_exchange(gs, idx, name):
    n = len(idx)

    def body(*refs):
        src, dst = refs[:n], refs[n:2 * n]
        send_sems, recv_sems = refs[2 * n:]
        x, y, c, _ = _place()
        cps = []
        for i in range(n):
            h = BIG[idx[i]][2] // 2
            cps.append(_remote(src[i].at[:, :, pl.ds((1 - c) * h, h), :], dst[i], send_sems.at[i], recv_sems.at[i],
                               (x, y, 1 - c)))
            cps[-1].start()
        for cp in cps:
            cp.wait()

    return pl.pallas_call(
        body, name=name, in_specs=[ANY] * n, out_specs=[ANY] * n,
        out_shape=[_sds((4, BIG[w][1], BIG[w][2] // 2, BIG[w][3]), BF16) for w in idx],
        scratch_shapes=[pltpu.SemaphoreType.DMA((n,)), pltpu.SemaphoreType.DMA((n,))],
    )(*gs)


def _pair_start(gs, idx, name):
    n = len(idx)

    def body(*refs):
        src, land = refs[:n], refs[n:2 * n]
        send_sems, recv_sems = refs[2 * n], refs[2 * n + 1]
        token = refs[4 * n + 2]
        x, y, c, _ = _place()
        for i in range(n):
            h = BIG[idx[i]][2] // 2
            _remote(src[i].at[:, :, pl.ds((1 - c) * h, h), :], land[i], send_sems.at[i], recv_sems.at[i],
                    (x, y, 1 - c)).start()
        token[...] = jnp.zeros(TOKEN, F32)

    lands = [lax.empty((4, BIG[w][1], BIG[w][2] // 2, BIG[w][3]), BF16) for w in idx]
    arrays = list(gs) + lands
    outs = pl.pallas_call(
        body, name=name, in_specs=[HBM] * (2 * n),
        out_specs=(SEM, SEM) + (HBM,) * (2 * n) + (pl.BlockSpec(memory_space=pltpu.VMEM),),
        out_shape=(pltpu.SemaphoreType.DMA((n,)), pltpu.SemaphoreType.DMA((n,)))
        + tuple(pltpu.HBM(a.shape, a.dtype) for a in arrays) + (_sds(TOKEN, F32),),
        input_output_aliases={i: 2 + i for i in range(2 * n)},
        compiler_params=pltpu.CompilerParams(has_side_effects=EFFECT),
    )(*[pltpu.with_memory_space_constraint(a, pltpu.HBM) for a in arrays])
    return outs[0], outs[1], list(outs[2:2 + n]), list(outs[2 + n:2 + 2 * n]), outs[2 + 2 * n]


def _pair_wait(send_sems, recv_sems, gs, lands, after, idx, name):
    n = len(idx)

    def body(*refs):
        src, land = refs[:n], refs[n:2 * n]
        send_sems, recv_sems = refs[2 * n], refs[2 * n + 1]
        x, y, c, _ = _place()
        for i in range(n):
            h = BIG[idx[i]][2] // 2
            cp = _remote(src[i].at[:, :, pl.ds((1 - c) * h, h), :], land[i], send_sems.at[i], recv_sems.at[i],
                         (x, y, 1 - c))
            cp.wait_send()
            cp.wait_recv()

    arrays = list(gs) + list(lands)
    outs = pl.pallas_call(
        body, name=name, in_specs=[HBM] * (2 * n) + [SEM, SEM] + [ANY] * len(after), out_specs=(HBM,) * (2 * n),
        out_shape=tuple(pltpu.HBM(a.shape, a.dtype) for a in arrays),
        input_output_aliases={i: i for i in range(2 * n)},
        compiler_params=pltpu.CompilerParams(has_side_effects=EFFECT),
    )(*arrays, send_sems, recv_sems, *after)
    return list(outs[:n]), list(outs[n:])


def _pair_sums(place, gs, r1s, idx, name):
    n = len(idx)
    dims = [(BIG[w][1], BIG[w][2] // 2, BIG[w][3]) for w in idx]

    def body(pref, *refs):
        for i in range(n):
            refs[2 * n + i][...] = (refs[i][...] + refs[n + i][...].astype(F32)).astype(BF16)

    mine = [pl.BlockSpec((1, k, h, cdim), lambda s, pref: (s, 0, pref[0], 0)) for k, h, cdim in dims]
    whole = [pl.BlockSpec((1, k, h, cdim), lambda s, pref: (s, 0, 0, 0)) for k, h, cdim in dims]
    grid_spec = pltpu.PrefetchScalarGridSpec(num_scalar_prefetch=1, grid=(4,), in_specs=mine + whole, out_specs=whole)
    return pl.pallas_call(
        body, name=name, grid_spec=grid_spec, out_shape=[_sds((4, k, h, cdim), BF16) for k, h, cdim in dims],
        compiler_params=_params(("parallel",)),
    )(place, *gs, *r1s)


def _chip_start(ps, idx, name):
    n = len(idx)

    def body(*refs):
        src, land = refs[:n], refs[n:2 * n]
        send_sems, recv_sems = refs[2 * n], refs[2 * n + 1]
        token = refs[4 * n + 2]
        x, y, c, chips = _place()
        for j, (px, py) in enumerate(chips):
            for i in range(n):
                _remote(src[i].at[2 * px + py], land[i].at[j], send_sems.at[j * n + i], recv_sems.at[j * n + i],
                        (px, py, c)).start()
        token[...] = jnp.zeros(TOKEN, F32)

    lands = [lax.empty((3,) + p.shape[1:], BF16) for p in ps]
    outs = pl.pallas_call(
        body, name=name, in_specs=[HBM] * (2 * n),
        out_specs=(SEM, SEM) + (HBM,) * (2 * n) + (pl.BlockSpec(memory_space=pltpu.VMEM),),
        out_shape=(pltpu.SemaphoreType.DMA((3 * n,)), pltpu.SemaphoreType.DMA((3 * n,)))
        + tuple(pltpu.HBM(a.shape, a.dtype) for a in list(ps) + lands) + (_sds(TOKEN, F32),),
        input_output_aliases={i: 2 + i for i in range(2 * n)},
        compiler_params=pltpu.CompilerParams(has_side_effects=EFFECT),
    )(*[pltpu.with_memory_space_constraint(a, pltpu.HBM) for a in list(ps) + lands])
    return outs[0], outs[1], list(outs[2:2 + n]), list(outs[2 + n:2 + 2 * n]), outs[2 + 2 * n]


def _chip_wait(send_sems, recv_sems, ps, lands, after, idx, name):
    n = len(idx)

    def body(*refs):
        src, land = refs[:n], refs[n:2 * n]
        send_sems, recv_sems = refs[2 * n], refs[2 * n + 1]
        x, y, c, chips = _place()
        for j, (px, py) in enumerate(chips):
            for i in range(n):
                cp = _remote(src[i].at[2 * px + py], land[i].at[j], send_sems.at[j * n + i], recv_sems.at[j * n + i],
                             (px, py, c))
                cp.wait_send()
                cp.wait_recv()

    arrays = list(ps) + list(lands)
    outs = pl.pallas_call(
        body, name=name, in_specs=[HBM] * (2 * n) + [SEM, SEM] + [ANY] * len(after), out_specs=(HBM,) * (2 * n),
        out_shape=tuple(pltpu.HBM(a.shape, a.dtype) for a in arrays),
        input_output_aliases={i: i for i in range(2 * n)},
        compiler_params=pltpu.CompilerParams(has_side_effects=EFFECT),
    )(*arrays, send_sems, recv_sems, *after)
    return list(outs[n:])


def _chip_sums(place, gs, r1s, r2s, idx, name):
    n = len(idx)
    dims = [(BIG[w][1], BIG[w][2] // 4, BIG[w][3]) for w in idx]

    def body(pref, *refs):
        for i in range(n):
            acc = refs[i][0] + refs[n + i][0].astype(F32)
            for j in range(3):
                acc = acc + refs[2 * n + i][j].astype(F32)
            refs[3 * n + i][...] = acc

    in_specs = ([pl.BlockSpec((1, k, q, cdim), lambda t, pref: (pref[1], 0, pref[0] * 2 + t, 0)) for k, q, cdim in dims]
                + [pl.BlockSpec((1, k, q, cdim), lambda t, pref: (pref[1], 0, t, 0)) for k, q, cdim in dims]
                + [pl.BlockSpec((3, k, q, cdim), lambda t, pref: (0, 0, t, 0)) for k, q, cdim in dims])
    out_specs = [pl.BlockSpec((k, q, cdim), lambda t, pref: (0, pref[0] * 2 + t, 0)) for k, q, cdim in dims]
    grid_spec = pltpu.PrefetchScalarGridSpec(num_scalar_prefetch=1, grid=(2,), in_specs=in_specs, out_specs=out_specs)
    return pl.pallas_call(
        body, name=name, grid_spec=grid_spec, out_shape=[_sds(BIG[w][1:], F32) for w in idx],
        compiler_params=_params(("parallel",)),
    )(place, *gs, *r1s, *r2s)


def _pair_gather(hs, idx, name):
    n = len(idx)

    def body(*refs):
        dst = refs[n:2 * n]
        send_sems, recv_sems = refs[2 * n:]
        x, y, c, _ = _place()
        cps = []
        for i in range(n):
            mine = _half(dst[i], idx[i], c)
            cps.append(_remote(mine, mine, send_sems.at[i], recv_sems.at[i], (x, y, 1 - c)))
            cps[-1].start()
        for i in range(n):
            theirs = _half(dst[i], idx[i], 1 - c)
            _remote(theirs, theirs, send_sems.at[i], recv_sems.at[i], (x, y, 1 - c)).wait_recv()
        for cp in cps:
            cp.wait_send()

    return pl.pallas_call(
        body, name=name, in_specs=[ANY] * n, out_specs=[ANY] * n,
        out_shape=[_sds(BIG[w][1:], F32) for w in idx],
        input_output_aliases={i: i for i in range(n)},
        scratch_shapes=[pltpu.SemaphoreType.DMA((n,)), pltpu.SemaphoreType.DMA((n,))],
    )(*hs)


SMALL_ROWS = 40


def _adamw_math(w, g, m, v):
    m = ADAM_B1 * m + (1.0 - ADAM_B1) * g
    v = ADAM_B2 * v + (1.0 - ADAM_B2) * (g * g)
    m_hat = m / (1.0 - ADAM_B1 ** ADAM_STEP)
    v_hat = v / (1.0 - ADAM_B2 ** ADAM_STEP)
    delta = -ADAM_LR * (m_hat / (jnp.sqrt(v_hat) + ADAM_EPS) + ADAM_WD * w)
    return delta, m, v


def _small_start(pack, after):
    def body(pack_ref, land_ref, after_ref, send_sems, recv_sems, pack_thru, land_thru, token):
        x, y, c, _ = _place()
        for r in range(1, 8):
            peer = (x if not r & 4 else 1 - x, y if not r & 2 else 1 - y, c if not r & 1 else 1 - c)
            _remote(pack_ref, land_ref.at[r - 1], send_sems.at[r - 1], recv_sems.at[r - 1], peer).start()
        token[...] = jnp.zeros(TOKEN, F32)

    land = lax.empty((7, SMALL_ROWS, D), F32)
    outs = pl.pallas_call(
        body, name="small_start", in_specs=[HBM, HBM, ANY],
        out_specs=(SEM, SEM, HBM, HBM, pl.BlockSpec(memory_space=pltpu.VMEM)),
        out_shape=(pltpu.SemaphoreType.DMA((7,)), pltpu.SemaphoreType.DMA((7,)), pltpu.HBM(pack.shape, F32),
                   pltpu.HBM(land.shape, F32), _sds(TOKEN, F32)),
        input_output_aliases={0: 2, 1: 3},
        compiler_params=pltpu.CompilerParams(has_side_effects=EFFECT),
    )(pltpu.with_memory_space_constraint(pack, pltpu.HBM), pltpu.with_memory_space_constraint(land, pltpu.HBM), after)
    return outs


def _small_wait(send_sems, recv_sems, pack, land, after):
    def body(pack_ref, land_ref, send_sems, recv_sems, *rest):
        x, y, c, _ = _place()
        for r in range(1, 8):
            peer = (x if not r & 4 else 1 - x, y if not r & 2 else 1 - y, c if not r & 1 else 1 - c)
            cp = _remote(pack_ref, land_ref.at[r - 1], send_sems.at[r - 1], recv_sems.at[r - 1], peer)
            cp.wait_send()
            cp.wait_recv()

    return pl.pallas_call(
        body, name="small_wait", in_specs=[HBM, HBM, SEM, SEM] + [ANY] * len(after), out_specs=(HBM, HBM),
        out_shape=(pltpu.HBM(pack.shape, F32), pltpu.HBM(land.shape, F32)),
        input_output_aliases={0: 0, 1: 1},
        compiler_params=pltpu.CompilerParams(has_side_effects=EFFECT),
    )(pack, land, send_sems, recv_sems, *after)


def _small_update(place, pack, land, ws, ms, vs):
    n = len(ws)

    def body(pref, pack_ref, land_ref, *refs):
        chip = pref[1]
        me = 2 * chip + pref[0]
        own = pack_ref[...]
        tot = None
        for dev in range(8):
            r = jnp.bitwise_xor(me, dev)
            term = jnp.where(r == 0, own, land_ref[jnp.maximum(r - 1, 0)])
            tot = term if tot is None else tot + term
        out, buf = refs[3 * n:-1], refs[-1]
        buf[...] = tot
        g_conv = jnp.zeros((3, SH_O), F32)
        for s in range(4):
            g_conv = g_conv + jnp.where(chip == s, buf[24:27, s * SH_O:(s + 1) * SH_O], 0.0)
        gs = [buf[0:2, :], buf[8:10, :], buf[16:17, :], g_conv]
        out[0][...] = buf[32:33, 0:128]
        for i in range(n):
            d, nm, nv = _adamw_math(refs[i][...], gs[i], refs[n + i][...], refs[2 * n + i][...])
            out[1 + i][...] = gs[i]
            out[1 + n + i][...] = d
            out[1 + 2 * n + i][...] = nm
            out[1 + 3 * n + i][...] = nv

    def full(shape):
        nd = len(shape)
        return pl.BlockSpec(shape, lambda i, pref: (0,) * nd)

    specs = [full(w.shape) for w in ws]
    grid_spec = pltpu.PrefetchScalarGridSpec(
        num_scalar_prefetch=1, grid=(1,),
        in_specs=[full(pack.shape), full(land.shape)] + specs * 3, out_specs=[full((1, 128))] + specs * 4,
        scratch_shapes=[pltpu.VMEM((SMALL_ROWS, D), F32)])
    outs = pl.pallas_call(
        body, name="small_update", grid_spec=grid_spec,
        out_shape=[_sds((1, 128), F32)] + [_sds(w.shape, F32) for w in ws] * 4,
        compiler_params=_params(("arbitrary",)),
    )(place, pack, land, *ws, *ms, *vs)
    return outs[0], outs[1:1 + n], outs[1 + n:1 + 2 * n], outs[1 + 2 * n:1 + 3 * n], outs[1 + 3 * n:]


def _adamw_layer(ws, gs, ms, vs, idx, name):
    n = len(idx)
    dims = [(BIG[w][1], BIG[w][2] // 4, BIG[w][3]) for w in idx]

    def body(*refs):
        for i in range(n):
            gv = refs[n + i][...]
            d, nm, nv = _adamw_math(refs[i][...], gv, refs[2 * n + i][...], refs[3 * n + i][...])
            refs[4 * n + i][...] = d
            refs[5 * n + i][...] = nm
            refs[6 * n + i][...] = nv
            refs[7 * n + i][...] = gv

    specs = [pl.BlockSpec((k, q, cdim), lambda t: (0, t, 0)) for k, q, cdim in dims]
    outs = pl.pallas_call(
        body, name=name, grid=(4,), in_specs=specs * 4, out_specs=specs * 4,
        out_shape=[_sds(BIG[w][1:], F32) for w in idx] * 4,
        compiler_params=_params(("parallel",)),
    )(*ws, *gs, *ms, *vs)
    return [tuple(outs[j * n + i] for j in range(4)) for i in range(n)]


def _pad_rows(a, rows):
    return jnp.pad(a, ((0, rows - a.shape[0]), (0, 0)))


def kernel(x, mem, positions, norm_g, mem_norm_g, w_mem_kv, attn_w_in, attn_w_out, conv_w_in, conv_w, conv_w_out, final_g, loss_target, m_norm_g, m_mem_norm_g, m_w_mem_kv, m_attn_w_in, m_attn_w_out, m_conv_w_in, m_conv_w, m_conv_w_out, m_final_g, v_norm_g, v_mem_norm_g, v_w_mem_kv, v_attn_w_in, v_attn_w_out, v_conv_w_in, v_conv_w, v_conv_w_out, v_final_g):
    mx, my, mc = lax.axis_index("x"), lax.axis_index("y"), lax.axis_index("c")
    place = jnp.stack([mc, 2 * mx + my]).astype(jnp.int32)

    w_big = [w_mem_kv, attn_w_in, attn_w_out, conv_w_in, conv_w_out]
    m_big = [m_w_mem_kv, m_attn_w_in, m_attn_w_out, m_conv_w_in, m_conv_w_out]
    v_big = [v_w_mem_kv, v_attn_w_in, v_attn_w_out, v_conv_w_in, v_conv_w_out]
    first, rest = (1,), (0, 2, 3, 4)
    wb1 = _cast_weights(place, [w_big[i] for i in first], place, first, "cast_w_in_a")
    a1_send, a1_recv, a1_bufs, a1_token = _gather_start(wb1, place, first, "gather_a1_start")
    wbr = _cast_weights(place, [w_big[i] for i in rest], a1_token, rest, "cast_weights")
    r_send, r_recv, r_bufs, gb_token = _gather_start(wbr, a1_token, rest, "gather_rest_start")
    a2_send, a2_recv, gb_send, gb_recv = r_send, r_recv, r_send, r_recv
    a2_bufs, gb_bufs = r_bufs[:2], r_bufs[2:]
    started, rest = rest, (0, 2)

    xs, tgt = x[0], loss_target[0]
    g0, g1 = norm_g[0:1], norm_g[1:2]
    rc, rs1, rs2 = _rope_tables(positions[0].astype(F32).reshape(S, 1), gb_token)
    a1_bufs = _gather_wait(a1_send, a1_recv, a1_bufs, [rc], first, "gather_a1_wait")
    w_in_a = _gather_forward(a1_bufs, first, "gather_a1_forward")[0].reshape(4, D, SH_A)
    hn0, q, k, v, qm0, z0 = _in_proj_a(xs, g0, w_in_a, rc, rs1, rs2, gb_token)
    a2_bufs = _gather_wait(a2_send, a2_recv, a2_bufs, [q], rest, "gather_a2_wait", started)
    f2_send, f2_recv, a2_bufs, f2_token = _forward_start(a2_bufs, None, q, rest, "forward_a2_start")
    fwd = [_attn_fwd(q, k, v, 0, f2_token)]
    fwd.append(_attn_fwd(q, k, v, 1, fwd[0][0]))
    cw_own = _pad_rows(conv_w[0], CW_ROWS)
    gb_bufs = _gather_wait(gb_send, gb_recv, gb_bufs, [fwd[1][0]], LAYER_B, "gather_b_wait", started)
    fb_send, fb_recv, gb_bufs, fb_token = _forward_start(gb_bufs, cw_own, fwd[1][0], LAYER_B, "forward_b_start")
    fwd.append(_attn_fwd(q, k, v, 2, fb_token))
    os_, ls, lss = [f[0] for f in fwd], [f[1] for f in fwd], [f[2] for f in fwd]
    wkv_f, w_out_a = _forward_wait(f2_send, f2_recv, a2_bufs, [os_[2]], rest, False, "forward_a2_wait")
    w_out_a = w_out_a.reshape(4, BR_A, SH_O)
    memn, kv = _mem_fwd(mem[0], mem_norm_g, wkv_f)
    h1 = _attn_out(os_, ls, qm0, kv[0], z0, xs, w_out_a)

    w_in_b, w_out_b, _, cw_f = _forward_wait(fb_send, fb_recv, gb_bufs, [h1], LAYER_B, True, "forward_b_wait")
    w_in_b = w_in_b.reshape(4, D, SH_B)
    w_out_b = w_out_b.reshape(BR_B, D)
    cw_f = lax.dynamic_update_slice(cw_f, cw_own[None], (2 * mx + my, 0, 0))
    cw8 = cw_f.transpose(1, 0, 2).reshape(CW_ROWS, D)
    hn1, bg, cg, u, qm1, z1 = _in_proj_b(h1, g1, w_in_b)
    dh2, loss_part, dfg = _conv_out_loss(bg, cg, u, cw8, qm1, kv[1], z1, h1, w_out_b, final_g.reshape(1, D), tgt)

    dproj_b, dw_out_b, dcw, dkv1, dw_out_b16 = _conv_bwd(dh2, bg, cg, u, cw8, qm1, kv[1], z1, w_out_b)
    dw_in_b, dw_in_b16 = _w_in_grad(hn1, dproj_b, IN_B, "w_in_b_grad")
    gs_b = [dw_in_b.reshape(4, 1, D, SH_B), dw_out_b.reshape(4, 1, BR_B // 4, D)]
    gb_b = [dw_in_b16.reshape(4, 1, D, SH_B), dw_out_b16.reshape(4, 1, BR_B // 4, D)]
    pb_send, pb_recv, gb_b, pb_land, pb_token = _pair_start(gb_b, LAYER_B, "pair_b_start")
    dh1, dg1 = _in_proj_bwd(dproj_b, w_in_b, h1, g1, dh2, pb_token, IN_B, "in_proj_b_bwd")
    _, r1_b = _pair_wait(pb_send, pb_recv, gb_b, pb_land, [dh1], LAYER_B, "pair_b_wait")
    ps_b = _pair_sums(place, gs_b, r1_b, LAYER_B, "pair_sums_b")
    cb_send, cb_recv, cb_src, cb_land, cb_token = _chip_start(ps_b, LAYER_B, "chip_b_start")

    outs = _attn_out_bwd(dh1, os_, ls, qm0, kv[0], z0, w_out_a, cb_token)
    dos, dds, dqm, dz, dw_out_a, dkv0, dw_out_a16 = outs[0:3], outs[3:6], outs[6], outs[7], outs[8], outs[9], outs[10]
    bwd = [_attn_bwd(q, k, v, dos[g], lss[g], dds[g], g) for g in range(3)]
    dproj_a = _qkv_bwd([b[0] for b in bwd], [b[1] for b in bwd], [b[2] for b in bwd], dqm, dz, rc, rs1, rs2)
    dw_in_a, dw_in_a16 = _w_in_grad(hn0, dproj_a, IN_A, "w_in_a_grad")
    dwkv, dwkv16, dmg = _mem_bwd(mem[0], mem_norm_g, memn, wkv_f, dkv0, dkv1)

    gs_a = [dwkv, dw_in_a.reshape(4, 1, D, SH_A), dw_out_a.reshape(4, 1, BR_A, SH_O)]
    r1_a = _pair_exchange([dwkv16, dw_in_a16.reshape(4, 1, D, SH_A), dw_out_a16.reshape(4, 1, BR_A, SH_O)], LAYER_A,
                          "pair_exchange_a")
    ps_a = _pair_sums(place, gs_a, r1_a, LAYER_A, "pair_sums_a")
    ca_send, ca_recv, ca_src, ca_land, ca_token = _chip_start(ps_a, LAYER_A, "chip_a_start")

    gx, dg0 = _in_proj_bwd(dproj_a, w_in_a, xs, g0, dh1, ca_token, IN_A, "in_proj_a_bwd")
    pack = jnp.concatenate([_pad_rows(jnp.concatenate([dg0, dg1], axis=0), 8), _pad_rows(dmg, 8), _pad_rows(dfg, 8),
                            dcw, _pad_rows(jnp.pad(loss_part, ((0, 0), (0, D - 128))), 8)], axis=0)
    sm_send, sm_recv, pack, sm_land, sm_token = _small_start(pack, ca_token)
    r2_b = _chip_wait(cb_send, cb_recv, cb_src, cb_land, [ca_token], LAYER_B, "chip_b_wait")
    hs_b = _chip_sums(place, gs_b, r1_b, r2_b, LAYER_B, "chip_sums_b")
    g_b = _pair_gather(hs_b, LAYER_B, "pair_gather_b")
    upd_b = _adamw_layer([w_big[w] for w in LAYER_B], g_b, [m_big[w] for w in LAYER_B], [v_big[w] for w in LAYER_B],
                         LAYER_B, "adamw_b")
    r2_a = _chip_wait(ca_send, ca_recv, ca_src, ca_land, [gx, upd_b[0][0], upd_b[1][0], sm_token], LAYER_A,
                      "chip_a_wait")
    hs_a = _chip_sums(place, gs_a, r1_a, r2_a, LAYER_A, "chip_sums_a")
    g_a = _pair_gather(hs_a, LAYER_A, "pair_gather_a")
    upd_a = _adamw_layer([w_big[w] for w in LAYER_A], g_a, [m_big[w] for w in LAYER_A], [v_big[w] for w in LAYER_A],
                         LAYER_A, "adamw_a")
    upd = upd_a + upd_b
    g_big = [u[3] for u in upd]
    pack, sm_land = _small_wait(sm_send, sm_recv, pack, sm_land, [r2_a[0]])
    sw = [norm_g, mem_norm_g, final_g.reshape(1, D), conv_w[0]]
    sm = [m_norm_g, m_mem_norm_g, m_final_g.reshape(1, D), m_conv_w[0]]
    sv = [v_norm_g, v_mem_norm_g, v_final_g.reshape(1, D), v_conv_w[0]]
    loss_row, sg, sd, snm, snv = _small_update(place, pack, sm_land, sw, sm, sv)
    loss = loss_row[0, 0]
    g_norm, g_memnorm, g_final, g_conv = sg

    def order(norm, memnorm, wkv, w_in_a, w_out_a, w_in_b, conv, w_out_b, final):
        return (norm, memnorm, wkv, w_in_a, w_out_a, w_in_b, conv.reshape(1, 3, SH_O), w_out_b, final.reshape(D))

    grads = order(g_norm, g_memnorm, g_big[0], g_big[1], g_big[2], g_big[3], g_conv, g_big[4], g_final)
    deltas = order(sd[0], sd[1], upd[0][0], upd[1][0], upd[2][0], upd[3][0], sd[3], upd[4][0], sd[2])
    new_m = order(snm[0], snm[1], upd[0][1], upd[1][1], upd[2][1], upd[3][1], snm[3], upd[4][1], snm[2])
    new_v = order(snv[0], snv[1], upd[0][2], upd[1][2], upd[2][2], upd[3][2], snv[3], upd[4][2], snv[2])
    return (loss, gx[None], *grads, *deltas, *new_m, *new_v)
```

```python
import functools

import numpy as np
import jax
import jax.numpy as jnp
from jax import lax
from jax.experimental import pallas as pl
from jax.experimental.pallas import tpu as pltpu

F32 = jnp.float32
BF16 = jnp.bfloat16

S = 2048
D = 1024
TM = 256
NT = S // TM
HD = 64
GW = 512
NQ = 3 * GW
MW = 256
NM = 256
IN_A = 3 * NQ + MW + GW + MW
IN_B = 3 * D + MW + D + MW
BR_A = GW + MW
BR_B = D + MW
SH_A = IN_A // 4
SH_B = IN_B // 4
SH_O = D // 4
QBLK = 128
DILATIONS = (1, 4, 16)
EPS = 1e-6
SCALE = HD ** -0.5
NEG = -1e30
ROPE_THETA = 500000.0

ADAM_LR = 0.001
ADAM_B1 = 0.9
ADAM_B2 = 0.999
ADAM_EPS = 1e-08
ADAM_WD = 0.01
ADAM_STEP = 10

VMEM_LIMIT_BYTES = 60 * 1024 * 1024


def _params(sem=None):
    if sem is None:
        return pltpu.CompilerParams(vmem_limit_bytes=VMEM_LIMIT_BYTES)
    return pltpu.CompilerParams(dimension_semantics=sem, vmem_limit_bytes=VMEM_LIMIT_BYTES)


def _full(shape):
    nd = len(shape)
    return pl.BlockSpec(shape, lambda *_: (0,) * nd)


def _rows(width, tm=TM):
    return pl.BlockSpec((tm, width), lambda i: (i, 0))


def _sds(shape, dtype):
    return jax.ShapeDtypeStruct(shape, dtype)


def _silu_parts(z):
    sig = 0.5 * jnp.tanh(0.5 * z) + 0.5
    return z * sig, sig * (1.0 + z * (1.0 - sig))


def _dot(a, b):
    return jnp.dot(a, b, preferred_element_type=F32)


def _dot_nt(a, b):
    return lax.dot_general(a, b, (((1,), (1,)), ((), ())), preferred_element_type=F32)


def _dot_tn(a, b):
    return lax.dot_general(a, b, (((0,), (0,)), ((), ())), preferred_element_type=F32)


def _rope_fwd(t, c, s1, s2):
    return t * c + pltpu.roll(t, 120, 1) * s1 + pltpu.roll(t, 8, 1) * s2


def _rope_bwd(g, c, s1, s2):
    return g * c + pltpu.roll(g * s1, 8, 1) + pltpu.roll(g * s2, 120, 1)


MEM_HEADS = MW // HD


def _stack_heads(x):
    head = lax.broadcasted_iota(jnp.int32, x.shape, 1) // HD
    return jnp.concatenate([jnp.where(head == h, x, 0.0) for h in range(MEM_HEADS)], axis=0).astype(BF16)


def _unstack_heads(x4):
    tm = x4.shape[0] // MEM_HEADS
    head = lax.broadcasted_iota(jnp.int32, (tm, MW), 1) // HD
    out = x4[:tm]
    for h in range(1, MEM_HEADS):
        out = jnp.where(head == h, x4[h * tm:(h + 1) * tm], out)
    return out


def _mem_attn(qm, kv):
    q4 = _stack_heads(qm.astype(F32))
    s = _dot_nt(q4, kv[:, :MW]) * SCALE
    e = jnp.exp(s - jnp.max(s, axis=-1, keepdims=True))
    p = e * (1.0 / jnp.sum(e, axis=-1, keepdims=True))
    return p, _unstack_heads(_dot(p.astype(BF16), kv[:, MW:])), q4


def _mem_attn_bwd(dmo, p, mo, q4, kv, dkv_ref):
    tm = dmo.shape[0]
    head = lax.broadcasted_iota(jnp.int32, dmo.shape, 1) // HD
    prod = dmo * mo
    delta = jnp.concatenate([jnp.sum(jnp.where(head == h, prod, 0.0), axis=-1, keepdims=True)
                             for h in range(MEM_HEADS)], axis=0)
    d4 = _stack_heads(dmo)
    ds = (p * (_dot_nt(d4, kv[:, MW:]) - delta) * SCALE).astype(BF16)
    dkv_ref[:, :MW] += _dot_tn(ds, q4)
    dkv_ref[:, MW:] += _dot_tn(p.astype(BF16), d4)
    return _unstack_heads(_dot(ds, kv[:, :MW]))


def _merge(o_refs, l_refs):
    ls = [r[...] for r in l_refs]
    m = jnp.maximum(jnp.maximum(ls[0], ls[1]), ls[2])
    es = [jnp.exp(l - m) for l in ls]
    inv = 1.0 / (es[0] + es[1] + es[2])
    ws = [e * inv for e in es]
    os_ = [r[...] for r in o_refs]
    mix = ws[0] * os_[0] + ws[1] * os_[1] + ws[2] * os_[2]
    return ws, mix


def _conv_taps(cg, u, cgp, up, first):
    a = cg * u
    ap = jnp.where(first, 0.0, cgp * up)
    row = lax.broadcasted_iota(jnp.int32, a.shape, 0)
    a1 = jnp.where(row == 0, ap[7:8, :], pltpu.roll(a, 1, 0))
    a2 = jnp.where(row == 0, ap[6:7, :], jnp.where(row == 1, ap[7:8, :], pltpu.roll(a, 2, 0)))
    return a, a1, a2


def _rope_tables(posf, after):
    half = 8
    invf = np.float32(ROPE_THETA) ** (-np.arange(half, dtype=np.float32) * np.float32(2.0 / 16))
    lane = np.arange(128)
    table = np.where((lane % HD) < 16, invf[lane % half], 0.0).astype(np.float32)[None, :]

    def body(pos_ref, invf_ref, c_ref, s1_ref, s2_ref):
        ang = pos_ref[...] * invf_ref[...]
        jm = lax.broadcasted_iota(jnp.int32, ang.shape, 1) & (HD - 1)
        cs = jnp.cos(ang)
        sn = jnp.sin(ang)
        c_ref[...] = jnp.where(jm < 16, cs, 1.0)
        s1_ref[...] = jnp.where(jm < 8, -sn, 0.0)
        s2_ref[...] = jnp.where((jm >= 8) & (jm < 16), sn, 0.0)

    out = _sds((S, 128), F32)
    return pl.pallas_call(
        functools.partial(_skip_arg, body, 2), name="rope_tables", grid=(NT,),
        in_specs=[_rows(1), _full((1, 128)), pl.BlockSpec(memory_space=pl.ANY)],
        out_specs=[_rows(128)] * 3, out_shape=[out] * 3,
        compiler_params=_params(("parallel",)),
    )(posf, jnp.asarray(table), after)


def _resident_weight(w_hbm, wbuf, sems):
    first = pl.program_id(0) == 0
    copies = [pltpu.make_async_copy(w_hbm.at[s], wbuf.at[s], sems.at[s]) for s in range(4)]

    @pl.when(first)
    def _():
        for cp in copies:
            cp.start()

    def fetch(s):
        @pl.when(first)
        def _():
            copies[s].wait()

        return wbuf[s]

    return fetch


def _in_proj_a(x, g0, w_in, c, s1, s2, after):
    def body(x_ref, g_ref, w_hbm, c_ref, s1_ref, s2_ref, hn_ref, q_ref, k_ref, v_ref, qm_ref, z_ref, proj, wbuf, wsem):
        weight = _resident_weight(w_hbm, wbuf, wsem)
        xf = x_ref[...]
        hn = xf * lax.rsqrt(jnp.mean(xf * xf, axis=-1, keepdims=True) + EPS) * g_ref[...]
        hb = hn.astype(BF16)
        hn_ref[...] = hb
        for s in range(4):
            proj[:, s * SH_A:(s + 1) * SH_A] = _dot(hb, weight(s))
        cc, a1, a2 = c_ref[...], s1_ref[...], s2_ref[...]
        for j in range(NQ // 128):
            q_ref[:, j * 128:(j + 1) * 128] = (
                _rope_fwd(proj[:, j * 128:(j + 1) * 128], cc, a1, a2) * SCALE).astype(BF16)
            k_ref[:, j * 128:(j + 1) * 128] = _rope_fwd(
                proj[:, NQ + j * 128:NQ + (j + 1) * 128], cc, a1, a2).astype(BF16)
        v_ref[...] = proj[:, 2 * NQ:3 * NQ].astype(BF16)
        qm_ref[...] = proj[:, 3 * NQ:3 * NQ + MW].astype(BF16)
        z_ref[...] = proj[:, 3 * NQ + MW:]

    return pl.pallas_call(
        functools.partial(_skip_arg, body, 6), name="in_proj_a", grid=(NT,),
        in_specs=[_rows(D), _full((1, D)), pl.BlockSpec(memory_space=pl.ANY), _rows(128), _rows(128), _rows(128),
                  pl.BlockSpec(memory_space=pl.ANY)],
        out_specs=[_rows(D), _rows(NQ), _rows(NQ), _rows(NQ), _rows(MW), _rows(BR_A)],
        out_shape=[_sds((S, D), BF16), _sds((S, NQ), BF16), _sds((S, NQ), BF16), _sds((S, NQ), BF16),
                   _sds((S, MW), BF16), _sds((S, BR_A), F32)],
        scratch_shapes=[pltpu.VMEM((TM, IN_A), F32), pltpu.VMEM((4, D, SH_A), BF16), pltpu.SemaphoreType.DMA((4,))],
        compiler_params=_params(("arbitrary",)),
    )(x, g0, w_in, c, s1, s2, after)


def _mem_fwd(mem, mg, wkv):
    def body(mem_ref, mg_ref, w_ref, memn_ref, kv_ref):
        mf = mem_ref[...]
        n = mf * lax.rsqrt(jnp.mean(mf * mf, axis=-1, keepdims=True) + EPS)
        for i in range(2):
            mn = (n * mg_ref[i:i + 1, :]).astype(BF16)
            memn_ref[i] = mn
            acc = _dot(mn[:, 0:NM], w_ref[0, i])
            for s in range(1, 4):
                acc += _dot(mn[:, s * NM:(s + 1) * NM], w_ref[s, i])
            kv_ref[i] = acc.astype(BF16)

    return pl.pallas_call(
        body, name="mem_fwd", grid=(1,),
        in_specs=[_full((NM, D)), _full((2, D)), _full((4, 2, NM, 2 * MW))],
        out_specs=[_full((2, NM, D)), _full((2, NM, 2 * MW))],
        out_shape=[_sds((2, NM, D), BF16), _sds((2, NM, 2 * MW), BF16)],
        compiler_params=_params(("arbitrary",)),
    )(mem, mg, wkv)


def _band_mask(j):
    qi = lax.broadcasted_iota(jnp.int32, (QBLK, 2 * QBLK), 0)
    kj = lax.broadcasted_iota(jnp.int32, (QBLK, 2 * QBLK), 1)
    dist = qi + QBLK - kj
    return (dist >= 0) & (dist <= QBLK) & ((kj >= QBLK) | (j > 0))


LANES = 128
NCHUNK = GW // LANES
FWD_UNROLL = 16
BWD_UNROLL = 4
CONV_CHUNK = 256


def _perm_matrix(d):
    n = TM // d
    p = np.zeros((TM, TM), np.float32)
    for r in range(d):
        for i in range(n):
            p[r * n + i, i * d + r] = 1.0
    return p


def _split_dot(p, x, parts):
    hi = x.astype(BF16)
    rem = x - hi.astype(F32)
    lo = rem.astype(BF16)
    both = _dot(p, jnp.concatenate([hi, lo], axis=1))
    acc = both[:, :LANES] + both[:, LANES:]
    if parts == 3:
        acc = acc + _dot(p, (rem - lo.astype(F32)).astype(BF16))
    return acc


def _pair_dot(p, a, b):
    both = _dot(p, jnp.concatenate([a, b], axis=1))
    return both[:, :LANES], both[:, LANES:]


def _tile_to_streams(y, dst, t, d):
    n, ln = TM // d, S // d
    for r in range(d):
        dst[r * ln + t * n:r * ln + (t + 1) * n, :] = y[r * n:(r + 1) * n].astype(dst.dtype)


def _tile_from_streams(src, t, d):
    n, ln = TM // d, S // d
    return jnp.concatenate([src[r * ln + t * n:r * ln + (t + 1) * n, :] for r in range(d)], axis=0)


def _head_masks():
    first = lax.broadcasted_iota(jnp.int32, (TM, LANES), 1) < HD
    return first, jnp.logical_not(first)


def _attn_fwd(q, k, v, g, after):
    d = DILATIONS[g]
    nb = S // d // QBLK
    perm = _perm_matrix(d)

    def body(q_ref, k_ref, v_ref, p_ref, pt_ref, o_ref, l_ref, ls_ref, q0, q1, ks, vs, os_):
        first, second = _head_masks()
        pm = p_ref[...]
        for t in range(NT):
            rows = slice(t * TM, (t + 1) * TM)
            if d == 1:
                qt = q_ref[rows, :].astype(F32)
            else:
                qt, kt = _pair_dot(pm, q_ref[rows, :], k_ref[rows, :])
                _tile_to_streams(kt, ks, t, d)
                _tile_to_streams(_dot(pm, v_ref[rows, :]), vs, t, d)
            _tile_to_streams(jnp.where(first, qt, 0.0), q0, t, d)
            _tile_to_streams(jnp.where(second, qt, 0.0), q1, t, d)
        kref, vref = (k_ref, v_ref) if d == 1 else (ks, vs)
        oref, lref = (o_ref, l_ref) if d == 1 else (os_, ls_ref)

        def blk(b, carry):
            r0 = pl.multiple_of(b * QBLK, QBLK)
            p0 = pl.multiple_of(jnp.maximum(b - 1, 0) * QBLK, QBLK)
            kk = jnp.concatenate([kref[pl.ds(p0, QBLK), :], kref[pl.ds(r0, QBLK), :]], axis=0)
            vv = jnp.concatenate([vref[pl.ds(p0, QBLK), :], vref[pl.ds(r0, QBLK), :]], axis=0)
            valid = _band_mask(b & (nb - 1))
            acc, lse = [], []
            for qh in (q0, q1):
                s = jnp.where(valid, _dot_nt(qh[pl.ds(r0, QBLK), :], kk), NEG)
                m = jnp.max(s, axis=-1, keepdims=True)
                e = jnp.exp(s - m)
                l = jnp.sum(e, axis=-1, keepdims=True)
                acc.append(_dot(e.astype(BF16), vv) * (1.0 / l))
                lse.append(m + jnp.log(l))
            f = first[:QBLK]
            oref[pl.ds(r0, QBLK), :] = jnp.where(f, acc[0], acc[1])
            lref[pl.ds(r0, QBLK), :] = jnp.where(f, lse[0], lse[1])
            return carry

        lax.fori_loop(0, S // QBLK, blk, 0, unroll=FWD_UNROLL)
        if d > 1:
            ptm = pt_ref[...]
            for t in range(NT):
                rows = slice(t * TM, (t + 1) * TM)
                o_ref[rows, :] = _split_dot(ptm, _tile_from_streams(os_, t, d), 2)
                l_ref[rows, :] = _split_dot(ptm, _tile_from_streams(ls_ref, t, d), 3)

    qkv_spec = pl.BlockSpec((S, LANES), lambda c: (0, g * NCHUNK + c))
    out_spec = pl.BlockSpec((S, LANES), lambda c: (0, c))
    n_out = 2 if d == 1 else 3
    inner = body if d > 1 else functools.partial(_drop_arg, body, 7)
    outs = pl.pallas_call(
        functools.partial(_skip_arg, inner, 5), name=f"attn_fwd_g{g}", grid=(NCHUNK,),
        in_specs=[qkv_spec] * 3 + [_full((TM, TM))] * 2 + [pl.BlockSpec(memory_space=pl.ANY)],
        out_specs=[out_spec] * n_out, out_shape=[_sds((S, GW), F32)] * n_out,
        scratch_shapes=[pltpu.VMEM((S, LANES), BF16)] * 4 + [pltpu.VMEM((S, LANES), F32)],
        compiler_params=_params(("parallel",)),
    )(q, k, v, jnp.asarray(perm, BF16), jnp.asarray(perm.T, BF16), after)
    return (outs[0], outs[1], outs[1]) if d == 1 else tuple(outs)


def _drop_arg(body, pos, *refs):
    return body(*refs[:pos], None, *refs[pos:])


def _attn_out(os_, ls, qm, kv0, z, x, w_out):
    def body(o0, o1, o2, l0, l1, l2, qm_ref, kv_ref, z_ref, x_ref, w_ref, h_ref, ybuf):
        _, mix = _merge((o0, o1, o2), (l0, l1, l2))
        sz, _ = _silu_parts(z_ref[...])
        ybuf[:, :GW] = (mix * sz[:, :GW]).astype(BF16)
        _, mo, _ = _mem_attn(qm_ref[...], kv_ref[...])
        ybuf[:, GW:] = (mo * sz[:, GW:]).astype(BF16)
        yb = ybuf[...]
        for s in range(4):
            cs = slice(s * SH_O, (s + 1) * SH_O)
            h_ref[:, cs] = x_ref[:, cs] + _dot(yb, w_ref[s])

    return pl.pallas_call(
        body, name="attn_out", grid=(NT,),
        in_specs=[_rows(GW)] * 6 + [_rows(MW), _full((NM, 2 * MW)), _rows(BR_A), _rows(D), _full((4, BR_A, SH_O))],
        out_specs=_rows(D), out_shape=_sds((S, D), F32),
        scratch_shapes=[pltpu.VMEM((TM, BR_A), BF16)],
        compiler_params=_params(("parallel",)),
    )(*os_, *ls, qm, kv0, z, x, w_out)


def _in_proj_b(h1, g1, w_in):
    def body(x_ref, g_ref, w_hbm, hn_ref, bg_ref, cg_ref, u_ref, qm_ref, z_ref, proj, wbuf, wsem):
        weight = _resident_weight(w_hbm, wbuf, wsem)
        xf = x_ref[...]
        hn = xf * lax.rsqrt(jnp.mean(xf * xf, axis=-1, keepdims=True) + EPS) * g_ref[...]
        hb = hn.astype(BF16)
        hn_ref[...] = hb
        for s in range(4):
            proj[:, s * SH_B:(s + 1) * SH_B] = _dot(hb, weight(s))
        bg_ref[...] = proj[:, :D]
        cg_ref[...] = proj[:, D:2 * D]
        u_ref[...] = proj[:, 2 * D:3 * D]
        qm_ref[...] = proj[:, 3 * D:3 * D + MW].astype(BF16)
        z_ref[...] = proj[:, 3 * D + MW:]

    return pl.pallas_call(
        body, name="in_proj_b", grid=(NT,),
        in_specs=[_rows(D), _full((1, D)), pl.BlockSpec(memory_space=pl.ANY)],
        out_specs=[_rows(D), _rows(D), _rows(D), _rows(D), _rows(MW), _rows(BR_B)],
        out_shape=[_sds((S, D), BF16), _sds((S, D), F32), _sds((S, D), F32), _sds((S, D), F32),
                   _sds((S, MW), BF16), _sds((S, BR_B), F32)],
        scratch_shapes=[pltpu.VMEM((TM, IN_B), F32), pltpu.VMEM((4, D, SH_B), BF16), pltpu.SemaphoreType.DMA((4,))],
        compiler_params=_params(("arbitrary",)),
    )(h1, g1, w_in)


def _prev8(width):
    return pl.BlockSpec((8, width), lambda i: (jnp.maximum(i * (TM // 8) - 1, 0), 0))


def _conv_out_loss(bg, cg, u, cw, qm, kv1, z, h1, w_out, fg, tgt):
    def body(bg_ref, cg_ref, u_ref, cgp_ref, up_ref, cw_ref, qm_ref, kv_ref, z_ref, h_ref, w_ref, fg_ref, t_ref,
             dh_ref, loss_ref, dfg_ref, ybuf):
        i = pl.program_id(0)
        a, a1, a2 = _conv_taps(cg_ref[...], u_ref[...], cgp_ref[...], up_ref[...], i == 0)
        conv = cw_ref[0:1, :] * a2 + cw_ref[1:2, :] * a1 + cw_ref[2:3, :] * a
        sz, _ = _silu_parts(z_ref[...])
        ybuf[:, :D] = (bg_ref[...] * conv * sz[:, :D]).astype(BF16)
        _, mo, _ = _mem_attn(qm_ref[...], kv_ref[...])
        ybuf[:, D:] = (mo * sz[:, D:]).astype(BF16)
        h2 = h_ref[...] + _dot(ybuf[...], w_ref[...])
        rstd = lax.rsqrt(jnp.mean(h2 * h2, axis=-1, keepdims=True) + EPS)
        n = h2 * rstd
        fgv = fg_ref[...]
        err = n * fgv - t_ref[...]
        dout = err * (1.0 / D)
        dn = dout * fgv
        dh_ref[...] = rstd * (dn - n * jnp.mean(dn * n, axis=-1, keepdims=True))

        @pl.when(i == 0)
        def _():
            loss_ref[...] = jnp.zeros_like(loss_ref)
            dfg_ref[...] = jnp.zeros_like(dfg_ref)

        loss_ref[...] += jnp.sum(err * err) * (0.5 / D)
        dfg_ref[...] += jnp.sum(dout * n, axis=0, keepdims=True)

    return pl.pallas_call(
        body, name="conv_out_loss", grid=(NT,),
        in_specs=[_rows(D), _rows(D), _rows(D), _prev8(D), _prev8(D), _full((8, D)), _rows(MW),
                  _full((NM, 2 * MW)), _rows(BR_B), _rows(D), _full((BR_B, D)), _full((1, D)), _rows(D)],
        out_specs=[_rows(D), _full((1, 128)), _full((1, D))],
        out_shape=[_sds((S, D), F32), _sds((1, 128), F32), _sds((1, D), F32)],
        scratch_shapes=[pltpu.VMEM((TM, BR_B), BF16)],
        compiler_params=_params(("arbitrary",)),
    )(bg, cg, u, cg, u, cw, qm, kv1, z, h1, w_out, fg, tgt)


def _conv_bwd(dh2, bg, cg, u, cw, qm, kv1, z, w_out):
    rev = lambda i: (NT - 1 - i, 0)
    rows = lambda w: pl.BlockSpec((TM, w), rev)
    prev8 = pl.BlockSpec((8, D), lambda i: (jnp.maximum((NT - 1 - i) * (TM // 8) - 1, 0), 0))

    def body(dh_ref, bg_ref, cg_ref, u_ref, cgp_ref, up_ref, cw_ref, qm_ref, kv_ref, z_ref, w_ref,
             dproj_ref, dw_ref, dcw_ref, dkv_ref, dwb_ref, ybuf, carry):
        i = pl.program_id(0)

        @pl.when(i == 0)
        def _():
            dw_ref[...] = jnp.zeros_like(dw_ref)
            dcw_ref[...] = jnp.zeros_like(dcw_ref)
            dkv_ref[...] = jnp.zeros_like(dkv_ref)
            carry[...] = jnp.zeros_like(carry)

        dhb = dh_ref[...].astype(BF16)
        dy = _dot_nt(dhb, w_ref[...])
        kvv = kv_ref[...]
        p, mo, q4 = _mem_attn(qm_ref[...], kvv)
        szm, dszm = _silu_parts(z_ref[:, D:])
        ybuf[:, D:] = (mo * szm).astype(BF16)
        dym = dy[:, D:]
        dproj_ref[:, 3 * D + MW + D:] = (dym * mo * dszm).astype(BF16)
        first_tile = i == NT - 1
        for c in range(D // CONV_CHUNK):
            cs = slice(c * CONV_CHUNK, (c + 1) * CONV_CHUNK)
            bgv, cgv, uv = bg_ref[:, cs], cg_ref[:, cs], u_ref[:, cs]
            a, a1, a2 = _conv_taps(cgv, uv, cgp_ref[:, cs], up_ref[:, cs], first_tile)
            w0, w1, w2 = cw_ref[0:1, cs], cw_ref[1:2, cs], cw_ref[2:3, cs]
            conv = w0 * a2 + w1 * a1 + w2 * a
            mix = bgv * conv
            sz, dsz = _silu_parts(z_ref[:, cs])
            ybuf[:, cs] = (mix * sz).astype(BF16)
            dyc = dy[:, cs]
            dproj_ref[:, 3 * D + MW + c * CONV_CHUNK:3 * D + MW + (c + 1) * CONV_CHUNK] = (
                dyc * mix * dsz).astype(BF16)
            dmix = dyc * sz
            dproj_ref[:, cs] = (dmix * conv).astype(BF16)
            dc = dmix * bgv
            nxt = carry[:, cs]
            row = lax.broadcasted_iota(jnp.int32, dc.shape, 0)
            dc1 = jnp.where(row == TM - 1, nxt[0:1, :], pltpu.roll(dc, TM - 1, 0))
            dc2 = jnp.where(row == TM - 2, nxt[0:1, :],
                            jnp.where(row == TM - 1, nxt[1:2, :], pltpu.roll(dc, TM - 2, 0)))
            carry[:, cs] = dc[0:8, :]
            da = w2 * dc + w1 * dc1 + w0 * dc2
            dproj_ref[:, D + c * CONV_CHUNK:D + (c + 1) * CONV_CHUNK] = (da * uv).astype(BF16)
            dproj_ref[:, 2 * D + c * CONV_CHUNK:2 * D + (c + 1) * CONV_CHUNK] = (da * cgv).astype(BF16)
            dcw_ref[0:1, cs] += jnp.sum(dc * a2, axis=0, keepdims=True)
            dcw_ref[1:2, cs] += jnp.sum(dc * a1, axis=0, keepdims=True)
            dcw_ref[2:3, cs] += jnp.sum(dc * a, axis=0, keepdims=True)
        dw_ref[...] += _dot_tn(ybuf[...], dhb)
        dproj_ref[:, 3 * D:3 * D + MW] = _mem_attn_bwd(dym * szm, p, mo, q4, kvv, dkv_ref).astype(BF16)

        @pl.when(i == NT - 1)
        def _():
            dwb_ref[...] = dw_ref[...].astype(BF16)

    return pl.pallas_call(
        body, name="conv_bwd", grid=(NT,),
        in_specs=[rows(D), rows(D), rows(D), rows(D), prev8, prev8, _full((8, D)), rows(MW),
                  _full((NM, 2 * MW)), rows(BR_B), _full((BR_B, D))],
        out_specs=[rows(IN_B), _full((BR_B, D)), _full((8, D)), _full((NM, 2 * MW)), _full((BR_B, D))],
        out_shape=[_sds((S, IN_B), BF16), _sds((BR_B, D), F32), _sds((8, D), F32), _sds((NM, 2 * MW), F32),
                   _sds((BR_B, D), BF16)],
        scratch_shapes=[pltpu.VMEM((TM, BR_B), BF16), pltpu.VMEM((8, D), F32)],
        compiler_params=_params(("arbitrary",)),
    )(dh2, bg, cg, u, cg, u, cw, qm, kv1, z, w_out)


def _in_proj_bwd(dproj, w_in, xin, g, dres, after, width, name):
    sh = width // 4

    def body(dp_ref, w_hbm, x_ref, g_ref, dr_ref, dx_ref, dg_ref, wbuf, wsem):
        i = pl.program_id(0)
        weight = _resident_weight(w_hbm, wbuf, wsem)
        dhn = _dot_nt(dp_ref[:, 0:sh], weight(0))
        for s in range(1, 4):
            dhn += _dot_nt(dp_ref[:, s * sh:(s + 1) * sh], weight(s))
        xf = x_ref[...]
        rstd = lax.rsqrt(jnp.mean(xf * xf, axis=-1, keepdims=True) + EPS)
        n = xf * rstd
        dn = dhn * g_ref[...]
        dx_ref[...] = dr_ref[...] + rstd * (dn - n * jnp.mean(dn * n, axis=-1, keepdims=True))

        @pl.when(i == 0)
        def _():
            dg_ref[...] = jnp.zeros_like(dg_ref)

        dg_ref[...] += jnp.sum(dhn * n, axis=0, keepdims=True)

    return pl.pallas_call(
        functools.partial(_skip_arg, body, 5), name=name, grid=(NT,),
        in_specs=[_rows(width), pl.BlockSpec(memory_space=pl.ANY), _rows(D), _full((1, D)), _rows(D),
                  pl.BlockSpec(memory_space=pl.ANY)],
        out_specs=[_rows(D), _full((1, D))],
        out_shape=[_sds((S, D), F32), _sds((1, D), F32)],
        scratch_shapes=[pltpu.VMEM((4, D, sh), BF16), pltpu.SemaphoreType.DMA((4,))],
        compiler_params=_params(("arbitrary",)),
    )(dproj, w_in, xin, g, dres, after)


def _w_in_grad(hn, dproj, width, name):
    sh = width // 4

    def body(hn_ref, dp_ref, dw_ref, dwb_ref):
        dw = _dot_tn(hn_ref[...], dp_ref[...])
        dw_ref[0] = dw
        dwb_ref[0] = dw.astype(BF16)

    spec = pl.BlockSpec((1, D, sh), lambda s: (s, 0, 0))
    return pl.pallas_call(
        body, name=name, grid=(4,),
        in_specs=[_full((S, D)), pl.BlockSpec((S, sh), lambda s: (0, s))],
        out_specs=[spec, spec], out_shape=[_sds((4, D, sh), F32), _sds((4, D, sh), BF16)],
        compiler_params=_params(("parallel",)),
    )(hn, dproj)


def _attn_out_bwd(dh1, os_, ls, qm, kv0, z, w_out, after):
    ones_bd = np.kron(np.eye(GW // HD, dtype=np.float32), np.ones((HD, HD), np.float32))

    def body(dh_ref, o0, o1, o2, l0, l1, l2, qm_ref, kv_ref, z_ref, w_ref, bd_ref,
             do0, do1, do2, dd0, dd1, dd2, dqm_ref, dz_ref, dw_ref, dkv_ref, dwb_ref, ybuf):
        i = pl.program_id(0)

        @pl.when(i == 0)
        def _():
            dw_ref[...] = jnp.zeros_like(dw_ref)
            dkv_ref[...] = jnp.zeros_like(dkv_ref)

        ws, mix = _merge((o0, o1, o2), (l0, l1, l2))
        sz, dsz = _silu_parts(z_ref[...])
        kvv = kv_ref[...]
        p, mo, q4 = _mem_attn(qm_ref[...], kvv)
        ybuf[:, :GW] = (mix * sz[:, :GW]).astype(BF16)
        ybuf[:, GW:] = (mo * sz[:, GW:]).astype(BF16)
        yb = ybuf[...]
        dh = dh_ref[...]
        dy = None
        for s in range(4):
            dhb = dh[:, s * SH_O:(s + 1) * SH_O].astype(BF16)
            dw_ref[s] += _dot_tn(yb, dhb)
            part = _dot_nt(dhb, w_ref[s])
            dy = part if dy is None else dy + part
        dcat = dy * sz
        dz_ref[:, :GW] = (dy[:, :GW] * mix * dsz[:, :GW]).astype(BF16)
        dz_ref[:, GW:] = (dy[:, GW:] * mo * dsz[:, GW:]).astype(BF16)
        dmix = dcat[:, :GW]
        prod = dmix * mix
        hi = prod.astype(BF16)
        lo = (prod - hi.astype(F32)).astype(BF16)
        bd = bd_ref[...]
        tot = _dot(hi, bd) + _dot(lo, bd)
        for w, do_ref, dd_ref in zip(ws, (do0, do1, do2), (dd0, dd1, dd2)):
            do_ref[...] = (w * dmix).astype(BF16)
            dd_ref[...] = w * tot

        dqm_ref[...] = _mem_attn_bwd(dcat[:, GW:], p, mo, q4, kvv, dkv_ref).astype(BF16)

        @pl.when(i == NT - 1)
        def _():
            dwb_ref[...] = dw_ref[...].astype(BF16)

    return pl.pallas_call(
        functools.partial(_skip_arg, body, 12), name="attn_out_bwd", grid=(NT,),
        in_specs=[_rows(D)] + [_rows(GW)] * 6 + [_rows(MW), _full((NM, 2 * MW)), _rows(BR_A),
                                                   _full((4, BR_A, SH_O)), _full((GW, GW)),
                                                   pl.BlockSpec(memory_space=pl.ANY)],
        out_specs=[_rows(GW)] * 6 + [_rows(MW), _rows(BR_A), _full((4, BR_A, SH_O)), _full((NM, 2 * MW)),
                                     _full((4, BR_A, SH_O))],
        out_shape=[_sds((S, GW), BF16)] * 3 + [_sds((S, GW), F32)] * 3 + [
            _sds((S, MW), BF16), _sds((S, BR_A), BF16), _sds((4, BR_A, SH_O), F32), _sds((NM, 2 * MW), F32),
            _sds((4, BR_A, SH_O), BF16)],
        scratch_shapes=[pltpu.VMEM((TM, BR_A), BF16)],
        compiler_params=_params(("arbitrary",)),
    )(dh1, *os_, *ls, qm, kv0, z, w_out, jnp.asarray(ones_bd, dtype=BF16), after)


def _attn_bwd(q, k, v, do, lse_s, dd, g):
    d = DILATIONS[g]
    nb = S // d // QBLK
    perm = _perm_matrix(d)

    def body(q_ref, k_ref, v_ref, do_ref, l_ref, dd_ref, p_ref, pt_ref, dq_ref, dk_ref, dv_ref,
             q0, q1, g0, g1, ks, vs, dds, dqs, dks, dvs):
        first, second = _head_masks()
        pm = p_ref[...]
        for t in range(NT):
            rows = slice(t * TM, (t + 1) * TM)
            if d == 1:
                qt = q_ref[rows, :].astype(F32)
                gt = do_ref[rows, :].astype(F32)
            else:
                qt, gt = _pair_dot(pm, q_ref[rows, :], do_ref[rows, :])
                kt, vt = _pair_dot(pm, k_ref[rows, :], v_ref[rows, :])
                _tile_to_streams(kt, ks, t, d)
                _tile_to_streams(vt, vs, t, d)
                _tile_to_streams(_split_dot(pm, dd_ref[rows, :], 2), dds, t, d)
            _tile_to_streams(jnp.where(first, qt, 0.0), q0, t, d)
            _tile_to_streams(jnp.where(second, qt, 0.0), q1, t, d)
            _tile_to_streams(jnp.where(first, gt, 0.0), g0, t, d)
            _tile_to_streams(jnp.where(second, gt, 0.0), g1, t, d)
        kref, vref, ddref = (k_ref, v_ref, dd_ref) if d == 1 else (ks, vs, dds)
        dqref, dkref, dvref = dqs, dks, dvs
        dkref[...] = jnp.zeros_like(dkref)
        dvref[...] = jnp.zeros_like(dvref)

        def blk(b, carry):
            r0 = pl.multiple_of(b * QBLK, QBLK)
            p0 = pl.multiple_of(jnp.maximum(b - 1, 0) * QBLK, QBLK)
            kk = jnp.concatenate([kref[pl.ds(p0, QBLK), :], kref[pl.ds(r0, QBLK), :]], axis=0)
            vv = jnp.concatenate([vref[pl.ds(p0, QBLK), :], vref[pl.ds(r0, QBLK), :]], axis=0)
            lb = l_ref[pl.ds(r0, QBLK), :]
            ddb = ddref[pl.ds(r0, QBLK), :]
            lcol = jnp.concatenate([lb[:, 0:1], lb[:, HD:HD + 1]], axis=0)
            dcol = jnp.concatenate([ddb[:, 0:1], ddb[:, HD:HD + 1]], axis=0)
            valid = _band_mask(b & (nb - 1))
            valid2 = jnp.concatenate([valid, valid], axis=0)
            qq = jnp.concatenate([q0[pl.ds(r0, QBLK), :], q1[pl.ds(r0, QBLK), :]], axis=0)
            gg = jnp.concatenate([g0[pl.ds(r0, QBLK), :], g1[pl.ds(r0, QBLK), :]], axis=0)
            p = jnp.where(valid2, jnp.exp(_dot_nt(qq, kk) - lcol), 0.0)
            ds = (p * (_dot_nt(gg, vv) - dcol)).astype(BF16)
            dq2 = _dot(ds, kk)
            dqref[pl.ds(r0, QBLK), :] = jnp.where(first[:QBLK], dq2[:QBLK], dq2[QBLK:])
            dkk = _dot_tn(ds, qq)
            dvv = _dot_tn(p.astype(BF16), gg)
            dkref[pl.ds(p0, QBLK), :] += dkk[:QBLK]
            dkref[pl.ds(r0, QBLK), :] += dkk[QBLK:]
            dvref[pl.ds(p0, QBLK), :] += dvv[:QBLK]
            dvref[pl.ds(r0, QBLK), :] += dvv[QBLK:]
            return carry

        lax.fori_loop(0, S // QBLK, blk, 0, unroll=BWD_UNROLL)

        ptm = pt_ref[...] if d > 1 else None
        for t in range(NT):
            rows = slice(t * TM, (t + 1) * TM)
            if d == 1:
                dq_ref[rows, :] = dqs[rows, :].astype(BF16)
                dk_ref[rows, :] = dks[rows, :].astype(BF16)
                dv_ref[rows, :] = dvs[rows, :].astype(BF16)
            else:
                tq, tk = _pair_dot(ptm, _tile_from_streams(dqs, t, d).astype(BF16),
                                   _tile_from_streams(dks, t, d).astype(BF16))
                dq_ref[rows, :] = tq.astype(BF16)
                dk_ref[rows, :] = tk.astype(BF16)
                dv_ref[rows, :] = _dot(ptm, _tile_from_streams(dvs, t, d).astype(BF16)).astype(BF16)

    qkv_spec = pl.BlockSpec((S, LANES), lambda c: (0, g * NCHUNK + c))
    one_spec = pl.BlockSpec((S, LANES), lambda c: (0, c))
    return pl.pallas_call(
        body, name=f"attn_bwd_g{g}", grid=(NCHUNK,),
        in_specs=[qkv_spec] * 3 + [one_spec] * 3 + [_full((TM, TM))] * 2, out_specs=[one_spec] * 3,
        out_shape=[_sds((S, GW), BF16)] * 3,
        scratch_shapes=[pltpu.VMEM((S, LANES), BF16)] * 6 + [pltpu.VMEM((S, LANES), F32)] * 4,
        compiler_params=_params(("parallel",)),
    )(q, k, v, do, lse_s, dd, jnp.asarray(perm, BF16), jnp.asarray(perm.T, BF16))


def _qkv_bwd(dqs, dks, dvs, dqm, dz, c, s1, s2):
    def body(q0, q1, q2, k0, k1, k2, v0, v1, v2, dqm_ref, dz_ref, c_ref, s1_ref, s2_ref, dp_ref):
        cc, a1, a2 = c_ref[...], s1_ref[...], s2_ref[...]
        for g, (qr, kr, vr) in enumerate(((q0, k0, v0), (q1, k1, v1), (q2, k2, v2))):
            for j in range(GW // 128):
                ls_ = slice(j * 128, (j + 1) * 128)
                c0 = g * GW + j * 128
                dp_ref[:, c0:c0 + 128] = (_rope_bwd(qr[:, ls_].astype(F32), cc, a1, a2) * SCALE).astype(BF16)
                dp_ref[:, NQ + c0:NQ + c0 + 128] = _rope_bwd(kr[:, ls_].astype(F32), cc, a1, a2).astype(BF16)
            dp_ref[:, 2 * NQ + g * GW:2 * NQ + (g + 1) * GW] = vr[...]
        dp_ref[:, 3 * NQ:3 * NQ + MW] = dqm_ref[...]
        dp_ref[:, 3 * NQ + MW:] = dz_ref[...]

    return pl.pallas_call(
        body, name="qkv_bwd", grid=(NT,),
        in_specs=[_rows(GW)] * 9 + [_rows(MW), _rows(BR_A), _rows(128), _rows(128), _rows(128)],
        out_specs=_rows(IN_A), out_shape=_sds((S, IN_A), BF16),
        compiler_params=_params(("parallel",)),
    )(*dqs, *dks, *dvs, dqm, dz, c, s1, s2)


def _mem_bwd(mem, mg, memn, wkv, dkv0, dkv1):
    def body(mem_ref, mg_ref, memn_ref, w_ref, d0_ref, d1_ref, dw_ref, dwb_ref, dg_ref):
        mf = mem_ref[...]
        n = mf * lax.rsqrt(jnp.mean(mf * mf, axis=-1, keepdims=True) + EPS)
        for i, d_ref in enumerate((d0_ref, d1_ref)):
            dkv = d_ref[...].astype(BF16)
            mn = memn_ref[i]
            for s in range(4):
                cs = slice(s * NM, (s + 1) * NM)
                dw = _dot_tn(mn[:, cs], dkv)
                dw_ref[s, i] = dw
                dwb_ref[s, i] = dw.astype(BF16)
                dmn = _dot_nt(dkv, w_ref[s, i])
                dg_ref[i:i + 1, cs] = jnp.sum(dmn * n[:, cs], axis=0, keepdims=True)

    return pl.pallas_call(
        body, name="mem_bwd", grid=(1,),
        in_specs=[_full((NM, D)), _full((2, D)), _full((2, NM, D)), _full((4, 2, NM, 2 * MW)),
                  _full((NM, 2 * MW)), _full((NM, 2 * MW))],
        out_specs=[_full((4, 2, NM, 2 * MW)), _full((4, 2, NM, 2 * MW)), _full((2, D))],
        out_shape=[_sds((4, 2, NM, 2 * MW), F32), _sds((4, 2, NM, 2 * MW), BF16), _sds((2, D), F32)],
        compiler_params=_params(("arbitrary",)),
    )(mem, mg, memn, wkv, dkv0, dkv1)


MESH = pl.DeviceIdType.MESH
ANY = pl.BlockSpec(memory_space=pl.ANY)
BIG = (("wkv", 2, NM, 2 * MW), ("w_in_a", 1, D, SH_A), ("w_out_a", 1, BR_A, SH_O),
       ("w_in_b", 1, D, SH_B), ("w_out_b", 1, BR_B // 4, D))
NBIG = len(BIG)
CW_ROWS = 8


def _place():
    x, y, c = lax.axis_index("x"), lax.axis_index("y"), lax.axis_index("c")
    chips = ((1 - x, y), (x, 1 - y), (1 - x, 1 - y))
    return x, y, c, chips


def _remote(src, dst, ssem, rsem, dev):
    return pltpu.make_async_remote_copy(src_ref=src, dst_ref=dst, send_sem=ssem, recv_sem=rsem,
                                        device_id=dev, device_id_type=MESH)


def _cast_weights(place, ws, after, idx, name):
    nblk = 4
    n = len(idx)
    dims = [BIG[w][1:] for w in idx]

    def body(pref, *refs):
        for i in range(n):
            refs[n + 1 + i][0] = refs[i][...].astype(BF16)

    grid_spec = pltpu.PrefetchScalarGridSpec(
        num_scalar_prefetch=1, grid=(nblk,),
        in_specs=[pl.BlockSpec((k, r // nblk, cdim), lambda i, pref: (0, i, 0)) for k, r, cdim in dims]
        + [pl.BlockSpec(memory_space=pl.ANY)],
        out_specs=[pl.BlockSpec((1, k, r // nblk, cdim), lambda i, pref: (pref[1], 0, i, 0)) for k, r, cdim in dims])
    return pl.pallas_call(
        body, name=name, grid_spec=grid_spec,
        out_shape=[_sds((4, k, r, cdim), BF16) for k, r, cdim in dims],
        compiler_params=_params(("parallel",)),
    )(place, *ws, after)


LAYER_A = (0, 1, 2)
LAYER_B = (3, 4)
HBM = pl.BlockSpec(memory_space=pltpu.HBM)
SEM = pl.BlockSpec(memory_space=pltpu.SEMAPHORE)
EFFECT = pltpu.SideEffectType.DATAFLOW_SIDE_EFFECTING
TOKEN = (8, 128)


def _half(ref, w, which):
    h = BIG[w][2] // 2
    return ref.at[:, pl.ds(which * h, h), :]


def _skip_arg(body, pos, *refs):
    return body(*refs[:pos], *refs[pos + 1:])


def _gather_start(wb, after, idx, name):
    n = len(idx)

    def body(*refs):
        src = refs[:n]
        send_sems, recv_sems = refs[n + 1], refs[n + 2]
        token = refs[2 * n + 3]
        x, y, c, chips = _place()
        me = 2 * x + y
        for j, (px, py) in enumerate(chips):
            for i in range(n):
                mine = _half(src[i].at[me], idx[i], c)
                _remote(mine, mine, send_sems.at[j * n + i], recv_sems.at[j * n + i], (px, py, c)).start()
        token[...] = jnp.zeros(TOKEN, F32)

    outs = pl.pallas_call(
        body, name=name, in_specs=[HBM] * n + [ANY],
        out_specs=(SEM, SEM) + (HBM,) * n + (pl.BlockSpec(memory_space=pltpu.VMEM),),
        out_shape=(pltpu.SemaphoreType.DMA((3 * n,)), pltpu.SemaphoreType.DMA((3 * n,)))
        + tuple(pltpu.HBM(w.shape, w.dtype) for w in wb) + (_sds(TOKEN, F32),),
        input_output_aliases={i: 2 + i for i in range(n)},
        compiler_params=pltpu.CompilerParams(has_side_effects=EFFECT),
    )(*[pltpu.with_memory_space_constraint(w, pltpu.HBM) for w in wb], after)
    return outs[0], outs[1], list(outs[2:2 + n]), outs[2 + n]


def _gather_wait(send_sems, recv_sems, wb, after, idx, name, started=None):
    n = len(idx)
    started = idx if started is None else started
    n_all = len(started)
    pos = [started.index(w) for w in idx]

    def body(*refs):
        buf = refs[:n]
        send_sems, recv_sems = refs[n], refs[n + 1]
        x, y, c, chips = _place()
        me = 2 * x + y
        for j, (px, py) in enumerate(chips):
            for i in range(n):
                mine = _half(buf[i].at[me], idx[i], c)
                got = _half(buf[i].at[2 * px + py], idx[i], c)
                k = j * n_all + pos[i]
                _remote(mine, mine, send_sems.at[k], recv_sems.at[k], (px, py, c)).wait_send()
                _remote(got, got, send_sems.at[k], recv_sems.at[k], (px, py, c)).wait_recv()

    outs = pl.pallas_call(
        body, name=name, in_specs=[HBM] * n + [SEM, SEM] + [ANY] * len(after), out_specs=(HBM,) * n,
        out_shape=tuple(pltpu.HBM(w.shape, w.dtype) for w in wb),
        input_output_aliases={i: i for i in range(n)},
        compiler_params=pltpu.CompilerParams(has_side_effects=EFFECT),
    )(*wb, send_sems, recv_sems, *after)
    return list(outs)


def _gather_forward(wb, idx, name):
    n = len(idx)

    def body(*refs):
        dst = refs[n:2 * n]
        send_sems, recv_sems = refs[2 * n], refs[2 * n + 1]
        x, y, c, chips = _place()
        cps = []
        for j, (px, py) in enumerate(chips):
            for i in range(n):
                got = _half(dst[i].at[2 * px + py], idx[i], c)
                cps.append(_remote(got, got, send_sems.at[j, i], recv_sems.at[j, i], (x, y, 1 - c)))
                cps[-1].start()
        for j, (px, py) in enumerate(chips):
            for i in range(n):
                got = _half(dst[i].at[2 * px + py], idx[i], 1 - c)
                _remote(got, got, send_sems.at[j, i], recv_sems.at[j, i], (x, y, 1 - c)).wait_recv()
        for cp in cps:
            cp.wait_send()

    return pl.pallas_call(
        body, name=name, in_specs=[ANY] * n, out_specs=[ANY] * n, out_shape=[_sds(w.shape, BF16) for w in wb],
        input_output_aliases={i: i for i in range(n)},
        scratch_shapes=[pltpu.SemaphoreType.DMA((3, n)), pltpu.SemaphoreType.DMA((3, n))],
    )(*wb)


def _forward_start(wb, cw, after, idx, name):
    n = len(idx)
    m = n if cw is None else n + 2

    def body(*refs):
        buf = refs[:n]
        send_sems, recv_sems = refs[m + 1], refs[m + 2]
        token = refs[2 * m + 3]
        x, y, c, chips = _place()
        for j, (px, py) in enumerate(chips):
            for i in range(n):
                got = _half(buf[i].at[2 * px + py], idx[i], c)
                _remote(got, got, send_sems.at[j * (n + 1) + i], recv_sems.at[j * (n + 1) + i], (x, y, 1 - c)).start()
            if cw is not None:
                _remote(refs[n], refs[n + 1].at[2 * x + y], send_sems.at[j * (n + 1) + n],
                        recv_sems.at[j * (n + 1) + n], (px, py, c)).start()
        token[...] = jnp.zeros(TOKEN, F32)

    arrays = list(wb) if cw is None else list(wb) + [cw, lax.empty((4, CW_ROWS, SH_O), F32)]
    outs = pl.pallas_call(
        body, name=name, in_specs=[HBM] * m + [ANY],
        out_specs=(SEM, SEM) + (HBM,) * m + (pl.BlockSpec(memory_space=pltpu.VMEM),),
        out_shape=(pltpu.SemaphoreType.DMA((3 * (n + 1),)), pltpu.SemaphoreType.DMA((3 * (n + 1),)))
        + tuple(pltpu.HBM(a.shape, a.dtype) for a in arrays) + (_sds(TOKEN, F32),),
        input_output_aliases={i: 2 + i for i in range(m)},
        compiler_params=pltpu.CompilerParams(has_side_effects=EFFECT),
    )(*[pltpu.with_memory_space_constraint(a, pltpu.HBM) for a in arrays], after)
    return outs[0], outs[1], list(outs[2:2 + m]), outs[2 + m]


def _forward_wait(send_sems, recv_sems, arrays, after, idx, with_cw, name):
    n = len(idx)
    m = len(arrays)

    def body(*refs):
        buf = refs[:n]
        send_sems, recv_sems = refs[m], refs[m + 1]
        x, y, c, chips = _place()
        for j, (px, py) in enumerate(chips):
            for i in range(n):
                sent = _half(buf[i].at[2 * px + py], idx[i], c)
                got = _half(buf[i].at[2 * px + py], idx[i], 1 - c)
                k = j * (n + 1) + i
                _remote(sent, sent, send_sems.at[k], recv_sems.at[k], (x, y, 1 - c)).wait_send()
                _remote(got, got, send_sems.at[k], recv_sems.at[k], (x, y, 1 - c)).wait_recv()
            if with_cw:
                k = j * (n + 1) + n
                theirs = refs[n + 1].at[2 * px + py]
                _remote(refs[n], theirs, send_sems.at[k], recv_sems.at[k], (px, py, c)).wait_send()
                _remote(refs[n], theirs, send_sems.at[k], recv_sems.at[k], (px, py, c)).wait_recv()

    outs = pl.pallas_call(
        body, name=name, in_specs=[HBM] * m + [SEM, SEM] + [ANY] * len(after), out_specs=(HBM,) * m,
        out_shape=tuple(pltpu.HBM(a.shape, a.dtype) for a in arrays),
        input_output_aliases={i: i for i in range(m)},
        compiler_params=pltpu.CompilerParams(has_side_effects=EFFECT),
    )(*arrays, send_sems, recv_sems, *after)
    return list(outs)


def _pair_exchange(gs, idx, name):
    n = len(idx)

    def body(*refs):
        src, dst = refs[:n], refs[n:2 * n]
        send_sems, recv_sems = refs[2 * n:]
        x, y, c, _ = _place()
        cps = []
        for i in range(n):
            h = BIG[idx[i]][2] // 2
            cps.append(_remote(src[i].at[:, :, pl.ds((1 - c) * h, h), :], dst[i], send_sems.at[i], recv_sems.at[i],
                               (x, y, 1 - c)))
            cps[-1].start()
        for cp in cps:
            cp.wait()

    return pl.pallas_call(
        body, name=name, in_specs=[ANY] * n, out_specs=[ANY] * n,
        out_shape=[_sds((4, BIG[w][1], BIG[w][2] // 2, BIG[w][3]), BF16) for w in idx],
        scratch_shapes=[pltpu.SemaphoreType.DMA((n,)), pltpu.SemaphoreType.DMA((n,))],
    )(*gs)


def _pair_start(gs, idx, name):
    n = len(idx)

    def body(*refs):
        src, land = refs[:n], refs[n:2 * n]
        send_sems, recv_sems = refs[2 * n], refs[2 * n + 1]
        token = refs[4 * n + 2]
        x, y, c, _ = _place()
        for i in range(n):
            h = BIG[idx[i]][2] // 2
            _remote(src[i].at[:, :, pl.ds((1 - c) * h, h), :], land[i], send_sems.at[i], recv_sems.at[i],
                    (x, y, 1 - c)).start()
        token[...] = jnp.zeros(TOKEN, F32)

    lands = [lax.empty((4, BIG[w][1], BIG[w][2] // 2, BIG[w][3]), BF16) for w in idx]
    arrays = list(gs) + lands
    outs = pl.pallas_call(
        body, name=name, in_specs=[HBM] * (2 * n),
        out_specs=(SEM, SEM) + (HBM,) * (2 * n) + (pl.BlockSpec(memory_space=pltpu.VMEM),),
        out_shape=(pltpu.SemaphoreType.DMA((n,)), pltpu.SemaphoreType.DMA((n,)))
        + tuple(pltpu.HBM(a.shape, a.dtype) for a in arrays) + (_sds(TOKEN, F32),),
        input_output_aliases={i: 2 + i for i in range(2 * n)},
        compiler_params=pltpu.CompilerParams(has_side_effects=EFFECT),
    )(*[pltpu.with_memory_space_constraint(a, pltpu.HBM) for a in arrays])
    return outs[0], outs[1], list(outs[2:2 + n]), list(outs[2 + n:2 + 2 * n]), outs[2 + 2 * n]


def _pair_wait(send_sems, recv_sems, gs, lands, after, idx, name):
    n = len(idx)

    def body(*refs):
        src, land = refs[:n], refs[n:2 * n]
        send_sems, recv_sems = refs[2 * n], refs[2 * n + 1]
        x, y, c, _ = _place()
        for i in range(n):
            h = BIG[idx[i]][2] // 2
            cp = _remote(src[i].at[:, :, pl.ds((1 - c) * h, h), :], land[i], send_sems.at[i], recv_sems.at[i],
                         (x, y, 1 - c))
            cp.wait_send()
            cp.wait_recv()

    arrays = list(gs) + list(lands)
    outs = pl.pallas_call(
        body, name=name, in_specs=[HBM] * (2 * n) + [SEM, SEM] + [ANY] * len(after), out_specs=(HBM,) * (2 * n),
        out_shape=tuple(pltpu.HBM(a.shape, a.dtype) for a in arrays),
        input_output_aliases={i: i for i in range(2 * n)},
        compiler_params=pltpu.CompilerParams(has_side_effects=EFFECT),
    )(*arrays, send_sems, recv_sems, *after)
    return list(outs[:n]), list(outs[n:])


def _pair_sums(place, gs, r1s, idx, name):
    n = len(idx)
    dims = [(BIG[w][1], BIG[w][2] // 2, BIG[w][3]) for w in idx]

    def body(pref, *refs):
        for i in range(n):
            refs[2 * n + i][...] = (refs[i][...] + refs[n + i][...].astype(F32)).astype(BF16)

    mine = [pl.BlockSpec((1, k, h, cdim), lambda s, pref: (s, 0, pref[0], 0)) for k, h, cdim in dims]
    whole = [pl.BlockSpec((1, k, h, cdim), lambda s, pref: (s, 0, 0, 0)) for k, h, cdim in dims]
    grid_spec = pltpu.PrefetchScalarGridSpec(num_scalar_prefetch=1, grid=(4,), in_specs=mine + whole, out_specs=whole)
    return pl.pallas_call(
        body, name=name, grid_spec=grid_spec, out_shape=[_sds((4, k, h, cdim), BF16) for k, h, cdim in dims],
        compiler_params=_params(("parallel",)),
    )(place, *gs, *r1s)


def _chip_start(ps, idx, name):
    n = len(idx)

    def body(*refs):
        src, land = refs[:n], refs[n:2 * n]
        send_sems, recv_sems = refs[2 * n], refs[2 * n + 1]
        token = refs[4 * n + 2]
        x, y, c, chips = _place()
        for j, (px, py) in enumerate(chips):
            for i in range(n):
                _remote(src[i].at[2 * px + py], land[i].at[j], send_sems.at[j * n + i], recv_sems.at[j * n + i],
                        (px, py, c)).start()
        token[...] = jnp.zeros(TOKEN, F32)

    lands = [lax.empty((3,) + p.shape[1:], BF16) for p in ps]
    outs = pl.pallas_call(
        body, name=name, in_specs=[HBM] * (2 * n),
        out_specs=(SEM, SEM) + (HBM,) * (2 * n) + (pl.BlockSpec(memory_space=pltpu.VMEM),),
        out_shape=(pltpu.SemaphoreType.DMA((3 * n,)), pltpu.SemaphoreType.DMA((3 * n,)))
        + tuple(pltpu.HBM(a.shape, a.dtype) for a in list(ps) + lands) + (_sds(TOKEN, F32),),
        input_output_aliases={i: 2 + i for i in range(2 * n)},
        compiler_params=pltpu.CompilerParams(has_side_effects=EFFECT),
    )(*[pltpu.with_memory_space_constraint(a, pltpu.HBM) for a in list(ps) + lands])
    return outs[0], outs[1], list(outs[2:2 + n]), list(outs[2 + n:2 + 2 * n]), outs[2 + 2 * n]


def _chip_wait(send_sems, recv_sems, ps, lands, after, idx, name):
    n = len(idx)

    def body(*refs):
        src, land = refs[:n], refs[n:2 * n]
        send_sems, recv_sems = refs[2 * n], refs[2 * n + 1]
        x, y, c, chips = _place()
        for j, (px, py) in enumerate(chips):
            for i in range(n):
                cp = _remote(src[i].at[2 * px + py], land[i].at[j], send_sems.at[j * n + i], recv_sems.at[j * n + i],
                             (px, py, c))
                cp.wait_send()
                cp.wait_recv()

    arrays = list(ps) + list(lands)
    outs = pl.pallas_call(
        body, name=name, in_specs=[HBM] * (2 * n) + [SEM, SEM] + [ANY] * len(after), out_specs=(HBM,) * (2 * n),
        out_shape=tuple(pltpu.HBM(a.shape, a.dtype) for a in arrays),
        input_output_aliases={i: i for i in range(2 * n)},
        compiler_params=pltpu.CompilerParams(has_side_effects=EFFECT),
    )(*arrays, send_sems, recv_sems, *after)
    return list(outs[n:])


def _chip_sums(place, gs, r1s, r2s, idx, name):
    n = len(idx)
    dims = [(BIG[w][1], BIG[w][2] // 4, BIG[w][3]) for w in idx]

    def body(pref, *refs):
        for i in range(n):
            acc = refs[i][0] + refs[n + i][0].astype(F32)
            for j in range(3):
                acc = acc + refs[2 * n + i][j].astype(F32)
            refs[3 * n + i][...] = acc

    in_specs = ([pl.BlockSpec((1, k, q, cdim), lambda t, pref: (pref[1], 0, pref[0] * 2 + t, 0)) for k, q, cdim in dims]
                + [pl.BlockSpec((1, k, q, cdim), lambda t, pref: (pref[1], 0, t, 0)) for k, q, cdim in dims]
                + [pl.BlockSpec((3, k, q, cdim), lambda t, pref: (0, 0, t, 0)) for k, q, cdim in dims])
    out_specs = [pl.BlockSpec((k, q, cdim), lambda t, pref: (0, pref[0] * 2 + t, 0)) for k, q, cdim in dims]
    grid_spec = pltpu.PrefetchScalarGridSpec(num_scalar_prefetch=1, grid=(2,), in_specs=in_specs, out_specs=out_specs)
    return pl.pallas_call(
        body, name=name, grid_spec=grid_spec, out_shape=[_sds(BIG[w][1:], F32) for w in idx],
        compiler_params=_params(("parallel",)),
    )(place, *gs, *r1s, *r2s)


def _pair_gather(hs, idx, name):
    n = len(idx)

    def body(*refs):
        dst = refs[n:2 * n]
        send_sems, recv_sems = refs[2 * n:]
        x, y, c, _ = _place()
        cps = []
        for i in range(n):
            mine = _half(dst[i], idx[i], c)
            cps.append(_remote(mine, mine, send_sems.at[i], recv_sems.at[i], (x, y, 1 - c)))
            cps[-1].start()
        for i in range(n):
            theirs = _half(dst[i], idx[i], 1 - c)
            _remote(theirs, theirs, send_sems.at[i], recv_sems.at[i], (x, y, 1 - c)).wait_recv()
        for cp in cps:
            cp.wait_send()

    return pl.pallas_call(
        body, name=name, in_specs=[ANY] * n, out_specs=[ANY] * n,
        out_shape=[_sds(BIG[w][1:], F32) for w in idx],
        input_output_aliases={i: i for i in range(n)},
        scratch_shapes=[pltpu.SemaphoreType.DMA((n,)), pltpu.SemaphoreType.DMA((n,))],
    )(*hs)


SMALL_ROWS = 40


def _adamw_math(w, g, m, v):
    m = ADAM_B1 * m + (1.0 - ADAM_B1) * g
    v = ADAM_B2 * v + (1.0 - ADAM_B2) * (g * g)
    m_hat = m / (1.0 - ADAM_B1 ** ADAM_STEP)
    v_hat = v / (1.0 - ADAM_B2 ** ADAM_STEP)
    delta = -ADAM_LR * (m_hat / (jnp.sqrt(v_hat) + ADAM_EPS) + ADAM_WD * w)
    return delta, m, v


def _small_start(pack, after):
    def body(pack_ref, land_ref, after_ref, send_sems, recv_sems, pack_thru, land_thru, token):
        x, y, c, _ = _place()
        for r in range(1, 8):
            peer = (x if not r & 4 else 1 - x, y if not r & 2 else 1 - y, c if not r & 1 else 1 - c)
            _remote(pack_ref, land_ref.at[r - 1], send_sems.at[r - 1], recv_sems.at[r - 1], peer).start()
        token[...] = jnp.zeros(TOKEN, F32)

    land = lax.empty((7, SMALL_ROWS, D), F32)
    outs = pl.pallas_call(
        body, name="small_start", in_specs=[HBM, HBM, ANY],
        out_specs=(SEM, SEM, HBM, HBM, pl.BlockSpec(memory_space=pltpu.VMEM)),
        out_shape=(pltpu.SemaphoreType.DMA((7,)), pltpu.SemaphoreType.DMA((7,)), pltpu.HBM(pack.shape, F32),
                   pltpu.HBM(land.shape, F32), _sds(TOKEN, F32)),
        input_output_aliases={0: 2, 1: 3},
        compiler_params=pltpu.CompilerParams(has_side_effects=EFFECT),
    )(pltpu.with_memory_space_constraint(pack, pltpu.HBM), pltpu.with_memory_space_constraint(land, pltpu.HBM), after)
    return outs


def _small_wait(send_sems, recv_sems, pack, land, after):
    def body(pack_ref, land_ref, send_sems, recv_sems, *rest):
        x, y, c, _ = _place()
        for r in range(1, 8):
            peer = (x if not r & 4 else 1 - x, y if not r & 2 else 1 - y, c if not r & 1 else 1 - c)
            cp = _remote(pack_ref, land_ref.at[r - 1], send_sems.at[r - 1], recv_sems.at[r - 1], peer)
            cp.wait_send()
            cp.wait_recv()

    return pl.pallas_call(
        body, name="small_wait", in_specs=[HBM, HBM, SEM, SEM] + [ANY] * len(after), out_specs=(HBM, HBM),
        out_shape=(pltpu.HBM(pack.shape, F32), pltpu.HBM(land.shape, F32)),
        input_output_aliases={0: 0, 1: 1},
        compiler_params=pltpu.CompilerParams(has_side_effects=EFFECT),
    )(pack, land, send_sems, recv_sems, *after)


def _small_update(place, pack, land, ws, ms, vs):
    n = len(ws)

    def body(pref, pack_ref, land_ref, *refs):
        chip = pref[1]
        me = 2 * chip + pref[0]
        own = pack_ref[...]
        tot = None
        for dev in range(8):
            r = jnp.bitwise_xor(me, dev)
            term = jnp.where(r == 0, own, land_ref[jnp.maximum(r - 1, 0)])
            tot = term if tot is None else tot + term
        out, buf = refs[3 * n:-1], refs[-1]
        buf[...] = tot
        g_conv = jnp.zeros((3, SH_O), F32)
        for s in range(4):
            g_conv = g_conv + jnp.where(chip == s, buf[24:27, s * SH_O:(s + 1) * SH_O], 0.0)
        gs = [buf[0:2, :], buf[8:10, :], buf[16:17, :], g_conv]
        out[0][...] = buf[32:33, 0:128]
        for i in range(n):
            d, nm, nv = _adamw_math(refs[i][...], gs[i], refs[n + i][...], refs[2 * n + i][...])
            out[1 + i][...] = gs[i]
            out[1 + n + i][...] = d
            out[1 + 2 * n + i][...] = nm
            out[1 + 3 * n + i][...] = nv

    def full(shape):
        nd = len(shape)
        return pl.BlockSpec(shape, lambda i, pref: (0,) * nd)

    specs = [full(w.shape) for w in ws]
    grid_spec = pltpu.PrefetchScalarGridSpec(
        num_scalar_prefetch=1, grid=(1,),
        in_specs=[full(pack.shape), full(land.shape)] + specs * 3, out_specs=[full((1, 128))] + specs * 4,
        scratch_shapes=[pltpu.VMEM((SMALL_ROWS, D), F32)])
    outs = pl.pallas_call(
        body, name="small_update", grid_spec=grid_spec,
        out_shape=[_sds((1, 128), F32)] + [_sds(w.shape, F32) for w in ws] * 4,
        compiler_params=_params(("arbitrary",)),
    )(place, pack, land, *ws, *ms, *vs)
    return outs[0], outs[1:1 + n], outs[1 + n:1 + 2 * n], outs[1 + 2 * n:1 + 3 * n], outs[1 + 3 * n:]


def _adamw_layer(ws, gs, ms, vs, idx, name):
    n = len(idx)
    dims = [(BIG[w][1], BIG[w][2] // 4, BIG[w][3]) for w in idx]

    def body(*refs):
        for i in range(n):
            gv = refs[n + i][...]
            d, nm, nv = _adamw_math(refs[i][...], gv, refs[2 * n + i][...], refs[3 * n + i][...])
            refs[4 * n + i][...] = d
            refs[5 * n + i][...] = nm
            refs[6 * n + i][...] = nv
            refs[7 * n + i][...] = gv

    specs = [pl.BlockSpec((k, q, cdim), lambda t: (0, t, 0)) for k, q, cdim in dims]
    outs = pl.pallas_call(
        body, name=name, grid=(4,), in_specs=specs * 4, out_specs=specs * 4,
        out_shape=[_sds(BIG[w][1:], F32) for w in idx] * 4,
        compiler_params=_params(("parallel",)),
    )(*ws, *gs, *ms, *vs)
    return [tuple(outs[j * n + i] for j in range(4)) for i in range(n)]


def _pad_rows(a, rows):
    return jnp.pad(a, ((0, rows - a.shape[0]), (0, 0)))


def kernel(x, mem, positions, norm_g, mem_norm_g, w_mem_kv, attn_w_in, attn_w_out, conv_w_in, conv_w, conv_w_out, final_g, loss_target, m_norm_g, m_mem_norm_g, m_w_mem_kv, m_attn_w_in, m_attn_w_out, m_conv_w_in, m_conv_w, m_conv_w_out, m_final_g, v_norm_g, v_mem_norm_g, v_w_mem_kv, v_attn_w_in, v_attn_w_out, v_conv_w_in, v_conv_w, v_conv_w_out, v_final_g):
    mx, my, mc = lax.axis_index("x"), lax.axis_index("y"), lax.axis_index("c")
    place = jnp.stack([mc, 2 * mx + my]).astype(jnp.int32)

    w_big = [w_mem_kv, attn_w_in, attn_w_out, conv_w_in, conv_w_out]
    m_big = [m_w_mem_kv, m_attn_w_in, m_attn_w_out, m_conv_w_in, m_conv_w_out]
    v_big = [v_w_mem_kv, v_attn_w_in, v_attn_w_out, v_conv_w_in, v_conv_w_out]
    first, rest = (1,), (0, 2, 3, 4)
    wb1 = _cast_weights(place, [w_big[i] for i in first], place, first, "cast_w_in_a")
    a1_send, a1_recv, a1_bufs, a1_token = _gather_start(wb1, place, first, "gather_a1_start")
    wbr = _cast_weights(place, [w_big[i] for i in rest], a1_token, rest, "cast_weights")
    r_send, r_recv, r_bufs, gb_token = _gather_start(wbr, a1_token, rest, "gather_rest_start")
    a2_send, a2_recv, gb_send, gb_recv = r_send, r_recv, r_send, r_recv
    a2_bufs, gb_bufs = r_bufs[:2], r_bufs[2:]
    started, rest = rest, (0, 2)

    xs, tgt = x[0], loss_target[0]
    g0, g1 = norm_g[0:1], norm_g[1:2]
    rc, rs1, rs2 = _rope_tables(positions[0].astype(F32).reshape(S, 1), gb_token)
    a1_bufs = _gather_wait(a1_send, a1_recv, a1_bufs, [rc], first, "gather_a1_wait")
    w_in_a = _gather_forward(a1_bufs, first, "gather_a1_forward")[0].reshape(4, D, SH_A)
    hn0, q, k, v, qm0, z0 = _in_proj_a(xs, g0, w_in_a, rc, rs1, rs2, gb_token)
    a2_bufs = _gather_wait(a2_send, a2_recv, a2_bufs, [q], rest, "gather_a2_wait", started)
    f2_send, f2_recv, a2_bufs, f2_token = _forward_start(a2_bufs, None, q, rest, "forward_a2_start")
    fwd = [_attn_fwd(q, k, v, 0, f2_token)]
    fwd.append(_attn_fwd(q, k, v, 1, fwd[0][0]))
    cw_own = _pad_rows(conv_w[0], CW_ROWS)
    gb_bufs = _gather_wait(gb_send, gb_recv, gb_bufs, [fwd[1][0]], LAYER_B, "gather_b_wait", started)
    fb_send, fb_recv, gb_bufs, fb_token = _forward_start(gb_bufs, cw_own, fwd[1][0], LAYER_B, "forward_b_start")
    fwd.append(_attn_fwd(q, k, v, 2, fb_token))
    os_, ls, lss = [f[0] for f in fwd], [f[1] for f in fwd], [f[2] for f in fwd]
    wkv_f, w_out_a = _forward_wait(f2_send, f2_recv, a2_bufs, [os_[2]], rest, False, "forward_a2_wait")
    w_out_a = w_out_a.reshape(4, BR_A, SH_O)
    memn, kv = _mem_fwd(mem[0], mem_norm_g, wkv_f)
    h1 = _attn_out(os_, ls, qm0, kv[0], z0, xs, w_out_a)

    w_in_b, w_out_b, _, cw_f = _forward_wait(fb_send, fb_recv, gb_bufs, [h1], LAYER_B, True, "forward_b_wait")
    w_in_b = w_in_b.reshape(4, D, SH_B)
    w_out_b = w_out_b.reshape(BR_B, D)
    cw_f = lax.dynamic_update_slice(cw_f, cw_own[None], (2 * mx + my, 0, 0))
    cw8 = cw_f.transpose(1, 0, 2).reshape(CW_ROWS, D)
    hn1, bg, cg, u, qm1, z1 = _in_proj_b(h1, g1, w_in_b)
    dh2, loss_part, dfg = _conv_out_loss(bg, cg, u, cw8, qm1, kv[1], z1, h1, w_out_b, final_g.reshape(1, D), tgt)

    dproj_b, dw_out_b, dcw, dkv1, dw_out_b16 = _conv_bwd(dh2, bg, cg, u, cw8, qm1, kv[1], z1, w_out_b)
    dw_in_b, dw_in_b16 = _w_in_grad(hn1, dproj_b, IN_B, "w_in_b_grad")
    gs_b = [dw_in_b.reshape(4, 1, D, SH_B), dw_out_b.reshape(4, 1, BR_B // 4, D)]
    gb_b = [dw_in_b16.reshape(4, 1, D, SH_B), dw_out_b16.reshape(4, 1, BR_B // 4, D)]
    pb_send, pb_recv, gb_b, pb_land, pb_token = _pair_start(gb_b, LAYER_B, "pair_b_start")
    dh1, dg1 = _in_proj_bwd(dproj_b, w_in_b, h1, g1, dh2, pb_token, IN_B, "in_proj_b_bwd")
    _, r1_b = _pair_wait(pb_send, pb_recv, gb_b, pb_land, [dh1], LAYER_B, "pair_b_wait")
    ps_b = _pair_sums(place, gs_b, r1_b, LAYER_B, "pair_sums_b")
    cb_send, cb_recv, cb_src, cb_land, cb_token = _chip_start(ps_b, LAYER_B, "chip_b_start")

    outs = _attn_out_bwd(dh1, os_, ls, qm0, kv[0], z0, w_out_a, cb_token)
    dos, dds, dqm, dz, dw_out_a, dkv0, dw_out_a16 = outs[0:3], outs[3:6], outs[6], outs[7], outs[8], outs[9], outs[10]
    bwd = [_attn_bwd(q, k, v, dos[g], lss[g], dds[g], g) for g in range(3)]
    dproj_a = _qkv_bwd([b[0] for b in bwd], [b[1] for b in bwd], [b[2] for b in bwd], dqm, dz, rc, rs1, rs2)
    dw_in_a, dw_in_a16 = _w_in_grad(hn0, dproj_a, IN_A, "w_in_a_grad")
    dwkv, dwkv16, dmg = _mem_bwd(mem[0], mem_norm_g, memn, wkv_f, dkv0, dkv1)

    gs_a = [dwkv, dw_in_a.reshape(4, 1, D, SH_A), dw_out_a.reshape(4, 1, BR_A, SH_O)]
    r1_a = _pair_exchange([dwkv16, dw_in_a16.reshape(4, 1, D, SH_A), dw_out_a16.reshape(4, 1, BR_A, SH_O)], LAYER_A,
                          "pair_exchange_a")
    ps_a = _pair_sums(place, gs_a, r1_a, LAYER_A, "pair_sums_a")
    ca_send, ca_recv, ca_src, ca_land, ca_token = _chip_start(ps_a, LAYER_A, "chip_a_start")

    gx, dg0 = _in_proj_bwd(dproj_a, w_in_a, xs, g0, dh1, ca_token, IN_A, "in_proj_a_bwd")
    pack = jnp.concatenate([_pad_rows(jnp.concatenate([dg0, dg1], axis=0), 8), _pad_rows(dmg, 8), _pad_rows(dfg, 8),
                            dcw, _pad_rows(jnp.pad(loss_part, ((0, 0), (0, D - 128))), 8)], axis=0)
    sm_send, sm_recv, pack, sm_land, sm_token = _small_start(pack, ca_token)
    r2_b = _chip_wait(cb_send, cb_recv, cb_src, cb_land, [ca_token], LAYER_B, "chip_b_wait")
    hs_b = _chip_sums(place, gs_b, r1_b, r2_b, LAYER_B, "chip_sums_b")
    g_b = _pair_gather(hs_b, LAYER_B, "pair_gather_b")
    upd_b = _adamw_layer([w_big[w] for w in LAYER_B], g_b, [m_big[w] for w in LAYER_B], [v_big[w] for w in LAYER_B],
                         LAYER_B, "adamw_b")
    r2_a = _chip_wait(ca_send, ca_recv, ca_src, ca_land, [gx, upd_b[0][0], upd_b[1][0], sm_token], LAYER_A,
                      "chip_a_wait")
    hs_a = _chip_sums(place, gs_a, r1_a, r2_a, LAYER_A, "chip_sums_a")
    g_a = _pair_gather(hs_a, LAYER_A, "pair_gather_a")
    upd_a = _adamw_layer([w_big[w] for w in LAYER_A], g_a, [m_big[w] for w in LAYER_A], [v_big[w] for w in LAYER_A],
                         LAYER_A, "adamw_a")
    upd = upd_a + upd_b
    g_big = [u[3] for u in upd]
    pack, sm_land = _small_wait(sm_send, sm_recv, pack, sm_land, [r2_a[0]])
    sw = [norm_g, mem_norm_g, final_g.reshape(1, D), conv_w[0]]
    sm = [m_norm_g, m_mem_norm_g, m_final_g.reshape(1, D), m_conv_w[0]]
    sv = [v_norm_g, v_mem_norm_g, v_final_g.reshape(1, D), v_conv_w[0]]
    loss_row, sg, sd, snm, snv = _small_update(place, pack, sm_land, sw, sm, sv)
    loss = loss_row[0, 0]
    g_norm, g_memnorm, g_final, g_conv = sg

    def order(norm, memnorm, wkv, w_in_a, w_out_a, w_in_b, conv, w_out_b, final):
        return (norm, memnorm, wkv, w_in_a, w_out_a, w_in_b, conv.reshape(1, 3, SH_O), w_out_b, final.reshape(D))

    grads = order(g_norm, g_memnorm, g_big[0], g_big[1], g_big[2], g_big[3], g_conv, g_big[4], g_final)
    deltas = order(sd[0], sd[1], upd[0][0], upd[1][0], upd[2][0], upd[3][0], sd[3], upd[4][0], sd[2])
    new_m = order(snm[0], snm[1], upd[0][1], upd[1][1], upd[2][1], upd[3][1], snm[3], upd[4][1], snm[2])
    new_v = order(snv[0], snv[1], upd[0][2], upd[1][2], upd[2][2], upd[3][2], snv[3], upd[4][2], snv[2])
    return (loss, gx[None], *grads, *deltas, *new_m, *new_v)
```

```python
import functools

import numpy as np
import jax
import jax.numpy as jnp
from jax import lax
from jax.experimental import pallas as pl
from jax.experimental.pallas import tpu as pltpu

F32 = jnp.float32
BF16 = jnp.bfloat16

S = 2048
D = 1024
TM = 256
NT = S // TM
HD = 64
GW = 512
NQ = 3 * GW
MW = 256
NM = 256
IN_A = 3 * NQ + MW + GW + MW
IN_B = 3 * D + MW + D + MW
BR_A = GW + MW
BR_B = D + MW
SH_A = IN_A // 4
SH_B = IN_B // 4
SH_O = D // 4
QBLK = 128
DILATIONS = (1, 4, 16)
EPS = 1e-6
SCALE = HD ** -0.5
NEG = -1e30
ROPE_THETA = 500000.0

ADAM_LR = 0.001
ADAM_B1 = 0.9
ADAM_B2 = 0.999
ADAM_EPS = 1e-08
ADAM_WD = 0.01
ADAM_STEP = 10

VMEM_LIMIT_BYTES = 60 * 1024 * 1024


def _params(sem=None):
    if sem is None:
        return pltpu.CompilerParams(vmem_limit_bytes=VMEM_LIMIT_BYTES)
    return pltpu.CompilerParams(dimension_semantics=sem, vmem_limit_bytes=VMEM_LIMIT_BYTES)


def _full(shape):
    nd = len(shape)
    return pl.BlockSpec(shape, lambda *_: (0,) * nd)


def _rows(width, tm=TM):
    return pl.BlockSpec((tm, width), lambda i: (i, 0))


def _sds(shape, dtype):
    return jax.ShapeDtypeStruct(shape, dtype)


def _silu_parts(z):
    sig = 0.5 * jnp.tanh(0.5 * z) + 0.5
    return z * sig, sig * (1.0 + z * (1.0 - sig))


def _dot(a, b):
    return jnp.dot(a, b, preferred_element_type=F32)


def _dot_nt(a, b):
    return lax.dot_general(a, b, (((1,), (1,)), ((), ())), preferred_element_type=F32)


def _dot_tn(a, b):
    return lax.dot_general(a, b, (((0,), (0,)), ((), ())), preferred_element_type=F32)


def _rope_fwd(t, c, s1, s2):
    return t * c + pltpu.roll(t, 120, 1) * s1 + pltpu.roll(t, 8, 1) * s2


def _rope_bwd(g, c, s1, s2):
    return g * c + pltpu.roll(g * s1, 8, 1) + pltpu.roll(g * s2, 120, 1)


MEM_HEADS = MW // HD


def _stack_heads(x):
    head = lax.broadcasted_iota(jnp.int32, x.shape, 1) // HD
    return jnp.concatenate([jnp.where(head == h, x, 0.0) for h in range(MEM_HEADS)], axis=0).astype(BF16)


def _unstack_heads(x4):
    tm = x4.shape[0] // MEM_HEADS
    head = lax.broadcasted_iota(jnp.int32, (tm, MW), 1) // HD
    out = x4[:tm]
    for h in range(1, MEM_HEADS):
        out = jnp.where(head == h, x4[h * tm:(h + 1) * tm], out)
    return out


def _mem_attn(qm, kv):
    q4 = _stack_heads(qm.astype(F32))
    s = _dot_nt(q4, kv[:, :MW]) * SCALE
    e = jnp.exp(s - jnp.max(s, axis=-1, keepdims=True))
    p = e * (1.0 / jnp.sum(e, axis=-1, keepdims=True))
    return p, _unstack_heads(_dot(p.astype(BF16), kv[:, MW:])), q4


def _mem_attn_bwd(dmo, p, mo, q4, kv, dkv_ref):
    tm = dmo.shape[0]
    head = lax.broadcasted_iota(jnp.int32, dmo.shape, 1) // HD
    prod = dmo * mo
    delta = jnp.concatenate([jnp.sum(jnp.where(head == h, prod, 0.0), axis=-1, keepdims=True)
                             for h in range(MEM_HEADS)], axis=0)
    d4 = _stack_heads(dmo)
    ds = (p * (_dot_nt(d4, kv[:, MW:]) - delta) * SCALE).astype(BF16)
    dkv_ref[:, :MW] += _dot_tn(ds, q4)
    dkv_ref[:, MW:] += _dot_tn(p.astype(BF16), d4)
    return _unstack_heads(_dot(ds, kv[:, :MW]))


def _merge(o_refs, l_refs):
    ls = [r[...] for r in l_refs]
    m = jnp.maximum(jnp.maximum(ls[0], ls[1]), ls[2])
    es = [jnp.exp(l - m) for l in ls]
    inv = 1.0 / (es[0] + es[1] + es[2])
    ws = [e * inv for e in es]
    os_ = [r[...] for r in o_refs]
    mix = ws[0] * os_[0] + ws[1] * os_[1] + ws[2] * os_[2]
    return ws, mix


def _conv_taps(cg, u, cgp, up, first):
    a = cg * u
    ap = jnp.where(first, 0.0, cgp * up)
    row = lax.broadcasted_iota(jnp.int32, a.shape, 0)
    a1 = jnp.where(row == 0, ap[7:8, :], pltpu.roll(a, 1, 0))
    a2 = jnp.where(row == 0, ap[6:7, :], jnp.where(row == 1, ap[7:8, :], pltpu.roll(a, 2, 0)))
    return a, a1, a2


def _rope_tables(posf, after):
    half = 8
    invf = np.float32(ROPE_THETA) ** (-np.arange(half, dtype=np.float32) * np.float32(2.0 / 16))
    lane = np.arange(128)
    table = np.where((lane % HD) < 16, invf[lane % half], 0.0).astype(np.float32)[None, :]

    def body(pos_ref, invf_ref, c_ref, s1_ref, s2_ref):
        ang = pos_ref[...] * invf_ref[...]
        jm = lax.broadcasted_iota(jnp.int32, ang.shape, 1) & (HD - 1)
        cs = jnp.cos(ang)
        sn = jnp.sin(ang)
        c_ref[...] = jnp.where(jm < 16, cs, 1.0)
        s1_ref[...] = jnp.where(jm < 8, -sn, 0.0)
        s2_ref[...] = jnp.where((jm >= 8) & (jm < 16), sn, 0.0)

    out = _sds((S, 128), F32)
    return pl.pallas_call(
        functools.partial(_skip_arg, body, 2), name="rope_tables", grid=(NT,),
        in_specs=[_rows(1), _full((1, 128)), pl.BlockSpec(memory_space=pl.ANY)],
        out_specs=[_rows(128)] * 3, out_shape=[out] * 3,
        compiler_params=_params(("parallel",)),
    )(posf, jnp.asarray(table), after)


def _in_proj_a(x, g0, w_in, c, s1, s2, after):
    def body(x_ref, g_ref, w_ref, c_ref, s1_ref, s2_ref, hn_ref, q_ref, k_ref, v_ref, qm_ref, z_ref, proj):
        xf = x_ref[...]
        hn = xf * lax.rsqrt(jnp.mean(xf * xf, axis=-1, keepdims=True) + EPS) * g_ref[...]
        hb = hn.astype(BF16)
        hn_ref[...] = hb
        for s in range(4):
            proj[:, s * SH_A:(s + 1) * SH_A] = _dot(hb, w_ref[s])
        cc, a1, a2 = c_ref[...], s1_ref[...], s2_ref[...]
        for j in range(NQ // 128):
            q_ref[:, j * 128:(j + 1) * 128] = (
                _rope_fwd(proj[:, j * 128:(j + 1) * 128], cc, a1, a2) * SCALE).astype(BF16)
            k_ref[:, j * 128:(j + 1) * 128] = _rope_fwd(
                proj[:, NQ + j * 128:NQ + (j + 1) * 128], cc, a1, a2).astype(BF16)
        v_ref[...] = proj[:, 2 * NQ:3 * NQ].astype(BF16)
        qm_ref[...] = proj[:, 3 * NQ:3 * NQ + MW].astype(BF16)
        z_ref[...] = proj[:, 3 * NQ + MW:]

    return pl.pallas_call(
        functools.partial(_skip_arg, body, 6), name="in_proj_a", grid=(NT,),
        in_specs=[_rows(D), _full((1, D)), _full((4, D, SH_A)), _rows(128), _rows(128), _rows(128),
                  pl.BlockSpec(memory_space=pl.ANY)],
        out_specs=[_rows(D), _rows(NQ), _rows(NQ), _rows(NQ), _rows(MW), _rows(BR_A)],
        out_shape=[_sds((S, D), BF16), _sds((S, NQ), BF16), _sds((S, NQ), BF16), _sds((S, NQ), BF16),
                   _sds((S, MW), BF16), _sds((S, BR_A), F32)],
        scratch_shapes=[pltpu.VMEM((TM, IN_A), F32)],
        compiler_params=_params(("parallel",)),
    )(x, g0, w_in, c, s1, s2, after)


def _mem_fwd(mem, mg, wkv):
    def body(mem_ref, mg_ref, w_ref, memn_ref, kv_ref):
        mf = mem_ref[...]
        n = mf * lax.rsqrt(jnp.mean(mf * mf, axis=-1, keepdims=True) + EPS)
        for i in range(2):
            mn = (n * mg_ref[i:i + 1, :]).astype(BF16)
            memn_ref[i] = mn
            acc = _dot(mn[:, 0:NM], w_ref[0, i])
            for s in range(1, 4):
                acc += _dot(mn[:, s * NM:(s + 1) * NM], w_ref[s, i])
            kv_ref[i] = acc.astype(BF16)

    return pl.pallas_call(
        body, name="mem_fwd", grid=(1,),
        in_specs=[_full((NM, D)), _full((2, D)), _full((4, 2, NM, 2 * MW))],
        out_specs=[_full((2, NM, D)), _full((2, NM, 2 * MW))],
        out_shape=[_sds((2, NM, D), BF16), _sds((2, NM, 2 * MW), BF16)],
        compiler_params=_params(("arbitrary",)),
    )(mem, mg, wkv)


def _band_mask(j):
    qi = lax.broadcasted_iota(jnp.int32, (QBLK, 2 * QBLK), 0)
    kj = lax.broadcasted_iota(jnp.int32, (QBLK, 2 * QBLK), 1)
    dist = qi + QBLK - kj
    return (dist >= 0) & (dist <= QBLK) & ((kj >= QBLK) | (j > 0))


LANES = 128
NCHUNK = GW // LANES
FWD_UNROLL = 16
BWD_UNROLL = 8
CONV_CHUNK = 256


def _perm_matrix(d):
    n = TM // d
    p = np.zeros((TM, TM), np.float32)
    for r in range(d):
        for i in range(n):
            p[r * n + i, i * d + r] = 1.0
    return p


def _split_dot(p, x, parts):
    hi = x.astype(BF16)
    rem = x - hi.astype(F32)
    lo = rem.astype(BF16)
    both = _dot(p, jnp.concatenate([hi, lo], axis=1))
    acc = both[:, :LANES] + both[:, LANES:]
    if parts == 3:
        acc = acc + _dot(p, (rem - lo.astype(F32)).astype(BF16))
    return acc


def _pair_dot(p, a, b):
    both = _dot(p, jnp.concatenate([a, b], axis=1))
    return both[:, :LANES], both[:, LANES:]


def _tile_to_streams(y, dst, t, d):
    n, ln = TM // d, S // d
    for r in range(d):
        dst[r * ln + t * n:r * ln + (t + 1) * n, :] = y[r * n:(r + 1) * n].astype(dst.dtype)


def _tile_from_streams(src, t, d):
    n, ln = TM // d, S // d
    return jnp.concatenate([src[r * ln + t * n:r * ln + (t + 1) * n, :] for r in range(d)], axis=0)


def _head_masks():
    first = lax.broadcasted_iota(jnp.int32, (TM, LANES), 1) < HD
    return first, jnp.logical_not(first)


def _attn_fwd(q, k, v, g, after):
    d = DILATIONS[g]
    nb = S // d // QBLK
    perm = _perm_matrix(d)

    def body(q_ref, k_ref, v_ref, p_ref, pt_ref, o_ref, l_ref, ls_ref, q0, q1, ks, vs, os_):
        first, second = _head_masks()
        pm = p_ref[...]
        for t in range(NT):
            rows = slice(t * TM, (t + 1) * TM)
            if d == 1:
                qt = q_ref[rows, :].astype(F32)
            else:
                qt, kt = _pair_dot(pm, q_ref[rows, :], k_ref[rows, :])
                _tile_to_streams(kt, ks, t, d)
                _tile_to_streams(_dot(pm, v_ref[rows, :]), vs, t, d)
            _tile_to_streams(jnp.where(first, qt, 0.0), q0, t, d)
            _tile_to_streams(jnp.where(second, qt, 0.0), q1, t, d)
        kref, vref = (k_ref, v_ref) if d == 1 else (ks, vs)
        oref, lref = (o_ref, l_ref) if d == 1 else (os_, ls_ref)

        def blk(b, carry):
            r0 = pl.multiple_of(b * QBLK, QBLK)
            p0 = pl.multiple_of(jnp.maximum(b - 1, 0) * QBLK, QBLK)
            kk = jnp.concatenate([kref[pl.ds(p0, QBLK), :], kref[pl.ds(r0, QBLK), :]], axis=0)
            vv = jnp.concatenate([vref[pl.ds(p0, QBLK), :], vref[pl.ds(r0, QBLK), :]], axis=0)
            valid = _band_mask(b & (nb - 1))
            acc, lse = [], []
            for qh in (q0, q1):
                s = jnp.where(valid, _dot_nt(qh[pl.ds(r0, QBLK), :], kk), NEG)
                m = jnp.max(s, axis=-1, keepdims=True)
                e = jnp.exp(s - m)
                l = jnp.sum(e, axis=-1, keepdims=True)
                acc.append(_dot(e.astype(BF16), vv) * (1.0 / l))
                lse.append(m + jnp.log(l))
            f = first[:QBLK]
            oref[pl.ds(r0, QBLK), :] = jnp.where(f, acc[0], acc[1])
            lref[pl.ds(r0, QBLK), :] = jnp.where(f, lse[0], lse[1])
            return carry

        lax.fori_loop(0, S // QBLK, blk, 0, unroll=FWD_UNROLL)
        if d > 1:
            ptm = pt_ref[...]
            for t in range(NT):
                rows = slice(t * TM, (t + 1) * TM)
                o_ref[rows, :] = _split_dot(ptm, _tile_from_streams(os_, t, d), 2)
                l_ref[rows, :] = _split_dot(ptm, _tile_from_streams(ls_ref, t, d), 3)

    qkv_spec = pl.BlockSpec((S, LANES), lambda c: (0, g * NCHUNK + c))
    out_spec = pl.BlockSpec((S, LANES), lambda c: (0, c))
    n_out = 2 if d == 1 else 3
    inner = body if d > 1 else functools.partial(_drop_arg, body, 7)
    outs = pl.pallas_call(
        functools.partial(_skip_arg, inner, 5), name=f"attn_fwd_g{g}", grid=(NCHUNK,),
        in_specs=[qkv_spec] * 3 + [_full((TM, TM))] * 2 + [pl.BlockSpec(memory_space=pl.ANY)],
        out_specs=[out_spec] * n_out, out_shape=[_sds((S, GW), F32)] * n_out,
        scratch_shapes=[pltpu.VMEM((S, LANES), BF16)] * 4 + [pltpu.VMEM((S, LANES), F32)],
        compiler_params=_params(("parallel",)),
    )(q, k, v, jnp.asarray(perm, BF16), jnp.asarray(perm.T, BF16), after)
    return (outs[0], outs[1], outs[1]) if d == 1 else tuple(outs)


def _drop_arg(body, pos, *refs):
    return body(*refs[:pos], None, *refs[pos:])


def _attn_out(os_, ls, qm, kv0, z, x, w_out):
    def body(o0, o1, o2, l0, l1, l2, qm_ref, kv_ref, z_ref, x_ref, w_ref, h_ref, ybuf):
        _, mix = _merge((o0, o1, o2), (l0, l1, l2))
        sz, _ = _silu_parts(z_ref[...])
        ybuf[:, :GW] = (mix * sz[:, :GW]).astype(BF16)
        _, mo, _ = _mem_attn(qm_ref[...], kv_ref[...])
        ybuf[:, GW:] = (mo * sz[:, GW:]).astype(BF16)
        yb = ybuf[...]
        for s in range(4):
            cs = slice(s * SH_O, (s + 1) * SH_O)
            h_ref[:, cs] = x_ref[:, cs] + _dot(yb, w_ref[s])

    return pl.pallas_call(
        body, name="attn_out", grid=(NT,),
        in_specs=[_rows(GW)] * 6 + [_rows(MW), _full((NM, 2 * MW)), _rows(BR_A), _rows(D), _full((4, BR_A, SH_O))],
        out_specs=_rows(D), out_shape=_sds((S, D), F32),
        scratch_shapes=[pltpu.VMEM((TM, BR_A), BF16)],
        compiler_params=_params(("parallel",)),
    )(*os_, *ls, qm, kv0, z, x, w_out)


def _in_proj_b(h1, g1, w_in):
    def body(x_ref, g_ref, w_ref, hn_ref, bg_ref, cg_ref, u_ref, qm_ref, z_ref, proj):
        xf = x_ref[...]
        hn = xf * lax.rsqrt(jnp.mean(xf * xf, axis=-1, keepdims=True) + EPS) * g_ref[...]
        hb = hn.astype(BF16)
        hn_ref[...] = hb
        for s in range(4):
            proj[:, s * SH_B:(s + 1) * SH_B] = _dot(hb, w_ref[s])
        bg_ref[...] = proj[:, :D]
        cg_ref[...] = proj[:, D:2 * D]
        u_ref[...] = proj[:, 2 * D:3 * D]
        qm_ref[...] = proj[:, 3 * D:3 * D + MW].astype(BF16)
        z_ref[...] = proj[:, 3 * D + MW:]

    return pl.pallas_call(
        body, name="in_proj_b", grid=(NT,),
        in_specs=[_rows(D), _full((1, D)), _full((4, D, SH_B))],
        out_specs=[_rows(D), _rows(D), _rows(D), _rows(D), _rows(MW), _rows(BR_B)],
        out_shape=[_sds((S, D), BF16), _sds((S, D), F32), _sds((S, D), F32), _sds((S, D), F32),
                   _sds((S, MW), BF16), _sds((S, BR_B), F32)],
        scratch_shapes=[pltpu.VMEM((TM, IN_B), F32)],
        compiler_params=_params(("parallel",)),
    )(h1, g1, w_in)


def _prev8(width):
    return pl.BlockSpec((8, width), lambda i: (jnp.maximum(i * (TM // 8) - 1, 0), 0))


def _conv_out_loss(bg, cg, u, cw, qm, kv1, z, h1, w_out, fg, tgt):
    def body(bg_ref, cg_ref, u_ref, cgp_ref, up_ref, cw_ref, qm_ref, kv_ref, z_ref, h_ref, w_ref, fg_ref, t_ref,
             dh_ref, loss_ref, dfg_ref, ybuf):
        i = pl.program_id(0)
        a, a1, a2 = _conv_taps(cg_ref[...], u_ref[...], cgp_ref[...], up_ref[...], i == 0)
        conv = cw_ref[0:1, :] * a2 + cw_ref[1:2, :] * a1 + cw_ref[2:3, :] * a
        sz, _ = _silu_parts(z_ref[...])
        ybuf[:, :D] = (bg_ref[...] * conv * sz[:, :D]).astype(BF16)
        _, mo, _ = _mem_attn(qm_ref[...], kv_ref[...])
        ybuf[:, D:] = (mo * sz[:, D:]).astype(BF16)
        h2 = h_ref[...] + _dot(ybuf[...], w_ref[...])
        rstd = lax.rsqrt(jnp.mean(h2 * h2, axis=-1, keepdims=True) + EPS)
        n = h2 * rstd
        fgv = fg_ref[...]
        err = n * fgv - t_ref[...]
        dout = err * (1.0 / D)
        dn = dout * fgv
        dh_ref[...] = rstd * (dn - n * jnp.mean(dn * n, axis=-1, keepdims=True))

        @pl.when(i == 0)
        def _():
            loss_ref[...] = jnp.zeros_like(loss_ref)
            dfg_ref[...] = jnp.zeros_like(dfg_ref)

        loss_ref[...] += jnp.sum(err * err) * (0.5 / D)
        dfg_ref[...] += jnp.sum(dout * n, axis=0, keepdims=True)

    return pl.pallas_call(
        body, name="conv_out_loss", grid=(NT,),
        in_specs=[_rows(D), _rows(D), _rows(D), _prev8(D), _prev8(D), _full((8, D)), _rows(MW),
                  _full((NM, 2 * MW)), _rows(BR_B), _rows(D), _full((BR_B, D)), _full((1, D)), _rows(D)],
        out_specs=[_rows(D), _full((1, 128)), _full((1, D))],
        out_shape=[_sds((S, D), F32), _sds((1, 128), F32), _sds((1, D), F32)],
        scratch_shapes=[pltpu.VMEM((TM, BR_B), BF16)],
        compiler_params=_params(("arbitrary",)),
    )(bg, cg, u, cg, u, cw, qm, kv1, z, h1, w_out, fg, tgt)


def _conv_bwd(dh2, bg, cg, u, cw, qm, kv1, z, w_out):
    rev = lambda i: (NT - 1 - i, 0)
    rows = lambda w: pl.BlockSpec((TM, w), rev)
    prev8 = pl.BlockSpec((8, D), lambda i: (jnp.maximum((NT - 1 - i) * (TM // 8) - 1, 0), 0))

    def body(dh_ref, bg_ref, cg_ref, u_ref, cgp_ref, up_ref, cw_ref, qm_ref, kv_ref, z_ref, w_ref,
             dproj_ref, dw_ref, dcw_ref, dkv_ref, dwb_ref, ybuf, carry):
        i = pl.program_id(0)

        @pl.when(i == 0)
        def _():
            dw_ref[...] = jnp.zeros_like(dw_ref)
            dcw_ref[...] = jnp.zeros_like(dcw_ref)
            dkv_ref[...] = jnp.zeros_like(dkv_ref)
            carry[...] = jnp.zeros_like(carry)

        dhb = dh_ref[...].astype(BF16)
        dy = _dot_nt(dhb, w_ref[...])
        kvv = kv_ref[...]
        p, mo, q4 = _mem_attn(qm_ref[...], kvv)
        szm, dszm = _silu_parts(z_ref[:, D:])
        ybuf[:, D:] = (mo * szm).astype(BF16)
        dym = dy[:, D:]
        dproj_ref[:, 3 * D + MW + D:] = (dym * mo * dszm).astype(BF16)
        first_tile = i == NT - 1
        for c in range(D // CONV_CHUNK):
            cs = slice(c * CONV_CHUNK, (c + 1) * CONV_CHUNK)
            bgv, cgv, uv = bg_ref[:, cs], cg_ref[:, cs], u_ref[:, cs]
            a, a1, a2 = _conv_taps(cgv, uv, cgp_ref[:, cs], up_ref[:, cs], first_tile)
            w0, w1, w2 = cw_ref[0:1, cs], cw_ref[1:2, cs], cw_ref[2:3, cs]
            conv = w0 * a2 + w1 * a1 + w2 * a
            mix = bgv * conv
            sz, dsz = _silu_parts(z_ref[:, cs])
            ybuf[:, cs] = (mix * sz).astype(BF16)
            dyc = dy[:, cs]
            dproj_ref[:, 3 * D + MW + c * CONV_CHUNK:3 * D + MW + (c + 1) * CONV_CHUNK] = (
                dyc * mix * dsz).astype(BF16)
            dmix = dyc * sz
            dproj_ref[:, cs] = (dmix * conv).astype(BF16)
            dc = dmix * bgv
            nxt = carry[:, cs]
            row = lax.broadcasted_iota(jnp.int32, dc.shape, 0)
            dc1 = jnp.where(row == TM - 1, nxt[0:1, :], pltpu.roll(dc, TM - 1, 0))
            dc2 = jnp.where(row == TM - 2, nxt[0:1, :],
                            jnp.where(row == TM - 1, nxt[1:2, :], pltpu.roll(dc, TM - 2, 0)))
            carry[:, cs] = dc[0:8, :]
            da = w2 * dc + w1 * dc1 + w0 * dc2
            dproj_ref[:, D + c * CONV_CHUNK:D + (c + 1) * CONV_CHUNK] = (da * uv).astype(BF16)
            dproj_ref[:, 2 * D + c * CONV_CHUNK:2 * D + (c + 1) * CONV_CHUNK] = (da * cgv).astype(BF16)
            dcw_ref[0:1, cs] += jnp.sum(dc * a2, axis=0, keepdims=True)
            dcw_ref[1:2, cs] += jnp.sum(dc * a1, axis=0, keepdims=True)
            dcw_ref[2:3, cs] += jnp.sum(dc * a, axis=0, keepdims=True)
        dw_ref[...] += _dot_tn(ybuf[...], dhb)
        dproj_ref[:, 3 * D:3 * D + MW] = _mem_attn_bwd(dym * szm, p, mo, q4, kvv, dkv_ref).astype(BF16)

        @pl.when(i == NT - 1)
        def _():
            dwb_ref[...] = dw_ref[...].astype(BF16)

    return pl.pallas_call(
        body, name="conv_bwd", grid=(NT,),
        in_specs=[rows(D), rows(D), rows(D), rows(D), prev8, prev8, _full((8, D)), rows(MW),
                  _full((NM, 2 * MW)), rows(BR_B), _full((BR_B, D))],
        out_specs=[rows(IN_B), _full((BR_B, D)), _full((8, D)), _full((NM, 2 * MW)), _full((BR_B, D))],
        out_shape=[_sds((S, IN_B), BF16), _sds((BR_B, D), F32), _sds((8, D), F32), _sds((NM, 2 * MW), F32),
                   _sds((BR_B, D), BF16)],
        scratch_shapes=[pltpu.VMEM((TM, BR_B), BF16), pltpu.VMEM((8, D), F32)],
        compiler_params=_params(("arbitrary",)),
    )(dh2, bg, cg, u, cg, u, cw, qm, kv1, z, w_out)


def _in_proj_bwd(dproj, w_in, xin, g, dres, after, width, name):
    sh = width // 4

    def body(dp_ref, w_ref, x_ref, g_ref, dr_ref, dx_ref, dg_ref):
        i = pl.program_id(0)
        dhn = _dot_nt(dp_ref[:, 0:sh], w_ref[0])
        for s in range(1, 4):
            dhn += _dot_nt(dp_ref[:, s * sh:(s + 1) * sh], w_ref[s])
        xf = x_ref[...]
        rstd = lax.rsqrt(jnp.mean(xf * xf, axis=-1, keepdims=True) + EPS)
        n = xf * rstd
        dn = dhn * g_ref[...]
        dx_ref[...] = dr_ref[...] + rstd * (dn - n * jnp.mean(dn * n, axis=-1, keepdims=True))

        @pl.when(i == 0)
        def _():
            dg_ref[...] = jnp.zeros_like(dg_ref)

        dg_ref[...] += jnp.sum(dhn * n, axis=0, keepdims=True)

    return pl.pallas_call(
        functools.partial(_skip_arg, body, 5), name=name, grid=(NT,),
        in_specs=[_rows(width), _full((4, D, sh)), _rows(D), _full((1, D)), _rows(D), pl.BlockSpec(memory_space=pl.ANY)],
        out_specs=[_rows(D), _full((1, D))],
        out_shape=[_sds((S, D), F32), _sds((1, D), F32)],
        compiler_params=_params(("arbitrary",)),
    )(dproj, w_in, xin, g, dres, after)


def _w_in_grad(hn, dproj, width, name):
    sh = width // 4

    def body(hn_ref, dp_ref, dw_ref, dwb_ref):
        dw = _dot_tn(hn_ref[...], dp_ref[...])
        dw_ref[0] = dw
        dwb_ref[0] = dw.astype(BF16)

    spec = pl.BlockSpec((1, D, sh), lambda s: (s, 0, 0))
    return pl.pallas_call(
        body, name=name, grid=(4,),
        in_specs=[_full((S, D)), pl.BlockSpec((S, sh), lambda s: (0, s))],
        out_specs=[spec, spec], out_shape=[_sds((4, D, sh), F32), _sds((4, D, sh), BF16)],
        compiler_params=_params(("parallel",)),
    )(hn, dproj)


def _attn_out_bwd(dh1, os_, ls, qm, kv0, z, w_out, after):
    ones_bd = np.kron(np.eye(GW // HD, dtype=np.float32), np.ones((HD, HD), np.float32))

    def body(dh_ref, o0, o1, o2, l0, l1, l2, qm_ref, kv_ref, z_ref, w_ref, bd_ref,
             do0, do1, do2, dd0, dd1, dd2, dqm_ref, dz_ref, dw_ref, dkv_ref, dwb_ref, ybuf):
        i = pl.program_id(0)

        @pl.when(i == 0)
        def _():
            dw_ref[...] = jnp.zeros_like(dw_ref)
            dkv_ref[...] = jnp.zeros_like(dkv_ref)

        ws, mix = _merge((o0, o1, o2), (l0, l1, l2))
        sz, dsz = _silu_parts(z_ref[...])
        kvv = kv_ref[...]
        p, mo, q4 = _mem_attn(qm_ref[...], kvv)
        ybuf[:, :GW] = (mix * sz[:, :GW]).astype(BF16)
        ybuf[:, GW:] = (mo * sz[:, GW:]).astype(BF16)
        yb = ybuf[...]
        dh = dh_ref[...]
        dy = None
        for s in range(4):
            dhb = dh[:, s * SH_O:(s + 1) * SH_O].astype(BF16)
            dw_ref[s] += _dot_tn(yb, dhb)
            part = _dot_nt(dhb, w_ref[s])
            dy = part if dy is None else dy + part
        dcat = dy * sz
        dz_ref[:, :GW] = (dy[:, :GW] * mix * dsz[:, :GW]).astype(BF16)
        dz_ref[:, GW:] = (dy[:, GW:] * mo * dsz[:, GW:]).astype(BF16)
        dmix = dcat[:, :GW]
        prod = dmix * mix
        hi = prod.astype(BF16)
        lo = (prod - hi.astype(F32)).astype(BF16)
        bd = bd_ref[...]
        tot = _dot(hi, bd) + _dot(lo, bd)
        for w, do_ref, dd_ref in zip(ws, (do0, do1, do2), (dd0, dd1, dd2)):
            do_ref[...] = (w * dmix).astype(BF16)
            dd_ref[...] = w * tot

        dqm_ref[...] = _mem_attn_bwd(dcat[:, GW:], p, mo, q4, kvv, dkv_ref).astype(BF16)

        @pl.when(i == NT - 1)
        def _():
            dwb_ref[...] = dw_ref[...].astype(BF16)

    return pl.pallas_call(
        functools.partial(_skip_arg, body, 12), name="attn_out_bwd", grid=(NT,),
        in_specs=[_rows(D)] + [_rows(GW)] * 6 + [_rows(MW), _full((NM, 2 * MW)), _rows(BR_A),
                                                   _full((4, BR_A, SH_O)), _full((GW, GW)),
                                                   pl.BlockSpec(memory_space=pl.ANY)],
        out_specs=[_rows(GW)] * 6 + [_rows(MW), _rows(BR_A), _full((4, BR_A, SH_O)), _full((NM, 2 * MW)),
                                     _full((4, BR_A, SH_O))],
        out_shape=[_sds((S, GW), BF16)] * 3 + [_sds((S, GW), F32)] * 3 + [
            _sds((S, MW), BF16), _sds((S, BR_A), BF16), _sds((4, BR_A, SH_O), F32), _sds((NM, 2 * MW), F32),
            _sds((4, BR_A, SH_O), BF16)],
        scratch_shapes=[pltpu.VMEM((TM, BR_A), BF16)],
        compiler_params=_params(("arbitrary",)),
    )(dh1, *os_, *ls, qm, kv0, z, w_out, jnp.asarray(ones_bd, dtype=BF16), after)


def _attn_bwd(q, k, v, do, lse_s, dd, g):
    d = DILATIONS[g]
    nb = S // d // QBLK
    perm = _perm_matrix(d)

    def body(q_ref, k_ref, v_ref, do_ref, l_ref, dd_ref, p_ref, pt_ref, dq_ref, dk_ref, dv_ref,
             q0, q1, g0, g1, ks, vs, dds, dqs, dks, dvs):
        first, second = _head_masks()
        pm = p_ref[...]
        for t in range(NT):
            rows = slice(t * TM, (t + 1) * TM)
            if d == 1:
                qt = q_ref[rows, :].astype(F32)
                gt = do_ref[rows, :].astype(F32)
            else:
                qt, gt = _pair_dot(pm, q_ref[rows, :], do_ref[rows, :])
                kt, vt = _pair_dot(pm, k_ref[rows, :], v_ref[rows, :])
                _tile_to_streams(kt, ks, t, d)
                _tile_to_streams(vt, vs, t, d)
                _tile_to_streams(_split_dot(pm, dd_ref[rows, :], 2), dds, t, d)
            _tile_to_streams(jnp.where(first, qt, 0.0), q0, t, d)
            _tile_to_streams(jnp.where(second, qt, 0.0), q1, t, d)
            _tile_to_streams(jnp.where(first, gt, 0.0), g0, t, d)
            _tile_to_streams(jnp.where(second, gt, 0.0), g1, t, d)
        kref, vref, ddref = (k_ref, v_ref, dd_ref) if d == 1 else (ks, vs, dds)
        dqref, dkref, dvref = dqs, dks, dvs
        dkref[...] = jnp.zeros_like(dkref)
        dvref[...] = jnp.zeros_like(dvref)

        def blk(b, carry):
            r0 = pl.multiple_of(b * QBLK, QBLK)
            p0 = pl.multiple_of(jnp.maximum(b - 1, 0) * QBLK, QBLK)
            kk = jnp.concatenate([kref[pl.ds(p0, QBLK), :], kref[pl.ds(r0, QBLK), :]], axis=0)
            vv = jnp.concatenate([vref[pl.ds(p0, QBLK), :], vref[pl.ds(r0, QBLK), :]], axis=0)
            lb = l_ref[pl.ds(r0, QBLK), :]
            ddb = ddref[pl.ds(r0, QBLK), :]
            lcol = jnp.concatenate([lb[:, 0:1], lb[:, HD:HD + 1]], axis=0)
            dcol = jnp.concatenate([ddb[:, 0:1], ddb[:, HD:HD + 1]], axis=0)
            valid = _band_mask(b & (nb - 1))
            valid2 = jnp.concatenate([valid, valid], axis=0)
            qq = jnp.concatenate([q0[pl.ds(r0, QBLK), :], q1[pl.ds(r0, QBLK), :]], axis=0)
            gg = jnp.concatenate([g0[pl.ds(r0, QBLK), :], g1[pl.ds(r0, QBLK), :]], axis=0)
            p = jnp.where(valid2, jnp.exp(_dot_nt(qq, kk) - lcol), 0.0)
            ds = (p * (_dot_nt(gg, vv) - dcol)).astype(BF16)
            dq2 = _dot(ds, kk)
            dqref[pl.ds(r0, QBLK), :] = jnp.where(first[:QBLK], dq2[:QBLK], dq2[QBLK:])
            dkk = _dot_tn(ds, qq)
            dvv = _dot_tn(p.astype(BF16), gg)
            dkref[pl.ds(p0, QBLK), :] += dkk[:QBLK]
            dkref[pl.ds(r0, QBLK), :] += dkk[QBLK:]
            dvref[pl.ds(p0, QBLK), :] += dvv[:QBLK]
            dvref[pl.ds(r0, QBLK), :] += dvv[QBLK:]
            return carry

        lax.fori_loop(0, S // QBLK, blk, 0, unroll=BWD_UNROLL)

        ptm = pt_ref[...] if d > 1 else None
        for t in range(NT):
            rows = slice(t * TM, (t + 1) * TM)
            if d == 1:
                dq_ref[rows, :] = dqs[rows, :].astype(BF16)
                dk_ref[rows, :] = dks[rows, :].astype(BF16)
                dv_ref[rows, :] = dvs[rows, :].astype(BF16)
            else:
                tq, tk = _pair_dot(ptm, _tile_from_streams(dqs, t, d).astype(BF16),
                                   _tile_from_streams(dks, t, d).astype(BF16))
                dq_ref[rows, :] = tq.astype(BF16)
                dk_ref[rows, :] = tk.astype(BF16)
                dv_ref[rows, :] = _dot(ptm, _tile_from_streams(dvs, t, d).astype(BF16)).astype(BF16)

    qkv_spec = pl.BlockSpec((S, LANES), lambda c: (0, g * NCHUNK + c))
    one_spec = pl.BlockSpec((S, LANES), lambda c: (0, c))
    return pl.pallas_call(
        body, name=f"attn_bwd_g{g}", grid=(NCHUNK,),
        in_specs=[qkv_spec] * 3 + [one_spec] * 3 + [_full((TM, TM))] * 2, out_specs=[one_spec] * 3,
        out_shape=[_sds((S, GW), BF16)] * 3,
        scratch_shapes=[pltpu.VMEM((S, LANES), BF16)] * 6 + [pltpu.VMEM((S, LANES), F32)] * 4,
        compiler_params=_params(("parallel",)),
    )(q, k, v, do, lse_s, dd, jnp.asarray(perm, BF16), jnp.asarray(perm.T, BF16))


def _qkv_bwd(dqs, dks, dvs, dqm, dz, c, s1, s2):
    def body(q0, q1, q2, k0, k1, k2, v0, v1, v2, dqm_ref, dz_ref, c_ref, s1_ref, s2_ref, dp_ref):
        cc, a1, a2 = c_ref[...], s1_ref[...], s2_ref[...]
        for g, (qr, kr, vr) in enumerate(((q0, k0, v0), (q1, k1, v1), (q2, k2, v2))):
            for j in range(GW // 128):
                ls_ = slice(j * 128, (j + 1) * 128)
                c0 = g * GW + j * 128
                dp_ref[:, c0:c0 + 128] = (_rope_bwd(qr[:, ls_].astype(F32), cc, a1, a2) * SCALE).astype(BF16)
                dp_ref[:, NQ + c0:NQ + c0 + 128] = _rope_bwd(kr[:, ls_].astype(F32), cc, a1, a2).astype(BF16)
            dp_ref[:, 2 * NQ + g * GW:2 * NQ + (g + 1) * GW] = vr[...]
        dp_ref[:, 3 * NQ:3 * NQ + MW] = dqm_ref[...]
        dp_ref[:, 3 * NQ + MW:] = dz_ref[...]

    return pl.pallas_call(
        body, name="qkv_bwd", grid=(NT,),
        in_specs=[_rows(GW)] * 9 + [_rows(MW), _rows(BR_A), _rows(128), _rows(128), _rows(128)],
        out_specs=_rows(IN_A), out_shape=_sds((S, IN_A), BF16),
        compiler_params=_params(("parallel",)),
    )(*dqs, *dks, *dvs, dqm, dz, c, s1, s2)


def _mem_bwd(mem, mg, memn, wkv, dkv0, dkv1):
    def body(mem_ref, mg_ref, memn_ref, w_ref, d0_ref, d1_ref, dw_ref, dwb_ref, dg_ref):
        mf = mem_ref[...]
        n = mf * lax.rsqrt(jnp.mean(mf * mf, axis=-1, keepdims=True) + EPS)
        for i, d_ref in enumerate((d0_ref, d1_ref)):
            dkv = d_ref[...].astype(BF16)
            mn = memn_ref[i]
            for s in range(4):
                cs = slice(s * NM, (s + 1) * NM)
                dw = _dot_tn(mn[:, cs], dkv)
                dw_ref[s, i] = dw
                dwb_ref[s, i] = dw.astype(BF16)
                dmn = _dot_nt(dkv, w_ref[s, i])
                dg_ref[i:i + 1, cs] = jnp.sum(dmn * n[:, cs], axis=0, keepdims=True)

    return pl.pallas_call(
        body, name="mem_bwd", grid=(1,),
        in_specs=[_full((NM, D)), _full((2, D)), _full((2, NM, D)), _full((4, 2, NM, 2 * MW)),
                  _full((NM, 2 * MW)), _full((NM, 2 * MW))],
        out_specs=[_full((4, 2, NM, 2 * MW)), _full((4, 2, NM, 2 * MW)), _full((2, D))],
        out_shape=[_sds((4, 2, NM, 2 * MW), F32), _sds((4, 2, NM, 2 * MW), BF16), _sds((2, D), F32)],
        compiler_params=_params(("arbitrary",)),
    )(mem, mg, memn, wkv, dkv0, dkv1)


MESH = pl.DeviceIdType.MESH
ANY = pl.BlockSpec(memory_space=pl.ANY)
BIG = (("wkv", 2, NM, 2 * MW), ("w_in_a", 1, D, SH_A), ("w_out_a", 1, BR_A, SH_O),
       ("w_in_b", 1, D, SH_B), ("w_out_b", 1, BR_B // 4, D))
NBIG = len(BIG)
CW_ROWS = 8


def _place():
    x, y, c = lax.axis_index("x"), lax.axis_index("y"), lax.axis_index("c")
    chips = ((1 - x, y), (x, 1 - y), (1 - x, 1 - y))
    return x, y, c, chips


def _remote(src, dst, ssem, rsem, dev):
    return pltpu.make_async_remote_copy(src_ref=src, dst_ref=dst, send_sem=ssem, recv_sem=rsem,
                                        device_id=dev, device_id_type=MESH)


def _cast_weights(place, ws, after, idx, name):
    nblk = 4
    n = len(idx)
    dims = [BIG[w][1:] for w in idx]

    def body(pref, *refs):
        for i in range(n):
            refs[n + 1 + i][0] = refs[i][...].astype(BF16)

    grid_spec = pltpu.PrefetchScalarGridSpec(
        num_scalar_prefetch=1, grid=(nblk,),
        in_specs=[pl.BlockSpec((k, r // nblk, cdim), lambda i, pref: (0, i, 0)) for k, r, cdim in dims]
        + [pl.BlockSpec(memory_space=pl.ANY)],
        out_specs=[pl.BlockSpec((1, k, r // nblk, cdim), lambda i, pref: (pref[1], 0, i, 0)) for k, r, cdim in dims])
    return pl.pallas_call(
        body, name=name, grid_spec=grid_spec,
        out_shape=[_sds((4, k, r, cdim), BF16) for k, r, cdim in dims],
        compiler_params=_params(("parallel",)),
    )(place, *ws, after)


LAYER_A = (0, 1, 2)
LAYER_B = (3, 4)
HBM = pl.BlockSpec(memory_space=pltpu.HBM)
SEM = pl.BlockSpec(memory_space=pltpu.SEMAPHORE)
EFFECT = pltpu.SideEffectType.DATAFLOW_SIDE_EFFECTING
TOKEN = (8, 128)


def _half(ref, w, which):
    h = BIG[w][2] // 2
    return ref.at[:, pl.ds(which * h, h), :]


def _skip_arg(body, pos, *refs):
    return body(*refs[:pos], *refs[pos + 1:])


def _gather_start(wb, after, idx, name):
    n = len(idx)

    def body(*refs):
        src = refs[:n]
        send_sems, recv_sems = refs[n + 1], refs[n + 2]
        token = refs[2 * n + 3]
        x, y, c, chips = _place()
        me = 2 * x + y
        for j, (px, py) in enumerate(chips):
            for i in range(n):
                mine = _half(src[i].at[me], idx[i], c)
                _remote(mine, mine, send_sems.at[j * n + i], recv_sems.at[j * n + i], (px, py, c)).start()
        token[...] = jnp.zeros(TOKEN, F32)

    outs = pl.pallas_call(
        body, name=name, in_specs=[HBM] * n + [ANY],
        out_specs=(SEM, SEM) + (HBM,) * n + (pl.BlockSpec(memory_space=pltpu.VMEM),),
        out_shape=(pltpu.SemaphoreType.DMA((3 * n,)), pltpu.SemaphoreType.DMA((3 * n,)))
        + tuple(pltpu.HBM(w.shape, w.dtype) for w in wb) + (_sds(TOKEN, F32),),
        input_output_aliases={i: 2 + i for i in range(n)},
        compiler_params=pltpu.CompilerParams(has_side_effects=EFFECT),
    )(*[pltpu.with_memory_space_constraint(w, pltpu.HBM) for w in wb], after)
    return outs[0], outs[1], list(outs[2:2 + n]), outs[2 + n]


def _gather_wait(send_sems, recv_sems, wb, after, idx, name, started=None):
    n = len(idx)
    started = idx if started is None else started
    n_all = len(started)
    pos = [started.index(w) for w in idx]

    def body(*refs):
        buf = refs[:n]
        send_sems, recv_sems = refs[n], refs[n + 1]
        x, y, c, chips = _place()
        me = 2 * x + y
        for j, (px, py) in enumerate(chips):
            for i in range(n):
                mine = _half(buf[i].at[me], idx[i], c)
                got = _half(buf[i].at[2 * px + py], idx[i], c)
                k = j * n_all + pos[i]
                _remote(mine, mine, send_sems.at[k], recv_sems.at[k], (px, py, c)).wait_send()
                _remote(got, got, send_sems.at[k], recv_sems.at[k], (px, py, c)).wait_recv()

    outs = pl.pallas_call(
        body, name=name, in_specs=[HBM] * n + [SEM, SEM] + [ANY] * len(after), out_specs=(HBM,) * n,
        out_shape=tuple(pltpu.HBM(w.shape, w.dtype) for w in wb),
        input_output_aliases={i: i for i in range(n)},
        compiler_params=pltpu.CompilerParams(has_side_effects=EFFECT),
    )(*wb, send_sems, recv_sems, *after)
    return list(outs)


def _gather_forward(wb, idx, name):
    n = len(idx)

    def body(*refs):
        dst = refs[n:2 * n]
        send_sems, recv_sems = refs[2 * n], refs[2 * n + 1]
        x, y, c, chips = _place()
        cps = []
        for j, (px, py) in enumerate(chips):
            for i in range(n):
                got = _half(dst[i].at[2 * px + py], idx[i], c)
                cps.append(_remote(got, got, send_sems.at[j, i], recv_sems.at[j, i], (x, y, 1 - c)))
                cps[-1].start()
        for j, (px, py) in enumerate(chips):
            for i in range(n):
                got = _half(dst[i].at[2 * px + py], idx[i], 1 - c)
                _remote(got, got, send_sems.at[j, i], recv_sems.at[j, i], (x, y, 1 - c)).wait_recv()
        for cp in cps:
            cp.wait_send()

    return pl.pallas_call(
        body, name=name, in_specs=[ANY] * n, out_specs=[ANY] * n, out_shape=[_sds(w.shape, BF16) for w in wb],
        input_output_aliases={i: i for i in range(n)},
        scratch_shapes=[pltpu.SemaphoreType.DMA((3, n)), pltpu.SemaphoreType.DMA((3, n))],
    )(*wb)


def _forward_start(wb, cw, after, idx, name):
    n = len(idx)
    m = n if cw is None else n + 2

    def body(*refs):
        buf = refs[:n]
        send_sems, recv_sems = refs[m + 1], refs[m + 2]
        token = refs[2 * m + 3]
        x, y, c, chips = _place()
        for j, (px, py) in enumerate(chips):
            for i in range(n):
                got = _half(buf[i].at[2 * px + py], idx[i], c)
                _remote(got, got, send_sems.at[j * (n + 1) + i], recv_sems.at[j * (n + 1) + i], (x, y, 1 - c)).start()
            if cw is not None:
                _remote(refs[n], refs[n + 1].at[2 * x + y], send_sems.at[j * (n + 1) + n],
                        recv_sems.at[j * (n + 1) + n], (px, py, c)).start()
        token[...] = jnp.zeros(TOKEN, F32)

    arrays = list(wb) if cw is None else list(wb) + [cw, lax.empty((4, CW_ROWS, SH_O), F32)]
    outs = pl.pallas_call(
        body, name=name, in_specs=[HBM] * m + [ANY],
        out_specs=(SEM, SEM) + (HBM,) * m + (pl.BlockSpec(memory_space=pltpu.VMEM),),
        out_shape=(pltpu.SemaphoreType.DMA((3 * (n + 1),)), pltpu.SemaphoreType.DMA((3 * (n + 1),)))
        + tuple(pltpu.HBM(a.shape, a.dtype) for a in arrays) + (_sds(TOKEN, F32),),
        input_output_aliases={i: 2 + i for i in range(m)},
        compiler_params=pltpu.CompilerParams(has_side_effects=EFFECT),
    )(*[pltpu.with_memory_space_constraint(a, pltpu.HBM) for a in arrays], after)
    return outs[0], outs[1], list(outs[2:2 + m]), outs[2 + m]


def _forward_wait(send_sems, recv_sems, arrays, after, idx, with_cw, name):
    n = len(idx)
    m = len(arrays)

    def body(*refs):
        buf = refs[:n]
        send_sems, recv_sems = refs[m], refs[m + 1]
        x, y, c, chips = _place()
        for j, (px, py) in enumerate(chips):
            for i in range(n):
                sent = _half(buf[i].at[2 * px + py], idx[i], c)
                got = _half(buf[i].at[2 * px + py], idx[i], 1 - c)
                k = j * (n + 1) + i
                _remote(sent, sent, send_sems.at[k], recv_sems.at[k], (x, y, 1 - c)).wait_send()
                _remote(got, got, send_sems.at[k], recv_sems.at[k], (x, y, 1 - c)).wait_recv()
            if with_cw:
                k = j * (n + 1) + n
                theirs = refs[n + 1].at[2 * px + py]
                _remote(refs[n], theirs, send_sems.at[k], recv_sems.at[k], (px, py, c)).wait_send()
                _remote(refs[n], theirs, send_sems.at[k], recv_sems.at[k], (px, py, c)).wait_recv()

    outs = pl.pallas_call(
        body, name=name, in_specs=[HBM] * m + [SEM, SEM] + [ANY] * len(after), out_specs=(HBM,) * m,
        out_shape=tuple(pltpu.HBM(a.shape, a.dtype) for a in arrays),
        input_output_aliases={i: i for i in range(m)},
        compiler_params=pltpu.CompilerParams(has_side_effects=EFFECT),
    )(*arrays, send_sems, recv_sems, *after)
    return list(outs)


def _pair_exchange(gs, idx, name):
    n = len(idx)

    def body(*refs):
        src, dst = refs[:n], refs[n:2 * n]
        send_sems, recv_sems = refs[2 * n:]
        x, y, c, _ = _place()
        cps = []
        for i in range(n):
            h = BIG[idx[i]][2] // 2
            cps.append(_remote(src[i].at[:, :, pl.ds((1 - c) * h, h), :], dst[i], send_sems.at[i], recv_sems.at[i],
                               (x, y, 1 - c)))
            cps[-1].start()
        for cp in cps:
            cp.wait()

    return pl.pallas_call(
        body, name=name, in_specs=[ANY] * n, out_specs=[ANY] * n,
        out_shape=[_sds((4, BIG[w][1], BIG[w][2] // 2, BIG[w][3]), BF16) for w in idx],
        scratch_shapes=[pltpu.SemaphoreType.DMA((n,)), pltpu.SemaphoreType.DMA((n,))],
    )(*gs)


def _pair_start(gs, idx, name):
    n = len(idx)

    def body(*refs):
        src, land = refs[:n], refs[n:2 * n]
        send_sems, recv_sems = refs[2 * n], refs[2 * n + 1]
        token = refs[4 * n + 2]
        x, y, c, _ = _place()
        for i in range(n):
            h = BIG[idx[i]][2] // 2
            _remote(src[i].at[:, :, pl.ds((1 - c) * h, h), :], land[i], send_sems.at[i], recv_sems.at[i],
                    (x, y, 1 - c)).start()
        token[...] = jnp.zeros(TOKEN, F32)

    lands = [lax.empty((4, BIG[w][1], BIG[w][2] // 2, BIG[w][3]), BF16) for w in idx]
    arrays = list(gs) + lands
    outs = pl.pallas_call(
        body, name=name, in_specs=[HBM] * (2 * n),
        out_specs=(SEM, SEM) + (HBM,) * (2 * n) + (pl.BlockSpec(memory_space=pltpu.VMEM),),
        out_shape=(pltpu.SemaphoreType.DMA((n,)), pltpu.SemaphoreType.DMA((n,)))
        + tuple(pltpu.HBM(a.shape, a.dtype) for a in arrays) + (_sds(TOKEN, F32),),
        input_output_aliases={i: 2 + i for i in range(2 * n)},
        compiler_params=pltpu.CompilerParams(has_side_effects=EFFECT),
    )(*[pltpu.with_memory_space_constraint(a, pltpu.HBM) for a in arrays])
    return outs[0], outs[1], list(outs[2:2 + n]), list(outs[2 + n:2 + 2 * n]), outs[2 + 2 * n]


def _pair_wait(send_sems, recv_sems, gs, lands, after, idx, name):
    n = len(idx)

    def body(*refs):
        src, land = refs[:n], refs[n:2 * n]
        send_sems, recv_sems = refs[2 * n], refs[2 * n + 1]
        x, y, c, _ = _place()
        for i in range(n):
            h = BIG[idx[i]][2] // 2
            cp = _remote(src[i].at[:, :, pl.ds((1 - c) * h, h), :], land[i], send_sems.at[i], recv_sems.at[i],
                         (x, y, 1 - c))
            cp.wait_send()
            cp.wait_recv()

    arrays = list(gs) + list(lands)
    outs = pl.pallas_call(
        body, name=name, in_specs=[HBM] * (2 * n) + [SEM, SEM] + [ANY] * len(after), out_specs=(HBM,) * (2 * n),
        out_shape=tuple(pltpu.HBM(a.shape, a.dtype) for a in arrays),
        input_output_aliases={i: i for i in range(2 * n)},
        compiler_params=pltpu.CompilerParams(has_side_effects=EFFECT),
    )(*arrays, send_sems, recv_sems, *after)
    return list(outs[:n]), list(outs[n:])


def _pair_sums(place, gs, r1s, idx, name):
    n = len(idx)
    dims = [(BIG[w][1], BIG[w][2] // 2, BIG[w][3]) for w in idx]

    def body(pref, *refs):
        for i in range(n):
            refs[2 * n + i][...] = (refs[i][...] + refs[n + i][...].astype(F32)).astype(BF16)

    mine = [pl.BlockSpec((1, k, h, cdim), lambda s, pref: (s, 0, pref[0], 0)) for k, h, cdim in dims]
    whole = [pl.BlockSpec((1, k, h, cdim), lambda s, pref: (s, 0, 0, 0)) for k, h, cdim in dims]
    grid_spec = pltpu.PrefetchScalarGridSpec(num_scalar_prefetch=1, grid=(4,), in_specs=mine + whole, out_specs=whole)
    return pl.pallas_call(
        body, name=name, grid_spec=grid_spec, out_shape=[_sds((4, k, h, cdim), BF16) for k, h, cdim in dims],
        compiler_params=_params(("parallel",)),
    )(place, *gs, *r1s)


def _chip_start(ps, idx, name):
    n = len(idx)

    def body(*refs):
        src, land = refs[:n], refs[n:2 * n]
        send_sems, recv_sems = refs[2 * n], refs[2 * n + 1]
        token = refs[4 * n + 2]
        x, y, c, chips = _place()
        for j, (px, py) in enumerate(chips):
            for i in range(n):
                _remote(src[i].at[2 * px + py], land[i].at[j], send_sems.at[j * n + i], recv_sems.at[j * n + i],
                        (px, py, c)).start()
        token[...] = jnp.zeros(TOKEN, F32)

    lands = [lax.empty((3,) + p.shape[1:], BF16) for p in ps]
    outs = pl.pallas_call(
        body, name=name, in_specs=[HBM] * (2 * n),
        out_specs=(SEM, SEM) + (HBM,) * (2 * n) + (pl.BlockSpec(memory_space=pltpu.VMEM),),
        out_shape=(pltpu.SemaphoreType.DMA((3 * n,)), pltpu.SemaphoreType.DMA((3 * n,)))
        + tuple(pltpu.HBM(a.shape, a.dtype) for a in list(ps) + lands) + (_sds(TOKEN, F32),),
        input_output_aliases={i: 2 + i for i in range(2 * n)},
        compiler_params=pltpu.CompilerParams(has_side_effects=EFFECT),
    )(*[pltpu.with_memory_space_constraint(a, pltpu.HBM) for a in list(ps) + lands])
    return outs[0], outs[1], list(outs[2:2 + n]), list(outs[2 + n:2 + 2 * n]), outs[2 + 2 * n]


def _chip_wait(send_sems, recv_sems, ps, lands, after, idx, name):
    n = len(idx)

    def body(*refs):
        src, land = refs[:n], refs[n:2 * n]
        send_sems, recv_sems = refs[2 * n], refs[2 * n + 1]
        x, y, c, chips = _place()
        for j, (px, py) in enumerate(chips):
            for i in range(n):
                cp = _remote(src[i].at[2 * px + py], land[i].at[j], send_sems.at[j * n + i], recv_sems.at[j * n + i],
                             (px, py, c))
                cp.wait_send()
                cp.wait_recv()

    arrays = list(ps) + list(lands)
    outs = pl.pallas_call(
        body, name=name, in_specs=[HBM] * (2 * n) + [SEM, SEM] + [ANY] * len(after), out_specs=(HBM,) * (2 * n),
        out_shape=tuple(pltpu.HBM(a.shape, a.dtype) for a in arrays),
        input_output_aliases={i: i for i in range(2 * n)},
        compiler_params=pltpu.CompilerParams(has_side_effects=EFFECT),
    )(*arrays, send_sems, recv_sems, *after)
    return list(outs[n:])


def _chip_sums(place, gs, r1s, r2s, idx, name):
    n = len(idx)
    dims = [(BIG[w][1], BIG[w][2] // 4, BIG[w][3]) for w in idx]

    def body(pref, *refs):
        for i in range(n):
            acc = refs[i][0] + refs[n + i][0].astype(F32)
            for j in range(3):
                acc = acc + refs[2 * n + i][j].astype(F32)
            refs[3 * n + i][...] = acc

    in_specs = ([pl.BlockSpec((1, k, q, cdim), lambda t, pref: (pref[1], 0, pref[0] * 2 + t, 0)) for k, q, cdim in dims]
                + [pl.BlockSpec((1, k, q, cdim), lambda t, pref: (pref[1], 0, t, 0)) for k, q, cdim in dims]
                + [pl.BlockSpec((3, k, q, cdim), lambda t, pref: (0, 0, t, 0)) for k, q, cdim in dims])
    out_specs = [pl.BlockSpec((k, q, cdim), lambda t, pref: (0, pref[0] * 2 + t, 0)) for k, q, cdim in dims]
    grid_spec = pltpu.PrefetchScalarGridSpec(num_scalar_prefetch=1, grid=(2,), in_specs=in_specs, out_specs=out_specs)
    return pl.pallas_call(
        body, name=name, grid_spec=grid_spec, out_shape=[_sds(BIG[w][1:], F32) for w in idx],
        compiler_params=_params(("parallel",)),
    )(place, *gs, *r1s, *r2s)


def _pair_gather(hs, idx, name):
    n = len(idx)

    def body(*refs):
        dst = refs[n:2 * n]
        send_sems, recv_sems = refs[2 * n:]
        x, y, c, _ = _place()
        cps = []
        for i in range(n):
            mine = _half(dst[i], idx[i], c)
            cps.append(_remote(mine, mine, send_sems.at[i], recv_sems.at[i], (x, y, 1 - c)))
            cps[-1].start()
        for i in range(n):
            theirs = _half(dst[i], idx[i], 1 - c)
            _remote(theirs, theirs, send_sems.at[i], recv_sems.at[i], (x, y, 1 - c)).wait_recv()
        for cp in cps:
            cp.wait_send()

    return pl.pallas_call(
        body, name=name, in_specs=[ANY] * n, out_specs=[ANY] * n,
        out_shape=[_sds(BIG[w][1:], F32) for w in idx],
        input_output_aliases={i: i for i in range(n)},
        scratch_shapes=[pltpu.SemaphoreType.DMA((n,)), pltpu.SemaphoreType.DMA((n,))],
    )(*hs)


SMALL_ROWS = 40


def _adamw_math(w, g, m, v):
    m = ADAM_B1 * m + (1.0 - ADAM_B1) * g
    v = ADAM_B2 * v + (1.0 - ADAM_B2) * (g * g)
    m_hat = m / (1.0 - ADAM_B1 ** ADAM_STEP)
    v_hat = v / (1.0 - ADAM_B2 ** ADAM_STEP)
    delta = -ADAM_LR * (m_hat / (jnp.sqrt(v_hat) + ADAM_EPS) + ADAM_WD * w)
    return delta, m, v


def _small_start(pack, after):
    def body(pack_ref, land_ref, after_ref, send_sems, recv_sems, pack_thru, land_thru, token):
        x, y, c, _ = _place()
        for r in range(1, 8):
            peer = (x if not r & 4 else 1 - x, y if not r & 2 else 1 - y, c if not r & 1 else 1 - c)
            _remote(pack_ref, land_ref.at[r - 1], send_sems.at[r - 1], recv_sems.at[r - 1], peer).start()
        token[...] = jnp.zeros(TOKEN, F32)

    land = lax.empty((7, SMALL_ROWS, D), F32)
    outs = pl.pallas_call(
        body, name="small_start", in_specs=[HBM, HBM, ANY],
        out_specs=(SEM, SEM, HBM, HBM, pl.BlockSpec(memory_space=pltpu.VMEM)),
        out_shape=(pltpu.SemaphoreType.DMA((7,)), pltpu.SemaphoreType.DMA((7,)), pltpu.HBM(pack.shape, F32),
                   pltpu.HBM(land.shape, F32), _sds(TOKEN, F32)),
        input_output_aliases={0: 2, 1: 3},
        compiler_params=pltpu.CompilerParams(has_side_effects=EFFECT),
    )(pltpu.with_memory_space_constraint(pack, pltpu.HBM), pltpu.with_memory_space_constraint(land, pltpu.HBM), after)
    return outs


def _small_wait(send_sems, recv_sems, pack, land, after):
    def body(pack_ref, land_ref, send_sems, recv_sems, *rest):
        x, y, c, _ = _place()
        for r in range(1, 8):
            peer = (x if not r & 4 else 1 - x, y if not r & 2 else 1 - y, c if not r & 1 else 1 - c)
            cp = _remote(pack_ref, land_ref.at[r - 1], send_sems.at[r - 1], recv_sems.at[r - 1], peer)
            cp.wait_send()
            cp.wait_recv()

    return pl.pallas_call(
        body, name="small_wait", in_specs=[HBM, HBM, SEM, SEM] + [ANY] * len(after), out_specs=(HBM, HBM),
        out_shape=(pltpu.HBM(pack.shape, F32), pltpu.HBM(land.shape, F32)),
        input_output_aliases={0: 0, 1: 1},
        compiler_params=pltpu.CompilerParams(has_side_effects=EFFECT),
    )(pack, land, send_sems, recv_sems, *after)


def _small_update(place, pack, land, ws, ms, vs):
    n = len(ws)

    def body(pref, pack_ref, land_ref, *refs):
        chip = pref[1]
        me = 2 * chip + pref[0]
        own = pack_ref[...]
        tot = None
        for dev in range(8):
            r = jnp.bitwise_xor(me, dev)
            term = jnp.where(r == 0, own, land_ref[jnp.maximum(r - 1, 0)])
            tot = term if tot is None else tot + term
        out, buf = refs[3 * n:-1], refs[-1]
        buf[...] = tot
        g_conv = jnp.zeros((3, SH_O), F32)
        for s in range(4):
            g_conv = g_conv + jnp.where(chip == s, buf[24:27, s * SH_O:(s + 1) * SH_O], 0.0)
        gs = [buf[0:2, :], buf[8:10, :], buf[16:17, :], g_conv]
        out[0][...] = buf[32:33, 0:128]
        for i in range(n):
            d, nm, nv = _adamw_math(refs[i][...], gs[i], refs[n + i][...], refs[2 * n + i][...])
            out[1 + i][...] = gs[i]
            out[1 + n + i][...] = d
            out[1 + 2 * n + i][...] = nm
            out[1 + 3 * n + i][...] = nv

    def full(shape):
        nd = len(shape)
        return pl.BlockSpec(shape, lambda i, pref: (0,) * nd)

    specs = [full(w.shape) for w in ws]
    grid_spec = pltpu.PrefetchScalarGridSpec(
        num_scalar_prefetch=1, grid=(1,),
        in_specs=[full(pack.shape), full(land.shape)] + specs * 3, out_specs=[full((1, 128))] + specs * 4,
        scratch_shapes=[pltpu.VMEM((SMALL_ROWS, D), F32)])
    outs = pl.pallas_call(
        body, name="small_update", grid_spec=grid_spec,
        out_shape=[_sds((1, 128), F32)] + [_sds(w.shape, F32) for w in ws] * 4,
        compiler_params=_params(("arbitrary",)),
    )(place, pack, land, *ws, *ms, *vs)
    return outs[0], outs[1:1 + n], outs[1 + n:1 + 2 * n], outs[1 + 2 * n:1 + 3 * n], outs[1 + 3 * n:]


def _adamw_layer(ws, gs, ms, vs, idx, name):
    n = len(idx)
    dims = [(BIG[w][1], BIG[w][2] // 4, BIG[w][3]) for w in idx]

    def body(*refs):
        for i in range(n):
            gv = refs[n + i][...]
            d, nm, nv = _adamw_math(refs[i][...], gv, refs[2 * n + i][...], refs[3 * n + i][...])
            refs[4 * n + i][...] = d
            refs[5 * n + i][...] = nm
            refs[6 * n + i][...] = nv
            refs[7 * n + i][...] = gv

    specs = [pl.BlockSpec((k, q, cdim), lambda t: (0, t, 0)) for k, q, cdim in dims]
    outs = pl.pallas_call(
        body, name=name, grid=(4,), in_specs=specs * 4, out_specs=specs * 4,
        out_shape=[_sds(BIG[w][1:], F32) for w in idx] * 4,
        compiler_params=_params(("parallel",)),
    )(*ws, *gs, *ms, *vs)
    return [tuple(outs[j * n + i] for j in range(4)) for i in range(n)]


def _pad_rows(a, rows):
    return jnp.pad(a, ((0, rows - a.shape[0]), (0, 0)))


def kernel(x, mem, positions, norm_g, mem_norm_g, w_mem_kv, attn_w_in, attn_w_out, conv_w_in, conv_w, conv_w_out, final_g, loss_target, m_norm_g, m_mem_norm_g, m_w_mem_kv, m_attn_w_in, m_attn_w_out, m_conv_w_in, m_conv_w, m_conv_w_out, m_final_g, v_norm_g, v_mem_norm_g, v_w_mem_kv, v_attn_w_in, v_attn_w_out, v_conv_w_in, v_conv_w, v_conv_w_out, v_final_g):
    mx, my, mc = lax.axis_index("x"), lax.axis_index("y"), lax.axis_index("c")
    place = jnp.stack([mc, 2 * mx + my]).astype(jnp.int32)

    w_big = [w_mem_kv, attn_w_in, attn_w_out, conv_w_in, conv_w_out]
    m_big = [m_w_mem_kv, m_attn_w_in, m_attn_w_out, m_conv_w_in, m_conv_w_out]
    v_big = [v_w_mem_kv, v_attn_w_in, v_attn_w_out, v_conv_w_in, v_conv_w_out]
    first, rest = (1,), (0, 2, 3, 4)
    wb1 = _cast_weights(place, [w_big[i] for i in first], place, first, "cast_w_in_a")
    a1_send, a1_recv, a1_bufs, a1_token = _gather_start(wb1, place, first, "gather_a1_start")
    wbr = _cast_weights(place, [w_big[i] for i in rest], a1_token, rest, "cast_weights")
    r_send, r_recv, r_bufs, gb_token = _gather_start(wbr, a1_token, rest, "gather_rest_start")
    a2_send, a2_recv, gb_send, gb_recv = r_send, r_recv, r_send, r_recv
    a2_bufs, gb_bufs = r_bufs[:2], r_bufs[2:]
    started, rest = rest, (0, 2)

    xs, tgt = x[0], loss_target[0]
    g0, g1 = norm_g[0:1], norm_g[1:2]
    rc, rs1, rs2 = _rope_tables(positions[0].astype(F32).reshape(S, 1), gb_token)
    a1_bufs = _gather_wait(a1_send, a1_recv, a1_bufs, [rc], first, "gather_a1_wait")
    w_in_a = _gather_forward(a1_bufs, first, "gather_a1_forward")[0].reshape(4, D, SH_A)
    hn0, q, k, v, qm0, z0 = _in_proj_a(xs, g0, w_in_a, rc, rs1, rs2, gb_token)
    a2_bufs = _gather_wait(a2_send, a2_recv, a2_bufs, [q], rest, "gather_a2_wait", started)
    f2_send, f2_recv, a2_bufs, f2_token = _forward_start(a2_bufs, None, q, rest, "forward_a2_start")
    fwd = [_attn_fwd(q, k, v, 0, f2_token)]
    fwd.append(_attn_fwd(q, k, v, 1, fwd[0][0]))
    cw_own = _pad_rows(conv_w[0], CW_ROWS)
    gb_bufs = _gather_wait(gb_send, gb_recv, gb_bufs, [fwd[1][0]], LAYER_B, "gather_b_wait", started)
    fb_send, fb_recv, gb_bufs, fb_token = _forward_start(gb_bufs, cw_own, fwd[1][0], LAYER_B, "forward_b_start")
    fwd.append(_attn_fwd(q, k, v, 2, fb_token))
    os_, ls, lss = [f[0] for f in fwd], [f[1] for f in fwd], [f[2] for f in fwd]
    wkv_f, w_out_a = _forward_wait(f2_send, f2_recv, a2_bufs, [os_[2]], rest, False, "forward_a2_wait")
    w_out_a = w_out_a.reshape(4, BR_A, SH_O)
    memn, kv = _mem_fwd(mem[0], mem_norm_g, wkv_f)
    h1 = _attn_out(os_, ls, qm0, kv[0], z0, xs, w_out_a)

    w_in_b, w_out_b, _, cw_f = _forward_wait(fb_send, fb_recv, gb_bufs, [h1], LAYER_B, True, "forward_b_wait")
    w_in_b = w_in_b.reshape(4, D, SH_B)
    w_out_b = w_out_b.reshape(BR_B, D)
    cw_f = lax.dynamic_update_slice(cw_f, cw_own[None], (2 * mx + my, 0, 0))
    cw8 = cw_f.transpose(1, 0, 2).reshape(CW_ROWS, D)
    hn1, bg, cg, u, qm1, z1 = _in_proj_b(h1, g1, w_in_b)
    dh2, loss_part, dfg = _conv_out_loss(bg, cg, u, cw8, qm1, kv[1], z1, h1, w_out_b, final_g.reshape(1, D), tgt)

    dproj_b, dw_out_b, dcw, dkv1, dw_out_b16 = _conv_bwd(dh2, bg, cg, u, cw8, qm1, kv[1], z1, w_out_b)
    dw_in_b, dw_in_b16 = _w_in_grad(hn1, dproj_b, IN_B, "w_in_b_grad")
    gs_b = [dw_in_b.reshape(4, 1, D, SH_B), dw_out_b.reshape(4, 1, BR_B // 4, D)]
    gb_b = [dw_in_b16.reshape(4, 1, D, SH_B), dw_out_b16.reshape(4, 1, BR_B // 4, D)]
    pb_send, pb_recv, gb_b, pb_land, pb_token = _pair_start(gb_b, LAYER_B, "pair_b_start")
    dh1, dg1 = _in_proj_bwd(dproj_b, w_in_b, h1, g1, dh2, pb_token, IN_B, "in_proj_b_bwd")
    _, r1_b = _pair_wait(pb_send, pb_recv, gb_b, pb_land, [dh1], LAYER_B, "pair_b_wait")
    ps_b = _pair_sums(place, gs_b, r1_b, LAYER_B, "pair_sums_b")
    cb_send, cb_recv, cb_src, cb_land, cb_token = _chip_start(ps_b, LAYER_B, "chip_b_start")

    outs = _attn_out_bwd(dh1, os_, ls, qm0, kv[0], z0, w_out_a, cb_token)
    dos, dds, dqm, dz, dw_out_a, dkv0, dw_out_a16 = outs[0:3], outs[3:6], outs[6], outs[7], outs[8], outs[9], outs[10]
    bwd = [_attn_bwd(q, k, v, dos[g], lss[g], dds[g], g) for g in range(3)]
    dproj_a = _qkv_bwd([b[0] for b in bwd], [b[1] for b in bwd], [b[2] for b in bwd], dqm, dz, rc, rs1, rs2)
    dw_in_a, dw_in_a16 = _w_in_grad(hn0, dproj_a, IN_A, "w_in_a_grad")
    dwkv, dwkv16, dmg = _mem_bwd(mem[0], mem_norm_g, memn, wkv_f, dkv0, dkv1)

    gs_a = [dwkv, dw_in_a.reshape(4, 1, D, SH_A), dw_out_a.reshape(4, 1, BR_A, SH_O)]
    r1_a = _pair_exchange([dwkv16, dw_in_a16.reshape(4, 1, D, SH_A), dw_out_a16.reshape(4, 1, BR_A, SH_O)], LAYER_A,
                          "pair_exchange_a")
    ps_a = _pair_sums(place, gs_a, r1_a, LAYER_A, "pair_sums_a")
    ca_send, ca_recv, ca_src, ca_land, ca_token = _chip_start(ps_a, LAYER_A, "chip_a_start")

    gx, dg0 = _in_proj_bwd(dproj_a, w_in_a, xs, g0, dh1, ca_token, IN_A, "in_proj_a_bwd")
    pack = jnp.concatenate([_pad_rows(jnp.concatenate([dg0, dg1], axis=0), 8), _pad_rows(dmg, 8), _pad_rows(dfg, 8),
                            dcw, _pad_rows(jnp.pad(loss_part, ((0, 0), (0, D - 128))), 8)], axis=0)
    sm_send, sm_recv, pack, sm_land, sm_token = _small_start(pack, ca_token)
    r2_b = _chip_wait(cb_send, cb_recv, cb_src, cb_land, [ca_token], LAYER_B, "chip_b_wait")
    hs_b = _chip_sums(place, gs_b, r1_b, r2_b, LAYER_B, "chip_sums_b")
    g_b = _pair_gather(hs_b, LAYER_B, "pair_gather_b")
    upd_b = _adamw_layer([w_big[w] for w in LAYER_B], g_b, [m_big[w] for w in LAYER_B], [v_big[w] for w in LAYER_B],
                         LAYER_B, "adamw_b")
    r2_a = _chip_wait(ca_send, ca_recv, ca_src, ca_land, [gx, upd_b[0][0], upd_b[1][0], sm_token], LAYER_A,
                      "chip_a_wait")
    hs_a = _chip_sums(place, gs_a, r1_a, r2_a, LAYER_A, "chip_sums_a")
    g_a = _pair_gather(hs_a, LAYER_A, "pair_gather_a")
    upd_a = _adamw_layer([w_big[w] for w in LAYER_A], g_a, [m_big[w] for w in LAYER_A], [v_big[w] for w in LAYER_A],
                         LAYER_A, "adamw_a")
    upd = upd_a + upd_b
    g_big = [u[3] for u in upd]
    pack, sm_land = _small_wait(sm_send, sm_recv, pack, sm_land, [r2_a[0]])
    sw = [norm_g, mem_norm_g, final_g.reshape(1, D), conv_w[0]]
    sm = [m_norm_g, m_mem_norm_g, m_final_g.reshape(1, D), m_conv_w[0]]
    sv = [v_norm_g, v_mem_norm_g, v_final_g.reshape(1, D), v_conv_w[0]]
    loss_row, sg, sd, snm, snv = _small_update(place, pack, sm_land, sw, sm, sv)
    loss = loss_row[0, 0]
    g_norm, g_memnorm, g_final, g_conv = sg

    def order(norm, memnorm, wkv, w_in_a, w_out_a, w_in_b, conv, w_out_b, final):
        return (norm, memnorm, wkv, w_in_a, w_out_a, w_in_b, conv.reshape(1, 3, SH_O), w_out_b, final.reshape(D))

    grads = order(g_norm, g_memnorm, g_big[0], g_big[1], g_big[2], g_big[3], g_conv, g_big[4], g_final)
    deltas = order(sd[0], sd[1], upd[0][0], upd[1][0], upd[2][0], upd[3][0], sd[3], upd[4][0], sd[2])
    new_m = order(snm[0], snm[1], upd[0][1], upd[1][1], upd[2][1], upd[3][1], snm[3], upd[4][1], snm[2])
    new_v = order(snv[0], snv[1], upd[0][2], upd[1][2], upd[2][2], upd[3][2], snv[3], upd[4][2], snv[2])
    return (loss, gx[None], *grads, *deltas, *new_m, *new_v)
```

```python
import functools

import numpy as np
import jax
import jax.numpy as jnp
from jax import lax
from jax.experimental import pallas as pl
from jax.experimental.pallas import tpu as pltpu

F32 = jnp.float32
BF16 = jnp.bfloat16

S = 2048
D = 1024
TM = 256
NT = S // TM
HD = 64
GW = 512
NQ = 3 * GW
MW = 256
NM = 256
IN_A = 3 * NQ + MW + GW + MW
IN_B = 3 * D + MW + D + MW
BR_A = GW + MW
BR_B = D + MW
SH_A = IN_A // 4
SH_B = IN_B // 4
SH_O = D // 4
QBLK = 128
DILATIONS = (1, 4, 16)
EPS = 1e-6
SCALE = HD ** -0.5
NEG = -1e30
ROPE_THETA = 500000.0

ADAM_LR = 0.001
ADAM_B1 = 0.9
ADAM_B2 = 0.999
ADAM_EPS = 1e-08
ADAM_WD = 0.01
ADAM_STEP = 10

VMEM_LIMIT_BYTES = 60 * 1024 * 1024


def _params(sem=None):
    if sem is None:
        return pltpu.CompilerParams(vmem_limit_bytes=VMEM_LIMIT_BYTES)
    return pltpu.CompilerParams(dimension_semantics=sem, vmem_limit_bytes=VMEM_LIMIT_BYTES)


def _full(shape):
    nd = len(shape)
    return pl.BlockSpec(shape, lambda *_: (0,) * nd)


def _rows(width, tm=TM):
    return pl.BlockSpec((tm, width), lambda i: (i, 0))


def _sds(shape, dtype):
    return jax.ShapeDtypeStruct(shape, dtype)


def _silu_parts(z):
    sig = 0.5 * jnp.tanh(0.5 * z) + 0.5
    return z * sig, sig * (1.0 + z * (1.0 - sig))


def _dot(a, b):
    return jnp.dot(a, b, preferred_element_type=F32)


def _dot_nt(a, b):
    return lax.dot_general(a, b, (((1,), (1,)), ((), ())), preferred_element_type=F32)


def _dot_tn(a, b):
    return lax.dot_general(a, b, (((0,), (0,)), ((), ())), preferred_element_type=F32)


def _rope_fwd(t, c, s1, s2):
    return t * c + pltpu.roll(t, 120, 1) * s1 + pltpu.roll(t, 8, 1) * s2


def _rope_bwd(g, c, s1, s2):
    return g * c + pltpu.roll(g * s1, 8, 1) + pltpu.roll(g * s2, 120, 1)


MEM_HEADS = MW // HD


def _stack_heads(x):
    head = lax.broadcasted_iota(jnp.int32, x.shape, 1) // HD
    return jnp.concatenate([jnp.where(head == h, x, 0.0) for h in range(MEM_HEADS)], axis=0).astype(BF16)


def _unstack_heads(x4):
    tm = x4.shape[0] // MEM_HEADS
    head = lax.broadcasted_iota(jnp.int32, (tm, MW), 1) // HD
    out = x4[:tm]
    for h in range(1, MEM_HEADS):
        out = jnp.where(head == h, x4[h * tm:(h + 1) * tm], out)
    return out


def _mem_attn(qm, kv):
    q4 = _stack_heads(qm.astype(F32))
    s = _dot_nt(q4, kv[:, :MW]) * SCALE
    e = jnp.exp(s - jnp.max(s, axis=-1, keepdims=True))
    p = e * (1.0 / jnp.sum(e, axis=-1, keepdims=True))
    return p, _unstack_heads(_dot(p.astype(BF16), kv[:, MW:])), q4


def _mem_attn_bwd(dmo, p, mo, q4, kv, dkv_ref):
    tm = dmo.shape[0]
    head = lax.broadcasted_iota(jnp.int32, dmo.shape, 1) // HD
    prod = dmo * mo
    delta = jnp.concatenate([jnp.sum(jnp.where(head == h, prod, 0.0), axis=-1, keepdims=True)
                             for h in range(MEM_HEADS)], axis=0)
    d4 = _stack_heads(dmo)
    ds = (p * (_dot_nt(d4, kv[:, MW:]) - delta) * SCALE).astype(BF16)
    dkv_ref[:, :MW] += _dot_tn(ds, q4)
    dkv_ref[:, MW:] += _dot_tn(p.astype(BF16), d4)
    return _unstack_heads(_dot(ds, kv[:, :MW]))


def _merge(o_refs, l_refs):
    ls = [r[...] for r in l_refs]
    m = jnp.maximum(jnp.maximum(ls[0], ls[1]), ls[2])
    es = [jnp.exp(l - m) for l in ls]
    inv = 1.0 / (es[0] + es[1] + es[2])
    ws = [e * inv for e in es]
    os_ = [r[...] for r in o_refs]
    mix = ws[0] * os_[0] + ws[1] * os_[1] + ws[2] * os_[2]
    return ws, mix


def _conv_taps(cg, u, cgp, up, first):
    a = cg * u
    ap = jnp.where(first, 0.0, cgp * up)
    row = lax.broadcasted_iota(jnp.int32, a.shape, 0)
    a1 = jnp.where(row == 0, ap[7:8, :], pltpu.roll(a, 1, 0))
    a2 = jnp.where(row == 0, ap[6:7, :], jnp.where(row == 1, ap[7:8, :], pltpu.roll(a, 2, 0)))
    return a, a1, a2


def _rope_tables(posf, after):
    half = 8
    invf = np.float32(ROPE_THETA) ** (-np.arange(half, dtype=np.float32) * np.float32(2.0 / 16))
    lane = np.arange(128)
    table = np.where((lane % HD) < 16, invf[lane % half], 0.0).astype(np.float32)[None, :]

    def body(pos_ref, invf_ref, c_ref, s1_ref, s2_ref):
        ang = pos_ref[...] * invf_ref[...]
        jm = lax.broadcasted_iota(jnp.int32, ang.shape, 1) & (HD - 1)
        cs = jnp.cos(ang)
        sn = jnp.sin(ang)
        c_ref[...] = jnp.where(jm < 16, cs, 1.0)
        s1_ref[...] = jnp.where(jm < 8, -sn, 0.0)
        s2_ref[...] = jnp.where((jm >= 8) & (jm < 16), sn, 0.0)

    out = _sds((S, 128), F32)
    return pl.pallas_call(
        functools.partial(_skip_arg, body, 2), name="rope_tables", grid=(NT,),
        in_specs=[_rows(1), _full((1, 128)), pl.BlockSpec(memory_space=pl.ANY)],
        out_specs=[_rows(128)] * 3, out_shape=[out] * 3,
        compiler_params=_params(("parallel",)),
    )(posf, jnp.asarray(table), after)


def _in_proj_a(x, g0, w_in, c, s1, s2, after):
    def body(x_ref, g_ref, w_ref, c_ref, s1_ref, s2_ref, hn_ref, q_ref, k_ref, v_ref, qm_ref, z_ref, proj):
        xf = x_ref[...]
        hn = xf * lax.rsqrt(jnp.mean(xf * xf, axis=-1, keepdims=True) + EPS) * g_ref[...]
        hb = hn.astype(BF16)
        hn_ref[...] = hb
        for s in range(4):
            proj[:, s * SH_A:(s + 1) * SH_A] = _dot(hb, w_ref[s])
        cc, a1, a2 = c_ref[...], s1_ref[...], s2_ref[...]
        for j in range(NQ // 128):
            q_ref[:, j * 128:(j + 1) * 128] = (
                _rope_fwd(proj[:, j * 128:(j + 1) * 128], cc, a1, a2) * SCALE).astype(BF16)
            k_ref[:, j * 128:(j + 1) * 128] = _rope_fwd(
                proj[:, NQ + j * 128:NQ + (j + 1) * 128], cc, a1, a2).astype(BF16)
        v_ref[...] = proj[:, 2 * NQ:3 * NQ].astype(BF16)
        qm_ref[...] = proj[:, 3 * NQ:3 * NQ + MW].astype(BF16)
        z_ref[...] = proj[:, 3 * NQ + MW:]

    return pl.pallas_call(
        functools.partial(_skip_arg, body, 6), name="in_proj_a", grid=(NT,),
        in_specs=[_rows(D), _full((1, D)), _full((4, D, SH_A)), _rows(128), _rows(128), _rows(128),
                  pl.BlockSpec(memory_space=pl.ANY)],
        out_specs=[_rows(D), _rows(NQ), _rows(NQ), _rows(NQ), _rows(MW), _rows(BR_A)],
        out_shape=[_sds((S, D), BF16), _sds((S, NQ), BF16), _sds((S, NQ), BF16), _sds((S, NQ), BF16),
                   _sds((S, MW), BF16), _sds((S, BR_A), F32)],
        scratch_shapes=[pltpu.VMEM((TM, IN_A), F32)],
        compiler_params=_params(("parallel",)),
    )(x, g0, w_in, c, s1, s2, after)


def _mem_fwd(mem, mg, wkv):
    def body(mem_ref, mg_ref, w_ref, memn_ref, kv_ref):
        mf = mem_ref[...]
        n = mf * lax.rsqrt(jnp.mean(mf * mf, axis=-1, keepdims=True) + EPS)
        for i in range(2):
            mn = (n * mg_ref[i:i + 1, :]).astype(BF16)
            memn_ref[i] = mn
            acc = _dot(mn[:, 0:NM], w_ref[0, i])
            for s in range(1, 4):
                acc += _dot(mn[:, s * NM:(s + 1) * NM], w_ref[s, i])
            kv_ref[i] = acc.astype(BF16)

    return pl.pallas_call(
        body, name="mem_fwd", grid=(1,),
        in_specs=[_full((NM, D)), _full((2, D)), _full((4, 2, NM, 2 * MW))],
        out_specs=[_full((2, NM, D)), _full((2, NM, 2 * MW))],
        out_shape=[_sds((2, NM, D), BF16), _sds((2, NM, 2 * MW), BF16)],
        compiler_params=_params(("arbitrary",)),
    )(mem, mg, wkv)


def _band_mask(j):
    qi = lax.broadcasted_iota(jnp.int32, (QBLK, 2 * QBLK), 0)
    kj = lax.broadcasted_iota(jnp.int32, (QBLK, 2 * QBLK), 1)
    dist = qi + QBLK - kj
    return (dist >= 0) & (dist <= QBLK) & ((kj >= QBLK) | (j > 0))


LANES = 128
NCHUNK = GW // LANES
FWD_UNROLL = 16
BWD_UNROLL = 16
CONV_CHUNK = 256


def _perm_matrix(d):
    n = TM // d
    p = np.zeros((TM, TM), np.float32)
    for r in range(d):
        for i in range(n):
            p[r * n + i, i * d + r] = 1.0
    return p


def _split_dot(p, x, parts):
    hi = x.astype(BF16)
    rem = x - hi.astype(F32)
    lo = rem.astype(BF16)
    both = _dot(p, jnp.concatenate([hi, lo], axis=1))
    acc = both[:, :LANES] + both[:, LANES:]
    if parts == 3:
        acc = acc + _dot(p, (rem - lo.astype(F32)).astype(BF16))
    return acc


def _pair_dot(p, a, b):
    both = _dot(p, jnp.concatenate([a, b], axis=1))
    return both[:, :LANES], both[:, LANES:]


def _tile_to_streams(y, dst, t, d):
    n, ln = TM // d, S // d
    for r in range(d):
        dst[r * ln + t * n:r * ln + (t + 1) * n, :] = y[r * n:(r + 1) * n].astype(dst.dtype)


def _tile_from_streams(src, t, d):
    n, ln = TM // d, S // d
    return jnp.concatenate([src[r * ln + t * n:r * ln + (t + 1) * n, :] for r in range(d)], axis=0)


def _head_masks():
    first = lax.broadcasted_iota(jnp.int32, (TM, LANES), 1) < HD
    return first, jnp.logical_not(first)


def _attn_fwd(q, k, v, g, after):
    d = DILATIONS[g]
    nb = S // d // QBLK
    perm = _perm_matrix(d)

    def body(q_ref, k_ref, v_ref, p_ref, pt_ref, o_ref, l_ref, ls_ref, q0, q1, ks, vs, os_):
        first, second = _head_masks()
        pm = p_ref[...]
        for t in range(NT):
            rows = slice(t * TM, (t + 1) * TM)
            if d == 1:
                qt = q_ref[rows, :].astype(F32)
            else:
                qt, kt = _pair_dot(pm, q_ref[rows, :], k_ref[rows, :])
                _tile_to_streams(kt, ks, t, d)
                _tile_to_streams(_dot(pm, v_ref[rows, :]), vs, t, d)
            _tile_to_streams(jnp.where(first, qt, 0.0), q0, t, d)
            _tile_to_streams(jnp.where(second, qt, 0.0), q1, t, d)
        kref, vref = (k_ref, v_ref) if d == 1 else (ks, vs)
        oref, lref = (o_ref, l_ref) if d == 1 else (os_, ls_ref)

        def blk(b, carry):
            r0 = pl.multiple_of(b * QBLK, QBLK)
            p0 = pl.multiple_of(jnp.maximum(b - 1, 0) * QBLK, QBLK)
            kk = jnp.concatenate([kref[pl.ds(p0, QBLK), :], kref[pl.ds(r0, QBLK), :]], axis=0)
            vv = jnp.concatenate([vref[pl.ds(p0, QBLK), :], vref[pl.ds(r0, QBLK), :]], axis=0)
            valid = _band_mask(b & (nb - 1))
            acc, lse = [], []
            for qh in (q0, q1):
                s = jnp.where(valid, _dot_nt(qh[pl.ds(r0, QBLK), :], kk), NEG)
                m = jnp.max(s, axis=-1, keepdims=True)
                e = jnp.exp(s - m)
                l = jnp.sum(e, axis=-1, keepdims=True)
                acc.append(_dot(e.astype(BF16), vv) * (1.0 / l))
                lse.append(m + jnp.log(l))
            f = first[:QBLK]
            oref[pl.ds(r0, QBLK), :] = jnp.where(f, acc[0], acc[1])
            lref[pl.ds(r0, QBLK), :] = jnp.where(f, lse[0], lse[1])
            return carry

        lax.fori_loop(0, S // QBLK, blk, 0, unroll=FWD_UNROLL)
        if d > 1:
            ptm = pt_ref[...]
            for t in range(NT):
                rows = slice(t * TM, (t + 1) * TM)
                o_ref[rows, :] = _split_dot(ptm, _tile_from_streams(os_, t, d), 2)
                l_ref[rows, :] = _split_dot(ptm, _tile_from_streams(ls_ref, t, d), 3)

    qkv_spec = pl.BlockSpec((S, LANES), lambda c: (0, g * NCHUNK + c))
    out_spec = pl.BlockSpec((S, LANES), lambda c: (0, c))
    n_out = 2 if d == 1 else 3
    inner = body if d > 1 else functools.partial(_drop_arg, body, 7)
    outs = pl.pallas_call(
        functools.partial(_skip_arg, inner, 5), name=f"attn_fwd_g{g}", grid=(NCHUNK,),
        in_specs=[qkv_spec] * 3 + [_full((TM, TM))] * 2 + [pl.BlockSpec(memory_space=pl.ANY)],
        out_specs=[out_spec] * n_out, out_shape=[_sds((S, GW), F32)] * n_out,
        scratch_shapes=[pltpu.VMEM((S, LANES), BF16)] * 4 + [pltpu.VMEM((S, LANES), F32)],
        compiler_params=_params(("parallel",)),
    )(q, k, v, jnp.asarray(perm, BF16), jnp.asarray(perm.T, BF16), after)
    return (outs[0], outs[1], outs[1]) if d == 1 else tuple(outs)


def _drop_arg(body, pos, *refs):
    return body(*refs[:pos], None, *refs[pos:])


def _attn_out(os_, ls, qm, kv0, z, x, w_out):
    def body(o0, o1, o2, l0, l1, l2, qm_ref, kv_ref, z_ref, x_ref, w_ref, h_ref, ybuf):
        _, mix = _merge((o0, o1, o2), (l0, l1, l2))
        sz, _ = _silu_parts(z_ref[...])
        ybuf[:, :GW] = (mix * sz[:, :GW]).astype(BF16)
        _, mo, _ = _mem_attn(qm_ref[...], kv_ref[...])
        ybuf[:, GW:] = (mo * sz[:, GW:]).astype(BF16)
        yb = ybuf[...]
        for s in range(4):
            cs = slice(s * SH_O, (s + 1) * SH_O)
            h_ref[:, cs] = x_ref[:, cs] + _dot(yb, w_ref[s])

    return pl.pallas_call(
        body, name="attn_out", grid=(NT,),
        in_specs=[_rows(GW)] * 6 + [_rows(MW), _full((NM, 2 * MW)), _rows(BR_A), _rows(D), _full((4, BR_A, SH_O))],
        out_specs=_rows(D), out_shape=_sds((S, D), F32),
        scratch_shapes=[pltpu.VMEM((TM, BR_A), BF16)],
        compiler_params=_params(("parallel",)),
    )(*os_, *ls, qm, kv0, z, x, w_out)


def _in_proj_b(h1, g1, w_in):
    def body(x_ref, g_ref, w_ref, hn_ref, bg_ref, cg_ref, u_ref, qm_ref, z_ref, proj):
        xf = x_ref[...]
        hn = xf * lax.rsqrt(jnp.mean(xf * xf, axis=-1, keepdims=True) + EPS) * g_ref[...]
        hb = hn.astype(BF16)
        hn_ref[...] = hb
        for s in range(4):
            proj[:, s * SH_B:(s + 1) * SH_B] = _dot(hb, w_ref[s])
        bg_ref[...] = proj[:, :D]
        cg_ref[...] = proj[:, D:2 * D]
        u_ref[...] = proj[:, 2 * D:3 * D]
        qm_ref[...] = proj[:, 3 * D:3 * D + MW].astype(BF16)
        z_ref[...] = proj[:, 3 * D + MW:]

    return pl.pallas_call(
        body, name="in_proj_b", grid=(NT,),
        in_specs=[_rows(D), _full((1, D)), _full((4, D, SH_B))],
        out_specs=[_rows(D), _rows(D), _rows(D), _rows(D), _rows(MW), _rows(BR_B)],
        out_shape=[_sds((S, D), BF16), _sds((S, D), F32), _sds((S, D), F32), _sds((S, D), F32),
                   _sds((S, MW), BF16), _sds((S, BR_B), F32)],
        scratch_shapes=[pltpu.VMEM((TM, IN_B), F32)],
        compiler_params=_params(("parallel",)),
    )(h1, g1, w_in)


def _prev8(width):
    return pl.BlockSpec((8, width), lambda i: (jnp.maximum(i * (TM // 8) - 1, 0), 0))


def _conv_out_loss(bg, cg, u, cw, qm, kv1, z, h1, w_out, fg, tgt):
    def body(bg_ref, cg_ref, u_ref, cgp_ref, up_ref, cw_ref, qm_ref, kv_ref, z_ref, h_ref, w_ref, fg_ref, t_ref,
             dh_ref, loss_ref, dfg_ref, ybuf):
        i = pl.program_id(0)
        a, a1, a2 = _conv_taps(cg_ref[...], u_ref[...], cgp_ref[...], up_ref[...], i == 0)
        conv = cw_ref[0:1, :] * a2 + cw_ref[1:2, :] * a1 + cw_ref[2:3, :] * a
        sz, _ = _silu_parts(z_ref[...])
        ybuf[:, :D] = (bg_ref[...] * conv * sz[:, :D]).astype(BF16)
        _, mo, _ = _mem_attn(qm_ref[...], kv_ref[...])
        ybuf[:, D:] = (mo * sz[:, D:]).astype(BF16)
        h2 = h_ref[...] + _dot(ybuf[...], w_ref[...])
        rstd = lax.rsqrt(jnp.mean(h2 * h2, axis=-1, keepdims=True) + EPS)
        n = h2 * rstd
        fgv = fg_ref[...]
        err = n * fgv - t_ref[...]
        dout = err * (1.0 / D)
        dn = dout * fgv
        dh_ref[...] = rstd * (dn - n * jnp.mean(dn * n, axis=-1, keepdims=True))

        @pl.when(i == 0)
        def _():
            loss_ref[...] = jnp.zeros_like(loss_ref)
            dfg_ref[...] = jnp.zeros_like(dfg_ref)

        loss_ref[...] += jnp.sum(err * err) * (0.5 / D)
        dfg_ref[...] += jnp.sum(dout * n, axis=0, keepdims=True)

    return pl.pallas_call(
        body, name="conv_out_loss", grid=(NT,),
        in_specs=[_rows(D), _rows(D), _rows(D), _prev8(D), _prev8(D), _full((8, D)), _rows(MW),
                  _full((NM, 2 * MW)), _rows(BR_B), _rows(D), _full((BR_B, D)), _full((1, D)), _rows(D)],
        out_specs=[_rows(D), _full((1, 128)), _full((1, D))],
        out_shape=[_sds((S, D), F32), _sds((1, 128), F32), _sds((1, D), F32)],
        scratch_shapes=[pltpu.VMEM((TM, BR_B), BF16)],
        compiler_params=_params(("arbitrary",)),
    )(bg, cg, u, cg, u, cw, qm, kv1, z, h1, w_out, fg, tgt)


def _conv_bwd(dh2, bg, cg, u, cw, qm, kv1, z, w_out):
    rev = lambda i: (NT - 1 - i, 0)
    rows = lambda w: pl.BlockSpec((TM, w), rev)
    prev8 = pl.BlockSpec((8, D), lambda i: (jnp.maximum((NT - 1 - i) * (TM // 8) - 1, 0), 0))

    def body(dh_ref, bg_ref, cg_ref, u_ref, cgp_ref, up_ref, cw_ref, qm_ref, kv_ref, z_ref, w_ref,
             dproj_ref, dw_ref, dcw_ref, dkv_ref, dwb_ref, ybuf, carry):
        i = pl.program_id(0)

        @pl.when(i == 0)
        def _():
            dw_ref[...] = jnp.zeros_like(dw_ref)
            dcw_ref[...] = jnp.zeros_like(dcw_ref)
            dkv_ref[...] = jnp.zeros_like(dkv_ref)
            carry[...] = jnp.zeros_like(carry)

        dhb = dh_ref[...].astype(BF16)
        dy = _dot_nt(dhb, w_ref[...])
        kvv = kv_ref[...]
        p, mo, q4 = _mem_attn(qm_ref[...], kvv)
        szm, dszm = _silu_parts(z_ref[:, D:])
        ybuf[:, D:] = (mo * szm).astype(BF16)
        dym = dy[:, D:]
        dproj_ref[:, 3 * D + MW + D:] = (dym * mo * dszm).astype(BF16)
        first_tile = i == NT - 1
        for c in range(D // CONV_CHUNK):
            cs = slice(c * CONV_CHUNK, (c + 1) * CONV_CHUNK)
            bgv, cgv, uv = bg_ref[:, cs], cg_ref[:, cs], u_ref[:, cs]
            a, a1, a2 = _conv_taps(cgv, uv, cgp_ref[:, cs], up_ref[:, cs], first_tile)
            w0, w1, w2 = cw_ref[0:1, cs], cw_ref[1:2, cs], cw_ref[2:3, cs]
            conv = w0 * a2 + w1 * a1 + w2 * a
            mix = bgv * conv
            sz, dsz = _silu_parts(z_ref[:, cs])
            ybuf[:, cs] = (mix * sz).astype(BF16)
            dyc = dy[:, cs]
            dproj_ref[:, 3 * D + MW + c * CONV_CHUNK:3 * D + MW + (c + 1) * CONV_CHUNK] = (
                dyc * mix * dsz).astype(BF16)
            dmix = dyc * sz
            dproj_ref[:, cs] = (dmix * conv).astype(BF16)
            dc = dmix * bgv
            nxt = carry[:, cs]
            row = lax.broadcasted_iota(jnp.int32, dc.shape, 0)
            dc1 = jnp.where(row == TM - 1, nxt[0:1, :], pltpu.roll(dc, TM - 1, 0))
            dc2 = jnp.where(row == TM - 2, nxt[0:1, :],
                            jnp.where(row == TM - 1, nxt[1:2, :], pltpu.roll(dc, TM - 2, 0)))
            carry[:, cs] = dc[0:8, :]
            da = w2 * dc + w1 * dc1 + w0 * dc2
            dproj_ref[:, D + c * CONV_CHUNK:D + (c + 1) * CONV_CHUNK] = (da * uv).astype(BF16)
            dproj_ref[:, 2 * D + c * CONV_CHUNK:2 * D + (c + 1) * CONV_CHUNK] = (da * cgv).astype(BF16)
            dcw_ref[0:1, cs] += jnp.sum(dc * a2, axis=0, keepdims=True)
            dcw_ref[1:2, cs] += jnp.sum(dc * a1, axis=0, keepdims=True)
            dcw_ref[2:3, cs] += jnp.sum(dc * a, axis=0, keepdims=True)
        dw_ref[...] += _dot_tn(ybuf[...], dhb)
        dproj_ref[:, 3 * D:3 * D + MW] = _mem_attn_bwd(dym * szm, p, mo, q4, kvv, dkv_ref).astype(BF16)

        @pl.when(i == NT - 1)
        def _():
            dwb_ref[...] = dw_ref[...].astype(BF16)

    return pl.pallas_call(
        body, name="conv_bwd", grid=(NT,),
        in_specs=[rows(D), rows(D), rows(D), rows(D), prev8, prev8, _full((8, D)), rows(MW),
                  _full((NM, 2 * MW)), rows(BR_B), _full((BR_B, D))],
        out_specs=[rows(IN_B), _full((BR_B, D)), _full((8, D)), _full((NM, 2 * MW)), _full((BR_B, D))],
        out_shape=[_sds((S, IN_B), BF16), _sds((BR_B, D), F32), _sds((8, D), F32), _sds((NM, 2 * MW), F32),
                   _sds((BR_B, D), BF16)],
        scratch_shapes=[pltpu.VMEM((TM, BR_B), BF16), pltpu.VMEM((8, D), F32)],
        compiler_params=_params(("arbitrary",)),
    )(dh2, bg, cg, u, cg, u, cw, qm, kv1, z, w_out)


def _in_proj_bwd(dproj, w_in, xin, g, dres, after, width, name):
    sh = width // 4

    def body(dp_ref, w_ref, x_ref, g_ref, dr_ref, dx_ref, dg_ref):
        i = pl.program_id(0)
        dhn = _dot_nt(dp_ref[:, 0:sh], w_ref[0])
        for s in range(1, 4):
            dhn += _dot_nt(dp_ref[:, s * sh:(s + 1) * sh], w_ref[s])
        xf = x_ref[...]
        rstd = lax.rsqrt(jnp.mean(xf * xf, axis=-1, keepdims=True) + EPS)
        n = xf * rstd
        dn = dhn * g_ref[...]
        dx_ref[...] = dr_ref[...] + rstd * (dn - n * jnp.mean(dn * n, axis=-1, keepdims=True))

        @pl.when(i == 0)
        def _():
            dg_ref[...] = jnp.zeros_like(dg_ref)

        dg_ref[...] += jnp.sum(dhn * n, axis=0, keepdims=True)

    return pl.pallas_call(
        functools.partial(_skip_arg, body, 5), name=name, grid=(NT,),
        in_specs=[_rows(width), _full((4, D, sh)), _rows(D), _full((1, D)), _rows(D), pl.BlockSpec(memory_space=pl.ANY)],
        out_specs=[_rows(D), _full((1, D))],
        out_shape=[_sds((S, D), F32), _sds((1, D), F32)],
        compiler_params=_params(("arbitrary",)),
    )(dproj, w_in, xin, g, dres, after)


def _w_in_grad(hn, dproj, width, name):
    sh = width // 4

    def body(hn_ref, dp_ref, dw_ref, dwb_ref):
        dw = _dot_tn(hn_ref[...], dp_ref[...])
        dw_ref[0] = dw
        dwb_ref[0] = dw.astype(BF16)

    spec = pl.BlockSpec((1, D, sh), lambda s: (s, 0, 0))
    return pl.pallas_call(
        body, name=name, grid=(4,),
        in_specs=[_full((S, D)), pl.BlockSpec((S, sh), lambda s: (0, s))],
        out_specs=[spec, spec], out_shape=[_sds((4, D, sh), F32), _sds((4, D, sh), BF16)],
        compiler_params=_params(("parallel",)),
    )(hn, dproj)


def _attn_out_bwd(dh1, os_, ls, qm, kv0, z, w_out, after):
    ones_bd = np.kron(np.eye(GW // HD, dtype=np.float32), np.ones((HD, HD), np.float32))

    def body(dh_ref, o0, o1, o2, l0, l1, l2, qm_ref, kv_ref, z_ref, w_ref, bd_ref,
             do0, do1, do2, dd0, dd1, dd2, dqm_ref, dz_ref, dw_ref, dkv_ref, dwb_ref, ybuf):
        i = pl.program_id(0)

        @pl.when(i == 0)
        def _():
            dw_ref[...] = jnp.zeros_like(dw_ref)
            dkv_ref[...] = jnp.zeros_like(dkv_ref)

        ws, mix = _merge((o0, o1, o2), (l0, l1, l2))
        sz, dsz = _silu_parts(z_ref[...])
        kvv = kv_ref[...]
        p, mo, q4 = _mem_attn(qm_ref[...], kvv)
        ybuf[:, :GW] = (mix * sz[:, :GW]).astype(BF16)
        ybuf[:, GW:] = (mo * sz[:, GW:]).astype(BF16)
        yb = ybuf[...]
        dh = dh_ref[...]
        dy = None
        for s in range(4):
            dhb = dh[:, s * SH_O:(s + 1) * SH_O].astype(BF16)
            dw_ref[s] += _dot_tn(yb, dhb)
            part = _dot_nt(dhb, w_ref[s])
            dy = part if dy is None else dy + part
        dcat = dy * sz
        dz_ref[:, :GW] = (dy[:, :GW] * mix * dsz[:, :GW]).astype(BF16)
        dz_ref[:, GW:] = (dy[:, GW:] * mo * dsz[:, GW:]).astype(BF16)
        dmix = dcat[:, :GW]
        prod = dmix * mix
        hi = prod.astype(BF16)
        lo = (prod - hi.astype(F32)).astype(BF16)
        bd = bd_ref[...]
        tot = _dot(hi, bd) + _dot(lo, bd)
        for w, do_ref, dd_ref in zip(ws, (do0, do1, do2), (dd0, dd1, dd2)):
            do_ref[...] = (w * dmix).astype(BF16)
            dd_ref[...] = w * tot

        dqm_ref[...] = _mem_attn_bwd(dcat[:, GW:], p, mo, q4, kvv, dkv_ref).astype(BF16)

        @pl.when(i == NT - 1)
        def _():
            dwb_ref[...] = dw_ref[...].astype(BF16)

    return pl.pallas_call(
        functools.partial(_skip_arg, body, 12), name="attn_out_bwd", grid=(NT,),
        in_specs=[_rows(D)] + [_rows(GW)] * 6 + [_rows(MW), _full((NM, 2 * MW)), _rows(BR_A),
                                                   _full((4, BR_A, SH_O)), _full((GW, GW)),
                                                   pl.BlockSpec(memory_space=pl.ANY)],
        out_specs=[_rows(GW)] * 6 + [_rows(MW), _rows(BR_A), _full((4, BR_A, SH_O)), _full((NM, 2 * MW)),
                                     _full((4, BR_A, SH_O))],
        out_shape=[_sds((S, GW), BF16)] * 3 + [_sds((S, GW), F32)] * 3 + [
            _sds((S, MW), BF16), _sds((S, BR_A), BF16), _sds((4, BR_A, SH_O), F32), _sds((NM, 2 * MW), F32),
            _sds((4, BR_A, SH_O), BF16)],
        scratch_shapes=[pltpu.VMEM((TM, BR_A), BF16)],
        compiler_params=_params(("arbitrary",)),
    )(dh1, *os_, *ls, qm, kv0, z, w_out, jnp.asarray(ones_bd, dtype=BF16), after)


def _attn_bwd(q, k, v, do, lse_s, dd, g):
    d = DILATIONS[g]
    nb = S // d // QBLK
    perm = _perm_matrix(d)

    def body(q_ref, k_ref, v_ref, do_ref, l_ref, dd_ref, p_ref, pt_ref, dq_ref, dk_ref, dv_ref,
             q0, q1, g0, g1, ks, vs, dds, dqs, dks, dvs):
        first, second = _head_masks()
        pm = p_ref[...]
        for t in range(NT):
            rows = slice(t * TM, (t + 1) * TM)
            if d == 1:
                qt = q_ref[rows, :].astype(F32)
                gt = do_ref[rows, :].astype(F32)
            else:
                qt, gt = _pair_dot(pm, q_ref[rows, :], do_ref[rows, :])
                kt, vt = _pair_dot(pm, k_ref[rows, :], v_ref[rows, :])
                _tile_to_streams(kt, ks, t, d)
                _tile_to_streams(vt, vs, t, d)
                _tile_to_streams(_split_dot(pm, dd_ref[rows, :], 2), dds, t, d)
            _tile_to_streams(jnp.where(first, qt, 0.0), q0, t, d)
            _tile_to_streams(jnp.where(second, qt, 0.0), q1, t, d)
            _tile_to_streams(jnp.where(first, gt, 0.0), g0, t, d)
            _tile_to_streams(jnp.where(second, gt, 0.0), g1, t, d)
        kref, vref, ddref = (k_ref, v_ref, dd_ref) if d == 1 else (ks, vs, dds)
        dqref, dkref, dvref = dqs, dks, dvs
        dkref[...] = jnp.zeros_like(dkref)
        dvref[...] = jnp.zeros_like(dvref)

        def blk(b, carry):
            r0 = pl.multiple_of(b * QBLK, QBLK)
            p0 = pl.multiple_of(jnp.maximum(b - 1, 0) * QBLK, QBLK)
            kk = jnp.concatenate([kref[pl.ds(p0, QBLK), :], kref[pl.ds(r0, QBLK), :]], axis=0)
            vv = jnp.concatenate([vref[pl.ds(p0, QBLK), :], vref[pl.ds(r0, QBLK), :]], axis=0)
            lb = l_ref[pl.ds(r0, QBLK), :]
            ddb = ddref[pl.ds(r0, QBLK), :]
            lcol = jnp.concatenate([lb[:, 0:1], lb[:, HD:HD + 1]], axis=0)
            dcol = jnp.concatenate([ddb[:, 0:1], ddb[:, HD:HD + 1]], axis=0)
            valid = _band_mask(b & (nb - 1))
            valid2 = jnp.concatenate([valid, valid], axis=0)
            qq = jnp.concatenate([q0[pl.ds(r0, QBLK), :], q1[pl.ds(r0, QBLK), :]], axis=0)
            gg = jnp.concatenate([g0[pl.ds(r0, QBLK), :], g1[pl.ds(r0, QBLK), :]], axis=0)
            p = jnp.where(valid2, jnp.exp(_dot_nt(qq, kk) - lcol), 0.0)
            ds = (p * (_dot_nt(gg, vv) - dcol)).astype(BF16)
            dq2 = _dot(ds, kk)
            dqref[pl.ds(r0, QBLK), :] = jnp.where(first[:QBLK], dq2[:QBLK], dq2[QBLK:])
            dkk = _dot_tn(ds, qq)
            dvv = _dot_tn(p.astype(BF16), gg)
            dkref[pl.ds(p0, QBLK), :] += dkk[:QBLK]
            dkref[pl.ds(r0, QBLK), :] += dkk[QBLK:]
            dvref[pl.ds(p0, QBLK), :] += dvv[:QBLK]
            dvref[pl.ds(r0, QBLK), :] += dvv[QBLK:]
            return carry

        lax.fori_loop(0, S // QBLK, blk, 0, unroll=BWD_UNROLL)

        ptm = pt_ref[...] if d > 1 else None
        for t in range(NT):
            rows = slice(t * TM, (t + 1) * TM)
            if d == 1:
                dq_ref[rows, :] = dqs[rows, :].astype(BF16)
                dk_ref[rows, :] = dks[rows, :].astype(BF16)
                dv_ref[rows, :] = dvs[rows, :].astype(BF16)
            else:
                tq, tk = _pair_dot(ptm, _tile_from_streams(dqs, t, d).astype(BF16),
                                   _tile_from_streams(dks, t, d).astype(BF16))
                dq_ref[rows, :] = tq.astype(BF16)
                dk_ref[rows, :] = tk.astype(BF16)
                dv_ref[rows, :] = _dot(ptm, _tile_from_streams(dvs, t, d).astype(BF16)).astype(BF16)

    qkv_spec = pl.BlockSpec((S, LANES), lambda c: (0, g * NCHUNK + c))
    one_spec = pl.BlockSpec((S, LANES), lambda c: (0, c))
    return pl.pallas_call(
        body, name=f"attn_bwd_g{g}", grid=(NCHUNK,),
        in_specs=[qkv_spec] * 3 + [one_spec] * 3 + [_full((TM, TM))] * 2, out_specs=[one_spec] * 3,
        out_shape=[_sds((S, GW), BF16)] * 3,
        scratch_shapes=[pltpu.VMEM((S, LANES), BF16)] * 6 + [pltpu.VMEM((S, LANES), F32)] * 4,
        compiler_params=_params(("parallel",)),
    )(q, k, v, do, lse_s, dd, jnp.asarray(perm, BF16), jnp.asarray(perm.T, BF16))


def _qkv_bwd(dqs, dks, dvs, dqm, dz, c, s1, s2):
    def body(q0, q1, q2, k0, k1, k2, v0, v1, v2, dqm_ref, dz_ref, c_ref, s1_ref, s2_ref, dp_ref):
        cc, a1, a2 = c_ref[...], s1_ref[...], s2_ref[...]
        for g, (qr, kr, vr) in enumerate(((q0, k0, v0), (q1, k1, v1), (q2, k2, v2))):
            for j in range(GW // 128):
                ls_ = slice(j * 128, (j + 1) * 128)
                c0 = g * GW + j * 128
                dp_ref[:, c0:c0 + 128] = (_rope_bwd(qr[:, ls_].astype(F32), cc, a1, a2) * SCALE).astype(BF16)
                dp_ref[:, NQ + c0:NQ + c0 + 128] = _rope_bwd(kr[:, ls_].astype(F32), cc, a1, a2).astype(BF16)
            dp_ref[:, 2 * NQ + g * GW:2 * NQ + (g + 1) * GW] = vr[...]
        dp_ref[:, 3 * NQ:3 * NQ + MW] = dqm_ref[...]
        dp_ref[:, 3 * NQ + MW:] = dz_ref[...]

    return pl.pallas_call(
        body, name="qkv_bwd", grid=(NT,),
        in_specs=[_rows(GW)] * 9 + [_rows(MW), _rows(BR_A), _rows(128), _rows(128), _rows(128)],
        out_specs=_rows(IN_A), out_shape=_sds((S, IN_A), BF16),
        compiler_params=_params(("parallel",)),
    )(*dqs, *dks, *dvs, dqm, dz, c, s1, s2)


def _mem_bwd(mem, mg, memn, wkv, dkv0, dkv1):
    def body(mem_ref, mg_ref, memn_ref, w_ref, d0_ref, d1_ref, dw_ref, dwb_ref, dg_ref):
        mf = mem_ref[...]
        n = mf * lax.rsqrt(jnp.mean(mf * mf, axis=-1, keepdims=True) + EPS)
        for i, d_ref in enumerate((d0_ref, d1_ref)):
            dkv = d_ref[...].astype(BF16)
            mn = memn_ref[i]
            for s in range(4):
                cs = slice(s * NM, (s + 1) * NM)
                dw = _dot_tn(mn[:, cs], dkv)
                dw_ref[s, i] = dw
                dwb_ref[s, i] = dw.astype(BF16)
                dmn = _dot_nt(dkv, w_ref[s, i])
                dg_ref[i:i + 1, cs] = jnp.sum(dmn * n[:, cs], axis=0, keepdims=True)

    return pl.pallas_call(
        body, name="mem_bwd", grid=(1,),
        in_specs=[_full((NM, D)), _full((2, D)), _full((2, NM, D)), _full((4, 2, NM, 2 * MW)),
                  _full((NM, 2 * MW)), _full((NM, 2 * MW))],
        out_specs=[_full((4, 2, NM, 2 * MW)), _full((4, 2, NM, 2 * MW)), _full((2, D))],
        out_shape=[_sds((4, 2, NM, 2 * MW), F32), _sds((4, 2, NM, 2 * MW), BF16), _sds((2, D), F32)],
        compiler_params=_params(("arbitrary",)),
    )(mem, mg, memn, wkv, dkv0, dkv1)


MESH = pl.DeviceIdType.MESH
ANY = pl.BlockSpec(memory_space=pl.ANY)
BIG = (("wkv", 2, NM, 2 * MW), ("w_in_a", 1, D, SH_A), ("w_out_a", 1, BR_A, SH_O),
       ("w_in_b", 1, D, SH_B), ("w_out_b", 1, BR_B // 4, D))
NBIG = len(BIG)
CW_ROWS = 8


def _place():
    x, y, c = lax.axis_index("x"), lax.axis_index("y"), lax.axis_index("c")
    chips = ((1 - x, y), (x, 1 - y), (1 - x, 1 - y))
    return x, y, c, chips


def _remote(src, dst, ssem, rsem, dev):
    return pltpu.make_async_remote_copy(src_ref=src, dst_ref=dst, send_sem=ssem, recv_sem=rsem,
                                        device_id=dev, device_id_type=MESH)


def _cast_weights(place, ws, after, idx, name):
    nblk = 4
    n = len(idx)
    dims = [BIG[w][1:] for w in idx]

    def body(pref, *refs):
        for i in range(n):
            refs[n + 1 + i][0] = refs[i][...].astype(BF16)

    grid_spec = pltpu.PrefetchScalarGridSpec(
        num_scalar_prefetch=1, grid=(nblk,),
        in_specs=[pl.BlockSpec((k, r // nblk, cdim), lambda i, pref: (0, i, 0)) for k, r, cdim in dims]
        + [pl.BlockSpec(memory_space=pl.ANY)],
        out_specs=[pl.BlockSpec((1, k, r // nblk, cdim), lambda i, pref: (pref[1], 0, i, 0)) for k, r, cdim in dims])
    return pl.pallas_call(
        body, name=name, grid_spec=grid_spec,
        out_shape=[_sds((4, k, r, cdim), BF16) for k, r, cdim in dims],
        compiler_params=_params(("parallel",)),
    )(place, *ws, after)


LAYER_A = (0, 1, 2)
LAYER_B = (3, 4)
HBM = pl.BlockSpec(memory_space=pltpu.HBM)
SEM = pl.BlockSpec(memory_space=pltpu.SEMAPHORE)
EFFECT = pltpu.SideEffectType.DATAFLOW_SIDE_EFFECTING
TOKEN = (8, 128)


def _half(ref, w, which):
    h = BIG[w][2] // 2
    return ref.at[:, pl.ds(which * h, h), :]


def _skip_arg(body, pos, *refs):
    return body(*refs[:pos], *refs[pos + 1:])


def _gather_start(wb, after, idx, name):
    n = len(idx)

    def body(*refs):
        src = refs[:n]
        send_sems, recv_sems = refs[n + 1], refs[n + 2]
        token = refs[2 * n + 3]
        x, y, c, chips = _place()
        me = 2 * x + y
        for j, (px, py) in enumerate(chips):
            for i in range(n):
                mine = _half(src[i].at[me], idx[i], c)
                _remote(mine, mine, send_sems.at[j * n + i], recv_sems.at[j * n + i], (px, py, c)).start()
        token[...] = jnp.zeros(TOKEN, F32)

    outs = pl.pallas_call(
        body, name=name, in_specs=[HBM] * n + [ANY],
        out_specs=(SEM, SEM) + (HBM,) * n + (pl.BlockSpec(memory_space=pltpu.VMEM),),
        out_shape=(pltpu.SemaphoreType.DMA((3 * n,)), pltpu.SemaphoreType.DMA((3 * n,)))
        + tuple(pltpu.HBM(w.shape, w.dtype) for w in wb) + (_sds(TOKEN, F32),),
        input_output_aliases={i: 2 + i for i in range(n)},
        compiler_params=pltpu.CompilerParams(has_side_effects=EFFECT),
    )(*[pltpu.with_memory_space_constraint(w, pltpu.HBM) for w in wb], after)
    return outs[0], outs[1], list(outs[2:2 + n]), outs[2 + n]


def _gather_wait(send_sems, recv_sems, wb, after, idx, name, started=None):
    n = len(idx)
    started = idx if started is None else started
    n_all = len(started)
    pos = [started.index(w) for w in idx]

    def body(*refs):
        buf = refs[:n]
        send_sems, recv_sems = refs[n], refs[n + 1]
        x, y, c, chips = _place()
        me = 2 * x + y
        for j, (px, py) in enumerate(chips):
            for i in range(n):
                mine = _half(buf[i].at[me], idx[i], c)
                got = _half(buf[i].at[2 * px + py], idx[i], c)
                k = j * n_all + pos[i]
                _remote(mine, mine, send_sems.at[k], recv_sems.at[k], (px, py, c)).wait_send()
                _remote(got, got, send_sems.at[k], recv_sems.at[k], (px, py, c)).wait_recv()

    outs = pl.pallas_call(
        body, name=name, in_specs=[HBM] * n + [SEM, SEM] + [ANY] * len(after), out_specs=(HBM,) * n,
        out_shape=tuple(pltpu.HBM(w.shape, w.dtype) for w in wb),
        input_output_aliases={i: i for i in range(n)},
        compiler_params=pltpu.CompilerParams(has_side_effects=EFFECT),
    )(*wb, send_sems, recv_sems, *after)
    return list(outs)


def _gather_forward(wb, idx, name):
    n = len(idx)

    def body(*refs):
        dst = refs[n:2 * n]
        send_sems, recv_sems = refs[2 * n], refs[2 * n + 1]
        x, y, c, chips = _place()
        cps = []
        for j, (px, py) in enumerate(chips):
            for i in range(n):
                got = _half(dst[i].at[2 * px + py], idx[i], c)
                cps.append(_remote(got, got, send_sems.at[j, i], recv_sems.at[j, i], (x, y, 1 - c)))
                cps[-1].start()
        for j, (px, py) in enumerate(chips):
            for i in range(n):
                got = _half(dst[i].at[2 * px + py], idx[i], 1 - c)
                _remote(got, got, send_sems.at[j, i], recv_sems.at[j, i], (x, y, 1 - c)).wait_recv()
        for cp in cps:
            cp.wait_send()

    return pl.pallas_call(
        body, name=name, in_specs=[ANY] * n, out_specs=[ANY] * n, out_shape=[_sds(w.shape, BF16) for w in wb],
        input_output_aliases={i: i for i in range(n)},
        scratch_shapes=[pltpu.SemaphoreType.DMA((3, n)), pltpu.SemaphoreType.DMA((3, n))],
    )(*wb)


def _forward_start(wb, cw, after, idx, name):
    n = len(idx)
    m = n if cw is None else n + 2

    def body(*refs):
        buf = refs[:n]
        send_sems, recv_sems = refs[m + 1], refs[m + 2]
        token = refs[2 * m + 3]
        x, y, c, chips = _place()
        for j, (px, py) in enumerate(chips):
            for i in range(n):
                got = _half(buf[i].at[2 * px + py], idx[i], c)
                _remote(got, got, send_sems.at[j * (n + 1) + i], recv_sems.at[j * (n + 1) + i], (x, y, 1 - c)).start()
            if cw is not None:
                _remote(refs[n], refs[n + 1].at[2 * x + y], send_sems.at[j * (n + 1) + n],
                        recv_sems.at[j * (n + 1) + n], (px, py, c)).start()
        token[...] = jnp.zeros(TOKEN, F32)

    arrays = list(wb) if cw is None else list(wb) + [cw, lax.empty((4, CW_ROWS, SH_O), F32)]
    outs = pl.pallas_call(
        body, name=name, in_specs=[HBM] * m + [ANY],
        out_specs=(SEM, SEM) + (HBM,) * m + (pl.BlockSpec(memory_space=pltpu.VMEM),),
        out_shape=(pltpu.SemaphoreType.DMA((3 * (n + 1),)), pltpu.SemaphoreType.DMA((3 * (n + 1),)))
        + tuple(pltpu.HBM(a.shape, a.dtype) for a in arrays) + (_sds(TOKEN, F32),),
        input_output_aliases={i: 2 + i for i in range(m)},
        compiler_params=pltpu.CompilerParams(has_side_effects=EFFECT),
    )(*[pltpu.with_memory_space_constraint(a, pltpu.HBM) for a in arrays], after)
    return outs[0], outs[1], list(outs[2:2 + m]), outs[2 + m]


def _forward_wait(send_sems, recv_sems, arrays, after, idx, with_cw, name):
    n = len(idx)
    m = len(arrays)

    def body(*refs):
        buf = refs[:n]
        send_sems, recv_sems = refs[m], refs[m + 1]
        x, y, c, chips = _place()
        for j, (px, py) in enumerate(chips):
            for i in range(n):
                sent = _half(buf[i].at[2 * px + py], idx[i], c)
                got = _half(buf[i].at[2 * px + py], idx[i], 1 - c)
                k = j * (n + 1) + i
                _remote(sent, sent, send_sems.at[k], recv_sems.at[k], (x, y, 1 - c)).wait_send()
                _remote(got, got, send_sems.at[k], recv_sems.at[k], (x, y, 1 - c)).wait_recv()
            if with_cw:
                k = j * (n + 1) + n
                theirs = refs[n + 1].at[2 * px + py]
                _remote(refs[n], theirs, send_sems.at[k], recv_sems.at[k], (px, py, c)).wait_send()
                _remote(refs[n], theirs, send_sems.at[k], recv_sems.at[k], (px, py, c)).wait_recv()

    outs = pl.pallas_call(
        body, name=name, in_specs=[HBM] * m + [SEM, SEM] + [ANY] * len(after), out_specs=(HBM,) * m,
        out_shape=tuple(pltpu.HBM(a.shape, a.dtype) for a in arrays),
        input_output_aliases={i: i for i in range(m)},
        compiler_params=pltpu.CompilerParams(has_side_effects=EFFECT),
    )(*arrays, send_sems, recv_sems, *after)
    return list(outs)


def _pair_exchange(gs, idx, name):
    n = len(idx)

    def body(*refs):
        src, dst = refs[:n], refs[n:2 * n]
        send_sems, recv_sems = refs[2 * n:]
        x, y, c, _ = _place()
        cps = []
        for i in range(n):
            h = BIG[idx[i]][2] // 2
            cps.append(_remote(src[i].at[:, :, pl.ds((1 - c) * h, h), :], dst[i], send_sems.at[i], recv_sems.at[i],
                               (x, y, 1 - c)))
            cps[-1].start()
        for cp in cps:
            cp.wait()

    return pl.pallas_call(
        body, name=name, in_specs=[ANY] * n, out_specs=[ANY] * n,
        out_shape=[_sds((4, BIG[w][1], BIG[w][2] // 2, BIG[w][3]), BF16) for w in idx],
        scratch_shapes=[pltpu.SemaphoreType.DMA((n,)), pltpu.SemaphoreType.DMA((n,))],
    )(*gs)


def _pair_start(gs, idx, name):
    n = len(idx)

    def body(*refs):
        src, land = refs[:n], refs[n:2 * n]
        send_sems, recv_sems = refs[2 * n], refs[2 * n + 1]
        token = refs[4 * n + 2]
        x, y, c, _ = _place()
        for i in range(n):
            h = BIG[idx[i]][2] // 2
            _remote(src[i].at[:, :, pl.ds((1 - c) * h, h), :], land[i], send_sems.at[i], recv_sems.at[i],
                    (x, y, 1 - c)).start()
        token[...] = jnp.zeros(TOKEN, F32)

    lands = [lax.empty((4, BIG[w][1], BIG[w][2] // 2, BIG[w][3]), BF16) for w in idx]
    arrays = list(gs) + lands
    outs = pl.pallas_call(
        body, name=name, in_specs=[HBM] * (2 * n),
        out_specs=(SEM, SEM) + (HBM,) * (2 * n) + (pl.BlockSpec(memory_space=pltpu.VMEM),),
        out_shape=(pltpu.SemaphoreType.DMA((n,)), pltpu.SemaphoreType.DMA((n,)))
        + tuple(pltpu.HBM(a.shape, a.dtype) for a in arrays) + (_sds(TOKEN, F32),),
        input_output_aliases={i: 2 + i for i in range(2 * n)},
        compiler_params=pltpu.CompilerParams(has_side_effects=EFFECT),
    )(*[pltpu.with_memory_space_constraint(a, pltpu.HBM) for a in arrays])
    return outs[0], outs[1], list(outs[2:2 + n]), list(outs[2 + n:2 + 2 * n]), outs[2 + 2 * n]


def _pair_wait(send_sems, recv_sems, gs, lands, after, idx, name):
    n = len(idx)

    def body(*refs):
        src, land = refs[:n], refs[n:2 * n]
        send_sems, recv_sems = refs[2 * n], refs[2 * n + 1]
        x, y, c, _ = _place()
        for i in range(n):
            h = BIG[idx[i]][2] // 2
            cp = _remote(src[i].at[:, :, pl.ds((1 - c) * h, h), :], land[i], send_sems.at[i], recv_sems.at[i],
                         (x, y, 1 - c))
            cp.wait_send()
            cp.wait_recv()

    arrays = list(gs) + list(lands)
    outs = pl.pallas_call(
        body, name=name, in_specs=[HBM] * (2 * n) + [SEM, SEM] + [ANY] * len(after), out_specs=(HBM,) * (2 * n),
        out_shape=tuple(pltpu.HBM(a.shape, a.dtype) for a in arrays),
        input_output_aliases={i: i for i in range(2 * n)},
        compiler_params=pltpu.CompilerParams(has_side_effects=EFFECT),
    )(*arrays, send_sems, recv_sems, *after)
    return list(outs[:n]), list(outs[n:])


def _pair_sums(place, gs, r1s, idx, name):
    n = len(idx)
    dims = [(BIG[w][1], BIG[w][2] // 2, BIG[w][3]) for w in idx]

    def body(pref, *refs):
        for i in range(n):
            refs[2 * n + i][...] = (refs[i][...] + refs[n + i][...].astype(F32)).astype(BF16)

    mine = [pl.BlockSpec((1, k, h, cdim), lambda s, pref: (s, 0, pref[0], 0)) for k, h, cdim in dims]
    whole = [pl.BlockSpec((1, k, h, cdim), lambda s, pref: (s, 0, 0, 0)) for k, h, cdim in dims]
    grid_spec = pltpu.PrefetchScalarGridSpec(num_scalar_prefetch=1, grid=(4,), in_specs=mine + whole, out_specs=whole)
    return pl.pallas_call(
        body, name=name, grid_spec=grid_spec, out_shape=[_sds((4, k, h, cdim), BF16) for k, h, cdim in dims],
        compiler_params=_params(("parallel",)),
    )(place, *gs, *r1s)


def _chip_start(ps, idx, name):
    n = len(idx)

    def body(*refs):
        src, land = refs[:n], refs[n:2 * n]
        send_sems, recv_sems = refs[2 * n], refs[2 * n + 1]
        token = refs[4 * n + 2]
        x, y, c, chips = _place()
        for j, (px, py) in enumerate(chips):
            for i in range(n):
                _remote(src[i].at[2 * px + py], land[i].at[j], send_sems.at[j * n + i], recv_sems.at[j * n + i],
                        (px, py, c)).start()
        token[...] = jnp.zeros(TOKEN, F32)

    lands = [lax.empty((3,) + p.shape[1:], BF16) for p in ps]
    outs = pl.pallas_call(
        body, name=name, in_specs=[HBM] * (2 * n),
        out_specs=(SEM, SEM) + (HBM,) * (2 * n) + (pl.BlockSpec(memory_space=pltpu.VMEM),),
        out_shape=(pltpu.SemaphoreType.DMA((3 * n,)), pltpu.SemaphoreType.DMA((3 * n,)))
        + tuple(pltpu.HBM(a.shape, a.dtype) for a in list(ps) + lands) + (_sds(TOKEN, F32),),
        input_output_aliases={i: 2 + i for i in range(2 * n)},
        compiler_params=pltpu.CompilerParams(has_side_effects=EFFECT),
    )(*[pltpu.with_memory_space_constraint(a, pltpu.HBM) for a in list(ps) + lands])
    return outs[0], outs[1], list(outs[2:2 + n]), list(outs[2 + n:2 + 2 * n]), outs[2 + 2 * n]


def _chip_wait(send_sems, recv_sems, ps, lands, after, idx, name):
    n = len(idx)

    def body(*refs):
        src, land = refs[:n], refs[n:2 * n]
        send_sems, recv_sems = refs[2 * n], refs[2 * n + 1]
        x, y, c, chips = _place()
        for j, (px, py) in enumerate(chips):
            for i in range(n):
                cp = _remote(src[i].at[2 * px + py], land[i].at[j], send_sems.at[j * n + i], recv_sems.at[j * n + i],
                             (px, py, c))
                cp.wait_send()
                cp.wait_recv()

    arrays = list(ps) + list(lands)
    outs = pl.pallas_call(
        body, name=name, in_specs=[HBM] * (2 * n) + [SEM, SEM] + [ANY] * len(after), out_specs=(HBM,) * (2 * n),
        out_shape=tuple(pltpu.HBM(a.shape, a.dtype) for a in arrays),
        input_output_aliases={i: i for i in range(2 * n)},
        compiler_params=pltpu.CompilerParams(has_side_effects=EFFECT),
    )(*arrays, send_sems, recv_sems, *after)
    return list(outs[n:])


def _chip_sums(place, gs, r1s, r2s, idx, name):
    n = len(idx)
    dims = [(BIG[w][1], BIG[w][2] // 4, BIG[w][3]) for w in idx]

    def body(pref, *refs):
        for i in range(n):
            acc = refs[i][0] + refs[n + i][0].astype(F32)
            for j in range(3):
                acc = acc + refs[2 * n + i][j].astype(F32)
            refs[3 * n + i][...] = acc

    in_specs = ([pl.BlockSpec((1, k, q, cdim), lambda t, pref: (pref[1], 0, pref[0] * 2 + t, 0)) for k, q, cdim in dims]
                + [pl.BlockSpec((1, k, q, cdim), lambda t, pref: (pref[1], 0, t, 0)) for k, q, cdim in dims]
                + [pl.BlockSpec((3, k, q, cdim), lambda t, pref: (0, 0, t, 0)) for k, q, cdim in dims])
    out_specs = [pl.BlockSpec((k, q, cdim), lambda t, pref: (0, pref[0] * 2 + t, 0)) for k, q, cdim in dims]
    grid_spec = pltpu.PrefetchScalarGridSpec(num_scalar_prefetch=1, grid=(2,), in_specs=in_specs, out_specs=out_specs)
    return pl.pallas_call(
        body, name=name, grid_spec=grid_spec, out_shape=[_sds(BIG[w][1:], F32) for w in idx],
        compiler_params=_params(("parallel",)),
    )(place, *gs, *r1s, *r2s)


def _pair_gather(hs, idx, name):
    n = len(idx)

    def body(*refs):
        dst = refs[n:2 * n]
        send_sems, recv_sems = refs[2 * n:]
        x, y, c, _ = _place()
        cps = []
        for i in range(n):
            mine = _half(dst[i], idx[i], c)
            cps.append(_remote(mine, mine, send_sems.at[i], recv_sems.at[i], (x, y, 1 - c)))
            cps[-1].start()
        for i in range(n):
            theirs = _half(dst[i], idx[i], 1 - c)
            _remote(theirs, theirs, send_sems.at[i], recv_sems.at[i], (x, y, 1 - c)).wait_recv()
        for cp in cps:
            cp.wait_send()

    return pl.pallas_call(
        body, name=name, in_specs=[ANY] * n, out_specs=[ANY] * n,
        out_shape=[_sds(BIG[w][1:], F32) for w in idx],
        input_output_aliases={i: i for i in range(n)},
        scratch_shapes=[pltpu.SemaphoreType.DMA((n,)), pltpu.SemaphoreType.DMA((n,))],
    )(*hs)


SMALL_ROWS = 40


def _adamw_math(w, g, m, v):
    m = ADAM_B1 * m + (1.0 - ADAM_B1) * g
    v = ADAM_B2 * v + (1.0 - ADAM_B2) * (g * g)
    m_hat = m / (1.0 - ADAM_B1 ** ADAM_STEP)
    v_hat = v / (1.0 - ADAM_B2 ** ADAM_STEP)
    delta = -ADAM_LR * (m_hat / (jnp.sqrt(v_hat) + ADAM_EPS) + ADAM_WD * w)
    return delta, m, v


def _small_start(pack, after):
    def body(pack_ref, land_ref, after_ref, send_sems, recv_sems, pack_thru, land_thru, token):
        x, y, c, _ = _place()
        for r in range(1, 8):
            peer = (x if not r & 4 else 1 - x, y if not r & 2 else 1 - y, c if not r & 1 else 1 - c)
            _remote(pack_ref, land_ref.at[r - 1], send_sems.at[r - 1], recv_sems.at[r - 1], peer).start()
        token[...] = jnp.zeros(TOKEN, F32)

    land = lax.empty((7, SMALL_ROWS, D), F32)
    outs = pl.pallas_call(
        body, name="small_start", in_specs=[HBM, HBM, ANY],
        out_specs=(SEM, SEM, HBM, HBM, pl.BlockSpec(memory_space=pltpu.VMEM)),
        out_shape=(pltpu.SemaphoreType.DMA((7,)), pltpu.SemaphoreType.DMA((7,)), pltpu.HBM(pack.shape, F32),
                   pltpu.HBM(land.shape, F32), _sds(TOKEN, F32)),
        input_output_aliases={0: 2, 1: 3},
        compiler_params=pltpu.CompilerParams(has_side_effects=EFFECT),
    )(pltpu.with_memory_space_constraint(pack, pltpu.HBM), pltpu.with_memory_space_constraint(land, pltpu.HBM), after)
    return outs


def _small_wait(send_sems, recv_sems, pack, land, after):
    def body(pack_ref, land_ref, send_sems, recv_sems, *rest):
        x, y, c, _ = _place()
        for r in range(1, 8):
            peer = (x if not r & 4 else 1 - x, y if not r & 2 else 1 - y, c if not r & 1 else 1 - c)
            cp = _remote(pack_ref, land_ref.at[r - 1], send_sems.at[r - 1], recv_sems.at[r - 1], peer)
            cp.wait_send()
            cp.wait_recv()

    return pl.pallas_call(
        body, name="small_wait", in_specs=[HBM, HBM, SEM, SEM] + [ANY] * len(after), out_specs=(HBM, HBM),
        out_shape=(pltpu.HBM(pack.shape, F32), pltpu.HBM(land.shape, F32)),
        input_output_aliases={0: 0, 1: 1},
        compiler_params=pltpu.CompilerParams(has_side_effects=EFFECT),
    )(pack, land, send_sems, recv_sems, *after)


def _small_update(place, pack, land, ws, ms, vs):
    n = len(ws)

    def body(pref, pack_ref, land_ref, *refs):
        chip = pref[1]
        me = 2 * chip + pref[0]
        own = pack_ref[...]
        tot = None
        for dev in range(8):
            r = jnp.bitwise_xor(me, dev)
            term = jnp.where(r == 0, own, land_ref[jnp.maximum(r - 1, 0)])
            tot = term if tot is None else tot + term
        out, buf = refs[3 * n:-1], refs[-1]
        buf[...] = tot
        g_conv = jnp.zeros((3, SH_O), F32)
        for s in range(4):
            g_conv = g_conv + jnp.where(chip == s, buf[24:27, s * SH_O:(s + 1) * SH_O], 0.0)
        gs = [buf[0:2, :], buf[8:10, :], buf[16:17, :], g_conv]
        out[0][...] = buf[32:33, 0:128]
        for i in range(n):
            d, nm, nv = _adamw_math(refs[i][...], gs[i], refs[n + i][...], refs[2 * n + i][...])
            out[1 + i][...] = gs[i]
            out[1 + n + i][...] = d
            out[1 + 2 * n + i][...] = nm
            out[1 + 3 * n + i][...] = nv

    def full(shape):
        nd = len(shape)
        return pl.BlockSpec(shape, lambda i, pref: (0,) * nd)

    specs = [full(w.shape) for w in ws]
    grid_spec = pltpu.PrefetchScalarGridSpec(
        num_scalar_prefetch=1, grid=(1,),
        in_specs=[full(pack.shape), full(land.shape)] + specs * 3, out_specs=[full((1, 128))] + specs * 4,
        scratch_shapes=[pltpu.VMEM((SMALL_ROWS, D), F32)])
    outs = pl.pallas_call(
        body, name="small_update", grid_spec=grid_spec,
        out_shape=[_sds((1, 128), F32)] + [_sds(w.shape, F32) for w in ws] * 4,
        compiler_params=_params(("arbitrary",)),
    )(place, pack, land, *ws, *ms, *vs)
    return outs[0], outs[1:1 + n], outs[1 + n:1 + 2 * n], outs[1 + 2 * n:1 + 3 * n], outs[1 + 3 * n:]


def _adamw_layer(ws, gs, ms, vs, idx, name):
    n = len(idx)
    dims = [(BIG[w][1], BIG[w][2] // 4, BIG[w][3]) for w in idx]

    def body(*refs):
        for i in range(n):
            gv = refs[n + i][...]
            d, nm, nv = _adamw_math(refs[i][...], gv, refs[2 * n + i][...], refs[3 * n + i][...])
            refs[4 * n + i][...] = d
            refs[5 * n + i][...] = nm
            refs[6 * n + i][...] = nv
            refs[7 * n + i][...] = gv

    specs = [pl.BlockSpec((k, q, cdim), lambda t: (0, t, 0)) for k, q, cdim in dims]
    outs = pl.pallas_call(
        body, name=name, grid=(4,), in_specs=specs * 4, out_specs=specs * 4,
        out_shape=[_sds(BIG[w][1:], F32) for w in idx] * 4,
        compiler_params=_params(("parallel",)),
    )(*ws, *gs, *ms, *vs)
    return [tuple(outs[j * n + i] for j in range(4)) for i in range(n)]


def _pad_rows(a, rows):
    return jnp.pad(a, ((0, rows - a.shape[0]), (0, 0)))


def kernel(x, mem, positions, norm_g, mem_norm_g, w_mem_kv, attn_w_in, attn_w_out, conv_w_in, conv_w, conv_w_out, final_g, loss_target, m_norm_g, m_mem_norm_g, m_w_mem_kv, m_attn_w_in, m_attn_w_out, m_conv_w_in, m_conv_w, m_conv_w_out, m_final_g, v_norm_g, v_mem_norm_g, v_w_mem_kv, v_attn_w_in, v_attn_w_out, v_conv_w_in, v_conv_w, v_conv_w_out, v_final_g):
    mx, my, mc = lax.axis_index("x"), lax.axis_index("y"), lax.axis_index("c")
    place = jnp.stack([mc, 2 * mx + my]).astype(jnp.int32)

    w_big = [w_mem_kv, attn_w_in, attn_w_out, conv_w_in, conv_w_out]
    m_big = [m_w_mem_kv, m_attn_w_in, m_attn_w_out, m_conv_w_in, m_conv_w_out]
    v_big = [v_w_mem_kv, v_attn_w_in, v_attn_w_out, v_conv_w_in, v_conv_w_out]
    first, rest = (1,), (0, 2, 3, 4)
    wb1 = _cast_weights(place, [w_big[i] for i in first], place, first, "cast_w_in_a")
    a1_send, a1_recv, a1_bufs, a1_token = _gather_start(wb1, place, first, "gather_a1_start")
    wbr = _cast_weights(place, [w_big[i] for i in rest], a1_token, rest, "cast_weights")
    r_send, r_recv, r_bufs, gb_token = _gather_start(wbr, a1_token, rest, "gather_rest_start")
    a2_send, a2_recv, gb_send, gb_recv = r_send, r_recv, r_send, r_recv
    a2_bufs, gb_bufs = r_bufs[:2], r_bufs[2:]
    started, rest = rest, (0, 2)

    xs, tgt = x[0], loss_target[0]
    g0, g1 = norm_g[0:1], norm_g[1:2]
    rc, rs1, rs2 = _rope_tables(positions[0].astype(F32).reshape(S, 1), gb_token)
    a1_bufs = _gather_wait(a1_send, a1_recv, a1_bufs, [rc], first, "gather_a1_wait")
    w_in_a = _gather_forward(a1_bufs, first, "gather_a1_forward")[0].reshape(4, D, SH_A)
    hn0, q, k, v, qm0, z0 = _in_proj_a(xs, g0, w_in_a, rc, rs1, rs2, gb_token)
    a2_bufs = _gather_wait(a2_send, a2_recv, a2_bufs, [q], rest, "gather_a2_wait", started)
    f2_send, f2_recv, a2_bufs, f2_token = _forward_start(a2_bufs, None, q, rest, "forward_a2_start")
    fwd = [_attn_fwd(q, k, v, 0, f2_token)]
    fwd.append(_attn_fwd(q, k, v, 1, fwd[0][0]))
    cw_own = _pad_rows(conv_w[0], CW_ROWS)
    gb_bufs = _gather_wait(gb_send, gb_recv, gb_bufs, [fwd[1][0]], LAYER_B, "gather_b_wait", started)
    fb_send, fb_recv, gb_bufs, fb_token = _forward_start(gb_bufs, cw_own, fwd[1][0], LAYER_B, "forward_b_start")
    fwd.append(_attn_fwd(q, k, v, 2, fb_token))
    os_, ls, lss = [f[0] for f in fwd], [f[1] for f in fwd], [f[2] for f in fwd]
    wkv_f, w_out_a = _forward_wait(f2_send, f2_recv, a2_bufs, [os_[2]], rest, False, "forward_a2_wait")
    w_out_a = w_out_a.reshape(4, BR_A, SH_O)
    memn, kv = _mem_fwd(mem[0], mem_norm_g, wkv_f)
    h1 = _attn_out(os_, ls, qm0, kv[0], z0, xs, w_out_a)

    w_in_b, w_out_b, _, cw_f = _forward_wait(fb_send, fb_recv, gb_bufs, [h1], LAYER_B, True, "forward_b_wait")
    w_in_b = w_in_b.reshape(4, D, SH_B)
    w_out_b = w_out_b.reshape(BR_B, D)
    cw_f = lax.dynamic_update_slice(cw_f, cw_own[None], (2 * mx + my, 0, 0))
    cw8 = cw_f.transpose(1, 0, 2).reshape(CW_ROWS, D)
    hn1, bg, cg, u, qm1, z1 = _in_proj_b(h1, g1, w_in_b)
    dh2, loss_part, dfg = _conv_out_loss(bg, cg, u, cw8, qm1, kv[1], z1, h1, w_out_b, final_g.reshape(1, D), tgt)

    dproj_b, dw_out_b, dcw, dkv1, dw_out_b16 = _conv_bwd(dh2, bg, cg, u, cw8, qm1, kv[1], z1, w_out_b)
    dw_in_b, dw_in_b16 = _w_in_grad(hn1, dproj_b, IN_B, "w_in_b_grad")
    gs_b = [dw_in_b.reshape(4, 1, D, SH_B), dw_out_b.reshape(4, 1, BR_B // 4, D)]
    gb_b = [dw_in_b16.reshape(4, 1, D, SH_B), dw_out_b16.reshape(4, 1, BR_B // 4, D)]
    pb_send, pb_recv, gb_b, pb_land, pb_token = _pair_start(gb_b, LAYER_B, "pair_b_start")
    dh1, dg1 = _in_proj_bwd(dproj_b, w_in_b, h1, g1, dh2, pb_token, IN_B, "in_proj_b_bwd")
    _, r1_b = _pair_wait(pb_send, pb_recv, gb_b, pb_land, [dh1], LAYER_B, "pair_b_wait")
    ps_b = _pair_sums(place, gs_b, r1_b, LAYER_B, "pair_sums_b")
    cb_send, cb_recv, cb_src, cb_land, cb_token = _chip_start(ps_b, LAYER_B, "chip_b_start")

    outs = _attn_out_bwd(dh1, os_, ls, qm0, kv[0], z0, w_out_a, cb_token)
    dos, dds, dqm, dz, dw_out_a, dkv0, dw_out_a16 = outs[0:3], outs[3:6], outs[6], outs[7], outs[8], outs[9], outs[10]
    bwd = [_attn_bwd(q, k, v, dos[g], lss[g], dds[g], g) for g in range(3)]
    dproj_a = _qkv_bwd([b[0] for b in bwd], [b[1] for b in bwd], [b[2] for b in bwd], dqm, dz, rc, rs1, rs2)
    dw_in_a, dw_in_a16 = _w_in_grad(hn0, dproj_a, IN_A, "w_in_a_grad")
    dwkv, dwkv16, dmg = _mem_bwd(mem[0], mem_norm_g, memn, wkv_f, dkv0, dkv1)

    gs_a = [dwkv, dw_in_a.reshape(4, 1, D, SH_A), dw_out_a.reshape(4, 1, BR_A, SH_O)]
    r1_a = _pair_exchange([dwkv16, dw_in_a16.reshape(4, 1, D, SH_A), dw_out_a16.reshape(4, 1, BR_A, SH_O)], LAYER_A,
                          "pair_exchange_a")
    ps_a = _pair_sums(place, gs_a, r1_a, LAYER_A, "pair_sums_a")
    ca_send, ca_recv, ca_src, ca_land, ca_token = _chip_start(ps_a, LAYER_A, "chip_a_start")

    gx, dg0 = _in_proj_bwd(dproj_a, w_in_a, xs, g0, dh1, ca_token, IN_A, "in_proj_a_bwd")
    pack = jnp.concatenate([_pad_rows(jnp.concatenate([dg0, dg1], axis=0), 8), _pad_rows(dmg, 8), _pad_rows(dfg, 8),
                            dcw, _pad_rows(jnp.pad(loss_part, ((0, 0), (0, D - 128))), 8)], axis=0)
    sm_send, sm_recv, pack, sm_land, sm_token = _small_start(pack, ca_token)
    r2_b = _chip_wait(cb_send, cb_recv, cb_src, cb_land, [ca_token], LAYER_B, "chip_b_wait")
    hs_b = _chip_sums(place, gs_b, r1_b, r2_b, LAYER_B, "chip_sums_b")
    g_b = _pair_gather(hs_b, LAYER_B, "pair_gather_b")
    upd_b = _adamw_layer([w_big[w] for w in LAYER_B], g_b, [m_big[w] for w in LAYER_B], [v_big[w] for w in LAYER_B],
                         LAYER_B, "adamw_b")
    r2_a = _chip_wait(ca_send, ca_recv, ca_src, ca_land, [gx, upd_b[0][0], upd_b[1][0], sm_token], LAYER_A,
                      "chip_a_wait")
    hs_a = _chip_sums(place, gs_a, r1_a, r2_a, LAYER_A, "chip_sums_a")
    g_a = _pair_gather(hs_a, LAYER_A, "pair_gather_a")
    upd_a = _adamw_layer([w_big[w] for w in LAYER_A], g_a, [m_big[w] for w in LAYER_A], [v_big[w] for w in LAYER_A],
                         LAYER_A, "adamw_a")
    upd = upd_a + upd_b
    g_big = [u[3] for u in upd]
    pack, sm_land = _small_wait(sm_send, sm_recv, pack, sm_land, [r2_a[0]])
    sw = [norm_g, mem_norm_g, final_g.reshape(1, D), conv_w[0]]
    sm = [m_norm_g, m_mem_norm_g, m_final_g.reshape(1, D), m_conv_w[0]]
    sv = [v_norm_g, v_mem_norm_g, v_final_g.reshape(1, D), v_conv_w[0]]
    loss_row, sg, sd, snm, snv = _small_update(place, pack, sm_land, sw, sm, sv)
    loss = loss_row[0, 0]
    g_norm, g_memnorm, g_final, g_conv = sg

    def order(norm, memnorm, wkv, w_in_a, w_out_a, w_in_b, conv, w_out_b, final):
        return (norm, memnorm, wkv, w_in_a, w_out_a, w_in_b, conv.reshape(1, 3, SH_O), w_out_b, final.reshape(D))

    grads = order(g_norm, g_memnorm, g_big[0], g_big[1], g_big[2], g_big[3], g_conv, g_big[4], g_final)
    deltas = order(sd[0], sd[1], upd[0][0], upd[1][0], upd[2][0], upd[3][0], sd[3], upd[4][0], sd[2])
    new_m = order(snm[0], snm[1], upd[0][1], upd[1][1], upd[2][1], upd[3][1], snm[3], upd[4][1], snm[2])
    new_v = order(snv[0], snv[1], upd[0][2], upd[1][2], upd[2][2], upd[3][2], snv[3], upd[4][2], snv[2])
    return (loss, gx[None], *grads, *deltas, *new_m, *new_v)
```

```python
import functools

import numpy as np
import jax
import jax.numpy as jnp
from jax import lax
from jax.experimental import pallas as pl
from jax.experimental.pallas import tpu as pltpu

F32 = jnp.float32
BF16 = jnp.bfloat16

S = 2048
D = 1024
TM = 256
NT = S // TM
HD = 64
GW = 512
NQ = 3 * GW
MW = 256
NM = 256
IN_A = 3 * NQ + MW + GW + MW
IN_B = 3 * D + MW + D + MW
BR_A = GW + MW
BR_B = D + MW
SH_A = IN_A // 4
SH_B = IN_B // 4
SH_O = D // 4
QBLK = 128
DILATIONS = (1, 4, 16)
EPS = 1e-6
SCALE = HD ** -0.5
NEG = -1e30
ROPE_THETA = 500000.0

ADAM_LR = 0.001
ADAM_B1 = 0.9
ADAM_B2 = 0.999
ADAM_EPS = 1e-08
ADAM_WD = 0.01
ADAM_STEP = 10

VMEM_LIMIT_BYTES = 60 * 1024 * 1024


def _params(sem=None):
    if sem is None:
        return pltpu.CompilerParams(vmem_limit_bytes=VMEM_LIMIT_BYTES)
    return pltpu.CompilerParams(dimension_semantics=sem, vmem_limit_bytes=VMEM_LIMIT_BYTES)


def _full(shape):
    nd = len(shape)
    return pl.BlockSpec(shape, lambda *_: (0,) * nd)


def _rows(width, tm=TM):
    return pl.BlockSpec((tm, width), lambda i: (i, 0))


def _sds(shape, dtype):
    return jax.ShapeDtypeStruct(shape, dtype)


def _silu_parts(z):
    sig = 0.5 * jnp.tanh(0.5 * z) + 0.5
    return z * sig, sig * (1.0 + z * (1.0 - sig))


def _dot(a, b):
    return jnp.dot(a, b, preferred_element_type=F32)


def _dot_nt(a, b):
    return lax.dot_general(a, b, (((1,), (1,)), ((), ())), preferred_element_type=F32)


def _dot_tn(a, b):
    return lax.dot_general(a, b, (((0,), (0,)), ((), ())), preferred_element_type=F32)


def _rope_fwd(t, c, s1, s2):
    return t * c + pltpu.roll(t, 120, 1) * s1 + pltpu.roll(t, 8, 1) * s2


def _rope_bwd(g, c, s1, s2):
    return g * c + pltpu.roll(g * s1, 8, 1) + pltpu.roll(g * s2, 120, 1)


MEM_HEADS = MW // HD


def _stack_heads(x):
    head = lax.broadcasted_iota(jnp.int32, x.shape, 1) // HD
    return jnp.concatenate([jnp.where(head == h, x, 0.0) for h in range(MEM_HEADS)], axis=0).astype(BF16)


def _unstack_heads(x4):
    tm = x4.shape[0] // MEM_HEADS
    head = lax.broadcasted_iota(jnp.int32, (tm, MW), 1) // HD
    out = x4[:tm]
    for h in range(1, MEM_HEADS):
        out = jnp.where(head == h, x4[h * tm:(h + 1) * tm], out)
    return out


def _mem_attn(qm, kv):
    q4 = _stack_heads(qm.astype(F32))
    s = _dot_nt(q4, kv[:, :MW]) * SCALE
    e = jnp.exp(s - jnp.max(s, axis=-1, keepdims=True))
    p = e * (1.0 / jnp.sum(e, axis=-1, keepdims=True))
    return p, _unstack_heads(_dot(p.astype(BF16), kv[:, MW:])), q4


def _mem_attn_bwd(dmo, p, mo, q4, kv, dkv_ref):
    tm = dmo.shape[0]
    head = lax.broadcasted_iota(jnp.int32, dmo.shape, 1) // HD
    prod = dmo * mo
    delta = jnp.concatenate([jnp.sum(jnp.where(head == h, prod, 0.0), axis=-1, keepdims=True)
                             for h in range(MEM_HEADS)], axis=0)
    d4 = _stack_heads(dmo)
    ds = (p * (_dot_nt(d4, kv[:, MW:]) - delta) * SCALE).astype(BF16)
    dkv_ref[:, :MW] += _dot_tn(ds, q4)
    dkv_ref[:, MW:] += _dot_tn(p.astype(BF16), d4)
    return _unstack_heads(_dot(ds, kv[:, :MW]))


def _merge(o_refs, l_refs):
    ls = [r[...] for r in l_refs]
    m = jnp.maximum(jnp.maximum(ls[0], ls[1]), ls[2])
    es = [jnp.exp(l - m) for l in ls]
    inv = 1.0 / (es[0] + es[1] + es[2])
    ws = [e * inv for e in es]
    os_ = [r[...] for r in o_refs]
    mix = ws[0] * os_[0] + ws[1] * os_[1] + ws[2] * os_[2]
    return ws, mix


def _conv_taps(cg, u, cgp, up, first):
    a = cg * u
    ap = jnp.where(first, 0.0, cgp * up)
    row = lax.broadcasted_iota(jnp.int32, a.shape, 0)
    a1 = jnp.where(row == 0, ap[7:8, :], pltpu.roll(a, 1, 0))
    a2 = jnp.where(row == 0, ap[6:7, :], jnp.where(row == 1, ap[7:8, :], pltpu.roll(a, 2, 0)))
    return a, a1, a2


def _rope_tables(posf, after):
    half = 8
    invf = np.float32(ROPE_THETA) ** (-np.arange(half, dtype=np.float32) * np.float32(2.0 / 16))
    lane = np.arange(128)
    table = np.where((lane % HD) < 16, invf[lane % half], 0.0).astype(np.float32)[None, :]

    def body(pos_ref, invf_ref, c_ref, s1_ref, s2_ref):
        ang = pos_ref[...] * invf_ref[...]
        jm = lax.broadcasted_iota(jnp.int32, ang.shape, 1) & (HD - 1)
        cs = jnp.cos(ang)
        sn = jnp.sin(ang)
        c_ref[...] = jnp.where(jm < 16, cs, 1.0)
        s1_ref[...] = jnp.where(jm < 8, -sn, 0.0)
        s2_ref[...] = jnp.where((jm >= 8) & (jm < 16), sn, 0.0)

    out = _sds((S, 128), F32)
    return pl.pallas_call(
        functools.partial(_skip_arg, body, 2), name="rope_tables", grid=(NT,),
        in_specs=[_rows(1), _full((1, 128)), pl.BlockSpec(memory_space=pl.ANY)],
        out_specs=[_rows(128)] * 3, out_shape=[out] * 3,
        compiler_params=_params(("parallel",)),
    )(posf, jnp.asarray(table), after)


def _in_proj_a(x, g0, w_in, c, s1, s2, after):
    def body(x_ref, g_ref, w_ref, c_ref, s1_ref, s2_ref, hn_ref, q_ref, k_ref, v_ref, qm_ref, z_ref, proj):
        xf = x_ref[...]
        hn = xf * lax.rsqrt(jnp.mean(xf * xf, axis=-1, keepdims=True) + EPS) * g_ref[...]
        hb = hn.astype(BF16)
        hn_ref[...] = hb
        for s in range(4):
            proj[:, s * SH_A:(s + 1) * SH_A] = _dot(hb, w_ref[s])
        cc, a1, a2 = c_ref[...], s1_ref[...], s2_ref[...]
        for j in range(NQ // 128):
            q_ref[:, j * 128:(j + 1) * 128] = (
                _rope_fwd(proj[:, j * 128:(j + 1) * 128], cc, a1, a2) * SCALE).astype(BF16)
            k_ref[:, j * 128:(j + 1) * 128] = _rope_fwd(
                proj[:, NQ + j * 128:NQ + (j + 1) * 128], cc, a1, a2).astype(BF16)
        v_ref[...] = proj[:, 2 * NQ:3 * NQ].astype(BF16)
        qm_ref[...] = proj[:, 3 * NQ:3 * NQ + MW].astype(BF16)
        z_ref[...] = proj[:, 3 * NQ + MW:]

    return pl.pallas_call(
        functools.partial(_skip_arg, body, 6), name="in_proj_a", grid=(NT,),
        in_specs=[_rows(D), _full((1, D)), _full((4, D, SH_A)), _rows(128), _rows(128), _rows(128),
                  pl.BlockSpec(memory_space=pl.ANY)],
        out_specs=[_rows(D), _rows(NQ), _rows(NQ), _rows(NQ), _rows(MW), _rows(BR_A)],
        out_shape=[_sds((S, D), BF16), _sds((S, NQ), BF16), _sds((S, NQ), BF16), _sds((S, NQ), BF16),
                   _sds((S, MW), BF16), _sds((S, BR_A), F32)],
        scratch_shapes=[pltpu.VMEM((TM, IN_A), F32)],
        compiler_params=_params(("parallel",)),
    )(x, g0, w_in, c, s1, s2, after)


def _mem_fwd(mem, mg, wkv):
    def body(mem_ref, mg_ref, w_ref, memn_ref, kv_ref):
        mf = mem_ref[...]
        n = mf * lax.rsqrt(jnp.mean(mf * mf, axis=-1, keepdims=True) + EPS)
        for i in range(2):
            mn = (n * mg_ref[i:i + 1, :]).astype(BF16)
            memn_ref[i] = mn
            acc = _dot(mn[:, 0:NM], w_ref[0, i])
            for s in range(1, 4):
                acc += _dot(mn[:, s * NM:(s + 1) * NM], w_ref[s, i])
            kv_ref[i] = acc.astype(BF16)

    return pl.pallas_call(
        body, name="mem_fwd", grid=(1,),
        in_specs=[_full((NM, D)), _full((2, D)), _full((4, 2, NM, 2 * MW))],
        out_specs=[_full((2, NM, D)), _full((2, NM, 2 * MW))],
        out_shape=[_sds((2, NM, D), BF16), _sds((2, NM, 2 * MW), BF16)],
        compiler_params=_params(("arbitrary",)),
    )(mem, mg, wkv)


def _band_mask(j):
    qi = lax.broadcasted_iota(jnp.int32, (QBLK, 2 * QBLK), 0)
    kj = lax.broadcasted_iota(jnp.int32, (QBLK, 2 * QBLK), 1)
    dist = qi + QBLK - kj
    return (dist >= 0) & (dist <= QBLK) & ((kj >= QBLK) | (j > 0))


LANES = 128
NCHUNK = GW // LANES
FWD_UNROLL = 16
BWD_UNROLL = 16
CONV_CHUNK = 256


def _perm_matrix(d):
    n = TM // d
    p = np.zeros((TM, TM), np.float32)
    for r in range(d):
        for i in range(n):
            p[r * n + i, i * d + r] = 1.0
    return p


def _split_dot(p, x):
    hi = x.astype(BF16)
    lo = (x - hi.astype(F32)).astype(BF16)
    both = _dot(p, jnp.concatenate([hi, lo], axis=1))
    return both[:, :LANES] + both[:, LANES:]


def _pair_dot(p, a, b):
    both = _dot(p, jnp.concatenate([a, b], axis=1))
    return both[:, :LANES], both[:, LANES:]


def _tile_to_streams(y, dst, t, d):
    n, ln = TM // d, S // d
    for r in range(d):
        dst[r * ln + t * n:r * ln + (t + 1) * n, :] = y[r * n:(r + 1) * n].astype(dst.dtype)


def _tile_from_streams(src, t, d):
    n, ln = TM // d, S // d
    return jnp.concatenate([src[r * ln + t * n:r * ln + (t + 1) * n, :] for r in range(d)], axis=0)


def _head_masks():
    first = lax.broadcasted_iota(jnp.int32, (TM, LANES), 1) < HD
    return first, jnp.logical_not(first)


def _attn_fwd(q, k, v, g, after):
    d = DILATIONS[g]
    nb = S // d // QBLK
    perm = _perm_matrix(d)

    def body(q_ref, k_ref, v_ref, p_ref, pt_ref, o_ref, l_ref, ls_ref, q0, q1, ks, vs, os_):
        first, second = _head_masks()
        pm = p_ref[...]
        for t in range(NT):
            rows = slice(t * TM, (t + 1) * TM)
            if d == 1:
                qt = q_ref[rows, :].astype(F32)
            else:
                qt, kt = _pair_dot(pm, q_ref[rows, :], k_ref[rows, :])
                _tile_to_streams(kt, ks, t, d)
                if t % 2 == 0:
                    va, vb = _pair_dot(pm, v_ref[rows, :], v_ref[(t + 1) * TM:(t + 2) * TM, :])
                    _tile_to_streams(va, vs, t, d)
                    _tile_to_streams(vb, vs, t + 1, d)
            _tile_to_streams(jnp.where(first, qt, 0.0), q0, t, d)
            _tile_to_streams(jnp.where(second, qt, 0.0), q1, t, d)
        kref, vref = (k_ref, v_ref) if d == 1 else (ks, vs)
        oref, lref = (o_ref, l_ref) if d == 1 else (os_, ls_ref)

        def blk(b, carry):
            r0 = pl.multiple_of(b * QBLK, QBLK)
            p0 = pl.multiple_of(jnp.maximum(b - 1, 0) * QBLK, QBLK)
            kk = jnp.concatenate([kref[pl.ds(p0, QBLK), :], kref[pl.ds(r0, QBLK), :]], axis=0)
            vv = jnp.concatenate([vref[pl.ds(p0, QBLK), :], vref[pl.ds(r0, QBLK), :]], axis=0)
            valid = _band_mask(b & (nb - 1))
            acc, lse = [], []
            for qh in (q0, q1):
                s = jnp.where(valid, _dot_nt(qh[pl.ds(r0, QBLK), :], kk), NEG)
                m = jnp.max(s, axis=-1, keepdims=True)
                e = jnp.exp(s - m)
                l = jnp.sum(e, axis=-1, keepdims=True)
                acc.append(_dot(e.astype(BF16), vv) * (1.0 / l))
                lse.append(m + jnp.log(l))
            f = first[:QBLK]
            oref[pl.ds(r0, QBLK), :] = jnp.where(f, acc[0], acc[1])
            lref[pl.ds(r0, QBLK), :] = jnp.where(f, lse[0], lse[1])
            return carry

        lax.fori_loop(0, S // QBLK, blk, 0, unroll=FWD_UNROLL)
        if d > 1:
            ptm = pt_ref[...]
            for t in range(NT):
                rows = slice(t * TM, (t + 1) * TM)
                o_ref[rows, :] = _split_dot(ptm, _tile_from_streams(os_, t, d))
                l_ref[rows, :] = _split_dot(ptm, _tile_from_streams(ls_ref, t, d))

    qkv_spec = pl.BlockSpec((S, LANES), lambda c: (0, g * NCHUNK + c))
    out_spec = pl.BlockSpec((S, LANES), lambda c: (0, c))
    n_out = 2 if d == 1 else 3
    inner = body if d > 1 else functools.partial(_drop_arg, body, 7)
    outs = pl.pallas_call(
        functools.partial(_skip_arg, inner, 5), name=f"attn_fwd_g{g}", grid=(NCHUNK,),
        in_specs=[qkv_spec] * 3 + [_full((TM, TM))] * 2 + [pl.BlockSpec(memory_space=pl.ANY)],
        out_specs=[out_spec] * n_out, out_shape=[_sds((S, GW), F32)] * n_out,
        scratch_shapes=[pltpu.VMEM((S, LANES), BF16)] * 4 + [pltpu.VMEM((S, LANES), F32)],
        compiler_params=_params(("parallel",)),
    )(q, k, v, jnp.asarray(perm, BF16), jnp.asarray(perm.T, BF16), after)
    return (outs[0], outs[1], outs[1]) if d == 1 else tuple(outs)


def _drop_arg(body, pos, *refs):
    return body(*refs[:pos], None, *refs[pos:])


def _attn_out(os_, ls, qm, kv0, z, x, w_out):
    def body(o0, o1, o2, l0, l1, l2, qm_ref, kv_ref, z_ref, x_ref, w_ref, h_ref, ybuf):
        _, mix = _merge((o0, o1, o2), (l0, l1, l2))
        sz, _ = _silu_parts(z_ref[...])
        ybuf[:, :GW] = (mix * sz[:, :GW]).astype(BF16)
        _, mo, _ = _mem_attn(qm_ref[...], kv_ref[...])
        ybuf[:, GW:] = (mo * sz[:, GW:]).astype(BF16)
        yb = ybuf[...]
        for s in range(4):
            cs = slice(s * SH_O, (s + 1) * SH_O)
            h_ref[:, cs] = x_ref[:, cs] + _dot(yb, w_ref[s])

    return pl.pallas_call(
        body, name="attn_out", grid=(NT,),
        in_specs=[_rows(GW)] * 6 + [_rows(MW), _full((NM, 2 * MW)), _rows(BR_A), _rows(D), _full((4, BR_A, SH_O))],
        out_specs=_rows(D), out_shape=_sds((S, D), F32),
        scratch_shapes=[pltpu.VMEM((TM, BR_A), BF16)],
        compiler_params=_params(("parallel",)),
    )(*os_, *ls, qm, kv0, z, x, w_out)


def _in_proj_b(h1, g1, w_in):
    def body(x_ref, g_ref, w_ref, hn_ref, bg_ref, cg_ref, u_ref, qm_ref, z_ref, proj):
        xf = x_ref[...]
        hn = xf * lax.rsqrt(jnp.mean(xf * xf, axis=-1, keepdims=True) + EPS) * g_ref[...]
        hb = hn.astype(BF16)
        hn_ref[...] = hb
        for s in range(4):
            proj[:, s * SH_B:(s + 1) * SH_B] = _dot(hb, w_ref[s])
        bg_ref[...] = proj[:, :D]
        cg_ref[...] = proj[:, D:2 * D]
        u_ref[...] = proj[:, 2 * D:3 * D]
        qm_ref[...] = proj[:, 3 * D:3 * D + MW].astype(BF16)
        z_ref[...] = proj[:, 3 * D + MW:]

    return pl.pallas_call(
        body, name="in_proj_b", grid=(NT,),
        in_specs=[_rows(D), _full((1, D)), _full((4, D, SH_B))],
        out_specs=[_rows(D), _rows(D), _rows(D), _rows(D), _rows(MW), _rows(BR_B)],
        out_shape=[_sds((S, D), BF16), _sds((S, D), F32), _sds((S, D), F32), _sds((S, D), F32),
                   _sds((S, MW), BF16), _sds((S, BR_B), F32)],
        scratch_shapes=[pltpu.VMEM((TM, IN_B), F32)],
        compiler_params=_params(("parallel",)),
    )(h1, g1, w_in)


def _prev8(width):
    return pl.BlockSpec((8, width), lambda i: (jnp.maximum(i * (TM // 8) - 1, 0), 0))


def _conv_out_loss(bg, cg, u, cw, qm, kv1, z, h1, w_out, fg, tgt):
    def body(bg_ref, cg_ref, u_ref, cgp_ref, up_ref, cw_ref, qm_ref, kv_ref, z_ref, h_ref, w_ref, fg_ref, t_ref,
             dh_ref, loss_ref, dfg_ref, ybuf):
        i = pl.program_id(0)
        a, a1, a2 = _conv_taps(cg_ref[...], u_ref[...], cgp_ref[...], up_ref[...], i == 0)
        conv = cw_ref[0:1, :] * a2 + cw_ref[1:2, :] * a1 + cw_ref[2:3, :] * a
        sz, _ = _silu_parts(z_ref[...])
        ybuf[:, :D] = (bg_ref[...] * conv * sz[:, :D]).astype(BF16)
        _, mo, _ = _mem_attn(qm_ref[...], kv_ref[...])
        ybuf[:, D:] = (mo * sz[:, D:]).astype(BF16)
        h2 = h_ref[...] + _dot(ybuf[...], w_ref[...])
        rstd = lax.rsqrt(jnp.mean(h2 * h2, axis=-1, keepdims=True) + EPS)
        n = h2 * rstd
        fgv = fg_ref[...]
        err = n * fgv - t_ref[...]
        dout = err * (1.0 / D)
        dn = dout * fgv
        dh_ref[...] = rstd * (dn - n * jnp.mean(dn * n, axis=-1, keepdims=True))

        @pl.when(i == 0)
        def _():
            loss_ref[...] = jnp.zeros_like(loss_ref)
            dfg_ref[...] = jnp.zeros_like(dfg_ref)

        loss_ref[...] += jnp.sum(err * err) * (0.5 / D)
        dfg_ref[...] += jnp.sum(dout * n, axis=0, keepdims=True)

    return pl.pallas_call(
        body, name="conv_out_loss", grid=(NT,),
        in_specs=[_rows(D), _rows(D), _rows(D), _prev8(D), _prev8(D), _full((8, D)), _rows(MW),
                  _full((NM, 2 * MW)), _rows(BR_B), _rows(D), _full((BR_B, D)), _full((1, D)), _rows(D)],
        out_specs=[_rows(D), _full((1, 128)), _full((1, D))],
        out_shape=[_sds((S, D), F32), _sds((1, 128), F32), _sds((1, D), F32)],
        scratch_shapes=[pltpu.VMEM((TM, BR_B), BF16)],
        compiler_params=_params(("arbitrary",)),
    )(bg, cg, u, cg, u, cw, qm, kv1, z, h1, w_out, fg, tgt)


def _conv_bwd(dh2, bg, cg, u, cw, qm, kv1, z, w_out):
    rev = lambda i: (NT - 1 - i, 0)
    rows = lambda w: pl.BlockSpec((TM, w), rev)
    prev8 = pl.BlockSpec((8, D), lambda i: (jnp.maximum((NT - 1 - i) * (TM // 8) - 1, 0), 0))

    def body(dh_ref, bg_ref, cg_ref, u_ref, cgp_ref, up_ref, cw_ref, qm_ref, kv_ref, z_ref, w_ref,
             dproj_ref, dw_ref, dcw_ref, dkv_ref, dwb_ref, ybuf, carry):
        i = pl.program_id(0)

        @pl.when(i == 0)
        def _():
            dw_ref[...] = jnp.zeros_like(dw_ref)
            dcw_ref[...] = jnp.zeros_like(dcw_ref)
            dkv_ref[...] = jnp.zeros_like(dkv_ref)
            carry[...] = jnp.zeros_like(carry)

        dhb = dh_ref[...].astype(BF16)
        dy = _dot_nt(dhb, w_ref[...])
        kvv = kv_ref[...]
        p, mo, q4 = _mem_attn(qm_ref[...], kvv)
        szm, dszm = _silu_parts(z_ref[:, D:])
        ybuf[:, D:] = (mo * szm).astype(BF16)
        dym = dy[:, D:]
        dproj_ref[:, 3 * D + MW + D:] = (dym * mo * dszm).astype(BF16)
        first_tile = i == NT - 1
        for c in range(D // CONV_CHUNK):
            cs = slice(c * CONV_CHUNK, (c + 1) * CONV_CHUNK)
            bgv, cgv, uv = bg_ref[:, cs], cg_ref[:, cs], u_ref[:, cs]
            a, a1, a2 = _conv_taps(cgv, uv, cgp_ref[:, cs], up_ref[:, cs], first_tile)
            w0, w1, w2 = cw_ref[0:1, cs], cw_ref[1:2, cs], cw_ref[2:3, cs]
            conv = w0 * a2 + w1 * a1 + w2 * a
            mix = bgv * conv
            sz, dsz = _silu_parts(z_ref[:, cs])
            ybuf[:, cs] = (mix * sz).astype(BF16)
            dyc = dy[:, cs]
            dproj_ref[:, 3 * D + MW + c * CONV_CHUNK:3 * D + MW + (c + 1) * CONV_CHUNK] = (
                dyc * mix * dsz).astype(BF16)
            dmix = dyc * sz
            dproj_ref[:, cs] = (dmix * conv).astype(BF16)
            dc = dmix * bgv
            nxt = carry[:, cs]
            row = lax.broadcasted_iota(jnp.int32, dc.shape, 0)
            dc1 = jnp.where(row == TM - 1, nxt[0:1, :], pltpu.roll(dc, TM - 1, 0))
            dc2 = jnp.where(row == TM - 2, nxt[0:1, :],
                            jnp.where(row == TM - 1, nxt[1:2, :], pltpu.roll(dc, TM - 2, 0)))
            carry[:, cs] = dc[0:8, :]
            da = w2 * dc + w1 * dc1 + w0 * dc2
            dproj_ref[:, D + c * CONV_CHUNK:D + (c + 1) * CONV_CHUNK] = (da * uv).astype(BF16)
            dproj_ref[:, 2 * D + c * CONV_CHUNK:2 * D + (c + 1) * CONV_CHUNK] = (da * cgv).astype(BF16)
            dcw_ref[0:1, cs] += jnp.sum(dc * a2, axis=0, keepdims=True)
            dcw_ref[1:2, cs] += jnp.sum(dc * a1, axis=0, keepdims=True)
            dcw_ref[2:3, cs] += jnp.sum(dc * a, axis=0, keepdims=True)
        dw_ref[...] += _dot_tn(ybuf[...], dhb)
        dproj_ref[:, 3 * D:3 * D + MW] = _mem_attn_bwd(dym * szm, p, mo, q4, kvv, dkv_ref).astype(BF16)

        @pl.when(i == NT - 1)
        def _():
            dwb_ref[...] = dw_ref[...].astype(BF16)

    return pl.pallas_call(
        body, name="conv_bwd", grid=(NT,),
        in_specs=[rows(D), rows(D), rows(D), rows(D), prev8, prev8, _full((8, D)), rows(MW),
                  _full((NM, 2 * MW)), rows(BR_B), _full((BR_B, D))],
        out_specs=[rows(IN_B), _full((BR_B, D)), _full((8, D)), _full((NM, 2 * MW)), _full((BR_B, D))],
        out_shape=[_sds((S, IN_B), BF16), _sds((BR_B, D), F32), _sds((8, D), F32), _sds((NM, 2 * MW), F32),
                   _sds((BR_B, D), BF16)],
        scratch_shapes=[pltpu.VMEM((TM, BR_B), BF16), pltpu.VMEM((8, D), F32)],
        compiler_params=_params(("arbitrary",)),
    )(dh2, bg, cg, u, cg, u, cw, qm, kv1, z, w_out)


def _in_proj_bwd(dproj, w_in, xin, g, dres, after, width, name):
    sh = width // 4

    def body(dp_ref, w_ref, x_ref, g_ref, dr_ref, dx_ref, dg_ref):
        i = pl.program_id(0)
        dhn = _dot_nt(dp_ref[:, 0:sh], w_ref[0])
        for s in range(1, 4):
            dhn += _dot_nt(dp_ref[:, s * sh:(s + 1) * sh], w_ref[s])
        xf = x_ref[...]
        rstd = lax.rsqrt(jnp.mean(xf * xf, axis=-1, keepdims=True) + EPS)
        n = xf * rstd
        dn = dhn * g_ref[...]
        dx_ref[...] = dr_ref[...] + rstd * (dn - n * jnp.mean(dn * n, axis=-1, keepdims=True))

        @pl.when(i == 0)
        def _():
            dg_ref[...] = jnp.zeros_like(dg_ref)

        dg_ref[...] += jnp.sum(dhn * n, axis=0, keepdims=True)

    return pl.pallas_call(
        functools.partial(_skip_arg, body, 5), name=name, grid=(NT,),
        in_specs=[_rows(width), _full((4, D, sh)), _rows(D), _full((1, D)), _rows(D), pl.BlockSpec(memory_space=pl.ANY)],
        out_specs=[_rows(D), _full((1, D))],
        out_shape=[_sds((S, D), F32), _sds((1, D), F32)],
        compiler_params=_params(("arbitrary",)),
    )(dproj, w_in, xin, g, dres, after)


def _w_in_grad(hn, dproj, width, name):
    sh = width // 4

    def body(hn_ref, dp_ref, dw_ref, dwb_ref):
        dw = _dot_tn(hn_ref[...], dp_ref[...])
        dw_ref[0] = dw
        dwb_ref[0] = dw.astype(BF16)

    spec = pl.BlockSpec((1, D, sh), lambda s: (s, 0, 0))
    return pl.pallas_call(
        body, name=name, grid=(4,),
        in_specs=[_full((S, D)), pl.BlockSpec((S, sh), lambda s: (0, s))],
        out_specs=[spec, spec], out_shape=[_sds((4, D, sh), F32), _sds((4, D, sh), BF16)],
        compiler_params=_params(("parallel",)),
    )(hn, dproj)


def _attn_out_bwd(dh1, os_, ls, qm, kv0, z, w_out, after):
    ones_bd = np.kron(np.eye(GW // HD, dtype=np.float32), np.ones((HD, HD), np.float32))

    def body(dh_ref, o0, o1, o2, l0, l1, l2, qm_ref, kv_ref, z_ref, w_ref, bd_ref,
             do0, do1, do2, dd0, dd1, dd2, dqm_ref, dz_ref, dw_ref, dkv_ref, dwb_ref, ybuf):
        i = pl.program_id(0)

        @pl.when(i == 0)
        def _():
            dw_ref[...] = jnp.zeros_like(dw_ref)
            dkv_ref[...] = jnp.zeros_like(dkv_ref)

        ws, mix = _merge((o0, o1, o2), (l0, l1, l2))
        sz, dsz = _silu_parts(z_ref[...])
        kvv = kv_ref[...]
        p, mo, q4 = _mem_attn(qm_ref[...], kvv)
        ybuf[:, :GW] = (mix * sz[:, :GW]).astype(BF16)
        ybuf[:, GW:] = (mo * sz[:, GW:]).astype(BF16)
        yb = ybuf[...]
        dh = dh_ref[...]
        dy = None
        for s in range(4):
            dhb = dh[:, s * SH_O:(s + 1) * SH_O].astype(BF16)
            dw_ref[s] += _dot_tn(yb, dhb)
            part = _dot_nt(dhb, w_ref[s])
            dy = part if dy is None else dy + part
        dcat = dy * sz
        dz_ref[:, :GW] = (dy[:, :GW] * mix * dsz[:, :GW]).astype(BF16)
        dz_ref[:, GW:] = (dy[:, GW:] * mo * dsz[:, GW:]).astype(BF16)
        dmix = dcat[:, :GW]
        prod = dmix * mix
        hi = prod.astype(BF16)
        lo = (prod - hi.astype(F32)).astype(BF16)
        bd = bd_ref[...]
        tot = _dot(hi, bd) + _dot(lo, bd)
        for w, do_ref, dd_ref in zip(ws, (do0, do1, do2), (dd0, dd1, dd2)):
            do_ref[...] = (w * dmix).astype(BF16)
            dd_ref[...] = w * tot

        dqm_ref[...] = _mem_attn_bwd(dcat[:, GW:], p, mo, q4, kvv, dkv_ref).astype(BF16)

        @pl.when(i == NT - 1)
        def _():
            dwb_ref[...] = dw_ref[...].astype(BF16)

    return pl.pallas_call(
        functools.partial(_skip_arg, body, 12), name="attn_out_bwd", grid=(NT,),
        in_specs=[_rows(D)] + [_rows(GW)] * 6 + [_rows(MW), _full((NM, 2 * MW)), _rows(BR_A),
                                                   _full((4, BR_A, SH_O)), _full((GW, GW)),
                                                   pl.BlockSpec(memory_space=pl.ANY)],
        out_specs=[_rows(GW)] * 6 + [_rows(MW), _rows(BR_A), _full((4, BR_A, SH_O)), _full((NM, 2 * MW)),
                                     _full((4, BR_A, SH_O))],
        out_shape=[_sds((S, GW), BF16)] * 3 + [_sds((S, GW), F32)] * 3 + [
            _sds((S, MW), BF16), _sds((S, BR_A), BF16), _sds((4, BR_A, SH_O), F32), _sds((NM, 2 * MW), F32),
            _sds((4, BR_A, SH_O), BF16)],
        scratch_shapes=[pltpu.VMEM((TM, BR_A), BF16)],
        compiler_params=_params(("arbitrary",)),
    )(dh1, *os_, *ls, qm, kv0, z, w_out, jnp.asarray(ones_bd, dtype=BF16), after)


def _attn_bwd(q, k, v, do, lse_s, dd, g):
    d = DILATIONS[g]
    nb = S // d // QBLK
    perm = _perm_matrix(d)

    def body(q_ref, k_ref, v_ref, do_ref, l_ref, dd_ref, p_ref, pt_ref, dq_ref, dk_ref, dv_ref,
             q0, q1, g0, g1, ks, vs, dds, dqs, dks, dvs):
        first, second = _head_masks()
        pm = p_ref[...]
        for t in range(NT):
            rows = slice(t * TM, (t + 1) * TM)
            if d == 1:
                qt = q_ref[rows, :].astype(F32)
                gt = do_ref[rows, :].astype(F32)
            else:
                qt, gt = _pair_dot(pm, q_ref[rows, :], do_ref[rows, :])
                kt, vt = _pair_dot(pm, k_ref[rows, :], v_ref[rows, :])
                _tile_to_streams(kt, ks, t, d)
                _tile_to_streams(vt, vs, t, d)
                _tile_to_streams(_split_dot(pm, dd_ref[rows, :]), dds, t, d)
            _tile_to_streams(jnp.where(first, qt, 0.0), q0, t, d)
            _tile_to_streams(jnp.where(second, qt, 0.0), q1, t, d)
            _tile_to_streams(jnp.where(first, gt, 0.0), g0, t, d)
            _tile_to_streams(jnp.where(second, gt, 0.0), g1, t, d)
        kref, vref, ddref = (k_ref, v_ref, dd_ref) if d == 1 else (ks, vs, dds)
        dqref, dkref, dvref = dqs, dks, dvs
        dkref[...] = jnp.zeros_like(dkref)
        dvref[...] = jnp.zeros_like(dvref)

        def blk(b, carry):
            r0 = pl.multiple_of(b * QBLK, QBLK)
            p0 = pl.multiple_of(jnp.maximum(b - 1, 0) * QBLK, QBLK)
            kk = jnp.concatenate([kref[pl.ds(p0, QBLK), :], kref[pl.ds(r0, QBLK), :]], axis=0)
            vv = jnp.concatenate([vref[pl.ds(p0, QBLK), :], vref[pl.ds(r0, QBLK), :]], axis=0)
            lb = l_ref[pl.ds(r0, QBLK), :]
            ddb = ddref[pl.ds(r0, QBLK), :]
            lcol = jnp.concatenate([lb[:, 0:1], lb[:, HD:HD + 1]], axis=0)
            dcol = jnp.concatenate([ddb[:, 0:1], ddb[:, HD:HD + 1]], axis=0)
            valid = _band_mask(b & (nb - 1))
            valid2 = jnp.concatenate([valid, valid], axis=0)
            qq = jnp.concatenate([q0[pl.ds(r0, QBLK), :], q1[pl.ds(r0, QBLK), :]], axis=0)
            gg = jnp.concatenate([g0[pl.ds(r0, QBLK), :], g1[pl.ds(r0, QBLK), :]], axis=0)
            p = jnp.where(valid2, jnp.exp(_dot_nt(qq, kk) - lcol), 0.0)
            ds = (p * (_dot_nt(gg, vv) - dcol)).astype(BF16)
            dq2 = _dot(ds, kk)
            dqref[pl.ds(r0, QBLK), :] = jnp.where(first[:QBLK], dq2[:QBLK], dq2[QBLK:])
            dkk = _dot_tn(ds, qq)
            dvv = _dot_tn(p.astype(BF16), gg)
            dkref[pl.ds(p0, QBLK), :] += dkk[:QBLK]
            dkref[pl.ds(r0, QBLK), :] += dkk[QBLK:]
            dvref[pl.ds(p0, QBLK), :] += dvv[:QBLK]
            dvref[pl.ds(r0, QBLK), :] += dvv[QBLK:]
            return carry

        lax.fori_loop(0, S // QBLK, blk, 0, unroll=BWD_UNROLL)

        ptm = pt_ref[...] if d > 1 else None
        for t in range(NT):
            rows = slice(t * TM, (t + 1) * TM)
            if d == 1:
                dq_ref[rows, :] = dqs[rows, :].astype(BF16)
                dk_ref[rows, :] = dks[rows, :].astype(BF16)
                dv_ref[rows, :] = dvs[rows, :].astype(BF16)
            else:
                tq, tk = _pair_dot(ptm, _tile_from_streams(dqs, t, d).astype(BF16),
                                   _tile_from_streams(dks, t, d).astype(BF16))
                dq_ref[rows, :] = tq.astype(BF16)
                dk_ref[rows, :] = tk.astype(BF16)
                if t % 2 == 0:
                    ta, tb = _pair_dot(ptm, _tile_from_streams(dvs, t, d).astype(BF16),
                                       _tile_from_streams(dvs, t + 1, d).astype(BF16))
                    dv_ref[rows, :] = ta.astype(BF16)
                    dv_ref[(t + 1) * TM:(t + 2) * TM, :] = tb.astype(BF16)

    qkv_spec = pl.BlockSpec((S, LANES), lambda c: (0, g * NCHUNK + c))
    one_spec = pl.BlockSpec((S, LANES), lambda c: (0, c))
    return pl.pallas_call(
        body, name=f"attn_bwd_g{g}", grid=(NCHUNK,),
        in_specs=[qkv_spec] * 3 + [one_spec] * 3 + [_full((TM, TM))] * 2, out_specs=[one_spec] * 3,
        out_shape=[_sds((S, GW), BF16)] * 3,
        scratch_shapes=[pltpu.VMEM((S, LANES), BF16)] * 6 + [pltpu.VMEM((S, LANES), F32)] * 4,
        compiler_params=_params(("parallel",)),
    )(q, k, v, do, lse_s, dd, jnp.asarray(perm, BF16), jnp.asarray(perm.T, BF16))


def _qkv_bwd(dqs, dks, dvs, dqm, dz, c, s1, s2):
    def body(q0, q1, q2, k0, k1, k2, v0, v1, v2, dqm_ref, dz_ref, c_ref, s1_ref, s2_ref, dp_ref):
        cc, a1, a2 = c_ref[...], s1_ref[...], s2_ref[...]
        for g, (qr, kr, vr) in enumerate(((q0, k0, v0), (q1, k1, v1), (q2, k2, v2))):
            for j in range(GW // 128):
                ls_ = slice(j * 128, (j + 1) * 128)
                c0 = g * GW + j * 128
                dp_ref[:, c0:c0 + 128] = (_rope_bwd(qr[:, ls_].astype(F32), cc, a1, a2) * SCALE).astype(BF16)
                dp_ref[:, NQ + c0:NQ + c0 + 128] = _rope_bwd(kr[:, ls_].astype(F32), cc, a1, a2).astype(BF16)
            dp_ref[:, 2 * NQ + g * GW:2 * NQ + (g + 1) * GW] = vr[...]
        dp_ref[:, 3 * NQ:3 * NQ + MW] = dqm_ref[...]
        dp_ref[:, 3 * NQ + MW:] = dz_ref[...]

    return pl.pallas_call(
        body, name="qkv_bwd", grid=(NT,),
        in_specs=[_rows(GW)] * 9 + [_rows(MW), _rows(BR_A), _rows(128), _rows(128), _rows(128)],
        out_specs=_rows(IN_A), out_shape=_sds((S, IN_A), BF16),
        compiler_params=_params(("parallel",)),
    )(*dqs, *dks, *dvs, dqm, dz, c, s1, s2)


def _mem_bwd(mem, mg, memn, wkv, dkv0, dkv1):
    def body(mem_ref, mg_ref, memn_ref, w_ref, d0_ref, d1_ref, dw_ref, dwb_ref, dg_ref):
        mf = mem_ref[...]
        n = mf * lax.rsqrt(jnp.mean(mf * mf, axis=-1, keepdims=True) + EPS)
        for i, d_ref in enumerate((d0_ref, d1_ref)):
            dkv = d_ref[...].astype(BF16)
            mn = memn_ref[i]
            for s in range(4):
                cs = slice(s * NM, (s + 1) * NM)
                dw = _dot_tn(mn[:, cs], dkv)
                dw_ref[s, i] = dw
                dwb_ref[s, i] = dw.astype(BF16)
                dmn = _dot_nt(dkv, w_ref[s, i])
                dg_ref[i:i + 1, cs] = jnp.sum(dmn * n[:, cs], axis=0, keepdims=True)

    return pl.pallas_call(
        body, name="mem_bwd", grid=(1,),
        in_specs=[_full((NM, D)), _full((2, D)), _full((2, NM, D)), _full((4, 2, NM, 2 * MW)),
                  _full((NM, 2 * MW)), _full((NM, 2 * MW))],
        out_specs=[_full((4, 2, NM, 2 * MW)), _full((4, 2, NM, 2 * MW)), _full((2, D))],
        out_shape=[_sds((4, 2, NM, 2 * MW), F32), _sds((4, 2, NM, 2 * MW), BF16), _sds((2, D), F32)],
        compiler_params=_params(("arbitrary",)),
    )(mem, mg, memn, wkv, dkv0, dkv1)


MESH = pl.DeviceIdType.MESH
ANY = pl.BlockSpec(memory_space=pl.ANY)
BIG = (("wkv", 2, NM, 2 * MW), ("w_in_a", 1, D, SH_A), ("w_out_a", 1, BR_A, SH_O),
       ("w_in_b", 1, D, SH_B), ("w_out_b", 1, BR_B // 4, D))
NBIG = len(BIG)
CW_ROWS = 8


def _place():
    x, y, c = lax.axis_index("x"), lax.axis_index("y"), lax.axis_index("c")
    chips = ((1 - x, y), (x, 1 - y), (1 - x, 1 - y))
    return x, y, c, chips


def _remote(src, dst, ssem, rsem, dev):
    return pltpu.make_async_remote_copy(src_ref=src, dst_ref=dst, send_sem=ssem, recv_sem=rsem,
                                        device_id=dev, device_id_type=MESH)


def _cast_weights(place, ws, after, idx, name):
    nblk = 4
    n = len(idx)
    dims = [BIG[w][1:] for w in idx]

    def body(pref, *refs):
        for i in range(n):
            refs[n + 1 + i][0] = refs[i][...].astype(BF16)

    grid_spec = pltpu.PrefetchScalarGridSpec(
        num_scalar_prefetch=1, grid=(nblk,),
        in_specs=[pl.BlockSpec((k, r // nblk, cdim), lambda i, pref: (0, i, 0)) for k, r, cdim in dims]
        + [pl.BlockSpec(memory_space=pl.ANY)],
        out_specs=[pl.BlockSpec((1, k, r // nblk, cdim), lambda i, pref: (pref[1], 0, i, 0)) for k, r, cdim in dims])
    return pl.pallas_call(
        body, name=name, grid_spec=grid_spec,
        out_shape=[_sds((4, k, r, cdim), BF16) for k, r, cdim in dims],
        compiler_params=_params(("parallel",)),
    )(place, *ws, after)


LAYER_A = (0, 1, 2)
LAYER_B = (3, 4)
HBM = pl.BlockSpec(memory_space=pltpu.HBM)
SEM = pl.BlockSpec(memory_space=pltpu.SEMAPHORE)
EFFECT = pltpu.SideEffectType.DATAFLOW_SIDE_EFFECTING
TOKEN = (8, 128)


def _half(ref, w, which):
    h = BIG[w][2] // 2
    return ref.at[:, pl.ds(which * h, h), :]


def _skip_arg(body, pos, *refs):
    return body(*refs[:pos], *refs[pos + 1:])


def _gather_start(wb, after, idx, name):
    n = len(idx)

    def body(*refs):
        src = refs[:n]
        send_sems, recv_sems = refs[n + 1], refs[n + 2]
        token = refs[2 * n + 3]
        x, y, c, chips = _place()
        me = 2 * x + y
        for j, (px, py) in enumerate(chips):
            for i in range(n):
                mine = _half(src[i].at[me], idx[i], c)
                _remote(mine, mine, send_sems.at[j * n + i], recv_sems.at[j * n + i], (px, py, c)).start()
        token[...] = jnp.zeros(TOKEN, F32)

    outs = pl.pallas_call(
        body, name=name, in_specs=[HBM] * n + [ANY],
        out_specs=(SEM, SEM) + (HBM,) * n + (pl.BlockSpec(memory_space=pltpu.VMEM),),
        out_shape=(pltpu.SemaphoreType.DMA((3 * n,)), pltpu.SemaphoreType.DMA((3 * n,)))
        + tuple(pltpu.HBM(w.shape, w.dtype) for w in wb) + (_sds(TOKEN, F32),),
        input_output_aliases={i: 2 + i for i in range(n)},
        compiler_params=pltpu.CompilerParams(has_side_effects=EFFECT),
    )(*[pltpu.with_memory_space_constraint(w, pltpu.HBM) for w in wb], after)
    return outs[0], outs[1], list(outs[2:2 + n]), outs[2 + n]


def _gather_wait(send_sems, recv_sems, wb, after, idx, name, started=None):
    n = len(idx)
    started = idx if started is None else started
    n_all = len(started)
    pos = [started.index(w) for w in idx]

    def body(*refs):
        buf = refs[:n]
        send_sems, recv_sems = refs[n], refs[n + 1]
        x, y, c, chips = _place()
        me = 2 * x + y
        for j, (px, py) in enumerate(chips):
            for i in range(n):
                mine = _half(buf[i].at[me], idx[i], c)
                got = _half(buf[i].at[2 * px + py], idx[i], c)
                k = j * n_all + pos[i]
                _remote(mine, mine, send_sems.at[k], recv_sems.at[k], (px, py, c)).wait_send()
                _remote(got, got, send_sems.at[k], recv_sems.at[k], (px, py, c)).wait_recv()

    outs = pl.pallas_call(
        body, name=name, in_specs=[HBM] * n + [SEM, SEM] + [ANY] * len(after), out_specs=(HBM,) * n,
        out_shape=tuple(pltpu.HBM(w.shape, w.dtype) for w in wb),
        input_output_aliases={i: i for i in range(n)},
        compiler_params=pltpu.CompilerParams(has_side_effects=EFFECT),
    )(*wb, send_sems, recv_sems, *after)
    return list(outs)


def _gather_forward(wb, idx, name):
    n = len(idx)

    def body(*refs):
        dst = refs[n:2 * n]
        send_sems, recv_sems = refs[2 * n], refs[2 * n + 1]
        x, y, c, chips = _place()
        cps = []
        for j, (px, py) in enumerate(chips):
            for i in range(n):
                got = _half(dst[i].at[2 * px + py], idx[i], c)
                cps.append(_remote(got, got, send_sems.at[j, i], recv_sems.at[j, i], (x, y, 1 - c)))
                cps[-1].start()
        for j, (px, py) in enumerate(chips):
            for i in range(n):
                got = _half(dst[i].at[2 * px + py], idx[i], 1 - c)
                _remote(got, got, send_sems.at[j, i], recv_sems.at[j, i], (x, y, 1 - c)).wait_recv()
        for cp in cps:
            cp.wait_send()

    return pl.pallas_call(
        body, name=name, in_specs=[ANY] * n, out_specs=[ANY] * n, out_shape=[_sds(w.shape, BF16) for w in wb],
        input_output_aliases={i: i for i in range(n)},
        scratch_shapes=[pltpu.SemaphoreType.DMA((3, n)), pltpu.SemaphoreType.DMA((3, n))],
    )(*wb)


def _forward_start(wb, cw, after, idx, name):
    n = len(idx)
    m = n if cw is None else n + 2

    def body(*refs):
        buf = refs[:n]
        send_sems, recv_sems = refs[m + 1], refs[m + 2]
        token = refs[2 * m + 3]
        x, y, c, chips = _place()
        for j, (px, py) in enumerate(chips):
            for i in range(n):
                got = _half(buf[i].at[2 * px + py], idx[i], c)
                _remote(got, got, send_sems.at[j * (n + 1) + i], recv_sems.at[j * (n + 1) + i], (x, y, 1 - c)).start()
            if cw is not None:
                _remote(refs[n], refs[n + 1].at[2 * x + y], send_sems.at[j * (n + 1) + n],
                        recv_sems.at[j * (n + 1) + n], (px, py, c)).start()
        token[...] = jnp.zeros(TOKEN, F32)

    arrays = list(wb) if cw is None else list(wb) + [cw, lax.empty((4, CW_ROWS, SH_O), F32)]
    outs = pl.pallas_call(
        body, name=name, in_specs=[HBM] * m + [ANY],
        out_specs=(SEM, SEM) + (HBM,) * m + (pl.BlockSpec(memory_space=pltpu.VMEM),),
        out_shape=(pltpu.SemaphoreType.DMA((3 * (n + 1),)), pltpu.SemaphoreType.DMA((3 * (n + 1),)))
        + tuple(pltpu.HBM(a.shape, a.dtype) for a in arrays) + (_sds(TOKEN, F32),),
        input_output_aliases={i: 2 + i for i in range(m)},
        compiler_params=pltpu.CompilerParams(has_side_effects=EFFECT),
    )(*[pltpu.with_memory_space_constraint(a, pltpu.HBM) for a in arrays], after)
    return outs[0], outs[1], list(outs[2:2 + m]), outs[2 + m]


def _forward_wait(send_sems, recv_sems, arrays, after, idx, with_cw, name):
    n = len(idx)
    m = len(arrays)

    def body(*refs):
        buf = refs[:n]
        send_sems, recv_sems = refs[m], refs[m + 1]
        x, y, c, chips = _place()
        for j, (px, py) in enumerate(chips):
            for i in range(n):
                sent = _half(buf[i].at[2 * px + py], idx[i], c)
                got = _half(buf[i].at[2 * px + py], idx[i], 1 - c)
                k = j * (n + 1) + i
                _remote(sent, sent, send_sems.at[k], recv_sems.at[k], (x, y, 1 - c)).wait_send()
                _remote(got, got, send_sems.at[k], recv_sems.at[k], (x, y, 1 - c)).wait_recv()
            if with_cw:
                k = j * (n + 1) + n
                theirs = refs[n + 1].at[2 * px + py]
                _remote(refs[n], theirs, send_sems.at[k], recv_sems.at[k], (px, py, c)).wait_send()
                _remote(refs[n], theirs, send_sems.at[k], recv_sems.at[k], (px, py, c)).wait_recv()

    outs = pl.pallas_call(
        body, name=name, in_specs=[HBM] * m + [SEM, SEM] + [ANY] * len(after), out_specs=(HBM,) * m,
        out_shape=tuple(pltpu.HBM(a.shape, a.dtype) for a in arrays),
        input_output_aliases={i: i for i in range(m)},
        compiler_params=pltpu.CompilerParams(has_side_effects=EFFECT),
    )(*arrays, send_sems, recv_sems, *after)
    return list(outs)


def _pair_exchange(gs, idx, name):
    n = len(idx)

    def body(*refs):
        src, dst = refs[:n], refs[n:2 * n]
        send_sems, recv_sems = refs[2 * n:]
        x, y, c, _ = _place()
        cps = []
        for i in range(n):
            h = BIG[idx[i]][2] // 2
            cps.append(_remote(src[i].at[:, :, pl.ds((1 - c) * h, h), :], dst[i], send_sems.at[i], recv_sems.at[i],
                               (x, y, 1 - c)))
            cps[-1].start()
        for cp in cps:
            cp.wait()

    return pl.pallas_call(
        body, name=name, in_specs=[ANY] * n, out_specs=[ANY] * n,
        out_shape=[_sds((4, BIG[w][1], BIG[w][2] // 2, BIG[w][3]), BF16) for w in idx],
        scratch_shapes=[pltpu.SemaphoreType.DMA((n,)), pltpu.SemaphoreType.DMA((n,))],
    )(*gs)


def _pair_start(gs, idx, name):
    n = len(idx)

    def body(*refs):
        src, land = refs[:n], refs[n:2 * n]
        send_sems, recv_sems = refs[2 * n], refs[2 * n + 1]
        token = refs[4 * n + 2]
        x, y, c, _ = _place()
        for i in range(n):
            h = BIG[idx[i]][2] // 2
            _remote(src[i].at[:, :, pl.ds((1 - c) * h, h), :], land[i], send_sems.at[i], recv_sems.at[i],
                    (x, y, 1 - c)).start()
        token[...] = jnp.zeros(TOKEN, F32)

    lands = [lax.empty((4, BIG[w][1], BIG[w][2] // 2, BIG[w][3]), BF16) for w in idx]
    arrays = list(gs) + lands
    outs = pl.pallas_call(
        body, name=name, in_specs=[HBM] * (2 * n),
        out_specs=(SEM, SEM) + (HBM,) * (2 * n) + (pl.BlockSpec(memory_space=pltpu.VMEM),),
        out_shape=(pltpu.SemaphoreType.DMA((n,)), pltpu.SemaphoreType.DMA((n,)))
        + tuple(pltpu.HBM(a.shape, a.dtype) for a in arrays) + (_sds(TOKEN, F32),),
        input_output_aliases={i: 2 + i for i in range(2 * n)},
        compiler_params=pltpu.CompilerParams(has_side_effects=EFFECT),
    )(*[pltpu.with_memory_space_constraint(a, pltpu.HBM) for a in arrays])
    return outs[0], outs[1], list(outs[2:2 + n]), list(outs[2 + n:2 + 2 * n]), outs[2 + 2 * n]


def _pair_wait(send_sems, recv_sems, gs, lands, after, idx, name):
    n = len(idx)

    def body(*refs):
        src, land = refs[:n], refs[n:2 * n]
        send_sems, recv_sems = refs[2 * n], refs[2 * n + 1]
        x, y, c, _ = _place()
        for i in range(n):
            h = BIG[idx[i]][2] // 2
            cp = _remote(src[i].at[:, :, pl.ds((1 - c) * h, h), :], land[i], send_sems.at[i], recv_sems.at[i],
                         (x, y, 1 - c))
            cp.wait_send()
            cp.wait_recv()

    arrays = list(gs) + list(lands)
    outs = pl.pallas_call(
        body, name=name, in_specs=[HBM] * (2 * n) + [SEM, SEM] + [ANY] * len(after), out_specs=(HBM,) * (2 * n),
        out_shape=tuple(pltpu.HBM(a.shape, a.dtype) for a in arrays),
        input_output_aliases={i: i for i in range(2 * n)},
        compiler_params=pltpu.CompilerParams(has_side_effects=EFFECT),
    )(*arrays, send_sems, recv_sems, *after)
    return list(outs[:n]), list(outs[n:])


def _pair_sums(place, gs, r1s, idx, name):
    n = len(idx)
    dims = [(BIG[w][1], BIG[w][2] // 2, BIG[w][3]) for w in idx]

    def body(pref, *refs):
        for i in range(n):
            refs[2 * n + i][...] = (refs[i][...] + refs[n + i][...].astype(F32)).astype(BF16)

    mine = [pl.BlockSpec((1, k, h, cdim), lambda s, pref: (s, 0, pref[0], 0)) for k, h, cdim in dims]
    whole = [pl.BlockSpec((1, k, h, cdim), lambda s, pref: (s, 0, 0, 0)) for k, h, cdim in dims]
    grid_spec = pltpu.PrefetchScalarGridSpec(num_scalar_prefetch=1, grid=(4,), in_specs=mine + whole, out_specs=whole)
    return pl.pallas_call(
        body, name=name, grid_spec=grid_spec, out_shape=[_sds((4, k, h, cdim), BF16) for k, h, cdim in dims],
        compiler_params=_params(("parallel",)),
    )(place, *gs, *r1s)


def _chip_start(ps, idx, name):
    n = len(idx)

    def body(*refs):
        src, land = refs[:n], refs[n:2 * n]
        send_sems, recv_sems = refs[2 * n], refs[2 * n + 1]
        token = refs[4 * n + 2]
        x, y, c, chips = _place()
        for j, (px, py) in enumerate(chips):
            for i in range(n):
                _remote(src[i].at[2 * px + py], land[i].at[j], send_sems.at[j * n + i], recv_sems.at[j * n + i],
                        (px, py, c)).start()
        token[...] = jnp.zeros(TOKEN, F32)

    lands = [lax.empty((3,) + p.shape[1:], BF16) for p in ps]
    outs = pl.pallas_call(
        body, name=name, in_specs=[HBM] * (2 * n),
        out_specs=(SEM, SEM) + (HBM,) * (2 * n) + (pl.BlockSpec(memory_space=pltpu.VMEM),),
        out_shape=(pltpu.SemaphoreType.DMA((3 * n,)), pltpu.SemaphoreType.DMA((3 * n,)))
        + tuple(pltpu.HBM(a.shape, a.dtype) for a in list(ps) + lands) + (_sds(TOKEN, F32),),
        input_output_aliases={i: 2 + i for i in range(2 * n)},
        compiler_params=pltpu.CompilerParams(has_side_effects=EFFECT),
    )(*[pltpu.with_memory_space_constraint(a, pltpu.HBM) for a in list(ps) + lands])
    return outs[0], outs[1], list(outs[2:2 + n]), list(outs[2 + n:2 + 2 * n]), outs[2 + 2 * n]


def _chip_wait(send_sems, recv_sems, ps, lands, after, idx, name):
    n = len(idx)

    def body(*refs):
        src, land = refs[:n], refs[n:2 * n]
        send_sems, recv_sems = refs[2 * n], refs[2 * n + 1]
        x, y, c, chips = _place()
        for j, (px, py) in enumerate(chips):
            for i in range(n):
                cp = _remote(src[i].at[2 * px + py], land[i].at[j], send_sems.at[j * n + i], recv_sems.at[j * n + i],
                             (px, py, c))
                cp.wait_send()
                cp.wait_recv()

    arrays = list(ps) + list(lands)
    outs = pl.pallas_call(
        body, name=name, in_specs=[HBM] * (2 * n) + [SEM, SEM] + [ANY] * len(after), out_specs=(HBM,) * (2 * n),
        out_shape=tuple(pltpu.HBM(a.shape, a.dtype) for a in arrays),
        input_output_aliases={i: i for i in range(2 * n)},
        compiler_params=pltpu.CompilerParams(has_side_effects=EFFECT),
    )(*arrays, send_sems, recv_sems, *after)
    return list(outs[n:])


def _chip_sums(place, gs, r1s, r2s, idx, name):
    n = len(idx)
    dims = [(BIG[w][1], BIG[w][2] // 4, BIG[w][3]) for w in idx]

    def body(pref, *refs):
        for i in range(n):
            acc = refs[i][0] + refs[n + i][0].astype(F32)
            for j in range(3):
                acc = acc + refs[2 * n + i][j].astype(F32)
            refs[3 * n + i][...] = acc

    in_specs = ([pl.BlockSpec((1, k, q, cdim), lambda t, pref: (pref[1], 0, pref[0] * 2 + t, 0)) for k, q, cdim in dims]
                + [pl.BlockSpec((1, k, q, cdim), lambda t, pref: (pref[1], 0, t, 0)) for k, q, cdim in dims]
                + [pl.BlockSpec((3, k, q, cdim), lambda t, pref: (0, 0, t, 0)) for k, q, cdim in dims])
    out_specs = [pl.BlockSpec((k, q, cdim), lambda t, pref: (0, pref[0] * 2 + t, 0)) for k, q, cdim in dims]
    grid_spec = pltpu.PrefetchScalarGridSpec(num_scalar_prefetch=1, grid=(2,), in_specs=in_specs, out_specs=out_specs)
    return pl.pallas_call(
        body, name=name, grid_spec=grid_spec, out_shape=[_sds(BIG[w][1:], F32) for w in idx],
        compiler_params=_params(("parallel",)),
    )(place, *gs, *r1s, *r2s)


def _pair_gather(hs, idx, name):
    n = len(idx)

    def body(*refs):
        dst = refs[n:2 * n]
        send_sems, recv_sems = refs[2 * n:]
        x, y, c, _ = _place()
        cps = []
        for i in range(n):
            mine = _half(dst[i], idx[i], c)
            cps.append(_remote(mine, mine, send_sems.at[i], recv_sems.at[i], (x, y, 1 - c)))
            cps[-1].start()
        for i in range(n):
            theirs = _half(dst[i], idx[i], 1 - c)
            _remote(theirs, theirs, send_sems.at[i], recv_sems.at[i], (x, y, 1 - c)).wait_recv()
        for cp in cps:
            cp.wait_send()

    return pl.pallas_call(
        body, name=name, in_specs=[ANY] * n, out_specs=[ANY] * n,
        out_shape=[_sds(BIG[w][1:], F32) for w in idx],
        input_output_aliases={i: i for i in range(n)},
        scratch_shapes=[pltpu.SemaphoreType.DMA((n,)), pltpu.SemaphoreType.DMA((n,))],
    )(*hs)


SMALL_ROWS = 40


def _adamw_math(w, g, m, v):
    m = ADAM_B1 * m + (1.0 - ADAM_B1) * g
    v = ADAM_B2 * v + (1.0 - ADAM_B2) * (g * g)
    m_hat = m / (1.0 - ADAM_B1 ** ADAM_STEP)
    v_hat = v / (1.0 - ADAM_B2 ** ADAM_STEP)
    delta = -ADAM_LR * (m_hat / (jnp.sqrt(v_hat) + ADAM_EPS) + ADAM_WD * w)
    return delta, m, v


def _small_start(pack, after):
    def body(pack_ref, land_ref, after_ref, send_sems, recv_sems, pack_thru, land_thru, token):
        x, y, c, _ = _place()
        for r in range(1, 8):
            peer = (x if not r & 4 else 1 - x, y if not r & 2 else 1 - y, c if not r & 1 else 1 - c)
            _remote(pack_ref, land_ref.at[r - 1], send_sems.at[r - 1], recv_sems.at[r - 1], peer).start()
        token[...] = jnp.zeros(TOKEN, F32)

    land = lax.empty((7, SMALL_ROWS, D), F32)
    outs = pl.pallas_call(
        body, name="small_start", in_specs=[HBM, HBM, ANY],
        out_specs=(SEM, SEM, HBM, HBM, pl.BlockSpec(memory_space=pltpu.VMEM)),
        out_shape=(pltpu.SemaphoreType.DMA((7,)), pltpu.SemaphoreType.DMA((7,)), pltpu.HBM(pack.shape, F32),
                   pltpu.HBM(land.shape, F32), _sds(TOKEN, F32)),
        input_output_aliases={0: 2, 1: 3},
        compiler_params=pltpu.CompilerParams(has_side_effects=EFFECT),
    )(pltpu.with_memory_space_constraint(pack, pltpu.HBM), pltpu.with_memory_space_constraint(land, pltpu.HBM), after)
    return outs


def _small_wait(send_sems, recv_sems, pack, land, after):
    def body(pack_ref, land_ref, send_sems, recv_sems, *rest):
        x, y, c, _ = _place()
        for r in range(1, 8):
            peer = (x if not r & 4 else 1 - x, y if not r & 2 else 1 - y, c if not r & 1 else 1 - c)
            cp = _remote(pack_ref, land_ref.at[r - 1], send_sems.at[r - 1], recv_sems.at[r - 1], peer)
            cp.wait_send()
            cp.wait_recv()

    return pl.pallas_call(
        body, name="small_wait", in_specs=[HBM, HBM, SEM, SEM] + [ANY] * len(after), out_specs=(HBM, HBM),
        out_shape=(pltpu.HBM(pack.shape, F32), pltpu.HBM(land.shape, F32)),
        input_output_aliases={0: 0, 1: 1},
        compiler_params=pltpu.CompilerParams(has_side_effects=EFFECT),
    )(pack, land, send_sems, recv_sems, *after)


def _small_update(place, pack, land, ws, ms, vs):
    n = len(ws)

    def body(pref, pack_ref, land_ref, *refs):
        chip = pref[1]
        me = 2 * chip + pref[0]
        own = pack_ref[...]
        tot = None
        for dev in range(8):
            r = jnp.bitwise_xor(me, dev)
            term = jnp.where(r == 0, own, land_ref[jnp.maximum(r - 1, 0)])
            tot = term if tot is None else tot + term
        out, buf = refs[3 * n:-1], refs[-1]
        buf[...] = tot
        g_conv = jnp.zeros((3, SH_O), F32)
        for s in range(4):
            g_conv = g_conv + jnp.where(chip == s, buf[24:27, s * SH_O:(s + 1) * SH_O], 0.0)
        gs = [buf[0:2, :], buf[8:10, :], buf[16:17, :], g_conv]
        out[0][...] = buf[32:33, 0:128]
        for i in range(n):
            d, nm, nv = _adamw_math(refs[i][...], gs[i], refs[n + i][...], refs[2 * n + i][...])
            out[1 + i][...] = gs[i]
            out[1 + n + i][...] = d
            out[1 + 2 * n + i][...] = nm
            out[1 + 3 * n + i][...] = nv

    def full(shape):
        nd = len(shape)
        return pl.BlockSpec(shape, lambda i, pref: (0,) * nd)

    specs = [full(w.shape) for w in ws]
    grid_spec = pltpu.PrefetchScalarGridSpec(
        num_scalar_prefetch=1, grid=(1,),
        in_specs=[full(pack.shape), full(land.shape)] + specs * 3, out_specs=[full((1, 128))] + specs * 4,
        scratch_shapes=[pltpu.VMEM((SMALL_ROWS, D), F32)])
    outs = pl.pallas_call(
        body, name="small_update", grid_spec=grid_spec,
        out_shape=[_sds((1, 128), F32)] + [_sds(w.shape, F32) for w in ws] * 4,
        compiler_params=_params(("arbitrary",)),
    )(place, pack, land, *ws, *ms, *vs)
    return outs[0], outs[1:1 + n], outs[1 + n:1 + 2 * n], outs[1 + 2 * n:1 + 3 * n], outs[1 + 3 * n:]


def _adamw_layer(ws, gs, ms, vs, idx, name):
    n = len(idx)
    dims = [(BIG[w][1], BIG[w][2] // 4, BIG[w][3]) for w in idx]

    def body(*refs):
        for i in range(n):
            gv = refs[n + i][...]
            d, nm, nv = _adamw_math(refs[i][...], gv, refs[2 * n + i][...], refs[3 * n + i][...])
            refs[4 * n + i][...] = d
            refs[5 * n + i][...] = nm
            refs[6 * n + i][...] = nv
            refs[7 * n + i][...] = gv

    specs = [pl.BlockSpec((k, q, cdim), lambda t: (0, t, 0)) for k, q, cdim in dims]
    outs = pl.pallas_call(
        body, name=name, grid=(4,), in_specs=specs * 4, out_specs=specs * 4,
        out_shape=[_sds(BIG[w][1:], F32) for w in idx] * 4,
        compiler_params=_params(("parallel",)),
    )(*ws, *gs, *ms, *vs)
    return [tuple(outs[j * n + i] for j in range(4)) for i in range(n)]


def _pad_rows(a, rows):
    return jnp.pad(a, ((0, rows - a.shape[0]), (0, 0)))


def kernel(x, mem, positions, norm_g, mem_norm_g, w_mem_kv, attn_w_in, attn_w_out, conv_w_in, conv_w, conv_w_out, final_g, loss_target, m_norm_g, m_mem_norm_g, m_w_mem_kv, m_attn_w_in, m_attn_w_out, m_conv_w_in, m_conv_w, m_conv_w_out, m_final_g, v_norm_g, v_mem_norm_g, v_w_mem_kv, v_attn_w_in, v_attn_w_out, v_conv_w_in, v_conv_w, v_conv_w_out, v_final_g):
    mx, my, mc = lax.axis_index("x"), lax.axis_index("y"), lax.axis_index("c")
    place = jnp.stack([mc, 2 * mx + my]).astype(jnp.int32)

    w_big = [w_mem_kv, attn_w_in, attn_w_out, conv_w_in, conv_w_out]
    m_big = [m_w_mem_kv, m_attn_w_in, m_attn_w_out, m_conv_w_in, m_conv_w_out]
    v_big = [v_w_mem_kv, v_attn_w_in, v_attn_w_out, v_conv_w_in, v_conv_w_out]
    first, rest = (1,), (0, 2, 3, 4)
    wb1 = _cast_weights(place, [w_big[i] for i in first], place, first, "cast_w_in_a")
    a1_send, a1_recv, a1_bufs, a1_token = _gather_start(wb1, place, first, "gather_a1_start")
    wbr = _cast_weights(place, [w_big[i] for i in rest], a1_token, rest, "cast_weights")
    r_send, r_recv, r_bufs, gb_token = _gather_start(wbr, a1_token, rest, "gather_rest_start")
    a2_send, a2_recv, gb_send, gb_recv = r_send, r_recv, r_send, r_recv
    a2_bufs, gb_bufs = r_bufs[:2], r_bufs[2:]
    started, rest = rest, (0, 2)

    xs, tgt = x[0], loss_target[0]
    g0, g1 = norm_g[0:1], norm_g[1:2]
    rc, rs1, rs2 = _rope_tables(positions[0].astype(F32).reshape(S, 1), gb_token)
    a1_bufs = _gather_wait(a1_send, a1_recv, a1_bufs, [rc], first, "gather_a1_wait")
    w_in_a = _gather_forward(a1_bufs, first, "gather_a1_forward")[0].reshape(4, D, SH_A)
    hn0, q, k, v, qm0, z0 = _in_proj_a(xs, g0, w_in_a, rc, rs1, rs2, gb_token)
    a2_bufs = _gather_wait(a2_send, a2_recv, a2_bufs, [q], rest, "gather_a2_wait", started)
    f2_send, f2_recv, a2_bufs, f2_token = _forward_start(a2_bufs, None, q, rest, "forward_a2_start")
    fwd = [_attn_fwd(q, k, v, 0, f2_token)]
    fwd.append(_attn_fwd(q, k, v, 1, fwd[0][0]))
    cw_own = _pad_rows(conv_w[0], CW_ROWS)
    gb_bufs = _gather_wait(gb_send, gb_recv, gb_bufs, [fwd[1][0]], LAYER_B, "gather_b_wait", started)
    fb_send, fb_recv, gb_bufs, fb_token = _forward_start(gb_bufs, cw_own, fwd[1][0], LAYER_B, "forward_b_start")
    fwd.append(_attn_fwd(q, k, v, 2, fb_token))
    os_, ls, lss = [f[0] for f in fwd], [f[1] for f in fwd], [f[2] for f in fwd]
    wkv_f, w_out_a = _forward_wait(f2_send, f2_recv, a2_bufs, [os_[2]], rest, False, "forward_a2_wait")
    w_out_a = w_out_a.reshape(4, BR_A, SH_O)
    memn, kv = _mem_fwd(mem[0], mem_norm_g, wkv_f)
    h1 = _attn_out(os_, ls, qm0, kv[0], z0, xs, w_out_a)

    w_in_b, w_out_b, _, cw_f = _forward_wait(fb_send, fb_recv, gb_bufs, [h1], LAYER_B, True, "forward_b_wait")
    w_in_b = w_in_b.reshape(4, D, SH_B)
    w_out_b = w_out_b.reshape(BR_B, D)
    cw_f = lax.dynamic_update_slice(cw_f, cw_own[None], (2 * mx + my, 0, 0))
    cw8 = cw_f.transpose(1, 0, 2).reshape(CW_ROWS, D)
    hn1, bg, cg, u, qm1, z1 = _in_proj_b(h1, g1, w_in_b)
    dh2, loss_part, dfg = _conv_out_loss(bg, cg, u, cw8, qm1, kv[1], z1, h1, w_out_b, final_g.reshape(1, D), tgt)

    dproj_b, dw_out_b, dcw, dkv1, dw_out_b16 = _conv_bwd(dh2, bg, cg, u, cw8, qm1, kv[1], z1, w_out_b)
    dw_in_b, dw_in_b16 = _w_in_grad(hn1, dproj_b, IN_B, "w_in_b_grad")
    gs_b = [dw_in_b.reshape(4, 1, D, SH_B), dw_out_b.reshape(4, 1, BR_B // 4, D)]
    gb_b = [dw_in_b16.reshape(4, 1, D, SH_B), dw_out_b16.reshape(4, 1, BR_B // 4, D)]
    pb_send, pb_recv, gb_b, pb_land, pb_token = _pair_start(gb_b, LAYER_B, "pair_b_start")
    dh1, dg1 = _in_proj_bwd(dproj_b, w_in_b, h1, g1, dh2, pb_token, IN_B, "in_proj_b_bwd")
    _, r1_b = _pair_wait(pb_send, pb_recv, gb_b, pb_land, [dh1], LAYER_B, "pair_b_wait")
    ps_b = _pair_sums(place, gs_b, r1_b, LAYER_B, "pair_sums_b")
    cb_send, cb_recv, cb_src, cb_land, cb_token = _chip_start(ps_b, LAYER_B, "chip_b_start")

    outs = _attn_out_bwd(dh1, os_, ls, qm0, kv[0], z0, w_out_a, cb_token)
    dos, dds, dqm, dz, dw_out_a, dkv0, dw_out_a16 = outs[0:3], outs[3:6], outs[6], outs[7], outs[8], outs[9], outs[10]
    bwd = [_attn_bwd(q, k, v, dos[g], lss[g], dds[g], g) for g in range(3)]
    dproj_a = _qkv_bwd([b[0] for b in bwd], [b[1] for b in bwd], [b[2] for b in bwd], dqm, dz, rc, rs1, rs2)
    dw_in_a, dw_in_a16 = _w_in_grad(hn0, dproj_a, IN_A, "w_in_a_grad")
    dwkv, dwkv16, dmg = _mem_bwd(mem[0], mem_norm_g, memn, wkv_f, dkv0, dkv1)

    gs_a = [dwkv, dw_in_a.reshape(4, 1, D, SH_A), dw_out_a.reshape(4, 1, BR_A, SH_O)]
    r1_a = _pair_exchange([dwkv16, dw_in_a16.reshape(4, 1, D, SH_A), dw_out_a16.reshape(4, 1, BR_A, SH_O)], LAYER_A,
                          "pair_exchange_a")
    ps_a = _pair_sums(place, gs_a, r1_a, LAYER_A, "pair_sums_a")
    ca_send, ca_recv, ca_src, ca_land, ca_token = _chip_start(ps_a, LAYER_A, "chip_a_start")

    gx, dg0 = _in_proj_bwd(dproj_a, w_in_a, xs, g0, dh1, ca_token, IN_A, "in_proj_a_bwd")
    pack = jnp.concatenate([_pad_rows(jnp.concatenate([dg0, dg1], axis=0), 8), _pad_rows(dmg, 8), _pad_rows(dfg, 8),
                            dcw, _pad_rows(jnp.pad(loss_part, ((0, 0), (0, D - 128))), 8)], axis=0)
    sm_send, sm_recv, pack, sm_land, sm_token = _small_start(pack, ca_token)
    r2_b = _chip_wait(cb_send, cb_recv, cb_src, cb_land, [ca_token], LAYER_B, "chip_b_wait")
    hs_b = _chip_sums(place, gs_b, r1_b, r2_b, LAYER_B, "chip_sums_b")
    g_b = _pair_gather(hs_b, LAYER_B, "pair_gather_b")
    upd_b = _adamw_layer([w_big[w] for w in LAYER_B], g_b, [m_big[w] for w in LAYER_B], [v_big[w] for w in LAYER_B],
                         LAYER_B, "adamw_b")
    r2_a = _chip_wait(ca_send, ca_recv, ca_src, ca_land, [gx, upd_b[0][0], upd_b[1][0], sm_token], LAYER_A,
                      "chip_a_wait")
    hs_a = _chip_sums(place, gs_a, r1_a, r2_a, LAYER_A, "chip_sums_a")
    g_a = _pair_gather(hs_a, LAYER_A, "pair_gather_a")
    upd_a = _adamw_layer([w_big[w] for w in LAYER_A], g_a, [m_big[w] for w in LAYER_A], [v_big[w] for w in LAYER_A],
                         LAYER_A, "adamw_a")
    upd = upd_a + upd_b
    g_big = [u[3] for u in upd]
    pack, sm_land = _small_wait(sm_send, sm_recv, pack, sm_land, [r2_a[0]])
    sw = [norm_g, mem_norm_g, final_g.reshape(1, D), conv_w[0]]
    sm = [m_norm_g, m_mem_norm_g, m_final_g.reshape(1, D), m_conv_w[0]]
    sv = [v_norm_g, v_mem_norm_g, v_final_g.reshape(1, D), v_conv_w[0]]
    loss_row, sg, sd, snm, snv = _small_update(place, pack, sm_land, sw, sm, sv)
    loss = loss_row[0, 0]
    g_norm, g_memnorm, g_final, g_conv = sg

    def order(norm, memnorm, wkv, w_in_a, w_out_a, w_in_b, conv, w_out_b, final):
        return (norm, memnorm, wkv, w_in_a, w_out_a, w_in_b, conv.reshape(1, 3, SH_O), w_out_b, final.reshape(D))

    grads = order(g_norm, g_memnorm, g_big[0], g_big[1], g_big[2], g_big[3], g_conv, g_big[4], g_final)
    deltas = order(sd[0], sd[1], upd[0][0], upd[1][0], upd[2][0], upd[3][0], sd[3], upd[4][0], sd[2])
    new_m = order(snm[0], snm[1], upd[0][1], upd[1][1], upd[2][1], upd[3][1], snm[3], upd[4][1], snm[2])
    new_v = order(snv[0], snv[1], upd[0][2], upd[1][2], upd[2][2], upd[3][2], snv[3], upd[4][2], snv[2])
    return (loss, gx[None], *grads, *deltas, *new_m, *new_v)
```

```python
import functools

import numpy as np
import jax
import jax.numpy as jnp
from jax import lax
from jax.experimental import pallas as pl
from jax.experimental.pallas import tpu as pltpu

F32 = jnp.float32
BF16 = jnp.bfloat16

S = 2048
D = 1024
TM = 256
NT = S // TM
HD = 64
GW = 512
NQ = 3 * GW
MW = 256
NM = 256
IN_A = 3 * NQ + MW + GW + MW
IN_B = 3 * D + MW + D + MW
BR_A = GW + MW
BR_B = D + MW
SH_A = IN_A // 4
SH_B = IN_B // 4
SH_O = D // 4
QBLK = 128
DILATIONS = (1, 4, 16)
EPS = 1e-6
SCALE = HD ** -0.5
NEG = -1e30
ROPE_THETA = 500000.0

ADAM_LR = 0.001
ADAM_B1 = 0.9
ADAM_B2 = 0.999
ADAM_EPS = 1e-08
ADAM_WD = 0.01
ADAM_STEP = 10

VMEM_LIMIT_BYTES = 60 * 1024 * 1024


def _params(sem=None):
    if sem is None:
        return pltpu.CompilerParams(vmem_limit_bytes=VMEM_LIMIT_BYTES)
    return pltpu.CompilerParams(dimension_semantics=sem, vmem_limit_bytes=VMEM_LIMIT_BYTES)


def _full(shape):
    nd = len(shape)
    return pl.BlockSpec(shape, lambda *_: (0,) * nd)


def _rows(width, tm=TM):
    return pl.BlockSpec((tm, width), lambda i: (i, 0))


def _sds(shape, dtype):
    return jax.ShapeDtypeStruct(shape, dtype)


def _silu_parts(z):
    sig = 0.5 * jnp.tanh(0.5 * z) + 0.5
    return z * sig, sig * (1.0 + z * (1.0 - sig))


def _dot(a, b):
    return jnp.dot(a, b, preferred_element_type=F32)


def _dot_nt(a, b):
    return lax.dot_general(a, b, (((1,), (1,)), ((), ())), preferred_element_type=F32)


def _dot_tn(a, b):
    return lax.dot_general(a, b, (((0,), (0,)), ((), ())), preferred_element_type=F32)


def _rope_fwd(t, c, s1, s2):
    return t * c + pltpu.roll(t, 120, 1) * s1 + pltpu.roll(t, 8, 1) * s2


def _rope_bwd(g, c, s1, s2):
    return g * c + pltpu.roll(g * s1, 8, 1) + pltpu.roll(g * s2, 120, 1)


MEM_HEADS = MW // HD


def _stack_heads(x):
    head = lax.broadcasted_iota(jnp.int32, x.shape, 1) // HD
    return jnp.concatenate([jnp.where(head == h, x, 0.0) for h in range(MEM_HEADS)], axis=0).astype(BF16)


def _unstack_heads(x4):
    tm = x4.shape[0] // MEM_HEADS
    head = lax.broadcasted_iota(jnp.int32, (tm, MW), 1) // HD
    out = x4[:tm]
    for h in range(1, MEM_HEADS):
        out = jnp.where(head == h, x4[h * tm:(h + 1) * tm], out)
    return out


def _mem_attn(qm, kv):
    q4 = _stack_heads(qm.astype(F32))
    s = _dot_nt(q4, kv[:, :MW]) * SCALE
    e = jnp.exp(s - jnp.max(s, axis=-1, keepdims=True))
    p = e * (1.0 / jnp.sum(e, axis=-1, keepdims=True))
    return p, _unstack_heads(_dot(p.astype(BF16), kv[:, MW:])), q4


def _mem_attn_bwd(dmo, p, mo, q4, kv, dkv_ref):
    tm = dmo.shape[0]
    head = lax.broadcasted_iota(jnp.int32, dmo.shape, 1) // HD
    prod = dmo * mo
    delta = jnp.concatenate([jnp.sum(jnp.where(head == h, prod, 0.0), axis=-1, keepdims=True)
                             for h in range(MEM_HEADS)], axis=0)
    d4 = _stack_heads(dmo)
    ds = (p * (_dot_nt(d4, kv[:, MW:]) - delta) * SCALE).astype(BF16)
    dkv_ref[:, :MW] += _dot_tn(ds, q4)
    dkv_ref[:, MW:] += _dot_tn(p.astype(BF16), d4)
    return _unstack_heads(_dot(ds, kv[:, :MW]))


def _merge(o_refs, l_refs):
    ls = [r[...] for r in l_refs]
    m = jnp.maximum(jnp.maximum(ls[0], ls[1]), ls[2])
    es = [jnp.exp(l - m) for l in ls]
    inv = 1.0 / (es[0] + es[1] + es[2])
    ws = [e * inv for e in es]
    os_ = [r[...] for r in o_refs]
    mix = ws[0] * os_[0] + ws[1] * os_[1] + ws[2] * os_[2]
    return ws, mix


def _conv_taps(cg, u, cgp, up, first):
    a = cg * u
    ap = jnp.where(first, 0.0, cgp * up)
    row = lax.broadcasted_iota(jnp.int32, a.shape, 0)
    a1 = jnp.where(row == 0, ap[7:8, :], pltpu.roll(a, 1, 0))
    a2 = jnp.where(row == 0, ap[6:7, :], jnp.where(row == 1, ap[7:8, :], pltpu.roll(a, 2, 0)))
    return a, a1, a2


def _rope_tables(posf, after):
    half = 8
    invf = np.float32(ROPE_THETA) ** (-np.arange(half, dtype=np.float32) * np.float32(2.0 / 16))
    lane = np.arange(128)
    table = np.where((lane % HD) < 16, invf[lane % half], 0.0).astype(np.float32)[None, :]

    def body(pos_ref, invf_ref, c_ref, s1_ref, s2_ref):
        ang = pos_ref[...] * invf_ref[...]
        jm = lax.broadcasted_iota(jnp.int32, ang.shape, 1) & (HD - 1)
        cs = jnp.cos(ang)
        sn = jnp.sin(ang)
        c_ref[...] = jnp.where(jm < 16, cs, 1.0)
        s1_ref[...] = jnp.where(jm < 8, -sn, 0.0)
        s2_ref[...] = jnp.where((jm >= 8) & (jm < 16), sn, 0.0)

    out = _sds((S, 128), F32)
    return pl.pallas_call(
        functools.partial(_skip_arg, body, 2), name="rope_tables", grid=(NT,),
        in_specs=[_rows(1), _full((1, 128)), pl.BlockSpec(memory_space=pl.ANY)],
        out_specs=[_rows(128)] * 3, out_shape=[out] * 3,
        compiler_params=_params(("parallel",)),
    )(posf, jnp.asarray(table), after)


def _in_proj_a(x, g0, w_in, c, s1, s2, after):
    def body(x_ref, g_ref, w_ref, c_ref, s1_ref, s2_ref, hn_ref, q_ref, k_ref, v_ref, qm_ref, z_ref, proj):
        xf = x_ref[...]
        hn = xf * lax.rsqrt(jnp.mean(xf * xf, axis=-1, keepdims=True) + EPS) * g_ref[...]
        hb = hn.astype(BF16)
        hn_ref[...] = hb
        for s in range(4):
            proj[:, s * SH_A:(s + 1) * SH_A] = _dot(hb, w_ref[s])
        cc, a1, a2 = c_ref[...], s1_ref[...], s2_ref[...]
        for j in range(NQ // 128):
            q_ref[:, j * 128:(j + 1) * 128] = (
                _rope_fwd(proj[:, j * 128:(j + 1) * 128], cc, a1, a2) * SCALE).astype(BF16)
            k_ref[:, j * 128:(j + 1) * 128] = _rope_fwd(
                proj[:, NQ + j * 128:NQ + (j + 1) * 128], cc, a1, a2).astype(BF16)
        v_ref[...] = proj[:, 2 * NQ:3 * NQ].astype(BF16)
        qm_ref[...] = proj[:, 3 * NQ:3 * NQ + MW].astype(BF16)
        z_ref[...] = proj[:, 3 * NQ + MW:]

    return pl.pallas_call(
        functools.partial(_skip_arg, body, 6), name="in_proj_a", grid=(NT,),
        in_specs=[_rows(D), _full((1, D)), _full((4, D, SH_A)), _rows(128), _rows(128), _rows(128),
                  pl.BlockSpec(memory_space=pl.ANY)],
        out_specs=[_rows(D), _rows(NQ), _rows(NQ), _rows(NQ), _rows(MW), _rows(BR_A)],
        out_shape=[_sds((S, D), BF16), _sds((S, NQ), BF16), _sds((S, NQ), BF16), _sds((S, NQ), BF16),
                   _sds((S, MW), BF16), _sds((S, BR_A), F32)],
        scratch_shapes=[pltpu.VMEM((TM, IN_A), F32)],
        compiler_params=_params(("parallel",)),
    )(x, g0, w_in, c, s1, s2, after)


def _mem_fwd(mem, mg, wkv):
    def body(mem_ref, mg_ref, w_ref, memn_ref, kv_ref):
        mf = mem_ref[...]
        n = mf * lax.rsqrt(jnp.mean(mf * mf, axis=-1, keepdims=True) + EPS)
        for i in range(2):
            mn = (n * mg_ref[i:i + 1, :]).astype(BF16)
            memn_ref[i] = mn
            acc = _dot(mn[:, 0:NM], w_ref[0, i])
            for s in range(1, 4):
                acc += _dot(mn[:, s * NM:(s + 1) * NM], w_ref[s, i])
            kv_ref[i] = acc.astype(BF16)

    return pl.pallas_call(
        body, name="mem_fwd", grid=(1,),
        in_specs=[_full((NM, D)), _full((2, D)), _full((4, 2, NM, 2 * MW))],
        out_specs=[_full((2, NM, D)), _full((2, NM, 2 * MW))],
        out_shape=[_sds((2, NM, D), BF16), _sds((2, NM, 2 * MW), BF16)],
        compiler_params=_params(("arbitrary",)),
    )(mem, mg, wkv)


def _band_mask(j):
    qi = lax.broadcasted_iota(jnp.int32, (QBLK, 2 * QBLK), 0)
    kj = lax.broadcasted_iota(jnp.int32, (QBLK, 2 * QBLK), 1)
    dist = qi + QBLK - kj
    return (dist >= 0) & (dist <= QBLK) & ((kj >= QBLK) | (j > 0))


LANES = 128
NCHUNK = GW // LANES
FWD_UNROLL = 16
BWD_UNROLL = 16
CONV_CHUNK = 256


def _perm_matrix(d):
    n = TM // d
    p = np.zeros((TM, TM), np.float32)
    for r in range(d):
        for i in range(n):
            p[r * n + i, i * d + r] = 1.0
    return p


def _split_dot(p, x):
    hi = x.astype(BF16)
    lo = (x - hi.astype(F32)).astype(BF16)
    both = _dot(p, jnp.concatenate([hi, lo], axis=1))
    return both[:, :LANES] + both[:, LANES:]


def _pair_dot(p, a, b):
    both = _dot(p, jnp.concatenate([a, b], axis=1))
    return both[:, :LANES], both[:, LANES:]


def _tile_to_streams(y, dst, t, d):
    n, ln = TM // d, S // d
    for r in range(d):
        dst[r * ln + t * n:r * ln + (t + 1) * n, :] = y[r * n:(r + 1) * n].astype(dst.dtype)


def _tile_from_streams(src, t, d):
    n, ln = TM // d, S // d
    return jnp.concatenate([src[r * ln + t * n:r * ln + (t + 1) * n, :] for r in range(d)], axis=0)


def _head_masks():
    first = lax.broadcasted_iota(jnp.int32, (TM, LANES), 1) < HD
    return first, jnp.logical_not(first)


def _attn_fwd(q, k, v, g, after):
    d = DILATIONS[g]
    nb = S // d // QBLK
    perm = _perm_matrix(d)

    def body(q_ref, k_ref, v_ref, p_ref, pt_ref, o_ref, l_ref, ls_ref, q0, q1, ks, vs, os_):
        first, second = _head_masks()
        pm = p_ref[...]
        for t in range(NT):
            rows = slice(t * TM, (t + 1) * TM)
            if d == 1:
                qt = q_ref[rows, :].astype(F32)
            else:
                qt, kt = _pair_dot(pm, q_ref[rows, :], k_ref[rows, :])
                _tile_to_streams(kt, ks, t, d)
                if t % 2 == 0:
                    va, vb = _pair_dot(pm, v_ref[rows, :], v_ref[(t + 1) * TM:(t + 2) * TM, :])
                    _tile_to_streams(va, vs, t, d)
                    _tile_to_streams(vb, vs, t + 1, d)
            _tile_to_streams(jnp.where(first, qt, 0.0), q0, t, d)
            _tile_to_streams(jnp.where(second, qt, 0.0), q1, t, d)
        kref, vref = (k_ref, v_ref) if d == 1 else (ks, vs)
        oref, lref = (o_ref, l_ref) if d == 1 else (os_, ls_ref)

        def blk(b, carry):
            r0 = pl.multiple_of(b * QBLK, QBLK)
            p0 = pl.multiple_of(jnp.maximum(b - 1, 0) * QBLK, QBLK)
            kk = jnp.concatenate([kref[pl.ds(p0, QBLK), :], kref[pl.ds(r0, QBLK), :]], axis=0)
            vv = jnp.concatenate([vref[pl.ds(p0, QBLK), :], vref[pl.ds(r0, QBLK), :]], axis=0)
            valid = _band_mask(b & (nb - 1))
            acc, lse = [], []
            for qh in (q0, q1):
                s = jnp.where(valid, _dot_nt(qh[pl.ds(r0, QBLK), :], kk), NEG)
                m = jnp.max(s, axis=-1, keepdims=True)
                e = jnp.exp(s - m)
                l = jnp.sum(e, axis=-1, keepdims=True)
                acc.append(_dot(e.astype(BF16), vv) * (1.0 / l))
                lse.append(m + jnp.log(l))
            f = first[:QBLK]
            oref[pl.ds(r0, QBLK), :] = jnp.where(f, acc[0], acc[1])
            lref[pl.ds(r0, QBLK), :] = jnp.where(f, lse[0], lse[1])
            return carry

        lax.fori_loop(0, S // QBLK, blk, 0, unroll=FWD_UNROLL)
        if d > 1:
            ptm = pt_ref[...]
            for t in range(NT):
                rows = slice(t * TM, (t + 1) * TM)
                o_ref[rows, :] = _split_dot(ptm, _tile_from_streams(os_, t, d))
                l_ref[rows, :] = _split_dot(ptm, _tile_from_streams(ls_ref, t, d))

    qkv_spec = pl.BlockSpec((S, LANES), lambda c: (0, g * NCHUNK + c))
    out_spec = pl.BlockSpec((S, LANES), lambda c: (0, c))
    streams = [pltpu.VMEM((S, LANES), BF16)] * 4
    if d == 1:
        inner, out_shape, scratch = functools.partial(_drop_arg, body, 7), [_sds((S, GW), F32)] * 2, streams
    else:
        inner, out_shape, scratch = body, [_sds((S, GW), F32)] * 3 + [_sds((S, GW), BF16)] * 4, []
    outs = pl.pallas_call(
        functools.partial(_skip_arg, inner, 5), name=f"attn_fwd_g{g}", grid=(NCHUNK,),
        in_specs=[qkv_spec] * 3 + [_full((TM, TM))] * 2 + [pl.BlockSpec(memory_space=pl.ANY)],
        out_specs=[out_spec] * len(out_shape), out_shape=out_shape,
        scratch_shapes=scratch + [pltpu.VMEM((S, LANES), F32)],
        compiler_params=_params(("parallel",)),
    )(q, k, v, jnp.asarray(perm, BF16), jnp.asarray(perm.T, BF16), after)
    return (outs[0], outs[1], outs[1], None) if d == 1 else (outs[0], outs[1], outs[2], tuple(outs[3:]))


def _drop_arg(body, pos, *refs):
    return body(*refs[:pos], None, *refs[pos:])


def _attn_out(os_, ls, qm, kv0, z, x, w_out):
    def body(o0, o1, o2, l0, l1, l2, qm_ref, kv_ref, z_ref, x_ref, w_ref, h_ref, ybuf):
        _, mix = _merge((o0, o1, o2), (l0, l1, l2))
        sz, _ = _silu_parts(z_ref[...])
        ybuf[:, :GW] = (mix * sz[:, :GW]).astype(BF16)
        _, mo, _ = _mem_attn(qm_ref[...], kv_ref[...])
        ybuf[:, GW:] = (mo * sz[:, GW:]).astype(BF16)
        yb = ybuf[...]
        for s in range(4):
            cs = slice(s * SH_O, (s + 1) * SH_O)
            h_ref[:, cs] = x_ref[:, cs] + _dot(yb, w_ref[s])

    return pl.pallas_call(
        body, name="attn_out", grid=(NT,),
        in_specs=[_rows(GW)] * 6 + [_rows(MW), _full((NM, 2 * MW)), _rows(BR_A), _rows(D), _full((4, BR_A, SH_O))],
        out_specs=_rows(D), out_shape=_sds((S, D), F32),
        scratch_shapes=[pltpu.VMEM((TM, BR_A), BF16)],
        compiler_params=_params(("parallel",)),
    )(*os_, *ls, qm, kv0, z, x, w_out)


def _in_proj_b(h1, g1, w_in):
    def body(x_ref, g_ref, w_ref, hn_ref, bg_ref, cg_ref, u_ref, qm_ref, z_ref, proj):
        xf = x_ref[...]
        hn = xf * lax.rsqrt(jnp.mean(xf * xf, axis=-1, keepdims=True) + EPS) * g_ref[...]
        hb = hn.astype(BF16)
        hn_ref[...] = hb
        for s in range(4):
            proj[:, s * SH_B:(s + 1) * SH_B] = _dot(hb, w_ref[s])
        bg_ref[...] = proj[:, :D]
        cg_ref[...] = proj[:, D:2 * D]
        u_ref[...] = proj[:, 2 * D:3 * D]
        qm_ref[...] = proj[:, 3 * D:3 * D + MW].astype(BF16)
        z_ref[...] = proj[:, 3 * D + MW:]

    return pl.pallas_call(
        body, name="in_proj_b", grid=(NT,),
        in_specs=[_rows(D), _full((1, D)), _full((4, D, SH_B))],
        out_specs=[_rows(D), _rows(D), _rows(D), _rows(D), _rows(MW), _rows(BR_B)],
        out_shape=[_sds((S, D), BF16), _sds((S, D), F32), _sds((S, D), F32), _sds((S, D), F32),
                   _sds((S, MW), BF16), _sds((S, BR_B), F32)],
        scratch_shapes=[pltpu.VMEM((TM, IN_B), F32)],
        compiler_params=_params(("parallel",)),
    )(h1, g1, w_in)


def _prev8(width):
    return pl.BlockSpec((8, width), lambda i: (jnp.maximum(i * (TM // 8) - 1, 0), 0))


def _conv_out_loss(bg, cg, u, cw, qm, kv1, z, h1, w_out, fg, tgt):
    def body(bg_ref, cg_ref, u_ref, cgp_ref, up_ref, cw_ref, qm_ref, kv_ref, z_ref, h_ref, w_ref, fg_ref, t_ref,
             dh_ref, loss_ref, dfg_ref, ybuf):
        i = pl.program_id(0)
        a, a1, a2 = _conv_taps(cg_ref[...], u_ref[...], cgp_ref[...], up_ref[...], i == 0)
        conv = cw_ref[0:1, :] * a2 + cw_ref[1:2, :] * a1 + cw_ref[2:3, :] * a
        sz, _ = _silu_parts(z_ref[...])
        ybuf[:, :D] = (bg_ref[...] * conv * sz[:, :D]).astype(BF16)
        _, mo, _ = _mem_attn(qm_ref[...], kv_ref[...])
        ybuf[:, D:] = (mo * sz[:, D:]).astype(BF16)
        h2 = h_ref[...] + _dot(ybuf[...], w_ref[...])
        rstd = lax.rsqrt(jnp.mean(h2 * h2, axis=-1, keepdims=True) + EPS)
        n = h2 * rstd
        fgv = fg_ref[...]
        err = n * fgv - t_ref[...]
        dout = err * (1.0 / D)
        dn = dout * fgv
        dh_ref[...] = rstd * (dn - n * jnp.mean(dn * n, axis=-1, keepdims=True))

        @pl.when(i == 0)
        def _():
            loss_ref[...] = jnp.zeros_like(loss_ref)
            dfg_ref[...] = jnp.zeros_like(dfg_ref)

        loss_ref[...] += jnp.sum(err * err) * (0.5 / D)
        dfg_ref[...] += jnp.sum(dout * n, axis=0, keepdims=True)

    return pl.pallas_call(
        body, name="conv_out_loss", grid=(NT,),
        in_specs=[_rows(D), _rows(D), _rows(D), _prev8(D), _prev8(D), _full((8, D)), _rows(MW),
                  _full((NM, 2 * MW)), _rows(BR_B), _rows(D), _full((BR_B, D)), _full((1, D)), _rows(D)],
        out_specs=[_rows(D), _full((1, 128)), _full((1, D))],
        out_shape=[_sds((S, D), F32), _sds((1, 128), F32), _sds((1, D), F32)],
        scratch_shapes=[pltpu.VMEM((TM, BR_B), BF16)],
        compiler_params=_params(("arbitrary",)),
    )(bg, cg, u, cg, u, cw, qm, kv1, z, h1, w_out, fg, tgt)


def _conv_bwd(dh2, bg, cg, u, cw, qm, kv1, z, w_out):
    rev = lambda i: (NT - 1 - i, 0)
    rows = lambda w: pl.BlockSpec((TM, w), rev)
    prev8 = pl.BlockSpec((8, D), lambda i: (jnp.maximum((NT - 1 - i) * (TM // 8) - 1, 0), 0))

    def body(dh_ref, bg_ref, cg_ref, u_ref, cgp_ref, up_ref, cw_ref, qm_ref, kv_ref, z_ref, w_ref,
             dproj_ref, dw_ref, dcw_ref, dkv_ref, dwb_ref, ybuf, carry):
        i = pl.program_id(0)

        @pl.when(i == 0)
        def _():
            dw_ref[...] = jnp.zeros_like(dw_ref)
            dcw_ref[...] = jnp.zeros_like(dcw_ref)
            dkv_ref[...] = jnp.zeros_like(dkv_ref)
            carry[...] = jnp.zeros_like(carry)

        dhb = dh_ref[...].astype(BF16)
        dy = _dot_nt(dhb, w_ref[...])
        kvv = kv_ref[...]
        p, mo, q4 = _mem_attn(qm_ref[...], kvv)
        szm, dszm = _silu_parts(z_ref[:, D:])
        ybuf[:, D:] = (mo * szm).astype(BF16)
        dym = dy[:, D:]
        dproj_ref[:, 3 * D + MW + D:] = (dym * mo * dszm).astype(BF16)
        first_tile = i == NT - 1
        for c in range(D // CONV_CHUNK):
            cs = slice(c * CONV_CHUNK, (c + 1) * CONV_CHUNK)
            bgv, cgv, uv = bg_ref[:, cs], cg_ref[:, cs], u_ref[:, cs]
            a, a1, a2 = _conv_taps(cgv, uv, cgp_ref[:, cs], up_ref[:, cs], first_tile)
            w0, w1, w2 = cw_ref[0:1, cs], cw_ref[1:2, cs], cw_ref[2:3, cs]
            conv = w0 * a2 + w1 * a1 + w2 * a
            mix = bgv * conv
            sz, dsz = _silu_parts(z_ref[:, cs])
            ybuf[:, cs] = (mix * sz).astype(BF16)
            dyc = dy[:, cs]
            dproj_ref[:, 3 * D + MW + c * CONV_CHUNK:3 * D + MW + (c + 1) * CONV_CHUNK] = (
                dyc * mix * dsz).astype(BF16)
            dmix = dyc * sz
            dproj_ref[:, cs] = (dmix * conv).astype(BF16)
            dc = dmix * bgv
            nxt = carry[:, cs]
            row = lax.broadcasted_iota(jnp.int32, dc.shape, 0)
            dc1 = jnp.where(row == TM - 1, nxt[0:1, :], pltpu.roll(dc, TM - 1, 0))
            dc2 = jnp.where(row == TM - 2, nxt[0:1, :],
                            jnp.where(row == TM - 1, nxt[1:2, :], pltpu.roll(dc, TM - 2, 0)))
            carry[:, cs] = dc[0:8, :]
            da = w2 * dc + w1 * dc1 + w0 * dc2
            dproj_ref[:, D + c * CONV_CHUNK:D + (c + 1) * CONV_CHUNK] = (da * uv).astype(BF16)
            dproj_ref[:, 2 * D + c * CONV_CHUNK:2 * D + (c + 1) * CONV_CHUNK] = (da * cgv).astype(BF16)
            dcw_ref[0:1, cs] += jnp.sum(dc * a2, axis=0, keepdims=True)
            dcw_ref[1:2, cs] += jnp.sum(dc * a1, axis=0, keepdims=True)
            dcw_ref[2:3, cs] += jnp.sum(dc * a, axis=0, keepdims=True)
        dw_ref[...] += _dot_tn(ybuf[...], dhb)
        dproj_ref[:, 3 * D:3 * D + MW] = _mem_attn_bwd(dym * szm, p, mo, q4, kvv, dkv_ref).astype(BF16)

        @pl.when(i == NT - 1)
        def _():
            dwb_ref[...] = dw_ref[...].astype(BF16)

    return pl.pallas_call(
        body, name="conv_bwd", grid=(NT,),
        in_specs=[rows(D), rows(D), rows(D), rows(D), prev8, prev8, _full((8, D)), rows(MW),
                  _full((NM, 2 * MW)), rows(BR_B), _full((BR_B, D))],
        out_specs=[rows(IN_B), _full((BR_B, D)), _full((8, D)), _full((NM, 2 * MW)), _full((BR_B, D))],
        out_shape=[_sds((S, IN_B), BF16), _sds((BR_B, D), F32), _sds((8, D), F32), _sds((NM, 2 * MW), F32),
                   _sds((BR_B, D), BF16)],
        scratch_shapes=[pltpu.VMEM((TM, BR_B), BF16), pltpu.VMEM((8, D), F32)],
        compiler_params=_params(("arbitrary",)),
    )(dh2, bg, cg, u, cg, u, cw, qm, kv1, z, w_out)


def _in_proj_bwd(dproj, w_in, xin, g, dres, after, width, name):
    sh = width // 4

    def body(dp_ref, w_ref, x_ref, g_ref, dr_ref, dx_ref, dg_ref):
        i = pl.program_id(0)
        dhn = _dot_nt(dp_ref[:, 0:sh], w_ref[0])
        for s in range(1, 4):
            dhn += _dot_nt(dp_ref[:, s * sh:(s + 1) * sh], w_ref[s])
        xf = x_ref[...]
        rstd = lax.rsqrt(jnp.mean(xf * xf, axis=-1, keepdims=True) + EPS)
        n = xf * rstd
        dn = dhn * g_ref[...]
        dx_ref[...] = dr_ref[...] + rstd * (dn - n * jnp.mean(dn * n, axis=-1, keepdims=True))

        @pl.when(i == 0)
        def _():
            dg_ref[...] = jnp.zeros_like(dg_ref)

        dg_ref[...] += jnp.sum(dhn * n, axis=0, keepdims=True)

    return pl.pallas_call(
        functools.partial(_skip_arg, body, 5), name=name, grid=(NT,),
        in_specs=[_rows(width), _full((4, D, sh)), _rows(D), _full((1, D)), _rows(D), pl.BlockSpec(memory_space=pl.ANY)],
        out_specs=[_rows(D), _full((1, D))],
        out_shape=[_sds((S, D), F32), _sds((1, D), F32)],
        compiler_params=_params(("arbitrary",)),
    )(dproj, w_in, xin, g, dres, after)


def _w_in_grad(hn, dproj, width, name):
    sh = width // 4

    def body(hn_ref, dp_ref, dw_ref, dwb_ref):
        dw = _dot_tn(hn_ref[...], dp_ref[...])
        dw_ref[0] = dw
        dwb_ref[0] = dw.astype(BF16)

    spec = pl.BlockSpec((1, D, sh), lambda s: (s, 0, 0))
    return pl.pallas_call(
        body, name=name, grid=(4,),
        in_specs=[_full((S, D)), pl.BlockSpec((S, sh), lambda s: (0, s))],
        out_specs=[spec, spec], out_shape=[_sds((4, D, sh), F32), _sds((4, D, sh), BF16)],
        compiler_params=_params(("parallel",)),
    )(hn, dproj)


def _attn_out_bwd(dh1, os_, ls, qm, kv0, z, w_out, after):
    ones_bd = np.kron(np.eye(GW // HD, dtype=np.float32), np.ones((HD, HD), np.float32))

    def body(dh_ref, o0, o1, o2, l0, l1, l2, qm_ref, kv_ref, z_ref, w_ref, bd_ref,
             do0, do1, do2, dd0, dd1, dd2, dqm_ref, dz_ref, dw_ref, dkv_ref, dwb_ref, ybuf):
        i = pl.program_id(0)

        @pl.when(i == 0)
        def _():
            dw_ref[...] = jnp.zeros_like(dw_ref)
            dkv_ref[...] = jnp.zeros_like(dkv_ref)

        ws, mix = _merge((o0, o1, o2), (l0, l1, l2))
        sz, dsz = _silu_parts(z_ref[...])
        kvv = kv_ref[...]
        p, mo, q4 = _mem_attn(qm_ref[...], kvv)
        ybuf[:, :GW] = (mix * sz[:, :GW]).astype(BF16)
        ybuf[:, GW:] = (mo * sz[:, GW:]).astype(BF16)
        yb = ybuf[...]
        dh = dh_ref[...]
        dy = None
        for s in range(4):
            dhb = dh[:, s * SH_O:(s + 1) * SH_O].astype(BF16)
            dw_ref[s] += _dot_tn(yb, dhb)
            part = _dot_nt(dhb, w_ref[s])
            dy = part if dy is None else dy + part
        dcat = dy * sz
        dz_ref[:, :GW] = (dy[:, :GW] * mix * dsz[:, :GW]).astype(BF16)
        dz_ref[:, GW:] = (dy[:, GW:] * mo * dsz[:, GW:]).astype(BF16)
        dmix = dcat[:, :GW]
        prod = dmix * mix
        hi = prod.astype(BF16)
        lo = (prod - hi.astype(F32)).astype(BF16)
        bd = bd_ref[...]
        tot = _dot(hi, bd) + _dot(lo, bd)
        for w, do_ref, dd_ref in zip(ws, (do0, do1, do2), (dd0, dd1, dd2)):
            do_ref[...] = (w * dmix).astype(BF16)
            dd_ref[...] = w * tot

        dqm_ref[...] = _mem_attn_bwd(dcat[:, GW:], p, mo, q4, kvv, dkv_ref).astype(BF16)

        @pl.when(i == NT - 1)
        def _():
            dwb_ref[...] = dw_ref[...].astype(BF16)

    return pl.pallas_call(
        functools.partial(_skip_arg, body, 12), name="attn_out_bwd", grid=(NT,),
        in_specs=[_rows(D)] + [_rows(GW)] * 6 + [_rows(MW), _full((NM, 2 * MW)), _rows(BR_A),
                                                   _full((4, BR_A, SH_O)), _full((GW, GW)),
                                                   pl.BlockSpec(memory_space=pl.ANY)],
        out_specs=[_rows(GW)] * 6 + [_rows(MW), _rows(BR_A), _full((4, BR_A, SH_O)), _full((NM, 2 * MW)),
                                     _full((4, BR_A, SH_O))],
        out_shape=[_sds((S, GW), BF16)] * 3 + [_sds((S, GW), F32)] * 3 + [
            _sds((S, MW), BF16), _sds((S, BR_A), BF16), _sds((4, BR_A, SH_O), F32), _sds((NM, 2 * MW), F32),
            _sds((4, BR_A, SH_O), BF16)],
        scratch_shapes=[pltpu.VMEM((TM, BR_A), BF16)],
        compiler_params=_params(("arbitrary",)),
    )(dh1, *os_, *ls, qm, kv0, z, w_out, jnp.asarray(ones_bd, dtype=BF16), after)


def _attn_bwd(q, k, v, do, lse_s, dd, g, streams):
    d = DILATIONS[g]
    nb = S // d // QBLK
    perm = _perm_matrix(d)
    n_src = 3 if d == 1 else 4

    def body(*refs):
        do_ref, l_ref, dd_ref, p_ref, pt_ref, dq_ref, dk_ref, dv_ref = refs[n_src:n_src + 8]
        first, second = _head_masks()
        if d == 1:
            q_ref, kref, vref = refs[:n_src]
            q0, q1, g0, g1, dqs, dks, dvs = refs[n_src + 8:]
            ddref = dd_ref
            for t in range(NT):
                rows = slice(t * TM, (t + 1) * TM)
                qt = q_ref[rows, :].astype(F32)
                gt = do_ref[rows, :].astype(F32)
                q0[rows, :] = jnp.where(first, qt, 0.0).astype(BF16)
                q1[rows, :] = jnp.where(second, qt, 0.0).astype(BF16)
                g0[rows, :] = jnp.where(first, gt, 0.0).astype(BF16)
                g1[rows, :] = jnp.where(second, gt, 0.0).astype(BF16)
        else:
            q0, q1, kref, vref = refs[:n_src]
            g0, g1, ddref, dqs, dks, dvs = refs[n_src + 8:]
            pm = p_ref[...]
            for t in range(0, NT, 2):
                pair = _pair_dot(pm, do_ref[t * TM:(t + 1) * TM, :], do_ref[(t + 1) * TM:(t + 2) * TM, :])
                for u, gt in enumerate(pair):
                    _tile_to_streams(jnp.where(first, gt, 0.0), g0, t + u, d)
                    _tile_to_streams(jnp.where(second, gt, 0.0), g1, t + u, d)
                    _tile_to_streams(_split_dot(pm, dd_ref[(t + u) * TM:(t + u + 1) * TM, :]), ddref, t + u, d)
        dqref, dkref, dvref = dqs, dks, dvs
        dkref[...] = jnp.zeros_like(dkref)
        dvref[...] = jnp.zeros_like(dvref)

        def blk(b, carry):
            r0 = pl.multiple_of(b * QBLK, QBLK)
            p0 = pl.multiple_of(jnp.maximum(b - 1, 0) * QBLK, QBLK)
            kk = jnp.concatenate([kref[pl.ds(p0, QBLK), :], kref[pl.ds(r0, QBLK), :]], axis=0)
            vv = jnp.concatenate([vref[pl.ds(p0, QBLK), :], vref[pl.ds(r0, QBLK), :]], axis=0)
            lb = l_ref[pl.ds(r0, QBLK), :]
            ddb = ddref[pl.ds(r0, QBLK), :]
            lcol = jnp.concatenate([lb[:, 0:1], lb[:, HD:HD + 1]], axis=0)
            dcol = jnp.concatenate([ddb[:, 0:1], ddb[:, HD:HD + 1]], axis=0)
            valid = _band_mask(b & (nb - 1))
            valid2 = jnp.concatenate([valid, valid], axis=0)
            qq = jnp.concatenate([q0[pl.ds(r0, QBLK), :], q1[pl.ds(r0, QBLK), :]], axis=0)
            gg = jnp.concatenate([g0[pl.ds(r0, QBLK), :], g1[pl.ds(r0, QBLK), :]], axis=0)
            p = jnp.where(valid2, jnp.exp(_dot_nt(qq, kk) - lcol), 0.0)
            ds = (p * (_dot_nt(gg, vv) - dcol)).astype(BF16)
            dq2 = _dot(ds, kk)
            dqref[pl.ds(r0, QBLK), :] = jnp.where(first[:QBLK], dq2[:QBLK], dq2[QBLK:])
            dkk = _dot_tn(ds, qq)
            dvv = _dot_tn(p.astype(BF16), gg)
            dkref[pl.ds(p0, QBLK), :] += dkk[:QBLK]
            dkref[pl.ds(r0, QBLK), :] += dkk[QBLK:]
            dvref[pl.ds(p0, QBLK), :] += dvv[:QBLK]
            dvref[pl.ds(r0, QBLK), :] += dvv[QBLK:]
            return carry

        lax.fori_loop(0, S // QBLK, blk, 0, unroll=BWD_UNROLL)

        ptm = pt_ref[...] if d > 1 else None
        for t in range(NT):
            rows = slice(t * TM, (t + 1) * TM)
            if d == 1:
                dq_ref[rows, :] = dqs[rows, :].astype(BF16)
                dk_ref[rows, :] = dks[rows, :].astype(BF16)
                dv_ref[rows, :] = dvs[rows, :].astype(BF16)
            else:
                tq, tk = _pair_dot(ptm, _tile_from_streams(dqs, t, d).astype(BF16),
                                   _tile_from_streams(dks, t, d).astype(BF16))
                dq_ref[rows, :] = tq.astype(BF16)
                dk_ref[rows, :] = tk.astype(BF16)
                if t % 2 == 0:
                    ta, tb = _pair_dot(ptm, _tile_from_streams(dvs, t, d).astype(BF16),
                                       _tile_from_streams(dvs, t + 1, d).astype(BF16))
                    dv_ref[rows, :] = ta.astype(BF16)
                    dv_ref[(t + 1) * TM:(t + 2) * TM, :] = tb.astype(BF16)

    qkv_spec = pl.BlockSpec((S, LANES), lambda c: (0, g * NCHUNK + c))
    one_spec = pl.BlockSpec((S, LANES), lambda c: (0, c))
    if d == 1:
        src, src_specs, n_bf16, n_f32 = (q, k, v), [qkv_spec] * 3, 4, 3
    else:
        src, src_specs, n_bf16, n_f32 = streams, [one_spec] * 4, 2, 4
    return pl.pallas_call(
        body, name=f"attn_bwd_g{g}", grid=(NCHUNK,),
        in_specs=src_specs + [one_spec] * 3 + [_full((TM, TM))] * 2, out_specs=[one_spec] * 3,
        out_shape=[_sds((S, GW), BF16)] * 3,
        scratch_shapes=[pltpu.VMEM((S, LANES), BF16)] * n_bf16 + [pltpu.VMEM((S, LANES), F32)] * n_f32,
        compiler_params=_params(("parallel",)),
    )(*src, do, lse_s, dd, jnp.asarray(perm, BF16), jnp.asarray(perm.T, BF16))


def _qkv_bwd(dqs, dks, dvs, dqm, dz, c, s1, s2):
    def body(q0, q1, q2, k0, k1, k2, v0, v1, v2, dqm_ref, dz_ref, c_ref, s1_ref, s2_ref, dp_ref):
        cc, a1, a2 = c_ref[...], s1_ref[...], s2_ref[...]
        for g, (qr, kr, vr) in enumerate(((q0, k0, v0), (q1, k1, v1), (q2, k2, v2))):
            for j in range(GW // 128):
                ls_ = slice(j * 128, (j + 1) * 128)
                c0 = g * GW + j * 128
                dp_ref[:, c0:c0 + 128] = (_rope_bwd(qr[:, ls_].astype(F32), cc, a1, a2) * SCALE).astype(BF16)
                dp_ref[:, NQ + c0:NQ + c0 + 128] = _rope_bwd(kr[:, ls_].astype(F32), cc, a1, a2).astype(BF16)
            dp_ref[:, 2 * NQ + g * GW:2 * NQ + (g + 1) * GW] = vr[...]
        dp_ref[:, 3 * NQ:3 * NQ + MW] = dqm_ref[...]
        dp_ref[:, 3 * NQ + MW:] = dz_ref[...]

    return pl.pallas_call(
        body, name="qkv_bwd", grid=(NT,),
        in_specs=[_rows(GW)] * 9 + [_rows(MW), _rows(BR_A), _rows(128), _rows(128), _rows(128)],
        out_specs=_rows(IN_A), out_shape=_sds((S, IN_A), BF16),
        compiler_params=_params(("parallel",)),
    )(*dqs, *dks, *dvs, dqm, dz, c, s1, s2)


def _mem_bwd(mem, mg, memn, wkv, dkv0, dkv1):
    def body(mem_ref, mg_ref, memn_ref, w_ref, d0_ref, d1_ref, dw_ref, dwb_ref, dg_ref):
        mf = mem_ref[...]
        n = mf * lax.rsqrt(jnp.mean(mf * mf, axis=-1, keepdims=True) + EPS)
        for i, d_ref in enumerate((d0_ref, d1_ref)):
            dkv = d_ref[...].astype(BF16)
            mn = memn_ref[i]
            for s in range(4):
                cs = slice(s * NM, (s + 1) * NM)
                dw = _dot_tn(mn[:, cs], dkv)
                dw_ref[s, i] = dw
                dwb_ref[s, i] = dw.astype(BF16)
                dmn = _dot_nt(dkv, w_ref[s, i])
                dg_ref[i:i + 1, cs] = jnp.sum(dmn * n[:, cs], axis=0, keepdims=True)

    return pl.pallas_call(
        body, name="mem_bwd", grid=(1,),
        in_specs=[_full((NM, D)), _full((2, D)), _full((2, NM, D)), _full((4, 2, NM, 2 * MW)),
                  _full((NM, 2 * MW)), _full((NM, 2 * MW))],
        out_specs=[_full((4, 2, NM, 2 * MW)), _full((4, 2, NM, 2 * MW)), _full((2, D))],
        out_shape=[_sds((4, 2, NM, 2 * MW), F32), _sds((4, 2, NM, 2 * MW), BF16), _sds((2, D), F32)],
        compiler_params=_params(("arbitrary",)),
    )(mem, mg, memn, wkv, dkv0, dkv1)


MESH = pl.DeviceIdType.MESH
ANY = pl.BlockSpec(memory_space=pl.ANY)
BIG = (("wkv", 2, NM, 2 * MW), ("w_in_a", 1, D, SH_A), ("w_out_a", 1, BR_A, SH_O),
       ("w_in_b", 1, D, SH_B), ("w_out_b", 1, BR_B // 4, D))
NBIG = len(BIG)
CW_ROWS = 8


def _place():
    x, y, c = lax.axis_index("x"), lax.axis_index("y"), lax.axis_index("c")
    chips = ((1 - x, y), (x, 1 - y), (1 - x, 1 - y))
    return x, y, c, chips


def _remote(src, dst, ssem, rsem, dev):
    return pltpu.make_async_remote_copy(src_ref=src, dst_ref=dst, send_sem=ssem, recv_sem=rsem,
                                        device_id=dev, device_id_type=MESH)


def _cast_weights(place, ws, after, idx, name):
    nblk = 4
    n = len(idx)
    dims = [BIG[w][1:] for w in idx]

    def body(pref, *refs):
        for i in range(n):
            refs[n + 1 + i][0] = refs[i][...].astype(BF16)

    grid_spec = pltpu.PrefetchScalarGridSpec(
        num_scalar_prefetch=1, grid=(nblk,),
        in_specs=[pl.BlockSpec((k, r // nblk, cdim), lambda i, pref: (0, i, 0)) for k, r, cdim in dims]
        + [pl.BlockSpec(memory_space=pl.ANY)],
        out_specs=[pl.BlockSpec((1, k, r // nblk, cdim), lambda i, pref: (pref[1], 0, i, 0)) for k, r, cdim in dims])
    return pl.pallas_call(
        body, name=name, grid_spec=grid_spec,
        out_shape=[_sds((4, k, r, cdim), BF16) for k, r, cdim in dims],
        compiler_params=_params(("parallel",)),
    )(place, *ws, after)


LAYER_A = (0, 1, 2)
LAYER_B = (3, 4)
HBM = pl.BlockSpec(memory_space=pltpu.HBM)
SEM = pl.BlockSpec(memory_space=pltpu.SEMAPHORE)
EFFECT = pltpu.SideEffectType.DATAFLOW_SIDE_EFFECTING
TOKEN = (8, 128)


def _half(ref, w, which):
    h = BIG[w][2] // 2
    return ref.at[:, pl.ds(which * h, h), :]


def _skip_arg(body, pos, *refs):
    return body(*refs[:pos], *refs[pos + 1:])


def _gather_start(wb, after, idx, name):
    n = len(idx)

    def body(*refs):
        src = refs[:n]
        send_sems, recv_sems = refs[n + 1], refs[n + 2]
        token = refs[2 * n + 3]
        x, y, c, chips = _place()
        me = 2 * x + y
        for j, (px, py) in enumerate(chips):
            for i in range(n):
                mine = _half(src[i].at[me], idx[i], c)
                _remote(mine, mine, send_sems.at[j * n + i], recv_sems.at[j * n + i], (px, py, c)).start()
        token[...] = jnp.zeros(TOKEN, F32)

    outs = pl.pallas_call(
        body, name=name, in_specs=[HBM] * n + [ANY],
        out_specs=(SEM, SEM) + (HBM,) * n + (pl.BlockSpec(memory_space=pltpu.VMEM),),
        out_shape=(pltpu.SemaphoreType.DMA((3 * n,)), pltpu.SemaphoreType.DMA((3 * n,)))
        + tuple(pltpu.HBM(w.shape, w.dtype) for w in wb) + (_sds(TOKEN, F32),),
        input_output_aliases={i: 2 + i for i in range(n)},
        compiler_params=pltpu.CompilerParams(has_side_effects=EFFECT),
    )(*[pltpu.with_memory_space_constraint(w, pltpu.HBM) for w in wb], after)
    return outs[0], outs[1], list(outs[2:2 + n]), outs[2 + n]


def _gather_wait(send_sems, recv_sems, wb, after, idx, name, started=None):
    n = len(idx)
    started = idx if started is None else started
    n_all = len(started)
    pos = [started.index(w) for w in idx]

    def body(*refs):
        buf = refs[:n]
        send_sems, recv_sems = refs[n], refs[n + 1]
        x, y, c, chips = _place()
        me = 2 * x + y
        for j, (px, py) in enumerate(chips):
            for i in range(n):
                mine = _half(buf[i].at[me], idx[i], c)
                got = _half(buf[i].at[2 * px + py], idx[i], c)
                k = j * n_all + pos[i]
                _remote(mine, mine, send_sems.at[k], recv_sems.at[k], (px, py, c)).wait_send()
                _remote(got, got, send_sems.at[k], recv_sems.at[k], (px, py, c)).wait_recv()

    outs = pl.pallas_call(
        body, name=name, in_specs=[HBM] * n + [SEM, SEM] + [ANY] * len(after), out_specs=(HBM,) * n,
        out_shape=tuple(pltpu.HBM(w.shape, w.dtype) for w in wb),
        input_output_aliases={i: i for i in range(n)},
        compiler_params=pltpu.CompilerParams(has_side_effects=EFFECT),
    )(*wb, send_sems, recv_sems, *after)
    return list(outs)


def _gather_forward(wb, idx, name):
    n = len(idx)

    def body(*refs):
        dst = refs[n:2 * n]
        send_sems, recv_sems = refs[2 * n], refs[2 * n + 1]
        x, y, c, chips = _place()
        cps = []
        for j, (px, py) in enumerate(chips):
            for i in range(n):
                got = _half(dst[i].at[2 * px + py], idx[i], c)
                cps.append(_remote(got, got, send_sems.at[j, i], recv_sems.at[j, i], (x, y, 1 - c)))
                cps[-1].start()
        for j, (px, py) in enumerate(chips):
            for i in range(n):
                got = _half(dst[i].at[2 * px + py], idx[i], 1 - c)
                _remote(got, got, send_sems.at[j, i], recv_sems.at[j, i], (x, y, 1 - c)).wait_recv()
        for cp in cps:
            cp.wait_send()

    return pl.pallas_call(
        body, name=name, in_specs=[ANY] * n, out_specs=[ANY] * n, out_shape=[_sds(w.shape, BF16) for w in wb],
        input_output_aliases={i: i for i in range(n)},
        scratch_shapes=[pltpu.SemaphoreType.DMA((3, n)), pltpu.SemaphoreType.DMA((3, n))],
    )(*wb)


def _forward_start(wb, cw, after, idx, name):
    n = len(idx)
    m = n if cw is None else n + 2

    def body(*refs):
        buf = refs[:n]
        send_sems, recv_sems = refs[m + 1], refs[m + 2]
        token = refs[2 * m + 3]
        x, y, c, chips = _place()
        for j, (px, py) in enumerate(chips):
            for i in range(n):
                got = _half(buf[i].at[2 * px + py], idx[i], c)
                _remote(got, got, send_sems.at[j * (n + 1) + i], recv_sems.at[j * (n + 1) + i], (x, y, 1 - c)).start()
            if cw is not None:
                _remote(refs[n], refs[n + 1].at[2 * x + y], send_sems.at[j * (n + 1) + n],
                        recv_sems.at[j * (n + 1) + n], (px, py, c)).start()
        token[...] = jnp.zeros(TOKEN, F32)

    arrays = list(wb) if cw is None else list(wb) + [cw, lax.empty((4, CW_ROWS, SH_O), F32)]
    outs = pl.pallas_call(
        body, name=name, in_specs=[HBM] * m + [ANY],
        out_specs=(SEM, SEM) + (HBM,) * m + (pl.BlockSpec(memory_space=pltpu.VMEM),),
        out_shape=(pltpu.SemaphoreType.DMA((3 * (n + 1),)), pltpu.SemaphoreType.DMA((3 * (n + 1),)))
        + tuple(pltpu.HBM(a.shape, a.dtype) for a in arrays) + (_sds(TOKEN, F32),),
        input_output_aliases={i: 2 + i for i in range(m)},
        compiler_params=pltpu.CompilerParams(has_side_effects=EFFECT),
    )(*[pltpu.with_memory_space_constraint(a, pltpu.HBM) for a in arrays], after)
    return outs[0], outs[1], list(outs[2:2 + m]), outs[2 + m]


def _forward_wait(send_sems, recv_sems, arrays, after, idx, with_cw, name):
    n = len(idx)
    m = len(arrays)

    def body(*refs):
        buf = refs[:n]
        send_sems, recv_sems = refs[m], refs[m + 1]
        x, y, c, chips = _place()
        for j, (px, py) in enumerate(chips):
            for i in range(n):
                sent = _half(buf[i].at[2 * px + py], idx[i], c)
                got = _half(buf[i].at[2 * px + py], idx[i], 1 - c)
                k = j * (n + 1) + i
                _remote(sent, sent, send_sems.at[k], recv_sems.at[k], (x, y, 1 - c)).wait_send()
                _remote(got, got, send_sems.at[k], recv_sems.at[k], (x, y, 1 - c)).wait_recv()
            if with_cw:
                k = j * (n + 1) + n
                theirs = refs[n + 1].at[2 * px + py]
                _remote(refs[n], theirs, send_sems.at[k], recv_sems.at[k], (px, py, c)).wait_send()
                _remote(refs[n], theirs, send_sems.at[k], recv_sems.at[k], (px, py, c)).wait_recv()

    outs = pl.pallas_call(
        body, name=name, in_specs=[HBM] * m + [SEM, SEM] + [ANY] * len(after), out_specs=(HBM,) * m,
        out_shape=tuple(pltpu.HBM(a.shape, a.dtype) for a in arrays),
        input_output_aliases={i: i for i in range(m)},
        compiler_params=pltpu.CompilerParams(has_side_effects=EFFECT),
    )(*arrays, send_sems, recv_sems, *after)
    return list(outs)


def _pair_exchange(gs, idx, name):
    n = len(idx)

    def body(*refs):
        src, dst = refs[:n], refs[n:2 * n]
        send_sems, recv_sems = refs[2 * n:]
        x, y, c, _ = _place()
        cps = []
        for i in range(n):
            h = BIG[idx[i]][2] // 2
            cps.append(_remote(src[i].at[:, :, pl.ds((1 - c) * h, h), :], dst[i], send_sems.at[i], recv_sems.at[i],
                               (x, y, 1 - c)))
            cps[-1].start()
        for cp in cps:
            cp.wait()

    return pl.pallas_call(
        body, name=name, in_specs=[ANY] * n, out_specs=[ANY] * n,
        out_shape=[_sds((4, BIG[w][1], BIG[w][2] // 2, BIG[w][3]), BF16) for w in idx],
        scratch_shapes=[pltpu.SemaphoreType.DMA((n,)), pltpu.SemaphoreType.DMA((n,))],
    )(*gs)


def _pair_start(gs, idx, name):
    n = len(idx)

    def body(*refs):
        src, land = refs[:n], refs[n:2 * n]
        send_sems, recv_sems = refs[2 * n], refs[2 * n + 1]
        token = refs[4 * n + 2]
        x, y, c, _ = _place()
        for i in range(n):
            h = BIG[idx[i]][2] // 2
            _remote(src[i].at[:, :, pl.ds((1 - c) * h, h), :], land[i], send_sems.at[i], recv_sems.at[i],
                    (x, y, 1 - c)).start()
        token[...] = jnp.zeros(TOKEN, F32)

    lands = [lax.empty((4, BIG[w][1], BIG[w][2] // 2, BIG[w][3]), BF16) for w in idx]
    arrays = list(gs) + lands
    outs = pl.pallas_call(
        body, name=name, in_specs=[HBM] * (2 * n),
        out_specs=(SEM, SEM) + (HBM,) * (2 * n) + (pl.BlockSpec(memory_space=pltpu.VMEM),),
        out_shape=(pltpu.SemaphoreType.DMA((n,)), pltpu.SemaphoreType.DMA((n,)))
        + tuple(pltpu.HBM(a.shape, a.dtype) for a in arrays) + (_sds(TOKEN, F32),),
        input_output_aliases={i: 2 + i for i in range(2 * n)},
        compiler_params=pltpu.CompilerParams(has_side_effects=EFFECT),
    )(*[pltpu.with_memory_space_constraint(a, pltpu.HBM) for a in arrays])
    return outs[0], outs[1], list(outs[2:2 + n]), list(outs[2 + n:2 + 2 * n]), outs[2 + 2 * n]


def _pair_wait(send_sems, recv_sems, gs, lands, after, idx, name):
    n = len(idx)

    def body(*refs):
        src, land = refs[:n], refs[n:2 * n]
        send_sems, recv_sems = refs[2 * n], refs[2 * n + 1]
        x, y, c, _ = _place()
        for i in range(n):
            h = BIG[idx[i]][2] // 2
            cp = _remote(src[i].at[:, :, pl.ds((1 - c) * h, h), :], land[i], send_sems.at[i], recv_sems.at[i],
                         (x, y, 1 - c))
            cp.wait_send()
            cp.wait_recv()

    arrays = list(gs) + list(lands)
    outs = pl.pallas_call(
        body, name=name, in_specs=[HBM] * (2 * n) + [SEM, SEM] + [ANY] * len(after), out_specs=(HBM,) * (2 * n),
        out_shape=tuple(pltpu.HBM(a.shape, a.dtype) for a in arrays),
        input_output_aliases={i: i for i in range(2 * n)},
        compiler_params=pltpu.CompilerParams(has_side_effects=EFFECT),
    )(*arrays, send_sems, recv_sems, *after)
    return list(outs[:n]), list(outs[n:])


def _pair_sums(place, gs, r1s, idx, name):
    n = len(idx)
    dims = [(BIG[w][1], BIG[w][2] // 2, BIG[w][3]) for w in idx]

    def body(pref, *refs):
        for i in range(n):
            refs[2 * n + i][...] = (refs[i][...] + refs[n + i][...].astype(F32)).astype(BF16)

    mine = [pl.BlockSpec((1, k, h, cdim), lambda s, pref: (s, 0, pref[0], 0)) for k, h, cdim in dims]
    whole = [pl.BlockSpec((1, k, h, cdim), lambda s, pref: (s, 0, 0, 0)) for k, h, cdim in dims]
    grid_spec = pltpu.PrefetchScalarGridSpec(num_scalar_prefetch=1, grid=(4,), in_specs=mine + whole, out_specs=whole)
    return pl.pallas_call(
        body, name=name, grid_spec=grid_spec, out_shape=[_sds((4, k, h, cdim), BF16) for k, h, cdim in dims],
        compiler_params=_params(("parallel",)),
    )(place, *gs, *r1s)


def _chip_start(ps, idx, name):
    n = len(idx)

    def body(*refs):
        src, land = refs[:n], refs[n:2 * n]
        send_sems, recv_sems = refs[2 * n], refs[2 * n + 1]
        token = refs[4 * n + 2]
        x, y, c, chips = _place()
        for j, (px, py) in enumerate(chips):
            for i in range(n):
                _remote(src[i].at[2 * px + py], land[i].at[j], send_sems.at[j * n + i], recv_sems.at[j * n + i],
                        (px, py, c)).start()
        token[...] = jnp.zeros(TOKEN, F32)

    lands = [lax.empty((3,) + p.shape[1:], BF16) for p in ps]
    outs = pl.pallas_call(
        body, name=name, in_specs=[HBM] * (2 * n),
        out_specs=(SEM, SEM) + (HBM,) * (2 * n) + (pl.BlockSpec(memory_space=pltpu.VMEM),),
        out_shape=(pltpu.SemaphoreType.DMA((3 * n,)), pltpu.SemaphoreType.DMA((3 * n,)))
        + tuple(pltpu.HBM(a.shape, a.dtype) for a in list(ps) + lands) + (_sds(TOKEN, F32),),
        input_output_aliases={i: 2 + i for i in range(2 * n)},
        compiler_params=pltpu.CompilerParams(has_side_effects=EFFECT),
    )(*[pltpu.with_memory_space_constraint(a, pltpu.HBM) for a in list(ps) + lands])
    return outs[0], outs[1], list(outs[2:2 + n]), list(outs[2 + n:2 + 2 * n]), outs[2 + 2 * n]


def _chip_wait(send_sems, recv_sems, ps, lands, after, idx, name):
    n = len(idx)

    def body(*refs):
        src, land = refs[:n], refs[n:2 * n]
        send_sems, recv_sems = refs[2 * n], refs[2 * n + 1]
        x, y, c, chips = _place()
        for j, (px, py) in enumerate(chips):
            for i in range(n):
                cp = _remote(src[i].at[2 * px + py], land[i].at[j], send_sems.at[j * n + i], recv_sems.at[j * n + i],
                             (px, py, c))
                cp.wait_send()
                cp.wait_recv()

    arrays = list(ps) + list(lands)
    outs = pl.pallas_call(
        body, name=name, in_specs=[HBM] * (2 * n) + [SEM, SEM] + [ANY] * len(after), out_specs=(HBM,) * (2 * n),
        out_shape=tuple(pltpu.HBM(a.shape, a.dtype) for a in arrays),
        input_output_aliases={i: i for i in range(2 * n)},
        compiler_params=pltpu.CompilerParams(has_side_effects=EFFECT),
    )(*arrays, send_sems, recv_sems, *after)
    return list(outs[n:])


def _chip_sums(place, gs, r1s, r2s, idx, name):
    n = len(idx)
    dims = [(BIG[w][1], BIG[w][2] // 4, BIG[w][3]) for w in idx]

    def body(pref, *refs):
        for i in range(n):
            acc = refs[i][0] + refs[n + i][0].astype(F32)
            for j in range(3):
                acc = acc + refs[2 * n + i][j].astype(F32)
            refs[3 * n + i][...] = acc

    in_specs = ([pl.BlockSpec((1, k, q, cdim), lambda t, pref: (pref[1], 0, pref[0] * 2 + t, 0)) for k, q, cdim in dims]
                + [pl.BlockSpec((1, k, q, cdim), lambda t, pref: (pref[1], 0, t, 0)) for k, q, cdim in dims]
                + [pl.BlockSpec((3, k, q, cdim), lambda t, pref: (0, 0, t, 0)) for k, q, cdim in dims])
    out_specs = [pl.BlockSpec((k, q, cdim), lambda t, pref: (0, pref[0] * 2 + t, 0)) for k, q, cdim in dims]
    grid_spec = pltpu.PrefetchScalarGridSpec(num_scalar_prefetch=1, grid=(2,), in_specs=in_specs, out_specs=out_specs)
    return pl.pallas_call(
        body, name=name, grid_spec=grid_spec, out_shape=[_sds(BIG[w][1:], F32) for w in idx],
        compiler_params=_params(("parallel",)),
    )(place, *gs, *r1s, *r2s)


def _pair_gather(hs, idx, name):
    n = len(idx)

    def body(*refs):
        dst = refs[n:2 * n]
        send_sems, recv_sems = refs[2 * n:]
        x, y, c, _ = _place()
        cps = []
        for i in range(n):
            mine = _half(dst[i], idx[i], c)
            cps.append(_remote(mine, mine, send_sems.at[i], recv_sems.at[i], (x, y, 1 - c)))
            cps[-1].start()
        for i in range(n):
            theirs = _half(dst[i], idx[i], 1 - c)
            _remote(theirs, theirs, send_sems.at[i], recv_sems.at[i], (x, y, 1 - c)).wait_recv()
        for cp in cps:
            cp.wait_send()

    return pl.pallas_call(
        body, name=name, in_specs=[ANY] * n, out_specs=[ANY] * n,
        out_shape=[_sds(BIG[w][1:], F32) for w in idx],
        input_output_aliases={i: i for i in range(n)},
        scratch_shapes=[pltpu.SemaphoreType.DMA((n,)), pltpu.SemaphoreType.DMA((n,))],
    )(*hs)


SMALL_ROWS = 40


def _adamw_math(w, g, m, v):
    m = ADAM_B1 * m + (1.0 - ADAM_B1) * g
    v = ADAM_B2 * v + (1.0 - ADAM_B2) * (g * g)
    m_hat = m / (1.0 - ADAM_B1 ** ADAM_STEP)
    v_hat = v / (1.0 - ADAM_B2 ** ADAM_STEP)
    delta = -ADAM_LR * (m_hat / (jnp.sqrt(v_hat) + ADAM_EPS) + ADAM_WD * w)
    return delta, m, v


def _small_start(pack, after):
    def body(pack_ref, land_ref, after_ref, send_sems, recv_sems, pack_thru, land_thru, token):
        x, y, c, _ = _place()
        for r in range(1, 8):
            peer = (x if not r & 4 else 1 - x, y if not r & 2 else 1 - y, c if not r & 1 else 1 - c)
            _remote(pack_ref, land_ref.at[r - 1], send_sems.at[r - 1], recv_sems.at[r - 1], peer).start()
        token[...] = jnp.zeros(TOKEN, F32)

    land = lax.empty((7, SMALL_ROWS, D), F32)
    outs = pl.pallas_call(
        body, name="small_start", in_specs=[HBM, HBM, ANY],
        out_specs=(SEM, SEM, HBM, HBM, pl.BlockSpec(memory_space=pltpu.VMEM)),
        out_shape=(pltpu.SemaphoreType.DMA((7,)), pltpu.SemaphoreType.DMA((7,)), pltpu.HBM(pack.shape, F32),
                   pltpu.HBM(land.shape, F32), _sds(TOKEN, F32)),
        input_output_aliases={0: 2, 1: 3},
        compiler_params=pltpu.CompilerParams(has_side_effects=EFFECT),
    )(pltpu.with_memory_space_constraint(pack, pltpu.HBM), pltpu.with_memory_space_constraint(land, pltpu.HBM), after)
    return outs


def _small_wait(send_sems, recv_sems, pack, land, after):
    def body(pack_ref, land_ref, send_sems, recv_sems, *rest):
        x, y, c, _ = _place()
        for r in range(1, 8):
            peer = (x if not r & 4 else 1 - x, y if not r & 2 else 1 - y, c if not r & 1 else 1 - c)
            cp = _remote(pack_ref, land_ref.at[r - 1], send_sems.at[r - 1], recv_sems.at[r - 1], peer)
            cp.wait_send()
            cp.wait_recv()

    return pl.pallas_call(
        body, name="small_wait", in_specs=[HBM, HBM, SEM, SEM] + [ANY] * len(after), out_specs=(HBM, HBM),
        out_shape=(pltpu.HBM(pack.shape, F32), pltpu.HBM(land.shape, F32)),
        input_output_aliases={0: 0, 1: 1},
        compiler_params=pltpu.CompilerParams(has_side_effects=EFFECT),
    )(pack, land, send_sems, recv_sems, *after)


def _small_update(place, pack, land, ws, ms, vs):
    n = len(ws)

    def body(pref, pack_ref, land_ref, *refs):
        chip = pref[1]
        me = 2 * chip + pref[0]
        own = pack_ref[...]
        tot = None
        for dev in range(8):
            r = jnp.bitwise_xor(me, dev)
            term = jnp.where(r == 0, own, land_ref[jnp.maximum(r - 1, 0)])
            tot = term if tot is None else tot + term
        out, buf = refs[3 * n:-1], refs[-1]
        buf[...] = tot
        g_conv = jnp.zeros((3, SH_O), F32)
        for s in range(4):
            g_conv = g_conv + jnp.where(chip == s, buf[24:27, s * SH_O:(s + 1) * SH_O], 0.0)
        gs = [buf[0:2, :], buf[8:10, :], buf[16:17, :], g_conv]
        out[0][...] = buf[32:33, 0:128]
        for i in range(n):
            d, nm, nv = _adamw_math(refs[i][...], gs[i], refs[n + i][...], refs[2 * n + i][...])
            out[1 + i][...] = gs[i]
            out[1 + n + i][...] = d
            out[1 + 2 * n + i][...] = nm
            out[1 + 3 * n + i][...] = nv

    def full(shape):
        nd = len(shape)
        return pl.BlockSpec(shape, lambda i, pref: (0,) * nd)

    specs = [full(w.shape) for w in ws]
    grid_spec = pltpu.PrefetchScalarGridSpec(
        num_scalar_prefetch=1, grid=(1,),
        in_specs=[full(pack.shape), full(land.shape)] + specs * 3, out_specs=[full((1, 128))] + specs * 4,
        scratch_shapes=[pltpu.VMEM((SMALL_ROWS, D), F32)])
    outs = pl.pallas_call(
        body, name="small_update", grid_spec=grid_spec,
        out_shape=[_sds((1, 128), F32)] + [_sds(w.shape, F32) for w in ws] * 4,
        compiler_params=_params(("arbitrary",)),
    )(place, pack, land, *ws, *ms, *vs)
    return outs[0], outs[1:1 + n], outs[1 + n:1 + 2 * n], outs[1 + 2 * n:1 + 3 * n], outs[1 + 3 * n:]


def _adamw_layer(ws, gs, ms, vs, idx, name):
    n = len(idx)
    dims = [(BIG[w][1], BIG[w][2] // 4, BIG[w][3]) for w in idx]

    def body(*refs):
        for i in range(n):
            gv = refs[n + i][...]
            d, nm, nv = _adamw_math(refs[i][...], gv, refs[2 * n + i][...], refs[3 * n + i][...])
            refs[4 * n + i][...] = d
            refs[5 * n + i][...] = nm
            refs[6 * n + i][...] = nv
            refs[7 * n + i][...] = gv

    specs = [pl.BlockSpec((k, q, cdim), lambda t: (0, t, 0)) for k, q, cdim in dims]
    outs = pl.pallas_call(
        body, name=name, grid=(4,), in_specs=specs * 4, out_specs=specs * 4,
        out_shape=[_sds(BIG[w][1:], F32) for w in idx] * 4,
        compiler_params=_params(("parallel",)),
    )(*ws, *gs, *ms, *vs)
    return [tuple(outs[j * n + i] for j in range(4)) for i in range(n)]


def _pad_rows(a, rows):
    return jnp.pad(a, ((0, rows - a.shape[0]), (0, 0)))


def kernel(x, mem, positions, norm_g, mem_norm_g, w_mem_kv, attn_w_in, attn_w_out, conv_w_in, conv_w, conv_w_out, final_g, loss_target, m_norm_g, m_mem_norm_g, m_w_mem_kv, m_attn_w_in, m_attn_w_out, m_conv_w_in, m_conv_w, m_conv_w_out, m_final_g, v_norm_g, v_mem_norm_g, v_w_mem_kv, v_attn_w_in, v_attn_w_out, v_conv_w_in, v_conv_w, v_conv_w_out, v_final_g):
    mx, my, mc = lax.axis_index("x"), lax.axis_index("y"), lax.axis_index("c")
    place = jnp.stack([mc, 2 * mx + my]).astype(jnp.int32)

    w_big = [w_mem_kv, attn_w_in, attn_w_out, conv_w_in, conv_w_out]
    m_big = [m_w_mem_kv, m_attn_w_in, m_attn_w_out, m_conv_w_in, m_conv_w_out]
    v_big = [v_w_mem_kv, v_attn_w_in, v_attn_w_out, v_conv_w_in, v_conv_w_out]
    first, rest = (1,), (0, 2, 3, 4)
    wb1 = _cast_weights(place, [w_big[i] for i in first], place, first, "cast_w_in_a")
    a1_send, a1_recv, a1_bufs, a1_token = _gather_start(wb1, place, first, "gather_a1_start")
    wbr = _cast_weights(place, [w_big[i] for i in rest], a1_token, rest, "cast_weights")
    r_send, r_recv, r_bufs, gb_token = _gather_start(wbr, a1_token, rest, "gather_rest_start")
    a2_send, a2_recv, gb_send, gb_recv = r_send, r_recv, r_send, r_recv
    a2_bufs, gb_bufs = r_bufs[:2], r_bufs[2:]
    started, rest = rest, (0, 2)

    xs, tgt = x[0], loss_target[0]
    g0, g1 = norm_g[0:1], norm_g[1:2]
    rc, rs1, rs2 = _rope_tables(positions[0].astype(F32).reshape(S, 1), gb_token)
    a1_bufs = _gather_wait(a1_send, a1_recv, a1_bufs, [rc], first, "gather_a1_wait")
    w_in_a = _gather_forward(a1_bufs, first, "gather_a1_forward")[0].reshape(4, D, SH_A)
    hn0, q, k, v, qm0, z0 = _in_proj_a(xs, g0, w_in_a, rc, rs1, rs2, gb_token)
    a2_bufs = _gather_wait(a2_send, a2_recv, a2_bufs, [q], rest, "gather_a2_wait", started)
    f2_send, f2_recv, a2_bufs, f2_token = _forward_start(a2_bufs, None, q, rest, "forward_a2_start")
    fwd = [_attn_fwd(q, k, v, 0, f2_token)]
    fwd.append(_attn_fwd(q, k, v, 1, fwd[0][0]))
    cw_own = _pad_rows(conv_w[0], CW_ROWS)
    gb_bufs = _gather_wait(gb_send, gb_recv, gb_bufs, [fwd[1][0]], LAYER_B, "gather_b_wait", started)
    fb_send, fb_recv, gb_bufs, fb_token = _forward_start(gb_bufs, cw_own, fwd[1][0], LAYER_B, "forward_b_start")
    fwd.append(_attn_fwd(q, k, v, 2, fb_token))
    os_, ls, lss = [f[0] for f in fwd], [f[1] for f in fwd], [f[2] for f in fwd]
    wkv_f, w_out_a = _forward_wait(f2_send, f2_recv, a2_bufs, [os_[2]], rest, False, "forward_a2_wait")
    w_out_a = w_out_a.reshape(4, BR_A, SH_O)
    memn, kv = _mem_fwd(mem[0], mem_norm_g, wkv_f)
    h1 = _attn_out(os_, ls, qm0, kv[0], z0, xs, w_out_a)

    w_in_b, w_out_b, _, cw_f = _forward_wait(fb_send, fb_recv, gb_bufs, [h1], LAYER_B, True, "forward_b_wait")
    w_in_b = w_in_b.reshape(4, D, SH_B)
    w_out_b = w_out_b.reshape(BR_B, D)
    cw_f = lax.dynamic_update_slice(cw_f, cw_own[None], (2 * mx + my, 0, 0))
    cw8 = cw_f.transpose(1, 0, 2).reshape(CW_ROWS, D)
    hn1, bg, cg, u, qm1, z1 = _in_proj_b(h1, g1, w_in_b)
    dh2, loss_part, dfg = _conv_out_loss(bg, cg, u, cw8, qm1, kv[1], z1, h1, w_out_b, final_g.reshape(1, D), tgt)

    dproj_b, dw_out_b, dcw, dkv1, dw_out_b16 = _conv_bwd(dh2, bg, cg, u, cw8, qm1, kv[1], z1, w_out_b)
    dw_in_b, dw_in_b16 = _w_in_grad(hn1, dproj_b, IN_B, "w_in_b_grad")
    gs_b = [dw_in_b.reshape(4, 1, D, SH_B), dw_out_b.reshape(4, 1, BR_B // 4, D)]
    gb_b = [dw_in_b16.reshape(4, 1, D, SH_B), dw_out_b16.reshape(4, 1, BR_B // 4, D)]
    pb_send, pb_recv, gb_b, pb_land, pb_token = _pair_start(gb_b, LAYER_B, "pair_b_start")
    dh1, dg1 = _in_proj_bwd(dproj_b, w_in_b, h1, g1, dh2, pb_token, IN_B, "in_proj_b_bwd")
    _, r1_b = _pair_wait(pb_send, pb_recv, gb_b, pb_land, [dh1], LAYER_B, "pair_b_wait")
    ps_b = _pair_sums(place, gs_b, r1_b, LAYER_B, "pair_sums_b")
    cb_send, cb_recv, cb_src, cb_land, cb_token = _chip_start(ps_b, LAYER_B, "chip_b_start")

    outs = _attn_out_bwd(dh1, os_, ls, qm0, kv[0], z0, w_out_a, cb_token)
    dos, dds, dqm, dz, dw_out_a, dkv0, dw_out_a16 = outs[0:3], outs[3:6], outs[6], outs[7], outs[8], outs[9], outs[10]
    bwd = [_attn_bwd(q, k, v, dos[g], lss[g], dds[g], g, fwd[g][3]) for g in range(3)]
    dproj_a = _qkv_bwd([b[0] for b in bwd], [b[1] for b in bwd], [b[2] for b in bwd], dqm, dz, rc, rs1, rs2)
    dw_in_a, dw_in_a16 = _w_in_grad(hn0, dproj_a, IN_A, "w_in_a_grad")
    dwkv, dwkv16, dmg = _mem_bwd(mem[0], mem_norm_g, memn, wkv_f, dkv0, dkv1)

    gs_a = [dwkv, dw_in_a.reshape(4, 1, D, SH_A), dw_out_a.reshape(4, 1, BR_A, SH_O)]
    r1_a = _pair_exchange([dwkv16, dw_in_a16.reshape(4, 1, D, SH_A), dw_out_a16.reshape(4, 1, BR_A, SH_O)], LAYER_A,
                          "pair_exchange_a")
    ps_a = _pair_sums(place, gs_a, r1_a, LAYER_A, "pair_sums_a")
    ca_send, ca_recv, ca_src, ca_land, ca_token = _chip_start(ps_a, LAYER_A, "chip_a_start")

    gx, dg0 = _in_proj_bwd(dproj_a, w_in_a, xs, g0, dh1, ca_token, IN_A, "in_proj_a_bwd")
    pack = jnp.concatenate([_pad_rows(jnp.concatenate([dg0, dg1], axis=0), 8), _pad_rows(dmg, 8), _pad_rows(dfg, 8),
                            dcw, _pad_rows(jnp.pad(loss_part, ((0, 0), (0, D - 128))), 8)], axis=0)
    sm_send, sm_recv, pack, sm_land, sm_token = _small_start(pack, ca_token)
    r2_b = _chip_wait(cb_send, cb_recv, cb_src, cb_land, [ca_token], LAYER_B, "chip_b_wait")
    hs_b = _chip_sums(place, gs_b, r1_b, r2_b, LAYER_B, "chip_sums_b")
    g_b = _pair_gather(hs_b, LAYER_B, "pair_gather_b")
    upd_b = _adamw_layer([w_big[w] for w in LAYER_B], g_b, [m_big[w] for w in LAYER_B], [v_big[w] for w in LAYER_B],
                         LAYER_B, "adamw_b")
    r2_a = _chip_wait(ca_send, ca_recv, ca_src, ca_land, [gx, upd_b[0][0], upd_b[1][0], sm_token], LAYER_A,
                      "chip_a_wait")
    hs_a = _chip_sums(place, gs_a, r1_a, r2_a, LAYER_A, "chip_sums_a")
    g_a = _pair_gather(hs_a, LAYER_A, "pair_gather_a")
    upd_a = _adamw_layer([w_big[w] for w in LAYER_A], g_a, [m_big[w] for w in LAYER_A], [v_big[w] for w in LAYER_A],
                         LAYER_A, "adamw_a")
    upd = upd_a + upd_b
    g_big = [u[3] for u in upd]
    pack, sm_land = _small_wait(sm_send, sm_recv, pack, sm_land, [r2_a[0]])
    sw = [norm_g, mem_norm_g, final_g.reshape(1, D), conv_w[0]]
    sm = [m_norm_g, m_mem_norm_g, m_final_g.reshape(1, D), m_conv_w[0]]
    sv = [v_norm_g, v_mem_norm_g, v_final_g.reshape(1, D), v_conv_w[0]]
    loss_row, sg, sd, snm, snv = _small_update(place, pack, sm_land, sw, sm, sv)
    loss = loss_row[0, 0]
    g_norm, g_memnorm, g_final, g_conv = sg

    def order(norm, memnorm, wkv, w_in_a, w_out_a, w_in_b, conv, w_out_b, final):
        return (norm, memnorm, wkv, w_in_a, w_out_a, w_in_b, conv.reshape(1, 3, SH_O), w_out_b, final.reshape(D))

    grads = order(g_norm, g_memnorm, g_big[0], g_big[1], g_big[2], g_big[3], g_conv, g_big[4], g_final)
    deltas = order(sd[0], sd[1], upd[0][0], upd[1][0], upd[2][0], upd[3][0], sd[3], upd[4][0], sd[2])
    new_m = order(snm[0], snm[1], upd[0][1], upd[1][1], upd[2][1], upd[3][1], snm[3], upd[4][1], snm[2])
    new_v = order(snv[0], snv[1], upd[0][2], upd[1][2], upd[2][2], upd[3][2], snv[3], upd[4][2], snv[2])
    return (loss, gx[None], *grads, *deltas, *new_m, *new_v)
```

```python
import functools

import numpy as np
import jax
import jax.numpy as jnp
from jax import lax
from jax.experimental import pallas as pl
from jax.experimental.pallas import tpu as pltpu

F32 = jnp.float32
BF16 = jnp.bfloat16

S = 2048
D = 1024
TM = 256
NT = S // TM
MX = 512
NX = S // MX
HD = 64
GW = 512
NQ = 3 * GW
MW = 256
NM = 256
IN_A = 3 * NQ + MW + GW + MW
IN_B = 3 * D + MW + D + MW
BR_A = GW + MW
BR_B = D + MW
SH_A = IN_A // 4
SH_B = IN_B // 4
SH_O = D // 4
QBLK = 128
DILATIONS = (1, 4, 16)
EPS = 1e-6
SCALE = HD ** -0.5
NEG = -1e30
ROPE_THETA = 500000.0

ADAM_LR = 0.001
ADAM_B1 = 0.9
ADAM_B2 = 0.999
ADAM_EPS = 1e-08
ADAM_WD = 0.01
ADAM_STEP = 10

VMEM_LIMIT_BYTES = 60 * 1024 * 1024


def _params(sem=None):
    if sem is None:
        return pltpu.CompilerParams(vmem_limit_bytes=VMEM_LIMIT_BYTES)
    return pltpu.CompilerParams(dimension_semantics=sem, vmem_limit_bytes=VMEM_LIMIT_BYTES)


def _full(shape):
    nd = len(shape)
    return pl.BlockSpec(shape, lambda *_: (0,) * nd)


def _rows(width, tm=TM):
    return pl.BlockSpec((tm, width), lambda i: (i, 0))


def _sds(shape, dtype):
    return jax.ShapeDtypeStruct(shape, dtype)


def _silu_parts(z):
    sig = 0.5 * jnp.tanh(0.5 * z) + 0.5
    return z * sig, sig * (1.0 + z * (1.0 - sig))


def _dot(a, b):
    return jnp.dot(a, b, preferred_element_type=F32)


def _dot_nt(a, b):
    return lax.dot_general(a, b, (((1,), (1,)), ((), ())), preferred_element_type=F32)


def _dot_tn(a, b):
    return lax.dot_general(a, b, (((0,), (0,)), ((), ())), preferred_element_type=F32)


def _rope_fwd(t, c, s1, s2):
    return t * c + pltpu.roll(t, 120, 1) * s1 + pltpu.roll(t, 8, 1) * s2


def _rope_bwd(g, c, s1, s2):
    return g * c + pltpu.roll(g * s1, 8, 1) + pltpu.roll(g * s2, 120, 1)


MEM_HEADS = MW // HD


def _stack_heads(x):
    head = lax.broadcasted_iota(jnp.int32, x.shape, 1) // HD
    return jnp.concatenate([jnp.where(head == h, x, 0.0) for h in range(MEM_HEADS)], axis=0).astype(BF16)


def _unstack_heads(x4):
    tm = x4.shape[0] // MEM_HEADS
    head = lax.broadcasted_iota(jnp.int32, (tm, MW), 1) // HD
    out = x4[:tm]
    for h in range(1, MEM_HEADS):
        out = jnp.where(head == h, x4[h * tm:(h + 1) * tm], out)
    return out


def _mem_attn(qm, kv):
    q4 = _stack_heads(qm.astype(F32))
    s = _dot_nt(q4, kv[:, :MW]) * SCALE
    e = jnp.exp(s - jnp.max(s, axis=-1, keepdims=True))
    p = e * (1.0 / jnp.sum(e, axis=-1, keepdims=True))
    return p, _unstack_heads(_dot(p.astype(BF16), kv[:, MW:])), q4


def _mem_attn_bwd(dmo, p, mo, q4, kv, dkv_ref):
    tm = dmo.shape[0]
    head = lax.broadcasted_iota(jnp.int32, dmo.shape, 1) // HD
    prod = dmo * mo
    delta = jnp.concatenate([jnp.sum(jnp.where(head == h, prod, 0.0), axis=-1, keepdims=True)
                             for h in range(MEM_HEADS)], axis=0)
    d4 = _stack_heads(dmo)
    ds = (p * (_dot_nt(d4, kv[:, MW:]) - delta) * SCALE).astype(BF16)
    dkv_ref[:, :MW] += _dot_tn(ds, q4)
    dkv_ref[:, MW:] += _dot_tn(p.astype(BF16), d4)
    return _unstack_heads(_dot(ds, kv[:, :MW]))


def _merge(o_refs, l_refs):
    ls = [r[...] for r in l_refs]
    m = jnp.maximum(jnp.maximum(ls[0], ls[1]), ls[2])
    es = [jnp.exp(l - m) for l in ls]
    inv = 1.0 / (es[0] + es[1] + es[2])
    ws = [e * inv for e in es]
    os_ = [r[...] for r in o_refs]
    mix = ws[0] * os_[0] + ws[1] * os_[1] + ws[2] * os_[2]
    return ws, mix


def _conv_taps(cg, u, cgp, up, first):
    a = cg * u
    ap = jnp.where(first, 0.0, cgp * up)
    row = lax.broadcasted_iota(jnp.int32, a.shape, 0)
    a1 = jnp.where(row == 0, ap[7:8, :], pltpu.roll(a, 1, 0))
    a2 = jnp.where(row == 0, ap[6:7, :], jnp.where(row == 1, ap[7:8, :], pltpu.roll(a, 2, 0)))
    return a, a1, a2


def _rope_tables(posf, after):
    half = 8
    invf = np.float32(ROPE_THETA) ** (-np.arange(half, dtype=np.float32) * np.float32(2.0 / 16))
    lane = np.arange(128)
    table = np.where((lane % HD) < 16, invf[lane % half], 0.0).astype(np.float32)[None, :]

    def body(pos_ref, invf_ref, c_ref, s1_ref, s2_ref):
        ang = pos_ref[...] * invf_ref[...]
        jm = lax.broadcasted_iota(jnp.int32, ang.shape, 1) & (HD - 1)
        cs = jnp.cos(ang)
        sn = jnp.sin(ang)
        c_ref[...] = jnp.where(jm < 16, cs, 1.0)
        s1_ref[...] = jnp.where(jm < 8, -sn, 0.0)
        s2_ref[...] = jnp.where((jm >= 8) & (jm < 16), sn, 0.0)

    out = _sds((S, 128), F32)
    return pl.pallas_call(
        functools.partial(_skip_arg, body, 2), name="rope_tables", grid=(NT,),
        in_specs=[_rows(1), _full((1, 128)), pl.BlockSpec(memory_space=pl.ANY)],
        out_specs=[_rows(128)] * 3, out_shape=[out] * 3,
        compiler_params=_params(("parallel",)),
    )(posf, jnp.asarray(table), after)


def _in_proj_a(x, g0, w_in, c, s1, s2, after):
    def body(x_ref, g_ref, w_ref, c_ref, s1_ref, s2_ref, hn_ref, q_ref, k_ref, v_ref, qm_ref, z_ref, proj):
        xf = x_ref[...]
        hn = xf * lax.rsqrt(jnp.mean(xf * xf, axis=-1, keepdims=True) + EPS) * g_ref[...]
        hb = hn.astype(BF16)
        hn_ref[...] = hb
        for s in range(4):
            proj[:, s * SH_A:(s + 1) * SH_A] = _dot(hb, w_ref[s])
        cc, a1, a2 = c_ref[...], s1_ref[...], s2_ref[...]
        for j in range(NQ // 128):
            q_ref[:, j * 128:(j + 1) * 128] = (
                _rope_fwd(proj[:, j * 128:(j + 1) * 128], cc, a1, a2) * SCALE).astype(BF16)
            k_ref[:, j * 128:(j + 1) * 128] = _rope_fwd(
                proj[:, NQ + j * 128:NQ + (j + 1) * 128], cc, a1, a2).astype(BF16)
        v_ref[...] = proj[:, 2 * NQ:3 * NQ].astype(BF16)
        qm_ref[...] = proj[:, 3 * NQ:3 * NQ + MW].astype(BF16)
        z_ref[...] = proj[:, 3 * NQ + MW:]

    return pl.pallas_call(
        functools.partial(_skip_arg, body, 6), name="in_proj_a", grid=(NT,),
        in_specs=[_rows(D), _full((1, D)), _full((4, D, SH_A)), _rows(128), _rows(128), _rows(128),
                  pl.BlockSpec(memory_space=pl.ANY)],
        out_specs=[_rows(D), _rows(NQ), _rows(NQ), _rows(NQ), _rows(MW), _rows(BR_A)],
        out_shape=[_sds((S, D), BF16), _sds((S, NQ), BF16), _sds((S, NQ), BF16), _sds((S, NQ), BF16),
                   _sds((S, MW), BF16), _sds((S, BR_A), F32)],
        scratch_shapes=[pltpu.VMEM((TM, IN_A), F32)],
        compiler_params=_params(("parallel",)),
    )(x, g0, w_in, c, s1, s2, after)


def _mem_fwd(mem, mg, wkv):
    def body(mem_ref, mg_ref, w_ref, memn_ref, kv_ref):
        mf = mem_ref[...]
        n = mf * lax.rsqrt(jnp.mean(mf * mf, axis=-1, keepdims=True) + EPS)
        for i in range(2):
            mn = (n * mg_ref[i:i + 1, :]).astype(BF16)
            memn_ref[i] = mn
            acc = _dot(mn[:, 0:NM], w_ref[0, i])
            for s in range(1, 4):
                acc += _dot(mn[:, s * NM:(s + 1) * NM], w_ref[s, i])
            kv_ref[i] = acc.astype(BF16)

    return pl.pallas_call(
        body, name="mem_fwd", grid=(1,),
        in_specs=[_full((NM, D)), _full((2, D)), _full((4, 2, NM, 2 * MW))],
        out_specs=[_full((2, NM, D)), _full((2, NM, 2 * MW))],
        out_shape=[_sds((2, NM, D), BF16), _sds((2, NM, 2 * MW), BF16)],
        compiler_params=_params(("arbitrary",)),
    )(mem, mg, wkv)


def _band_mask(j):
    qi = lax.broadcasted_iota(jnp.int32, (QBLK, 2 * QBLK), 0)
    kj = lax.broadcasted_iota(jnp.int32, (QBLK, 2 * QBLK), 1)
    dist = qi + QBLK - kj
    return (dist >= 0) & (dist <= QBLK) & ((kj >= QBLK) | (j > 0))


LANES = 128
NCHUNK = GW // LANES
FWD_UNROLL = 16
BWD_UNROLL = 16
CONV_CHUNK = 256


def _perm_matrix(d):
    n = TM // d
    p = np.zeros((TM, TM), np.float32)
    for r in range(d):
        for i in range(n):
            p[r * n + i, i * d + r] = 1.0
    return p


def _split_dot(p, x):
    hi = x.astype(BF16)
    lo = (x - hi.astype(F32)).astype(BF16)
    both = _dot(p, jnp.concatenate([hi, lo], axis=1))
    return both[:, :LANES] + both[:, LANES:]


def _pair_dot(p, a, b):
    both = _dot(p, jnp.concatenate([a, b], axis=1))
    return both[:, :LANES], both[:, LANES:]


def _tile_to_streams(y, dst, t, d):
    n, ln = TM // d, S // d
    for r in range(d):
        dst[r * ln + t * n:r * ln + (t + 1) * n, :] = y[r * n:(r + 1) * n].astype(dst.dtype)


def _tile_from_streams(src, t, d):
    n, ln = TM // d, S // d
    return jnp.concatenate([src[r * ln + t * n:r * ln + (t + 1) * n, :] for r in range(d)], axis=0)


def _head_masks():
    first = lax.broadcasted_iota(jnp.int32, (TM, LANES), 1) < HD
    return first, jnp.logical_not(first)


def _attn_fwd(q, k, v, g, after):
    d = DILATIONS[g]
    nb = S // d // QBLK
    perm = _perm_matrix(d)

    def body(q_ref, k_ref, v_ref, p_ref, pt_ref, o_ref, l_ref, ls_ref, q0, q1, ks, vs, os_):
        first, second = _head_masks()
        pm = p_ref[...]
        for t in range(NT):
            rows = slice(t * TM, (t + 1) * TM)
            if d == 1:
                qt = q_ref[rows, :].astype(F32)
            else:
                qt, kt = _pair_dot(pm, q_ref[rows, :], k_ref[rows, :])
                _tile_to_streams(kt, ks, t, d)
                if t % 2 == 0:
                    va, vb = _pair_dot(pm, v_ref[rows, :], v_ref[(t + 1) * TM:(t + 2) * TM, :])
                    _tile_to_streams(va, vs, t, d)
                    _tile_to_streams(vb, vs, t + 1, d)
            _tile_to_streams(jnp.where(first, qt, 0.0), q0, t, d)
            _tile_to_streams(jnp.where(second, qt, 0.0), q1, t, d)
        kref, vref = (k_ref, v_ref) if d == 1 else (ks, vs)
        oref, lref = (o_ref, l_ref) if d == 1 else (os_, ls_ref)

        def blk(b, carry):
            r0 = pl.multiple_of(b * QBLK, QBLK)
            p0 = pl.multiple_of(jnp.maximum(b - 1, 0) * QBLK, QBLK)
            kk = jnp.concatenate([kref[pl.ds(p0, QBLK), :], kref[pl.ds(r0, QBLK), :]], axis=0)
            vv = jnp.concatenate([vref[pl.ds(p0, QBLK), :], vref[pl.ds(r0, QBLK), :]], axis=0)
            valid = _band_mask(b & (nb - 1))
            acc, lse = [], []
            for qh in (q0, q1):
                s = jnp.where(valid, _dot_nt(qh[pl.ds(r0, QBLK), :], kk), NEG)
                m = jnp.max(s, axis=-1, keepdims=True)
                e = jnp.exp(s - m)
                l = jnp.sum(e, axis=-1, keepdims=True)
                acc.append(_dot(e.astype(BF16), vv) * (1.0 / l))
                lse.append(m + jnp.log(l))
            f = first[:QBLK]
            oref[pl.ds(r0, QBLK), :] = jnp.where(f, acc[0], acc[1])
            lref[pl.ds(r0, QBLK), :] = jnp.where(f, lse[0], lse[1])
            return carry

        lax.fori_loop(0, S // QBLK, blk, 0, unroll=FWD_UNROLL)
        if d > 1:
            ptm = pt_ref[...]
            for t in range(NT):
                rows = slice(t * TM, (t + 1) * TM)
                o_ref[rows, :] = _split_dot(ptm, _tile_from_streams(os_, t, d))
                l_ref[rows, :] = _split_dot(ptm, _tile_from_streams(ls_ref, t, d))

    qkv_spec = pl.BlockSpec((S, LANES), lambda c: (0, g * NCHUNK + c))
    out_spec = pl.BlockSpec((S, LANES), lambda c: (0, c))
    n_out = 2 if d == 1 else 3
    inner = body if d > 1 else functools.partial(_drop_arg, body, 7)
    outs = pl.pallas_call(
        functools.partial(_skip_arg, inner, 5), name=f"attn_fwd_g{g}", grid=(NCHUNK,),
        in_specs=[qkv_spec] * 3 + [_full((TM, TM))] * 2 + [pl.BlockSpec(memory_space=pl.ANY)],
        out_specs=[out_spec] * n_out, out_shape=[_sds((S, GW), F32)] * n_out,
        scratch_shapes=[pltpu.VMEM((S, LANES), BF16)] * 4 + [pltpu.VMEM((S, LANES), F32)],
        compiler_params=_params(("parallel",)),
    )(q, k, v, jnp.asarray(perm, BF16), jnp.asarray(perm.T, BF16), after)
    return (outs[0], outs[1], outs[1]) if d == 1 else tuple(outs)


def _drop_arg(body, pos, *refs):
    return body(*refs[:pos], None, *refs[pos:])


def _attn_out(os_, ls, qm, kv0, z, x, w_out):
    def body(o0, o1, o2, l0, l1, l2, qm_ref, kv_ref, z_ref, x_ref, w_ref, h_ref, ybuf):
        _, mix = _merge((o0, o1, o2), (l0, l1, l2))
        sz, _ = _silu_parts(z_ref[...])
        ybuf[:, :GW] = (mix * sz[:, :GW]).astype(BF16)
        _, mo, _ = _mem_attn(qm_ref[...], kv_ref[...])
        ybuf[:, GW:] = (mo * sz[:, GW:]).astype(BF16)
        yb = ybuf[...]
        for s in range(4):
            cs = slice(s * SH_O, (s + 1) * SH_O)
            h_ref[:, cs] = x_ref[:, cs] + _dot(yb, w_ref[s])

    return pl.pallas_call(
        body, name="attn_out", grid=(NX,),
        in_specs=[_rows(GW, MX)] * 6 + [_rows(MW, MX), _full((NM, 2 * MW)), _rows(BR_A, MX), _rows(D, MX),
                                        _full((4, BR_A, SH_O))],
        out_specs=_rows(D, MX), out_shape=_sds((S, D), F32),
        scratch_shapes=[pltpu.VMEM((MX, BR_A), BF16)],
        compiler_params=_params(("parallel",)),
    )(*os_, *ls, qm, kv0, z, x, w_out)


def _in_proj_b(h1, g1, w_in):
    def body(x_ref, g_ref, w_ref, hn_ref, bg_ref, cg_ref, u_ref, qm_ref, z_ref, proj):
        xf = x_ref[...]
        hn = xf * lax.rsqrt(jnp.mean(xf * xf, axis=-1, keepdims=True) + EPS) * g_ref[...]
        hb = hn.astype(BF16)
        hn_ref[...] = hb
        for s in range(4):
            proj[:, s * SH_B:(s + 1) * SH_B] = _dot(hb, w_ref[s])
        bg_ref[...] = proj[:, :D]
        cg_ref[...] = proj[:, D:2 * D]
        u_ref[...] = proj[:, 2 * D:3 * D]
        qm_ref[...] = proj[:, 3 * D:3 * D + MW].astype(BF16)
        z_ref[...] = proj[:, 3 * D + MW:]

    return pl.pallas_call(
        body, name="in_proj_b", grid=(NT,),
        in_specs=[_rows(D), _full((1, D)), _full((4, D, SH_B))],
        out_specs=[_rows(D), _rows(D), _rows(D), _rows(D), _rows(MW), _rows(BR_B)],
        out_shape=[_sds((S, D), BF16), _sds((S, D), F32), _sds((S, D), F32), _sds((S, D), F32),
                   _sds((S, MW), BF16), _sds((S, BR_B), F32)],
        scratch_shapes=[pltpu.VMEM((TM, IN_B), F32)],
        compiler_params=_params(("parallel",)),
    )(h1, g1, w_in)


def _prev8(width):
    return pl.BlockSpec((8, width), lambda i: (jnp.maximum(i * (MX // 8) - 1, 0), 0))


def _conv_out_loss(bg, cg, u, cw, qm, kv1, z, h1, w_out, fg, tgt):
    def body(bg_ref, cg_ref, u_ref, cgp_ref, up_ref, cw_ref, qm_ref, kv_ref, z_ref, h_ref, w_ref, fg_ref, t_ref,
             dh_ref, loss_ref, dfg_ref, ybuf):
        i = pl.program_id(0)
        a, a1, a2 = _conv_taps(cg_ref[...], u_ref[...], cgp_ref[...], up_ref[...], i == 0)
        conv = cw_ref[0:1, :] * a2 + cw_ref[1:2, :] * a1 + cw_ref[2:3, :] * a
        sz, _ = _silu_parts(z_ref[...])
        ybuf[:, :D] = (bg_ref[...] * conv * sz[:, :D]).astype(BF16)
        _, mo, _ = _mem_attn(qm_ref[...], kv_ref[...])
        ybuf[:, D:] = (mo * sz[:, D:]).astype(BF16)
        h2 = h_ref[...] + _dot(ybuf[...], w_ref[...])
        rstd = lax.rsqrt(jnp.mean(h2 * h2, axis=-1, keepdims=True) + EPS)
        n = h2 * rstd
        fgv = fg_ref[...]
        err = n * fgv - t_ref[...]
        dout = err * (1.0 / D)
        dn = dout * fgv
        dh_ref[...] = rstd * (dn - n * jnp.mean(dn * n, axis=-1, keepdims=True))

        @pl.when(i == 0)
        def _():
            loss_ref[...] = jnp.zeros_like(loss_ref)
            dfg_ref[...] = jnp.zeros_like(dfg_ref)

        loss_ref[...] += jnp.sum(err * err) * (0.5 / D)
        dfg_ref[...] += jnp.sum(dout * n, axis=0, keepdims=True)

    return pl.pallas_call(
        body, name="conv_out_loss", grid=(NX,),
        in_specs=[_rows(D, MX), _rows(D, MX), _rows(D, MX), _prev8(D), _prev8(D), _full((8, D)), _rows(MW, MX),
                  _full((NM, 2 * MW)), _rows(BR_B, MX), _rows(D, MX), _full((BR_B, D)), _full((1, D)), _rows(D, MX)],
        out_specs=[_rows(D, MX), _full((1, 128)), _full((1, D))],
        out_shape=[_sds((S, D), F32), _sds((1, 128), F32), _sds((1, D), F32)],
        scratch_shapes=[pltpu.VMEM((MX, BR_B), BF16)],
        compiler_params=_params(("arbitrary",)),
    )(bg, cg, u, cg, u, cw, qm, kv1, z, h1, w_out, fg, tgt)


def _conv_bwd(dh2, bg, cg, u, cw, qm, kv1, z, w_out):
    rev = lambda i: (NX - 1 - i, 0)
    rows = lambda w: pl.BlockSpec((MX, w), rev)
    prev8 = pl.BlockSpec((8, D), lambda i: (jnp.maximum((NX - 1 - i) * (MX // 8) - 1, 0), 0))

    def body(dh_ref, bg_ref, cg_ref, u_ref, cgp_ref, up_ref, cw_ref, qm_ref, kv_ref, z_ref, w_ref,
             dproj_ref, dw_ref, dcw_ref, dkv_ref, dwb_ref, ybuf, carry):
        i = pl.program_id(0)

        @pl.when(i == 0)
        def _():
            dw_ref[...] = jnp.zeros_like(dw_ref)
            dcw_ref[...] = jnp.zeros_like(dcw_ref)
            dkv_ref[...] = jnp.zeros_like(dkv_ref)
            carry[...] = jnp.zeros_like(carry)

        dhb = dh_ref[...].astype(BF16)
        dy = _dot_nt(dhb, w_ref[...])
        kvv = kv_ref[...]
        p, mo, q4 = _mem_attn(qm_ref[...], kvv)
        szm, dszm = _silu_parts(z_ref[:, D:])
        ybuf[:, D:] = (mo * szm).astype(BF16)
        dym = dy[:, D:]
        dproj_ref[:, 3 * D + MW + D:] = (dym * mo * dszm).astype(BF16)
        first_tile = i == NX - 1
        for c in range(D // CONV_CHUNK):
            cs = slice(c * CONV_CHUNK, (c + 1) * CONV_CHUNK)
            bgv, cgv, uv = bg_ref[:, cs], cg_ref[:, cs], u_ref[:, cs]
            a, a1, a2 = _conv_taps(cgv, uv, cgp_ref[:, cs], up_ref[:, cs], first_tile)
            w0, w1, w2 = cw_ref[0:1, cs], cw_ref[1:2, cs], cw_ref[2:3, cs]
            conv = w0 * a2 + w1 * a1 + w2 * a
            mix = bgv * conv
            sz, dsz = _silu_parts(z_ref[:, cs])
            ybuf[:, cs] = (mix * sz).astype(BF16)
            dyc = dy[:, cs]
            dproj_ref[:, 3 * D + MW + c * CONV_CHUNK:3 * D + MW + (c + 1) * CONV_CHUNK] = (
                dyc * mix * dsz).astype(BF16)
            dmix = dyc * sz
            dproj_ref[:, cs] = (dmix * conv).astype(BF16)
            dc = dmix * bgv
            nxt = carry[:, cs]
            row = lax.broadcasted_iota(jnp.int32, dc.shape, 0)
            dc1 = jnp.where(row == MX - 1, nxt[0:1, :], pltpu.roll(dc, MX - 1, 0))
            dc2 = jnp.where(row == MX - 2, nxt[0:1, :],
                            jnp.where(row == MX - 1, nxt[1:2, :], pltpu.roll(dc, MX - 2, 0)))
            carry[:, cs] = dc[0:8, :]
            da = w2 * dc + w1 * dc1 + w0 * dc2
            dproj_ref[:, D + c * CONV_CHUNK:D + (c + 1) * CONV_CHUNK] = (da * uv).astype(BF16)
            dproj_ref[:, 2 * D + c * CONV_CHUNK:2 * D + (c + 1) * CONV_CHUNK] = (da * cgv).astype(BF16)
            dcw_ref[0:1, cs] += jnp.sum(dc * a2, axis=0, keepdims=True)
            dcw_ref[1:2, cs] += jnp.sum(dc * a1, axis=0, keepdims=True)
            dcw_ref[2:3, cs] += jnp.sum(dc * a, axis=0, keepdims=True)
        dw_ref[...] += _dot_tn(ybuf[...], dhb)
        dproj_ref[:, 3 * D:3 * D + MW] = _mem_attn_bwd(dym * szm, p, mo, q4, kvv, dkv_ref).astype(BF16)

        @pl.when(i == NX - 1)
        def _():
            dwb_ref[...] = dw_ref[...].astype(BF16)

    return pl.pallas_call(
        body, name="conv_bwd", grid=(NX,),
        in_specs=[rows(D), rows(D), rows(D), rows(D), prev8, prev8, _full((8, D)), rows(MW),
                  _full((NM, 2 * MW)), rows(BR_B), _full((BR_B, D))],
        out_specs=[rows(IN_B), _full((BR_B, D)), _full((8, D)), _full((NM, 2 * MW)), _full((BR_B, D))],
        out_shape=[_sds((S, IN_B), BF16), _sds((BR_B, D), F32), _sds((8, D), F32), _sds((NM, 2 * MW), F32),
                   _sds((BR_B, D), BF16)],
        scratch_shapes=[pltpu.VMEM((MX, BR_B), BF16), pltpu.VMEM((8, D), F32)],
        compiler_params=_params(("arbitrary",)),
    )(dh2, bg, cg, u, cg, u, cw, qm, kv1, z, w_out)


def _in_proj_bwd(dproj, w_in, xin, g, dres, after, width, name):
    sh = width // 4

    def body(dp_ref, w_ref, x_ref, g_ref, dr_ref, dx_ref, dg_ref):
        i = pl.program_id(0)
        dhn = _dot_nt(dp_ref[:, 0:sh], w_ref[0])
        for s in range(1, 4):
            dhn += _dot_nt(dp_ref[:, s * sh:(s + 1) * sh], w_ref[s])
        xf = x_ref[...]
        rstd = lax.rsqrt(jnp.mean(xf * xf, axis=-1, keepdims=True) + EPS)
        n = xf * rstd
        dn = dhn * g_ref[...]
        dx_ref[...] = dr_ref[...] + rstd * (dn - n * jnp.mean(dn * n, axis=-1, keepdims=True))

        @pl.when(i == 0)
        def _():
            dg_ref[...] = jnp.zeros_like(dg_ref)

        dg_ref[...] += jnp.sum(dhn * n, axis=0, keepdims=True)

    return pl.pallas_call(
        functools.partial(_skip_arg, body, 5), name=name, grid=(NT,),
        in_specs=[_rows(width), _full((4, D, sh)), _rows(D), _full((1, D)), _rows(D), pl.BlockSpec(memory_space=pl.ANY)],
        out_specs=[_rows(D), _full((1, D))],
        out_shape=[_sds((S, D), F32), _sds((1, D), F32)],
        compiler_params=_params(("arbitrary",)),
    )(dproj, w_in, xin, g, dres, after)


def _w_in_grad(hn, dproj, width, name):
    sh = width // 4

    def body(hn_ref, dp_ref, dw_ref, dwb_ref):
        dw = _dot_tn(hn_ref[...], dp_ref[...])
        dw_ref[0] = dw
        dwb_ref[0] = dw.astype(BF16)

    spec = pl.BlockSpec((1, D, sh), lambda s: (s, 0, 0))
    return pl.pallas_call(
        body, name=name, grid=(4,),
        in_specs=[_full((S, D)), pl.BlockSpec((S, sh), lambda s: (0, s))],
        out_specs=[spec, spec], out_shape=[_sds((4, D, sh), F32), _sds((4, D, sh), BF16)],
        compiler_params=_params(("parallel",)),
    )(hn, dproj)


def _attn_out_bwd(dh1, os_, ls, qm, kv0, z, w_out, after):
    ones_bd = np.kron(np.eye(GW // HD, dtype=np.float32), np.ones((HD, HD), np.float32))

    def body(dh_ref, o0, o1, o2, l0, l1, l2, qm_ref, kv_ref, z_ref, w_ref, bd_ref,
             do0, do1, do2, dd0, dd1, dd2, dqm_ref, dz_ref, dw_ref, dkv_ref, dwb_ref, ybuf):
        i = pl.program_id(0)

        @pl.when(i == 0)
        def _():
            dw_ref[...] = jnp.zeros_like(dw_ref)
            dkv_ref[...] = jnp.zeros_like(dkv_ref)

        ws, mix = _merge((o0, o1, o2), (l0, l1, l2))
        sz, dsz = _silu_parts(z_ref[...])
        kvv = kv_ref[...]
        p, mo, q4 = _mem_attn(qm_ref[...], kvv)
        ybuf[:, :GW] = (mix * sz[:, :GW]).astype(BF16)
        ybuf[:, GW:] = (mo * sz[:, GW:]).astype(BF16)
        yb = ybuf[...]
        dh = dh_ref[...]
        dy = None
        for s in range(4):
            dhb = dh[:, s * SH_O:(s + 1) * SH_O].astype(BF16)
            dw_ref[s] += _dot_tn(yb, dhb)
            part = _dot_nt(dhb, w_ref[s])
            dy = part if dy is None else dy + part
        dcat = dy * sz
        dz_ref[:, :GW] = (dy[:, :GW] * mix * dsz[:, :GW]).astype(BF16)
        dz_ref[:, GW:] = (dy[:, GW:] * mo * dsz[:, GW:]).astype(BF16)
        dmix = dcat[:, :GW]
        prod = dmix * mix
        hi = prod.astype(BF16)
        lo = (prod - hi.astype(F32)).astype(BF16)
        bd = bd_ref[...]
        tot = _dot(hi, bd) + _dot(lo, bd)
        for w, do_ref, dd_ref in zip(ws, (do0, do1, do2), (dd0, dd1, dd2)):
            do_ref[...] = (w * dmix).astype(BF16)
            dd_ref[...] = w * tot

        dqm_ref[...] = _mem_attn_bwd(dcat[:, GW:], p, mo, q4, kvv, dkv_ref).astype(BF16)

        @pl.when(i == NX - 1)
        def _():
            dwb_ref[...] = dw_ref[...].astype(BF16)

    return pl.pallas_call(
        functools.partial(_skip_arg, body, 12), name="attn_out_bwd", grid=(NX,),
        in_specs=[_rows(D, MX)] + [_rows(GW, MX)] * 6 + [_rows(MW, MX), _full((NM, 2 * MW)), _rows(BR_A, MX),
                                                           _full((4, BR_A, SH_O)), _full((GW, GW)),
                                                           pl.BlockSpec(memory_space=pl.ANY)],
        out_specs=[_rows(GW, MX)] * 6 + [_rows(MW, MX), _rows(BR_A, MX), _full((4, BR_A, SH_O)),
                                         _full((NM, 2 * MW)), _full((4, BR_A, SH_O))],
        out_shape=[_sds((S, GW), BF16)] * 3 + [_sds((S, GW), F32)] * 3 + [
            _sds((S, MW), BF16), _sds((S, BR_A), BF16), _sds((4, BR_A, SH_O), F32), _sds((NM, 2 * MW), F32),
            _sds((4, BR_A, SH_O), BF16)],
        scratch_shapes=[pltpu.VMEM((MX, BR_A), BF16)],
        compiler_params=_params(("arbitrary",)),
    )(dh1, *os_, *ls, qm, kv0, z, w_out, jnp.asarray(ones_bd, dtype=BF16), after)


def _attn_bwd(q, k, v, do, lse_s, dd, g):
    d = DILATIONS[g]
    nb = S // d // QBLK
    perm = _perm_matrix(d)

    def body(q_ref, k_ref, v_ref, do_ref, l_ref, dd_ref, p_ref, pt_ref, dq_ref, dk_ref, dv_ref,
             q0, q1, g0, g1, ks, vs, dds, dqs, dks, dvs):
        first, second = _head_masks()
        pm = p_ref[...]
        for t in range(NT):
            rows = slice(t * TM, (t + 1) * TM)
            if d == 1:
                qt = q_ref[rows, :].astype(F32)
                gt = do_ref[rows, :].astype(F32)
            else:
                qt, gt = _pair_dot(pm, q_ref[rows, :], do_ref[rows, :])
                kt, vt = _pair_dot(pm, k_ref[rows, :], v_ref[rows, :])
                _tile_to_streams(kt, ks, t, d)
                _tile_to_streams(vt, vs, t, d)
                _tile_to_streams(_split_dot(pm, dd_ref[rows, :]), dds, t, d)
            _tile_to_streams(jnp.where(first, qt, 0.0), q0, t, d)
            _tile_to_streams(jnp.where(second, qt, 0.0), q1, t, d)
            _tile_to_streams(jnp.where(first, gt, 0.0), g0, t, d)
            _tile_to_streams(jnp.where(second, gt, 0.0), g1, t, d)
        kref, vref, ddref = (k_ref, v_ref, dd_ref) if d == 1 else (ks, vs, dds)
        dqref, dkref, dvref = dqs, dks, dvs
        dkref[...] = jnp.zeros_like(dkref)
        dvref[...] = jnp.zeros_like(dvref)

        def blk(b, carry):
            r0 = pl.multiple_of(b * QBLK, QBLK)
            p0 = pl.multiple_of(jnp.maximum(b - 1, 0) * QBLK, QBLK)
            kk = jnp.concatenate([kref[pl.ds(p0, QBLK), :], kref[pl.ds(r0, QBLK), :]], axis=0)
            vv = jnp.concatenate([vref[pl.ds(p0, QBLK), :], vref[pl.ds(r0, QBLK), :]], axis=0)
            lb = l_ref[pl.ds(r0, QBLK), :]
            ddb = ddref[pl.ds(r0, QBLK), :]
            lcol = jnp.concatenate([lb[:, 0:1], lb[:, HD:HD + 1]], axis=0)
            dcol = jnp.concatenate([ddb[:, 0:1], ddb[:, HD:HD + 1]], axis=0)
            valid = _band_mask(b & (nb - 1))
            valid2 = jnp.concatenate([valid, valid], axis=0)
            qq = jnp.concatenate([q0[pl.ds(r0, QBLK), :], q1[pl.ds(r0, QBLK), :]], axis=0)
            gg = jnp.concatenate([g0[pl.ds(r0, QBLK), :], g1[pl.ds(r0, QBLK), :]], axis=0)
            p = jnp.where(valid2, jnp.exp(_dot_nt(qq, kk) - lcol), 0.0)
            ds = (p * (_dot_nt(gg, vv) - dcol)).astype(BF16)
            dq2 = _dot(ds, kk)
            dqref[pl.ds(r0, QBLK), :] = jnp.where(first[:QBLK], dq2[:QBLK], dq2[QBLK:])
            dkk = _dot_tn(ds, qq)
            dvv = _dot_tn(p.astype(BF16), gg)
            dkref[pl.ds(p0, QBLK), :] += dkk[:QBLK]
            dkref[pl.ds(r0, QBLK), :] += dkk[QBLK:]
            dvref[pl.ds(p0, QBLK), :] += dvv[:QBLK]
            dvref[pl.ds(r0, QBLK), :] += dvv[QBLK:]
            return carry

        lax.fori_loop(0, S // QBLK, blk, 0, unroll=BWD_UNROLL)

        ptm = pt_ref[...] if d > 1 else None
        for t in range(NT):
            rows = slice(t * TM, (t + 1) * TM)
            if d == 1:
                dq_ref[rows, :] = dqs[rows, :].astype(BF16)
                dk_ref[rows, :] = dks[rows, :].astype(BF16)
                dv_ref[rows, :] = dvs[rows, :].astype(BF16)
            else:
                tq, tk = _pair_dot(ptm, _tile_from_streams(dqs, t, d).astype(BF16),
                                   _tile_from_streams(dks, t, d).astype(BF16))
                dq_ref[rows, :] = tq.astype(BF16)
                dk_ref[rows, :] = tk.astype(BF16)
                if t % 2 == 0:
                    ta, tb = _pair_dot(ptm, _tile_from_streams(dvs, t, d).astype(BF16),
                                       _tile_from_streams(dvs, t + 1, d).astype(BF16))
                    dv_ref[rows, :] = ta.astype(BF16)
                    dv_ref[(t + 1) * TM:(t + 2) * TM, :] = tb.astype(BF16)

    qkv_spec = pl.BlockSpec((S, LANES), lambda c: (0, g * NCHUNK + c))
    one_spec = pl.BlockSpec((S, LANES), lambda c: (0, c))
    return pl.pallas_call(
        body, name=f"attn_bwd_g{g}", grid=(NCHUNK,),
        in_specs=[qkv_spec] * 3 + [one_spec] * 3 + [_full((TM, TM))] * 2, out_specs=[one_spec] * 3,
        out_shape=[_sds((S, GW), BF16)] * 3,
        scratch_shapes=[pltpu.VMEM((S, LANES), BF16)] * 6 + [pltpu.VMEM((S, LANES), F32)] * 4,
        compiler_params=_params(("parallel",)),
    )(q, k, v, do, lse_s, dd, jnp.asarray(perm, BF16), jnp.asarray(perm.T, BF16))


def _qkv_bwd(dqs, dks, dvs, dqm, dz, c, s1, s2):
    def body(q0, q1, q2, k0, k1, k2, v0, v1, v2, dqm_ref, dz_ref, c_ref, s1_ref, s2_ref, dp_ref):
        cc, a1, a2 = c_ref[...], s1_ref[...], s2_ref[...]
        for g, (qr, kr, vr) in enumerate(((q0, k0, v0), (q1, k1, v1), (q2, k2, v2))):
            for j in range(GW // 128):
                ls_ = slice(j * 128, (j + 1) * 128)
                c0 = g * GW + j * 128
                dp_ref[:, c0:c0 + 128] = (_rope_bwd(qr[:, ls_].astype(F32), cc, a1, a2) * SCALE).astype(BF16)
                dp_ref[:, NQ + c0:NQ + c0 + 128] = _rope_bwd(kr[:, ls_].astype(F32), cc, a1, a2).astype(BF16)
            dp_ref[:, 2 * NQ + g * GW:2 * NQ + (g + 1) * GW] = vr[...]
        dp_ref[:, 3 * NQ:3 * NQ + MW] = dqm_ref[...]
        dp_ref[:, 3 * NQ + MW:] = dz_ref[...]

    return pl.pallas_call(
        body, name="qkv_bwd", grid=(NT,),
        in_specs=[_rows(GW)] * 9 + [_rows(MW), _rows(BR_A), _rows(128), _rows(128), _rows(128)],
        out_specs=_rows(IN_A), out_shape=_sds((S, IN_A), BF16),
        compiler_params=_params(("parallel",)),
    )(*dqs, *dks, *dvs, dqm, dz, c, s1, s2)


def _mem_bwd(mem, mg, memn, wkv, dkv0, dkv1):
    def body(mem_ref, mg_ref, memn_ref, w_ref, d0_ref, d1_ref, dw_ref, dwb_ref, dg_ref):
        mf = mem_ref[...]
        n = mf * lax.rsqrt(jnp.mean(mf * mf, axis=-1, keepdims=True) + EPS)
        for i, d_ref in enumerate((d0_ref, d1_ref)):
            dkv = d_ref[...].astype(BF16)
            mn = memn_ref[i]
            for s in range(4):
                cs = slice(s * NM, (s + 1) * NM)
                dw = _dot_tn(mn[:, cs], dkv)
                dw_ref[s, i] = dw
                dwb_ref[s, i] = dw.astype(BF16)
                dmn = _dot_nt(dkv, w_ref[s, i])
                dg_ref[i:i + 1, cs] = jnp.sum(dmn * n[:, cs], axis=0, keepdims=True)

    return pl.pallas_call(
        body, name="mem_bwd", grid=(1,),
        in_specs=[_full((NM, D)), _full((2, D)), _full((2, NM, D)), _full((4, 2, NM, 2 * MW)),
                  _full((NM, 2 * MW)), _full((NM, 2 * MW))],
        out_specs=[_full((4, 2, NM, 2 * MW)), _full((4, 2, NM, 2 * MW)), _full((2, D))],
        out_shape=[_sds((4, 2, NM, 2 * MW), F32), _sds((4, 2, NM, 2 * MW), BF16), _sds((2, D), F32)],
        compiler_params=_params(("arbitrary",)),
    )(mem, mg, memn, wkv, dkv0, dkv1)


MESH = pl.DeviceIdType.MESH
ANY = pl.BlockSpec(memory_space=pl.ANY)
BIG = (("wkv", 2, NM, 2 * MW), ("w_in_a", 1, D, SH_A), ("w_out_a", 1, BR_A, SH_O),
       ("w_in_b", 1, D, SH_B), ("w_out_b", 1, BR_B // 4, D))
NBIG = len(BIG)
CW_ROWS = 8


def _place():
    x, y, c = lax.axis_index("x"), lax.axis_index("y"), lax.axis_index("c")
    chips = ((1 - x, y), (x, 1 - y), (1 - x, 1 - y))
    return x, y, c, chips


def _remote(src, dst, ssem, rsem, dev):
    return pltpu.make_async_remote_copy(src_ref=src, dst_ref=dst, send_sem=ssem, recv_sem=rsem,
                                        device_id=dev, device_id_type=MESH)


def _cast_weights(place, ws, after, idx, name):
    nblk = 4
    n = len(idx)
    dims = [BIG[w][1:] for w in idx]

    def body(pref, *refs):
        for i in range(n):
            refs[n + 1 + i][0] = refs[i][...].astype(BF16)

    grid_spec = pltpu.PrefetchScalarGridSpec(
        num_scalar_prefetch=1, grid=(nblk,),
        in_specs=[pl.BlockSpec((k, r // nblk, cdim), lambda i, pref: (0, i, 0)) for k, r, cdim in dims]
        + [pl.BlockSpec(memory_space=pl.ANY)],
        out_specs=[pl.BlockSpec((1, k, r // nblk, cdim), lambda i, pref: (pref[1], 0, i, 0)) for k, r, cdim in dims])
    return pl.pallas_call(
        body, name=name, grid_spec=grid_spec,
        out_shape=[_sds((4, k, r, cdim), BF16) for k, r, cdim in dims],
        compiler_params=_params(("parallel",)),
    )(place, *ws, after)


LAYER_A = (0, 1, 2)
LAYER_B = (3, 4)
HBM = pl.BlockSpec(memory_space=pltpu.HBM)
SEM = pl.BlockSpec(memory_space=pltpu.SEMAPHORE)
EFFECT = pltpu.SideEffectType.DATAFLOW_SIDE_EFFECTING
TOKEN = (8, 128)


def _half(ref, w, which):
    h = BIG[w][2] // 2
    return ref.at[:, pl.ds(which * h, h), :]


def _skip_arg(body, pos, *refs):
    return body(*refs[:pos], *refs[pos + 1:])


def _gather_start(wb, after, idx, name):
    n = len(idx)

    def body(*refs):
        src = refs[:n]
        send_sems, recv_sems = refs[n + 1], refs[n + 2]
        token = refs[2 * n + 3]
        x, y, c, chips = _place()
        me = 2 * x + y
        for j, (px, py) in enumerate(chips):
            for i in range(n):
                mine = _half(src[i].at[me], idx[i], c)
                _remote(mine, mine, send_sems.at[j * n + i], recv_sems.at[j * n + i], (px, py, c)).start()
        token[...] = jnp.zeros(TOKEN, F32)

    outs = pl.pallas_call(
        body, name=name, in_specs=[HBM] * n + [ANY],
        out_specs=(SEM, SEM) + (HBM,) * n + (pl.BlockSpec(memory_space=pltpu.VMEM),),
        out_shape=(pltpu.SemaphoreType.DMA((3 * n,)), pltpu.SemaphoreType.DMA((3 * n,)))
        + tuple(pltpu.HBM(w.shape, w.dtype) for w in wb) + (_sds(TOKEN, F32),),
        input_output_aliases={i: 2 + i for i in range(n)},
        compiler_params=pltpu.CompilerParams(has_side_effects=EFFECT),
    )(*[pltpu.with_memory_space_constraint(w, pltpu.HBM) for w in wb], after)
    return outs[0], outs[1], list(outs[2:2 + n]), outs[2 + n]


def _gather_wait(send_sems, recv_sems, wb, after, idx, name, started=None):
    n = len(idx)
    started = idx if started is None else started
    n_all = len(started)
    pos = [started.index(w) for w in idx]

    def body(*refs):
        buf = refs[:n]
        send_sems, recv_sems = refs[n], refs[n + 1]
        x, y, c, chips = _place()
        me = 2 * x + y
        for j, (px, py) in enumerate(chips):
            for i in range(n):
                mine = _half(buf[i].at[me], idx[i], c)
                got = _half(buf[i].at[2 * px + py], idx[i], c)
                k = j * n_all + pos[i]
                _remote(mine, mine, send_sems.at[k], recv_sems.at[k], (px, py, c)).wait_send()
                _remote(got, got, send_sems.at[k], recv_sems.at[k], (px, py, c)).wait_recv()

    outs = pl.pallas_call(
        body, name=name, in_specs=[HBM] * n + [SEM, SEM] + [ANY] * len(after), out_specs=(HBM,) * n,
        out_shape=tuple(pltpu.HBM(w.shape, w.dtype) for w in wb),
        input_output_aliases={i: i for i in range(n)},
        compiler_params=pltpu.CompilerParams(has_side_effects=EFFECT),
    )(*wb, send_sems, recv_sems, *after)
    return list(outs)


def _gather_forward(wb, idx, name):
    n = len(idx)

    def body(*refs):
        dst = refs[n:2 * n]
        send_sems, recv_sems = refs[2 * n], refs[2 * n + 1]
        x, y, c, chips = _place()
        cps = []
        for j, (px, py) in enumerate(chips):
            for i in range(n):
                got = _half(dst[i].at[2 * px + py], idx[i], c)
                cps.append(_remote(got, got, send_sems.at[j, i], recv_sems.at[j, i], (x, y, 1 - c)))
                cps[-1].start()
        for j, (px, py) in enumerate(chips):
            for i in range(n):
                got = _half(dst[i].at[2 * px + py], idx[i], 1 - c)
                _remote(got, got, send_sems.at[j, i], recv_sems.at[j, i], (x, y, 1 - c)).wait_recv()
        for cp in cps:
            cp.wait_send()

    return pl.pallas_call(
        body, name=name, in_specs=[ANY] * n, out_specs=[ANY] * n, out_shape=[_sds(w.shape, BF16) for w in wb],
        input_output_aliases={i: i for i in range(n)},
        scratch_shapes=[pltpu.SemaphoreType.DMA((3, n)), pltpu.SemaphoreType.DMA((3, n))],
    )(*wb)


def _forward_start(wb, cw, after, idx, name):
    n = len(idx)
    m = n if cw is None else n + 2

    def body(*refs):
        buf = refs[:n]
        send_sems, recv_sems = refs[m + 1], refs[m + 2]
        token = refs[2 * m + 3]
        x, y, c, chips = _place()
        for j, (px, py) in enumerate(chips):
            for i in range(n):
                got = _half(buf[i].at[2 * px + py], idx[i], c)
                _remote(got, got, send_sems.at[j * (n + 1) + i], recv_sems.at[j * (n + 1) + i], (x, y, 1 - c)).start()
            if cw is not None:
                _remote(refs[n], refs[n + 1].at[2 * x + y], send_sems.at[j * (n + 1) + n],
                        recv_sems.at[j * (n + 1) + n], (px, py, c)).start()
        token[...] = jnp.zeros(TOKEN, F32)

    arrays = list(wb) if cw is None else list(wb) + [cw, lax.empty((4, CW_ROWS, SH_O), F32)]
    outs = pl.pallas_call(
        body, name=name, in_specs=[HBM] * m + [ANY],
        out_specs=(SEM, SEM) + (HBM,) * m + (pl.BlockSpec(memory_space=pltpu.VMEM),),
        out_shape=(pltpu.SemaphoreType.DMA((3 * (n + 1),)), pltpu.SemaphoreType.DMA((3 * (n + 1),)))
        + tuple(pltpu.HBM(a.shape, a.dtype) for a in arrays) + (_sds(TOKEN, F32),),
        input_output_aliases={i: 2 + i for i in range(m)},
        compiler_params=pltpu.CompilerParams(has_side_effects=EFFECT),
    )(*[pltpu.with_memory_space_constraint(a, pltpu.HBM) for a in arrays], after)
    return outs[0], outs[1], list(outs[2:2 + m]), outs[2 + m]


def _forward_wait(send_sems, recv_sems, arrays, after, idx, with_cw, name):
    n = len(idx)
    m = len(arrays)

    def body(*refs):
        buf = refs[:n]
        send_sems, recv_sems = refs[m], refs[m + 1]
        x, y, c, chips = _place()
        for j, (px, py) in enumerate(chips):
            for i in range(n):
                sent = _half(buf[i].at[2 * px + py], idx[i], c)
                got = _half(buf[i].at[2 * px + py], idx[i], 1 - c)
                k = j * (n + 1) + i
                _remote(sent, sent, send_sems.at[k], recv_sems.at[k], (x, y, 1 - c)).wait_send()
                _remote(got, got, send_sems.at[k], recv_sems.at[k], (x, y, 1 - c)).wait_recv()
            if with_cw:
                k = j * (n + 1) + n
                theirs = refs[n + 1].at[2 * px + py]
                _remote(refs[n], theirs, send_sems.at[k], recv_sems.at[k], (px, py, c)).wait_send()
                _remote(refs[n], theirs, send_sems.at[k], recv_sems.at[k], (px, py, c)).wait_recv()

    outs = pl.pallas_call(
        body, name=name, in_specs=[HBM] * m + [SEM, SEM] + [ANY] * len(after), out_specs=(HBM,) * m,
        out_shape=tuple(pltpu.HBM(a.shape, a.dtype) for a in arrays),
        input_output_aliases={i: i for i in range(m)},
        compiler_params=pltpu.CompilerParams(has_side_effects=EFFECT),
    )(*arrays, send_sems, recv_sems, *after)
    return list(outs)


def _pair_exchange(gs, idx, name):
    n = len(idx)

    def body(*refs):
        src, dst = refs[:n], refs[n:2 * n]
        send_sems, recv_sems = refs[2 * n:]
        x, y, c, _ = _place()
        cps = []
        for i in range(n):
            h = BIG[idx[i]][2] // 2
            cps.append(_remote(src[i].at[:, :, pl.ds((1 - c) * h, h), :], dst[i], send_sems.at[i], recv_sems.at[i],
                               (x, y, 1 - c)))
            cps[-1].start()
        for cp in cps:
            cp.wait()

    return pl.pallas_call(
        body, name=name, in_specs=[ANY] * n, out_specs=[ANY] * n,
        out_shape=[_sds((4, BIG[w][1], BIG[w][2] // 2, BIG[w][3]), BF16) for w in idx],
        scratch_shapes=[pltpu.SemaphoreType.DMA((n,)), pltpu.SemaphoreType.DMA((n,))],
    )(*gs)


def _pair_start(gs, idx, name):
    n = len(idx)

    def body(*refs):
        src, land = refs[:n], refs[n:2 * n]
        send_sems, recv_sems = refs[2 * n], refs[2 * n + 1]
        token = refs[4 * n + 2]
        x, y, c, _ = _place()
        for i in range(n):
            h = BIG[idx[i]][2] // 2
            _remote(src[i].at[:, :, pl.ds((1 - c) * h, h), :], land[i], send_sems.at[i], recv_sems.at[i],
                    (x, y, 1 - c)).start()
        token[...] = jnp.zeros(TOKEN, F32)

    lands = [lax.empty((4, BIG[w][1], BIG[w][2] // 2, BIG[w][3]), BF16) for w in idx]
    arrays = list(gs) + lands
    outs = pl.pallas_call(
        body, name=name, in_specs=[HBM] * (2 * n),
        out_specs=(SEM, SEM) + (HBM,) * (2 * n) + (pl.BlockSpec(memory_space=pltpu.VMEM),),
        out_shape=(pltpu.SemaphoreType.DMA((n,)), pltpu.SemaphoreType.DMA((n,)))
        + tuple(pltpu.HBM(a.shape, a.dtype) for a in arrays) + (_sds(TOKEN, F32),),
        input_output_aliases={i: 2 + i for i in range(2 * n)},
        compiler_params=pltpu.CompilerParams(has_side_effects=EFFECT),
    )(*[pltpu.with_memory_space_constraint(a, pltpu.HBM) for a in arrays])
    return outs[0], outs[1], list(outs[2:2 + n]), list(outs[2 + n:2 + 2 * n]), outs[2 + 2 * n]


def _pair_wait(send_sems, recv_sems, gs, lands, after, idx, name):
    n = len(idx)

    def body(*refs):
        src, land = refs[:n], refs[n:2 * n]
        send_sems, recv_sems = refs[2 * n], refs[2 * n + 1]
        x, y, c, _ = _place()
        for i in range(n):
            h = BIG[idx[i]][2] // 2
            cp = _remote(src[i].at[:, :, pl.ds((1 - c) * h, h), :], land[i], send_sems.at[i], recv_sems.at[i],
                         (x, y, 1 - c))
            cp.wait_send()
            cp.wait_recv()

    arrays = list(gs) + list(lands)
    outs = pl.pallas_call(
        body, name=name, in_specs=[HBM] * (2 * n) + [SEM, SEM] + [ANY] * len(after), out_specs=(HBM,) * (2 * n),
        out_shape=tuple(pltpu.HBM(a.shape, a.dtype) for a in arrays),
        input_output_aliases={i: i for i in range(2 * n)},
        compiler_params=pltpu.CompilerParams(has_side_effects=EFFECT),
    )(*arrays, send_sems, recv_sems, *after)
    return list(outs[:n]), list(outs[n:])


def _pair_sums(place, gs, r1s, idx, name):
    n = len(idx)
    dims = [(BIG[w][1], BIG[w][2] // 2, BIG[w][3]) for w in idx]

    def body(pref, *refs):
        for i in range(n):
            refs[2 * n + i][...] = (refs[i][...] + refs[n + i][...].astype(F32)).astype(BF16)

    mine = [pl.BlockSpec((1, k, h, cdim), lambda s, pref: (s, 0, pref[0], 0)) for k, h, cdim in dims]
    whole = [pl.BlockSpec((1, k, h, cdim), lambda s, pref: (s, 0, 0, 0)) for k, h, cdim in dims]
    grid_spec = pltpu.PrefetchScalarGridSpec(num_scalar_prefetch=1, grid=(4,), in_specs=mine + whole, out_specs=whole)
    return pl.pallas_call(
        body, name=name, grid_spec=grid_spec, out_shape=[_sds((4, k, h, cdim), BF16) for k, h, cdim in dims],
        compiler_params=_params(("parallel",)),
    )(place, *gs, *r1s)


def _chip_start(ps, idx, name):
    n = len(idx)

    def body(*refs):
        src, land = refs[:n], refs[n:2 * n]
        send_sems, recv_sems = refs[2 * n], refs[2 * n + 1]
        token = refs[4 * n + 2]
        x, y, c, chips = _place()
        for j, (px, py) in enumerate(chips):
            for i in range(n):
                _remote(src[i].at[2 * px + py], land[i].at[j], send_sems.at[j * n + i], recv_sems.at[j * n + i],
                        (px, py, c)).start()
        token[...] = jnp.zeros(TOKEN, F32)

    lands = [lax.empty((3,) + p.shape[1:], BF16) for p in ps]
    outs = pl.pallas_call(
        body, name=name, in_specs=[HBM] * (2 * n),
        out_specs=(SEM, SEM) + (HBM,) * (2 * n) + (pl.BlockSpec(memory_space=pltpu.VMEM),),
        out_shape=(pltpu.SemaphoreType.DMA((3 * n,)), pltpu.SemaphoreType.DMA((3 * n,)))
        + tuple(pltpu.HBM(a.shape, a.dtype) for a in list(ps) + lands) + (_sds(TOKEN, F32),),
        input_output_aliases={i: 2 + i for i in range(2 * n)},
        compiler_params=pltpu.CompilerParams(has_side_effects=EFFECT),
    )(*[pltpu.with_memory_space_constraint(a, pltpu.HBM) for a in list(ps) + lands])
    return outs[0], outs[1], list(outs[2:2 + n]), list(outs[2 + n:2 + 2 * n]), outs[2 + 2 * n]


def _chip_wait(send_sems, recv_sems, ps, lands, after, idx, name):
    n = len(idx)

    def body(*refs):
        src, land = refs[:n], refs[n:2 * n]
        send_sems, recv_sems = refs[2 * n], refs[2 * n + 1]
        x, y, c, chips = _place()
        for j, (px, py) in enumerate(chips):
            for i in range(n):
                cp = _remote(src[i].at[2 * px + py], land[i].at[j], send_sems.at[j * n + i], recv_sems.at[j * n + i],
                             (px, py, c))
                cp.wait_send()
                cp.wait_recv()

    arrays = list(ps) + list(lands)
    outs = pl.pallas_call(
        body, name=name, in_specs=[HBM] * (2 * n) + [SEM, SEM] + [ANY] * len(after), out_specs=(HBM,) * (2 * n),
        out_shape=tuple(pltpu.HBM(a.shape, a.dtype) for a in arrays),
        input_output_aliases={i: i for i in range(2 * n)},
        compiler_params=pltpu.CompilerParams(has_side_effects=EFFECT),
    )(*arrays, send_sems, recv_sems, *after)
    return list(outs[n:])


def _chip_sums(place, gs, r1s, r2s, idx, name):
    n = len(idx)
    dims = [(BIG[w][1], BIG[w][2] // 4, BIG[w][3]) for w in idx]

    def body(pref, *refs):
        for i in range(n):
            acc = refs[i][0] + refs[n + i][0].astype(F32)
            for j in range(3):
                acc = acc + refs[2 * n + i][j].astype(F32)
            refs[3 * n + i][...] = acc

    in_specs = ([pl.BlockSpec((1, k, q, cdim), lambda t, pref: (pref[1], 0, pref[0] * 2 + t, 0)) for k, q, cdim in dims]
                + [pl.BlockSpec((1, k, q, cdim), lambda t, pref: (pref[1], 0, t, 0)) for k, q, cdim in dims]
                + [pl.BlockSpec((3, k, q, cdim), lambda t, pref: (0, 0, t, 0)) for k, q, cdim in dims])
    out_specs = [pl.BlockSpec((k, q, cdim), lambda t, pref: (0, pref[0] * 2 + t, 0)) for k, q, cdim in dims]
    grid_spec = pltpu.PrefetchScalarGridSpec(num_scalar_prefetch=1, grid=(2,), in_specs=in_specs, out_specs=out_specs)
    return pl.pallas_call(
        body, name=name, grid_spec=grid_spec, out_shape=[_sds(BIG[w][1:], F32) for w in idx],
        compiler_params=_params(("parallel",)),
    )(place, *gs, *r1s, *r2s)


def _pair_gather(hs, idx, name):
    n = len(idx)

    def body(*refs):
        dst = refs[n:2 * n]
        send_sems, recv_sems = refs[2 * n:]
        x, y, c, _ = _place()
        cps = []
        for i in range(n):
            mine = _half(dst[i], idx[i], c)
            cps.append(_remote(mine, mine, send_sems.at[i], recv_sems.at[i], (x, y, 1 - c)))
            cps[-1].start()
        for i in range(n):
            theirs = _half(dst[i], idx[i], 1 - c)
            _remote(theirs, theirs, send_sems.at[i], recv_sems.at[i], (x, y, 1 - c)).wait_recv()
        for cp in cps:
            cp.wait_send()

    return pl.pallas_call(
        body, name=name, in_specs=[ANY] * n, out_specs=[ANY] * n,
        out_shape=[_sds(BIG[w][1:], F32) for w in idx],
        input_output_aliases={i: i for i in range(n)},
        scratch_shapes=[pltpu.SemaphoreType.DMA((n,)), pltpu.SemaphoreType.DMA((n,))],
    )(*hs)


SMALL_ROWS = 40


def _adamw_math(w, g, m, v):
    m = ADAM_B1 * m + (1.0 - ADAM_B1) * g
    v = ADAM_B2 * v + (1.0 - ADAM_B2) * (g * g)
    m_hat = m / (1.0 - ADAM_B1 ** ADAM_STEP)
    v_hat = v / (1.0 - ADAM_B2 ** ADAM_STEP)
    delta = -ADAM_LR * (m_hat / (jnp.sqrt(v_hat) + ADAM_EPS) + ADAM_WD * w)
    return delta, m, v


def _small_start(pack, after):
    def body(pack_ref, land_ref, after_ref, send_sems, recv_sems, pack_thru, land_thru, token):
        x, y, c, _ = _place()
        for r in range(1, 8):
            peer = (x if not r & 4 else 1 - x, y if not r & 2 else 1 - y, c if not r & 1 else 1 - c)
            _remote(pack_ref, land_ref.at[r - 1], send_sems.at[r - 1], recv_sems.at[r - 1], peer).start()
        token[...] = jnp.zeros(TOKEN, F32)

    land = lax.empty((7, SMALL_ROWS, D), F32)
    outs = pl.pallas_call(
        body, name="small_start", in_specs=[HBM, HBM, ANY],
        out_specs=(SEM, SEM, HBM, HBM, pl.BlockSpec(memory_space=pltpu.VMEM)),
        out_shape=(pltpu.SemaphoreType.DMA((7,)), pltpu.SemaphoreType.DMA((7,)), pltpu.HBM(pack.shape, F32),
                   pltpu.HBM(land.shape, F32), _sds(TOKEN, F32)),
        input_output_aliases={0: 2, 1: 3},
        compiler_params=pltpu.CompilerParams(has_side_effects=EFFECT),
    )(pltpu.with_memory_space_constraint(pack, pltpu.HBM), pltpu.with_memory_space_constraint(land, pltpu.HBM), after)
    return outs


def _small_wait(send_sems, recv_sems, pack, land, after):
    def body(pack_ref, land_ref, send_sems, recv_sems, *rest):
        x, y, c, _ = _place()
        for r in range(1, 8):
            peer = (x if not r & 4 else 1 - x, y if not r & 2 else 1 - y, c if not r & 1 else 1 - c)
            cp = _remote(pack_ref, land_ref.at[r - 1], send_sems.at[r - 1], recv_sems.at[r - 1], peer)
            cp.wait_send()
            cp.wait_recv()

    return pl.pallas_call(
        body, name="small_wait", in_specs=[HBM, HBM, SEM, SEM] + [ANY] * len(after), out_specs=(HBM, HBM),
        out_shape=(pltpu.HBM(pack.shape, F32), pltpu.HBM(land.shape, F32)),
        input_output_aliases={0: 0, 1: 1},
        compiler_params=pltpu.CompilerParams(has_side_effects=EFFECT),
    )(pack, land, send_sems, recv_sems, *after)


def _small_update(place, pack, land, ws, ms, vs):
    n = len(ws)

    def body(pref, pack_ref, land_ref, *refs):
        chip = pref[1]
        me = 2 * chip + pref[0]
        own = pack_ref[...]
        tot = None
        for dev in range(8):
            r = jnp.bitwise_xor(me, dev)
            term = jnp.where(r == 0, own, land_ref[jnp.maximum(r - 1, 0)])
            tot = term if tot is None else tot + term
        out, buf = refs[3 * n:-1], refs[-1]
        buf[...] = tot
        g_conv = jnp.zeros((3, SH_O), F32)
        for s in range(4):
            g_conv = g_conv + jnp.where(chip == s, buf[24:27, s * SH_O:(s + 1) * SH_O], 0.0)
        gs = [buf[0:2, :], buf[8:10, :], buf[16:17, :], g_conv]
        out[0][...] = buf[32:33, 0:128]
        for i in range(n):
            d, nm, nv = _adamw_math(refs[i][...], gs[i], refs[n + i][...], refs[2 * n + i][...])
            out[1 + i][...] = gs[i]
            out[1 + n + i][...] = d
            out[1 + 2 * n + i][...] = nm
            out[1 + 3 * n + i][...] = nv

    def full(shape):
        nd = len(shape)
        return pl.BlockSpec(shape, lambda i, pref: (0,) * nd)

    specs = [full(w.shape) for w in ws]
    grid_spec = pltpu.PrefetchScalarGridSpec(
        num_scalar_prefetch=1, grid=(1,),
        in_specs=[full(pack.shape), full(land.shape)] + specs * 3, out_specs=[full((1, 128))] + specs * 4,
        scratch_shapes=[pltpu.VMEM((SMALL_ROWS, D), F32)])
    outs = pl.pallas_call(
        body, name="small_update", grid_spec=grid_spec,
        out_shape=[_sds((1, 128), F32)] + [_sds(w.shape, F32) for w in ws] * 4,
        compiler_params=_params(("arbitrary",)),
    )(place, pack, land, *ws, *ms, *vs)
    return outs[0], outs[1:1 + n], outs[1 + n:1 + 2 * n], outs[1 + 2 * n:1 + 3 * n], outs[1 + 3 * n:]


def _adamw_layer(ws, gs, ms, vs, idx, name):
    n = len(idx)
    dims = [(BIG[w][1], BIG[w][2] // 4, BIG[w][3]) for w in idx]

    def body(*refs):
        for i in range(n):
            gv = refs[n + i][...]
            d, nm, nv = _adamw_math(refs[i][...], gv, refs[2 * n + i][...], refs[3 * n + i][...])
            refs[4 * n + i][...] = d
            refs[5 * n + i][...] = nm
            refs[6 * n + i][...] = nv
            refs[7 * n + i][...] = gv

    specs = [pl.BlockSpec((k, q, cdim), lambda t: (0, t, 0)) for k, q, cdim in dims]
    outs = pl.pallas_call(
        body, name=name, grid=(4,), in_specs=specs * 4, out_specs=specs * 4,
        out_shape=[_sds(BIG[w][1:], F32) for w in idx] * 4,
        compiler_params=_params(("parallel",)),
    )(*ws, *gs, *ms, *vs)
    return [tuple(outs[j * n + i] for j in range(4)) for i in range(n)]


def _pad_rows(a, rows):
    return jnp.pad(a, ((0, rows - a.shape[0]), (0, 0)))


def kernel(x, mem, positions, norm_g, mem_norm_g, w_mem_kv, attn_w_in, attn_w_out, conv_w_in, conv_w, conv_w_out, final_g, loss_target, m_norm_g, m_mem_norm_g, m_w_mem_kv, m_attn_w_in, m_attn_w_out, m_conv_w_in, m_conv_w, m_conv_w_out, m_final_g, v_norm_g, v_mem_norm_g, v_w_mem_kv, v_attn_w_in, v_attn_w_out, v_conv_w_in, v_conv_w, v_conv_w_out, v_final_g):
    mx, my, mc = lax.axis_index("x"), lax.axis_index("y"), lax.axis_index("c")
    place = jnp.stack([mc, 2 * mx + my]).astype(jnp.int32)

    w_big = [w_mem_kv, attn_w_in, attn_w_out, conv_w_in, conv_w_out]
    m_big = [m_w_mem_kv, m_attn_w_in, m_attn_w_out, m_conv_w_in, m_conv_w_out]
    v_big = [v_w_mem_kv, v_attn_w_in, v_attn_w_out, v_conv_w_in, v_conv_w_out]
    first, rest = (1,), (0, 2, 3, 4)
    wb1 = _cast_weights(place, [w_big[i] for i in first], place, first, "cast_w_in_a")
    a1_send, a1_recv, a1_bufs, a1_token = _gather_start(wb1, place, first, "gather_a1_start")
    wbr = _cast_weights(place, [w_big[i] for i in rest], a1_token, rest, "cast_weights")
    r_send, r_recv, r_bufs, gb_token = _gather_start(wbr, a1_token, rest, "gather_rest_start")
    a2_send, a2_recv, gb_send, gb_recv = r_send, r_recv, r_send, r_recv
    a2_bufs, gb_bufs = r_bufs[:2], r_bufs[2:]
    started, rest = rest, (0, 2)

    xs, tgt = x[0], loss_target[0]
    g0, g1 = norm_g[0:1], norm_g[1:2]
    rc, rs1, rs2 = _rope_tables(positions[0].astype(F32).reshape(S, 1), gb_token)
    a1_bufs = _gather_wait(a1_send, a1_recv, a1_bufs, [rc], first, "gather_a1_wait")
    w_in_a = _gather_forward(a1_bufs, first, "gather_a1_forward")[0].reshape(4, D, SH_A)
    hn0, q, k, v, qm0, z0 = _in_proj_a(xs, g0, w_in_a, rc, rs1, rs2, gb_token)
    a2_bufs = _gather_wait(a2_send, a2_recv, a2_bufs, [q], rest, "gather_a2_wait", started)
    f2_send, f2_recv, a2_bufs, f2_token = _forward_start(a2_bufs, None, q, rest, "forward_a2_start")
    fwd = [_attn_fwd(q, k, v, 0, f2_token)]
    fwd.append(_attn_fwd(q, k, v, 1, fwd[0][0]))
    cw_own = _pad_rows(conv_w[0], CW_ROWS)
    gb_bufs = _gather_wait(gb_send, gb_recv, gb_bufs, [fwd[1][0]], LAYER_B, "gather_b_wait", started)
    fb_send, fb_recv, gb_bufs, fb_token = _forward_start(gb_bufs, cw_own, fwd[1][0], LAYER_B, "forward_b_start")
    fwd.append(_attn_fwd(q, k, v, 2, fb_token))
    os_, ls, lss = [f[0] for f in fwd], [f[1] for f in fwd], [f[2] for f in fwd]
    wkv_f, w_out_a = _forward_wait(f2_send, f2_recv, a2_bufs, [os_[2]], rest, False, "forward_a2_wait")
    w_out_a = w_out_a.reshape(4, BR_A, SH_O)
    memn, kv = _mem_fwd(mem[0], mem_norm_g, wkv_f)
    h1 = _attn_out(os_, ls, qm0, kv[0], z0, xs, w_out_a)

    w_in_b, w_out_b, _, cw_f = _forward_wait(fb_send, fb_recv, gb_bufs, [h1], LAYER_B, True, "forward_b_wait")
    w_in_b = w_in_b.reshape(4, D, SH_B)
    w_out_b = w_out_b.reshape(BR_B, D)
    cw_f = lax.dynamic_update_slice(cw_f, cw_own[None], (2 * mx + my, 0, 0))
    cw8 = cw_f.transpose(1, 0, 2).reshape(CW_ROWS, D)
    hn1, bg, cg, u, qm1, z1 = _in_proj_b(h1, g1, w_in_b)
    dh2, loss_part, dfg = _conv_out_loss(bg, cg, u, cw8, qm1, kv[1], z1, h1, w_out_b, final_g.reshape(1, D), tgt)

    dproj_b, dw_out_b, dcw, dkv1, dw_out_b16 = _conv_bwd(dh2, bg, cg, u, cw8, qm1, kv[1], z1, w_out_b)
    dw_in_b, dw_in_b16 = _w_in_grad(hn1, dproj_b, IN_B, "w_in_b_grad")
    gs_b = [dw_in_b.reshape(4, 1, D, SH_B), dw_out_b.reshape(4, 1, BR_B // 4, D)]
    gb_b = [dw_in_b16.reshape(4, 1, D, SH_B), dw_out_b16.reshape(4, 1, BR_B // 4, D)]
    pb_send, pb_recv, gb_b, pb_land, pb_token = _pair_start(gb_b, LAYER_B, "pair_b_start")
    dh1, dg1 = _in_proj_bwd(dproj_b, w_in_b, h1, g1, dh2, pb_token, IN_B, "in_proj_b_bwd")
    _, r1_b = _pair_wait(pb_send, pb_recv, gb_b, pb_land, [dh1], LAYER_B, "pair_b_wait")
    ps_b = _pair_sums(place, gs_b, r1_b, LAYER_B, "pair_sums_b")
    cb_send, cb_recv, cb_src, cb_land, cb_token = _chip_start(ps_b, LAYER_B, "chip_b_start")

    outs = _attn_out_bwd(dh1, os_, ls, qm0, kv[0], z0, w_out_a, cb_token)
    dos, dds, dqm, dz, dw_out_a, dkv0, dw_out_a16 = outs[0:3], outs[3:6], outs[6], outs[7], outs[8], outs[9], outs[10]
    bwd = [_attn_bwd(q, k, v, dos[g], lss[g], dds[g], g) for g in range(3)]
    dproj_a = _qkv_bwd([b[0] for b in bwd], [b[1] for b in bwd], [b[2] for b in bwd], dqm, dz, rc, rs1, rs2)
    dw_in_a, dw_in_a16 = _w_in_grad(hn0, dproj_a, IN_A, "w_in_a_grad")
    dwkv, dwkv16, dmg = _mem_bwd(mem[0], mem_norm_g, memn, wkv_f, dkv0, dkv1)

    gs_a = [dwkv, dw_in_a.reshape(4, 1, D, SH_A), dw_out_a.reshape(4, 1, BR_A, SH_O)]
    r1_a = _pair_exchange([dwkv16, dw_in_a16.reshape(4, 1, D, SH_A), dw_out_a16.reshape(4, 1, BR_A, SH_O)], LAYER_A,
                          "pair_exchange_a")
    ps_a = _pair_sums(place, gs_a, r1_a, LAYER_A, "pair_sums_a")
    ca_send, ca_recv, ca_src, ca_land, ca_token = _chip_start(ps_a, LAYER_A, "chip_a_start")

    gx, dg0 = _in_proj_bwd(dproj_a, w_in_a, xs, g0, dh1, ca_token, IN_A, "in_proj_a_bwd")
    pack = jnp.concatenate([_pad_rows(jnp.concatenate([dg0, dg1], axis=0), 8), _pad_rows(dmg, 8), _pad_rows(dfg, 8),
                            dcw, _pad_rows(jnp.pad(loss_part, ((0, 0), (0, D - 128))), 8)], axis=0)
    sm_send, sm_recv, pack, sm_land, sm_token = _small_start(pack, ca_token)
    r2_b = _chip_wait(cb_send, cb_recv, cb_src, cb_land, [ca_token], LAYER_B, "chip_b_wait")
    hs_b = _chip_sums(place, gs_b, r1_b, r2_b, LAYER_B, "chip_sums_b")
    g_b = _pair_gather(hs_b, LAYER_B, "pair_gather_b")
    upd_b = _adamw_layer([w_big[w] for w in LAYER_B], g_b, [m_big[w] for w in LAYER_B], [v_big[w] for w in LAYER_B],
                         LAYER_B, "adamw_b")
    r2_a = _chip_wait(ca_send, ca_recv, ca_src, ca_land, [gx, upd_b[0][0], upd_b[1][0], sm_token], LAYER_A,
                      "chip_a_wait")
    hs_a = _chip_sums(place, gs_a, r1_a, r2_a, LAYER_A, "chip_sums_a")
    g_a = _pair_gather(hs_a, LAYER_A, "pair_gather_a")
    upd_a = _adamw_layer([w_big[w] for w in LAYER_A], g_a, [m_big[w] for w in LAYER_A], [v_big[w] for w in LAYER_A],
                         LAYER_A, "adamw_a")
    upd = upd_a + upd_b
    g_big = [u[3] for u in upd]
    pack, sm_land = _small_wait(sm_send, sm_recv, pack, sm_land, [r2_a[0]])
    sw = [norm_g, mem_norm_g, final_g.reshape(1, D), conv_w[0]]
    sm = [m_norm_g, m_mem_norm_g, m_final_g.reshape(1, D), m_conv_w[0]]
    sv = [v_norm_g, v_mem_norm_g, v_final_g.reshape(1, D), v_conv_w[0]]
    loss_row, sg, sd, snm, snv = _small_update(place, pack, sm_land, sw, sm, sv)
    loss = loss_row[0, 0]
    g_norm, g_memnorm, g_final, g_conv = sg

    def order(norm, memnorm, wkv, w_in_a, w_out_a, w_in_b, conv, w_out_b, final):
        return (norm, memnorm, wkv, w_in_a, w_out_a, w_in_b, conv.reshape(1, 3, SH_O), w_out_b, final.reshape(D))

    grads = order(g_norm, g_memnorm, g_big[0], g_big[1], g_big[2], g_big[3], g_conv, g_big[4], g_final)
    deltas = order(sd[0], sd[1], upd[0][0], upd[1][0], upd[2][0], upd[3][0], sd[3], upd[4][0], sd[2])
    new_m = order(snm[0], snm[1], upd[0][1], upd[1][1], upd[2][1], upd[3][1], snm[3], upd[4][1], snm[2])
    new_v = order(snv[0], snv[1], upd[0][2], upd[1][2], upd[2][2], upd[3][2], snv[3], upd[4][2], snv[2])
    return (loss, gx[None], *grads, *deltas, *new_m, *new_v)
```

```python
import functools

import numpy as np
import jax
import jax.numpy as jnp
from jax import lax
from jax.experimental import pallas as pl
from jax.experimental.pallas import tpu as pltpu

F32 = jnp.float32
BF16 = jnp.bfloat16

S = 2048
D = 1024
TM = 256
NT = S // TM
MX = 512
NX = S // MX
HD = 64
GW = 512
NQ = 3 * GW
MW = 256
NM = 256
IN_A = 3 * NQ + MW + GW + MW
IN_B = 3 * D + MW + D + MW
BR_A = GW + MW
BR_B = D + MW
SH_A = IN_A // 4
SH_B = IN_B // 4
SH_O = D // 4
QBLK = 128
DILATIONS = (1, 4, 16)
EPS = 1e-6
SCALE = HD ** -0.5
NEG = -1e30
ROPE_THETA = 500000.0

ADAM_LR = 0.001
ADAM_B1 = 0.9
ADAM_B2 = 0.999
ADAM_EPS = 1e-08
ADAM_WD = 0.01
ADAM_STEP = 10

VMEM_LIMIT_BYTES = 60 * 1024 * 1024


def _params(sem=None):
    if sem is None:
        return pltpu.CompilerParams(vmem_limit_bytes=VMEM_LIMIT_BYTES)
    return pltpu.CompilerParams(dimension_semantics=sem, vmem_limit_bytes=VMEM_LIMIT_BYTES)


def _full(shape):
    nd = len(shape)
    return pl.BlockSpec(shape, lambda *_: (0,) * nd)


def _rows(width, tm=TM):
    return pl.BlockSpec((tm, width), lambda i: (i, 0))


def _sds(shape, dtype):
    return jax.ShapeDtypeStruct(shape, dtype)


def _silu_parts(z):
    sig = 0.5 * jnp.tanh(0.5 * z) + 0.5
    return z * sig, sig * (1.0 + z * (1.0 - sig))


def _dot(a, b):
    return jnp.dot(a, b, preferred_element_type=F32)


def _dot_nt(a, b):
    return lax.dot_general(a, b, (((1,), (1,)), ((), ())), preferred_element_type=F32)


def _dot_tn(a, b):
    return lax.dot_general(a, b, (((0,), (0,)), ((), ())), preferred_element_type=F32)


def _rope_fwd(t, c, s1, s2):
    return t * c + pltpu.roll(t, 120, 1) * s1 + pltpu.roll(t, 8, 1) * s2


def _rope_bwd(g, c, s1, s2):
    return g * c + pltpu.roll(g * s1, 8, 1) + pltpu.roll(g * s2, 120, 1)


MEM_HEADS = MW // HD


def _stack_heads(x):
    head = lax.broadcasted_iota(jnp.int32, x.shape, 1) // HD
    return jnp.concatenate([jnp.where(head == h, x, 0.0) for h in range(MEM_HEADS)], axis=0).astype(BF16)


def _unstack_heads(x4):
    tm = x4.shape[0] // MEM_HEADS
    head = lax.broadcasted_iota(jnp.int32, (tm, MW), 1) // HD
    out = x4[:tm]
    for h in range(1, MEM_HEADS):
        out = jnp.where(head == h, x4[h * tm:(h + 1) * tm], out)
    return out


def _mem_attn(qm, kv):
    q4 = _stack_heads(qm.astype(F32))
    s = _dot_nt(q4, kv[:, :MW]) * SCALE
    e = jnp.exp(s - jnp.max(s, axis=-1, keepdims=True))
    p = e * (1.0 / jnp.sum(e, axis=-1, keepdims=True))
    return p, _unstack_heads(_dot(p.astype(BF16), kv[:, MW:])), q4


def _mem_attn_bwd(dmo, p, mo, q4, kv, dkv_ref):
    tm = dmo.shape[0]
    head = lax.broadcasted_iota(jnp.int32, dmo.shape, 1) // HD
    prod = dmo * mo
    delta = jnp.concatenate([jnp.sum(jnp.where(head == h, prod, 0.0), axis=-1, keepdims=True)
                             for h in range(MEM_HEADS)], axis=0)
    d4 = _stack_heads(dmo)
    ds = (p * (_dot_nt(d4, kv[:, MW:]) - delta) * SCALE).astype(BF16)
    dkv_ref[:, :MW] += _dot_tn(ds, q4)
    dkv_ref[:, MW:] += _dot_tn(p.astype(BF16), d4)
    return _unstack_heads(_dot(ds, kv[:, :MW]))


def _merge(o_refs, l_refs):
    ls = [r[...] for r in l_refs]
    m = jnp.maximum(jnp.maximum(ls[0], ls[1]), ls[2])
    es = [jnp.exp(l - m) for l in ls]
    inv = 1.0 / (es[0] + es[1] + es[2])
    ws = [e * inv for e in es]
    os_ = [r[...] for r in o_refs]
    mix = ws[0] * os_[0] + ws[1] * os_[1] + ws[2] * os_[2]
    return ws, mix


def _conv_taps(cg, u, cgp, up, first):
    a = cg * u
    ap = jnp.where(first, 0.0, cgp * up)
    row = lax.broadcasted_iota(jnp.int32, a.shape, 0)
    a1 = jnp.where(row == 0, ap[7:8, :], pltpu.roll(a, 1, 0))
    a2 = jnp.where(row == 0, ap[6:7, :], jnp.where(row == 1, ap[7:8, :], pltpu.roll(a, 2, 0)))
    return a, a1, a2


def _rope_tables(posf, after):
    half = 8
    invf = np.float32(ROPE_THETA) ** (-np.arange(half, dtype=np.float32) * np.float32(2.0 / 16))
    lane = np.arange(128)
    table = np.where((lane % HD) < 16, invf[lane % half], 0.0).astype(np.float32)[None, :]

    def body(pos_ref, invf_ref, c_ref, s1_ref, s2_ref):
        ang = pos_ref[...] * invf_ref[...]
        jm = lax.broadcasted_iota(jnp.int32, ang.shape, 1) & (HD - 1)
        cs = jnp.cos(ang)
        sn = jnp.sin(ang)
        c_ref[...] = jnp.where(jm < 16, cs, 1.0)
        s1_ref[...] = jnp.where(jm < 8, -sn, 0.0)
        s2_ref[...] = jnp.where((jm >= 8) & (jm < 16), sn, 0.0)

    out = _sds((S, 128), F32)
    return pl.pallas_call(
        functools.partial(_skip_arg, body, 2), name="rope_tables", grid=(NT,),
        in_specs=[_rows(1), _full((1, 128)), pl.BlockSpec(memory_space=pl.ANY)],
        out_specs=[_rows(128)] * 3, out_shape=[out] * 3,
        compiler_params=_params(("parallel",)),
    )(posf, jnp.asarray(table), after)


def _in_proj_a(x, g0, w_in, c, s1, s2, after):
    def body(x_ref, g_ref, w_ref, c_ref, s1_ref, s2_ref, hn_ref, q_ref, k_ref, v_ref, qm_ref, z_ref, proj):
        xf = x_ref[...]
        hn = xf * lax.rsqrt(jnp.mean(xf * xf, axis=-1, keepdims=True) + EPS) * g_ref[...]
        hb = hn.astype(BF16)
        hn_ref[...] = hb
        for s in range(4):
            proj[:, s * SH_A:(s + 1) * SH_A] = _dot(hb, w_ref[s])
        cc, a1, a2 = c_ref[...], s1_ref[...], s2_ref[...]
        for j in range(NQ // 128):
            q_ref[:, j * 128:(j + 1) * 128] = (
                _rope_fwd(proj[:, j * 128:(j + 1) * 128], cc, a1, a2) * SCALE).astype(BF16)
            k_ref[:, j * 128:(j + 1) * 128] = _rope_fwd(
                proj[:, NQ + j * 128:NQ + (j + 1) * 128], cc, a1, a2).astype(BF16)
        v_ref[...] = proj[:, 2 * NQ:3 * NQ].astype(BF16)
        qm_ref[...] = proj[:, 3 * NQ:3 * NQ + MW].astype(BF16)
        z_ref[...] = proj[:, 3 * NQ + MW:]

    return pl.pallas_call(
        functools.partial(_skip_arg, body, 6), name="in_proj_a", grid=(NT,),
        in_specs=[_rows(D), _full((1, D)), _full((4, D, SH_A)), _rows(128), _rows(128), _rows(128),
                  pl.BlockSpec(memory_space=pl.ANY)],
        out_specs=[_rows(D), _rows(NQ), _rows(NQ), _rows(NQ), _rows(MW), _rows(BR_A)],
        out_shape=[_sds((S, D), BF16), _sds((S, NQ), BF16), _sds((S, NQ), BF16), _sds((S, NQ), BF16),
                   _sds((S, MW), BF16), _sds((S, BR_A), F32)],
        scratch_shapes=[pltpu.VMEM((TM, IN_A), F32)],
        compiler_params=_params(("parallel",)),
    )(x, g0, w_in, c, s1, s2, after)


def _mem_fwd(mem, mg, wkv):
    def body(mem_ref, mg_ref, w_ref, memn_ref, kv_ref):
        mf = mem_ref[...]
        n = mf * lax.rsqrt(jnp.mean(mf * mf, axis=-1, keepdims=True) + EPS)
        for i in range(2):
            mn = (n * mg_ref[i:i + 1, :]).astype(BF16)
            memn_ref[i] = mn
            acc = _dot(mn[:, 0:NM], w_ref[0, i])
            for s in range(1, 4):
                acc += _dot(mn[:, s * NM:(s + 1) * NM], w_ref[s, i])
            kv_ref[i] = acc.astype(BF16)

    return pl.pallas_call(
        body, name="mem_fwd", grid=(1,),
        in_specs=[_full((NM, D)), _full((2, D)), _full((4, 2, NM, 2 * MW))],
        out_specs=[_full((2, NM, D)), _full((2, NM, 2 * MW))],
        out_shape=[_sds((2, NM, D), BF16), _sds((2, NM, 2 * MW), BF16)],
        compiler_params=_params(("arbitrary",)),
    )(mem, mg, wkv)


def _band_mask(j):
    qi = lax.broadcasted_iota(jnp.int32, (QBLK, 2 * QBLK), 0)
    kj = lax.broadcasted_iota(jnp.int32, (QBLK, 2 * QBLK), 1)
    dist = qi + QBLK - kj
    return (dist >= 0) & (dist <= QBLK) & ((kj >= QBLK) | (j > 0))


LANES = 128
NCHUNK = GW // LANES
FWD_UNROLL = 16
BWD_UNROLL = 16
CONV_CHUNK = 256


def _perm_matrix(d):
    n = TM // d
    p = np.zeros((TM, TM), np.float32)
    for r in range(d):
        for i in range(n):
            p[r * n + i, i * d + r] = 1.0
    return p


def _split_dot(p, x):
    hi = x.astype(BF16)
    lo = (x - hi.astype(F32)).astype(BF16)
    both = _dot(p, jnp.concatenate([hi, lo], axis=1))
    return both[:, :LANES] + both[:, LANES:]


def _pair_dot(p, a, b):
    both = _dot(p, jnp.concatenate([a, b], axis=1))
    return both[:, :LANES], both[:, LANES:]


def _tile_to_streams(y, dst, t, d):
    n, ln = TM // d, S // d
    for r in range(d):
        dst[r * ln + t * n:r * ln + (t + 1) * n, :] = y[r * n:(r + 1) * n].astype(dst.dtype)


def _tile_from_streams(src, t, d):
    n, ln = TM // d, S // d
    return jnp.concatenate([src[r * ln + t * n:r * ln + (t + 1) * n, :] for r in range(d)], axis=0)


def _head_masks():
    first = lax.broadcasted_iota(jnp.int32, (TM, LANES), 1) < HD
    return first, jnp.logical_not(first)


def _attn_fwd(q, k, v, g, after):
    d = DILATIONS[g]
    nb = S // d // QBLK
    perm = _perm_matrix(d)

    def body(q_ref, k_ref, v_ref, p_ref, pt_ref, o_ref, l_ref, ls_ref, q0, q1, ks, vs, os_):
        first, second = _head_masks()
        pm = p_ref[...]
        for t in range(NT):
            rows = slice(t * TM, (t + 1) * TM)
            if d == 1:
                qt = q_ref[rows, :].astype(F32)
            else:
                qt, kt = _pair_dot(pm, q_ref[rows, :], k_ref[rows, :])
                _tile_to_streams(kt, ks, t, d)
                if t % 2 == 0:
                    va, vb = _pair_dot(pm, v_ref[rows, :], v_ref[(t + 1) * TM:(t + 2) * TM, :])
                    _tile_to_streams(va, vs, t, d)
                    _tile_to_streams(vb, vs, t + 1, d)
            _tile_to_streams(jnp.where(first, qt, 0.0), q0, t, d)
            _tile_to_streams(jnp.where(second, qt, 0.0), q1, t, d)
        kref, vref = (k_ref, v_ref) if d == 1 else (ks, vs)
        oref, lref = (o_ref, l_ref) if d == 1 else (os_, ls_ref)

        def blk(b, carry):
            r0 = pl.multiple_of(b * QBLK, QBLK)
            p0 = pl.multiple_of(jnp.maximum(b - 1, 0) * QBLK, QBLK)
            kk = jnp.concatenate([kref[pl.ds(p0, QBLK), :], kref[pl.ds(r0, QBLK), :]], axis=0)
            vv = jnp.concatenate([vref[pl.ds(p0, QBLK), :], vref[pl.ds(r0, QBLK), :]], axis=0)
            valid = _band_mask(b & (nb - 1))
            acc, lse = [], []
            for qh in (q0, q1):
                s = jnp.where(valid, _dot_nt(qh[pl.ds(r0, QBLK), :], kk), NEG)
                m = jnp.max(s, axis=-1, keepdims=True)
                e = jnp.exp(s - m)
                l = jnp.sum(e, axis=-1, keepdims=True)
                acc.append(_dot(e.astype(BF16), vv) * (1.0 / l))
                lse.append(m + jnp.log(l))
            f = first[:QBLK]
            oref[pl.ds(r0, QBLK), :] = jnp.where(f, acc[0], acc[1])
            lref[pl.ds(r0, QBLK), :] = jnp.where(f, lse[0], lse[1])
            return carry

        lax.fori_loop(0, S // QBLK, blk, 0, unroll=FWD_UNROLL)
        if d > 1:
            ptm = pt_ref[...]
            for t in range(NT):
                rows = slice(t * TM, (t + 1) * TM)
                o_ref[rows, :] = _split_dot(ptm, _tile_from_streams(os_, t, d))
                l_ref[rows, :] = _split_dot(ptm, _tile_from_streams(ls_ref, t, d))

    qkv_spec = pl.BlockSpec((S, LANES), lambda c: (0, g * NCHUNK + c))
    out_spec = pl.BlockSpec((S, LANES), lambda c: (0, c))
    n_out = 2 if d == 1 else 3
    inner = body if d > 1 else functools.partial(_drop_arg, body, 7)
    outs = pl.pallas_call(
        functools.partial(_skip_arg, inner, 5), name=f"attn_fwd_g{g}", grid=(NCHUNK,),
        in_specs=[qkv_spec] * 3 + [_full((TM, TM))] * 2 + [pl.BlockSpec(memory_space=pl.ANY)],
        out_specs=[out_spec] * n_out, out_shape=[_sds((S, GW), F32)] * n_out,
        scratch_shapes=[pltpu.VMEM((S, LANES), BF16)] * 4 + [pltpu.VMEM((S, LANES), F32)],
        compiler_params=_params(("parallel",)),
    )(q, k, v, jnp.asarray(perm, BF16), jnp.asarray(perm.T, BF16), after)
    return (outs[0], outs[1], outs[1]) if d == 1 else tuple(outs)


def _drop_arg(body, pos, *refs):
    return body(*refs[:pos], None, *refs[pos:])


def _attn_out(os_, ls, qm, kv0, z, x, w_out):
    def body(o0, o1, o2, l0, l1, l2, qm_ref, kv_ref, z_ref, x_ref, w_ref, h_ref, ybuf):
        _, mix = _merge((o0, o1, o2), (l0, l1, l2))
        sz, _ = _silu_parts(z_ref[...])
        ybuf[:, :GW] = (mix * sz[:, :GW]).astype(BF16)
        _, mo, _ = _mem_attn(qm_ref[...], kv_ref[...])
        ybuf[:, GW:] = (mo * sz[:, GW:]).astype(BF16)
        yb = ybuf[...]
        for s in range(4):
            cs = slice(s * SH_O, (s + 1) * SH_O)
            h_ref[:, cs] = x_ref[:, cs] + _dot(yb, w_ref[s])

    return pl.pallas_call(
        body, name="attn_out", grid=(NX,),
        in_specs=[_rows(GW, MX)] * 6 + [_rows(MW, MX), _full((NM, 2 * MW)), _rows(BR_A, MX), _rows(D, MX),
                                        _full((4, BR_A, SH_O))],
        out_specs=_rows(D, MX), out_shape=_sds((S, D), F32),
        scratch_shapes=[pltpu.VMEM((MX, BR_A), BF16)],
        compiler_params=_params(("parallel",)),
    )(*os_, *ls, qm, kv0, z, x, w_out)


def _in_proj_b(h1, g1, w_in):
    def body(x_ref, g_ref, w_ref, hn_ref, bg_ref, cg_ref, u_ref, qm_ref, z_ref, proj):
        xf = x_ref[...]
        hn = xf * lax.rsqrt(jnp.mean(xf * xf, axis=-1, keepdims=True) + EPS) * g_ref[...]
        hb = hn.astype(BF16)
        hn_ref[...] = hb
        for s in range(4):
            proj[:, s * SH_B:(s + 1) * SH_B] = _dot(hb, w_ref[s])
        bg_ref[...] = proj[:, :D]
        cg_ref[...] = proj[:, D:2 * D]
        u_ref[...] = proj[:, 2 * D:3 * D]
        qm_ref[...] = proj[:, 3 * D:3 * D + MW].astype(BF16)
        z_ref[...] = proj[:, 3 * D + MW:]

    return pl.pallas_call(
        body, name="in_proj_b", grid=(NT,),
        in_specs=[_rows(D), _full((1, D)), _full((4, D, SH_B))],
        out_specs=[_rows(D), _rows(D), _rows(D), _rows(D), _rows(MW), _rows(BR_B)],
        out_shape=[_sds((S, D), BF16), _sds((S, D), F32), _sds((S, D), F32), _sds((S, D), F32),
                   _sds((S, MW), BF16), _sds((S, BR_B), F32)],
        scratch_shapes=[pltpu.VMEM((TM, IN_B), F32)],
        compiler_params=_params(("parallel",)),
    )(h1, g1, w_in)


def _prev8(width):
    return pl.BlockSpec((8, width), lambda i: (jnp.maximum(i * (MX // 8) - 1, 0), 0))


def _conv_out_loss(bg, cg, u, cw, qm, kv1, z, h1, w_out, fg, tgt):
    def body(bg_ref, cg_ref, u_ref, cgp_ref, up_ref, cw_ref, qm_ref, kv_ref, z_ref, h_ref, w_ref, fg_ref, t_ref,
             dh_ref, loss_ref, dfg_ref, ybuf):
        i = pl.program_id(0)
        a, a1, a2 = _conv_taps(cg_ref[...], u_ref[...], cgp_ref[...], up_ref[...], i == 0)
        conv = cw_ref[0:1, :] * a2 + cw_ref[1:2, :] * a1 + cw_ref[2:3, :] * a
        sz, _ = _silu_parts(z_ref[...])
        ybuf[:, :D] = (bg_ref[...] * conv * sz[:, :D]).astype(BF16)
        _, mo, _ = _mem_attn(qm_ref[...], kv_ref[...])
        ybuf[:, D:] = (mo * sz[:, D:]).astype(BF16)
        h2 = h_ref[...] + _dot(ybuf[...], w_ref[...])
        rstd = lax.rsqrt(jnp.mean(h2 * h2, axis=-1, keepdims=True) + EPS)
        n = h2 * rstd
        fgv = fg_ref[...]
        err = n * fgv - t_ref[...]
        dout = err * (1.0 / D)
        dn = dout * fgv
        dh_ref[...] = rstd * (dn - n * jnp.mean(dn * n, axis=-1, keepdims=True))

        @pl.when(i == 0)
        def _():
            loss_ref[...] = jnp.zeros_like(loss_ref)
            dfg_ref[...] = jnp.zeros_like(dfg_ref)

        loss_ref[...] += jnp.sum(err * err) * (0.5 / D)
        dfg_ref[...] += jnp.sum(dout * n, axis=0, keepdims=True)

    return pl.pallas_call(
        body, name="conv_out_loss", grid=(NX,),
        in_specs=[_rows(D, MX), _rows(D, MX), _rows(D, MX), _prev8(D), _prev8(D), _full((8, D)), _rows(MW, MX),
                  _full((NM, 2 * MW)), _rows(BR_B, MX), _rows(D, MX), _full((BR_B, D)), _full((1, D)), _rows(D, MX)],
        out_specs=[_rows(D, MX), _full((1, 128)), _full((1, D))],
        out_shape=[_sds((S, D), F32), _sds((1, 128), F32), _sds((1, D), F32)],
        scratch_shapes=[pltpu.VMEM((MX, BR_B), BF16)],
        compiler_params=_params(("arbitrary",)),
    )(bg, cg, u, cg, u, cw, qm, kv1, z, h1, w_out, fg, tgt)


def _conv_bwd(dh2, bg, cg, u, cw, qm, kv1, z, w_out):
    rev = lambda i: (NX - 1 - i, 0)
    rows = lambda w: pl.BlockSpec((MX, w), rev)
    prev8 = pl.BlockSpec((8, D), lambda i: (jnp.maximum((NX - 1 - i) * (MX // 8) - 1, 0), 0))

    def body(dh_ref, bg_ref, cg_ref, u_ref, cgp_ref, up_ref, cw_ref, qm_ref, kv_ref, z_ref, w_ref,
             dproj_ref, dw_ref, dcw_ref, dkv_ref, dwb_ref, ybuf, carry):
        i = pl.program_id(0)

        @pl.when(i == 0)
        def _():
            dw_ref[...] = jnp.zeros_like(dw_ref)
            dcw_ref[...] = jnp.zeros_like(dcw_ref)
            dkv_ref[...] = jnp.zeros_like(dkv_ref)
            carry[...] = jnp.zeros_like(carry)

        dhb = dh_ref[...].astype(BF16)
        dy = _dot_nt(dhb, w_ref[...])
        kvv = kv_ref[...]
        p, mo, q4 = _mem_attn(qm_ref[...], kvv)
        szm, dszm = _silu_parts(z_ref[:, D:])
        ybuf[:, D:] = (mo * szm).astype(BF16)
        dym = dy[:, D:]
        dproj_ref[:, 3 * D + MW + D:] = (dym * mo * dszm).astype(BF16)
        first_tile = i == NX - 1
        for c in range(D // CONV_CHUNK):
            cs = slice(c * CONV_CHUNK, (c + 1) * CONV_CHUNK)
            bgv, cgv, uv = bg_ref[:, cs], cg_ref[:, cs], u_ref[:, cs]
            a, a1, a2 = _conv_taps(cgv, uv, cgp_ref[:, cs], up_ref[:, cs], first_tile)
            w0, w1, w2 = cw_ref[0:1, cs], cw_ref[1:2, cs], cw_ref[2:3, cs]
            conv = w0 * a2 + w1 * a1 + w2 * a
            mix = bgv * conv
            sz, dsz = _silu_parts(z_ref[:, cs])
            ybuf[:, cs] = (mix * sz).astype(BF16)
            dyc = dy[:, cs]
            dproj_ref[:, 3 * D + MW + c * CONV_CHUNK:3 * D + MW + (c + 1) * CONV_CHUNK] = (
                dyc * mix * dsz).astype(BF16)
            dmix = dyc * sz
            dproj_ref[:, cs] = (dmix * conv).astype(BF16)
            dc = dmix * bgv
            nxt = carry[:, cs]
            row = lax.broadcasted_iota(jnp.int32, dc.shape, 0)
            dc1 = jnp.where(row == MX - 1, nxt[0:1, :], pltpu.roll(dc, MX - 1, 0))
            dc2 = jnp.where(row == MX - 2, nxt[0:1, :],
                            jnp.where(row == MX - 1, nxt[1:2, :], pltpu.roll(dc, MX - 2, 0)))
            carry[:, cs] = dc[0:8, :]
            da = w2 * dc + w1 * dc1 + w0 * dc2
            dproj_ref[:, D + c * CONV_CHUNK:D + (c + 1) * CONV_CHUNK] = (da * uv).astype(BF16)
            dproj_ref[:, 2 * D + c * CONV_CHUNK:2 * D + (c + 1) * CONV_CHUNK] = (da * cgv).astype(BF16)
            dcw_ref[0:1, cs] += jnp.sum(dc * a2, axis=0, keepdims=True)
            dcw_ref[1:2, cs] += jnp.sum(dc * a1, axis=0, keepdims=True)
            dcw_ref[2:3, cs] += jnp.sum(dc * a, axis=0, keepdims=True)
        dw_ref[...] += _dot_tn(ybuf[...], dhb)
        dproj_ref[:, 3 * D:3 * D + MW] = _mem_attn_bwd(dym * szm, p, mo, q4, kvv, dkv_ref).astype(BF16)

        @pl.when(i == NX - 1)
        def _():
            dwb_ref[...] = dw_ref[...].astype(BF16)

    return pl.pallas_call(
        body, name="conv_bwd", grid=(NX,),
        in_specs=[rows(D), rows(D), rows(D), rows(D), prev8, prev8, _full((8, D)), rows(MW),
                  _full((NM, 2 * MW)), rows(BR_B), _full((BR_B, D))],
        out_specs=[rows(IN_B), _full((BR_B, D)), _full((8, D)), _full((NM, 2 * MW)), _full((BR_B, D))],
        out_shape=[_sds((S, IN_B), BF16), _sds((BR_B, D), F32), _sds((8, D), F32), _sds((NM, 2 * MW), F32),
                   _sds((BR_B, D), BF16)],
        scratch_shapes=[pltpu.VMEM((MX, BR_B), BF16), pltpu.VMEM((8, D), F32)],
        compiler_params=_params(("arbitrary",)),
    )(dh2, bg, cg, u, cg, u, cw, qm, kv1, z, w_out)


def _in_proj_bwd(dproj, w_in, xin, g, dres, after, width, name):
    sh = width // 4

    def body(dp_ref, w_ref, x_ref, g_ref, dr_ref, dx_ref, dg_ref):
        i = pl.program_id(0)
        dhn = _dot_nt(dp_ref[:, 0:sh], w_ref[0])
        for s in range(1, 4):
            dhn += _dot_nt(dp_ref[:, s * sh:(s + 1) * sh], w_ref[s])
        xf = x_ref[...]
        rstd = lax.rsqrt(jnp.mean(xf * xf, axis=-1, keepdims=True) + EPS)
        n = xf * rstd
        dn = dhn * g_ref[...]
        dx_ref[...] = dr_ref[...] + rstd * (dn - n * jnp.mean(dn * n, axis=-1, keepdims=True))

        @pl.when(i == 0)
        def _():
            dg_ref[...] = jnp.zeros_like(dg_ref)

        dg_ref[...] += jnp.sum(dhn * n, axis=0, keepdims=True)

    return pl.pallas_call(
        functools.partial(_skip_arg, body, 5), name=name, grid=(NT,),
        in_specs=[_rows(width), _full((4, D, sh)), _rows(D), _full((1, D)), _rows(D), pl.BlockSpec(memory_space=pl.ANY)],
        out_specs=[_rows(D), _full((1, D))],
        out_shape=[_sds((S, D), F32), _sds((1, D), F32)],
        compiler_params=_params(("arbitrary",)),
    )(dproj, w_in, xin, g, dres, after)


def _w_in_grad(hn, dproj, width, name):
    sh = width // 4

    def body(hn_ref, dp_ref, dw_ref, dwb_ref):
        dw = _dot_tn(hn_ref[...], dp_ref[...])
        dw_ref[0] = dw
        dwb_ref[0] = dw.astype(BF16)

    spec = pl.BlockSpec((1, D, sh), lambda s: (s, 0, 0))
    return pl.pallas_call(
        body, name=name, grid=(4,),
        in_specs=[_full((S, D)), pl.BlockSpec((S, sh), lambda s: (0, s))],
        out_specs=[spec, spec], out_shape=[_sds((4, D, sh), F32), _sds((4, D, sh), BF16)],
        compiler_params=_params(("parallel",)),
    )(hn, dproj)


def _attn_out_bwd(dh1, os_, ls, qm, kv0, z, w_out, after):
    ones_bd = np.kron(np.eye(GW // HD, dtype=np.float32), np.ones((HD, HD), np.float32))

    def body(dh_ref, o0, o1, o2, l0, l1, l2, qm_ref, kv_ref, z_ref, w_ref, bd_ref,
             do0, do1, do2, dd0, dd1, dd2, dqm_ref, dz_ref, dw_ref, dkv_ref, dwb_ref, ybuf):
        i = pl.program_id(0)

        @pl.when(i == 0)
        def _():
            dw_ref[...] = jnp.zeros_like(dw_ref)
            dkv_ref[...] = jnp.zeros_like(dkv_ref)

        ws, mix = _merge((o0, o1, o2), (l0, l1, l2))
        sz, dsz = _silu_parts(z_ref[...])
        kvv = kv_ref[...]
        p, mo, q4 = _mem_attn(qm_ref[...], kvv)
        ybuf[:, :GW] = (mix * sz[:, :GW]).astype(BF16)
        ybuf[:, GW:] = (mo * sz[:, GW:]).astype(BF16)
        yb = ybuf[...]
        dh = dh_ref[...]
        dy = None
        for s in range(4):
            dhb = dh[:, s * SH_O:(s + 1) * SH_O].astype(BF16)
            dw_ref[s] += _dot_tn(yb, dhb)
            part = _dot_nt(dhb, w_ref[s])
            dy = part if dy is None else dy + part
        dcat = dy * sz
        dz_ref[:, :GW] = (dy[:, :GW] * mix * dsz[:, :GW]).astype(BF16)
        dz_ref[:, GW:] = (dy[:, GW:] * mo * dsz[:, GW:]).astype(BF16)
        dmix = dcat[:, :GW]
        prod = dmix * mix
        hi = prod.astype(BF16)
        lo = (prod - hi.astype(F32)).astype(BF16)
        bd = bd_ref[...]
        tot = _dot(hi, bd) + _dot(lo, bd)
        for w, do_ref, dd_ref in zip(ws, (do0, do1, do2), (dd0, dd1, dd2)):
            do_ref[...] = (w * dmix).astype(BF16)
            dd_ref[...] = w * tot

        dqm_ref[...] = _mem_attn_bwd(dcat[:, GW:], p, mo, q4, kvv, dkv_ref).astype(BF16)

        @pl.when(i == NX - 1)
        def _():
            dwb_ref[...] = dw_ref[...].astype(BF16)

    return pl.pallas_call(
        functools.partial(_skip_arg, body, 12), name="attn_out_bwd", grid=(NX,),
        in_specs=[_rows(D, MX)] + [_rows(GW, MX)] * 6 + [_rows(MW, MX), _full((NM, 2 * MW)), _rows(BR_A, MX),
                                                           _full((4, BR_A, SH_O)), _full((GW, GW)),
                                                           pl.BlockSpec(memory_space=pl.ANY)],
        out_specs=[_rows(GW, MX)] * 6 + [_rows(MW, MX), _rows(BR_A, MX), _full((4, BR_A, SH_O)),
                                         _full((NM, 2 * MW)), _full((4, BR_A, SH_O))],
        out_shape=[_sds((S, GW), BF16)] * 3 + [_sds((S, GW), F32)] * 3 + [
            _sds((S, MW), BF16), _sds((S, BR_A), BF16), _sds((4, BR_A, SH_O), F32), _sds((NM, 2 * MW), F32),
            _sds((4, BR_A, SH_O), BF16)],
        scratch_shapes=[pltpu.VMEM((MX, BR_A), BF16)],
        compiler_params=_params(("arbitrary",)),
    )(dh1, *os_, *ls, qm, kv0, z, w_out, jnp.asarray(ones_bd, dtype=BF16), after)


def _attn_bwd(q, k, v, do, lse_s, dd, g):
    d = DILATIONS[g]
    nb = S // d // QBLK
    perm = _perm_matrix(d)

    def body(q_ref, k_ref, v_ref, do_ref, l_ref, dd_ref, p_ref, pt_ref, dq_ref, dk_ref, dv_ref,
             q0, q1, g0, g1, ks, vs, dds, dqs, dks, dvs):
        first, second = _head_masks()
        pm = p_ref[...]
        for t in range(NT):
            rows = slice(t * TM, (t + 1) * TM)
            if d == 1:
                qt = q_ref[rows, :].astype(F32)
                gt = do_ref[rows, :].astype(F32)
            else:
                qt, gt = _pair_dot(pm, q_ref[rows, :], do_ref[rows, :])
                kt, vt = _pair_dot(pm, k_ref[rows, :], v_ref[rows, :])
                _tile_to_streams(kt, ks, t, d)
                _tile_to_streams(vt, vs, t, d)
                _tile_to_streams(_split_dot(pm, dd_ref[rows, :]), dds, t, d)
            _tile_to_streams(jnp.where(first, qt, 0.0), q0, t, d)
            _tile_to_streams(jnp.where(second, qt, 0.0), q1, t, d)
            _tile_to_streams(jnp.where(first, gt, 0.0), g0, t, d)
            _tile_to_streams(jnp.where(second, gt, 0.0), g1, t, d)
        kref, vref, ddref = (k_ref, v_ref, dd_ref) if d == 1 else (ks, vs, dds)
        dqref, dkref, dvref = dqs, dks, dvs
        dkref[...] = jnp.zeros_like(dkref)
        dvref[...] = jnp.zeros_like(dvref)

        def blk(b, carry):
            r0 = pl.multiple_of(b * QBLK, QBLK)
            p0 = pl.multiple_of(jnp.maximum(b - 1, 0) * QBLK, QBLK)
            kk = jnp.concatenate([kref[pl.ds(p0, QBLK), :], kref[pl.ds(r0, QBLK), :]], axis=0)
            vv = jnp.concatenate([vref[pl.ds(p0, QBLK), :], vref[pl.ds(r0, QBLK), :]], axis=0)
            lb = l_ref[pl.ds(r0, QBLK), :]
            ddb = ddref[pl.ds(r0, QBLK), :]
            lcol = jnp.concatenate([lb[:, 0:1], lb[:, HD:HD + 1]], axis=0)
            dcol = jnp.concatenate([ddb[:, 0:1], ddb[:, HD:HD + 1]], axis=0)
            valid = _band_mask(b & (nb - 1))
            valid2 = jnp.concatenate([valid, valid], axis=0)
            qq = jnp.concatenate([q0[pl.ds(r0, QBLK), :], q1[pl.ds(r0, QBLK), :]], axis=0)
            gg = jnp.concatenate([g0[pl.ds(r0, QBLK), :], g1[pl.ds(r0, QBLK), :]], axis=0)
            p = jnp.where(valid2, jnp.exp(_dot_nt(qq, kk) - lcol), 0.0)
            ds = (p * (_dot_nt(gg, vv) - dcol)).astype(BF16)
            dq2 = _dot(ds, kk)
            dqref[pl.ds(r0, QBLK), :] = jnp.where(first[:QBLK], dq2[:QBLK], dq2[QBLK:])
            dkk = _dot_tn(ds, qq)
            dvv = _dot_tn(p.astype(BF16), gg)
            dkref[pl.ds(p0, QBLK), :] += dkk[:QBLK]
            dkref[pl.ds(r0, QBLK), :] += dkk[QBLK:]
            dvref[pl.ds(p0, QBLK), :] += dvv[:QBLK]
            dvref[pl.ds(r0, QBLK), :] += dvv[QBLK:]
            return carry

        lax.fori_loop(0, S // QBLK, blk, 0, unroll=BWD_UNROLL)

        ptm = pt_ref[...] if d > 1 else None
        for t in range(NT):
            rows = slice(t * TM, (t + 1) * TM)
            if d == 1:
                dq_ref[rows, :] = dqs[rows, :].astype(BF16)
                dk_ref[rows, :] = dks[rows, :].astype(BF16)
                dv_ref[rows, :] = dvs[rows, :].astype(BF16)
            else:
                tq, tk = _pair_dot(ptm, _tile_from_streams(dqs, t, d).astype(BF16),
                                   _tile_from_streams(dks, t, d).astype(BF16))
                dq_ref[rows, :] = tq.astype(BF16)
                dk_ref[rows, :] = tk.astype(BF16)
                if t % 2 == 0:
                    ta, tb = _pair_dot(ptm, _tile_from_streams(dvs, t, d).astype(BF16),
                                       _tile_from_streams(dvs, t + 1, d).astype(BF16))
                    dv_ref[rows, :] = ta.astype(BF16)
                    dv_ref[(t + 1) * TM:(t + 2) * TM, :] = tb.astype(BF16)

    qkv_spec = pl.BlockSpec((S, LANES), lambda c: (0, g * NCHUNK + c))
    one_spec = pl.BlockSpec((S, LANES), lambda c: (0, c))
    return pl.pallas_call(
        body, name=f"attn_bwd_g{g}", grid=(NCHUNK,),
        in_specs=[qkv_spec] * 3 + [one_spec] * 3 + [_full((TM, TM))] * 2, out_specs=[one_spec] * 3,
        out_shape=[_sds((S, GW), BF16)] * 3,
        scratch_shapes=[pltpu.VMEM((S, LANES), BF16)] * 6 + [pltpu.VMEM((S, LANES), F32)] * 4,
        compiler_params=_params(("parallel",)),
    )(q, k, v, do, lse_s, dd, jnp.asarray(perm, BF16), jnp.asarray(perm.T, BF16))


def _qkv_bwd(dqs, dks, dvs, dqm, dz, c, s1, s2):
    def body(q0, q1, q2, k0, k1, k2, v0, v1, v2, dqm_ref, dz_ref, c_ref, s1_ref, s2_ref, dp_ref):
        cc, a1, a2 = c_ref[...], s1_ref[...], s2_ref[...]
        for g, (qr, kr, vr) in enumerate(((q0, k0, v0), (q1, k1, v1), (q2, k2, v2))):
            for j in range(GW // 128):
                ls_ = slice(j * 128, (j + 1) * 128)
                c0 = g * GW + j * 128
                dp_ref[:, c0:c0 + 128] = (_rope_bwd(qr[:, ls_].astype(F32), cc, a1, a2) * SCALE).astype(BF16)
                dp_ref[:, NQ + c0:NQ + c0 + 128] = _rope_bwd(kr[:, ls_].astype(F32), cc, a1, a2).astype(BF16)
            dp_ref[:, 2 * NQ + g * GW:2 * NQ + (g + 1) * GW] = vr[...]
        dp_ref[:, 3 * NQ:3 * NQ + MW] = dqm_ref[...]
        dp_ref[:, 3 * NQ + MW:] = dz_ref[...]

    return pl.pallas_call(
        body, name="qkv_bwd", grid=(NT,),
        in_specs=[_rows(GW)] * 9 + [_rows(MW), _rows(BR_A), _rows(128), _rows(128), _rows(128)],
        out_specs=_rows(IN_A), out_shape=_sds((S, IN_A), BF16),
        compiler_params=_params(("parallel",)),
    )(*dqs, *dks, *dvs, dqm, dz, c, s1, s2)


def _mem_bwd(mem, mg, memn, wkv, dkv0, dkv1):
    def body(mem_ref, mg_ref, memn_ref, w_ref, d0_ref, d1_ref, dw_ref, dwb_ref, dg_ref):
        mf = mem_ref[...]
        n = mf * lax.rsqrt(jnp.mean(mf * mf, axis=-1, keepdims=True) + EPS)
        for i, d_ref in enumerate((d0_ref, d1_ref)):
            dkv = d_ref[...].astype(BF16)
            mn = memn_ref[i]
            for s in range(4):
                cs = slice(s * NM, (s + 1) * NM)
                dw = _dot_tn(mn[:, cs], dkv)
                dw_ref[s, i] = dw
                dwb_ref[s, i] = dw.astype(BF16)
                dmn = _dot_nt(dkv, w_ref[s, i])
                dg_ref[i:i + 1, cs] = jnp.sum(dmn * n[:, cs], axis=0, keepdims=True)

    return pl.pallas_call(
        body, name="mem_bwd", grid=(1,),
        in_specs=[_full((NM, D)), _full((2, D)), _full((2, NM, D)), _full((4, 2, NM, 2 * MW)),
                  _full((NM, 2 * MW)), _full((NM, 2 * MW))],
        out_specs=[_full((4, 2, NM, 2 * MW)), _full((4, 2, NM, 2 * MW)), _full((2, D))],
        out_shape=[_sds((4, 2, NM, 2 * MW), F32), _sds((4, 2, NM, 2 * MW), BF16), _sds((2, D), F32)],
        compiler_params=_params(("arbitrary",)),
    )(mem, mg, memn, wkv, dkv0, dkv1)


MESH = pl.DeviceIdType.MESH
ANY = pl.BlockSpec(memory_space=pl.ANY)
BIG = (("wkv", 2, NM, 2 * MW), ("w_in_a", 1, D, SH_A), ("w_out_a", 1, BR_A, SH_O),
       ("w_in_b", 1, D, SH_B), ("w_out_b", 1, BR_B // 4, D))
NBIG = len(BIG)
CW_ROWS = 8


def _place():
    x, y, c = lax.axis_index("x"), lax.axis_index("y"), lax.axis_index("c")
    chips = ((1 - x, y), (x, 1 - y), (1 - x, 1 - y))
    return x, y, c, chips


def _remote(src, dst, ssem, rsem, dev):
    return pltpu.make_async_remote_copy(src_ref=src, dst_ref=dst, send_sem=ssem, recv_sem=rsem,
                                        device_id=dev, device_id_type=MESH)


def _cast_weights(place, ws, after, idx, name):
    nblk = 4
    n = len(idx)
    dims = [BIG[w][1:] for w in idx]

    def body(pref, *refs):
        for i in range(n):
            refs[n + 1 + i][0] = refs[i][...].astype(BF16)

    grid_spec = pltpu.PrefetchScalarGridSpec(
        num_scalar_prefetch=1, grid=(nblk,),
        in_specs=[pl.BlockSpec((k, r // nblk, cdim), lambda i, pref: (0, i, 0)) for k, r, cdim in dims]
        + [pl.BlockSpec(memory_space=pl.ANY)],
        out_specs=[pl.BlockSpec((1, k, r // nblk, cdim), lambda i, pref: (pref[1], 0, i, 0)) for k, r, cdim in dims])
    return pl.pallas_call(
        body, name=name, grid_spec=grid_spec,
        out_shape=[_sds((4, k, r, cdim), BF16) for k, r, cdim in dims],
        compiler_params=_params(("parallel",)),
    )(place, *ws, after)


LAYER_A = (0, 1, 2)
LAYER_B = (3, 4)
HBM = pl.BlockSpec(memory_space=pltpu.HBM)
SEM = pl.BlockSpec(memory_space=pltpu.SEMAPHORE)
EFFECT = pltpu.SideEffectType.DATAFLOW_SIDE_EFFECTING
TOKEN = (8, 128)


def _half(ref, w, which):
    h = BIG[w][2] // 2
    return ref.at[:, pl.ds(which * h, h), :]


def _skip_arg(body, pos, *refs):
    return body(*refs[:pos], *refs[pos + 1:])


def _gather_start(wb, after, idx, name):
    n = len(idx)

    def body(*refs):
        src = refs[:n]
        send_sems, recv_sems = refs[n + 1], refs[n + 2]
        token = refs[2 * n + 3]
        x, y, c, chips = _place()
        me = 2 * x + y
        for j, (px, py) in enumerate(chips):
            for i in range(n):
                mine = _half(src[i].at[me], idx[i], c)
                _remote(mine, mine, send_sems.at[j * n + i], recv_sems.at[j * n + i], (px, py, c)).start()
        token[...] = jnp.zeros(TOKEN, F32)

    outs = pl.pallas_call(
        body, name=name, in_specs=[HBM] * n + [ANY],
        out_specs=(SEM, SEM) + (HBM,) * n + (pl.BlockSpec(memory_space=pltpu.VMEM),),
        out_shape=(pltpu.SemaphoreType.DMA((3 * n,)), pltpu.SemaphoreType.DMA((3 * n,)))
        + tuple(pltpu.HBM(w.shape, w.dtype) for w in wb) + (_sds(TOKEN, F32),),
        input_output_aliases={i: 2 + i for i in range(n)},
        compiler_params=pltpu.CompilerParams(has_side_effects=EFFECT),
    )(*[pltpu.with_memory_space_constraint(w, pltpu.HBM) for w in wb], after)
    return outs[0], outs[1], list(outs[2:2 + n]), outs[2 + n]


def _gather_wait(send_sems, recv_sems, wb, after, idx, name, started=None):
    n = len(idx)
    started = idx if started is None else started
    n_all = len(started)
    pos = [started.index(w) for w in idx]

    def body(*refs):
        buf = refs[:n]
        send_sems, recv_sems = refs[n], refs[n + 1]
        x, y, c, chips = _place()
        me = 2 * x + y
        for j, (px, py) in enumerate(chips):
            for i in range(n):
                mine = _half(buf[i].at[me], idx[i], c)
                got = _half(buf[i].at[2 * px + py], idx[i], c)
                k = j * n_all + pos[i]
                _remote(mine, mine, send_sems.at[k], recv_sems.at[k], (px, py, c)).wait_send()
                _remote(got, got, send_sems.at[k], recv_sems.at[k], (px, py, c)).wait_recv()

    outs = pl.pallas_call(
        body, name=name, in_specs=[HBM] * n + [SEM, SEM] + [ANY] * len(after), out_specs=(HBM,) * n,
        out_shape=tuple(pltpu.HBM(w.shape, w.dtype) for w in wb),
        input_output_aliases={i: i for i in range(n)},
        compiler_params=pltpu.CompilerParams(has_side_effects=EFFECT),
    )(*wb, send_sems, recv_sems, *after)
    return list(outs)


def _gather_forward(wb, idx, name, barrier_id):
    n = len(idx)

    def body(*refs):
        dst = refs[n:2 * n]
        send_sems, recv_sems = refs[2 * n], refs[2 * n + 1]
        x, y, c, chips = _place()
        _sibling_barrier(x, y, c)
        cps = []
        for j, (px, py) in enumerate(chips):
            for i in range(n):
                got = _half(dst[i].at[2 * px + py], idx[i], c)
                cps.append(_remote(got, got, send_sems.at[j, i], recv_sems.at[j, i], (x, y, 1 - c)))
                cps[-1].start()
        for j, (px, py) in enumerate(chips):
            for i in range(n):
                got = _half(dst[i].at[2 * px + py], idx[i], 1 - c)
                _remote(got, got, send_sems.at[j, i], recv_sems.at[j, i], (x, y, 1 - c)).wait_recv()
        for cp in cps:
            cp.wait_send()

    return pl.pallas_call(
        body, name=name, in_specs=[ANY] * n, out_specs=[ANY] * n, out_shape=[_sds(w.shape, BF16) for w in wb],
        input_output_aliases={i: i for i in range(n)},
        scratch_shapes=[pltpu.SemaphoreType.DMA((3, n)), pltpu.SemaphoreType.DMA((3, n))],
        compiler_params=pltpu.CompilerParams(collective_id=barrier_id),
    )(*wb)


def _forward_start(wb, cw, after, idx, name):
    n = len(idx)
    m = n if cw is None else n + 2

    def body(*refs):
        buf = refs[:n]
        send_sems, recv_sems = refs[m + 1], refs[m + 2]
        token = refs[2 * m + 3]
        x, y, c, chips = _place()
        for j, (px, py) in enumerate(chips):
            for i in range(n):
                got = _half(buf[i].at[2 * px + py], idx[i], c)
                _remote(got, got, send_sems.at[j * (n + 1) + i], recv_sems.at[j * (n + 1) + i], (x, y, 1 - c)).start()
            if cw is not None:
                _remote(refs[n], refs[n + 1].at[2 * x + y], send_sems.at[j * (n + 1) + n],
                        recv_sems.at[j * (n + 1) + n], (px, py, c)).start()
        token[...] = jnp.zeros(TOKEN, F32)

    arrays = list(wb) if cw is None else list(wb) + [cw, lax.empty((4, CW_ROWS, SH_O), F32)]
    outs = pl.pallas_call(
        body, name=name, in_specs=[HBM] * m + [ANY],
        out_specs=(SEM, SEM) + (HBM,) * m + (pl.BlockSpec(memory_space=pltpu.VMEM),),
        out_shape=(pltpu.SemaphoreType.DMA((3 * (n + 1),)), pltpu.SemaphoreType.DMA((3 * (n + 1),)))
        + tuple(pltpu.HBM(a.shape, a.dtype) for a in arrays) + (_sds(TOKEN, F32),),
        input_output_aliases={i: 2 + i for i in range(m)},
        compiler_params=pltpu.CompilerParams(has_side_effects=EFFECT),
    )(*[pltpu.with_memory_space_constraint(a, pltpu.HBM) for a in arrays], after)
    return outs[0], outs[1], list(outs[2:2 + m]), outs[2 + m]


def _forward_wait(send_sems, recv_sems, arrays, after, idx, with_cw, name):
    n = len(idx)
    m = len(arrays)

    def body(*refs):
        buf = refs[:n]
        send_sems, recv_sems = refs[m], refs[m + 1]
        x, y, c, chips = _place()
        for j, (px, py) in enumerate(chips):
            for i in range(n):
                sent = _half(buf[i].at[2 * px + py], idx[i], c)
                got = _half(buf[i].at[2 * px + py], idx[i], 1 - c)
                k = j * (n + 1) + i
                _remote(sent, sent, send_sems.at[k], recv_sems.at[k], (x, y, 1 - c)).wait_send()
                _remote(got, got, send_sems.at[k], recv_sems.at[k], (x, y, 1 - c)).wait_recv()
            if with_cw:
                k = j * (n + 1) + n
                theirs = refs[n + 1].at[2 * px + py]
                _remote(refs[n], theirs, send_sems.at[k], recv_sems.at[k], (px, py, c)).wait_send()
                _remote(refs[n], theirs, send_sems.at[k], recv_sems.at[k], (px, py, c)).wait_recv()

    outs = pl.pallas_call(
        body, name=name, in_specs=[HBM] * m + [SEM, SEM] + [ANY] * len(after), out_specs=(HBM,) * m,
        out_shape=tuple(pltpu.HBM(a.shape, a.dtype) for a in arrays),
        input_output_aliases={i: i for i in range(m)},
        compiler_params=pltpu.CompilerParams(has_side_effects=EFFECT),
    )(*arrays, send_sems, recv_sems, *after)
    return list(outs)


def _sibling_barrier(x, y, c):
    barrier = pltpu.get_barrier_semaphore()
    pl.semaphore_signal(barrier, inc=1, device_id=(x, y, 1 - c), device_id_type=MESH)
    pl.semaphore_wait(barrier, 1)


def _pair_exchange(gs, idx, name, barrier_id):
    n = len(idx)

    def body(*refs):
        src, dst = refs[:n], refs[n:2 * n]
        send_sems, recv_sems = refs[2 * n:]
        x, y, c, _ = _place()
        _sibling_barrier(x, y, c)
        cps = []
        for i in range(n):
            h = BIG[idx[i]][2] // 2
            cps.append(_remote(src[i].at[:, :, pl.ds((1 - c) * h, h), :], dst[i], send_sems.at[i], recv_sems.at[i],
                               (x, y, 1 - c)))
            cps[-1].start()
        for cp in cps:
            cp.wait()

    return pl.pallas_call(
        body, name=name, in_specs=[ANY] * n, out_specs=[ANY] * n,
        out_shape=[_sds((4, BIG[w][1], BIG[w][2] // 2, BIG[w][3]), BF16) for w in idx],
        scratch_shapes=[pltpu.SemaphoreType.DMA((n,)), pltpu.SemaphoreType.DMA((n,))],
        compiler_params=pltpu.CompilerParams(collective_id=barrier_id),
    )(*gs)


def _pair_start(gs, idx, name):
    n = len(idx)

    def body(*refs):
        src, land = refs[:n], refs[n:2 * n]
        send_sems, recv_sems = refs[2 * n], refs[2 * n + 1]
        token = refs[4 * n + 2]
        x, y, c, _ = _place()
        for i in range(n):
            h = BIG[idx[i]][2] // 2
            _remote(src[i].at[:, :, pl.ds((1 - c) * h, h), :], land[i], send_sems.at[i], recv_sems.at[i],
                    (x, y, 1 - c)).start()
        token[...] = jnp.zeros(TOKEN, F32)

    lands = [lax.empty((4, BIG[w][1], BIG[w][2] // 2, BIG[w][3]), BF16) for w in idx]
    arrays = list(gs) + lands
    outs = pl.pallas_call(
        body, name=name, in_specs=[HBM] * (2 * n),
        out_specs=(SEM, SEM) + (HBM,) * (2 * n) + (pl.BlockSpec(memory_space=pltpu.VMEM),),
        out_shape=(pltpu.SemaphoreType.DMA((n,)), pltpu.SemaphoreType.DMA((n,)))
        + tuple(pltpu.HBM(a.shape, a.dtype) for a in arrays) + (_sds(TOKEN, F32),),
        input_output_aliases={i: 2 + i for i in range(2 * n)},
        compiler_params=pltpu.CompilerParams(has_side_effects=EFFECT),
    )(*[pltpu.with_memory_space_constraint(a, pltpu.HBM) for a in arrays])
    return outs[0], outs[1], list(outs[2:2 + n]), list(outs[2 + n:2 + 2 * n]), outs[2 + 2 * n]


def _pair_wait(send_sems, recv_sems, gs, lands, after, idx, name):
    n = len(idx)

    def body(*refs):
        src, land = refs[:n], refs[n:2 * n]
        send_sems, recv_sems = refs[2 * n], refs[2 * n + 1]
        x, y, c, _ = _place()
        for i in range(n):
            h = BIG[idx[i]][2] // 2
            cp = _remote(src[i].at[:, :, pl.ds((1 - c) * h, h), :], land[i], send_sems.at[i], recv_sems.at[i],
                         (x, y, 1 - c))
            cp.wait_send()
            cp.wait_recv()

    arrays = list(gs) + list(lands)
    outs = pl.pallas_call(
        body, name=name, in_specs=[HBM] * (2 * n) + [SEM, SEM] + [ANY] * len(after), out_specs=(HBM,) * (2 * n),
        out_shape=tuple(pltpu.HBM(a.shape, a.dtype) for a in arrays),
        input_output_aliases={i: i for i in range(2 * n)},
        compiler_params=pltpu.CompilerParams(has_side_effects=EFFECT),
    )(*arrays, send_sems, recv_sems, *after)
    return list(outs[:n]), list(outs[n:])


def _pair_sums(place, gs, r1s, idx, name):
    n = len(idx)
    dims = [(BIG[w][1], BIG[w][2] // 2, BIG[w][3]) for w in idx]

    def body(pref, *refs):
        for i in range(n):
            refs[2 * n + i][...] = (refs[i][...] + refs[n + i][...].astype(F32)).astype(BF16)

    mine = [pl.BlockSpec((1, k, h, cdim), lambda s, pref: (s, 0, pref[0], 0)) for k, h, cdim in dims]
    whole = [pl.BlockSpec((1, k, h, cdim), lambda s, pref: (s, 0, 0, 0)) for k, h, cdim in dims]
    grid_spec = pltpu.PrefetchScalarGridSpec(num_scalar_prefetch=1, grid=(4,), in_specs=mine + whole, out_specs=whole)
    return pl.pallas_call(
        body, name=name, grid_spec=grid_spec, out_shape=[_sds((4, k, h, cdim), BF16) for k, h, cdim in dims],
        compiler_params=_params(("parallel",)),
    )(place, *gs, *r1s)


def _chip_start(ps, idx, name):
    n = len(idx)

    def body(*refs):
        src, land = refs[:n], refs[n:2 * n]
        send_sems, recv_sems = refs[2 * n], refs[2 * n + 1]
        token = refs[4 * n + 2]
        x, y, c, chips = _place()
        for j, (px, py) in enumerate(chips):
            for i in range(n):
                _remote(src[i].at[2 * px + py], land[i].at[j], send_sems.at[j * n + i], recv_sems.at[j * n + i],
                        (px, py, c)).start()
        token[...] = jnp.zeros(TOKEN, F32)

    lands = [lax.empty((3,) + p.shape[1:], BF16) for p in ps]
    outs = pl.pallas_call(
        body, name=name, in_specs=[HBM] * (2 * n),
        out_specs=(SEM, SEM) + (HBM,) * (2 * n) + (pl.BlockSpec(memory_space=pltpu.VMEM),),
        out_shape=(pltpu.SemaphoreType.DMA((3 * n,)), pltpu.SemaphoreType.DMA((3 * n,)))
        + tuple(pltpu.HBM(a.shape, a.dtype) for a in list(ps) + lands) + (_sds(TOKEN, F32),),
        input_output_aliases={i: 2 + i for i in range(2 * n)},
        compiler_params=pltpu.CompilerParams(has_side_effects=EFFECT),
    )(*[pltpu.with_memory_space_constraint(a, pltpu.HBM) for a in list(ps) + lands])
    return outs[0], outs[1], list(outs[2:2 + n]), list(outs[2 + n:2 + 2 * n]), outs[2 + 2 * n]


def _chip_wait(send_sems, recv_sems, ps, lands, after, idx, name):
    n = len(idx)

    def body(*refs):
        src, land = refs[:n], refs[n:2 * n]
        send_sems, recv_sems = refs[2 * n], refs[2 * n + 1]
        x, y, c, chips = _place()
        for j, (px, py) in enumerate(chips):
            for i in range(n):
                cp = _remote(src[i].at[2 * px + py], land[i].at[j], send_sems.at[j * n + i], recv_sems.at[j * n + i],
                             (px, py, c))
                cp.wait_send()
                cp.wait_recv()

    arrays = list(ps) + list(lands)
    outs = pl.pallas_call(
        body, name=name, in_specs=[HBM] * (2 * n) + [SEM, SEM] + [ANY] * len(after), out_specs=(HBM,) * (2 * n),
        out_shape=tuple(pltpu.HBM(a.shape, a.dtype) for a in arrays),
        input_output_aliases={i: i for i in range(2 * n)},
        compiler_params=pltpu.CompilerParams(has_side_effects=EFFECT),
    )(*arrays, send_sems, recv_sems, *after)
    return list(outs[n:])


def _chip_sums(place, gs, r1s, r2s, idx, name):
    n = len(idx)
    dims = [(BIG[w][1], BIG[w][2] // 4, BIG[w][3]) for w in idx]

    def body(pref, *refs):
        for i in range(n):
            acc = refs[i][0] + refs[n + i][0].astype(F32)
            for j in range(3):
                acc = acc + refs[2 * n + i][j].astype(F32)
            refs[3 * n + i][...] = acc

    in_specs = ([pl.BlockSpec((1, k, q, cdim), lambda t, pref: (pref[1], 0, pref[0] * 2 + t, 0)) for k, q, cdim in dims]
                + [pl.BlockSpec((1, k, q, cdim), lambda t, pref: (pref[1], 0, t, 0)) for k, q, cdim in dims]
                + [pl.BlockSpec((3, k, q, cdim), lambda t, pref: (0, 0, t, 0)) for k, q, cdim in dims])
    out_specs = [pl.BlockSpec((k, q, cdim), lambda t, pref: (0, pref[0] * 2 + t, 0)) for k, q, cdim in dims]
    grid_spec = pltpu.PrefetchScalarGridSpec(num_scalar_prefetch=1, grid=(2,), in_specs=in_specs, out_specs=out_specs)
    return pl.pallas_call(
        body, name=name, grid_spec=grid_spec, out_shape=[_sds(BIG[w][1:], F32) for w in idx],
        compiler_params=_params(("parallel",)),
    )(place, *gs, *r1s, *r2s)


def _pair_gather(hs, idx, name, barrier_id):
    n = len(idx)

    def body(*refs):
        dst = refs[n:2 * n]
        send_sems, recv_sems = refs[2 * n:]
        x, y, c, _ = _place()
        _sibling_barrier(x, y, c)
        cps = []
        for i in range(n):
            mine = _half(dst[i], idx[i], c)
            cps.append(_remote(mine, mine, send_sems.at[i], recv_sems.at[i], (x, y, 1 - c)))
            cps[-1].start()
        for i in range(n):
            theirs = _half(dst[i], idx[i], 1 - c)
            _remote(theirs, theirs, send_sems.at[i], recv_sems.at[i], (x, y, 1 - c)).wait_recv()
        for cp in cps:
            cp.wait_send()

    return pl.pallas_call(
        body, name=name, in_specs=[ANY] * n, out_specs=[ANY] * n,
        out_shape=[_sds(BIG[w][1:], F32) for w in idx],
        input_output_aliases={i: i for i in range(n)},
        scratch_shapes=[pltpu.SemaphoreType.DMA((n,)), pltpu.SemaphoreType.DMA((n,))],
        compiler_params=pltpu.CompilerParams(collective_id=barrier_id),
    )(*hs)


SMALL_ROWS = 40


def _adamw_math(w, g, m, v):
    m = ADAM_B1 * m + (1.0 - ADAM_B1) * g
    v = ADAM_B2 * v + (1.0 - ADAM_B2) * (g * g)
    m_hat = m / (1.0 - ADAM_B1 ** ADAM_STEP)
    v_hat = v / (1.0 - ADAM_B2 ** ADAM_STEP)
    delta = -ADAM_LR * (m_hat / (jnp.sqrt(v_hat) + ADAM_EPS) + ADAM_WD * w)
    return delta, m, v


def _small_start(pack, after):
    def body(pack_ref, land_ref, after_ref, send_sems, recv_sems, pack_thru, land_thru, token):
        x, y, c, _ = _place()
        for r in range(1, 8):
            peer = (x if not r & 4 else 1 - x, y if not r & 2 else 1 - y, c if not r & 1 else 1 - c)
            _remote(pack_ref, land_ref.at[r - 1], send_sems.at[r - 1], recv_sems.at[r - 1], peer).start()
        token[...] = jnp.zeros(TOKEN, F32)

    land = lax.empty((7, SMALL_ROWS, D), F32)
    outs = pl.pallas_call(
        body, name="small_start", in_specs=[HBM, HBM, ANY],
        out_specs=(SEM, SEM, HBM, HBM, pl.BlockSpec(memory_space=pltpu.VMEM)),
        out_shape=(pltpu.SemaphoreType.DMA((7,)), pltpu.SemaphoreType.DMA((7,)), pltpu.HBM(pack.shape, F32),
                   pltpu.HBM(land.shape, F32), _sds(TOKEN, F32)),
        input_output_aliases={0: 2, 1: 3},
        compiler_params=pltpu.CompilerParams(has_side_effects=EFFECT),
    )(pltpu.with_memory_space_constraint(pack, pltpu.HBM), pltpu.with_memory_space_constraint(land, pltpu.HBM), after)
    return outs


def _small_wait(send_sems, recv_sems, pack, land, after):
    def body(pack_ref, land_ref, send_sems, recv_sems, *rest):
        x, y, c, _ = _place()
        for r in range(1, 8):
            peer = (x if not r & 4 else 1 - x, y if not r & 2 else 1 - y, c if not r & 1 else 1 - c)
            cp = _remote(pack_ref, land_ref.at[r - 1], send_sems.at[r - 1], recv_sems.at[r - 1], peer)
            cp.wait_send()
            cp.wait_recv()

    return pl.pallas_call(
        body, name="small_wait", in_specs=[HBM, HBM, SEM, SEM] + [ANY] * len(after), out_specs=(HBM, HBM),
        out_shape=(pltpu.HBM(pack.shape, F32), pltpu.HBM(land.shape, F32)),
        input_output_aliases={0: 0, 1: 1},
        compiler_params=pltpu.CompilerParams(has_side_effects=EFFECT),
    )(pack, land, send_sems, recv_sems, *after)


def _small_update(place, pack, land, ws, ms, vs):
    n = len(ws)

    def body(pref, pack_ref, land_ref, *refs):
        chip = pref[1]
        me = 2 * chip + pref[0]
        own = pack_ref[...]
        tot = None
        for dev in range(8):
            r = jnp.bitwise_xor(me, dev)
            term = jnp.where(r == 0, own, land_ref[jnp.maximum(r - 1, 0)])
            tot = term if tot is None else tot + term
        out, buf = refs[3 * n:-1], refs[-1]
        buf[...] = tot
        g_conv = jnp.zeros((3, SH_O), F32)
        for s in range(4):
            g_conv = g_conv + jnp.where(chip == s, buf[24:27, s * SH_O:(s + 1) * SH_O], 0.0)
        gs = [buf[0:2, :], buf[8:10, :], buf[16:17, :], g_conv]
        out[0][...] = buf[32:33, 0:128]
        for i in range(n):
            d, nm, nv = _adamw_math(refs[i][...], gs[i], refs[n + i][...], refs[2 * n + i][...])
            out[1 + i][...] = gs[i]
            out[1 + n + i][...] = d
            out[1 + 2 * n + i][...] = nm
            out[1 + 3 * n + i][...] = nv

    def full(shape):
        nd = len(shape)
        return pl.BlockSpec(shape, lambda i, pref: (0,) * nd)

    specs = [full(w.shape) for w in ws]
    grid_spec = pltpu.PrefetchScalarGridSpec(
        num_scalar_prefetch=1, grid=(1,),
        in_specs=[full(pack.shape), full(land.shape)] + specs * 3, out_specs=[full((1, 128))] + specs * 4,
        scratch_shapes=[pltpu.VMEM((SMALL_ROWS, D), F32)])
    outs = pl.pallas_call(
        body, name="small_update", grid_spec=grid_spec,
        out_shape=[_sds((1, 128), F32)] + [_sds(w.shape, F32) for w in ws] * 4,
        compiler_params=_params(("arbitrary",)),
    )(place, pack, land, *ws, *ms, *vs)
    return outs[0], outs[1:1 + n], outs[1 + n:1 + 2 * n], outs[1 + 2 * n:1 + 3 * n], outs[1 + 3 * n:]


def _adamw_layer(ws, gs, ms, vs, idx, name):
    n = len(idx)
    dims = [(BIG[w][1], BIG[w][2] // 4, BIG[w][3]) for w in idx]

    def body(*refs):
        for i in range(n):
            gv = refs[n + i][...]
            d, nm, nv = _adamw_math(refs[i][...], gv, refs[2 * n + i][...], refs[3 * n + i][...])
            refs[4 * n + i][...] = d
            refs[5 * n + i][...] = nm
            refs[6 * n + i][...] = nv
            refs[7 * n + i][...] = gv

    specs = [pl.BlockSpec((k, q, cdim), lambda t: (0, t, 0)) for k, q, cdim in dims]
    outs = pl.pallas_call(
        body, name=name, grid=(4,), in_specs=specs * 4, out_specs=specs * 4,
        out_shape=[_sds(BIG[w][1:], F32) for w in idx] * 4,
        compiler_params=_params(("parallel",)),
    )(*ws, *gs, *ms, *vs)
    return [tuple(outs[j * n + i] for j in range(4)) for i in range(n)]


def _pad_rows(a, rows):
    return jnp.pad(a, ((0, rows - a.shape[0]), (0, 0)))


def kernel(x, mem, positions, norm_g, mem_norm_g, w_mem_kv, attn_w_in, attn_w_out, conv_w_in, conv_w, conv_w_out, final_g, loss_target, m_norm_g, m_mem_norm_g, m_w_mem_kv, m_attn_w_in, m_attn_w_out, m_conv_w_in, m_conv_w, m_conv_w_out, m_final_g, v_norm_g, v_mem_norm_g, v_w_mem_kv, v_attn_w_in, v_attn_w_out, v_conv_w_in, v_conv_w, v_conv_w_out, v_final_g):
    mx, my, mc = lax.axis_index("x"), lax.axis_index("y"), lax.axis_index("c")
    place = jnp.stack([mc, 2 * mx + my]).astype(jnp.int32)

    w_big = [w_mem_kv, attn_w_in, attn_w_out, conv_w_in, conv_w_out]
    m_big = [m_w_mem_kv, m_attn_w_in, m_attn_w_out, m_conv_w_in, m_conv_w_out]
    v_big = [v_w_mem_kv, v_attn_w_in, v_attn_w_out, v_conv_w_in, v_conv_w_out]
    first, rest = (1,), (0, 2, 3, 4)
    wb1 = _cast_weights(place, [w_big[i] for i in first], place, first, "cast_w_in_a")
    a1_send, a1_recv, a1_bufs, a1_token = _gather_start(wb1, place, first, "gather_a1_start")
    wbr = _cast_weights(place, [w_big[i] for i in rest], a1_token, rest, "cast_weights")
    r_send, r_recv, r_bufs, gb_token = _gather_start(wbr, a1_token, rest, "gather_rest_start")
    a2_send, a2_recv, gb_send, gb_recv = r_send, r_recv, r_send, r_recv
    a2_bufs, gb_bufs = r_bufs[:2], r_bufs[2:]
    started, rest = rest, (0, 2)

    xs, tgt = x[0], loss_target[0]
    g0, g1 = norm_g[0:1], norm_g[1:2]
    rc, rs1, rs2 = _rope_tables(positions[0].astype(F32).reshape(S, 1), gb_token)
    a1_bufs = _gather_wait(a1_send, a1_recv, a1_bufs, [rc], first, "gather_a1_wait")
    w_in_a = _gather_forward(a1_bufs, first, "gather_a1_forward", 0)[0].reshape(4, D, SH_A)
    hn0, q, k, v, qm0, z0 = _in_proj_a(xs, g0, w_in_a, rc, rs1, rs2, gb_token)
    a2_bufs = _gather_wait(a2_send, a2_recv, a2_bufs, [q], rest, "gather_a2_wait", started)
    f2_send, f2_recv, a2_bufs, f2_token = _forward_start(a2_bufs, None, q, rest, "forward_a2_start")
    fwd = [_attn_fwd(q, k, v, 0, f2_token)]
    fwd.append(_attn_fwd(q, k, v, 1, fwd[0][0]))
    cw_own = _pad_rows(conv_w[0], CW_ROWS)
    gb_bufs = _gather_wait(gb_send, gb_recv, gb_bufs, [fwd[1][0]], LAYER_B, "gather_b_wait", started)
    fb_send, fb_recv, gb_bufs, fb_token = _forward_start(gb_bufs, cw_own, fwd[1][0], LAYER_B, "forward_b_start")
    fwd.append(_attn_fwd(q, k, v, 2, fb_token))
    os_, ls, lss = [f[0] for f in fwd], [f[1] for f in fwd], [f[2] for f in fwd]
    wkv_f, w_out_a = _forward_wait(f2_send, f2_recv, a2_bufs, [os_[2]], rest, False, "forward_a2_wait")
    w_out_a = w_out_a.reshape(4, BR_A, SH_O)
    memn, kv = _mem_fwd(mem[0], mem_norm_g, wkv_f)
    h1 = _attn_out(os_, ls, qm0, kv[0], z0, xs, w_out_a)

    w_in_b, w_out_b, _, cw_f = _forward_wait(fb_send, fb_recv, gb_bufs, [h1], LAYER_B, True, "forward_b_wait")
    w_in_b = w_in_b.reshape(4, D, SH_B)
    w_out_b = w_out_b.reshape(BR_B, D)
    cw_f = lax.dynamic_update_slice(cw_f, cw_own[None], (2 * mx + my, 0, 0))
    cw8 = cw_f.transpose(1, 0, 2).reshape(CW_ROWS, D)
    hn1, bg, cg, u, qm1, z1 = _in_proj_b(h1, g1, w_in_b)
    dh2, loss_part, dfg = _conv_out_loss(bg, cg, u, cw8, qm1, kv[1], z1, h1, w_out_b, final_g.reshape(1, D), tgt)

    dproj_b, dw_out_b, dcw, dkv1, dw_out_b16 = _conv_bwd(dh2, bg, cg, u, cw8, qm1, kv[1], z1, w_out_b)
    dw_in_b, dw_in_b16 = _w_in_grad(hn1, dproj_b, IN_B, "w_in_b_grad")
    gs_b = [dw_in_b.reshape(4, 1, D, SH_B), dw_out_b.reshape(4, 1, BR_B // 4, D)]
    gb_b = [dw_in_b16.reshape(4, 1, D, SH_B), dw_out_b16.reshape(4, 1, BR_B // 4, D)]
    pb_send, pb_recv, gb_b, pb_land, pb_token = _pair_start(gb_b, LAYER_B, "pair_b_start")
    dh1, dg1 = _in_proj_bwd(dproj_b, w_in_b, h1, g1, dh2, pb_token, IN_B, "in_proj_b_bwd")
    _, r1_b = _pair_wait(pb_send, pb_recv, gb_b, pb_land, [dh1], LAYER_B, "pair_b_wait")
    ps_b = _pair_sums(place, gs_b, r1_b, LAYER_B, "pair_sums_b")
    cb_send, cb_recv, cb_src, cb_land, cb_token = _chip_start(ps_b, LAYER_B, "chip_b_start")

    outs = _attn_out_bwd(dh1, os_, ls, qm0, kv[0], z0, w_out_a, cb_token)
    dos, dds, dqm, dz, dw_out_a, dkv0, dw_out_a16 = outs[0:3], outs[3:6], outs[6], outs[7], outs[8], outs[9], outs[10]
    bwd = [_attn_bwd(q, k, v, dos[g], lss[g], dds[g], g) for g in range(3)]
    dproj_a = _qkv_bwd([b[0] for b in bwd], [b[1] for b in bwd], [b[2] for b in bwd], dqm, dz, rc, rs1, rs2)
    dw_in_a, dw_in_a16 = _w_in_grad(hn0, dproj_a, IN_A, "w_in_a_grad")
    dwkv, dwkv16, dmg = _mem_bwd(mem[0], mem_norm_g, memn, wkv_f, dkv0, dkv1)

    gs_a = [dwkv, dw_in_a.reshape(4, 1, D, SH_A), dw_out_a.reshape(4, 1, BR_A, SH_O)]
    r1_a = _pair_exchange([dwkv16, dw_in_a16.reshape(4, 1, D, SH_A), dw_out_a16.reshape(4, 1, BR_A, SH_O)], LAYER_A,
                          "pair_exchange_a", 1)
    ps_a = _pair_sums(place, gs_a, r1_a, LAYER_A, "pair_sums_a")
    ca_send, ca_recv, ca_src, ca_land, ca_token = _chip_start(ps_a, LAYER_A, "chip_a_start")

    gx, dg0 = _in_proj_bwd(dproj_a, w_in_a, xs, g0, dh1, ca_token, IN_A, "in_proj_a_bwd")
    pack = jnp.concatenate([_pad_rows(jnp.concatenate([dg0, dg1], axis=0), 8), _pad_rows(dmg, 8), _pad_rows(dfg, 8),
                            dcw, _pad_rows(jnp.pad(loss_part, ((0, 0), (0, D - 128))), 8)], axis=0)
    sm_send, sm_recv, pack, sm_land, sm_token = _small_start(pack, ca_token)
    r2_b = _chip_wait(cb_send, cb_recv, cb_src, cb_land, [ca_token], LAYER_B, "chip_b_wait")
    hs_b = _chip_sums(place, gs_b, r1_b, r2_b, LAYER_B, "chip_sums_b")
    g_b = _pair_gather(hs_b, LAYER_B, "pair_gather_b", 2)
    upd_b = _adamw_layer([w_big[w] for w in LAYER_B], g_b, [m_big[w] for w in LAYER_B], [v_big[w] for w in LAYER_B],
                         LAYER_B, "adamw_b")
    r2_a = _chip_wait(ca_send, ca_recv, ca_src, ca_land, [gx, upd_b[0][0], upd_b[1][0], sm_token], LAYER_A,
                      "chip_a_wait")
    hs_a = _chip_sums(place, gs_a, r1_a, r2_a, LAYER_A, "chip_sums_a")
    g_a = _pair_gather(hs_a, LAYER_A, "pair_gather_a", 3)
    upd_a = _adamw_layer([w_big[w] for w in LAYER_A], g_a, [m_big[w] for w in LAYER_A], [v_big[w] for w in LAYER_A],
                         LAYER_A, "adamw_a")
    upd = upd_a + upd_b
    g_big = [u[3] for u in upd]
    pack, sm_land = _small_wait(sm_send, sm_recv, pack, sm_land, [r2_a[0]])
    sw = [norm_g, mem_norm_g, final_g.reshape(1, D), conv_w[0]]
    sm = [m_norm_g, m_mem_norm_g, m_final_g.reshape(1, D), m_conv_w[0]]
    sv = [v_norm_g, v_mem_norm_g, v_final_g.reshape(1, D), v_conv_w[0]]
    loss_row, sg, sd, snm, snv = _small_update(place, pack, sm_land, sw, sm, sv)
    loss = loss_row[0, 0]
    g_norm, g_memnorm, g_final, g_conv = sg

    def order(norm, memnorm, wkv, w_in_a, w_out_a, w_in_b, conv, w_out_b, final):
        return (norm, memnorm, wkv, w_in_a, w_out_a, w_in_b, conv.reshape(1, 3, SH_O), w_out_b, final.reshape(D))

    grads = order(g_norm, g_memnorm, g_big[0], g_big[1], g_big[2], g_big[3], g_conv, g_big[4], g_final)
    deltas = order(sd[0], sd[1], upd[0][0], upd[1][0], upd[2][0], upd[3][0], sd[3], upd[4][0], sd[2])
    new_m = order(snm[0], snm[1], upd[0][1], upd[1][1], upd[2][1], upd[3][1], snm[3], upd[4][1], snm[2])
    new_v = order(snv[0], snv[1], upd[0][2], upd[1][2], upd[2][2], upd[3][2], snv[3], upd[4][2], snv[2])
    return (loss, gx[None], *grads, *deltas, *new_m, *new_v)
```

```python
import functools

import numpy as np
import jax
import jax.numpy as jnp
from jax import lax
from jax.experimental import pallas as pl
from jax.experimental.pallas import tpu as pltpu

F32 = jnp.float32
BF16 = jnp.bfloat16

S = 2048
D = 1024
TM = 256
NT = S // TM
MX = 512
NX = S // MX
HD = 64
GW = 512
NQ = 3 * GW
MW = 256
NM = 256
IN_A = 3 * NQ + MW + GW + MW
IN_B = 3 * D + MW + D + MW
BR_A = GW + MW
BR_B = D + MW
SH_A = IN_A // 4
SH_B = IN_B // 4
SH_O = D // 4
QBLK = 128
DILATIONS = (1, 4, 16)
EPS = 1e-6
SCALE = HD ** -0.5
NEG = -1e30
ROPE_THETA = 500000.0

ADAM_LR = 0.001
ADAM_B1 = 0.9
ADAM_B2 = 0.999
ADAM_EPS = 1e-08
ADAM_WD = 0.01
ADAM_STEP = 10

VMEM_LIMIT_BYTES = 60 * 1024 * 1024


def _params(sem=None):
    if sem is None:
        return pltpu.CompilerParams(vmem_limit_bytes=VMEM_LIMIT_BYTES)
    return pltpu.CompilerParams(dimension_semantics=sem, vmem_limit_bytes=VMEM_LIMIT_BYTES)


def _full(shape):
    nd = len(shape)
    return pl.BlockSpec(shape, lambda *_: (0,) * nd)


def _rows(width, tm=TM):
    return pl.BlockSpec((tm, width), lambda i: (i, 0))


def _sds(shape, dtype):
    return jax.ShapeDtypeStruct(shape, dtype)


def _silu_parts(z):
    sig = 0.5 * jnp.tanh(0.5 * z) + 0.5
    return z * sig, sig * (1.0 + z * (1.0 - sig))


def _dot(a, b):
    return jnp.dot(a, b, preferred_element_type=F32)


def _dot_nt(a, b):
    return lax.dot_general(a, b, (((1,), (1,)), ((), ())), preferred_element_type=F32)


def _dot_tn(a, b):
    return lax.dot_general(a, b, (((0,), (0,)), ((), ())), preferred_element_type=F32)


def _rope_fwd(t, c, s1, s2):
    return t * c + pltpu.roll(t, 120, 1) * s1 + pltpu.roll(t, 8, 1) * s2


def _rope_bwd(g, c, s1, s2):
    return g * c + pltpu.roll(g * s1, 8, 1) + pltpu.roll(g * s2, 120, 1)


MEM_HEADS = MW // HD


def _stack_heads(x):
    head = lax.broadcasted_iota(jnp.int32, x.shape, 1) // HD
    return jnp.concatenate([jnp.where(head == h, x, 0.0) for h in range(MEM_HEADS)], axis=0).astype(BF16)


def _unstack_heads(x4):
    tm = x4.shape[0] // MEM_HEADS
    head = lax.broadcasted_iota(jnp.int32, (tm, MW), 1) // HD
    out = x4[:tm]
    for h in range(1, MEM_HEADS):
        out = jnp.where(head == h, x4[h * tm:(h + 1) * tm], out)
    return out


def _mem_attn(qm, kv):
    q4 = _stack_heads(qm.astype(F32))
    s = _dot_nt(q4, kv[:, :MW]) * SCALE
    e = jnp.exp(s - jnp.max(s, axis=-1, keepdims=True))
    p = e * (1.0 / jnp.sum(e, axis=-1, keepdims=True))
    return p, _unstack_heads(_dot(p.astype(BF16), kv[:, MW:])), q4


def _mem_attn_bwd(dmo, p, mo, q4, kv, dkv_ref):
    tm = dmo.shape[0]
    head = lax.broadcasted_iota(jnp.int32, dmo.shape, 1) // HD
    prod = dmo * mo
    delta = jnp.concatenate([jnp.sum(jnp.where(head == h, prod, 0.0), axis=-1, keepdims=True)
                             for h in range(MEM_HEADS)], axis=0)
    d4 = _stack_heads(dmo)
    ds = (p * (_dot_nt(d4, kv[:, MW:]) - delta) * SCALE).astype(BF16)
    dkv_ref[:, :MW] += _dot_tn(ds, q4)
    dkv_ref[:, MW:] += _dot_tn(p.astype(BF16), d4)
    return _unstack_heads(_dot(ds, kv[:, :MW]))


def _merge(o_refs, l_refs):
    ls = [r[...] for r in l_refs]
    m = jnp.maximum(jnp.maximum(ls[0], ls[1]), ls[2])
    es = [jnp.exp(l - m) for l in ls]
    inv = 1.0 / (es[0] + es[1] + es[2])
    ws = [e * inv for e in es]
    os_ = [r[...] for r in o_refs]
    mix = ws[0] * os_[0] + ws[1] * os_[1] + ws[2] * os_[2]
    return ws, mix


def _conv_taps(cg, u, cgp, up, first):
    a = cg * u
    ap = jnp.where(first, 0.0, cgp * up)
    row = lax.broadcasted_iota(jnp.int32, a.shape, 0)
    a1 = jnp.where(row == 0, ap[7:8, :], pltpu.roll(a, 1, 0))
    a2 = jnp.where(row == 0, ap[6:7, :], jnp.where(row == 1, ap[7:8, :], pltpu.roll(a, 2, 0)))
    return a, a1, a2


def _rope_tables(posf, after):
    half = 8
    invf = np.float32(ROPE_THETA) ** (-np.arange(half, dtype=np.float32) * np.float32(2.0 / 16))
    lane = np.arange(128)
    table = np.where((lane % HD) < 16, invf[lane % half], 0.0).astype(np.float32)[None, :]

    def body(pos_ref, invf_ref, c_ref, s1_ref, s2_ref):
        ang = pos_ref[...] * invf_ref[...]
        jm = lax.broadcasted_iota(jnp.int32, ang.shape, 1) & (HD - 1)
        cs = jnp.cos(ang)
        sn = jnp.sin(ang)
        c_ref[...] = jnp.where(jm < 16, cs, 1.0)
        s1_ref[...] = jnp.where(jm < 8, -sn, 0.0)
        s2_ref[...] = jnp.where((jm >= 8) & (jm < 16), sn, 0.0)

    out = _sds((S, 128), F32)
    return pl.pallas_call(
        functools.partial(_skip_arg, body, 2), name="rope_tables", grid=(NT,),
        in_specs=[_rows(1), _full((1, 128)), pl.BlockSpec(memory_space=pl.ANY)],
        out_specs=[_rows(128)] * 3, out_shape=[out] * 3,
        compiler_params=_params(("parallel",)),
    )(posf, jnp.asarray(table), after)


def _in_proj_a(x, g0, w_in, c, s1, s2, after):
    def body(x_ref, g_ref, w_ref, c_ref, s1_ref, s2_ref, hn_ref, q_ref, k_ref, v_ref, qm_ref, z_ref, proj):
        xf = x_ref[...]
        hn = xf * lax.rsqrt(jnp.mean(xf * xf, axis=-1, keepdims=True) + EPS) * g_ref[...]
        hb = hn.astype(BF16)
        hn_ref[...] = hb
        for s in range(4):
            proj[:, s * SH_A:(s + 1) * SH_A] = _dot(hb, w_ref[s])
        cc, a1, a2 = c_ref[...], s1_ref[...], s2_ref[...]
        for j in range(NQ // 128):
            q_ref[:, j * 128:(j + 1) * 128] = (
                _rope_fwd(proj[:, j * 128:(j + 1) * 128], cc, a1, a2) * SCALE).astype(BF16)
            k_ref[:, j * 128:(j + 1) * 128] = _rope_fwd(
                proj[:, NQ + j * 128:NQ + (j + 1) * 128], cc, a1, a2).astype(BF16)
        v_ref[...] = proj[:, 2 * NQ:3 * NQ].astype(BF16)
        qm_ref[...] = proj[:, 3 * NQ:3 * NQ + MW].astype(BF16)
        z_ref[...] = proj[:, 3 * NQ + MW:]

    return pl.pallas_call(
        functools.partial(_skip_arg, body, 6), name="in_proj_a", grid=(NT,),
        in_specs=[_rows(D), _full((1, D)), _full((4, D, SH_A)), _rows(128), _rows(128), _rows(128),
                  pl.BlockSpec(memory_space=pl.ANY)],
        out_specs=[_rows(D), _rows(NQ), _rows(NQ), _rows(NQ), _rows(MW), _rows(BR_A)],
        out_shape=[_sds((S, D), BF16), _sds((S, NQ), BF16), _sds((S, NQ), BF16), _sds((S, NQ), BF16),
                   _sds((S, MW), BF16), _sds((S, BR_A), F32)],
        scratch_shapes=[pltpu.VMEM((TM, IN_A), F32)],
        compiler_params=_params(("parallel",)),
    )(x, g0, w_in, c, s1, s2, after)


def _mem_fwd(mem, mg, wkv):
    def body(mem_ref, mg_ref, w_ref, memn_ref, kv_ref):
        mf = mem_ref[...]
        n = mf * lax.rsqrt(jnp.mean(mf * mf, axis=-1, keepdims=True) + EPS)
        for i in range(2):
            mn = (n * mg_ref[i:i + 1, :]).astype(BF16)
            memn_ref[i] = mn
            acc = _dot(mn[:, 0:NM], w_ref[0, i])
            for s in range(1, 4):
                acc += _dot(mn[:, s * NM:(s + 1) * NM], w_ref[s, i])
            kv_ref[i] = acc.astype(BF16)

    return pl.pallas_call(
        body, name="mem_fwd", grid=(1,),
        in_specs=[_full((NM, D)), _full((2, D)), _full((4, 2, NM, 2 * MW))],
        out_specs=[_full((2, NM, D)), _full((2, NM, 2 * MW))],
        out_shape=[_sds((2, NM, D), BF16), _sds((2, NM, 2 * MW), BF16)],
        compiler_params=_params(("arbitrary",)),
    )(mem, mg, wkv)


def _band_mask(j):
    qi = lax.broadcasted_iota(jnp.int32, (QBLK, 2 * QBLK), 0)
    kj = lax.broadcasted_iota(jnp.int32, (QBLK, 2 * QBLK), 1)
    dist = qi + QBLK - kj
    return (dist >= 0) & (dist <= QBLK) & ((kj >= QBLK) | (j > 0))


LANES = 128
NCHUNK = GW // LANES
FWD_UNROLL = 16
BWD_UNROLL = 16
CONV_CHUNK = 256


def _perm_matrix(d):
    n = TM // d
    p = np.zeros((TM, TM), np.float32)
    for r in range(d):
        for i in range(n):
            p[r * n + i, i * d + r] = 1.0
    return p


def _split_dot(p, x):
    hi = x.astype(BF16)
    lo = (x - hi.astype(F32)).astype(BF16)
    both = _dot(p, jnp.concatenate([hi, lo], axis=1))
    return both[:, :LANES] + both[:, LANES:]


def _pair_dot(p, a, b):
    both = _dot(p, jnp.concatenate([a, b], axis=1))
    return both[:, :LANES], both[:, LANES:]


def _tile_to_streams(y, dst, t, d):
    n, ln = TM // d, S // d
    for r in range(d):
        dst[r * ln + t * n:r * ln + (t + 1) * n, :] = y[r * n:(r + 1) * n].astype(dst.dtype)


def _tile_from_streams(src, t, d):
    n, ln = TM // d, S // d
    return jnp.concatenate([src[r * ln + t * n:r * ln + (t + 1) * n, :] for r in range(d)], axis=0)


def _head_masks():
    first = lax.broadcasted_iota(jnp.int32, (TM, LANES), 1) < HD
    return first, jnp.logical_not(first)


def _attn_fwd(q, k, v, g, after):
    d = DILATIONS[g]
    nb = S // d // QBLK
    perm = _perm_matrix(d)

    def body(q_ref, k_ref, v_ref, p_ref, pt_ref, o_ref, l_ref, ls_ref, q0, q1, ks, vs, os_):
        first, second = _head_masks()
        pm = p_ref[...]
        for t in range(NT):
            rows = slice(t * TM, (t + 1) * TM)
            if d == 1:
                qt = q_ref[rows, :].astype(F32)
            else:
                qt, kt = _pair_dot(pm, q_ref[rows, :], k_ref[rows, :])
                _tile_to_streams(kt, ks, t, d)
                if t % 2 == 0:
                    va, vb = _pair_dot(pm, v_ref[rows, :], v_ref[(t + 1) * TM:(t + 2) * TM, :])
                    _tile_to_streams(va, vs, t, d)
                    _tile_to_streams(vb, vs, t + 1, d)
            _tile_to_streams(jnp.where(first, qt, 0.0), q0, t, d)
            _tile_to_streams(jnp.where(second, qt, 0.0), q1, t, d)
        kref, vref = (k_ref, v_ref) if d == 1 else (ks, vs)
        oref, lref = (o_ref, l_ref) if d == 1 else (os_, ls_ref)

        def blk(b, carry):
            r0 = pl.multiple_of(b * QBLK, QBLK)
            p0 = pl.multiple_of(jnp.maximum(b - 1, 0) * QBLK, QBLK)
            kk = jnp.concatenate([kref[pl.ds(p0, QBLK), :], kref[pl.ds(r0, QBLK), :]], axis=0)
            vv = jnp.concatenate([vref[pl.ds(p0, QBLK), :], vref[pl.ds(r0, QBLK), :]], axis=0)
            valid = _band_mask(b & (nb - 1))
            acc, lse = [], []
            for qh in (q0, q1):
                s = jnp.where(valid, _dot_nt(qh[pl.ds(r0, QBLK), :], kk), NEG)
                m = jnp.max(s, axis=-1, keepdims=True)
                e = jnp.exp(s - m)
                l = jnp.sum(e, axis=-1, keepdims=True)
                acc.append(_dot(e.astype(BF16), vv) * (1.0 / l))
                lse.append(m + jnp.log(l))
            f = first[:QBLK]
            oref[pl.ds(r0, QBLK), :] = jnp.where(f, acc[0], acc[1])
            lref[pl.ds(r0, QBLK), :] = jnp.where(f, lse[0], lse[1])
            return carry

        lax.fori_loop(0, S // QBLK, blk, 0, unroll=FWD_UNROLL)
        if d > 1:
            ptm = pt_ref[...]
            for t in range(NT):
                rows = slice(t * TM, (t + 1) * TM)
                o_ref[rows, :] = _split_dot(ptm, _tile_from_streams(os_, t, d))
                l_ref[rows, :] = _split_dot(ptm, _tile_from_streams(ls_ref, t, d))

    qkv_spec = pl.BlockSpec((S, LANES), lambda c: (0, g * NCHUNK + c))
    out_spec = pl.BlockSpec((S, LANES), lambda c: (0, c))
    n_out = 2 if d == 1 else 3
    inner = body if d > 1 else functools.partial(_drop_arg, body, 7)
    outs = pl.pallas_call(
        functools.partial(_skip_arg, inner, 5), name=f"attn_fwd_g{g}", grid=(NCHUNK,),
        in_specs=[qkv_spec] * 3 + [_full((TM, TM))] * 2 + [pl.BlockSpec(memory_space=pl.ANY)],
        out_specs=[out_spec] * n_out, out_shape=[_sds((S, GW), F32)] * n_out,
        scratch_shapes=[pltpu.VMEM((S, LANES), BF16)] * 4 + [pltpu.VMEM((S, LANES), F32)],
        compiler_params=_params(("parallel",)),
    )(q, k, v, jnp.asarray(perm, BF16), jnp.asarray(perm.T, BF16), after)
    return (outs[0], outs[1], outs[1]) if d == 1 else tuple(outs)


def _drop_arg(body, pos, *refs):
    return body(*refs[:pos], None, *refs[pos:])


def _attn_out(os_, ls, qm, kv0, z, x, w_out):
    def body(o0, o1, o2, l0, l1, l2, qm_ref, kv_ref, z_ref, x_ref, w_ref, h_ref, ybuf):
        _, mix = _merge((o0, o1, o2), (l0, l1, l2))
        sz, _ = _silu_parts(z_ref[...])
        ybuf[:, :GW] = (mix * sz[:, :GW]).astype(BF16)
        _, mo, _ = _mem_attn(qm_ref[...], kv_ref[...])
        ybuf[:, GW:] = (mo * sz[:, GW:]).astype(BF16)
        yb = ybuf[...]
        for s in range(4):
            cs = slice(s * SH_O, (s + 1) * SH_O)
            h_ref[:, cs] = x_ref[:, cs] + _dot(yb, w_ref[s])

    return pl.pallas_call(
        body, name="attn_out", grid=(NX,),
        in_specs=[_rows(GW, MX)] * 6 + [_rows(MW, MX), _full((NM, 2 * MW)), _rows(BR_A, MX), _rows(D, MX),
                                        _full((4, BR_A, SH_O))],
        out_specs=_rows(D, MX), out_shape=_sds((S, D), F32),
        scratch_shapes=[pltpu.VMEM((MX, BR_A), BF16)],
        compiler_params=_params(("parallel",)),
    )(*os_, *ls, qm, kv0, z, x, w_out)


def _in_proj_b(h1, g1, w_in):
    def body(x_ref, g_ref, w_ref, hn_ref, bg_ref, cg_ref, u_ref, qm_ref, z_ref, proj):
        xf = x_ref[...]
        hn = xf * lax.rsqrt(jnp.mean(xf * xf, axis=-1, keepdims=True) + EPS) * g_ref[...]
        hb = hn.astype(BF16)
        hn_ref[...] = hb
        for s in range(4):
            proj[:, s * SH_B:(s + 1) * SH_B] = _dot(hb, w_ref[s])
        bg_ref[...] = proj[:, :D]
        cg_ref[...] = proj[:, D:2 * D]
        u_ref[...] = proj[:, 2 * D:3 * D]
        qm_ref[...] = proj[:, 3 * D:3 * D + MW].astype(BF16)
        z_ref[...] = proj[:, 3 * D + MW:]

    return pl.pallas_call(
        body, name="in_proj_b", grid=(NT,),
        in_specs=[_rows(D), _full((1, D)), _full((4, D, SH_B))],
        out_specs=[_rows(D), _rows(D), _rows(D), _rows(D), _rows(MW), _rows(BR_B)],
        out_shape=[_sds((S, D), BF16), _sds((S, D), F32), _sds((S, D), F32), _sds((S, D), F32),
                   _sds((S, MW), BF16), _sds((S, BR_B), F32)],
        scratch_shapes=[pltpu.VMEM((TM, IN_B), F32)],
        compiler_params=_params(("parallel",)),
    )(h1, g1, w_in)


def _prev8(width):
    return pl.BlockSpec((8, width), lambda i: (jnp.maximum(i * (MX // 8) - 1, 0), 0))


def _conv_out_loss(bg, cg, u, cw, qm, kv1, z, h1, w_out, fg, tgt):
    def body(bg_ref, cg_ref, u_ref, cgp_ref, up_ref, cw_ref, qm_ref, kv_ref, z_ref, h_ref, w_ref, fg_ref, t_ref,
             dh_ref, loss_ref, dfg_ref, ybuf):
        i = pl.program_id(0)
        a, a1, a2 = _conv_taps(cg_ref[...], u_ref[...], cgp_ref[...], up_ref[...], i == 0)
        conv = cw_ref[0:1, :] * a2 + cw_ref[1:2, :] * a1 + cw_ref[2:3, :] * a
        sz, _ = _silu_parts(z_ref[...])
        ybuf[:, :D] = (bg_ref[...] * conv * sz[:, :D]).astype(BF16)
        _, mo, _ = _mem_attn(qm_ref[...], kv_ref[...])
        ybuf[:, D:] = (mo * sz[:, D:]).astype(BF16)
        h2 = h_ref[...] + _dot(ybuf[...], w_ref[...])
        rstd = lax.rsqrt(jnp.mean(h2 * h2, axis=-1, keepdims=True) + EPS)
        n = h2 * rstd
        fgv = fg_ref[...]
        err = n * fgv - t_ref[...]
        dout = err * (1.0 / D)
        dn = dout * fgv
        dh_ref[...] = rstd * (dn - n * jnp.mean(dn * n, axis=-1, keepdims=True))

        @pl.when(i == 0)
        def _():
            loss_ref[...] = jnp.zeros_like(loss_ref)
            dfg_ref[...] = jnp.zeros_like(dfg_ref)

        loss_ref[...] += jnp.sum(err * err) * (0.5 / D)
        dfg_ref[...] += jnp.sum(dout * n, axis=0, keepdims=True)

    return pl.pallas_call(
        body, name="conv_out_loss", grid=(NX,),
        in_specs=[_rows(D, MX), _rows(D, MX), _rows(D, MX), _prev8(D), _prev8(D), _full((8, D)), _rows(MW, MX),
                  _full((NM, 2 * MW)), _rows(BR_B, MX), _rows(D, MX), _full((BR_B, D)), _full((1, D)), _rows(D, MX)],
        out_specs=[_rows(D, MX), _full((1, 128)), _full((1, D))],
        out_shape=[_sds((S, D), F32), _sds((1, 128), F32), _sds((1, D), F32)],
        scratch_shapes=[pltpu.VMEM((MX, BR_B), BF16)],
        compiler_params=_params(("arbitrary",)),
    )(bg, cg, u, cg, u, cw, qm, kv1, z, h1, w_out, fg, tgt)


def _conv_bwd(dh2, bg, cg, u, cw, qm, kv1, z, w_out):
    rev = lambda i: (NX - 1 - i, 0)
    rows = lambda w: pl.BlockSpec((MX, w), rev)
    prev8 = pl.BlockSpec((8, D), lambda i: (jnp.maximum((NX - 1 - i) * (MX // 8) - 1, 0), 0))

    def body(dh_ref, bg_ref, cg_ref, u_ref, cgp_ref, up_ref, cw_ref, qm_ref, kv_ref, z_ref, w_ref,
             dproj_ref, dw_ref, dcw_ref, dkv_ref, dwb_ref, ybuf, carry):
        i = pl.program_id(0)

        @pl.when(i == 0)
        def _():
            dw_ref[...] = jnp.zeros_like(dw_ref)
            dcw_ref[...] = jnp.zeros_like(dcw_ref)
            dkv_ref[...] = jnp.zeros_like(dkv_ref)
            carry[...] = jnp.zeros_like(carry)

        dhb = dh_ref[...].astype(BF16)
        dy = _dot_nt(dhb, w_ref[...])
        kvv = kv_ref[...]
        p, mo, q4 = _mem_attn(qm_ref[...], kvv)
        szm, dszm = _silu_parts(z_ref[:, D:])
        ybuf[:, D:] = (mo * szm).astype(BF16)
        dym = dy[:, D:]
        dproj_ref[:, 3 * D + MW + D:] = (dym * mo * dszm).astype(BF16)
        first_tile = i == NX - 1
        for c in range(D // CONV_CHUNK):
            cs = slice(c * CONV_CHUNK, (c + 1) * CONV_CHUNK)
            bgv, cgv, uv = bg_ref[:, cs], cg_ref[:, cs], u_ref[:, cs]
            a, a1, a2 = _conv_taps(cgv, uv, cgp_ref[:, cs], up_ref[:, cs], first_tile)
            w0, w1, w2 = cw_ref[0:1, cs], cw_ref[1:2, cs], cw_ref[2:3, cs]
            conv = w0 * a2 + w1 * a1 + w2 * a
            mix = bgv * conv
            sz, dsz = _silu_parts(z_ref[:, cs])
            ybuf[:, cs] = (mix * sz).astype(BF16)
            dyc = dy[:, cs]
            dproj_ref[:, 3 * D + MW + c * CONV_CHUNK:3 * D + MW + (c + 1) * CONV_CHUNK] = (
                dyc * mix * dsz).astype(BF16)
            dmix = dyc * sz
            dproj_ref[:, cs] = (dmix * conv).astype(BF16)
            dc = dmix * bgv
            nxt = carry[:, cs]
            row = lax.broadcasted_iota(jnp.int32, dc.shape, 0)
            dc1 = jnp.where(row == MX - 1, nxt[0:1, :], pltpu.roll(dc, MX - 1, 0))
            dc2 = jnp.where(row == MX - 2, nxt[0:1, :],
                            jnp.where(row == MX - 1, nxt[1:2, :], pltpu.roll(dc, MX - 2, 0)))
            carry[:, cs] = dc[0:8, :]
            da = w2 * dc + w1 * dc1 + w0 * dc2
            dproj_ref[:, D + c * CONV_CHUNK:D + (c + 1) * CONV_CHUNK] = (da * uv).astype(BF16)
            dproj_ref[:, 2 * D + c * CONV_CHUNK:2 * D + (c + 1) * CONV_CHUNK] = (da * cgv).astype(BF16)
            dcw_ref[0:1, cs] += jnp.sum(dc * a2, axis=0, keepdims=True)
            dcw_ref[1:2, cs] += jnp.sum(dc * a1, axis=0, keepdims=True)
            dcw_ref[2:3, cs] += jnp.sum(dc * a, axis=0, keepdims=True)
        dw_ref[...] += _dot_tn(ybuf[...], dhb)
        dproj_ref[:, 3 * D:3 * D + MW] = _mem_attn_bwd(dym * szm, p, mo, q4, kvv, dkv_ref).astype(BF16)

        @pl.when(i == NX - 1)
        def _():
            dwb_ref[...] = dw_ref[...].astype(BF16)

    return pl.pallas_call(
        body, name="conv_bwd", grid=(NX,),
        in_specs=[rows(D), rows(D), rows(D), rows(D), prev8, prev8, _full((8, D)), rows(MW),
                  _full((NM, 2 * MW)), rows(BR_B), _full((BR_B, D))],
        out_specs=[rows(IN_B), _full((BR_B, D)), _full((8, D)), _full((NM, 2 * MW)), _full((BR_B, D))],
        out_shape=[_sds((S, IN_B), BF16), _sds((BR_B, D), F32), _sds((8, D), F32), _sds((NM, 2 * MW), F32),
                   _sds((BR_B, D), BF16)],
        scratch_shapes=[pltpu.VMEM((MX, BR_B), BF16), pltpu.VMEM((8, D), F32)],
        compiler_params=_params(("arbitrary",)),
    )(dh2, bg, cg, u, cg, u, cw, qm, kv1, z, w_out)


def _in_proj_bwd(dproj, w_in, xin, g, dres, after, width, name):
    sh = width // 4

    def body(dp_ref, w_ref, x_ref, g_ref, dr_ref, dx_ref, dg_ref):
        i = pl.program_id(0)
        dhn = _dot_nt(dp_ref[:, 0:sh], w_ref[0])
        for s in range(1, 4):
            dhn += _dot_nt(dp_ref[:, s * sh:(s + 1) * sh], w_ref[s])
        xf = x_ref[...]
        rstd = lax.rsqrt(jnp.mean(xf * xf, axis=-1, keepdims=True) + EPS)
        n = xf * rstd
        dn = dhn * g_ref[...]
        dx_ref[...] = dr_ref[...] + rstd * (dn - n * jnp.mean(dn * n, axis=-1, keepdims=True))

        @pl.when(i == 0)
        def _():
            dg_ref[...] = jnp.zeros_like(dg_ref)

        dg_ref[...] += jnp.sum(dhn * n, axis=0, keepdims=True)

    return pl.pallas_call(
        functools.partial(_skip_arg, body, 5), name=name, grid=(NT,),
        in_specs=[_rows(width), _full((4, D, sh)), _rows(D), _full((1, D)), _rows(D), pl.BlockSpec(memory_space=pl.ANY)],
        out_specs=[_rows(D), _full((1, D))],
        out_shape=[_sds((S, D), F32), _sds((1, D), F32)],
        compiler_params=_params(("arbitrary",)),
    )(dproj, w_in, xin, g, dres, after)


def _w_in_grad(hn, dproj, width, name):
    sh = width // 4

    def body(hn_ref, dp_ref, dw_ref, dwb_ref):
        dw = _dot_tn(hn_ref[...], dp_ref[...])
        dw_ref[0] = dw
        dwb_ref[0] = dw.astype(BF16)

    spec = pl.BlockSpec((1, D, sh), lambda s: (s, 0, 0))
    return pl.pallas_call(
        body, name=name, grid=(4,),
        in_specs=[_full((S, D)), pl.BlockSpec((S, sh), lambda s: (0, s))],
        out_specs=[spec, spec], out_shape=[_sds((4, D, sh), F32), _sds((4, D, sh), BF16)],
        compiler_params=_params(("parallel",)),
    )(hn, dproj)


def _attn_out_bwd(dh1, os_, ls, qm, kv0, z, w_out, after):
    ones_bd = np.kron(np.eye(GW // HD, dtype=np.float32), np.ones((HD, HD), np.float32))

    def body(dh_ref, o0, o1, o2, l0, l1, l2, qm_ref, kv_ref, z_ref, w_ref, bd_ref,
             do0, do1, do2, dd0, dd1, dd2, dqm_ref, dz_ref, dw_ref, dkv_ref, dwb_ref, ybuf):
        i = pl.program_id(0)

        @pl.when(i == 0)
        def _():
            dw_ref[...] = jnp.zeros_like(dw_ref)
            dkv_ref[...] = jnp.zeros_like(dkv_ref)

        ws, mix = _merge((o0, o1, o2), (l0, l1, l2))
        sz, dsz = _silu_parts(z_ref[...])
        kvv = kv_ref[...]
        p, mo, q4 = _mem_attn(qm_ref[...], kvv)
        ybuf[:, :GW] = (mix * sz[:, :GW]).astype(BF16)
        ybuf[:, GW:] = (mo * sz[:, GW:]).astype(BF16)
        yb = ybuf[...]
        dh = dh_ref[...]
        dy = None
        for s in range(4):
            dhb = dh[:, s * SH_O:(s + 1) * SH_O].astype(BF16)
            dw_ref[s] += _dot_tn(yb, dhb)
            part = _dot_nt(dhb, w_ref[s])
            dy = part if dy is None else dy + part
        dcat = dy * sz
        dz_ref[:, :GW] = (dy[:, :GW] * mix * dsz[:, :GW]).astype(BF16)
        dz_ref[:, GW:] = (dy[:, GW:] * mo * dsz[:, GW:]).astype(BF16)
        dmix = dcat[:, :GW]
        prod = dmix * mix
        hi = prod.astype(BF16)
        lo = (prod - hi.astype(F32)).astype(BF16)
        bd = bd_ref[...]
        tot = _dot(hi, bd) + _dot(lo, bd)
        for w, do_ref, dd_ref in zip(ws, (do0, do1, do2), (dd0, dd1, dd2)):
            do_ref[...] = (w * dmix).astype(BF16)
            dd_ref[...] = w * tot

        dqm_ref[...] = _mem_attn_bwd(dcat[:, GW:], p, mo, q4, kvv, dkv_ref).astype(BF16)

        @pl.when(i == NX - 1)
        def _():
            dwb_ref[...] = dw_ref[...].astype(BF16)

    return pl.pallas_call(
        functools.partial(_skip_arg, body, 12), name="attn_out_bwd", grid=(NX,),
        in_specs=[_rows(D, MX)] + [_rows(GW, MX)] * 6 + [_rows(MW, MX), _full((NM, 2 * MW)), _rows(BR_A, MX),
                                                           _full((4, BR_A, SH_O)), _full((GW, GW)),
                                                           pl.BlockSpec(memory_space=pl.ANY)],
        out_specs=[_rows(GW, MX)] * 6 + [_rows(MW, MX), _rows(BR_A, MX), _full((4, BR_A, SH_O)),
                                         _full((NM, 2 * MW)), _full((4, BR_A, SH_O))],
        out_shape=[_sds((S, GW), BF16)] * 3 + [_sds((S, GW), F32)] * 3 + [
            _sds((S, MW), BF16), _sds((S, BR_A), BF16), _sds((4, BR_A, SH_O), F32), _sds((NM, 2 * MW), F32),
            _sds((4, BR_A, SH_O), BF16)],
        scratch_shapes=[pltpu.VMEM((MX, BR_A), BF16)],
        compiler_params=_params(("arbitrary",)),
    )(dh1, *os_, *ls, qm, kv0, z, w_out, jnp.asarray(ones_bd, dtype=BF16), after)


def _attn_bwd(q, k, v, do, lse_s, dd, g):
    d = DILATIONS[g]
    nb = S // d // QBLK
    perm = _perm_matrix(d)

    def body(q_ref, k_ref, v_ref, do_ref, l_ref, dd_ref, p_ref, pt_ref, dq_ref, dk_ref, dv_ref,
             q0, q1, g0, g1, ks, vs, dds, dqs, dks, dvs):
        first, second = _head_masks()
        pm = p_ref[...]
        for t in range(NT):
            rows = slice(t * TM, (t + 1) * TM)
            if d == 1:
                qt = q_ref[rows, :].astype(F32)
                gt = do_ref[rows, :].astype(F32)
            else:
                qt, gt = _pair_dot(pm, q_ref[rows, :], do_ref[rows, :])
                kt, vt = _pair_dot(pm, k_ref[rows, :], v_ref[rows, :])
                _tile_to_streams(kt, ks, t, d)
                _tile_to_streams(vt, vs, t, d)
                _tile_to_streams(_split_dot(pm, dd_ref[rows, :]), dds, t, d)
            _tile_to_streams(jnp.where(first, qt, 0.0), q0, t, d)
            _tile_to_streams(jnp.where(second, qt, 0.0), q1, t, d)
            _tile_to_streams(jnp.where(first, gt, 0.0), g0, t, d)
            _tile_to_streams(jnp.where(second, gt, 0.0), g1, t, d)
        kref, vref, ddref = (k_ref, v_ref, dd_ref) if d == 1 else (ks, vs, dds)
        dqref, dkref, dvref = dqs, dks, dvs
        dkref[...] = jnp.zeros_like(dkref)
        dvref[...] = jnp.zeros_like(dvref)

        def blk(b, carry):
            r0 = pl.multiple_of(b * QBLK, QBLK)
            p0 = pl.multiple_of(jnp.maximum(b - 1, 0) * QBLK, QBLK)
            kk = jnp.concatenate([kref[pl.ds(p0, QBLK), :], kref[pl.ds(r0, QBLK), :]], axis=0)
            vv = jnp.concatenate([vref[pl.ds(p0, QBLK), :], vref[pl.ds(r0, QBLK), :]], axis=0)
            lb = l_ref[pl.ds(r0, QBLK), :]
            ddb = ddref[pl.ds(r0, QBLK), :]
            lcol = jnp.concatenate([lb[:, 0:1], lb[:, HD:HD + 1]], axis=0)
            dcol = jnp.concatenate([ddb[:, 0:1], ddb[:, HD:HD + 1]], axis=0)
            valid = _band_mask(b & (nb - 1))
            valid2 = jnp.concatenate([valid, valid], axis=0)
            qq = jnp.concatenate([q0[pl.ds(r0, QBLK), :], q1[pl.ds(r0, QBLK), :]], axis=0)
            gg = jnp.concatenate([g0[pl.ds(r0, QBLK), :], g1[pl.ds(r0, QBLK), :]], axis=0)
            p = jnp.where(valid2, jnp.exp(_dot_nt(qq, kk) - lcol), 0.0)
            ds = (p * (_dot_nt(gg, vv) - dcol)).astype(BF16)
            dq2 = _dot(ds, kk)
            dqref[pl.ds(r0, QBLK), :] = jnp.where(first[:QBLK], dq2[:QBLK], dq2[QBLK:])
            dkk = _dot_tn(ds, qq)
            dvv = _dot_tn(p.astype(BF16), gg)
            dkref[pl.ds(p0, QBLK), :] += dkk[:QBLK]
            dkref[pl.ds(r0, QBLK), :] += dkk[QBLK:]
            dvref[pl.ds(p0, QBLK), :] += dvv[:QBLK]
            dvref[pl.ds(r0, QBLK), :] += dvv[QBLK:]
            return carry

        lax.fori_loop(0, S // QBLK, blk, 0, unroll=BWD_UNROLL)

        ptm = pt_ref[...] if d > 1 else None
        for t in range(NT):
            rows = slice(t * TM, (t + 1) * TM)
            if d == 1:
                dq_ref[rows, :] = dqs[rows, :].astype(BF16)
                dk_ref[rows, :] = dks[rows, :].astype(BF16)
                dv_ref[rows, :] = dvs[rows, :].astype(BF16)
            else:
                tq, tk = _pair_dot(ptm, _tile_from_streams(dqs, t, d).astype(BF16),
                                   _tile_from_streams(dks, t, d).astype(BF16))
                dq_ref[rows, :] = tq.astype(BF16)
                dk_ref[rows, :] = tk.astype(BF16)
                if t % 2 == 0:
                    ta, tb = _pair_dot(ptm, _tile_from_streams(dvs, t, d).astype(BF16),
                                       _tile_from_streams(dvs, t + 1, d).astype(BF16))
                    dv_ref[rows, :] = ta.astype(BF16)
                    dv_ref[(t + 1) * TM:(t + 2) * TM, :] = tb.astype(BF16)

    qkv_spec = pl.BlockSpec((S, LANES), lambda c: (0, g * NCHUNK + c))
    one_spec = pl.BlockSpec((S, LANES), lambda c: (0, c))
    return pl.pallas_call(
        body, name=f"attn_bwd_g{g}", grid=(NCHUNK,),
        in_specs=[qkv_spec] * 3 + [one_spec] * 3 + [_full((TM, TM))] * 2, out_specs=[one_spec] * 3,
        out_shape=[_sds((S, GW), BF16)] * 3,
        scratch_shapes=[pltpu.VMEM((S, LANES), BF16)] * 6 + [pltpu.VMEM((S, LANES), F32)] * 4,
        compiler_params=_params(("parallel",)),
    )(q, k, v, do, lse_s, dd, jnp.asarray(perm, BF16), jnp.asarray(perm.T, BF16))


def _qkv_bwd(dqs, dks, dvs, dqm, dz, c, s1, s2):
    def body(q0, q1, q2, k0, k1, k2, v0, v1, v2, dqm_ref, dz_ref, c_ref, s1_ref, s2_ref, dp_ref):
        cc, a1, a2 = c_ref[...], s1_ref[...], s2_ref[...]
        for g, (qr, kr, vr) in enumerate(((q0, k0, v0), (q1, k1, v1), (q2, k2, v2))):
            for j in range(GW // 128):
                ls_ = slice(j * 128, (j + 1) * 128)
                c0 = g * GW + j * 128
                dp_ref[:, c0:c0 + 128] = (_rope_bwd(qr[:, ls_].astype(F32), cc, a1, a2) * SCALE).astype(BF16)
                dp_ref[:, NQ + c0:NQ + c0 + 128] = _rope_bwd(kr[:, ls_].astype(F32), cc, a1, a2).astype(BF16)
            dp_ref[:, 2 * NQ + g * GW:2 * NQ + (g + 1) * GW] = vr[...]
        dp_ref[:, 3 * NQ:3 * NQ + MW] = dqm_ref[...]
        dp_ref[:, 3 * NQ + MW:] = dz_ref[...]

    return pl.pallas_call(
        body, name="qkv_bwd", grid=(NT,),
        in_specs=[_rows(GW)] * 9 + [_rows(MW), _rows(BR_A), _rows(128), _rows(128), _rows(128)],
        out_specs=_rows(IN_A), out_shape=_sds((S, IN_A), BF16),
        compiler_params=_params(("parallel",)),
    )(*dqs, *dks, *dvs, dqm, dz, c, s1, s2)


def _mem_bwd(mem, mg, memn, wkv, dkv0, dkv1):
    def body(mem_ref, mg_ref, memn_ref, w_ref, d0_ref, d1_ref, dw_ref, dwb_ref, dg_ref):
        mf = mem_ref[...]
        n = mf * lax.rsqrt(jnp.mean(mf * mf, axis=-1, keepdims=True) + EPS)
        for i, d_ref in enumerate((d0_ref, d1_ref)):
            dkv = d_ref[...].astype(BF16)
            mn = memn_ref[i]
            for s in range(4):
                cs = slice(s * NM, (s + 1) * NM)
                dw = _dot_tn(mn[:, cs], dkv)
                dw_ref[s, i] = dw
                dwb_ref[s, i] = dw.astype(BF16)
                dmn = _dot_nt(dkv, w_ref[s, i])
                dg_ref[i:i + 1, cs] = jnp.sum(dmn * n[:, cs], axis=0, keepdims=True)

    return pl.pallas_call(
        body, name="mem_bwd", grid=(1,),
        in_specs=[_full((NM, D)), _full((2, D)), _full((2, NM, D)), _full((4, 2, NM, 2 * MW)),
                  _full((NM, 2 * MW)), _full((NM, 2 * MW))],
        out_specs=[_full((4, 2, NM, 2 * MW)), _full((4, 2, NM, 2 * MW)), _full((2, D))],
        out_shape=[_sds((4, 2, NM, 2 * MW), F32), _sds((4, 2, NM, 2 * MW), BF16), _sds((2, D), F32)],
        compiler_params=_params(("arbitrary",)),
    )(mem, mg, memn, wkv, dkv0, dkv1)


MESH = pl.DeviceIdType.MESH
ANY = pl.BlockSpec(memory_space=pl.ANY)
BIG = (("wkv", 2, NM, 2 * MW), ("w_in_a", 1, D, SH_A), ("w_out_a", 1, BR_A, SH_O),
       ("w_in_b", 1, D, SH_B), ("w_out_b", 1, BR_B // 4, D))
NBIG = len(BIG)
CW_ROWS = 8


def _place():
    x, y, c = lax.axis_index("x"), lax.axis_index("y"), lax.axis_index("c")
    chips = ((1 - x, y), (x, 1 - y), (1 - x, 1 - y))
    return x, y, c, chips


def _remote(src, dst, ssem, rsem, dev):
    return pltpu.make_async_remote_copy(src_ref=src, dst_ref=dst, send_sem=ssem, recv_sem=rsem,
                                        device_id=dev, device_id_type=MESH)


def _cast_weights(place, ws, after, idx, name):
    nblk = 4
    n = len(idx)
    dims = [BIG[w][1:] for w in idx]

    def body(pref, *refs):
        for i in range(n):
            refs[n + 1 + i][0] = refs[i][...].astype(BF16)

    grid_spec = pltpu.PrefetchScalarGridSpec(
        num_scalar_prefetch=1, grid=(nblk,),
        in_specs=[pl.BlockSpec((k, r // nblk, cdim), lambda i, pref: (0, i, 0)) for k, r, cdim in dims]
        + [pl.BlockSpec(memory_space=pl.ANY)],
        out_specs=[pl.BlockSpec((1, k, r // nblk, cdim), lambda i, pref: (pref[1], 0, i, 0)) for k, r, cdim in dims])
    return pl.pallas_call(
        body, name=name, grid_spec=grid_spec,
        out_shape=[_sds((4, k, r, cdim), BF16) for k, r, cdim in dims],
        compiler_params=_params(("parallel",)),
    )(place, *ws, after)


LAYER_A = (0, 1, 2)
LAYER_B = (3, 4)
HBM = pl.BlockSpec(memory_space=pltpu.HBM)
SEM = pl.BlockSpec(memory_space=pltpu.SEMAPHORE)
EFFECT = pltpu.SideEffectType.DATAFLOW_SIDE_EFFECTING
TOKEN = (8, 128)


def _half(ref, w, which):
    h = BIG[w][2] // 2
    return ref.at[:, pl.ds(which * h, h), :]


def _skip_arg(body, pos, *refs):
    return body(*refs[:pos], *refs[pos + 1:])


def _gather_start(wb, after, idx, name, barrier_id):
    n = len(idx)

    def body(*refs):
        src = refs[:n]
        send_sems, recv_sems = refs[n + 1], refs[n + 2]
        token = refs[2 * n + 3]
        x, y, c, chips = _place()
        _peer_barrier([(px, py, c) for px, py in chips])
        me = 2 * x + y
        for j, (px, py) in enumerate(chips):
            for i in range(n):
                mine = _half(src[i].at[me], idx[i], c)
                _remote(mine, mine, send_sems.at[j * n + i], recv_sems.at[j * n + i], (px, py, c)).start()
        token[...] = jnp.zeros(TOKEN, F32)

    outs = pl.pallas_call(
        body, name=name, in_specs=[HBM] * n + [ANY],
        out_specs=(SEM, SEM) + (HBM,) * n + (pl.BlockSpec(memory_space=pltpu.VMEM),),
        out_shape=(pltpu.SemaphoreType.DMA((3 * n,)), pltpu.SemaphoreType.DMA((3 * n,)))
        + tuple(pltpu.HBM(w.shape, w.dtype) for w in wb) + (_sds(TOKEN, F32),),
        input_output_aliases={i: 2 + i for i in range(n)},
        compiler_params=pltpu.CompilerParams(has_side_effects=EFFECT, collective_id=barrier_id),
    )(*[pltpu.with_memory_space_constraint(w, pltpu.HBM) for w in wb], after)
    return outs[0], outs[1], list(outs[2:2 + n]), outs[2 + n]


def _gather_wait(send_sems, recv_sems, wb, after, idx, name, started=None):
    n = len(idx)
    started = idx if started is None else started
    n_all = len(started)
    pos = [started.index(w) for w in idx]

    def body(*refs):
        buf = refs[:n]
        send_sems, recv_sems = refs[n], refs[n + 1]
        x, y, c, chips = _place()
        me = 2 * x + y
        for j, (px, py) in enumerate(chips):
            for i in range(n):
                mine = _half(buf[i].at[me], idx[i], c)
                got = _half(buf[i].at[2 * px + py], idx[i], c)
                k = j * n_all + pos[i]
                _remote(mine, mine, send_sems.at[k], recv_sems.at[k], (px, py, c)).wait_send()
                _remote(got, got, send_sems.at[k], recv_sems.at[k], (px, py, c)).wait_recv()

    outs = pl.pallas_call(
        body, name=name, in_specs=[HBM] * n + [SEM, SEM] + [ANY] * len(after), out_specs=(HBM,) * n,
        out_shape=tuple(pltpu.HBM(w.shape, w.dtype) for w in wb),
        input_output_aliases={i: i for i in range(n)},
        compiler_params=pltpu.CompilerParams(has_side_effects=EFFECT),
    )(*wb, send_sems, recv_sems, *after)
    return list(outs)


def _gather_forward(wb, idx, name, barrier_id):
    n = len(idx)

    def body(*refs):
        dst = refs[n:2 * n]
        send_sems, recv_sems = refs[2 * n], refs[2 * n + 1]
        x, y, c, chips = _place()
        _sibling_barrier(x, y, c)
        cps = []
        for j, (px, py) in enumerate(chips):
            for i in range(n):
                got = _half(dst[i].at[2 * px + py], idx[i], c)
                cps.append(_remote(got, got, send_sems.at[j, i], recv_sems.at[j, i], (x, y, 1 - c)))
                cps[-1].start()
        for j, (px, py) in enumerate(chips):
            for i in range(n):
                got = _half(dst[i].at[2 * px + py], idx[i], 1 - c)
                _remote(got, got, send_sems.at[j, i], recv_sems.at[j, i], (x, y, 1 - c)).wait_recv()
        for cp in cps:
            cp.wait_send()

    return pl.pallas_call(
        body, name=name, in_specs=[ANY] * n, out_specs=[ANY] * n, out_shape=[_sds(w.shape, BF16) for w in wb],
        input_output_aliases={i: i for i in range(n)},
        scratch_shapes=[pltpu.SemaphoreType.DMA((3, n)), pltpu.SemaphoreType.DMA((3, n))],
        compiler_params=pltpu.CompilerParams(collective_id=barrier_id),
    )(*wb)


def _forward_start(wb, cw, after, idx, name):
    n = len(idx)
    m = n if cw is None else n + 2

    def body(*refs):
        buf = refs[:n]
        send_sems, recv_sems = refs[m + 1], refs[m + 2]
        token = refs[2 * m + 3]
        x, y, c, chips = _place()
        for j, (px, py) in enumerate(chips):
            for i in range(n):
                got = _half(buf[i].at[2 * px + py], idx[i], c)
                _remote(got, got, send_sems.at[j * (n + 1) + i], recv_sems.at[j * (n + 1) + i], (x, y, 1 - c)).start()
            if cw is not None:
                _remote(refs[n], refs[n + 1].at[2 * x + y], send_sems.at[j * (n + 1) + n],
                        recv_sems.at[j * (n + 1) + n], (px, py, c)).start()
        token[...] = jnp.zeros(TOKEN, F32)

    arrays = list(wb) if cw is None else list(wb) + [cw, lax.empty((4, CW_ROWS, SH_O), F32)]
    outs = pl.pallas_call(
        body, name=name, in_specs=[HBM] * m + [ANY],
        out_specs=(SEM, SEM) + (HBM,) * m + (pl.BlockSpec(memory_space=pltpu.VMEM),),
        out_shape=(pltpu.SemaphoreType.DMA((3 * (n + 1),)), pltpu.SemaphoreType.DMA((3 * (n + 1),)))
        + tuple(pltpu.HBM(a.shape, a.dtype) for a in arrays) + (_sds(TOKEN, F32),),
        input_output_aliases={i: 2 + i for i in range(m)},
        compiler_params=pltpu.CompilerParams(has_side_effects=EFFECT),
    )(*[pltpu.with_memory_space_constraint(a, pltpu.HBM) for a in arrays], after)
    return outs[0], outs[1], list(outs[2:2 + m]), outs[2 + m]


def _forward_wait(send_sems, recv_sems, arrays, after, idx, with_cw, name):
    n = len(idx)
    m = len(arrays)

    def body(*refs):
        buf = refs[:n]
        send_sems, recv_sems = refs[m], refs[m + 1]
        x, y, c, chips = _place()
        for j, (px, py) in enumerate(chips):
            for i in range(n):
                sent = _half(buf[i].at[2 * px + py], idx[i], c)
                got = _half(buf[i].at[2 * px + py], idx[i], 1 - c)
                k = j * (n + 1) + i
                _remote(sent, sent, send_sems.at[k], recv_sems.at[k], (x, y, 1 - c)).wait_send()
                _remote(got, got, send_sems.at[k], recv_sems.at[k], (x, y, 1 - c)).wait_recv()
            if with_cw:
                k = j * (n + 1) + n
                theirs = refs[n + 1].at[2 * px + py]
                _remote(refs[n], theirs, send_sems.at[k], recv_sems.at[k], (px, py, c)).wait_send()
                _remote(refs[n], theirs, send_sems.at[k], recv_sems.at[k], (px, py, c)).wait_recv()

    outs = pl.pallas_call(
        body, name=name, in_specs=[HBM] * m + [SEM, SEM] + [ANY] * len(after), out_specs=(HBM,) * m,
        out_shape=tuple(pltpu.HBM(a.shape, a.dtype) for a in arrays),
        input_output_aliases={i: i for i in range(m)},
        compiler_params=pltpu.CompilerParams(has_side_effects=EFFECT),
    )(*arrays, send_sems, recv_sems, *after)
    return list(outs)


def _peer_barrier(peers):
    barrier = pltpu.get_barrier_semaphore()
    for peer in peers:
        pl.semaphore_signal(barrier, inc=1, device_id=peer, device_id_type=MESH)
    pl.semaphore_wait(barrier, len(peers))


def _sibling_barrier(x, y, c):
    _peer_barrier([(x, y, 1 - c)])


def _pair_exchange(gs, idx, name, barrier_id):
    n = len(idx)

    def body(*refs):
        src, dst = refs[:n], refs[n:2 * n]
        send_sems, recv_sems = refs[2 * n:]
        x, y, c, _ = _place()
        _sibling_barrier(x, y, c)
        cps = []
        for i in range(n):
            h = BIG[idx[i]][2] // 2
            cps.append(_remote(src[i].at[:, :, pl.ds((1 - c) * h, h), :], dst[i], send_sems.at[i], recv_sems.at[i],
                               (x, y, 1 - c)))
            cps[-1].start()
        for cp in cps:
            cp.wait()

    return pl.pallas_call(
        body, name=name, in_specs=[ANY] * n, out_specs=[ANY] * n,
        out_shape=[_sds((4, BIG[w][1], BIG[w][2] // 2, BIG[w][3]), BF16) for w in idx],
        scratch_shapes=[pltpu.SemaphoreType.DMA((n,)), pltpu.SemaphoreType.DMA((n,))],
        compiler_params=pltpu.CompilerParams(collective_id=barrier_id),
    )(*gs)


def _pair_start(gs, idx, name, barrier_id):
    n = len(idx)

    def body(*refs):
        src, land = refs[:n], refs[n:2 * n]
        send_sems, recv_sems = refs[2 * n], refs[2 * n + 1]
        token = refs[4 * n + 2]
        x, y, c, _ = _place()
        _sibling_barrier(x, y, c)
        for i in range(n):
            h = BIG[idx[i]][2] // 2
            _remote(src[i].at[:, :, pl.ds((1 - c) * h, h), :], land[i], send_sems.at[i], recv_sems.at[i],
                    (x, y, 1 - c)).start()
        token[...] = jnp.zeros(TOKEN, F32)

    lands = [lax.empty((4, BIG[w][1], BIG[w][2] // 2, BIG[w][3]), BF16) for w in idx]
    arrays = list(gs) + lands
    outs = pl.pallas_call(
        body, name=name, in_specs=[HBM] * (2 * n),
        out_specs=(SEM, SEM) + (HBM,) * (2 * n) + (pl.BlockSpec(memory_space=pltpu.VMEM),),
        out_shape=(pltpu.SemaphoreType.DMA((n,)), pltpu.SemaphoreType.DMA((n,)))
        + tuple(pltpu.HBM(a.shape, a.dtype) for a in arrays) + (_sds(TOKEN, F32),),
        input_output_aliases={i: 2 + i for i in range(2 * n)},
        compiler_params=pltpu.CompilerParams(has_side_effects=EFFECT, collective_id=barrier_id),
    )(*[pltpu.with_memory_space_constraint(a, pltpu.HBM) for a in arrays])
    return outs[0], outs[1], list(outs[2:2 + n]), list(outs[2 + n:2 + 2 * n]), outs[2 + 2 * n]


def _pair_wait(send_sems, recv_sems, gs, lands, after, idx, name):
    n = len(idx)

    def body(*refs):
        src, land = refs[:n], refs[n:2 * n]
        send_sems, recv_sems = refs[2 * n], refs[2 * n + 1]
        x, y, c, _ = _place()
        for i in range(n):
            h = BIG[idx[i]][2] // 2
            cp = _remote(src[i].at[:, :, pl.ds((1 - c) * h, h), :], land[i], send_sems.at[i], recv_sems.at[i],
                         (x, y, 1 - c))
            cp.wait_send()
            cp.wait_recv()

    arrays = list(gs) + list(lands)
    outs = pl.pallas_call(
        body, name=name, in_specs=[HBM] * (2 * n) + [SEM, SEM] + [ANY] * len(after), out_specs=(HBM,) * (2 * n),
        out_shape=tuple(pltpu.HBM(a.shape, a.dtype) for a in arrays),
        input_output_aliases={i: i for i in range(2 * n)},
        compiler_params=pltpu.CompilerParams(has_side_effects=EFFECT),
    )(*arrays, send_sems, recv_sems, *after)
    return list(outs[:n]), list(outs[n:])


def _pair_sums(place, gs, r1s, idx, name):
    n = len(idx)
    dims = [(BIG[w][1], BIG[w][2] // 2, BIG[w][3]) for w in idx]

    def body(pref, *refs):
        for i in range(n):
            refs[2 * n + i][...] = (refs[i][...] + refs[n + i][...].astype(F32)).astype(BF16)

    mine = [pl.BlockSpec((1, k, h, cdim), lambda s, pref: (s, 0, pref[0], 0)) for k, h, cdim in dims]
    whole = [pl.BlockSpec((1, k, h, cdim), lambda s, pref: (s, 0, 0, 0)) for k, h, cdim in dims]
    grid_spec = pltpu.PrefetchScalarGridSpec(num_scalar_prefetch=1, grid=(4,), in_specs=mine + whole, out_specs=whole)
    return pl.pallas_call(
        body, name=name, grid_spec=grid_spec, out_shape=[_sds((4, k, h, cdim), BF16) for k, h, cdim in dims],
        compiler_params=_params(("parallel",)),
    )(place, *gs, *r1s)


def _chip_start(ps, idx, name, barrier_id):
    n = len(idx)

    def body(*refs):
        src, land = refs[:n], refs[n:2 * n]
        send_sems, recv_sems = refs[2 * n], refs[2 * n + 1]
        token = refs[4 * n + 2]
        x, y, c, chips = _place()
        _peer_barrier([(px, py, c) for px, py in chips])
        for j, (px, py) in enumerate(chips):
            for i in range(n):
                _remote(src[i].at[2 * px + py], land[i].at[j], send_sems.at[j * n + i], recv_sems.at[j * n + i],
                        (px, py, c)).start()
        token[...] = jnp.zeros(TOKEN, F32)

    lands = [lax.empty((3,) + p.shape[1:], BF16) for p in ps]
    outs = pl.pallas_call(
        body, name=name, in_specs=[HBM] * (2 * n),
        out_specs=(SEM, SEM) + (HBM,) * (2 * n) + (pl.BlockSpec(memory_space=pltpu.VMEM),),
        out_shape=(pltpu.SemaphoreType.DMA((3 * n,)), pltpu.SemaphoreType.DMA((3 * n,)))
        + tuple(pltpu.HBM(a.shape, a.dtype) for a in list(ps) + lands) + (_sds(TOKEN, F32),),
        input_output_aliases={i: 2 + i for i in range(2 * n)},
        compiler_params=pltpu.CompilerParams(has_side_effects=EFFECT, collective_id=barrier_id),
    )(*[pltpu.with_memory_space_constraint(a, pltpu.HBM) for a in list(ps) + lands])
    return outs[0], outs[1], list(outs[2:2 + n]), list(outs[2 + n:2 + 2 * n]), outs[2 + 2 * n]


def _chip_wait(send_sems, recv_sems, ps, lands, after, idx, name):
    n = len(idx)

    def body(*refs):
        src, land = refs[:n], refs[n:2 * n]
        send_sems, recv_sems = refs[2 * n], refs[2 * n + 1]
        x, y, c, chips = _place()
        for j, (px, py) in enumerate(chips):
            for i in range(n):
                cp = _remote(src[i].at[2 * px + py], land[i].at[j], send_sems.at[j * n + i], recv_sems.at[j * n + i],
                             (px, py, c))
                cp.wait_send()
                cp.wait_recv()

    arrays = list(ps) + list(lands)
    outs = pl.pallas_call(
        body, name=name, in_specs=[HBM] * (2 * n) + [SEM, SEM] + [ANY] * len(after), out_specs=(HBM,) * (2 * n),
        out_shape=tuple(pltpu.HBM(a.shape, a.dtype) for a in arrays),
        input_output_aliases={i: i for i in range(2 * n)},
        compiler_params=pltpu.CompilerParams(has_side_effects=EFFECT),
    )(*arrays, send_sems, recv_sems, *after)
    return list(outs[n:])


def _chip_sums(place, gs, r1s, r2s, idx, name):
    n = len(idx)
    dims = [(BIG[w][1], BIG[w][2] // 4, BIG[w][3]) for w in idx]

    def body(pref, *refs):
        for i in range(n):
            acc = refs[i][0] + refs[n + i][0].astype(F32)
            for j in range(3):
                acc = acc + refs[2 * n + i][j].astype(F32)
            refs[3 * n + i][...] = acc

    in_specs = ([pl.BlockSpec((1, k, q, cdim), lambda t, pref: (pref[1], 0, pref[0] * 2 + t, 0)) for k, q, cdim in dims]
                + [pl.BlockSpec((1, k, q, cdim), lambda t, pref: (pref[1], 0, t, 0)) for k, q, cdim in dims]
                + [pl.BlockSpec((3, k, q, cdim), lambda t, pref: (0, 0, t, 0)) for k, q, cdim in dims])
    out_specs = [pl.BlockSpec((k, q, cdim), lambda t, pref: (0, pref[0] * 2 + t, 0)) for k, q, cdim in dims]
    grid_spec = pltpu.PrefetchScalarGridSpec(num_scalar_prefetch=1, grid=(2,), in_specs=in_specs, out_specs=out_specs)
    return pl.pallas_call(
        body, name=name, grid_spec=grid_spec, out_shape=[_sds(BIG[w][1:], F32) for w in idx],
        compiler_params=_params(("parallel",)),
    )(place, *gs, *r1s, *r2s)


def _pair_gather(hs, idx, name, barrier_id):
    n = len(idx)

    def body(*refs):
        dst = refs[n:2 * n]
        send_sems, recv_sems = refs[2 * n:]
        x, y, c, _ = _place()
        _sibling_barrier(x, y, c)
        cps = []
        for i in range(n):
            mine = _half(dst[i], idx[i], c)
            cps.append(_remote(mine, mine, send_sems.at[i], recv_sems.at[i], (x, y, 1 - c)))
            cps[-1].start()
        for i in range(n):
            theirs = _half(dst[i], idx[i], 1 - c)
            _remote(theirs, theirs, send_sems.at[i], recv_sems.at[i], (x, y, 1 - c)).wait_recv()
        for cp in cps:
            cp.wait_send()

    return pl.pallas_call(
        body, name=name, in_specs=[ANY] * n, out_specs=[ANY] * n,
        out_shape=[_sds(BIG[w][1:], F32) for w in idx],
        input_output_aliases={i: i for i in range(n)},
        scratch_shapes=[pltpu.SemaphoreType.DMA((n,)), pltpu.SemaphoreType.DMA((n,))],
        compiler_params=pltpu.CompilerParams(collective_id=barrier_id),
    )(*hs)


SMALL_ROWS = 40


def _adamw_math(w, g, m, v):
    m = ADAM_B1 * m + (1.0 - ADAM_B1) * g
    v = ADAM_B2 * v + (1.0 - ADAM_B2) * (g * g)
    m_hat = m / (1.0 - ADAM_B1 ** ADAM_STEP)
    v_hat = v / (1.0 - ADAM_B2 ** ADAM_STEP)
    delta = -ADAM_LR * (m_hat / (jnp.sqrt(v_hat) + ADAM_EPS) + ADAM_WD * w)
    return delta, m, v


def _small_start(pack, after):
    def body(pack_ref, land_ref, after_ref, send_sems, recv_sems, pack_thru, land_thru, token):
        x, y, c, _ = _place()
        for r in range(1, 8):
            peer = (x if not r & 4 else 1 - x, y if not r & 2 else 1 - y, c if not r & 1 else 1 - c)
            _remote(pack_ref, land_ref.at[r - 1], send_sems.at[r - 1], recv_sems.at[r - 1], peer).start()
        token[...] = jnp.zeros(TOKEN, F32)

    land = lax.empty((7, SMALL_ROWS, D), F32)
    outs = pl.pallas_call(
        body, name="small_start", in_specs=[HBM, HBM, ANY],
        out_specs=(SEM, SEM, HBM, HBM, pl.BlockSpec(memory_space=pltpu.VMEM)),
        out_shape=(pltpu.SemaphoreType.DMA((7,)), pltpu.SemaphoreType.DMA((7,)), pltpu.HBM(pack.shape, F32),
                   pltpu.HBM(land.shape, F32), _sds(TOKEN, F32)),
        input_output_aliases={0: 2, 1: 3},
        compiler_params=pltpu.CompilerParams(has_side_effects=EFFECT),
    )(pltpu.with_memory_space_constraint(pack, pltpu.HBM), pltpu.with_memory_space_constraint(land, pltpu.HBM), after)
    return outs


def _small_wait(send_sems, recv_sems, pack, land, after):
    def body(pack_ref, land_ref, send_sems, recv_sems, *rest):
        x, y, c, _ = _place()
        for r in range(1, 8):
            peer = (x if not r & 4 else 1 - x, y if not r & 2 else 1 - y, c if not r & 1 else 1 - c)
            cp = _remote(pack_ref, land_ref.at[r - 1], send_sems.at[r - 1], recv_sems.at[r - 1], peer)
            cp.wait_send()
            cp.wait_recv()

    return pl.pallas_call(
        body, name="small_wait", in_specs=[HBM, HBM, SEM, SEM] + [ANY] * len(after), out_specs=(HBM, HBM),
        out_shape=(pltpu.HBM(pack.shape, F32), pltpu.HBM(land.shape, F32)),
        input_output_aliases={0: 0, 1: 1},
        compiler_params=pltpu.CompilerParams(has_side_effects=EFFECT),
    )(pack, land, send_sems, recv_sems, *after)


def _small_update(place, pack, land, ws, ms, vs):
    n = len(ws)

    def body(pref, pack_ref, land_ref, *refs):
        chip = pref[1]
        me = 2 * chip + pref[0]
        own = pack_ref[...]
        tot = None
        for dev in range(8):
            r = jnp.bitwise_xor(me, dev)
            term = jnp.where(r == 0, own, land_ref[jnp.maximum(r - 1, 0)])
            tot = term if tot is None else tot + term
        out, buf = refs[3 * n:-1], refs[-1]
        buf[...] = tot
        g_conv = jnp.zeros((3, SH_O), F32)
        for s in range(4):
            g_conv = g_conv + jnp.where(chip == s, buf[24:27, s * SH_O:(s + 1) * SH_O], 0.0)
        gs = [buf[0:2, :], buf[8:10, :], buf[16:17, :], g_conv]
        out[0][...] = buf[32:33, 0:128]
        for i in range(n):
            d, nm, nv = _adamw_math(refs[i][...], gs[i], refs[n + i][...], refs[2 * n + i][...])
            out[1 + i][...] = gs[i]
            out[1 + n + i][...] = d
            out[1 + 2 * n + i][...] = nm
            out[1 + 3 * n + i][...] = nv

    def full(shape):
        nd = len(shape)
        return pl.BlockSpec(shape, lambda i, pref: (0,) * nd)

    specs = [full(w.shape) for w in ws]
    grid_spec = pltpu.PrefetchScalarGridSpec(
        num_scalar_prefetch=1, grid=(1,),
        in_specs=[full(pack.shape), full(land.shape)] + specs * 3, out_specs=[full((1, 128))] + specs * 4,
        scratch_shapes=[pltpu.VMEM((SMALL_ROWS, D), F32)])
    outs = pl.pallas_call(
        body, name="small_update", grid_spec=grid_spec,
        out_shape=[_sds((1, 128), F32)] + [_sds(w.shape, F32) for w in ws] * 4,
        compiler_params=_params(("arbitrary",)),
    )(place, pack, land, *ws, *ms, *vs)
    return outs[0], outs[1:1 + n], outs[1 + n:1 + 2 * n], outs[1 + 2 * n:1 + 3 * n], outs[1 + 3 * n:]


def _adamw_layer(ws, gs, ms, vs, idx, name):
    n = len(idx)
    dims = [(BIG[w][1], BIG[w][2] // 4, BIG[w][3]) for w in idx]

    def body(*refs):
        for i in range(n):
            gv = refs[n + i][...]
            d, nm, nv = _adamw_math(refs[i][...], gv, refs[2 * n + i][...], refs[3 * n + i][...])
            refs[4 * n + i][...] = d
            refs[5 * n + i][...] = nm
            refs[6 * n + i][...] = nv
            refs[7 * n + i][...] = gv

    specs = [pl.BlockSpec((k, q, cdim), lambda t: (0, t, 0)) for k, q, cdim in dims]
    outs = pl.pallas_call(
        body, name=name, grid=(4,), in_specs=specs * 4, out_specs=specs * 4,
        out_shape=[_sds(BIG[w][1:], F32) for w in idx] * 4,
        compiler_params=_params(("parallel",)),
    )(*ws, *gs, *ms, *vs)
    return [tuple(outs[j * n + i] for j in range(4)) for i in range(n)]


def _pad_rows(a, rows):
    return jnp.pad(a, ((0, rows - a.shape[0]), (0, 0)))


def kernel(x, mem, positions, norm_g, mem_norm_g, w_mem_kv, attn_w_in, attn_w_out, conv_w_in, conv_w, conv_w_out, final_g, loss_target, m_norm_g, m_mem_norm_g, m_w_mem_kv, m_attn_w_in, m_attn_w_out, m_conv_w_in, m_conv_w, m_conv_w_out, m_final_g, v_norm_g, v_mem_norm_g, v_w_mem_kv, v_attn_w_in, v_attn_w_out, v_conv_w_in, v_conv_w, v_conv_w_out, v_final_g):
    mx, my, mc = lax.axis_index("x"), lax.axis_index("y"), lax.axis_index("c")
    place = jnp.stack([mc, 2 * mx + my]).astype(jnp.int32)

    w_big = [w_mem_kv, attn_w_in, attn_w_out, conv_w_in, conv_w_out]
    m_big = [m_w_mem_kv, m_attn_w_in, m_attn_w_out, m_conv_w_in, m_conv_w_out]
    v_big = [v_w_mem_kv, v_attn_w_in, v_attn_w_out, v_conv_w_in, v_conv_w_out]
    first, rest = (1,), (0, 2, 3, 4)
    wb1 = _cast_weights(place, [w_big[i] for i in first], place, first, "cast_w_in_a")
    a1_send, a1_recv, a1_bufs, a1_token = _gather_start(wb1, place, first, "gather_a1_start", 4)
    wbr = _cast_weights(place, [w_big[i] for i in rest], a1_token, rest, "cast_weights")
    r_send, r_recv, r_bufs, gb_token = _gather_start(wbr, a1_token, rest, "gather_rest_start", 5)
    a2_send, a2_recv, gb_send, gb_recv = r_send, r_recv, r_send, r_recv
    a2_bufs, gb_bufs = r_bufs[:2], r_bufs[2:]
    started, rest = rest, (0, 2)

    xs, tgt = x[0], loss_target[0]
    g0, g1 = norm_g[0:1], norm_g[1:2]
    rc, rs1, rs2 = _rope_tables(positions[0].astype(F32).reshape(S, 1), gb_token)
    a1_bufs = _gather_wait(a1_send, a1_recv, a1_bufs, [rc], first, "gather_a1_wait")
    w_in_a = _gather_forward(a1_bufs, first, "gather_a1_forward", 0)[0].reshape(4, D, SH_A)
    hn0, q, k, v, qm0, z0 = _in_proj_a(xs, g0, w_in_a, rc, rs1, rs2, gb_token)
    a2_bufs = _gather_wait(a2_send, a2_recv, a2_bufs, [q], rest, "gather_a2_wait", started)
    f2_send, f2_recv, a2_bufs, f2_token = _forward_start(a2_bufs, None, q, rest, "forward_a2_start")
    fwd = [_attn_fwd(q, k, v, 0, f2_token)]
    fwd.append(_attn_fwd(q, k, v, 1, fwd[0][0]))
    cw_own = _pad_rows(conv_w[0], CW_ROWS)
    gb_bufs = _gather_wait(gb_send, gb_recv, gb_bufs, [fwd[1][0]], LAYER_B, "gather_b_wait", started)
    fb_send, fb_recv, gb_bufs, fb_token = _forward_start(gb_bufs, cw_own, fwd[1][0], LAYER_B, "forward_b_start")
    fwd.append(_attn_fwd(q, k, v, 2, fb_token))
    os_, ls, lss = [f[0] for f in fwd], [f[1] for f in fwd], [f[2] for f in fwd]
    wkv_f, w_out_a = _forward_wait(f2_send, f2_recv, a2_bufs, [os_[2]], rest, False, "forward_a2_wait")
    w_out_a = w_out_a.reshape(4, BR_A, SH_O)
    memn, kv = _mem_fwd(mem[0], mem_norm_g, wkv_f)
    h1 = _attn_out(os_, ls, qm0, kv[0], z0, xs, w_out_a)

    w_in_b, w_out_b, _, cw_f = _forward_wait(fb_send, fb_recv, gb_bufs, [h1], LAYER_B, True, "forward_b_wait")
    w_in_b = w_in_b.reshape(4, D, SH_B)
    w_out_b = w_out_b.reshape(BR_B, D)
    cw_f = lax.dynamic_update_slice(cw_f, cw_own[None], (2 * mx + my, 0, 0))
    cw8 = cw_f.transpose(1, 0, 2).reshape(CW_ROWS, D)
    hn1, bg, cg, u, qm1, z1 = _in_proj_b(h1, g1, w_in_b)
    dh2, loss_part, dfg = _conv_out_loss(bg, cg, u, cw8, qm1, kv[1], z1, h1, w_out_b, final_g.reshape(1, D), tgt)

    dproj_b, dw_out_b, dcw, dkv1, dw_out_b16 = _conv_bwd(dh2, bg, cg, u, cw8, qm1, kv[1], z1, w_out_b)
    dw_in_b, dw_in_b16 = _w_in_grad(hn1, dproj_b, IN_B, "w_in_b_grad")
    gs_b = [dw_in_b.reshape(4, 1, D, SH_B), dw_out_b.reshape(4, 1, BR_B // 4, D)]
    gb_b = [dw_in_b16.reshape(4, 1, D, SH_B), dw_out_b16.reshape(4, 1, BR_B // 4, D)]
    pb_send, pb_recv, gb_b, pb_land, pb_token = _pair_start(gb_b, LAYER_B, "pair_b_start", 6)
    dh1, dg1 = _in_proj_bwd(dproj_b, w_in_b, h1, g1, dh2, pb_token, IN_B, "in_proj_b_bwd")
    _, r1_b = _pair_wait(pb_send, pb_recv, gb_b, pb_land, [dh1], LAYER_B, "pair_b_wait")
    ps_b = _pair_sums(place, gs_b, r1_b, LAYER_B, "pair_sums_b")
    cb_send, cb_recv, cb_src, cb_land, cb_token = _chip_start(ps_b, LAYER_B, "chip_b_start", 7)

    outs = _attn_out_bwd(dh1, os_, ls, qm0, kv[0], z0, w_out_a, cb_token)
    dos, dds, dqm, dz, dw_out_a, dkv0, dw_out_a16 = outs[0:3], outs[3:6], outs[6], outs[7], outs[8], outs[9], outs[10]
    bwd = [_attn_bwd(q, k, v, dos[g], lss[g], dds[g], g) for g in range(3)]
    dproj_a = _qkv_bwd([b[0] for b in bwd], [b[1] for b in bwd], [b[2] for b in bwd], dqm, dz, rc, rs1, rs2)
    dw_in_a, dw_in_a16 = _w_in_grad(hn0, dproj_a, IN_A, "w_in_a_grad")
    dwkv, dwkv16, dmg = _mem_bwd(mem[0], mem_norm_g, memn, wkv_f, dkv0, dkv1)

    gs_a = [dwkv, dw_in_a.reshape(4, 1, D, SH_A), dw_out_a.reshape(4, 1, BR_A, SH_O)]
    r1_a = _pair_exchange([dwkv16, dw_in_a16.reshape(4, 1, D, SH_A), dw_out_a16.reshape(4, 1, BR_A, SH_O)], LAYER_A,
                          "pair_exchange_a", 1)
    ps_a = _pair_sums(place, gs_a, r1_a, LAYER_A, "pair_sums_a")
    ca_send, ca_recv, ca_src, ca_land, ca_token = _chip_start(ps_a, LAYER_A, "chip_a_start", 8)

    gx, dg0 = _in_proj_bwd(dproj_a, w_in_a, xs, g0, dh1, ca_token, IN_A, "in_proj_a_bwd")
    pack = jnp.concatenate([_pad_rows(jnp.concatenate([dg0, dg1], axis=0), 8), _pad_rows(dmg, 8), _pad_rows(dfg, 8),
                            dcw, _pad_rows(jnp.pad(loss_part, ((0, 0), (0, D - 128))), 8)], axis=0)
    sm_send, sm_recv, pack, sm_land, sm_token = _small_start(pack, ca_token)
    r2_b = _chip_wait(cb_send, cb_recv, cb_src, cb_land, [ca_token], LAYER_B, "chip_b_wait")
    hs_b = _chip_sums(place, gs_b, r1_b, r2_b, LAYER_B, "chip_sums_b")
    g_b = _pair_gather(hs_b, LAYER_B, "pair_gather_b", 2)
    upd_b = _adamw_layer([w_big[w] for w in LAYER_B], g_b, [m_big[w] for w in LAYER_B], [v_big[w] for w in LAYER_B],
                         LAYER_B, "adamw_b")
    r2_a = _chip_wait(ca_send, ca_recv, ca_src, ca_land, [gx, upd_b[0][0], upd_b[1][0], sm_token], LAYER_A,
                      "chip_a_wait")
    hs_a = _chip_sums(place, gs_a, r1_a, r2_a, LAYER_A, "chip_sums_a")
    g_a = _pair_gather(hs_a, LAYER_A, "pair_gather_a", 3)
    upd_a = _adamw_layer([w_big[w] for w in LAYER_A], g_a, [m_big[w] for w in LAYER_A], [v_big[w] for w in LAYER_A],
                         LAYER_A, "adamw_a")
    upd = upd_a + upd_b
    g_big = [u[3] for u in upd]
    pack, sm_land = _small_wait(sm_send, sm_recv, pack, sm_land, [r2_a[0]])
    sw = [norm_g, mem_norm_g, final_g.reshape(1, D), conv_w[0]]
    sm = [m_norm_g, m_mem_norm_g, m_final_g.reshape(1, D), m_conv_w[0]]
    sv = [v_norm_g, v_mem_norm_g, v_final_g.reshape(1, D), v_conv_w[0]]
    loss_row, sg, sd, snm, snv = _small_update(place, pack, sm_land, sw, sm, sv)
    loss = loss_row[0, 0]
    g_norm, g_memnorm, g_final, g_conv = sg

    def order(norm, memnorm, wkv, w_in_a, w_out_a, w_in_b, conv, w_out_b, final):
        return (norm, memnorm, wkv, w_in_a, w_out_a, w_in_b, conv.reshape(1, 3, SH_O), w_out_b, final.reshape(D))

    grads = order(g_norm, g_memnorm, g_big[0], g_big[1], g_big[2], g_big[3], g_conv, g_big[4], g_final)
    deltas = order(sd[0], sd[1], upd[0][0], upd[1][0], upd[2][0], upd[3][0], sd[3], upd[4][0], sd[2])
    new_m = order(snm[0], snm[1], upd[0][1], upd[1][1], upd[2][1], upd[3][1], snm[3], upd[4][1], snm[2])
    new_v = order(snv[0], snv[1], upd[0][2], upd[1][2], upd[2][2], upd[3][2], snv[3], upd[4][2], snv[2])
    return (loss, gx[None], *grads, *deltas, *new_m, *new_v)
```

```python
import functools

import numpy as np
import jax
import jax.numpy as jnp
from jax import lax
from jax.experimental import pallas as pl
from jax.experimental.pallas import tpu as pltpu

F32 = jnp.float32
BF16 = jnp.bfloat16

S = 2048
D = 1024
TM = 256
NT = S // TM
MX = 512
NX = S // MX
HD = 64
GW = 512
NQ = 3 * GW
MW = 256
NM = 256
IN_A = 3 * NQ + MW + GW + MW
IN_B = 3 * D + MW + D + MW
BR_A = GW + MW
BR_B = D + MW
SH_A = IN_A // 4
SH_B = IN_B // 4
SH_O = D // 4
QBLK = 128
DILATIONS = (1, 4, 16)
EPS = 1e-6
SCALE = HD ** -0.5
NEG = -1e30
ROPE_THETA = 500000.0

ADAM_LR = 0.001
ADAM_B1 = 0.9
ADAM_B2 = 0.999
ADAM_EPS = 1e-08
ADAM_WD = 0.01
ADAM_STEP = 10

VMEM_LIMIT_BYTES = 60 * 1024 * 1024


def _params(sem=None):
    if sem is None:
        return pltpu.CompilerParams(vmem_limit_bytes=VMEM_LIMIT_BYTES)
    return pltpu.CompilerParams(dimension_semantics=sem, vmem_limit_bytes=VMEM_LIMIT_BYTES)


def _full(shape):
    nd = len(shape)
    return pl.BlockSpec(shape, lambda *_: (0,) * nd)


def _rows(width, tm=TM):
    return pl.BlockSpec((tm, width), lambda i: (i, 0))


def _sds(shape, dtype):
    return jax.ShapeDtypeStruct(shape, dtype)


def _silu_parts(z):
    sig = 0.5 * jnp.tanh(0.5 * z) + 0.5
    return z * sig, sig * (1.0 + z * (1.0 - sig))


def _dot(a, b):
    return jnp.dot(a, b, preferred_element_type=F32)


def _dot_nt(a, b):
    return lax.dot_general(a, b, (((1,), (1,)), ((), ())), preferred_element_type=F32)


def _dot_tn(a, b):
    return lax.dot_general(a, b, (((0,), (0,)), ((), ())), preferred_element_type=F32)


def _rope_fwd(t, c, s1, s2):
    return t * c + pltpu.roll(t, 120, 1) * s1 + pltpu.roll(t, 8, 1) * s2


def _rope_bwd(g, c, s1, s2):
    return g * c + pltpu.roll(g * s1, 8, 1) + pltpu.roll(g * s2, 120, 1)


MEM_HEADS = MW // HD


def _stack_heads(x):
    head = lax.broadcasted_iota(jnp.int32, x.shape, 1) // HD
    return jnp.concatenate([jnp.where(head == h, x, 0.0) for h in range(MEM_HEADS)], axis=0).astype(BF16)


def _unstack_heads(x4):
    tm = x4.shape[0] // MEM_HEADS
    head = lax.broadcasted_iota(jnp.int32, (tm, MW), 1) // HD
    out = x4[:tm]
    for h in range(1, MEM_HEADS):
        out = jnp.where(head == h, x4[h * tm:(h + 1) * tm], out)
    return out


def _mem_attn(qm, kv):
    q4 = _stack_heads(qm.astype(F32))
    s = _dot_nt(q4, kv[:, :MW]) * SCALE
    e = jnp.exp(s - jnp.max(s, axis=-1, keepdims=True))
    p = e * (1.0 / jnp.sum(e, axis=-1, keepdims=True))
    return p, _unstack_heads(_dot(p.astype(BF16), kv[:, MW:])), q4


def _mem_attn_bwd(dmo, p, mo, q4, kv, dkv_ref):
    tm = dmo.shape[0]
    head = lax.broadcasted_iota(jnp.int32, dmo.shape, 1) // HD
    prod = dmo * mo
    delta = jnp.concatenate([jnp.sum(jnp.where(head == h, prod, 0.0), axis=-1, keepdims=True)
                             for h in range(MEM_HEADS)], axis=0)
    d4 = _stack_heads(dmo)
    ds = (p * (_dot_nt(d4, kv[:, MW:]) - delta) * SCALE).astype(BF16)
    dkv_ref[:, :MW] += _dot_tn(ds, q4)
    dkv_ref[:, MW:] += _dot_tn(p.astype(BF16), d4)
    return _unstack_heads(_dot(ds, kv[:, :MW]))


def _merge(o_refs, l_refs):
    ls = [r[...] for r in l_refs]
    m = jnp.maximum(jnp.maximum(ls[0], ls[1]), ls[2])
    es = [jnp.exp(l - m) for l in ls]
    inv = 1.0 / (es[0] + es[1] + es[2])
    ws = [e * inv for e in es]
    os_ = [r[...] for r in o_refs]
    mix = ws[0] * os_[0] + ws[1] * os_[1] + ws[2] * os_[2]
    return ws, mix


def _conv_taps(cg, u, cgp, up, first):
    a = cg * u
    ap = jnp.where(first, 0.0, cgp * up)
    row = lax.broadcasted_iota(jnp.int32, a.shape, 0)
    a1 = jnp.where(row == 0, ap[7:8, :], pltpu.roll(a, 1, 0))
    a2 = jnp.where(row == 0, ap[6:7, :], jnp.where(row == 1, ap[7:8, :], pltpu.roll(a, 2, 0)))
    return a, a1, a2


def _rope_tables(posf, after):
    half = 8
    invf = np.float32(ROPE_THETA) ** (-np.arange(half, dtype=np.float32) * np.float32(2.0 / 16))
    lane = np.arange(128)
    table = np.where((lane % HD) < 16, invf[lane % half], 0.0).astype(np.float32)[None, :]

    def body(pos_ref, invf_ref, c_ref, s1_ref, s2_ref):
        ang = pos_ref[...] * invf_ref[...]
        jm = lax.broadcasted_iota(jnp.int32, ang.shape, 1) & (HD - 1)
        cs = jnp.cos(ang)
        sn = jnp.sin(ang)
        c_ref[...] = jnp.where(jm < 16, cs, 1.0)
        s1_ref[...] = jnp.where(jm < 8, -sn, 0.0)
        s2_ref[...] = jnp.where((jm >= 8) & (jm < 16), sn, 0.0)

    out = _sds((S, 128), F32)
    return pl.pallas_call(
        functools.partial(_skip_arg, body, 2), name="rope_tables", grid=(NT,),
        in_specs=[_rows(1), _full((1, 128)), pl.BlockSpec(memory_space=pl.ANY)],
        out_specs=[_rows(128)] * 3, out_shape=[out] * 3,
        compiler_params=_params(("parallel",)),
    )(posf, jnp.asarray(table), after)


def _in_proj_a(x, g0, w_in, c, s1, s2, after):
    def body(x_ref, g_ref, w_ref, c_ref, s1_ref, s2_ref, hn_ref, q_ref, k_ref, v_ref, qm_ref, z_ref, proj):
        xf = x_ref[...]
        hn = xf * lax.rsqrt(jnp.mean(xf * xf, axis=-1, keepdims=True) + EPS) * g_ref[...]
        hb = hn.astype(BF16)
        hn_ref[...] = hb
        for s in range(4):
            proj[:, s * SH_A:(s + 1) * SH_A] = _dot(hb, w_ref[s])
        cc, a1, a2 = c_ref[...], s1_ref[...], s2_ref[...]
        for j in range(NQ // 128):
            q_ref[:, j * 128:(j + 1) * 128] = (
                _rope_fwd(proj[:, j * 128:(j + 1) * 128], cc, a1, a2) * SCALE).astype(BF16)
            k_ref[:, j * 128:(j + 1) * 128] = _rope_fwd(
                proj[:, NQ + j * 128:NQ + (j + 1) * 128], cc, a1, a2).astype(BF16)
        v_ref[...] = proj[:, 2 * NQ:3 * NQ].astype(BF16)
        qm_ref[...] = proj[:, 3 * NQ:3 * NQ + MW].astype(BF16)
        z_ref[...] = proj[:, 3 * NQ + MW:]

    return pl.pallas_call(
        functools.partial(_skip_arg, body, 6), name="in_proj_a", grid=(NT,),
        in_specs=[_rows(D), _full((1, D)), _full((4, D, SH_A)), _rows(128), _rows(128), _rows(128),
                  pl.BlockSpec(memory_space=pl.ANY)],
        out_specs=[_rows(D), _rows(NQ), _rows(NQ), _rows(NQ), _rows(MW), _rows(BR_A)],
        out_shape=[_sds((S, D), BF16), _sds((S, NQ), BF16), _sds((S, NQ), BF16), _sds((S, NQ), BF16),
                   _sds((S, MW), BF16), _sds((S, BR_A), F32)],
        scratch_shapes=[pltpu.VMEM((TM, IN_A), F32)],
        compiler_params=_params(("parallel",)),
    )(x, g0, w_in, c, s1, s2, after)


def _mem_fwd(mem, mg, wkv):
    def body(mem_ref, mg_ref, w_ref, memn_ref, kv_ref):
        mf = mem_ref[...]
        n = mf * lax.rsqrt(jnp.mean(mf * mf, axis=-1, keepdims=True) + EPS)
        for i in range(2):
            mn = (n * mg_ref[i:i + 1, :]).astype(BF16)
            memn_ref[i] = mn
            acc = _dot(mn[:, 0:NM], w_ref[0, i])
            for s in range(1, 4):
                acc += _dot(mn[:, s * NM:(s + 1) * NM], w_ref[s, i])
            kv_ref[i] = acc.astype(BF16)

    return pl.pallas_call(
        body, name="mem_fwd", grid=(1,),
        in_specs=[_full((NM, D)), _full((2, D)), _full((4, 2, NM, 2 * MW))],
        out_specs=[_full((2, NM, D)), _full((2, NM, 2 * MW))],
        out_shape=[_sds((2, NM, D), BF16), _sds((2, NM, 2 * MW), BF16)],
        compiler_params=_params(("arbitrary",)),
    )(mem, mg, wkv)


def _band_mask(j):
    qi = lax.broadcasted_iota(jnp.int32, (QBLK, 2 * QBLK), 0)
    kj = lax.broadcasted_iota(jnp.int32, (QBLK, 2 * QBLK), 1)
    dist = qi + QBLK - kj
    return (dist >= 0) & (dist <= QBLK) & ((kj >= QBLK) | (j > 0))


LANES = 128
NCHUNK = GW // LANES
FWD_UNROLL = 16
BWD_UNROLL = 16
CONV_CHUNK = 256


def _perm_matrix(d):
    n = TM // d
    p = np.zeros((TM, TM), np.float32)
    for r in range(d):
        for i in range(n):
            p[r * n + i, i * d + r] = 1.0
    return p


def _split_dot(p, x):
    hi = x.astype(BF16)
    lo = (x - hi.astype(F32)).astype(BF16)
    both = _dot(p, jnp.concatenate([hi, lo], axis=1))
    return both[:, :LANES] + both[:, LANES:]


def _pair_dot(p, a, b):
    both = _dot(p, jnp.concatenate([a, b], axis=1))
    return both[:, :LANES], both[:, LANES:]


def _tile_to_streams(y, dst, t, d):
    n, ln = TM // d, S // d
    for r in range(d):
        dst[r * ln + t * n:r * ln + (t + 1) * n, :] = y[r * n:(r + 1) * n].astype(dst.dtype)


def _tile_from_streams(src, t, d):
    n, ln = TM // d, S // d
    return jnp.concatenate([src[r * ln + t * n:r * ln + (t + 1) * n, :] for r in range(d)], axis=0)


def _head_masks():
    first = lax.broadcasted_iota(jnp.int32, (TM, LANES), 1) < HD
    return first, jnp.logical_not(first)


def _attn_fwd(q, k, v, g, after):
    d = DILATIONS[g]
    nb = S // d // QBLK
    perm = _perm_matrix(d)

    def body(q_ref, k_ref, v_ref, p_ref, pt_ref, o_ref, l_ref, ls_ref, q0, q1, ks, vs, os_):
        first, second = _head_masks()
        pm = p_ref[...]
        for t in range(NT):
            rows = slice(t * TM, (t + 1) * TM)
            if d == 1:
                qt = q_ref[rows, :].astype(F32)
            else:
                qt, kt = _pair_dot(pm, q_ref[rows, :], k_ref[rows, :])
                _tile_to_streams(kt, ks, t, d)
                if t % 2 == 0:
                    va, vb = _pair_dot(pm, v_ref[rows, :], v_ref[(t + 1) * TM:(t + 2) * TM, :])
                    _tile_to_streams(va, vs, t, d)
                    _tile_to_streams(vb, vs, t + 1, d)
            _tile_to_streams(jnp.where(first, qt, 0.0), q0, t, d)
            _tile_to_streams(jnp.where(second, qt, 0.0), q1, t, d)
        kref, vref = (k_ref, v_ref) if d == 1 else (ks, vs)
        oref, lref = (o_ref, l_ref) if d == 1 else (os_, ls_ref)

        def blk(b, carry):
            r0 = pl.multiple_of(b * QBLK, QBLK)
            p0 = pl.multiple_of(jnp.maximum(b - 1, 0) * QBLK, QBLK)
            kk = jnp.concatenate([kref[pl.ds(p0, QBLK), :], kref[pl.ds(r0, QBLK), :]], axis=0)
            vv = jnp.concatenate([vref[pl.ds(p0, QBLK), :], vref[pl.ds(r0, QBLK), :]], axis=0)
            valid = _band_mask(b & (nb - 1))
            acc, lse = [], []
            for qh in (q0, q1):
                s = jnp.where(valid, _dot_nt(qh[pl.ds(r0, QBLK), :], kk), NEG)
                m = jnp.max(s, axis=-1, keepdims=True)
                e = jnp.exp(s - m)
                l = jnp.sum(e, axis=-1, keepdims=True)
                acc.append(_dot(e.astype(BF16), vv) * (1.0 / l))
                lse.append(m + jnp.log(l))
            f = first[:QBLK]
            oref[pl.ds(r0, QBLK), :] = jnp.where(f, acc[0], acc[1])
            lref[pl.ds(r0, QBLK), :] = jnp.where(f, lse[0], lse[1])
            return carry

        lax.fori_loop(0, S // QBLK, blk, 0, unroll=FWD_UNROLL)
        if d > 1:
            ptm = pt_ref[...]
            for t in range(NT):
                rows = slice(t * TM, (t + 1) * TM)
                o_ref[rows, :] = _split_dot(ptm, _tile_from_streams(os_, t, d))
                l_ref[rows, :] = _split_dot(ptm, _tile_from_streams(ls_ref, t, d))

    qkv_spec = pl.BlockSpec((S, LANES), lambda c: (0, g * NCHUNK + c))
    out_spec = pl.BlockSpec((S, LANES), lambda c: (0, c))
    n_out = 2 if d == 1 else 3
    inner = body if d > 1 else functools.partial(_drop_arg, body, 7)
    outs = pl.pallas_call(
        functools.partial(_skip_arg, inner, 5), name=f"attn_fwd_g{g}", grid=(NCHUNK,),
        in_specs=[qkv_spec] * 3 + [_full((TM, TM))] * 2 + [pl.BlockSpec(memory_space=pl.ANY)],
        out_specs=[out_spec] * n_out, out_shape=[_sds((S, GW), F32)] * n_out,
        scratch_shapes=[pltpu.VMEM((S, LANES), BF16)] * 4 + [pltpu.VMEM((S, LANES), F32)],
        compiler_params=_params(("parallel",)),
    )(q, k, v, jnp.asarray(perm, BF16), jnp.asarray(perm.T, BF16), after)
    return (outs[0], outs[1], outs[1]) if d == 1 else tuple(outs)


def _drop_arg(body, pos, *refs):
    return body(*refs[:pos], None, *refs[pos:])


def _attn_out(os_, ls, qm, kv0, z, x, w_out):
    def body(o0, o1, o2, l0, l1, l2, qm_ref, kv_ref, z_ref, x_ref, w_ref, h_ref, ybuf):
        _, mix = _merge((o0, o1, o2), (l0, l1, l2))
        sz, _ = _silu_parts(z_ref[...])
        ybuf[:, :GW] = (mix * sz[:, :GW]).astype(BF16)
        _, mo, _ = _mem_attn(qm_ref[...], kv_ref[...])
        ybuf[:, GW:] = (mo * sz[:, GW:]).astype(BF16)
        yb = ybuf[...]
        for s in range(4):
            cs = slice(s * SH_O, (s + 1) * SH_O)
            h_ref[:, cs] = x_ref[:, cs] + _dot(yb, w_ref[s])

    return pl.pallas_call(
        body, name="attn_out", grid=(NX,),
        in_specs=[_rows(GW, MX)] * 6 + [_rows(MW, MX), _full((NM, 2 * MW)), _rows(BR_A, MX), _rows(D, MX),
                                        _full((4, BR_A, SH_O))],
        out_specs=_rows(D, MX), out_shape=_sds((S, D), F32),
        scratch_shapes=[pltpu.VMEM((MX, BR_A), BF16)],
        compiler_params=_params(("parallel",)),
    )(*os_, *ls, qm, kv0, z, x, w_out)


def _in_proj_b(h1, g1, w_in):
    def body(x_ref, g_ref, w_ref, hn_ref, bg_ref, cg_ref, u_ref, qm_ref, z_ref, proj):
        xf = x_ref[...]
        hn = xf * lax.rsqrt(jnp.mean(xf * xf, axis=-1, keepdims=True) + EPS) * g_ref[...]
        hb = hn.astype(BF16)
        hn_ref[...] = hb
        for s in range(4):
            proj[:, s * SH_B:(s + 1) * SH_B] = _dot(hb, w_ref[s])
        bg_ref[...] = proj[:, :D]
        cg_ref[...] = proj[:, D:2 * D]
        u_ref[...] = proj[:, 2 * D:3 * D]
        qm_ref[...] = proj[:, 3 * D:3 * D + MW].astype(BF16)
        z_ref[...] = proj[:, 3 * D + MW:]

    return pl.pallas_call(
        body, name="in_proj_b", grid=(NT,),
        in_specs=[_rows(D), _full((1, D)), _full((4, D, SH_B))],
        out_specs=[_rows(D), _rows(D), _rows(D), _rows(D), _rows(MW), _rows(BR_B)],
        out_shape=[_sds((S, D), BF16), _sds((S, D), F32), _sds((S, D), F32), _sds((S, D), F32),
                   _sds((S, MW), BF16), _sds((S, BR_B), F32)],
        scratch_shapes=[pltpu.VMEM((TM, IN_B), F32)],
        compiler_params=_params(("parallel",)),
    )(h1, g1, w_in)


def _prev8(width):
    return pl.BlockSpec((8, width), lambda i: (jnp.maximum(i * (MX // 8) - 1, 0), 0))


def _conv_out_loss(bg, cg, u, cw, qm, kv1, z, h1, w_out, fg, tgt):
    def body(bg_ref, cg_ref, u_ref, cgp_ref, up_ref, cw_ref, qm_ref, kv_ref, z_ref, h_ref, w_ref, fg_ref, t_ref,
             dh_ref, loss_ref, dfg_ref, ybuf):
        i = pl.program_id(0)
        a, a1, a2 = _conv_taps(cg_ref[...], u_ref[...], cgp_ref[...], up_ref[...], i == 0)
        conv = cw_ref[0:1, :] * a2 + cw_ref[1:2, :] * a1 + cw_ref[2:3, :] * a
        sz, _ = _silu_parts(z_ref[...])
        ybuf[:, :D] = (bg_ref[...] * conv * sz[:, :D]).astype(BF16)
        _, mo, _ = _mem_attn(qm_ref[...], kv_ref[...])
        ybuf[:, D:] = (mo * sz[:, D:]).astype(BF16)
        h2 = h_ref[...] + _dot(ybuf[...], w_ref[...])
        rstd = lax.rsqrt(jnp.mean(h2 * h2, axis=-1, keepdims=True) + EPS)
        n = h2 * rstd
        fgv = fg_ref[...]
        err = n * fgv - t_ref[...]
        dout = err * (1.0 / D)
        dn = dout * fgv
        dh_ref[...] = rstd * (dn - n * jnp.mean(dn * n, axis=-1, keepdims=True))

        @pl.when(i == 0)
        def _():
            loss_ref[...] = jnp.zeros_like(loss_ref)
            dfg_ref[...] = jnp.zeros_like(dfg_ref)

        loss_ref[...] += jnp.sum(err * err) * (0.5 / D)
        dfg_ref[...] += jnp.sum(dout * n, axis=0, keepdims=True)

    return pl.pallas_call(
        body, name="conv_out_loss", grid=(NX,),
        in_specs=[_rows(D, MX), _rows(D, MX), _rows(D, MX), _prev8(D), _prev8(D), _full((8, D)), _rows(MW, MX),
                  _full((NM, 2 * MW)), _rows(BR_B, MX), _rows(D, MX), _full((BR_B, D)), _full((1, D)), _rows(D, MX)],
        out_specs=[_rows(D, MX), _full((1, 128)), _full((1, D))],
        out_shape=[_sds((S, D), F32), _sds((1, 128), F32), _sds((1, D), F32)],
        scratch_shapes=[pltpu.VMEM((MX, BR_B), BF16)],
        compiler_params=_params(("arbitrary",)),
    )(bg, cg, u, cg, u, cw, qm, kv1, z, h1, w_out, fg, tgt)


def _conv_bwd(dh2, bg, cg, u, cw, qm, kv1, z, w_out):
    rev = lambda i: (NX - 1 - i, 0)
    rows = lambda w: pl.BlockSpec((MX, w), rev)
    prev8 = pl.BlockSpec((8, D), lambda i: (jnp.maximum((NX - 1 - i) * (MX // 8) - 1, 0), 0))

    def body(dh_ref, bg_ref, cg_ref, u_ref, cgp_ref, up_ref, cw_ref, qm_ref, kv_ref, z_ref, w_ref,
             dproj_ref, dw_ref, dcw_ref, dkv_ref, dwb_ref, ybuf, carry):
        i = pl.program_id(0)

        @pl.when(i == 0)
        def _():
            dw_ref[...] = jnp.zeros_like(dw_ref)
            dcw_ref[...] = jnp.zeros_like(dcw_ref)
            dkv_ref[...] = jnp.zeros_like(dkv_ref)
            carry[...] = jnp.zeros_like(carry)

        dhb = dh_ref[...].astype(BF16)
        dy = _dot_nt(dhb, w_ref[...])
        kvv = kv_ref[...]
        p, mo, q4 = _mem_attn(qm_ref[...], kvv)
        szm, dszm = _silu_parts(z_ref[:, D:])
        ybuf[:, D:] = (mo * szm).astype(BF16)
        dym = dy[:, D:]
        dproj_ref[:, 3 * D + MW + D:] = (dym * mo * dszm).astype(BF16)
        first_tile = i == NX - 1
        for c in range(D // CONV_CHUNK):
            cs = slice(c * CONV_CHUNK, (c + 1) * CONV_CHUNK)
            bgv, cgv, uv = bg_ref[:, cs], cg_ref[:, cs], u_ref[:, cs]
            a, a1, a2 = _conv_taps(cgv, uv, cgp_ref[:, cs], up_ref[:, cs], first_tile)
            w0, w1, w2 = cw_ref[0:1, cs], cw_ref[1:2, cs], cw_ref[2:3, cs]
            conv = w0 * a2 + w1 * a1 + w2 * a
            mix = bgv * conv
            sz, dsz = _silu_parts(z_ref[:, cs])
            ybuf[:, cs] = (mix * sz).astype(BF16)
            dyc = dy[:, cs]
            dproj_ref[:, 3 * D + MW + c * CONV_CHUNK:3 * D + MW + (c + 1) * CONV_CHUNK] = (
                dyc * mix * dsz).astype(BF16)
            dmix = dyc * sz
            dproj_ref[:, cs] = (dmix * conv).astype(BF16)
            dc = dmix * bgv
            nxt = carry[:, cs]
            row = lax.broadcasted_iota(jnp.int32, dc.shape, 0)
            dc1 = jnp.where(row == MX - 1, nxt[0:1, :], pltpu.roll(dc, MX - 1, 0))
            dc2 = jnp.where(row == MX - 2, nxt[0:1, :],
                            jnp.where(row == MX - 1, nxt[1:2, :], pltpu.roll(dc, MX - 2, 0)))
            carry[:, cs] = dc[0:8, :]
            da = w2 * dc + w1 * dc1 + w0 * dc2
            dproj_ref[:, D + c * CONV_CHUNK:D + (c + 1) * CONV_CHUNK] = (da * uv).astype(BF16)
            dproj_ref[:, 2 * D + c * CONV_CHUNK:2 * D + (c + 1) * CONV_CHUNK] = (da * cgv).astype(BF16)
            dcw_ref[0:1, cs] += jnp.sum(dc * a2, axis=0, keepdims=True)
            dcw_ref[1:2, cs] += jnp.sum(dc * a1, axis=0, keepdims=True)
            dcw_ref[2:3, cs] += jnp.sum(dc * a, axis=0, keepdims=True)
        dw_ref[...] += _dot_tn(ybuf[...], dhb)
        dproj_ref[:, 3 * D:3 * D + MW] = _mem_attn_bwd(dym * szm, p, mo, q4, kvv, dkv_ref).astype(BF16)

        @pl.when(i == NX - 1)
        def _():
            dwb_ref[...] = dw_ref[...].astype(BF16)

    return pl.pallas_call(
        body, name="conv_bwd", grid=(NX,),
        in_specs=[rows(D), rows(D), rows(D), rows(D), prev8, prev8, _full((8, D)), rows(MW),
                  _full((NM, 2 * MW)), rows(BR_B), _full((BR_B, D))],
        out_specs=[rows(IN_B), _full((BR_B, D)), _full((8, D)), _full((NM, 2 * MW)), _full((BR_B, D))],
        out_shape=[_sds((S, IN_B), BF16), _sds((BR_B, D), F32), _sds((8, D), F32), _sds((NM, 2 * MW), F32),
                   _sds((BR_B, D), BF16)],
        scratch_shapes=[pltpu.VMEM((MX, BR_B), BF16), pltpu.VMEM((8, D), F32)],
        compiler_params=_params(("arbitrary",)),
    )(dh2, bg, cg, u, cg, u, cw, qm, kv1, z, w_out)


def _in_proj_bwd(dproj, w_in, xin, g, dres, after, width, name):
    sh = width // 4

    def body(dp_ref, w_ref, x_ref, g_ref, dr_ref, dx_ref, dg_ref):
        i = pl.program_id(0)
        dhn = _dot_nt(dp_ref[:, 0:sh], w_ref[0])
        for s in range(1, 4):
            dhn += _dot_nt(dp_ref[:, s * sh:(s + 1) * sh], w_ref[s])
        xf = x_ref[...]
        rstd = lax.rsqrt(jnp.mean(xf * xf, axis=-1, keepdims=True) + EPS)
        n = xf * rstd
        dn = dhn * g_ref[...]
        dx_ref[...] = dr_ref[...] + rstd * (dn - n * jnp.mean(dn * n, axis=-1, keepdims=True))

        @pl.when(i == 0)
        def _():
            dg_ref[...] = jnp.zeros_like(dg_ref)

        dg_ref[...] += jnp.sum(dhn * n, axis=0, keepdims=True)

    return pl.pallas_call(
        functools.partial(_skip_arg, body, 5), name=name, grid=(NT,),
        in_specs=[_rows(width), _full((4, D, sh)), _rows(D), _full((1, D)), _rows(D), pl.BlockSpec(memory_space=pl.ANY)],
        out_specs=[_rows(D), _full((1, D))],
        out_shape=[_sds((S, D), F32), _sds((1, D), F32)],
        compiler_params=_params(("arbitrary",)),
    )(dproj, w_in, xin, g, dres, after)


def _w_in_grad(hn, dproj, width, name):
    sh = width // 4

    def body(hn_ref, dp_ref, dw_ref, dwb_ref):
        dw = _dot_tn(hn_ref[...], dp_ref[...])
        dw_ref[0] = dw
        dwb_ref[0] = dw.astype(BF16)

    spec = pl.BlockSpec((1, D, sh), lambda s: (s, 0, 0))
    return pl.pallas_call(
        body, name=name, grid=(4,),
        in_specs=[_full((S, D)), pl.BlockSpec((S, sh), lambda s: (0, s))],
        out_specs=[spec, spec], out_shape=[_sds((4, D, sh), F32), _sds((4, D, sh), BF16)],
        compiler_params=_params(("parallel",)),
    )(hn, dproj)


def _attn_out_bwd(dh1, os_, ls, qm, kv0, z, w_out, after):
    ones_bd = np.kron(np.eye(GW // HD, dtype=np.float32), np.ones((HD, HD), np.float32))

    def body(dh_ref, o0, o1, o2, l0, l1, l2, qm_ref, kv_ref, z_ref, w_ref, bd_ref,
             do0, do1, do2, dd0, dd1, dd2, dqm_ref, dz_ref, dw_ref, dkv_ref, dwb_ref, ybuf):
        i = pl.program_id(0)

        @pl.when(i == 0)
        def _():
            dw_ref[...] = jnp.zeros_like(dw_ref)
            dkv_ref[...] = jnp.zeros_like(dkv_ref)

        ws, mix = _merge((o0, o1, o2), (l0, l1, l2))
        sz, dsz = _silu_parts(z_ref[...])
        kvv = kv_ref[...]
        p, mo, q4 = _mem_attn(qm_ref[...], kvv)
        ybuf[:, :GW] = (mix * sz[:, :GW]).astype(BF16)
        ybuf[:, GW:] = (mo * sz[:, GW:]).astype(BF16)
        yb = ybuf[...]
        dh = dh_ref[...]
        dy = None
        for s in range(4):
            dhb = dh[:, s * SH_O:(s + 1) * SH_O].astype(BF16)
            dw_ref[s] += _dot_tn(yb, dhb)
            part = _dot_nt(dhb, w_ref[s])
            dy = part if dy is None else dy + part
        dcat = dy * sz
        dz_ref[:, :GW] = (dy[:, :GW] * mix * dsz[:, :GW]).astype(BF16)
        dz_ref[:, GW:] = (dy[:, GW:] * mo * dsz[:, GW:]).astype(BF16)
        dmix = dcat[:, :GW]
        prod = dmix * mix
        hi = prod.astype(BF16)
        lo = (prod - hi.astype(F32)).astype(BF16)
        bd = bd_ref[...]
        tot = _dot(hi, bd) + _dot(lo, bd)
        for w, do_ref, dd_ref in zip(ws, (do0, do1, do2), (dd0, dd1, dd2)):
            do_ref[...] = (w * dmix).astype(BF16)
            dd_ref[...] = w * tot

        dqm_ref[...] = _mem_attn_bwd(dcat[:, GW:], p, mo, q4, kvv, dkv_ref).astype(BF16)

        @pl.when(i == NX - 1)
        def _():
            dwb_ref[...] = dw_ref[...].astype(BF16)

    return pl.pallas_call(
        functools.partial(_skip_arg, body, 12), name="attn_out_bwd", grid=(NX,),
        in_specs=[_rows(D, MX)] + [_rows(GW, MX)] * 6 + [_rows(MW, MX), _full((NM, 2 * MW)), _rows(BR_A, MX),
                                                           _full((4, BR_A, SH_O)), _full((GW, GW)),
                                                           pl.BlockSpec(memory_space=pl.ANY)],
        out_specs=[_rows(GW, MX)] * 6 + [_rows(MW, MX), _rows(BR_A, MX), _full((4, BR_A, SH_O)),
                                         _full((NM, 2 * MW)), _full((4, BR_A, SH_O))],
        out_shape=[_sds((S, GW), BF16)] * 3 + [_sds((S, GW), F32)] * 3 + [
            _sds((S, MW), BF16), _sds((S, BR_A), BF16), _sds((4, BR_A, SH_O), F32), _sds((NM, 2 * MW), F32),
            _sds((4, BR_A, SH_O), BF16)],
        scratch_shapes=[pltpu.VMEM((MX, BR_A), BF16)],
        compiler_params=_params(("arbitrary",)),
    )(dh1, *os_, *ls, qm, kv0, z, w_out, jnp.asarray(ones_bd, dtype=BF16), after)


def _attn_bwd(q, k, v, do, lse_s, dd, g):
    d = DILATIONS[g]
    nb = S // d // QBLK
    perm = _perm_matrix(d)

    def body(q_ref, k_ref, v_ref, do_ref, l_ref, dd_ref, p_ref, pt_ref, dq_ref, dk_ref, dv_ref,
             q0, q1, g0, g1, ks, vs, dds, dqs, dks, dvs):
        first, second = _head_masks()
        pm = p_ref[...]
        for t in range(NT):
            rows = slice(t * TM, (t + 1) * TM)
            if d == 1:
                qt = q_ref[rows, :].astype(F32)
                gt = do_ref[rows, :].astype(F32)
            else:
                qt, gt = _pair_dot(pm, q_ref[rows, :], do_ref[rows, :])
                kt, vt = _pair_dot(pm, k_ref[rows, :], v_ref[rows, :])
                _tile_to_streams(kt, ks, t, d)
                _tile_to_streams(vt, vs, t, d)
                _tile_to_streams(_split_dot(pm, dd_ref[rows, :]), dds, t, d)
            _tile_to_streams(jnp.where(first, qt, 0.0), q0, t, d)
            _tile_to_streams(jnp.where(second, qt, 0.0), q1, t, d)
            _tile_to_streams(jnp.where(first, gt, 0.0), g0, t, d)
            _tile_to_streams(jnp.where(second, gt, 0.0), g1, t, d)
        kref, vref, ddref = (k_ref, v_ref, dd_ref) if d == 1 else (ks, vs, dds)
        dqref, dkref, dvref = dqs, dks, dvs
        dkref[...] = jnp.zeros_like(dkref)
        dvref[...] = jnp.zeros_like(dvref)

        def blk(b, carry):
            r0 = pl.multiple_of(b * QBLK, QBLK)
            p0 = pl.multiple_of(jnp.maximum(b - 1, 0) * QBLK, QBLK)
            kk = jnp.concatenate([kref[pl.ds(p0, QBLK), :], kref[pl.ds(r0, QBLK), :]], axis=0)
            vv = jnp.concatenate([vref[pl.ds(p0, QBLK), :], vref[pl.ds(r0, QBLK), :]], axis=0)
            lb = l_ref[pl.ds(r0, QBLK), :]
            ddb = ddref[pl.ds(r0, QBLK), :]
            lcol = jnp.concatenate([lb[:, 0:1], lb[:, HD:HD + 1]], axis=0)
            dcol = jnp.concatenate([ddb[:, 0:1], ddb[:, HD:HD + 1]], axis=0)
            valid = _band_mask(b & (nb - 1))
            valid2 = jnp.concatenate([valid, valid], axis=0)
            qq = jnp.concatenate([q0[pl.ds(r0, QBLK), :], q1[pl.ds(r0, QBLK), :]], axis=0)
            gg = jnp.concatenate([g0[pl.ds(r0, QBLK), :], g1[pl.ds(r0, QBLK), :]], axis=0)
            p = jnp.where(valid2, jnp.exp(_dot_nt(qq, kk) - lcol), 0.0)
            ds = (p * (_dot_nt(gg, vv) - dcol)).astype(BF16)
            dq2 = _dot(ds, kk)
            dqref[pl.ds(r0, QBLK), :] = jnp.where(first[:QBLK], dq2[:QBLK], dq2[QBLK:])
            dkk = _dot_tn(ds, qq)
            dvv = _dot_tn(p.astype(BF16), gg)
            dkref[pl.ds(p0, QBLK), :] += dkk[:QBLK]
            dkref[pl.ds(r0, QBLK), :] += dkk[QBLK:]
            dvref[pl.ds(p0, QBLK), :] += dvv[:QBLK]
            dvref[pl.ds(r0, QBLK), :] += dvv[QBLK:]
            return carry

        lax.fori_loop(0, S // QBLK, blk, 0, unroll=BWD_UNROLL)

        ptm = pt_ref[...] if d > 1 else None
        for t in range(NT):
            rows = slice(t * TM, (t + 1) * TM)
            if d == 1:
                dq_ref[rows, :] = dqs[rows, :].astype(BF16)
                dk_ref[rows, :] = dks[rows, :].astype(BF16)
                dv_ref[rows, :] = dvs[rows, :].astype(BF16)
            else:
                tq, tk = _pair_dot(ptm, _tile_from_streams(dqs, t, d).astype(BF16),
                                   _tile_from_streams(dks, t, d).astype(BF16))
                dq_ref[rows, :] = tq.astype(BF16)
                dk_ref[rows, :] = tk.astype(BF16)
                if t % 2 == 0:
                    ta, tb = _pair_dot(ptm, _tile_from_streams(dvs, t, d).astype(BF16),
                                       _tile_from_streams(dvs, t + 1, d).astype(BF16))
                    dv_ref[rows, :] = ta.astype(BF16)
                    dv_ref[(t + 1) * TM:(t + 2) * TM, :] = tb.astype(BF16)

    qkv_spec = pl.BlockSpec((S, LANES), lambda c: (0, g * NCHUNK + c))
    one_spec = pl.BlockSpec((S, LANES), lambda c: (0, c))
    return pl.pallas_call(
        body, name=f"attn_bwd_g{g}", grid=(NCHUNK,),
        in_specs=[qkv_spec] * 3 + [one_spec] * 3 + [_full((TM, TM))] * 2, out_specs=[one_spec] * 3,
        out_shape=[_sds((S, GW), BF16)] * 3,
        scratch_shapes=[pltpu.VMEM((S, LANES), BF16)] * 6 + [pltpu.VMEM((S, LANES), F32)] * 4,
        compiler_params=_params(("parallel",)),
    )(q, k, v, do, lse_s, dd, jnp.asarray(perm, BF16), jnp.asarray(perm.T, BF16))


def _qkv_bwd(dqs, dks, dvs, dqm, dz, c, s1, s2):
    def body(q0, q1, q2, k0, k1, k2, v0, v1, v2, dqm_ref, dz_ref, c_ref, s1_ref, s2_ref, dp_ref):
        cc, a1, a2 = c_ref[...], s1_ref[...], s2_ref[...]
        for g, (qr, kr, vr) in enumerate(((q0, k0, v0), (q1, k1, v1), (q2, k2, v2))):
            for j in range(GW // 128):
                ls_ = slice(j * 128, (j + 1) * 128)
                c0 = g * GW + j * 128
                dp_ref[:, c0:c0 + 128] = (_rope_bwd(qr[:, ls_].astype(F32), cc, a1, a2) * SCALE).astype(BF16)
                dp_ref[:, NQ + c0:NQ + c0 + 128] = _rope_bwd(kr[:, ls_].astype(F32), cc, a1, a2).astype(BF16)
            dp_ref[:, 2 * NQ + g * GW:2 * NQ + (g + 1) * GW] = vr[...]
        dp_ref[:, 3 * NQ:3 * NQ + MW] = dqm_ref[...]
        dp_ref[:, 3 * NQ + MW:] = dz_ref[...]

    return pl.pallas_call(
        body, name="qkv_bwd", grid=(NT,),
        in_specs=[_rows(GW)] * 9 + [_rows(MW), _rows(BR_A), _rows(128), _rows(128), _rows(128)],
        out_specs=_rows(IN_A), out_shape=_sds((S, IN_A), BF16),
        compiler_params=_params(("parallel",)),
    )(*dqs, *dks, *dvs, dqm, dz, c, s1, s2)


def _mem_bwd(mem, mg, memn, wkv, dkv0, dkv1):
    def body(mem_ref, mg_ref, memn_ref, w_ref, d0_ref, d1_ref, dw_ref, dwb_ref, dg_ref):
        mf = mem_ref[...]
        n = mf * lax.rsqrt(jnp.mean(mf * mf, axis=-1, keepdims=True) + EPS)
        for i, d_ref in enumerate((d0_ref, d1_ref)):
            dkv = d_ref[...].astype(BF16)
            mn = memn_ref[i]
            for s in range(4):
                cs = slice(s * NM, (s + 1) * NM)
                dw = _dot_tn(mn[:, cs], dkv)
                dw_ref[s, i] = dw
                dwb_ref[s, i] = dw.astype(BF16)
                dmn = _dot_nt(dkv, w_ref[s, i])
                dg_ref[i:i + 1, cs] = jnp.sum(dmn * n[:, cs], axis=0, keepdims=True)

    return pl.pallas_call(
        body, name="mem_bwd", grid=(1,),
        in_specs=[_full((NM, D)), _full((2, D)), _full((2, NM, D)), _full((4, 2, NM, 2 * MW)),
                  _full((NM, 2 * MW)), _full((NM, 2 * MW))],
        out_specs=[_full((4, 2, NM, 2 * MW)), _full((4, 2, NM, 2 * MW)), _full((2, D))],
        out_shape=[_sds((4, 2, NM, 2 * MW), F32), _sds((4, 2, NM, 2 * MW), BF16), _sds((2, D), F32)],
        compiler_params=_params(("arbitrary",)),
    )(mem, mg, memn, wkv, dkv0, dkv1)


MESH = pl.DeviceIdType.MESH
ANY = pl.BlockSpec(memory_space=pl.ANY)
BIG = (("wkv", 2, NM, 2 * MW), ("w_in_a", 1, D, SH_A), ("w_out_a", 1, BR_A, SH_O),
       ("w_in_b", 1, D, SH_B), ("w_out_b", 1, BR_B // 4, D))
NBIG = len(BIG)
CW_ROWS = 8


def _place():
    x, y, c = lax.axis_index("x"), lax.axis_index("y"), lax.axis_index("c")
    chips = ((1 - x, y), (x, 1 - y), (1 - x, 1 - y))
    return x, y, c, chips


def _remote(src, dst, ssem, rsem, dev):
    return pltpu.make_async_remote_copy(src_ref=src, dst_ref=dst, send_sem=ssem, recv_sem=rsem,
                                        device_id=dev, device_id_type=MESH)


def _cast_weights(place, ws, after, idx, name):
    nblk = 4
    n = len(idx)
    dims = [BIG[w][1:] for w in idx]

    def body(pref, *refs):
        for i in range(n):
            refs[n + 1 + i][0] = refs[i][...].astype(BF16)

    grid_spec = pltpu.PrefetchScalarGridSpec(
        num_scalar_prefetch=1, grid=(nblk,),
        in_specs=[pl.BlockSpec((k, r // nblk, cdim), lambda i, pref: (0, i, 0)) for k, r, cdim in dims]
        + [pl.BlockSpec(memory_space=pl.ANY)],
        out_specs=[pl.BlockSpec((1, k, r // nblk, cdim), lambda i, pref: (pref[1], 0, i, 0)) for k, r, cdim in dims])
    return pl.pallas_call(
        body, name=name, grid_spec=grid_spec,
        out_shape=[_sds((4, k, r, cdim), BF16) for k, r, cdim in dims],
        compiler_params=_params(("parallel",)),
    )(place, *ws, after)


LAYER_A = (0, 1, 2)
LAYER_B = (3, 4)
HBM = pl.BlockSpec(memory_space=pltpu.HBM)
SEM = pl.BlockSpec(memory_space=pltpu.SEMAPHORE)
EFFECT = pltpu.SideEffectType.DATAFLOW_SIDE_EFFECTING
TOKEN = (8, 128)


def _half(ref, w, which):
    h = BIG[w][2] // 2
    return ref.at[:, pl.ds(which * h, h), :]


def _skip_arg(body, pos, *refs):
    return body(*refs[:pos], *refs[pos + 1:])


def _gather_start(wb, after, idx, name, barrier_id):
    n = len(idx)

    def body(*refs):
        src = refs[:n]
        send_sems, recv_sems = refs[n + 1], refs[n + 2]
        token = refs[2 * n + 3]
        x, y, c, chips = _place()
        _peer_barrier([(px, py, c) for px, py in chips])
        me = 2 * x + y
        for j, (px, py) in enumerate(chips):
            for i in range(n):
                mine = _half(src[i].at[me], idx[i], c)
                _remote(mine, mine, send_sems.at[j * n + i], recv_sems.at[j * n + i], (px, py, c)).start()
        token[...] = jnp.zeros(TOKEN, F32)

    outs = pl.pallas_call(
        body, name=name, in_specs=[HBM] * n + [ANY],
        out_specs=(SEM, SEM) + (HBM,) * n + (pl.BlockSpec(memory_space=pltpu.VMEM),),
        out_shape=(pltpu.SemaphoreType.DMA((3 * n,)), pltpu.SemaphoreType.DMA((3 * n,)))
        + tuple(pltpu.HBM(w.shape, w.dtype) for w in wb) + (_sds(TOKEN, F32),),
        input_output_aliases={i: 2 + i for i in range(n)},
        compiler_params=pltpu.CompilerParams(has_side_effects=EFFECT, collective_id=barrier_id),
    )(*[pltpu.with_memory_space_constraint(w, pltpu.HBM) for w in wb], after)
    return outs[0], outs[1], list(outs[2:2 + n]), outs[2 + n]


def _gather_wait(send_sems, recv_sems, wb, after, idx, name, started=None):
    n = len(idx)
    started = idx if started is None else started
    n_all = len(started)
    pos = [started.index(w) for w in idx]

    def body(*refs):
        buf = refs[:n]
        send_sems, recv_sems = refs[n], refs[n + 1]
        x, y, c, chips = _place()
        me = 2 * x + y
        for j, (px, py) in enumerate(chips):
            for i in range(n):
                mine = _half(buf[i].at[me], idx[i], c)
                got = _half(buf[i].at[2 * px + py], idx[i], c)
                k = j * n_all + pos[i]
                _remote(mine, mine, send_sems.at[k], recv_sems.at[k], (px, py, c)).wait_send()
                _remote(got, got, send_sems.at[k], recv_sems.at[k], (px, py, c)).wait_recv()

    outs = pl.pallas_call(
        body, name=name, in_specs=[HBM] * n + [SEM, SEM] + [ANY] * len(after), out_specs=(HBM,) * n,
        out_shape=tuple(pltpu.HBM(w.shape, w.dtype) for w in wb),
        input_output_aliases={i: i for i in range(n)},
        compiler_params=pltpu.CompilerParams(has_side_effects=EFFECT),
    )(*wb, send_sems, recv_sems, *after)
    return list(outs)


def _gather_forward(wb, idx, name, barrier_id):
    n = len(idx)

    def body(*refs):
        dst = refs[n:2 * n]
        send_sems, recv_sems = refs[2 * n], refs[2 * n + 1]
        x, y, c, chips = _place()
        _sibling_barrier(x, y, c)
        cps = []
        for j, (px, py) in enumerate(chips):
            for i in range(n):
                got = _half(dst[i].at[2 * px + py], idx[i], c)
                cps.append(_remote(got, got, send_sems.at[j, i], recv_sems.at[j, i], (x, y, 1 - c)))
                cps[-1].start()
        for j, (px, py) in enumerate(chips):
            for i in range(n):
                got = _half(dst[i].at[2 * px + py], idx[i], 1 - c)
                _remote(got, got, send_sems.at[j, i], recv_sems.at[j, i], (x, y, 1 - c)).wait_recv()
        for cp in cps:
            cp.wait_send()

    return pl.pallas_call(
        body, name=name, in_specs=[ANY] * n, out_specs=[ANY] * n, out_shape=[_sds(w.shape, BF16) for w in wb],
        input_output_aliases={i: i for i in range(n)},
        scratch_shapes=[pltpu.SemaphoreType.DMA((3, n)), pltpu.SemaphoreType.DMA((3, n))],
        compiler_params=pltpu.CompilerParams(collective_id=barrier_id),
    )(*wb)


def _forward_start(wb, cw, after, idx, name, barrier_id):
    n = len(idx)
    m = n if cw is None else n + 2

    def body(*refs):
        buf = refs[:n]
        send_sems, recv_sems = refs[m + 1], refs[m + 2]
        token = refs[2 * m + 3]
        x, y, c, chips = _place()
        _peer_barrier([(x, y, 1 - c)] + ([] if cw is None else [(px, py, c) for px, py in chips]))
        for j, (px, py) in enumerate(chips):
            for i in range(n):
                got = _half(buf[i].at[2 * px + py], idx[i], c)
                _remote(got, got, send_sems.at[j * (n + 1) + i], recv_sems.at[j * (n + 1) + i], (x, y, 1 - c)).start()
            if cw is not None:
                _remote(refs[n], refs[n + 1].at[2 * x + y], send_sems.at[j * (n + 1) + n],
                        recv_sems.at[j * (n + 1) + n], (px, py, c)).start()
        token[...] = jnp.zeros(TOKEN, F32)

    arrays = list(wb) if cw is None else list(wb) + [cw, lax.empty((4, CW_ROWS, SH_O), F32)]
    outs = pl.pallas_call(
        body, name=name, in_specs=[HBM] * m + [ANY],
        out_specs=(SEM, SEM) + (HBM,) * m + (pl.BlockSpec(memory_space=pltpu.VMEM),),
        out_shape=(pltpu.SemaphoreType.DMA((3 * (n + 1),)), pltpu.SemaphoreType.DMA((3 * (n + 1),)))
        + tuple(pltpu.HBM(a.shape, a.dtype) for a in arrays) + (_sds(TOKEN, F32),),
        input_output_aliases={i: 2 + i for i in range(m)},
        compiler_params=pltpu.CompilerParams(has_side_effects=EFFECT, collective_id=barrier_id),
    )(*[pltpu.with_memory_space_constraint(a, pltpu.HBM) for a in arrays], after)
    return outs[0], outs[1], list(outs[2:2 + m]), outs[2 + m]


def _forward_wait(send_sems, recv_sems, arrays, after, idx, with_cw, name):
    n = len(idx)
    m = len(arrays)

    def body(*refs):
        buf = refs[:n]
        send_sems, recv_sems = refs[m], refs[m + 1]
        x, y, c, chips = _place()
        for j, (px, py) in enumerate(chips):
            for i in range(n):
                sent = _half(buf[i].at[2 * px + py], idx[i], c)
                got = _half(buf[i].at[2 * px + py], idx[i], 1 - c)
                k = j * (n + 1) + i
                _remote(sent, sent, send_sems.at[k], recv_sems.at[k], (x, y, 1 - c)).wait_send()
                _remote(got, got, send_sems.at[k], recv_sems.at[k], (x, y, 1 - c)).wait_recv()
            if with_cw:
                k = j * (n + 1) + n
                theirs = refs[n + 1].at[2 * px + py]
                _remote(refs[n], theirs, send_sems.at[k], recv_sems.at[k], (px, py, c)).wait_send()
                _remote(refs[n], theirs, send_sems.at[k], recv_sems.at[k], (px, py, c)).wait_recv()

    outs = pl.pallas_call(
        body, name=name, in_specs=[HBM] * m + [SEM, SEM] + [ANY] * len(after), out_specs=(HBM,) * m,
        out_shape=tuple(pltpu.HBM(a.shape, a.dtype) for a in arrays),
        input_output_aliases={i: i for i in range(m)},
        compiler_params=pltpu.CompilerParams(has_side_effects=EFFECT),
    )(*arrays, send_sems, recv_sems, *after)
    return list(outs)


def _peer_barrier(peers):
    barrier = pltpu.get_barrier_semaphore()
    for peer in peers:
        pl.semaphore_signal(barrier, inc=1, device_id=peer, device_id_type=MESH)
    pl.semaphore_wait(barrier, len(peers))


def _sibling_barrier(x, y, c):
    _peer_barrier([(x, y, 1 - c)])


def _pair_exchange(gs, idx, name, barrier_id):
    n = len(idx)

    def body(*refs):
        src, dst = refs[:n], refs[n:2 * n]
        send_sems, recv_sems = refs[2 * n:]
        x, y, c, _ = _place()
        _sibling_barrier(x, y, c)
        cps = []
        for i in range(n):
            h = BIG[idx[i]][2] // 2
            cps.append(_remote(src[i].at[:, :, pl.ds((1 - c) * h, h), :], dst[i], send_sems.at[i], recv_sems.at[i],
                               (x, y, 1 - c)))
            cps[-1].start()
        for cp in cps:
            cp.wait()

    return pl.pallas_call(
        body, name=name, in_specs=[ANY] * n, out_specs=[ANY] * n,
        out_shape=[_sds((4, BIG[w][1], BIG[w][2] // 2, BIG[w][3]), BF16) for w in idx],
        scratch_shapes=[pltpu.SemaphoreType.DMA((n,)), pltpu.SemaphoreType.DMA((n,))],
        compiler_params=pltpu.CompilerParams(collective_id=barrier_id),
    )(*gs)


def _pair_start(gs, idx, name, barrier_id):
    n = len(idx)

    def body(*refs):
        src, land = refs[:n], refs[n:2 * n]
        send_sems, recv_sems = refs[2 * n], refs[2 * n + 1]
        token = refs[4 * n + 2]
        x, y, c, _ = _place()
        _sibling_barrier(x, y, c)
        for i in range(n):
            h = BIG[idx[i]][2] // 2
            _remote(src[i].at[:, :, pl.ds((1 - c) * h, h), :], land[i], send_sems.at[i], recv_sems.at[i],
                    (x, y, 1 - c)).start()
        token[...] = jnp.zeros(TOKEN, F32)

    lands = [lax.empty((4, BIG[w][1], BIG[w][2] // 2, BIG[w][3]), BF16) for w in idx]
    arrays = list(gs) + lands
    outs = pl.pallas_call(
        body, name=name, in_specs=[HBM] * (2 * n),
        out_specs=(SEM, SEM) + (HBM,) * (2 * n) + (pl.BlockSpec(memory_space=pltpu.VMEM),),
        out_shape=(pltpu.SemaphoreType.DMA((n,)), pltpu.SemaphoreType.DMA((n,)))
        + tuple(pltpu.HBM(a.shape, a.dtype) for a in arrays) + (_sds(TOKEN, F32),),
        input_output_aliases={i: 2 + i for i in range(2 * n)},
        compiler_params=pltpu.CompilerParams(has_side_effects=EFFECT, collective_id=barrier_id),
    )(*[pltpu.with_memory_space_constraint(a, pltpu.HBM) for a in arrays])
    return outs[0], outs[1], list(outs[2:2 + n]), list(outs[2 + n:2 + 2 * n]), outs[2 + 2 * n]


def _pair_wait(send_sems, recv_sems, gs, lands, after, idx, name):
    n = len(idx)

    def body(*refs):
        src, land = refs[:n], refs[n:2 * n]
        send_sems, recv_sems = refs[2 * n], refs[2 * n + 1]
        x, y, c, _ = _place()
        for i in range(n):
            h = BIG[idx[i]][2] // 2
            cp = _remote(src[i].at[:, :, pl.ds((1 - c) * h, h), :], land[i], send_sems.at[i], recv_sems.at[i],
                         (x, y, 1 - c))
            cp.wait_send()
            cp.wait_recv()

    arrays = list(gs) + list(lands)
    outs = pl.pallas_call(
        body, name=name, in_specs=[HBM] * (2 * n) + [SEM, SEM] + [ANY] * len(after), out_specs=(HBM,) * (2 * n),
        out_shape=tuple(pltpu.HBM(a.shape, a.dtype) for a in arrays),
        input_output_aliases={i: i for i in range(2 * n)},
        compiler_params=pltpu.CompilerParams(has_side_effects=EFFECT),
    )(*arrays, send_sems, recv_sems, *after)
    return list(outs[:n]), list(outs[n:])


def _pair_sums(place, gs, r1s, idx, name):
    n = len(idx)
    dims = [(BIG[w][1], BIG[w][2] // 2, BIG[w][3]) for w in idx]

    def body(pref, *refs):
        for i in range(n):
            refs[2 * n + i][...] = (refs[i][...] + refs[n + i][...].astype(F32)).astype(BF16)

    mine = [pl.BlockSpec((1, k, h, cdim), lambda s, pref: (s, 0, pref[0], 0)) for k, h, cdim in dims]
    whole = [pl.BlockSpec((1, k, h, cdim), lambda s, pref: (s, 0, 0, 0)) for k, h, cdim in dims]
    grid_spec = pltpu.PrefetchScalarGridSpec(num_scalar_prefetch=1, grid=(4,), in_specs=mine + whole, out_specs=whole)
    return pl.pallas_call(
        body, name=name, grid_spec=grid_spec, out_shape=[_sds((4, k, h, cdim), BF16) for k, h, cdim in dims],
        compiler_params=_params(("parallel",)),
    )(place, *gs, *r1s)


def _chip_start(ps, idx, name, barrier_id):
    n = len(idx)

    def body(*refs):
        src, land = refs[:n], refs[n:2 * n]
        send_sems, recv_sems = refs[2 * n], refs[2 * n + 1]
        token = refs[4 * n + 2]
        x, y, c, chips = _place()
        _peer_barrier([(px, py, c) for px, py in chips])
        for j, (px, py) in enumerate(chips):
            for i in range(n):
                _remote(src[i].at[2 * px + py], land[i].at[j], send_sems.at[j * n + i], recv_sems.at[j * n + i],
                        (px, py, c)).start()
        token[...] = jnp.zeros(TOKEN, F32)

    lands = [lax.empty((3,) + p.shape[1:], BF16) for p in ps]
    outs = pl.pallas_call(
        body, name=name, in_specs=[HBM] * (2 * n),
        out_specs=(SEM, SEM) + (HBM,) * (2 * n) + (pl.BlockSpec(memory_space=pltpu.VMEM),),
        out_shape=(pltpu.SemaphoreType.DMA((3 * n,)), pltpu.SemaphoreType.DMA((3 * n,)))
        + tuple(pltpu.HBM(a.shape, a.dtype) for a in list(ps) + lands) + (_sds(TOKEN, F32),),
        input_output_aliases={i: 2 + i for i in range(2 * n)},
        compiler_params=pltpu.CompilerParams(has_side_effects=EFFECT, collective_id=barrier_id),
    )(*[pltpu.with_memory_space_constraint(a, pltpu.HBM) for a in list(ps) + lands])
    return outs[0], outs[1], list(outs[2:2 + n]), list(outs[2 + n:2 + 2 * n]), outs[2 + 2 * n]


def _chip_wait(send_sems, recv_sems, ps, lands, after, idx, name):
    n = len(idx)

    def body(*refs):
        src, land = refs[:n], refs[n:2 * n]
        send_sems, recv_sems = refs[2 * n], refs[2 * n + 1]
        x, y, c, chips = _place()
        for j, (px, py) in enumerate(chips):
            for i in range(n):
                cp = _remote(src[i].at[2 * px + py], land[i].at[j], send_sems.at[j * n + i], recv_sems.at[j * n + i],
                             (px, py, c))
                cp.wait_send()
                cp.wait_recv()

    arrays = list(ps) + list(lands)
    outs = pl.pallas_call(
        body, name=name, in_specs=[HBM] * (2 * n) + [SEM, SEM] + [ANY] * len(after), out_specs=(HBM,) * (2 * n),
        out_shape=tuple(pltpu.HBM(a.shape, a.dtype) for a in arrays),
        input_output_aliases={i: i for i in range(2 * n)},
        compiler_params=pltpu.CompilerParams(has_side_effects=EFFECT),
    )(*arrays, send_sems, recv_sems, *after)
    return list(outs[n:])


def _chip_sums(place, gs, r1s, r2s, idx, name):
    n = len(idx)
    dims = [(BIG[w][1], BIG[w][2] // 4, BIG[w][3]) for w in idx]

    def body(pref, *refs):
        for i in range(n):
            acc = refs[i][0] + refs[n + i][0].astype(F32)
            for j in range(3):
                acc = acc + refs[2 * n + i][j].astype(F32)
            refs[3 * n + i][...] = acc

    in_specs = ([pl.BlockSpec((1, k, q, cdim), lambda t, pref: (pref[1], 0, pref[0] * 2 + t, 0)) for k, q, cdim in dims]
                + [pl.BlockSpec((1, k, q, cdim), lambda t, pref: (pref[1], 0, t, 0)) for k, q, cdim in dims]
                + [pl.BlockSpec((3, k, q, cdim), lambda t, pref: (0, 0, t, 0)) for k, q, cdim in dims])
    out_specs = [pl.BlockSpec((k, q, cdim), lambda t, pref: (0, pref[0] * 2 + t, 0)) for k, q, cdim in dims]
    grid_spec = pltpu.PrefetchScalarGridSpec(num_scalar_prefetch=1, grid=(2,), in_specs=in_specs, out_specs=out_specs)
    return pl.pallas_call(
        body, name=name, grid_spec=grid_spec, out_shape=[_sds(BIG[w][1:], F32) for w in idx],
        compiler_params=_params(("parallel",)),
    )(place, *gs, *r1s, *r2s)


def _pair_gather(hs, idx, name, barrier_id):
    n = len(idx)

    def body(*refs):
        dst = refs[n:2 * n]
        send_sems, recv_sems = refs[2 * n:]
        x, y, c, _ = _place()
        _sibling_barrier(x, y, c)
        cps = []
        for i in range(n):
            mine = _half(dst[i], idx[i], c)
            cps.append(_remote(mine, mine, send_sems.at[i], recv_sems.at[i], (x, y, 1 - c)))
            cps[-1].start()
        for i in range(n):
            theirs = _half(dst[i], idx[i], 1 - c)
            _remote(theirs, theirs, send_sems.at[i], recv_sems.at[i], (x, y, 1 - c)).wait_recv()
        for cp in cps:
            cp.wait_send()

    return pl.pallas_call(
        body, name=name, in_specs=[ANY] * n, out_specs=[ANY] * n,
        out_shape=[_sds(BIG[w][1:], F32) for w in idx],
        input_output_aliases={i: i for i in range(n)},
        scratch_shapes=[pltpu.SemaphoreType.DMA((n,)), pltpu.SemaphoreType.DMA((n,))],
        compiler_params=pltpu.CompilerParams(collective_id=barrier_id),
    )(*hs)


SMALL_ROWS = 40


def _adamw_math(w, g, m, v):
    m = ADAM_B1 * m + (1.0 - ADAM_B1) * g
    v = ADAM_B2 * v + (1.0 - ADAM_B2) * (g * g)
    m_hat = m / (1.0 - ADAM_B1 ** ADAM_STEP)
    v_hat = v / (1.0 - ADAM_B2 ** ADAM_STEP)
    delta = -ADAM_LR * (m_hat / (jnp.sqrt(v_hat) + ADAM_EPS) + ADAM_WD * w)
    return delta, m, v


def _small_start(pack, after):
    def body(pack_ref, land_ref, after_ref, send_sems, recv_sems, pack_thru, land_thru, token):
        x, y, c, _ = _place()
        for r in range(1, 8):
            peer = (x if not r & 4 else 1 - x, y if not r & 2 else 1 - y, c if not r & 1 else 1 - c)
            _remote(pack_ref, land_ref.at[r - 1], send_sems.at[r - 1], recv_sems.at[r - 1], peer).start()
        token[...] = jnp.zeros(TOKEN, F32)

    land = lax.empty((7, SMALL_ROWS, D), F32)
    outs = pl.pallas_call(
        body, name="small_start", in_specs=[HBM, HBM, ANY],
        out_specs=(SEM, SEM, HBM, HBM, pl.BlockSpec(memory_space=pltpu.VMEM)),
        out_shape=(pltpu.SemaphoreType.DMA((7,)), pltpu.SemaphoreType.DMA((7,)), pltpu.HBM(pack.shape, F32),
                   pltpu.HBM(land.shape, F32), _sds(TOKEN, F32)),
        input_output_aliases={0: 2, 1: 3},
        compiler_params=pltpu.CompilerParams(has_side_effects=EFFECT),
    )(pltpu.with_memory_space_constraint(pack, pltpu.HBM), pltpu.with_memory_space_constraint(land, pltpu.HBM), after)
    return outs


def _small_wait(send_sems, recv_sems, pack, land, after):
    def body(pack_ref, land_ref, send_sems, recv_sems, *rest):
        x, y, c, _ = _place()
        for r in range(1, 8):
            peer = (x if not r & 4 else 1 - x, y if not r & 2 else 1 - y, c if not r & 1 else 1 - c)
            cp = _remote(pack_ref, land_ref.at[r - 1], send_sems.at[r - 1], recv_sems.at[r - 1], peer)
            cp.wait_send()
            cp.wait_recv()

    return pl.pallas_call(
        body, name="small_wait", in_specs=[HBM, HBM, SEM, SEM] + [ANY] * len(after), out_specs=(HBM, HBM),
        out_shape=(pltpu.HBM(pack.shape, F32), pltpu.HBM(land.shape, F32)),
        input_output_aliases={0: 0, 1: 1},
        compiler_params=pltpu.CompilerParams(has_side_effects=EFFECT),
    )(pack, land, send_sems, recv_sems, *after)


def _small_update(place, pack, land, ws, ms, vs):
    n = len(ws)

    def body(pref, pack_ref, land_ref, *refs):
        chip = pref[1]
        me = 2 * chip + pref[0]
        own = pack_ref[...]
        tot = None
        for dev in range(8):
            r = jnp.bitwise_xor(me, dev)
            term = jnp.where(r == 0, own, land_ref[jnp.maximum(r - 1, 0)])
            tot = term if tot is None else tot + term
        out, buf = refs[3 * n:-1], refs[-1]
        buf[...] = tot
        g_conv = jnp.zeros((3, SH_O), F32)
        for s in range(4):
            g_conv = g_conv + jnp.where(chip == s, buf[24:27, s * SH_O:(s + 1) * SH_O], 0.0)
        gs = [buf[0:2, :], buf[8:10, :], buf[16:17, :], g_conv]
        out[0][...] = buf[32:33, 0:128]
        for i in range(n):
            d, nm, nv = _adamw_math(refs[i][...], gs[i], refs[n + i][...], refs[2 * n + i][...])
            out[1 + i][...] = gs[i]
            out[1 + n + i][...] = d
            out[1 + 2 * n + i][...] = nm
            out[1 + 3 * n + i][...] = nv

    def full(shape):
        nd = len(shape)
        return pl.BlockSpec(shape, lambda i, pref: (0,) * nd)

    specs = [full(w.shape) for w in ws]
    grid_spec = pltpu.PrefetchScalarGridSpec(
        num_scalar_prefetch=1, grid=(1,),
        in_specs=[full(pack.shape), full(land.shape)] + specs * 3, out_specs=[full((1, 128))] + specs * 4,
        scratch_shapes=[pltpu.VMEM((SMALL_ROWS, D), F32)])
    outs = pl.pallas_call(
        body, name="small_update", grid_spec=grid_spec,
        out_shape=[_sds((1, 128), F32)] + [_sds(w.shape, F32) for w in ws] * 4,
        compiler_params=_params(("arbitrary",)),
    )(place, pack, land, *ws, *ms, *vs)
    return outs[0], outs[1:1 + n], outs[1 + n:1 + 2 * n], outs[1 + 2 * n:1 + 3 * n], outs[1 + 3 * n:]


def _adamw_layer(ws, gs, ms, vs, idx, name):
    n = len(idx)
    dims = [(BIG[w][1], BIG[w][2] // 4, BIG[w][3]) for w in idx]

    def body(*refs):
        for i in range(n):
            gv = refs[n + i][...]
            d, nm, nv = _adamw_math(refs[i][...], gv, refs[2 * n + i][...], refs[3 * n + i][...])
            refs[4 * n + i][...] = d
            refs[5 * n + i][...] = nm
            refs[6 * n + i][...] = nv
            refs[7 * n + i][...] = gv

    specs = [pl.BlockSpec((k, q, cdim), lambda t: (0, t, 0)) for k, q, cdim in dims]
    outs = pl.pallas_call(
        body, name=name, grid=(4,), in_specs=specs * 4, out_specs=specs * 4,
        out_shape=[_sds(BIG[w][1:], F32) for w in idx] * 4,
        compiler_params=_params(("parallel",)),
    )(*ws, *gs, *ms, *vs)
    return [tuple(outs[j * n + i] for j in range(4)) for i in range(n)]


def _pad_rows(a, rows):
    return jnp.pad(a, ((0, rows - a.shape[0]), (0, 0)))


def kernel(x, mem, positions, norm_g, mem_norm_g, w_mem_kv, attn_w_in, attn_w_out, conv_w_in, conv_w, conv_w_out, final_g, loss_target, m_norm_g, m_mem_norm_g, m_w_mem_kv, m_attn_w_in, m_attn_w_out, m_conv_w_in, m_conv_w, m_conv_w_out, m_final_g, v_norm_g, v_mem_norm_g, v_w_mem_kv, v_attn_w_in, v_attn_w_out, v_conv_w_in, v_conv_w, v_conv_w_out, v_final_g):
    mx, my, mc = lax.axis_index("x"), lax.axis_index("y"), lax.axis_index("c")
    place = jnp.stack([mc, 2 * mx + my]).astype(jnp.int32)

    w_big = [w_mem_kv, attn_w_in, attn_w_out, conv_w_in, conv_w_out]
    m_big = [m_w_mem_kv, m_attn_w_in, m_attn_w_out, m_conv_w_in, m_conv_w_out]
    v_big = [v_w_mem_kv, v_attn_w_in, v_attn_w_out, v_conv_w_in, v_conv_w_out]
    first, rest = (1,), (0, 2, 3, 4)
    wb1 = _cast_weights(place, [w_big[i] for i in first], place, first, "cast_w_in_a")
    a1_send, a1_recv, a1_bufs, a1_token = _gather_start(wb1, place, first, "gather_a1_start", 4)
    wbr = _cast_weights(place, [w_big[i] for i in rest], a1_token, rest, "cast_weights")
    r_send, r_recv, r_bufs, gb_token = _gather_start(wbr, a1_token, rest, "gather_rest_start", 5)
    a2_send, a2_recv, gb_send, gb_recv = r_send, r_recv, r_send, r_recv
    a2_bufs, gb_bufs = r_bufs[:2], r_bufs[2:]
    started, rest = rest, (0, 2)

    xs, tgt = x[0], loss_target[0]
    g0, g1 = norm_g[0:1], norm_g[1:2]
    rc, rs1, rs2 = _rope_tables(positions[0].astype(F32).reshape(S, 1), gb_token)
    a1_bufs = _gather_wait(a1_send, a1_recv, a1_bufs, [rc], first, "gather_a1_wait")
    w_in_a = _gather_forward(a1_bufs, first, "gather_a1_forward", 0)[0].reshape(4, D, SH_A)
    hn0, q, k, v, qm0, z0 = _in_proj_a(xs, g0, w_in_a, rc, rs1, rs2, gb_token)
    a2_bufs = _gather_wait(a2_send, a2_recv, a2_bufs, [q], rest, "gather_a2_wait", started)
    f2_send, f2_recv, a2_bufs, f2_token = _forward_start(a2_bufs, None, q, rest, "forward_a2_start", 9)
    fwd = [_attn_fwd(q, k, v, 0, f2_token)]
    fwd.append(_attn_fwd(q, k, v, 1, fwd[0][0]))
    cw_own = _pad_rows(conv_w[0], CW_ROWS)
    gb_bufs = _gather_wait(gb_send, gb_recv, gb_bufs, [fwd[1][0]], LAYER_B, "gather_b_wait", started)
    fb_send, fb_recv, gb_bufs, fb_token = _forward_start(gb_bufs, cw_own, fwd[1][0], LAYER_B, "forward_b_start", 10)
    fwd.append(_attn_fwd(q, k, v, 2, fb_token))
    os_, ls, lss = [f[0] for f in fwd], [f[1] for f in fwd], [f[2] for f in fwd]
    wkv_f, w_out_a = _forward_wait(f2_send, f2_recv, a2_bufs, [os_[2]], rest, False, "forward_a2_wait")
    w_out_a = w_out_a.reshape(4, BR_A, SH_O)
    memn, kv = _mem_fwd(mem[0], mem_norm_g, wkv_f)
    h1 = _attn_out(os_, ls, qm0, kv[0], z0, xs, w_out_a)

    w_in_b, w_out_b, _, cw_f = _forward_wait(fb_send, fb_recv, gb_bufs, [h1], LAYER_B, True, "forward_b_wait")
    w_in_b = w_in_b.reshape(4, D, SH_B)
    w_out_b = w_out_b.reshape(BR_B, D)
    cw_f = lax.dynamic_update_slice(cw_f, cw_own[None], (2 * mx + my, 0, 0))
    cw8 = cw_f.transpose(1, 0, 2).reshape(CW_ROWS, D)
    hn1, bg, cg, u, qm1, z1 = _in_proj_b(h1, g1, w_in_b)
    dh2, loss_part, dfg = _conv_out_loss(bg, cg, u, cw8, qm1, kv[1], z1, h1, w_out_b, final_g.reshape(1, D), tgt)

    dproj_b, dw_out_b, dcw, dkv1, dw_out_b16 = _conv_bwd(dh2, bg, cg, u, cw8, qm1, kv[1], z1, w_out_b)
    dw_in_b, dw_in_b16 = _w_in_grad(hn1, dproj_b, IN_B, "w_in_b_grad")
    gs_b = [dw_in_b.reshape(4, 1, D, SH_B), dw_out_b.reshape(4, 1, BR_B // 4, D)]
    gb_b = [dw_in_b16.reshape(4, 1, D, SH_B), dw_out_b16.reshape(4, 1, BR_B // 4, D)]
    pb_send, pb_recv, gb_b, pb_land, pb_token = _pair_start(gb_b, LAYER_B, "pair_b_start", 6)
    dh1, dg1 = _in_proj_bwd(dproj_b, w_in_b, h1, g1, dh2, pb_token, IN_B, "in_proj_b_bwd")
    _, r1_b = _pair_wait(pb_send, pb_recv, gb_b, pb_land, [dh1], LAYER_B, "pair_b_wait")
    ps_b = _pair_sums(place, gs_b, r1_b, LAYER_B, "pair_sums_b")
    cb_send, cb_recv, cb_src, cb_land, cb_token = _chip_start(ps_b, LAYER_B, "chip_b_start", 7)

    outs = _attn_out_bwd(dh1, os_, ls, qm0, kv[0], z0, w_out_a, cb_token)
    dos, dds, dqm, dz, dw_out_a, dkv0, dw_out_a16 = outs[0:3], outs[3:6], outs[6], outs[7], outs[8], outs[9], outs[10]
    bwd = [_attn_bwd(q, k, v, dos[g], lss[g], dds[g], g) for g in range(3)]
    dproj_a = _qkv_bwd([b[0] for b in bwd], [b[1] for b in bwd], [b[2] for b in bwd], dqm, dz, rc, rs1, rs2)
    dw_in_a, dw_in_a16 = _w_in_grad(hn0, dproj_a, IN_A, "w_in_a_grad")
    dwkv, dwkv16, dmg = _mem_bwd(mem[0], mem_norm_g, memn, wkv_f, dkv0, dkv1)

    gs_a = [dwkv, dw_in_a.reshape(4, 1, D, SH_A), dw_out_a.reshape(4, 1, BR_A, SH_O)]
    r1_a = _pair_exchange([dwkv16, dw_in_a16.reshape(4, 1, D, SH_A), dw_out_a16.reshape(4, 1, BR_A, SH_O)], LAYER_A,
                          "pair_exchange_a", 1)
    ps_a = _pair_sums(place, gs_a, r1_a, LAYER_A, "pair_sums_a")
    ca_send, ca_recv, ca_src, ca_land, ca_token = _chip_start(ps_a, LAYER_A, "chip_a_start", 8)

    gx, dg0 = _in_proj_bwd(dproj_a, w_in_a, xs, g0, dh1, ca_token, IN_A, "in_proj_a_bwd")
    pack = jnp.concatenate([_pad_rows(jnp.concatenate([dg0, dg1], axis=0), 8), _pad_rows(dmg, 8), _pad_rows(dfg, 8),
                            dcw, _pad_rows(jnp.pad(loss_part, ((0, 0), (0, D - 128))), 8)], axis=0)
    sm_send, sm_recv, pack, sm_land, sm_token = _small_start(pack, ca_token)
    r2_b = _chip_wait(cb_send, cb_recv, cb_src, cb_land, [ca_token], LAYER_B, "chip_b_wait")
    hs_b = _chip_sums(place, gs_b, r1_b, r2_b, LAYER_B, "chip_sums_b")
    g_b = _pair_gather(hs_b, LAYER_B, "pair_gather_b", 2)
    upd_b = _adamw_layer([w_big[w] for w in LAYER_B], g_b, [m_big[w] for w in LAYER_B], [v_big[w] for w in LAYER_B],
                         LAYER_B, "adamw_b")
    r2_a = _chip_wait(ca_send, ca_recv, ca_src, ca_land, [gx, upd_b[0][0], upd_b[1][0], sm_token], LAYER_A,
                      "chip_a_wait")
    hs_a = _chip_sums(place, gs_a, r1_a, r2_a, LAYER_A, "chip_sums_a")
    g_a = _pair_gather(hs_a, LAYER_A, "pair_gather_a", 3)
    upd_a = _adamw_layer([w_big[w] for w in LAYER_A], g_a, [m_big[w] for w in LAYER_A], [v_big[w] for w in LAYER_A],
                         LAYER_A, "adamw_a")
    upd = upd_a + upd_b
    g_big = [u[3] for u in upd]
    pack, sm_land = _small_wait(sm_send, sm_recv, pack, sm_land, [r2_a[0]])
    sw = [norm_g, mem_norm_g, final_g.reshape(1, D), conv_w[0]]
    sm = [m_norm_g, m_mem_norm_g, m_final_g.reshape(1, D), m_conv_w[0]]
    sv = [v_norm_g, v_mem_norm_g, v_final_g.reshape(1, D), v_conv_w[0]]
    loss_row, sg, sd, snm, snv = _small_update(place, pack, sm_land, sw, sm, sv)
    loss = loss_row[0, 0]
    g_norm, g_memnorm, g_final, g_conv = sg

    def order(norm, memnorm, wkv, w_in_a, w_out_a, w_in_b, conv, w_out_b, final):
        return (norm, memnorm, wkv, w_in_a, w_out_a, w_in_b, conv.reshape(1, 3, SH_O), w_out_b, final.reshape(D))

    grads = order(g_norm, g_memnorm, g_big[0], g_big[1], g_big[2], g_big[3], g_conv, g_big[4], g_final)
    deltas = order(sd[0], sd[1], upd[0][0], upd[1][0], upd[2][0], upd[3][0], sd[3], upd[4][0], sd[2])
    new_m = order(snm[0], snm[1], upd[0][1], upd[1][1], upd[2][1], upd[3][1], snm[3], upd[4][1], snm[2])
    new_v = order(snv[0], snv[1], upd[0][2], upd[1][2], upd[2][2], upd[3][2], snv[3], upd[4][2], snv[2])
    return (loss, gx[None], *grads, *deltas, *new_m, *new_v)
```

```python
import functools

import numpy as np
import jax
import jax.numpy as jnp
from jax import lax
from jax.experimental import pallas as pl
from jax.experimental.pallas import tpu as pltpu

F32 = jnp.float32
BF16 = jnp.bfloat16

S = 2048
D = 1024
TM = 256
NT = S // TM
MX = 512
NX = S // MX
HD = 64
GW = 512
NQ = 3 * GW
MW = 256
NM = 256
IN_A = 3 * NQ + MW + GW + MW
IN_B = 3 * D + MW + D + MW
BR_A = GW + MW
BR_B = D + MW
SH_A = IN_A // 4
SH_B = IN_B // 4
SH_O = D // 4
QBLK = 128
DILATIONS = (1, 4, 16)
EPS = 1e-6
SCALE = HD ** -0.5
NEG = -1e30
ROPE_THETA = 500000.0

ADAM_LR = 0.001
ADAM_B1 = 0.9
ADAM_B2 = 0.999
ADAM_EPS = 1e-08
ADAM_WD = 0.01
ADAM_STEP = 10

VMEM_LIMIT_BYTES = 60 * 1024 * 1024


def _params(sem=None):
    if sem is None:
        return pltpu.CompilerParams(vmem_limit_bytes=VMEM_LIMIT_BYTES)
    return pltpu.CompilerParams(dimension_semantics=sem, vmem_limit_bytes=VMEM_LIMIT_BYTES)


def _full(shape):
    nd = len(shape)
    return pl.BlockSpec(shape, lambda *_: (0,) * nd)


def _rows(width, tm=TM):
    return pl.BlockSpec((tm, width), lambda i: (i, 0))


def _sds(shape, dtype):
    return jax.ShapeDtypeStruct(shape, dtype)


def _silu_parts(z):
    sig = 0.5 * jnp.tanh(0.5 * z) + 0.5
    return z * sig, sig * (1.0 + z * (1.0 - sig))


def _dot(a, b):
    return jnp.dot(a, b, preferred_element_type=F32)


def _dot_nt(a, b):
    return lax.dot_general(a, b, (((1,), (1,)), ((), ())), preferred_element_type=F32)


def _dot_tn(a, b):
    return lax.dot_general(a, b, (((0,), (0,)), ((), ())), preferred_element_type=F32)


def _rope_fwd(t, c, s1, s2):
    return t * c + pltpu.roll(t, 120, 1) * s1 + pltpu.roll(t, 8, 1) * s2


def _rope_bwd(g, c, s1, s2):
    return g * c + pltpu.roll(g * s1, 8, 1) + pltpu.roll(g * s2, 120, 1)


MEM_HEADS = MW // HD


def _stack_heads(x):
    head = lax.broadcasted_iota(jnp.int32, x.shape, 1) // HD
    return jnp.concatenate([jnp.where(head == h, x, 0.0) for h in range(MEM_HEADS)], axis=0).astype(BF16)


def _unstack_heads(x4):
    tm = x4.shape[0] // MEM_HEADS
    head = lax.broadcasted_iota(jnp.int32, (tm, MW), 1) // HD
    out = x4[:tm]
    for h in range(1, MEM_HEADS):
        out = jnp.where(head == h, x4[h * tm:(h + 1) * tm], out)
    return out


def _mem_attn(qm, kv):
    q4 = _stack_heads(qm.astype(F32))
    s = _dot_nt(q4, kv[:, :MW]) * SCALE
    e = jnp.exp(s - jnp.max(s, axis=-1, keepdims=True))
    p = e * (1.0 / jnp.sum(e, axis=-1, keepdims=True))
    return p, _unstack_heads(_dot(p.astype(BF16), kv[:, MW:])), q4


def _mem_attn_bwd(dmo, p, mo, q4, kv, dkv_ref):
    tm = dmo.shape[0]
    head = lax.broadcasted_iota(jnp.int32, dmo.shape, 1) // HD
    prod = dmo * mo
    delta = jnp.concatenate([jnp.sum(jnp.where(head == h, prod, 0.0), axis=-1, keepdims=True)
                             for h in range(MEM_HEADS)], axis=0)
    d4 = _stack_heads(dmo)
    ds = (p * (_dot_nt(d4, kv[:, MW:]) - delta) * SCALE).astype(BF16)
    dkv_ref[:, :MW] += _dot_tn(ds, q4)
    dkv_ref[:, MW:] += _dot_tn(p.astype(BF16), d4)
    return _unstack_heads(_dot(ds, kv[:, :MW]))


def _merge(o_refs, l_refs):
    ls = [r[...] for r in l_refs]
    m = jnp.maximum(jnp.maximum(ls[0], ls[1]), ls[2])
    es = [jnp.exp(l - m) for l in ls]
    inv = 1.0 / (es[0] + es[1] + es[2])
    ws = [e * inv for e in es]
    os_ = [r[...] for r in o_refs]
    mix = ws[0] * os_[0] + ws[1] * os_[1] + ws[2] * os_[2]
    return ws, mix


def _conv_taps(cg, u, cgp, up, first):
    a = cg * u
    ap = jnp.where(first, 0.0, cgp * up)
    row = lax.broadcasted_iota(jnp.int32, a.shape, 0)
    a1 = jnp.where(row == 0, ap[7:8, :], pltpu.roll(a, 1, 0))
    a2 = jnp.where(row == 0, ap[6:7, :], jnp.where(row == 1, ap[7:8, :], pltpu.roll(a, 2, 0)))
    return a, a1, a2


def _rope_tables(posf, after):
    half = 8
    invf = np.float32(ROPE_THETA) ** (-np.arange(half, dtype=np.float32) * np.float32(2.0 / 16))
    lane = np.arange(128)
    table = np.where((lane % HD) < 16, invf[lane % half], 0.0).astype(np.float32)[None, :]

    def body(pos_ref, invf_ref, c_ref, s1_ref, s2_ref):
        ang = pos_ref[...] * invf_ref[...]
        jm = lax.broadcasted_iota(jnp.int32, ang.shape, 1) & (HD - 1)
        cs = jnp.cos(ang)
        sn = jnp.sin(ang)
        c_ref[...] = jnp.where(jm < 16, cs, 1.0)
        s1_ref[...] = jnp.where(jm < 8, -sn, 0.0)
        s2_ref[...] = jnp.where((jm >= 8) & (jm < 16), sn, 0.0)

    out = _sds((S, 128), F32)
    return pl.pallas_call(
        functools.partial(_skip_arg, body, 2), name="rope_tables", grid=(NT,),
        in_specs=[_rows(1), _full((1, 128)), pl.BlockSpec(memory_space=pl.ANY)],
        out_specs=[_rows(128)] * 3, out_shape=[out] * 3,
        compiler_params=_params(("parallel",)),
    )(posf, jnp.asarray(table), after)


def _in_proj_a(x, g0, w_in, c, s1, s2, after):
    def body(x_ref, g_ref, w_ref, c_ref, s1_ref, s2_ref, hn_ref, q_ref, k_ref, v_ref, qm_ref, z_ref, proj):
        xf = x_ref[...]
        hn = xf * lax.rsqrt(jnp.mean(xf * xf, axis=-1, keepdims=True) + EPS) * g_ref[...]
        hb = hn.astype(BF16)
        hn_ref[...] = hb
        for s in range(4):
            proj[:, s * SH_A:(s + 1) * SH_A] = _dot(hb, w_ref[s])
        cc, a1, a2 = c_ref[...], s1_ref[...], s2_ref[...]
        for j in range(NQ // 128):
            q_ref[:, j * 128:(j + 1) * 128] = (
                _rope_fwd(proj[:, j * 128:(j + 1) * 128], cc, a1, a2) * SCALE).astype(BF16)
            k_ref[:, j * 128:(j + 1) * 128] = _rope_fwd(
                proj[:, NQ + j * 128:NQ + (j + 1) * 128], cc, a1, a2).astype(BF16)
        v_ref[...] = proj[:, 2 * NQ:3 * NQ].astype(BF16)
        qm_ref[...] = proj[:, 3 * NQ:3 * NQ + MW].astype(BF16)
        z_ref[...] = proj[:, 3 * NQ + MW:]

    return pl.pallas_call(
        functools.partial(_skip_arg, body, 6), name="in_proj_a", grid=(NT,),
        in_specs=[_rows(D), _full((1, D)), _full((4, D, SH_A)), _rows(128), _rows(128), _rows(128),
                  pl.BlockSpec(memory_space=pl.ANY)],
        out_specs=[_rows(D), _rows(NQ), _rows(NQ), _rows(NQ), _rows(MW), _rows(BR_A)],
        out_shape=[_sds((S, D), BF16), _sds((S, NQ), BF16), _sds((S, NQ), BF16), _sds((S, NQ), BF16),
                   _sds((S, MW), BF16), _sds((S, BR_A), F32)],
        scratch_shapes=[pltpu.VMEM((TM, IN_A), F32)],
        compiler_params=_params(("parallel",)),
    )(x, g0, w_in, c, s1, s2, after)


def _mem_fwd(mem, mg, wkv):
    def body(mem_ref, mg_ref, w_ref, memn_ref, kv_ref):
        mf = mem_ref[...]
        n = mf * lax.rsqrt(jnp.mean(mf * mf, axis=-1, keepdims=True) + EPS)
        for i in range(2):
            mn = (n * mg_ref[i:i + 1, :]).astype(BF16)
            memn_ref[i] = mn
            acc = _dot(mn[:, 0:NM], w_ref[0, i])
            for s in range(1, 4):
                acc += _dot(mn[:, s * NM:(s + 1) * NM], w_ref[s, i])
            kv_ref[i] = acc.astype(BF16)

    return pl.pallas_call(
        body, name="mem_fwd", grid=(1,),
        in_specs=[_full((NM, D)), _full((2, D)), _full((4, 2, NM, 2 * MW))],
        out_specs=[_full((2, NM, D)), _full((2, NM, 2 * MW))],
        out_shape=[_sds((2, NM, D), BF16), _sds((2, NM, 2 * MW), BF16)],
        compiler_params=_params(("arbitrary",)),
    )(mem, mg, wkv)


def _band_mask(j):
    qi = lax.broadcasted_iota(jnp.int32, (QBLK, 2 * QBLK), 0)
    kj = lax.broadcasted_iota(jnp.int32, (QBLK, 2 * QBLK), 1)
    dist = qi + QBLK - kj
    return (dist >= 0) & (dist <= QBLK) & ((kj >= QBLK) | (j > 0))


LANES = 128
NCHUNK = GW // LANES
FWD_UNROLL = 16
BWD_UNROLL = 16
CONV_CHUNK = 256


def _perm_matrix(d):
    n = TM // d
    p = np.zeros((TM, TM), np.float32)
    for r in range(d):
        for i in range(n):
            p[r * n + i, i * d + r] = 1.0
    return p


def _split_dot(p, x):
    hi = x.astype(BF16)
    lo = (x - hi.astype(F32)).astype(BF16)
    both = _dot(p, jnp.concatenate([hi, lo], axis=1))
    return both[:, :LANES] + both[:, LANES:]


def _pair_dot(p, a, b):
    both = _dot(p, jnp.concatenate([a, b], axis=1))
    return both[:, :LANES], both[:, LANES:]


def _tile_to_streams(y, dst, t, d):
    n, ln = TM // d, S // d
    for r in range(d):
        dst[r * ln + t * n:r * ln + (t + 1) * n, :] = y[r * n:(r + 1) * n].astype(dst.dtype)


def _tile_from_streams(src, t, d):
    n, ln = TM // d, S // d
    return jnp.concatenate([src[r * ln + t * n:r * ln + (t + 1) * n, :] for r in range(d)], axis=0)


def _head_masks():
    first = lax.broadcasted_iota(jnp.int32, (TM, LANES), 1) < HD
    return first, jnp.logical_not(first)


def _attn_fwd(q, k, v, g, after):
    d = DILATIONS[g]
    nb = S // d // QBLK
    perm = _perm_matrix(d)

    def body(q_ref, k_ref, v_ref, p_ref, pt_ref, o_ref, l_ref, ls_ref, q0, q1, ks, vs, os_):
        first, second = _head_masks()
        pm = p_ref[...]
        for t in range(NT):
            rows = slice(t * TM, (t + 1) * TM)
            if d == 1:
                qt = q_ref[rows, :].astype(F32)
            else:
                qt, kt = _pair_dot(pm, q_ref[rows, :], k_ref[rows, :])
                _tile_to_streams(kt, ks, t, d)
                if t % 2 == 0:
                    va, vb = _pair_dot(pm, v_ref[rows, :], v_ref[(t + 1) * TM:(t + 2) * TM, :])
                    _tile_to_streams(va, vs, t, d)
                    _tile_to_streams(vb, vs, t + 1, d)
            _tile_to_streams(jnp.where(first, qt, 0.0), q0, t, d)
            _tile_to_streams(jnp.where(second, qt, 0.0), q1, t, d)
        kref, vref = (k_ref, v_ref) if d == 1 else (ks, vs)
        oref, lref = (o_ref, l_ref) if d == 1 else (os_, ls_ref)

        def blk(b, carry):
            r0 = pl.multiple_of(b * QBLK, QBLK)
            p0 = pl.multiple_of(jnp.maximum(b - 1, 0) * QBLK, QBLK)
            kk = jnp.concatenate([kref[pl.ds(p0, QBLK), :], kref[pl.ds(r0, QBLK), :]], axis=0)
            vv = jnp.concatenate([vref[pl.ds(p0, QBLK), :], vref[pl.ds(r0, QBLK), :]], axis=0)
            valid = _band_mask(b & (nb - 1))
            acc, lse = [], []
            for qh in (q0, q1):
                s = jnp.where(valid, _dot_nt(qh[pl.ds(r0, QBLK), :], kk), NEG)
                m = jnp.max(s, axis=-1, keepdims=True)
                e = jnp.exp(s - m)
                l = jnp.sum(e, axis=-1, keepdims=True)
                acc.append(_dot(e.astype(BF16), vv) * (1.0 / l))
                lse.append(m + jnp.log(l))
            f = first[:QBLK]
            oref[pl.ds(r0, QBLK), :] = jnp.where(f, acc[0], acc[1])
            lref[pl.ds(r0, QBLK), :] = jnp.where(f, lse[0], lse[1])
            return carry

        lax.fori_loop(0, S // QBLK, blk, 0, unroll=FWD_UNROLL)
        if d > 1:
            ptm = pt_ref[...]
            for t in range(NT):
                rows = slice(t * TM, (t + 1) * TM)
                o_ref[rows, :] = _split_dot(ptm, _tile_from_streams(os_, t, d))
                l_ref[rows, :] = _split_dot(ptm, _tile_from_streams(ls_ref, t, d))

    qkv_spec = pl.BlockSpec((S, LANES), lambda c: (0, g * NCHUNK + c))
    out_spec = pl.BlockSpec((S, LANES), lambda c: (0, c))
    n_out = 2 if d == 1 else 3
    inner = body if d > 1 else functools.partial(_drop_arg, body, 7)
    outs = pl.pallas_call(
        functools.partial(_skip_arg, inner, 5), name=f"attn_fwd_g{g}", grid=(NCHUNK,),
        in_specs=[qkv_spec] * 3 + [_full((TM, TM))] * 2 + [pl.BlockSpec(memory_space=pl.ANY)],
        out_specs=[out_spec] * n_out, out_shape=[_sds((S, GW), F32)] * n_out,
        scratch_shapes=[pltpu.VMEM((S, LANES), BF16)] * 4 + [pltpu.VMEM((S, LANES), F32)],
        compiler_params=_params(("parallel",)),
    )(q, k, v, jnp.asarray(perm, BF16), jnp.asarray(perm.T, BF16), after)
    return (outs[0], outs[1], outs[1]) if d == 1 else tuple(outs)


def _drop_arg(body, pos, *refs):
    return body(*refs[:pos], None, *refs[pos:])


def _attn_out(os_, ls, qm, kv0, z, x, w_out):
    def body(o0, o1, o2, l0, l1, l2, qm_ref, kv_ref, z_ref, x_ref, w_ref, h_ref, ybuf):
        _, mix = _merge((o0, o1, o2), (l0, l1, l2))
        sz, _ = _silu_parts(z_ref[...])
        ybuf[:, :GW] = (mix * sz[:, :GW]).astype(BF16)
        _, mo, _ = _mem_attn(qm_ref[...], kv_ref[...])
        ybuf[:, GW:] = (mo * sz[:, GW:]).astype(BF16)
        yb = ybuf[...]
        for s in range(4):
            cs = slice(s * SH_O, (s + 1) * SH_O)
            h_ref[:, cs] = x_ref[:, cs] + _dot(yb, w_ref[s])

    return pl.pallas_call(
        body, name="attn_out", grid=(NX,),
        in_specs=[_rows(GW, MX)] * 6 + [_rows(MW, MX), _full((NM, 2 * MW)), _rows(BR_A, MX), _rows(D, MX),
                                        _full((4, BR_A, SH_O))],
        out_specs=_rows(D, MX), out_shape=_sds((S, D), F32),
        scratch_shapes=[pltpu.VMEM((MX, BR_A), BF16)],
        compiler_params=_params(("parallel",)),
    )(*os_, *ls, qm, kv0, z, x, w_out)


def _in_proj_b(h1, g1, w_in):
    def body(x_ref, g_ref, w_ref, hn_ref, bg_ref, cg_ref, u_ref, qm_ref, z_ref, proj):
        xf = x_ref[...]
        hn = xf * lax.rsqrt(jnp.mean(xf * xf, axis=-1, keepdims=True) + EPS) * g_ref[...]
        hb = hn.astype(BF16)
        hn_ref[...] = hb
        for s in range(4):
            proj[:, s * SH_B:(s + 1) * SH_B] = _dot(hb, w_ref[s])
        bg_ref[...] = proj[:, :D]
        cg_ref[...] = proj[:, D:2 * D]
        u_ref[...] = proj[:, 2 * D:3 * D]
        qm_ref[...] = proj[:, 3 * D:3 * D + MW].astype(BF16)
        z_ref[...] = proj[:, 3 * D + MW:]

    return pl.pallas_call(
        body, name="in_proj_b", grid=(NT,),
        in_specs=[_rows(D), _full((1, D)), _full((4, D, SH_B))],
        out_specs=[_rows(D), _rows(D), _rows(D), _rows(D), _rows(MW), _rows(BR_B)],
        out_shape=[_sds((S, D), BF16), _sds((S, D), F32), _sds((S, D), F32), _sds((S, D), F32),
                   _sds((S, MW), BF16), _sds((S, BR_B), F32)],
        scratch_shapes=[pltpu.VMEM((TM, IN_B), F32)],
        compiler_params=_params(("parallel",)),
    )(h1, g1, w_in)


def _prev8(width):
    return pl.BlockSpec((8, width), lambda i: (jnp.maximum(i * (MX // 8) - 1, 0), 0))


def _conv_out_loss(bg, cg, u, cw, qm, kv1, z, h1, w_out, fg, tgt):
    def body(bg_ref, cg_ref, u_ref, cgp_ref, up_ref, cw_ref, qm_ref, kv_ref, z_ref, h_ref, w_ref, fg_ref, t_ref,
             dh_ref, loss_ref, dfg_ref, ybuf):
        i = pl.program_id(0)
        a, a1, a2 = _conv_taps(cg_ref[...], u_ref[...], cgp_ref[...], up_ref[...], i == 0)
        conv = cw_ref[0:1, :] * a2 + cw_ref[1:2, :] * a1 + cw_ref[2:3, :] * a
        sz, _ = _silu_parts(z_ref[...])
        ybuf[:, :D] = (bg_ref[...] * conv * sz[:, :D]).astype(BF16)
        _, mo, _ = _mem_attn(qm_ref[...], kv_ref[...])
        ybuf[:, D:] = (mo * sz[:, D:]).astype(BF16)
        h2 = h_ref[...] + _dot(ybuf[...], w_ref[...])
        rstd = lax.rsqrt(jnp.mean(h2 * h2, axis=-1, keepdims=True) + EPS)
        n = h2 * rstd
        fgv = fg_ref[...]
        err = n * fgv - t_ref[...]
        dout = err * (1.0 / D)
        dn = dout * fgv
        dh_ref[...] = rstd * (dn - n * jnp.mean(dn * n, axis=-1, keepdims=True))

        @pl.when(i == 0)
        def _():
            loss_ref[...] = jnp.zeros_like(loss_ref)
            dfg_ref[...] = jnp.zeros_like(dfg_ref)

        loss_ref[...] += jnp.sum(err * err) * (0.5 / D)
        dfg_ref[...] += jnp.sum(dout * n, axis=0, keepdims=True)

    return pl.pallas_call(
        body, name="conv_out_loss", grid=(NX,),
        in_specs=[_rows(D, MX), _rows(D, MX), _rows(D, MX), _prev8(D), _prev8(D), _full((8, D)), _rows(MW, MX),
                  _full((NM, 2 * MW)), _rows(BR_B, MX), _rows(D, MX), _full((BR_B, D)), _full((1, D)), _rows(D, MX)],
        out_specs=[_rows(D, MX), _full((1, 128)), _full((1, D))],
        out_shape=[_sds((S, D), F32), _sds((1, 128), F32), _sds((1, D), F32)],
        scratch_shapes=[pltpu.VMEM((MX, BR_B), BF16)],
        compiler_params=_params(("arbitrary",)),
    )(bg, cg, u, cg, u, cw, qm, kv1, z, h1, w_out, fg, tgt)


def _conv_bwd(dh2, bg, cg, u, cw, qm, kv1, z, w_out):
    rev = lambda i: (NX - 1 - i, 0)
    rows = lambda w: pl.BlockSpec((MX, w), rev)
    prev8 = pl.BlockSpec((8, D), lambda i: (jnp.maximum((NX - 1 - i) * (MX // 8) - 1, 0), 0))

    def body(dh_ref, bg_ref, cg_ref, u_ref, cgp_ref, up_ref, cw_ref, qm_ref, kv_ref, z_ref, w_ref,
             dproj_ref, dw_ref, dcw_ref, dkv_ref, dwb_ref, ybuf, carry):
        i = pl.program_id(0)

        @pl.when(i == 0)
        def _():
            dw_ref[...] = jnp.zeros_like(dw_ref)
            dcw_ref[...] = jnp.zeros_like(dcw_ref)
            dkv_ref[...] = jnp.zeros_like(dkv_ref)
            carry[...] = jnp.zeros_like(carry)

        dhb = dh_ref[...].astype(BF16)
        dy = _dot_nt(dhb, w_ref[...])
        kvv = kv_ref[...]
        p, mo, q4 = _mem_attn(qm_ref[...], kvv)
        szm, dszm = _silu_parts(z_ref[:, D:])
        ybuf[:, D:] = (mo * szm).astype(BF16)
        dym = dy[:, D:]
        dproj_ref[:, 3 * D + MW + D:] = (dym * mo * dszm).astype(BF16)
        first_tile = i == NX - 1
        for c in range(D // CONV_CHUNK):
            cs = slice(c * CONV_CHUNK, (c + 1) * CONV_CHUNK)
            bgv, cgv, uv = bg_ref[:, cs], cg_ref[:, cs], u_ref[:, cs]
            a, a1, a2 = _conv_taps(cgv, uv, cgp_ref[:, cs], up_ref[:, cs], first_tile)
            w0, w1, w2 = cw_ref[0:1, cs], cw_ref[1:2, cs], cw_ref[2:3, cs]
            conv = w0 * a2 + w1 * a1 + w2 * a
            mix = bgv * conv
            sz, dsz = _silu_parts(z_ref[:, cs])
            ybuf[:, cs] = (mix * sz).astype(BF16)
            dyc = dy[:, cs]
            dproj_ref[:, 3 * D + MW + c * CONV_CHUNK:3 * D + MW + (c + 1) * CONV_CHUNK] = (
                dyc * mix * dsz).astype(BF16)
            dmix = dyc * sz
            dproj_ref[:, cs] = (dmix * conv).astype(BF16)
            dc = dmix * bgv
            nxt = carry[:, cs]
            row = lax.broadcasted_iota(jnp.int32, dc.shape, 0)
            dc1 = jnp.where(row == MX - 1, nxt[0:1, :], pltpu.roll(dc, MX - 1, 0))
            dc2 = jnp.where(row == MX - 2, nxt[0:1, :],
                            jnp.where(row == MX - 1, nxt[1:2, :], pltpu.roll(dc, MX - 2, 0)))
            carry[:, cs] = dc[0:8, :]
            da = w2 * dc + w1 * dc1 + w0 * dc2
            dproj_ref[:, D + c * CONV_CHUNK:D + (c + 1) * CONV_CHUNK] = (da * uv).astype(BF16)
            dproj_ref[:, 2 * D + c * CONV_CHUNK:2 * D + (c + 1) * CONV_CHUNK] = (da * cgv).astype(BF16)
            dcw_ref[0:1, cs] += jnp.sum(dc * a2, axis=0, keepdims=True)
            dcw_ref[1:2, cs] += jnp.sum(dc * a1, axis=0, keepdims=True)
            dcw_ref[2:3, cs] += jnp.sum(dc * a, axis=0, keepdims=True)
        dw_ref[...] += _dot_tn(ybuf[...], dhb)
        dproj_ref[:, 3 * D:3 * D + MW] = _mem_attn_bwd(dym * szm, p, mo, q4, kvv, dkv_ref).astype(BF16)

        @pl.when(i == NX - 1)
        def _():
            dwb_ref[...] = dw_ref[...].astype(BF16)

    return pl.pallas_call(
        body, name="conv_bwd", grid=(NX,),
        in_specs=[rows(D), rows(D), rows(D), rows(D), prev8, prev8, _full((8, D)), rows(MW),
                  _full((NM, 2 * MW)), rows(BR_B), _full((BR_B, D))],
        out_specs=[rows(IN_B), _full((BR_B, D)), _full((8, D)), _full((NM, 2 * MW)), _full((BR_B, D))],
        out_shape=[_sds((S, IN_B), BF16), _sds((BR_B, D), F32), _sds((8, D), F32), _sds((NM, 2 * MW), F32),
                   _sds((BR_B, D), BF16)],
        scratch_shapes=[pltpu.VMEM((MX, BR_B), BF16), pltpu.VMEM((8, D), F32)],
        compiler_params=_params(("arbitrary",)),
    )(dh2, bg, cg, u, cg, u, cw, qm, kv1, z, w_out)


def _in_proj_bwd(dproj, w_in, xin, g, dres, after, width, name):
    sh = width // 4

    def body(dp_ref, w_ref, x_ref, g_ref, dr_ref, dx_ref, dg_ref):
        i = pl.program_id(0)
        dhn = _dot_nt(dp_ref[:, 0:sh], w_ref[0])
        for s in range(1, 4):
            dhn += _dot_nt(dp_ref[:, s * sh:(s + 1) * sh], w_ref[s])
        xf = x_ref[...]
        rstd = lax.rsqrt(jnp.mean(xf * xf, axis=-1, keepdims=True) + EPS)
        n = xf * rstd
        dn = dhn * g_ref[...]
        dx_ref[...] = dr_ref[...] + rstd * (dn - n * jnp.mean(dn * n, axis=-1, keepdims=True))

        @pl.when(i == 0)
        def _():
            dg_ref[...] = jnp.zeros_like(dg_ref)

        dg_ref[...] += jnp.sum(dhn * n, axis=0, keepdims=True)

    return pl.pallas_call(
        functools.partial(_skip_arg, body, 5), name=name, grid=(NT,),
        in_specs=[_rows(width), _full((4, D, sh)), _rows(D), _full((1, D)), _rows(D), pl.BlockSpec(memory_space=pl.ANY)],
        out_specs=[_rows(D), _full((1, D))],
        out_shape=[_sds((S, D), F32), _sds((1, D), F32)],
        compiler_params=_params(("arbitrary",)),
    )(dproj, w_in, xin, g, dres, after)


def _w_in_grad(hn, dproj, width, name):
    sh = width // 4

    def body(hn_ref, dp_ref, dw_ref, dwb_ref):
        dw = _dot_tn(hn_ref[...], dp_ref[...])
        dw_ref[0] = dw
        dwb_ref[0] = dw.astype(BF16)

    spec = pl.BlockSpec((1, D, sh), lambda s: (s, 0, 0))
    return pl.pallas_call(
        body, name=name, grid=(4,),
        in_specs=[_full((S, D)), pl.BlockSpec((S, sh), lambda s: (0, s))],
        out_specs=[spec, spec], out_shape=[_sds((4, D, sh), F32), _sds((4, D, sh), BF16)],
        compiler_params=_params(("parallel",)),
    )(hn, dproj)


def _attn_out_bwd(dh1, os_, ls, qm, kv0, z, w_out, after):
    ones_bd = np.kron(np.eye(GW // HD, dtype=np.float32), np.ones((HD, HD), np.float32))

    def body(dh_ref, o0, o1, o2, l0, l1, l2, qm_ref, kv_ref, z_ref, w_ref, bd_ref,
             do0, do1, do2, dd0, dd1, dd2, dqm_ref, dz_ref, dw_ref, dkv_ref, dwb_ref, ybuf):
        i = pl.program_id(0)

        @pl.when(i == 0)
        def _():
            dw_ref[...] = jnp.zeros_like(dw_ref)
            dkv_ref[...] = jnp.zeros_like(dkv_ref)

        ws, mix = _merge((o0, o1, o2), (l0, l1, l2))
        sz, dsz = _silu_parts(z_ref[...])
        kvv = kv_ref[...]
        p, mo, q4 = _mem_attn(qm_ref[...], kvv)
        ybuf[:, :GW] = (mix * sz[:, :GW]).astype(BF16)
        ybuf[:, GW:] = (mo * sz[:, GW:]).astype(BF16)
        yb = ybuf[...]
        dh = dh_ref[...]
        dy = None
        for s in range(4):
            dhb = dh[:, s * SH_O:(s + 1) * SH_O].astype(BF16)
            dw_ref[s] += _dot_tn(yb, dhb)
            part = _dot_nt(dhb, w_ref[s])
            dy = part if dy is None else dy + part
        dcat = dy * sz
        dz_ref[:, :GW] = (dy[:, :GW] * mix * dsz[:, :GW]).astype(BF16)
        dz_ref[:, GW:] = (dy[:, GW:] * mo * dsz[:, GW:]).astype(BF16)
        dmix = dcat[:, :GW]
        prod = dmix * mix
        hi = prod.astype(BF16)
        lo = (prod - hi.astype(F32)).astype(BF16)
        bd = bd_ref[...]
        tot = _dot(hi, bd) + _dot(lo, bd)
        for w, do_ref, dd_ref in zip(ws, (do0, do1, do2), (dd0, dd1, dd2)):
            do_ref[...] = (w * dmix).astype(BF16)
            dd_ref[...] = w * tot

        dqm_ref[...] = _mem_attn_bwd(dcat[:, GW:], p, mo, q4, kvv, dkv_ref).astype(BF16)

        @pl.when(i == NX - 1)
        def _():
            dwb_ref[...] = dw_ref[...].astype(BF16)

    return pl.pallas_call(
        functools.partial(_skip_arg, body, 12), name="attn_out_bwd", grid=(NX,),
        in_specs=[_rows(D, MX)] + [_rows(GW, MX)] * 6 + [_rows(MW, MX), _full((NM, 2 * MW)), _rows(BR_A, MX),
                                                           _full((4, BR_A, SH_O)), _full((GW, GW)),
                                                           pl.BlockSpec(memory_space=pl.ANY)],
        out_specs=[_rows(GW, MX)] * 6 + [_rows(MW, MX), _rows(BR_A, MX), _full((4, BR_A, SH_O)),
                                         _full((NM, 2 * MW)), _full((4, BR_A, SH_O))],
        out_shape=[_sds((S, GW), BF16)] * 3 + [_sds((S, GW), F32)] * 3 + [
            _sds((S, MW), BF16), _sds((S, BR_A), BF16), _sds((4, BR_A, SH_O), F32), _sds((NM, 2 * MW), F32),
            _sds((4, BR_A, SH_O), BF16)],
        scratch_shapes=[pltpu.VMEM((MX, BR_A), BF16)],
        compiler_params=_params(("arbitrary",)),
    )(dh1, *os_, *ls, qm, kv0, z, w_out, jnp.asarray(ones_bd, dtype=BF16), after)


def _attn_bwd(q, k, v, do, lse_s, dd, g):
    d = DILATIONS[g]
    nb = S // d // QBLK
    perm = _perm_matrix(d)

    def body(q_ref, k_ref, v_ref, do_ref, l_ref, dd_ref, p_ref, pt_ref, dq_ref, dk_ref, dv_ref,
             q0, q1, g0, g1, ks, vs, dds, dqs, dks, dvs):
        first, second = _head_masks()
        pm = p_ref[...]
        for t in range(NT):
            rows = slice(t * TM, (t + 1) * TM)
            if d == 1:
                qt = q_ref[rows, :].astype(F32)
                gt = do_ref[rows, :].astype(F32)
            else:
                qt, gt = _pair_dot(pm, q_ref[rows, :], do_ref[rows, :])
                kt, vt = _pair_dot(pm, k_ref[rows, :], v_ref[rows, :])
                _tile_to_streams(kt, ks, t, d)
                _tile_to_streams(vt, vs, t, d)
                _tile_to_streams(_split_dot(pm, dd_ref[rows, :]), dds, t, d)
            _tile_to_streams(jnp.where(first, qt, 0.0), q0, t, d)
            _tile_to_streams(jnp.where(second, qt, 0.0), q1, t, d)
            _tile_to_streams(jnp.where(first, gt, 0.0), g0, t, d)
            _tile_to_streams(jnp.where(second, gt, 0.0), g1, t, d)
        kref, vref, ddref = (k_ref, v_ref, dd_ref) if d == 1 else (ks, vs, dds)
        dqref, dkref, dvref = dqs, dks, dvs
        dkref[...] = jnp.zeros_like(dkref)
        dvref[...] = jnp.zeros_like(dvref)

        def blk(b, carry):
            r0 = pl.multiple_of(b * QBLK, QBLK)
            p0 = pl.multiple_of(jnp.maximum(b - 1, 0) * QBLK, QBLK)
            kk = jnp.concatenate([kref[pl.ds(p0, QBLK), :], kref[pl.ds(r0, QBLK), :]], axis=0)
            vv = jnp.concatenate([vref[pl.ds(p0, QBLK), :], vref[pl.ds(r0, QBLK), :]], axis=0)
            lb = l_ref[pl.ds(r0, QBLK), :]
            ddb = ddref[pl.ds(r0, QBLK), :]
            lcol = jnp.concatenate([lb[:, 0:1], lb[:, HD:HD + 1]], axis=0)
            dcol = jnp.concatenate([ddb[:, 0:1], ddb[:, HD:HD + 1]], axis=0)
            valid = _band_mask(b & (nb - 1))
            valid2 = jnp.concatenate([valid, valid], axis=0)
            qq = jnp.concatenate([q0[pl.ds(r0, QBLK), :], q1[pl.ds(r0, QBLK), :]], axis=0)
            gg = jnp.concatenate([g0[pl.ds(r0, QBLK), :], g1[pl.ds(r0, QBLK), :]], axis=0)
            p = jnp.where(valid2, jnp.exp(_dot_nt(qq, kk) - lcol), 0.0)
            ds = (p * (_dot_nt(gg, vv) - dcol)).astype(BF16)
            dq2 = _dot(ds, kk)
            dqref[pl.ds(r0, QBLK), :] = jnp.where(first[:QBLK], dq2[:QBLK], dq2[QBLK:])
            dkk = _dot_tn(ds, qq)
            dvv = _dot_tn(p.astype(BF16), gg)
            dkref[pl.ds(p0, QBLK), :] += dkk[:QBLK]
            dkref[pl.ds(r0, QBLK), :] += dkk[QBLK:]
            dvref[pl.ds(p0, QBLK), :] += dvv[:QBLK]
            dvref[pl.ds(r0, QBLK), :] += dvv[QBLK:]
            return carry

        lax.fori_loop(0, S // QBLK, blk, 0, unroll=BWD_UNROLL)

        ptm = pt_ref[...] if d > 1 else None
        for t in range(NT):
            rows = slice(t * TM, (t + 1) * TM)
            if d == 1:
                dq_ref[rows, :] = dqs[rows, :].astype(BF16)
                dk_ref[rows, :] = dks[rows, :].astype(BF16)
                dv_ref[rows, :] = dvs[rows, :].astype(BF16)
            else:
                tq, tk = _pair_dot(ptm, _tile_from_streams(dqs, t, d).astype(BF16),
                                   _tile_from_streams(dks, t, d).astype(BF16))
                dq_ref[rows, :] = tq.astype(BF16)
                dk_ref[rows, :] = tk.astype(BF16)
                if t % 2 == 0:
                    ta, tb = _pair_dot(ptm, _tile_from_streams(dvs, t, d).astype(BF16),
                                       _tile_from_streams(dvs, t + 1, d).astype(BF16))
                    dv_ref[rows, :] = ta.astype(BF16)
                    dv_ref[(t + 1) * TM:(t + 2) * TM, :] = tb.astype(BF16)

    qkv_spec = pl.BlockSpec((S, LANES), lambda c: (0, g * NCHUNK + c))
    one_spec = pl.BlockSpec((S, LANES), lambda c: (0, c))
    return pl.pallas_call(
        body, name=f"attn_bwd_g{g}", grid=(NCHUNK,),
        in_specs=[qkv_spec] * 3 + [one_spec] * 3 + [_full((TM, TM))] * 2, out_specs=[one_spec] * 3,
        out_shape=[_sds((S, GW), BF16)] * 3,
        scratch_shapes=[pltpu.VMEM((S, LANES), BF16)] * 6 + [pltpu.VMEM((S, LANES), F32)] * 4,
        compiler_params=_params(("parallel",)),
    )(q, k, v, do, lse_s, dd, jnp.asarray(perm, BF16), jnp.asarray(perm.T, BF16))


def _qkv_bwd(dqs, dks, dvs, dqm, dz, c, s1, s2):
    def body(q0, q1, q2, k0, k1, k2, v0, v1, v2, dqm_ref, dz_ref, c_ref, s1_ref, s2_ref, dp_ref):
        cc, a1, a2 = c_ref[...], s1_ref[...], s2_ref[...]
        for g, (qr, kr, vr) in enumerate(((q0, k0, v0), (q1, k1, v1), (q2, k2, v2))):
            for j in range(GW // 128):
                ls_ = slice(j * 128, (j + 1) * 128)
                c0 = g * GW + j * 128
                dp_ref[:, c0:c0 + 128] = (_rope_bwd(qr[:, ls_].astype(F32), cc, a1, a2) * SCALE).astype(BF16)
                dp_ref[:, NQ + c0:NQ + c0 + 128] = _rope_bwd(kr[:, ls_].astype(F32), cc, a1, a2).astype(BF16)
            dp_ref[:, 2 * NQ + g * GW:2 * NQ + (g + 1) * GW] = vr[...]
        dp_ref[:, 3 * NQ:3 * NQ + MW] = dqm_ref[...]
        dp_ref[:, 3 * NQ + MW:] = dz_ref[...]

    return pl.pallas_call(
        body, name="qkv_bwd", grid=(NT,),
        in_specs=[_rows(GW)] * 9 + [_rows(MW), _rows(BR_A), _rows(128), _rows(128), _rows(128)],
        out_specs=_rows(IN_A), out_shape=_sds((S, IN_A), BF16),
        compiler_params=_params(("parallel",)),
    )(*dqs, *dks, *dvs, dqm, dz, c, s1, s2)


def _mem_bwd(mem, mg, memn, wkv, dkv0, dkv1):
    def body(mem_ref, mg_ref, memn_ref, w_ref, d0_ref, d1_ref, dw_ref, dwb_ref, dg_ref):
        mf = mem_ref[...]
        n = mf * lax.rsqrt(jnp.mean(mf * mf, axis=-1, keepdims=True) + EPS)
        for i, d_ref in enumerate((d0_ref, d1_ref)):
            dkv = d_ref[...].astype(BF16)
            mn = memn_ref[i]
            for s in range(4):
                cs = slice(s * NM, (s + 1) * NM)
                dw = _dot_tn(mn[:, cs], dkv)
                dw_ref[s, i] = dw
                dwb_ref[s, i] = dw.astype(BF16)
                dmn = _dot_nt(dkv, w_ref[s, i])
                dg_ref[i:i + 1, cs] = jnp.sum(dmn * n[:, cs], axis=0, keepdims=True)

    return pl.pallas_call(
        body, name="mem_bwd", grid=(1,),
        in_specs=[_full((NM, D)), _full((2, D)), _full((2, NM, D)), _full((4, 2, NM, 2 * MW)),
                  _full((NM, 2 * MW)), _full((NM, 2 * MW))],
        out_specs=[_full((4, 2, NM, 2 * MW)), _full((4, 2, NM, 2 * MW)), _full((2, D))],
        out_shape=[_sds((4, 2, NM, 2 * MW), F32), _sds((4, 2, NM, 2 * MW), BF16), _sds((2, D), F32)],
        compiler_params=_params(("arbitrary",)),
    )(mem, mg, memn, wkv, dkv0, dkv1)


MESH = pl.DeviceIdType.MESH
ANY = pl.BlockSpec(memory_space=pl.ANY)
BIG = (("wkv", 2, NM, 2 * MW), ("w_in_a", 1, D, SH_A), ("w_out_a", 1, BR_A, SH_O),
       ("w_in_b", 1, D, SH_B), ("w_out_b", 1, BR_B // 4, D))
NBIG = len(BIG)
CW_ROWS = 8


def _place():
    x, y, c = lax.axis_index("x"), lax.axis_index("y"), lax.axis_index("c")
    chips = ((1 - x, y), (x, 1 - y), (1 - x, 1 - y))
    return x, y, c, chips


def _remote(src, dst, ssem, rsem, dev):
    return pltpu.make_async_remote_copy(src_ref=src, dst_ref=dst, send_sem=ssem, recv_sem=rsem,
                                        device_id=dev, device_id_type=MESH)


def _cast_weights(place, ws, after, idx, name):
    nblk = 4
    n = len(idx)
    dims = [BIG[w][1:] for w in idx]

    def body(pref, *refs):
        for i in range(n):
            refs[n + 1 + i][0] = refs[i][...].astype(BF16)

    grid_spec = pltpu.PrefetchScalarGridSpec(
        num_scalar_prefetch=1, grid=(nblk,),
        in_specs=[pl.BlockSpec((k, r // nblk, cdim), lambda i, pref: (0, i, 0)) for k, r, cdim in dims]
        + [pl.BlockSpec(memory_space=pl.ANY)],
        out_specs=[pl.BlockSpec((1, k, r // nblk, cdim), lambda i, pref: (pref[1], 0, i, 0)) for k, r, cdim in dims])
    return pl.pallas_call(
        body, name=name, grid_spec=grid_spec,
        out_shape=[_sds((4, k, r, cdim), BF16) for k, r, cdim in dims],
        compiler_params=_params(("parallel",)),
    )(place, *ws, after)


LAYER_A = (0, 1, 2)
LAYER_B = (3, 4)
HBM = pl.BlockSpec(memory_space=pltpu.HBM)
SEM = pl.BlockSpec(memory_space=pltpu.SEMAPHORE)
EFFECT = pltpu.SideEffectType.DATAFLOW_SIDE_EFFECTING
TOKEN = (8, 128)


def _half(ref, w, which):
    h = BIG[w][2] // 2
    return ref.at[:, pl.ds(which * h, h), :]


def _skip_arg(body, pos, *refs):
    return body(*refs[:pos], *refs[pos + 1:])


def _gather_start(wb, after, idx, name, barrier_id):
    n = len(idx)

    def body(*refs):
        src = refs[:n]
        send_sems, recv_sems = refs[n + 1], refs[n + 2]
        token = refs[2 * n + 3]
        x, y, c, chips = _place()
        _peer_barrier([(px, py, c) for px, py in chips])
        me = 2 * x + y
        for i in range(n):
            for j, (px, py) in enumerate(chips):
                mine = _half(src[i].at[me], idx[i], c)
                _remote(mine, mine, send_sems.at[j * n + i], recv_sems.at[j * n + i], (px, py, c)).start()
        token[...] = jnp.zeros(TOKEN, F32)

    outs = pl.pallas_call(
        body, name=name, in_specs=[HBM] * n + [ANY],
        out_specs=(SEM, SEM) + (HBM,) * n + (pl.BlockSpec(memory_space=pltpu.VMEM),),
        out_shape=(pltpu.SemaphoreType.DMA((3 * n,)), pltpu.SemaphoreType.DMA((3 * n,)))
        + tuple(pltpu.HBM(w.shape, w.dtype) for w in wb) + (_sds(TOKEN, F32),),
        input_output_aliases={i: 2 + i for i in range(n)},
        compiler_params=pltpu.CompilerParams(has_side_effects=EFFECT, collective_id=barrier_id),
    )(*[pltpu.with_memory_space_constraint(w, pltpu.HBM) for w in wb], after)
    return outs[0], outs[1], list(outs[2:2 + n]), outs[2 + n]


def _gather_wait(send_sems, recv_sems, wb, after, idx, name, started=None):
    n = len(idx)
    started = idx if started is None else started
    n_all = len(started)
    pos = [started.index(w) for w in idx]

    def body(*refs):
        buf = refs[:n]
        send_sems, recv_sems = refs[n], refs[n + 1]
        x, y, c, chips = _place()
        me = 2 * x + y
        for j, (px, py) in enumerate(chips):
            for i in range(n):
                mine = _half(buf[i].at[me], idx[i], c)
                got = _half(buf[i].at[2 * px + py], idx[i], c)
                k = j * n_all + pos[i]
                _remote(mine, mine, send_sems.at[k], recv_sems.at[k], (px, py, c)).wait_send()
                _remote(got, got, send_sems.at[k], recv_sems.at[k], (px, py, c)).wait_recv()

    outs = pl.pallas_call(
        body, name=name, in_specs=[HBM] * n + [SEM, SEM] + [ANY] * len(after), out_specs=(HBM,) * n,
        out_shape=tuple(pltpu.HBM(w.shape, w.dtype) for w in wb),
        input_output_aliases={i: i for i in range(n)},
        compiler_params=pltpu.CompilerParams(has_side_effects=EFFECT),
    )(*wb, send_sems, recv_sems, *after)
    return list(outs)


def _gather_forward(wb, idx, name, barrier_id):
    n = len(idx)

    def body(*refs):
        dst = refs[n:2 * n]
        send_sems, recv_sems = refs[2 * n], refs[2 * n + 1]
        x, y, c, chips = _place()
        _sibling_barrier(x, y, c)
        cps = []
        for j, (px, py) in enumerate(chips):
            for i in range(n):
                got = _half(dst[i].at[2 * px + py], idx[i], c)
                cps.append(_remote(got, got, send_sems.at[j, i], recv_sems.at[j, i], (x, y, 1 - c)))
                cps[-1].start()
        for j, (px, py) in enumerate(chips):
            for i in range(n):
                got = _half(dst[i].at[2 * px + py], idx[i], 1 - c)
                _remote(got, got, send_sems.at[j, i], recv_sems.at[j, i], (x, y, 1 - c)).wait_recv()
        for cp in cps:
            cp.wait_send()

    return pl.pallas_call(
        body, name=name, in_specs=[ANY] * n, out_specs=[ANY] * n, out_shape=[_sds(w.shape, BF16) for w in wb],
        input_output_aliases={i: i for i in range(n)},
        scratch_shapes=[pltpu.SemaphoreType.DMA((3, n)), pltpu.SemaphoreType.DMA((3, n))],
        compiler_params=pltpu.CompilerParams(collective_id=barrier_id),
    )(*wb)


def _forward_start(wb, cw, after, idx, name, barrier_id):
    n = len(idx)
    m = n if cw is None else n + 2

    def body(*refs):
        buf = refs[:n]
        send_sems, recv_sems = refs[m + 1], refs[m + 2]
        token = refs[2 * m + 3]
        x, y, c, chips = _place()
        _peer_barrier([(x, y, 1 - c)] + ([] if cw is None else [(px, py, c) for px, py in chips]))
        for j, (px, py) in enumerate(chips):
            for i in range(n):
                got = _half(buf[i].at[2 * px + py], idx[i], c)
                _remote(got, got, send_sems.at[j * (n + 1) + i], recv_sems.at[j * (n + 1) + i], (x, y, 1 - c)).start()
            if cw is not None:
                _remote(refs[n], refs[n + 1].at[2 * x + y], send_sems.at[j * (n + 1) + n],
                        recv_sems.at[j * (n + 1) + n], (px, py, c)).start()
        token[...] = jnp.zeros(TOKEN, F32)

    arrays = list(wb) if cw is None else list(wb) + [cw, lax.empty((4, CW_ROWS, SH_O), F32)]
    outs = pl.pallas_call(
        body, name=name, in_specs=[HBM] * m + [ANY],
        out_specs=(SEM, SEM) + (HBM,) * m + (pl.BlockSpec(memory_space=pltpu.VMEM),),
        out_shape=(pltpu.SemaphoreType.DMA((3 * (n + 1),)), pltpu.SemaphoreType.DMA((3 * (n + 1),)))
        + tuple(pltpu.HBM(a.shape, a.dtype) for a in arrays) + (_sds(TOKEN, F32),),
        input_output_aliases={i: 2 + i for i in range(m)},
        compiler_params=pltpu.CompilerParams(has_side_effects=EFFECT, collective_id=barrier_id),
    )(*[pltpu.with_memory_space_constraint(a, pltpu.HBM) for a in arrays], after)
    return outs[0], outs[1], list(outs[2:2 + m]), outs[2 + m]


def _forward_wait(send_sems, recv_sems, arrays, after, idx, with_cw, name):
    n = len(idx)
    m = len(arrays)

    def body(*refs):
        buf = refs[:n]
        send_sems, recv_sems = refs[m], refs[m + 1]
        x, y, c, chips = _place()
        for j, (px, py) in enumerate(chips):
            for i in range(n):
                sent = _half(buf[i].at[2 * px + py], idx[i], c)
                got = _half(buf[i].at[2 * px + py], idx[i], 1 - c)
                k = j * (n + 1) + i
                _remote(sent, sent, send_sems.at[k], recv_sems.at[k], (x, y, 1 - c)).wait_send()
                _remote(got, got, send_sems.at[k], recv_sems.at[k], (x, y, 1 - c)).wait_recv()
            if with_cw:
                k = j * (n + 1) + n
                theirs = refs[n + 1].at[2 * px + py]
                _remote(refs[n], theirs, send_sems.at[k], recv_sems.at[k], (px, py, c)).wait_send()
                _remote(refs[n], theirs, send_sems.at[k], recv_sems.at[k], (px, py, c)).wait_recv()

    outs = pl.pallas_call(
        body, name=name, in_specs=[HBM] * m + [SEM, SEM] + [ANY] * len(after), out_specs=(HBM,) * m,
        out_shape=tuple(pltpu.HBM(a.shape, a.dtype) for a in arrays),
        input_output_aliases={i: i for i in range(m)},
        compiler_params=pltpu.CompilerParams(has_side_effects=EFFECT),
    )(*arrays, send_sems, recv_sems, *after)
    return list(outs)


def _peer_barrier(peers):
    barrier = pltpu.get_barrier_semaphore()
    for peer in peers:
        pl.semaphore_signal(barrier, inc=1, device_id=peer, device_id_type=MESH)
    pl.semaphore_wait(barrier, len(peers))


def _sibling_barrier(x, y, c):
    _peer_barrier([(x, y, 1 - c)])


def _pair_exchange(gs, idx, name, barrier_id):
    n = len(idx)

    def body(*refs):
        src, dst = refs[:n], refs[n:2 * n]
        send_sems, recv_sems = refs[2 * n:]
        x, y, c, _ = _place()
        _sibling_barrier(x, y, c)
        cps = []
        for i in range(n):
            h = BIG[idx[i]][2] // 2
            cps.append(_remote(src[i].at[:, :, pl.ds((1 - c) * h, h), :], dst[i], send_sems.at[i], recv_sems.at[i],
                               (x, y, 1 - c)))
            cps[-1].start()
        for cp in cps:
            cp.wait()

    return pl.pallas_call(
        body, name=name, in_specs=[ANY] * n, out_specs=[ANY] * n,
        out_shape=[_sds((4, BIG[w][1], BIG[w][2] // 2, BIG[w][3]), BF16) for w in idx],
        scratch_shapes=[pltpu.SemaphoreType.DMA((n,)), pltpu.SemaphoreType.DMA((n,))],
        compiler_params=pltpu.CompilerParams(collective_id=barrier_id),
    )(*gs)


def _pair_start(gs, idx, name, barrier_id):
    n = len(idx)

    def body(*refs):
        src, land = refs[:n], refs[n:2 * n]
        send_sems, recv_sems = refs[2 * n], refs[2 * n + 1]
        token = refs[4 * n + 2]
        x, y, c, _ = _place()
        _sibling_barrier(x, y, c)
        for i in range(n):
            h = BIG[idx[i]][2] // 2
            _remote(src[i].at[:, :, pl.ds((1 - c) * h, h), :], land[i], send_sems.at[i], recv_sems.at[i],
                    (x, y, 1 - c)).start()
        token[...] = jnp.zeros(TOKEN, F32)

    lands = [lax.empty((4, BIG[w][1], BIG[w][2] // 2, BIG[w][3]), BF16) for w in idx]
    arrays = list(gs) + lands
    outs = pl.pallas_call(
        body, name=name, in_specs=[HBM] * (2 * n),
        out_specs=(SEM, SEM) + (HBM,) * (2 * n) + (pl.BlockSpec(memory_space=pltpu.VMEM),),
        out_shape=(pltpu.SemaphoreType.DMA((n,)), pltpu.SemaphoreType.DMA((n,)))
        + tuple(pltpu.HBM(a.shape, a.dtype) for a in arrays) + (_sds(TOKEN, F32),),
        input_output_aliases={i: 2 + i for i in range(2 * n)},
        compiler_params=pltpu.CompilerParams(has_side_effects=EFFECT, collective_id=barrier_id),
    )(*[pltpu.with_memory_space_constraint(a, pltpu.HBM) for a in arrays])
    return outs[0], outs[1], list(outs[2:2 + n]), list(outs[2 + n:2 + 2 * n]), outs[2 + 2 * n]


def _pair_wait(send_sems, recv_sems, gs, lands, after, idx, name):
    n = len(idx)

    def body(*refs):
        src, land = refs[:n], refs[n:2 * n]
        send_sems, recv_sems = refs[2 * n], refs[2 * n + 1]
        x, y, c, _ = _place()
        for i in range(n):
            h = BIG[idx[i]][2] // 2
            cp = _remote(src[i].at[:, :, pl.ds((1 - c) * h, h), :], land[i], send_sems.at[i], recv_sems.at[i],
                         (x, y, 1 - c))
            cp.wait_send()
            cp.wait_recv()

    arrays = list(gs) + list(lands)
    outs = pl.pallas_call(
        body, name=name, in_specs=[HBM] * (2 * n) + [SEM, SEM] + [ANY] * len(after), out_specs=(HBM,) * (2 * n),
        out_shape=tuple(pltpu.HBM(a.shape, a.dtype) for a in arrays),
        input_output_aliases={i: i for i in range(2 * n)},
        compiler_params=pltpu.CompilerParams(has_side_effects=EFFECT),
    )(*arrays, send_sems, recv_sems, *after)
    return list(outs[:n]), list(outs[n:])


def _pair_sums(place, gs, r1s, idx, name):
    n = len(idx)
    dims = [(BIG[w][1], BIG[w][2] // 2, BIG[w][3]) for w in idx]

    def body(pref, *refs):
        for i in range(n):
            refs[2 * n + i][...] = (refs[i][...] + refs[n + i][...].astype(F32)).astype(BF16)

    mine = [pl.BlockSpec((1, k, h, cdim), lambda s, pref: (s, 0, pref[0], 0)) for k, h, cdim in dims]
    whole = [pl.BlockSpec((1, k, h, cdim), lambda s, pref: (s, 0, 0, 0)) for k, h, cdim in dims]
    grid_spec = pltpu.PrefetchScalarGridSpec(num_scalar_prefetch=1, grid=(4,), in_specs=mine + whole, out_specs=whole)
    return pl.pallas_call(
        body, name=name, grid_spec=grid_spec, out_shape=[_sds((4, k, h, cdim), BF16) for k, h, cdim in dims],
        compiler_params=_params(("parallel",)),
    )(place, *gs, *r1s)


def _chip_start(ps, idx, name, barrier_id):
    n = len(idx)

    def body(*refs):
        src, land = refs[:n], refs[n:2 * n]
        send_sems, recv_sems = refs[2 * n], refs[2 * n + 1]
        token = refs[4 * n + 2]
        x, y, c, chips = _place()
        _peer_barrier([(px, py, c) for px, py in chips])
        for j, (px, py) in enumerate(chips):
            for i in range(n):
                _remote(src[i].at[2 * px + py], land[i].at[j], send_sems.at[j * n + i], recv_sems.at[j * n + i],
                        (px, py, c)).start()
        token[...] = jnp.zeros(TOKEN, F32)

    lands = [lax.empty((3,) + p.shape[1:], BF16) for p in ps]
    outs = pl.pallas_call(
        body, name=name, in_specs=[HBM] * (2 * n),
        out_specs=(SEM, SEM) + (HBM,) * (2 * n) + (pl.BlockSpec(memory_space=pltpu.VMEM),),
        out_shape=(pltpu.SemaphoreType.DMA((3 * n,)), pltpu.SemaphoreType.DMA((3 * n,)))
        + tuple(pltpu.HBM(a.shape, a.dtype) for a in list(ps) + lands) + (_sds(TOKEN, F32),),
        input_output_aliases={i: 2 + i for i in range(2 * n)},
        compiler_params=pltpu.CompilerParams(has_side_effects=EFFECT, collective_id=barrier_id),
    )(*[pltpu.with_memory_space_constraint(a, pltpu.HBM) for a in list(ps) + lands])
    return outs[0], outs[1], list(outs[2:2 + n]), list(outs[2 + n:2 + 2 * n]), outs[2 + 2 * n]


def _chip_wait(send_sems, recv_sems, ps, lands, after, idx, name):
    n = len(idx)

    def body(*refs):
        src, land = refs[:n], refs[n:2 * n]
        send_sems, recv_sems = refs[2 * n], refs[2 * n + 1]
        x, y, c, chips = _place()
        for j, (px, py) in enumerate(chips):
            for i in range(n):
                cp = _remote(src[i].at[2 * px + py], land[i].at[j], send_sems.at[j * n + i], recv_sems.at[j * n + i],
                             (px, py, c))
                cp.wait_send()
                cp.wait_recv()

    arrays = list(ps) + list(lands)
    outs = pl.pallas_call(
        body, name=name, in_specs=[HBM] * (2 * n) + [SEM, SEM] + [ANY] * len(after), out_specs=(HBM,) * (2 * n),
        out_shape=tuple(pltpu.HBM(a.shape, a.dtype) for a in arrays),
        input_output_aliases={i: i for i in range(2 * n)},
        compiler_params=pltpu.CompilerParams(has_side_effects=EFFECT),
    )(*arrays, send_sems, recv_sems, *after)
    return list(outs[n:])


def _chip_sums(place, gs, r1s, r2s, idx, name):
    n = len(idx)
    dims = [(BIG[w][1], BIG[w][2] // 4, BIG[w][3]) for w in idx]

    def body(pref, *refs):
        for i in range(n):
            acc = refs[i][0] + refs[n + i][0].astype(F32)
            for j in range(3):
                acc = acc + refs[2 * n + i][j].astype(F32)
            refs[3 * n + i][...] = acc

    in_specs = ([pl.BlockSpec((1, k, q, cdim), lambda t, pref: (pref[1], 0, pref[0] * 2 + t, 0)) for k, q, cdim in dims]
                + [pl.BlockSpec((1, k, q, cdim), lambda t, pref: (pref[1], 0, t, 0)) for k, q, cdim in dims]
                + [pl.BlockSpec((3, k, q, cdim), lambda t, pref: (0, 0, t, 0)) for k, q, cdim in dims])
    out_specs = [pl.BlockSpec((k, q, cdim), lambda t, pref: (0, pref[0] * 2 + t, 0)) for k, q, cdim in dims]
    grid_spec = pltpu.PrefetchScalarGridSpec(num_scalar_prefetch=1, grid=(2,), in_specs=in_specs, out_specs=out_specs)
    return pl.pallas_call(
        body, name=name, grid_spec=grid_spec, out_shape=[_sds(BIG[w][1:], F32) for w in idx],
        compiler_params=_params(("parallel",)),
    )(place, *gs, *r1s, *r2s)


def _pair_gather(hs, idx, name, barrier_id):
    n = len(idx)

    def body(*refs):
        dst = refs[n:2 * n]
        send_sems, recv_sems = refs[2 * n:]
        x, y, c, _ = _place()
        _sibling_barrier(x, y, c)
        cps = []
        for i in range(n):
            mine = _half(dst[i], idx[i], c)
            cps.append(_remote(mine, mine, send_sems.at[i], recv_sems.at[i], (x, y, 1 - c)))
            cps[-1].start()
        for i in range(n):
            theirs = _half(dst[i], idx[i], 1 - c)
            _remote(theirs, theirs, send_sems.at[i], recv_sems.at[i], (x, y, 1 - c)).wait_recv()
        for cp in cps:
            cp.wait_send()

    return pl.pallas_call(
        body, name=name, in_specs=[ANY] * n, out_specs=[ANY] * n,
        out_shape=[_sds(BIG[w][1:], F32) for w in idx],
        input_output_aliases={i: i for i in range(n)},
        scratch_shapes=[pltpu.SemaphoreType.DMA((n,)), pltpu.SemaphoreType.DMA((n,))],
        compiler_params=pltpu.CompilerParams(collective_id=barrier_id),
    )(*hs)


SMALL_ROWS = 40


def _adamw_math(w, g, m, v):
    m = ADAM_B1 * m + (1.0 - ADAM_B1) * g
    v = ADAM_B2 * v + (1.0 - ADAM_B2) * (g * g)
    m_hat = m / (1.0 - ADAM_B1 ** ADAM_STEP)
    v_hat = v / (1.0 - ADAM_B2 ** ADAM_STEP)
    delta = -ADAM_LR * (m_hat / (jnp.sqrt(v_hat) + ADAM_EPS) + ADAM_WD * w)
    return delta, m, v


def _small_start(pack, after):
    def body(pack_ref, land_ref, after_ref, send_sems, recv_sems, pack_thru, land_thru, token):
        x, y, c, _ = _place()
        for r in range(1, 8):
            peer = (x if not r & 4 else 1 - x, y if not r & 2 else 1 - y, c if not r & 1 else 1 - c)
            _remote(pack_ref, land_ref.at[r - 1], send_sems.at[r - 1], recv_sems.at[r - 1], peer).start()
        token[...] = jnp.zeros(TOKEN, F32)

    land = lax.empty((7, SMALL_ROWS, D), F32)
    outs = pl.pallas_call(
        body, name="small_start", in_specs=[HBM, HBM, ANY],
        out_specs=(SEM, SEM, HBM, HBM, pl.BlockSpec(memory_space=pltpu.VMEM)),
        out_shape=(pltpu.SemaphoreType.DMA((7,)), pltpu.SemaphoreType.DMA((7,)), pltpu.HBM(pack.shape, F32),
                   pltpu.HBM(land.shape, F32), _sds(TOKEN, F32)),
        input_output_aliases={0: 2, 1: 3},
        compiler_params=pltpu.CompilerParams(has_side_effects=EFFECT),
    )(pltpu.with_memory_space_constraint(pack, pltpu.HBM), pltpu.with_memory_space_constraint(land, pltpu.HBM), after)
    return outs


def _small_wait(send_sems, recv_sems, pack, land, after):
    def body(pack_ref, land_ref, send_sems, recv_sems, *rest):
        x, y, c, _ = _place()
        for r in range(1, 8):
            peer = (x if not r & 4 else 1 - x, y if not r & 2 else 1 - y, c if not r & 1 else 1 - c)
            cp = _remote(pack_ref, land_ref.at[r - 1], send_sems.at[r - 1], recv_sems.at[r - 1], peer)
            cp.wait_send()
            cp.wait_recv()

    return pl.pallas_call(
        body, name="small_wait", in_specs=[HBM, HBM, SEM, SEM] + [ANY] * len(after), out_specs=(HBM, HBM),
        out_shape=(pltpu.HBM(pack.shape, F32), pltpu.HBM(land.shape, F32)),
        input_output_aliases={0: 0, 1: 1},
        compiler_params=pltpu.CompilerParams(has_side_effects=EFFECT),
    )(pack, land, send_sems, recv_sems, *after)


def _small_update(place, pack, land, ws, ms, vs):
    n = len(ws)

    def body(pref, pack_ref, land_ref, *refs):
        chip = pref[1]
        me = 2 * chip + pref[0]
        own = pack_ref[...]
        tot = None
        for dev in range(8):
            r = jnp.bitwise_xor(me, dev)
            term = jnp.where(r == 0, own, land_ref[jnp.maximum(r - 1, 0)])
            tot = term if tot is None else tot + term
        out, buf = refs[3 * n:-1], refs[-1]
        buf[...] = tot
        g_conv = jnp.zeros((3, SH_O), F32)
        for s in range(4):
            g_conv = g_conv + jnp.where(chip == s, buf[24:27, s * SH_O:(s + 1) * SH_O], 0.0)
        gs = [buf[0:2, :], buf[8:10, :], buf[16:17, :], g_conv]
        out[0][...] = buf[32:33, 0:128]
        for i in range(n):
            d, nm, nv = _adamw_math(refs[i][...], gs[i], refs[n + i][...], refs[2 * n + i][...])
            out[1 + i][...] = gs[i]
            out[1 + n + i][...] = d
            out[1 + 2 * n + i][...] = nm
            out[1 + 3 * n + i][...] = nv

    def full(shape):
        nd = len(shape)
        return pl.BlockSpec(shape, lambda i, pref: (0,) * nd)

    specs = [full(w.shape) for w in ws]
    grid_spec = pltpu.PrefetchScalarGridSpec(
        num_scalar_prefetch=1, grid=(1,),
        in_specs=[full(pack.shape), full(land.shape)] + specs * 3, out_specs=[full((1, 128))] + specs * 4,
        scratch_shapes=[pltpu.VMEM((SMALL_ROWS, D), F32)])
    outs = pl.pallas_call(
        body, name="small_update", grid_spec=grid_spec,
        out_shape=[_sds((1, 128), F32)] + [_sds(w.shape, F32) for w in ws] * 4,
        compiler_params=_params(("arbitrary",)),
    )(place, pack, land, *ws, *ms, *vs)
    return outs[0], outs[1:1 + n], outs[1 + n:1 + 2 * n], outs[1 + 2 * n:1 + 3 * n], outs[1 + 3 * n:]


def _adamw_layer(ws, gs, ms, vs, idx, name):
    n = len(idx)
    dims = [(BIG[w][1], BIG[w][2] // 4, BIG[w][3]) for w in idx]

    def body(*refs):
        for i in range(n):
            gv = refs[n + i][...]
            d, nm, nv = _adamw_math(refs[i][...], gv, refs[2 * n + i][...], refs[3 * n + i][...])
            refs[4 * n + i][...] = d
            refs[5 * n + i][...] = nm
            refs[6 * n + i][...] = nv
            refs[7 * n + i][...] = gv

    specs = [pl.BlockSpec((k, q, cdim), lambda t: (0, t, 0)) for k, q, cdim in dims]
    outs = pl.pallas_call(
        body, name=name, grid=(4,), in_specs=specs * 4, out_specs=specs * 4,
        out_shape=[_sds(BIG[w][1:], F32) for w in idx] * 4,
        compiler_params=_params(("parallel",)),
    )(*ws, *gs, *ms, *vs)
    return [tuple(outs[j * n + i] for j in range(4)) for i in range(n)]


def _pad_rows(a, rows):
    return jnp.pad(a, ((0, rows - a.shape[0]), (0, 0)))


def kernel(x, mem, positions, norm_g, mem_norm_g, w_mem_kv, attn_w_in, attn_w_out, conv_w_in, conv_w, conv_w_out, final_g, loss_target, m_norm_g, m_mem_norm_g, m_w_mem_kv, m_attn_w_in, m_attn_w_out, m_conv_w_in, m_conv_w, m_conv_w_out, m_final_g, v_norm_g, v_mem_norm_g, v_w_mem_kv, v_attn_w_in, v_attn_w_out, v_conv_w_in, v_conv_w, v_conv_w_out, v_final_g):
    mx, my, mc = lax.axis_index("x"), lax.axis_index("y"), lax.axis_index("c")
    place = jnp.stack([mc, 2 * mx + my]).astype(jnp.int32)

    w_big = [w_mem_kv, attn_w_in, attn_w_out, conv_w_in, conv_w_out]
    m_big = [m_w_mem_kv, m_attn_w_in, m_attn_w_out, m_conv_w_in, m_conv_w_out]
    v_big = [v_w_mem_kv, v_attn_w_in, v_attn_w_out, v_conv_w_in, v_conv_w_out]
    first, rest = (1,), (0, 2, 3, 4)
    wb1 = _cast_weights(place, [w_big[i] for i in first], place, first, "cast_w_in_a")
    a1_send, a1_recv, a1_bufs, a1_token = _gather_start(wb1, place, first, "gather_a1_start", 4)
    wbr = _cast_weights(place, [w_big[i] for i in rest], a1_token, rest, "cast_weights")
    r_send, r_recv, r_bufs, gb_token = _gather_start(wbr, a1_token, rest, "gather_rest_start", 5)
    a2_send, a2_recv, gb_send, gb_recv = r_send, r_recv, r_send, r_recv
    a2_bufs, gb_bufs = r_bufs[:2], r_bufs[2:]
    started, rest = rest, (0, 2)

    xs, tgt = x[0], loss_target[0]
    g0, g1 = norm_g[0:1], norm_g[1:2]
    rc, rs1, rs2 = _rope_tables(positions[0].astype(F32).reshape(S, 1), gb_token)
    a1_bufs = _gather_wait(a1_send, a1_recv, a1_bufs, [rc], first, "gather_a1_wait")
    w_in_a = _gather_forward(a1_bufs, first, "gather_a1_forward", 0)[0].reshape(4, D, SH_A)
    hn0, q, k, v, qm0, z0 = _in_proj_a(xs, g0, w_in_a, rc, rs1, rs2, gb_token)
    a2_bufs = _gather_wait(a2_send, a2_recv, a2_bufs, [q], rest, "gather_a2_wait", started)
    f2_send, f2_recv, a2_bufs, f2_token = _forward_start(a2_bufs, None, q, rest, "forward_a2_start", 9)
    fwd = [_attn_fwd(q, k, v, 0, f2_token)]
    fwd.append(_attn_fwd(q, k, v, 1, fwd[0][0]))
    cw_own = _pad_rows(conv_w[0], CW_ROWS)
    gb_bufs = _gather_wait(gb_send, gb_recv, gb_bufs, [fwd[1][0]], LAYER_B, "gather_b_wait", started)
    fb_send, fb_recv, gb_bufs, fb_token = _forward_start(gb_bufs, cw_own, fwd[1][0], LAYER_B, "forward_b_start", 10)
    fwd.append(_attn_fwd(q, k, v, 2, fb_token))
    os_, ls, lss = [f[0] for f in fwd], [f[1] for f in fwd], [f[2] for f in fwd]
    wkv_f, w_out_a = _forward_wait(f2_send, f2_recv, a2_bufs, [os_[2]], rest, False, "forward_a2_wait")
    w_out_a = w_out_a.reshape(4, BR_A, SH_O)
    memn, kv = _mem_fwd(mem[0], mem_norm_g, wkv_f)
    h1 = _attn_out(os_, ls, qm0, kv[0], z0, xs, w_out_a)

    w_in_b, w_out_b, _, cw_f = _forward_wait(fb_send, fb_recv, gb_bufs, [h1], LAYER_B, True, "forward_b_wait")
    w_in_b = w_in_b.reshape(4, D, SH_B)
    w_out_b = w_out_b.reshape(BR_B, D)
    cw_f = lax.dynamic_update_slice(cw_f, cw_own[None], (2 * mx + my, 0, 0))
    cw8 = cw_f.transpose(1, 0, 2).reshape(CW_ROWS, D)
    hn1, bg, cg, u, qm1, z1 = _in_proj_b(h1, g1, w_in_b)
    dh2, loss_part, dfg = _conv_out_loss(bg, cg, u, cw8, qm1, kv[1], z1, h1, w_out_b, final_g.reshape(1, D), tgt)

    dproj_b, dw_out_b, dcw, dkv1, dw_out_b16 = _conv_bwd(dh2, bg, cg, u, cw8, qm1, kv[1], z1, w_out_b)
    dw_in_b, dw_in_b16 = _w_in_grad(hn1, dproj_b, IN_B, "w_in_b_grad")
    gs_b = [dw_in_b.reshape(4, 1, D, SH_B), dw_out_b.reshape(4, 1, BR_B // 4, D)]
    gb_b = [dw_in_b16.reshape(4, 1, D, SH_B), dw_out_b16.reshape(4, 1, BR_B // 4, D)]
    pb_send, pb_recv, gb_b, pb_land, pb_token = _pair_start(gb_b, LAYER_B, "pair_b_start", 6)
    dh1, dg1 = _in_proj_bwd(dproj_b, w_in_b, h1, g1, dh2, pb_token, IN_B, "in_proj_b_bwd")
    _, r1_b = _pair_wait(pb_send, pb_recv, gb_b, pb_land, [dh1], LAYER_B, "pair_b_wait")
    ps_b = _pair_sums(place, gs_b, r1_b, LAYER_B, "pair_sums_b")
    cb_send, cb_recv, cb_src, cb_land, cb_token = _chip_start(ps_b, LAYER_B, "chip_b_start", 7)

    outs = _attn_out_bwd(dh1, os_, ls, qm0, kv[0], z0, w_out_a, cb_token)
    dos, dds, dqm, dz, dw_out_a, dkv0, dw_out_a16 = outs[0:3], outs[3:6], outs[6], outs[7], outs[8], outs[9], outs[10]
    bwd = [_attn_bwd(q, k, v, dos[g], lss[g], dds[g], g) for g in range(3)]
    dproj_a = _qkv_bwd([b[0] for b in bwd], [b[1] for b in bwd], [b[2] for b in bwd], dqm, dz, rc, rs1, rs2)
    dw_in_a, dw_in_a16 = _w_in_grad(hn0, dproj_a, IN_A, "w_in_a_grad")
    dwkv, dwkv16, dmg = _mem_bwd(mem[0], mem_norm_g, memn, wkv_f, dkv0, dkv1)

    gs_a = [dwkv, dw_in_a.reshape(4, 1, D, SH_A), dw_out_a.reshape(4, 1, BR_A, SH_O)]
    r1_a = _pair_exchange([dwkv16, dw_in_a16.reshape(4, 1, D, SH_A), dw_out_a16.reshape(4, 1, BR_A, SH_O)], LAYER_A,
                          "pair_exchange_a", 1)
    ps_a = _pair_sums(place, gs_a, r1_a, LAYER_A, "pair_sums_a")
    ca_send, ca_recv, ca_src, ca_land, ca_token = _chip_start(ps_a, LAYER_A, "chip_a_start", 8)

    gx, dg0 = _in_proj_bwd(dproj_a, w_in_a, xs, g0, dh1, ca_token, IN_A, "in_proj_a_bwd")
    pack = jnp.concatenate([_pad_rows(jnp.concatenate([dg0, dg1], axis=0), 8), _pad_rows(dmg, 8), _pad_rows(dfg, 8),
                            dcw, _pad_rows(jnp.pad(loss_part, ((0, 0), (0, D - 128))), 8)], axis=0)
    sm_send, sm_recv, pack, sm_land, sm_token = _small_start(pack, ca_token)
    r2_b = _chip_wait(cb_send, cb_recv, cb_src, cb_land, [ca_token], LAYER_B, "chip_b_wait")
    hs_b = _chip_sums(place, gs_b, r1_b, r2_b, LAYER_B, "chip_sums_b")
    g_b = _pair_gather(hs_b, LAYER_B, "pair_gather_b", 2)
    upd_b = _adamw_layer([w_big[w] for w in LAYER_B], g_b, [m_big[w] for w in LAYER_B], [v_big[w] for w in LAYER_B],
                         LAYER_B, "adamw_b")
    r2_a = _chip_wait(ca_send, ca_recv, ca_src, ca_land, [gx, upd_b[0][0], upd_b[1][0], sm_token], LAYER_A,
                      "chip_a_wait")
    hs_a = _chip_sums(place, gs_a, r1_a, r2_a, LAYER_A, "chip_sums_a")
    g_a = _pair_gather(hs_a, LAYER_A, "pair_gather_a", 3)
    upd_a = _adamw_layer([w_big[w] for w in LAYER_A], g_a, [m_big[w] for w in LAYER_A], [v_big[w] for w in LAYER_A],
                         LAYER_A, "adamw_a")
    upd = upd_a + upd_b
    g_big = [u[3] for u in upd]
    pack, sm_land = _small_wait(sm_send, sm_recv, pack, sm_land, [r2_a[0]])
    sw = [norm_g, mem_norm_g, final_g.reshape(1, D), conv_w[0]]
    sm = [m_norm_g, m_mem_norm_g, m_final_g.reshape(1, D), m_conv_w[0]]
    sv = [v_norm_g, v_mem_norm_g, v_final_g.reshape(1, D), v_conv_w[0]]
    loss_row, sg, sd, snm, snv = _small_update(place, pack, sm_land, sw, sm, sv)
    loss = loss_row[0, 0]
    g_norm, g_memnorm, g_final, g_conv = sg

    def order(norm, memnorm, wkv, w_in_a, w_out_a, w_in_b, conv, w_out_b, final):
        return (norm, memnorm, wkv, w_in_a, w_out_a, w_in_b, conv.reshape(1, 3, SH_O), w_out_b, final.reshape(D))

    grads = order(g_norm, g_memnorm, g_big[0], g_big[1], g_big[2], g_big[3], g_conv, g_big[4], g_final)
    deltas = order(sd[0], sd[1], upd[0][0], upd[1][0], upd[2][0], upd[3][0], sd[3], upd[4][0], sd[2])
    new_m = order(snm[0], snm[1], upd[0][1], upd[1][1], upd[2][1], upd[3][1], snm[3], upd[4][1], snm[2])
    new_v = order(snv[0], snv[1], upd[0][2], upd[1][2], upd[2][2], upd[3][2], snv[3], upd[4][2], snv[2])
    return (loss, gx[None], *grads, *deltas, *new_m, *new_v)
```

```python
import functools

import numpy as np
import jax
import jax.numpy as jnp
from jax import lax
from jax.experimental import pallas as pl
from jax.experimental.pallas import tpu as pltpu

F32 = jnp.float32
BF16 = jnp.bfloat16

S = 2048
D = 1024
TM = 256
NT = S // TM
MX = 512
NX = S // MX
HD = 64
GW = 512
NQ = 3 * GW
MW = 256
NM = 256
IN_A = 3 * NQ + MW + GW + MW
IN_B = 3 * D + MW + D + MW
BR_A = GW + MW
BR_B = D + MW
SH_A = IN_A // 4
SH_B = IN_B // 4
SH_O = D // 4
QBLK = 128
DILATIONS = (1, 4, 16)
EPS = 1e-6
SCALE = HD ** -0.5
NEG = -1e30
ROPE_THETA = 500000.0

ADAM_LR = 0.001
ADAM_B1 = 0.9
ADAM_B2 = 0.999
ADAM_EPS = 1e-08
ADAM_WD = 0.01
ADAM_STEP = 10

VMEM_LIMIT_BYTES = 60 * 1024 * 1024


def _params(sem=None):
    if sem is None:
        return pltpu.CompilerParams(vmem_limit_bytes=VMEM_LIMIT_BYTES)
    return pltpu.CompilerParams(dimension_semantics=sem, vmem_limit_bytes=VMEM_LIMIT_BYTES)


def _full(shape):
    nd = len(shape)
    return pl.BlockSpec(shape, lambda *_: (0,) * nd)


def _rows(width, tm=TM):
    return pl.BlockSpec((tm, width), lambda i: (i, 0))


def _sds(shape, dtype):
    return jax.ShapeDtypeStruct(shape, dtype)


def _silu_parts(z):
    sig = 0.5 * jnp.tanh(0.5 * z) + 0.5
    return z * sig, sig * (1.0 + z * (1.0 - sig))


def _dot(a, b):
    return jnp.dot(a, b, preferred_element_type=F32)


def _dot_nt(a, b):
    return lax.dot_general(a, b, (((1,), (1,)), ((), ())), preferred_element_type=F32)


def _dot_tn(a, b):
    return lax.dot_general(a, b, (((0,), (0,)), ((), ())), preferred_element_type=F32)


def _rope_fwd(t, c, s1, s2):
    return t * c + pltpu.roll(t, 120, 1) * s1 + pltpu.roll(t, 8, 1) * s2


def _rope_bwd(g, c, s1, s2):
    return g * c + pltpu.roll(g * s1, 8, 1) + pltpu.roll(g * s2, 120, 1)


MEM_HEADS = MW // HD


def _stack_heads(x):
    head = lax.broadcasted_iota(jnp.int32, x.shape, 1) // HD
    return jnp.concatenate([jnp.where(head == h, x, 0.0) for h in range(MEM_HEADS)], axis=0).astype(BF16)


def _unstack_heads(x4):
    tm = x4.shape[0] // MEM_HEADS
    head = lax.broadcasted_iota(jnp.int32, (tm, MW), 1) // HD
    out = x4[:tm]
    for h in range(1, MEM_HEADS):
        out = jnp.where(head == h, x4[h * tm:(h + 1) * tm], out)
    return out


def _mem_attn(qm, kv):
    q4 = _stack_heads(qm.astype(F32))
    s = _dot_nt(q4, kv[:, :MW]) * SCALE
    e = jnp.exp(s - jnp.max(s, axis=-1, keepdims=True))
    p = e * (1.0 / jnp.sum(e, axis=-1, keepdims=True))
    return p, _unstack_heads(_dot(p.astype(BF16), kv[:, MW:])), q4


def _mem_attn_bwd(dmo, p, mo, q4, kv, dkv_ref):
    tm = dmo.shape[0]
    head = lax.broadcasted_iota(jnp.int32, dmo.shape, 1) // HD
    prod = dmo * mo
    delta = jnp.concatenate([jnp.sum(jnp.where(head == h, prod, 0.0), axis=-1, keepdims=True)
                             for h in range(MEM_HEADS)], axis=0)
    d4 = _stack_heads(dmo)
    ds = (p * (_dot_nt(d4, kv[:, MW:]) - delta) * SCALE).astype(BF16)
    dkv_ref[:, :MW] += _dot_tn(ds, q4)
    dkv_ref[:, MW:] += _dot_tn(p.astype(BF16), d4)
    return _unstack_heads(_dot(ds, kv[:, :MW]))


def _merge(o_refs, l_refs):
    ls = [r[...] for r in l_refs]
    m = jnp.maximum(jnp.maximum(ls[0], ls[1]), ls[2])
    es = [jnp.exp(l - m) for l in ls]
    inv = 1.0 / (es[0] + es[1] + es[2])
    ws = [e * inv for e in es]
    os_ = [r[...] for r in o_refs]
    mix = ws[0] * os_[0] + ws[1] * os_[1] + ws[2] * os_[2]
    return ws, mix


def _conv_taps(cg, u, cgp, up, first):
    a = cg * u
    ap = jnp.where(first, 0.0, cgp * up)
    row = lax.broadcasted_iota(jnp.int32, a.shape, 0)
    a1 = jnp.where(row == 0, ap[7:8, :], pltpu.roll(a, 1, 0))
    a2 = jnp.where(row == 0, ap[6:7, :], jnp.where(row == 1, ap[7:8, :], pltpu.roll(a, 2, 0)))
    return a, a1, a2


def _rope_tables(posf, after):
    half = 8
    invf = np.float32(ROPE_THETA) ** (-np.arange(half, dtype=np.float32) * np.float32(2.0 / 16))
    lane = np.arange(128)
    table = np.where((lane % HD) < 16, invf[lane % half], 0.0).astype(np.float32)[None, :]

    def body(pos_ref, invf_ref, c_ref, s1_ref, s2_ref):
        ang = pos_ref[...] * invf_ref[...]
        jm = lax.broadcasted_iota(jnp.int32, ang.shape, 1) & (HD - 1)
        cs = jnp.cos(ang)
        sn = jnp.sin(ang)
        c_ref[...] = jnp.where(jm < 16, cs, 1.0)
        s1_ref[...] = jnp.where(jm < 8, -sn, 0.0)
        s2_ref[...] = jnp.where((jm >= 8) & (jm < 16), sn, 0.0)

    out = _sds((S, 128), F32)
    return pl.pallas_call(
        functools.partial(_skip_arg, body, 2), name="rope_tables", grid=(NT,),
        in_specs=[_rows(1), _full((1, 128)), pl.BlockSpec(memory_space=pl.ANY)],
        out_specs=[_rows(128)] * 3, out_shape=[out] * 3,
        compiler_params=_params(("parallel",)),
    )(posf, jnp.asarray(table), after)


def _in_proj_a(x, g0, w_in, c, s1, s2, after):
    def body(x_ref, g_ref, w_ref, c_ref, s1_ref, s2_ref, hn_ref, q_ref, k_ref, v_ref, qm_ref, z_ref, proj):
        xf = x_ref[...]
        hn = xf * lax.rsqrt(jnp.mean(xf * xf, axis=-1, keepdims=True) + EPS) * g_ref[...]
        hb = hn.astype(BF16)
        hn_ref[...] = hb
        for s in range(4):
            proj[:, s * SH_A:(s + 1) * SH_A] = _dot(hb, w_ref[s])
        cc, a1, a2 = c_ref[...], s1_ref[...], s2_ref[...]
        for j in range(NQ // 128):
            q_ref[:, j * 128:(j + 1) * 128] = (
                _rope_fwd(proj[:, j * 128:(j + 1) * 128], cc, a1, a2) * SCALE).astype(BF16)
            k_ref[:, j * 128:(j + 1) * 128] = _rope_fwd(
                proj[:, NQ + j * 128:NQ + (j + 1) * 128], cc, a1, a2).astype(BF16)
        v_ref[...] = proj[:, 2 * NQ:3 * NQ].astype(BF16)
        qm_ref[...] = proj[:, 3 * NQ:3 * NQ + MW].astype(BF16)
        z_ref[...] = proj[:, 3 * NQ + MW:]

    return pl.pallas_call(
        functools.partial(_skip_arg, body, 6), name="in_proj_a", grid=(NT,),
        in_specs=[_rows(D), _full((1, D)), _full((4, D, SH_A)), _rows(128), _rows(128), _rows(128),
                  pl.BlockSpec(memory_space=pl.ANY)],
        out_specs=[_rows(D), _rows(NQ), _rows(NQ), _rows(NQ), _rows(MW), _rows(BR_A)],
        out_shape=[_sds((S, D), BF16), _sds((S, NQ), BF16), _sds((S, NQ), BF16), _sds((S, NQ), BF16),
                   _sds((S, MW), BF16), _sds((S, BR_A), F32)],
        scratch_shapes=[pltpu.VMEM((TM, IN_A), F32)],
        compiler_params=_params(("parallel",)),
    )(x, g0, w_in, c, s1, s2, after)


def _mem_fwd(mem, mg, wkv):
    def body(mem_ref, mg_ref, w_ref, memn_ref, kv_ref):
        mf = mem_ref[...]
        n = mf * lax.rsqrt(jnp.mean(mf * mf, axis=-1, keepdims=True) + EPS)
        for i in range(2):
            mn = (n * mg_ref[i:i + 1, :]).astype(BF16)
            memn_ref[i] = mn
            acc = _dot(mn[:, 0:NM], w_ref[0, i])
            for s in range(1, 4):
                acc += _dot(mn[:, s * NM:(s + 1) * NM], w_ref[s, i])
            kv_ref[i] = acc.astype(BF16)

    return pl.pallas_call(
        body, name="mem_fwd", grid=(1,),
        in_specs=[_full((NM, D)), _full((2, D)), _full((4, 2, NM, 2 * MW))],
        out_specs=[_full((2, NM, D)), _full((2, NM, 2 * MW))],
        out_shape=[_sds((2, NM, D), BF16), _sds((2, NM, 2 * MW), BF16)],
        compiler_params=_params(("arbitrary",)),
    )(mem, mg, wkv)


def _band_mask(j):
    qi = lax.broadcasted_iota(jnp.int32, (QBLK, 2 * QBLK), 0)
    kj = lax.broadcasted_iota(jnp.int32, (QBLK, 2 * QBLK), 1)
    dist = qi + QBLK - kj
    return (dist >= 0) & (dist <= QBLK) & ((kj >= QBLK) | (j > 0))


LANES = 128
NCHUNK = GW // LANES
FWD_UNROLL = 16
BWD_UNROLL = 16
CONV_CHUNK = 256


def _perm_matrix(d):
    n = TM // d
    p = np.zeros((TM, TM), np.float32)
    for r in range(d):
        for i in range(n):
            p[r * n + i, i * d + r] = 1.0
    return p


def _split_dot(p, x):
    hi = x.astype(BF16)
    lo = (x - hi.astype(F32)).astype(BF16)
    both = _dot(p, jnp.concatenate([hi, lo], axis=1))
    return both[:, :LANES] + both[:, LANES:]


def _pair_dot(p, a, b):
    both = _dot(p, jnp.concatenate([a, b], axis=1))
    return both[:, :LANES], both[:, LANES:]


def _tile_to_streams(y, dst, t, d):
    n, ln = TM // d, S // d
    for r in range(d):
        dst[r * ln + t * n:r * ln + (t + 1) * n, :] = y[r * n:(r + 1) * n].astype(dst.dtype)


def _tile_from_streams(src, t, d):
    n, ln = TM // d, S // d
    return jnp.concatenate([src[r * ln + t * n:r * ln + (t + 1) * n, :] for r in range(d)], axis=0)


def _head_masks():
    first = lax.broadcasted_iota(jnp.int32, (TM, LANES), 1) < HD
    return first, jnp.logical_not(first)


def _attn_fwd(q, k, v, g, after):
    d = DILATIONS[g]
    nb = S // d // QBLK
    perm = _perm_matrix(d)

    def body(q_ref, k_ref, v_ref, p_ref, pt_ref, o_ref, l_ref, ls_ref, q0, q1, ks, vs, os_):
        first, second = _head_masks()
        pm = p_ref[...]
        for t in range(NT):
            rows = slice(t * TM, (t + 1) * TM)
            if d == 1:
                qt = q_ref[rows, :].astype(F32)
            else:
                qt, kt = _pair_dot(pm, q_ref[rows, :], k_ref[rows, :])
                _tile_to_streams(kt, ks, t, d)
                if t % 2 == 0:
                    va, vb = _pair_dot(pm, v_ref[rows, :], v_ref[(t + 1) * TM:(t + 2) * TM, :])
                    _tile_to_streams(va, vs, t, d)
                    _tile_to_streams(vb, vs, t + 1, d)
            _tile_to_streams(jnp.where(first, qt, 0.0), q0, t, d)
            _tile_to_streams(jnp.where(second, qt, 0.0), q1, t, d)
        kref, vref = (k_ref, v_ref) if d == 1 else (ks, vs)
        oref, lref = (o_ref, l_ref) if d == 1 else (os_, ls_ref)

        def blk(b, carry):
            r0 = pl.multiple_of(b * QBLK, QBLK)
            p0 = pl.multiple_of(jnp.maximum(b - 1, 0) * QBLK, QBLK)
            kk = jnp.concatenate([kref[pl.ds(p0, QBLK), :], kref[pl.ds(r0, QBLK), :]], axis=0)
            vv = jnp.concatenate([vref[pl.ds(p0, QBLK), :], vref[pl.ds(r0, QBLK), :]], axis=0)
            valid = _band_mask(b & (nb - 1))
            acc, lse = [], []
            for qh in (q0, q1):
                s = jnp.where(valid, _dot_nt(qh[pl.ds(r0, QBLK), :], kk), NEG)
                m = jnp.max(s, axis=-1, keepdims=True)
                e = jnp.exp(s - m)
                l = jnp.sum(e, axis=-1, keepdims=True)
                acc.append(_dot(e.astype(BF16), vv) * (1.0 / l))
                lse.append(m + jnp.log(l))
            f = first[:QBLK]
            oref[pl.ds(r0, QBLK), :] = jnp.where(f, acc[0], acc[1])
            lref[pl.ds(r0, QBLK), :] = jnp.where(f, lse[0], lse[1])
            return carry

        lax.fori_loop(0, S // QBLK, blk, 0, unroll=FWD_UNROLL)
        if d > 1:
            ptm = pt_ref[...]
            for t in range(NT):
                rows = slice(t * TM, (t + 1) * TM)
                o_ref[rows, :] = _split_dot(ptm, _tile_from_streams(os_, t, d))
                l_ref[rows, :] = _split_dot(ptm, _tile_from_streams(ls_ref, t, d))

    qkv_spec = pl.BlockSpec((S, LANES), lambda c: (0, g * NCHUNK + c))
    out_spec = pl.BlockSpec((S, LANES), lambda c: (0, c))
    n_out = 2 if d == 1 else 3
    inner = body if d > 1 else functools.partial(_drop_arg, body, 7)
    outs = pl.pallas_call(
        functools.partial(_skip_arg, inner, 5), name=f"attn_fwd_g{g}", grid=(NCHUNK,),
        in_specs=[qkv_spec] * 3 + [_full((TM, TM))] * 2 + [pl.BlockSpec(memory_space=pl.ANY)],
        out_specs=[out_spec] * n_out, out_shape=[_sds((S, GW), F32)] * n_out,
        scratch_shapes=[pltpu.VMEM((S, LANES), BF16)] * 4 + [pltpu.VMEM((S, LANES), F32)],
        compiler_params=_params(("parallel",)),
    )(q, k, v, jnp.asarray(perm, BF16), jnp.asarray(perm.T, BF16), after)
    return (outs[0], outs[1], outs[1]) if d == 1 else tuple(outs)


def _drop_arg(body, pos, *refs):
    return body(*refs[:pos], None, *refs[pos:])


def _attn_out(os_, ls, qm, kv0, z, x, w_out):
    def body(o0, o1, o2, l0, l1, l2, qm_ref, kv_ref, z_ref, x_ref, w_ref, h_ref, ybuf):
        _, mix = _merge((o0, o1, o2), (l0, l1, l2))
        sz, _ = _silu_parts(z_ref[...])
        ybuf[:, :GW] = (mix * sz[:, :GW]).astype(BF16)
        _, mo, _ = _mem_attn(qm_ref[...], kv_ref[...])
        ybuf[:, GW:] = (mo * sz[:, GW:]).astype(BF16)
        yb = ybuf[...]
        for s in range(4):
            cs = slice(s * SH_O, (s + 1) * SH_O)
            h_ref[:, cs] = x_ref[:, cs] + _dot(yb, w_ref[s])

    return pl.pallas_call(
        body, name="attn_out", grid=(NX,),
        in_specs=[_rows(GW, MX)] * 6 + [_rows(MW, MX), _full((NM, 2 * MW)), _rows(BR_A, MX), _rows(D, MX),
                                        _full((4, BR_A, SH_O))],
        out_specs=_rows(D, MX), out_shape=_sds((S, D), F32),
        scratch_shapes=[pltpu.VMEM((MX, BR_A), BF16)],
        compiler_params=_params(("parallel",)),
    )(*os_, *ls, qm, kv0, z, x, w_out)


def _in_proj_b(h1, g1, w_in):
    def body(x_ref, g_ref, w_ref, hn_ref, bg_ref, cg_ref, u_ref, qm_ref, z_ref, proj):
        xf = x_ref[...]
        hn = xf * lax.rsqrt(jnp.mean(xf * xf, axis=-1, keepdims=True) + EPS) * g_ref[...]
        hb = hn.astype(BF16)
        hn_ref[...] = hb
        for s in range(4):
            proj[:, s * SH_B:(s + 1) * SH_B] = _dot(hb, w_ref[s])
        bg_ref[...] = proj[:, :D]
        cg_ref[...] = proj[:, D:2 * D]
        u_ref[...] = proj[:, 2 * D:3 * D]
        qm_ref[...] = proj[:, 3 * D:3 * D + MW].astype(BF16)
        z_ref[...] = proj[:, 3 * D + MW:]

    return pl.pallas_call(
        body, name="in_proj_b", grid=(NT,),
        in_specs=[_rows(D), _full((1, D)), _full((4, D, SH_B))],
        out_specs=[_rows(D), _rows(D), _rows(D), _rows(D), _rows(MW), _rows(BR_B)],
        out_shape=[_sds((S, D), BF16), _sds((S, D), F32), _sds((S, D), F32), _sds((S, D), F32),
                   _sds((S, MW), BF16), _sds((S, BR_B), F32)],
        scratch_shapes=[pltpu.VMEM((TM, IN_B), F32)],
        compiler_params=_params(("parallel",)),
    )(h1, g1, w_in)


def _prev8(width):
    return pl.BlockSpec((8, width), lambda i: (jnp.maximum(i * (MX // 8) - 1, 0), 0))


def _conv_out_loss(bg, cg, u, cw, qm, kv1, z, h1, w_out, fg, tgt):
    def body(bg_ref, cg_ref, u_ref, cgp_ref, up_ref, cw_ref, qm_ref, kv_ref, z_ref, h_ref, w_ref, fg_ref, t_ref,
             dh_ref, loss_ref, dfg_ref, ybuf):
        i = pl.program_id(0)
        a, a1, a2 = _conv_taps(cg_ref[...], u_ref[...], cgp_ref[...], up_ref[...], i == 0)
        conv = cw_ref[0:1, :] * a2 + cw_ref[1:2, :] * a1 + cw_ref[2:3, :] * a
        sz, _ = _silu_parts(z_ref[...])
        ybuf[:, :D] = (bg_ref[...] * conv * sz[:, :D]).astype(BF16)
        _, mo, _ = _mem_attn(qm_ref[...], kv_ref[...])
        ybuf[:, D:] = (mo * sz[:, D:]).astype(BF16)
        h2 = h_ref[...] + _dot(ybuf[...], w_ref[...])
        rstd = lax.rsqrt(jnp.mean(h2 * h2, axis=-1, keepdims=True) + EPS)
        n = h2 * rstd
        fgv = fg_ref[...]
        err = n * fgv - t_ref[...]
        dout = err * (1.0 / D)
        dn = dout * fgv
        dh_ref[...] = rstd * (dn - n * jnp.mean(dn * n, axis=-1, keepdims=True))

        @pl.when(i == 0)
        def _():
            loss_ref[...] = jnp.zeros_like(loss_ref)
            dfg_ref[...] = jnp.zeros_like(dfg_ref)

        loss_ref[...] += jnp.sum(err * err) * (0.5 / D)
        dfg_ref[...] += jnp.sum(dout * n, axis=0, keepdims=True)

    return pl.pallas_call(
        body, name="conv_out_loss", grid=(NX,),
        in_specs=[_rows(D, MX), _rows(D, MX), _rows(D, MX), _prev8(D), _prev8(D), _full((8, D)), _rows(MW, MX),
                  _full((NM, 2 * MW)), _rows(BR_B, MX), _rows(D, MX), _full((BR_B, D)), _full((1, D)), _rows(D, MX)],
        out_specs=[_rows(D, MX), _full((1, 128)), _full((1, D))],
        out_shape=[_sds((S, D), F32), _sds((1, 128), F32), _sds((1, D), F32)],
        scratch_shapes=[pltpu.VMEM((MX, BR_B), BF16)],
        compiler_params=_params(("arbitrary",)),
    )(bg, cg, u, cg, u, cw, qm, kv1, z, h1, w_out, fg, tgt)


def _conv_bwd(dh2, bg, cg, u, cw, qm, kv1, z, w_out):
    rev = lambda i: (NX - 1 - i, 0)
    rows = lambda w: pl.BlockSpec((MX, w), rev)
    prev8 = pl.BlockSpec((8, D), lambda i: (jnp.maximum((NX - 1 - i) * (MX // 8) - 1, 0), 0))

    def body(dh_ref, bg_ref, cg_ref, u_ref, cgp_ref, up_ref, cw_ref, qm_ref, kv_ref, z_ref, w_ref,
             dproj_ref, dw_ref, dcw_ref, dkv_ref, dwb_ref, ybuf, carry):
        i = pl.program_id(0)

        @pl.when(i == 0)
        def _():
            dw_ref[...] = jnp.zeros_like(dw_ref)
            dcw_ref[...] = jnp.zeros_like(dcw_ref)
            dkv_ref[...] = jnp.zeros_like(dkv_ref)
            carry[...] = jnp.zeros_like(carry)

        dhb = dh_ref[...].astype(BF16)
        dy = _dot_nt(dhb, w_ref[...])
        kvv = kv_ref[...]
        p, mo, q4 = _mem_attn(qm_ref[...], kvv)
        szm, dszm = _silu_parts(z_ref[:, D:])
        ybuf[:, D:] = (mo * szm).astype(BF16)
        dym = dy[:, D:]
        dproj_ref[:, 3 * D + MW + D:] = (dym * mo * dszm).astype(BF16)
        first_tile = i == NX - 1
        for c in range(D // CONV_CHUNK):
            cs = slice(c * CONV_CHUNK, (c + 1) * CONV_CHUNK)
            bgv, cgv, uv = bg_ref[:, cs], cg_ref[:, cs], u_ref[:, cs]
            a, a1, a2 = _conv_taps(cgv, uv, cgp_ref[:, cs], up_ref[:, cs], first_tile)
            w0, w1, w2 = cw_ref[0:1, cs], cw_ref[1:2, cs], cw_ref[2:3, cs]
            conv = w0 * a2 + w1 * a1 + w2 * a
            mix = bgv * conv
            sz, dsz = _silu_parts(z_ref[:, cs])
            ybuf[:, cs] = (mix * sz).astype(BF16)
            dyc = dy[:, cs]
            dproj_ref[:, 3 * D + MW + c * CONV_CHUNK:3 * D + MW + (c + 1) * CONV_CHUNK] = (
                dyc * mix * dsz).astype(BF16)
            dmix = dyc * sz
            dproj_ref[:, cs] = (dmix * conv).astype(BF16)
            dc = dmix * bgv
            nxt = carry[:, cs]
            row = lax.broadcasted_iota(jnp.int32, dc.shape, 0)
            dc1 = jnp.where(row == MX - 1, nxt[0:1, :], pltpu.roll(dc, MX - 1, 0))
            dc2 = jnp.where(row == MX - 2, nxt[0:1, :],
                            jnp.where(row == MX - 1, nxt[1:2, :], pltpu.roll(dc, MX - 2, 0)))
            carry[:, cs] = dc[0:8, :]
            da = w2 * dc + w1 * dc1 + w0 * dc2
            dproj_ref[:, D + c * CONV_CHUNK:D + (c + 1) * CONV_CHUNK] = (da * uv).astype(BF16)
            dproj_ref[:, 2 * D + c * CONV_CHUNK:2 * D + (c + 1) * CONV_CHUNK] = (da * cgv).astype(BF16)
            dcw_ref[0:1, cs] += jnp.sum(dc * a2, axis=0, keepdims=True)
            dcw_ref[1:2, cs] += jnp.sum(dc * a1, axis=0, keepdims=True)
            dcw_ref[2:3, cs] += jnp.sum(dc * a, axis=0, keepdims=True)
        dw_ref[...] += _dot_tn(ybuf[...], dhb)
        dproj_ref[:, 3 * D:3 * D + MW] = _mem_attn_bwd(dym * szm, p, mo, q4, kvv, dkv_ref).astype(BF16)

        @pl.when(i == NX - 1)
        def _():
            dwb_ref[...] = dw_ref[...].astype(BF16)

    return pl.pallas_call(
        body, name="conv_bwd", grid=(NX,),
        in_specs=[rows(D), rows(D), rows(D), rows(D), prev8, prev8, _full((8, D)), rows(MW),
                  _full((NM, 2 * MW)), rows(BR_B), _full((BR_B, D))],
        out_specs=[rows(IN_B), _full((BR_B, D)), _full((8, D)), _full((NM, 2 * MW)), _full((BR_B, D))],
        out_shape=[_sds((S, IN_B), BF16), _sds((BR_B, D), F32), _sds((8, D), F32), _sds((NM, 2 * MW), F32),
                   _sds((BR_B, D), BF16)],
        scratch_shapes=[pltpu.VMEM((MX, BR_B), BF16), pltpu.VMEM((8, D), F32)],
        compiler_params=_params(("arbitrary",)),
    )(dh2, bg, cg, u, cg, u, cw, qm, kv1, z, w_out)


def _in_proj_bwd(dproj, w_in, xin, g, dres, after, width, name):
    sh = width // 4

    def body(dp_ref, w_ref, x_ref, g_ref, dr_ref, dx_ref, dg_ref):
        i = pl.program_id(0)
        dhn = _dot_nt(dp_ref[:, 0:sh], w_ref[0])
        for s in range(1, 4):
            dhn += _dot_nt(dp_ref[:, s * sh:(s + 1) * sh], w_ref[s])
        xf = x_ref[...]
        rstd = lax.rsqrt(jnp.mean(xf * xf, axis=-1, keepdims=True) + EPS)
        n = xf * rstd
        dn = dhn * g_ref[...]
        dx_ref[...] = dr_ref[...] + rstd * (dn - n * jnp.mean(dn * n, axis=-1, keepdims=True))

        @pl.when(i == 0)
        def _():
            dg_ref[...] = jnp.zeros_like(dg_ref)

        dg_ref[...] += jnp.sum(dhn * n, axis=0, keepdims=True)

    return pl.pallas_call(
        functools.partial(_skip_arg, body, 5), name=name, grid=(NT,),
        in_specs=[_rows(width), _full((4, D, sh)), _rows(D), _full((1, D)), _rows(D), pl.BlockSpec(memory_space=pl.ANY)],
        out_specs=[_rows(D), _full((1, D))],
        out_shape=[_sds((S, D), F32), _sds((1, D), F32)],
        compiler_params=_params(("arbitrary",)),
    )(dproj, w_in, xin, g, dres, after)


def _w_in_grad(hn, dproj, width, name):
    sh = width // 4

    def body(hn_ref, dp_ref, dw_ref, dwb_ref):
        dw = _dot_tn(hn_ref[...], dp_ref[...])
        dw_ref[0] = dw
        dwb_ref[0] = dw.astype(BF16)

    spec = pl.BlockSpec((1, D, sh), lambda s: (s, 0, 0))
    return pl.pallas_call(
        body, name=name, grid=(4,),
        in_specs=[_full((S, D)), pl.BlockSpec((S, sh), lambda s: (0, s))],
        out_specs=[spec, spec], out_shape=[_sds((4, D, sh), F32), _sds((4, D, sh), BF16)],
        compiler_params=_params(("parallel",)),
    )(hn, dproj)


def _attn_out_bwd(dh1, os_, ls, qm, kv0, z, w_out, after):
    ones_bd = np.kron(np.eye(GW // HD, dtype=np.float32), np.ones((HD, HD), np.float32))

    def body(dh_ref, o0, o1, o2, l0, l1, l2, qm_ref, kv_ref, z_ref, w_ref, bd_ref,
             do0, do1, do2, dd0, dd1, dd2, dqm_ref, dz_ref, dw_ref, dkv_ref, dwb_ref, ybuf):
        i = pl.program_id(0)

        @pl.when(i == 0)
        def _():
            dw_ref[...] = jnp.zeros_like(dw_ref)
            dkv_ref[...] = jnp.zeros_like(dkv_ref)

        ws, mix = _merge((o0, o1, o2), (l0, l1, l2))
        sz, dsz = _silu_parts(z_ref[...])
        kvv = kv_ref[...]
        p, mo, q4 = _mem_attn(qm_ref[...], kvv)
        ybuf[:, :GW] = (mix * sz[:, :GW]).astype(BF16)
        ybuf[:, GW:] = (mo * sz[:, GW:]).astype(BF16)
        yb = ybuf[...]
        dh = dh_ref[...]
        dy = None
        for s in range(4):
            dhb = dh[:, s * SH_O:(s + 1) * SH_O].astype(BF16)
            dw_ref[s] += _dot_tn(yb, dhb)
            part = _dot_nt(dhb, w_ref[s])
            dy = part if dy is None else dy + part
        dcat = dy * sz
        dz_ref[:, :GW] = (dy[:, :GW] * mix * dsz[:, :GW]).astype(BF16)
        dz_ref[:, GW:] = (dy[:, GW:] * mo * dsz[:, GW:]).astype(BF16)
        dmix = dcat[:, :GW]
        prod = dmix * mix
        hi = prod.astype(BF16)
        lo = (prod - hi.astype(F32)).astype(BF16)
        bd = bd_ref[...]
        tot = _dot(hi, bd) + _dot(lo, bd)
        for w, do_ref, dd_ref in zip(ws, (do0, do1, do2), (dd0, dd1, dd2)):
            do_ref[...] = (w * dmix).astype(BF16)
            dd_ref[...] = w * tot

        dqm_ref[...] = _mem_attn_bwd(dcat[:, GW:], p, mo, q4, kvv, dkv_ref).astype(BF16)

        @pl.when(i == NX - 1)
        def _():
            dwb_ref[...] = dw_ref[...].astype(BF16)

    return pl.pallas_call(
        functools.partial(_skip_arg, body, 12), name="attn_out_bwd", grid=(NX,),
        in_specs=[_rows(D, MX)] + [_rows(GW, MX)] * 6 + [_rows(MW, MX), _full((NM, 2 * MW)), _rows(BR_A, MX),
                                                           _full((4, BR_A, SH_O)), _full((GW, GW)),
                                                           pl.BlockSpec(memory_space=pl.ANY)],
        out_specs=[_rows(GW, MX)] * 6 + [_rows(MW, MX), _rows(BR_A, MX), _full((4, BR_A, SH_O)),
                                         _full((NM, 2 * MW)), _full((4, BR_A, SH_O))],
        out_shape=[_sds((S, GW), BF16)] * 3 + [_sds((S, GW), F32)] * 3 + [
            _sds((S, MW), BF16), _sds((S, BR_A), BF16), _sds((4, BR_A, SH_O), F32), _sds((NM, 2 * MW), F32),
            _sds((4, BR_A, SH_O), BF16)],
        scratch_shapes=[pltpu.VMEM((MX, BR_A), BF16)],
        compiler_params=_params(("arbitrary",)),
    )(dh1, *os_, *ls, qm, kv0, z, w_out, jnp.asarray(ones_bd, dtype=BF16), after)


def _attn_bwd(q, k, v, do, lse_s, dd, g):
    d = DILATIONS[g]
    nb = S // d // QBLK
    perm = _perm_matrix(d)

    def body(q_ref, k_ref, v_ref, do_ref, l_ref, dd_ref, p_ref, pt_ref, dq_ref, dk_ref, dv_ref,
             q0, q1, g0, g1, ks, vs, dds, dqs, dks, dvs):
        first, second = _head_masks()
        pm = p_ref[...]
        for t in range(NT):
            rows = slice(t * TM, (t + 1) * TM)
            if d == 1:
                qt = q_ref[rows, :].astype(F32)
                gt = do_ref[rows, :].astype(F32)
            else:
                qt, gt = _pair_dot(pm, q_ref[rows, :], do_ref[rows, :])
                kt, vt = _pair_dot(pm, k_ref[rows, :], v_ref[rows, :])
                _tile_to_streams(kt, ks, t, d)
                _tile_to_streams(vt, vs, t, d)
                _tile_to_streams(_split_dot(pm, dd_ref[rows, :]), dds, t, d)
            _tile_to_streams(jnp.where(first, qt, 0.0), q0, t, d)
            _tile_to_streams(jnp.where(second, qt, 0.0), q1, t, d)
            _tile_to_streams(jnp.where(first, gt, 0.0), g0, t, d)
            _tile_to_streams(jnp.where(second, gt, 0.0), g1, t, d)
        kref, vref, ddref = (k_ref, v_ref, dd_ref) if d == 1 else (ks, vs, dds)
        dqref, dkref, dvref = dqs, dks, dvs
        dkref[...] = jnp.zeros_like(dkref)
        dvref[...] = jnp.zeros_like(dvref)

        def blk(b, carry):
            r0 = pl.multiple_of(b * QBLK, QBLK)
            p0 = pl.multiple_of(jnp.maximum(b - 1, 0) * QBLK, QBLK)
            kk = jnp.concatenate([kref[pl.ds(p0, QBLK), :], kref[pl.ds(r0, QBLK), :]], axis=0)
            vv = jnp.concatenate([vref[pl.ds(p0, QBLK), :], vref[pl.ds(r0, QBLK), :]], axis=0)
            lb = l_ref[pl.ds(r0, QBLK), :]
            ddb = ddref[pl.ds(r0, QBLK), :]
            lcol = jnp.concatenate([lb[:, 0:1], lb[:, HD:HD + 1]], axis=0)
            dcol = jnp.concatenate([ddb[:, 0:1], ddb[:, HD:HD + 1]], axis=0)
            valid = _band_mask(b & (nb - 1))
            valid2 = jnp.concatenate([valid, valid], axis=0)
            qq = jnp.concatenate([q0[pl.ds(r0, QBLK), :], q1[pl.ds(r0, QBLK), :]], axis=0)
            gg = jnp.concatenate([g0[pl.ds(r0, QBLK), :], g1[pl.ds(r0, QBLK), :]], axis=0)
            p = jnp.where(valid2, jnp.exp(_dot_nt(qq, kk) - lcol), 0.0)
            ds = (p * (_dot_nt(gg, vv) - dcol)).astype(BF16)
            dq2 = _dot(ds, kk)
            dqref[pl.ds(r0, QBLK), :] = jnp.where(first[:QBLK], dq2[:QBLK], dq2[QBLK:])
            dkk = _dot_tn(ds, qq)
            dvv = _dot_tn(p.astype(BF16), gg)
            dkref[pl.ds(p0, QBLK), :] += dkk[:QBLK]
            dkref[pl.ds(r0, QBLK), :] += dkk[QBLK:]
            dvref[pl.ds(p0, QBLK), :] += dvv[:QBLK]
            dvref[pl.ds(r0, QBLK), :] += dvv[QBLK:]
            return carry

        lax.fori_loop(0, S // QBLK, blk, 0, unroll=BWD_UNROLL)

        ptm = pt_ref[...] if d > 1 else None
        for t in range(NT):
            rows = slice(t * TM, (t + 1) * TM)
            if d == 1:
                dq_ref[rows, :] = dqs[rows, :].astype(BF16)
                dk_ref[rows, :] = dks[rows, :].astype(BF16)
                dv_ref[rows, :] = dvs[rows, :].astype(BF16)
            else:
                tq, tk = _pair_dot(ptm, _tile_from_streams(dqs, t, d).astype(BF16),
                                   _tile_from_streams(dks, t, d).astype(BF16))
                dq_ref[rows, :] = tq.astype(BF16)
                dk_ref[rows, :] = tk.astype(BF16)
                if t % 2 == 0:
                    ta, tb = _pair_dot(ptm, _tile_from_streams(dvs, t, d).astype(BF16),
                                       _tile_from_streams(dvs, t + 1, d).astype(BF16))
                    dv_ref[rows, :] = ta.astype(BF16)
                    dv_ref[(t + 1) * TM:(t + 2) * TM, :] = tb.astype(BF16)

    qkv_spec = pl.BlockSpec((S, LANES), lambda c: (0, g * NCHUNK + c))
    one_spec = pl.BlockSpec((S, LANES), lambda c: (0, c))
    return pl.pallas_call(
        body, name=f"attn_bwd_g{g}", grid=(NCHUNK,),
        in_specs=[qkv_spec] * 3 + [one_spec] * 3 + [_full((TM, TM))] * 2, out_specs=[one_spec] * 3,
        out_shape=[_sds((S, GW), BF16)] * 3,
        scratch_shapes=[pltpu.VMEM((S, LANES), BF16)] * 6 + [pltpu.VMEM((S, LANES), F32)] * 4,
        compiler_params=_params(("parallel",)),
    )(q, k, v, do, lse_s, dd, jnp.asarray(perm, BF16), jnp.asarray(perm.T, BF16))


def _qkv_bwd(dqs, dks, dvs, dqm, dz, c, s1, s2):
    def body(q0, q1, q2, k0, k1, k2, v0, v1, v2, dqm_ref, dz_ref, c_ref, s1_ref, s2_ref, dp_ref):
        cc, a1, a2 = c_ref[...], s1_ref[...], s2_ref[...]
        for g, (qr, kr, vr) in enumerate(((q0, k0, v0), (q1, k1, v1), (q2, k2, v2))):
            for j in range(GW // 128):
                ls_ = slice(j * 128, (j + 1) * 128)
                c0 = g * GW + j * 128
                dp_ref[:, c0:c0 + 128] = (_rope_bwd(qr[:, ls_].astype(F32), cc, a1, a2) * SCALE).astype(BF16)
                dp_ref[:, NQ + c0:NQ + c0 + 128] = _rope_bwd(kr[:, ls_].astype(F32), cc, a1, a2).astype(BF16)
            dp_ref[:, 2 * NQ + g * GW:2 * NQ + (g + 1) * GW] = vr[...]
        dp_ref[:, 3 * NQ:3 * NQ + MW] = dqm_ref[...]
        dp_ref[:, 3 * NQ + MW:] = dz_ref[...]

    return pl.pallas_call(
        body, name="qkv_bwd", grid=(NT,),
        in_specs=[_rows(GW)] * 9 + [_rows(MW), _rows(BR_A), _rows(128), _rows(128), _rows(128)],
        out_specs=_rows(IN_A), out_shape=_sds((S, IN_A), BF16),
        compiler_params=_params(("parallel",)),
    )(*dqs, *dks, *dvs, dqm, dz, c, s1, s2)


def _mem_bwd(mem, mg, memn, wkv, dkv0, dkv1):
    def body(mem_ref, mg_ref, memn_ref, w_ref, d0_ref, d1_ref, dw_ref, dwb_ref, dg_ref):
        mf = mem_ref[...]
        n = mf * lax.rsqrt(jnp.mean(mf * mf, axis=-1, keepdims=True) + EPS)
        for i, d_ref in enumerate((d0_ref, d1_ref)):
            dkv = d_ref[...].astype(BF16)
            mn = memn_ref[i]
            for s in range(4):
                cs = slice(s * NM, (s + 1) * NM)
                dw = _dot_tn(mn[:, cs], dkv)
                dw_ref[s, i] = dw
                dwb_ref[s, i] = dw.astype(BF16)
                dmn = _dot_nt(dkv, w_ref[s, i])
                dg_ref[i:i + 1, cs] = jnp.sum(dmn * n[:, cs], axis=0, keepdims=True)

    return pl.pallas_call(
        body, name="mem_bwd", grid=(1,),
        in_specs=[_full((NM, D)), _full((2, D)), _full((2, NM, D)), _full((4, 2, NM, 2 * MW)),
                  _full((NM, 2 * MW)), _full((NM, 2 * MW))],
        out_specs=[_full((4, 2, NM, 2 * MW)), _full((4, 2, NM, 2 * MW)), _full((2, D))],
        out_shape=[_sds((4, 2, NM, 2 * MW), F32), _sds((4, 2, NM, 2 * MW), BF16), _sds((2, D), F32)],
        compiler_params=_params(("arbitrary",)),
    )(mem, mg, memn, wkv, dkv0, dkv1)


MESH = pl.DeviceIdType.MESH
ANY = pl.BlockSpec(memory_space=pl.ANY)
BIG = (("wkv", 2, NM, 2 * MW), ("w_in_a", 1, D, SH_A), ("w_out_a", 1, BR_A, SH_O),
       ("w_in_b", 1, D, SH_B), ("w_out_b", 1, BR_B // 4, D))
NBIG = len(BIG)
CW_ROWS = 8


def _place():
    x, y, c = lax.axis_index("x"), lax.axis_index("y"), lax.axis_index("c")
    chips = ((1 - x, y), (x, 1 - y), (1 - x, 1 - y))
    return x, y, c, chips


def _remote(src, dst, ssem, rsem, dev):
    return pltpu.make_async_remote_copy(src_ref=src, dst_ref=dst, send_sem=ssem, recv_sem=rsem,
                                        device_id=dev, device_id_type=MESH)


def _cast_weights(place, ws, after, idx, name):
    nblk = 4
    n = len(idx)
    dims = [BIG[w][1:] for w in idx]

    def body(pref, *refs):
        for i in range(n):
            refs[n + 1 + i][0] = refs[i][...].astype(BF16)

    grid_spec = pltpu.PrefetchScalarGridSpec(
        num_scalar_prefetch=1, grid=(nblk,),
        in_specs=[pl.BlockSpec((k, r // nblk, cdim), lambda i, pref: (0, i, 0)) for k, r, cdim in dims]
        + [pl.BlockSpec(memory_space=pl.ANY)],
        out_specs=[pl.BlockSpec((1, k, r // nblk, cdim), lambda i, pref: (pref[1], 0, i, 0)) for k, r, cdim in dims])
    return pl.pallas_call(
        body, name=name, grid_spec=grid_spec,
        out_shape=[_sds((4, k, r, cdim), BF16) for k, r, cdim in dims],
        compiler_params=_params(("parallel",)),
    )(place, *ws, after)


LAYER_A = (0, 1, 2)
LAYER_B = (3, 4)
HBM = pl.BlockSpec(memory_space=pltpu.HBM)
SEM = pl.BlockSpec(memory_space=pltpu.SEMAPHORE)
EFFECT = pltpu.SideEffectType.DATAFLOW_SIDE_EFFECTING
TOKEN = (8, 128)


def _half(ref, w, which):
    h = BIG[w][2] // 2
    return ref.at[:, pl.ds(which * h, h), :]


def _skip_arg(body, pos, *refs):
    return body(*refs[:pos], *refs[pos + 1:])


def _gather_start(wb, after, idx, name, barrier_id):
    n = len(idx)

    def body(*refs):
        src = refs[:n]
        send_sems, recv_sems = refs[n + 1], refs[n + 2]
        token = refs[2 * n + 3]
        x, y, c, chips = _place()
        _peer_barrier([(px, py, c) for px, py in chips])
        me = 2 * x + y
        for i in range(n):
            for j, (px, py) in enumerate(chips):
                mine = _half(src[i].at[me], idx[i], c)
                _remote(mine, mine, send_sems.at[j * n + i], recv_sems.at[j * n + i], (px, py, c)).start()
        token[...] = jnp.zeros(TOKEN, F32)

    outs = pl.pallas_call(
        body, name=name, in_specs=[HBM] * n + [ANY],
        out_specs=(SEM, SEM) + (HBM,) * n + (pl.BlockSpec(memory_space=pltpu.VMEM),),
        out_shape=(pltpu.SemaphoreType.DMA((3 * n,)), pltpu.SemaphoreType.DMA((3 * n,)))
        + tuple(pltpu.HBM(w.shape, w.dtype) for w in wb) + (_sds(TOKEN, F32),),
        input_output_aliases={i: 2 + i for i in range(n)},
        compiler_params=pltpu.CompilerParams(has_side_effects=EFFECT, collective_id=barrier_id),
    )(*[pltpu.with_memory_space_constraint(w, pltpu.HBM) for w in wb], after)
    return outs[0], outs[1], list(outs[2:2 + n]), outs[2 + n]


def _gather_wait(send_sems, recv_sems, wb, after, idx, name, started=None):
    n = len(idx)
    started = idx if started is None else started
    n_all = len(started)
    pos = [started.index(w) for w in idx]

    def body(*refs):
        buf = refs[:n]
        send_sems, recv_sems = refs[n], refs[n + 1]
        x, y, c, chips = _place()
        me = 2 * x + y
        for j, (px, py) in enumerate(chips):
            for i in range(n):
                mine = _half(buf[i].at[me], idx[i], c)
                got = _half(buf[i].at[2 * px + py], idx[i], c)
                k = j * n_all + pos[i]
                _remote(mine, mine, send_sems.at[k], recv_sems.at[k], (px, py, c)).wait_send()
                _remote(got, got, send_sems.at[k], recv_sems.at[k], (px, py, c)).wait_recv()

    outs = pl.pallas_call(
        body, name=name, in_specs=[HBM] * n + [SEM, SEM] + [ANY] * len(after), out_specs=(HBM,) * n,
        out_shape=tuple(pltpu.HBM(w.shape, w.dtype) for w in wb),
        input_output_aliases={i: i for i in range(n)},
        compiler_params=pltpu.CompilerParams(has_side_effects=EFFECT),
    )(*wb, send_sems, recv_sems, *after)
    return list(outs)


def _gather_forward(wb, idx, name, barrier_id):
    n = len(idx)

    def body(*refs):
        dst = refs[n:2 * n]
        send_sems, recv_sems = refs[2 * n], refs[2 * n + 1]
        x, y, c, chips = _place()
        _sibling_barrier(x, y, c)
        cps = []
        for j, (px, py) in enumerate(chips):
            for i in range(n):
                got = _half(dst[i].at[2 * px + py], idx[i], c)
                cps.append(_remote(got, got, send_sems.at[j, i], recv_sems.at[j, i], (x, y, 1 - c)))
                cps[-1].start()
        for j, (px, py) in enumerate(chips):
            for i in range(n):
                got = _half(dst[i].at[2 * px + py], idx[i], 1 - c)
                _remote(got, got, send_sems.at[j, i], recv_sems.at[j, i], (x, y, 1 - c)).wait_recv()
        for cp in cps:
            cp.wait_send()

    return pl.pallas_call(
        body, name=name, in_specs=[ANY] * n, out_specs=[ANY] * n, out_shape=[_sds(w.shape, BF16) for w in wb],
        input_output_aliases={i: i for i in range(n)},
        scratch_shapes=[pltpu.SemaphoreType.DMA((3, n)), pltpu.SemaphoreType.DMA((3, n))],
        compiler_params=pltpu.CompilerParams(collective_id=barrier_id),
    )(*wb)


def _forward_start(wb, cw, after, idx, name, barrier_id):
    n = len(idx)
    m = n if cw is None else n + 2

    def body(*refs):
        buf = refs[:n]
        send_sems, recv_sems = refs[m + 1], refs[m + 2]
        token = refs[2 * m + 3]
        x, y, c, chips = _place()
        _peer_barrier([(x, y, 1 - c)] + ([] if cw is None else [(px, py, c) for px, py in chips]))
        for j, (px, py) in enumerate(chips):
            for i in range(n):
                got = _half(buf[i].at[2 * px + py], idx[i], c)
                _remote(got, got, send_sems.at[j * (n + 1) + i], recv_sems.at[j * (n + 1) + i], (x, y, 1 - c)).start()
            if cw is not None:
                _remote(refs[n], refs[n + 1].at[2 * x + y], send_sems.at[j * (n + 1) + n],
                        recv_sems.at[j * (n + 1) + n], (px, py, c)).start()
        token[...] = jnp.zeros(TOKEN, F32)

    arrays = list(wb) if cw is None else list(wb) + [cw, lax.empty((4, CW_ROWS, SH_O), F32)]
    outs = pl.pallas_call(
        body, name=name, in_specs=[HBM] * m + [ANY],
        out_specs=(SEM, SEM) + (HBM,) * m + (pl.BlockSpec(memory_space=pltpu.VMEM),),
        out_shape=(pltpu.SemaphoreType.DMA((3 * (n + 1),)), pltpu.SemaphoreType.DMA((3 * (n + 1),)))
        + tuple(pltpu.HBM(a.shape, a.dtype) for a in arrays) + (_sds(TOKEN, F32),),
        input_output_aliases={i: 2 + i for i in range(m)},
        compiler_params=pltpu.CompilerParams(has_side_effects=EFFECT, collective_id=barrier_id),
    )(*[pltpu.with_memory_space_constraint(a, pltpu.HBM) for a in arrays], after)
    return outs[0], outs[1], list(outs[2:2 + m]), outs[2 + m]


def _forward_wait(send_sems, recv_sems, arrays, after, idx, with_cw, name):
    n = len(idx)
    m = len(arrays)

    def body(*refs):
        buf = refs[:n]
        send_sems, recv_sems = refs[m], refs[m + 1]
        x, y, c, chips = _place()
        for j, (px, py) in enumerate(chips):
            for i in range(n):
                sent = _half(buf[i].at[2 * px + py], idx[i], c)
                got = _half(buf[i].at[2 * px + py], idx[i], 1 - c)
                k = j * (n + 1) + i
                _remote(sent, sent, send_sems.at[k], recv_sems.at[k], (x, y, 1 - c)).wait_send()
                _remote(got, got, send_sems.at[k], recv_sems.at[k], (x, y, 1 - c)).wait_recv()
            if with_cw:
                k = j * (n + 1) + n
                theirs = refs[n + 1].at[2 * px + py]
                _remote(refs[n], theirs, send_sems.at[k], recv_sems.at[k], (px, py, c)).wait_send()
                _remote(refs[n], theirs, send_sems.at[k], recv_sems.at[k], (px, py, c)).wait_recv()

    outs = pl.pallas_call(
        body, name=name, in_specs=[HBM] * m + [SEM, SEM] + [ANY] * len(after), out_specs=(HBM,) * m,
        out_shape=tuple(pltpu.HBM(a.shape, a.dtype) for a in arrays),
        input_output_aliases={i: i for i in range(m)},
        compiler_params=pltpu.CompilerParams(has_side_effects=EFFECT),
    )(*arrays, send_sems, recv_sems, *after)
    return list(outs)


def _peer_barrier(peers):
    barrier = pltpu.get_barrier_semaphore()
    for peer in peers:
        pl.semaphore_signal(barrier, inc=1, device_id=peer, device_id_type=MESH)
    pl.semaphore_wait(barrier, len(peers))


def _sibling_barrier(x, y, c):
    _peer_barrier([(x, y, 1 - c)])


def _pair_exchange(gs, idx, name, barrier_id):
    n = len(idx)

    def body(*refs):
        src, dst = refs[:n], refs[n:2 * n]
        send_sems, recv_sems = refs[2 * n:]
        x, y, c, _ = _place()
        _sibling_barrier(x, y, c)
        cps = []
        for i in range(n):
            h = BIG[idx[i]][2] // 2
            cps.append(_remote(src[i].at[:, :, pl.ds((1 - c) * h, h), :], dst[i], send_sems.at[i], recv_sems.at[i],
                               (x, y, 1 - c)))
            cps[-1].start()
        for cp in cps:
            cp.wait()

    return pl.pallas_call(
        body, name=name, in_specs=[ANY] * n, out_specs=[ANY] * n,
        out_shape=[_sds((4, BIG[w][1], BIG[w][2] // 2, BIG[w][3]), BF16) for w in idx],
        scratch_shapes=[pltpu.SemaphoreType.DMA((n,)), pltpu.SemaphoreType.DMA((n,))],
        compiler_params=pltpu.CompilerParams(collective_id=barrier_id),
    )(*gs)


def _pair_start(gs, idx, name, barrier_id):
    n = len(idx)

    def body(*refs):
        src, land = refs[:n], refs[n:2 * n]
        send_sems, recv_sems = refs[2 * n], refs[2 * n + 1]
        token = refs[4 * n + 2]
        x, y, c, _ = _place()
        _sibling_barrier(x, y, c)
        for i in range(n):
            h = BIG[idx[i]][2] // 2
            _remote(src[i].at[:, :, pl.ds((1 - c) * h, h), :], land[i], send_sems.at[i], recv_sems.at[i],
                    (x, y, 1 - c)).start()
        token[...] = jnp.zeros(TOKEN, F32)

    lands = [lax.empty((4, BIG[w][1], BIG[w][2] // 2, BIG[w][3]), BF16) for w in idx]
    arrays = list(gs) + lands
    outs = pl.pallas_call(
        body, name=name, in_specs=[HBM] * (2 * n),
        out_specs=(SEM, SEM) + (HBM,) * (2 * n) + (pl.BlockSpec(memory_space=pltpu.VMEM),),
        out_shape=(pltpu.SemaphoreType.DMA((n,)), pltpu.SemaphoreType.DMA((n,)))
        + tuple(pltpu.HBM(a.shape, a.dtype) for a in arrays) + (_sds(TOKEN, F32),),
        input_output_aliases={i: 2 + i for i in range(2 * n)},
        compiler_params=pltpu.CompilerParams(has_side_effects=EFFECT, collective_id=barrier_id),
    )(*[pltpu.with_memory_space_constraint(a, pltpu.HBM) for a in arrays])
    return outs[0], outs[1], list(outs[2:2 + n]), list(outs[2 + n:2 + 2 * n]), outs[2 + 2 * n]


def _pair_wait(send_sems, recv_sems, gs, lands, after, idx, name):
    n = len(idx)

    def body(*refs):
        src, land = refs[:n], refs[n:2 * n]
        send_sems, recv_sems = refs[2 * n], refs[2 * n + 1]
        x, y, c, _ = _place()
        for i in range(n):
            h = BIG[idx[i]][2] // 2
            cp = _remote(src[i].at[:, :, pl.ds((1 - c) * h, h), :], land[i], send_sems.at[i], recv_sems.at[i],
                         (x, y, 1 - c))
            cp.wait_send()
            cp.wait_recv()

    arrays = list(gs) + list(lands)
    outs = pl.pallas_call(
        body, name=name, in_specs=[HBM] * (2 * n) + [SEM, SEM] + [ANY] * len(after), out_specs=(HBM,) * (2 * n),
        out_shape=tuple(pltpu.HBM(a.shape, a.dtype) for a in arrays),
        input_output_aliases={i: i for i in range(2 * n)},
        compiler_params=pltpu.CompilerParams(has_side_effects=EFFECT),
    )(*arrays, send_sems, recv_sems, *after)
    return list(outs[:n]), list(outs[n:])


def _pair_sums(place, gs, r1s, idx, name):
    n = len(idx)
    dims = [(BIG[w][1], BIG[w][2] // 2, BIG[w][3]) for w in idx]

    def body(pref, *refs):
        for i in range(n):
            refs[2 * n + i][...] = (refs[i][...] + refs[n + i][...].astype(F32)).astype(BF16)

    mine = [pl.BlockSpec((1, k, h, cdim), lambda s, pref: (s, 0, pref[0], 0)) for k, h, cdim in dims]
    whole = [pl.BlockSpec((1, k, h, cdim), lambda s, pref: (s, 0, 0, 0)) for k, h, cdim in dims]
    grid_spec = pltpu.PrefetchScalarGridSpec(num_scalar_prefetch=1, grid=(4,), in_specs=mine + whole, out_specs=whole)
    return pl.pallas_call(
        body, name=name, grid_spec=grid_spec, out_shape=[_sds((4, k, h, cdim), BF16) for k, h, cdim in dims],
        compiler_params=_params(("parallel",)),
    )(place, *gs, *r1s)


def _chip_start(ps, idx, name, barrier_id):
    n = len(idx)

    def body(*refs):
        src, land = refs[:n], refs[n:2 * n]
        send_sems, recv_sems = refs[2 * n], refs[2 * n + 1]
        token = refs[4 * n + 2]
        x, y, c, chips = _place()
        _peer_barrier([(px, py, c) for px, py in chips])
        for j, (px, py) in enumerate(chips):
            for i in range(n):
                _remote(src[i].at[2 * px + py], land[i].at[j], send_sems.at[j * n + i], recv_sems.at[j * n + i],
                        (px, py, c)).start()
        token[...] = jnp.zeros(TOKEN, F32)

    lands = [lax.empty((3,) + p.shape[1:], BF16) for p in ps]
    outs = pl.pallas_call(
        body, name=name, in_specs=[HBM] * (2 * n),
        out_specs=(SEM, SEM) + (HBM,) * (2 * n) + (pl.BlockSpec(memory_space=pltpu.VMEM),),
        out_shape=(pltpu.SemaphoreType.DMA((3 * n,)), pltpu.SemaphoreType.DMA((3 * n,)))
        + tuple(pltpu.HBM(a.shape, a.dtype) for a in list(ps) + lands) + (_sds(TOKEN, F32),),
        input_output_aliases={i: 2 + i for i in range(2 * n)},
        compiler_params=pltpu.CompilerParams(has_side_effects=EFFECT, collective_id=barrier_id),
    )(*[pltpu.with_memory_space_constraint(a, pltpu.HBM) for a in list(ps) + lands])
    return outs[0], outs[1], list(outs[2:2 + n]), list(outs[2 + n:2 + 2 * n]), outs[2 + 2 * n]


def _chip_wait(send_sems, recv_sems, ps, lands, after, idx, name):
    n = len(idx)

    def body(*refs):
        src, land = refs[:n], refs[n:2 * n]
        send_sems, recv_sems = refs[2 * n], refs[2 * n + 1]
        x, y, c, chips = _place()
        for j, (px, py) in enumerate(chips):
            for i in range(n):
                cp = _remote(src[i].at[2 * px + py], land[i].at[j], send_sems.at[j * n + i], recv_sems.at[j * n + i],
                             (px, py, c))
                cp.wait_send()
                cp.wait_recv()

    arrays = list(ps) + list(lands)
    outs = pl.pallas_call(
        body, name=name, in_specs=[HBM] * (2 * n) + [SEM, SEM] + [ANY] * len(after), out_specs=(HBM,) * (2 * n),
        out_shape=tuple(pltpu.HBM(a.shape, a.dtype) for a in arrays),
        input_output_aliases={i: i for i in range(2 * n)},
        compiler_params=pltpu.CompilerParams(has_side_effects=EFFECT),
    )(*arrays, send_sems, recv_sems, *after)
    return list(outs[n:])


def _chip_sums(place, gs, r1s, r2s, idx, name):
    n = len(idx)
    dims = [(BIG[w][1], BIG[w][2] // 4, BIG[w][3]) for w in idx]

    def body(pref, *refs):
        for i in range(n):
            acc = refs[i][0] + refs[n + i][0].astype(F32)
            for j in range(3):
                acc = acc + refs[2 * n + i][j].astype(F32)
            refs[3 * n + i][...] = acc

    in_specs = ([pl.BlockSpec((1, k, q, cdim), lambda t, pref: (pref[1], 0, pref[0] * 2 + t, 0)) for k, q, cdim in dims]
                + [pl.BlockSpec((1, k, q, cdim), lambda t, pref: (pref[1], 0, t, 0)) for k, q, cdim in dims]
                + [pl.BlockSpec((3, k, q, cdim), lambda t, pref: (0, 0, t, 0)) for k, q, cdim in dims])
    out_specs = [pl.BlockSpec((k, q, cdim), lambda t, pref: (0, pref[0] * 2 + t, 0)) for k, q, cdim in dims]
    grid_spec = pltpu.PrefetchScalarGridSpec(num_scalar_prefetch=1, grid=(2,), in_specs=in_specs, out_specs=out_specs)
    return pl.pallas_call(
        body, name=name, grid_spec=grid_spec, out_shape=[_sds(BIG[w][1:], F32) for w in idx],
        compiler_params=_params(("parallel",)),
    )(place, *gs, *r1s, *r2s)


def _pair_gather(hs, idx, name, barrier_id):
    n = len(idx)

    def body(*refs):
        dst = refs[n:2 * n]
        send_sems, recv_sems = refs[2 * n:]
        x, y, c, _ = _place()
        _sibling_barrier(x, y, c)
        cps = []
        for i in range(n):
            mine = _half(dst[i], idx[i], c)
            cps.append(_remote(mine, mine, send_sems.at[i], recv_sems.at[i], (x, y, 1 - c)))
            cps[-1].start()
        for i in range(n):
            theirs = _half(dst[i], idx[i], 1 - c)
            _remote(theirs, theirs, send_sems.at[i], recv_sems.at[i], (x, y, 1 - c)).wait_recv()
        for cp in cps:
            cp.wait_send()

    return pl.pallas_call(
        body, name=name, in_specs=[ANY] * n, out_specs=[ANY] * n,
        out_shape=[_sds(BIG[w][1:], F32) for w in idx],
        input_output_aliases={i: i for i in range(n)},
        scratch_shapes=[pltpu.SemaphoreType.DMA((n,)), pltpu.SemaphoreType.DMA((n,))],
        compiler_params=pltpu.CompilerParams(collective_id=barrier_id),
    )(*hs)


SMALL_ROWS = 40


def _adamw_math(w, g, m, v):
    m = ADAM_B1 * m + (1.0 - ADAM_B1) * g
    v = ADAM_B2 * v + (1.0 - ADAM_B2) * (g * g)
    m_hat = m / (1.0 - ADAM_B1 ** ADAM_STEP)
    v_hat = v / (1.0 - ADAM_B2 ** ADAM_STEP)
    delta = -ADAM_LR * (m_hat / (jnp.sqrt(v_hat) + ADAM_EPS) + ADAM_WD * w)
    return delta, m, v


def _small_start(pack, after):
    def body(pack_ref, land_ref, after_ref, send_sems, recv_sems, pack_thru, land_thru, token):
        x, y, c, _ = _place()
        for r in range(1, 8):
            peer = (x if not r & 4 else 1 - x, y if not r & 2 else 1 - y, c if not r & 1 else 1 - c)
            _remote(pack_ref, land_ref.at[r - 1], send_sems.at[r - 1], recv_sems.at[r - 1], peer).start()
        token[...] = jnp.zeros(TOKEN, F32)

    land = lax.empty((7, SMALL_ROWS, D), F32)
    outs = pl.pallas_call(
        body, name="small_start", in_specs=[HBM, HBM, ANY],
        out_specs=(SEM, SEM, HBM, HBM, pl.BlockSpec(memory_space=pltpu.VMEM)),
        out_shape=(pltpu.SemaphoreType.DMA((7,)), pltpu.SemaphoreType.DMA((7,)), pltpu.HBM(pack.shape, F32),
                   pltpu.HBM(land.shape, F32), _sds(TOKEN, F32)),
        input_output_aliases={0: 2, 1: 3},
        compiler_params=pltpu.CompilerParams(has_side_effects=EFFECT),
    )(pltpu.with_memory_space_constraint(pack, pltpu.HBM), pltpu.with_memory_space_constraint(land, pltpu.HBM), after)
    return outs


def _small_wait(send_sems, recv_sems, pack, land, after):
    def body(pack_ref, land_ref, send_sems, recv_sems, *rest):
        x, y, c, _ = _place()
        for r in range(1, 8):
            peer = (x if not r & 4 else 1 - x, y if not r & 2 else 1 - y, c if not r & 1 else 1 - c)
            cp = _remote(pack_ref, land_ref.at[r - 1], send_sems.at[r - 1], recv_sems.at[r - 1], peer)
            cp.wait_send()
            cp.wait_recv()

    return pl.pallas_call(
        body, name="small_wait", in_specs=[HBM, HBM, SEM, SEM] + [ANY] * len(after), out_specs=(HBM, HBM),
        out_shape=(pltpu.HBM(pack.shape, F32), pltpu.HBM(land.shape, F32)),
        input_output_aliases={0: 0, 1: 1},
        compiler_params=pltpu.CompilerParams(has_side_effects=EFFECT),
    )(pack, land, send_sems, recv_sems, *after)


def _small_update(place, pack, land, ws, ms, vs):
    n = len(ws)

    def body(pref, pack_ref, land_ref, *refs):
        chip = pref[1]
        me = 2 * chip + pref[0]
        own = pack_ref[...]
        tot = None
        for dev in range(8):
            r = jnp.bitwise_xor(me, dev)
            term = jnp.where(r == 0, own, land_ref[jnp.maximum(r - 1, 0)])
            tot = term if tot is None else tot + term
        out, buf = refs[3 * n:-1], refs[-1]
        buf[...] = tot
        g_conv = jnp.zeros((3, SH_O), F32)
        for s in range(4):
            g_conv = g_conv + jnp.where(chip == s, buf[24:27, s * SH_O:(s + 1) * SH_O], 0.0)
        gs = [buf[0:2, :], buf[8:10, :], buf[16:17, :], g_conv]
        out[0][...] = buf[32:33, 0:128]
        for i in range(n):
            d, nm, nv = _adamw_math(refs[i][...], gs[i], refs[n + i][...], refs[2 * n + i][...])
            out[1 + i][...] = gs[i]
            out[1 + n + i][...] = d
            out[1 + 2 * n + i][...] = nm
            out[1 + 3 * n + i][...] = nv

    def full(shape):
        nd = len(shape)
        return pl.BlockSpec(shape, lambda i, pref: (0,) * nd)

    specs = [full(w.shape) for w in ws]
    grid_spec = pltpu.PrefetchScalarGridSpec(
        num_scalar_prefetch=1, grid=(1,),
        in_specs=[full(pack.shape), full(land.shape)] + specs * 3, out_specs=[full((1, 128))] + specs * 4,
        scratch_shapes=[pltpu.VMEM((SMALL_ROWS, D), F32)])
    outs = pl.pallas_call(
        body, name="small_update", grid_spec=grid_spec,
        out_shape=[_sds((1, 128), F32)] + [_sds(w.shape, F32) for w in ws] * 4,
        compiler_params=_params(("arbitrary",)),
    )(place, pack, land, *ws, *ms, *vs)
    return outs[0], outs[1:1 + n], outs[1 + n:1 + 2 * n], outs[1 + 2 * n:1 + 3 * n], outs[1 + 3 * n:]


def _adamw_layer(ws, gs, ms, vs, idx, name):
    n = len(idx)
    dims = [(BIG[w][1], BIG[w][2] // 4, BIG[w][3]) for w in idx]

    def body(*refs):
        for i in range(n):
            gv = refs[n + i][...]
            d, nm, nv = _adamw_math(refs[i][...], gv, refs[2 * n + i][...], refs[3 * n + i][...])
            refs[4 * n + i][...] = d
            refs[5 * n + i][...] = nm
            refs[6 * n + i][...] = nv
            refs[7 * n + i][...] = gv

    specs = [pl.BlockSpec((k, q, cdim), lambda t: (0, t, 0)) for k, q, cdim in dims]
    outs = pl.pallas_call(
        body, name=name, grid=(4,), in_specs=specs * 4, out_specs=specs * 4,
        out_shape=[_sds(BIG[w][1:], F32) for w in idx] * 4,
        compiler_params=_params(("parallel",)),
    )(*ws, *gs, *ms, *vs)
    return [tuple(outs[j * n + i] for j in range(4)) for i in range(n)]


def _pad_rows(a, rows):
    return jnp.pad(a, ((0, rows - a.shape[0]), (0, 0)))


def kernel(x, mem, positions, norm_g, mem_norm_g, w_mem_kv, attn_w_in, attn_w_out, conv_w_in, conv_w, conv_w_out, final_g, loss_target, m_norm_g, m_mem_norm_g, m_w_mem_kv, m_attn_w_in, m_attn_w_out, m_conv_w_in, m_conv_w, m_conv_w_out, m_final_g, v_norm_g, v_mem_norm_g, v_w_mem_kv, v_attn_w_in, v_attn_w_out, v_conv_w_in, v_conv_w, v_conv_w_out, v_final_g):
    mx, my, mc = lax.axis_index("x"), lax.axis_index("y"), lax.axis_index("c")
    place = jnp.stack([mc, 2 * mx + my]).astype(jnp.int32)

    w_big = [w_mem_kv, attn_w_in, attn_w_out, conv_w_in, conv_w_out]
    m_big = [m_w_mem_kv, m_attn_w_in, m_attn_w_out, m_conv_w_in, m_conv_w_out]
    v_big = [v_w_mem_kv, v_attn_w_in, v_attn_w_out, v_conv_w_in, v_conv_w_out]
    first, rest = (1,), (0, 2, 3, 4)
    wb1 = _cast_weights(place, [w_big[i] for i in first], place, first, "cast_w_in_a")
    a1_send, a1_recv, a1_bufs, a1_token = _gather_start(wb1, place, first, "gather_a1_start", 4)
    wbr = _cast_weights(place, [w_big[i] for i in rest], a1_token, rest, "cast_weights")
    r_send, r_recv, r_bufs, gb_token = _gather_start(wbr, a1_token, rest, "gather_rest_start", 5)
    a2_send, a2_recv, gb_send, gb_recv = r_send, r_recv, r_send, r_recv
    a2_bufs, gb_bufs = r_bufs[:2], r_bufs[2:]
    started, rest = rest, (0, 2)

    xs, tgt = x[0], loss_target[0]
    g0, g1 = norm_g[0:1], norm_g[1:2]
    rc, rs1, rs2 = _rope_tables(positions[0].astype(F32).reshape(S, 1), gb_token)
    a1_bufs = _gather_wait(a1_send, a1_recv, a1_bufs, [rc], first, "gather_a1_wait")
    w_in_a = _gather_forward(a1_bufs, first, "gather_a1_forward", 0)[0].reshape(4, D, SH_A)
    hn0, q, k, v, qm0, z0 = _in_proj_a(xs, g0, w_in_a, rc, rs1, rs2, gb_token)
    a2_bufs = _gather_wait(a2_send, a2_recv, a2_bufs, [q], rest, "gather_a2_wait", started)
    f2_send, f2_recv, a2_bufs, f2_token = _forward_start(a2_bufs, None, q, rest, "forward_a2_start", 9)
    fwd = [_attn_fwd(q, k, v, 0, f2_token)]
    fwd.append(_attn_fwd(q, k, v, 1, fwd[0][0]))
    fwd.append(_attn_fwd(q, k, v, 2, fwd[1][0]))
    os_, ls, lss = [f[0] for f in fwd], [f[1] for f in fwd], [f[2] for f in fwd]
    cw_own = _pad_rows(conv_w[0], CW_ROWS)
    gb_bufs = _gather_wait(gb_send, gb_recv, gb_bufs, [os_[2]], LAYER_B, "gather_b_wait", started)
    fb_send, fb_recv, gb_bufs, fb_token = _forward_start(gb_bufs, cw_own, os_[2], LAYER_B, "forward_b_start", 10)
    wkv_f, w_out_a = _forward_wait(f2_send, f2_recv, a2_bufs, [os_[2], fb_token], rest, False, "forward_a2_wait")
    w_out_a = w_out_a.reshape(4, BR_A, SH_O)
    memn, kv = _mem_fwd(mem[0], mem_norm_g, wkv_f)
    h1 = _attn_out(os_, ls, qm0, kv[0], z0, xs, w_out_a)

    w_in_b, w_out_b, _, cw_f = _forward_wait(fb_send, fb_recv, gb_bufs, [h1], LAYER_B, True, "forward_b_wait")
    w_in_b = w_in_b.reshape(4, D, SH_B)
    w_out_b = w_out_b.reshape(BR_B, D)
    cw_f = lax.dynamic_update_slice(cw_f, cw_own[None], (2 * mx + my, 0, 0))
    cw8 = cw_f.transpose(1, 0, 2).reshape(CW_ROWS, D)
    hn1, bg, cg, u, qm1, z1 = _in_proj_b(h1, g1, w_in_b)
    dh2, loss_part, dfg = _conv_out_loss(bg, cg, u, cw8, qm1, kv[1], z1, h1, w_out_b, final_g.reshape(1, D), tgt)

    dproj_b, dw_out_b, dcw, dkv1, dw_out_b16 = _conv_bwd(dh2, bg, cg, u, cw8, qm1, kv[1], z1, w_out_b)
    dw_in_b, dw_in_b16 = _w_in_grad(hn1, dproj_b, IN_B, "w_in_b_grad")
    gs_b = [dw_in_b.reshape(4, 1, D, SH_B), dw_out_b.reshape(4, 1, BR_B // 4, D)]
    gb_b = [dw_in_b16.reshape(4, 1, D, SH_B), dw_out_b16.reshape(4, 1, BR_B // 4, D)]
    pb_send, pb_recv, gb_b, pb_land, pb_token = _pair_start(gb_b, LAYER_B, "pair_b_start", 6)
    dh1, dg1 = _in_proj_bwd(dproj_b, w_in_b, h1, g1, dh2, pb_token, IN_B, "in_proj_b_bwd")
    _, r1_b = _pair_wait(pb_send, pb_recv, gb_b, pb_land, [dh1], LAYER_B, "pair_b_wait")
    ps_b = _pair_sums(place, gs_b, r1_b, LAYER_B, "pair_sums_b")
    cb_send, cb_recv, cb_src, cb_land, cb_token = _chip_start(ps_b, LAYER_B, "chip_b_start", 7)

    outs = _attn_out_bwd(dh1, os_, ls, qm0, kv[0], z0, w_out_a, cb_token)
    dos, dds, dqm, dz, dw_out_a, dkv0, dw_out_a16 = outs[0:3], outs[3:6], outs[6], outs[7], outs[8], outs[9], outs[10]
    bwd = [_attn_bwd(q, k, v, dos[g], lss[g], dds[g], g) for g in range(3)]
    dproj_a = _qkv_bwd([b[0] for b in bwd], [b[1] for b in bwd], [b[2] for b in bwd], dqm, dz, rc, rs1, rs2)
    dw_in_a, dw_in_a16 = _w_in_grad(hn0, dproj_a, IN_A, "w_in_a_grad")
    dwkv, dwkv16, dmg = _mem_bwd(mem[0], mem_norm_g, memn, wkv_f, dkv0, dkv1)

    gs_a = [dwkv, dw_in_a.reshape(4, 1, D, SH_A), dw_out_a.reshape(4, 1, BR_A, SH_O)]
    r1_a = _pair_exchange([dwkv16, dw_in_a16.reshape(4, 1, D, SH_A), dw_out_a16.reshape(4, 1, BR_A, SH_O)], LAYER_A,
                          "pair_exchange_a", 1)
    ps_a = _pair_sums(place, gs_a, r1_a, LAYER_A, "pair_sums_a")
    ca_send, ca_recv, ca_src, ca_land, ca_token = _chip_start(ps_a, LAYER_A, "chip_a_start", 8)

    gx, dg0 = _in_proj_bwd(dproj_a, w_in_a, xs, g0, dh1, ca_token, IN_A, "in_proj_a_bwd")
    pack = jnp.concatenate([_pad_rows(jnp.concatenate([dg0, dg1], axis=0), 8), _pad_rows(dmg, 8), _pad_rows(dfg, 8),
                            dcw, _pad_rows(jnp.pad(loss_part, ((0, 0), (0, D - 128))), 8)], axis=0)
    sm_send, sm_recv, pack, sm_land, sm_token = _small_start(pack, ca_token)
    r2_b = _chip_wait(cb_send, cb_recv, cb_src, cb_land, [ca_token], LAYER_B, "chip_b_wait")
    hs_b = _chip_sums(place, gs_b, r1_b, r2_b, LAYER_B, "chip_sums_b")
    g_b = _pair_gather(hs_b, LAYER_B, "pair_gather_b", 2)
    upd_b = _adamw_layer([w_big[w] for w in LAYER_B], g_b, [m_big[w] for w in LAYER_B], [v_big[w] for w in LAYER_B],
                         LAYER_B, "adamw_b")
    r2_a = _chip_wait(ca_send, ca_recv, ca_src, ca_land, [gx, upd_b[0][0], upd_b[1][0], sm_token], LAYER_A,
                      "chip_a_wait")
    hs_a = _chip_sums(place, gs_a, r1_a, r2_a, LAYER_A, "chip_sums_a")
    g_a = _pair_gather(hs_a, LAYER_A, "pair_gather_a", 3)
    upd_a = _adamw_layer([w_big[w] for w in LAYER_A], g_a, [m_big[w] for w in LAYER_A], [v_big[w] for w in LAYER_A],
                         LAYER_A, "adamw_a")
    upd = upd_a + upd_b
    g_big = [u[3] for u in upd]
    pack, sm_land = _small_wait(sm_send, sm_recv, pack, sm_land, [r2_a[0]])
    sw = [norm_g, mem_norm_g, final_g.reshape(1, D), conv_w[0]]
    sm = [m_norm_g, m_mem_norm_g, m_final_g.reshape(1, D), m_conv_w[0]]
    sv = [v_norm_g, v_mem_norm_g, v_final_g.reshape(1, D), v_conv_w[0]]
    loss_row, sg, sd, snm, snv = _small_update(place, pack, sm_land, sw, sm, sv)
    loss = loss_row[0, 0]
    g_norm, g_memnorm, g_final, g_conv = sg

    def order(norm, memnorm, wkv, w_in_a, w_out_a, w_in_b, conv, w_out_b, final):
        return (norm, memnorm, wkv, w_in_a, w_out_a, w_in_b, conv.reshape(1, 3, SH_O), w_out_b, final.reshape(D))

    grads = order(g_norm, g_memnorm, g_big[0], g_big[1], g_big[2], g_big[3], g_conv, g_big[4], g_final)
    deltas = order(sd[0], sd[1], upd[0][0], upd[1][0], upd[2][0], upd[3][0], sd[3], upd[4][0], sd[2])
    new_m = order(snm[0], snm[1], upd[0][1], upd[1][1], upd[2][1], upd[3][1], snm[3], upd[4][1], snm[2])
    new_v = order(snv[0], snv[1], upd[0][2], upd[1][2], upd[2][2], upd[3][2], snv[3], upd[4][2], snv[2])
    return (loss, gx[None], *grads, *deltas, *new_m, *new_v)
```

```python
import functools

import numpy as np
import jax
import jax.numpy as jnp
from jax import lax
from jax.experimental import pallas as pl
from jax.experimental.pallas import tpu as pltpu

F32 = jnp.float32
BF16 = jnp.bfloat16

S = 2048
D = 1024
TM = 256
NT = S // TM
MX = 512
NX = S // MX
HD = 64
GW = 512
NQ = 3 * GW
MW = 256
NM = 256
IN_A = 3 * NQ + MW + GW + MW
IN_B = 3 * D + MW + D + MW
BR_A = GW + MW
BR_B = D + MW
SH_A = IN_A // 4
SH_B = IN_B // 4
SH_O = D // 4
QBLK = 128
DILATIONS = (1, 4, 16)
EPS = 1e-6
SCALE = HD ** -0.5
NEG = -1e30
ROPE_THETA = 500000.0

ADAM_LR = 0.001
ADAM_B1 = 0.9
ADAM_B2 = 0.999
ADAM_EPS = 1e-08
ADAM_WD = 0.01
ADAM_STEP = 10

VMEM_LIMIT_BYTES = 60 * 1024 * 1024


def _params(sem=None):
    if sem is None:
        return pltpu.CompilerParams(vmem_limit_bytes=VMEM_LIMIT_BYTES)
    return pltpu.CompilerParams(dimension_semantics=sem, vmem_limit_bytes=VMEM_LIMIT_BYTES)


def _full(shape):
    nd = len(shape)
    return pl.BlockSpec(shape, lambda *_: (0,) * nd)


def _rows(width, tm=TM):
    return pl.BlockSpec((tm, width), lambda i: (i, 0))


def _sds(shape, dtype):
    return jax.ShapeDtypeStruct(shape, dtype)


def _silu_parts(z):
    sig = 0.5 * jnp.tanh(0.5 * z) + 0.5
    return z * sig, sig * (1.0 + z * (1.0 - sig))


def _dot(a, b):
    return jnp.dot(a, b, preferred_element_type=F32)


def _dot_nt(a, b):
    return lax.dot_general(a, b, (((1,), (1,)), ((), ())), preferred_element_type=F32)


def _dot_tn(a, b):
    return lax.dot_general(a, b, (((0,), (0,)), ((), ())), preferred_element_type=F32)


def _rope_fwd(t, c, s1, s2):
    return t * c + pltpu.roll(t, 120, 1) * s1 + pltpu.roll(t, 8, 1) * s2


def _rope_bwd(g, c, s1, s2):
    return g * c + pltpu.roll(g * s1, 8, 1) + pltpu.roll(g * s2, 120, 1)


MEM_HEADS = MW // HD


def _stack_heads(x):
    head = lax.broadcasted_iota(jnp.int32, x.shape, 1) // HD
    return jnp.concatenate([jnp.where(head == h, x, 0.0) for h in range(MEM_HEADS)], axis=0).astype(BF16)


def _unstack_heads(x4):
    tm = x4.shape[0] // MEM_HEADS
    head = lax.broadcasted_iota(jnp.int32, (tm, MW), 1) // HD
    out = x4[:tm]
    for h in range(1, MEM_HEADS):
        out = jnp.where(head == h, x4[h * tm:(h + 1) * tm], out)
    return out


def _mem_attn(qm, kv):
    q4 = _stack_heads(qm.astype(F32))
    s = _dot_nt(q4, kv[:, :MW]) * SCALE
    e = jnp.exp(s - jnp.max(s, axis=-1, keepdims=True))
    p = e * (1.0 / jnp.sum(e, axis=-1, keepdims=True))
    return p, _unstack_heads(_dot(p.astype(BF16), kv[:, MW:])), q4


def _mem_attn_bwd(dmo, p, mo, q4, kv, dkv_ref):
    tm = dmo.shape[0]
    head = lax.broadcasted_iota(jnp.int32, dmo.shape, 1) // HD
    prod = dmo * mo
    delta = jnp.concatenate([jnp.sum(jnp.where(head == h, prod, 0.0), axis=-1, keepdims=True)
                             for h in range(MEM_HEADS)], axis=0)
    d4 = _stack_heads(dmo)
    ds = (p * (_dot_nt(d4, kv[:, MW:]) - delta) * SCALE).astype(BF16)
    dkv_ref[:, :MW] += _dot_tn(ds, q4)
    dkv_ref[:, MW:] += _dot_tn(p.astype(BF16), d4)
    return _unstack_heads(_dot(ds, kv[:, :MW]))


def _merge(o_refs, l_refs):
    ls = [r[...] for r in l_refs]
    m = jnp.maximum(jnp.maximum(ls[0], ls[1]), ls[2])
    es = [jnp.exp(l - m) for l in ls]
    inv = 1.0 / (es[0] + es[1] + es[2])
    ws = [e * inv for e in es]
    os_ = [r[...] for r in o_refs]
    mix = ws[0] * os_[0] + ws[1] * os_[1] + ws[2] * os_[2]
    return ws, mix


def _conv_taps(cg, u, cgp, up, first):
    a = cg * u
    ap = jnp.where(first, 0.0, cgp * up)
    row = lax.broadcasted_iota(jnp.int32, a.shape, 0)
    a1 = jnp.where(row == 0, ap[7:8, :], pltpu.roll(a, 1, 0))
    a2 = jnp.where(row == 0, ap[6:7, :], jnp.where(row == 1, ap[7:8, :], pltpu.roll(a, 2, 0)))
    return a, a1, a2


def _rope_tables(posf, after):
    half = 8
    invf = np.float32(ROPE_THETA) ** (-np.arange(half, dtype=np.float32) * np.float32(2.0 / 16))
    lane = np.arange(128)
    table = np.where((lane % HD) < 16, invf[lane % half], 0.0).astype(np.float32)[None, :]

    def body(pos_ref, invf_ref, c_ref, s1_ref, s2_ref):
        ang = pos_ref[...] * invf_ref[...]
        jm = lax.broadcasted_iota(jnp.int32, ang.shape, 1) & (HD - 1)
        cs = jnp.cos(ang)
        sn = jnp.sin(ang)
        c_ref[...] = jnp.where(jm < 16, cs, 1.0)
        s1_ref[...] = jnp.where(jm < 8, -sn, 0.0)
        s2_ref[...] = jnp.where((jm >= 8) & (jm < 16), sn, 0.0)

    out = _sds((S, 128), F32)
    return pl.pallas_call(
        functools.partial(_skip_arg, body, 2), name="rope_tables", grid=(NT,),
        in_specs=[_rows(1), _full((1, 128)), pl.BlockSpec(memory_space=pl.ANY)],
        out_specs=[_rows(128)] * 3, out_shape=[out] * 3,
        compiler_params=_params(("parallel",)),
    )(posf, jnp.asarray(table), after)


def _in_proj_a(x, g0, w_in, c, s1, s2, after):
    def body(x_ref, g_ref, w_ref, c_ref, s1_ref, s2_ref, hn_ref, q_ref, k_ref, v_ref, qm_ref, z_ref, proj):
        xf = x_ref[...]
        hn = xf * lax.rsqrt(jnp.mean(xf * xf, axis=-1, keepdims=True) + EPS) * g_ref[...]
        hb = hn.astype(BF16)
        hn_ref[...] = hb
        for s in range(4):
            proj[:, s * SH_A:(s + 1) * SH_A] = _dot(hb, w_ref[s])
        cc, a1, a2 = c_ref[...], s1_ref[...], s2_ref[...]
        for j in range(NQ // 128):
            q_ref[:, j * 128:(j + 1) * 128] = (
                _rope_fwd(proj[:, j * 128:(j + 1) * 128], cc, a1, a2) * SCALE).astype(BF16)
            k_ref[:, j * 128:(j + 1) * 128] = _rope_fwd(
                proj[:, NQ + j * 128:NQ + (j + 1) * 128], cc, a1, a2).astype(BF16)
        v_ref[...] = proj[:, 2 * NQ:3 * NQ].astype(BF16)
        qm_ref[...] = proj[:, 3 * NQ:3 * NQ + MW].astype(BF16)
        z_ref[...] = proj[:, 3 * NQ + MW:]

    return pl.pallas_call(
        functools.partial(_skip_arg, body, 6), name="in_proj_a", grid=(NT,),
        in_specs=[_rows(D), _full((1, D)), _full((4, D, SH_A)), _rows(128), _rows(128), _rows(128),
                  pl.BlockSpec(memory_space=pl.ANY)],
        out_specs=[_rows(D), _rows(NQ), _rows(NQ), _rows(NQ), _rows(MW), _rows(BR_A)],
        out_shape=[_sds((S, D), BF16), _sds((S, NQ), BF16), _sds((S, NQ), BF16), _sds((S, NQ), BF16),
                   _sds((S, MW), BF16), _sds((S, BR_A), F32)],
        scratch_shapes=[pltpu.VMEM((TM, IN_A), F32)],
        compiler_params=_params(("parallel",)),
    )(x, g0, w_in, c, s1, s2, after)


def _mem_fwd(mem, mg, wkv):
    def body(mem_ref, mg_ref, w_ref, memn_ref, kv_ref):
        mf = mem_ref[...]
        n = mf * lax.rsqrt(jnp.mean(mf * mf, axis=-1, keepdims=True) + EPS)
        for i in range(2):
            mn = (n * mg_ref[i:i + 1, :]).astype(BF16)
            memn_ref[i] = mn
            acc = _dot(mn[:, 0:NM], w_ref[0, i])
            for s in range(1, 4):
                acc += _dot(mn[:, s * NM:(s + 1) * NM], w_ref[s, i])
            kv_ref[i] = acc.astype(BF16)

    return pl.pallas_call(
        body, name="mem_fwd", grid=(1,),
        in_specs=[_full((NM, D)), _full((2, D)), _full((4, 2, NM, 2 * MW))],
        out_specs=[_full((2, NM, D)), _full((2, NM, 2 * MW))],
        out_shape=[_sds((2, NM, D), BF16), _sds((2, NM, 2 * MW), BF16)],
        compiler_params=_params(("arbitrary",)),
    )(mem, mg, wkv)


def _band_mask(j):
    qi = lax.broadcasted_iota(jnp.int32, (QBLK, 2 * QBLK), 0)
    kj = lax.broadcasted_iota(jnp.int32, (QBLK, 2 * QBLK), 1)
    dist = qi + QBLK - kj
    return (dist >= 0) & (dist <= QBLK) & ((kj >= QBLK) | (j > 0))


LANES = 128
NCHUNK = GW // LANES
FWD_UNROLL = 16
BWD_UNROLL = 16
CONV_CHUNK = 256


def _perm_matrix(d):
    n = TM // d
    p = np.zeros((TM, TM), np.float32)
    for r in range(d):
        for i in range(n):
            p[r * n + i, i * d + r] = 1.0
    return p


def _split_dot(p, x):
    hi = x.astype(BF16)
    lo = (x - hi.astype(F32)).astype(BF16)
    both = _dot(p, jnp.concatenate([hi, lo], axis=1))
    return both[:, :LANES] + both[:, LANES:]


def _pair_dot(p, a, b):
    both = _dot(p, jnp.concatenate([a, b], axis=1))
    return both[:, :LANES], both[:, LANES:]


def _tile_to_streams(y, dst, t, d):
    n, ln = TM // d, S // d
    for r in range(d):
        dst[r * ln + t * n:r * ln + (t + 1) * n, :] = y[r * n:(r + 1) * n].astype(dst.dtype)


def _tile_from_streams(src, t, d):
    n, ln = TM // d, S // d
    return jnp.concatenate([src[r * ln + t * n:r * ln + (t + 1) * n, :] for r in range(d)], axis=0)


def _head_masks():
    first = lax.broadcasted_iota(jnp.int32, (TM, LANES), 1) < HD
    return first, jnp.logical_not(first)


def _attn_fwd(q, k, v, g, after):
    d = DILATIONS[g]
    nb = S // d // QBLK
    perm = _perm_matrix(d)

    def body(q_ref, k_ref, v_ref, p_ref, pt_ref, o_ref, l_ref, ls_ref, q0, q1, ks, vs, os_):
        first, second = _head_masks()
        pm = p_ref[...]
        for t in range(NT):
            rows = slice(t * TM, (t + 1) * TM)
            if d == 1:
                qt = q_ref[rows, :].astype(F32)
            else:
                qt, kt = _pair_dot(pm, q_ref[rows, :], k_ref[rows, :])
                _tile_to_streams(kt, ks, t, d)
                if t % 2 == 0:
                    va, vb = _pair_dot(pm, v_ref[rows, :], v_ref[(t + 1) * TM:(t + 2) * TM, :])
                    _tile_to_streams(va, vs, t, d)
                    _tile_to_streams(vb, vs, t + 1, d)
            _tile_to_streams(jnp.where(first, qt, 0.0), q0, t, d)
            _tile_to_streams(jnp.where(second, qt, 0.0), q1, t, d)
        kref, vref = (k_ref, v_ref) if d == 1 else (ks, vs)
        oref, lref = (o_ref, l_ref) if d == 1 else (os_, ls_ref)

        def blk(b, carry):
            r0 = pl.multiple_of(b * QBLK, QBLK)
            p0 = pl.multiple_of(jnp.maximum(b - 1, 0) * QBLK, QBLK)
            kk = jnp.concatenate([kref[pl.ds(p0, QBLK), :], kref[pl.ds(r0, QBLK), :]], axis=0)
            vv = jnp.concatenate([vref[pl.ds(p0, QBLK), :], vref[pl.ds(r0, QBLK), :]], axis=0)
            valid = _band_mask(b & (nb - 1))
            acc, lse = [], []
            for qh in (q0, q1):
                s = jnp.where(valid, _dot_nt(qh[pl.ds(r0, QBLK), :], kk), NEG)
                m = jnp.max(s, axis=-1, keepdims=True)
                e = jnp.exp(s - m)
                l = jnp.sum(e, axis=-1, keepdims=True)
                acc.append(_dot(e.astype(BF16), vv) * (1.0 / l))
                lse.append(m + jnp.log(l))
            f = first[:QBLK]
            oref[pl.ds(r0, QBLK), :] = jnp.where(f, acc[0], acc[1])
            lref[pl.ds(r0, QBLK), :] = jnp.where(f, lse[0], lse[1])
            return carry

        lax.fori_loop(0, S // QBLK, blk, 0, unroll=FWD_UNROLL)
        if d > 1:
            ptm = pt_ref[...]
            for t in range(NT):
                rows = slice(t * TM, (t + 1) * TM)
                o_ref[rows, :] = _split_dot(ptm, _tile_from_streams(os_, t, d))
                l_ref[rows, :] = _split_dot(ptm, _tile_from_streams(ls_ref, t, d))

    qkv_spec = pl.BlockSpec((S, LANES), lambda c: (0, g * NCHUNK + c))
    out_spec = pl.BlockSpec((S, LANES), lambda c: (0, c))
    n_out = 2 if d == 1 else 3
    inner = body if d > 1 else functools.partial(_drop_arg, body, 7)
    outs = pl.pallas_call(
        functools.partial(_skip_arg, inner, 5), name=f"attn_fwd_g{g}", grid=(NCHUNK,),
        in_specs=[qkv_spec] * 3 + [_full((TM, TM))] * 2 + [pl.BlockSpec(memory_space=pl.ANY)],
        out_specs=[out_spec] * n_out, out_shape=[_sds((S, GW), F32)] * n_out,
        scratch_shapes=[pltpu.VMEM((S, LANES), BF16)] * 4 + [pltpu.VMEM((S, LANES), F32)],
        compiler_params=_params(("parallel",)),
    )(q, k, v, jnp.asarray(perm, BF16), jnp.asarray(perm.T, BF16), after)
    return (outs[0], outs[1], outs[1]) if d == 1 else tuple(outs)


def _drop_arg(body, pos, *refs):
    return body(*refs[:pos], None, *refs[pos:])


def _attn_out(os_, ls, qm, kv0, z, x, w_out):
    def body(o0, o1, o2, l0, l1, l2, qm_ref, kv_ref, z_ref, x_ref, w_ref, h_ref, ybuf):
        _, mix = _merge((o0, o1, o2), (l0, l1, l2))
        sz, _ = _silu_parts(z_ref[...])
        ybuf[:, :GW] = (mix * sz[:, :GW]).astype(BF16)
        _, mo, _ = _mem_attn(qm_ref[...], kv_ref[...])
        ybuf[:, GW:] = (mo * sz[:, GW:]).astype(BF16)
        yb = ybuf[...]
        for s in range(4):
            cs = slice(s * SH_O, (s + 1) * SH_O)
            h_ref[:, cs] = x_ref[:, cs] + _dot(yb, w_ref[s])

    return pl.pallas_call(
        body, name="attn_out", grid=(NX,),
        in_specs=[_rows(GW, MX)] * 6 + [_rows(MW, MX), _full((NM, 2 * MW)), _rows(BR_A, MX), _rows(D, MX),
                                        _full((4, BR_A, SH_O))],
        out_specs=_rows(D, MX), out_shape=_sds((S, D), F32),
        scratch_shapes=[pltpu.VMEM((MX, BR_A), BF16)],
        compiler_params=_params(("parallel",)),
    )(*os_, *ls, qm, kv0, z, x, w_out)


def _in_proj_b(h1, g1, w_in):
    def body(x_ref, g_ref, w_ref, hn_ref, bg_ref, cg_ref, u_ref, qm_ref, z_ref, proj):
        xf = x_ref[...]
        hn = xf * lax.rsqrt(jnp.mean(xf * xf, axis=-1, keepdims=True) + EPS) * g_ref[...]
        hb = hn.astype(BF16)
        hn_ref[...] = hb
        for s in range(4):
            proj[:, s * SH_B:(s + 1) * SH_B] = _dot(hb, w_ref[s])
        bg_ref[...] = proj[:, :D]
        cg_ref[...] = proj[:, D:2 * D]
        u_ref[...] = proj[:, 2 * D:3 * D]
        qm_ref[...] = proj[:, 3 * D:3 * D + MW].astype(BF16)
        z_ref[...] = proj[:, 3 * D + MW:]

    return pl.pallas_call(
        body, name="in_proj_b", grid=(NT,),
        in_specs=[_rows(D), _full((1, D)), _full((4, D, SH_B))],
        out_specs=[_rows(D), _rows(D), _rows(D), _rows(D), _rows(MW), _rows(BR_B)],
        out_shape=[_sds((S, D), BF16), _sds((S, D), F32), _sds((S, D), F32), _sds((S, D), F32),
                   _sds((S, MW), BF16), _sds((S, BR_B), F32)],
        scratch_shapes=[pltpu.VMEM((TM, IN_B), F32)],
        compiler_params=_params(("parallel",)),
    )(h1, g1, w_in)


def _prev8(width):
    return pl.BlockSpec((8, width), lambda i: (jnp.maximum(i * (MX // 8) - 1, 0), 0))


def _conv_out_loss(bg, cg, u, cw, qm, kv1, z, h1, w_out, fg, tgt):
    def body(bg_ref, cg_ref, u_ref, cgp_ref, up_ref, cw_ref, qm_ref, kv_ref, z_ref, h_ref, w_ref, fg_ref, t_ref,
             dh_ref, loss_ref, dfg_ref, ybuf):
        i = pl.program_id(0)
        a, a1, a2 = _conv_taps(cg_ref[...], u_ref[...], cgp_ref[...], up_ref[...], i == 0)
        conv = cw_ref[0:1, :] * a2 + cw_ref[1:2, :] * a1 + cw_ref[2:3, :] * a
        sz, _ = _silu_parts(z_ref[...])
        ybuf[:, :D] = (bg_ref[...] * conv * sz[:, :D]).astype(BF16)
        _, mo, _ = _mem_attn(qm_ref[...], kv_ref[...])
        ybuf[:, D:] = (mo * sz[:, D:]).astype(BF16)
        h2 = h_ref[...] + _dot(ybuf[...], w_ref[...])
        rstd = lax.rsqrt(jnp.mean(h2 * h2, axis=-1, keepdims=True) + EPS)
        n = h2 * rstd
        fgv = fg_ref[...]
        err = n * fgv - t_ref[...]
        dout = err * (1.0 / D)
        dn = dout * fgv
        dh_ref[...] = rstd * (dn - n * jnp.mean(dn * n, axis=-1, keepdims=True))

        @pl.when(i == 0)
        def _():
            loss_ref[...] = jnp.zeros_like(loss_ref)
            dfg_ref[...] = jnp.zeros_like(dfg_ref)

        loss_ref[...] += jnp.sum(err * err) * (0.5 / D)
        dfg_ref[...] += jnp.sum(dout * n, axis=0, keepdims=True)

    return pl.pallas_call(
        body, name="conv_out_loss", grid=(NX,),
        in_specs=[_rows(D, MX), _rows(D, MX), _rows(D, MX), _prev8(D), _prev8(D), _full((8, D)), _rows(MW, MX),
                  _full((NM, 2 * MW)), _rows(BR_B, MX), _rows(D, MX), _full((BR_B, D)), _full((1, D)), _rows(D, MX)],
        out_specs=[_rows(D, MX), _full((1, 128)), _full((1, D))],
        out_shape=[_sds((S, D), F32), _sds((1, 128), F32), _sds((1, D), F32)],
        scratch_shapes=[pltpu.VMEM((MX, BR_B), BF16)],
        compiler_params=_params(("arbitrary",)),
    )(bg, cg, u, cg, u, cw, qm, kv1, z, h1, w_out, fg, tgt)


def _conv_bwd(dh2, bg, cg, u, cw, qm, kv1, z, w_out):
    rev = lambda i: (NX - 1 - i, 0)
    rows = lambda w: pl.BlockSpec((MX, w), rev)
    prev8 = pl.BlockSpec((8, D), lambda i: (jnp.maximum((NX - 1 - i) * (MX // 8) - 1, 0), 0))

    def body(dh_ref, bg_ref, cg_ref, u_ref, cgp_ref, up_ref, cw_ref, qm_ref, kv_ref, z_ref, w_ref,
             dproj_ref, dw_ref, dcw_ref, dkv_ref, dwb_ref, ybuf, carry):
        i = pl.program_id(0)

        @pl.when(i == 0)
        def _():
            dw_ref[...] = jnp.zeros_like(dw_ref)
            dcw_ref[...] = jnp.zeros_like(dcw_ref)
            dkv_ref[...] = jnp.zeros_like(dkv_ref)
            carry[...] = jnp.zeros_like(carry)

        dhb = dh_ref[...].astype(BF16)
        dy = _dot_nt(dhb, w_ref[...])
        kvv = kv_ref[...]
        p, mo, q4 = _mem_attn(qm_ref[...], kvv)
        szm, dszm = _silu_parts(z_ref[:, D:])
        ybuf[:, D:] = (mo * szm).astype(BF16)
        dym = dy[:, D:]
        dproj_ref[:, 3 * D + MW + D:] = (dym * mo * dszm).astype(BF16)
        first_tile = i == NX - 1
        for c in range(D // CONV_CHUNK):
            cs = slice(c * CONV_CHUNK, (c + 1) * CONV_CHUNK)
            bgv, cgv, uv = bg_ref[:, cs], cg_ref[:, cs], u_ref[:, cs]
            a, a1, a2 = _conv_taps(cgv, uv, cgp_ref[:, cs], up_ref[:, cs], first_tile)
            w0, w1, w2 = cw_ref[0:1, cs], cw_ref[1:2, cs], cw_ref[2:3, cs]
            conv = w0 * a2 + w1 * a1 + w2 * a
            mix = bgv * conv
            sz, dsz = _silu_parts(z_ref[:, cs])
            ybuf[:, cs] = (mix * sz).astype(BF16)
            dyc = dy[:, cs]
            dproj_ref[:, 3 * D + MW + c * CONV_CHUNK:3 * D + MW + (c + 1) * CONV_CHUNK] = (
                dyc * mix * dsz).astype(BF16)
            dmix = dyc * sz
            dproj_ref[:, cs] = (dmix * conv).astype(BF16)
            dc = dmix * bgv
            nxt = carry[:, cs]
            row = lax.broadcasted_iota(jnp.int32, dc.shape, 0)
            dc1 = jnp.where(row == MX - 1, nxt[0:1, :], pltpu.roll(dc, MX - 1, 0))
            dc2 = jnp.where(row == MX - 2, nxt[0:1, :],
                            jnp.where(row == MX - 1, nxt[1:2, :], pltpu.roll(dc, MX - 2, 0)))
            carry[:, cs] = dc[0:8, :]
            da = w2 * dc + w1 * dc1 + w0 * dc2
            dproj_ref[:, D + c * CONV_CHUNK:D + (c + 1) * CONV_CHUNK] = (da * uv).astype(BF16)
            dproj_ref[:, 2 * D + c * CONV_CHUNK:2 * D + (c + 1) * CONV_CHUNK] = (da * cgv).astype(BF16)
            dcw_ref[0:1, cs] += jnp.sum(dc * a2, axis=0, keepdims=True)
            dcw_ref[1:2, cs] += jnp.sum(dc * a1, axis=0, keepdims=True)
            dcw_ref[2:3, cs] += jnp.sum(dc * a, axis=0, keepdims=True)
        dw_ref[...] += _dot_tn(ybuf[...], dhb)
        dproj_ref[:, 3 * D:3 * D + MW] = _mem_attn_bwd(dym * szm, p, mo, q4, kvv, dkv_ref).astype(BF16)

        @pl.when(i == NX - 1)
        def _():
            dwb_ref[...] = dw_ref[...].astype(BF16)

    return pl.pallas_call(
        body, name="conv_bwd", grid=(NX,),
        in_specs=[rows(D), rows(D), rows(D), rows(D), prev8, prev8, _full((8, D)), rows(MW),
                  _full((NM, 2 * MW)), rows(BR_B), _full((BR_B, D))],
        out_specs=[rows(IN_B), _full((BR_B, D)), _full((8, D)), _full((NM, 2 * MW)), _full((BR_B, D))],
        out_shape=[_sds((S, IN_B), BF16), _sds((BR_B, D), F32), _sds((8, D), F32), _sds((NM, 2 * MW), F32),
                   _sds((BR_B, D), BF16)],
        scratch_shapes=[pltpu.VMEM((MX, BR_B), BF16), pltpu.VMEM((8, D), F32)],
        compiler_params=_params(("arbitrary",)),
    )(dh2, bg, cg, u, cg, u, cw, qm, kv1, z, w_out)


def _in_proj_bwd(dproj, w_in, xin, g, dres, after, width, name):
    sh = width // 4

    def body(dp_ref, w_ref, x_ref, g_ref, dr_ref, dx_ref, dg_ref):
        i = pl.program_id(0)
        dhn = _dot_nt(dp_ref[:, 0:sh], w_ref[0])
        for s in range(1, 4):
            dhn += _dot_nt(dp_ref[:, s * sh:(s + 1) * sh], w_ref[s])
        xf = x_ref[...]
        rstd = lax.rsqrt(jnp.mean(xf * xf, axis=-1, keepdims=True) + EPS)
        n = xf * rstd
        dn = dhn * g_ref[...]
        dx_ref[...] = dr_ref[...] + rstd * (dn - n * jnp.mean(dn * n, axis=-1, keepdims=True))

        @pl.when(i == 0)
        def _():
            dg_ref[...] = jnp.zeros_like(dg_ref)

        dg_ref[...] += jnp.sum(dhn * n, axis=0, keepdims=True)

    return pl.pallas_call(
        functools.partial(_skip_arg, body, 5), name=name, grid=(NT,),
        in_specs=[_rows(width), _full((4, D, sh)), _rows(D), _full((1, D)), _rows(D), pl.BlockSpec(memory_space=pl.ANY)],
        out_specs=[_rows(D), _full((1, D))],
        out_shape=[_sds((S, D), F32), _sds((1, D), F32)],
        compiler_params=_params(("arbitrary",)),
    )(dproj, w_in, xin, g, dres, after)


def _w_in_grad(hn, dproj, width, name):
    sh = width // 4

    def body(hn_ref, dp_ref, dw_ref, dwb_ref):
        dw = _dot_tn(hn_ref[...], dp_ref[...])
        dw_ref[0] = dw
        dwb_ref[0] = dw.astype(BF16)

    spec = pl.BlockSpec((1, D, sh), lambda s: (s, 0, 0))
    return pl.pallas_call(
        body, name=name, grid=(4,),
        in_specs=[_full((S, D)), pl.BlockSpec((S, sh), lambda s: (0, s))],
        out_specs=[spec, spec], out_shape=[_sds((4, D, sh), F32), _sds((4, D, sh), BF16)],
        compiler_params=_params(("parallel",)),
    )(hn, dproj)


def _attn_out_bwd(dh1, os_, ls, qm, kv0, z, w_out, after):
    ones_bd = np.kron(np.eye(GW // HD, dtype=np.float32), np.ones((HD, HD), np.float32))

    def body(dh_ref, o0, o1, o2, l0, l1, l2, qm_ref, kv_ref, z_ref, w_ref, bd_ref,
             do0, do1, do2, dd0, dd1, dd2, dqm_ref, dz_ref, dw_ref, dkv_ref, dwb_ref, ybuf):
        i = pl.program_id(0)

        @pl.when(i == 0)
        def _():
            dw_ref[...] = jnp.zeros_like(dw_ref)
            dkv_ref[...] = jnp.zeros_like(dkv_ref)

        ws, mix = _merge((o0, o1, o2), (l0, l1, l2))
        sz, dsz = _silu_parts(z_ref[...])
        kvv = kv_ref[...]
        p, mo, q4 = _mem_attn(qm_ref[...], kvv)
        ybuf[:, :GW] = (mix * sz[:, :GW]).astype(BF16)
        ybuf[:, GW:] = (mo * sz[:, GW:]).astype(BF16)
        yb = ybuf[...]
        dh = dh_ref[...]
        dy = None
        for s in range(4):
            dhb = dh[:, s * SH_O:(s + 1) * SH_O].astype(BF16)
            dw_ref[s] += _dot_tn(yb, dhb)
            part = _dot_nt(dhb, w_ref[s])
            dy = part if dy is None else dy + part
        dcat = dy * sz
        dz_ref[:, :GW] = (dy[:, :GW] * mix * dsz[:, :GW]).astype(BF16)
        dz_ref[:, GW:] = (dy[:, GW:] * mo * dsz[:, GW:]).astype(BF16)
        dmix = dcat[:, :GW]
        prod = dmix * mix
        hi = prod.astype(BF16)
        lo = (prod - hi.astype(F32)).astype(BF16)
        bd = bd_ref[...]
        tot = _dot(hi, bd) + _dot(lo, bd)
        for w, do_ref, dd_ref in zip(ws, (do0, do1, do2), (dd0, dd1, dd2)):
            do_ref[...] = (w * dmix).astype(BF16)
            dd_ref[...] = w * tot

        dqm_ref[...] = _mem_attn_bwd(dcat[:, GW:], p, mo, q4, kvv, dkv_ref).astype(BF16)

        @pl.when(i == NX - 1)
        def _():
            dwb_ref[...] = dw_ref[...].astype(BF16)

    return pl.pallas_call(
        functools.partial(_skip_arg, body, 12), name="attn_out_bwd", grid=(NX,),
        in_specs=[_rows(D, MX)] + [_rows(GW, MX)] * 6 + [_rows(MW, MX), _full((NM, 2 * MW)), _rows(BR_A, MX),
                                                           _full((4, BR_A, SH_O)), _full((GW, GW)),
                                                           pl.BlockSpec(memory_space=pl.ANY)],
        out_specs=[_rows(GW, MX)] * 6 + [_rows(MW, MX), _rows(BR_A, MX), _full((4, BR_A, SH_O)),
                                         _full((NM, 2 * MW)), _full((4, BR_A, SH_O))],
        out_shape=[_sds((S, GW), BF16)] * 3 + [_sds((S, GW), F32)] * 3 + [
            _sds((S, MW), BF16), _sds((S, BR_A), BF16), _sds((4, BR_A, SH_O), F32), _sds((NM, 2 * MW), F32),
            _sds((4, BR_A, SH_O), BF16)],
        scratch_shapes=[pltpu.VMEM((MX, BR_A), BF16)],
        compiler_params=_params(("arbitrary",)),
    )(dh1, *os_, *ls, qm, kv0, z, w_out, jnp.asarray(ones_bd, dtype=BF16), after)


def _attn_bwd(q, k, v, do, lse_s, dd, g):
    d = DILATIONS[g]
    nb = S // d // QBLK
    perm = _perm_matrix(d)

    def body(q_ref, k_ref, v_ref, do_ref, l_ref, dd_ref, p_ref, pt_ref, dq_ref, dk_ref, dv_ref,
             q0, q1, g0, g1, ks, vs, dds, dqs, dks, dvs):
        first, second = _head_masks()
        pm = p_ref[...]
        for t in range(NT):
            rows = slice(t * TM, (t + 1) * TM)
            if d == 1:
                qt = q_ref[rows, :].astype(F32)
                gt = do_ref[rows, :].astype(F32)
            else:
                qt, gt = _pair_dot(pm, q_ref[rows, :], do_ref[rows, :])
                kt, vt = _pair_dot(pm, k_ref[rows, :], v_ref[rows, :])
                _tile_to_streams(kt, ks, t, d)
                _tile_to_streams(vt, vs, t, d)
                _tile_to_streams(_split_dot(pm, dd_ref[rows, :]), dds, t, d)
            _tile_to_streams(jnp.where(first, qt, 0.0), q0, t, d)
            _tile_to_streams(jnp.where(second, qt, 0.0), q1, t, d)
            _tile_to_streams(jnp.where(first, gt, 0.0), g0, t, d)
            _tile_to_streams(jnp.where(second, gt, 0.0), g1, t, d)
        kref, vref, ddref = (k_ref, v_ref, dd_ref) if d == 1 else (ks, vs, dds)
        dqref, dkref, dvref = dqs, dks, dvs
        dkref[...] = jnp.zeros_like(dkref)
        dvref[...] = jnp.zeros_like(dvref)

        def blk(b, carry):
            r0 = pl.multiple_of(b * QBLK, QBLK)
            p0 = pl.multiple_of(jnp.maximum(b - 1, 0) * QBLK, QBLK)
            kk = jnp.concatenate([kref[pl.ds(p0, QBLK), :], kref[pl.ds(r0, QBLK), :]], axis=0)
            vv = jnp.concatenate([vref[pl.ds(p0, QBLK), :], vref[pl.ds(r0, QBLK), :]], axis=0)
            lb = l_ref[pl.ds(r0, QBLK), :]
            ddb = ddref[pl.ds(r0, QBLK), :]
            lcol = jnp.concatenate([lb[:, 0:1], lb[:, HD:HD + 1]], axis=0)
            dcol = jnp.concatenate([ddb[:, 0:1], ddb[:, HD:HD + 1]], axis=0)
            valid = _band_mask(b & (nb - 1))
            valid2 = jnp.concatenate([valid, valid], axis=0)
            qq = jnp.concatenate([q0[pl.ds(r0, QBLK), :], q1[pl.ds(r0, QBLK), :]], axis=0)
            gg = jnp.concatenate([g0[pl.ds(r0, QBLK), :], g1[pl.ds(r0, QBLK), :]], axis=0)
            p = jnp.where(valid2, jnp.exp(_dot_nt(qq, kk) - lcol), 0.0)
            ds = (p * (_dot_nt(gg, vv) - dcol)).astype(BF16)
            dq2 = _dot(ds, kk)
            dqref[pl.ds(r0, QBLK), :] = jnp.where(first[:QBLK], dq2[:QBLK], dq2[QBLK:])
            dkk = _dot_tn(ds, qq)
            dvv = _dot_tn(p.astype(BF16), gg)
            dkref[pl.ds(p0, QBLK), :] += dkk[:QBLK]
            dkref[pl.ds(r0, QBLK), :] += dkk[QBLK:]
            dvref[pl.ds(p0, QBLK), :] += dvv[:QBLK]
            dvref[pl.ds(r0, QBLK), :] += dvv[QBLK:]
            return carry

        lax.fori_loop(0, S // QBLK, blk, 0, unroll=BWD_UNROLL)

        ptm = pt_ref[...] if d > 1 else None
        for t in range(NT):
            rows = slice(t * TM, (t + 1) * TM)
            if d == 1:
                dq_ref[rows, :] = dqs[rows, :].astype(BF16)
                dk_ref[rows, :] = dks[rows, :].astype(BF16)
                dv_ref[rows, :] = dvs[rows, :].astype(BF16)
            else:
                tq, tk = _pair_dot(ptm, _tile_from_streams(dqs, t, d).astype(BF16),
                                   _tile_from_streams(dks, t, d).astype(BF16))
                dq_ref[rows, :] = tq.astype(BF16)
                dk_ref[rows, :] = tk.astype(BF16)
                if t % 2 == 0:
                    ta, tb = _pair_dot(ptm, _tile_from_streams(dvs, t, d).astype(BF16),
                                       _tile_from_streams(dvs, t + 1, d).astype(BF16))
                    dv_ref[rows, :] = ta.astype(BF16)
                    dv_ref[(t + 1) * TM:(t + 2) * TM, :] = tb.astype(BF16)

    qkv_spec = pl.BlockSpec((S, LANES), lambda c: (0, g * NCHUNK + c))
    one_spec = pl.BlockSpec((S, LANES), lambda c: (0, c))
    return pl.pallas_call(
        body, name=f"attn_bwd_g{g}", grid=(NCHUNK,),
        in_specs=[qkv_spec] * 3 + [one_spec] * 3 + [_full((TM, TM))] * 2, out_specs=[one_spec] * 3,
        out_shape=[_sds((S, GW), BF16)] * 3,
        scratch_shapes=[pltpu.VMEM((S, LANES), BF16)] * 6 + [pltpu.VMEM((S, LANES), F32)] * 4,
        compiler_params=_params(("parallel",)),
    )(q, k, v, do, lse_s, dd, jnp.asarray(perm, BF16), jnp.asarray(perm.T, BF16))


def _qkv_bwd(dqs, dks, dvs, dqm, dz, c, s1, s2):
    def body(q0, q1, q2, k0, k1, k2, v0, v1, v2, dqm_ref, dz_ref, c_ref, s1_ref, s2_ref, dp_ref):
        cc, a1, a2 = c_ref[...], s1_ref[...], s2_ref[...]
        for g, (qr, kr, vr) in enumerate(((q0, k0, v0), (q1, k1, v1), (q2, k2, v2))):
            for j in range(GW // 128):
                ls_ = slice(j * 128, (j + 1) * 128)
                c0 = g * GW + j * 128
                dp_ref[:, c0:c0 + 128] = (_rope_bwd(qr[:, ls_].astype(F32), cc, a1, a2) * SCALE).astype(BF16)
                dp_ref[:, NQ + c0:NQ + c0 + 128] = _rope_bwd(kr[:, ls_].astype(F32), cc, a1, a2).astype(BF16)
            dp_ref[:, 2 * NQ + g * GW:2 * NQ + (g + 1) * GW] = vr[...]
        dp_ref[:, 3 * NQ:3 * NQ + MW] = dqm_ref[...]
        dp_ref[:, 3 * NQ + MW:] = dz_ref[...]

    return pl.pallas_call(
        body, name="qkv_bwd", grid=(NT,),
        in_specs=[_rows(GW)] * 9 + [_rows(MW), _rows(BR_A), _rows(128), _rows(128), _rows(128)],
        out_specs=_rows(IN_A), out_shape=_sds((S, IN_A), BF16),
        compiler_params=_params(("parallel",)),
    )(*dqs, *dks, *dvs, dqm, dz, c, s1, s2)


def _mem_bwd(mem, mg, memn, wkv, dkv0, dkv1):
    def body(mem_ref, mg_ref, memn_ref, w_ref, d0_ref, d1_ref, dw_ref, dwb_ref, dg_ref):
        mf = mem_ref[...]
        n = mf * lax.rsqrt(jnp.mean(mf * mf, axis=-1, keepdims=True) + EPS)
        for i, d_ref in enumerate((d0_ref, d1_ref)):
            dkv = d_ref[...].astype(BF16)
            mn = memn_ref[i]
            for s in range(4):
                cs = slice(s * NM, (s + 1) * NM)
                dw = _dot_tn(mn[:, cs], dkv)
                dw_ref[s, i] = dw
                dwb_ref[s, i] = dw.astype(BF16)
                dmn = _dot_nt(dkv, w_ref[s, i])
                dg_ref[i:i + 1, cs] = jnp.sum(dmn * n[:, cs], axis=0, keepdims=True)

    return pl.pallas_call(
        body, name="mem_bwd", grid=(1,),
        in_specs=[_full((NM, D)), _full((2, D)), _full((2, NM, D)), _full((4, 2, NM, 2 * MW)),
                  _full((NM, 2 * MW)), _full((NM, 2 * MW))],
        out_specs=[_full((4, 2, NM, 2 * MW)), _full((4, 2, NM, 2 * MW)), _full((2, D))],
        out_shape=[_sds((4, 2, NM, 2 * MW), F32), _sds((4, 2, NM, 2 * MW), BF16), _sds((2, D), F32)],
        compiler_params=_params(("arbitrary",)),
    )(mem, mg, memn, wkv, dkv0, dkv1)


MESH = pl.DeviceIdType.MESH
ANY = pl.BlockSpec(memory_space=pl.ANY)
BIG = (("wkv", 2, NM, 2 * MW), ("w_in_a", 1, D, SH_A), ("w_out_a", 1, BR_A, SH_O),
       ("w_in_b", 1, D, SH_B), ("w_out_b", 1, BR_B // 4, D))
NBIG = len(BIG)
CW_ROWS = 8


def _place():
    x, y, c = lax.axis_index("x"), lax.axis_index("y"), lax.axis_index("c")
    chips = ((1 - x, y), (x, 1 - y), (1 - x, 1 - y))
    return x, y, c, chips


def _remote(src, dst, ssem, rsem, dev):
    return pltpu.make_async_remote_copy(src_ref=src, dst_ref=dst, send_sem=ssem, recv_sem=rsem,
                                        device_id=dev, device_id_type=MESH)


def _cast_weights(place, ws, after, idx, name):
    nblk = 4
    n = len(idx)
    dims = [BIG[w][1:] for w in idx]

    def body(pref, *refs):
        for i in range(n):
            refs[n + 1 + i][0] = refs[i][...].astype(BF16)

    grid_spec = pltpu.PrefetchScalarGridSpec(
        num_scalar_prefetch=1, grid=(nblk,),
        in_specs=[pl.BlockSpec((k, r // nblk, cdim), lambda i, pref: (0, i, 0)) for k, r, cdim in dims]
        + [pl.BlockSpec(memory_space=pl.ANY)],
        out_specs=[pl.BlockSpec((1, k, r // nblk, cdim), lambda i, pref: (pref[1], 0, i, 0)) for k, r, cdim in dims])
    return pl.pallas_call(
        body, name=name, grid_spec=grid_spec,
        out_shape=[_sds((4, k, r, cdim), BF16) for k, r, cdim in dims],
        compiler_params=_params(("parallel",)),
    )(place, *ws, after)


LAYER_A = (0, 1, 2)
LAYER_B = (3, 4)
HBM = pl.BlockSpec(memory_space=pltpu.HBM)
SEM = pl.BlockSpec(memory_space=pltpu.SEMAPHORE)
EFFECT = pltpu.SideEffectType.DATAFLOW_SIDE_EFFECTING
TOKEN = (8, 128)


def _half(ref, w, which):
    h = BIG[w][2] // 2
    return ref.at[:, pl.ds(which * h, h), :]


def _skip_arg(body, pos, *refs):
    return body(*refs[:pos], *refs[pos + 1:])


def _gather_start(wb, after, idx, name, barrier_id):
    n = len(idx)

    def body(*refs):
        src = refs[:n]
        send_sems, recv_sems = refs[n + 1], refs[n + 2]
        token = refs[2 * n + 3]
        x, y, c, chips = _place()
        _peer_barrier([(px, py, c) for px, py in chips])
        me = 2 * x + y
        for i in range(n):
            for j, (px, py) in enumerate(chips):
                mine = _half(src[i].at[me], idx[i], c)
                _remote(mine, mine, send_sems.at[j * n + i], recv_sems.at[j * n + i], (px, py, c)).start()
        token[...] = jnp.zeros(TOKEN, F32)

    outs = pl.pallas_call(
        body, name=name, in_specs=[HBM] * n + [ANY],
        out_specs=(SEM, SEM) + (HBM,) * n + (pl.BlockSpec(memory_space=pltpu.VMEM),),
        out_shape=(pltpu.SemaphoreType.DMA((3 * n,)), pltpu.SemaphoreType.DMA((3 * n,)))
        + tuple(pltpu.HBM(w.shape, w.dtype) for w in wb) + (_sds(TOKEN, F32),),
        input_output_aliases={i: 2 + i for i in range(n)},
        compiler_params=pltpu.CompilerParams(has_side_effects=EFFECT, collective_id=barrier_id),
    )(*[pltpu.with_memory_space_constraint(w, pltpu.HBM) for w in wb], after)
    return outs[0], outs[1], list(outs[2:2 + n]), outs[2 + n]


def _gather_wait(send_sems, recv_sems, wb, after, idx, name, started=None):
    n = len(idx)
    started = idx if started is None else started
    n_all = len(started)
    pos = [started.index(w) for w in idx]

    def body(*refs):
        buf = refs[:n]
        send_sems, recv_sems = refs[n], refs[n + 1]
        x, y, c, chips = _place()
        me = 2 * x + y
        for j, (px, py) in enumerate(chips):
            for i in range(n):
                mine = _half(buf[i].at[me], idx[i], c)
                got = _half(buf[i].at[2 * px + py], idx[i], c)
                k = j * n_all + pos[i]
                _remote(mine, mine, send_sems.at[k], recv_sems.at[k], (px, py, c)).wait_send()
                _remote(got, got, send_sems.at[k], recv_sems.at[k], (px, py, c)).wait_recv()

    outs = pl.pallas_call(
        body, name=name, in_specs=[HBM] * n + [SEM, SEM] + [ANY] * len(after), out_specs=(HBM,) * n,
        out_shape=tuple(pltpu.HBM(w.shape, w.dtype) for w in wb),
        input_output_aliases={i: i for i in range(n)},
        compiler_params=pltpu.CompilerParams(has_side_effects=EFFECT),
    )(*wb, send_sems, recv_sems, *after)
    return list(outs)


def _gather_forward(wb, idx, name, barrier_id):
    n = len(idx)

    def body(*refs):
        dst = refs[n:2 * n]
        send_sems, recv_sems = refs[2 * n], refs[2 * n + 1]
        x, y, c, chips = _place()
        _sibling_barrier(x, y, c)
        cps = []
        for j, (px, py) in enumerate(chips):
            for i in range(n):
                got = _half(dst[i].at[2 * px + py], idx[i], c)
                cps.append(_remote(got, got, send_sems.at[j, i], recv_sems.at[j, i], (x, y, 1 - c)))
                cps[-1].start()
        for j, (px, py) in enumerate(chips):
            for i in range(n):
                got = _half(dst[i].at[2 * px + py], idx[i], 1 - c)
                _remote(got, got, send_sems.at[j, i], recv_sems.at[j, i], (x, y, 1 - c)).wait_recv()
        for cp in cps:
            cp.wait_send()

    return pl.pallas_call(
        body, name=name, in_specs=[ANY] * n, out_specs=[ANY] * n, out_shape=[_sds(w.shape, BF16) for w in wb],
        input_output_aliases={i: i for i in range(n)},
        scratch_shapes=[pltpu.SemaphoreType.DMA((3, n)), pltpu.SemaphoreType.DMA((3, n))],
        compiler_params=pltpu.CompilerParams(collective_id=barrier_id),
    )(*wb)


def _forward_start(wb, cw, after, idx, name, barrier_id):
    n = len(idx)
    m = n if cw is None else n + 2

    def body(*refs):
        buf = refs[:n]
        send_sems, recv_sems = refs[m + 1], refs[m + 2]
        token = refs[2 * m + 3]
        x, y, c, chips = _place()
        _peer_barrier([(x, y, 1 - c)] + ([] if cw is None else [(px, py, c) for px, py in chips]))
        for j, (px, py) in enumerate(chips):
            for i in range(n):
                got = _half(buf[i].at[2 * px + py], idx[i], c)
                _remote(got, got, send_sems.at[j * (n + 1) + i], recv_sems.at[j * (n + 1) + i], (x, y, 1 - c)).start()
            if cw is not None:
                _remote(refs[n], refs[n + 1].at[2 * x + y], send_sems.at[j * (n + 1) + n],
                        recv_sems.at[j * (n + 1) + n], (px, py, c)).start()
        token[...] = jnp.zeros(TOKEN, F32)

    arrays = list(wb) if cw is None else list(wb) + [cw, lax.empty((4, CW_ROWS, SH_O), F32)]
    outs = pl.pallas_call(
        body, name=name, in_specs=[HBM] * m + [ANY],
        out_specs=(SEM, SEM) + (HBM,) * m + (pl.BlockSpec(memory_space=pltpu.VMEM),),
        out_shape=(pltpu.SemaphoreType.DMA((3 * (n + 1),)), pltpu.SemaphoreType.DMA((3 * (n + 1),)))
        + tuple(pltpu.HBM(a.shape, a.dtype) for a in arrays) + (_sds(TOKEN, F32),),
        input_output_aliases={i: 2 + i for i in range(m)},
        compiler_params=pltpu.CompilerParams(has_side_effects=EFFECT, collective_id=barrier_id),
    )(*[pltpu.with_memory_space_constraint(a, pltpu.HBM) for a in arrays], after)
    return outs[0], outs[1], list(outs[2:2 + m]), outs[2 + m]


def _forward_wait(send_sems, recv_sems, arrays, after, idx, with_cw, name):
    n = len(idx)
    m = len(arrays)

    def body(*refs):
        buf = refs[:n]
        send_sems, recv_sems = refs[m], refs[m + 1]
        x, y, c, chips = _place()
        for j, (px, py) in enumerate(chips):
            for i in range(n):
                sent = _half(buf[i].at[2 * px + py], idx[i], c)
                got = _half(buf[i].at[2 * px + py], idx[i], 1 - c)
                k = j * (n + 1) + i
                _remote(sent, sent, send_sems.at[k], recv_sems.at[k], (x, y, 1 - c)).wait_send()
                _remote(got, got, send_sems.at[k], recv_sems.at[k], (x, y, 1 - c)).wait_recv()
            if with_cw:
                k = j * (n + 1) + n
                theirs = refs[n + 1].at[2 * px + py]
                _remote(refs[n], theirs, send_sems.at[k], recv_sems.at[k], (px, py, c)).wait_send()
                _remote(refs[n], theirs, send_sems.at[k], recv_sems.at[k], (px, py, c)).wait_recv()

    outs = pl.pallas_call(
        body, name=name, in_specs=[HBM] * m + [SEM, SEM] + [ANY] * len(after), out_specs=(HBM,) * m,
        out_shape=tuple(pltpu.HBM(a.shape, a.dtype) for a in arrays),
        input_output_aliases={i: i for i in range(m)},
        compiler_params=pltpu.CompilerParams(has_side_effects=EFFECT),
    )(*arrays, send_sems, recv_sems, *after)
    return list(outs)


def _peer_barrier(peers):
    barrier = pltpu.get_barrier_semaphore()
    for peer in peers:
        pl.semaphore_signal(barrier, inc=1, device_id=peer, device_id_type=MESH)
    pl.semaphore_wait(barrier, len(peers))


def _sibling_barrier(x, y, c):
    _peer_barrier([(x, y, 1 - c)])


def _pair_exchange(gs, idx, name, barrier_id):
    n = len(idx)

    def body(*refs):
        src, dst = refs[:n], refs[n:2 * n]
        send_sems, recv_sems = refs[2 * n:]
        x, y, c, _ = _place()
        _sibling_barrier(x, y, c)
        cps = []
        for i in range(n):
            h = BIG[idx[i]][2] // 2
            cps.append(_remote(src[i].at[:, :, pl.ds((1 - c) * h, h), :], dst[i], send_sems.at[i], recv_sems.at[i],
                               (x, y, 1 - c)))
            cps[-1].start()
        for cp in cps:
            cp.wait()

    return pl.pallas_call(
        body, name=name, in_specs=[ANY] * n, out_specs=[ANY] * n,
        out_shape=[_sds((4, BIG[w][1], BIG[w][2] // 2, BIG[w][3]), BF16) for w in idx],
        scratch_shapes=[pltpu.SemaphoreType.DMA((n,)), pltpu.SemaphoreType.DMA((n,))],
        compiler_params=pltpu.CompilerParams(collective_id=barrier_id),
    )(*gs)


def _pair_start(gs, idx, name, barrier_id):
    n = len(idx)

    def body(*refs):
        src, land = refs[:n], refs[n:2 * n]
        send_sems, recv_sems = refs[2 * n], refs[2 * n + 1]
        token = refs[4 * n + 2]
        x, y, c, _ = _place()
        _sibling_barrier(x, y, c)
        for i in range(n):
            h = BIG[idx[i]][2] // 2
            _remote(src[i].at[:, :, pl.ds((1 - c) * h, h), :], land[i], send_sems.at[i], recv_sems.at[i],
                    (x, y, 1 - c)).start()
        token[...] = jnp.zeros(TOKEN, F32)

    lands = [lax.empty((4, BIG[w][1], BIG[w][2] // 2, BIG[w][3]), BF16) for w in idx]
    arrays = list(gs) + lands
    outs = pl.pallas_call(
        body, name=name, in_specs=[HBM] * (2 * n),
        out_specs=(SEM, SEM) + (HBM,) * (2 * n) + (pl.BlockSpec(memory_space=pltpu.VMEM),),
        out_shape=(pltpu.SemaphoreType.DMA((n,)), pltpu.SemaphoreType.DMA((n,)))
        + tuple(pltpu.HBM(a.shape, a.dtype) for a in arrays) + (_sds(TOKEN, F32),),
        input_output_aliases={i: 2 + i for i in range(2 * n)},
        compiler_params=pltpu.CompilerParams(has_side_effects=EFFECT, collective_id=barrier_id),
    )(*[pltpu.with_memory_space_constraint(a, pltpu.HBM) for a in arrays])
    return outs[0], outs[1], list(outs[2:2 + n]), list(outs[2 + n:2 + 2 * n]), outs[2 + 2 * n]


def _pair_wait(send_sems, recv_sems, gs, lands, after, idx, name):
    n = len(idx)

    def body(*refs):
        src, land = refs[:n], refs[n:2 * n]
        send_sems, recv_sems = refs[2 * n], refs[2 * n + 1]
        x, y, c, _ = _place()
        for i in range(n):
            h = BIG[idx[i]][2] // 2
            cp = _remote(src[i].at[:, :, pl.ds((1 - c) * h, h), :], land[i], send_sems.at[i], recv_sems.at[i],
                         (x, y, 1 - c))
            cp.wait_send()
            cp.wait_recv()

    arrays = list(gs) + list(lands)
    outs = pl.pallas_call(
        body, name=name, in_specs=[HBM] * (2 * n) + [SEM, SEM] + [ANY] * len(after), out_specs=(HBM,) * (2 * n),
        out_shape=tuple(pltpu.HBM(a.shape, a.dtype) for a in arrays),
        input_output_aliases={i: i for i in range(2 * n)},
        compiler_params=pltpu.CompilerParams(has_side_effects=EFFECT),
    )(*arrays, send_sems, recv_sems, *after)
    return list(outs[:n]), list(outs[n:])


def _pair_sums(place, gs, r1s, idx, name):
    n = len(idx)
    dims = [(BIG[w][1], BIG[w][2] // 2, BIG[w][3]) for w in idx]

    def body(pref, *refs):
        for i in range(n):
            refs[2 * n + i][...] = (refs[i][...] + refs[n + i][...].astype(F32)).astype(BF16)

    mine = [pl.BlockSpec((1, k, h, cdim), lambda s, pref: (s, 0, pref[0], 0)) for k, h, cdim in dims]
    whole = [pl.BlockSpec((1, k, h, cdim), lambda s, pref: (s, 0, 0, 0)) for k, h, cdim in dims]
    grid_spec = pltpu.PrefetchScalarGridSpec(num_scalar_prefetch=1, grid=(4,), in_specs=mine + whole, out_specs=whole)
    return pl.pallas_call(
        body, name=name, grid_spec=grid_spec, out_shape=[_sds((4, k, h, cdim), BF16) for k, h, cdim in dims],
        compiler_params=_params(("parallel",)),
    )(place, *gs, *r1s)


def _chip_start(ps, idx, name, barrier_id):
    n = len(idx)

    def body(*refs):
        src, land = refs[:n], refs[n:2 * n]
        send_sems, recv_sems = refs[2 * n], refs[2 * n + 1]
        token = refs[4 * n + 2]
        x, y, c, chips = _place()
        _peer_barrier([(px, py, c) for px, py in chips])
        for j, (px, py) in enumerate(chips):
            for i in range(n):
                _remote(src[i].at[2 * px + py], land[i].at[j], send_sems.at[j * n + i], recv_sems.at[j * n + i],
                        (px, py, c)).start()
        token[...] = jnp.zeros(TOKEN, F32)

    lands = [lax.empty((3,) + p.shape[1:], BF16) for p in ps]
    outs = pl.pallas_call(
        body, name=name, in_specs=[HBM] * (2 * n),
        out_specs=(SEM, SEM) + (HBM,) * (2 * n) + (pl.BlockSpec(memory_space=pltpu.VMEM),),
        out_shape=(pltpu.SemaphoreType.DMA((3 * n,)), pltpu.SemaphoreType.DMA((3 * n,)))
        + tuple(pltpu.HBM(a.shape, a.dtype) for a in list(ps) + lands) + (_sds(TOKEN, F32),),
        input_output_aliases={i: 2 + i for i in range(2 * n)},
        compiler_params=pltpu.CompilerParams(has_side_effects=EFFECT, collective_id=barrier_id),
    )(*[pltpu.with_memory_space_constraint(a, pltpu.HBM) for a in list(ps) + lands])
    return outs[0], outs[1], list(outs[2:2 + n]), list(outs[2 + n:2 + 2 * n]), outs[2 + 2 * n]


def _chip_wait(send_sems, recv_sems, ps, lands, after, idx, name):
    n = len(idx)

    def body(*refs):
        src, land = refs[:n], refs[n:2 * n]
        send_sems, recv_sems = refs[2 * n], refs[2 * n + 1]
        x, y, c, chips = _place()
        for j, (px, py) in enumerate(chips):
            for i in range(n):
                cp = _remote(src[i].at[2 * px + py], land[i].at[j], send_sems.at[j * n + i], recv_sems.at[j * n + i],
                             (px, py, c))
                cp.wait_send()
                cp.wait_recv()

    arrays = list(ps) + list(lands)
    outs = pl.pallas_call(
        body, name=name, in_specs=[HBM] * (2 * n) + [SEM, SEM] + [ANY] * len(after), out_specs=(HBM,) * (2 * n),
        out_shape=tuple(pltpu.HBM(a.shape, a.dtype) for a in arrays),
        input_output_aliases={i: i for i in range(2 * n)},
        compiler_params=pltpu.CompilerParams(has_side_effects=EFFECT),
    )(*arrays, send_sems, recv_sems, *after)
    return list(outs[n:])


def _chip_sums(place, gs, r1s, r2s, idx, name):
    n = len(idx)
    dims = [(BIG[w][1], BIG[w][2] // 4, BIG[w][3]) for w in idx]

    def body(pref, *refs):
        for i in range(n):
            acc = refs[i][0] + refs[n + i][0].astype(F32)
            for j in range(3):
                acc = acc + refs[2 * n + i][j].astype(F32)
            refs[3 * n + i][...] = acc

    in_specs = ([pl.BlockSpec((1, k, q, cdim), lambda t, pref: (pref[1], 0, pref[0] * 2 + t, 0)) for k, q, cdim in dims]
                + [pl.BlockSpec((1, k, q, cdim), lambda t, pref: (pref[1], 0, t, 0)) for k, q, cdim in dims]
                + [pl.BlockSpec((3, k, q, cdim), lambda t, pref: (0, 0, t, 0)) for k, q, cdim in dims])
    out_specs = [pl.BlockSpec((k, q, cdim), lambda t, pref: (0, pref[0] * 2 + t, 0)) for k, q, cdim in dims]
    grid_spec = pltpu.PrefetchScalarGridSpec(num_scalar_prefetch=1, grid=(2,), in_specs=in_specs, out_specs=out_specs)
    return pl.pallas_call(
        body, name=name, grid_spec=grid_spec, out_shape=[_sds(BIG[w][1:], F32) for w in idx],
        compiler_params=_params(("parallel",)),
    )(place, *gs, *r1s, *r2s)


def _pair_gather(hs, idx, name, barrier_id):
    n = len(idx)

    def body(*refs):
        dst = refs[n:2 * n]
        send_sems, recv_sems = refs[2 * n:]
        x, y, c, _ = _place()
        _sibling_barrier(x, y, c)
        cps = []
        for i in range(n):
            mine = _half(dst[i], idx[i], c)
            cps.append(_remote(mine, mine, send_sems.at[i], recv_sems.at[i], (x, y, 1 - c)))
            cps[-1].start()
        for i in range(n):
            theirs = _half(dst[i], idx[i], 1 - c)
            _remote(theirs, theirs, send_sems.at[i], recv_sems.at[i], (x, y, 1 - c)).wait_recv()
        for cp in cps:
            cp.wait_send()

    return pl.pallas_call(
        body, name=name, in_specs=[ANY] * n, out_specs=[ANY] * n,
        out_shape=[_sds(BIG[w][1:], F32) for w in idx],
        input_output_aliases={i: i for i in range(n)},
        scratch_shapes=[pltpu.SemaphoreType.DMA((n,)), pltpu.SemaphoreType.DMA((n,))],
        compiler_params=pltpu.CompilerParams(collective_id=barrier_id),
    )(*hs)


SMALL_ROWS = 40


def _adamw_math(w, g, m, v):
    m = ADAM_B1 * m + (1.0 - ADAM_B1) * g
    v = ADAM_B2 * v + (1.0 - ADAM_B2) * (g * g)
    m_hat = m / (1.0 - ADAM_B1 ** ADAM_STEP)
    v_hat = v / (1.0 - ADAM_B2 ** ADAM_STEP)
    delta = -ADAM_LR * (m_hat / (jnp.sqrt(v_hat) + ADAM_EPS) + ADAM_WD * w)
    return delta, m, v


def _small_start(pack, after):
    def body(pack_ref, land_ref, after_ref, send_sems, recv_sems, pack_thru, land_thru, token):
        x, y, c, _ = _place()
        for r in range(1, 8):
            peer = (x if not r & 4 else 1 - x, y if not r & 2 else 1 - y, c if not r & 1 else 1 - c)
            _remote(pack_ref, land_ref.at[r - 1], send_sems.at[r - 1], recv_sems.at[r - 1], peer).start()
        token[...] = jnp.zeros(TOKEN, F32)

    land = lax.empty((7, SMALL_ROWS, D), F32)
    outs = pl.pallas_call(
        body, name="small_start", in_specs=[HBM, HBM, ANY],
        out_specs=(SEM, SEM, HBM, HBM, pl.BlockSpec(memory_space=pltpu.VMEM)),
        out_shape=(pltpu.SemaphoreType.DMA((7,)), pltpu.SemaphoreType.DMA((7,)), pltpu.HBM(pack.shape, F32),
                   pltpu.HBM(land.shape, F32), _sds(TOKEN, F32)),
        input_output_aliases={0: 2, 1: 3},
        compiler_params=pltpu.CompilerParams(has_side_effects=EFFECT),
    )(pltpu.with_memory_space_constraint(pack, pltpu.HBM), pltpu.with_memory_space_constraint(land, pltpu.HBM), after)
    return outs


def _small_wait(send_sems, recv_sems, pack, land, after):
    def body(pack_ref, land_ref, send_sems, recv_sems, *rest):
        x, y, c, _ = _place()
        for r in range(1, 8):
            peer = (x if not r & 4 else 1 - x, y if not r & 2 else 1 - y, c if not r & 1 else 1 - c)
            cp = _remote(pack_ref, land_ref.at[r - 1], send_sems.at[r - 1], recv_sems.at[r - 1], peer)
            cp.wait_send()
            cp.wait_recv()

    return pl.pallas_call(
        body, name="small_wait", in_specs=[HBM, HBM, SEM, SEM] + [ANY] * len(after), out_specs=(HBM, HBM),
        out_shape=(pltpu.HBM(pack.shape, F32), pltpu.HBM(land.shape, F32)),
        input_output_aliases={0: 0, 1: 1},
        compiler_params=pltpu.CompilerParams(has_side_effects=EFFECT),
    )(pack, land, send_sems, recv_sems, *after)


def _small_update(place, pack, land, ws, ms, vs):
    n = len(ws)

    def body(pref, pack_ref, land_ref, *refs):
        chip = pref[1]
        me = 2 * chip + pref[0]
        own = pack_ref[...]
        tot = None
        for dev in range(8):
            r = jnp.bitwise_xor(me, dev)
            term = jnp.where(r == 0, own, land_ref[jnp.maximum(r - 1, 0)])
            tot = term if tot is None else tot + term
        out, buf = refs[3 * n:-1], refs[-1]
        buf[...] = tot
        g_conv = jnp.zeros((3, SH_O), F32)
        for s in range(4):
            g_conv = g_conv + jnp.where(chip == s, buf[24:27, s * SH_O:(s + 1) * SH_O], 0.0)
        gs = [buf[0:2, :], buf[8:10, :], buf[16:17, :], g_conv]
        out[0][...] = buf[32:33, 0:128]
        for i in range(n):
            d, nm, nv = _adamw_math(refs[i][...], gs[i], refs[n + i][...], refs[2 * n + i][...])
            for j, val in enumerate((gs[i], d, nm, nv)):
                out[1 + j * n + i][...] = val.reshape(flat[i])

    def full(shape):
        nd = len(shape)
        return pl.BlockSpec(shape, lambda i, pref: (0,) * nd)

    flat = [w.shape[1:] if w.shape[0] == 1 else w.shape for w in ws]
    specs = [full(w.shape) for w in ws]
    grid_spec = pltpu.PrefetchScalarGridSpec(
        num_scalar_prefetch=1, grid=(1,),
        in_specs=[full(pack.shape), full(land.shape)] + specs * 3,
        out_specs=[full((1, 128))] + [full(s) for s in flat] * 4,
        scratch_shapes=[pltpu.VMEM((SMALL_ROWS, D), F32)])
    outs = pl.pallas_call(
        body, name="small_update", grid_spec=grid_spec,
        out_shape=[_sds((1, 128), F32)] + [_sds(s, F32) for s in flat] * 4,
        compiler_params=_params(("arbitrary",)),
    )(place, pack, land, *ws, *ms, *vs)
    return outs[0], outs[1:1 + n], outs[1 + n:1 + 2 * n], outs[1 + 2 * n:1 + 3 * n], outs[1 + 3 * n:]


def _adamw_layer(ws, gs, ms, vs, idx, name):
    n = len(idx)
    dims = [(BIG[w][1], BIG[w][2] // 4, BIG[w][3]) for w in idx]

    def body(*refs):
        for i in range(n):
            gv = refs[n + i][...]
            d, nm, nv = _adamw_math(refs[i][...], gv, refs[2 * n + i][...], refs[3 * n + i][...])
            refs[4 * n + i][...] = d
            refs[5 * n + i][...] = nm
            refs[6 * n + i][...] = nv
            refs[7 * n + i][...] = gv

    specs = [pl.BlockSpec((k, q, cdim), lambda t: (0, t, 0)) for k, q, cdim in dims]
    outs = pl.pallas_call(
        body, name=name, grid=(4,), in_specs=specs * 4, out_specs=specs * 4,
        out_shape=[_sds(BIG[w][1:], F32) for w in idx] * 4,
        compiler_params=_params(("parallel",)),
    )(*ws, *gs, *ms, *vs)
    return [tuple(outs[j * n + i] for j in range(4)) for i in range(n)]


def _pad_rows(a, rows):
    return jnp.pad(a, ((0, rows - a.shape[0]), (0, 0)))


def kernel(x, mem, positions, norm_g, mem_norm_g, w_mem_kv, attn_w_in, attn_w_out, conv_w_in, conv_w, conv_w_out, final_g, loss_target, m_norm_g, m_mem_norm_g, m_w_mem_kv, m_attn_w_in, m_attn_w_out, m_conv_w_in, m_conv_w, m_conv_w_out, m_final_g, v_norm_g, v_mem_norm_g, v_w_mem_kv, v_attn_w_in, v_attn_w_out, v_conv_w_in, v_conv_w, v_conv_w_out, v_final_g):
    mx, my, mc = lax.axis_index("x"), lax.axis_index("y"), lax.axis_index("c")
    place = jnp.stack([mc, 2 * mx + my]).astype(jnp.int32)

    w_big = [w_mem_kv, attn_w_in, attn_w_out, conv_w_in, conv_w_out]
    m_big = [m_w_mem_kv, m_attn_w_in, m_attn_w_out, m_conv_w_in, m_conv_w_out]
    v_big = [v_w_mem_kv, v_attn_w_in, v_attn_w_out, v_conv_w_in, v_conv_w_out]
    first, rest = (1,), (0, 2, 3, 4)
    wb1 = _cast_weights(place, [w_big[i] for i in first], place, first, "cast_w_in_a")
    a1_send, a1_recv, a1_bufs, a1_token = _gather_start(wb1, place, first, "gather_a1_start", 4)
    wbr = _cast_weights(place, [w_big[i] for i in rest], a1_token, rest, "cast_weights")
    r_send, r_recv, r_bufs, gb_token = _gather_start(wbr, a1_token, rest, "gather_rest_start", 5)
    a2_send, a2_recv, gb_send, gb_recv = r_send, r_recv, r_send, r_recv
    a2_bufs, gb_bufs = r_bufs[:2], r_bufs[2:]
    started, rest = rest, (0, 2)

    xs, tgt = x[0], loss_target[0]
    g0, g1 = norm_g[0:1], norm_g[1:2]
    rc, rs1, rs2 = _rope_tables(positions[0].astype(F32).reshape(S, 1), gb_token)
    a1_bufs = _gather_wait(a1_send, a1_recv, a1_bufs, [rc], first, "gather_a1_wait")
    w_in_a = _gather_forward(a1_bufs, first, "gather_a1_forward", 0)[0].reshape(4, D, SH_A)
    hn0, q, k, v, qm0, z0 = _in_proj_a(xs, g0, w_in_a, rc, rs1, rs2, gb_token)
    a2_bufs = _gather_wait(a2_send, a2_recv, a2_bufs, [q], rest, "gather_a2_wait", started)
    f2_send, f2_recv, a2_bufs, f2_token = _forward_start(a2_bufs, None, q, rest, "forward_a2_start", 9)
    fwd = [_attn_fwd(q, k, v, 0, f2_token)]
    fwd.append(_attn_fwd(q, k, v, 1, fwd[0][0]))
    fwd.append(_attn_fwd(q, k, v, 2, fwd[1][0]))
    os_, ls, lss = [f[0] for f in fwd], [f[1] for f in fwd], [f[2] for f in fwd]
    cw_own = _pad_rows(conv_w[0], CW_ROWS)
    gb_bufs = _gather_wait(gb_send, gb_recv, gb_bufs, [os_[2]], LAYER_B, "gather_b_wait", started)
    fb_send, fb_recv, gb_bufs, fb_token = _forward_start(gb_bufs, cw_own, os_[2], LAYER_B, "forward_b_start", 10)
    wkv_f, w_out_a = _forward_wait(f2_send, f2_recv, a2_bufs, [os_[2], fb_token], rest, False, "forward_a2_wait")
    w_out_a = w_out_a.reshape(4, BR_A, SH_O)
    memn, kv = _mem_fwd(mem[0], mem_norm_g, wkv_f)
    h1 = _attn_out(os_, ls, qm0, kv[0], z0, xs, w_out_a)

    w_in_b, w_out_b, _, cw_f = _forward_wait(fb_send, fb_recv, gb_bufs, [h1], LAYER_B, True, "forward_b_wait")
    w_in_b = w_in_b.reshape(4, D, SH_B)
    w_out_b = w_out_b.reshape(BR_B, D)
    cw_f = lax.dynamic_update_slice(cw_f, cw_own[None], (2 * mx + my, 0, 0))
    cw8 = cw_f.transpose(1, 0, 2).reshape(CW_ROWS, D)
    hn1, bg, cg, u, qm1, z1 = _in_proj_b(h1, g1, w_in_b)
    dh2, loss_part, dfg = _conv_out_loss(bg, cg, u, cw8, qm1, kv[1], z1, h1, w_out_b, final_g.reshape(1, D), tgt)

    dproj_b, dw_out_b, dcw, dkv1, dw_out_b16 = _conv_bwd(dh2, bg, cg, u, cw8, qm1, kv[1], z1, w_out_b)
    dw_in_b, dw_in_b16 = _w_in_grad(hn1, dproj_b, IN_B, "w_in_b_grad")
    gs_b = [dw_in_b.reshape(4, 1, D, SH_B), dw_out_b.reshape(4, 1, BR_B // 4, D)]
    gb_b = [dw_in_b16.reshape(4, 1, D, SH_B), dw_out_b16.reshape(4, 1, BR_B // 4, D)]
    pb_send, pb_recv, gb_b, pb_land, pb_token = _pair_start(gb_b, LAYER_B, "pair_b_start", 6)
    dh1, dg1 = _in_proj_bwd(dproj_b, w_in_b, h1, g1, dh2, pb_token, IN_B, "in_proj_b_bwd")
    _, r1_b = _pair_wait(pb_send, pb_recv, gb_b, pb_land, [dh1], LAYER_B, "pair_b_wait")
    ps_b = _pair_sums(place, gs_b, r1_b, LAYER_B, "pair_sums_b")
    cb_send, cb_recv, cb_src, cb_land, cb_token = _chip_start(ps_b, LAYER_B, "chip_b_start", 7)

    outs = _attn_out_bwd(dh1, os_, ls, qm0, kv[0], z0, w_out_a, cb_token)
    dos, dds, dqm, dz, dw_out_a, dkv0, dw_out_a16 = outs[0:3], outs[3:6], outs[6], outs[7], outs[8], outs[9], outs[10]
    bwd = [_attn_bwd(q, k, v, dos[g], lss[g], dds[g], g) for g in range(3)]
    dproj_a = _qkv_bwd([b[0] for b in bwd], [b[1] for b in bwd], [b[2] for b in bwd], dqm, dz, rc, rs1, rs2)
    dw_in_a, dw_in_a16 = _w_in_grad(hn0, dproj_a, IN_A, "w_in_a_grad")
    dwkv, dwkv16, dmg = _mem_bwd(mem[0], mem_norm_g, memn, wkv_f, dkv0, dkv1)

    gs_a = [dwkv, dw_in_a.reshape(4, 1, D, SH_A), dw_out_a.reshape(4, 1, BR_A, SH_O)]
    r1_a = _pair_exchange([dwkv16, dw_in_a16.reshape(4, 1, D, SH_A), dw_out_a16.reshape(4, 1, BR_A, SH_O)], LAYER_A,
                          "pair_exchange_a", 1)
    ps_a = _pair_sums(place, gs_a, r1_a, LAYER_A, "pair_sums_a")
    ca_send, ca_recv, ca_src, ca_land, ca_token = _chip_start(ps_a, LAYER_A, "chip_a_start", 8)

    gx, dg0 = _in_proj_bwd(dproj_a, w_in_a, xs, g0, dh1, ca_token, IN_A, "in_proj_a_bwd")
    pack = jnp.concatenate([_pad_rows(jnp.concatenate([dg0, dg1], axis=0), 8), _pad_rows(dmg, 8), _pad_rows(dfg, 8),
                            dcw, _pad_rows(jnp.pad(loss_part, ((0, 0), (0, D - 128))), 8)], axis=0)
    sm_send, sm_recv, pack, sm_land, sm_token = _small_start(pack, ca_token)
    r2_b = _chip_wait(cb_send, cb_recv, cb_src, cb_land, [ca_token], LAYER_B, "chip_b_wait")
    hs_b = _chip_sums(place, gs_b, r1_b, r2_b, LAYER_B, "chip_sums_b")
    g_b = _pair_gather(hs_b, LAYER_B, "pair_gather_b", 2)
    upd_b = _adamw_layer([w_big[w] for w in LAYER_B], g_b, [m_big[w] for w in LAYER_B], [v_big[w] for w in LAYER_B],
                         LAYER_B, "adamw_b")
    r2_a = _chip_wait(ca_send, ca_recv, ca_src, ca_land, [gx, upd_b[0][0], upd_b[1][0], sm_token], LAYER_A,
                      "chip_a_wait")
    hs_a = _chip_sums(place, gs_a, r1_a, r2_a, LAYER_A, "chip_sums_a")
    g_a = _pair_gather(hs_a, LAYER_A, "pair_gather_a", 3)
    upd_a = _adamw_layer([w_big[w] for w in LAYER_A], g_a, [m_big[w] for w in LAYER_A], [v_big[w] for w in LAYER_A],
                         LAYER_A, "adamw_a")
    upd = upd_a + upd_b
    g_big = [u[3] for u in upd]
    pack, sm_land = _small_wait(sm_send, sm_recv, pack, sm_land, [r2_a[0]])
    sw = [norm_g, mem_norm_g, final_g.reshape(1, D), conv_w[0]]
    sm = [m_norm_g, m_mem_norm_g, m_final_g.reshape(1, D), m_conv_w[0]]
    sv = [v_norm_g, v_mem_norm_g, v_final_g.reshape(1, D), v_conv_w[0]]
    loss_row, sg, sd, snm, snv = _small_update(place, pack, sm_land, sw, sm, sv)
    loss = loss_row[0, 0]
    g_norm, g_memnorm, g_final, g_conv = sg

    def order(norm, memnorm, wkv, w_in_a, w_out_a, w_in_b, conv, w_out_b, final):
        return (norm, memnorm, wkv, w_in_a, w_out_a, w_in_b, conv.reshape(1, 3, SH_O), w_out_b, final.reshape(D))

    grads = order(g_norm, g_memnorm, g_big[0], g_big[1], g_big[2], g_big[3], g_conv, g_big[4], g_final)
    deltas = order(sd[0], sd[1], upd[0][0], upd[1][0], upd[2][0], upd[3][0], sd[3], upd[4][0], sd[2])
    new_m = order(snm[0], snm[1], upd[0][1], upd[1][1], upd[2][1], upd[3][1], snm[3], upd[4][1], snm[2])
    new_v = order(snv[0], snv[1], upd[0][2], upd[1][2], upd[2][2], upd[3][2], snv[3], upd[4][2], snv[2])
    return (loss, gx[None], *grads, *deltas, *new_m, *new_v)
```

```python
import functools

import numpy as np
import jax
import jax.numpy as jnp
from jax import lax
from jax.experimental import pallas as pl
from jax.experimental.pallas import tpu as pltpu

F32 = jnp.float32
BF16 = jnp.bfloat16

S = 2048
D = 1024
TM = 256
NT = S // TM
MX = 512
NX = S // MX
HD = 64
GW = 512
NQ = 3 * GW
MW = 256
NM = 256
IN_A = 3 * NQ + MW + GW + MW
IN_B = 3 * D + MW + D + MW
BR_A = GW + MW
BR_B = D + MW
SH_A = IN_A // 4
SH_B = IN_B // 4
SH_O = D // 4
QBLK = 128
DILATIONS = (1, 4, 16)
EPS = 1e-6
SCALE = HD ** -0.5
NEG = -1e30
ROPE_THETA = 500000.0

ADAM_LR = 0.001
ADAM_B1 = 0.9
ADAM_B2 = 0.999
ADAM_EPS = 1e-08
ADAM_WD = 0.01
ADAM_STEP = 10

VMEM_LIMIT_BYTES = 60 * 1024 * 1024


def _params(sem=None):
    if sem is None:
        return pltpu.CompilerParams(vmem_limit_bytes=VMEM_LIMIT_BYTES)
    return pltpu.CompilerParams(dimension_semantics=sem, vmem_limit_bytes=VMEM_LIMIT_BYTES)


def _full(shape):
    nd = len(shape)
    return pl.BlockSpec(shape, lambda *_: (0,) * nd)


def _rows(width, tm=TM):
    return pl.BlockSpec((tm, width), lambda i: (i, 0))


def _sds(shape, dtype):
    return jax.ShapeDtypeStruct(shape, dtype)


def _silu_parts(z):
    sig = 0.5 * jnp.tanh(0.5 * z) + 0.5
    return z * sig, sig * (1.0 + z * (1.0 - sig))


def _dot(a, b):
    return jnp.dot(a, b, preferred_element_type=F32)


def _dot_nt(a, b):
    return lax.dot_general(a, b, (((1,), (1,)), ((), ())), preferred_element_type=F32)


def _dot_tn(a, b):
    return lax.dot_general(a, b, (((0,), (0,)), ((), ())), preferred_element_type=F32)


def _rope_fwd(t, c, s1, s2):
    return t * c + pltpu.roll(t, 120, 1) * s1 + pltpu.roll(t, 8, 1) * s2


def _rope_bwd(g, c, s1, s2):
    return g * c + pltpu.roll(g * s1, 8, 1) + pltpu.roll(g * s2, 120, 1)


MEM_HEADS = MW // HD


def _stack_heads(x):
    head = lax.broadcasted_iota(jnp.int32, x.shape, 1) // HD
    return jnp.concatenate([jnp.where(head == h, x, 0.0) for h in range(MEM_HEADS)], axis=0).astype(BF16)


def _unstack_heads(x4):
    tm = x4.shape[0] // MEM_HEADS
    head = lax.broadcasted_iota(jnp.int32, (tm, MW), 1) // HD
    out = x4[:tm]
    for h in range(1, MEM_HEADS):
        out = jnp.where(head == h, x4[h * tm:(h + 1) * tm], out)
    return out


def _mem_attn(qm, kv):
    q4 = _stack_heads(qm.astype(F32))
    s = _dot_nt(q4, kv[:, :MW]) * SCALE
    e = jnp.exp(s - jnp.max(s, axis=-1, keepdims=True))
    p = e * (1.0 / jnp.sum(e, axis=-1, keepdims=True))
    return p, _unstack_heads(_dot(p.astype(BF16), kv[:, MW:])), q4


def _mem_attn_bwd(dmo, p, mo, q4, kv, dkv_ref):
    tm = dmo.shape[0]
    head = lax.broadcasted_iota(jnp.int32, dmo.shape, 1) // HD
    prod = dmo * mo
    delta = jnp.concatenate([jnp.sum(jnp.where(head == h, prod, 0.0), axis=-1, keepdims=True)
                             for h in range(MEM_HEADS)], axis=0)
    d4 = _stack_heads(dmo)
    ds = (p * (_dot_nt(d4, kv[:, MW:]) - delta) * SCALE).astype(BF16)
    dkv_ref[:, :MW] += _dot_tn(ds, q4)
    dkv_ref[:, MW:] += _dot_tn(p.astype(BF16), d4)
    return _unstack_heads(_dot(ds, kv[:, :MW]))


def _merge(o_refs, l_refs):
    ls = [r[...] for r in l_refs]
    m = jnp.maximum(jnp.maximum(ls[0], ls[1]), ls[2])
    es = [jnp.exp(l - m) for l in ls]
    inv = 1.0 / (es[0] + es[1] + es[2])
    ws = [e * inv for e in es]
    os_ = [r[...] for r in o_refs]
    mix = ws[0] * os_[0] + ws[1] * os_[1] + ws[2] * os_[2]
    return ws, mix


def _conv_taps(cg, u, cgp, up, first):
    a = cg * u
    ap = jnp.where(first, 0.0, cgp * up)
    row = lax.broadcasted_iota(jnp.int32, a.shape, 0)
    a1 = jnp.where(row == 0, ap[7:8, :], pltpu.roll(a, 1, 0))
    a2 = jnp.where(row == 0, ap[6:7, :], jnp.where(row == 1, ap[7:8, :], pltpu.roll(a, 2, 0)))
    return a, a1, a2


def _rope_tables(posf, after):
    half = 8
    invf = np.float32(ROPE_THETA) ** (-np.arange(half, dtype=np.float32) * np.float32(2.0 / 16))
    lane = np.arange(128)
    table = np.where((lane % HD) < 16, invf[lane % half], 0.0).astype(np.float32)[None, :]

    def body(pos_ref, invf_ref, c_ref, s1_ref, s2_ref):
        ang = pos_ref[...] * invf_ref[...]
        jm = lax.broadcasted_iota(jnp.int32, ang.shape, 1) & (HD - 1)
        cs = jnp.cos(ang)
        sn = jnp.sin(ang)
        c_ref[...] = jnp.where(jm < 16, cs, 1.0)
        s1_ref[...] = jnp.where(jm < 8, -sn, 0.0)
        s2_ref[...] = jnp.where((jm >= 8) & (jm < 16), sn, 0.0)

    out = _sds((S, 128), F32)
    return pl.pallas_call(
        functools.partial(_skip_arg, body, 2), name="rope_tables", grid=(NT,),
        in_specs=[_rows(1), _full((1, 128)), pl.BlockSpec(memory_space=pl.ANY)],
        out_specs=[_rows(128)] * 3, out_shape=[out] * 3,
        compiler_params=_params(("parallel",)),
    )(posf, jnp.asarray(table), after)


def _in_proj_a(x, g0, w_in, c, s1, s2, after):
    def body(x_ref, g_ref, w_ref, c_ref, s1_ref, s2_ref, hn_ref, q_ref, k_ref, v_ref, qm_ref, z_ref, proj):
        xf = x_ref[...]
        hn = xf * lax.rsqrt(jnp.mean(xf * xf, axis=-1, keepdims=True) + EPS) * g_ref[...]
        hb = hn.astype(BF16)
        hn_ref[...] = hb
        for s in range(4):
            proj[:, s * SH_A:(s + 1) * SH_A] = _dot(hb, w_ref[s])
        cc, a1, a2 = c_ref[...], s1_ref[...], s2_ref[...]
        for j in range(NQ // 128):
            q_ref[:, j * 128:(j + 1) * 128] = (
                _rope_fwd(proj[:, j * 128:(j + 1) * 128], cc, a1, a2) * SCALE).astype(BF16)
            k_ref[:, j * 128:(j + 1) * 128] = _rope_fwd(
                proj[:, NQ + j * 128:NQ + (j + 1) * 128], cc, a1, a2).astype(BF16)
        v_ref[...] = proj[:, 2 * NQ:3 * NQ].astype(BF16)
        qm_ref[...] = proj[:, 3 * NQ:3 * NQ + MW].astype(BF16)
        z_ref[...] = proj[:, 3 * NQ + MW:]

    return pl.pallas_call(
        functools.partial(_skip_arg, body, 6), name="in_proj_a", grid=(NT,),
        in_specs=[_rows(D), _full((1, D)), _full((4, D, SH_A)), _rows(128), _rows(128), _rows(128),
                  pl.BlockSpec(memory_space=pl.ANY)],
        out_specs=[_rows(D), _rows(NQ), _rows(NQ), _rows(NQ), _rows(MW), _rows(BR_A)],
        out_shape=[_sds((S, D), BF16), _sds((S, NQ), BF16), _sds((S, NQ), BF16), _sds((S, NQ), BF16),
                   _sds((S, MW), BF16), _sds((S, BR_A), F32)],
        scratch_shapes=[pltpu.VMEM((TM, IN_A), F32)],
        compiler_params=_params(("parallel",)),
    )(x, g0, w_in, c, s1, s2, after)


def _mem_fwd(mem, mg, wkv):
    def body(mem_ref, mg_ref, w_ref, memn_ref, kv_ref):
        mf = mem_ref[...]
        n = mf * lax.rsqrt(jnp.mean(mf * mf, axis=-1, keepdims=True) + EPS)
        for i in range(2):
            mn = (n * mg_ref[i:i + 1, :]).astype(BF16)
            memn_ref[i] = mn
            acc = _dot(mn[:, 0:NM], w_ref[0, i])
            for s in range(1, 4):
                acc += _dot(mn[:, s * NM:(s + 1) * NM], w_ref[s, i])
            kv_ref[i] = acc.astype(BF16)

    return pl.pallas_call(
        body, name="mem_fwd", grid=(1,),
        in_specs=[_full((NM, D)), _full((2, D)), _full((4, 2, NM, 2 * MW))],
        out_specs=[_full((2, NM, D)), _full((2, NM, 2 * MW))],
        out_shape=[_sds((2, NM, D), BF16), _sds((2, NM, 2 * MW), BF16)],
        compiler_params=_params(("arbitrary",)),
    )(mem, mg, wkv)


def _band_mask(j):
    qi = lax.broadcasted_iota(jnp.int32, (QBLK, 2 * QBLK), 0)
    kj = lax.broadcasted_iota(jnp.int32, (QBLK, 2 * QBLK), 1)
    dist = qi + QBLK - kj
    return (dist >= 0) & (dist <= QBLK) & ((kj >= QBLK) | (j > 0))


LANES = 128
NCHUNK = GW // LANES
FWD_UNROLL = 16
BWD_UNROLL = 16
CONV_CHUNK = 256


def _perm_matrix(d):
    n = TM // d
    p = np.zeros((TM, TM), np.float32)
    for r in range(d):
        for i in range(n):
            p[r * n + i, i * d + r] = 1.0
    return p


def _split_dot(p, x):
    hi = x.astype(BF16)
    lo = (x - hi.astype(F32)).astype(BF16)
    both = _dot(p, jnp.concatenate([hi, lo], axis=1))
    return both[:, :LANES] + both[:, LANES:]


def _pair_dot(p, a, b):
    both = _dot(p, jnp.concatenate([a, b], axis=1))
    return both[:, :LANES], both[:, LANES:]


def _tile_to_streams(y, dst, t, d):
    n, ln = TM // d, S // d
    for r in range(d):
        dst[r * ln + t * n:r * ln + (t + 1) * n, :] = y[r * n:(r + 1) * n].astype(dst.dtype)


def _tile_from_streams(src, t, d):
    n, ln = TM // d, S // d
    return jnp.concatenate([src[r * ln + t * n:r * ln + (t + 1) * n, :] for r in range(d)], axis=0)


def _head_masks():
    first = lax.broadcasted_iota(jnp.int32, (TM, LANES), 1) < HD
    return first, jnp.logical_not(first)


def _attn_fwd(q, k, v, g, after):
    d = DILATIONS[g]
    nb = S // d // QBLK
    perm = _perm_matrix(d)

    def body(q_ref, k_ref, v_ref, p_ref, pt_ref, o_ref, l_ref, ls_ref, q0, q1, ks, vs, os_):
        first, second = _head_masks()
        pm = p_ref[...]
        for t in range(NT):
            rows = slice(t * TM, (t + 1) * TM)
            if d == 1:
                qt = q_ref[rows, :].astype(F32)
            else:
                qt, kt = _pair_dot(pm, q_ref[rows, :], k_ref[rows, :])
                _tile_to_streams(kt, ks, t, d)
                if t % 2 == 0:
                    va, vb = _pair_dot(pm, v_ref[rows, :], v_ref[(t + 1) * TM:(t + 2) * TM, :])
                    _tile_to_streams(va, vs, t, d)
                    _tile_to_streams(vb, vs, t + 1, d)
            _tile_to_streams(jnp.where(first, qt, 0.0), q0, t, d)
            _tile_to_streams(jnp.where(second, qt, 0.0), q1, t, d)
        kref, vref = (k_ref, v_ref) if d == 1 else (ks, vs)
        oref, lref = (o_ref, l_ref) if d == 1 else (os_, ls_ref)

        def blk(b, carry):
            r0 = pl.multiple_of(b * QBLK, QBLK)
            p0 = pl.multiple_of(jnp.maximum(b - 1, 0) * QBLK, QBLK)
            kk = jnp.concatenate([kref[pl.ds(p0, QBLK), :], kref[pl.ds(r0, QBLK), :]], axis=0)
            vv = jnp.concatenate([vref[pl.ds(p0, QBLK), :], vref[pl.ds(r0, QBLK), :]], axis=0)
            valid = _band_mask(b & (nb - 1))
            acc, lse = [], []
            for qh in (q0, q1):
                s = jnp.where(valid, _dot_nt(qh[pl.ds(r0, QBLK), :], kk), NEG)
                m = jnp.max(s, axis=-1, keepdims=True)
                e = jnp.exp(s - m)
                l = jnp.sum(e, axis=-1, keepdims=True)
                acc.append(_dot(e.astype(BF16), vv) * (1.0 / l))
                lse.append(m + jnp.log(l))
            f = first[:QBLK]
            oref[pl.ds(r0, QBLK), :] = jnp.where(f, acc[0], acc[1])
            lref[pl.ds(r0, QBLK), :] = jnp.where(f, lse[0], lse[1])
            return carry

        lax.fori_loop(0, S // QBLK, blk, 0, unroll=FWD_UNROLL)
        if d > 1:
            ptm = pt_ref[...]
            for t in range(NT):
                rows = slice(t * TM, (t + 1) * TM)
                o_ref[rows, :] = _split_dot(ptm, _tile_from_streams(os_, t, d))
                l_ref[rows, :] = _split_dot(ptm, _tile_from_streams(ls_ref, t, d))

    qkv_spec = pl.BlockSpec((S, LANES), lambda c: (0, g * NCHUNK + c))
    out_spec = pl.BlockSpec((S, LANES), lambda c: (0, c))
    n_out = 2 if d == 1 else 3
    inner = body if d > 1 else functools.partial(_drop_arg, body, 7)
    outs = pl.pallas_call(
        functools.partial(_skip_arg, inner, 5), name=f"attn_fwd_g{g}", grid=(NCHUNK,),
        in_specs=[qkv_spec] * 3 + [_full((TM, TM))] * 2 + [pl.BlockSpec(memory_space=pl.ANY)],
        out_specs=[out_spec] * n_out, out_shape=[_sds((S, GW), F32)] * n_out,
        scratch_shapes=[pltpu.VMEM((S, LANES), BF16)] * 4 + [pltpu.VMEM((S, LANES), F32)],
        compiler_params=_params(("parallel",)),
    )(q, k, v, jnp.asarray(perm, BF16), jnp.asarray(perm.T, BF16), after)
    return (outs[0], outs[1], outs[1]) if d == 1 else tuple(outs)


def _drop_arg(body, pos, *refs):
    return body(*refs[:pos], None, *refs[pos:])


def _attn_out(os_, ls, qm, kv0, z, x, w_out):
    def body(o0, o1, o2, l0, l1, l2, qm_ref, kv_ref, z_ref, x_ref, w_ref, h_ref, ybuf):
        _, mix = _merge((o0, o1, o2), (l0, l1, l2))
        sz, _ = _silu_parts(z_ref[...])
        ybuf[:, :GW] = (mix * sz[:, :GW]).astype(BF16)
        _, mo, _ = _mem_attn(qm_ref[...], kv_ref[...])
        ybuf[:, GW:] = (mo * sz[:, GW:]).astype(BF16)
        yb = ybuf[...]
        for s in range(4):
            cs = slice(s * SH_O, (s + 1) * SH_O)
            h_ref[:, cs] = x_ref[:, cs] + _dot(yb, w_ref[s])

    return pl.pallas_call(
        body, name="attn_out", grid=(NX,),
        in_specs=[_rows(GW, MX)] * 6 + [_rows(MW, MX), _full((NM, 2 * MW)), _rows(BR_A, MX), _rows(D, MX),
                                        _full((4, BR_A, SH_O))],
        out_specs=_rows(D, MX), out_shape=_sds((S, D), F32),
        scratch_shapes=[pltpu.VMEM((MX, BR_A), BF16)],
        compiler_params=_params(("parallel",)),
    )(*os_, *ls, qm, kv0, z, x, w_out)


def _in_proj_b(h1, g1, w_in):
    def body(x_ref, g_ref, w_ref, hn_ref, bg_ref, cg_ref, u_ref, qm_ref, z_ref, proj):
        xf = x_ref[...]
        hn = xf * lax.rsqrt(jnp.mean(xf * xf, axis=-1, keepdims=True) + EPS) * g_ref[...]
        hb = hn.astype(BF16)
        hn_ref[...] = hb
        for s in range(4):
            proj[:, s * SH_B:(s + 1) * SH_B] = _dot(hb, w_ref[s])
        bg_ref[...] = proj[:, :D]
        cg_ref[...] = proj[:, D:2 * D]
        u_ref[...] = proj[:, 2 * D:3 * D]
        qm_ref[...] = proj[:, 3 * D:3 * D + MW].astype(BF16)
        z_ref[...] = proj[:, 3 * D + MW:]

    return pl.pallas_call(
        body, name="in_proj_b", grid=(NT,),
        in_specs=[_rows(D), _full((1, D)), _full((4, D, SH_B))],
        out_specs=[_rows(D), _rows(D), _rows(D), _rows(D), _rows(MW), _rows(BR_B)],
        out_shape=[_sds((S, D), BF16), _sds((S, D), F32), _sds((S, D), F32), _sds((S, D), F32),
                   _sds((S, MW), BF16), _sds((S, BR_B), F32)],
        scratch_shapes=[pltpu.VMEM((TM, IN_B), F32)],
        compiler_params=_params(("parallel",)),
    )(h1, g1, w_in)


def _prev8(width):
    return pl.BlockSpec((8, width), lambda i: (jnp.maximum(i * (MX // 8) - 1, 0), 0))


def _conv_out_loss(bg, cg, u, cw, qm, kv1, z, h1, w_out, fg, tgt):
    def body(bg_ref, cg_ref, u_ref, cgp_ref, up_ref, cw_ref, qm_ref, kv_ref, z_ref, h_ref, w_ref, fg_ref, t_ref,
             dh_ref, loss_ref, dfg_ref, ybuf):
        i = pl.program_id(0)
        a, a1, a2 = _conv_taps(cg_ref[...], u_ref[...], cgp_ref[...], up_ref[...], i == 0)
        conv = cw_ref[0:1, :] * a2 + cw_ref[1:2, :] * a1 + cw_ref[2:3, :] * a
        sz, _ = _silu_parts(z_ref[...])
        ybuf[:, :D] = (bg_ref[...] * conv * sz[:, :D]).astype(BF16)
        _, mo, _ = _mem_attn(qm_ref[...], kv_ref[...])
        ybuf[:, D:] = (mo * sz[:, D:]).astype(BF16)
        h2 = h_ref[...] + _dot(ybuf[...], w_ref[...])
        rstd = lax.rsqrt(jnp.mean(h2 * h2, axis=-1, keepdims=True) + EPS)
        n = h2 * rstd
        fgv = fg_ref[...]
        err = n * fgv - t_ref[...]
        dout = err * (1.0 / D)
        dn = dout * fgv
        dh_ref[...] = rstd * (dn - n * jnp.mean(dn * n, axis=-1, keepdims=True))

        @pl.when(i == 0)
        def _():
            loss_ref[...] = jnp.zeros_like(loss_ref)
            dfg_ref[...] = jnp.zeros_like(dfg_ref)

        loss_ref[...] += jnp.sum(err * err) * (0.5 / D)
        dfg_ref[...] += jnp.sum(dout * n, axis=0, keepdims=True)

    return pl.pallas_call(
        body, name="conv_out_loss", grid=(NX,),
        in_specs=[_rows(D, MX), _rows(D, MX), _rows(D, MX), _prev8(D), _prev8(D), _full((8, D)), _rows(MW, MX),
                  _full((NM, 2 * MW)), _rows(BR_B, MX), _rows(D, MX), _full((BR_B, D)), _full((1, D)), _rows(D, MX)],
        out_specs=[_rows(D, MX), _full((1, 128)), _full((1, D))],
        out_shape=[_sds((S, D), F32), _sds((1, 128), F32), _sds((1, D), F32)],
        scratch_shapes=[pltpu.VMEM((MX, BR_B), BF16)],
        compiler_params=_params(("arbitrary",)),
    )(bg, cg, u, cg, u, cw, qm, kv1, z, h1, w_out, fg, tgt)


def _conv_bwd(dh2, bg, cg, u, cw, qm, kv1, z, w_out):
    rev = lambda i: (NX - 1 - i, 0)
    rows = lambda w: pl.BlockSpec((MX, w), rev)
    prev8 = pl.BlockSpec((8, D), lambda i: (jnp.maximum((NX - 1 - i) * (MX // 8) - 1, 0), 0))

    def body(dh_ref, bg_ref, cg_ref, u_ref, cgp_ref, up_ref, cw_ref, qm_ref, kv_ref, z_ref, w_ref,
             dproj_ref, dw_ref, dcw_ref, dkv_ref, dwb_ref, ybuf, carry):
        i = pl.program_id(0)

        @pl.when(i == 0)
        def _():
            dw_ref[...] = jnp.zeros_like(dw_ref)
            dcw_ref[...] = jnp.zeros_like(dcw_ref)
            dkv_ref[...] = jnp.zeros_like(dkv_ref)
            carry[...] = jnp.zeros_like(carry)

        dhb = dh_ref[...].astype(BF16)
        dy = _dot_nt(dhb, w_ref[...])
        kvv = kv_ref[...]
        p, mo, q4 = _mem_attn(qm_ref[...], kvv)
        szm, dszm = _silu_parts(z_ref[:, D:])
        ybuf[:, D:] = (mo * szm).astype(BF16)
        dym = dy[:, D:]
        dproj_ref[:, 3 * D + MW + D:] = (dym * mo * dszm).astype(BF16)
        first_tile = i == NX - 1
        for c in range(D // CONV_CHUNK):
            cs = slice(c * CONV_CHUNK, (c + 1) * CONV_CHUNK)
            bgv, cgv, uv = bg_ref[:, cs], cg_ref[:, cs], u_ref[:, cs]
            a, a1, a2 = _conv_taps(cgv, uv, cgp_ref[:, cs], up_ref[:, cs], first_tile)
            w0, w1, w2 = cw_ref[0:1, cs], cw_ref[1:2, cs], cw_ref[2:3, cs]
            conv = w0 * a2 + w1 * a1 + w2 * a
            mix = bgv * conv
            sz, dsz = _silu_parts(z_ref[:, cs])
            ybuf[:, cs] = (mix * sz).astype(BF16)
            dyc = dy[:, cs]
            dproj_ref[:, 3 * D + MW + c * CONV_CHUNK:3 * D + MW + (c + 1) * CONV_CHUNK] = (
                dyc * mix * dsz).astype(BF16)
            dmix = dyc * sz
            dproj_ref[:, cs] = (dmix * conv).astype(BF16)
            dc = dmix * bgv
            nxt = carry[:, cs]
            row = lax.broadcasted_iota(jnp.int32, dc.shape, 0)
            dc1 = jnp.where(row == MX - 1, nxt[0:1, :], pltpu.roll(dc, MX - 1, 0))
            dc2 = jnp.where(row == MX - 2, nxt[0:1, :],
                            jnp.where(row == MX - 1, nxt[1:2, :], pltpu.roll(dc, MX - 2, 0)))
            carry[:, cs] = dc[0:8, :]
            da = w2 * dc + w1 * dc1 + w0 * dc2
            dproj_ref[:, D + c * CONV_CHUNK:D + (c + 1) * CONV_CHUNK] = (da * uv).astype(BF16)
            dproj_ref[:, 2 * D + c * CONV_CHUNK:2 * D + (c + 1) * CONV_CHUNK] = (da * cgv).astype(BF16)
            dcw_ref[0:1, cs] += jnp.sum(dc * a2, axis=0, keepdims=True)
            dcw_ref[1:2, cs] += jnp.sum(dc * a1, axis=0, keepdims=True)
            dcw_ref[2:3, cs] += jnp.sum(dc * a, axis=0, keepdims=True)
        dw_ref[...] += _dot_tn(ybuf[...], dhb)
        dproj_ref[:, 3 * D:3 * D + MW] = _mem_attn_bwd(dym * szm, p, mo, q4, kvv, dkv_ref).astype(BF16)

        @pl.when(i == NX - 1)
        def _():
            dwb_ref[...] = dw_ref[...].astype(BF16)

    return pl.pallas_call(
        body, name="conv_bwd", grid=(NX,),
        in_specs=[rows(D), rows(D), rows(D), rows(D), prev8, prev8, _full((8, D)), rows(MW),
                  _full((NM, 2 * MW)), rows(BR_B), _full((BR_B, D))],
        out_specs=[rows(IN_B), _full((BR_B, D)), _full((8, D)), _full((NM, 2 * MW)), _full((BR_B, D))],
        out_shape=[_sds((S, IN_B), BF16), _sds((BR_B, D), F32), _sds((8, D), F32), _sds((NM, 2 * MW), F32),
                   _sds((BR_B, D), BF16)],
        scratch_shapes=[pltpu.VMEM((MX, BR_B), BF16), pltpu.VMEM((8, D), F32)],
        compiler_params=_params(("arbitrary",)),
    )(dh2, bg, cg, u, cg, u, cw, qm, kv1, z, w_out)


def _in_proj_bwd(dproj, w_in, xin, g, dres, after, width, name):
    sh = width // 4

    def body(dp_ref, w_ref, x_ref, g_ref, dr_ref, dx_ref, dg_ref):
        i = pl.program_id(0)
        dhn = _dot_nt(dp_ref[:, 0:sh], w_ref[0])
        for s in range(1, 4):
            dhn += _dot_nt(dp_ref[:, s * sh:(s + 1) * sh], w_ref[s])
        xf = x_ref[...]
        rstd = lax.rsqrt(jnp.mean(xf * xf, axis=-1, keepdims=True) + EPS)
        n = xf * rstd
        dn = dhn * g_ref[...]
        dx_ref[...] = dr_ref[...] + rstd * (dn - n * jnp.mean(dn * n, axis=-1, keepdims=True))

        @pl.when(i == 0)
        def _():
            dg_ref[...] = jnp.zeros_like(dg_ref)

        dg_ref[...] += jnp.sum(dhn * n, axis=0, keepdims=True)

    return pl.pallas_call(
        functools.partial(_skip_arg, body, 5), name=name, grid=(NT,),
        in_specs=[_rows(width), _full((4, D, sh)), _rows(D), _full((1, D)), _rows(D), pl.BlockSpec(memory_space=pl.ANY)],
        out_specs=[_rows(D), _full((1, D))],
        out_shape=[_sds((S, D), F32), _sds((1, D), F32)],
        compiler_params=_params(("arbitrary",)),
    )(dproj, w_in, xin, g, dres, after)


def _w_in_grad(hn, dproj, width, name):
    sh = width // 4

    def body(hn_ref, dp_ref, dw_ref, dwb_ref):
        dw = _dot_tn(hn_ref[...], dp_ref[...])
        dw_ref[0] = dw
        dwb_ref[0] = dw.astype(BF16)

    spec = pl.BlockSpec((1, D, sh), lambda s: (s, 0, 0))
    return pl.pallas_call(
        body, name=name, grid=(4,),
        in_specs=[_full((S, D)), pl.BlockSpec((S, sh), lambda s: (0, s))],
        out_specs=[spec, spec], out_shape=[_sds((4, D, sh), F32), _sds((4, D, sh), BF16)],
        compiler_params=_params(("parallel",)),
    )(hn, dproj)


def _attn_out_bwd(dh1, os_, ls, qm, kv0, z, w_out, after):
    ones_bd = np.kron(np.eye(GW // HD, dtype=np.float32), np.ones((HD, HD), np.float32))

    def body(dh_ref, o0, o1, o2, l0, l1, l2, qm_ref, kv_ref, z_ref, w_ref, bd_ref,
             do0, do1, do2, dd0, dd1, dd2, dqm_ref, dz_ref, dw_ref, dkv_ref, dwb_ref, ybuf):
        i = pl.program_id(0)

        @pl.when(i == 0)
        def _():
            dw_ref[...] = jnp.zeros_like(dw_ref)
            dkv_ref[...] = jnp.zeros_like(dkv_ref)

        ws, mix = _merge((o0, o1, o2), (l0, l1, l2))
        sz, dsz = _silu_parts(z_ref[...])
        kvv = kv_ref[...]
        p, mo, q4 = _mem_attn(qm_ref[...], kvv)
        ybuf[:, :GW] = (mix * sz[:, :GW]).astype(BF16)
        ybuf[:, GW:] = (mo * sz[:, GW:]).astype(BF16)
        yb = ybuf[...]
        dh = dh_ref[...]
        dy = None
        for s in range(4):
            dhb = dh[:, s * SH_O:(s + 1) * SH_O].astype(BF16)
            dw_ref[s] += _dot_tn(yb, dhb)
            part = _dot_nt(dhb, w_ref[s])
            dy = part if dy is None else dy + part
        dcat = dy * sz
        dz_ref[:, :GW] = (dy[:, :GW] * mix * dsz[:, :GW]).astype(BF16)
        dz_ref[:, GW:] = (dy[:, GW:] * mo * dsz[:, GW:]).astype(BF16)
        dmix = dcat[:, :GW]
        prod = dmix * mix
        hi = prod.astype(BF16)
        lo = (prod - hi.astype(F32)).astype(BF16)
        bd = bd_ref[...]
        tot = _dot(hi, bd) + _dot(lo, bd)
        for w, do_ref, dd_ref in zip(ws, (do0, do1, do2), (dd0, dd1, dd2)):
            do_ref[...] = (w * dmix).astype(BF16)
            dd_ref[...] = w * tot

        dqm_ref[...] = _mem_attn_bwd(dcat[:, GW:], p, mo, q4, kvv, dkv_ref).astype(BF16)

        @pl.when(i == NX - 1)
        def _():
            dwb_ref[...] = dw_ref[...].astype(BF16)

    return pl.pallas_call(
        functools.partial(_skip_arg, body, 12), name="attn_out_bwd", grid=(NX,),
        in_specs=[_rows(D, MX)] + [_rows(GW, MX)] * 6 + [_rows(MW, MX), _full((NM, 2 * MW)), _rows(BR_A, MX),
                                                           _full((4, BR_A, SH_O)), _full((GW, GW)),
                                                           pl.BlockSpec(memory_space=pl.ANY)],
        out_specs=[_rows(GW, MX)] * 6 + [_rows(MW, MX), _rows(BR_A, MX), _full((4, BR_A, SH_O)),
                                         _full((NM, 2 * MW)), _full((4, BR_A, SH_O))],
        out_shape=[_sds((S, GW), BF16)] * 3 + [_sds((S, GW), F32)] * 3 + [
            _sds((S, MW), BF16), _sds((S, BR_A), BF16), _sds((4, BR_A, SH_O), F32), _sds((NM, 2 * MW), F32),
            _sds((4, BR_A, SH_O), BF16)],
        scratch_shapes=[pltpu.VMEM((MX, BR_A), BF16)],
        compiler_params=_params(("arbitrary",)),
    )(dh1, *os_, *ls, qm, kv0, z, w_out, jnp.asarray(ones_bd, dtype=BF16), after)


def _attn_bwd(q, k, v, do, lse_s, dd, g):
    d = DILATIONS[g]
    nb = S // d // QBLK
    perm = _perm_matrix(d)

    def body(q_ref, k_ref, v_ref, do_ref, l_ref, dd_ref, p_ref, pt_ref, dq_ref, dk_ref, dv_ref,
             q0, q1, g0, g1, ks, vs, dds, dqs, dks, dvs):
        first, second = _head_masks()
        pm = p_ref[...]
        for t in range(NT):
            rows = slice(t * TM, (t + 1) * TM)
            if d == 1:
                qt = q_ref[rows, :].astype(F32)
                gt = do_ref[rows, :].astype(F32)
            else:
                qt, gt = _pair_dot(pm, q_ref[rows, :], do_ref[rows, :])
                kt, vt = _pair_dot(pm, k_ref[rows, :], v_ref[rows, :])
                _tile_to_streams(kt, ks, t, d)
                _tile_to_streams(vt, vs, t, d)
                _tile_to_streams(_split_dot(pm, dd_ref[rows, :]), dds, t, d)
            _tile_to_streams(jnp.where(first, qt, 0.0), q0, t, d)
            _tile_to_streams(jnp.where(second, qt, 0.0), q1, t, d)
            _tile_to_streams(jnp.where(first, gt, 0.0), g0, t, d)
            _tile_to_streams(jnp.where(second, gt, 0.0), g1, t, d)
        kref, vref, ddref = (k_ref, v_ref, dd_ref) if d == 1 else (ks, vs, dds)
        dqref, dkref, dvref = dqs, dks, dvs
        dkref[...] = jnp.zeros_like(dkref)
        dvref[...] = jnp.zeros_like(dvref)

        def blk(b, carry):
            r0 = pl.multiple_of(b * QBLK, QBLK)
            p0 = pl.multiple_of(jnp.maximum(b - 1, 0) * QBLK, QBLK)
            kk = jnp.concatenate([kref[pl.ds(p0, QBLK), :], kref[pl.ds(r0, QBLK), :]], axis=0)
            vv = jnp.concatenate([vref[pl.ds(p0, QBLK), :], vref[pl.ds(r0, QBLK), :]], axis=0)
            lb = l_ref[pl.ds(r0, QBLK), :]
            ddb = ddref[pl.ds(r0, QBLK), :]
            lcol = jnp.concatenate([lb[:, 0:1], lb[:, HD:HD + 1]], axis=0)
            dcol = jnp.concatenate([ddb[:, 0:1], ddb[:, HD:HD + 1]], axis=0)
            valid = _band_mask(b & (nb - 1))
            valid2 = jnp.concatenate([valid, valid], axis=0)
            qq = jnp.concatenate([q0[pl.ds(r0, QBLK), :], q1[pl.ds(r0, QBLK), :]], axis=0)
            gg = jnp.concatenate([g0[pl.ds(r0, QBLK), :], g1[pl.ds(r0, QBLK), :]], axis=0)
            p = jnp.where(valid2, jnp.exp(_dot_nt(qq, kk) - lcol), 0.0)
            ds = (p * (_dot_nt(gg, vv) - dcol)).astype(BF16)
            dq2 = _dot(ds, kk)
            dqref[pl.ds(r0, QBLK), :] = jnp.where(first[:QBLK], dq2[:QBLK], dq2[QBLK:])
            dkk = _dot_tn(ds, qq)
            dvv = _dot_tn(p.astype(BF16), gg)
            dkref[pl.ds(p0, QBLK), :] += dkk[:QBLK]
            dkref[pl.ds(r0, QBLK), :] += dkk[QBLK:]
            dvref[pl.ds(p0, QBLK), :] += dvv[:QBLK]
            dvref[pl.ds(r0, QBLK), :] += dvv[QBLK:]
            return carry

        lax.fori_loop(0, S // QBLK, blk, 0, unroll=BWD_UNROLL)

        ptm = pt_ref[...] if d > 1 else None
        for t in range(NT):
            rows = slice(t * TM, (t + 1) * TM)
            if d == 1:
                dq_ref[rows, :] = dqs[rows, :].astype(BF16)
                dk_ref[rows, :] = dks[rows, :].astype(BF16)
                dv_ref[rows, :] = dvs[rows, :].astype(BF16)
            else:
                tq, tk = _pair_dot(ptm, _tile_from_streams(dqs, t, d).astype(BF16),
                                   _tile_from_streams(dks, t, d).astype(BF16))
                dq_ref[rows, :] = tq.astype(BF16)
                dk_ref[rows, :] = tk.astype(BF16)
                if t % 2 == 0:
                    ta, tb = _pair_dot(ptm, _tile_from_streams(dvs, t, d).astype(BF16),
                                       _tile_from_streams(dvs, t + 1, d).astype(BF16))
                    dv_ref[rows, :] = ta.astype(BF16)
                    dv_ref[(t + 1) * TM:(t + 2) * TM, :] = tb.astype(BF16)

    qkv_spec = pl.BlockSpec((S, LANES), lambda c: (0, g * NCHUNK + c))
    one_spec = pl.BlockSpec((S, LANES), lambda c: (0, c))
    return pl.pallas_call(
        body, name=f"attn_bwd_g{g}", grid=(NCHUNK,),
        in_specs=[qkv_spec] * 3 + [one_spec] * 3 + [_full((TM, TM))] * 2, out_specs=[one_spec] * 3,
        out_shape=[_sds((S, GW), BF16)] * 3,
        scratch_shapes=[pltpu.VMEM((S, LANES), BF16)] * 6 + [pltpu.VMEM((S, LANES), F32)] * 4,
        compiler_params=_params(("parallel",)),
    )(q, k, v, do, lse_s, dd, jnp.asarray(perm, BF16), jnp.asarray(perm.T, BF16))


def _qkv_bwd(dqs, dks, dvs, dqm, dz, c, s1, s2):
    def body(q0, q1, q2, k0, k1, k2, v0, v1, v2, dqm_ref, dz_ref, c_ref, s1_ref, s2_ref, dp_ref):
        cc, a1, a2 = c_ref[...], s1_ref[...], s2_ref[...]
        for g, (qr, kr, vr) in enumerate(((q0, k0, v0), (q1, k1, v1), (q2, k2, v2))):
            for j in range(GW // 128):
                ls_ = slice(j * 128, (j + 1) * 128)
                c0 = g * GW + j * 128
                dp_ref[:, c0:c0 + 128] = (_rope_bwd(qr[:, ls_].astype(F32), cc, a1, a2) * SCALE).astype(BF16)
                dp_ref[:, NQ + c0:NQ + c0 + 128] = _rope_bwd(kr[:, ls_].astype(F32), cc, a1, a2).astype(BF16)
            dp_ref[:, 2 * NQ + g * GW:2 * NQ + (g + 1) * GW] = vr[...]
        dp_ref[:, 3 * NQ:3 * NQ + MW] = dqm_ref[...]
        dp_ref[:, 3 * NQ + MW:] = dz_ref[...]

    return pl.pallas_call(
        body, name="qkv_bwd", grid=(NT,),
        in_specs=[_rows(GW)] * 9 + [_rows(MW), _rows(BR_A), _rows(128), _rows(128), _rows(128)],
        out_specs=_rows(IN_A), out_shape=_sds((S, IN_A), BF16),
        compiler_params=_params(("parallel",)),
    )(*dqs, *dks, *dvs, dqm, dz, c, s1, s2)


def _mem_bwd(mem, mg, memn, wkv, dkv0, dkv1):
    def body(mem_ref, mg_ref, memn_ref, w_ref, d0_ref, d1_ref, dw_ref, dwb_ref, dg_ref):
        mf = mem_ref[...]
        n = mf * lax.rsqrt(jnp.mean(mf * mf, axis=-1, keepdims=True) + EPS)
        for i, d_ref in enumerate((d0_ref, d1_ref)):
            dkv = d_ref[...].astype(BF16)
            mn = memn_ref[i]
            for s in range(4):
                cs = slice(s * NM, (s + 1) * NM)
                dw = _dot_tn(mn[:, cs], dkv)
                dw_ref[s, i] = dw
                dwb_ref[s, i] = dw.astype(BF16)
                dmn = _dot_nt(dkv, w_ref[s, i])
                dg_ref[i:i + 1, cs] = jnp.sum(dmn * n[:, cs], axis=0, keepdims=True)

    return pl.pallas_call(
        body, name="mem_bwd", grid=(1,),
        in_specs=[_full((NM, D)), _full((2, D)), _full((2, NM, D)), _full((4, 2, NM, 2 * MW)),
                  _full((NM, 2 * MW)), _full((NM, 2 * MW))],
        out_specs=[_full((4, 2, NM, 2 * MW)), _full((4, 2, NM, 2 * MW)), _full((2, D))],
        out_shape=[_sds((4, 2, NM, 2 * MW), F32), _sds((4, 2, NM, 2 * MW), BF16), _sds((2, D), F32)],
        compiler_params=_params(("arbitrary",)),
    )(mem, mg, memn, wkv, dkv0, dkv1)


MESH = pl.DeviceIdType.MESH
ANY = pl.BlockSpec(memory_space=pl.ANY)
BIG = (("wkv", 2, NM, 2 * MW), ("w_in_a", 1, D, SH_A), ("w_out_a", 1, BR_A, SH_O),
       ("w_in_b", 1, D, SH_B), ("w_out_b", 1, BR_B // 4, D))
NBIG = len(BIG)
CW_ROWS = 8


def _place():
    x, y, c = lax.axis_index("x"), lax.axis_index("y"), lax.axis_index("c")
    chips = ((1 - x, y), (x, 1 - y), (1 - x, 1 - y))
    return x, y, c, chips


def _remote(src, dst, ssem, rsem, dev):
    return pltpu.make_async_remote_copy(src_ref=src, dst_ref=dst, send_sem=ssem, recv_sem=rsem,
                                        device_id=dev, device_id_type=MESH)


def _cast_weights(place, ws, after, idx, name):
    nblk = 4
    n = len(idx)
    dims = [BIG[w][1:] for w in idx]

    def body(pref, *refs):
        for i in range(n):
            refs[n + 1 + i][0] = refs[i][...].astype(BF16)

    grid_spec = pltpu.PrefetchScalarGridSpec(
        num_scalar_prefetch=1, grid=(nblk,),
        in_specs=[pl.BlockSpec((k, r // nblk, cdim), lambda i, pref: (0, i, 0)) for k, r, cdim in dims]
        + [pl.BlockSpec(memory_space=pl.ANY)],
        out_specs=[pl.BlockSpec((1, k, r // nblk, cdim), lambda i, pref: (pref[1], 0, i, 0)) for k, r, cdim in dims])
    return pl.pallas_call(
        body, name=name, grid_spec=grid_spec,
        out_shape=[_sds((4, k, r, cdim), BF16) for k, r, cdim in dims],
        compiler_params=_params(("parallel",)),
    )(place, *ws, after)


LAYER_A = (0, 1, 2)
LAYER_B = (3, 4)
HBM = pl.BlockSpec(memory_space=pltpu.HBM)
SEM = pl.BlockSpec(memory_space=pltpu.SEMAPHORE)
EFFECT = pltpu.SideEffectType.DATAFLOW_SIDE_EFFECTING
TOKEN = (8, 128)


def _half(ref, w, which):
    h = BIG[w][2] // 2
    return ref.at[:, pl.ds(which * h, h), :]


def _skip_arg(body, pos, *refs):
    return body(*refs[:pos], *refs[pos + 1:])


def _gather_start(wb, after, idx, name, barrier_id):
    n = len(idx)

    def body(*refs):
        src = refs[:n]
        send_sems, recv_sems = refs[n + 1], refs[n + 2]
        token = refs[2 * n + 3]
        x, y, c, chips = _place()
        _peer_barrier([(px, py, c) for px, py in chips])
        me = 2 * x + y
        for i in range(n):
            for j, (px, py) in enumerate(chips):
                mine = _half(src[i].at[me], idx[i], c)
                _remote(mine, mine, send_sems.at[j * n + i], recv_sems.at[j * n + i], (px, py, c)).start()
        token[...] = jnp.zeros(TOKEN, F32)

    outs = pl.pallas_call(
        body, name=name, in_specs=[HBM] * n + [ANY],
        out_specs=(SEM, SEM) + (HBM,) * n + (pl.BlockSpec(memory_space=pltpu.VMEM),),
        out_shape=(pltpu.SemaphoreType.DMA((3 * n,)), pltpu.SemaphoreType.DMA((3 * n,)))
        + tuple(pltpu.HBM(w.shape, w.dtype) for w in wb) + (_sds(TOKEN, F32),),
        input_output_aliases={i: 2 + i for i in range(n)},
        compiler_params=pltpu.CompilerParams(has_side_effects=EFFECT, collective_id=barrier_id),
    )(*[pltpu.with_memory_space_constraint(w, pltpu.HBM) for w in wb], after)
    return outs[0], outs[1], list(outs[2:2 + n]), outs[2 + n]


def _gather_wait(send_sems, recv_sems, wb, after, idx, name, started=None):
    n = len(idx)
    started = idx if started is None else started
    n_all = len(started)
    pos = [started.index(w) for w in idx]

    def body(*refs):
        buf = refs[:n]
        send_sems, recv_sems = refs[n], refs[n + 1]
        x, y, c, chips = _place()
        me = 2 * x + y
        for j, (px, py) in enumerate(chips):
            for i in range(n):
                mine = _half(buf[i].at[me], idx[i], c)
                got = _half(buf[i].at[2 * px + py], idx[i], c)
                k = j * n_all + pos[i]
                _remote(mine, mine, send_sems.at[k], recv_sems.at[k], (px, py, c)).wait_send()
                _remote(got, got, send_sems.at[k], recv_sems.at[k], (px, py, c)).wait_recv()

    outs = pl.pallas_call(
        body, name=name, in_specs=[HBM] * n + [SEM, SEM] + [ANY] * len(after), out_specs=(HBM,) * n,
        out_shape=tuple(pltpu.HBM(w.shape, w.dtype) for w in wb),
        input_output_aliases={i: i for i in range(n)},
        compiler_params=pltpu.CompilerParams(has_side_effects=EFFECT),
    )(*wb, send_sems, recv_sems, *after)
    return list(outs)


def _gather_forward(wb, idx, name, barrier_id):
    n = len(idx)

    def body(*refs):
        dst = refs[n:2 * n]
        send_sems, recv_sems = refs[2 * n], refs[2 * n + 1]
        x, y, c, chips = _place()
        _sibling_barrier(x, y, c)
        cps = []
        for j, (px, py) in enumerate(chips):
            for i in range(n):
                got = _half(dst[i].at[2 * px + py], idx[i], c)
                cps.append(_remote(got, got, send_sems.at[j, i], recv_sems.at[j, i], (x, y, 1 - c)))
                cps[-1].start()
        for j, (px, py) in enumerate(chips):
            for i in range(n):
                got = _half(dst[i].at[2 * px + py], idx[i], 1 - c)
                _remote(got, got, send_sems.at[j, i], recv_sems.at[j, i], (x, y, 1 - c)).wait_recv()
        for cp in cps:
            cp.wait_send()

    return pl.pallas_call(
        body, name=name, in_specs=[ANY] * n, out_specs=[ANY] * n, out_shape=[_sds(w.shape, BF16) for w in wb],
        input_output_aliases={i: i for i in range(n)},
        scratch_shapes=[pltpu.SemaphoreType.DMA((3, n)), pltpu.SemaphoreType.DMA((3, n))],
        compiler_params=pltpu.CompilerParams(collective_id=barrier_id),
    )(*wb)


def _forward_start(wb, cw, after, idx, name, barrier_id):
    n = len(idx)
    m = n if cw is None else n + 2

    def body(*refs):
        buf = refs[:n]
        send_sems, recv_sems = refs[m + 1], refs[m + 2]
        token = refs[2 * m + 3]
        x, y, c, chips = _place()
        _peer_barrier([(x, y, 1 - c)] + ([] if cw is None else [(px, py, c) for px, py in chips]))
        for j, (px, py) in enumerate(chips):
            for i in range(n):
                got = _half(buf[i].at[2 * px + py], idx[i], c)
                _remote(got, got, send_sems.at[j * (n + 1) + i], recv_sems.at[j * (n + 1) + i], (x, y, 1 - c)).start()
            if cw is not None:
                _remote(refs[n], refs[n + 1].at[2 * x + y], send_sems.at[j * (n + 1) + n],
                        recv_sems.at[j * (n + 1) + n], (px, py, c)).start()
        token[...] = jnp.zeros(TOKEN, F32)

    arrays = list(wb) if cw is None else list(wb) + [cw, lax.empty((4, CW_ROWS, SH_O), F32)]
    outs = pl.pallas_call(
        body, name=name, in_specs=[HBM] * m + [ANY],
        out_specs=(SEM, SEM) + (HBM,) * m + (pl.BlockSpec(memory_space=pltpu.VMEM),),
        out_shape=(pltpu.SemaphoreType.DMA((3 * (n + 1),)), pltpu.SemaphoreType.DMA((3 * (n + 1),)))
        + tuple(pltpu.HBM(a.shape, a.dtype) for a in arrays) + (_sds(TOKEN, F32),),
        input_output_aliases={i: 2 + i for i in range(m)},
        compiler_params=pltpu.CompilerParams(has_side_effects=EFFECT, collective_id=barrier_id),
    )(*[pltpu.with_memory_space_constraint(a, pltpu.HBM) for a in arrays], after)
    return outs[0], outs[1], list(outs[2:2 + m]), outs[2 + m]


def _forward_wait(send_sems, recv_sems, arrays, after, idx, with_cw, name):
    n = len(idx)
    m = len(arrays)

    def body(*refs):
        buf = refs[:n]
        send_sems, recv_sems = refs[m], refs[m + 1]
        x, y, c, chips = _place()
        for j, (px, py) in enumerate(chips):
            for i in range(n):
                sent = _half(buf[i].at[2 * px + py], idx[i], c)
                got = _half(buf[i].at[2 * px + py], idx[i], 1 - c)
                k = j * (n + 1) + i
                _remote(sent, sent, send_sems.at[k], recv_sems.at[k], (x, y, 1 - c)).wait_send()
                _remote(got, got, send_sems.at[k], recv_sems.at[k], (x, y, 1 - c)).wait_recv()
            if with_cw:
                k = j * (n + 1) + n
                theirs = refs[n + 1].at[2 * px + py]
                _remote(refs[n], theirs, send_sems.at[k], recv_sems.at[k], (px, py, c)).wait_send()
                _remote(refs[n], theirs, send_sems.at[k], recv_sems.at[k], (px, py, c)).wait_recv()

    outs = pl.pallas_call(
        body, name=name, in_specs=[HBM] * m + [SEM, SEM] + [ANY] * len(after), out_specs=(HBM,) * m,
        out_shape=tuple(pltpu.HBM(a.shape, a.dtype) for a in arrays),
        input_output_aliases={i: i for i in range(m)},
        compiler_params=pltpu.CompilerParams(has_side_effects=EFFECT),
    )(*arrays, send_sems, recv_sems, *after)
    return list(outs)


def _peer_barrier(peers):
    barrier = pltpu.get_barrier_semaphore()
    for peer in peers:
        pl.semaphore_signal(barrier, inc=1, device_id=peer, device_id_type=MESH)
    pl.semaphore_wait(barrier, len(peers))


def _sibling_barrier(x, y, c):
    _peer_barrier([(x, y, 1 - c)])


def _pair_exchange(gs, idx, name, barrier_id):
    n = len(idx)

    def body(*refs):
        src, dst = refs[:n], refs[n:2 * n]
        send_sems, recv_sems = refs[2 * n:]
        x, y, c, _ = _place()
        _sibling_barrier(x, y, c)
        cps = []
        for i in range(n):
            h = BIG[idx[i]][2] // 2
            cps.append(_remote(src[i].at[:, :, pl.ds((1 - c) * h, h), :], dst[i], send_sems.at[i], recv_sems.at[i],
                               (x, y, 1 - c)))
            cps[-1].start()
        for cp in cps:
            cp.wait()

    return pl.pallas_call(
        body, name=name, in_specs=[ANY] * n, out_specs=[ANY] * n,
        out_shape=[_sds((4, BIG[w][1], BIG[w][2] // 2, BIG[w][3]), BF16) for w in idx],
        scratch_shapes=[pltpu.SemaphoreType.DMA((n,)), pltpu.SemaphoreType.DMA((n,))],
        compiler_params=pltpu.CompilerParams(collective_id=barrier_id),
    )(*gs)


def _pair_start(gs, idx, name, barrier_id):
    n = len(idx)

    def body(*refs):
        src, land = refs[:n], refs[n:2 * n]
        send_sems, recv_sems = refs[2 * n], refs[2 * n + 1]
        token = refs[4 * n + 2]
        x, y, c, _ = _place()
        _sibling_barrier(x, y, c)
        for i in range(n):
            h = BIG[idx[i]][2] // 2
            _remote(src[i].at[:, :, pl.ds((1 - c) * h, h), :], land[i], send_sems.at[i], recv_sems.at[i],
                    (x, y, 1 - c)).start()
        token[...] = jnp.zeros(TOKEN, F32)

    lands = [lax.empty((4, BIG[w][1], BIG[w][2] // 2, BIG[w][3]), BF16) for w in idx]
    arrays = list(gs) + lands
    outs = pl.pallas_call(
        body, name=name, in_specs=[HBM] * (2 * n),
        out_specs=(SEM, SEM) + (HBM,) * (2 * n) + (pl.BlockSpec(memory_space=pltpu.VMEM),),
        out_shape=(pltpu.SemaphoreType.DMA((n,)), pltpu.SemaphoreType.DMA((n,)))
        + tuple(pltpu.HBM(a.shape, a.dtype) for a in arrays) + (_sds(TOKEN, F32),),
        input_output_aliases={i: 2 + i for i in range(2 * n)},
        compiler_params=pltpu.CompilerParams(has_side_effects=EFFECT, collective_id=barrier_id),
    )(*[pltpu.with_memory_space_constraint(a, pltpu.HBM) for a in arrays])
    return outs[0], outs[1], list(outs[2:2 + n]), list(outs[2 + n:2 + 2 * n]), outs[2 + 2 * n]


def _pair_wait(send_sems, recv_sems, gs, lands, after, idx, name):
    n = len(idx)

    def body(*refs):
        src, land = refs[:n], refs[n:2 * n]
        send_sems, recv_sems = refs[2 * n], refs[2 * n + 1]
        x, y, c, _ = _place()
        for i in range(n):
            h = BIG[idx[i]][2] // 2
            cp = _remote(src[i].at[:, :, pl.ds((1 - c) * h, h), :], land[i], send_sems.at[i], recv_sems.at[i],
                         (x, y, 1 - c))
            cp.wait_send()
            cp.wait_recv()

    arrays = list(gs) + list(lands)
    outs = pl.pallas_call(
        body, name=name, in_specs=[HBM] * (2 * n) + [SEM, SEM] + [ANY] * len(after), out_specs=(HBM,) * (2 * n),
        out_shape=tuple(pltpu.HBM(a.shape, a.dtype) for a in arrays),
        input_output_aliases={i: i for i in range(2 * n)},
        compiler_params=pltpu.CompilerParams(has_side_effects=EFFECT),
    )(*arrays, send_sems, recv_sems, *after)
    return list(outs[:n]), list(outs[n:])


def _pair_sums(place, gs, r1s, idx, name):
    n = len(idx)
    dims = [(BIG[w][1], BIG[w][2] // 2, BIG[w][3]) for w in idx]

    def body(pref, *refs):
        for i in range(n):
            refs[2 * n + i][...] = (refs[i][...] + refs[n + i][...].astype(F32)).astype(BF16)

    mine = [pl.BlockSpec((1, k, h, cdim), lambda s, pref: (s, 0, pref[0], 0)) for k, h, cdim in dims]
    whole = [pl.BlockSpec((1, k, h, cdim), lambda s, pref: (s, 0, 0, 0)) for k, h, cdim in dims]
    grid_spec = pltpu.PrefetchScalarGridSpec(num_scalar_prefetch=1, grid=(4,), in_specs=mine + whole, out_specs=whole)
    return pl.pallas_call(
        body, name=name, grid_spec=grid_spec, out_shape=[_sds((4, k, h, cdim), BF16) for k, h, cdim in dims],
        compiler_params=_params(("parallel",)),
    )(place, *gs, *r1s)


def _chip_start(ps, idx, name, barrier_id):
    n = len(idx)

    def body(*refs):
        src, land = refs[:n], refs[n:2 * n]
        send_sems, recv_sems = refs[2 * n], refs[2 * n + 1]
        token = refs[4 * n + 2]
        x, y, c, chips = _place()
        _peer_barrier([(px, py, c) for px, py in chips])
        for j, (px, py) in enumerate(chips):
            for i in range(n):
                _remote(src[i].at[2 * px + py], land[i].at[j], send_sems.at[j * n + i], recv_sems.at[j * n + i],
                        (px, py, c)).start()
        token[...] = jnp.zeros(TOKEN, F32)

    lands = [lax.empty((3,) + p.shape[1:], BF16) for p in ps]
    outs = pl.pallas_call(
        body, name=name, in_specs=[HBM] * (2 * n),
        out_specs=(SEM, SEM) + (HBM,) * (2 * n) + (pl.BlockSpec(memory_space=pltpu.VMEM),),
        out_shape=(pltpu.SemaphoreType.DMA((3 * n,)), pltpu.SemaphoreType.DMA((3 * n,)))
        + tuple(pltpu.HBM(a.shape, a.dtype) for a in list(ps) + lands) + (_sds(TOKEN, F32),),
        input_output_aliases={i: 2 + i for i in range(2 * n)},
        compiler_params=pltpu.CompilerParams(has_side_effects=EFFECT, collective_id=barrier_id),
    )(*[pltpu.with_memory_space_constraint(a, pltpu.HBM) for a in list(ps) + lands])
    return outs[0], outs[1], list(outs[2:2 + n]), list(outs[2 + n:2 + 2 * n]), outs[2 + 2 * n]


def _chip_wait(send_sems, recv_sems, ps, lands, after, idx, name):
    n = len(idx)

    def body(*refs):
        src, land = refs[:n], refs[n:2 * n]
        send_sems, recv_sems = refs[2 * n], refs[2 * n + 1]
        x, y, c, chips = _place()
        for j, (px, py) in enumerate(chips):
            for i in range(n):
                cp = _remote(src[i].at[2 * px + py], land[i].at[j], send_sems.at[j * n + i], recv_sems.at[j * n + i],
                             (px, py, c))
                cp.wait_send()
                cp.wait_recv()

    arrays = list(ps) + list(lands)
    outs = pl.pallas_call(
        body, name=name, in_specs=[HBM] * (2 * n) + [SEM, SEM] + [ANY] * len(after), out_specs=(HBM,) * (2 * n),
        out_shape=tuple(pltpu.HBM(a.shape, a.dtype) for a in arrays),
        input_output_aliases={i: i for i in range(2 * n)},
        compiler_params=pltpu.CompilerParams(has_side_effects=EFFECT),
    )(*arrays, send_sems, recv_sems, *after)
    return list(outs[n:])


def _chip_sums(place, gs, r1s, r2s, idx, name):
    n = len(idx)
    dims = [(BIG[w][1], BIG[w][2] // 4, BIG[w][3]) for w in idx]

    def body(pref, *refs):
        for i in range(n):
            acc = refs[i][0] + refs[n + i][0].astype(F32)
            for j in range(3):
                acc = acc + refs[2 * n + i][j].astype(F32)
            refs[3 * n + i][...] = acc

    in_specs = ([pl.BlockSpec((1, k, q, cdim), lambda t, pref: (pref[1], 0, pref[0] * 2 + t, 0)) for k, q, cdim in dims]
                + [pl.BlockSpec((1, k, q, cdim), lambda t, pref: (pref[1], 0, t, 0)) for k, q, cdim in dims]
                + [pl.BlockSpec((3, k, q, cdim), lambda t, pref: (0, 0, t, 0)) for k, q, cdim in dims])
    out_specs = [pl.BlockSpec((k, q, cdim), lambda t, pref: (0, pref[0] * 2 + t, 0)) for k, q, cdim in dims]
    grid_spec = pltpu.PrefetchScalarGridSpec(num_scalar_prefetch=1, grid=(2,), in_specs=in_specs, out_specs=out_specs)
    return pl.pallas_call(
        body, name=name, grid_spec=grid_spec, out_shape=[_sds(BIG[w][1:], F32) for w in idx],
        compiler_params=_params(("parallel",)),
    )(place, *gs, *r1s, *r2s)


def _pair_gather(hs, idx, name, barrier_id):
    n = len(idx)

    def body(*refs):
        dst = refs[n:2 * n]
        send_sems, recv_sems = refs[2 * n:]
        x, y, c, _ = _place()
        _sibling_barrier(x, y, c)
        cps = []
        for i in range(n):
            mine = _half(dst[i], idx[i], c)
            cps.append(_remote(mine, mine, send_sems.at[i], recv_sems.at[i], (x, y, 1 - c)))
            cps[-1].start()
        for i in range(n):
            theirs = _half(dst[i], idx[i], 1 - c)
            _remote(theirs, theirs, send_sems.at[i], recv_sems.at[i], (x, y, 1 - c)).wait_recv()
        for cp in cps:
            cp.wait_send()

    return pl.pallas_call(
        body, name=name, in_specs=[ANY] * n, out_specs=[ANY] * n,
        out_shape=[_sds(BIG[w][1:], F32) for w in idx],
        input_output_aliases={i: i for i in range(n)},
        scratch_shapes=[pltpu.SemaphoreType.DMA((n,)), pltpu.SemaphoreType.DMA((n,))],
        compiler_params=pltpu.CompilerParams(collective_id=barrier_id),
    )(*hs)


SMALL_ROWS = 40


def _adamw_math(w, g, m, v):
    m = ADAM_B1 * m + (1.0 - ADAM_B1) * g
    v = ADAM_B2 * v + (1.0 - ADAM_B2) * (g * g)
    m_hat = m / (1.0 - ADAM_B1 ** ADAM_STEP)
    v_hat = v / (1.0 - ADAM_B2 ** ADAM_STEP)
    delta = -ADAM_LR * (m_hat / (jnp.sqrt(v_hat) + ADAM_EPS) + ADAM_WD * w)
    return delta, m, v


def _small_start(pack, after):
    def body(pack_ref, land_ref, after_ref, send_sems, recv_sems, pack_thru, land_thru, token):
        x, y, c, _ = _place()
        for r in range(1, 8):
            peer = (x if not r & 4 else 1 - x, y if not r & 2 else 1 - y, c if not r & 1 else 1 - c)
            _remote(pack_ref, land_ref.at[r - 1], send_sems.at[r - 1], recv_sems.at[r - 1], peer).start()
        token[...] = jnp.zeros(TOKEN, F32)

    land = lax.empty((7, SMALL_ROWS, D), F32)
    outs = pl.pallas_call(
        body, name="small_start", in_specs=[HBM, HBM, ANY],
        out_specs=(SEM, SEM, HBM, HBM, pl.BlockSpec(memory_space=pltpu.VMEM)),
        out_shape=(pltpu.SemaphoreType.DMA((7,)), pltpu.SemaphoreType.DMA((7,)), pltpu.HBM(pack.shape, F32),
                   pltpu.HBM(land.shape, F32), _sds(TOKEN, F32)),
        input_output_aliases={0: 2, 1: 3},
        compiler_params=pltpu.CompilerParams(has_side_effects=EFFECT),
    )(pltpu.with_memory_space_constraint(pack, pltpu.HBM), pltpu.with_memory_space_constraint(land, pltpu.HBM), after)
    return outs


def _small_wait(send_sems, recv_sems, pack, land, after):
    def body(pack_ref, land_ref, send_sems, recv_sems, *rest):
        x, y, c, _ = _place()
        for r in range(1, 8):
            peer = (x if not r & 4 else 1 - x, y if not r & 2 else 1 - y, c if not r & 1 else 1 - c)
            cp = _remote(pack_ref, land_ref.at[r - 1], send_sems.at[r - 1], recv_sems.at[r - 1], peer)
            cp.wait_send()
            cp.wait_recv()

    return pl.pallas_call(
        body, name="small_wait", in_specs=[HBM, HBM, SEM, SEM] + [ANY] * len(after), out_specs=(HBM, HBM),
        out_shape=(pltpu.HBM(pack.shape, F32), pltpu.HBM(land.shape, F32)),
        input_output_aliases={0: 0, 1: 1},
        compiler_params=pltpu.CompilerParams(has_side_effects=EFFECT),
    )(pack, land, send_sems, recv_sems, *after)


def _small_update(place, pack, land, ws, ms, vs, flat):
    n = len(ws)

    def body(pref, pack_ref, land_ref, *refs):
        chip = pref[1]
        me = 2 * chip + pref[0]
        own = pack_ref[...]
        tot = None
        for dev in range(8):
            r = jnp.bitwise_xor(me, dev)
            term = jnp.where(r == 0, own, land_ref[jnp.maximum(r - 1, 0)])
            tot = term if tot is None else tot + term
        out, buf = refs[3 * n:-1], refs[-1]
        buf[...] = tot
        g_conv = jnp.zeros((3, SH_O), F32)
        for s in range(4):
            g_conv = g_conv + jnp.where(chip == s, buf[24:27, s * SH_O:(s + 1) * SH_O], 0.0)
        gs = [buf[0:2, :], buf[8:10, :], buf[16:17, :], g_conv]
        out[0][...] = buf[32:33, 0:128]
        for i in range(n):
            d, nm, nv = _adamw_math(refs[i][...], gs[i], refs[n + i][...], refs[2 * n + i][...])
            for j, val in enumerate((gs[i], d, nm, nv)):
                out[1 + j * n + i][...] = val.reshape(flat[i])

    def full(shape):
        nd = len(shape)
        return pl.BlockSpec(shape, lambda i, pref: (0,) * nd)

    specs = [full(w.shape) for w in ws]
    grid_spec = pltpu.PrefetchScalarGridSpec(
        num_scalar_prefetch=1, grid=(1,),
        in_specs=[full(pack.shape), full(land.shape)] + specs * 3,
        out_specs=[full((1, 128))] + [full(s) for s in flat] * 4,
        scratch_shapes=[pltpu.VMEM((SMALL_ROWS, D), F32)])
    outs = pl.pallas_call(
        body, name="small_update", grid_spec=grid_spec,
        out_shape=[_sds((1, 128), F32)] + [_sds(s, F32) for s in flat] * 4,
        compiler_params=_params(("arbitrary",)),
    )(place, pack, land, *ws, *ms, *vs)
    return outs[0], outs[1:1 + n], outs[1 + n:1 + 2 * n], outs[1 + 2 * n:1 + 3 * n], outs[1 + 3 * n:]


def _adamw_layer(ws, gs, ms, vs, idx, name):
    n = len(idx)
    dims = [(BIG[w][1], BIG[w][2] // 4, BIG[w][3]) for w in idx]

    def body(*refs):
        for i in range(n):
            gv = refs[n + i][...]
            d, nm, nv = _adamw_math(refs[i][...], gv, refs[2 * n + i][...], refs[3 * n + i][...])
            refs[4 * n + i][...] = d
            refs[5 * n + i][...] = nm
            refs[6 * n + i][...] = nv
            refs[7 * n + i][...] = gv

    specs = [pl.BlockSpec((k, q, cdim), lambda t: (0, t, 0)) for k, q, cdim in dims]
    outs = pl.pallas_call(
        body, name=name, grid=(4,), in_specs=specs * 4, out_specs=specs * 4,
        out_shape=[_sds(BIG[w][1:], F32) for w in idx] * 4,
        compiler_params=_params(("parallel",)),
    )(*ws, *gs, *ms, *vs)
    return [tuple(outs[j * n + i] for j in range(4)) for i in range(n)]


def _pad_rows(a, rows):
    return jnp.pad(a, ((0, rows - a.shape[0]), (0, 0)))


def kernel(x, mem, positions, norm_g, mem_norm_g, w_mem_kv, attn_w_in, attn_w_out, conv_w_in, conv_w, conv_w_out, final_g, loss_target, m_norm_g, m_mem_norm_g, m_w_mem_kv, m_attn_w_in, m_attn_w_out, m_conv_w_in, m_conv_w, m_conv_w_out, m_final_g, v_norm_g, v_mem_norm_g, v_w_mem_kv, v_attn_w_in, v_attn_w_out, v_conv_w_in, v_conv_w, v_conv_w_out, v_final_g):
    mx, my, mc = lax.axis_index("x"), lax.axis_index("y"), lax.axis_index("c")
    place = jnp.stack([mc, 2 * mx + my]).astype(jnp.int32)

    w_big = [w_mem_kv, attn_w_in, attn_w_out, conv_w_in, conv_w_out]
    m_big = [m_w_mem_kv, m_attn_w_in, m_attn_w_out, m_conv_w_in, m_conv_w_out]
    v_big = [v_w_mem_kv, v_attn_w_in, v_attn_w_out, v_conv_w_in, v_conv_w_out]
    first, rest = (1,), (0, 2, 3, 4)
    wb1 = _cast_weights(place, [w_big[i] for i in first], place, first, "cast_w_in_a")
    a1_send, a1_recv, a1_bufs, a1_token = _gather_start(wb1, place, first, "gather_a1_start", 4)
    wbr = _cast_weights(place, [w_big[i] for i in rest], a1_token, rest, "cast_weights")
    r_send, r_recv, r_bufs, gb_token = _gather_start(wbr, a1_token, rest, "gather_rest_start", 5)
    a2_send, a2_recv, gb_send, gb_recv = r_send, r_recv, r_send, r_recv
    a2_bufs, gb_bufs = r_bufs[:2], r_bufs[2:]
    started, rest = rest, (0, 2)

    xs, tgt = x[0], loss_target[0]
    g0, g1 = norm_g[0:1], norm_g[1:2]
    rc, rs1, rs2 = _rope_tables(positions[0].astype(F32).reshape(S, 1), gb_token)
    a1_bufs = _gather_wait(a1_send, a1_recv, a1_bufs, [rc], first, "gather_a1_wait")
    w_in_a = _gather_forward(a1_bufs, first, "gather_a1_forward", 0)[0].reshape(4, D, SH_A)
    hn0, q, k, v, qm0, z0 = _in_proj_a(xs, g0, w_in_a, rc, rs1, rs2, gb_token)
    a2_bufs = _gather_wait(a2_send, a2_recv, a2_bufs, [q], rest, "gather_a2_wait", started)
    f2_send, f2_recv, a2_bufs, f2_token = _forward_start(a2_bufs, None, q, rest, "forward_a2_start", 9)
    fwd = [_attn_fwd(q, k, v, 0, f2_token)]
    fwd.append(_attn_fwd(q, k, v, 1, fwd[0][0]))
    fwd.append(_attn_fwd(q, k, v, 2, fwd[1][0]))
    os_, ls, lss = [f[0] for f in fwd], [f[1] for f in fwd], [f[2] for f in fwd]
    cw_own = _pad_rows(conv_w[0], CW_ROWS)
    gb_bufs = _gather_wait(gb_send, gb_recv, gb_bufs, [os_[2]], LAYER_B, "gather_b_wait", started)
    fb_send, fb_recv, gb_bufs, fb_token = _forward_start(gb_bufs, cw_own, os_[2], LAYER_B, "forward_b_start", 10)
    wkv_f, w_out_a = _forward_wait(f2_send, f2_recv, a2_bufs, [os_[2], fb_token], rest, False, "forward_a2_wait")
    w_out_a = w_out_a.reshape(4, BR_A, SH_O)
    memn, kv = _mem_fwd(mem[0], mem_norm_g, wkv_f)
    h1 = _attn_out(os_, ls, qm0, kv[0], z0, xs, w_out_a)

    w_in_b, w_out_b, _, cw_f = _forward_wait(fb_send, fb_recv, gb_bufs, [h1], LAYER_B, True, "forward_b_wait")
    w_in_b = w_in_b.reshape(4, D, SH_B)
    w_out_b = w_out_b.reshape(BR_B, D)
    cw_f = lax.dynamic_update_slice(cw_f, cw_own[None], (2 * mx + my, 0, 0))
    cw8 = cw_f.transpose(1, 0, 2).reshape(CW_ROWS, D)
    hn1, bg, cg, u, qm1, z1 = _in_proj_b(h1, g1, w_in_b)
    dh2, loss_part, dfg = _conv_out_loss(bg, cg, u, cw8, qm1, kv[1], z1, h1, w_out_b, final_g.reshape(1, D), tgt)

    dproj_b, dw_out_b, dcw, dkv1, dw_out_b16 = _conv_bwd(dh2, bg, cg, u, cw8, qm1, kv[1], z1, w_out_b)
    dw_in_b, dw_in_b16 = _w_in_grad(hn1, dproj_b, IN_B, "w_in_b_grad")
    gs_b = [dw_in_b.reshape(4, 1, D, SH_B), dw_out_b.reshape(4, 1, BR_B // 4, D)]
    gb_b = [dw_in_b16.reshape(4, 1, D, SH_B), dw_out_b16.reshape(4, 1, BR_B // 4, D)]
    pb_send, pb_recv, gb_b, pb_land, pb_token = _pair_start(gb_b, LAYER_B, "pair_b_start", 6)
    dh1, dg1 = _in_proj_bwd(dproj_b, w_in_b, h1, g1, dh2, pb_token, IN_B, "in_proj_b_bwd")
    _, r1_b = _pair_wait(pb_send, pb_recv, gb_b, pb_land, [dh1], LAYER_B, "pair_b_wait")
    ps_b = _pair_sums(place, gs_b, r1_b, LAYER_B, "pair_sums_b")
    cb_send, cb_recv, cb_src, cb_land, cb_token = _chip_start(ps_b, LAYER_B, "chip_b_start", 7)

    outs = _attn_out_bwd(dh1, os_, ls, qm0, kv[0], z0, w_out_a, cb_token)
    dos, dds, dqm, dz, dw_out_a, dkv0, dw_out_a16 = outs[0:3], outs[3:6], outs[6], outs[7], outs[8], outs[9], outs[10]
    bwd = [_attn_bwd(q, k, v, dos[g], lss[g], dds[g], g) for g in range(3)]
    dproj_a = _qkv_bwd([b[0] for b in bwd], [b[1] for b in bwd], [b[2] for b in bwd], dqm, dz, rc, rs1, rs2)
    dw_in_a, dw_in_a16 = _w_in_grad(hn0, dproj_a, IN_A, "w_in_a_grad")
    dwkv, dwkv16, dmg = _mem_bwd(mem[0], mem_norm_g, memn, wkv_f, dkv0, dkv1)

    gs_a = [dwkv, dw_in_a.reshape(4, 1, D, SH_A), dw_out_a.reshape(4, 1, BR_A, SH_O)]
    r1_a = _pair_exchange([dwkv16, dw_in_a16.reshape(4, 1, D, SH_A), dw_out_a16.reshape(4, 1, BR_A, SH_O)], LAYER_A,
                          "pair_exchange_a", 1)
    ps_a = _pair_sums(place, gs_a, r1_a, LAYER_A, "pair_sums_a")
    ca_send, ca_recv, ca_src, ca_land, ca_token = _chip_start(ps_a, LAYER_A, "chip_a_start", 8)

    gx, dg0 = _in_proj_bwd(dproj_a, w_in_a, xs, g0, dh1, ca_token, IN_A, "in_proj_a_bwd")
    pack = jnp.concatenate([_pad_rows(jnp.concatenate([dg0, dg1], axis=0), 8), _pad_rows(dmg, 8), _pad_rows(dfg, 8),
                            dcw, _pad_rows(jnp.pad(loss_part, ((0, 0), (0, D - 128))), 8)], axis=0)
    sm_send, sm_recv, pack, sm_land, sm_token = _small_start(pack, ca_token)
    r2_b = _chip_wait(cb_send, cb_recv, cb_src, cb_land, [ca_token], LAYER_B, "chip_b_wait")
    hs_b = _chip_sums(place, gs_b, r1_b, r2_b, LAYER_B, "chip_sums_b")
    g_b = _pair_gather(hs_b, LAYER_B, "pair_gather_b", 2)
    upd_b = _adamw_layer([w_big[w] for w in LAYER_B], g_b, [m_big[w] for w in LAYER_B], [v_big[w] for w in LAYER_B],
                         LAYER_B, "adamw_b")
    r2_a = _chip_wait(ca_send, ca_recv, ca_src, ca_land, [gx, upd_b[0][0], upd_b[1][0], sm_token], LAYER_A,
                      "chip_a_wait")
    hs_a = _chip_sums(place, gs_a, r1_a, r2_a, LAYER_A, "chip_sums_a")
    g_a = _pair_gather(hs_a, LAYER_A, "pair_gather_a", 3)
    upd_a = _adamw_layer([w_big[w] for w in LAYER_A], g_a, [m_big[w] for w in LAYER_A], [v_big[w] for w in LAYER_A],
                         LAYER_A, "adamw_a")
    upd = upd_a + upd_b
    g_big = [u[3] for u in upd]
    pack, sm_land = _small_wait(sm_send, sm_recv, pack, sm_land, [r2_a[0]])
    sw = [norm_g, mem_norm_g, final_g.reshape(1, D), conv_w[0]]
    sm = [m_norm_g, m_mem_norm_g, m_final_g.reshape(1, D), m_conv_w[0]]
    sv = [v_norm_g, v_mem_norm_g, v_final_g.reshape(1, D), v_conv_w[0]]
    loss_row, sg, sd, snm, snv = _small_update(place, pack, sm_land, sw, sm, sv,
                                               [norm_g.shape, mem_norm_g.shape, final_g.shape, conv_w.shape])
    loss = loss_row[0, 0]
    g_norm, g_memnorm, g_final, g_conv = sg

    def order(norm, memnorm, wkv, w_in_a, w_out_a, w_in_b, conv, w_out_b, final):
        return (norm, memnorm, wkv, w_in_a, w_out_a, w_in_b, conv, w_out_b, final)

    grads = order(g_norm, g_memnorm, g_big[0], g_big[1], g_big[2], g_big[3], g_conv, g_big[4], g_final)
    deltas = order(sd[0], sd[1], upd[0][0], upd[1][0], upd[2][0], upd[3][0], sd[3], upd[4][0], sd[2])
    new_m = order(snm[0], snm[1], upd[0][1], upd[1][1], upd[2][1], upd[3][1], snm[3], upd[4][1], snm[2])
    new_v = order(snv[0], snv[1], upd[0][2], upd[1][2], upd[2][2], upd[3][2], snv[3], upd[4][2], snv[2])
    return (loss, gx[None], *grads, *deltas, *new_m, *new_v)
```

```python
import functools

import numpy as np
import jax
import jax.numpy as jnp
from jax import lax
from jax.experimental import pallas as pl
from jax.experimental.pallas import tpu as pltpu

F32 = jnp.float32
BF16 = jnp.bfloat16

S = 2048
D = 1024
TM = 256
NT = S // TM
MX = 512
NX = S // MX
HD = 64
GW = 512
NQ = 3 * GW
MW = 256
NM = 256
IN_A = 3 * NQ + MW + GW + MW
IN_B = 3 * D + MW + D + MW
BR_A = GW + MW
BR_B = D + MW
SH_A = IN_A // 4
SH_B = IN_B // 4
SH_O = D // 4
QBLK = 128
DILATIONS = (1, 4, 16)
EPS = 1e-6
SCALE = HD ** -0.5
NEG = -1e30
ROPE_THETA = 500000.0

ADAM_LR = 0.001
ADAM_B1 = 0.9
ADAM_B2 = 0.999
ADAM_EPS = 1e-08
ADAM_WD = 0.01
ADAM_STEP = 10

VMEM_LIMIT_BYTES = 60 * 1024 * 1024


def _params(sem=None):
    if sem is None:
        return pltpu.CompilerParams(vmem_limit_bytes=VMEM_LIMIT_BYTES)
    return pltpu.CompilerParams(dimension_semantics=sem, vmem_limit_bytes=VMEM_LIMIT_BYTES)


def _full(shape):
    nd = len(shape)
    return pl.BlockSpec(shape, lambda *_: (0,) * nd)


def _rows(width, tm=TM):
    return pl.BlockSpec((tm, width), lambda i: (i, 0))


def _sds(shape, dtype):
    return jax.ShapeDtypeStruct(shape, dtype)


def _silu_parts(z):
    sig = 0.5 * jnp.tanh(0.5 * z) + 0.5
    return z * sig, sig * (1.0 + z * (1.0 - sig))


def _dot(a, b):
    return jnp.dot(a, b, preferred_element_type=F32)


def _dot_nt(a, b):
    return lax.dot_general(a, b, (((1,), (1,)), ((), ())), preferred_element_type=F32)


def _dot_tn(a, b):
    return lax.dot_general(a, b, (((0,), (0,)), ((), ())), preferred_element_type=F32)


def _rope_fwd(t, c, s1, s2):
    return t * c + pltpu.roll(t, 120, 1) * s1 + pltpu.roll(t, 8, 1) * s2


def _rope_bwd(g, c, s1, s2):
    return g * c + pltpu.roll(g * s1, 8, 1) + pltpu.roll(g * s2, 120, 1)


MEM_HEADS = MW // HD


def _stack_heads(x):
    head = lax.broadcasted_iota(jnp.int32, x.shape, 1) // HD
    return jnp.concatenate([jnp.where(head == h, x, 0.0) for h in range(MEM_HEADS)], axis=0).astype(BF16)


def _unstack_heads(x4):
    tm = x4.shape[0] // MEM_HEADS
    head = lax.broadcasted_iota(jnp.int32, (tm, MW), 1) // HD
    out = x4[:tm]
    for h in range(1, MEM_HEADS):
        out = jnp.where(head == h, x4[h * tm:(h + 1) * tm], out)
    return out


def _mem_attn(qm, kv):
    q4 = _stack_heads(qm.astype(F32))
    s = _dot_nt(q4, kv[:, :MW]) * SCALE
    e = jnp.exp(s - jnp.max(s, axis=-1, keepdims=True))
    p = e * (1.0 / jnp.sum(e, axis=-1, keepdims=True))
    return p, _unstack_heads(_dot(p.astype(BF16), kv[:, MW:])), q4


def _mem_attn_bwd(dmo, p, mo, q4, kv, dkv_ref):
    tm = dmo.shape[0]
    head = lax.broadcasted_iota(jnp.int32, dmo.shape, 1) // HD
    prod = dmo * mo
    delta = jnp.concatenate([jnp.sum(jnp.where(head == h, prod, 0.0), axis=-1, keepdims=True)
                             for h in range(MEM_HEADS)], axis=0)
    d4 = _stack_heads(dmo)
    ds = (p * (_dot_nt(d4, kv[:, MW:]) - delta) * SCALE).astype(BF16)
    dkv_ref[:, :MW] += _dot_tn(ds, q4)
    dkv_ref[:, MW:] += _dot_tn(p.astype(BF16), d4)
    return _unstack_heads(_dot(ds, kv[:, :MW]))


def _merge(o_refs, l_refs):
    ls = [r[...] for r in l_refs]
    m = jnp.maximum(jnp.maximum(ls[0], ls[1]), ls[2])
    es = [jnp.exp(l - m) for l in ls]
    inv = 1.0 / (es[0] + es[1] + es[2])
    ws = [e * inv for e in es]
    os_ = [r[...] for r in o_refs]
    mix = ws[0] * os_[0] + ws[1] * os_[1] + ws[2] * os_[2]
    return ws, mix


def _conv_taps(cg, u, cgp, up, first):
    a = cg * u
    ap = jnp.where(first, 0.0, cgp * up)
    row = lax.broadcasted_iota(jnp.int32, a.shape, 0)
    a1 = jnp.where(row == 0, ap[7:8, :], pltpu.roll(a, 1, 0))
    a2 = jnp.where(row == 0, ap[6:7, :], jnp.where(row == 1, ap[7:8, :], pltpu.roll(a, 2, 0)))
    return a, a1, a2


def _rope_tables(posf, after):
    half = 8
    invf = np.float32(ROPE_THETA) ** (-np.arange(half, dtype=np.float32) * np.float32(2.0 / 16))
    lane = np.arange(128)
    table = np.where((lane % HD) < 16, invf[lane % half], 0.0).astype(np.float32)[None, :]

    def body(pos_ref, invf_ref, c_ref, s1_ref, s2_ref):
        ang = pos_ref[...] * invf_ref[...]
        jm = lax.broadcasted_iota(jnp.int32, ang.shape, 1) & (HD - 1)
        cs = jnp.cos(ang)
        sn = jnp.sin(ang)
        c_ref[...] = jnp.where(jm < 16, cs, 1.0)
        s1_ref[...] = jnp.where(jm < 8, -sn, 0.0)
        s2_ref[...] = jnp.where((jm >= 8) & (jm < 16), sn, 0.0)

    out = _sds((S, 128), F32)
    return pl.pallas_call(
        functools.partial(_skip_arg, body, 2), name="rope_tables", grid=(NT,),
        in_specs=[_rows(1), _full((1, 128)), pl.BlockSpec(memory_space=pl.ANY)],
        out_specs=[_rows(128)] * 3, out_shape=[out] * 3,
        compiler_params=_params(("parallel",)),
    )(posf, jnp.asarray(table), after)


def _in_proj_a(x, g0, w_in, c, s1, s2, after):
    def body(x_ref, g_ref, w_ref, c_ref, s1_ref, s2_ref, hn_ref, q_ref, k_ref, v_ref, qm_ref, z_ref, proj):
        xf = x_ref[...]
        hn = xf * lax.rsqrt(jnp.mean(xf * xf, axis=-1, keepdims=True) + EPS) * g_ref[...]
        hb = hn.astype(BF16)
        hn_ref[...] = hb
        for s in range(4):
            proj[:, s * SH_A:(s + 1) * SH_A] = _dot(hb, w_ref[s])
        cc, a1, a2 = c_ref[...], s1_ref[...], s2_ref[...]
        for j in range(NQ // 128):
            q_ref[:, j * 128:(j + 1) * 128] = (
                _rope_fwd(proj[:, j * 128:(j + 1) * 128], cc, a1, a2) * SCALE).astype(BF16)
            k_ref[:, j * 128:(j + 1) * 128] = _rope_fwd(
                proj[:, NQ + j * 128:NQ + (j + 1) * 128], cc, a1, a2).astype(BF16)
        v_ref[...] = proj[:, 2 * NQ:3 * NQ].astype(BF16)
        qm_ref[...] = proj[:, 3 * NQ:3 * NQ + MW].astype(BF16)
        z_ref[...] = proj[:, 3 * NQ + MW:]

    return pl.pallas_call(
        functools.partial(_skip_arg, body, 6), name="in_proj_a", grid=(NT,),
        in_specs=[_rows(D), _full((1, D)), _full((4, D, SH_A)), _rows(128), _rows(128), _rows(128),
                  pl.BlockSpec(memory_space=pl.ANY)],
        out_specs=[_rows(D), _rows(NQ), _rows(NQ), _rows(NQ), _rows(MW), _rows(BR_A)],
        out_shape=[_sds((S, D), BF16), _sds((S, NQ), BF16), _sds((S, NQ), BF16), _sds((S, NQ), BF16),
                   _sds((S, MW), BF16), _sds((S, BR_A), F32)],
        scratch_shapes=[pltpu.VMEM((TM, IN_A), F32)],
        compiler_params=_params(("parallel",)),
    )(x, g0, w_in, c, s1, s2, after)


def _mem_fwd(mem, mg, wkv):
    def body(mem_ref, mg_ref, w_ref, memn_ref, kv_ref):
        mf = mem_ref[...]
        n = mf * lax.rsqrt(jnp.mean(mf * mf, axis=-1, keepdims=True) + EPS)
        for i in range(2):
            mn = (n * mg_ref[i:i + 1, :]).astype(BF16)
            memn_ref[i] = mn
            acc = _dot(mn[:, 0:NM], w_ref[0, i])
            for s in range(1, 4):
                acc += _dot(mn[:, s * NM:(s + 1) * NM], w_ref[s, i])
            kv_ref[i] = acc.astype(BF16)

    return pl.pallas_call(
        body, name="mem_fwd", grid=(1,),
        in_specs=[_full((NM, D)), _full((2, D)), _full((4, 2, NM, 2 * MW))],
        out_specs=[_full((2, NM, D)), _full((2, NM, 2 * MW))],
        out_shape=[_sds((2, NM, D), BF16), _sds((2, NM, 2 * MW), BF16)],
        compiler_params=_params(("arbitrary",)),
    )(mem, mg, wkv)


def _band_mask(j):
    qi = lax.broadcasted_iota(jnp.int32, (QBLK, 2 * QBLK), 0)
    kj = lax.broadcasted_iota(jnp.int32, (QBLK, 2 * QBLK), 1)
    dist = qi + QBLK - kj
    return (dist >= 0) & (dist <= QBLK) & ((kj >= QBLK) | (j > 0))


LANES = 128
NCHUNK = GW // LANES
FWD_UNROLL = 16
BWD_UNROLL = 16
CONV_CHUNK = 256


def _perm_matrix(d):
    n = TM // d
    p = np.zeros((TM, TM), np.float32)
    for r in range(d):
        for i in range(n):
            p[r * n + i, i * d + r] = 1.0
    return p


def _split_dot(p, x):
    hi = x.astype(BF16)
    lo = (x - hi.astype(F32)).astype(BF16)
    both = _dot(p, jnp.concatenate([hi, lo], axis=1))
    return both[:, :LANES] + both[:, LANES:]


def _pair_dot(p, a, b):
    both = _dot(p, jnp.concatenate([a, b], axis=1))
    return both[:, :LANES], both[:, LANES:]


def _tile_to_streams(y, dst, t, d):
    n, ln = TM // d, S // d
    for r in range(d):
        dst[r * ln + t * n:r * ln + (t + 1) * n, :] = y[r * n:(r + 1) * n].astype(dst.dtype)


def _tile_from_streams(src, t, d):
    n, ln = TM // d, S // d
    return jnp.concatenate([src[r * ln + t * n:r * ln + (t + 1) * n, :] for r in range(d)], axis=0)


def _head_masks():
    first = lax.broadcasted_iota(jnp.int32, (TM, LANES), 1) < HD
    return first, jnp.logical_not(first)


def _attn_fwd(q, k, v, g, after):
    d = DILATIONS[g]
    nb = S // d // QBLK
    perm = _perm_matrix(d)

    def body(q_ref, k_ref, v_ref, p_ref, pt_ref, o_ref, l_ref, ls_ref, q0, q1, ks, vs, os_):
        first, second = _head_masks()
        pm = p_ref[...]
        for t in range(NT):
            rows = slice(t * TM, (t + 1) * TM)
            if d == 1:
                qt = q_ref[rows, :].astype(F32)
            else:
                qt, kt = _pair_dot(pm, q_ref[rows, :], k_ref[rows, :])
                _tile_to_streams(kt, ks, t, d)
                if t % 2 == 0:
                    va, vb = _pair_dot(pm, v_ref[rows, :], v_ref[(t + 1) * TM:(t + 2) * TM, :])
                    _tile_to_streams(va, vs, t, d)
                    _tile_to_streams(vb, vs, t + 1, d)
            _tile_to_streams(jnp.where(first, qt, 0.0), q0, t, d)
            _tile_to_streams(jnp.where(second, qt, 0.0), q1, t, d)
        kref, vref = (k_ref, v_ref) if d == 1 else (ks, vs)
        oref, lref = (o_ref, l_ref) if d == 1 else (os_, ls_ref)

        def blk(b, carry):
            r0 = pl.multiple_of(b * QBLK, QBLK)
            p0 = pl.multiple_of(jnp.maximum(b - 1, 0) * QBLK, QBLK)
            kk = jnp.concatenate([kref[pl.ds(p0, QBLK), :], kref[pl.ds(r0, QBLK), :]], axis=0)
            vv = jnp.concatenate([vref[pl.ds(p0, QBLK), :], vref[pl.ds(r0, QBLK), :]], axis=0)
            valid = _band_mask(b & (nb - 1))
            acc, lse = [], []
            for qh in (q0, q1):
                s = jnp.where(valid, _dot_nt(qh[pl.ds(r0, QBLK), :], kk), NEG)
                m = jnp.max(s, axis=-1, keepdims=True)
                e = jnp.exp(s - m)
                l = jnp.sum(e, axis=-1, keepdims=True)
                acc.append(_dot(e.astype(BF16), vv) * (1.0 / l))
                lse.append(m + jnp.log(l))
            f = first[:QBLK]
            oref[pl.ds(r0, QBLK), :] = jnp.where(f, acc[0], acc[1])
            lref[pl.ds(r0, QBLK), :] = jnp.where(f, lse[0], lse[1])
            return carry

        lax.fori_loop(0, S // QBLK, blk, 0, unroll=FWD_UNROLL)
        if d > 1:
            ptm = pt_ref[...]
            for t in range(NT):
                rows = slice(t * TM, (t + 1) * TM)
                o_ref[rows, :] = _split_dot(ptm, _tile_from_streams(os_, t, d))
                l_ref[rows, :] = _split_dot(ptm, _tile_from_streams(ls_ref, t, d))

    qkv_spec = pl.BlockSpec((S, LANES), lambda c: (0, g * NCHUNK + c))
    out_spec = pl.BlockSpec((S, LANES), lambda c: (0, c))
    n_out = 2 if d == 1 else 3
    inner = body if d > 1 else functools.partial(_drop_arg, body, 7)
    outs = pl.pallas_call(
        functools.partial(_skip_arg, inner, 5), name=f"attn_fwd_g{g}", grid=(NCHUNK,),
        in_specs=[qkv_spec] * 3 + [_full((TM, TM))] * 2 + [pl.BlockSpec(memory_space=pl.ANY)],
        out_specs=[out_spec] * n_out, out_shape=[_sds((S, GW), F32)] * n_out,
        scratch_shapes=[pltpu.VMEM((S, LANES), BF16)] * 4 + [pltpu.VMEM((S, LANES), F32)],
        compiler_params=_params(("parallel",)),
    )(q, k, v, jnp.asarray(perm, BF16), jnp.asarray(perm.T, BF16), after)
    return (outs[0], outs[1], outs[1]) if d == 1 else tuple(outs)


def _drop_arg(body, pos, *refs):
    return body(*refs[:pos], None, *refs[pos:])


def _attn_out(os_, ls, qm, kv0, z, x, w_out):
    def body(o0, o1, o2, l0, l1, l2, qm_ref, kv_ref, z_ref, x_ref, w_ref, h_ref, ybuf):
        _, mix = _merge((o0, o1, o2), (l0, l1, l2))
        sz, _ = _silu_parts(z_ref[...])
        ybuf[:, :GW] = (mix * sz[:, :GW]).astype(BF16)
        _, mo, _ = _mem_attn(qm_ref[...], kv_ref[...])
        ybuf[:, GW:] = (mo * sz[:, GW:]).astype(BF16)
        yb = ybuf[...]
        for s in range(4):
            cs = slice(s * SH_O, (s + 1) * SH_O)
            h_ref[:, cs] = x_ref[:, cs] + _dot(yb, w_ref[s])

    return pl.pallas_call(
        body, name="attn_out", grid=(NX,),
        in_specs=[_rows(GW, MX)] * 6 + [_rows(MW, MX), _full((NM, 2 * MW)), _rows(BR_A, MX), _rows(D, MX),
                                        _full((4, BR_A, SH_O))],
        out_specs=_rows(D, MX), out_shape=_sds((S, D), F32),
        scratch_shapes=[pltpu.VMEM((MX, BR_A), BF16)],
        compiler_params=_params(("parallel",)),
    )(*os_, *ls, qm, kv0, z, x, w_out)


def _in_proj_b(h1, g1, w_in):
    def body(x_ref, g_ref, w_ref, hn_ref, bg_ref, cg_ref, u_ref, qm_ref, z_ref, proj):
        xf = x_ref[...]
        hn = xf * lax.rsqrt(jnp.mean(xf * xf, axis=-1, keepdims=True) + EPS) * g_ref[...]
        hb = hn.astype(BF16)
        hn_ref[...] = hb
        for s in range(4):
            proj[:, s * SH_B:(s + 1) * SH_B] = _dot(hb, w_ref[s])
        bg_ref[...] = proj[:, :D]
        cg_ref[...] = proj[:, D:2 * D]
        u_ref[...] = proj[:, 2 * D:3 * D]
        qm_ref[...] = proj[:, 3 * D:3 * D + MW].astype(BF16)
        z_ref[...] = proj[:, 3 * D + MW:]

    return pl.pallas_call(
        body, name="in_proj_b", grid=(NT,),
        in_specs=[_rows(D), _full((1, D)), _full((4, D, SH_B))],
        out_specs=[_rows(D), _rows(D), _rows(D), _rows(D), _rows(MW), _rows(BR_B)],
        out_shape=[_sds((S, D), BF16), _sds((S, D), F32), _sds((S, D), F32), _sds((S, D), F32),
                   _sds((S, MW), BF16), _sds((S, BR_B), F32)],
        scratch_shapes=[pltpu.VMEM((TM, IN_B), F32)],
        compiler_params=_params(("parallel",)),
    )(h1, g1, w_in)


def _prev8(width):
    return pl.BlockSpec((8, width), lambda i: (jnp.maximum(i * (MX // 8) - 1, 0), 0))


def _conv_out_loss(bg, cg, u, cw, qm, kv1, z, h1, w_out, fg, tgt):
    def body(bg_ref, cg_ref, u_ref, cgp_ref, up_ref, cw_ref, qm_ref, kv_ref, z_ref, h_ref, w_ref, fg_ref, t_ref,
             dh_ref, loss_ref, dfg_ref, ybuf):
        i = pl.program_id(0)
        a, a1, a2 = _conv_taps(cg_ref[...], u_ref[...], cgp_ref[...], up_ref[...], i == 0)
        conv = cw_ref[0:1, :] * a2 + cw_ref[1:2, :] * a1 + cw_ref[2:3, :] * a
        sz, _ = _silu_parts(z_ref[...])
        ybuf[:, :D] = (bg_ref[...] * conv * sz[:, :D]).astype(BF16)
        _, mo, _ = _mem_attn(qm_ref[...], kv_ref[...])
        ybuf[:, D:] = (mo * sz[:, D:]).astype(BF16)
        h2 = h_ref[...] + _dot(ybuf[...], w_ref[...])
        rstd = lax.rsqrt(jnp.mean(h2 * h2, axis=-1, keepdims=True) + EPS)
        n = h2 * rstd
        fgv = fg_ref[...]
        err = n * fgv - t_ref[...]
        dout = err * (1.0 / D)
        dn = dout * fgv
        dh_ref[...] = rstd * (dn - n * jnp.mean(dn * n, axis=-1, keepdims=True))

        @pl.when(i == 0)
        def _():
            loss_ref[...] = jnp.zeros_like(loss_ref)
            dfg_ref[...] = jnp.zeros_like(dfg_ref)

        loss_ref[...] += jnp.sum(err * err) * (0.5 / D)
        dfg_ref[...] += jnp.sum(dout * n, axis=0, keepdims=True)

    return pl.pallas_call(
        body, name="conv_out_loss", grid=(NX,),
        in_specs=[_rows(D, MX), _rows(D, MX), _rows(D, MX), _prev8(D), _prev8(D), _full((8, D)), _rows(MW, MX),
                  _full((NM, 2 * MW)), _rows(BR_B, MX), _rows(D, MX), _full((BR_B, D)), _full((1, D)), _rows(D, MX)],
        out_specs=[_rows(D, MX), _full((1, 128)), _full((1, D))],
        out_shape=[_sds((S, D), F32), _sds((1, 128), F32), _sds((1, D), F32)],
        scratch_shapes=[pltpu.VMEM((MX, BR_B), BF16)],
        compiler_params=_params(("arbitrary",)),
    )(bg, cg, u, cg, u, cw, qm, kv1, z, h1, w_out, fg, tgt)


def _conv_bwd(dh2, bg, cg, u, cw, qm, kv1, z, w_out):
    rev = lambda i: (NX - 1 - i, 0)
    rows = lambda w: pl.BlockSpec((MX, w), rev)
    prev8 = pl.BlockSpec((8, D), lambda i: (jnp.maximum((NX - 1 - i) * (MX // 8) - 1, 0), 0))

    def body(dh_ref, bg_ref, cg_ref, u_ref, cgp_ref, up_ref, cw_ref, qm_ref, kv_ref, z_ref, w_ref,
             dproj_ref, dw_ref, dcw_ref, dkv_ref, dwb_ref, ybuf, carry):
        i = pl.program_id(0)

        @pl.when(i == 0)
        def _():
            dw_ref[...] = jnp.zeros_like(dw_ref)
            dcw_ref[...] = jnp.zeros_like(dcw_ref)
            dkv_ref[...] = jnp.zeros_like(dkv_ref)
            carry[...] = jnp.zeros_like(carry)

        dhb = dh_ref[...].astype(BF16)
        dy = _dot_nt(dhb, w_ref[...])
        kvv = kv_ref[...]
        p, mo, q4 = _mem_attn(qm_ref[...], kvv)
        szm, dszm = _silu_parts(z_ref[:, D:])
        ybuf[:, D:] = (mo * szm).astype(BF16)
        dym = dy[:, D:]
        dproj_ref[:, 3 * D + MW + D:] = (dym * mo * dszm).astype(BF16)
        first_tile = i == NX - 1
        for c in range(D // CONV_CHUNK):
            cs = slice(c * CONV_CHUNK, (c + 1) * CONV_CHUNK)
            bgv, cgv, uv = bg_ref[:, cs], cg_ref[:, cs], u_ref[:, cs]
            a, a1, a2 = _conv_taps(cgv, uv, cgp_ref[:, cs], up_ref[:, cs], first_tile)
            w0, w1, w2 = cw_ref[0:1, cs], cw_ref[1:2, cs], cw_ref[2:3, cs]
            conv = w0 * a2 + w1 * a1 + w2 * a
            mix = bgv * conv
            sz, dsz = _silu_parts(z_ref[:, cs])
            ybuf[:, cs] = (mix * sz).astype(BF16)
            dyc = dy[:, cs]
            dproj_ref[:, 3 * D + MW + c * CONV_CHUNK:3 * D + MW + (c + 1) * CONV_CHUNK] = (
                dyc * mix * dsz).astype(BF16)
            dmix = dyc * sz
            dproj_ref[:, cs] = (dmix * conv).astype(BF16)
            dc = dmix * bgv
            nxt = carry[:, cs]
            row = lax.broadcasted_iota(jnp.int32, dc.shape, 0)
            dc1 = jnp.where(row == MX - 1, nxt[0:1, :], pltpu.roll(dc, MX - 1, 0))
            dc2 = jnp.where(row == MX - 2, nxt[0:1, :],
                            jnp.where(row == MX - 1, nxt[1:2, :], pltpu.roll(dc, MX - 2, 0)))
            carry[:, cs] = dc[0:8, :]
            da = w2 * dc + w1 * dc1 + w0 * dc2
            dproj_ref[:, D + c * CONV_CHUNK:D + (c + 1) * CONV_CHUNK] = (da * uv).astype(BF16)
            dproj_ref[:, 2 * D + c * CONV_CHUNK:2 * D + (c + 1) * CONV_CHUNK] = (da * cgv).astype(BF16)
            dcw_ref[0:1, cs] += jnp.sum(dc * a2, axis=0, keepdims=True)
            dcw_ref[1:2, cs] += jnp.sum(dc * a1, axis=0, keepdims=True)
            dcw_ref[2:3, cs] += jnp.sum(dc * a, axis=0, keepdims=True)
        dw_ref[...] += _dot_tn(ybuf[...], dhb)
        dproj_ref[:, 3 * D:3 * D + MW] = _mem_attn_bwd(dym * szm, p, mo, q4, kvv, dkv_ref).astype(BF16)

        @pl.when(i == NX - 1)
        def _():
            dwb_ref[...] = dw_ref[...].astype(BF16)

    return pl.pallas_call(
        body, name="conv_bwd", grid=(NX,),
        in_specs=[rows(D), rows(D), rows(D), rows(D), prev8, prev8, _full((8, D)), rows(MW),
                  _full((NM, 2 * MW)), rows(BR_B), _full((BR_B, D))],
        out_specs=[rows(IN_B), _full((BR_B, D)), _full((8, D)), _full((NM, 2 * MW)), _full((BR_B, D))],
        out_shape=[_sds((S, IN_B), BF16), _sds((BR_B, D), F32), _sds((8, D), F32), _sds((NM, 2 * MW), F32),
                   _sds((BR_B, D), BF16)],
        scratch_shapes=[pltpu.VMEM((MX, BR_B), BF16), pltpu.VMEM((8, D), F32)],
        compiler_params=_params(("arbitrary",)),
    )(dh2, bg, cg, u, cg, u, cw, qm, kv1, z, w_out)


def _in_proj_bwd(dproj, w_in, xin, g, dres, after, width, name):
    sh = width // 4

    def body(dp_ref, w_ref, x_ref, g_ref, dr_ref, dx_ref, dg_ref):
        i = pl.program_id(0)
        dhn = _dot_nt(dp_ref[:, 0:sh], w_ref[0])
        for s in range(1, 4):
            dhn += _dot_nt(dp_ref[:, s * sh:(s + 1) * sh], w_ref[s])
        xf = x_ref[...]
        rstd = lax.rsqrt(jnp.mean(xf * xf, axis=-1, keepdims=True) + EPS)
        n = xf * rstd
        dn = dhn * g_ref[...]
        dx_ref[...] = dr_ref[...] + rstd * (dn - n * jnp.mean(dn * n, axis=-1, keepdims=True))

        @pl.when(i == 0)
        def _():
            dg_ref[...] = jnp.zeros_like(dg_ref)

        dg_ref[...] += jnp.sum(dhn * n, axis=0, keepdims=True)

    return pl.pallas_call(
        functools.partial(_skip_arg, body, 5), name=name, grid=(NT,),
        in_specs=[_rows(width), _full((4, D, sh)), _rows(D), _full((1, D)), _rows(D), pl.BlockSpec(memory_space=pl.ANY)],
        out_specs=[_rows(D), _full((1, D))],
        out_shape=[_sds((S, D), F32), _sds((1, D), F32)],
        compiler_params=_params(("arbitrary",)),
    )(dproj, w_in, xin, g, dres, after)


def _w_in_grad(hn, dproj, width, name):
    sh = width // 4

    def body(hn_ref, dp_ref, dw_ref, dwb_ref):
        dw = _dot_tn(hn_ref[...], dp_ref[...])
        dw_ref[0] = dw
        dwb_ref[0] = dw.astype(BF16)

    spec = pl.BlockSpec((1, D, sh), lambda s: (s, 0, 0))
    return pl.pallas_call(
        body, name=name, grid=(4,),
        in_specs=[_full((S, D)), pl.BlockSpec((S, sh), lambda s: (0, s))],
        out_specs=[spec, spec], out_shape=[_sds((4, D, sh), F32), _sds((4, D, sh), BF16)],
        compiler_params=_params(("parallel",)),
    )(hn, dproj)


def _attn_out_bwd(dh1, os_, ls, qm, kv0, z, w_out, after):
    ones_bd = np.kron(np.eye(GW // HD, dtype=np.float32), np.ones((HD, HD), np.float32))

    def body(dh_ref, o0, o1, o2, l0, l1, l2, qm_ref, kv_ref, z_ref, w_ref, bd_ref,
             do0, do1, do2, dd0, dd1, dd2, dqm_ref, dz_ref, dw_ref, dkv_ref, dwb_ref, ybuf):
        i = pl.program_id(0)

        @pl.when(i == 0)
        def _():
            dw_ref[...] = jnp.zeros_like(dw_ref)
            dkv_ref[...] = jnp.zeros_like(dkv_ref)

        ws, mix = _merge((o0, o1, o2), (l0, l1, l2))
        sz, dsz = _silu_parts(z_ref[...])
        kvv = kv_ref[...]
        p, mo, q4 = _mem_attn(qm_ref[...], kvv)
        ybuf[:, :GW] = (mix * sz[:, :GW]).astype(BF16)
        ybuf[:, GW:] = (mo * sz[:, GW:]).astype(BF16)
        yb = ybuf[...]
        dh = dh_ref[...]
        dy = None
        for s in range(4):
            dhb = dh[:, s * SH_O:(s + 1) * SH_O].astype(BF16)
            dw_ref[s] += _dot_tn(yb, dhb)
            part = _dot_nt(dhb, w_ref[s])
            dy = part if dy is None else dy + part
        dcat = dy * sz
        dz_ref[:, :GW] = (dy[:, :GW] * mix * dsz[:, :GW]).astype(BF16)
        dz_ref[:, GW:] = (dy[:, GW:] * mo * dsz[:, GW:]).astype(BF16)
        dmix = dcat[:, :GW]
        prod = dmix * mix
        hi = prod.astype(BF16)
        lo = (prod - hi.astype(F32)).astype(BF16)
        bd = bd_ref[...]
        tot = _dot(hi, bd) + _dot(lo, bd)
        for w, do_ref, dd_ref in zip(ws, (do0, do1, do2), (dd0, dd1, dd2)):
            do_ref[...] = (w * dmix).astype(BF16)
            dd_ref[...] = w * tot

        dqm_ref[...] = _mem_attn_bwd(dcat[:, GW:], p, mo, q4, kvv, dkv_ref).astype(BF16)

        @pl.when(i == NX - 1)
        def _():
            dwb_ref[...] = dw_ref[...].astype(BF16)

    return pl.pallas_call(
        functools.partial(_skip_arg, body, 12), name="attn_out_bwd", grid=(NX,),
        in_specs=[_rows(D, MX)] + [_rows(GW, MX)] * 6 + [_rows(MW, MX), _full((NM, 2 * MW)), _rows(BR_A, MX),
                                                           _full((4, BR_A, SH_O)), _full((GW, GW)),
                                                           pl.BlockSpec(memory_space=pl.ANY)],
        out_specs=[_rows(GW, MX)] * 6 + [_rows(MW, MX), _rows(BR_A, MX), _full((4, BR_A, SH_O)),
                                         _full((NM, 2 * MW)), _full((4, BR_A, SH_O))],
        out_shape=[_sds((S, GW), BF16)] * 3 + [_sds((S, GW), F32)] * 3 + [
            _sds((S, MW), BF16), _sds((S, BR_A), BF16), _sds((4, BR_A, SH_O), F32), _sds((NM, 2 * MW), F32),
            _sds((4, BR_A, SH_O), BF16)],
        scratch_shapes=[pltpu.VMEM((MX, BR_A), BF16)],
        compiler_params=_params(("arbitrary",)),
    )(dh1, *os_, *ls, qm, kv0, z, w_out, jnp.asarray(ones_bd, dtype=BF16), after)


def _attn_bwd(q, k, v, do, lse_s, dd, g):
    d = DILATIONS[g]
    nb = S // d // QBLK
    perm = _perm_matrix(d)

    def body(q_ref, k_ref, v_ref, do_ref, l_ref, dd_ref, p_ref, pt_ref, dq_ref, dk_ref, dv_ref,
             q0, q1, g0, g1, ks, vs, dds, dqs, dks, dvs):
        first, second = _head_masks()
        pm = p_ref[...]
        for t in range(NT):
            rows = slice(t * TM, (t + 1) * TM)
            if d == 1:
                qt = q_ref[rows, :].astype(F32)
                gt = do_ref[rows, :].astype(F32)
            else:
                qt, gt = _pair_dot(pm, q_ref[rows, :], do_ref[rows, :])
                kt, vt = _pair_dot(pm, k_ref[rows, :], v_ref[rows, :])
                _tile_to_streams(kt, ks, t, d)
                _tile_to_streams(vt, vs, t, d)
                _tile_to_streams(_split_dot(pm, dd_ref[rows, :]), dds, t, d)
            _tile_to_streams(jnp.where(first, qt, 0.0), q0, t, d)
            _tile_to_streams(jnp.where(second, qt, 0.0), q1, t, d)
            _tile_to_streams(jnp.where(first, gt, 0.0), g0, t, d)
            _tile_to_streams(jnp.where(second, gt, 0.0), g1, t, d)
        kref, vref, ddref = (k_ref, v_ref, dd_ref) if d == 1 else (ks, vs, dds)
        dqref, dkref, dvref = dqs, dks, dvs
        dkref[...] = jnp.zeros_like(dkref)
        dvref[...] = jnp.zeros_like(dvref)

        def blk(b, carry):
            r0 = pl.multiple_of(b * QBLK, QBLK)
            p0 = pl.multiple_of(jnp.maximum(b - 1, 0) * QBLK, QBLK)
            kk = jnp.concatenate([kref[pl.ds(p0, QBLK), :], kref[pl.ds(r0, QBLK), :]], axis=0)
            vv = jnp.concatenate([vref[pl.ds(p0, QBLK), :], vref[pl.ds(r0, QBLK), :]], axis=0)
            lb = l_ref[pl.ds(r0, QBLK), :]
            ddb = ddref[pl.ds(r0, QBLK), :]
            lcol = jnp.concatenate([lb[:, 0:1], lb[:, HD:HD + 1]], axis=0)
            dcol = jnp.concatenate([ddb[:, 0:1], ddb[:, HD:HD + 1]], axis=0)
            valid = _band_mask(b & (nb - 1))
            valid2 = jnp.concatenate([valid, valid], axis=0)
            qq = jnp.concatenate([q0[pl.ds(r0, QBLK), :], q1[pl.ds(r0, QBLK), :]], axis=0)
            gg = jnp.concatenate([g0[pl.ds(r0, QBLK), :], g1[pl.ds(r0, QBLK), :]], axis=0)
            p = jnp.where(valid2, jnp.exp(_dot_nt(qq, kk) - lcol), 0.0)
            ds = (p * (_dot_nt(gg, vv) - dcol)).astype(BF16)
            dq2 = _dot(ds, kk)
            dqref[pl.ds(r0, QBLK), :] = jnp.where(first[:QBLK], dq2[:QBLK], dq2[QBLK:])
            dkk = _dot_tn(ds, qq)
            dvv = _dot_tn(p.astype(BF16), gg)
            dkref[pl.ds(p0, QBLK), :] += dkk[:QBLK]
            dkref[pl.ds(r0, QBLK), :] += dkk[QBLK:]
            dvref[pl.ds(p0, QBLK), :] += dvv[:QBLK]
            dvref[pl.ds(r0, QBLK), :] += dvv[QBLK:]
            return carry

        lax.fori_loop(0, S // QBLK, blk, 0, unroll=BWD_UNROLL)

        ptm = pt_ref[...] if d > 1 else None
        for t in range(NT):
            rows = slice(t * TM, (t + 1) * TM)
            if d == 1:
                dq_ref[rows, :] = dqs[rows, :].astype(BF16)
                dk_ref[rows, :] = dks[rows, :].astype(BF16)
                dv_ref[rows, :] = dvs[rows, :].astype(BF16)
            else:
                tq, tk = _pair_dot(ptm, _tile_from_streams(dqs, t, d).astype(BF16),
                                   _tile_from_streams(dks, t, d).astype(BF16))
                dq_ref[rows, :] = tq.astype(BF16)
                dk_ref[rows, :] = tk.astype(BF16)
                if t % 2 == 0:
                    ta, tb = _pair_dot(ptm, _tile_from_streams(dvs, t, d).astype(BF16),
                                       _tile_from_streams(dvs, t + 1, d).astype(BF16))
                    dv_ref[rows, :] = ta.astype(BF16)
                    dv_ref[(t + 1) * TM:(t + 2) * TM, :] = tb.astype(BF16)

    qkv_spec = pl.BlockSpec((S, LANES), lambda c: (0, g * NCHUNK + c))
    one_spec = pl.BlockSpec((S, LANES), lambda c: (0, c))
    return pl.pallas_call(
        body, name=f"attn_bwd_g{g}", grid=(NCHUNK,),
        in_specs=[qkv_spec] * 3 + [one_spec] * 3 + [_full((TM, TM))] * 2, out_specs=[one_spec] * 3,
        out_shape=[_sds((S, GW), BF16)] * 3,
        scratch_shapes=[pltpu.VMEM((S, LANES), BF16)] * 6 + [pltpu.VMEM((S, LANES), F32)] * 4,
        compiler_params=_params(("parallel",)),
    )(q, k, v, do, lse_s, dd, jnp.asarray(perm, BF16), jnp.asarray(perm.T, BF16))


def _qkv_bwd(dqs, dks, dvs, dqm, dz, c, s1, s2):
    def body(q0, q1, q2, k0, k1, k2, v0, v1, v2, dqm_ref, dz_ref, c_ref, s1_ref, s2_ref, dp_ref):
        cc, a1, a2 = c_ref[...], s1_ref[...], s2_ref[...]
        for g, (qr, kr, vr) in enumerate(((q0, k0, v0), (q1, k1, v1), (q2, k2, v2))):
            for j in range(GW // 128):
                ls_ = slice(j * 128, (j + 1) * 128)
                c0 = g * GW + j * 128
                dp_ref[:, c0:c0 + 128] = (_rope_bwd(qr[:, ls_].astype(F32), cc, a1, a2) * SCALE).astype(BF16)
                dp_ref[:, NQ + c0:NQ + c0 + 128] = _rope_bwd(kr[:, ls_].astype(F32), cc, a1, a2).astype(BF16)
            dp_ref[:, 2 * NQ + g * GW:2 * NQ + (g + 1) * GW] = vr[...]
        dp_ref[:, 3 * NQ:3 * NQ + MW] = dqm_ref[...]
        dp_ref[:, 3 * NQ + MW:] = dz_ref[...]

    return pl.pallas_call(
        body, name="qkv_bwd", grid=(NT,),
        in_specs=[_rows(GW)] * 9 + [_rows(MW), _rows(BR_A), _rows(128), _rows(128), _rows(128)],
        out_specs=_rows(IN_A), out_shape=_sds((S, IN_A), BF16),
        compiler_params=_params(("parallel",)),
    )(*dqs, *dks, *dvs, dqm, dz, c, s1, s2)


def _mem_bwd(mem, mg, memn, wkv, dkv0, dkv1):
    def body(mem_ref, mg_ref, memn_ref, w_ref, d0_ref, d1_ref, dw_ref, dwb_ref, dg_ref):
        mf = mem_ref[...]
        n = mf * lax.rsqrt(jnp.mean(mf * mf, axis=-1, keepdims=True) + EPS)
        for i, d_ref in enumerate((d0_ref, d1_ref)):
            dkv = d_ref[...].astype(BF16)
            mn = memn_ref[i]
            for s in range(4):
                cs = slice(s * NM, (s + 1) * NM)
                dw = _dot_tn(mn[:, cs], dkv)
                dw_ref[s, i] = dw
                dwb_ref[s, i] = dw.astype(BF16)
                dmn = _dot_nt(dkv, w_ref[s, i])
                dg_ref[i:i + 1, cs] = jnp.sum(dmn * n[:, cs], axis=0, keepdims=True)

    return pl.pallas_call(
        body, name="mem_bwd", grid=(1,),
        in_specs=[_full((NM, D)), _full((2, D)), _full((2, NM, D)), _full((4, 2, NM, 2 * MW)),
                  _full((NM, 2 * MW)), _full((NM, 2 * MW))],
        out_specs=[_full((4, 2, NM, 2 * MW)), _full((4, 2, NM, 2 * MW)), _full((2, D))],
        out_shape=[_sds((4, 2, NM, 2 * MW), F32), _sds((4, 2, NM, 2 * MW), BF16), _sds((2, D), F32)],
        compiler_params=_params(("arbitrary",)),
    )(mem, mg, memn, wkv, dkv0, dkv1)


MESH = pl.DeviceIdType.MESH
ANY = pl.BlockSpec(memory_space=pl.ANY)
BIG = (("wkv", 2, NM, 2 * MW), ("w_in_a", 1, D, SH_A), ("w_out_a", 1, BR_A, SH_O),
       ("w_in_b", 1, D, SH_B), ("w_out_b", 1, BR_B // 4, D))
NBIG = len(BIG)
CW_ROWS = 8


def _place():
    x, y, c = lax.axis_index("x"), lax.axis_index("y"), lax.axis_index("c")
    chips = ((1 - x, y), (x, 1 - y), (1 - x, 1 - y))
    return x, y, c, chips


def _remote(src, dst, ssem, rsem, dev):
    return pltpu.make_async_remote_copy(src_ref=src, dst_ref=dst, send_sem=ssem, recv_sem=rsem,
                                        device_id=dev, device_id_type=MESH)


def _cast_weights(place, ws, after, idx, name):
    nblk = 4
    n = len(idx)
    dims = [BIG[w][1:] for w in idx]

    def body(pref, *refs):
        for i in range(n):
            refs[n + 1 + i][0] = refs[i][...].astype(BF16)

    grid_spec = pltpu.PrefetchScalarGridSpec(
        num_scalar_prefetch=1, grid=(nblk,),
        in_specs=[pl.BlockSpec((k, r // nblk, cdim), lambda i, pref: (0, i, 0)) for k, r, cdim in dims]
        + [pl.BlockSpec(memory_space=pl.ANY)],
        out_specs=[pl.BlockSpec((1, k, r // nblk, cdim), lambda i, pref: (pref[1], 0, i, 0)) for k, r, cdim in dims])
    return pl.pallas_call(
        body, name=name, grid_spec=grid_spec,
        out_shape=[_sds((4, k, r, cdim), BF16) for k, r, cdim in dims],
        compiler_params=_params(("parallel",)),
    )(place, *ws, after)


LAYER_A = (0, 1, 2)
LAYER_B = (3, 4)
HBM = pl.BlockSpec(memory_space=pltpu.HBM)
SEM = pl.BlockSpec(memory_space=pltpu.SEMAPHORE)
EFFECT = pltpu.SideEffectType.DATAFLOW_SIDE_EFFECTING
TOKEN = (8, 128)


def _half(ref, w, which):
    h = BIG[w][2] // 2
    return ref.at[:, pl.ds(which * h, h), :]


def _skip_arg(body, pos, *refs):
    return body(*refs[:pos], *refs[pos + 1:])


def _gather_start(wb, after, idx, name, barrier_id):
    n = len(idx)

    def body(*refs):
        src = refs[:n]
        send_sems, recv_sems = refs[n + 1], refs[n + 2]
        token = refs[2 * n + 3]
        x, y, c, chips = _place()
        _peer_barrier([(px, py, c) for px, py in chips])
        me = 2 * x + y
        for i in range(n):
            for j, (px, py) in enumerate(chips):
                mine = _half(src[i].at[me], idx[i], c)
                _remote(mine, mine, send_sems.at[j * n + i], recv_sems.at[j * n + i], (px, py, c)).start()
        token[...] = jnp.zeros(TOKEN, F32)

    outs = pl.pallas_call(
        body, name=name, in_specs=[HBM] * n + [ANY],
        out_specs=(SEM, SEM) + (HBM,) * n + (pl.BlockSpec(memory_space=pltpu.VMEM),),
        out_shape=(pltpu.SemaphoreType.DMA((3 * n,)), pltpu.SemaphoreType.DMA((3 * n,)))
        + tuple(pltpu.HBM(w.shape, w.dtype) for w in wb) + (_sds(TOKEN, F32),),
        input_output_aliases={i: 2 + i for i in range(n)},
        compiler_params=pltpu.CompilerParams(has_side_effects=EFFECT, collective_id=barrier_id),
    )(*[pltpu.with_memory_space_constraint(w, pltpu.HBM) for w in wb], after)
    return outs[0], outs[1], list(outs[2:2 + n]), outs[2 + n]


def _gather_wait(send_sems, recv_sems, wb, after, idx, name, started=None):
    n = len(idx)
    started = idx if started is None else started
    n_all = len(started)
    pos = [started.index(w) for w in idx]

    def body(*refs):
        buf = refs[:n]
        send_sems, recv_sems = refs[n], refs[n + 1]
        x, y, c, chips = _place()
        me = 2 * x + y
        for j, (px, py) in enumerate(chips):
            for i in range(n):
                mine = _half(buf[i].at[me], idx[i], c)
                got = _half(buf[i].at[2 * px + py], idx[i], c)
                k = j * n_all + pos[i]
                _remote(mine, mine, send_sems.at[k], recv_sems.at[k], (px, py, c)).wait_send()
                _remote(got, got, send_sems.at[k], recv_sems.at[k], (px, py, c)).wait_recv()

    outs = pl.pallas_call(
        body, name=name, in_specs=[HBM] * n + [SEM, SEM] + [ANY] * len(after), out_specs=(HBM,) * n,
        out_shape=tuple(pltpu.HBM(w.shape, w.dtype) for w in wb),
        input_output_aliases={i: i for i in range(n)},
        compiler_params=pltpu.CompilerParams(has_side_effects=EFFECT),
    )(*wb, send_sems, recv_sems, *after)
    return list(outs)


def _gather_forward(wb, idx, name, barrier_id):
    n = len(idx)

    def body(*refs):
        dst = refs[n:2 * n]
        send_sems, recv_sems = refs[2 * n], refs[2 * n + 1]
        x, y, c, chips = _place()
        _sibling_barrier(x, y, c)
        cps = []
        for j, (px, py) in enumerate(chips):
            for i in range(n):
                got = _half(dst[i].at[2 * px + py], idx[i], c)
                cps.append(_remote(got, got, send_sems.at[j, i], recv_sems.at[j, i], (x, y, 1 - c)))
                cps[-1].start()
        for j, (px, py) in enumerate(chips):
            for i in range(n):
                got = _half(dst[i].at[2 * px + py], idx[i], 1 - c)
                _remote(got, got, send_sems.at[j, i], recv_sems.at[j, i], (x, y, 1 - c)).wait_recv()
        for cp in cps:
            cp.wait_send()

    return pl.pallas_call(
        body, name=name, in_specs=[ANY] * n, out_specs=[ANY] * n, out_shape=[_sds(w.shape, BF16) for w in wb],
        input_output_aliases={i: i for i in range(n)},
        scratch_shapes=[pltpu.SemaphoreType.DMA((3, n)), pltpu.SemaphoreType.DMA((3, n))],
        compiler_params=pltpu.CompilerParams(collective_id=barrier_id),
    )(*wb)


def _forward_start(wb, cw, after, idx, name, barrier_id):
    n = len(idx)
    m = n if cw is None else n + 2

    def body(*refs):
        buf = refs[:n]
        send_sems, recv_sems = refs[m + 1], refs[m + 2]
        token = refs[2 * m + 3]
        x, y, c, chips = _place()
        _peer_barrier([(x, y, 1 - c)] + ([] if cw is None else [(px, py, c) for px, py in chips]))
        for j, (px, py) in enumerate(chips):
            for i in range(n):
                got = _half(buf[i].at[2 * px + py], idx[i], c)
                _remote(got, got, send_sems.at[j * (n + 1) + i], recv_sems.at[j * (n + 1) + i], (x, y, 1 - c)).start()
            if cw is not None:
                _remote(refs[n], refs[n + 1].at[2 * x + y], send_sems.at[j * (n + 1) + n],
                        recv_sems.at[j * (n + 1) + n], (px, py, c)).start()
        token[...] = jnp.zeros(TOKEN, F32)

    arrays = list(wb) if cw is None else list(wb) + [cw, lax.empty((4, CW_ROWS, SH_O), F32)]
    outs = pl.pallas_call(
        body, name=name, in_specs=[HBM] * m + [ANY],
        out_specs=(SEM, SEM) + (HBM,) * m + (pl.BlockSpec(memory_space=pltpu.VMEM),),
        out_shape=(pltpu.SemaphoreType.DMA((3 * (n + 1),)), pltpu.SemaphoreType.DMA((3 * (n + 1),)))
        + tuple(pltpu.HBM(a.shape, a.dtype) for a in arrays) + (_sds(TOKEN, F32),),
        input_output_aliases={i: 2 + i for i in range(m)},
        compiler_params=pltpu.CompilerParams(has_side_effects=EFFECT, collective_id=barrier_id),
    )(*[pltpu.with_memory_space_constraint(a, pltpu.HBM) for a in arrays], after)
    return outs[0], outs[1], list(outs[2:2 + m]), outs[2 + m]


def _forward_wait(send_sems, recv_sems, arrays, after, idx, with_cw, name):
    n = len(idx)
    m = len(arrays)

    def body(*refs):
        buf = refs[:n]
        send_sems, recv_sems = refs[m], refs[m + 1]
        x, y, c, chips = _place()
        for j, (px, py) in enumerate(chips):
            for i in range(n):
                sent = _half(buf[i].at[2 * px + py], idx[i], c)
                got = _half(buf[i].at[2 * px + py], idx[i], 1 - c)
                k = j * (n + 1) + i
                _remote(sent, sent, send_sems.at[k], recv_sems.at[k], (x, y, 1 - c)).wait_send()
                _remote(got, got, send_sems.at[k], recv_sems.at[k], (x, y, 1 - c)).wait_recv()
            if with_cw:
                k = j * (n + 1) + n
                theirs = refs[n + 1].at[2 * px + py]
                _remote(refs[n], theirs, send_sems.at[k], recv_sems.at[k], (px, py, c)).wait_send()
                _remote(refs[n], theirs, send_sems.at[k], recv_sems.at[k], (px, py, c)).wait_recv()

    outs = pl.pallas_call(
        body, name=name, in_specs=[HBM] * m + [SEM, SEM] + [ANY] * len(after), out_specs=(HBM,) * m,
        out_shape=tuple(pltpu.HBM(a.shape, a.dtype) for a in arrays),
        input_output_aliases={i: i for i in range(m)},
        compiler_params=pltpu.CompilerParams(has_side_effects=EFFECT),
    )(*arrays, send_sems, recv_sems, *after)
    return list(outs)


def _peer_barrier(peers):
    barrier = pltpu.get_barrier_semaphore()
    for peer in peers:
        pl.semaphore_signal(barrier, inc=1, device_id=peer, device_id_type=MESH)
    pl.semaphore_wait(barrier, len(peers))


def _sibling_barrier(x, y, c):
    _peer_barrier([(x, y, 1 - c)])


def _pair_exchange(gs, idx, name, barrier_id):
    n = len(idx)

    def body(*refs):
        src, dst = refs[:n], refs[n:2 * n]
        send_sems, recv_sems = refs[2 * n:]
        x, y, c, _ = _place()
        _sibling_barrier(x, y, c)
        cps = []
        for i in range(n):
            h = BIG[idx[i]][2] // 2
            cps.append(_remote(src[i].at[:, :, pl.ds((1 - c) * h, h), :], dst[i], send_sems.at[i], recv_sems.at[i],
                               (x, y, 1 - c)))
            cps[-1].start()
        for cp in cps:
            cp.wait()

    return pl.pallas_call(
        body, name=name, in_specs=[ANY] * n, out_specs=[ANY] * n,
        out_shape=[_sds((4, BIG[w][1], BIG[w][2] // 2, BIG[w][3]), BF16) for w in idx],
        scratch_shapes=[pltpu.SemaphoreType.DMA((n,)), pltpu.SemaphoreType.DMA((n,))],
        compiler_params=pltpu.CompilerParams(collective_id=barrier_id),
    )(*gs)


def _pair_start(gs, idx, name, barrier_id):
    n = len(idx)

    def body(*refs):
        src, land = refs[:n], refs[n:2 * n]
        send_sems, recv_sems = refs[2 * n], refs[2 * n + 1]
        token = refs[4 * n + 2]
        x, y, c, _ = _place()
        _sibling_barrier(x, y, c)
        for i in range(n):
            h = BIG[idx[i]][2] // 2
            _remote(src[i].at[:, :, pl.ds((1 - c) * h, h), :], land[i], send_sems.at[i], recv_sems.at[i],
                    (x, y, 1 - c)).start()
        token[...] = jnp.zeros(TOKEN, F32)

    lands = [lax.empty((4, BIG[w][1], BIG[w][2] // 2, BIG[w][3]), BF16) for w in idx]
    arrays = list(gs) + lands
    outs = pl.pallas_call(
        body, name=name, in_specs=[HBM] * (2 * n),
        out_specs=(SEM, SEM) + (HBM,) * (2 * n) + (pl.BlockSpec(memory_space=pltpu.VMEM),),
        out_shape=(pltpu.SemaphoreType.DMA((n,)), pltpu.SemaphoreType.DMA((n,)))
        + tuple(pltpu.HBM(a.shape, a.dtype) for a in arrays) + (_sds(TOKEN, F32),),
        input_output_aliases={i: 2 + i for i in range(2 * n)},
        compiler_params=pltpu.CompilerParams(has_side_effects=EFFECT, collective_id=barrier_id),
    )(*[pltpu.with_memory_space_constraint(a, pltpu.HBM) for a in arrays])
    return outs[0], outs[1], list(outs[2:2 + n]), list(outs[2 + n:2 + 2 * n]), outs[2 + 2 * n]


def _pair_wait(send_sems, recv_sems, gs, lands, after, idx, name):
    n = len(idx)

    def body(*refs):
        src, land = refs[:n], refs[n:2 * n]
        send_sems, recv_sems = refs[2 * n], refs[2 * n + 1]
        x, y, c, _ = _place()
        for i in range(n):
            h = BIG[idx[i]][2] // 2
            cp = _remote(src[i].at[:, :, pl.ds((1 - c) * h, h), :], land[i], send_sems.at[i], recv_sems.at[i],
                         (x, y, 1 - c))
            cp.wait_send()
            cp.wait_recv()

    arrays = list(gs) + list(lands)
    outs = pl.pallas_call(
        body, name=name, in_specs=[HBM] * (2 * n) + [SEM, SEM] + [ANY] * len(after), out_specs=(HBM,) * (2 * n),
        out_shape=tuple(pltpu.HBM(a.shape, a.dtype) for a in arrays),
        input_output_aliases={i: i for i in range(2 * n)},
        compiler_params=pltpu.CompilerParams(has_side_effects=EFFECT),
    )(*arrays, send_sems, recv_sems, *after)
    return list(outs[:n]), list(outs[n:])


def _pair_sums(place, gs, r1s, idx, name):
    n = len(idx)
    dims = [(BIG[w][1], BIG[w][2] // 2, BIG[w][3]) for w in idx]

    def body(pref, *refs):
        for i in range(n):
            refs[2 * n + i][...] = (refs[i][...] + refs[n + i][...].astype(F32)).astype(BF16)

    mine = [pl.BlockSpec((1, k, h, cdim), lambda s, pref: (s, 0, pref[0], 0)) for k, h, cdim in dims]
    whole = [pl.BlockSpec((1, k, h, cdim), lambda s, pref: (s, 0, 0, 0)) for k, h, cdim in dims]
    grid_spec = pltpu.PrefetchScalarGridSpec(num_scalar_prefetch=1, grid=(4,), in_specs=mine + whole, out_specs=whole)
    return pl.pallas_call(
        body, name=name, grid_spec=grid_spec, out_shape=[_sds((4, k, h, cdim), BF16) for k, h, cdim in dims],
        compiler_params=_params(("parallel",)),
    )(place, *gs, *r1s)


def _chip_start(ps, idx, name, barrier_id):
    n = len(idx)

    def body(*refs):
        src, land = refs[:n], refs[n:2 * n]
        send_sems, recv_sems = refs[2 * n], refs[2 * n + 1]
        token = refs[4 * n + 2]
        x, y, c, chips = _place()
        _peer_barrier([(px, py, c) for px, py in chips])
        for j, (px, py) in enumerate(chips):
            for i in range(n):
                _remote(src[i].at[2 * px + py], land[i].at[j], send_sems.at[j * n + i], recv_sems.at[j * n + i],
                        (px, py, c)).start()
        token[...] = jnp.zeros(TOKEN, F32)

    lands = [lax.empty((3,) + p.shape[1:], BF16) for p in ps]
    outs = pl.pallas_call(
        body, name=name, in_specs=[HBM] * (2 * n),
        out_specs=(SEM, SEM) + (HBM,) * (2 * n) + (pl.BlockSpec(memory_space=pltpu.VMEM),),
        out_shape=(pltpu.SemaphoreType.DMA((3 * n,)), pltpu.SemaphoreType.DMA((3 * n,)))
        + tuple(pltpu.HBM(a.shape, a.dtype) for a in list(ps) + lands) + (_sds(TOKEN, F32),),
        input_output_aliases={i: 2 + i for i in range(2 * n)},
        compiler_params=pltpu.CompilerParams(has_side_effects=EFFECT, collective_id=barrier_id),
    )(*[pltpu.with_memory_space_constraint(a, pltpu.HBM) for a in list(ps) + lands])
    return outs[0], outs[1], list(outs[2:2 + n]), list(outs[2 + n:2 + 2 * n]), outs[2 + 2 * n]


def _chip_wait(send_sems, recv_sems, ps, lands, after, idx, name):
    n = len(idx)

    def body(*refs):
        src, land = refs[:n], refs[n:2 * n]
        send_sems, recv_sems = refs[2 * n], refs[2 * n + 1]
        x, y, c, chips = _place()
        for j, (px, py) in enumerate(chips):
            for i in range(n):
                cp = _remote(src[i].at[2 * px + py], land[i].at[j], send_sems.at[j * n + i], recv_sems.at[j * n + i],
                             (px, py, c))
                cp.wait_send()
                cp.wait_recv()

    arrays = list(ps) + list(lands)
    outs = pl.pallas_call(
        body, name=name, in_specs=[HBM] * (2 * n) + [SEM, SEM] + [ANY] * len(after), out_specs=(HBM,) * (2 * n),
        out_shape=tuple(pltpu.HBM(a.shape, a.dtype) for a in arrays),
        input_output_aliases={i: i for i in range(2 * n)},
        compiler_params=pltpu.CompilerParams(has_side_effects=EFFECT),
    )(*arrays, send_sems, recv_sems, *after)
    return list(outs[n:])


def _chip_sums(place, gs, r1s, r2s, idx, name):
    n = len(idx)
    dims = [(BIG[w][1], BIG[w][2] // 4, BIG[w][3]) for w in idx]

    def body(pref, *refs):
        for i in range(n):
            acc = refs[i][0] + refs[n + i][0].astype(F32)
            for j in range(3):
                acc = acc + refs[2 * n + i][j].astype(F32)
            refs[3 * n + i][...] = acc

    in_specs = ([pl.BlockSpec((1, k, q, cdim), lambda t, pref: (pref[1], 0, pref[0] * 2 + t, 0)) for k, q, cdim in dims]
                + [pl.BlockSpec((1, k, q, cdim), lambda t, pref: (pref[1], 0, t, 0)) for k, q, cdim in dims]
                + [pl.BlockSpec((3, k, q, cdim), lambda t, pref: (0, 0, t, 0)) for k, q, cdim in dims])
    out_specs = [pl.BlockSpec((k, q, cdim), lambda t, pref: (0, pref[0] * 2 + t, 0)) for k, q, cdim in dims]
    grid_spec = pltpu.PrefetchScalarGridSpec(num_scalar_prefetch=1, grid=(2,), in_specs=in_specs, out_specs=out_specs)
    return pl.pallas_call(
        body, name=name, grid_spec=grid_spec, out_shape=[_sds(BIG[w][1:], F32) for w in idx],
        compiler_params=_params(("parallel",)),
    )(place, *gs, *r1s, *r2s)


def _pair_gather(hs, idx, name, barrier_id):
    n = len(idx)

    def body(*refs):
        dst = refs[n:2 * n]
        send_sems, recv_sems = refs[2 * n:]
        x, y, c, _ = _place()
        _sibling_barrier(x, y, c)
        cps = []
        for i in range(n):
            mine = _half(dst[i], idx[i], c)
            cps.append(_remote(mine, mine, send_sems.at[i], recv_sems.at[i], (x, y, 1 - c)))
            cps[-1].start()
        for i in range(n):
            theirs = _half(dst[i], idx[i], 1 - c)
            _remote(theirs, theirs, send_sems.at[i], recv_sems.at[i], (x, y, 1 - c)).wait_recv()
        for cp in cps:
            cp.wait_send()

    return pl.pallas_call(
        body, name=name, in_specs=[ANY] * n, out_specs=[ANY] * n,
        out_shape=[_sds(BIG[w][1:], F32) for w in idx],
        input_output_aliases={i: i for i in range(n)},
        scratch_shapes=[pltpu.SemaphoreType.DMA((n,)), pltpu.SemaphoreType.DMA((n,))],
        compiler_params=pltpu.CompilerParams(collective_id=barrier_id),
    )(*hs)


SMALL_ROWS = 40


def _adamw_math(w, g, m, v):
    m = ADAM_B1 * m + (1.0 - ADAM_B1) * g
    v = ADAM_B2 * v + (1.0 - ADAM_B2) * (g * g)
    m_hat = m / (1.0 - ADAM_B1 ** ADAM_STEP)
    v_hat = v / (1.0 - ADAM_B2 ** ADAM_STEP)
    delta = -ADAM_LR * (m_hat / (jnp.sqrt(v_hat) + ADAM_EPS) + ADAM_WD * w)
    return delta, m, v


def _small_start(pack, after):
    def body(pack_ref, land_ref, after_ref, send_sems, recv_sems, pack_thru, land_thru, token):
        x, y, c, _ = _place()
        for r in range(1, 8):
            peer = (x if not r & 4 else 1 - x, y if not r & 2 else 1 - y, c if not r & 1 else 1 - c)
            _remote(pack_ref, land_ref.at[r - 1], send_sems.at[r - 1], recv_sems.at[r - 1], peer).start()
        token[...] = jnp.zeros(TOKEN, F32)

    land = lax.empty((7, SMALL_ROWS, D), F32)
    outs = pl.pallas_call(
        body, name="small_start", in_specs=[HBM, HBM, ANY],
        out_specs=(SEM, SEM, HBM, HBM, pl.BlockSpec(memory_space=pltpu.VMEM)),
        out_shape=(pltpu.SemaphoreType.DMA((7,)), pltpu.SemaphoreType.DMA((7,)), pltpu.HBM(pack.shape, F32),
                   pltpu.HBM(land.shape, F32), _sds(TOKEN, F32)),
        input_output_aliases={0: 2, 1: 3},
        compiler_params=pltpu.CompilerParams(has_side_effects=EFFECT),
    )(pltpu.with_memory_space_constraint(pack, pltpu.HBM), pltpu.with_memory_space_constraint(land, pltpu.HBM), after)
    return outs


def _small_wait(send_sems, recv_sems, pack, land, after):
    def body(pack_ref, land_ref, send_sems, recv_sems, *rest):
        x, y, c, _ = _place()
        for r in range(1, 8):
            peer = (x if not r & 4 else 1 - x, y if not r & 2 else 1 - y, c if not r & 1 else 1 - c)
            cp = _remote(pack_ref, land_ref.at[r - 1], send_sems.at[r - 1], recv_sems.at[r - 1], peer)
            cp.wait_send()
            cp.wait_recv()

    return pl.pallas_call(
        body, name="small_wait", in_specs=[HBM, HBM, SEM, SEM] + [ANY] * len(after), out_specs=(HBM, HBM),
        out_shape=(pltpu.HBM(pack.shape, F32), pltpu.HBM(land.shape, F32)),
        input_output_aliases={0: 0, 1: 1},
        compiler_params=pltpu.CompilerParams(has_side_effects=EFFECT),
    )(pack, land, send_sems, recv_sems, *after)


def _small_update(place, pack, land, ws, ms, vs, flat):
    n = len(ws)

    def body(pref, pack_ref, land_ref, *refs):
        chip = pref[1]
        me = 2 * chip + pref[0]
        own = pack_ref[...]
        tot = None
        for dev in range(8):
            r = jnp.bitwise_xor(me, dev)
            term = jnp.where(r == 0, own, land_ref[jnp.maximum(r - 1, 0)])
            tot = term if tot is None else tot + term
        out, buf = refs[3 * n:-1], refs[-1]
        buf[...] = tot
        g_conv = jnp.zeros((3, SH_O), F32)
        for s in range(4):
            g_conv = g_conv + jnp.where(chip == s, buf[24:27, s * SH_O:(s + 1) * SH_O], 0.0)
        gs = [buf[0:2, :], buf[8:10, :], buf[16:17, :], g_conv]
        out[0][...] = buf[32:33, 0:128]
        for i in range(n):
            d, nm, nv = _adamw_math(refs[i][...], gs[i], refs[n + i][...], refs[2 * n + i][...])
            for j, val in enumerate((gs[i], d, nm, nv)):
                if len(flat[i]) == 3:
                    for r in range(flat[i][0]):
                        out[1 + j * n + i][r] = val[r:r + 1, :]
                else:
                    out[1 + j * n + i][...] = val.reshape(flat[i])

    def full(shape):
        nd = len(shape)
        return pl.BlockSpec(shape, lambda i, pref: (0,) * nd)

    specs = [full(w.shape) for w in ws]
    grid_spec = pltpu.PrefetchScalarGridSpec(
        num_scalar_prefetch=1, grid=(1,),
        in_specs=[full(pack.shape), full(land.shape)] + specs * 3,
        out_specs=[full((1, 128))] + [full(s) for s in flat] * 4,
        scratch_shapes=[pltpu.VMEM((SMALL_ROWS, D), F32)])
    outs = pl.pallas_call(
        body, name="small_update", grid_spec=grid_spec,
        out_shape=[_sds((1, 128), F32)] + [_sds(s, F32) for s in flat] * 4,
        compiler_params=_params(("arbitrary",)),
    )(place, pack, land, *ws, *ms, *vs)
    return outs[0], outs[1:1 + n], outs[1 + n:1 + 2 * n], outs[1 + 2 * n:1 + 3 * n], outs[1 + 3 * n:]


def _adamw_layer(ws, gs, ms, vs, idx, name):
    n = len(idx)
    dims = [(BIG[w][1], BIG[w][2] // 4, BIG[w][3]) for w in idx]

    def body(*refs):
        for i in range(n):
            gv = refs[n + i][...]
            d, nm, nv = _adamw_math(refs[i][...], gv, refs[2 * n + i][...], refs[3 * n + i][...])
            refs[4 * n + i][...] = d
            refs[5 * n + i][...] = nm
            refs[6 * n + i][...] = nv
            refs[7 * n + i][...] = gv

    specs = [pl.BlockSpec((k, q, cdim), lambda t: (0, t, 0)) for k, q, cdim in dims]
    outs = pl.pallas_call(
        body, name=name, grid=(4,), in_specs=specs * 4, out_specs=specs * 4,
        out_shape=[_sds(BIG[w][1:], F32) for w in idx] * 4,
        compiler_params=_params(("parallel",)),
    )(*ws, *gs, *ms, *vs)
    return [tuple(outs[j * n + i] for j in range(4)) for i in range(n)]


def _pad_rows(a, rows):
    return jnp.pad(a, ((0, rows - a.shape[0]), (0, 0)))


def kernel(x, mem, positions, norm_g, mem_norm_g, w_mem_kv, attn_w_in, attn_w_out, conv_w_in, conv_w, conv_w_out, final_g, loss_target, m_norm_g, m_mem_norm_g, m_w_mem_kv, m_attn_w_in, m_attn_w_out, m_conv_w_in, m_conv_w, m_conv_w_out, m_final_g, v_norm_g, v_mem_norm_g, v_w_mem_kv, v_attn_w_in, v_attn_w_out, v_conv_w_in, v_conv_w, v_conv_w_out, v_final_g):
    mx, my, mc = lax.axis_index("x"), lax.axis_index("y"), lax.axis_index("c")
    place = jnp.stack([mc, 2 * mx + my]).astype(jnp.int32)

    w_big = [w_mem_kv, attn_w_in, attn_w_out, conv_w_in, conv_w_out]
    m_big = [m_w_mem_kv, m_attn_w_in, m_attn_w_out, m_conv_w_in, m_conv_w_out]
    v_big = [v_w_mem_kv, v_attn_w_in, v_attn_w_out, v_conv_w_in, v_conv_w_out]
    first, rest = (1,), (0, 2, 3, 4)
    wb1 = _cast_weights(place, [w_big[i] for i in first], place, first, "cast_w_in_a")
    a1_send, a1_recv, a1_bufs, a1_token = _gather_start(wb1, place, first, "gather_a1_start", 4)
    wbr = _cast_weights(place, [w_big[i] for i in rest], a1_token, rest, "cast_weights")
    r_send, r_recv, r_bufs, gb_token = _gather_start(wbr, a1_token, rest, "gather_rest_start", 5)
    a2_send, a2_recv, gb_send, gb_recv = r_send, r_recv, r_send, r_recv
    a2_bufs, gb_bufs = r_bufs[:2], r_bufs[2:]
    started, rest = rest, (0, 2)

    xs, tgt = x[0], loss_target[0]
    g0, g1 = norm_g[0:1], norm_g[1:2]
    rc, rs1, rs2 = _rope_tables(positions[0].astype(F32).reshape(S, 1), gb_token)
    a1_bufs = _gather_wait(a1_send, a1_recv, a1_bufs, [rc], first, "gather_a1_wait")
    w_in_a = _gather_forward(a1_bufs, first, "gather_a1_forward", 0)[0].reshape(4, D, SH_A)
    hn0, q, k, v, qm0, z0 = _in_proj_a(xs, g0, w_in_a, rc, rs1, rs2, gb_token)
    a2_bufs = _gather_wait(a2_send, a2_recv, a2_bufs, [q], rest, "gather_a2_wait", started)
    f2_send, f2_recv, a2_bufs, f2_token = _forward_start(a2_bufs, None, q, rest, "forward_a2_start", 9)
    fwd = [_attn_fwd(q, k, v, 0, f2_token)]
    fwd.append(_attn_fwd(q, k, v, 1, fwd[0][0]))
    fwd.append(_attn_fwd(q, k, v, 2, fwd[1][0]))
    os_, ls, lss = [f[0] for f in fwd], [f[1] for f in fwd], [f[2] for f in fwd]
    cw_own = _pad_rows(conv_w[0], CW_ROWS)
    gb_bufs = _gather_wait(gb_send, gb_recv, gb_bufs, [os_[2]], LAYER_B, "gather_b_wait", started)
    fb_send, fb_recv, gb_bufs, fb_token = _forward_start(gb_bufs, cw_own, os_[2], LAYER_B, "forward_b_start", 10)
    wkv_f, w_out_a = _forward_wait(f2_send, f2_recv, a2_bufs, [os_[2], fb_token], rest, False, "forward_a2_wait")
    w_out_a = w_out_a.reshape(4, BR_A, SH_O)
    memn, kv = _mem_fwd(mem[0], mem_norm_g, wkv_f)
    h1 = _attn_out(os_, ls, qm0, kv[0], z0, xs, w_out_a)

    w_in_b, w_out_b, _, cw_f = _forward_wait(fb_send, fb_recv, gb_bufs, [h1], LAYER_B, True, "forward_b_wait")
    w_in_b = w_in_b.reshape(4, D, SH_B)
    w_out_b = w_out_b.reshape(BR_B, D)
    cw_f = lax.dynamic_update_slice(cw_f, cw_own[None], (2 * mx + my, 0, 0))
    cw8 = cw_f.transpose(1, 0, 2).reshape(CW_ROWS, D)
    hn1, bg, cg, u, qm1, z1 = _in_proj_b(h1, g1, w_in_b)
    dh2, loss_part, dfg = _conv_out_loss(bg, cg, u, cw8, qm1, kv[1], z1, h1, w_out_b, final_g.reshape(1, D), tgt)

    dproj_b, dw_out_b, dcw, dkv1, dw_out_b16 = _conv_bwd(dh2, bg, cg, u, cw8, qm1, kv[1], z1, w_out_b)
    dw_in_b, dw_in_b16 = _w_in_grad(hn1, dproj_b, IN_B, "w_in_b_grad")
    gs_b = [dw_in_b.reshape(4, 1, D, SH_B), dw_out_b.reshape(4, 1, BR_B // 4, D)]
    gb_b = [dw_in_b16.reshape(4, 1, D, SH_B), dw_out_b16.reshape(4, 1, BR_B // 4, D)]
    pb_send, pb_recv, gb_b, pb_land, pb_token = _pair_start(gb_b, LAYER_B, "pair_b_start", 6)
    dh1, dg1 = _in_proj_bwd(dproj_b, w_in_b, h1, g1, dh2, pb_token, IN_B, "in_proj_b_bwd")
    _, r1_b = _pair_wait(pb_send, pb_recv, gb_b, pb_land, [dh1], LAYER_B, "pair_b_wait")
    ps_b = _pair_sums(place, gs_b, r1_b, LAYER_B, "pair_sums_b")
    cb_send, cb_recv, cb_src, cb_land, cb_token = _chip_start(ps_b, LAYER_B, "chip_b_start", 7)

    outs = _attn_out_bwd(dh1, os_, ls, qm0, kv[0], z0, w_out_a, cb_token)
    dos, dds, dqm, dz, dw_out_a, dkv0, dw_out_a16 = outs[0:3], outs[3:6], outs[6], outs[7], outs[8], outs[9], outs[10]
    bwd = [_attn_bwd(q, k, v, dos[g], lss[g], dds[g], g) for g in range(3)]
    dproj_a = _qkv_bwd([b[0] for b in bwd], [b[1] for b in bwd], [b[2] for b in bwd], dqm, dz, rc, rs1, rs2)
    dw_in_a, dw_in_a16 = _w_in_grad(hn0, dproj_a, IN_A, "w_in_a_grad")
    dwkv, dwkv16, dmg = _mem_bwd(mem[0], mem_norm_g, memn, wkv_f, dkv0, dkv1)

    gs_a = [dwkv, dw_in_a.reshape(4, 1, D, SH_A), dw_out_a.reshape(4, 1, BR_A, SH_O)]
    r1_a = _pair_exchange([dwkv16, dw_in_a16.reshape(4, 1, D, SH_A), dw_out_a16.reshape(4, 1, BR_A, SH_O)], LAYER_A,
                          "pair_exchange_a", 1)
    ps_a = _pair_sums(place, gs_a, r1_a, LAYER_A, "pair_sums_a")
    ca_send, ca_recv, ca_src, ca_land, ca_token = _chip_start(ps_a, LAYER_A, "chip_a_start", 8)

    gx, dg0 = _in_proj_bwd(dproj_a, w_in_a, xs, g0, dh1, ca_token, IN_A, "in_proj_a_bwd")
    pack = jnp.concatenate([_pad_rows(jnp.concatenate([dg0, dg1], axis=0), 8), _pad_rows(dmg, 8), _pad_rows(dfg, 8),
                            dcw, _pad_rows(jnp.pad(loss_part, ((0, 0), (0, D - 128))), 8)], axis=0)
    sm_send, sm_recv, pack, sm_land, sm_token = _small_start(pack, ca_token)
    r2_b = _chip_wait(cb_send, cb_recv, cb_src, cb_land, [ca_token], LAYER_B, "chip_b_wait")
    hs_b = _chip_sums(place, gs_b, r1_b, r2_b, LAYER_B, "chip_sums_b")
    g_b = _pair_gather(hs_b, LAYER_B, "pair_gather_b", 2)
    upd_b = _adamw_layer([w_big[w] for w in LAYER_B], g_b, [m_big[w] for w in LAYER_B], [v_big[w] for w in LAYER_B],
                         LAYER_B, "adamw_b")
    r2_a = _chip_wait(ca_send, ca_recv, ca_src, ca_land, [gx, upd_b[0][0], upd_b[1][0], sm_token], LAYER_A,
                      "chip_a_wait")
    hs_a = _chip_sums(place, gs_a, r1_a, r2_a, LAYER_A, "chip_sums_a")
    g_a = _pair_gather(hs_a, LAYER_A, "pair_gather_a", 3)
    upd_a = _adamw_layer([w_big[w] for w in LAYER_A], g_a, [m_big[w] for w in LAYER_A], [v_big[w] for w in LAYER_A],
                         LAYER_A, "adamw_a")
    upd = upd_a + upd_b
    g_big = [u[3] for u in upd]
    pack, sm_land = _small_wait(sm_send, sm_recv, pack, sm_land, [r2_a[0]])
    sw = [norm_g, mem_norm_g, final_g.reshape(1, D), conv_w[0]]
    sm = [m_norm_g, m_mem_norm_g, m_final_g.reshape(1, D), m_conv_w[0]]
    sv = [v_norm_g, v_mem_norm_g, v_final_g.reshape(1, D), v_conv_w[0]]
    loss_row, sg, sd, snm, snv = _small_update(place, pack, sm_land, sw, sm, sv,
                                               [norm_g.shape, mem_norm_g.shape, final_g.shape, (3, 1, SH_O)])
    loss = loss_row[0, 0]
    g_norm, g_memnorm, g_final, g_conv = sg

    def order(norm, memnorm, wkv, w_in_a, w_out_a, w_in_b, conv, w_out_b, final):
        return (norm, memnorm, wkv, w_in_a, w_out_a, w_in_b, conv.transpose(1, 0, 2), w_out_b, final)

    grads = order(g_norm, g_memnorm, g_big[0], g_big[1], g_big[2], g_big[3], g_conv, g_big[4], g_final)
    deltas = order(sd[0], sd[1], upd[0][0], upd[1][0], upd[2][0], upd[3][0], sd[3], upd[4][0], sd[2])
    new_m = order(snm[0], snm[1], upd[0][1], upd[1][1], upd[2][1], upd[3][1], snm[3], upd[4][1], snm[2])
    new_v = order(snv[0], snv[1], upd[0][2], upd[1][2], upd[2][2], upd[3][2], snv[3], upd[4][2], snv[2])
    return (loss, gx[None], *grads, *deltas, *new_m, *new_v)
```

```python
import functools

import numpy as np
import jax
import jax.numpy as jnp
from jax import lax
from jax.experimental import pallas as pl
from jax.experimental.pallas import tpu as pltpu

F32 = jnp.float32
BF16 = jnp.bfloat16

S = 2048
D = 1024
TM = 256
NT = S // TM
MX = 512
NX = S // MX
HD = 64
GW = 512
NQ = 3 * GW
MW = 256
NM = 256
IN_A = 3 * NQ + MW + GW + MW
IN_B = 3 * D + MW + D + MW
BR_A = GW + MW
BR_B = D + MW
SH_A = IN_A // 4
SH_B = IN_B // 4
SH_O = D // 4
QBLK = 128
DILATIONS = (1, 4, 16)
EPS = 1e-6
SCALE = HD ** -0.5
NEG = -1e30
ROPE_THETA = 500000.0

ADAM_LR = 0.001
ADAM_B1 = 0.9
ADAM_B2 = 0.999
ADAM_EPS = 1e-08
ADAM_WD = 0.01
ADAM_STEP = 10

VMEM_LIMIT_BYTES = 60 * 1024 * 1024


def _params(sem=None):
    if sem is None:
        return pltpu.CompilerParams(vmem_limit_bytes=VMEM_LIMIT_BYTES)
    return pltpu.CompilerParams(dimension_semantics=sem, vmem_limit_bytes=VMEM_LIMIT_BYTES)


def _full(shape):
    nd = len(shape)
    return pl.BlockSpec(shape, lambda *_: (0,) * nd)


def _rows(width, tm=TM):
    return pl.BlockSpec((tm, width), lambda i: (i, 0))


def _sds(shape, dtype):
    return jax.ShapeDtypeStruct(shape, dtype)


def _silu_parts(z):
    sig = 0.5 * jnp.tanh(0.5 * z) + 0.5
    return z * sig, sig * (1.0 + z * (1.0 - sig))


def _dot(a, b):
    return jnp.dot(a, b, preferred_element_type=F32)


def _dot_nt(a, b):
    return lax.dot_general(a, b, (((1,), (1,)), ((), ())), preferred_element_type=F32)


def _dot_tn(a, b):
    return lax.dot_general(a, b, (((0,), (0,)), ((), ())), preferred_element_type=F32)


def _rope_fwd(t, c, s1, s2):
    return t * c + pltpu.roll(t, 120, 1) * s1 + pltpu.roll(t, 8, 1) * s2


def _rope_bwd(g, c, s1, s2):
    return g * c + pltpu.roll(g * s1, 8, 1) + pltpu.roll(g * s2, 120, 1)


MEM_HEADS = MW // HD


def _stack_heads(x):
    head = lax.broadcasted_iota(jnp.int32, x.shape, 1) // HD
    return jnp.concatenate([jnp.where(head == h, x, 0.0) for h in range(MEM_HEADS)], axis=0).astype(BF16)


def _unstack_heads(x4):
    tm = x4.shape[0] // MEM_HEADS
    head = lax.broadcasted_iota(jnp.int32, (tm, MW), 1) // HD
    out = x4[:tm]
    for h in range(1, MEM_HEADS):
        out = jnp.where(head == h, x4[h * tm:(h + 1) * tm], out)
    return out


def _mem_attn(qm, kv):
    q4 = _stack_heads(qm.astype(F32))
    s = _dot_nt(q4, kv[:, :MW]) * SCALE
    e = jnp.exp(s - jnp.max(s, axis=-1, keepdims=True))
    p = e * (1.0 / jnp.sum(e, axis=-1, keepdims=True))
    return p, _unstack_heads(_dot(p.astype(BF16), kv[:, MW:])), q4


def _mem_attn_bwd(dmo, p, mo, q4, kv, dkv_ref):
    tm = dmo.shape[0]
    head = lax.broadcasted_iota(jnp.int32, dmo.shape, 1) // HD
    prod = dmo * mo
    delta = jnp.concatenate([jnp.sum(jnp.where(head == h, prod, 0.0), axis=-1, keepdims=True)
                             for h in range(MEM_HEADS)], axis=0)
    d4 = _stack_heads(dmo)
    ds = (p * (_dot_nt(d4, kv[:, MW:]) - delta) * SCALE).astype(BF16)
    dkv_ref[:, :MW] += _dot_tn(ds, q4)
    dkv_ref[:, MW:] += _dot_tn(p.astype(BF16), d4)
    return _unstack_heads(_dot(ds, kv[:, :MW]))


def _merge(o_refs, l_refs):
    ls = [r[...] for r in l_refs]
    m = jnp.maximum(jnp.maximum(ls[0], ls[1]), ls[2])
    es = [jnp.exp(l - m) for l in ls]
    inv = 1.0 / (es[0] + es[1] + es[2])
    ws = [e * inv for e in es]
    os_ = [r[...] for r in o_refs]
    mix = ws[0] * os_[0] + ws[1] * os_[1] + ws[2] * os_[2]
    return ws, mix


def _conv_taps(cg, u, cgp, up, first):
    a = cg * u
    ap = jnp.where(first, 0.0, cgp * up)
    row = lax.broadcasted_iota(jnp.int32, a.shape, 0)
    a1 = jnp.where(row == 0, ap[7:8, :], pltpu.roll(a, 1, 0))
    a2 = jnp.where(row == 0, ap[6:7, :], jnp.where(row == 1, ap[7:8, :], pltpu.roll(a, 2, 0)))
    return a, a1, a2


def _rope_tables(posf, after):
    half = 8
    invf = np.float32(ROPE_THETA) ** (-np.arange(half, dtype=np.float32) * np.float32(2.0 / 16))
    lane = np.arange(128)
    table = np.where((lane % HD) < 16, invf[lane % half], 0.0).astype(np.float32)[None, :]

    def body(pos_ref, invf_ref, c_ref, s1_ref, s2_ref):
        ang = pos_ref[...] * invf_ref[...]
        jm = lax.broadcasted_iota(jnp.int32, ang.shape, 1) & (HD - 1)
        cs = jnp.cos(ang)
        sn = jnp.sin(ang)
        c_ref[...] = jnp.where(jm < 16, cs, 1.0)
        s1_ref[...] = jnp.where(jm < 8, -sn, 0.0)
        s2_ref[...] = jnp.where((jm >= 8) & (jm < 16), sn, 0.0)

    out = _sds((S, 128), F32)
    return pl.pallas_call(
        functools.partial(_skip_arg, body, 2), name="rope_tables", grid=(NT,),
        in_specs=[_rows(1), _full((1, 128)), pl.BlockSpec(memory_space=pl.ANY)],
        out_specs=[_rows(128)] * 3, out_shape=[out] * 3,
        compiler_params=_params(("parallel",)),
    )(posf, jnp.asarray(table), after)


def _in_proj_a(x, g0, w_in, c, s1, s2, after):
    def body(x_ref, g_ref, w_ref, c_ref, s1_ref, s2_ref, hn_ref, q_ref, k_ref, v_ref, qm_ref, z_ref, proj):
        xf = x_ref[...]
        hn = xf * lax.rsqrt(jnp.mean(xf * xf, axis=-1, keepdims=True) + EPS) * g_ref[...]
        hb = hn.astype(BF16)
        hn_ref[...] = hb
        for s in range(4):
            proj[:, s * SH_A:(s + 1) * SH_A] = _dot(hb, w_ref[s])
        cc, a1, a2 = c_ref[...], s1_ref[...], s2_ref[...]
        for j in range(NQ // 128):
            q_ref[:, j * 128:(j + 1) * 128] = (
                _rope_fwd(proj[:, j * 128:(j + 1) * 128], cc, a1, a2) * SCALE).astype(BF16)
            k_ref[:, j * 128:(j + 1) * 128] = _rope_fwd(
                proj[:, NQ + j * 128:NQ + (j + 1) * 128], cc, a1, a2).astype(BF16)
        v_ref[...] = proj[:, 2 * NQ:3 * NQ].astype(BF16)
        qm_ref[...] = proj[:, 3 * NQ:3 * NQ + MW].astype(BF16)
        z_ref[...] = proj[:, 3 * NQ + MW:]

    return pl.pallas_call(
        functools.partial(_skip_arg, body, 6), name="in_proj_a", grid=(NT,),
        in_specs=[_rows(D), _full((1, D)), _full((4, D, SH_A)), _rows(128), _rows(128), _rows(128),
                  pl.BlockSpec(memory_space=pl.ANY)],
        out_specs=[_rows(D), _rows(NQ), _rows(NQ), _rows(NQ), _rows(MW), _rows(BR_A)],
        out_shape=[_sds((S, D), BF16), _sds((S, NQ), BF16), _sds((S, NQ), BF16), _sds((S, NQ), BF16),
                   _sds((S, MW), BF16), _sds((S, BR_A), F32)],
        scratch_shapes=[pltpu.VMEM((TM, IN_A), F32)],
        compiler_params=_params(("parallel",)),
    )(x, g0, w_in, c, s1, s2, after)


def _mem_fwd(mem, mg, wkv):
    def body(mem_ref, mg_ref, w_ref, memn_ref, *kv_refs):
        mf = mem_ref[...]
        n = mf * lax.rsqrt(jnp.mean(mf * mf, axis=-1, keepdims=True) + EPS)
        for i in range(2):
            mn = (n * mg_ref[i:i + 1, :]).astype(BF16)
            memn_ref[i] = mn
            acc = _dot(mn[:, 0:NM], w_ref[0, i])
            for s in range(1, 4):
                acc += _dot(mn[:, s * NM:(s + 1) * NM], w_ref[s, i])
            kv_refs[i][...] = acc.astype(BF16)

    memn, kv0, kv1 = pl.pallas_call(
        body, name="mem_fwd", grid=(1,),
        in_specs=[_full((NM, D)), _full((2, D)), _full((4, 2, NM, 2 * MW))],
        out_specs=[_full((2, NM, D)), _full((NM, 2 * MW)), _full((NM, 2 * MW))],
        out_shape=[_sds((2, NM, D), BF16), _sds((NM, 2 * MW), BF16), _sds((NM, 2 * MW), BF16)],
        compiler_params=_params(("arbitrary",)),
    )(mem, mg, wkv)
    return memn, (kv0, kv1)


def _band_mask(j):
    qi = lax.broadcasted_iota(jnp.int32, (QBLK, 2 * QBLK), 0)
    kj = lax.broadcasted_iota(jnp.int32, (QBLK, 2 * QBLK), 1)
    dist = qi + QBLK - kj
    return (dist >= 0) & (dist <= QBLK) & ((kj >= QBLK) | (j > 0))


LANES = 128
NCHUNK = GW // LANES
FWD_UNROLL = 16
BWD_UNROLL = 16
CONV_CHUNK = 256


def _perm_matrix(d):
    n = TM // d
    p = np.zeros((TM, TM), np.float32)
    for r in range(d):
        for i in range(n):
            p[r * n + i, i * d + r] = 1.0
    return p


def _split_dot(p, x):
    hi = x.astype(BF16)
    lo = (x - hi.astype(F32)).astype(BF16)
    both = _dot(p, jnp.concatenate([hi, lo], axis=1))
    return both[:, :LANES] + both[:, LANES:]


def _pair_dot(p, a, b):
    both = _dot(p, jnp.concatenate([a, b], axis=1))
    return both[:, :LANES], both[:, LANES:]


def _tile_to_streams(y, dst, t, d):
    n, ln = TM // d, S // d
    for r in range(d):
        dst[r * ln + t * n:r * ln + (t + 1) * n, :] = y[r * n:(r + 1) * n].astype(dst.dtype)


def _tile_from_streams(src, t, d):
    n, ln = TM // d, S // d
    return jnp.concatenate([src[r * ln + t * n:r * ln + (t + 1) * n, :] for r in range(d)], axis=0)


def _head_masks():
    first = lax.broadcasted_iota(jnp.int32, (TM, LANES), 1) < HD
    return first, jnp.logical_not(first)


def _attn_fwd(q, k, v, g, after):
    d = DILATIONS[g]
    nb = S // d // QBLK
    perm = _perm_matrix(d)

    def body(q_ref, k_ref, v_ref, p_ref, pt_ref, o_ref, l_ref, ls_ref, q0, q1, ks, vs, os_):
        first, second = _head_masks()
        pm = p_ref[...]
        for t in range(NT):
            rows = slice(t * TM, (t + 1) * TM)
            if d == 1:
                qt = q_ref[rows, :].astype(F32)
            else:
                qt, kt = _pair_dot(pm, q_ref[rows, :], k_ref[rows, :])
                _tile_to_streams(kt, ks, t, d)
                if t % 2 == 0:
                    va, vb = _pair_dot(pm, v_ref[rows, :], v_ref[(t + 1) * TM:(t + 2) * TM, :])
                    _tile_to_streams(va, vs, t, d)
                    _tile_to_streams(vb, vs, t + 1, d)
            _tile_to_streams(jnp.where(first, qt, 0.0), q0, t, d)
            _tile_to_streams(jnp.where(second, qt, 0.0), q1, t, d)
        kref, vref = (k_ref, v_ref) if d == 1 else (ks, vs)
        oref, lref = (o_ref, l_ref) if d == 1 else (os_, ls_ref)

        def blk(b, carry):
            r0 = pl.multiple_of(b * QBLK, QBLK)
            p0 = pl.multiple_of(jnp.maximum(b - 1, 0) * QBLK, QBLK)
            kk = jnp.concatenate([kref[pl.ds(p0, QBLK), :], kref[pl.ds(r0, QBLK), :]], axis=0)
            vv = jnp.concatenate([vref[pl.ds(p0, QBLK), :], vref[pl.ds(r0, QBLK), :]], axis=0)
            valid = _band_mask(b & (nb - 1))
            acc, lse = [], []
            for qh in (q0, q1):
                s = jnp.where(valid, _dot_nt(qh[pl.ds(r0, QBLK), :], kk), NEG)
                m = jnp.max(s, axis=-1, keepdims=True)
                e = jnp.exp(s - m)
                l = jnp.sum(e, axis=-1, keepdims=True)
                acc.append(_dot(e.astype(BF16), vv) * (1.0 / l))
                lse.append(m + jnp.log(l))
            f = first[:QBLK]
            oref[pl.ds(r0, QBLK), :] = jnp.where(f, acc[0], acc[1])
            lref[pl.ds(r0, QBLK), :] = jnp.where(f, lse[0], lse[1])
            return carry

        lax.fori_loop(0, S // QBLK, blk, 0, unroll=FWD_UNROLL)
        if d > 1:
            ptm = pt_ref[...]
            for t in range(NT):
                rows = slice(t * TM, (t + 1) * TM)
                o_ref[rows, :] = _split_dot(ptm, _tile_from_streams(os_, t, d))
                l_ref[rows, :] = _split_dot(ptm, _tile_from_streams(ls_ref, t, d))

    qkv_spec = pl.BlockSpec((S, LANES), lambda c: (0, g * NCHUNK + c))
    out_spec = pl.BlockSpec((S, LANES), lambda c: (0, c))
    n_out = 2 if d == 1 else 3
    inner = body if d > 1 else functools.partial(_drop_arg, body, 7)
    outs = pl.pallas_call(
        functools.partial(_skip_arg, inner, 5), name=f"attn_fwd_g{g}", grid=(NCHUNK,),
        in_specs=[qkv_spec] * 3 + [_full((TM, TM))] * 2 + [pl.BlockSpec(memory_space=pl.ANY)],
        out_specs=[out_spec] * n_out, out_shape=[_sds((S, GW), F32)] * n_out,
        scratch_shapes=[pltpu.VMEM((S, LANES), BF16)] * 4 + [pltpu.VMEM((S, LANES), F32)],
        compiler_params=_params(("parallel",)),
    )(q, k, v, jnp.asarray(perm, BF16), jnp.asarray(perm.T, BF16), after)
    return (outs[0], outs[1], outs[1]) if d == 1 else tuple(outs)


def _drop_arg(body, pos, *refs):
    return body(*refs[:pos], None, *refs[pos:])


def _attn_out(os_, ls, qm, kv0, z, x, w_out):
    def body(o0, o1, o2, l0, l1, l2, qm_ref, kv_ref, z_ref, x_ref, w_ref, h_ref, ybuf):
        _, mix = _merge((o0, o1, o2), (l0, l1, l2))
        sz, _ = _silu_parts(z_ref[...])
        ybuf[:, :GW] = (mix * sz[:, :GW]).astype(BF16)
        _, mo, _ = _mem_attn(qm_ref[...], kv_ref[...])
        ybuf[:, GW:] = (mo * sz[:, GW:]).astype(BF16)
        yb = ybuf[...]
        for s in range(4):
            cs = slice(s * SH_O, (s + 1) * SH_O)
            h_ref[:, cs] = x_ref[:, cs] + _dot(yb, w_ref[s])

    return pl.pallas_call(
        body, name="attn_out", grid=(NX,),
        in_specs=[_rows(GW, MX)] * 6 + [_rows(MW, MX), _full((NM, 2 * MW)), _rows(BR_A, MX), _rows(D, MX),
                                        _full((4, BR_A, SH_O))],
        out_specs=_rows(D, MX), out_shape=_sds((S, D), F32),
        scratch_shapes=[pltpu.VMEM((MX, BR_A), BF16)],
        compiler_params=_params(("parallel",)),
    )(*os_, *ls, qm, kv0, z, x, w_out)


def _in_proj_b(h1, g1, w_in):
    def body(x_ref, g_ref, w_ref, hn_ref, bg_ref, cg_ref, u_ref, qm_ref, z_ref, proj):
        xf = x_ref[...]
        hn = xf * lax.rsqrt(jnp.mean(xf * xf, axis=-1, keepdims=True) + EPS) * g_ref[...]
        hb = hn.astype(BF16)
        hn_ref[...] = hb
        for s in range(4):
            proj[:, s * SH_B:(s + 1) * SH_B] = _dot(hb, w_ref[s])
        bg_ref[...] = proj[:, :D]
        cg_ref[...] = proj[:, D:2 * D]
        u_ref[...] = proj[:, 2 * D:3 * D]
        qm_ref[...] = proj[:, 3 * D:3 * D + MW].astype(BF16)
        z_ref[...] = proj[:, 3 * D + MW:]

    return pl.pallas_call(
        body, name="in_proj_b", grid=(NT,),
        in_specs=[_rows(D), _full((1, D)), _full((4, D, SH_B))],
        out_specs=[_rows(D), _rows(D), _rows(D), _rows(D), _rows(MW), _rows(BR_B)],
        out_shape=[_sds((S, D), BF16), _sds((S, D), F32), _sds((S, D), F32), _sds((S, D), F32),
                   _sds((S, MW), BF16), _sds((S, BR_B), F32)],
        scratch_shapes=[pltpu.VMEM((TM, IN_B), F32)],
        compiler_params=_params(("parallel",)),
    )(h1, g1, w_in)


def _prev8(width):
    return pl.BlockSpec((8, width), lambda i: (jnp.maximum(i * (MX // 8) - 1, 0), 0))


def _conv_out_loss(bg, cg, u, cw, qm, kv1, z, h1, w_out, fg, tgt):
    def body(bg_ref, cg_ref, u_ref, cgp_ref, up_ref, cw_ref, qm_ref, kv_ref, z_ref, h_ref, w_ref, fg_ref, t_ref,
             dh_ref, loss_ref, dfg_ref, ybuf):
        i = pl.program_id(0)
        a, a1, a2 = _conv_taps(cg_ref[...], u_ref[...], cgp_ref[...], up_ref[...], i == 0)
        conv = cw_ref[0:1, :] * a2 + cw_ref[1:2, :] * a1 + cw_ref[2:3, :] * a
        sz, _ = _silu_parts(z_ref[...])
        ybuf[:, :D] = (bg_ref[...] * conv * sz[:, :D]).astype(BF16)
        _, mo, _ = _mem_attn(qm_ref[...], kv_ref[...])
        ybuf[:, D:] = (mo * sz[:, D:]).astype(BF16)
        h2 = h_ref[...] + _dot(ybuf[...], w_ref[...])
        rstd = lax.rsqrt(jnp.mean(h2 * h2, axis=-1, keepdims=True) + EPS)
        n = h2 * rstd
        fgv = fg_ref[...]
        err = n * fgv - t_ref[...]
        dout = err * (1.0 / D)
        dn = dout * fgv
        dh_ref[...] = rstd * (dn - n * jnp.mean(dn * n, axis=-1, keepdims=True))

        @pl.when(i == 0)
        def _():
            loss_ref[...] = jnp.zeros_like(loss_ref)
            dfg_ref[...] = jnp.zeros_like(dfg_ref)

        loss_ref[...] += jnp.sum(err * err) * (0.5 / D)
        dfg_ref[...] += jnp.sum(dout * n, axis=0, keepdims=True)

    return pl.pallas_call(
        body, name="conv_out_loss", grid=(NX,),
        in_specs=[_rows(D, MX), _rows(D, MX), _rows(D, MX), _prev8(D), _prev8(D), _full((8, D)), _rows(MW, MX),
                  _full((NM, 2 * MW)), _rows(BR_B, MX), _rows(D, MX), _full((BR_B, D)), _full((1, D)), _rows(D, MX)],
        out_specs=[_rows(D, MX), _full((1, 128)), _full((1, D))],
        out_shape=[_sds((S, D), F32), _sds((1, 128), F32), _sds((1, D), F32)],
        scratch_shapes=[pltpu.VMEM((MX, BR_B), BF16)],
        compiler_params=_params(("arbitrary",)),
    )(bg, cg, u, cg, u, cw, qm, kv1, z, h1, w_out, fg, tgt)


def _conv_bwd(dh2, bg, cg, u, cw, qm, kv1, z, w_out):
    rev = lambda i: (NX - 1 - i, 0)
    rows = lambda w: pl.BlockSpec((MX, w), rev)
    prev8 = pl.BlockSpec((8, D), lambda i: (jnp.maximum((NX - 1 - i) * (MX // 8) - 1, 0), 0))

    def body(dh_ref, bg_ref, cg_ref, u_ref, cgp_ref, up_ref, cw_ref, qm_ref, kv_ref, z_ref, w_ref,
             dproj_ref, dw_ref, dcw_ref, dkv_ref, dwb_ref, ybuf, carry):
        i = pl.program_id(0)

        @pl.when(i == 0)
        def _():
            dw_ref[...] = jnp.zeros_like(dw_ref)
            dcw_ref[...] = jnp.zeros_like(dcw_ref)
            dkv_ref[...] = jnp.zeros_like(dkv_ref)
            carry[...] = jnp.zeros_like(carry)

        dhb = dh_ref[...].astype(BF16)
        dy = _dot_nt(dhb, w_ref[...])
        kvv = kv_ref[...]
        p, mo, q4 = _mem_attn(qm_ref[...], kvv)
        szm, dszm = _silu_parts(z_ref[:, D:])
        ybuf[:, D:] = (mo * szm).astype(BF16)
        dym = dy[:, D:]
        dproj_ref[:, 3 * D + MW + D:] = (dym * mo * dszm).astype(BF16)
        first_tile = i == NX - 1
        for c in range(D // CONV_CHUNK):
            cs = slice(c * CONV_CHUNK, (c + 1) * CONV_CHUNK)
            bgv, cgv, uv = bg_ref[:, cs], cg_ref[:, cs], u_ref[:, cs]
            a, a1, a2 = _conv_taps(cgv, uv, cgp_ref[:, cs], up_ref[:, cs], first_tile)
            w0, w1, w2 = cw_ref[0:1, cs], cw_ref[1:2, cs], cw_ref[2:3, cs]
            conv = w0 * a2 + w1 * a1 + w2 * a
            mix = bgv * conv
            sz, dsz = _silu_parts(z_ref[:, cs])
            ybuf[:, cs] = (mix * sz).astype(BF16)
            dyc = dy[:, cs]
            dproj_ref[:, 3 * D + MW + c * CONV_CHUNK:3 * D + MW + (c + 1) * CONV_CHUNK] = (
                dyc * mix * dsz).astype(BF16)
            dmix = dyc * sz
            dproj_ref[:, cs] = (dmix * conv).astype(BF16)
            dc = dmix * bgv
            nxt = carry[:, cs]
            row = lax.broadcasted_iota(jnp.int32, dc.shape, 0)
            dc1 = jnp.where(row == MX - 1, nxt[0:1, :], pltpu.roll(dc, MX - 1, 0))
            dc2 = jnp.where(row == MX - 2, nxt[0:1, :],
                            jnp.where(row == MX - 1, nxt[1:2, :], pltpu.roll(dc, MX - 2, 0)))
            carry[:, cs] = dc[0:8, :]
            da = w2 * dc + w1 * dc1 + w0 * dc2
            dproj_ref[:, D + c * CONV_CHUNK:D + (c + 1) * CONV_CHUNK] = (da * uv).astype(BF16)
            dproj_ref[:, 2 * D + c * CONV_CHUNK:2 * D + (c + 1) * CONV_CHUNK] = (da * cgv).astype(BF16)
            dcw_ref[0:1, cs] += jnp.sum(dc * a2, axis=0, keepdims=True)
            dcw_ref[1:2, cs] += jnp.sum(dc * a1, axis=0, keepdims=True)
            dcw_ref[2:3, cs] += jnp.sum(dc * a, axis=0, keepdims=True)
        dw_ref[...] += _dot_tn(ybuf[...], dhb)
        dproj_ref[:, 3 * D:3 * D + MW] = _mem_attn_bwd(dym * szm, p, mo, q4, kvv, dkv_ref).astype(BF16)

        @pl.when(i == NX - 1)
        def _():
            dwb_ref[...] = dw_ref[...].astype(BF16)

    return pl.pallas_call(
        body, name="conv_bwd", grid=(NX,),
        in_specs=[rows(D), rows(D), rows(D), rows(D), prev8, prev8, _full((8, D)), rows(MW),
                  _full((NM, 2 * MW)), rows(BR_B), _full((BR_B, D))],
        out_specs=[rows(IN_B), _full((BR_B, D)), _full((8, D)), _full((NM, 2 * MW)), _full((BR_B, D))],
        out_shape=[_sds((S, IN_B), BF16), _sds((BR_B, D), F32), _sds((8, D), F32), _sds((NM, 2 * MW), F32),
                   _sds((BR_B, D), BF16)],
        scratch_shapes=[pltpu.VMEM((MX, BR_B), BF16), pltpu.VMEM((8, D), F32)],
        compiler_params=_params(("arbitrary",)),
    )(dh2, bg, cg, u, cg, u, cw, qm, kv1, z, w_out)


def _in_proj_bwd(dproj, w_in, xin, g, dres, after, width, name):
    sh = width // 4

    def body(dp_ref, w_ref, x_ref, g_ref, dr_ref, dx_ref, dg_ref):
        i = pl.program_id(0)
        dhn = _dot_nt(dp_ref[:, 0:sh], w_ref[0])
        for s in range(1, 4):
            dhn += _dot_nt(dp_ref[:, s * sh:(s + 1) * sh], w_ref[s])
        xf = x_ref[...]
        rstd = lax.rsqrt(jnp.mean(xf * xf, axis=-1, keepdims=True) + EPS)
        n = xf * rstd
        dn = dhn * g_ref[...]
        dx_ref[...] = dr_ref[...] + rstd * (dn - n * jnp.mean(dn * n, axis=-1, keepdims=True))

        @pl.when(i == 0)
        def _():
            dg_ref[...] = jnp.zeros_like(dg_ref)

        dg_ref[...] += jnp.sum(dhn * n, axis=0, keepdims=True)

    return pl.pallas_call(
        functools.partial(_skip_arg, body, 5), name=name, grid=(NT,),
        in_specs=[_rows(width), _full((4, D, sh)), _rows(D), _full((1, D)), _rows(D), pl.BlockSpec(memory_space=pl.ANY)],
        out_specs=[_rows(D), _full((1, D))],
        out_shape=[_sds((S, D), F32), _sds((1, D), F32)],
        compiler_params=_params(("arbitrary",)),
    )(dproj, w_in, xin, g, dres, after)


def _w_in_grad(hn, dproj, width, name):
    sh = width // 4

    def body(hn_ref, dp_ref, dw_ref, dwb_ref):
        dw = _dot_tn(hn_ref[...], dp_ref[...])
        dw_ref[0] = dw
        dwb_ref[0] = dw.astype(BF16)

    spec = pl.BlockSpec((1, D, sh), lambda s: (s, 0, 0))
    return pl.pallas_call(
        body, name=name, grid=(4,),
        in_specs=[_full((S, D)), pl.BlockSpec((S, sh), lambda s: (0, s))],
        out_specs=[spec, spec], out_shape=[_sds((4, D, sh), F32), _sds((4, D, sh), BF16)],
        compiler_params=_params(("parallel",)),
    )(hn, dproj)


def _attn_out_bwd(dh1, os_, ls, qm, kv0, z, w_out, after):
    ones_bd = np.kron(np.eye(GW // HD, dtype=np.float32), np.ones((HD, HD), np.float32))

    def body(dh_ref, o0, o1, o2, l0, l1, l2, qm_ref, kv_ref, z_ref, w_ref, bd_ref,
             do0, do1, do2, dd0, dd1, dd2, dqm_ref, dz_ref, dw_ref, dkv_ref, dwb_ref, ybuf):
        i = pl.program_id(0)

        @pl.when(i == 0)
        def _():
            dw_ref[...] = jnp.zeros_like(dw_ref)
            dkv_ref[...] = jnp.zeros_like(dkv_ref)

        ws, mix = _merge((o0, o1, o2), (l0, l1, l2))
        sz, dsz = _silu_parts(z_ref[...])
        kvv = kv_ref[...]
        p, mo, q4 = _mem_attn(qm_ref[...], kvv)
        ybuf[:, :GW] = (mix * sz[:, :GW]).astype(BF16)
        ybuf[:, GW:] = (mo * sz[:, GW:]).astype(BF16)
        yb = ybuf[...]
        dh = dh_ref[...]
        dy = None
        for s in range(4):
            dhb = dh[:, s * SH_O:(s + 1) * SH_O].astype(BF16)
            dw_ref[s] += _dot_tn(yb, dhb)
            part = _dot_nt(dhb, w_ref[s])
            dy = part if dy is None else dy + part
        dcat = dy * sz
        dz_ref[:, :GW] = (dy[:, :GW] * mix * dsz[:, :GW]).astype(BF16)
        dz_ref[:, GW:] = (dy[:, GW:] * mo * dsz[:, GW:]).astype(BF16)
        dmix = dcat[:, :GW]
        prod = dmix * mix
        hi = prod.astype(BF16)
        lo = (prod - hi.astype(F32)).astype(BF16)
        bd = bd_ref[...]
        tot = _dot(hi, bd) + _dot(lo, bd)
        for w, do_ref, dd_ref in zip(ws, (do0, do1, do2), (dd0, dd1, dd2)):
            do_ref[...] = (w * dmix).astype(BF16)
            dd_ref[...] = w * tot

        dqm_ref[...] = _mem_attn_bwd(dcat[:, GW:], p, mo, q4, kvv, dkv_ref).astype(BF16)

        @pl.when(i == NX - 1)
        def _():
            dwb_ref[...] = dw_ref[...].astype(BF16)

    return pl.pallas_call(
        functools.partial(_skip_arg, body, 12), name="attn_out_bwd", grid=(NX,),
        in_specs=[_rows(D, MX)] + [_rows(GW, MX)] * 6 + [_rows(MW, MX), _full((NM, 2 * MW)), _rows(BR_A, MX),
                                                           _full((4, BR_A, SH_O)), _full((GW, GW)),
                                                           pl.BlockSpec(memory_space=pl.ANY)],
        out_specs=[_rows(GW, MX)] * 6 + [_rows(MW, MX), _rows(BR_A, MX), _full((4, BR_A, SH_O)),
                                         _full((NM, 2 * MW)), _full((4, BR_A, SH_O))],
        out_shape=[_sds((S, GW), BF16)] * 3 + [_sds((S, GW), F32)] * 3 + [
            _sds((S, MW), BF16), _sds((S, BR_A), BF16), _sds((4, BR_A, SH_O), F32), _sds((NM, 2 * MW), F32),
            _sds((4, BR_A, SH_O), BF16)],
        scratch_shapes=[pltpu.VMEM((MX, BR_A), BF16)],
        compiler_params=_params(("arbitrary",)),
    )(dh1, *os_, *ls, qm, kv0, z, w_out, jnp.asarray(ones_bd, dtype=BF16), after)


def _attn_bwd(q, k, v, do, lse_s, dd, g):
    d = DILATIONS[g]
    nb = S // d // QBLK
    perm = _perm_matrix(d)

    def body(q_ref, k_ref, v_ref, do_ref, l_ref, dd_ref, p_ref, pt_ref, dq_ref, dk_ref, dv_ref,
             q0, q1, g0, g1, ks, vs, dds, dqs, dks, dvs):
        first, second = _head_masks()
        pm = p_ref[...]
        for t in range(NT):
            rows = slice(t * TM, (t + 1) * TM)
            if d == 1:
                qt = q_ref[rows, :].astype(F32)
                gt = do_ref[rows, :].astype(F32)
            else:
                qt, gt = _pair_dot(pm, q_ref[rows, :], do_ref[rows, :])
                kt, vt = _pair_dot(pm, k_ref[rows, :], v_ref[rows, :])
                _tile_to_streams(kt, ks, t, d)
                _tile_to_streams(vt, vs, t, d)
                _tile_to_streams(_split_dot(pm, dd_ref[rows, :]), dds, t, d)
            _tile_to_streams(jnp.where(first, qt, 0.0), q0, t, d)
            _tile_to_streams(jnp.where(second, qt, 0.0), q1, t, d)
            _tile_to_streams(jnp.where(first, gt, 0.0), g0, t, d)
            _tile_to_streams(jnp.where(second, gt, 0.0), g1, t, d)
        kref, vref, ddref = (k_ref, v_ref, dd_ref) if d == 1 else (ks, vs, dds)
        dqref, dkref, dvref = dqs, dks, dvs
        dkref[...] = jnp.zeros_like(dkref)
        dvref[...] = jnp.zeros_like(dvref)

        def blk(b, carry):
            r0 = pl.multiple_of(b * QBLK, QBLK)
            p0 = pl.multiple_of(jnp.maximum(b - 1, 0) * QBLK, QBLK)
            kk = jnp.concatenate([kref[pl.ds(p0, QBLK), :], kref[pl.ds(r0, QBLK), :]], axis=0)
            vv = jnp.concatenate([vref[pl.ds(p0, QBLK), :], vref[pl.ds(r0, QBLK), :]], axis=0)
            lb = l_ref[pl.ds(r0, QBLK), :]
            ddb = ddref[pl.ds(r0, QBLK), :]
            lcol = jnp.concatenate([lb[:, 0:1], lb[:, HD:HD + 1]], axis=0)
            dcol = jnp.concatenate([ddb[:, 0:1], ddb[:, HD:HD + 1]], axis=0)
            valid = _band_mask(b & (nb - 1))
            valid2 = jnp.concatenate([valid, valid], axis=0)
            qq = jnp.concatenate([q0[pl.ds(r0, QBLK), :], q1[pl.ds(r0, QBLK), :]], axis=0)
            gg = jnp.concatenate([g0[pl.ds(r0, QBLK), :], g1[pl.ds(r0, QBLK), :]], axis=0)
            p = jnp.where(valid2, jnp.exp(_dot_nt(qq, kk) - lcol), 0.0)
            ds = (p * (_dot_nt(gg, vv) - dcol)).astype(BF16)
            dq2 = _dot(ds, kk)
            dqref[pl.ds(r0, QBLK), :] = jnp.where(first[:QBLK], dq2[:QBLK], dq2[QBLK:])
            dkk = _dot_tn(ds, qq)
            dvv = _dot_tn(p.astype(BF16), gg)
            dkref[pl.ds(p0, QBLK), :] += dkk[:QBLK]
            dkref[pl.ds(r0, QBLK), :] += dkk[QBLK:]
            dvref[pl.ds(p0, QBLK), :] += dvv[:QBLK]
            dvref[pl.ds(r0, QBLK), :] += dvv[QBLK:]
            return carry

        lax.fori_loop(0, S // QBLK, blk, 0, unroll=BWD_UNROLL)

        ptm = pt_ref[...] if d > 1 else None
        for t in range(NT):
            rows = slice(t * TM, (t + 1) * TM)
            if d == 1:
                dq_ref[rows, :] = dqs[rows, :].astype(BF16)
                dk_ref[rows, :] = dks[rows, :].astype(BF16)
                dv_ref[rows, :] = dvs[rows, :].astype(BF16)
            else:
                tq, tk = _pair_dot(ptm, _tile_from_streams(dqs, t, d).astype(BF16),
                                   _tile_from_streams(dks, t, d).astype(BF16))
                dq_ref[rows, :] = tq.astype(BF16)
                dk_ref[rows, :] = tk.astype(BF16)
                if t % 2 == 0:
                    ta, tb = _pair_dot(ptm, _tile_from_streams(dvs, t, d).astype(BF16),
                                       _tile_from_streams(dvs, t + 1, d).astype(BF16))
                    dv_ref[rows, :] = ta.astype(BF16)
                    dv_ref[(t + 1) * TM:(t + 2) * TM, :] = tb.astype(BF16)

    qkv_spec = pl.BlockSpec((S, LANES), lambda c: (0, g * NCHUNK + c))
    one_spec = pl.BlockSpec((S, LANES), lambda c: (0, c))
    return pl.pallas_call(
        body, name=f"attn_bwd_g{g}", grid=(NCHUNK,),
        in_specs=[qkv_spec] * 3 + [one_spec] * 3 + [_full((TM, TM))] * 2, out_specs=[one_spec] * 3,
        out_shape=[_sds((S, GW), BF16)] * 3,
        scratch_shapes=[pltpu.VMEM((S, LANES), BF16)] * 6 + [pltpu.VMEM((S, LANES), F32)] * 4,
        compiler_params=_params(("parallel",)),
    )(q, k, v, do, lse_s, dd, jnp.asarray(perm, BF16), jnp.asarray(perm.T, BF16))


def _qkv_bwd(dqs, dks, dvs, dqm, dz, c, s1, s2):
    def body(q0, q1, q2, k0, k1, k2, v0, v1, v2, dqm_ref, dz_ref, c_ref, s1_ref, s2_ref, dp_ref):
        cc, a1, a2 = c_ref[...], s1_ref[...], s2_ref[...]
        for g, (qr, kr, vr) in enumerate(((q0, k0, v0), (q1, k1, v1), (q2, k2, v2))):
            for j in range(GW // 128):
                ls_ = slice(j * 128, (j + 1) * 128)
                c0 = g * GW + j * 128
                dp_ref[:, c0:c0 + 128] = (_rope_bwd(qr[:, ls_].astype(F32), cc, a1, a2) * SCALE).astype(BF16)
                dp_ref[:, NQ + c0:NQ + c0 + 128] = _rope_bwd(kr[:, ls_].astype(F32), cc, a1, a2).astype(BF16)
            dp_ref[:, 2 * NQ + g * GW:2 * NQ + (g + 1) * GW] = vr[...]
        dp_ref[:, 3 * NQ:3 * NQ + MW] = dqm_ref[...]
        dp_ref[:, 3 * NQ + MW:] = dz_ref[...]

    return pl.pallas_call(
        body, name="qkv_bwd", grid=(NT,),
        in_specs=[_rows(GW)] * 9 + [_rows(MW), _rows(BR_A), _rows(128), _rows(128), _rows(128)],
        out_specs=_rows(IN_A), out_shape=_sds((S, IN_A), BF16),
        compiler_params=_params(("parallel",)),
    )(*dqs, *dks, *dvs, dqm, dz, c, s1, s2)


def _mem_bwd(mem, mg, memn, wkv, dkv0, dkv1):
    def body(mem_ref, mg_ref, memn_ref, w_ref, d0_ref, d1_ref, dw_ref, dwb_ref, dg_ref):
        mf = mem_ref[...]
        n = mf * lax.rsqrt(jnp.mean(mf * mf, axis=-1, keepdims=True) + EPS)
        for i, d_ref in enumerate((d0_ref, d1_ref)):
            dkv = d_ref[...].astype(BF16)
            mn = memn_ref[i]
            for s in range(4):
                cs = slice(s * NM, (s + 1) * NM)
                dw = _dot_tn(mn[:, cs], dkv)
                dw_ref[s, i] = dw
                dwb_ref[s, i] = dw.astype(BF16)
                dmn = _dot_nt(dkv, w_ref[s, i])
                dg_ref[i:i + 1, cs] = jnp.sum(dmn * n[:, cs], axis=0, keepdims=True)

    return pl.pallas_call(
        body, name="mem_bwd", grid=(1,),
        in_specs=[_full((NM, D)), _full((2, D)), _full((2, NM, D)), _full((4, 2, NM, 2 * MW)),
                  _full((NM, 2 * MW)), _full((NM, 2 * MW))],
        out_specs=[_full((4, 2, NM, 2 * MW)), _full((4, 2, NM, 2 * MW)), _full((2, D))],
        out_shape=[_sds((4, 2, NM, 2 * MW), F32), _sds((4, 2, NM, 2 * MW), BF16), _sds((2, D), F32)],
        compiler_params=_params(("arbitrary",)),
    )(mem, mg, memn, wkv, dkv0, dkv1)


MESH = pl.DeviceIdType.MESH
ANY = pl.BlockSpec(memory_space=pl.ANY)
BIG = (("wkv", 2, NM, 2 * MW), ("w_in_a", 1, D, SH_A), ("w_out_a", 1, BR_A, SH_O),
       ("w_in_b", 1, D, SH_B), ("w_out_b", 1, BR_B // 4, D))
NBIG = len(BIG)
CW_ROWS = 8


def _place():
    x, y, c = lax.axis_index("x"), lax.axis_index("y"), lax.axis_index("c")
    chips = ((1 - x, y), (x, 1 - y), (1 - x, 1 - y))
    return x, y, c, chips


def _remote(src, dst, ssem, rsem, dev):
    return pltpu.make_async_remote_copy(src_ref=src, dst_ref=dst, send_sem=ssem, recv_sem=rsem,
                                        device_id=dev, device_id_type=MESH)


def _cast_weights(place, ws, after, idx, name):
    nblk = 4
    n = len(idx)
    dims = [BIG[w][1:] for w in idx]

    def body(pref, *refs):
        for i in range(n):
            refs[n + 1 + i][0] = refs[i][...].astype(BF16)

    grid_spec = pltpu.PrefetchScalarGridSpec(
        num_scalar_prefetch=1, grid=(nblk,),
        in_specs=[pl.BlockSpec((k, r // nblk, cdim), lambda i, pref: (0, i, 0)) for k, r, cdim in dims]
        + [pl.BlockSpec(memory_space=pl.ANY)],
        out_specs=[pl.BlockSpec((1, k, r // nblk, cdim), lambda i, pref: (pref[1], 0, i, 0)) for k, r, cdim in dims])
    return pl.pallas_call(
        body, name=name, grid_spec=grid_spec,
        out_shape=[_sds((4, k, r, cdim), BF16) for k, r, cdim in dims],
        compiler_params=_params(("parallel",)),
    )(place, *ws, after)


LAYER_A = (0, 1, 2)
LAYER_B = (3, 4)
HBM = pl.BlockSpec(memory_space=pltpu.HBM)
SEM = pl.BlockSpec(memory_space=pltpu.SEMAPHORE)
EFFECT = pltpu.SideEffectType.DATAFLOW_SIDE_EFFECTING
TOKEN = (8, 128)


def _half(ref, w, which):
    h = BIG[w][2] // 2
    return ref.at[:, pl.ds(which * h, h), :]


def _skip_arg(body, pos, *refs):
    return body(*refs[:pos], *refs[pos + 1:])


def _gather_start(wb, after, idx, name, barrier_id):
    n = len(idx)

    def body(*refs):
        src = refs[:n]
        send_sems, recv_sems = refs[n + 1], refs[n + 2]
        token = refs[2 * n + 3]
        x, y, c, chips = _place()
        _peer_barrier([(px, py, c) for px, py in chips])
        me = 2 * x + y
        for i in range(n):
            for j, (px, py) in enumerate(chips):
                mine = _half(src[i].at[me], idx[i], c)
                _remote(mine, mine, send_sems.at[j * n + i], recv_sems.at[j * n + i], (px, py, c)).start()
        token[...] = jnp.zeros(TOKEN, F32)

    outs = pl.pallas_call(
        body, name=name, in_specs=[HBM] * n + [ANY],
        out_specs=(SEM, SEM) + (HBM,) * n + (pl.BlockSpec(memory_space=pltpu.VMEM),),
        out_shape=(pltpu.SemaphoreType.DMA((3 * n,)), pltpu.SemaphoreType.DMA((3 * n,)))
        + tuple(pltpu.HBM(w.shape, w.dtype) for w in wb) + (_sds(TOKEN, F32),),
        input_output_aliases={i: 2 + i for i in range(n)},
        compiler_params=pltpu.CompilerParams(has_side_effects=EFFECT, collective_id=barrier_id),
    )(*[pltpu.with_memory_space_constraint(w, pltpu.HBM) for w in wb], after)
    return outs[0], outs[1], list(outs[2:2 + n]), outs[2 + n]


def _gather_wait(send_sems, recv_sems, wb, after, idx, name, started=None):
    n = len(idx)
    started = idx if started is None else started
    n_all = len(started)
    pos = [started.index(w) for w in idx]

    def body(*refs):
        buf = refs[:n]
        send_sems, recv_sems = refs[n], refs[n + 1]
        x, y, c, chips = _place()
        me = 2 * x + y
        for j, (px, py) in enumerate(chips):
            for i in range(n):
                mine = _half(buf[i].at[me], idx[i], c)
                got = _half(buf[i].at[2 * px + py], idx[i], c)
                k = j * n_all + pos[i]
                _remote(mine, mine, send_sems.at[k], recv_sems.at[k], (px, py, c)).wait_send()
                _remote(got, got, send_sems.at[k], recv_sems.at[k], (px, py, c)).wait_recv()

    outs = pl.pallas_call(
        body, name=name, in_specs=[HBM] * n + [SEM, SEM] + [ANY] * len(after), out_specs=(HBM,) * n,
        out_shape=tuple(pltpu.HBM(w.shape, w.dtype) for w in wb),
        input_output_aliases={i: i for i in range(n)},
        compiler_params=pltpu.CompilerParams(has_side_effects=EFFECT),
    )(*wb, send_sems, recv_sems, *after)
    return list(outs)


def _gather_forward(wb, idx, name, barrier_id):
    n = len(idx)

    def body(*refs):
        dst = refs[n:2 * n]
        send_sems, recv_sems = refs[2 * n], refs[2 * n + 1]
        x, y, c, chips = _place()
        _sibling_barrier(x, y, c)
        cps = []
        for j, (px, py) in enumerate(chips):
            for i in range(n):
                got = _half(dst[i].at[2 * px + py], idx[i], c)
                cps.append(_remote(got, got, send_sems.at[j, i], recv_sems.at[j, i], (x, y, 1 - c)))
                cps[-1].start()
        for j, (px, py) in enumerate(chips):
            for i in range(n):
                got = _half(dst[i].at[2 * px + py], idx[i], 1 - c)
                _remote(got, got, send_sems.at[j, i], recv_sems.at[j, i], (x, y, 1 - c)).wait_recv()
        for cp in cps:
            cp.wait_send()

    return pl.pallas_call(
        body, name=name, in_specs=[ANY] * n, out_specs=[ANY] * n, out_shape=[_sds(w.shape, BF16) for w in wb],
        input_output_aliases={i: i for i in range(n)},
        scratch_shapes=[pltpu.SemaphoreType.DMA((3, n)), pltpu.SemaphoreType.DMA((3, n))],
        compiler_params=pltpu.CompilerParams(collective_id=barrier_id),
    )(*wb)


def _forward_start(wb, cw, after, idx, name, barrier_id):
    n = len(idx)
    m = n if cw is None else n + 2

    def body(*refs):
        buf = refs[:n]
        send_sems, recv_sems = refs[m + 1], refs[m + 2]
        token = refs[2 * m + 3]
        x, y, c, chips = _place()
        _peer_barrier([(x, y, 1 - c)] + ([] if cw is None else [(px, py, c) for px, py in chips]))
        for j, (px, py) in enumerate(chips):
            for i in range(n):
                got = _half(buf[i].at[2 * px + py], idx[i], c)
                _remote(got, got, send_sems.at[j * (n + 1) + i], recv_sems.at[j * (n + 1) + i], (x, y, 1 - c)).start()
            if cw is not None:
                _remote(refs[n], refs[n + 1].at[2 * x + y], send_sems.at[j * (n + 1) + n],
                        recv_sems.at[j * (n + 1) + n], (px, py, c)).start()
        token[...] = jnp.zeros(TOKEN, F32)

    arrays = list(wb) if cw is None else list(wb) + [cw, lax.empty((4, CW_ROWS, SH_O), F32)]
    outs = pl.pallas_call(
        body, name=name, in_specs=[HBM] * m + [ANY],
        out_specs=(SEM, SEM) + (HBM,) * m + (pl.BlockSpec(memory_space=pltpu.VMEM),),
        out_shape=(pltpu.SemaphoreType.DMA((3 * (n + 1),)), pltpu.SemaphoreType.DMA((3 * (n + 1),)))
        + tuple(pltpu.HBM(a.shape, a.dtype) for a in arrays) + (_sds(TOKEN, F32),),
        input_output_aliases={i: 2 + i for i in range(m)},
        compiler_params=pltpu.CompilerParams(has_side_effects=EFFECT, collective_id=barrier_id),
    )(*[pltpu.with_memory_space_constraint(a, pltpu.HBM) for a in arrays], after)
    return outs[0], outs[1], list(outs[2:2 + m]), outs[2 + m]


def _forward_wait(send_sems, recv_sems, arrays, after, idx, with_cw, name):
    n = len(idx)
    m = len(arrays)

    def body(*refs):
        buf = refs[:n]
        send_sems, recv_sems = refs[m], refs[m + 1]
        x, y, c, chips = _place()
        for j, (px, py) in enumerate(chips):
            for i in range(n):
                sent = _half(buf[i].at[2 * px + py], idx[i], c)
                got = _half(buf[i].at[2 * px + py], idx[i], 1 - c)
                k = j * (n + 1) + i
                _remote(sent, sent, send_sems.at[k], recv_sems.at[k], (x, y, 1 - c)).wait_send()
                _remote(got, got, send_sems.at[k], recv_sems.at[k], (x, y, 1 - c)).wait_recv()
            if with_cw:
                k = j * (n + 1) + n
                theirs = refs[n + 1].at[2 * px + py]
                _remote(refs[n], theirs, send_sems.at[k], recv_sems.at[k], (px, py, c)).wait_send()
                _remote(refs[n], theirs, send_sems.at[k], recv_sems.at[k], (px, py, c)).wait_recv()

    outs = pl.pallas_call(
        body, name=name, in_specs=[HBM] * m + [SEM, SEM] + [ANY] * len(after), out_specs=(HBM,) * m,
        out_shape=tuple(pltpu.HBM(a.shape, a.dtype) for a in arrays),
        input_output_aliases={i: i for i in range(m)},
        compiler_params=pltpu.CompilerParams(has_side_effects=EFFECT),
    )(*arrays, send_sems, recv_sems, *after)
    return list(outs)


def _peer_barrier(peers):
    barrier = pltpu.get_barrier_semaphore()
    for peer in peers:
        pl.semaphore_signal(barrier, inc=1, device_id=peer, device_id_type=MESH)
    pl.semaphore_wait(barrier, len(peers))


def _sibling_barrier(x, y, c):
    _peer_barrier([(x, y, 1 - c)])


def _pair_exchange(gs, idx, name, barrier_id):
    n = len(idx)

    def body(*refs):
        src, dst = refs[:n], refs[n:2 * n]
        send_sems, recv_sems = refs[2 * n:]
        x, y, c, _ = _place()
        _sibling_barrier(x, y, c)
        cps = []
        for i in range(n):
            h = BIG[idx[i]][2] // 2
            cps.append(_remote(src[i].at[:, :, pl.ds((1 - c) * h, h), :], dst[i], send_sems.at[i], recv_sems.at[i],
                               (x, y, 1 - c)))
            cps[-1].start()
        for cp in cps:
            cp.wait()

    return pl.pallas_call(
        body, name=name, in_specs=[ANY] * n, out_specs=[ANY] * n,
        out_shape=[_sds((4, BIG[w][1], BIG[w][2] // 2, BIG[w][3]), BF16) for w in idx],
        scratch_shapes=[pltpu.SemaphoreType.DMA((n,)), pltpu.SemaphoreType.DMA((n,))],
        compiler_params=pltpu.CompilerParams(collective_id=barrier_id),
    )(*gs)


def _pair_start(gs, idx, name, barrier_id):
    n = len(idx)

    def body(*refs):
        src, land = refs[:n], refs[n:2 * n]
        send_sems, recv_sems = refs[2 * n], refs[2 * n + 1]
        token = refs[4 * n + 2]
        x, y, c, _ = _place()
        _sibling_barrier(x, y, c)
        for i in range(n):
            h = BIG[idx[i]][2] // 2
            _remote(src[i].at[:, :, pl.ds((1 - c) * h, h), :], land[i], send_sems.at[i], recv_sems.at[i],
                    (x, y, 1 - c)).start()
        token[...] = jnp.zeros(TOKEN, F32)

    lands = [lax.empty((4, BIG[w][1], BIG[w][2] // 2, BIG[w][3]), BF16) for w in idx]
    arrays = list(gs) + lands
    outs = pl.pallas_call(
        body, name=name, in_specs=[HBM] * (2 * n),
        out_specs=(SEM, SEM) + (HBM,) * (2 * n) + (pl.BlockSpec(memory_space=pltpu.VMEM),),
        out_shape=(pltpu.SemaphoreType.DMA((n,)), pltpu.SemaphoreType.DMA((n,)))
        + tuple(pltpu.HBM(a.shape, a.dtype) for a in arrays) + (_sds(TOKEN, F32),),
        input_output_aliases={i: 2 + i for i in range(2 * n)},
        compiler_params=pltpu.CompilerParams(has_side_effects=EFFECT, collective_id=barrier_id),
    )(*[pltpu.with_memory_space_constraint(a, pltpu.HBM) for a in arrays])
    return outs[0], outs[1], list(outs[2:2 + n]), list(outs[2 + n:2 + 2 * n]), outs[2 + 2 * n]


def _pair_wait(send_sems, recv_sems, gs, lands, after, idx, name):
    n = len(idx)

    def body(*refs):
        src, land = refs[:n], refs[n:2 * n]
        send_sems, recv_sems = refs[2 * n], refs[2 * n + 1]
        x, y, c, _ = _place()
        for i in range(n):
            h = BIG[idx[i]][2] // 2
            cp = _remote(src[i].at[:, :, pl.ds((1 - c) * h, h), :], land[i], send_sems.at[i], recv_sems.at[i],
                         (x, y, 1 - c))
            cp.wait_send()
            cp.wait_recv()

    arrays = list(gs) + list(lands)
    outs = pl.pallas_call(
        body, name=name, in_specs=[HBM] * (2 * n) + [SEM, SEM] + [ANY] * len(after), out_specs=(HBM,) * (2 * n),
        out_shape=tuple(pltpu.HBM(a.shape, a.dtype) for a in arrays),
        input_output_aliases={i: i for i in range(2 * n)},
        compiler_params=pltpu.CompilerParams(has_side_effects=EFFECT),
    )(*arrays, send_sems, recv_sems, *after)
    return list(outs[:n]), list(outs[n:])


def _pair_sums(place, gs, r1s, idx, name):
    n = len(idx)
    dims = [(BIG[w][1], BIG[w][2] // 2, BIG[w][3]) for w in idx]

    def body(pref, *refs):
        for i in range(n):
            refs[2 * n + i][...] = (refs[i][...] + refs[n + i][...].astype(F32)).astype(BF16)

    mine = [pl.BlockSpec((1, k, h, cdim), lambda s, pref: (s, 0, pref[0], 0)) for k, h, cdim in dims]
    whole = [pl.BlockSpec((1, k, h, cdim), lambda s, pref: (s, 0, 0, 0)) for k, h, cdim in dims]
    grid_spec = pltpu.PrefetchScalarGridSpec(num_scalar_prefetch=1, grid=(4,), in_specs=mine + whole, out_specs=whole)
    return pl.pallas_call(
        body, name=name, grid_spec=grid_spec, out_shape=[_sds((4, k, h, cdim), BF16) for k, h, cdim in dims],
        compiler_params=_params(("parallel",)),
    )(place, *gs, *r1s)


def _chip_start(ps, idx, name, barrier_id):
    n = len(idx)

    def body(*refs):
        src, land = refs[:n], refs[n:2 * n]
        send_sems, recv_sems = refs[2 * n], refs[2 * n + 1]
        token = refs[4 * n + 2]
        x, y, c, chips = _place()
        _peer_barrier([(px, py, c) for px, py in chips])
        for j, (px, py) in enumerate(chips):
            for i in range(n):
                _remote(src[i].at[2 * px + py], land[i].at[j], send_sems.at[j * n + i], recv_sems.at[j * n + i],
                        (px, py, c)).start()
        token[...] = jnp.zeros(TOKEN, F32)

    lands = [lax.empty((3,) + p.shape[1:], BF16) for p in ps]
    outs = pl.pallas_call(
        body, name=name, in_specs=[HBM] * (2 * n),
        out_specs=(SEM, SEM) + (HBM,) * (2 * n) + (pl.BlockSpec(memory_space=pltpu.VMEM),),
        out_shape=(pltpu.SemaphoreType.DMA((3 * n,)), pltpu.SemaphoreType.DMA((3 * n,)))
        + tuple(pltpu.HBM(a.shape, a.dtype) for a in list(ps) + lands) + (_sds(TOKEN, F32),),
        input_output_aliases={i: 2 + i for i in range(2 * n)},
        compiler_params=pltpu.CompilerParams(has_side_effects=EFFECT, collective_id=barrier_id),
    )(*[pltpu.with_memory_space_constraint(a, pltpu.HBM) for a in list(ps) + lands])
    return outs[0], outs[1], list(outs[2:2 + n]), list(outs[2 + n:2 + 2 * n]), outs[2 + 2 * n]


def _chip_wait(send_sems, recv_sems, ps, lands, after, idx, name):
    n = len(idx)

    def body(*refs):
        src, land = refs[:n], refs[n:2 * n]
        send_sems, recv_sems = refs[2 * n], refs[2 * n + 1]
        x, y, c, chips = _place()
        for j, (px, py) in enumerate(chips):
            for i in range(n):
                cp = _remote(src[i].at[2 * px + py], land[i].at[j], send_sems.at[j * n + i], recv_sems.at[j * n + i],
                             (px, py, c))
                cp.wait_send()
                cp.wait_recv()

    arrays = list(ps) + list(lands)
    outs = pl.pallas_call(
        body, name=name, in_specs=[HBM] * (2 * n) + [SEM, SEM] + [ANY] * len(after), out_specs=(HBM,) * (2 * n),
        out_shape=tuple(pltpu.HBM(a.shape, a.dtype) for a in arrays),
        input_output_aliases={i: i for i in range(2 * n)},
        compiler_params=pltpu.CompilerParams(has_side_effects=EFFECT),
    )(*arrays, send_sems, recv_sems, *after)
    return list(outs[n:])


def _chip_sums(place, gs, r1s, r2s, idx, name):
    n = len(idx)
    dims = [(BIG[w][1], BIG[w][2] // 4, BIG[w][3]) for w in idx]

    def body(pref, *refs):
        for i in range(n):
            acc = refs[i][0] + refs[n + i][0].astype(F32)
            for j in range(3):
                acc = acc + refs[2 * n + i][j].astype(F32)
            refs[3 * n + i][...] = acc

    in_specs = ([pl.BlockSpec((1, k, q, cdim), lambda t, pref: (pref[1], 0, pref[0] * 2 + t, 0)) for k, q, cdim in dims]
                + [pl.BlockSpec((1, k, q, cdim), lambda t, pref: (pref[1], 0, t, 0)) for k, q, cdim in dims]
                + [pl.BlockSpec((3, k, q, cdim), lambda t, pref: (0, 0, t, 0)) for k, q, cdim in dims])
    out_specs = [pl.BlockSpec((k, q, cdim), lambda t, pref: (0, pref[0] * 2 + t, 0)) for k, q, cdim in dims]
    grid_spec = pltpu.PrefetchScalarGridSpec(num_scalar_prefetch=1, grid=(2,), in_specs=in_specs, out_specs=out_specs)
    return pl.pallas_call(
        body, name=name, grid_spec=grid_spec, out_shape=[_sds(BIG[w][1:], F32) for w in idx],
        compiler_params=_params(("parallel",)),
    )(place, *gs, *r1s, *r2s)


def _pair_gather(hs, idx, name, barrier_id):
    n = len(idx)

    def body(*refs):
        dst = refs[n:2 * n]
        send_sems, recv_sems = refs[2 * n:]
        x, y, c, _ = _place()
        _sibling_barrier(x, y, c)
        cps = []
        for i in range(n):
            mine = _half(dst[i], idx[i], c)
            cps.append(_remote(mine, mine, send_sems.at[i], recv_sems.at[i], (x, y, 1 - c)))
            cps[-1].start()
        for i in range(n):
            theirs = _half(dst[i], idx[i], 1 - c)
            _remote(theirs, theirs, send_sems.at[i], recv_sems.at[i], (x, y, 1 - c)).wait_recv()
        for cp in cps:
            cp.wait_send()

    return pl.pallas_call(
        body, name=name, in_specs=[ANY] * n, out_specs=[ANY] * n,
        out_shape=[_sds(BIG[w][1:], F32) for w in idx],
        input_output_aliases={i: i for i in range(n)},
        scratch_shapes=[pltpu.SemaphoreType.DMA((n,)), pltpu.SemaphoreType.DMA((n,))],
        compiler_params=pltpu.CompilerParams(collective_id=barrier_id),
    )(*hs)


SMALL_ROWS = 40


def _adamw_math(w, g, m, v):
    m = ADAM_B1 * m + (1.0 - ADAM_B1) * g
    v = ADAM_B2 * v + (1.0 - ADAM_B2) * (g * g)
    m_hat = m / (1.0 - ADAM_B1 ** ADAM_STEP)
    v_hat = v / (1.0 - ADAM_B2 ** ADAM_STEP)
    delta = -ADAM_LR * (m_hat / (jnp.sqrt(v_hat) + ADAM_EPS) + ADAM_WD * w)
    return delta, m, v


def _small_start(pack, after):
    def body(pack_ref, land_ref, after_ref, send_sems, recv_sems, pack_thru, land_thru, token):
        x, y, c, _ = _place()
        for r in range(1, 8):
            peer = (x if not r & 4 else 1 - x, y if not r & 2 else 1 - y, c if not r & 1 else 1 - c)
            _remote(pack_ref, land_ref.at[r - 1], send_sems.at[r - 1], recv_sems.at[r - 1], peer).start()
        token[...] = jnp.zeros(TOKEN, F32)

    land = lax.empty((7, SMALL_ROWS, D), F32)
    outs = pl.pallas_call(
        body, name="small_start", in_specs=[HBM, HBM, ANY],
        out_specs=(SEM, SEM, HBM, HBM, pl.BlockSpec(memory_space=pltpu.VMEM)),
        out_shape=(pltpu.SemaphoreType.DMA((7,)), pltpu.SemaphoreType.DMA((7,)), pltpu.HBM(pack.shape, F32),
                   pltpu.HBM(land.shape, F32), _sds(TOKEN, F32)),
        input_output_aliases={0: 2, 1: 3},
        compiler_params=pltpu.CompilerParams(has_side_effects=EFFECT),
    )(pltpu.with_memory_space_constraint(pack, pltpu.HBM), pltpu.with_memory_space_constraint(land, pltpu.HBM), after)
    return outs


def _small_wait(send_sems, recv_sems, pack, land, after):
    def body(pack_ref, land_ref, send_sems, recv_sems, *rest):
        x, y, c, _ = _place()
        for r in range(1, 8):
            peer = (x if not r & 4 else 1 - x, y if not r & 2 else 1 - y, c if not r & 1 else 1 - c)
            cp = _remote(pack_ref, land_ref.at[r - 1], send_sems.at[r - 1], recv_sems.at[r - 1], peer)
            cp.wait_send()
            cp.wait_recv()

    return pl.pallas_call(
        body, name="small_wait", in_specs=[HBM, HBM, SEM, SEM] + [ANY] * len(after), out_specs=(HBM, HBM),
        out_shape=(pltpu.HBM(pack.shape, F32), pltpu.HBM(land.shape, F32)),
        input_output_aliases={0: 0, 1: 1},
        compiler_params=pltpu.CompilerParams(has_side_effects=EFFECT),
    )(pack, land, send_sems, recv_sems, *after)


def _small_update(place, pack, land, ws, ms, vs, flat):
    n = len(ws)

    def body(pref, pack_ref, land_ref, *refs):
        chip = pref[1]
        me = 2 * chip + pref[0]
        own = pack_ref[...]
        tot = None
        for dev in range(8):
            r = jnp.bitwise_xor(me, dev)
            term = jnp.where(r == 0, own, land_ref[jnp.maximum(r - 1, 0)])
            tot = term if tot is None else tot + term
        out, buf = refs[3 * n:-1], refs[-1]
        buf[...] = tot
        g_conv = jnp.zeros((3, SH_O), F32)
        for s in range(4):
            g_conv = g_conv + jnp.where(chip == s, buf[24:27, s * SH_O:(s + 1) * SH_O], 0.0)
        gs = [buf[0:2, :], buf[8:10, :], buf[16:17, :], g_conv]
        out[0][...] = buf[32:33, 0:128]
        for i in range(n):
            d, nm, nv = _adamw_math(refs[i][...], gs[i], refs[n + i][...], refs[2 * n + i][...])
            for j, val in enumerate((gs[i], d, nm, nv)):
                if len(flat[i]) == 3:
                    for r in range(flat[i][0]):
                        out[1 + j * n + i][r] = val[r:r + 1, :]
                else:
                    out[1 + j * n + i][...] = val.reshape(flat[i])

    def full(shape):
        nd = len(shape)
        return pl.BlockSpec(shape, lambda i, pref: (0,) * nd)

    specs = [full(w.shape) for w in ws]
    grid_spec = pltpu.PrefetchScalarGridSpec(
        num_scalar_prefetch=1, grid=(1,),
        in_specs=[full(pack.shape), full(land.shape)] + specs * 3,
        out_specs=[full((1, 128))] + [full(s) for s in flat] * 4,
        scratch_shapes=[pltpu.VMEM((SMALL_ROWS, D), F32)])
    outs = pl.pallas_call(
        body, name="small_update", grid_spec=grid_spec,
        out_shape=[_sds((1, 128), F32)] + [_sds(s, F32) for s in flat] * 4,
        compiler_params=_params(("arbitrary",)),
    )(place, pack, land, *ws, *ms, *vs)
    return outs[0], outs[1:1 + n], outs[1 + n:1 + 2 * n], outs[1 + 2 * n:1 + 3 * n], outs[1 + 3 * n:]


def _adamw_layer(ws, gs, ms, vs, idx, name):
    n = len(idx)
    dims = [(BIG[w][1], BIG[w][2] // 4, BIG[w][3]) for w in idx]

    def body(*refs):
        for i in range(n):
            gv = refs[n + i][...]
            d, nm, nv = _adamw_math(refs[i][...], gv, refs[2 * n + i][...], refs[3 * n + i][...])
            refs[4 * n + i][...] = d
            refs[5 * n + i][...] = nm
            refs[6 * n + i][...] = nv
            refs[7 * n + i][...] = gv

    specs = [pl.BlockSpec((k, q, cdim), lambda t: (0, t, 0)) for k, q, cdim in dims]
    outs = pl.pallas_call(
        body, name=name, grid=(4,), in_specs=specs * 4, out_specs=specs * 4,
        out_shape=[_sds(BIG[w][1:], F32) for w in idx] * 4,
        compiler_params=_params(("parallel",)),
    )(*ws, *gs, *ms, *vs)
    return [tuple(outs[j * n + i] for j in range(4)) for i in range(n)]


def _pad_rows(a, rows):
    return jnp.pad(a, ((0, rows - a.shape[0]), (0, 0)))


def kernel(x, mem, positions, norm_g, mem_norm_g, w_mem_kv, attn_w_in, attn_w_out, conv_w_in, conv_w, conv_w_out, final_g, loss_target, m_norm_g, m_mem_norm_g, m_w_mem_kv, m_attn_w_in, m_attn_w_out, m_conv_w_in, m_conv_w, m_conv_w_out, m_final_g, v_norm_g, v_mem_norm_g, v_w_mem_kv, v_attn_w_in, v_attn_w_out, v_conv_w_in, v_conv_w, v_conv_w_out, v_final_g):
    mx, my, mc = lax.axis_index("x"), lax.axis_index("y"), lax.axis_index("c")
    place = jnp.stack([mc, 2 * mx + my]).astype(jnp.int32)

    w_big = [w_mem_kv, attn_w_in, attn_w_out, conv_w_in, conv_w_out]
    m_big = [m_w_mem_kv, m_attn_w_in, m_attn_w_out, m_conv_w_in, m_conv_w_out]
    v_big = [v_w_mem_kv, v_attn_w_in, v_attn_w_out, v_conv_w_in, v_conv_w_out]
    first, rest = (1,), (0, 2, 3, 4)
    wb1 = _cast_weights(place, [w_big[i] for i in first], place, first, "cast_w_in_a")
    a1_send, a1_recv, a1_bufs, a1_token = _gather_start(wb1, place, first, "gather_a1_start", 4)
    wbr = _cast_weights(place, [w_big[i] for i in rest], a1_token, rest, "cast_weights")
    r_send, r_recv, r_bufs, gb_token = _gather_start(wbr, a1_token, rest, "gather_rest_start", 5)
    a2_send, a2_recv, gb_send, gb_recv = r_send, r_recv, r_send, r_recv
    a2_bufs, gb_bufs = r_bufs[:2], r_bufs[2:]
    started, rest = rest, (0, 2)

    xs, tgt = x[0], loss_target[0]
    g0, g1 = norm_g[0:1], norm_g[1:2]
    rc, rs1, rs2 = _rope_tables(positions[0].astype(F32).reshape(S, 1), gb_token)
    a1_bufs = _gather_wait(a1_send, a1_recv, a1_bufs, [rc], first, "gather_a1_wait")
    w_in_a = _gather_forward(a1_bufs, first, "gather_a1_forward", 0)[0].reshape(4, D, SH_A)
    hn0, q, k, v, qm0, z0 = _in_proj_a(xs, g0, w_in_a, rc, rs1, rs2, gb_token)
    a2_bufs = _gather_wait(a2_send, a2_recv, a2_bufs, [q], rest, "gather_a2_wait", started)
    f2_send, f2_recv, a2_bufs, f2_token = _forward_start(a2_bufs, None, q, rest, "forward_a2_start", 9)
    fwd = [_attn_fwd(q, k, v, 0, f2_token)]
    fwd.append(_attn_fwd(q, k, v, 1, fwd[0][0]))
    fwd.append(_attn_fwd(q, k, v, 2, fwd[1][0]))
    os_, ls, lss = [f[0] for f in fwd], [f[1] for f in fwd], [f[2] for f in fwd]
    cw_own = _pad_rows(conv_w[0], CW_ROWS)
    gb_bufs = _gather_wait(gb_send, gb_recv, gb_bufs, [os_[2]], LAYER_B, "gather_b_wait", started)
    fb_send, fb_recv, gb_bufs, fb_token = _forward_start(gb_bufs, cw_own, os_[2], LAYER_B, "forward_b_start", 10)
    wkv_f, w_out_a = _forward_wait(f2_send, f2_recv, a2_bufs, [os_[2], fb_token], rest, False, "forward_a2_wait")
    w_out_a = w_out_a.reshape(4, BR_A, SH_O)
    memn, kv = _mem_fwd(mem[0], mem_norm_g, wkv_f)
    h1 = _attn_out(os_, ls, qm0, kv[0], z0, xs, w_out_a)

    w_in_b, w_out_b, _, cw_f = _forward_wait(fb_send, fb_recv, gb_bufs, [h1], LAYER_B, True, "forward_b_wait")
    w_in_b = w_in_b.reshape(4, D, SH_B)
    w_out_b = w_out_b.reshape(BR_B, D)
    cw_f = lax.dynamic_update_slice(cw_f, cw_own[None], (2 * mx + my, 0, 0))
    cw8 = cw_f.transpose(1, 0, 2).reshape(CW_ROWS, D)
    hn1, bg, cg, u, qm1, z1 = _in_proj_b(h1, g1, w_in_b)
    dh2, loss_part, dfg = _conv_out_loss(bg, cg, u, cw8, qm1, kv[1], z1, h1, w_out_b, final_g.reshape(1, D), tgt)

    dproj_b, dw_out_b, dcw, dkv1, dw_out_b16 = _conv_bwd(dh2, bg, cg, u, cw8, qm1, kv[1], z1, w_out_b)
    dw_in_b, dw_in_b16 = _w_in_grad(hn1, dproj_b, IN_B, "w_in_b_grad")
    gs_b = [dw_in_b.reshape(4, 1, D, SH_B), dw_out_b.reshape(4, 1, BR_B // 4, D)]
    gb_b = [dw_in_b16.reshape(4, 1, D, SH_B), dw_out_b16.reshape(4, 1, BR_B // 4, D)]
    pb_send, pb_recv, gb_b, pb_land, pb_token = _pair_start(gb_b, LAYER_B, "pair_b_start", 6)
    dh1, dg1 = _in_proj_bwd(dproj_b, w_in_b, h1, g1, dh2, pb_token, IN_B, "in_proj_b_bwd")
    _, r1_b = _pair_wait(pb_send, pb_recv, gb_b, pb_land, [dh1], LAYER_B, "pair_b_wait")
    ps_b = _pair_sums(place, gs_b, r1_b, LAYER_B, "pair_sums_b")
    cb_send, cb_recv, cb_src, cb_land, cb_token = _chip_start(ps_b, LAYER_B, "chip_b_start", 7)

    outs = _attn_out_bwd(dh1, os_, ls, qm0, kv[0], z0, w_out_a, cb_token)
    dos, dds, dqm, dz, dw_out_a, dkv0, dw_out_a16 = outs[0:3], outs[3:6], outs[6], outs[7], outs[8], outs[9], outs[10]
    bwd = [_attn_bwd(q, k, v, dos[g], lss[g], dds[g], g) for g in range(3)]
    dproj_a = _qkv_bwd([b[0] for b in bwd], [b[1] for b in bwd], [b[2] for b in bwd], dqm, dz, rc, rs1, rs2)
    dw_in_a, dw_in_a16 = _w_in_grad(hn0, dproj_a, IN_A, "w_in_a_grad")
    dwkv, dwkv16, dmg = _mem_bwd(mem[0], mem_norm_g, memn, wkv_f, dkv0, dkv1)

    gs_a = [dwkv, dw_in_a.reshape(4, 1, D, SH_A), dw_out_a.reshape(4, 1, BR_A, SH_O)]
    r1_a = _pair_exchange([dwkv16, dw_in_a16.reshape(4, 1, D, SH_A), dw_out_a16.reshape(4, 1, BR_A, SH_O)], LAYER_A,
                          "pair_exchange_a", 1)
    ps_a = _pair_sums(place, gs_a, r1_a, LAYER_A, "pair_sums_a")
    ca_send, ca_recv, ca_src, ca_land, ca_token = _chip_start(ps_a, LAYER_A, "chip_a_start", 8)

    gx, dg0 = _in_proj_bwd(dproj_a, w_in_a, xs, g0, dh1, ca_token, IN_A, "in_proj_a_bwd")
    pack = jnp.concatenate([_pad_rows(jnp.concatenate([dg0, dg1], axis=0), 8), _pad_rows(dmg, 8), _pad_rows(dfg, 8),
                            dcw, _pad_rows(jnp.pad(loss_part, ((0, 0), (0, D - 128))), 8)], axis=0)
    sm_send, sm_recv, pack, sm_land, sm_token = _small_start(pack, ca_token)
    r2_b = _chip_wait(cb_send, cb_recv, cb_src, cb_land, [ca_token], LAYER_B, "chip_b_wait")
    hs_b = _chip_sums(place, gs_b, r1_b, r2_b, LAYER_B, "chip_sums_b")
    g_b = _pair_gather(hs_b, LAYER_B, "pair_gather_b", 2)
    upd_b = _adamw_layer([w_big[w] for w in LAYER_B], g_b, [m_big[w] for w in LAYER_B], [v_big[w] for w in LAYER_B],
                         LAYER_B, "adamw_b")
    r2_a = _chip_wait(ca_send, ca_recv, ca_src, ca_land, [gx, upd_b[0][0], upd_b[1][0], sm_token], LAYER_A,
                      "chip_a_wait")
    hs_a = _chip_sums(place, gs_a, r1_a, r2_a, LAYER_A, "chip_sums_a")
    g_a = _pair_gather(hs_a, LAYER_A, "pair_gather_a", 3)
    upd_a = _adamw_layer([w_big[w] for w in LAYER_A], g_a, [m_big[w] for w in LAYER_A], [v_big[w] for w in LAYER_A],
                         LAYER_A, "adamw_a")
    upd = upd_a + upd_b
    g_big = [u[3] for u in upd]
    pack, sm_land = _small_wait(sm_send, sm_recv, pack, sm_land, [r2_a[0]])
    sw = [norm_g, mem_norm_g, final_g.reshape(1, D), conv_w[0]]
    sm = [m_norm_g, m_mem_norm_g, m_final_g.reshape(1, D), m_conv_w[0]]
    sv = [v_norm_g, v_mem_norm_g, v_final_g.reshape(1, D), v_conv_w[0]]
    loss_row, sg, sd, snm, snv = _small_update(place, pack, sm_land, sw, sm, sv,
                                               [norm_g.shape, mem_norm_g.shape, final_g.shape, (3, 1, SH_O)])
    loss = loss_row[0, 0]
    g_norm, g_memnorm, g_final, g_conv = sg

    def order(norm, memnorm, wkv, w_in_a, w_out_a, w_in_b, conv, w_out_b, final):
        return (norm, memnorm, wkv, w_in_a, w_out_a, w_in_b, conv.transpose(1, 0, 2), w_out_b, final)

    grads = order(g_norm, g_memnorm, g_big[0], g_big[1], g_big[2], g_big[3], g_conv, g_big[4], g_final)
    deltas = order(sd[0], sd[1], upd[0][0], upd[1][0], upd[2][0], upd[3][0], sd[3], upd[4][0], sd[2])
    new_m = order(snm[0], snm[1], upd[0][1], upd[1][1], upd[2][1], upd[3][1], snm[3], upd[4][1], snm[2])
    new_v = order(snv[0], snv[1], upd[0][2], upd[1][2], upd[2][2], upd[3][2], snv[3], upd[4][2], snv[2])
    return (loss, gx[None], *grads, *deltas, *new_m, *new_v)
```
